```python
import jax, jax.numpy as jnp
from jax import lax
import numpy as np

D_MODEL = 1024
BATCH = 16
SEQ = 2048
DEPTH = 1

POOL_WINDOWS = (2, 4, 8, 16)
POOL_GROUPS = len(POOL_WINDOWS)
POOL_WIDTH = D_MODEL // 2
POOL_GROUP_DIM = POOL_WIDTH // POOL_GROUPS
HEAD_DIM = 64
ATTN_WIDTH = D_MODEL // 2
N_HEADS = ATTN_WIDTH // HEAD_DIM
Q_BLOCK = 128
N_BRANCHES = 2
D_FF = ((8 * D_MODEL + 3 * 256 - 1) // (3 * 256)) * 256
RMS_EPS = 1e-6
IN_SPLITS = (POOL_WIDTH, ATTN_WIDTH, ATTN_WIDTH, ATTN_WIDTH, N_HEADS, D_MODEL, D_MODEL)
IN_WIDTH = sum(IN_SPLITS)

kernel_name = "hybrid_pool_fox_gated_block"


def rmsnorm(x, g):
    xf = x.astype(jnp.float32)
    r = lax.rsqrt(jnp.mean(xf * xf, axis=-1, keepdims=True) + RMS_EPS)
    return (xf * r).astype(x.dtype) * g


def causal_multiscale_pool(u):
    B, S, _ = u.shape
    ug = u.reshape(B, S, POOL_GROUPS, POOL_GROUP_DIM)
    c = jnp.cumsum(ug.astype(jnp.float32), axis=1)
    t = jnp.arange(S)
    outs = []
    for g, w in enumerate(POOL_WINDOWS):
        cg = c[:, :, g]
        c_prev = jnp.pad(cg, ((0, 0), (w, 0), (0, 0)))[:, :S]
        cnt = jnp.minimum(t + 1, w).astype(jnp.float32)[None, :, None]
        outs.append((cg - c_prev) / cnt)
    mean = jnp.stack(outs, axis=2)
    return (mean - ug.astype(jnp.float32)).astype(u.dtype)


def forgetting_attention(q, k, v, log_f):
    B, S, H, Dh = q.shape
    q = q.transpose(0, 2, 1, 3)
    k = k.transpose(0, 2, 1, 3)
    v = v.transpose(0, 2, 1, 3)
    F = jnp.cumsum(log_f, axis=1).transpose(0, 2, 1)
    scale = HEAD_DIM ** -0.5
    outs = []
    for i in range(S // Q_BLOCK):
        lo, hi = i * Q_BLOCK, (i + 1) * Q_BLOCK
        s = jnp.einsum('bhqd,bhkd->bhqk', q[:, :, lo:hi], k[:, :, :hi]).astype(jnp.float32) * scale
        s = s + F[:, :, lo:hi, None] - F[:, :, None, :hi]
        mask = (lo + jnp.arange(Q_BLOCK))[:, None] >= jnp.arange(hi)[None, :]
        s = jnp.where(mask, s, -jnp.inf)
        p = jax.nn.softmax(s, axis=-1).astype(v.dtype)
        outs.append(jnp.einsum('bhqk,bhkd->bhqd', p, v[:, :, :hi]))
    o = jnp.concatenate(outs, axis=2)
    return o.transpose(0, 2, 1, 3).reshape(B, S, H * Dh)


def _fwd_setup_inputs(seed: int = 0) -> dict:
    key = jax.random.key(seed)
    ks = jax.random.split(key, 16)
    f32 = jnp.float32
    n = lambda k, shape, fan_in: jax.random.normal(k, shape, f32) * (fan_in ** -0.5)
    return {
        "x": jax.random.normal(ks[0], (BATCH, SEQ, D_MODEL), f32),
        "norm1_g": 1.0 + 0.02 * jax.random.normal(ks[1], (DEPTH, D_MODEL), f32),
        "w_in": n(ks[2], (DEPTH, D_MODEL, IN_WIDTH), D_MODEL),
        "b_forget": 2.0 + 0.5 * jax.random.normal(ks[3], (DEPTH, N_HEADS), f32),
        "pool_mix": n(ks[4], (DEPTH, POOL_GROUPS, POOL_GROUP_DIM, POOL_GROUP_DIM), POOL_GROUP_DIM),
        "pool_scale": 1.0 + 0.1 * jax.random.normal(ks[5], (DEPTH, POOL_WIDTH), f32),
        "w_pool_out": n(ks[6], (DEPTH, POOL_WIDTH, D_MODEL), POOL_WIDTH),
        "w_attn_out": n(ks[7], (DEPTH, ATTN_WIDTH, D_MODEL), ATTN_WIDTH),
        "w_out": n(ks[8], (DEPTH, D_MODEL, D_MODEL), D_MODEL),
        "norm2_g": 1.0 + 0.02 * jax.random.normal(ks[9], (DEPTH, D_MODEL), f32),
        "w_ffn_gate": n(ks[10], (DEPTH, D_MODEL, D_FF), D_MODEL),
        "w_ffn_up": n(ks[11], (DEPTH, D_MODEL, D_FF), D_MODEL),
        "w_ffn_down": n(ks[12], (DEPTH, D_FF, D_MODEL), D_FF),
        "norm_f_g": 1.0 + 0.02 * jax.random.normal(ks[13], (D_MODEL,), f32),
    }


def _fwd_reference(x, norm1_g, w_in, b_forget, pool_mix, pool_scale, w_pool_out, w_attn_out, w_out,
              norm2_g, w_ffn_gate, w_ffn_up, w_ffn_down, norm_f_g):
    B, S, _ = x.shape
    offs = np.cumsum((0,) + IN_SPLITS)
    for l in range(DEPTH):
        h = rmsnorm(x, norm1_g[l])
        z = h @ w_in[l]
        u, q, k, v, fl, gp, ga = [z[..., int(offs[i]):int(offs[i + 1])] for i in range(len(IN_SPLITS))]

        p = causal_multiscale_pool(u)
        p = jnp.einsum('bsgc,gcd->bsgd', p, pool_mix[l]).reshape(B, S, POOL_WIDTH) * pool_scale[l]
        pool_y = p @ w_pool_out[l]

        log_f = jax.nn.log_sigmoid(fl.astype(jnp.float32) + b_forget[l].astype(jnp.float32))
        a = forgetting_attention(q.reshape(B, S, N_HEADS, HEAD_DIM), k.reshape(B, S, N_HEADS, HEAD_DIM),
                                 v.reshape(B, S, N_HEADS, HEAD_DIM), log_f)
        attn_y = a @ w_attn_out[l]

        merged = jax.nn.sigmoid(gp) * pool_y + jax.nn.sigmoid(ga) * attn_y
        x = x + merged @ w_out[l]

        h2 = rmsnorm(x, norm2_g[l])
        x = x + (jax.nn.silu(h2 @ w_ffn_gate[l]) * (h2 @ w_ffn_up[l])) @ w_ffn_down[l]
    return rmsnorm(x, norm_f_g)


import jax as _jax
import jax.numpy as _jnp

TWIN_FORMAT = 'train_step'
FWD_PARAMS = ['x', 'norm1_g', 'w_in', 'b_forget', 'pool_mix', 'pool_scale', 'w_pool_out', 'w_attn_out', 'w_out', 'norm2_g', 'w_ffn_gate', 'w_ffn_up', 'w_ffn_down', 'norm_f_g']
TWIN_WEIGHTS = ['norm1_g', 'w_in', 'b_forget', 'pool_mix', 'pool_scale', 'w_pool_out', 'w_attn_out', 'w_out', 'norm2_g', 'w_ffn_gate', 'w_ffn_up', 'w_ffn_down', 'norm_f_g']
TWIN_DIFF_INPUT = 'x'
TWIN_INPUTS = ['x', 'norm1_g', 'w_in', 'b_forget', 'pool_mix', 'pool_scale', 'w_pool_out', 'w_attn_out', 'w_out', 'norm2_g', 'w_ffn_gate', 'w_ffn_up', 'w_ffn_down', 'norm_f_g', 'loss_target', 'm_norm1_g', 'm_w_in', 'm_b_forget', 'm_pool_mix', 'm_pool_scale', 'm_w_pool_out', 'm_w_attn_out', 'm_w_out', 'm_norm2_g', 'm_w_ffn_gate', 'm_w_ffn_up', 'm_w_ffn_down', 'm_norm_f_g', 'v_norm1_g', 'v_w_in', 'v_b_forget', 'v_pool_mix', 'v_pool_scale', 'v_w_pool_out', 'v_w_attn_out', 'v_w_out', 'v_norm2_g', 'v_w_ffn_gate', 'v_w_ffn_up', 'v_w_ffn_down', 'v_norm_f_g']
TWIN_OUTPUTS = ['loss', 'grad_x', 'grad_norm1_g', 'grad_w_in', 'grad_b_forget', 'grad_pool_mix', 'grad_pool_scale', 'grad_w_pool_out', 'grad_w_attn_out', 'grad_w_out', 'grad_norm2_g', 'grad_w_ffn_gate', 'grad_w_ffn_up', 'grad_w_ffn_down', 'grad_norm_f_g', 'delta_norm1_g', 'delta_w_in', 'delta_b_forget', 'delta_pool_mix', 'delta_pool_scale', 'delta_w_pool_out', 'delta_w_attn_out', 'delta_w_out', 'delta_norm2_g', 'delta_w_ffn_gate', 'delta_w_ffn_up', 'delta_w_ffn_down', 'delta_norm_f_g', 'new_m_norm1_g', 'new_m_w_in', 'new_m_b_forget', 'new_m_pool_mix', 'new_m_pool_scale', 'new_m_w_pool_out', 'new_m_w_attn_out', 'new_m_w_out', 'new_m_norm2_g', 'new_m_w_ffn_gate', 'new_m_w_ffn_up', 'new_m_w_ffn_down', 'new_m_norm_f_g', 'new_v_norm1_g', 'new_v_w_in', 'new_v_b_forget', 'new_v_pool_mix', 'new_v_pool_scale', 'new_v_w_pool_out', 'new_v_w_attn_out', 'new_v_w_out', 'new_v_norm2_g', 'new_v_w_ffn_gate', 'new_v_w_ffn_up', 'new_v_w_ffn_down', 'new_v_norm_f_g']
TWIN_LEAF_KINDS = {'loss': 'loss', 'grad_x': 'grad_x', 'grad_norm1_g': 'grad_w', 'grad_w_in': 'grad_w', 'grad_b_forget': 'grad_w', 'grad_pool_mix': 'grad_w', 'grad_pool_scale': 'grad_w', 'grad_w_pool_out': 'grad_w', 'grad_w_attn_out': 'grad_w', 'grad_w_out': 'grad_w', 'grad_norm2_g': 'grad_w', 'grad_w_ffn_gate': 'grad_w', 'grad_w_ffn_up': 'grad_w', 'grad_w_ffn_down': 'grad_w', 'grad_norm_f_g': 'grad_w', 'delta_norm1_g': 'delta_w', 'delta_w_in': 'delta_w', 'delta_b_forget': 'delta_w', 'delta_pool_mix': 'delta_w', 'delta_pool_scale': 'delta_w', 'delta_w_pool_out': 'delta_w', 'delta_w_attn_out': 'delta_w', 'delta_w_out': 'delta_w', 'delta_norm2_g': 'delta_w', 'delta_w_ffn_gate': 'delta_w', 'delta_w_ffn_up': 'delta_w', 'delta_w_ffn_down': 'delta_w', 'delta_norm_f_g': 'delta_w', 'new_m_norm1_g': 'new_m', 'new_m_w_in': 'new_m', 'new_m_b_forget': 'new_m', 'new_m_pool_mix': 'new_m', 'new_m_pool_scale': 'new_m', 'new_m_w_pool_out': 'new_m', 'new_m_w_attn_out': 'new_m', 'new_m_w_out': 'new_m', 'new_m_norm2_g': 'new_m', 'new_m_w_ffn_gate': 'new_m', 'new_m_w_ffn_up': 'new_m', 'new_m_w_ffn_down': 'new_m', 'new_m_norm_f_g': 'new_m', 'new_v_norm1_g': 'new_v', 'new_v_w_in': 'new_v', 'new_v_b_forget': 'new_v', 'new_v_pool_mix': 'new_v', 'new_v_pool_scale': 'new_v', 'new_v_w_pool_out': 'new_v', 'new_v_w_attn_out': 'new_v', 'new_v_w_out': 'new_v', 'new_v_norm2_g': 'new_v', 'new_v_w_ffn_gate': 'new_v', 'new_v_w_ffn_up': 'new_v', 'new_v_w_ffn_down': 'new_v', 'new_v_norm_f_g': 'new_v'}


def _forward(args):
    return _fwd_reference(*[args[k] for k in FWD_PARAMS])


def _output_shape():
    out = _jax.eval_shape(lambda: _forward(_fwd_setup_inputs(0)))
    return out.shape, out.dtype

N_MICROBATCH = 1
ADAM_LR = 0.001
ADAM_B1 = 0.9
ADAM_B2 = 0.999
ADAM_EPS = 1e-08
ADAM_WD = 0.01
ADAM_STEP = 10
PER_EXAMPLE_BATCH_AXIS = {'x': 0, 'loss_target': 0}
SHARED_INPUTS = []
_WEIGHT_DTYPES = {'norm1_g': _jnp.float32, 'w_in': _jnp.float32, 'b_forget': _jnp.float32, 'pool_mix': _jnp.float32, 'pool_scale': _jnp.float32, 'w_pool_out': _jnp.float32, 'w_attn_out': _jnp.float32, 'w_out': _jnp.float32, 'norm2_g': _jnp.float32, 'w_ffn_gate': _jnp.float32, 'w_ffn_up': _jnp.float32, 'w_ffn_down': _jnp.float32, 'norm_f_g': _jnp.float32}
MOMENT_SCALE = {'norm1_g': 1.262269e-01, 'w_in': 5.370638e-02, 'b_forget': 3.632744e-01, 'pool_mix': 1.182657e-01, 'pool_scale': 1.302165e-01, 'w_pool_out': 8.408672e-02, 'w_attn_out': 4.018798e-02, 'w_out': 9.194831e-02, 'norm2_g': 1.228831e-01, 'w_ffn_gate': 5.142384e-02, 'w_ffn_up': 4.978921e-02, 'w_ffn_down': 8.252188e-02, 'norm_f_g': 3.197834e+01}


def _to_microbatches(a, axis):
    t = _jnp.moveaxis(a, axis, 0)
    t = t.reshape((N_MICROBATCH, t.shape[0] // N_MICROBATCH) + t.shape[1:])
    return _jnp.moveaxis(t, 1, axis + 1)


def setup_inputs(seed: int = 0) -> dict:
    inp = _fwd_setup_inputs(seed)
    key = _jax.random.fold_in(_jax.random.key(seed), 7919)
    shape, _ = _output_shape()
    out = dict(inp)
    out["loss_target"] = _jax.random.normal(_jax.random.fold_in(key, 0), shape, _jnp.float32)
    for i, name in enumerate(TWIN_WEIGHTS):
        w = inp[name].astype(_jnp.float32)
        if MOMENT_SCALE is None:
            s = _jnp.sqrt(_jnp.mean(_jnp.square(w)) + 1e-30)
        else:
            s = MOMENT_SCALE[name]
        km, kv = _jax.random.split(_jax.random.fold_in(key, i + 1))
        out[name] = w
        out["m_" + name] = s * _jax.random.normal(km, w.shape, _jnp.float32)
        out["v_" + name] = (s * s) * _jax.random.uniform(kv, w.shape, _jnp.float32, 0.5, 1.5)
    if N_MICROBATCH > 1:
        for name, axis in PER_EXAMPLE_BATCH_AXIS.items():
            out[name] = _to_microbatches(out[name], axis)
    return {'x': out['x'], 'norm1_g': out['norm1_g'], 'w_in': out['w_in'], 'b_forget': out['b_forget'], 'pool_mix': out['pool_mix'], 'pool_scale': out['pool_scale'], 'w_pool_out': out['w_pool_out'], 'w_attn_out': out['w_attn_out'], 'w_out': out['w_out'], 'norm2_g': out['norm2_g'], 'w_ffn_gate': out['w_ffn_gate'], 'w_ffn_up': out['w_ffn_up'], 'w_ffn_down': out['w_ffn_down'], 'norm_f_g': out['norm_f_g'], 'loss_target': out['loss_target'], 'm_norm1_g': out['m_norm1_g'], 'm_w_in': out['m_w_in'], 'm_b_forget': out['m_b_forget'], 'm_pool_mix': out['m_pool_mix'], 'm_pool_scale': out['m_pool_scale'], 'm_w_pool_out': out['m_w_pool_out'], 'm_w_attn_out': out['m_w_attn_out'], 'm_w_out': out['m_w_out'], 'm_norm2_g': out['m_norm2_g'], 'm_w_ffn_gate': out['m_w_ffn_gate'], 'm_w_ffn_up': out['m_w_ffn_up'], 'm_w_ffn_down': out['m_w_ffn_down'], 'm_norm_f_g': out['m_norm_f_g'], 'v_norm1_g': out['v_norm1_g'], 'v_w_in': out['v_w_in'], 'v_b_forget': out['v_b_forget'], 'v_pool_mix': out['v_pool_mix'], 'v_pool_scale': out['v_pool_scale'], 'v_w_pool_out': out['v_w_pool_out'], 'v_w_attn_out': out['v_w_attn_out'], 'v_w_out': out['v_w_out'], 'v_norm2_g': out['v_norm2_g'], 'v_w_ffn_gate': out['v_w_ffn_gate'], 'v_w_ffn_up': out['v_w_ffn_up'], 'v_w_ffn_down': out['v_w_ffn_down'], 'v_norm_f_g': out['v_norm_f_g']}


def _loss(weights, diff, rest, loss_target):
    with _jax.named_scope("forward"):
        args = {**rest, TWIN_DIFF_INPUT: diff, **{k: w.astype(_WEIGHT_DTYPES[k]) for k, w in weights.items()}}
        y = _forward(args)
    with _jax.named_scope("loss_head"):
        err = _jnp.square(y.astype(_jnp.float32) - loss_target)
        return 0.5 * _jnp.sum(_jnp.mean(err, axis=-1)) if err.ndim else 0.5 * err


def _adamw(w, g, m, v):
    m = ADAM_B1 * m + (1.0 - ADAM_B1) * g
    v = ADAM_B2 * v + (1.0 - ADAM_B2) * _jnp.square(g)
    m_hat = m / (1.0 - ADAM_B1 ** ADAM_STEP)
    v_hat = v / (1.0 - ADAM_B2 ** ADAM_STEP)
    delta = -ADAM_LR * (m_hat / (_jnp.sqrt(v_hat) + ADAM_EPS) + ADAM_WD * w)
    return delta, m, v


def reference(x, norm1_g, w_in, b_forget, pool_mix, pool_scale, w_pool_out, w_attn_out, w_out, norm2_g, w_ffn_gate, w_ffn_up, w_ffn_down, norm_f_g, loss_target, m_norm1_g, m_w_in, m_b_forget, m_pool_mix, m_pool_scale, m_w_pool_out, m_w_attn_out, m_w_out, m_norm2_g, m_w_ffn_gate, m_w_ffn_up, m_w_ffn_down, m_norm_f_g, v_norm1_g, v_w_in, v_b_forget, v_pool_mix, v_pool_scale, v_w_pool_out, v_w_attn_out, v_w_out, v_norm2_g, v_w_ffn_gate, v_w_ffn_up, v_w_ffn_down, v_norm_f_g):
    given = dict(x=x, norm1_g=norm1_g, w_in=w_in, b_forget=b_forget, pool_mix=pool_mix, pool_scale=pool_scale, w_pool_out=w_pool_out, w_attn_out=w_attn_out, w_out=w_out, norm2_g=norm2_g, w_ffn_gate=w_ffn_gate, w_ffn_up=w_ffn_up, w_ffn_down=w_ffn_down, norm_f_g=norm_f_g, loss_target=loss_target, m_norm1_g=m_norm1_g, m_w_in=m_w_in, m_b_forget=m_b_forget, m_pool_mix=m_pool_mix, m_pool_scale=m_pool_scale, m_w_pool_out=m_w_pool_out, m_w_attn_out=m_w_attn_out, m_w_out=m_w_out, m_norm2_g=m_norm2_g, m_w_ffn_gate=m_w_ffn_gate, m_w_ffn_up=m_w_ffn_up, m_w_ffn_down=m_w_ffn_down, m_norm_f_g=m_norm_f_g, v_norm1_g=v_norm1_g, v_w_in=v_w_in, v_b_forget=v_b_forget, v_pool_mix=v_pool_mix, v_pool_scale=v_pool_scale, v_w_pool_out=v_w_pool_out, v_w_attn_out=v_w_attn_out, v_w_out=v_w_out, v_norm2_g=v_norm2_g, v_w_ffn_gate=v_w_ffn_gate, v_w_ffn_up=v_w_ffn_up, v_w_ffn_down=v_w_ffn_down, v_norm_f_g=v_norm_f_g)
    weights = {n: given[n] for n in TWIN_WEIGHTS}
    shared = {n: given[n] for n in SHARED_INPUTS}
    per_example = {n: given[n] for n in ['x']}
    grad_fn = _jax.value_and_grad(_loss, argnums=(0, 1))

    def one_microbatch(ex, loss_target):
        ex = dict(ex)
        diff = ex.pop(TWIN_DIFF_INPUT)
        return grad_fn(weights, diff, {**shared, **ex}, loss_target)

    if N_MICROBATCH == 1:
        loss, (grad_w, grad_x) = one_microbatch(per_example, given["loss_target"])
    else:
        def body(carry, xs):
            loss_sum, grad_sum = carry
            l_k, (gw_k, gx_k) = one_microbatch(xs[0], xs[1])
            with _jax.named_scope("update"):
                return (loss_sum + l_k, _jax.tree.map(_jnp.add, grad_sum, gw_k)), gx_k

        init = (_jnp.zeros((), _jnp.float32), _jax.tree.map(_jnp.zeros_like, weights))
        (loss, grad_w), grad_x = _jax.lax.scan(body, init, (per_example, given["loss_target"]))
    with _jax.named_scope("update"):
        delta_w, new_m, new_v = {}, {}, {}
        for n in TWIN_WEIGHTS:
            delta_w[n], new_m[n], new_v[n] = _adamw(weights[n], grad_w[n], given["m_" + n], given["v_" + n])
    return (loss, grad_x, *[grad_w[n] for n in TWIN_WEIGHTS], *[delta_w[n] for n in TWIN_WEIGHTS],
            *[new_m[n] for n in TWIN_WEIGHTS], *[new_v[n] for n in TWIN_WEIGHTS])
```

```python
import functools

import jax
import jax.numpy as jnp
from jax import lax
from jax.experimental import pallas as pl
from jax.experimental.pallas import tpu as pltpu

F32 = jnp.float32
BF16 = jnp.bfloat16

D_MODEL = 1024
POOL_WINDOWS = (2, 4, 8, 16)
POOL_GROUPS = 4
POOL_GROUP_DIM = 128
POOL_WIDTH = 512
HEAD_DIM = 64
N_HEADS = 8
ATTN_WIDTH = 512
D_FF = 2816
RMS_EPS = 1e-6
ATTN_SCALE = HEAD_DIM ** -0.5
NEG_BIG = -1e30

ADAM_LR = 0.001
ADAM_B1 = 0.9
ADAM_B2 = 0.999
ADAM_EPS = 1e-08
ADAM_WD = 0.01
ADAM_STEP = 10

LANES = 128
N_CHIPS = 4
N_DEV = 8
VMEM_LIMIT_V7X = 52 * 1024 * 1024
MESH = pl.DeviceIdType.MESH
ANY = pl.BlockSpec(memory_space=pl.ANY)


def _cparams(*sem):
    return pltpu.CompilerParams(dimension_semantics=sem if sem else None, vmem_limit_bytes=VMEM_LIMIT_V7X)


def _dot(a, b):
    return lax.dot_general(a, b, (((1,), (0,)), ((), ())), preferred_element_type=F32)


def _dot_nt(a, b):
    return lax.dot_general(a, b, (((1,), (1,)), ((), ())), preferred_element_type=F32)


def _dot_tn(a, b):
    return lax.dot_general(a, b, (((0,), (0,)), ((), ())), preferred_element_type=F32)


def _sigmoid(x):
    return jax.nn.sigmoid(x)


def _rms_fwd(x, g):
    r = lax.rsqrt(jnp.mean(x * x, axis=-1, keepdims=True) + RMS_EPS)
    return (x * r) * g


def _rms_bwd(x, g, dy):
    r = lax.rsqrt(jnp.mean(x * x, axis=-1, keepdims=True) + RMS_EPS)
    xh = x * r
    dg = jnp.sum(dy * xh, axis=0, keepdims=True)
    dxh = dy * g
    dx = r * (dxh - xh * jnp.mean(dxh * xh, axis=-1, keepdims=True))
    return dx, dg


def _matmul(name, a, b, mode, out_dtype, tm, tn, tk):
    if mode == "nn":
        (m, k), (_, n) = a.shape, b.shape
    elif mode == "nt":
        (m, k), (n, _) = a.shape, b.shape
    else:
        (k, m), (_, n) = a.shape, b.shape
    tm, tn, tk = min(tm, m), min(tn, n), min(tk, k)
    assert m % tm == 0 and n % tn == 0 and k % tk == 0, (name, m, n, k, tm, tn, tk)
    nk = k // tk
    if mode == "tn":
        a_spec = pl.BlockSpec((tk, tm), lambda i, j, kk: (kk, i))
    else:
        a_spec = pl.BlockSpec((tm, tk), lambda i, j, kk: (i, kk))
    if mode == "nt":
        b_spec = pl.BlockSpec((tn, tk), lambda i, j, kk: (j, kk))
    else:
        b_spec = pl.BlockSpec((tk, tn), lambda i, j, kk: (kk, j))
    dot = {"nn": _dot, "nt": _dot_nt, "tn": _dot_tn}[mode]
    use_scratch = nk > 1 and out_dtype != F32

    def body(a_ref, b_ref, o_ref, *scratch):
        prod = dot(a_ref[...].astype(BF16), b_ref[...].astype(BF16))
        if nk == 1:
            o_ref[...] = prod.astype(out_dtype)
            return
        acc = scratch[0] if use_scratch else o_ref
        kk = pl.program_id(2)

        @pl.when(kk == 0)
        def _():
            acc[...] = prod

        @pl.when(kk > 0)
        def _():
            acc[...] += prod

        if use_scratch:
            @pl.when(kk == nk - 1)
            def _():
                o_ref[...] = acc[...].astype(out_dtype)

    return pl.pallas_call(
        body,
        name=name,
        out_shape=jax.ShapeDtypeStruct((m, n), out_dtype),
        grid=(m // tm, n // tn, nk),
        in_specs=[a_spec, b_spec],
        out_specs=pl.BlockSpec((tm, tn), lambda i, j, kk: (i, j)),
        scratch_shapes=[pltpu.VMEM((tm, tn), F32)] if use_scratch else [],
        compiler_params=_cparams("parallel", "parallel", "arbitrary"),
    )(a, b)


def _norm_fwd(name, x, g, tm):
    t, d = x.shape
    tm = min(tm, t)

    def body(x_ref, g_ref, h_ref):
        h_ref[...] = _rms_fwd(x_ref[...], g_ref[...]).astype(BF16)

    return pl.pallas_call(
        body, name=name, out_shape=jax.ShapeDtypeStruct((t, d), BF16), grid=(t // tm,),
        in_specs=[pl.BlockSpec((tm, d), lambda i: (i, 0)), pl.BlockSpec((1, d), lambda i: (0, 0))],
        out_specs=pl.BlockSpec((tm, d), lambda i: (i, 0)),
        compiler_params=_cparams("parallel"),
    )(x, g)


def _split3(x):
    hi = x.astype(BF16)
    r1 = x - hi.astype(F32)
    mid = r1.astype(BF16)
    lo = (r1 - mid.astype(F32)).astype(BF16)
    return hi, mid, lo


def _tri_dot(tri, x):
    hi, mid, lo = _split3(x)
    return _dot(tri, hi) + _dot(tri, mid) + _dot(tri, lo)


def _forget_fwd(h, wf, bf, seq):
    t, d = h.shape
    cb = min(256, seq)

    def body(h_ref, wf_ref, bf_ref, fl_ref, fc_ref):
        fl = _dot(h_ref[...], wf_ref[...])
        fl_ref[...] = fl
        xx = fl + bf_ref[...]
        lf = jnp.minimum(xx, 0.0) - jnp.log(1.0 + jnp.exp(-jnp.abs(xx)))
        ri = lax.broadcasted_iota(jnp.int32, (cb, cb), 0)
        ci = lax.broadcasted_iota(jnp.int32, (cb, cb), 1)
        tri = (ri >= ci).astype(BF16)
        carry = jnp.zeros((1, LANES), F32)
        for blk in range(seq // cb):
            cs = _tri_dot(tri, lf[blk * cb:(blk + 1) * cb]) + carry
            fc_ref[blk * cb:(blk + 1) * cb, :] = cs
            carry = cs[cb - 1:cb, :]

    return pl.pallas_call(
        body, name="forget_fwd",
        out_shape=(jax.ShapeDtypeStruct((t, LANES), F32), jax.ShapeDtypeStruct((t, LANES), F32)),
        grid=(t // seq,),
        in_specs=[pl.BlockSpec((seq, d), lambda b: (b, 0)), pl.BlockSpec((d, LANES), lambda b: (0, 0)),
                  pl.BlockSpec((1, LANES), lambda b: (0, 0))],
        out_specs=(pl.BlockSpec((seq, LANES), lambda b: (b, 0)), pl.BlockSpec((seq, LANES), lambda b: (b, 0))),
        compiler_params=_cparams("parallel"),
    )(h, wf, bf)


def _pool_fwd(u, mix, scale, seq):
    t = u.shape[0]

    def body(u_ref, mix_ref, sc_ref, p_ref, ps_ref):
        tpos = lax.broadcasted_iota(jnp.int32, (seq, POOL_GROUP_DIM), 0)
        for g in range(POOL_GROUPS):
            sl = slice(g * POOL_GROUP_DIM, (g + 1) * POOL_GROUP_DIM)
            ug = u_ref[:, sl]
            s = ug
            for lvl in range(g + 1):
                d = 2 ** lvl
                s = s + jnp.where(tpos >= d, pltpu.roll(s, d, 0), 0.0)
            cnt = jnp.minimum(tpos + 1, POOL_WINDOWS[g]).astype(F32)
            pb = (s / cnt - ug).astype(BF16)
            p_ref[:, sl] = pb
            ps_ref[:, sl] = (_dot(pb, mix_ref[g]) * sc_ref[:, sl]).astype(BF16)

    return pl.pallas_call(
        body, name="pool_fwd",
        out_shape=(jax.ShapeDtypeStruct((t, POOL_WIDTH), BF16), jax.ShapeDtypeStruct((t, POOL_WIDTH), BF16)),
        grid=(t // seq,),
        in_specs=[pl.BlockSpec((seq, POOL_WIDTH), lambda b: (b, 0)),
                  pl.BlockSpec((POOL_GROUPS, POOL_GROUP_DIM, POOL_GROUP_DIM), lambda b: (0, 0, 0)),
                  pl.BlockSpec((1, POOL_WIDTH), lambda b: (0, 0))],
        out_specs=(pl.BlockSpec((seq, POOL_WIDTH), lambda b: (b, 0)), pl.BlockSpec((seq, POOL_WIDTH), lambda b: (b, 0))),
        compiler_params=_cparams("parallel"),
    )(u, mix, scale)


def _attn_fwd(qkv, fcol, frow, seq, tq):
    t = qkv.shape[0]
    nq = seq // tq

    def body(q_ref, k_ref, v_ref, fc_ref, fr_ref, o_ref, lse_ref):
        i = pl.program_id(1)
        rowg = i * tq + lax.broadcasted_iota(jnp.int32, (tq, tq), 0)
        coli = lax.broadcasted_iota(jnp.int32, (tq, tq), 1)
        for hd in range(N_HEADS):
            sl = slice(hd * HEAD_DIM, (hd + 1) * HEAD_DIM)
            qh = q_ref[:, sl]
            fc = fc_ref[:, hd:hd + 1]

            def step(j, carry, sl=sl, qh=qh, fc=fc, hd=hd):
                m, l, acc = carry
                r0 = pl.multiple_of(j * tq, tq)
                kh = k_ref[pl.ds(r0, tq), sl]
                vh = v_ref[pl.ds(r0, tq), sl]
                fr = fr_ref[hd, pl.ds(j, 1), :]
                s = _dot_nt(qh, kh) * ATTN_SCALE + (fc - fr)
                s = jnp.where(rowg >= j * tq + coli, s, NEG_BIG)
                m_new = jnp.maximum(m, jnp.max(s, axis=1, keepdims=True))
                alpha = jnp.exp(m - m_new)
                p = jnp.exp(s - m_new)
                l = alpha * l + jnp.sum(p, axis=1, keepdims=True)
                acc = alpha * acc + _dot(p.astype(BF16), vh)
                return m_new, l, acc

            init = (jnp.full((tq, 1), NEG_BIG, F32), jnp.zeros((tq, 1), F32), jnp.zeros((tq, HEAD_DIM), F32))
            m, l, acc = lax.fori_loop(0, i + 1, step, init)
            o_ref[:, sl] = acc / l
            lse_ref[:, hd:hd + 1] = m + jnp.log(l)

    return pl.pallas_call(
        body, name="attn_fwd",
        out_shape=(jax.ShapeDtypeStruct((t, ATTN_WIDTH), F32), jax.ShapeDtypeStruct((t, N_HEADS), F32)),
        grid=(t // seq, nq),
        in_specs=[pl.BlockSpec((tq, ATTN_WIDTH), lambda b, i: (b * nq + i, 0)),
                  pl.BlockSpec((seq, ATTN_WIDTH), lambda b, i: (b, 1)),
                  pl.BlockSpec((seq, ATTN_WIDTH), lambda b, i: (b, 2)),
                  pl.BlockSpec((tq, N_HEADS), lambda b, i: (b * nq + i, 0)),
                  pl.BlockSpec((N_HEADS, nq, tq), lambda b, i: (0, b, 0))],
        out_specs=(pl.BlockSpec((tq, ATTN_WIDTH), lambda b, i: (b * nq + i, 0)),
                   pl.BlockSpec((tq, N_HEADS), lambda b, i: (b * nq + i, 0))),
        compiler_params=_cparams("parallel", "arbitrary"),
    )(qkv, qkv, qkv, fcol, frow)


def _merge_fwd(x, ps, o, g2, wpo, wao, wout, tm):
    t, d = x.shape
    tm = min(tm, t)

    def body(x_ref, ps_ref, o_ref, gp_ref, ga_ref, wpo_ref, wao_ref, wout_ref, mg_ref, x1_ref):
        py = _dot(ps_ref[...], wpo_ref[...])
        ay = _dot(o_ref[...].astype(BF16), wao_ref[...])
        mb = (_sigmoid(gp_ref[...]) * py + _sigmoid(ga_ref[...]) * ay).astype(BF16)
        mg_ref[...] = mb
        x1_ref[...] = x_ref[...] + _dot(mb, wout_ref[...])

    row = lambda w: pl.BlockSpec((tm, w), lambda i: (i, 0))
    full = lambda a: pl.BlockSpec(a.shape, lambda i: (0, 0))
    return pl.pallas_call(
        body, name="merge_fwd",
        out_shape=(jax.ShapeDtypeStruct((t, d), BF16), jax.ShapeDtypeStruct((t, d), F32)),
        grid=(t // tm,),
        in_specs=[row(d), row(POOL_WIDTH), row(ATTN_WIDTH), pl.BlockSpec((tm, d), lambda i: (i, 0)),
                  pl.BlockSpec((tm, d), lambda i: (i, 1)), full(wpo), full(wao), full(wout)],
        out_specs=(row(d), row(d)),
        compiler_params=_cparams("parallel"),
    )(x, ps, o, g2, g2, wpo, wao, wout)


def _ffn_fwd(x1, g, wg, wu, wd, tm, tf):
    t, d = x1.shape
    f = wg.shape[1]
    tm = min(tm, t)
    nf = f // tf

    def body(x1_ref, g_ref, wg_ref, wu_ref, wd_ref, h2_ref, gt_ref, up_ref, act_ref, x2_ref):
        j = pl.program_id(1)

        @pl.when(j == 0)
        def _():
            h2_ref[...] = _rms_fwd(x1_ref[...], g_ref[...]).astype(BF16)

        h2 = h2_ref[...]
        gt = _dot(h2, wg_ref[...])
        up = _dot(h2, wu_ref[...])
        act = (gt * _sigmoid(gt) * up).astype(BF16)
        gt_ref[...] = gt
        up_ref[...] = up
        act_ref[...] = act
        prod = _dot(act, wd_ref[...])

        @pl.when(j == 0)
        def _():
            x2_ref[...] = prod

        @pl.when(j > 0)
        def _():
            x2_ref[...] += prod

        @pl.when(j == nf - 1)
        def _():
            x2_ref[...] += x1_ref[...]

    return pl.pallas_call(
        body, name="ffn_fwd",
        out_shape=(jax.ShapeDtypeStruct((t, d), BF16), jax.ShapeDtypeStruct((t, f), F32),
                   jax.ShapeDtypeStruct((t, f), F32), jax.ShapeDtypeStruct((t, f), BF16),
                   jax.ShapeDtypeStruct((t, d), F32)),
        grid=(t // tm, nf),
        in_specs=[pl.BlockSpec((tm, d), lambda i, j: (i, 0)), pl.BlockSpec((1, d), lambda i, j: (0, 0)),
                  pl.BlockSpec((d, tf), lambda i, j: (0, j)), pl.BlockSpec((d, tf), lambda i, j: (0, j)),
                  pl.BlockSpec((tf, d), lambda i, j: (j, 0))],
        out_specs=(pl.BlockSpec((tm, d), lambda i, j: (i, 0)), pl.BlockSpec((tm, tf), lambda i, j: (i, j)),
                   pl.BlockSpec((tm, tf), lambda i, j: (i, j)), pl.BlockSpec((tm, tf), lambda i, j: (i, j)),
                   pl.BlockSpec((tm, d), lambda i, j: (i, 0))),
        compiler_params=_cparams("parallel", "arbitrary"),
    )(x1, g, wg, wu, wd)


def _final_fwd_bwd(x2, target, g, tm):
    t, d = x2.shape
    tm = min(tm, t)

    def body(x_ref, t_ref, g_ref, loss_ref, dx_ref, dg_ref):
        i = pl.program_id(0)
        x = x_ref[...]
        gg = g_ref[...]
        err = _rms_fwd(x, gg) - t_ref[...]
        part = 0.5 * jnp.sum(jnp.mean(err * err, axis=-1, keepdims=True), axis=0, keepdims=True)
        dx, dg = _rms_bwd(x, gg, err * (1.0 / d))
        dx_ref[...] = dx

        @pl.when(i == 0)
        def _():
            loss_ref[...] = jnp.zeros_like(loss_ref)
            dg_ref[...] = jnp.zeros_like(dg_ref)

        loss_ref[...] += jnp.broadcast_to(part, loss_ref.shape)
        dg_ref[...] += dg

    return pl.pallas_call(
        body, name="final_fwd_bwd",
        out_shape=(jax.ShapeDtypeStruct((1, LANES), F32), jax.ShapeDtypeStruct((t, d), F32),
                   jax.ShapeDtypeStruct((1, d), F32)),
        grid=(t // tm,),
        in_specs=[pl.BlockSpec((tm, d), lambda i: (i, 0)), pl.BlockSpec((tm, d), lambda i: (i, 0)),
                  pl.BlockSpec((1, d), lambda i: (0, 0))],
        out_specs=(pl.BlockSpec((1, LANES), lambda i: (0, 0)), pl.BlockSpec((tm, d), lambda i: (i, 0)),
                   pl.BlockSpec((1, d), lambda i: (0, 0))),
        compiler_params=_cparams("arbitrary"),
    )(x2, target, g)


def _ffn_bwd(dx2, x1, g, gt, up, wg, wu, wd, tm, tf):
    t, d = dx2.shape
    f = gt.shape[1]
    tm = min(tm, t)
    nf = f // tf

    def body(dx2_ref, x1_ref, g_ref, gt_ref, up_ref, wg_ref, wu_ref, wd_ref, dgt_ref, dup_ref, dx1_ref, dg_ref, acc_ref):
        i, j = pl.program_id(0), pl.program_id(1)
        dact = _dot_nt(dx2_ref[...].astype(BF16), wd_ref[...])
        gtv = gt_ref[...]
        sg = _sigmoid(gtv)
        dup = (dact * (gtv * sg)).astype(BF16)
        dgt = (dact * up_ref[...] * (sg * (1.0 + gtv * (1.0 - sg)))).astype(BF16)
        dgt_ref[...] = dgt
        dup_ref[...] = dup
        contrib = _dot_nt(dgt, wg_ref[...]) + _dot_nt(dup, wu_ref[...])

        @pl.when(j == 0)
        def _():
            acc_ref[...] = contrib

        @pl.when(j > 0)
        def _():
            acc_ref[...] += contrib

        @pl.when(jnp.logical_and(i == 0, j == 0))
        def _():
            dg_ref[...] = jnp.zeros_like(dg_ref)

        @pl.when(j == nf - 1)
        def _():
            dxn, dg = _rms_bwd(x1_ref[...], g_ref[...], acc_ref[...])
            dx1_ref[...] = dx2_ref[...] + dxn
            dg_ref[...] += dg

    return pl.pallas_call(
        body, name="ffn_bwd",
        out_shape=(jax.ShapeDtypeStruct((t, f), BF16), jax.ShapeDtypeStruct((t, f), BF16),
                   jax.ShapeDtypeStruct((t, d), F32), jax.ShapeDtypeStruct((1, d), F32)),
        grid=(t // tm, nf),
        in_specs=[pl.BlockSpec((tm, d), lambda i, j: (i, 0)), pl.BlockSpec((tm, d), lambda i, j: (i, 0)),
                  pl.BlockSpec((1, d), lambda i, j: (0, 0)),
                  pl.BlockSpec((tm, tf), lambda i, j: (i, j)), pl.BlockSpec((tm, tf), lambda i, j: (i, j)),
                  pl.BlockSpec((d, tf), lambda i, j: (0, j)), pl.BlockSpec((d, tf), lambda i, j: (0, j)),
                  pl.BlockSpec((tf, d), lambda i, j: (j, 0))],
        out_specs=(pl.BlockSpec((tm, tf), lambda i, j: (i, j)), pl.BlockSpec((tm, tf), lambda i, j: (i, j)),
                   pl.BlockSpec((tm, d), lambda i, j: (i, 0)), pl.BlockSpec((1, d), lambda i, j: (0, 0))),
        scratch_shapes=[pltpu.VMEM((tm, d), F32)],
        compiler_params=_cparams("arbitrary", "arbitrary"),
    )(dx2, x1, g, gt, up, wg, wu, wd)


def _merge_bwd(dx1, ps, o, g2, wpo, wao, wout, tm):
    t, d = dx1.shape
    tm = min(tm, t)

    def body(dx1_ref, ps_ref, o_ref, gp_ref, ga_ref, wpo_ref, wao_ref, wout_ref, dpy_ref, day_ref, dg2_ref, dps_ref, da_ref):
        dm = _dot_nt(dx1_ref[...].astype(BF16), wout_ref[...])
        py = _dot(ps_ref[...], wpo_ref[...])
        ay = _dot(o_ref[...].astype(BF16), wao_ref[...])
        sp = _sigmoid(gp_ref[...])
        sa = _sigmoid(ga_ref[...])
        dpy = (dm * sp).astype(BF16)
        day = (dm * sa).astype(BF16)
        dpy_ref[...] = dpy
        day_ref[...] = day
        dg2_ref[:, :d] = (dm * py * (sp * (1.0 - sp))).astype(BF16)
        dg2_ref[:, d:] = (dm * ay * (sa * (1.0 - sa))).astype(BF16)
        dps_ref[...] = _dot_nt(dpy, wpo_ref[...])
        da_ref[...] = _dot_nt(day, wao_ref[...]).astype(BF16)

    row = lambda w: pl.BlockSpec((tm, w), lambda i: (i, 0))
    full = lambda a: pl.BlockSpec(a.shape, lambda i: (0, 0))
    return pl.pallas_call(
        body, name="merge_bwd",
        out_shape=(jax.ShapeDtypeStruct((t, d), BF16), jax.ShapeDtypeStruct((t, d), BF16),
                   jax.ShapeDtypeStruct((t, 2 * d), BF16), jax.ShapeDtypeStruct((t, POOL_WIDTH), F32),
                   jax.ShapeDtypeStruct((t, ATTN_WIDTH), BF16)),
        grid=(t // tm,),
        in_specs=[row(d), row(POOL_WIDTH), row(ATTN_WIDTH), pl.BlockSpec((tm, d), lambda i: (i, 0)),
                  pl.BlockSpec((tm, d), lambda i: (i, 1)), full(wpo), full(wao), full(wout)],
        out_specs=(row(d), row(d), row(2 * d), row(POOL_WIDTH), row(ATTN_WIDTH)),
        compiler_params=_cparams("parallel"),
    )(dx1, ps, o, g2, g2, wpo, wao, wout)


def _attn_bwd(qkv, do, lse4, fcol4, frow, seq, tq):
    t = qkv.shape[0]
    nq = seq // tq
    hp_n = N_HEADS // 2

    def body(q_ref, k_ref, v_ref, do_ref, lse_ref, fc_ref, fr_ref, dq_ref, dk_ref, dv_ref, dfr_ref,
             dk_acc, dv_acc, p_buf, dp_buf):
        rowi = lax.broadcasted_iota(jnp.int32, (tq, tq), 0)
        coli = lax.broadcasted_iota(jnp.int32, (tq, tq), 1)
        dfr_ref[...] = jnp.zeros_like(dfr_ref)
        for e in range(2):
            sl = slice(e * HEAD_DIM, (e + 1) * HEAD_DIM)
            dk_acc[...] = jnp.zeros_like(dk_acc)
            dv_acc[...] = jnp.zeros_like(dv_acc)

            def q_step(i, _, sl=sl, e=e):
                q0 = pl.multiple_of(i * tq, tq)
                qh = q_ref[pl.ds(q0, tq), sl]
                doh = do_ref[pl.ds(q0, tq), sl]
                ls = lse_ref[pl.ds(q0, tq), e:e + 1]
                fc = fc_ref[pl.ds(q0, tq), e:e + 1]

                def sweep1(j, dl):
                    r0 = pl.multiple_of(j * tq, tq)
                    kh = k_ref[pl.ds(r0, tq), sl]
                    vh = v_ref[pl.ds(r0, tq), sl]
                    fr = fr_ref[e, pl.ds(j, 1), :]
                    s = _dot_nt(qh, kh) * ATTN_SCALE + (fc - fr)
                    s = jnp.where(q0 + rowi >= r0 + coli, s, NEG_BIG)
                    p = jnp.exp(s - ls)
                    dp = _dot_nt(doh, vh)
                    p_buf[j] = p
                    dp_buf[j] = dp
                    dv_acc[pl.ds(r0, tq), :] += _dot_tn(p.astype(BF16), doh)
                    return dl + jnp.sum(p * dp, axis=1, keepdims=True)

                dl = lax.fori_loop(0, i + 1, sweep1, jnp.zeros((tq, 1), F32))

                def sweep2(j, dq):
                    r0 = pl.multiple_of(j * tq, tq)
                    kh = k_ref[pl.ds(r0, tq), sl]
                    ds = p_buf[j] * (dp_buf[j] - dl)
                    dfr_ref[e, pl.ds(j, 1), :] += jnp.sum(ds, axis=0, keepdims=True)
                    dsb = ds.astype(BF16)
                    dk_acc[pl.ds(r0, tq), :] += _dot_tn(dsb, qh)
                    return dq + _dot(dsb, kh)

                dq = lax.fori_loop(0, i + 1, sweep2, jnp.zeros((tq, HEAD_DIM), F32))
                dq_ref[pl.ds(q0, tq), sl] = (dq * ATTN_SCALE).astype(BF16)
                return 0

            lax.fori_loop(0, nq, q_step, 0)
            dk_ref[:, sl] = (dk_acc[...] * ATTN_SCALE).astype(BF16)
            dv_ref[:, sl] = dv_acc[...].astype(BF16)

    col = lambda off: pl.BlockSpec((seq, LANES), lambda b, hp: (b, off + hp))
    pair = pl.BlockSpec((None, seq, 2), lambda b, hp: (hp, b, 0))
    rows = pl.BlockSpec((2, nq, tq), lambda b, hp: (hp, b, 0))
    return pl.pallas_call(
        body, name="attn_bwd",
        out_shape=(jax.ShapeDtypeStruct((t, ATTN_WIDTH), BF16),) * 3 + (jax.ShapeDtypeStruct((N_HEADS, t // tq, tq), F32),),
        grid=(t // seq, hp_n),
        in_specs=[col(0), col(hp_n), col(2 * hp_n), col(0), pair, pair, rows],
        out_specs=(col(0), col(0), col(0), rows),
        scratch_shapes=[pltpu.VMEM((seq, HEAD_DIM), F32), pltpu.VMEM((seq, HEAD_DIM), F32),
                        pltpu.VMEM((nq, tq, tq), F32), pltpu.VMEM((nq, tq, tq), F32)],
        compiler_params=_cparams("parallel", "arbitrary"),
    )(qkv, qkv, qkv, do, lse4, fcol4, frow)


def _forget_bwd(dfc, fl, bf, seq):
    t = fl.shape[0]
    cb = min(256, seq)
    nb = seq // cb

    def body(dfc_ref, fl_ref, bf_ref, dfl_ref, db_ref):
        b = pl.program_id(0)
        ri = lax.broadcasted_iota(jnp.int32, (cb, cb), 0)
        ci = lax.broadcasted_iota(jnp.int32, (cb, cb), 1)
        tri = (ci >= ri).astype(BF16)
        carry = jnp.zeros((1, LANES), F32)
        dbs = jnp.zeros((1, LANES), F32)
        for blk in reversed(range(nb)):
            rs = slice(blk * cb, (blk + 1) * cb)
            dlf = _tri_dot(tri, -dfc_ref[rs, :]) + carry
            carry = dlf[0:1, :]
            dfl = dlf * _sigmoid(-(fl_ref[rs, :] + bf_ref[...]))
            dfl_ref[rs, :] = dfl.astype(BF16)
            dbs = dbs + jnp.sum(dfl, axis=0, keepdims=True)

        @pl.when(b == 0)
        def _():
            db_ref[...] = jnp.zeros_like(db_ref)

        db_ref[...] += dbs

    return pl.pallas_call(
        body, name="forget_bwd",
        out_shape=(jax.ShapeDtypeStruct((t, LANES), BF16), jax.ShapeDtypeStruct((1, LANES), F32)),
        grid=(t // seq,),
        in_specs=[pl.BlockSpec((seq, LANES), lambda b: (b, 0)), pl.BlockSpec((seq, LANES), lambda b: (b, 0)),
                  pl.BlockSpec((1, LANES), lambda b: (0, 0))],
        out_specs=(pl.BlockSpec((seq, LANES), lambda b: (b, 0)), pl.BlockSpec((1, LANES), lambda b: (0, 0))),
        compiler_params=_cparams("arbitrary"),
    )(dfc, fl, bf)


def _pool_bwd(dps, p, mix, scale, seq):
    t = dps.shape[0]

    def body(dps_ref, p_ref, mix_ref, sc_ref, du_ref, dmix_ref, dsc_ref):
        b = pl.program_id(0)

        @pl.when(b == 0)
        def _():
            dmix_ref[...] = jnp.zeros_like(dmix_ref)
            dsc_ref[...] = jnp.zeros_like(dsc_ref)

        tpos = lax.broadcasted_iota(jnp.int32, (seq, POOL_GROUP_DIM), 0)
        for g in range(POOL_GROUPS):
            sl = slice(g * POOL_GROUP_DIM, (g + 1) * POOL_GROUP_DIM)
            pb = p_ref[:, sl]
            dpsg = dps_ref[:, sl]
            pm = _dot(pb, mix_ref[g])
            dsc_ref[:, sl] += jnp.sum(dpsg * pm, axis=0, keepdims=True)
            dpm = (dpsg * sc_ref[:, sl]).astype(BF16)
            dmix_ref[g] += _dot_tn(pb, dpm)
            dp = _dot_nt(dpm, mix_ref[g])
            cnt = jnp.minimum(tpos + 1, POOL_WINDOWS[g]).astype(F32)
            s = dp / cnt
            for lvl in range(g + 1):
                d = 2 ** lvl
                s = s + jnp.where(tpos < seq - d, pltpu.roll(s, seq - d, 0), 0.0)
            du_ref[:, sl] = (s - dp).astype(BF16)

    return pl.pallas_call(
        body, name="pool_bwd",
        out_shape=(jax.ShapeDtypeStruct((t, POOL_WIDTH), BF16),
                   jax.ShapeDtypeStruct((POOL_GROUPS, POOL_GROUP_DIM, POOL_GROUP_DIM), F32),
                   jax.ShapeDtypeStruct((1, POOL_WIDTH), F32)),
        grid=(t // seq,),
        in_specs=[pl.BlockSpec((seq, POOL_WIDTH), lambda b: (b, 0)), pl.BlockSpec((seq, POOL_WIDTH), lambda b: (b, 0)),
                  pl.BlockSpec((POOL_GROUPS, POOL_GROUP_DIM, POOL_GROUP_DIM), lambda b: (0, 0, 0)),
                  pl.BlockSpec((1, POOL_WIDTH), lambda b: (0, 0))],
        out_specs=(pl.BlockSpec((seq, POOL_WIDTH), lambda b: (b, 0)),
                   pl.BlockSpec((POOL_GROUPS, POOL_GROUP_DIM, POOL_GROUP_DIM), lambda b: (0, 0, 0)),
                   pl.BlockSpec((1, POOL_WIDTH), lambda b: (0, 0))),
        compiler_params=_cparams("arbitrary"),
    )(dps, p, mix, scale)


def _in_bwd(du, dq, dk, dv, dg2, dfl, dx1, x, g, wu, wqkv, wg2, wft, tm):
    t, d = x.shape
    tm = min(tm, t)
    aw = ATTN_WIDTH

    def body(du_ref, dq_ref, dk_ref, dv_ref, dg2_ref, dfl_ref, dx1_ref, x_ref, g_ref, wu_ref, wqkv_ref, wg2_ref, wft_ref,
             dx_ref, dg_ref):
        i = pl.program_id(0)
        dh = _dot_nt(du_ref[...], wu_ref[...])
        dh += _dot_nt(dq_ref[...], wqkv_ref[:, 0:aw])
        dh += _dot_nt(dk_ref[...], wqkv_ref[:, aw:2 * aw])
        dh += _dot_nt(dv_ref[...], wqkv_ref[:, 2 * aw:3 * aw])
        dh += _dot_nt(dg2_ref[...], wg2_ref[...])
        dh += _dot(dfl_ref[...], wft_ref[...])
        dxn, dg = _rms_bwd(x_ref[...], g_ref[...], dh)
        dx_ref[...] = dx1_ref[...] + dxn

        @pl.when(i == 0)
        def _():
            dg_ref[...] = jnp.zeros_like(dg_ref)

        dg_ref[...] += dg

    row = lambda w: pl.BlockSpec((tm, w), lambda i: (i, 0))
    full = lambda a: pl.BlockSpec(a.shape, lambda i: (0, 0))
    return pl.pallas_call(
        body, name="in_bwd",
        out_shape=(jax.ShapeDtypeStruct((t, d), F32), jax.ShapeDtypeStruct((1, d), F32)),
        grid=(t // tm,),
        in_specs=[row(POOL_WIDTH), row(aw), row(aw), row(aw), row(2 * d), row(LANES), row(d), row(d),
                  pl.BlockSpec((1, d), lambda i: (0, 0)), full(wu), full(wqkv), full(wg2), full(wft)],
        out_specs=(row(d), pl.BlockSpec((1, d), lambda i: (0, 0))),
        compiler_params=_cparams("arbitrary"),
    )(du, dq, dk, dv, dg2, dfl, dx1, x, g, wu, wqkv, wg2, wft)


def _local_step(x, target, seq, norm1_g, w_in, b_forget, pool_mix, pool_scale, w_pool_out, w_attn_out, w_out,
                norm2_g, w_ffn_gate, w_ffn_up, w_ffn_down, norm_f_g):
    t, d = x.shape
    tq = min(256, seq)
    aw = ATTN_WIDTH
    o_q, o_f, o_g = POOL_WIDTH, POOL_WIDTH + 3 * aw, POOL_WIDTH + 3 * aw + N_HEADS
    wu = w_in[:, :o_q]
    wqkv = w_in[:, o_q:o_f]
    wf = jnp.pad(w_in[:, o_f:o_g], ((0, 0), (0, LANES - N_HEADS)))
    wg2 = w_in[:, o_g:]
    wft = wf.T
    bf = jnp.pad(b_forget, ((0, 0), (0, LANES - N_HEADS)))
    mixb = pool_mix.astype(BF16)

    h = _norm_fwd("norm1_fwd", x, norm1_g, 512)
    u = _matmul("mm_u", h, wu, "nn", F32, 1024, 512, d)
    qkv = _matmul("mm_qkv", h, wqkv, "nn", BF16, 1024, 512, d)
    g2 = _matmul("mm_gates", h, wg2, "nn", F32, 1024, 512, d)
    fl, fcum = _forget_fwd(h, wf, bf, seq)
    fcol = fcum[:, :N_HEADS]
    frow = fcol.T.reshape(N_HEADS, t // tq, tq)
    p, ps = _pool_fwd(u, mixb, pool_scale, seq)
    o, lse = _attn_fwd(qkv, fcol, frow, seq, tq)
    merged, x1 = _merge_fwd(x, ps, o, g2, w_pool_out, w_attn_out, w_out, 256)
    h2, gt, up, act, x2 = _ffn_fwd(x1, norm2_g, w_ffn_gate, w_ffn_up, w_ffn_down, 512, 256)
    loss, dx2, d_gf = _final_fwd_bwd(x2, target, norm_f_g, 512)

    dgt, dup, dx1, d_g2n = _ffn_bwd(dx2, x1, norm2_g, gt, up, w_ffn_gate, w_ffn_up, w_ffn_down, 512, 256)
    d_wd = _matmul("dw_down", act, dx2, "tn", F32, 1408, 1024, 512)
    d_wg = _matmul("dw_gate", h2, dgt, "tn", F32, 1024, 1408, 512)
    d_wu = _matmul("dw_up", h2, dup, "tn", F32, 1024, 1408, 512)
    dpy, day, dg2, dps, da = _merge_bwd(dx1, ps, o, g2, w_pool_out, w_attn_out, w_out, 256)
    d_wout = _matmul("dw_out", merged, dx1, "tn", F32, 1024, 1024, 512)
    d_wpo = _matmul("dw_pool_out", ps, dpy, "tn", F32, 512, 1024, 512)
    d_wao = _matmul("dw_attn_out", o, day, "tn", F32, 512, 1024, 512)
    pairs = lambda a: a.reshape(t, N_HEADS // 2, 2).transpose(1, 0, 2)
    dq, dk, dv, dfr = _attn_bwd(qkv, da, pairs(lse), pairs(fcol), frow, seq, tq)
    dfc = jnp.pad(dfr.reshape(N_HEADS, t).T, ((0, 0), (0, LANES - N_HEADS)))
    dfl, d_bf = _forget_bwd(dfc, fl, bf, seq)
    du, d_mix, d_scale = _pool_bwd(dps, p, mixb, pool_scale, seq)
    dx, d_g1 = _in_bwd(du, dq, dk, dv, dg2, dfl, dx1, x, norm1_g, wu, wqkv, wg2, wft, 256)
    d_wu_in = _matmul("dw_in_u", h, du, "tn", F32, 1024, 512, 512)
    d_wq = _matmul("dw_in_q", h, dq, "tn", F32, 1024, 512, 512)
    d_wk = _matmul("dw_in_k", h, dk, "tn", F32, 1024, 512, 512)
    d_wv = _matmul("dw_in_v", h, dv, "tn", F32, 1024, 512, 512)
    d_wf = _matmul("dw_in_f", h, dfl, "tn", F32, 1024, LANES, 512)
    d_wg2 = _matmul("dw_in_gates", h, dg2, "tn", F32, 1024, 1024, 512)
    d_win = jnp.concatenate([d_wu_in, d_wq, d_wk, d_wv, d_wf[:, :N_HEADS], d_wg2], axis=1)

    big = (d_win, d_wpo, d_wao, d_wout, d_wg, d_wu, d_wd)
    small = (d_g1, d_bf[:, :N_HEADS], d_mix, d_scale, d_g2n, d_gf)
    return loss, dx, big, small


def _position():
    return lax.axis_index("x"), lax.axis_index("y"), lax.axis_index("c")


def _remote(src, dst, send_sem, recv_sem, device):
    return pltpu.make_async_remote_copy(src_ref=src, dst_ref=dst, send_sem=send_sem, recv_sem=recv_sem,
                                        device_id=device, device_id_type=MESH)


def _gather_weights(shards):
    n = len(shards)

    def body(*refs):
        w, g = refs[:n], refs[n:2 * n]
        send, recv, loc = refs[2 * n:]
        x, y, c = _position()
        me = 2 * x + y
        sibling = (x, y, 1 - c)
        chips = [(1 - x, y), (x, 1 - y), (1 - x, 1 - y)]
        local, sends = [], []
        for k in range(n):
            cp = pltpu.make_async_copy(w[k], g[k].at[me], loc.at[k])
            cp.start()
            local.append(cp)
            for j, (ox, oy) in enumerate(chips):
                cp = _remote(w[k].at[c], g[k].at[me, c], send.at[6 * k + j], recv.at[6 * k + j], (ox, oy, c))
                cp.start()
                sends.append(cp)
        for k in range(n):
            for j, (ox, oy) in enumerate(chips):
                blk = g[k].at[2 * ox + oy, c]
                _remote(blk, blk, send.at[6 * k + j], recv.at[6 * k + j], (ox, oy, c)).wait_recv()
                cp = _remote(blk, blk, send.at[6 * k + 3 + j], recv.at[6 * k + 3 + j], sibling)
                cp.start()
                sends.append(cp)
        for k in range(n):
            for j, (ox, oy) in enumerate(chips):
                blk = g[k].at[2 * ox + oy, 1 - c]
                _remote(blk, blk, send.at[6 * k + 3 + j], recv.at[6 * k + 3 + j], sibling).wait_recv()
        for cp in sends:
            cp.wait_send()
        for cp in local:
            cp.wait()

    return pl.pallas_call(
        body, name="gather_weights",
        out_shape=tuple(jax.ShapeDtypeStruct((N_CHIPS,) + s.shape, s.dtype) for s in shards),
        in_specs=[ANY] * n, out_specs=tuple([ANY] * n),
        scratch_shapes=[pltpu.SemaphoreType.DMA((6 * n,)), pltpu.SemaphoreType.DMA((6 * n,)), pltpu.SemaphoreType.DMA((n,))],
        compiler_params=pltpu.CompilerParams(has_side_effects=True),
    )(*shards)


def _exchange(name, gives, keeps, peer_fn, give_fn, keep_fn, recv_shapes):
    n = len(gives)

    def body(*refs):
        give, keep = refs[:n], refs[n:2 * n]
        recv, kept = refs[2 * n:3 * n], refs[3 * n:4 * n]
        send_sem, recv_sem, loc = refs[4 * n:]
        pos = _position()
        peer = peer_fn(*pos)
        copies = []
        for k in range(n):
            cp = _remote(give_fn(give[k], *pos), recv[k], send_sem.at[k], recv_sem.at[k], peer)
            cp.start()
            copies.append(cp)
            cp = pltpu.make_async_copy(keep_fn(keep[k], *pos), kept[k], loc.at[k])
            cp.start()
            copies.append(cp)
        for cp in copies:
            cp.wait()

    outs = pl.pallas_call(
        body, name=name,
        out_shape=tuple(jax.ShapeDtypeStruct(s, g.dtype) for s, g in zip(recv_shapes, gives))
        + tuple(jax.ShapeDtypeStruct(s, k.dtype) for s, k in zip(recv_shapes, keeps)),
        in_specs=[ANY] * (2 * n), out_specs=tuple([ANY] * (2 * n)),
        scratch_shapes=[pltpu.SemaphoreType.DMA((n,)), pltpu.SemaphoreType.DMA((n,)), pltpu.SemaphoreType.DMA((n,))],
        compiler_params=pltpu.CompilerParams(has_side_effects=True),
    )(*gives, *keeps)
    return outs[:n], outs[n:]


def _add_pair(name, a, b, want_bf16):
    n, r, c = a.shape
    blk = pl.BlockSpec((None, r, c), lambda i: (i, 0, 0))

    def body(a_ref, b_ref, *outs):
        s = a_ref[...] + b_ref[...].astype(F32)
        outs[0][...] = s
        if want_bf16:
            outs[1][...] = s.astype(BF16)

    shapes = (jax.ShapeDtypeStruct(a.shape, F32),) + ((jax.ShapeDtypeStruct(a.shape, BF16),) if want_bf16 else ())
    outs = pl.pallas_call(
        body, name=name, out_shape=shapes, grid=(n,), in_specs=[blk, blk],
        out_specs=tuple([blk] * len(shapes)), compiler_params=_cparams("parallel"),
    )(a, b)
    return outs if want_bf16 else (outs[0], None)


def _reduce_scatter(grads):
    n = len(grads)
    shp = [g.shape[2:] for g in grads]
    recv, kept = _exchange(
        "rs_c", grads, grads, lambda x, y, c: (x, y, 1 - c),
        lambda ref, x, y, c: ref.at[1 - c], lambda ref, x, y, c: ref.at[c], [(N_CHIPS,) + s for s in shp])
    p1 = [_add_pair(f"rs_c_add{k}", kept[k], recv[k], True) for k in range(n)]
    recv, kept = _exchange(
        "rs_x", [p[1] for p in p1], [p[0] for p in p1], lambda x, y, c: (1 - x, y, c),
        lambda ref, x, y, c: ref.at[pl.ds(2 * (1 - x), 2)], lambda ref, x, y, c: ref.at[pl.ds(2 * x, 2)],
        [(2,) + s for s in shp])
    p2 = [_add_pair(f"rs_x_add{k}", kept[k], recv[k], True) for k in range(n)]
    recv, kept = _exchange(
        "rs_y", [p[1] for p in p2], [p[0] for p in p2], lambda x, y, c: (x, 1 - y, c),
        lambda ref, x, y, c: ref.at[pl.ds(1 - y, 1)], lambda ref, x, y, c: ref.at[pl.ds(y, 1)],
        [(1,) + s for s in shp])
    p3 = [_add_pair(f"rs_y_add{k}", kept[k], recv[k], False)[0] for k in range(n)]

    def body(*refs):
        src, out = refs[:n], refs[n:2 * n]
        send_sem, recv_sem, loc = refs[2 * n:]
        x, y, c = _position()
        copies = []
        for k in range(n):
            cp = _remote(src[k].at[0], out[k].at[c], send_sem.at[k], recv_sem.at[k], (x, y, 1 - c))
            cp.start()
            copies.append(cp)
            cp = pltpu.make_async_copy(src[k].at[0], out[k].at[c], loc.at[k])
            cp.start()
            copies.append(cp)
        for k in range(n):
            mine = out[k].at[1 - c]
            _remote(mine, mine, send_sem.at[k], recv_sem.at[k], (x, y, 1 - c)).wait_recv()
            copies[2 * k].wait_send()
            copies[2 * k + 1].wait()

    return pl.pallas_call(
        body, name="rs_swap_halves",
        out_shape=tuple(jax.ShapeDtypeStruct((2,) + s, F32) for s in shp),
        in_specs=[ANY] * n, out_specs=tuple([ANY] * n),
        scratch_shapes=[pltpu.SemaphoreType.DMA((n,)), pltpu.SemaphoreType.DMA((n,)), pltpu.SemaphoreType.DMA((n,))],
        compiler_params=pltpu.CompilerParams(has_side_effects=True),
    )(*p3)


def _all_reduce_small(v):
    r = v.shape[0]

    def body(v_ref, out_ref, buf, send_sems, recv_sems, local_sem):
        x, y, c = _position()
        me, sibling = (x, y, c), (x, y, 1 - c)
        chips = [(1 - x, y), (x, 1 - y), (1 - x, 1 - y)]

        def rows(px, py, pc):
            return buf.at[pl.ds((4 * px + 2 * py + pc) * r, r), :]

        def copy(k, block, to, src=None):
            return _remote(rows(*block) if src is None else src, rows(*block), send_sems.at[k], recv_sems.at[k], to)

        mine = pltpu.make_async_copy(v_ref, rows(*me), local_sem)
        mine.start()
        first = [copy(0, me, sibling, src=v_ref)]
        first += [copy(1 + j, me, (*chip, c), src=v_ref) for j, chip in enumerate(chips)]
        for cp in first:
            cp.start()
        passed = [copy(4 + j, (*chip, c), sibling) for j, chip in enumerate(chips)]
        for j, chip in enumerate(chips):
            copy(1 + j, (*chip, c), me).wait_recv()
            passed[j].start()
        copy(0, sibling, me).wait_recv()
        for j, chip in enumerate(chips):
            copy(4 + j, (*chip, 1 - c), me).wait_recv()
        for cp in first + passed:
            cp.wait_send()
        mine.wait()
        acc = buf[0:r, :]
        for dev in range(1, N_DEV):
            acc = acc + buf[dev * r:(dev + 1) * r, :]
        out_ref[...] = acc

    return pl.pallas_call(
        body, name="all_reduce_small",
        out_shape=jax.ShapeDtypeStruct(v.shape, F32),
        in_specs=[pl.BlockSpec(memory_space=pltpu.VMEM)],
        out_specs=pl.BlockSpec(memory_space=pltpu.VMEM),
        scratch_shapes=[pltpu.VMEM((N_DEV * r, LANES), F32), pltpu.SemaphoreType.DMA((7,)),
                        pltpu.SemaphoreType.DMA((7,)), pltpu.SemaphoreType.DMA],
        compiler_params=pltpu.CompilerParams(has_side_effects=True, vmem_limit_bytes=VMEM_LIMIT_V7X),
    )(v)


def _adamw(name, w, g, m, v, tr):
    r, c = w.shape
    tr = min(tr, r)
    assert r % tr == 0

    def body(w_ref, g_ref, m_ref, v_ref, d_ref, mo_ref, vo_ref):
        gg = g_ref[...]
        mn = ADAM_B1 * m_ref[...] + (1.0 - ADAM_B1) * gg
        vn = ADAM_B2 * v_ref[...] + (1.0 - ADAM_B2) * (gg * gg)
        m_hat = mn / (1.0 - ADAM_B1 ** ADAM_STEP)
        v_hat = vn / (1.0 - ADAM_B2 ** ADAM_STEP)
        d_ref[...] = -ADAM_LR * (m_hat / (jnp.sqrt(v_hat) + ADAM_EPS) + ADAM_WD * w_ref[...])
        mo_ref[...] = mn
        vo_ref[...] = vn

    blk = pl.BlockSpec((tr, c), lambda i: (i, 0))
    return pl.pallas_call(
        body, name=name, out_shape=(jax.ShapeDtypeStruct((r, c), F32),) * 3, grid=(r // tr,),
        in_specs=[blk] * 4, out_specs=(blk,) * 3, compiler_params=_cparams("parallel"),
    )(w, g, m, v)


def _col_sharded_to_comm(g):
    k, n = g.shape
    return g.reshape(2, k // 2, N_CHIPS, n // N_CHIPS).transpose(0, 2, 1, 3)


def _row_sharded_to_comm(g):
    r, c = g.shape
    return g.reshape(N_CHIPS, 2, r // (2 * N_CHIPS), c).transpose(1, 0, 2, 3)


def _pack_small(g1, bfv, mix, scale, g2n, gf, extra=None):
    row8 = jnp.pad(bfv.reshape(1, N_HEADS), ((0, 0), (0, LANES - N_HEADS)))
    if extra is not None:
        row8 = row8 + jnp.pad(extra[:, :1], ((0, 0), (N_HEADS, LANES - N_HEADS - 1)))
    return jnp.concatenate([
        g1.reshape(8, LANES), jnp.pad(row8, ((0, 7), (0, 0))), mix.reshape(512, LANES),
        jnp.pad(scale.reshape(4, LANES), ((0, 4), (0, 0))), g2n.reshape(8, LANES), gf.reshape(8, LANES)], axis=0)


def _unpack_small(s, like):
    g1, bfv, mix, scale, g2n, gf = like
    return (s[0:8].reshape(g1.shape), s[8, :N_HEADS].reshape(bfv.shape), s[16:528].reshape(mix.shape),
            s[528:532].reshape(scale.shape), s[536:544].reshape(g2n.shape), s[544:552].reshape(gf.shape))


def kernel(x, norm1_g, w_in, b_forget, pool_mix, pool_scale, w_pool_out, w_attn_out, w_out, norm2_g, w_ffn_gate, w_ffn_up, w_ffn_down, norm_f_g, loss_target, m_norm1_g, m_w_in, m_b_forget, m_pool_mix, m_pool_scale, m_w_pool_out, m_w_attn_out, m_w_out, m_norm2_g, m_w_ffn_gate, m_w_ffn_up, m_w_ffn_down, m_norm_f_g, v_norm1_g, v_w_in, v_b_forget, v_pool_mix, v_pool_scale, v_w_pool_out, v_w_attn_out, v_w_out, v_norm2_g, v_w_ffn_gate, v_w_ffn_up, v_w_ffn_down, v_norm_f_g):
    nb, seq, d = x.shape
    big_w = (w_in, w_pool_out, w_attn_out, w_out, w_ffn_gate, w_ffn_up, w_ffn_down)
    big_m = (m_w_in, m_w_pool_out, m_w_attn_out, m_w_out, m_w_ffn_gate, m_w_ffn_up, m_w_ffn_down)
    big_v = (v_w_in, v_w_pool_out, v_w_attn_out, v_w_out, v_w_ffn_gate, v_w_ffn_up, v_w_ffn_down)
    row_sharded = (False, False, False, True, False, False, True)
    small_w = (norm1_g, b_forget, pool_mix, pool_scale, norm2_g, norm_f_g)
    small_m = (m_norm1_g, m_b_forget, m_pool_mix, m_pool_scale, m_norm2_g, m_norm_f_g)
    small_v = (v_norm1_g, v_b_forget, v_pool_mix, v_pool_scale, v_norm2_g, v_norm_f_g)

    gathered = _gather_weights([w[0].astype(BF16).reshape(2, w.shape[1] // 2, w.shape[2]) for w in big_w])
    full = []
    for gw, w, rs in zip(gathered, big_w, row_sharded):
        r, c = w.shape[1:]
        gw = gw.reshape(N_CHIPS, r, c)
        full.append(gw.reshape(N_CHIPS * r, c) if rs else gw.transpose(1, 0, 2).reshape(r, N_CHIPS * c))

    loss, dx, big_g, small_g = _local_step(
        x.reshape(nb * seq, d), loss_target.reshape(nb * seq, d), seq,
        norm1_g, full[0], b_forget, pool_mix[0], pool_scale, full[1], full[2], full[3], norm2_g,
        full[4], full[5], full[6], norm_f_g.reshape(1, d))

    comm = [_row_sharded_to_comm(g) if rs else _col_sharded_to_comm(g) for g, rs in zip(big_g, row_sharded)]
    reduced = _reduce_scatter(comm)
    small_sum = _all_reduce_small(_pack_small(*small_g, extra=loss))
    loss_out = small_sum[8, N_HEADS]

    grads, deltas, new_m, new_v = [None] * 13, [None] * 13, [None] * 13, [None] * 13
    big_pos = (1, 5, 6, 7, 9, 10, 11)
    small_pos = (0, 2, 3, 4, 8, 12)
    for k, pos in enumerate(big_pos):
        w = big_w[k]
        r, c = w.shape[1:]
        g = reduced[k].reshape(r, c)
        dl, mn, vn = _adamw(f"adamw_big{k}", w[0], g, big_m[k][0], big_v[k][0], 256 if r % 256 == 0 else r // 2)
        grads[pos], deltas[pos], new_m[pos], new_v[pos] = (a.reshape(w.shape) for a in (g, dl, mn, vn))
    dl, mn, vn = _adamw("adamw_small", _pack_small(*small_w), small_sum * _small_mask(), _pack_small(*small_m),
                        _pack_small(*small_v), 552)
    for pos, g, a, b, e in zip(small_pos, _unpack_small(small_sum, small_w), _unpack_small(dl, small_w),
                               _unpack_small(mn, small_w), _unpack_small(vn, small_w)):
        grads[pos], deltas[pos], new_m[pos], new_v[pos] = g, a, b, e

    return (loss_out, dx.reshape(nb, seq, d), *grads, *deltas, *new_m, *new_v)


def _small_mask():
    rows = lax.broadcasted_iota(jnp.int32, (552, LANES), 0)
    lanes = lax.broadcasted_iota(jnp.int32, (552, LANES), 1)
    return jnp.where(jnp.logical_and(rows == 8, lanes == N_HEADS), 0.0, 1.0).astype(F32)
```

```python
import functools

import jax
import jax.numpy as jnp
from jax import lax
from jax.experimental import pallas as pl
from jax.experimental.pallas import tpu as pltpu

F32 = jnp.float32
BF16 = jnp.bfloat16

D_MODEL = 1024
POOL_WINDOWS = (2, 4, 8, 16)
POOL_GROUPS = 4
POOL_GROUP_DIM = 128
POOL_WIDTH = 512
HEAD_DIM = 64
N_HEADS = 8
ATTN_WIDTH = 512
D_FF = 2816
RMS_EPS = 1e-6
ATTN_SCALE = HEAD_DIM ** -0.5
NEG_BIG = -1e30

ADAM_LR = 0.001
ADAM_B1 = 0.9
ADAM_B2 = 0.999
ADAM_EPS = 1e-08
ADAM_WD = 0.01
ADAM_STEP = 10

LANES = 128
N_CHIPS = 4
N_DEV = 8
VMEM_LIMIT_V7X = 52 * 1024 * 1024
MESH = pl.DeviceIdType.MESH
ANY = pl.BlockSpec(memory_space=pl.ANY)


def _cparams(*sem):
    return pltpu.CompilerParams(dimension_semantics=sem if sem else None, vmem_limit_bytes=VMEM_LIMIT_V7X)


def _dot(a, b):
    return lax.dot_general(a, b, (((1,), (0,)), ((), ())), preferred_element_type=F32)


def _dot_nt(a, b):
    return lax.dot_general(a, b, (((1,), (1,)), ((), ())), preferred_element_type=F32)


def _dot_tn(a, b):
    return lax.dot_general(a, b, (((0,), (0,)), ((), ())), preferred_element_type=F32)


def _sigmoid(x):
    return jax.nn.sigmoid(x)


def _rms_fwd(x, g):
    r = lax.rsqrt(jnp.mean(x * x, axis=-1, keepdims=True) + RMS_EPS)
    return (x * r) * g


def _rms_bwd(x, g, dy):
    r = lax.rsqrt(jnp.mean(x * x, axis=-1, keepdims=True) + RMS_EPS)
    xh = x * r
    dg = jnp.sum(dy * xh, axis=0, keepdims=True)
    dxh = dy * g
    dx = r * (dxh - xh * jnp.mean(dxh * xh, axis=-1, keepdims=True))
    return dx, dg


def _matmul(name, a, b, mode, out_dtype, tm, tn, tk):
    if mode == "nn":
        (m, k), (_, n) = a.shape, b.shape
    elif mode == "nt":
        (m, k), (n, _) = a.shape, b.shape
    else:
        (k, m), (_, n) = a.shape, b.shape
    tm, tn, tk = min(tm, m), min(tn, n), min(tk, k)
    assert m % tm == 0 and n % tn == 0 and k % tk == 0, (name, m, n, k, tm, tn, tk)
    nk = k // tk
    if mode == "tn":
        a_spec = pl.BlockSpec((tk, tm), lambda i, j, kk: (kk, i))
    else:
        a_spec = pl.BlockSpec((tm, tk), lambda i, j, kk: (i, kk))
    if mode == "nt":
        b_spec = pl.BlockSpec((tn, tk), lambda i, j, kk: (j, kk))
    else:
        b_spec = pl.BlockSpec((tk, tn), lambda i, j, kk: (kk, j))
    dot = {"nn": _dot, "nt": _dot_nt, "tn": _dot_tn}[mode]
    use_scratch = nk > 1 and out_dtype != F32

    def body(a_ref, b_ref, o_ref, *scratch):
        prod = dot(a_ref[...].astype(BF16), b_ref[...].astype(BF16))
        if nk == 1:
            o_ref[...] = prod.astype(out_dtype)
            return
        acc = scratch[0] if use_scratch else o_ref
        kk = pl.program_id(2)

        @pl.when(kk == 0)
        def _():
            acc[...] = prod

        @pl.when(kk > 0)
        def _():
            acc[...] += prod

        if use_scratch:
            @pl.when(kk == nk - 1)
            def _():
                o_ref[...] = acc[...].astype(out_dtype)

    return pl.pallas_call(
        body,
        name=name,
        out_shape=jax.ShapeDtypeStruct((m, n), out_dtype),
        grid=(m // tm, n // tn, nk),
        in_specs=[a_spec, b_spec],
        out_specs=pl.BlockSpec((tm, tn), lambda i, j, kk: (i, j)),
        scratch_shapes=[pltpu.VMEM((tm, tn), F32)] if use_scratch else [],
        compiler_params=_cparams("parallel", "parallel", "arbitrary"),
    )(a, b)


def _norm_fwd(name, x, g, tm):
    t, d = x.shape
    tm = min(tm, t)

    def body(x_ref, g_ref, h_ref):
        h_ref[...] = _rms_fwd(x_ref[...], g_ref[...]).astype(BF16)

    return pl.pallas_call(
        body, name=name, out_shape=jax.ShapeDtypeStruct((t, d), BF16), grid=(t // tm,),
        in_specs=[pl.BlockSpec((tm, d), lambda i: (i, 0)), pl.BlockSpec((1, d), lambda i: (0, 0))],
        out_specs=pl.BlockSpec((tm, d), lambda i: (i, 0)),
        compiler_params=_cparams("parallel"),
    )(x, g)


def _split3(x):
    hi = x.astype(BF16)
    r1 = x - hi.astype(F32)
    mid = r1.astype(BF16)
    lo = (r1 - mid.astype(F32)).astype(BF16)
    return hi, mid, lo


def _tri_dot(tri, x):
    hi, mid, lo = _split3(x)
    return _dot(tri, hi) + _dot(tri, mid) + _dot(tri, lo)


def _forget_fwd(h, wf, bf, seq):
    t, d = h.shape
    cb = min(256, seq)

    def body(h_ref, wf_ref, bf_ref, fl_ref, fc_ref):
        fl = _dot(h_ref[...], wf_ref[...])
        fl_ref[...] = fl
        xx = fl + bf_ref[...]
        lf = jnp.minimum(xx, 0.0) - jnp.log(1.0 + jnp.exp(-jnp.abs(xx)))
        ri = lax.broadcasted_iota(jnp.int32, (cb, cb), 0)
        ci = lax.broadcasted_iota(jnp.int32, (cb, cb), 1)
        tri = (ri >= ci).astype(BF16)
        carry = jnp.zeros((1, LANES), F32)
        for blk in range(seq // cb):
            cs = _tri_dot(tri, lf[blk * cb:(blk + 1) * cb]) + carry
            fc_ref[blk * cb:(blk + 1) * cb, :] = cs
            carry = cs[cb - 1:cb, :]

    return pl.pallas_call(
        body, name="forget_fwd",
        out_shape=(jax.ShapeDtypeStruct((t, LANES), F32), jax.ShapeDtypeStruct((t, LANES), F32)),
        grid=(t // seq,),
        in_specs=[pl.BlockSpec((seq, d), lambda b: (b, 0)), pl.BlockSpec((d, LANES), lambda b: (0, 0)),
                  pl.BlockSpec((1, LANES), lambda b: (0, 0))],
        out_specs=(pl.BlockSpec((seq, LANES), lambda b: (b, 0)), pl.BlockSpec((seq, LANES), lambda b: (b, 0))),
        compiler_params=_cparams("parallel"),
    )(h, wf, bf)


def _pool_fwd(u, mix, scale, seq):
    t = u.shape[0]

    def body(u_ref, mix_ref, sc_ref, p_ref, ps_ref):
        tpos = lax.broadcasted_iota(jnp.int32, (seq, POOL_GROUP_DIM), 0)
        for g in range(POOL_GROUPS):
            sl = slice(g * POOL_GROUP_DIM, (g + 1) * POOL_GROUP_DIM)
            ug = u_ref[:, sl]
            s = ug
            for lvl in range(g + 1):
                d = 2 ** lvl
                s = s + jnp.where(tpos >= d, pltpu.roll(s, d, 0), 0.0)
            cnt = jnp.minimum(tpos + 1, POOL_WINDOWS[g]).astype(F32)
            pb = (s / cnt - ug).astype(BF16)
            p_ref[:, sl] = pb
            ps_ref[:, sl] = (_dot(pb, mix_ref[g]) * sc_ref[:, sl]).astype(BF16)

    return pl.pallas_call(
        body, name="pool_fwd",
        out_shape=(jax.ShapeDtypeStruct((t, POOL_WIDTH), BF16), jax.ShapeDtypeStruct((t, POOL_WIDTH), BF16)),
        grid=(t // seq,),
        in_specs=[pl.BlockSpec((seq, POOL_WIDTH), lambda b: (b, 0)),
                  pl.BlockSpec((POOL_GROUPS, POOL_GROUP_DIM, POOL_GROUP_DIM), lambda b: (0, 0, 0)),
                  pl.BlockSpec((1, POOL_WIDTH), lambda b: (0, 0))],
        out_specs=(pl.BlockSpec((seq, POOL_WIDTH), lambda b: (b, 0)), pl.BlockSpec((seq, POOL_WIDTH), lambda b: (b, 0))),
        compiler_params=_cparams("parallel"),
    )(u, mix, scale)


def _attn_fwd(qkv, fcol, frow, seq, tq):
    t = qkv.shape[0]
    nq = seq // tq

    def body(q_ref, k_ref, v_ref, fc_ref, fr_ref, o_ref, lse_ref):
        i = pl.program_id(1)
        rowg = i * tq + lax.broadcasted_iota(jnp.int32, (tq, tq), 0)
        coli = lax.broadcasted_iota(jnp.int32, (tq, tq), 1)
        for hd in range(N_HEADS):
            sl = slice(hd * HEAD_DIM, (hd + 1) * HEAD_DIM)
            qh = q_ref[:, sl]
            fc = fc_ref[:, hd:hd + 1]

            def step(j, carry, sl=sl, qh=qh, fc=fc, hd=hd):
                m, l, acc = carry
                r0 = pl.multiple_of(j * tq, tq)
                kh = k_ref[pl.ds(r0, tq), sl]
                vh = v_ref[pl.ds(r0, tq), sl]
                fr = fr_ref[hd, pl.ds(j, 1), :]
                s = _dot_nt(qh, kh) * ATTN_SCALE + (fc - fr)
                s = jnp.where(rowg >= j * tq + coli, s, NEG_BIG)
                m_new = jnp.maximum(m, jnp.max(s, axis=1, keepdims=True))
                alpha = jnp.exp(m - m_new)
                p = jnp.exp(s - m_new)
                l = alpha * l + jnp.sum(p, axis=1, keepdims=True)
                acc = alpha * acc + _dot(p.astype(BF16), vh)
                return m_new, l, acc

            init = (jnp.full((tq, 1), NEG_BIG, F32), jnp.zeros((tq, 1), F32), jnp.zeros((tq, HEAD_DIM), F32))
            m, l, acc = lax.fori_loop(0, i + 1, step, init)
            o_ref[:, sl] = acc / l
            lse_ref[:, hd:hd + 1] = m + jnp.log(l)

    return pl.pallas_call(
        body, name="attn_fwd",
        out_shape=(jax.ShapeDtypeStruct((t, ATTN_WIDTH), F32), jax.ShapeDtypeStruct((t, N_HEADS), F32)),
        grid=(t // seq, nq),
        in_specs=[pl.BlockSpec((tq, ATTN_WIDTH), lambda b, i: (b * nq + i, 0)),
                  pl.BlockSpec((seq, ATTN_WIDTH), lambda b, i: (b, 1)),
                  pl.BlockSpec((seq, ATTN_WIDTH), lambda b, i: (b, 2)),
                  pl.BlockSpec((tq, N_HEADS), lambda b, i: (b * nq + i, 0)),
                  pl.BlockSpec((N_HEADS, nq, tq), lambda b, i: (0, b, 0))],
        out_specs=(pl.BlockSpec((tq, ATTN_WIDTH), lambda b, i: (b * nq + i, 0)),
                   pl.BlockSpec((tq, N_HEADS), lambda b, i: (b * nq + i, 0))),
        compiler_params=_cparams("parallel", "arbitrary"),
    )(qkv, qkv, qkv, fcol, frow)


def _merge_fwd(x, ps, o, g2, wpo, wao, wout, tm):
    t, d = x.shape
    tm = min(tm, t)

    def body(x_ref, ps_ref, o_ref, gp_ref, ga_ref, wpo_ref, wao_ref, wout_ref, mg_ref, x1_ref):
        py = _dot(ps_ref[...], wpo_ref[...])
        ay = _dot(o_ref[...].astype(BF16), wao_ref[...])
        mb = (_sigmoid(gp_ref[...]) * py + _sigmoid(ga_ref[...]) * ay).astype(BF16)
        mg_ref[...] = mb
        x1_ref[...] = x_ref[...] + _dot(mb, wout_ref[...])

    row = lambda w: pl.BlockSpec((tm, w), lambda i: (i, 0))
    full = lambda a: pl.BlockSpec(a.shape, lambda i: (0, 0))
    return pl.pallas_call(
        body, name="merge_fwd",
        out_shape=(jax.ShapeDtypeStruct((t, d), BF16), jax.ShapeDtypeStruct((t, d), F32)),
        grid=(t // tm,),
        in_specs=[row(d), row(POOL_WIDTH), row(ATTN_WIDTH), pl.BlockSpec((tm, d), lambda i: (i, 0)),
                  pl.BlockSpec((tm, d), lambda i: (i, 1)), full(wpo), full(wao), full(wout)],
        out_specs=(row(d), row(d)),
        compiler_params=_cparams("parallel"),
    )(x, ps, o, g2, g2, wpo, wao, wout)


def _ffn_fwd(x1, g, wg, wu, wd, tm, tf):
    t, d = x1.shape
    f = wg.shape[1]
    tm = min(tm, t)
    nf = f // tf

    def body(x1_ref, g_ref, wg_ref, wu_ref, wd_ref, h2_ref, gt_ref, up_ref, act_ref, x2_ref):
        j = pl.program_id(1)

        @pl.when(j == 0)
        def _():
            h2_ref[...] = _rms_fwd(x1_ref[...], g_ref[...]).astype(BF16)

        h2 = h2_ref[...]
        gt = _dot(h2, wg_ref[...])
        up = _dot(h2, wu_ref[...])
        act = (gt * _sigmoid(gt) * up).astype(BF16)
        gt_ref[...] = gt
        up_ref[...] = up
        act_ref[...] = act
        prod = _dot(act, wd_ref[...])

        @pl.when(j == 0)
        def _():
            x2_ref[...] = prod

        @pl.when(j > 0)
        def _():
            x2_ref[...] += prod

        @pl.when(j == nf - 1)
        def _():
            x2_ref[...] += x1_ref[...]

    return pl.pallas_call(
        body, name="ffn_fwd",
        out_shape=(jax.ShapeDtypeStruct((t, d), BF16), jax.ShapeDtypeStruct((t, f), F32),
                   jax.ShapeDtypeStruct((t, f), F32), jax.ShapeDtypeStruct((t, f), BF16),
                   jax.ShapeDtypeStruct((t, d), F32)),
        grid=(t // tm, nf),
        in_specs=[pl.BlockSpec((tm, d), lambda i, j: (i, 0)), pl.BlockSpec((1, d), lambda i, j: (0, 0)),
                  pl.BlockSpec((d, tf), lambda i, j: (0, j)), pl.BlockSpec((d, tf), lambda i, j: (0, j)),
                  pl.BlockSpec((tf, d), lambda i, j: (j, 0))],
        out_specs=(pl.BlockSpec((tm, d), lambda i, j: (i, 0)), pl.BlockSpec((tm, tf), lambda i, j: (i, j)),
                   pl.BlockSpec((tm, tf), lambda i, j: (i, j)), pl.BlockSpec((tm, tf), lambda i, j: (i, j)),
                   pl.BlockSpec((tm, d), lambda i, j: (i, 0))),
        compiler_params=_cparams("parallel", "arbitrary"),
    )(x1, g, wg, wu, wd)


def _final_fwd_bwd(x2, target, g, tm):
    t, d = x2.shape
    tm = min(tm, t)

    def body(x_ref, t_ref, g_ref, loss_ref, dx_ref, dg_ref):
        i = pl.program_id(0)
        x = x_ref[...]
        gg = g_ref[...]
        err = _rms_fwd(x, gg) - t_ref[...]
        part = 0.5 * jnp.sum(jnp.mean(err * err, axis=-1, keepdims=True), axis=0, keepdims=True)
        dx, dg = _rms_bwd(x, gg, err * (1.0 / d))
        dx_ref[...] = dx

        @pl.when(i == 0)
        def _():
            loss_ref[...] = jnp.zeros_like(loss_ref)
            dg_ref[...] = jnp.zeros_like(dg_ref)

        loss_ref[...] += jnp.broadcast_to(part, loss_ref.shape)
        dg_ref[...] += dg

    return pl.pallas_call(
        body, name="final_fwd_bwd",
        out_shape=(jax.ShapeDtypeStruct((1, LANES), F32), jax.ShapeDtypeStruct((t, d), F32),
                   jax.ShapeDtypeStruct((1, d), F32)),
        grid=(t // tm,),
        in_specs=[pl.BlockSpec((tm, d), lambda i: (i, 0)), pl.BlockSpec((tm, d), lambda i: (i, 0)),
                  pl.BlockSpec((1, d), lambda i: (0, 0))],
        out_specs=(pl.BlockSpec((1, LANES), lambda i: (0, 0)), pl.BlockSpec((tm, d), lambda i: (i, 0)),
                   pl.BlockSpec((1, d), lambda i: (0, 0))),
        compiler_params=_cparams("arbitrary"),
    )(x2, target, g)


def _ffn_bwd(dx2, x1, g, gt, up, wg, wu, wd, tm, tf):
    t, d = dx2.shape
    f = gt.shape[1]
    tm = min(tm, t)
    nf = f // tf

    def body(dx2_ref, x1_ref, g_ref, gt_ref, up_ref, wg_ref, wu_ref, wd_ref, dgt_ref, dup_ref, dx1_ref, dg_ref, acc_ref):
        i, j = pl.program_id(0), pl.program_id(1)
        dact = _dot_nt(dx2_ref[...].astype(BF16), wd_ref[...])
        gtv = gt_ref[...]
        sg = _sigmoid(gtv)
        dup = (dact * (gtv * sg)).astype(BF16)
        dgt = (dact * up_ref[...] * (sg * (1.0 + gtv * (1.0 - sg)))).astype(BF16)
        dgt_ref[...] = dgt
        dup_ref[...] = dup
        contrib = _dot_nt(dgt, wg_ref[...]) + _dot_nt(dup, wu_ref[...])

        @pl.when(j == 0)
        def _():
            acc_ref[...] = contrib

        @pl.when(j > 0)
        def _():
            acc_ref[...] += contrib

        @pl.when(jnp.logical_and(i == 0, j == 0))
        def _():
            dg_ref[...] = jnp.zeros_like(dg_ref)

        @pl.when(j == nf - 1)
        def _():
            dxn, dg = _rms_bwd(x1_ref[...], g_ref[...], acc_ref[...])
            dx1_ref[...] = dx2_ref[...] + dxn
            dg_ref[...] += dg

    return pl.pallas_call(
        body, name="ffn_bwd",
        out_shape=(jax.ShapeDtypeStruct((t, f), BF16), jax.ShapeDtypeStruct((t, f), BF16),
                   jax.ShapeDtypeStruct((t, d), F32), jax.ShapeDtypeStruct((1, d), F32)),
        grid=(t // tm, nf),
        in_specs=[pl.BlockSpec((tm, d), lambda i, j: (i, 0)), pl.BlockSpec((tm, d), lambda i, j: (i, 0)),
                  pl.BlockSpec((1, d), lambda i, j: (0, 0)),
                  pl.BlockSpec((tm, tf), lambda i, j: (i, j)), pl.BlockSpec((tm, tf), lambda i, j: (i, j)),
                  pl.BlockSpec((d, tf), lambda i, j: (0, j)), pl.BlockSpec((d, tf), lambda i, j: (0, j)),
                  pl.BlockSpec((tf, d), lambda i, j: (j, 0))],
        out_specs=(pl.BlockSpec((tm, tf), lambda i, j: (i, j)), pl.BlockSpec((tm, tf), lambda i, j: (i, j)),
                   pl.BlockSpec((tm, d), lambda i, j: (i, 0)), pl.BlockSpec((1, d), lambda i, j: (0, 0))),
        scratch_shapes=[pltpu.VMEM((tm, d), F32)],
        compiler_params=_cparams("arbitrary", "arbitrary"),
    )(dx2, x1, g, gt, up, wg, wu, wd)


def _merge_bwd(dx1, ps, o, g2, wpo, wao, wout, tm):
    t, d = dx1.shape
    tm = min(tm, t)

    def body(dx1_ref, ps_ref, o_ref, gp_ref, ga_ref, wpo_ref, wao_ref, wout_ref, dpy_ref, day_ref, dg2_ref, dps_ref, da_ref):
        dm = _dot_nt(dx1_ref[...].astype(BF16), wout_ref[...])
        py = _dot(ps_ref[...], wpo_ref[...])
        ay = _dot(o_ref[...].astype(BF16), wao_ref[...])
        sp = _sigmoid(gp_ref[...])
        sa = _sigmoid(ga_ref[...])
        dpy = (dm * sp).astype(BF16)
        day = (dm * sa).astype(BF16)
        dpy_ref[...] = dpy
        day_ref[...] = day
        dg2_ref[:, :d] = (dm * py * (sp * (1.0 - sp))).astype(BF16)
        dg2_ref[:, d:] = (dm * ay * (sa * (1.0 - sa))).astype(BF16)
        dps_ref[...] = _dot_nt(dpy, wpo_ref[...])
        da_ref[...] = _dot_nt(day, wao_ref[...]).astype(BF16)

    row = lambda w: pl.BlockSpec((tm, w), lambda i: (i, 0))
    full = lambda a: pl.BlockSpec(a.shape, lambda i: (0, 0))
    return pl.pallas_call(
        body, name="merge_bwd",
        out_shape=(jax.ShapeDtypeStruct((t, d), BF16), jax.ShapeDtypeStruct((t, d), BF16),
                   jax.ShapeDtypeStruct((t, 2 * d), BF16), jax.ShapeDtypeStruct((t, POOL_WIDTH), F32),
                   jax.ShapeDtypeStruct((t, ATTN_WIDTH), BF16)),
        grid=(t // tm,),
        in_specs=[row(d), row(POOL_WIDTH), row(ATTN_WIDTH), pl.BlockSpec((tm, d), lambda i: (i, 0)),
                  pl.BlockSpec((tm, d), lambda i: (i, 1)), full(wpo), full(wao), full(wout)],
        out_specs=(row(d), row(d), row(2 * d), row(POOL_WIDTH), row(ATTN_WIDTH)),
        compiler_params=_cparams("parallel"),
    )(dx1, ps, o, g2, g2, wpo, wao, wout)


def _attn_bwd(qkv, do, lse4, fcol4, frow, seq, tq):
    t = qkv.shape[0]
    nq = seq // tq
    hp_n = N_HEADS // 2

    def body(q_ref, k_ref, v_ref, do_ref, lse_ref, fc_ref, fr_ref, dq_ref, dk_ref, dv_ref, dfr_ref,
             dk_acc, dv_acc, p_buf, dp_buf):
        rowi = lax.broadcasted_iota(jnp.int32, (tq, tq), 0)
        coli = lax.broadcasted_iota(jnp.int32, (tq, tq), 1)
        dfr_ref[...] = jnp.zeros_like(dfr_ref)
        for e in range(2):
            sl = slice(e * HEAD_DIM, (e + 1) * HEAD_DIM)
            dk_acc[...] = jnp.zeros_like(dk_acc)
            dv_acc[...] = jnp.zeros_like(dv_acc)

            def q_step(i, _, sl=sl, e=e):
                q0 = pl.multiple_of(i * tq, tq)
                qh = q_ref[pl.ds(q0, tq), sl]
                doh = do_ref[pl.ds(q0, tq), sl]
                ls = lse_ref[pl.ds(q0, tq), e:e + 1]
                fc = fc_ref[pl.ds(q0, tq), e:e + 1]

                def sweep1(j, dl):
                    r0 = pl.multiple_of(j * tq, tq)
                    kh = k_ref[pl.ds(r0, tq), sl]
                    vh = v_ref[pl.ds(r0, tq), sl]
                    fr = fr_ref[e, pl.ds(j, 1), :]
                    s = _dot_nt(qh, kh) * ATTN_SCALE + (fc - fr)
                    s = jnp.where(q0 + rowi >= r0 + coli, s, NEG_BIG)
                    p = jnp.exp(s - ls)
                    dp = _dot_nt(doh, vh)
                    p_buf[j] = p
                    dp_buf[j] = dp
                    dv_acc[pl.ds(r0, tq), :] += _dot_tn(p.astype(BF16), doh)
                    return dl + jnp.sum(p * dp, axis=1, keepdims=True)

                dl = lax.fori_loop(0, i + 1, sweep1, jnp.zeros((tq, 1), F32))

                def sweep2(j, dq):
                    r0 = pl.multiple_of(j * tq, tq)
                    kh = k_ref[pl.ds(r0, tq), sl]
                    ds = p_buf[j] * (dp_buf[j] - dl)
                    dfr_ref[e, pl.ds(j, 1), :] += jnp.sum(ds, axis=0, keepdims=True)
                    dsb = ds.astype(BF16)
                    dk_acc[pl.ds(r0, tq), :] += _dot_tn(dsb, qh)
                    return dq + _dot(dsb, kh)

                dq = lax.fori_loop(0, i + 1, sweep2, jnp.zeros((tq, HEAD_DIM), F32))
                dq_ref[pl.ds(q0, tq), sl] = (dq * ATTN_SCALE).astype(BF16)
                return 0

            lax.fori_loop(0, nq, q_step, 0)
            dk_ref[:, sl] = (dk_acc[...] * ATTN_SCALE).astype(BF16)
            dv_ref[:, sl] = dv_acc[...].astype(BF16)

    col = lambda off: pl.BlockSpec((seq, LANES), lambda b, hp: (b, off + hp))
    pair = pl.BlockSpec((None, seq, 2), lambda b, hp: (hp, b, 0))
    rows = pl.BlockSpec((2, nq, tq), lambda b, hp: (hp, b, 0))
    return pl.pallas_call(
        body, name="attn_bwd",
        out_shape=(jax.ShapeDtypeStruct((t, ATTN_WIDTH), BF16),) * 3 + (jax.ShapeDtypeStruct((N_HEADS, t // tq, tq), F32),),
        grid=(t // seq, hp_n),
        in_specs=[col(0), col(hp_n), col(2 * hp_n), col(0), pair, pair, rows],
        out_specs=(col(0), col(0), col(0), rows),
        scratch_shapes=[pltpu.VMEM((seq, HEAD_DIM), F32), pltpu.VMEM((seq, HEAD_DIM), F32),
                        pltpu.VMEM((nq, tq, tq), F32), pltpu.VMEM((nq, tq, tq), F32)],
        compiler_params=_cparams("parallel", "arbitrary"),
    )(qkv, qkv, qkv, do, lse4, fcol4, frow)


def _forget_bwd(dfc, fl, bf, seq):
    t = fl.shape[0]
    cb = min(256, seq)
    nb = seq // cb

    def body(dfc_ref, fl_ref, bf_ref, dfl_ref, db_ref):
        b = pl.program_id(0)
        ri = lax.broadcasted_iota(jnp.int32, (cb, cb), 0)
        ci = lax.broadcasted_iota(jnp.int32, (cb, cb), 1)
        tri = (ci >= ri).astype(BF16)
        carry = jnp.zeros((1, LANES), F32)
        dbs = jnp.zeros((1, LANES), F32)
        for blk in reversed(range(nb)):
            rs = slice(blk * cb, (blk + 1) * cb)
            dlf = _tri_dot(tri, -dfc_ref[rs, :]) + carry
            carry = dlf[0:1, :]
            dfl = dlf * _sigmoid(-(fl_ref[rs, :] + bf_ref[...]))
            dfl_ref[rs, :] = dfl.astype(BF16)
            dbs = dbs + jnp.sum(dfl, axis=0, keepdims=True)

        @pl.when(b == 0)
        def _():
            db_ref[...] = jnp.zeros_like(db_ref)

        db_ref[...] += dbs

    return pl.pallas_call(
        body, name="forget_bwd",
        out_shape=(jax.ShapeDtypeStruct((t, LANES), BF16), jax.ShapeDtypeStruct((1, LANES), F32)),
        grid=(t // seq,),
        in_specs=[pl.BlockSpec((seq, LANES), lambda b: (b, 0)), pl.BlockSpec((seq, LANES), lambda b: (b, 0)),
                  pl.BlockSpec((1, LANES), lambda b: (0, 0))],
        out_specs=(pl.BlockSpec((seq, LANES), lambda b: (b, 0)), pl.BlockSpec((1, LANES), lambda b: (0, 0))),
        compiler_params=_cparams("arbitrary"),
    )(dfc, fl, bf)


def _pool_bwd(dps, p, mix, scale, seq):
    t = dps.shape[0]

    def body(dps_ref, p_ref, mix_ref, sc_ref, du_ref, dmix_ref, dsc_ref):
        b = pl.program_id(0)

        @pl.when(b == 0)
        def _():
            dmix_ref[...] = jnp.zeros_like(dmix_ref)
            dsc_ref[...] = jnp.zeros_like(dsc_ref)

        tpos = lax.broadcasted_iota(jnp.int32, (seq, POOL_GROUP_DIM), 0)
        for g in range(POOL_GROUPS):
            sl = slice(g * POOL_GROUP_DIM, (g + 1) * POOL_GROUP_DIM)
            pb = p_ref[:, sl]
            dpsg = dps_ref[:, sl]
            pm = _dot(pb, mix_ref[g])
            dsc_ref[:, sl] += jnp.sum(dpsg * pm, axis=0, keepdims=True)
            dpm = (dpsg * sc_ref[:, sl]).astype(BF16)
            dmix_ref[g] += _dot_tn(pb, dpm)
            dp = _dot_nt(dpm, mix_ref[g])
            cnt = jnp.minimum(tpos + 1, POOL_WINDOWS[g]).astype(F32)
            s = dp / cnt
            for lvl in range(g + 1):
                d = 2 ** lvl
                s = s + jnp.where(tpos < seq - d, pltpu.roll(s, seq - d, 0), 0.0)
            du_ref[:, sl] = (s - dp).astype(BF16)

    return pl.pallas_call(
        body, name="pool_bwd",
        out_shape=(jax.ShapeDtypeStruct((t, POOL_WIDTH), BF16),
                   jax.ShapeDtypeStruct((POOL_GROUPS, POOL_GROUP_DIM, POOL_GROUP_DIM), F32),
                   jax.ShapeDtypeStruct((1, POOL_WIDTH), F32)),
        grid=(t // seq,),
        in_specs=[pl.BlockSpec((seq, POOL_WIDTH), lambda b: (b, 0)), pl.BlockSpec((seq, POOL_WIDTH), lambda b: (b, 0)),
                  pl.BlockSpec((POOL_GROUPS, POOL_GROUP_DIM, POOL_GROUP_DIM), lambda b: (0, 0, 0)),
                  pl.BlockSpec((1, POOL_WIDTH), lambda b: (0, 0))],
        out_specs=(pl.BlockSpec((seq, POOL_WIDTH), lambda b: (b, 0)),
                   pl.BlockSpec((POOL_GROUPS, POOL_GROUP_DIM, POOL_GROUP_DIM), lambda b: (0, 0, 0)),
                   pl.BlockSpec((1, POOL_WIDTH), lambda b: (0, 0))),
        compiler_params=_cparams("arbitrary"),
    )(dps, p, mix, scale)


def _in_bwd(du, dq, dk, dv, dg2, dfl, dx1, x, g, wu, wqkv, wg2, wft, tm):
    t, d = x.shape
    tm = min(tm, t)
    aw = ATTN_WIDTH

    def body(du_ref, dq_ref, dk_ref, dv_ref, dg2_ref, dfl_ref, dx1_ref, x_ref, g_ref, wu_ref, wqkv_ref, wg2_ref, wft_ref,
             dx_ref, dg_ref):
        i = pl.program_id(0)
        dh = _dot_nt(du_ref[...], wu_ref[...])
        dh += _dot_nt(dq_ref[...], wqkv_ref[:, 0:aw])
        dh += _dot_nt(dk_ref[...], wqkv_ref[:, aw:2 * aw])
        dh += _dot_nt(dv_ref[...], wqkv_ref[:, 2 * aw:3 * aw])
        dh += _dot_nt(dg2_ref[...], wg2_ref[...])
        dh += _dot(dfl_ref[...], wft_ref[...])
        dxn, dg = _rms_bwd(x_ref[...], g_ref[...], dh)
        dx_ref[...] = dx1_ref[...] + dxn

        @pl.when(i == 0)
        def _():
            dg_ref[...] = jnp.zeros_like(dg_ref)

        dg_ref[...] += dg

    row = lambda w: pl.BlockSpec((tm, w), lambda i: (i, 0))
    full = lambda a: pl.BlockSpec(a.shape, lambda i: (0, 0))
    return pl.pallas_call(
        body, name="in_bwd",
        out_shape=(jax.ShapeDtypeStruct((t, d), F32), jax.ShapeDtypeStruct((1, d), F32)),
        grid=(t // tm,),
        in_specs=[row(POOL_WIDTH), row(aw), row(aw), row(aw), row(2 * d), row(LANES), row(d), row(d),
                  pl.BlockSpec((1, d), lambda i: (0, 0)), full(wu), full(wqkv), full(wg2), full(wft)],
        out_specs=(row(d), pl.BlockSpec((1, d), lambda i: (0, 0))),
        compiler_params=_cparams("arbitrary"),
    )(du, dq, dk, dv, dg2, dfl, dx1, x, g, wu, wqkv, wg2, wft)


def _local_step(x, target, seq, norm1_g, w_in, b_forget, pool_mix, pool_scale, w_pool_out, w_attn_out, w_out,
                norm2_g, w_ffn_gate, w_ffn_up, w_ffn_down, norm_f_g):
    t, d = x.shape
    tq = min(256, seq)
    aw = ATTN_WIDTH
    o_q, o_f, o_g = POOL_WIDTH, POOL_WIDTH + 3 * aw, POOL_WIDTH + 3 * aw + N_HEADS
    wu = w_in[:, :o_q]
    wqkv = w_in[:, o_q:o_f]
    wf = jnp.pad(w_in[:, o_f:o_g], ((0, 0), (0, LANES - N_HEADS)))
    wg2 = w_in[:, o_g:]
    wft = wf.T
    bf = jnp.pad(b_forget, ((0, 0), (0, LANES - N_HEADS)))
    mixb = pool_mix.astype(BF16)

    h = _norm_fwd("norm1_fwd", x, norm1_g, 512)
    u = _matmul("mm_u", h, wu, "nn", F32, 1024, 512, d)
    qkv = _matmul("mm_qkv", h, wqkv, "nn", BF16, 1024, 512, d)
    g2 = _matmul("mm_gates", h, wg2, "nn", F32, 1024, 512, d)
    fl, fcum = _forget_fwd(h, wf, bf, seq)
    fcol = fcum[:, :N_HEADS]
    frow = fcol.T.reshape(N_HEADS, t // tq, tq)
    p, ps = _pool_fwd(u, mixb, pool_scale, seq)
    o, lse = _attn_fwd(qkv, fcol, frow, seq, tq)
    merged, x1 = _merge_fwd(x, ps, o, g2, w_pool_out, w_attn_out, w_out, 256)
    h2, gt, up, act, x2 = _ffn_fwd(x1, norm2_g, w_ffn_gate, w_ffn_up, w_ffn_down, 512, 256)
    loss, dx2, d_gf = _final_fwd_bwd(x2, target, norm_f_g, 512)

    dgt, dup, dx1, d_g2n = _ffn_bwd(dx2, x1, norm2_g, gt, up, w_ffn_gate, w_ffn_up, w_ffn_down, 512, 256)
    d_wd = _matmul("dw_down", act, dx2, "tn", F32, 1408, 1024, 512)
    d_wg = _matmul("dw_gate", h2, dgt, "tn", F32, 1024, 1408, 512)
    d_wu = _matmul("dw_up", h2, dup, "tn", F32, 1024, 1408, 512)
    dpy, day, dg2, dps, da = _merge_bwd(dx1, ps, o, g2, w_pool_out, w_attn_out, w_out, 256)
    d_wout = _matmul("dw_out", merged, dx1, "tn", F32, 1024, 1024, 512)
    d_wpo = _matmul("dw_pool_out", ps, dpy, "tn", F32, 512, 1024, 512)
    d_wao = _matmul("dw_attn_out", o, day, "tn", F32, 512, 1024, 512)
    pairs = lambda a: a.reshape(t, N_HEADS // 2, 2).transpose(1, 0, 2)
    dq, dk, dv, dfr = _attn_bwd(qkv, da, pairs(lse), pairs(fcol), frow, seq, tq)
    dfc = jnp.pad(dfr.reshape(N_HEADS, t).T, ((0, 0), (0, LANES - N_HEADS)))
    dfl, d_bf = _forget_bwd(dfc, fl, bf, seq)
    du, d_mix, d_scale = _pool_bwd(dps, p, mixb, pool_scale, seq)
    dx, d_g1 = _in_bwd(du, dq, dk, dv, dg2, dfl, dx1, x, norm1_g, wu, wqkv, wg2, wft, 256)
    d_wu_in = _matmul("dw_in_u", h, du, "tn", F32, 1024, 512, 512)
    d_wq = _matmul("dw_in_q", h, dq, "tn", F32, 1024, 512, 512)
    d_wk = _matmul("dw_in_k", h, dk, "tn", F32, 1024, 512, 512)
    d_wv = _matmul("dw_in_v", h, dv, "tn", F32, 1024, 512, 512)
    d_wf = _matmul("dw_in_f", h, dfl, "tn", F32, 1024, LANES, 512)
    d_wg2 = _matmul("dw_in_gates", h, dg2, "tn", F32, 1024, 1024, 512)
    d_win = jnp.concatenate([d_wu_in, d_wq, d_wk, d_wv, d_wf[:, :N_HEADS], d_wg2], axis=1)

    big = (d_win, d_wpo, d_wao, d_wout, d_wg, d_wu, d_wd)
    small = (d_g1, d_bf[:, :N_HEADS], d_mix, d_scale, d_g2n, d_gf)
    return loss, dx, big, small


def _position():
    return lax.axis_index("x"), lax.axis_index("y"), lax.axis_index("c")


def _remote(src, dst, send_sem, recv_sem, device):
    return pltpu.make_async_remote_copy(src_ref=src, dst_ref=dst, send_sem=send_sem, recv_sem=recv_sem,
                                        device_id=device, device_id_type=MESH)


def _gather_weights(shards):
    n = len(shards)

    def body(*refs):
        w, g = refs[:n], refs[n:2 * n]
        send, recv = refs[2 * n:]
        x, y, c = _position()
        me = 2 * x + y
        sibling = (x, y, 1 - c)
        chips = [(1 - x, y), (x, 1 - y), (1 - x, 1 - y)]
        sends = []
        for k in range(n):
            for j, (ox, oy) in enumerate(chips):
                cp = _remote(w[k].at[c], g[k].at[me, c], send.at[6 * k + j], recv.at[6 * k + j], (ox, oy, c))
                cp.start()
                sends.append(cp)
        for k in range(n):
            for j, (ox, oy) in enumerate(chips):
                blk = g[k].at[2 * ox + oy, c]
                _remote(blk, blk, send.at[6 * k + j], recv.at[6 * k + j], (ox, oy, c)).wait_recv()
                cp = _remote(blk, blk, send.at[6 * k + 3 + j], recv.at[6 * k + 3 + j], sibling)
                cp.start()
                sends.append(cp)
        for k in range(n):
            for j, (ox, oy) in enumerate(chips):
                blk = g[k].at[2 * ox + oy, 1 - c]
                _remote(blk, blk, send.at[6 * k + 3 + j], recv.at[6 * k + 3 + j], sibling).wait_recv()
        for cp in sends:
            cp.wait_send()

    return pl.pallas_call(
        body, name="gather_weights",
        out_shape=tuple(jax.ShapeDtypeStruct((N_CHIPS,) + s.shape, s.dtype) for s in shards),
        in_specs=[ANY] * n, out_specs=tuple([ANY] * n),
        scratch_shapes=[pltpu.SemaphoreType.DMA((6 * n,)), pltpu.SemaphoreType.DMA((6 * n,))],
        compiler_params=pltpu.CompilerParams(has_side_effects=True),
    )(*shards)


def _swap(name, gives, peer_fn, give_fn, recv_shapes):
    n = len(gives)

    def body(*refs):
        give, recv = refs[:n], refs[n:2 * n]
        send_sem, recv_sem = refs[2 * n:]
        pos = _position()
        copies = [_remote(give_fn(give[k], *pos), recv[k], send_sem.at[k], recv_sem.at[k], peer_fn(*pos))
                  for k in range(n)]
        for cp in copies:
            cp.start()
        for cp in copies:
            cp.wait()

    return pl.pallas_call(
        body, name=name,
        out_shape=tuple(jax.ShapeDtypeStruct(s, g.dtype) for s, g in zip(recv_shapes, gives)),
        in_specs=[ANY] * n, out_specs=tuple([ANY] * n),
        scratch_shapes=[pltpu.SemaphoreType.DMA((n,)), pltpu.SemaphoreType.DMA((n,))],
        compiler_params=pltpu.CompilerParams(has_side_effects=True),
    )(*gives)


def _add_keep_give(name, pos, a, a_keep, a_give, b, b_keep, b_give, steps):
    r, c = a.shape[-2:]

    def spec(arr, fn):
        lead = arr.ndim - 2
        return pl.BlockSpec((None,) * lead + (r, c), lambda i, p: tuple(fn(i, p)) + (0, 0))

    out_spec = pl.BlockSpec((None, r, c), lambda i, p: (i, 0, 0))

    def body(p_ref, ak_ref, bk_ref, ag_ref, bg_ref, keep_ref, give_ref):
        keep_ref[...] = ak_ref[...] + bk_ref[...].astype(F32)
        give_ref[...] = (ag_ref[...] + bg_ref[...].astype(F32)).astype(BF16)

    return pl.pallas_call(
        body, name=name,
        out_shape=(jax.ShapeDtypeStruct((steps, r, c), F32), jax.ShapeDtypeStruct((steps, r, c), BF16)),
        grid_spec=pltpu.PrefetchScalarGridSpec(
            num_scalar_prefetch=1, grid=(steps,),
            in_specs=[spec(a, a_keep), spec(b, b_keep), spec(a, a_give), spec(b, b_give)],
            out_specs=(out_spec, out_spec)),
        compiler_params=_cparams("parallel"),
    )(pos, a, b, a, b)


def _add_last(name, a, b):
    _, r, c = a.shape
    blk = pl.BlockSpec((None, r, c), lambda i: (0, 0, 0))

    def body(a_ref, b_ref, o_ref):
        o_ref[...] = a_ref[...] + b_ref[...].astype(F32)

    return pl.pallas_call(
        body, name=name, out_shape=jax.ShapeDtypeStruct((r, c), F32), grid=(1,), in_specs=[blk, blk],
        out_specs=pl.BlockSpec((r, c), lambda i: (0, 0)), compiler_params=_cparams("arbitrary"),
    )(a, b)


def _reduce_scatter(grads):
    n = len(grads)
    shp = [g.shape[2:] for g in grads]
    x, y, c = _position()
    pos_cx = jnp.stack([c, x]).astype(jnp.int32)
    pos_y = jnp.stack([y]).astype(jnp.int32)
    recv = _swap("rs_c", grads, lambda x, y, c: (x, y, 1 - c), lambda ref, x, y, c: ref.at[1 - c],
                 [(N_CHIPS,) + s for s in shp])
    p1 = [_add_keep_give(
        f"rs_c_add{k}", pos_cx,
        grads[k], lambda i, p: (p[0], 2 * p[1] + i), lambda i, p: (p[0], 2 * (1 - p[1]) + i),
        recv[k], lambda i, p: (2 * p[1] + i,), lambda i, p: (2 * (1 - p[1]) + i,), 2) for k in range(n)]
    recv = _swap("rs_x", [p[1] for p in p1], lambda x, y, c: (1 - x, y, c), lambda ref, x, y, c: ref,
                 [(2,) + s for s in shp])
    p2 = [_add_keep_give(
        f"rs_x_add{k}", pos_y,
        p1[k][0], lambda i, p: (p[0],), lambda i, p: (1 - p[0],),
        recv[k], lambda i, p: (p[0],), lambda i, p: (1 - p[0],), 1) for k in range(n)]
    recv = _swap("rs_y", [p[1] for p in p2], lambda x, y, c: (x, 1 - y, c), lambda ref, x, y, c: ref,
                 [(1,) + s for s in shp])
    mine = [_add_last(f"rs_y_add{k}", p2[k][0], recv[k]) for k in range(n)]
    other = _swap("rs_swap_halves", mine, lambda x, y, c: (x, y, 1 - c), lambda ref, x, y, c: ref, shp)
    return list(zip(mine, other))


def _all_reduce_small(v):
    r = v.shape[0]

    def body(v_ref, out_ref, buf, send_sems, recv_sems, local_sem):
        x, y, c = _position()
        me, sibling = (x, y, c), (x, y, 1 - c)
        chips = [(1 - x, y), (x, 1 - y), (1 - x, 1 - y)]

        def rows(px, py, pc):
            return buf.at[pl.ds((4 * px + 2 * py + pc) * r, r), :]

        def copy(k, block, to, src=None):
            return _remote(rows(*block) if src is None else src, rows(*block), send_sems.at[k], recv_sems.at[k], to)

        mine = pltpu.make_async_copy(v_ref, rows(*me), local_sem)
        mine.start()
        first = [copy(0, me, sibling, src=v_ref)]
        first += [copy(1 + j, me, (*chip, c), src=v_ref) for j, chip in enumerate(chips)]
        for cp in first:
            cp.start()
        passed = [copy(4 + j, (*chip, c), sibling) for j, chip in enumerate(chips)]
        for j, chip in enumerate(chips):
            copy(1 + j, (*chip, c), me).wait_recv()
            passed[j].start()
        copy(0, sibling, me).wait_recv()
        for j, chip in enumerate(chips):
            copy(4 + j, (*chip, 1 - c), me).wait_recv()
        for cp in first + passed:
            cp.wait_send()
        mine.wait()
        acc = buf[0:r, :]
        for dev in range(1, N_DEV):
            acc = acc + buf[dev * r:(dev + 1) * r, :]
        out_ref[...] = acc

    return pl.pallas_call(
        body, name="all_reduce_small",
        out_shape=jax.ShapeDtypeStruct(v.shape, F32),
        in_specs=[pl.BlockSpec(memory_space=pltpu.VMEM)],
        out_specs=pl.BlockSpec(memory_space=pltpu.VMEM),
        scratch_shapes=[pltpu.VMEM((N_DEV * r, LANES), F32), pltpu.SemaphoreType.DMA((7,)),
                        pltpu.SemaphoreType.DMA((7,)), pltpu.SemaphoreType.DMA],
        compiler_params=pltpu.CompilerParams(has_side_effects=True, vmem_limit_bytes=VMEM_LIMIT_V7X),
    )(v)


def _adamw_update(w, gg, m, v):
    mn = ADAM_B1 * m + (1.0 - ADAM_B1) * gg
    vn = ADAM_B2 * v + (1.0 - ADAM_B2) * (gg * gg)
    m_hat = mn / (1.0 - ADAM_B1 ** ADAM_STEP)
    v_hat = vn / (1.0 - ADAM_B2 ** ADAM_STEP)
    return -ADAM_LR * (m_hat / (jnp.sqrt(v_hat) + ADAM_EPS) + ADAM_WD * w), mn, vn


def _adamw(name, w, g, m, v):
    def body(w_ref, g_ref, m_ref, v_ref, d_ref, mo_ref, vo_ref):
        d_ref[...], mo_ref[...], vo_ref[...] = _adamw_update(w_ref[...], g_ref[...], m_ref[...], v_ref[...])

    blk = pl.BlockSpec(w.shape, lambda i: (0, 0))
    return pl.pallas_call(
        body, name=name, out_shape=(jax.ShapeDtypeStruct(w.shape, F32),) * 3, grid=(1,),
        in_specs=[blk] * 4, out_specs=(blk,) * 3, compiler_params=_cparams("arbitrary"),
    )(w, g, m, v)


def _adamw_halves(name, pos_c, w, g_mine, g_other, m, v, tr):
    r, c = w.shape
    rh = r // 2
    tr = tr if rh % tr == 0 else rh
    nt = rh // tr

    def body(p_ref, w_ref, gm_ref, go_ref, m_ref, v_ref, g_ref, d_ref, mo_ref, vo_ref):
        gg = jnp.where(pl.program_id(0) == p_ref[0], gm_ref[...], go_ref[...])
        g_ref[...] = gg
        d_ref[...], mo_ref[...], vo_ref[...] = _adamw_update(w_ref[...], gg, m_ref[...], v_ref[...])

    full = pl.BlockSpec((tr, c), lambda h, i, p: (h * nt + i, 0))
    half = pl.BlockSpec((tr, c), lambda h, i, p: (i, 0))
    return pl.pallas_call(
        body, name=name, out_shape=(jax.ShapeDtypeStruct((r, c), F32),) * 4,
        grid_spec=pltpu.PrefetchScalarGridSpec(
            num_scalar_prefetch=1, grid=(2, nt),
            in_specs=[full, half, half, full, full], out_specs=(full,) * 4),
        compiler_params=_cparams("parallel", "parallel"),
    )(pos_c, w, g_mine, g_other, m, v)


def _col_sharded_to_comm(g):
    k, n = g.shape
    return g.reshape(2, k // 2, N_CHIPS, n // N_CHIPS).transpose(0, 2, 1, 3)


def _row_sharded_to_comm(g):
    r, c = g.shape
    return g.reshape(N_CHIPS, 2, r // (2 * N_CHIPS), c).transpose(1, 0, 2, 3)


def _pack_small(g1, bfv, mix, scale, g2n, gf, extra=None):
    row8 = jnp.pad(bfv.reshape(1, N_HEADS), ((0, 0), (0, LANES - N_HEADS)))
    if extra is not None:
        row8 = row8 + jnp.pad(extra[:, :1], ((0, 0), (N_HEADS, LANES - N_HEADS - 1)))
    return jnp.concatenate([
        g1.reshape(8, LANES), jnp.pad(row8, ((0, 7), (0, 0))), mix.reshape(512, LANES),
        jnp.pad(scale.reshape(4, LANES), ((0, 4), (0, 0))), g2n.reshape(8, LANES), gf.reshape(8, LANES)], axis=0)


def _unpack_small(s, like):
    g1, bfv, mix, scale, g2n, gf = like
    return (s[0:8].reshape(g1.shape), s[8, :N_HEADS].reshape(bfv.shape), s[16:528].reshape(mix.shape),
            s[528:532].reshape(scale.shape), s[536:544].reshape(g2n.shape), s[544:552].reshape(gf.shape))


def kernel(x, norm1_g, w_in, b_forget, pool_mix, pool_scale, w_pool_out, w_attn_out, w_out, norm2_g, w_ffn_gate, w_ffn_up, w_ffn_down, norm_f_g, loss_target, m_norm1_g, m_w_in, m_b_forget, m_pool_mix, m_pool_scale, m_w_pool_out, m_w_attn_out, m_w_out, m_norm2_g, m_w_ffn_gate, m_w_ffn_up, m_w_ffn_down, m_norm_f_g, v_norm1_g, v_w_in, v_b_forget, v_pool_mix, v_pool_scale, v_w_pool_out, v_w_attn_out, v_w_out, v_norm2_g, v_w_ffn_gate, v_w_ffn_up, v_w_ffn_down, v_norm_f_g):
    nb, seq, d = x.shape
    big_w = (w_in, w_pool_out, w_attn_out, w_out, w_ffn_gate, w_ffn_up, w_ffn_down)
    big_m = (m_w_in, m_w_pool_out, m_w_attn_out, m_w_out, m_w_ffn_gate, m_w_ffn_up, m_w_ffn_down)
    big_v = (v_w_in, v_w_pool_out, v_w_attn_out, v_w_out, v_w_ffn_gate, v_w_ffn_up, v_w_ffn_down)
    row_sharded = (False, False, False, True, False, False, True)
    small_w = (norm1_g, b_forget, pool_mix, pool_scale, norm2_g, norm_f_g)
    small_m = (m_norm1_g, m_b_forget, m_pool_mix, m_pool_scale, m_norm2_g, m_norm_f_g)
    small_v = (v_norm1_g, v_b_forget, v_pool_mix, v_pool_scale, v_norm2_g, v_norm_f_g)

    me = 2 * lax.axis_index("x") + lax.axis_index("y")
    local = [w[0].astype(BF16).reshape(2, w.shape[1] // 2, w.shape[2]) for w in big_w]
    gathered = _gather_weights(local)
    full = []
    for gw, lw, w, rs in zip(gathered, local, big_w, row_sharded):
        r, c = w.shape[1:]
        gw = lax.dynamic_update_index_in_dim(gw, lw, me, 0).reshape(N_CHIPS, r, c)
        full.append(gw.reshape(N_CHIPS * r, c) if rs else gw.transpose(1, 0, 2).reshape(r, N_CHIPS * c))

    loss, dx, big_g, small_g = _local_step(
        x.reshape(nb * seq, d), loss_target.reshape(nb * seq, d), seq,
        norm1_g, full[0], b_forget, pool_mix[0], pool_scale, full[1], full[2], full[3], norm2_g,
        full[4], full[5], full[6], norm_f_g.reshape(1, d))

    comm = [_row_sharded_to_comm(g) if rs else _col_sharded_to_comm(g) for g, rs in zip(big_g, row_sharded)]
    reduced = _reduce_scatter(comm)
    small_sum = _all_reduce_small(_pack_small(*small_g, extra=loss))
    loss_out = small_sum[8, N_HEADS]

    grads, deltas, new_m, new_v = [None] * 13, [None] * 13, [None] * 13, [None] * 13
    big_pos = (1, 5, 6, 7, 9, 10, 11)
    small_pos = (0, 2, 3, 4, 8, 12)
    pos_c = jnp.stack([lax.axis_index("c")]).astype(jnp.int32)
    for k, pos in enumerate(big_pos):
        w = big_w[k]
        outs = _adamw_halves(f"adamw_big{k}", pos_c, w[0], reduced[k][0], reduced[k][1], big_m[k][0], big_v[k][0], 256)
        grads[pos], deltas[pos], new_m[pos], new_v[pos] = (a.reshape(w.shape) for a in outs)
    dl, mn, vn = _adamw("adamw_small", _pack_small(*small_w), small_sum * _small_mask(), _pack_small(*small_m),
                        _pack_small(*small_v))
    for pos, g, a, b, e in zip(small_pos, _unpack_small(small_sum, small_w), _unpack_small(dl, small_w),
                               _unpack_small(mn, small_w), _unpack_small(vn, small_w)):
        grads[pos], deltas[pos], new_m[pos], new_v[pos] = g, a, b, e

    return (loss_out, dx.reshape(nb, seq, d), *grads, *deltas, *new_m, *new_v)


def _small_mask():
    rows = lax.broadcasted_iota(jnp.int32, (552, LANES), 0)
    lanes = lax.broadcasted_iota(jnp.int32, (552, LANES), 1)
    return jnp.where(jnp.logical_and(rows == 8, lanes == N_HEADS), 0.0, 1.0).astype(F32)
```

```python
import functools

import jax
import jax.numpy as jnp
from jax import lax
from jax.experimental import pallas as pl
from jax.experimental.pallas import tpu as pltpu

F32 = jnp.float32
BF16 = jnp.bfloat16

D_MODEL = 1024
POOL_WINDOWS = (2, 4, 8, 16)
POOL_GROUPS = 4
POOL_GROUP_DIM = 128
POOL_WIDTH = 512
HEAD_DIM = 64
N_HEADS = 8
ATTN_WIDTH = 512
D_FF = 2816
RMS_EPS = 1e-6
ATTN_SCALE = HEAD_DIM ** -0.5
NEG_BIG = -1e30

ADAM_LR = 0.001
ADAM_B1 = 0.9
ADAM_B2 = 0.999
ADAM_EPS = 1e-08
ADAM_WD = 0.01
ADAM_STEP = 10

LANES = 128
N_CHIPS = 4
N_DEV = 8
VMEM_LIMIT_V7X = 52 * 1024 * 1024
MESH = pl.DeviceIdType.MESH
ANY = pl.BlockSpec(memory_space=pl.ANY)


def _cparams(*sem):
    return pltpu.CompilerParams(dimension_semantics=sem if sem else None, vmem_limit_bytes=VMEM_LIMIT_V7X)


def _dot(a, b):
    return lax.dot_general(a, b, (((1,), (0,)), ((), ())), preferred_element_type=F32)


def _dot_nt(a, b):
    return lax.dot_general(a, b, (((1,), (1,)), ((), ())), preferred_element_type=F32)


def _dot_tn(a, b):
    return lax.dot_general(a, b, (((0,), (0,)), ((), ())), preferred_element_type=F32)


def _sigmoid(x):
    return jax.nn.sigmoid(x)


def _rms_fwd(x, g):
    r = lax.rsqrt(jnp.mean(x * x, axis=-1, keepdims=True) + RMS_EPS)
    return (x * r) * g


def _rms_bwd(x, g, dy):
    r = lax.rsqrt(jnp.mean(x * x, axis=-1, keepdims=True) + RMS_EPS)
    xh = x * r
    dg = jnp.sum(dy * xh, axis=0, keepdims=True)
    dxh = dy * g
    dx = r * (dxh - xh * jnp.mean(dxh * xh, axis=-1, keepdims=True))
    return dx, dg


def _matmul(name, a, b, mode, out_dtype, tm, tn, tk):
    if mode == "nn":
        (m, k), (_, n) = a.shape, b.shape
    elif mode == "nt":
        (m, k), (n, _) = a.shape, b.shape
    else:
        (k, m), (_, n) = a.shape, b.shape
    tm, tn, tk = min(tm, m), min(tn, n), min(tk, k)
    assert m % tm == 0 and n % tn == 0 and k % tk == 0, (name, m, n, k, tm, tn, tk)
    nk = k // tk
    if mode == "tn":
        a_spec = pl.BlockSpec((tk, tm), lambda i, j, kk: (kk, i))
    else:
        a_spec = pl.BlockSpec((tm, tk), lambda i, j, kk: (i, kk))
    if mode == "nt":
        b_spec = pl.BlockSpec((tn, tk), lambda i, j, kk: (j, kk))
    else:
        b_spec = pl.BlockSpec((tk, tn), lambda i, j, kk: (kk, j))
    dot = {"nn": _dot, "nt": _dot_nt, "tn": _dot_tn}[mode]
    use_scratch = nk > 1 and out_dtype != F32

    def body(a_ref, b_ref, o_ref, *scratch):
        prod = dot(a_ref[...].astype(BF16), b_ref[...].astype(BF16))
        if nk == 1:
            o_ref[...] = prod.astype(out_dtype)
            return
        acc = scratch[0] if use_scratch else o_ref
        kk = pl.program_id(2)

        @pl.when(kk == 0)
        def _():
            acc[...] = prod

        @pl.when(kk > 0)
        def _():
            acc[...] += prod

        if use_scratch:
            @pl.when(kk == nk - 1)
            def _():
                o_ref[...] = acc[...].astype(out_dtype)

    return pl.pallas_call(
        body,
        name=name,
        out_shape=jax.ShapeDtypeStruct((m, n), out_dtype),
        grid=(m // tm, n // tn, nk),
        in_specs=[a_spec, b_spec],
        out_specs=pl.BlockSpec((tm, tn), lambda i, j, kk: (i, j)),
        scratch_shapes=[pltpu.VMEM((tm, tn), F32)] if use_scratch else [],
        compiler_params=_cparams("parallel", "parallel", "arbitrary"),
    )(a, b)


def _norm_fwd(name, x, g, tm):
    t, d = x.shape
    tm = min(tm, t)

    def body(x_ref, g_ref, h_ref):
        h_ref[...] = _rms_fwd(x_ref[...], g_ref[...]).astype(BF16)

    return pl.pallas_call(
        body, name=name, out_shape=jax.ShapeDtypeStruct((t, d), BF16), grid=(t // tm,),
        in_specs=[pl.BlockSpec((tm, d), lambda i: (i, 0)), pl.BlockSpec((1, d), lambda i: (0, 0))],
        out_specs=pl.BlockSpec((tm, d), lambda i: (i, 0)),
        compiler_params=_cparams("parallel"),
    )(x, g)


def _split3(x):
    hi = x.astype(BF16)
    r1 = x - hi.astype(F32)
    mid = r1.astype(BF16)
    lo = (r1 - mid.astype(F32)).astype(BF16)
    return hi, mid, lo


def _tri_dot(tri, x):
    hi, mid, lo = _split3(x)
    return _dot(tri, hi) + _dot(tri, mid) + _dot(tri, lo)


def _forget_fwd(h, wf, bf, seq):
    t, d = h.shape
    cb = min(256, seq)

    def body(h_ref, wf_ref, bf_ref, fl_ref, fc_ref):
        fl = _dot(h_ref[...], wf_ref[...])
        fl_ref[...] = fl
        xx = fl + bf_ref[...]
        lf = jnp.minimum(xx, 0.0) - jnp.log(1.0 + jnp.exp(-jnp.abs(xx)))
        ri = lax.broadcasted_iota(jnp.int32, (cb, cb), 0)
        ci = lax.broadcasted_iota(jnp.int32, (cb, cb), 1)
        tri = (ri >= ci).astype(BF16)
        carry = jnp.zeros((1, LANES), F32)
        for blk in range(seq // cb):
            cs = _tri_dot(tri, lf[blk * cb:(blk + 1) * cb]) + carry
            fc_ref[blk * cb:(blk + 1) * cb, :] = cs
            carry = cs[cb - 1:cb, :]

    return pl.pallas_call(
        body, name="forget_fwd",
        out_shape=(jax.ShapeDtypeStruct((t, LANES), F32), jax.ShapeDtypeStruct((t, LANES), F32)),
        grid=(t // seq,),
        in_specs=[pl.BlockSpec((seq, d), lambda b: (b, 0)), pl.BlockSpec((d, LANES), lambda b: (0, 0)),
                  pl.BlockSpec((1, LANES), lambda b: (0, 0))],
        out_specs=(pl.BlockSpec((seq, LANES), lambda b: (b, 0)), pl.BlockSpec((seq, LANES), lambda b: (b, 0))),
        compiler_params=_cparams("parallel"),
    )(h, wf, bf)


def _pool_fwd(u, mix, scale, seq):
    t = u.shape[0]

    def body(u_ref, mix_ref, sc_ref, p_ref, ps_ref):
        tpos = lax.broadcasted_iota(jnp.int32, (seq, POOL_GROUP_DIM), 0)
        for g in range(POOL_GROUPS):
            sl = slice(g * POOL_GROUP_DIM, (g + 1) * POOL_GROUP_DIM)
            ug = u_ref[:, sl]
            s = ug
            for lvl in range(g + 1):
                d = 2 ** lvl
                s = s + jnp.where(tpos >= d, pltpu.roll(s, d, 0), 0.0)
            cnt = jnp.minimum(tpos + 1, POOL_WINDOWS[g]).astype(F32)
            pb = (s / cnt - ug).astype(BF16)
            p_ref[:, sl] = pb
            ps_ref[:, sl] = (_dot(pb, mix_ref[g]) * sc_ref[:, sl]).astype(BF16)

    return pl.pallas_call(
        body, name="pool_fwd",
        out_shape=(jax.ShapeDtypeStruct((t, POOL_WIDTH), BF16), jax.ShapeDtypeStruct((t, POOL_WIDTH), BF16)),
        grid=(t // seq,),
        in_specs=[pl.BlockSpec((seq, POOL_WIDTH), lambda b: (b, 0)),
                  pl.BlockSpec((POOL_GROUPS, POOL_GROUP_DIM, POOL_GROUP_DIM), lambda b: (0, 0, 0)),
                  pl.BlockSpec((1, POOL_WIDTH), lambda b: (0, 0))],
        out_specs=(pl.BlockSpec((seq, POOL_WIDTH), lambda b: (b, 0)), pl.BlockSpec((seq, POOL_WIDTH), lambda b: (b, 0))),
        compiler_params=_cparams("parallel"),
    )(u, mix, scale)


def _aug_constants():
    w = N_HEADS * LANES
    rows = jnp.arange(3 * LANES)
    piece, head = rows // LANES, rows % LANES
    cols = jnp.arange(w)
    live = (head < N_HEADS)[:, None]
    pq = (live & (cols[None, :] == (head * LANES + HEAD_DIM + piece)[:, None])).astype(BF16)
    pk = -(live & (cols[None, :] == (head * LANES + HEAD_DIM + 3 + piece)[:, None])).astype(BF16)
    lane = cols % LANES
    oq = ((lane >= HEAD_DIM + 3) & (lane < HEAD_DIM + 6)).astype(F32)[None, :]
    ok = ((lane >= HEAD_DIM) & (lane < HEAD_DIM + 3)).astype(F32)[None, :]
    return pq, pk, oq, ok


def _head_blocks(w):
    d = w.shape[0]
    return jnp.pad(w.reshape(d, N_HEADS, HEAD_DIM), ((0, 0), (0, 0), (0, LANES - HEAD_DIM))).reshape(d, N_HEADS * LANES)


def _attn_prep(h, wq, wk, wv, fcum, tm):
    t, d = h.shape
    tm = min(tm, t)
    w = N_HEADS * LANES
    pq, pk, oq, ok = _aug_constants()

    def body(h_ref, wq_ref, wk_ref, wv_ref, f_ref, pq_ref, pk_ref, oq_ref, ok_ref, qa_ref, ka_ref, v_ref):
        hh = h_ref[...]
        fs = jnp.concatenate(_split3(f_ref[...]), axis=1)
        q = _dot(hh, wq_ref[...]).astype(BF16).astype(F32) * ATTN_SCALE
        qa_ref[...] = (q + _dot(fs, pq_ref[...]) + oq_ref[...]).astype(BF16)
        k = _dot(hh, wk_ref[...]).astype(BF16).astype(F32)
        ka_ref[...] = (k + _dot(fs, pk_ref[...]) + ok_ref[...]).astype(BF16)
        v_ref[...] = _dot(hh, wv_ref[...]).astype(BF16)

    row = lambda n: pl.BlockSpec((tm, n), lambda i: (i, 0))
    full = lambda a: pl.BlockSpec(a.shape, lambda i: (0, 0))
    return pl.pallas_call(
        body, name="attn_prep",
        out_shape=(jax.ShapeDtypeStruct((t, w), BF16), jax.ShapeDtypeStruct((t, w), BF16),
                   jax.ShapeDtypeStruct((t, ATTN_WIDTH), BF16)),
        grid=(t // tm,),
        in_specs=[row(d), full(wq), full(wk), full(wv), row(LANES), full(pq), full(pk), full(oq), full(ok)],
        out_specs=(row(w), row(w), row(ATTN_WIDTH)),
        compiler_params=_cparams("parallel"),
    )(h, wq, wk, wv, fcum, pq, pk, oq, ok)


def _fold_lanes(x, op):
    out = x[:, :LANES]
    for g in range(1, x.shape[1] // LANES):
        out = op(out, x[:, g * LANES:(g + 1) * LANES])
    return out


def _attn_fwd(qa, ka, v, seq, tq):
    t = qa.shape[0]
    nq = seq // tq
    hp_n = N_HEADS // 2
    heads = [slice(e * LANES, (e + 1) * LANES) for e in range(2)]

    def body(q_ref, k_ref, v_ref, o_ref, lse_ref, s_buf):
        i = pl.program_id(2)
        diag_ok = lax.broadcasted_iota(jnp.int32, (tq, tq), 0) >= lax.broadcasted_iota(jnp.int32, (tq, tq), 1)
        qs = [q_ref[:, hl] for hl in heads]

        def sweep1(j, mxs):
            r0 = pl.multiple_of(j * tq, tq)
            out = []
            for e, hl in enumerate(heads):
                s = _dot_nt(qs[e], k_ref[pl.ds(r0, tq), hl])
                s = jnp.where(jnp.logical_or(diag_ok, j < i), s, NEG_BIG)
                s_buf[e, j] = s
                out.append(jnp.maximum(mxs[e], _fold_lanes(s, jnp.maximum)))
            return tuple(out)

        mxs = lax.fori_loop(0, i + 1, sweep1, (jnp.full((tq, LANES), NEG_BIG, F32),) * 2)
        ms = [jnp.max(mx, axis=1, keepdims=True) for mx in mxs]

        def sweep2(j, carry):
            r0 = pl.multiple_of(j * tq, tq)
            vv = v_ref[pl.ds(r0, tq), :]
            out = []
            for e in range(2):
                p = jnp.exp(s_buf[e, j] - ms[e])
                out += [carry[2 * e] + _fold_lanes(p, jnp.add), carry[2 * e + 1] + _dot(p.astype(BF16), vv)]
            return tuple(out)

        res = lax.fori_loop(0, i + 1, sweep2, (jnp.zeros((tq, LANES), F32),) * 4)
        outs = []
        for e in range(2):
            l = jnp.sum(res[2 * e], axis=1, keepdims=True)
            outs.append(res[2 * e + 1] / l)
            lse_ref[:, e:e + 1] = ms[e] + jnp.log(l)
        lane = lax.broadcasted_iota(jnp.int32, (tq, LANES), 1)
        o_ref[...] = jnp.where(lane < HEAD_DIM, outs[0], outs[1])

    return pl.pallas_call(
        body, name="attn_fwd",
        out_shape=(jax.ShapeDtypeStruct((t, ATTN_WIDTH), F32), jax.ShapeDtypeStruct((hp_n, t, 2), F32)),
        grid=(t // seq, hp_n, nq),
        in_specs=[pl.BlockSpec((tq, 2 * LANES), lambda b, hp, i: (b * nq + i, hp)),
                  pl.BlockSpec((seq, 2 * LANES), lambda b, hp, i: (b, hp)),
                  pl.BlockSpec((seq, LANES), lambda b, hp, i: (b, hp))],
        out_specs=(pl.BlockSpec((tq, LANES), lambda b, hp, i: (b * nq + i, hp)),
                   pl.BlockSpec((None, tq, 2), lambda b, hp, i: (hp, b * nq + i, 0))),
        scratch_shapes=[pltpu.VMEM((2, nq, tq, tq), F32)],
        compiler_params=_cparams("parallel", "parallel", "arbitrary"),
    )(qa, ka, v)


def _merge_fwd(x, ps, o, g2, wpo, wao, wout, tm):
    t, d = x.shape
    tm = min(tm, t)

    def body(x_ref, ps_ref, o_ref, gp_ref, ga_ref, wpo_ref, wao_ref, wout_ref, mg_ref, x1_ref):
        py = _dot(ps_ref[...], wpo_ref[...])
        ay = _dot(o_ref[...].astype(BF16), wao_ref[...])
        mb = (_sigmoid(gp_ref[...]) * py + _sigmoid(ga_ref[...]) * ay).astype(BF16)
        mg_ref[...] = mb
        x1_ref[...] = x_ref[...] + _dot(mb, wout_ref[...])

    row = lambda w: pl.BlockSpec((tm, w), lambda i: (i, 0))
    full = lambda a: pl.BlockSpec(a.shape, lambda i: (0, 0))
    return pl.pallas_call(
        body, name="merge_fwd",
        out_shape=(jax.ShapeDtypeStruct((t, d), BF16), jax.ShapeDtypeStruct((t, d), F32)),
        grid=(t // tm,),
        in_specs=[row(d), row(POOL_WIDTH), row(ATTN_WIDTH), pl.BlockSpec((tm, d), lambda i: (i, 0)),
                  pl.BlockSpec((tm, d), lambda i: (i, 1)), full(wpo), full(wao), full(wout)],
        out_specs=(row(d), row(d)),
        compiler_params=_cparams("parallel"),
    )(x, ps, o, g2, g2, wpo, wao, wout)


def _ffn_fwd(x1, g, wg, wu, wd, tm, tf):
    t, d = x1.shape
    f = wg.shape[1]
    tm = min(tm, t)
    nf = f // tf

    def body(x1_ref, g_ref, wg_ref, wu_ref, wd_ref, h2_ref, gt_ref, up_ref, act_ref, x2_ref):
        j = pl.program_id(1)

        @pl.when(j == 0)
        def _():
            h2_ref[...] = _rms_fwd(x1_ref[...], g_ref[...]).astype(BF16)

        h2 = h2_ref[...]
        gt = _dot(h2, wg_ref[...])
        up = _dot(h2, wu_ref[...])
        act = (gt * _sigmoid(gt) * up).astype(BF16)
        gt_ref[...] = gt
        up_ref[...] = up
        act_ref[...] = act
        prod = _dot(act, wd_ref[...])

        @pl.when(j == 0)
        def _():
            x2_ref[...] = prod

        @pl.when(j > 0)
        def _():
            x2_ref[...] += prod

        @pl.when(j == nf - 1)
        def _():
            x2_ref[...] += x1_ref[...]

    return pl.pallas_call(
        body, name="ffn_fwd",
        out_shape=(jax.ShapeDtypeStruct((t, d), BF16), jax.ShapeDtypeStruct((t, f), F32),
                   jax.ShapeDtypeStruct((t, f), F32), jax.ShapeDtypeStruct((t, f), BF16),
                   jax.ShapeDtypeStruct((t, d), F32)),
        grid=(t // tm, nf),
        in_specs=[pl.BlockSpec((tm, d), lambda i, j: (i, 0)), pl.BlockSpec((1, d), lambda i, j: (0, 0)),
                  pl.BlockSpec((d, tf), lambda i, j: (0, j)), pl.BlockSpec((d, tf), lambda i, j: (0, j)),
                  pl.BlockSpec((tf, d), lambda i, j: (j, 0))],
        out_specs=(pl.BlockSpec((tm, d), lambda i, j: (i, 0)), pl.BlockSpec((tm, tf), lambda i, j: (i, j)),
                   pl.BlockSpec((tm, tf), lambda i, j: (i, j)), pl.BlockSpec((tm, tf), lambda i, j: (i, j)),
                   pl.BlockSpec((tm, d), lambda i, j: (i, 0))),
        compiler_params=_cparams("parallel", "arbitrary"),
    )(x1, g, wg, wu, wd)


def _final_fwd_bwd(x2, target, g, tm):
    t, d = x2.shape
    tm = min(tm, t)

    def body(x_ref, t_ref, g_ref, loss_ref, dx_ref, dg_ref):
        i = pl.program_id(0)
        x = x_ref[...]
        gg = g_ref[...]
        err = _rms_fwd(x, gg) - t_ref[...]
        part = 0.5 * jnp.sum(jnp.mean(err * err, axis=-1, keepdims=True), axis=0, keepdims=True)
        dx, dg = _rms_bwd(x, gg, err * (1.0 / d))
        dx_ref[...] = dx

        @pl.when(i == 0)
        def _():
            loss_ref[...] = jnp.zeros_like(loss_ref)
            dg_ref[...] = jnp.zeros_like(dg_ref)

        loss_ref[...] += jnp.broadcast_to(part, loss_ref.shape)
        dg_ref[...] += dg

    return pl.pallas_call(
        body, name="final_fwd_bwd",
        out_shape=(jax.ShapeDtypeStruct((1, LANES), F32), jax.ShapeDtypeStruct((t, d), F32),
                   jax.ShapeDtypeStruct((1, d), F32)),
        grid=(t // tm,),
        in_specs=[pl.BlockSpec((tm, d), lambda i: (i, 0)), pl.BlockSpec((tm, d), lambda i: (i, 0)),
                  pl.BlockSpec((1, d), lambda i: (0, 0))],
        out_specs=(pl.BlockSpec((1, LANES), lambda i: (0, 0)), pl.BlockSpec((tm, d), lambda i: (i, 0)),
                   pl.BlockSpec((1, d), lambda i: (0, 0))),
        compiler_params=_cparams("arbitrary"),
    )(x2, target, g)


def _ffn_bwd(dx2, x1, g, gt, up, wg, wu, wd, tm, tf):
    t, d = dx2.shape
    f = gt.shape[1]
    tm = min(tm, t)
    nf = f // tf

    def body(dx2_ref, x1_ref, g_ref, gt_ref, up_ref, wg_ref, wu_ref, wd_ref, dgt_ref, dup_ref, dx1_ref, dg_ref, acc_ref):
        i, j = pl.program_id(0), pl.program_id(1)
        dact = _dot_nt(dx2_ref[...].astype(BF16), wd_ref[...])
        gtv = gt_ref[...]
        sg = _sigmoid(gtv)
        dup = (dact * (gtv * sg)).astype(BF16)
        dgt = (dact * up_ref[...] * (sg * (1.0 + gtv * (1.0 - sg)))).astype(BF16)
        dgt_ref[...] = dgt
        dup_ref[...] = dup
        contrib = _dot_nt(dgt, wg_ref[...]) + _dot_nt(dup, wu_ref[...])

        @pl.when(j == 0)
        def _():
            acc_ref[...] = contrib

        @pl.when(j > 0)
        def _():
            acc_ref[...] += contrib

        @pl.when(jnp.logical_and(i == 0, j == 0))
        def _():
            dg_ref[...] = jnp.zeros_like(dg_ref)

        @pl.when(j == nf - 1)
        def _():
            dxn, dg = _rms_bwd(x1_ref[...], g_ref[...], acc_ref[...])
            dx1_ref[...] = dx2_ref[...] + dxn
            dg_ref[...] += dg

    return pl.pallas_call(
        body, name="ffn_bwd",
        out_shape=(jax.ShapeDtypeStruct((t, f), BF16), jax.ShapeDtypeStruct((t, f), BF16),
                   jax.ShapeDtypeStruct((t, d), F32), jax.ShapeDtypeStruct((1, d), F32)),
        grid=(t // tm, nf),
        in_specs=[pl.BlockSpec((tm, d), lambda i, j: (i, 0)), pl.BlockSpec((tm, d), lambda i, j: (i, 0)),
                  pl.BlockSpec((1, d), lambda i, j: (0, 0)),
                  pl.BlockSpec((tm, tf), lambda i, j: (i, j)), pl.BlockSpec((tm, tf), lambda i, j: (i, j)),
                  pl.BlockSpec((d, tf), lambda i, j: (0, j)), pl.BlockSpec((d, tf), lambda i, j: (0, j)),
                  pl.BlockSpec((tf, d), lambda i, j: (j, 0))],
        out_specs=(pl.BlockSpec((tm, tf), lambda i, j: (i, j)), pl.BlockSpec((tm, tf), lambda i, j: (i, j)),
                   pl.BlockSpec((tm, d), lambda i, j: (i, 0)), pl.BlockSpec((1, d), lambda i, j: (0, 0))),
        scratch_shapes=[pltpu.VMEM((tm, d), F32)],
        compiler_params=_cparams("arbitrary", "arbitrary"),
    )(dx2, x1, g, gt, up, wg, wu, wd)


def _merge_bwd(dx1, ps, o, g2, wpo, wao, wout, tm):
    t, d = dx1.shape
    tm = min(tm, t)

    def body(dx1_ref, ps_ref, o_ref, gp_ref, ga_ref, wpo_ref, wao_ref, wout_ref, dpy_ref, day_ref, dg2_ref, dps_ref, da_ref):
        dm = _dot_nt(dx1_ref[...].astype(BF16), wout_ref[...])
        py = _dot(ps_ref[...], wpo_ref[...])
        ay = _dot(o_ref[...].astype(BF16), wao_ref[...])
        sp = _sigmoid(gp_ref[...])
        sa = _sigmoid(ga_ref[...])
        dpy = (dm * sp).astype(BF16)
        day = (dm * sa).astype(BF16)
        dpy_ref[...] = dpy
        day_ref[...] = day
        dg2_ref[:, :d] = (dm * py * (sp * (1.0 - sp))).astype(BF16)
        dg2_ref[:, d:] = (dm * ay * (sa * (1.0 - sa))).astype(BF16)
        dps_ref[...] = _dot_nt(dpy, wpo_ref[...])
        da_ref[...] = _dot_nt(day, wao_ref[...]).astype(BF16)

    row = lambda w: pl.BlockSpec((tm, w), lambda i: (i, 0))
    full = lambda a: pl.BlockSpec(a.shape, lambda i: (0, 0))
    return pl.pallas_call(
        body, name="merge_bwd",
        out_shape=(jax.ShapeDtypeStruct((t, d), BF16), jax.ShapeDtypeStruct((t, d), BF16),
                   jax.ShapeDtypeStruct((t, 2 * d), BF16), jax.ShapeDtypeStruct((t, POOL_WIDTH), F32),
                   jax.ShapeDtypeStruct((t, ATTN_WIDTH), BF16)),
        grid=(t // tm,),
        in_specs=[row(d), row(POOL_WIDTH), row(ATTN_WIDTH), pl.BlockSpec((tm, d), lambda i: (i, 0)),
                  pl.BlockSpec((tm, d), lambda i: (i, 1)), full(wpo), full(wao), full(wout)],
        out_specs=(row(d), row(d), row(2 * d), row(POOL_WIDTH), row(ATTN_WIDTH)),
        compiler_params=_cparams("parallel"),
    )(dx1, ps, o, g2, g2, wpo, wao, wout)


def _attn_bwd(qa, ka, v, do, lse4, seq, tq):
    t = qa.shape[0]
    nq = seq // tq
    hp_n = N_HEADS // 2
    heads = [slice(e * LANES, (e + 1) * LANES) for e in range(2)]

    def body(q_ref, k_ref, v_ref, do_ref, lse_ref, dq_ref, dk_ref, dv_ref, dfr_ref, dk_acc, dv_acc, p_buf, dp_buf):
        diag_ok = lax.broadcasted_iota(jnp.int32, (tq, tq), 0) >= lax.broadcasted_iota(jnp.int32, (tq, tq), 1)
        lane_q = lax.broadcasted_iota(jnp.int32, (tq, LANES), 1)
        lane_s = lax.broadcasted_iota(jnp.int32, (seq, LANES), 1)
        mine_q = [lane_q < HEAD_DIM, lane_q >= HEAD_DIM]
        dv_acc[...] = jnp.zeros_like(dv_acc)
        dk_acc[...] = jnp.zeros_like(dk_acc)
        dfr_ref[...] = jnp.zeros_like(dfr_ref)

        def q_step(i, _):
            q0 = pl.multiple_of(i * tq, tq)
            qs = [q_ref[pl.ds(q0, tq), hl] for hl in heads]
            dov = do_ref[pl.ds(q0, tq), :]
            dos = [jnp.where(mq, dov, jnp.zeros((), BF16)) for mq in mine_q]
            lss = [lse_ref[pl.ds(q0, tq), e:e + 1] for e in range(2)]

            def sweep1(j, dls):
                r0 = pl.multiple_of(j * tq, tq)
                vv = v_ref[pl.ds(r0, tq), :]
                out = []
                for e, hl in enumerate(heads):
                    s = _dot_nt(qs[e], k_ref[pl.ds(r0, tq), hl])
                    s = jnp.where(jnp.logical_or(diag_ok, j < i), s, NEG_BIG)
                    p = jnp.exp(s - lss[e])
                    dp = _dot_nt(dos[e], vv)
                    p_buf[e, j] = p
                    dp_buf[e, j] = dp
                    dv_acc[pl.ds(r0, tq), :] += _dot_tn(p.astype(BF16), dos[e])
                    out.append(dls[e] + _fold_lanes(p * dp, jnp.add))
                return tuple(out)

            dls = lax.fori_loop(0, i + 1, sweep1, (jnp.zeros((tq, LANES), F32),) * 2)
            dls = [jnp.sum(d, axis=1, keepdims=True) for d in dls]

            def sweep2(j, dqs):
                r0 = pl.multiple_of(j * tq, tq)
                out = []
                for e, hl in enumerate(heads):
                    ds = p_buf[e, j] * (dp_buf[e, j] - dls[e])
                    dfr_ref[e, pl.ds(j, 1), :] += jnp.sum(ds, axis=0, keepdims=True)
                    dsb = ds.astype(BF16)
                    dk_acc[e, pl.ds(r0, tq), :] += _dot_tn(dsb, qs[e])
                    out.append(dqs[e] + _dot(dsb, k_ref[pl.ds(r0, tq), hl]))
                return tuple(out)

            dqs = lax.fori_loop(0, i + 1, sweep2, (jnp.zeros((tq, LANES), F32),) * 2)
            dq = jnp.where(mine_q[0], dqs[0], pltpu.roll(dqs[1], HEAD_DIM, 1)) * ATTN_SCALE
            dq_ref[pl.ds(q0, tq), :] = dq.astype(BF16)
            return 0

        lax.fori_loop(0, nq, q_step, 0)
        dk_ref[...] = jnp.where(lane_s < HEAD_DIM, dk_acc[0], pltpu.roll(dk_acc[1], HEAD_DIM, 1)).astype(BF16)
        dv_ref[...] = dv_acc[...].astype(BF16)

    wide = pl.BlockSpec((seq, 2 * LANES), lambda b, hp: (b, hp))
    col = pl.BlockSpec((seq, LANES), lambda b, hp: (b, hp))
    pair = pl.BlockSpec((None, seq, 2), lambda b, hp: (hp, b, 0))
    return pl.pallas_call(
        body, name="attn_bwd",
        out_shape=(jax.ShapeDtypeStruct((t, ATTN_WIDTH), BF16),) * 3 + (jax.ShapeDtypeStruct((N_HEADS, t // tq, tq), F32),),
        grid=(t // seq, hp_n),
        in_specs=[wide, wide, col, col, pair],
        out_specs=(col, col, col, pl.BlockSpec((2, nq, tq), lambda b, hp: (hp, b, 0))),
        scratch_shapes=[pltpu.VMEM((2, seq, LANES), F32), pltpu.VMEM((seq, LANES), F32),
                        pltpu.VMEM((2, nq, tq, tq), F32), pltpu.VMEM((2, nq, tq, tq), F32)],
        compiler_params=_cparams("parallel", "arbitrary"),
    )(qa, ka, v, do, lse4)


def _forget_bwd(dfc, fl, bf, seq):
    t = fl.shape[0]
    cb = min(256, seq)
    nb = seq // cb

    def body(dfc_ref, fl_ref, bf_ref, dfl_ref, db_ref):
        b = pl.program_id(0)
        ri = lax.broadcasted_iota(jnp.int32, (cb, cb), 0)
        ci = lax.broadcasted_iota(jnp.int32, (cb, cb), 1)
        tri = (ci >= ri).astype(BF16)
        carry = jnp.zeros((1, LANES), F32)
        dbs = jnp.zeros((1, LANES), F32)
        for blk in reversed(range(nb)):
            rs = slice(blk * cb, (blk + 1) * cb)
            dlf = _tri_dot(tri, -dfc_ref[rs, :]) + carry
            carry = dlf[0:1, :]
            dfl = dlf * _sigmoid(-(fl_ref[rs, :] + bf_ref[...]))
            dfl_ref[rs, :] = dfl.astype(BF16)
            dbs = dbs + jnp.sum(dfl, axis=0, keepdims=True)

        @pl.when(b == 0)
        def _():
            db_ref[...] = jnp.zeros_like(db_ref)

        db_ref[...] += dbs

    return pl.pallas_call(
        body, name="forget_bwd",
        out_shape=(jax.ShapeDtypeStruct((t, LANES), BF16), jax.ShapeDtypeStruct((1, LANES), F32)),
        grid=(t // seq,),
        in_specs=[pl.BlockSpec((seq, LANES), lambda b: (b, 0)), pl.BlockSpec((seq, LANES), lambda b: (b, 0)),
                  pl.BlockSpec((1, LANES), lambda b: (0, 0))],
        out_specs=(pl.BlockSpec((seq, LANES), lambda b: (b, 0)), pl.BlockSpec((1, LANES), lambda b: (0, 0))),
        compiler_params=_cparams("arbitrary"),
    )(dfc, fl, bf)


def _pool_bwd(dps, p, mix, scale, seq):
    t = dps.shape[0]

    def body(dps_ref, p_ref, mix_ref, sc_ref, du_ref, dmix_ref, dsc_ref):
        b = pl.program_id(0)

        @pl.when(b == 0)
        def _():
            dmix_ref[...] = jnp.zeros_like(dmix_ref)
            dsc_ref[...] = jnp.zeros_like(dsc_ref)

        tpos = lax.broadcasted_iota(jnp.int32, (seq, POOL_GROUP_DIM), 0)
        for g in range(POOL_GROUPS):
            sl = slice(g * POOL_GROUP_DIM, (g + 1) * POOL_GROUP_DIM)
            pb = p_ref[:, sl]
            dpsg = dps_ref[:, sl]
            pm = _dot(pb, mix_ref[g])
            dsc_ref[:, sl] += jnp.sum(dpsg * pm, axis=0, keepdims=True)
            dpm = (dpsg * sc_ref[:, sl]).astype(BF16)
            dmix_ref[g] += _dot_tn(pb, dpm)
            dp = _dot_nt(dpm, mix_ref[g])
            cnt = jnp.minimum(tpos + 1, POOL_WINDOWS[g]).astype(F32)
            s = dp / cnt
            for lvl in range(g + 1):
                d = 2 ** lvl
                s = s + jnp.where(tpos < seq - d, pltpu.roll(s, seq - d, 0), 0.0)
            du_ref[:, sl] = (s - dp).astype(BF16)

    return pl.pallas_call(
        body, name="pool_bwd",
        out_shape=(jax.ShapeDtypeStruct((t, POOL_WIDTH), BF16),
                   jax.ShapeDtypeStruct((POOL_GROUPS, POOL_GROUP_DIM, POOL_GROUP_DIM), F32),
                   jax.ShapeDtypeStruct((1, POOL_WIDTH), F32)),
        grid=(t // seq,),
        in_specs=[pl.BlockSpec((seq, POOL_WIDTH), lambda b: (b, 0)), pl.BlockSpec((seq, POOL_WIDTH), lambda b: (b, 0)),
                  pl.BlockSpec((POOL_GROUPS, POOL_GROUP_DIM, POOL_GROUP_DIM), lambda b: (0, 0, 0)),
                  pl.BlockSpec((1, POOL_WIDTH), lambda b: (0, 0))],
        out_specs=(pl.BlockSpec((seq, POOL_WIDTH), lambda b: (b, 0)),
                   pl.BlockSpec((POOL_GROUPS, POOL_GROUP_DIM, POOL_GROUP_DIM), lambda b: (0, 0, 0)),
                   pl.BlockSpec((1, POOL_WIDTH), lambda b: (0, 0))),
        compiler_params=_cparams("arbitrary"),
    )(dps, p, mix, scale)


def _in_bwd(du, dq, dk, dv, dg2, dfl, dx1, x, g, wu, wqkv, wg2, wft, tm):
    t, d = x.shape
    tm = min(tm, t)
    aw = ATTN_WIDTH

    def body(du_ref, dq_ref, dk_ref, dv_ref, dg2_ref, dfl_ref, dx1_ref, x_ref, g_ref, wu_ref, wqkv_ref, wg2_ref, wft_ref,
             dx_ref, dg_ref):
        i = pl.program_id(0)
        dh = _dot_nt(du_ref[...], wu_ref[...])
        dh += _dot_nt(dq_ref[...], wqkv_ref[:, 0:aw])
        dh += _dot_nt(dk_ref[...], wqkv_ref[:, aw:2 * aw])
        dh += _dot_nt(dv_ref[...], wqkv_ref[:, 2 * aw:3 * aw])
        dh += _dot_nt(dg2_ref[...], wg2_ref[...])
        dh += _dot(dfl_ref[...], wft_ref[...])
        dxn, dg = _rms_bwd(x_ref[...], g_ref[...], dh)
        dx_ref[...] = dx1_ref[...] + dxn

        @pl.when(i == 0)
        def _():
            dg_ref[...] = jnp.zeros_like(dg_ref)

        dg_ref[...] += dg

    row = lambda w: pl.BlockSpec((tm, w), lambda i: (i, 0))
    full = lambda a: pl.BlockSpec(a.shape, lambda i: (0, 0))
    return pl.pallas_call(
        body, name="in_bwd",
        out_shape=(jax.ShapeDtypeStruct((t, d), F32), jax.ShapeDtypeStruct((1, d), F32)),
        grid=(t // tm,),
        in_specs=[row(POOL_WIDTH), row(aw), row(aw), row(aw), row(2 * d), row(LANES), row(d), row(d),
                  pl.BlockSpec((1, d), lambda i: (0, 0)), full(wu), full(wqkv), full(wg2), full(wft)],
        out_specs=(row(d), pl.BlockSpec((1, d), lambda i: (0, 0))),
        compiler_params=_cparams("arbitrary"),
    )(du, dq, dk, dv, dg2, dfl, dx1, x, g, wu, wqkv, wg2, wft)


def _local_step(x, target, seq, norm1_g, w_in, b_forget, pool_mix, pool_scale, w_pool_out, w_attn_out, w_out,
                norm2_g, w_ffn_gate, w_ffn_up, w_ffn_down, norm_f_g):
    t, d = x.shape
    tq = min(256, seq)
    aw = ATTN_WIDTH
    o_q, o_f, o_g = POOL_WIDTH, POOL_WIDTH + 3 * aw, POOL_WIDTH + 3 * aw + N_HEADS
    wu = w_in[:, :o_q]
    wqkv = w_in[:, o_q:o_f]
    wf = jnp.pad(w_in[:, o_f:o_g], ((0, 0), (0, LANES - N_HEADS)))
    wg2 = w_in[:, o_g:]
    wft = wf.T
    bf = jnp.pad(b_forget, ((0, 0), (0, LANES - N_HEADS)))
    mixb = pool_mix.astype(BF16)

    h = _norm_fwd("norm1_fwd", x, norm1_g, 512)
    u = _matmul("mm_u", h, wu, "nn", F32, 1024, 512, d)
    g2 = _matmul("mm_gates", h, wg2, "nn", F32, 1024, 512, d)
    fl, fcum = _forget_fwd(h, wf, bf, seq)
    qa, ka, v = _attn_prep(h, _head_blocks(wqkv[:, :aw]), _head_blocks(wqkv[:, aw:2 * aw]), wqkv[:, 2 * aw:], fcum, 512)
    p, ps = _pool_fwd(u, mixb, pool_scale, seq)
    o, lse = _attn_fwd(qa, ka, v, seq, tq)
    merged, x1 = _merge_fwd(x, ps, o, g2, w_pool_out, w_attn_out, w_out, 256)
    h2, gt, up, act, x2 = _ffn_fwd(x1, norm2_g, w_ffn_gate, w_ffn_up, w_ffn_down, 512, 256)
    loss, dx2, d_gf = _final_fwd_bwd(x2, target, norm_f_g, 512)

    dgt, dup, dx1, d_g2n = _ffn_bwd(dx2, x1, norm2_g, gt, up, w_ffn_gate, w_ffn_up, w_ffn_down, 512, 256)
    d_wd = _matmul("dw_down", act, dx2, "tn", F32, 1408, 1024, 512)
    d_wg = _matmul("dw_gate", h2, dgt, "tn", F32, 1024, 1408, 512)
    d_wu = _matmul("dw_up", h2, dup, "tn", F32, 1024, 1408, 512)
    dpy, day, dg2, dps, da = _merge_bwd(dx1, ps, o, g2, w_pool_out, w_attn_out, w_out, 256)
    d_wout = _matmul("dw_out", merged, dx1, "tn", F32, 1024, 1024, 512)
    d_wpo = _matmul("dw_pool_out", ps, dpy, "tn", F32, 512, 1024, 512)
    d_wao = _matmul("dw_attn_out", o, day, "tn", F32, 512, 1024, 512)
    dq, dk, dv, dfr = _attn_bwd(qa, ka, v, da, lse, seq, tq)
    dfc = jnp.pad(dfr.reshape(N_HEADS, t).T, ((0, 0), (0, LANES - N_HEADS)))
    dfl, d_bf = _forget_bwd(dfc, fl, bf, seq)
    du, d_mix, d_scale = _pool_bwd(dps, p, mixb, pool_scale, seq)
    dx, d_g1 = _in_bwd(du, dq, dk, dv, dg2, dfl, dx1, x, norm1_g, wu, wqkv, wg2, wft, 256)
    d_wu_in = _matmul("dw_in_u", h, du, "tn", F32, 1024, 512, 512)
    d_wq = _matmul("dw_in_q", h, dq, "tn", F32, 1024, 512, 512)
    d_wk = _matmul("dw_in_k", h, dk, "tn", F32, 1024, 512, 512)
    d_wv = _matmul("dw_in_v", h, dv, "tn", F32, 1024, 512, 512)
    d_wf = _matmul("dw_in_f", h, dfl, "tn", F32, 1024, LANES, 512)
    d_wg2 = _matmul("dw_in_gates", h, dg2, "tn", F32, 1024, 1024, 512)
    d_win = jnp.concatenate([d_wu_in, d_wq, d_wk, d_wv, d_wf[:, :N_HEADS], d_wg2], axis=1)

    big = (d_win, d_wpo, d_wao, d_wout, d_wg, d_wu, d_wd)
    small = (d_g1, d_bf[:, :N_HEADS], d_mix, d_scale, d_g2n, d_gf)
    return loss, dx, big, small


def _position():
    return lax.axis_index("x"), lax.axis_index("y"), lax.axis_index("c")


def _remote(src, dst, send_sem, recv_sem, device):
    return pltpu.make_async_remote_copy(src_ref=src, dst_ref=dst, send_sem=send_sem, recv_sem=recv_sem,
                                        device_id=device, device_id_type=MESH)


def _gather_weights(shards):
    n = len(shards)

    def body(*refs):
        w, g = refs[:n], refs[n:2 * n]
        send, recv = refs[2 * n:]
        x, y, c = _position()
        me = 2 * x + y
        sibling = (x, y, 1 - c)
        chips = [(1 - x, y), (x, 1 - y), (1 - x, 1 - y)]
        sends = []
        for k in range(n):
            for j, (ox, oy) in enumerate(chips):
                cp = _remote(w[k].at[c], g[k].at[me, c], send.at[6 * k + j], recv.at[6 * k + j], (ox, oy, c))
                cp.start()
                sends.append(cp)
        for k in range(n):
            for j, (ox, oy) in enumerate(chips):
                blk = g[k].at[2 * ox + oy, c]
                _remote(blk, blk, send.at[6 * k + j], recv.at[6 * k + j], (ox, oy, c)).wait_recv()
                cp = _remote(blk, blk, send.at[6 * k + 3 + j], recv.at[6 * k + 3 + j], sibling)
                cp.start()
                sends.append(cp)
        for k in range(n):
            for j, (ox, oy) in enumerate(chips):
                blk = g[k].at[2 * ox + oy, 1 - c]
                _remote(blk, blk, send.at[6 * k + 3 + j], recv.at[6 * k + 3 + j], sibling).wait_recv()
        for cp in sends:
            cp.wait_send()

    return pl.pallas_call(
        body, name="gather_weights",
        out_shape=tuple(jax.ShapeDtypeStruct((N_CHIPS,) + s.shape, s.dtype) for s in shards),
        in_specs=[ANY] * n, out_specs=tuple([ANY] * n),
        scratch_shapes=[pltpu.SemaphoreType.DMA((6 * n,)), pltpu.SemaphoreType.DMA((6 * n,))],
        compiler_params=pltpu.CompilerParams(has_side_effects=True),
    )(*shards)


def _swap(name, gives, peer_fn, give_fn, recv_shapes):
    n = len(gives)

    def body(*refs):
        give, recv = refs[:n], refs[n:2 * n]
        send_sem, recv_sem = refs[2 * n:]
        pos = _position()
        copies = [_remote(give_fn(give[k], *pos), recv[k], send_sem.at[k], recv_sem.at[k], peer_fn(*pos))
                  for k in range(n)]
        for cp in copies:
            cp.start()
        for cp in copies:
            cp.wait()

    return pl.pallas_call(
        body, name=name,
        out_shape=tuple(jax.ShapeDtypeStruct(s, g.dtype) for s, g in zip(recv_shapes, gives)),
        in_specs=[ANY] * n, out_specs=tuple([ANY] * n),
        scratch_shapes=[pltpu.SemaphoreType.DMA((n,)), pltpu.SemaphoreType.DMA((n,))],
        compiler_params=pltpu.CompilerParams(has_side_effects=True),
    )(*gives)


def _add_keep_give(name, pos, a, a_keep, a_give, b, b_keep, b_give, steps):
    r, c = a.shape[-2:]

    def spec(arr, fn):
        lead = arr.ndim - 2
        return pl.BlockSpec((None,) * lead + (r, c), lambda i, p: tuple(fn(i, p)) + (0, 0))

    out_spec = pl.BlockSpec((None, r, c), lambda i, p: (i, 0, 0))

    def body(p_ref, ak_ref, bk_ref, ag_ref, bg_ref, keep_ref, give_ref):
        keep_ref[...] = ak_ref[...] + bk_ref[...].astype(F32)
        give_ref[...] = (ag_ref[...] + bg_ref[...].astype(F32)).astype(BF16)

    return pl.pallas_call(
        body, name=name,
        out_shape=(jax.ShapeDtypeStruct((steps, r, c), F32), jax.ShapeDtypeStruct((steps, r, c), BF16)),
        grid_spec=pltpu.PrefetchScalarGridSpec(
            num_scalar_prefetch=1, grid=(steps,),
            in_specs=[spec(a, a_keep), spec(b, b_keep), spec(a, a_give), spec(b, b_give)],
            out_specs=(out_spec, out_spec)),
        compiler_params=_cparams("parallel"),
    )(pos, a, b, a, b)


def _add_last(name, a, b):
    _, r, c = a.shape
    blk = pl.BlockSpec((None, r, c), lambda i: (0, 0, 0))

    def body(a_ref, b_ref, o_ref):
        o_ref[...] = a_ref[...] + b_ref[...].astype(F32)

    return pl.pallas_call(
        body, name=name, out_shape=jax.ShapeDtypeStruct((r, c), F32), grid=(1,), in_specs=[blk, blk],
        out_specs=pl.BlockSpec((r, c), lambda i: (0, 0)), compiler_params=_cparams("arbitrary"),
    )(a, b)


def _reduce_scatter(grads):
    n = len(grads)
    shp = [g.shape[2:] for g in grads]
    x, y, c = _position()
    pos_cx = jnp.stack([c, x]).astype(jnp.int32)
    pos_y = jnp.stack([y]).astype(jnp.int32)
    recv = _swap("rs_c", grads, lambda x, y, c: (x, y, 1 - c), lambda ref, x, y, c: ref.at[1 - c],
                 [(N_CHIPS,) + s for s in shp])
    p1 = [_add_keep_give(
        f"rs_c_add{k}", pos_cx,
        grads[k], lambda i, p: (p[0], 2 * p[1] + i), lambda i, p: (p[0], 2 * (1 - p[1]) + i),
        recv[k], lambda i, p: (2 * p[1] + i,), lambda i, p: (2 * (1 - p[1]) + i,), 2) for k in range(n)]
    recv = _swap("rs_x", [p[1] for p in p1], lambda x, y, c: (1 - x, y, c), lambda ref, x, y, c: ref,
                 [(2,) + s for s in shp])
    p2 = [_add_keep_give(
        f"rs_x_add{k}", pos_y,
        p1[k][0], lambda i, p: (p[0],), lambda i, p: (1 - p[0],),
        recv[k], lambda i, p: (p[0],), lambda i, p: (1 - p[0],), 1) for k in range(n)]
    recv = _swap("rs_y", [p[1] for p in p2], lambda x, y, c: (x, 1 - y, c), lambda ref, x, y, c: ref,
                 [(1,) + s for s in shp])
    mine = [_add_last(f"rs_y_add{k}", p2[k][0], recv[k]) for k in range(n)]
    other = _swap("rs_swap_halves", mine, lambda x, y, c: (x, y, 1 - c), lambda ref, x, y, c: ref, shp)
    return list(zip(mine, other))


def _all_reduce_small(v):
    r = v.shape[0]

    def body(v_ref, out_ref, buf, send_sems, recv_sems, local_sem):
        x, y, c = _position()
        me, sibling = (x, y, c), (x, y, 1 - c)
        chips = [(1 - x, y), (x, 1 - y), (1 - x, 1 - y)]

        def rows(px, py, pc):
            return buf.at[pl.ds((4 * px + 2 * py + pc) * r, r), :]

        def copy(k, block, to, src=None):
            return _remote(rows(*block) if src is None else src, rows(*block), send_sems.at[k], recv_sems.at[k], to)

        mine = pltpu.make_async_copy(v_ref, rows(*me), local_sem)
        mine.start()
        first = [copy(0, me, sibling, src=v_ref)]
        first += [copy(1 + j, me, (*chip, c), src=v_ref) for j, chip in enumerate(chips)]
        for cp in first:
            cp.start()
        passed = [copy(4 + j, (*chip, c), sibling) for j, chip in enumerate(chips)]
        for j, chip in enumerate(chips):
            copy(1 + j, (*chip, c), me).wait_recv()
            passed[j].start()
        copy(0, sibling, me).wait_recv()
        for j, chip in enumerate(chips):
            copy(4 + j, (*chip, 1 - c), me).wait_recv()
        for cp in first + passed:
            cp.wait_send()
        mine.wait()
        acc = buf[0:r, :]
        for dev in range(1, N_DEV):
            acc = acc + buf[dev * r:(dev + 1) * r, :]
        out_ref[...] = acc

    return pl.pallas_call(
        body, name="all_reduce_small",
        out_shape=jax.ShapeDtypeStruct(v.shape, F32),
        in_specs=[pl.BlockSpec(memory_space=pltpu.VMEM)],
        out_specs=pl.BlockSpec(memory_space=pltpu.VMEM),
        scratch_shapes=[pltpu.VMEM((N_DEV * r, LANES), F32), pltpu.SemaphoreType.DMA((7,)),
                        pltpu.SemaphoreType.DMA((7,)), pltpu.SemaphoreType.DMA],
        compiler_params=pltpu.CompilerParams(has_side_effects=True, vmem_limit_bytes=VMEM_LIMIT_V7X),
    )(v)


def _adamw_update(w, gg, m, v):
    mn = ADAM_B1 * m + (1.0 - ADAM_B1) * gg
    vn = ADAM_B2 * v + (1.0 - ADAM_B2) * (gg * gg)
    m_hat = mn / (1.0 - ADAM_B1 ** ADAM_STEP)
    v_hat = vn / (1.0 - ADAM_B2 ** ADAM_STEP)
    return -ADAM_LR * (m_hat / (jnp.sqrt(v_hat) + ADAM_EPS) + ADAM_WD * w), mn, vn


def _adamw(name, w, g, m, v):
    def body(w_ref, g_ref, m_ref, v_ref, d_ref, mo_ref, vo_ref):
        d_ref[...], mo_ref[...], vo_ref[...] = _adamw_update(w_ref[...], g_ref[...], m_ref[...], v_ref[...])

    blk = pl.BlockSpec(w.shape, lambda i: (0, 0))
    return pl.pallas_call(
        body, name=name, out_shape=(jax.ShapeDtypeStruct(w.shape, F32),) * 3, grid=(1,),
        in_specs=[blk] * 4, out_specs=(blk,) * 3, compiler_params=_cparams("arbitrary"),
    )(w, g, m, v)


def _adamw_halves(name, pos_c, w, g_mine, g_other, m, v, tr):
    r, c = w.shape
    rh = r // 2
    tr = tr if rh % tr == 0 else rh
    nt = rh // tr

    def body(p_ref, w_ref, gm_ref, go_ref, m_ref, v_ref, g_ref, d_ref, mo_ref, vo_ref):
        gg = jnp.where(pl.program_id(0) == p_ref[0], gm_ref[...], go_ref[...])
        g_ref[...] = gg
        d_ref[...], mo_ref[...], vo_ref[...] = _adamw_update(w_ref[...], gg, m_ref[...], v_ref[...])

    full = pl.BlockSpec((tr, c), lambda h, i, p: (h * nt + i, 0))
    half = pl.BlockSpec((tr, c), lambda h, i, p: (i, 0))
    return pl.pallas_call(
        body, name=name, out_shape=(jax.ShapeDtypeStruct((r, c), F32),) * 4,
        grid_spec=pltpu.PrefetchScalarGridSpec(
            num_scalar_prefetch=1, grid=(2, nt),
            in_specs=[full, half, half, full, full], out_specs=(full,) * 4),
        compiler_params=_cparams("parallel", "parallel"),
    )(pos_c, w, g_mine, g_other, m, v)


def _col_sharded_to_comm(g):
    k, n = g.shape
    return g.reshape(2, k // 2, N_CHIPS, n // N_CHIPS).transpose(0, 2, 1, 3)


def _row_sharded_to_comm(g):
    r, c = g.shape
    return g.reshape(N_CHIPS, 2, r // (2 * N_CHIPS), c).transpose(1, 0, 2, 3)


def _pack_small(g1, bfv, mix, scale, g2n, gf, extra=None):
    row8 = jnp.pad(bfv.reshape(1, N_HEADS), ((0, 0), (0, LANES - N_HEADS)))
    if extra is not None:
        row8 = row8 + jnp.pad(extra[:, :1], ((0, 0), (N_HEADS, LANES - N_HEADS - 1)))
    return jnp.concatenate([
        g1.reshape(8, LANES), jnp.pad(row8, ((0, 7), (0, 0))), mix.reshape(512, LANES),
        jnp.pad(scale.reshape(4, LANES), ((0, 4), (0, 0))), g2n.reshape(8, LANES), gf.reshape(8, LANES)], axis=0)


def _unpack_small(s, like):
    g1, bfv, mix, scale, g2n, gf = like
    return (s[0:8].reshape(g1.shape), s[8, :N_HEADS].reshape(bfv.shape), s[16:528].reshape(mix.shape),
            s[528:532].reshape(scale.shape), s[536:544].reshape(g2n.shape), s[544:552].reshape(gf.shape))


def kernel(x, norm1_g, w_in, b_forget, pool_mix, pool_scale, w_pool_out, w_attn_out, w_out, norm2_g, w_ffn_gate, w_ffn_up, w_ffn_down, norm_f_g, loss_target, m_norm1_g, m_w_in, m_b_forget, m_pool_mix, m_pool_scale, m_w_pool_out, m_w_attn_out, m_w_out, m_norm2_g, m_w_ffn_gate, m_w_ffn_up, m_w_ffn_down, m_norm_f_g, v_norm1_g, v_w_in, v_b_forget, v_pool_mix, v_pool_scale, v_w_pool_out, v_w_attn_out, v_w_out, v_norm2_g, v_w_ffn_gate, v_w_ffn_up, v_w_ffn_down, v_norm_f_g):
    nb, seq, d = x.shape
    big_w = (w_in, w_pool_out, w_attn_out, w_out, w_ffn_gate, w_ffn_up, w_ffn_down)
    big_m = (m_w_in, m_w_pool_out, m_w_attn_out, m_w_out, m_w_ffn_gate, m_w_ffn_up, m_w_ffn_down)
    big_v = (v_w_in, v_w_pool_out, v_w_attn_out, v_w_out, v_w_ffn_gate, v_w_ffn_up, v_w_ffn_down)
    row_sharded = (False, False, False, True, False, False, True)
    small_w = (norm1_g, b_forget, pool_mix, pool_scale, norm2_g, norm_f_g)
    small_m = (m_norm1_g, m_b_forget, m_pool_mix, m_pool_scale, m_norm2_g, m_norm_f_g)
    small_v = (v_norm1_g, v_b_forget, v_pool_mix, v_pool_scale, v_norm2_g, v_norm_f_g)

    me = 2 * lax.axis_index("x") + lax.axis_index("y")
    local = [w[0].astype(BF16).reshape(2, w.shape[1] // 2, w.shape[2]) for w in big_w]
    gathered = _gather_weights(local)
    full = []
    for gw, lw, w, rs in zip(gathered, local, big_w, row_sharded):
        r, c = w.shape[1:]
        gw = lax.dynamic_update_index_in_dim(gw, lw, me, 0).reshape(N_CHIPS, r, c)
        full.append(gw.reshape(N_CHIPS * r, c) if rs else gw.transpose(1, 0, 2).reshape(r, N_CHIPS * c))

    loss, dx, big_g, small_g = _local_step(
        x.reshape(nb * seq, d), loss_target.reshape(nb * seq, d), seq,
        norm1_g, full[0], b_forget, pool_mix[0], pool_scale, full[1], full[2], full[3], norm2_g,
        full[4], full[5], full[6], norm_f_g.reshape(1, d))

    comm = [_row_sharded_to_comm(g) if rs else _col_sharded_to_comm(g) for g, rs in zip(big_g, row_sharded)]
    reduced = _reduce_scatter(comm)
    small_sum = _all_reduce_small(_pack_small(*small_g, extra=loss))
    loss_out = small_sum[8, N_HEADS]

    grads, deltas, new_m, new_v = [None] * 13, [None] * 13, [None] * 13, [None] * 13
    big_pos = (1, 5, 6, 7, 9, 10, 11)
    small_pos = (0, 2, 3, 4, 8, 12)
    pos_c = jnp.stack([lax.axis_index("c")]).astype(jnp.int32)
    for k, pos in enumerate(big_pos):
        w = big_w[k]
        outs = _adamw_halves(f"adamw_big{k}", pos_c, w[0], reduced[k][0], reduced[k][1], big_m[k][0], big_v[k][0], 256)
        grads[pos], deltas[pos], new_m[pos], new_v[pos] = (a.reshape(w.shape) for a in outs)
    dl, mn, vn = _adamw("adamw_small", _pack_small(*small_w), small_sum * _small_mask(), _pack_small(*small_m),
                        _pack_small(*small_v))
    for pos, g, a, b, e in zip(small_pos, _unpack_small(small_sum, small_w), _unpack_small(dl, small_w),
                               _unpack_small(mn, small_w), _unpack_small(vn, small_w)):
        grads[pos], deltas[pos], new_m[pos], new_v[pos] = g, a, b, e

    return (loss_out, dx.reshape(nb, seq, d), *grads, *deltas, *new_m, *new_v)


def _small_mask():
    rows = lax.broadcasted_iota(jnp.int32, (552, LANES), 0)
    lanes = lax.broadcasted_iota(jnp.int32, (552, LANES), 1)
    return jnp.where(jnp.logical_and(rows == 8, lanes == N_HEADS), 0.0, 1.0).astype(F32)
```

```python
import functools

import jax
import jax.numpy as jnp
from jax import lax
from jax.experimental import pallas as pl
from jax.experimental.pallas import tpu as pltpu

F32 = jnp.float32
BF16 = jnp.bfloat16

D_MODEL = 1024
POOL_WINDOWS = (2, 4, 8, 16)
POOL_GROUPS = 4
POOL_GROUP_DIM = 128
POOL_WIDTH = 512
HEAD_DIM = 64
N_HEADS = 8
ATTN_WIDTH = 512
D_FF = 2816
RMS_EPS = 1e-6
ATTN_SCALE = HEAD_DIM ** -0.5
NEG_BIG = -1e30

ADAM_LR = 0.001
ADAM_B1 = 0.9
ADAM_B2 = 0.999
ADAM_EPS = 1e-08
ADAM_WD = 0.01
ADAM_STEP = 10

LANES = 128
N_CHIPS = 4
N_DEV = 8
VMEM_LIMIT_V7X = 52 * 1024 * 1024
MESH = pl.DeviceIdType.MESH
ANY = pl.BlockSpec(memory_space=pl.ANY)


def _cparams(*sem):
    return pltpu.CompilerParams(dimension_semantics=sem if sem else None, vmem_limit_bytes=VMEM_LIMIT_V7X)


def _dot(a, b):
    return lax.dot_general(a, b, (((1,), (0,)), ((), ())), preferred_element_type=F32)


def _dot_nt(a, b):
    return lax.dot_general(a, b, (((1,), (1,)), ((), ())), preferred_element_type=F32)


def _dot_tn(a, b):
    return lax.dot_general(a, b, (((0,), (0,)), ((), ())), preferred_element_type=F32)


def _sigmoid(x):
    return jax.nn.sigmoid(x)


def _rms_fwd(x, g):
    r = lax.rsqrt(jnp.mean(x * x, axis=-1, keepdims=True) + RMS_EPS)
    return (x * r) * g


def _rms_bwd(x, g, dy):
    r = lax.rsqrt(jnp.mean(x * x, axis=-1, keepdims=True) + RMS_EPS)
    xh = x * r
    dg = jnp.sum(dy * xh, axis=0, keepdims=True)
    dxh = dy * g
    dx = r * (dxh - xh * jnp.mean(dxh * xh, axis=-1, keepdims=True))
    return dx, dg


def _matmul(name, a, b, mode, out_dtype, tm, tn, tk):
    if mode == "nn":
        (m, k), (_, n) = a.shape, b.shape
    elif mode == "nt":
        (m, k), (n, _) = a.shape, b.shape
    else:
        (k, m), (_, n) = a.shape, b.shape
    tm, tn, tk = min(tm, m), min(tn, n), min(tk, k)
    assert m % tm == 0 and n % tn == 0 and k % tk == 0, (name, m, n, k, tm, tn, tk)
    nk = k // tk
    if mode == "tn":
        a_spec = pl.BlockSpec((tk, tm), lambda i, j, kk: (kk, i))
    else:
        a_spec = pl.BlockSpec((tm, tk), lambda i, j, kk: (i, kk))
    if mode == "nt":
        b_spec = pl.BlockSpec((tn, tk), lambda i, j, kk: (j, kk))
    else:
        b_spec = pl.BlockSpec((tk, tn), lambda i, j, kk: (kk, j))
    dot = {"nn": _dot, "nt": _dot_nt, "tn": _dot_tn}[mode]
    use_scratch = nk > 1 and out_dtype != F32

    def body(a_ref, b_ref, o_ref, *scratch):
        prod = dot(a_ref[...].astype(BF16), b_ref[...].astype(BF16))
        if nk == 1:
            o_ref[...] = prod.astype(out_dtype)
            return
        acc = scratch[0] if use_scratch else o_ref
        kk = pl.program_id(2)

        @pl.when(kk == 0)
        def _():
            acc[...] = prod

        @pl.when(kk > 0)
        def _():
            acc[...] += prod

        if use_scratch:
            @pl.when(kk == nk - 1)
            def _():
                o_ref[...] = acc[...].astype(out_dtype)

    return pl.pallas_call(
        body,
        name=name,
        out_shape=jax.ShapeDtypeStruct((m, n), out_dtype),
        grid=(m // tm, n // tn, nk),
        in_specs=[a_spec, b_spec],
        out_specs=pl.BlockSpec((tm, tn), lambda i, j, kk: (i, j)),
        scratch_shapes=[pltpu.VMEM((tm, tn), F32)] if use_scratch else [],
        compiler_params=_cparams("parallel", "parallel", "arbitrary"),
    )(a, b)


def _norm_fwd(name, x, g, tm):
    t, d = x.shape
    tm = min(tm, t)

    def body(x_ref, g_ref, h_ref):
        h_ref[...] = _rms_fwd(x_ref[...], g_ref[...]).astype(BF16)

    return pl.pallas_call(
        body, name=name, out_shape=jax.ShapeDtypeStruct((t, d), BF16), grid=(t // tm,),
        in_specs=[pl.BlockSpec((tm, d), lambda i: (i, 0)), pl.BlockSpec((1, d), lambda i: (0, 0))],
        out_specs=pl.BlockSpec((tm, d), lambda i: (i, 0)),
        compiler_params=_cparams("parallel"),
    )(x, g)


def _split3(x):
    hi = x.astype(BF16)
    r1 = x - hi.astype(F32)
    mid = r1.astype(BF16)
    lo = (r1 - mid.astype(F32)).astype(BF16)
    return hi, mid, lo


def _tri_dot(tri, x):
    hi, mid, lo = _split3(x)
    return _dot(tri, hi) + _dot(tri, mid) + _dot(tri, lo)


def _forget_fwd(h, wf, bf, seq):
    t, d = h.shape
    cb = min(256, seq)

    def body(h_ref, wf_ref, bf_ref, fl_ref, fc_ref):
        fl = _dot(h_ref[...], wf_ref[...])
        fl_ref[...] = fl
        xx = fl + bf_ref[...]
        lf = jnp.minimum(xx, 0.0) - jnp.log(1.0 + jnp.exp(-jnp.abs(xx)))
        ri = lax.broadcasted_iota(jnp.int32, (cb, cb), 0)
        ci = lax.broadcasted_iota(jnp.int32, (cb, cb), 1)
        tri = (ri >= ci).astype(BF16)
        carry = jnp.zeros((1, LANES), F32)
        for blk in range(seq // cb):
            cs = _tri_dot(tri, lf[blk * cb:(blk + 1) * cb]) + carry
            fc_ref[blk * cb:(blk + 1) * cb, :] = cs
            carry = cs[cb - 1:cb, :]

    return pl.pallas_call(
        body, name="forget_fwd",
        out_shape=(jax.ShapeDtypeStruct((t, LANES), F32), jax.ShapeDtypeStruct((t, LANES), F32)),
        grid=(t // seq,),
        in_specs=[pl.BlockSpec((seq, d), lambda b: (b, 0)), pl.BlockSpec((d, LANES), lambda b: (0, 0)),
                  pl.BlockSpec((1, LANES), lambda b: (0, 0))],
        out_specs=(pl.BlockSpec((seq, LANES), lambda b: (b, 0)), pl.BlockSpec((seq, LANES), lambda b: (b, 0))),
        compiler_params=_cparams("parallel"),
    )(h, wf, bf)


def _pool_fwd(u, mix, scale, seq):
    t = u.shape[0]

    def body(u_ref, mix_ref, sc_ref, p_ref, ps_ref):
        tpos = lax.broadcasted_iota(jnp.int32, (seq, POOL_GROUP_DIM), 0)
        for g in range(POOL_GROUPS):
            sl = slice(g * POOL_GROUP_DIM, (g + 1) * POOL_GROUP_DIM)
            ug = u_ref[:, sl]
            s = ug
            for lvl in range(g + 1):
                d = 2 ** lvl
                s = s + jnp.where(tpos >= d, pltpu.roll(s, d, 0), 0.0)
            cnt = jnp.minimum(tpos + 1, POOL_WINDOWS[g]).astype(F32)
            pb = (s / cnt - ug).astype(BF16)
            p_ref[:, sl] = pb
            ps_ref[:, sl] = (_dot(pb, mix_ref[g]) * sc_ref[:, sl]).astype(BF16)

    return pl.pallas_call(
        body, name="pool_fwd",
        out_shape=(jax.ShapeDtypeStruct((t, POOL_WIDTH), BF16), jax.ShapeDtypeStruct((t, POOL_WIDTH), BF16)),
        grid=(t // seq,),
        in_specs=[pl.BlockSpec((seq, POOL_WIDTH), lambda b: (b, 0)),
                  pl.BlockSpec((POOL_GROUPS, POOL_GROUP_DIM, POOL_GROUP_DIM), lambda b: (0, 0, 0)),
                  pl.BlockSpec((1, POOL_WIDTH), lambda b: (0, 0))],
        out_specs=(pl.BlockSpec((seq, POOL_WIDTH), lambda b: (b, 0)), pl.BlockSpec((seq, POOL_WIDTH), lambda b: (b, 0))),
        compiler_params=_cparams("parallel"),
    )(u, mix, scale)


def _aug_constants():
    w = N_HEADS * LANES
    rows = jnp.arange(3 * LANES)
    piece, head = rows // LANES, rows % LANES
    cols = jnp.arange(w)
    live = (head < N_HEADS)[:, None]
    pq = (live & (cols[None, :] == (head * LANES + HEAD_DIM + piece)[:, None])).astype(BF16)
    pk = -(live & (cols[None, :] == (head * LANES + HEAD_DIM + 3 + piece)[:, None])).astype(BF16)
    lane = cols % LANES
    oq = ((lane >= HEAD_DIM + 3) & (lane < HEAD_DIM + 6)).astype(F32)[None, :]
    ok = ((lane >= HEAD_DIM) & (lane < HEAD_DIM + 3)).astype(F32)[None, :]
    return pq, pk, oq, ok


def _head_blocks(w):
    d = w.shape[0]
    return jnp.pad(w.reshape(d, N_HEADS, HEAD_DIM), ((0, 0), (0, 0), (0, LANES - HEAD_DIM))).reshape(d, N_HEADS * LANES)


def _attn_prep(h, wq, wk, wv, fcum, tm):
    t, d = h.shape
    tm = min(tm, t)
    w = N_HEADS * LANES
    pq, pk, oq, ok = _aug_constants()

    def body(h_ref, wq_ref, wk_ref, wv_ref, f_ref, pq_ref, pk_ref, oq_ref, ok_ref, qa_ref, ka_ref, v_ref):
        hh = h_ref[...]
        fs = jnp.concatenate(_split3(f_ref[...]), axis=1)
        q = _dot(hh, wq_ref[...]).astype(BF16).astype(F32) * ATTN_SCALE
        qa_ref[...] = (q + _dot(fs, pq_ref[...]) + oq_ref[...]).astype(BF16)
        k = _dot(hh, wk_ref[...]).astype(BF16).astype(F32)
        ka_ref[...] = (k + _dot(fs, pk_ref[...]) + ok_ref[...]).astype(BF16)
        v_ref[...] = _dot(hh, wv_ref[...]).astype(BF16)

    row = lambda n: pl.BlockSpec((tm, n), lambda i: (i, 0))
    full = lambda a: pl.BlockSpec(a.shape, lambda i: (0, 0))
    return pl.pallas_call(
        body, name="attn_prep",
        out_shape=(jax.ShapeDtypeStruct((t, w), BF16), jax.ShapeDtypeStruct((t, w), BF16),
                   jax.ShapeDtypeStruct((t, ATTN_WIDTH), BF16)),
        grid=(t // tm,),
        in_specs=[row(d), full(wq), full(wk), full(wv), row(LANES), full(pq), full(pk), full(oq), full(ok)],
        out_specs=(row(w), row(w), row(ATTN_WIDTH)),
        compiler_params=_cparams("parallel"),
    )(h, wq, wk, wv, fcum, pq, pk, oq, ok)


def _fold_lanes(x, op):
    out = x[:, :LANES]
    for g in range(1, x.shape[1] // LANES):
        out = op(out, x[:, g * LANES:(g + 1) * LANES])
    return out


def _attn_fwd(qa, ka, v, seq, tq):
    t = qa.shape[0]
    nq = seq // tq
    hp_n = N_HEADS // 2
    heads = [slice(e * LANES, (e + 1) * LANES) for e in range(2)]

    def body(q_ref, k_ref, v_ref, o_ref, lse_ref, s_buf):
        i = pl.program_id(2)
        diag_ok = lax.broadcasted_iota(jnp.int32, (tq, tq), 0) >= lax.broadcasted_iota(jnp.int32, (tq, tq), 1)
        qs = [q_ref[:, hl] for hl in heads]

        def sweep1(j, mxs):
            r0 = pl.multiple_of(j * tq, tq)
            out = []
            for e, hl in enumerate(heads):
                s = _dot_nt(qs[e], k_ref[pl.ds(r0, tq), hl])
                s = jnp.where(jnp.logical_or(diag_ok, j < i), s, NEG_BIG)
                s_buf[e, j] = s
                out.append(jnp.maximum(mxs[e], _fold_lanes(s, jnp.maximum)))
            return tuple(out)

        mxs = lax.fori_loop(0, i + 1, sweep1, (jnp.full((tq, LANES), NEG_BIG, F32),) * 2)
        ms = [jnp.max(mx, axis=1, keepdims=True) for mx in mxs]

        def sweep2(j, carry):
            r0 = pl.multiple_of(j * tq, tq)
            vv = v_ref[pl.ds(r0, tq), :]
            out = []
            for e in range(2):
                p = jnp.exp(s_buf[e, j] - ms[e])
                out += [carry[2 * e] + _fold_lanes(p, jnp.add), carry[2 * e + 1] + _dot(p.astype(BF16), vv)]
            return tuple(out)

        res = lax.fori_loop(0, i + 1, sweep2, (jnp.zeros((tq, LANES), F32),) * 4)
        outs = []
        for e in range(2):
            l = jnp.sum(res[2 * e], axis=1, keepdims=True)
            outs.append(res[2 * e + 1] / l)
            lse_ref[:, e:e + 1] = ms[e] + jnp.log(l)
        lane = lax.broadcasted_iota(jnp.int32, (tq, LANES), 1)
        o_ref[...] = jnp.where(lane < HEAD_DIM, outs[0], outs[1])

    return pl.pallas_call(
        body, name="attn_fwd",
        out_shape=(jax.ShapeDtypeStruct((t, ATTN_WIDTH), F32), jax.ShapeDtypeStruct((hp_n, t, 2), F32)),
        grid=(t // seq, hp_n, nq),
        in_specs=[pl.BlockSpec((tq, 2 * LANES), lambda b, hp, i: (b * nq + i, hp)),
                  pl.BlockSpec((seq, 2 * LANES), lambda b, hp, i: (b, hp)),
                  pl.BlockSpec((seq, LANES), lambda b, hp, i: (b, hp))],
        out_specs=(pl.BlockSpec((tq, LANES), lambda b, hp, i: (b * nq + i, hp)),
                   pl.BlockSpec((None, tq, 2), lambda b, hp, i: (hp, b * nq + i, 0))),
        scratch_shapes=[pltpu.VMEM((2, nq, tq, tq), F32)],
        compiler_params=_cparams("parallel", "parallel", "arbitrary"),
    )(qa, ka, v)


def _merge_fwd(x, ps, o, g2, wpo, wao, wout, tm):
    t, d = x.shape
    tm = min(tm, t)

    def body(x_ref, ps_ref, o_ref, gp_ref, ga_ref, wpo_ref, wao_ref, wout_ref, mg_ref, x1_ref):
        py = _dot(ps_ref[...], wpo_ref[...])
        ay = _dot(o_ref[...].astype(BF16), wao_ref[...])
        mb = (_sigmoid(gp_ref[...]) * py + _sigmoid(ga_ref[...]) * ay).astype(BF16)
        mg_ref[...] = mb
        x1_ref[...] = x_ref[...] + _dot(mb, wout_ref[...])

    row = lambda w: pl.BlockSpec((tm, w), lambda i: (i, 0))
    full = lambda a: pl.BlockSpec(a.shape, lambda i: (0, 0))
    return pl.pallas_call(
        body, name="merge_fwd",
        out_shape=(jax.ShapeDtypeStruct((t, d), BF16), jax.ShapeDtypeStruct((t, d), F32)),
        grid=(t // tm,),
        in_specs=[row(d), row(POOL_WIDTH), row(ATTN_WIDTH), pl.BlockSpec((tm, d), lambda i: (i, 0)),
                  pl.BlockSpec((tm, d), lambda i: (i, 1)), full(wpo), full(wao), full(wout)],
        out_specs=(row(d), row(d)),
        compiler_params=_cparams("parallel"),
    )(x, ps, o, g2, g2, wpo, wao, wout)


def _ffn_fwd(x1, g, wg, wu, wd, tm, tf):
    t, d = x1.shape
    f = wg.shape[0]
    tm = min(tm, t)
    nf = f // tf

    def body(x1_ref, g_ref, wg_ref, wu_ref, wd_ref, h2_ref, gt_ref, up_ref, act_ref, x2_ref):
        j = pl.program_id(1)

        @pl.when(j == 0)
        def _():
            h2_ref[...] = _rms_fwd(x1_ref[...], g_ref[...]).astype(BF16)

        h2 = h2_ref[...]
        gt = _dot_nt(h2, wg_ref[...])
        up = _dot_nt(h2, wu_ref[...])
        act = (gt * _sigmoid(gt) * up).astype(BF16)
        gt_ref[...] = gt
        up_ref[...] = up
        act_ref[...] = act
        prod = _dot(act, wd_ref[...])

        @pl.when(j == 0)
        def _():
            x2_ref[...] = prod

        @pl.when(j > 0)
        def _():
            x2_ref[...] += prod

        @pl.when(j == nf - 1)
        def _():
            x2_ref[...] += x1_ref[...]

    return pl.pallas_call(
        body, name="ffn_fwd",
        out_shape=(jax.ShapeDtypeStruct((t, d), BF16), jax.ShapeDtypeStruct((t, f), F32),
                   jax.ShapeDtypeStruct((t, f), F32), jax.ShapeDtypeStruct((t, f), BF16),
                   jax.ShapeDtypeStruct((t, d), F32)),
        grid=(t // tm, nf),
        in_specs=[pl.BlockSpec((tm, d), lambda i, j: (i, 0)), pl.BlockSpec((1, d), lambda i, j: (0, 0)),
                  pl.BlockSpec((tf, d), lambda i, j: (j, 0)), pl.BlockSpec((tf, d), lambda i, j: (j, 0)),
                  pl.BlockSpec((tf, d), lambda i, j: (j, 0))],
        out_specs=(pl.BlockSpec((tm, d), lambda i, j: (i, 0)), pl.BlockSpec((tm, tf), lambda i, j: (i, j)),
                   pl.BlockSpec((tm, tf), lambda i, j: (i, j)), pl.BlockSpec((tm, tf), lambda i, j: (i, j)),
                   pl.BlockSpec((tm, d), lambda i, j: (i, 0))),
        compiler_params=_cparams("parallel", "arbitrary"),
    )(x1, g, wg, wu, wd)


def _final_fwd_bwd(x2, target, g, tm):
    t, d = x2.shape
    tm = min(tm, t)

    def body(x_ref, t_ref, g_ref, loss_ref, dx_ref, dg_ref):
        i = pl.program_id(0)
        x = x_ref[...]
        gg = g_ref[...]
        err = _rms_fwd(x, gg) - t_ref[...]
        part = 0.5 * jnp.sum(jnp.mean(err * err, axis=-1, keepdims=True), axis=0, keepdims=True)
        dx, dg = _rms_bwd(x, gg, err * (1.0 / d))
        dx_ref[...] = dx

        @pl.when(i == 0)
        def _():
            loss_ref[...] = jnp.zeros_like(loss_ref)
            dg_ref[...] = jnp.zeros_like(dg_ref)

        loss_ref[...] += jnp.broadcast_to(part, loss_ref.shape)
        dg_ref[...] += dg

    return pl.pallas_call(
        body, name="final_fwd_bwd",
        out_shape=(jax.ShapeDtypeStruct((1, LANES), F32), jax.ShapeDtypeStruct((t, d), F32),
                   jax.ShapeDtypeStruct((1, d), F32)),
        grid=(t // tm,),
        in_specs=[pl.BlockSpec((tm, d), lambda i: (i, 0)), pl.BlockSpec((tm, d), lambda i: (i, 0)),
                  pl.BlockSpec((1, d), lambda i: (0, 0))],
        out_specs=(pl.BlockSpec((1, LANES), lambda i: (0, 0)), pl.BlockSpec((tm, d), lambda i: (i, 0)),
                   pl.BlockSpec((1, d), lambda i: (0, 0))),
        compiler_params=_cparams("arbitrary"),
    )(x2, target, g)


def _ffn_bwd(dx2, x1, g, gt, up, wg, wu, wd, tm, tf):
    t, d = dx2.shape
    f = gt.shape[1]
    tm = min(tm, t)
    nf = f // tf

    def body(dx2_ref, x1_ref, g_ref, gt_ref, up_ref, wg_ref, wu_ref, wd_ref, dgt_ref, dup_ref, dx1_ref, dg_ref, acc_ref):
        i, j = pl.program_id(0), pl.program_id(1)
        dact = _dot_nt(dx2_ref[...].astype(BF16), wd_ref[...])
        gtv = gt_ref[...]
        sg = _sigmoid(gtv)
        dup = (dact * (gtv * sg)).astype(BF16)
        dgt = (dact * up_ref[...] * (sg * (1.0 + gtv * (1.0 - sg)))).astype(BF16)
        dgt_ref[...] = dgt
        dup_ref[...] = dup
        contrib = _dot(dgt, wg_ref[...]) + _dot(dup, wu_ref[...])

        @pl.when(j == 0)
        def _():
            acc_ref[...] = contrib

        @pl.when(j > 0)
        def _():
            acc_ref[...] += contrib

        @pl.when(jnp.logical_and(i == 0, j == 0))
        def _():
            dg_ref[...] = jnp.zeros_like(dg_ref)

        @pl.when(j == nf - 1)
        def _():
            dxn, dg = _rms_bwd(x1_ref[...], g_ref[...], acc_ref[...])
            dx1_ref[...] = dx2_ref[...] + dxn
            dg_ref[...] += dg

    return pl.pallas_call(
        body, name="ffn_bwd",
        out_shape=(jax.ShapeDtypeStruct((t, f), BF16), jax.ShapeDtypeStruct((t, f), BF16),
                   jax.ShapeDtypeStruct((t, d), F32), jax.ShapeDtypeStruct((1, d), F32)),
        grid=(t // tm, nf),
        in_specs=[pl.BlockSpec((tm, d), lambda i, j: (i, 0)), pl.BlockSpec((tm, d), lambda i, j: (i, 0)),
                  pl.BlockSpec((1, d), lambda i, j: (0, 0)),
                  pl.BlockSpec((tm, tf), lambda i, j: (i, j)), pl.BlockSpec((tm, tf), lambda i, j: (i, j)),
                  pl.BlockSpec((tf, d), lambda i, j: (j, 0)), pl.BlockSpec((tf, d), lambda i, j: (j, 0)),
                  pl.BlockSpec((tf, d), lambda i, j: (j, 0))],
        out_specs=(pl.BlockSpec((tm, tf), lambda i, j: (i, j)), pl.BlockSpec((tm, tf), lambda i, j: (i, j)),
                   pl.BlockSpec((tm, d), lambda i, j: (i, 0)), pl.BlockSpec((1, d), lambda i, j: (0, 0))),
        scratch_shapes=[pltpu.VMEM((tm, d), F32)],
        compiler_params=_cparams("arbitrary", "arbitrary"),
    )(dx2, x1, g, gt, up, wg, wu, wd)


def _merge_bwd(dx1, ps, o, g2, wpo, wao, wout, tm):
    t, d = dx1.shape
    tm = min(tm, t)

    def body(dx1_ref, ps_ref, o_ref, gp_ref, ga_ref, wpo_ref, wao_ref, wout_ref, dpy_ref, day_ref, dg2_ref, dps_ref, da_ref):
        dm = _dot_nt(dx1_ref[...].astype(BF16), wout_ref[...])
        py = _dot(ps_ref[...], wpo_ref[...])
        ay = _dot(o_ref[...].astype(BF16), wao_ref[...])
        sp = _sigmoid(gp_ref[...])
        sa = _sigmoid(ga_ref[...])
        dpy = (dm * sp).astype(BF16)
        day = (dm * sa).astype(BF16)
        dpy_ref[...] = dpy
        day_ref[...] = day
        dg2_ref[:, :d] = (dm * py * (sp * (1.0 - sp))).astype(BF16)
        dg2_ref[:, d:] = (dm * ay * (sa * (1.0 - sa))).astype(BF16)
        dps_ref[...] = _dot_nt(dpy, wpo_ref[...])
        da_ref[...] = _dot_nt(day, wao_ref[...]).astype(BF16)

    row = lambda w: pl.BlockSpec((tm, w), lambda i: (i, 0))
    full = lambda a: pl.BlockSpec(a.shape, lambda i: (0, 0))
    return pl.pallas_call(
        body, name="merge_bwd",
        out_shape=(jax.ShapeDtypeStruct((t, d), BF16), jax.ShapeDtypeStruct((t, d), BF16),
                   jax.ShapeDtypeStruct((t, 2 * d), BF16), jax.ShapeDtypeStruct((t, POOL_WIDTH), F32),
                   jax.ShapeDtypeStruct((t, ATTN_WIDTH), BF16)),
        grid=(t // tm,),
        in_specs=[row(d), row(POOL_WIDTH), row(ATTN_WIDTH), pl.BlockSpec((tm, d), lambda i: (i, 0)),
                  pl.BlockSpec((tm, d), lambda i: (i, 1)), full(wpo), full(wao), full(wout)],
        out_specs=(row(d), row(d), row(2 * d), row(POOL_WIDTH), row(ATTN_WIDTH)),
        compiler_params=_cparams("parallel"),
    )(dx1, ps, o, g2, g2, wpo, wao, wout)


def _attn_bwd(qa, ka, v, do, lse4, seq, tq):
    t = qa.shape[0]
    nq = seq // tq
    hp_n = N_HEADS // 2
    heads = [slice(e * LANES, (e + 1) * LANES) for e in range(2)]

    def body(q_ref, k_ref, v_ref, do_ref, lse_ref, dq_ref, dk_ref, dv_ref, dfr_ref, dk_acc, dv_acc, p_buf, dp_buf):
        diag_ok = lax.broadcasted_iota(jnp.int32, (tq, tq), 0) >= lax.broadcasted_iota(jnp.int32, (tq, tq), 1)
        lane_q = lax.broadcasted_iota(jnp.int32, (tq, LANES), 1)
        lane_s = lax.broadcasted_iota(jnp.int32, (seq, LANES), 1)
        mine_q = [lane_q < HEAD_DIM, lane_q >= HEAD_DIM]
        dv_acc[...] = jnp.zeros_like(dv_acc)
        dk_acc[...] = jnp.zeros_like(dk_acc)
        dfr_ref[...] = jnp.zeros_like(dfr_ref)

        def q_step(i, _):
            q0 = pl.multiple_of(i * tq, tq)
            qs = [q_ref[pl.ds(q0, tq), hl] for hl in heads]
            dov = do_ref[pl.ds(q0, tq), :]
            dos = [jnp.where(mq, dov, jnp.zeros((), BF16)) for mq in mine_q]
            lss = [lse_ref[pl.ds(q0, tq), e:e + 1] for e in range(2)]

            def sweep1(j, dls):
                r0 = pl.multiple_of(j * tq, tq)
                vv = v_ref[pl.ds(r0, tq), :]
                out = []
                for e, hl in enumerate(heads):
                    s = _dot_nt(qs[e], k_ref[pl.ds(r0, tq), hl])
                    s = jnp.where(jnp.logical_or(diag_ok, j < i), s, NEG_BIG)
                    p = jnp.exp(s - lss[e])
                    dp = _dot_nt(dos[e], vv)
                    p_buf[e, j] = p
                    dp_buf[e, j] = dp
                    dv_acc[pl.ds(r0, tq), :] += _dot_tn(p.astype(BF16), dos[e])
                    out.append(dls[e] + _fold_lanes(p * dp, jnp.add))
                return tuple(out)

            dls = lax.fori_loop(0, i + 1, sweep1, (jnp.zeros((tq, LANES), F32),) * 2)
            dls = [jnp.sum(d, axis=1, keepdims=True) for d in dls]

            def sweep2(j, dqs):
                r0 = pl.multiple_of(j * tq, tq)
                out = []
                for e, hl in enumerate(heads):
                    ds = p_buf[e, j] * (dp_buf[e, j] - dls[e])
                    dfr_ref[e, pl.ds(j, 1), :] += jnp.sum(ds, axis=0, keepdims=True)
                    dsb = ds.astype(BF16)
                    dk_acc[e, pl.ds(r0, tq), :] += _dot_tn(dsb, qs[e])
                    out.append(dqs[e] + _dot(dsb, k_ref[pl.ds(r0, tq), hl]))
                return tuple(out)

            dqs = lax.fori_loop(0, i + 1, sweep2, (jnp.zeros((tq, LANES), F32),) * 2)
            dq = jnp.where(mine_q[0], dqs[0], pltpu.roll(dqs[1], HEAD_DIM, 1)) * ATTN_SCALE
            dq_ref[pl.ds(q0, tq), :] = dq.astype(BF16)
            return 0

        lax.fori_loop(0, nq, q_step, 0)
        dk_ref[...] = jnp.where(lane_s < HEAD_DIM, dk_acc[0], pltpu.roll(dk_acc[1], HEAD_DIM, 1)).astype(BF16)
        dv_ref[...] = dv_acc[...].astype(BF16)

    wide = pl.BlockSpec((seq, 2 * LANES), lambda b, hp: (b, hp))
    col = pl.BlockSpec((seq, LANES), lambda b, hp: (b, hp))
    pair = pl.BlockSpec((None, seq, 2), lambda b, hp: (hp, b, 0))
    return pl.pallas_call(
        body, name="attn_bwd",
        out_shape=(jax.ShapeDtypeStruct((t, ATTN_WIDTH), BF16),) * 3 + (jax.ShapeDtypeStruct((N_HEADS, t // tq, tq), F32),),
        grid=(t // seq, hp_n),
        in_specs=[wide, wide, col, col, pair],
        out_specs=(col, col, col, pl.BlockSpec((2, nq, tq), lambda b, hp: (hp, b, 0))),
        scratch_shapes=[pltpu.VMEM((2, seq, LANES), F32), pltpu.VMEM((seq, LANES), F32),
                        pltpu.VMEM((2, nq, tq, tq), F32), pltpu.VMEM((2, nq, tq, tq), F32)],
        compiler_params=_cparams("parallel", "arbitrary"),
    )(qa, ka, v, do, lse4)


def _forget_bwd(dfc, fl, bf, seq):
    t = fl.shape[0]
    cb = min(256, seq)
    nb = seq // cb

    def body(dfc_ref, fl_ref, bf_ref, dfl_ref, db_ref):
        b = pl.program_id(0)
        ri = lax.broadcasted_iota(jnp.int32, (cb, cb), 0)
        ci = lax.broadcasted_iota(jnp.int32, (cb, cb), 1)
        tri = (ci >= ri).astype(BF16)
        carry = jnp.zeros((1, LANES), F32)
        dbs = jnp.zeros((1, LANES), F32)
        for blk in reversed(range(nb)):
            rs = slice(blk * cb, (blk + 1) * cb)
            dlf = _tri_dot(tri, -dfc_ref[rs, :]) + carry
            carry = dlf[0:1, :]
            dfl = dlf * _sigmoid(-(fl_ref[rs, :] + bf_ref[...]))
            dfl_ref[rs, :] = dfl.astype(BF16)
            dbs = dbs + jnp.sum(dfl, axis=0, keepdims=True)

        @pl.when(b == 0)
        def _():
            db_ref[...] = jnp.zeros_like(db_ref)

        db_ref[...] += dbs

    return pl.pallas_call(
        body, name="forget_bwd",
        out_shape=(jax.ShapeDtypeStruct((t, LANES), BF16), jax.ShapeDtypeStruct((1, LANES), F32)),
        grid=(t // seq,),
        in_specs=[pl.BlockSpec((seq, LANES), lambda b: (b, 0)), pl.BlockSpec((seq, LANES), lambda b: (b, 0)),
                  pl.BlockSpec((1, LANES), lambda b: (0, 0))],
        out_specs=(pl.BlockSpec((seq, LANES), lambda b: (b, 0)), pl.BlockSpec((1, LANES), lambda b: (0, 0))),
        compiler_params=_cparams("arbitrary"),
    )(dfc, fl, bf)


def _pool_bwd(dps, p, mix, scale, seq):
    t = dps.shape[0]

    def body(dps_ref, p_ref, mix_ref, sc_ref, du_ref, dmix_ref, dsc_ref):
        b = pl.program_id(0)

        @pl.when(b == 0)
        def _():
            dmix_ref[...] = jnp.zeros_like(dmix_ref)
            dsc_ref[...] = jnp.zeros_like(dsc_ref)

        tpos = lax.broadcasted_iota(jnp.int32, (seq, POOL_GROUP_DIM), 0)
        for g in range(POOL_GROUPS):
            sl = slice(g * POOL_GROUP_DIM, (g + 1) * POOL_GROUP_DIM)
            pb = p_ref[:, sl]
            dpsg = dps_ref[:, sl]
            pm = _dot(pb, mix_ref[g])
            dsc_ref[:, sl] += jnp.sum(dpsg * pm, axis=0, keepdims=True)
            dpm = (dpsg * sc_ref[:, sl]).astype(BF16)
            dmix_ref[g] += _dot_tn(pb, dpm)
            dp = _dot_nt(dpm, mix_ref[g])
            cnt = jnp.minimum(tpos + 1, POOL_WINDOWS[g]).astype(F32)
            s = dp / cnt
            for lvl in range(g + 1):
                d = 2 ** lvl
                s = s + jnp.where(tpos < seq - d, pltpu.roll(s, seq - d, 0), 0.0)
            du_ref[:, sl] = (s - dp).astype(BF16)

    return pl.pallas_call(
        body, name="pool_bwd",
        out_shape=(jax.ShapeDtypeStruct((t, POOL_WIDTH), BF16),
                   jax.ShapeDtypeStruct((POOL_GROUPS, POOL_GROUP_DIM, POOL_GROUP_DIM), F32),
                   jax.ShapeDtypeStruct((1, POOL_WIDTH), F32)),
        grid=(t // seq,),
        in_specs=[pl.BlockSpec((seq, POOL_WIDTH), lambda b: (b, 0)), pl.BlockSpec((seq, POOL_WIDTH), lambda b: (b, 0)),
                  pl.BlockSpec((POOL_GROUPS, POOL_GROUP_DIM, POOL_GROUP_DIM), lambda b: (0, 0, 0)),
                  pl.BlockSpec((1, POOL_WIDTH), lambda b: (0, 0))],
        out_specs=(pl.BlockSpec((seq, POOL_WIDTH), lambda b: (b, 0)),
                   pl.BlockSpec((POOL_GROUPS, POOL_GROUP_DIM, POOL_GROUP_DIM), lambda b: (0, 0, 0)),
                   pl.BlockSpec((1, POOL_WIDTH), lambda b: (0, 0))),
        compiler_params=_cparams("arbitrary"),
    )(dps, p, mix, scale)


def _in_bwd(du, dq, dk, dv, dg2, dfl, dx1, x, g, wu, wqkv, wg2, wft, tm):
    t, d = x.shape
    tm = min(tm, t)
    aw = ATTN_WIDTH

    def body(du_ref, dq_ref, dk_ref, dv_ref, dg2_ref, dfl_ref, dx1_ref, x_ref, g_ref, wu_ref, wqkv_ref, wg2_ref, wft_ref,
             dx_ref, dg_ref):
        i = pl.program_id(0)
        dh = _dot_nt(du_ref[...], wu_ref[...])
        dh += _dot_nt(dq_ref[...], wqkv_ref[:, 0:aw])
        dh += _dot_nt(dk_ref[...], wqkv_ref[:, aw:2 * aw])
        dh += _dot_nt(dv_ref[...], wqkv_ref[:, 2 * aw:3 * aw])
        dh += _dot_nt(dg2_ref[...], wg2_ref[...])
        dh += _dot(dfl_ref[...], wft_ref[...])
        dxn, dg = _rms_bwd(x_ref[...], g_ref[...], dh)
        dx_ref[...] = dx1_ref[...] + dxn

        @pl.when(i == 0)
        def _():
            dg_ref[...] = jnp.zeros_like(dg_ref)

        dg_ref[...] += dg

    row = lambda w: pl.BlockSpec((tm, w), lambda i: (i, 0))
    full = lambda a: pl.BlockSpec(a.shape, lambda i: (0, 0))
    return pl.pallas_call(
        body, name="in_bwd",
        out_shape=(jax.ShapeDtypeStruct((t, d), F32), jax.ShapeDtypeStruct((1, d), F32)),
        grid=(t // tm,),
        in_specs=[row(POOL_WIDTH), row(aw), row(aw), row(aw), row(2 * d), row(LANES), row(d), row(d),
                  pl.BlockSpec((1, d), lambda i: (0, 0)), full(wu), full(wqkv), full(wg2), full(wft)],
        out_specs=(row(d), pl.BlockSpec((1, d), lambda i: (0, 0))),
        compiler_params=_cparams("arbitrary"),
    )(du, dq, dk, dv, dg2, dfl, dx1, x, g, wu, wqkv, wg2, wft)


def _local_step(x, target, seq, norm1_g, w_in, b_forget, pool_mix, pool_scale, w_pool_out, w_attn_out, w_out,
                norm2_g, w_ffn_gate, w_ffn_up, w_ffn_down, norm_f_g):
    t, d = x.shape
    tq = min(256, seq)
    aw = ATTN_WIDTH
    o_q, o_f, o_g = POOL_WIDTH, POOL_WIDTH + 3 * aw, POOL_WIDTH + 3 * aw + N_HEADS
    wu = w_in[:, :o_q]
    wqkv = w_in[:, o_q:o_f]
    wf = jnp.pad(w_in[:, o_f:o_g], ((0, 0), (0, LANES - N_HEADS)))
    wg2 = w_in[:, o_g:]
    wft = wf.T
    bf = jnp.pad(b_forget, ((0, 0), (0, LANES - N_HEADS)))
    mixb = pool_mix.astype(BF16)

    h = _norm_fwd("norm1_fwd", x, norm1_g, 512)
    u = _matmul("mm_u", h, wu, "nn", F32, 1024, 512, d)
    g2 = _matmul("mm_gates", h, wg2, "nn", F32, 1024, 512, d)
    fl, fcum = _forget_fwd(h, wf, bf, seq)
    qa, ka, v = _attn_prep(h, _head_blocks(wqkv[:, :aw]), _head_blocks(wqkv[:, aw:2 * aw]), wqkv[:, 2 * aw:], fcum, 512)
    p, ps = _pool_fwd(u, mixb, pool_scale, seq)
    o, lse = _attn_fwd(qa, ka, v, seq, tq)
    merged, x1 = _merge_fwd(x, ps, o, g2, w_pool_out, w_attn_out, w_out, 256)
    h2, gt, up, act, x2 = _ffn_fwd(x1, norm2_g, w_ffn_gate, w_ffn_up, w_ffn_down, 1024, 256)
    loss, dx2, d_gf = _final_fwd_bwd(x2, target, norm_f_g, 512)

    dgt, dup, dx1, d_g2n = _ffn_bwd(dx2, x1, norm2_g, gt, up, w_ffn_gate, w_ffn_up, w_ffn_down, 1024, 256)
    d_wd = _matmul("dw_down", act, dx2, "tn", F32, 1408, 1024, 512)
    d_wg = _matmul("dw_gate", dgt, h2, "tn", F32, 1408, 1024, 512)
    d_wu = _matmul("dw_up", dup, h2, "tn", F32, 1408, 1024, 512)
    dpy, day, dg2, dps, da = _merge_bwd(dx1, ps, o, g2, w_pool_out, w_attn_out, w_out, 256)
    d_wout = _matmul("dw_out", merged, dx1, "tn", F32, 1024, 1024, 512)
    d_wpo = _matmul("dw_pool_out", ps, dpy, "tn", F32, 512, 1024, 512)
    d_wao = _matmul("dw_attn_out", o, day, "tn", F32, 512, 1024, 512)
    dq, dk, dv, dfr = _attn_bwd(qa, ka, v, da, lse, seq, tq)
    dfc = jnp.pad(dfr.reshape(N_HEADS, t).T, ((0, 0), (0, LANES - N_HEADS)))
    dfl, d_bf = _forget_bwd(dfc, fl, bf, seq)
    du, d_mix, d_scale = _pool_bwd(dps, p, mixb, pool_scale, seq)
    dx, d_g1 = _in_bwd(du, dq, dk, dv, dg2, dfl, dx1, x, norm1_g, wu, wqkv, wg2, wft, 256)
    d_wu_in = _matmul("dw_in_u", h, du, "tn", F32, 1024, 512, 512)
    d_wq = _matmul("dw_in_q", h, dq, "tn", F32, 1024, 512, 512)
    d_wk = _matmul("dw_in_k", h, dk, "tn", F32, 1024, 512, 512)
    d_wv = _matmul("dw_in_v", h, dv, "tn", F32, 1024, 512, 512)
    d_wf = _matmul("dw_in_f", h, dfl, "tn", F32, 1024, LANES, 512)
    d_wg2 = _matmul("dw_in_gates", h, dg2, "tn", F32, 1024, 1024, 512)
    d_win = jnp.concatenate([d_wu_in, d_wq, d_wk, d_wv, d_wf[:, :N_HEADS], d_wg2], axis=1)

    big = (d_win, d_wpo, d_wao, d_wout, d_wg, d_wu, d_wd)
    small = (d_g1, d_bf[:, :N_HEADS], d_mix, d_scale, d_g2n, d_gf)
    return loss, dx, big, small


def _position():
    return lax.axis_index("x"), lax.axis_index("y"), lax.axis_index("c")


def _remote(src, dst, send_sem, recv_sem, device):
    return pltpu.make_async_remote_copy(src_ref=src, dst_ref=dst, send_sem=send_sem, recv_sem=recv_sem,
                                        device_id=device, device_id_type=MESH)


def _gather_weights(shards):
    n = len(shards)

    def body(*refs):
        w, g = refs[:n], refs[n:2 * n]
        send, recv = refs[2 * n:]
        x, y, c = _position()
        me = 2 * x + y
        sibling = (x, y, 1 - c)
        chips = [(1 - x, y), (x, 1 - y), (1 - x, 1 - y)]
        sends = []
        for k in range(n):
            for j, (ox, oy) in enumerate(chips):
                cp = _remote(w[k].at[c], g[k].at[me, c], send.at[6 * k + j], recv.at[6 * k + j], (ox, oy, c))
                cp.start()
                sends.append(cp)
        for k in range(n):
            for j, (ox, oy) in enumerate(chips):
                blk = g[k].at[2 * ox + oy, c]
                _remote(blk, blk, send.at[6 * k + j], recv.at[6 * k + j], (ox, oy, c)).wait_recv()
                cp = _remote(blk, blk, send.at[6 * k + 3 + j], recv.at[6 * k + 3 + j], sibling)
                cp.start()
                sends.append(cp)
        for k in range(n):
            for j, (ox, oy) in enumerate(chips):
                blk = g[k].at[2 * ox + oy, 1 - c]
                _remote(blk, blk, send.at[6 * k + 3 + j], recv.at[6 * k + 3 + j], sibling).wait_recv()
        for cp in sends:
            cp.wait_send()

    return pl.pallas_call(
        body, name="gather_weights",
        out_shape=tuple(jax.ShapeDtypeStruct((N_CHIPS,) + s.shape, s.dtype) for s in shards),
        in_specs=[ANY] * n, out_specs=tuple([ANY] * n),
        scratch_shapes=[pltpu.SemaphoreType.DMA((6 * n,)), pltpu.SemaphoreType.DMA((6 * n,))],
        compiler_params=pltpu.CompilerParams(has_side_effects=True),
    )(*shards)


def _swap(name, gives, peer_fn, give_fn, recv_shapes):
    n = len(gives)

    def body(*refs):
        give, recv = refs[:n], refs[n:2 * n]
        send_sem, recv_sem = refs[2 * n:]
        pos = _position()
        copies = [_remote(give_fn(give[k], *pos), recv[k], send_sem.at[k], recv_sem.at[k], peer_fn(*pos))
                  for k in range(n)]
        for cp in copies:
            cp.start()
        for cp in copies:
            cp.wait()

    return pl.pallas_call(
        body, name=name,
        out_shape=tuple(jax.ShapeDtypeStruct(s, g.dtype) for s, g in zip(recv_shapes, gives)),
        in_specs=[ANY] * n, out_specs=tuple([ANY] * n),
        scratch_shapes=[pltpu.SemaphoreType.DMA((n,)), pltpu.SemaphoreType.DMA((n,))],
        compiler_params=pltpu.CompilerParams(has_side_effects=True),
    )(*gives)


def _add_keep_give(name, pos, a, a_keep, a_give, b, b_keep, b_give, steps):
    r, c = a.shape[-2:]

    def spec(arr, fn):
        lead = arr.ndim - 2
        return pl.BlockSpec((None,) * lead + (r, c), lambda i, p: tuple(fn(i, p)) + (0, 0))

    out_spec = pl.BlockSpec((None, r, c), lambda i, p: (i, 0, 0))

    def body(p_ref, ak_ref, bk_ref, ag_ref, bg_ref, keep_ref, give_ref):
        keep_ref[...] = ak_ref[...] + bk_ref[...].astype(F32)
        give_ref[...] = (ag_ref[...] + bg_ref[...].astype(F32)).astype(BF16)

    return pl.pallas_call(
        body, name=name,
        out_shape=(jax.ShapeDtypeStruct((steps, r, c), F32), jax.ShapeDtypeStruct((steps, r, c), BF16)),
        grid_spec=pltpu.PrefetchScalarGridSpec(
            num_scalar_prefetch=1, grid=(steps,),
            in_specs=[spec(a, a_keep), spec(b, b_keep), spec(a, a_give), spec(b, b_give)],
            out_specs=(out_spec, out_spec)),
        compiler_params=_cparams("parallel"),
    )(pos, a, b, a, b)


def _add_last(name, a, b):
    _, r, c = a.shape
    blk = pl.BlockSpec((None, r, c), lambda i: (0, 0, 0))

    def body(a_ref, b_ref, o_ref):
        o_ref[...] = a_ref[...] + b_ref[...].astype(F32)

    return pl.pallas_call(
        body, name=name, out_shape=jax.ShapeDtypeStruct((r, c), F32), grid=(1,), in_specs=[blk, blk],
        out_specs=pl.BlockSpec((r, c), lambda i: (0, 0)), compiler_params=_cparams("arbitrary"),
    )(a, b)


def _swap_halves_c(grads):
    n = len(grads)

    def body(*refs):
        give, recv = refs[:n], refs[n:2 * n]
        send_sem, recv_sem = refs[2 * n:]
        x, y, c = _position()
        copies = [_remote(give[k].at[j, 1 - c], recv[k].at[j], send_sem.at[N_CHIPS * k + j],
                          recv_sem.at[N_CHIPS * k + j], (x, y, 1 - c))
                  for k in range(n) for j in range(N_CHIPS)]
        for cp in copies:
            cp.start()
        for cp in copies:
            cp.wait()

    return pl.pallas_call(
        body, name="rs_c",
        out_shape=tuple(jax.ShapeDtypeStruct((N_CHIPS,) + g.shape[2:], g.dtype) for g in grads),
        in_specs=[ANY] * n, out_specs=tuple([ANY] * n),
        scratch_shapes=[pltpu.SemaphoreType.DMA((N_CHIPS * n,)), pltpu.SemaphoreType.DMA((N_CHIPS * n,))],
        compiler_params=pltpu.CompilerParams(has_side_effects=True),
    )(*grads)


def _reduce_scatter(grads):
    n = len(grads)
    shp = [g.shape[2:] for g in grads]
    x, y, c = _position()
    pos_cx = jnp.stack([c, x]).astype(jnp.int32)
    pos_y = jnp.stack([y]).astype(jnp.int32)
    recv = _swap_halves_c(grads)
    p1 = [_add_keep_give(
        f"rs_c_add{k}", pos_cx,
        grads[k], lambda i, p: (2 * p[1] + i, p[0]), lambda i, p: (2 * (1 - p[1]) + i, p[0]),
        recv[k], lambda i, p: (2 * p[1] + i,), lambda i, p: (2 * (1 - p[1]) + i,), 2) for k in range(n)]
    recv = _swap("rs_x", [p[1] for p in p1], lambda x, y, c: (1 - x, y, c), lambda ref, x, y, c: ref,
                 [(2,) + s for s in shp])
    p2 = [_add_keep_give(
        f"rs_x_add{k}", pos_y,
        p1[k][0], lambda i, p: (p[0],), lambda i, p: (1 - p[0],),
        recv[k], lambda i, p: (p[0],), lambda i, p: (1 - p[0],), 1) for k in range(n)]
    recv = _swap("rs_y", [p[1] for p in p2], lambda x, y, c: (x, 1 - y, c), lambda ref, x, y, c: ref,
                 [(1,) + s for s in shp])
    mine = [_add_last(f"rs_y_add{k}", p2[k][0], recv[k]) for k in range(n)]
    other = _swap("rs_swap_halves", mine, lambda x, y, c: (x, y, 1 - c), lambda ref, x, y, c: ref, shp)
    return list(zip(mine, other))


def _all_reduce_small(v):
    r = v.shape[0]

    def body(v_ref, out_ref, buf, send_sems, recv_sems, local_sem):
        x, y, c = _position()
        me, sibling = (x, y, c), (x, y, 1 - c)
        chips = [(1 - x, y), (x, 1 - y), (1 - x, 1 - y)]

        def rows(px, py, pc):
            return buf.at[pl.ds((4 * px + 2 * py + pc) * r, r), :]

        def copy(k, block, to, src=None):
            return _remote(rows(*block) if src is None else src, rows(*block), send_sems.at[k], recv_sems.at[k], to)

        mine = pltpu.make_async_copy(v_ref, rows(*me), local_sem)
        mine.start()
        first = [copy(0, me, sibling, src=v_ref)]
        first += [copy(1 + j, me, (*chip, c), src=v_ref) for j, chip in enumerate(chips)]
        for cp in first:
            cp.start()
        passed = [copy(4 + j, (*chip, c), sibling) for j, chip in enumerate(chips)]
        for j, chip in enumerate(chips):
            copy(1 + j, (*chip, c), me).wait_recv()
            passed[j].start()
        copy(0, sibling, me).wait_recv()
        for j, chip in enumerate(chips):
            copy(4 + j, (*chip, 1 - c), me).wait_recv()
        for cp in first + passed:
            cp.wait_send()
        mine.wait()
        acc = buf[0:r, :]
        for dev in range(1, N_DEV):
            acc = acc + buf[dev * r:(dev + 1) * r, :]
        out_ref[...] = acc

    return pl.pallas_call(
        body, name="all_reduce_small",
        out_shape=jax.ShapeDtypeStruct(v.shape, F32),
        in_specs=[pl.BlockSpec(memory_space=pltpu.VMEM)],
        out_specs=pl.BlockSpec(memory_space=pltpu.VMEM),
        scratch_shapes=[pltpu.VMEM((N_DEV * r, LANES), F32), pltpu.SemaphoreType.DMA((7,)),
                        pltpu.SemaphoreType.DMA((7,)), pltpu.SemaphoreType.DMA],
        compiler_params=pltpu.CompilerParams(has_side_effects=True, vmem_limit_bytes=VMEM_LIMIT_V7X),
    )(v)


def _adamw_update(w, gg, m, v):
    mn = ADAM_B1 * m + (1.0 - ADAM_B1) * gg
    vn = ADAM_B2 * v + (1.0 - ADAM_B2) * (gg * gg)
    m_hat = mn / (1.0 - ADAM_B1 ** ADAM_STEP)
    v_hat = vn / (1.0 - ADAM_B2 ** ADAM_STEP)
    return -ADAM_LR * (m_hat / (jnp.sqrt(v_hat) + ADAM_EPS) + ADAM_WD * w), mn, vn


def _adamw(name, w, g, m, v):
    def body(w_ref, g_ref, m_ref, v_ref, d_ref, mo_ref, vo_ref):
        d_ref[...], mo_ref[...], vo_ref[...] = _adamw_update(w_ref[...], g_ref[...], m_ref[...], v_ref[...])

    blk = pl.BlockSpec(w.shape, lambda i: (0, 0))
    return pl.pallas_call(
        body, name=name, out_shape=(jax.ShapeDtypeStruct(w.shape, F32),) * 3, grid=(1,),
        in_specs=[blk] * 4, out_specs=(blk,) * 3, compiler_params=_cparams("arbitrary"),
    )(w, g, m, v)


def _adamw_halves(name, pos_c, w, g_mine, g_other, m, v, tr):
    r, c = w.shape
    rh = r // 2
    tr = tr if rh % tr == 0 else rh
    nt = rh // tr

    def body(p_ref, w_ref, gm_ref, go_ref, m_ref, v_ref, g_ref, d_ref, mo_ref, vo_ref):
        gg = jnp.where(pl.program_id(0) == p_ref[0], gm_ref[...], go_ref[...])
        g_ref[...] = gg
        d_ref[...], mo_ref[...], vo_ref[...] = _adamw_update(w_ref[...], gg, m_ref[...], v_ref[...])

    full = pl.BlockSpec((tr, c), lambda h, i, p: (h * nt + i, 0))
    half = pl.BlockSpec((tr, c), lambda h, i, p: (i, 0))
    return pl.pallas_call(
        body, name=name, out_shape=(jax.ShapeDtypeStruct((r, c), F32),) * 4,
        grid_spec=pltpu.PrefetchScalarGridSpec(
            num_scalar_prefetch=1, grid=(2, nt),
            in_specs=[full, half, half, full, full], out_specs=(full,) * 4),
        compiler_params=_cparams("parallel", "parallel"),
    )(pos_c, w, g_mine, g_other, m, v)


def _col_sharded_to_comm(g):
    k, n = g.shape
    return g.reshape(2, k // 2, N_CHIPS, n // N_CHIPS).transpose(2, 0, 1, 3)


def _row_sharded_to_comm(g):
    r, c = g.shape
    return g.reshape(N_CHIPS, 2, r // (2 * N_CHIPS), c)


def _pack_small(g1, bfv, mix, scale, g2n, gf, extra=None):
    row8 = jnp.pad(bfv.reshape(1, N_HEADS), ((0, 0), (0, LANES - N_HEADS)))
    if extra is not None:
        row8 = row8 + jnp.pad(extra[:, :1], ((0, 0), (N_HEADS, LANES - N_HEADS - 1)))
    return jnp.concatenate([
        g1.reshape(8, LANES), jnp.pad(row8, ((0, 7), (0, 0))), mix.reshape(512, LANES),
        jnp.pad(scale.reshape(4, LANES), ((0, 4), (0, 0))), g2n.reshape(8, LANES), gf.reshape(8, LANES)], axis=0)


def _unpack_small(s, like):
    g1, bfv, mix, scale, g2n, gf = like
    return (s[0:8].reshape(g1.shape), s[8, :N_HEADS].reshape(bfv.shape), s[16:528].reshape(mix.shape),
            s[528:532].reshape(scale.shape), s[536:544].reshape(g2n.shape), s[544:552].reshape(gf.shape))


def kernel(x, norm1_g, w_in, b_forget, pool_mix, pool_scale, w_pool_out, w_attn_out, w_out, norm2_g, w_ffn_gate, w_ffn_up, w_ffn_down, norm_f_g, loss_target, m_norm1_g, m_w_in, m_b_forget, m_pool_mix, m_pool_scale, m_w_pool_out, m_w_attn_out, m_w_out, m_norm2_g, m_w_ffn_gate, m_w_ffn_up, m_w_ffn_down, m_norm_f_g, v_norm1_g, v_w_in, v_b_forget, v_pool_mix, v_pool_scale, v_w_pool_out, v_w_attn_out, v_w_out, v_norm2_g, v_w_ffn_gate, v_w_ffn_up, v_w_ffn_down, v_norm_f_g):
    nb, seq, d = x.shape
    big_w = (w_in, w_pool_out, w_attn_out, w_out, w_ffn_gate, w_ffn_up, w_ffn_down)
    big_m = (m_w_in, m_w_pool_out, m_w_attn_out, m_w_out, m_w_ffn_gate, m_w_ffn_up, m_w_ffn_down)
    big_v = (v_w_in, v_w_pool_out, v_w_attn_out, v_w_out, v_w_ffn_gate, v_w_ffn_up, v_w_ffn_down)
    transposed = (False, False, False, False, True, True, False)
    row_sharded = (False, False, False, True, True, True, True)
    small_w = (norm1_g, b_forget, pool_mix, pool_scale, norm2_g, norm_f_g)
    small_m = (m_norm1_g, m_b_forget, m_pool_mix, m_pool_scale, m_norm2_g, m_norm_f_g)
    small_v = (v_norm1_g, v_b_forget, v_pool_mix, v_pool_scale, v_norm2_g, v_norm_f_g)
    view = lambda a, tr: a[0].T if tr else a[0]
    unview = lambda a, tr, like: (a.T if tr else a).reshape(like.shape)

    me = 2 * lax.axis_index("x") + lax.axis_index("y")
    local = [view(w, tr).astype(BF16) for w, tr in zip(big_w, transposed)]
    local = [lw.reshape(2, lw.shape[0] // 2, lw.shape[1]) for lw in local]
    gathered = _gather_weights(local)
    full = []
    for gw, lw, rs in zip(gathered, local, row_sharded):
        r, c = 2 * lw.shape[1], lw.shape[2]
        gw = lax.dynamic_update_index_in_dim(gw, lw, me, 0).reshape(N_CHIPS, r, c)
        full.append(gw.reshape(N_CHIPS * r, c) if rs else gw.transpose(1, 0, 2).reshape(r, N_CHIPS * c))

    loss, dx, big_g, small_g = _local_step(
        x.reshape(nb * seq, d), loss_target.reshape(nb * seq, d), seq,
        norm1_g, full[0], b_forget, pool_mix[0], pool_scale, full[1], full[2], full[3], norm2_g,
        full[4], full[5], full[6], norm_f_g.reshape(1, d))

    comm = [_row_sharded_to_comm(g) if rs else _col_sharded_to_comm(g) for g, rs in zip(big_g, row_sharded)]
    reduced = _reduce_scatter(comm)
    small_sum = _all_reduce_small(_pack_small(*small_g, extra=loss))
    loss_out = small_sum[8, N_HEADS]

    grads, deltas, new_m, new_v = [None] * 13, [None] * 13, [None] * 13, [None] * 13
    big_pos = (1, 5, 6, 7, 9, 10, 11)
    small_pos = (0, 2, 3, 4, 8, 12)
    pos_c = jnp.stack([lax.axis_index("c")]).astype(jnp.int32)
    for k, pos in enumerate(big_pos):
        w, tr = big_w[k], transposed[k]
        outs = _adamw_halves(f"adamw_big{k}", pos_c, view(w, tr), reduced[k][0], reduced[k][1], view(big_m[k], tr),
                             view(big_v[k], tr), 256)
        grads[pos], deltas[pos], new_m[pos], new_v[pos] = (unview(a, tr, w) for a in outs)
    dl, mn, vn = _adamw("adamw_small", _pack_small(*small_w), small_sum * _small_mask(), _pack_small(*small_m),
                        _pack_small(*small_v))
    for pos, g, a, b, e in zip(small_pos, _unpack_small(small_sum, small_w), _unpack_small(dl, small_w),
                               _unpack_small(mn, small_w), _unpack_small(vn, small_w)):
        grads[pos], deltas[pos], new_m[pos], new_v[pos] = g, a, b, e

    return (loss_out, dx.reshape(nb, seq, d), *grads, *deltas, *new_m, *new_v)


def _small_mask():
    rows = lax.broadcasted_iota(jnp.int32, (552, LANES), 0)
    lanes = lax.broadcasted_iota(jnp.int32, (552, LANES), 1)
    return jnp.where(jnp.logical_and(rows == 8, lanes == N_HEADS), 0.0, 1.0).astype(F32)
```

```python
import functools

import jax
import jax.numpy as jnp
from jax import lax
from jax.experimental import pallas as pl
from jax.experimental.pallas import tpu as pltpu

F32 = jnp.float32
BF16 = jnp.bfloat16

D_MODEL = 1024
POOL_WINDOWS = (2, 4, 8, 16)
POOL_GROUPS = 4
POOL_GROUP_DIM = 128
POOL_WIDTH = 512
HEAD_DIM = 64
N_HEADS = 8
ATTN_WIDTH = 512
D_FF = 2816
RMS_EPS = 1e-6
ATTN_SCALE = HEAD_DIM ** -0.5
NEG_BIG = -1e30

ADAM_LR = 0.001
ADAM_B1 = 0.9
ADAM_B2 = 0.999
ADAM_EPS = 1e-08
ADAM_WD = 0.01
ADAM_STEP = 10

LANES = 128
N_CHIPS = 4
N_DEV = 8
VMEM_LIMIT_V7X = 52 * 1024 * 1024
MESH = pl.DeviceIdType.MESH
ANY = pl.BlockSpec(memory_space=pl.ANY)


def _cparams(*sem):
    return pltpu.CompilerParams(dimension_semantics=sem if sem else None, vmem_limit_bytes=VMEM_LIMIT_V7X)


def _dot(a, b):
    return lax.dot_general(a, b, (((1,), (0,)), ((), ())), preferred_element_type=F32)


def _dot_nt(a, b):
    return lax.dot_general(a, b, (((1,), (1,)), ((), ())), preferred_element_type=F32)


def _dot_tn(a, b):
    return lax.dot_general(a, b, (((0,), (0,)), ((), ())), preferred_element_type=F32)


def _sigmoid(x):
    return jax.nn.sigmoid(x)


def _rms_fwd(x, g):
    r = lax.rsqrt(jnp.mean(x * x, axis=-1, keepdims=True) + RMS_EPS)
    return (x * r) * g


def _rms_bwd(x, g, dy):
    r = lax.rsqrt(jnp.mean(x * x, axis=-1, keepdims=True) + RMS_EPS)
    xh = x * r
    dg = jnp.sum(dy * xh, axis=0, keepdims=True)
    dxh = dy * g
    dx = r * (dxh - xh * jnp.mean(dxh * xh, axis=-1, keepdims=True))
    return dx, dg


def _matmul(name, a, b, mode, out_dtype, tm, tn, tk):
    if mode == "nn":
        (m, k), (_, n) = a.shape, b.shape
    elif mode == "nt":
        (m, k), (n, _) = a.shape, b.shape
    else:
        (k, m), (_, n) = a.shape, b.shape
    tm, tn, tk = min(tm, m), min(tn, n), min(tk, k)
    assert m % tm == 0 and n % tn == 0 and k % tk == 0, (name, m, n, k, tm, tn, tk)
    nk = k // tk
    if mode == "tn":
        a_spec = pl.BlockSpec((tk, tm), lambda i, j, kk: (kk, i))
    else:
        a_spec = pl.BlockSpec((tm, tk), lambda i, j, kk: (i, kk))
    if mode == "nt":
        b_spec = pl.BlockSpec((tn, tk), lambda i, j, kk: (j, kk))
    else:
        b_spec = pl.BlockSpec((tk, tn), lambda i, j, kk: (kk, j))
    dot = {"nn": _dot, "nt": _dot_nt, "tn": _dot_tn}[mode]
    use_scratch = nk > 1 and out_dtype != F32

    def body(a_ref, b_ref, o_ref, *scratch):
        prod = dot(a_ref[...].astype(BF16), b_ref[...].astype(BF16))
        if nk == 1:
            o_ref[...] = prod.astype(out_dtype)
            return
        acc = scratch[0] if use_scratch else o_ref
        kk = pl.program_id(2)

        @pl.when(kk == 0)
        def _():
            acc[...] = prod

        @pl.when(kk > 0)
        def _():
            acc[...] += prod

        if use_scratch:
            @pl.when(kk == nk - 1)
            def _():
                o_ref[...] = acc[...].astype(out_dtype)

    return pl.pallas_call(
        body,
        name=name,
        out_shape=jax.ShapeDtypeStruct((m, n), out_dtype),
        grid=(m // tm, n // tn, nk),
        in_specs=[a_spec, b_spec],
        out_specs=pl.BlockSpec((tm, tn), lambda i, j, kk: (i, j)),
        scratch_shapes=[pltpu.VMEM((tm, tn), F32)] if use_scratch else [],
        compiler_params=_cparams("parallel", "parallel", "arbitrary"),
    )(a, b)


def _norm_fwd(name, x, g, tm):
    t, d = x.shape
    tm = min(tm, t)

    def body(x_ref, g_ref, h_ref):
        h_ref[...] = _rms_fwd(x_ref[...], g_ref[...]).astype(BF16)

    return pl.pallas_call(
        body, name=name, out_shape=jax.ShapeDtypeStruct((t, d), BF16), grid=(t // tm,),
        in_specs=[pl.BlockSpec((tm, d), lambda i: (i, 0)), pl.BlockSpec((1, d), lambda i: (0, 0))],
        out_specs=pl.BlockSpec((tm, d), lambda i: (i, 0)),
        compiler_params=_cparams("parallel"),
    )(x, g)


def _split3(x):
    hi = x.astype(BF16)
    r1 = x - hi.astype(F32)
    mid = r1.astype(BF16)
    lo = (r1 - mid.astype(F32)).astype(BF16)
    return hi, mid, lo


def _tri_dot(tri, x):
    hi, mid, lo = _split3(x)
    return _dot(tri, hi) + _dot(tri, mid) + _dot(tri, lo)


def _forget_fwd(h, wf, bf, seq):
    t, d = h.shape
    cb = min(256, seq)

    def body(h_ref, wf_ref, bf_ref, fl_ref, fc_ref):
        fl = _dot(h_ref[...], wf_ref[...])
        fl_ref[...] = fl
        xx = fl + bf_ref[...]
        lf = jnp.minimum(xx, 0.0) - jnp.log(1.0 + jnp.exp(-jnp.abs(xx)))
        ri = lax.broadcasted_iota(jnp.int32, (cb, cb), 0)
        ci = lax.broadcasted_iota(jnp.int32, (cb, cb), 1)
        tri = (ri >= ci).astype(BF16)
        carry = jnp.zeros((1, LANES), F32)
        for blk in range(seq // cb):
            cs = _tri_dot(tri, lf[blk * cb:(blk + 1) * cb]) + carry
            fc_ref[blk * cb:(blk + 1) * cb, :] = cs
            carry = cs[cb - 1:cb, :]

    return pl.pallas_call(
        body, name="forget_fwd",
        out_shape=(jax.ShapeDtypeStruct((t, LANES), F32), jax.ShapeDtypeStruct((t, LANES), F32)),
        grid=(t // seq,),
        in_specs=[pl.BlockSpec((seq, d), lambda b: (b, 0)), pl.BlockSpec((d, LANES), lambda b: (0, 0)),
                  pl.BlockSpec((1, LANES), lambda b: (0, 0))],
        out_specs=(pl.BlockSpec((seq, LANES), lambda b: (b, 0)), pl.BlockSpec((seq, LANES), lambda b: (b, 0))),
        compiler_params=_cparams("parallel"),
    )(h, wf, bf)


def _pool_fwd(u, mix, scale, seq):
    t = u.shape[0]

    def body(u_ref, mix_ref, sc_ref, p_ref, ps_ref):
        tpos = lax.broadcasted_iota(jnp.int32, (seq, POOL_GROUP_DIM), 0)
        for g in range(POOL_GROUPS):
            sl = slice(g * POOL_GROUP_DIM, (g + 1) * POOL_GROUP_DIM)
            ug = u_ref[:, sl]
            s = ug
            for lvl in range(g + 1):
                d = 2 ** lvl
                s = s + jnp.where(tpos >= d, pltpu.roll(s, d, 0), 0.0)
            cnt = jnp.minimum(tpos + 1, POOL_WINDOWS[g]).astype(F32)
            pb = (s / cnt - ug).astype(BF16)
            p_ref[:, sl] = pb
            ps_ref[:, sl] = (_dot(pb, mix_ref[g]) * sc_ref[:, sl]).astype(BF16)

    return pl.pallas_call(
        body, name="pool_fwd",
        out_shape=(jax.ShapeDtypeStruct((t, POOL_WIDTH), BF16), jax.ShapeDtypeStruct((t, POOL_WIDTH), BF16)),
        grid=(t // seq,),
        in_specs=[pl.BlockSpec((seq, POOL_WIDTH), lambda b: (b, 0)),
                  pl.BlockSpec((POOL_GROUPS, POOL_GROUP_DIM, POOL_GROUP_DIM), lambda b: (0, 0, 0)),
                  pl.BlockSpec((1, POOL_WIDTH), lambda b: (0, 0))],
        out_specs=(pl.BlockSpec((seq, POOL_WIDTH), lambda b: (b, 0)), pl.BlockSpec((seq, POOL_WIDTH), lambda b: (b, 0))),
        compiler_params=_cparams("parallel"),
    )(u, mix, scale)


def _aug_constants():
    w = N_HEADS * LANES
    rows = jnp.arange(3 * LANES)
    piece, head = rows // LANES, rows % LANES
    cols = jnp.arange(w)
    live = (head < N_HEADS)[:, None]
    pq = (live & (cols[None, :] == (head * LANES + HEAD_DIM + piece)[:, None])).astype(BF16)
    pk = -(live & (cols[None, :] == (head * LANES + HEAD_DIM + 3 + piece)[:, None])).astype(BF16)
    lane = cols % LANES
    oq = ((lane >= HEAD_DIM + 3) & (lane < HEAD_DIM + 6)).astype(F32)[None, :]
    ok = ((lane >= HEAD_DIM) & (lane < HEAD_DIM + 3)).astype(F32)[None, :]
    return pq, pk, oq, ok


def _head_blocks(w):
    d = w.shape[0]
    return jnp.pad(w.reshape(d, N_HEADS, HEAD_DIM), ((0, 0), (0, 0), (0, LANES - HEAD_DIM))).reshape(d, N_HEADS * LANES)


def _attn_prep(h, wq, wk, wv, fcum, tm):
    t, d = h.shape
    tm = min(tm, t)
    w = N_HEADS * LANES
    pq, pk, oq, ok = _aug_constants()

    def body(h_ref, wq_ref, wk_ref, wv_ref, f_ref, pq_ref, pk_ref, oq_ref, ok_ref, qa_ref, ka_ref, v_ref):
        hh = h_ref[...]
        fs = jnp.concatenate(_split3(f_ref[...]), axis=1)
        q = _dot(hh, wq_ref[...]).astype(BF16).astype(F32) * ATTN_SCALE
        qa_ref[...] = (q + _dot(fs, pq_ref[...]) + oq_ref[...]).astype(BF16)
        k = _dot(hh, wk_ref[...]).astype(BF16).astype(F32)
        ka_ref[...] = (k + _dot(fs, pk_ref[...]) + ok_ref[...]).astype(BF16)
        v_ref[...] = _dot(hh, wv_ref[...]).astype(BF16)

    row = lambda n: pl.BlockSpec((tm, n), lambda i: (i, 0))
    full = lambda a: pl.BlockSpec(a.shape, lambda i: (0, 0))
    return pl.pallas_call(
        body, name="attn_prep",
        out_shape=(jax.ShapeDtypeStruct((t, w), BF16), jax.ShapeDtypeStruct((t, w), BF16),
                   jax.ShapeDtypeStruct((t, ATTN_WIDTH), BF16)),
        grid=(t // tm,),
        in_specs=[row(d), full(wq), full(wk), full(wv), row(LANES), full(pq), full(pk), full(oq), full(ok)],
        out_specs=(row(w), row(w), row(ATTN_WIDTH)),
        compiler_params=_cparams("parallel"),
    )(h, wq, wk, wv, fcum, pq, pk, oq, ok)


def _fold_lanes(x, op):
    out = x[:, :LANES]
    for g in range(1, x.shape[1] // LANES):
        out = op(out, x[:, g * LANES:(g + 1) * LANES])
    return out


def _attn_fwd(qa, ka, v, seq, tq):
    t = qa.shape[0]
    nq = seq // tq
    hp_n = N_HEADS // 2
    heads = [slice(e * LANES, (e + 1) * LANES) for e in range(2)]

    def body(q_ref, k_ref, v_ref, o_ref, lse_ref, s_buf):
        i = pl.program_id(2)
        diag_ok = lax.broadcasted_iota(jnp.int32, (tq, tq), 0) >= lax.broadcasted_iota(jnp.int32, (tq, tq), 1)
        qs = [q_ref[:, hl] for hl in heads]

        def sweep1(j, mxs):
            r0 = pl.multiple_of(j * tq, tq)
            out = []
            for e, hl in enumerate(heads):
                s = _dot_nt(qs[e], k_ref[pl.ds(r0, tq), hl])
                s = jnp.where(jnp.logical_or(diag_ok, j < i), s, NEG_BIG)
                s_buf[e, j] = s
                out.append(jnp.maximum(mxs[e], _fold_lanes(s, jnp.maximum)))
            return tuple(out)

        mxs = lax.fori_loop(0, i + 1, sweep1, (jnp.full((tq, LANES), NEG_BIG, F32),) * 2)
        ms = [jnp.max(mx, axis=1, keepdims=True) for mx in mxs]

        def sweep2(j, carry):
            r0 = pl.multiple_of(j * tq, tq)
            vv = v_ref[pl.ds(r0, tq), :]
            out = []
            for e in range(2):
                p = jnp.exp(s_buf[e, j] - ms[e])
                out += [carry[2 * e] + _fold_lanes(p, jnp.add), carry[2 * e + 1] + _dot(p.astype(BF16), vv)]
            return tuple(out)

        res = lax.fori_loop(0, i + 1, sweep2, (jnp.zeros((tq, LANES), F32),) * 4)
        outs = []
        for e in range(2):
            l = jnp.sum(res[2 * e], axis=1, keepdims=True)
            outs.append(res[2 * e + 1] / l)
            lse_ref[:, e:e + 1] = ms[e] + jnp.log(l)
        lane = lax.broadcasted_iota(jnp.int32, (tq, LANES), 1)
        o_ref[...] = jnp.where(lane < HEAD_DIM, outs[0], outs[1])

    return pl.pallas_call(
        body, name="attn_fwd",
        out_shape=(jax.ShapeDtypeStruct((t, ATTN_WIDTH), F32), jax.ShapeDtypeStruct((hp_n, t, 2), F32)),
        grid=(t // seq, hp_n, nq),
        in_specs=[pl.BlockSpec((tq, 2 * LANES), lambda b, hp, i: (b * nq + i, hp)),
                  pl.BlockSpec((seq, 2 * LANES), lambda b, hp, i: (b, hp)),
                  pl.BlockSpec((seq, LANES), lambda b, hp, i: (b, hp))],
        out_specs=(pl.BlockSpec((tq, LANES), lambda b, hp, i: (b * nq + i, hp)),
                   pl.BlockSpec((None, tq, 2), lambda b, hp, i: (hp, b * nq + i, 0))),
        scratch_shapes=[pltpu.VMEM((2, nq, tq, tq), F32)],
        compiler_params=_cparams("parallel", "parallel", "arbitrary"),
    )(qa, ka, v)


def _merge_fwd(x, ps, o, g2, wpo, wao, wout, tm):
    t, d = x.shape
    tm = min(tm, t)

    def body(x_ref, ps_ref, o_ref, gp_ref, ga_ref, wpo_ref, wao_ref, wout_ref, mg_ref, x1_ref):
        py = _dot(ps_ref[...], wpo_ref[...])
        ay = _dot(o_ref[...].astype(BF16), wao_ref[...])
        mb = (_sigmoid(gp_ref[...]) * py + _sigmoid(ga_ref[...]) * ay).astype(BF16)
        mg_ref[...] = mb
        x1_ref[...] = x_ref[...] + _dot(mb, wout_ref[...])

    row = lambda w: pl.BlockSpec((tm, w), lambda i: (i, 0))
    full = lambda a: pl.BlockSpec(a.shape, lambda i: (0, 0))
    return pl.pallas_call(
        body, name="merge_fwd",
        out_shape=(jax.ShapeDtypeStruct((t, d), BF16), jax.ShapeDtypeStruct((t, d), F32)),
        grid=(t // tm,),
        in_specs=[row(d), row(POOL_WIDTH), row(ATTN_WIDTH), pl.BlockSpec((tm, d), lambda i: (i, 0)),
                  pl.BlockSpec((tm, d), lambda i: (i, 1)), full(wpo), full(wao), full(wout)],
        out_specs=(row(d), row(d)),
        compiler_params=_cparams("parallel"),
    )(x, ps, o, g2, g2, wpo, wao, wout)


def _ffn_fwd(x1, g, wg, wu, wd, tm, tf):
    t, d = x1.shape
    f = wg.shape[0]
    tm = min(tm, t)
    nf = f // tf

    def body(x1_ref, g_ref, wg_ref, wu_ref, wd_ref, h2_ref, gt_ref, up_ref, act_ref, x2_ref):
        j = pl.program_id(1)

        @pl.when(j == 0)
        def _():
            h2_ref[...] = _rms_fwd(x1_ref[...], g_ref[...]).astype(BF16)

        h2 = h2_ref[...]
        gt = _dot_nt(h2, wg_ref[...])
        up = _dot_nt(h2, wu_ref[...])
        act = (gt * _sigmoid(gt) * up).astype(BF16)
        gt_ref[...] = gt
        up_ref[...] = up
        act_ref[...] = act
        prod = _dot(act, wd_ref[...])

        @pl.when(j == 0)
        def _():
            x2_ref[...] = prod

        @pl.when(j > 0)
        def _():
            x2_ref[...] += prod

        @pl.when(j == nf - 1)
        def _():
            x2_ref[...] += x1_ref[...]

    return pl.pallas_call(
        body, name="ffn_fwd",
        out_shape=(jax.ShapeDtypeStruct((t, d), BF16), jax.ShapeDtypeStruct((t, f), F32),
                   jax.ShapeDtypeStruct((t, f), F32), jax.ShapeDtypeStruct((t, f), BF16),
                   jax.ShapeDtypeStruct((t, d), F32)),
        grid=(t // tm, nf),
        in_specs=[pl.BlockSpec((tm, d), lambda i, j: (i, 0)), pl.BlockSpec((1, d), lambda i, j: (0, 0)),
                  pl.BlockSpec((tf, d), lambda i, j: (j, 0)), pl.BlockSpec((tf, d), lambda i, j: (j, 0)),
                  pl.BlockSpec((tf, d), lambda i, j: (j, 0))],
        out_specs=(pl.BlockSpec((tm, d), lambda i, j: (i, 0)), pl.BlockSpec((tm, tf), lambda i, j: (i, j)),
                   pl.BlockSpec((tm, tf), lambda i, j: (i, j)), pl.BlockSpec((tm, tf), lambda i, j: (i, j)),
                   pl.BlockSpec((tm, d), lambda i, j: (i, 0))),
        compiler_params=_cparams("parallel", "arbitrary"),
    )(x1, g, wg, wu, wd)


def _final_fwd_bwd(x2, target, g, tm):
    t, d = x2.shape
    tm = min(tm, t)

    def body(x_ref, t_ref, g_ref, loss_ref, dx_ref, dg_ref):
        i = pl.program_id(0)
        x = x_ref[...]
        gg = g_ref[...]
        err = _rms_fwd(x, gg) - t_ref[...]
        part = 0.5 * jnp.sum(jnp.mean(err * err, axis=-1, keepdims=True), axis=0, keepdims=True)
        dx, dg = _rms_bwd(x, gg, err * (1.0 / d))
        dx_ref[...] = dx

        @pl.when(i == 0)
        def _():
            loss_ref[...] = jnp.zeros_like(loss_ref)
            dg_ref[...] = jnp.zeros_like(dg_ref)

        loss_ref[...] += jnp.broadcast_to(part, loss_ref.shape)
        dg_ref[...] += dg

    return pl.pallas_call(
        body, name="final_fwd_bwd",
        out_shape=(jax.ShapeDtypeStruct((1, LANES), F32), jax.ShapeDtypeStruct((t, d), F32),
                   jax.ShapeDtypeStruct((1, d), F32)),
        grid=(t // tm,),
        in_specs=[pl.BlockSpec((tm, d), lambda i: (i, 0)), pl.BlockSpec((tm, d), lambda i: (i, 0)),
                  pl.BlockSpec((1, d), lambda i: (0, 0))],
        out_specs=(pl.BlockSpec((1, LANES), lambda i: (0, 0)), pl.BlockSpec((tm, d), lambda i: (i, 0)),
                   pl.BlockSpec((1, d), lambda i: (0, 0))),
        compiler_params=_cparams("arbitrary"),
    )(x2, target, g)


def _ffn_bwd(dx2, x1, g, gt, up, wg, wu, wd, tm, tf):
    t, d = dx2.shape
    f = gt.shape[1]
    tm = min(tm, t)
    nf = f // tf

    def body(dx2_ref, x1_ref, g_ref, gt_ref, up_ref, wg_ref, wu_ref, wd_ref, dgt_ref, dup_ref, dx1_ref, dg_ref, acc_ref):
        i, j = pl.program_id(0), pl.program_id(1)
        dact = _dot_nt(dx2_ref[...].astype(BF16), wd_ref[...])
        gtv = gt_ref[...]
        sg = _sigmoid(gtv)
        dup = (dact * (gtv * sg)).astype(BF16)
        dgt = (dact * up_ref[...] * (sg * (1.0 + gtv * (1.0 - sg)))).astype(BF16)
        dgt_ref[...] = dgt
        dup_ref[...] = dup
        contrib = _dot(dgt, wg_ref[...]) + _dot(dup, wu_ref[...])

        @pl.when(j == 0)
        def _():
            acc_ref[...] = contrib

        @pl.when(j > 0)
        def _():
            acc_ref[...] += contrib

        @pl.when(jnp.logical_and(i == 0, j == 0))
        def _():
            dg_ref[...] = jnp.zeros_like(dg_ref)

        @pl.when(j == nf - 1)
        def _():
            dxn, dg = _rms_bwd(x1_ref[...], g_ref[...], acc_ref[...])
            dx1_ref[...] = dx2_ref[...] + dxn
            dg_ref[...] += dg

    return pl.pallas_call(
        body, name="ffn_bwd",
        out_shape=(jax.ShapeDtypeStruct((t, f), BF16), jax.ShapeDtypeStruct((t, f), BF16),
                   jax.ShapeDtypeStruct((t, d), F32), jax.ShapeDtypeStruct((1, d), F32)),
        grid=(t // tm, nf),
        in_specs=[pl.BlockSpec((tm, d), lambda i, j: (i, 0)), pl.BlockSpec((tm, d), lambda i, j: (i, 0)),
                  pl.BlockSpec((1, d), lambda i, j: (0, 0)),
                  pl.BlockSpec((tm, tf), lambda i, j: (i, j)), pl.BlockSpec((tm, tf), lambda i, j: (i, j)),
                  pl.BlockSpec((tf, d), lambda i, j: (j, 0)), pl.BlockSpec((tf, d), lambda i, j: (j, 0)),
                  pl.BlockSpec((tf, d), lambda i, j: (j, 0))],
        out_specs=(pl.BlockSpec((tm, tf), lambda i, j: (i, j)), pl.BlockSpec((tm, tf), lambda i, j: (i, j)),
                   pl.BlockSpec((tm, d), lambda i, j: (i, 0)), pl.BlockSpec((1, d), lambda i, j: (0, 0))),
        scratch_shapes=[pltpu.VMEM((tm, d), F32)],
        compiler_params=_cparams("arbitrary", "arbitrary"),
    )(dx2, x1, g, gt, up, wg, wu, wd)


def _merge_bwd(dx1, ps, o, g2, wpo, wao, wout, tm):
    t, d = dx1.shape
    tm = min(tm, t)

    def body(dx1_ref, ps_ref, o_ref, gp_ref, ga_ref, wpo_ref, wao_ref, wout_ref, dpy_ref, day_ref, dg2_ref, dps_ref, da_ref):
        dm = _dot_nt(dx1_ref[...].astype(BF16), wout_ref[...])
        py = _dot(ps_ref[...], wpo_ref[...])
        ay = _dot(o_ref[...].astype(BF16), wao_ref[...])
        sp = _sigmoid(gp_ref[...])
        sa = _sigmoid(ga_ref[...])
        dpy = (dm * sp).astype(BF16)
        day = (dm * sa).astype(BF16)
        dpy_ref[...] = dpy
        day_ref[...] = day
        dg2_ref[:, :d] = (dm * py * (sp * (1.0 - sp))).astype(BF16)
        dg2_ref[:, d:] = (dm * ay * (sa * (1.0 - sa))).astype(BF16)
        dps_ref[...] = _dot_nt(dpy, wpo_ref[...])
        da_ref[...] = _dot_nt(day, wao_ref[...]).astype(BF16)

    row = lambda w: pl.BlockSpec((tm, w), lambda i: (i, 0))
    full = lambda a: pl.BlockSpec(a.shape, lambda i: (0, 0))
    return pl.pallas_call(
        body, name="merge_bwd",
        out_shape=(jax.ShapeDtypeStruct((t, d), BF16), jax.ShapeDtypeStruct((t, d), BF16),
                   jax.ShapeDtypeStruct((t, 2 * d), BF16), jax.ShapeDtypeStruct((t, POOL_WIDTH), F32),
                   jax.ShapeDtypeStruct((t, ATTN_WIDTH), BF16)),
        grid=(t // tm,),
        in_specs=[row(d), row(POOL_WIDTH), row(ATTN_WIDTH), pl.BlockSpec((tm, d), lambda i: (i, 0)),
                  pl.BlockSpec((tm, d), lambda i: (i, 1)), full(wpo), full(wao), full(wout)],
        out_specs=(row(d), row(d), row(2 * d), row(POOL_WIDTH), row(ATTN_WIDTH)),
        compiler_params=_cparams("parallel"),
    )(dx1, ps, o, g2, g2, wpo, wao, wout)


def _attn_bwd(qa, ka, v, do, lse4, seq, tq):
    t = qa.shape[0]
    nq = seq // tq
    hp_n = N_HEADS // 2
    heads = [slice(e * LANES, (e + 1) * LANES) for e in range(2)]

    def body(q_ref, k_ref, v_ref, do_ref, lse_ref, dq_ref, dk_ref, dv_ref, dfr_ref, dk_acc, dv_acc, p_buf, dp_buf):
        diag_ok = lax.broadcasted_iota(jnp.int32, (tq, tq), 0) >= lax.broadcasted_iota(jnp.int32, (tq, tq), 1)
        lane_q = lax.broadcasted_iota(jnp.int32, (tq, LANES), 1)
        lane_s = lax.broadcasted_iota(jnp.int32, (seq, LANES), 1)
        mine_q = [lane_q < HEAD_DIM, lane_q >= HEAD_DIM]
        dv_acc[...] = jnp.zeros_like(dv_acc)
        dk_acc[...] = jnp.zeros_like(dk_acc)
        dfr_ref[...] = jnp.zeros_like(dfr_ref)

        def q_step(i, _):
            q0 = pl.multiple_of(i * tq, tq)
            qs = [q_ref[pl.ds(q0, tq), hl] for hl in heads]
            dov = do_ref[pl.ds(q0, tq), :]
            dos = [jnp.where(mq, dov, jnp.zeros((), BF16)) for mq in mine_q]
            lss = [lse_ref[pl.ds(q0, tq), e:e + 1] for e in range(2)]

            def sweep1(j, dls):
                r0 = pl.multiple_of(j * tq, tq)
                vv = v_ref[pl.ds(r0, tq), :]
                out = []
                for e, hl in enumerate(heads):
                    s = _dot_nt(qs[e], k_ref[pl.ds(r0, tq), hl])
                    s = jnp.where(jnp.logical_or(diag_ok, j < i), s, NEG_BIG)
                    p = jnp.exp(s - lss[e])
                    dp = _dot_nt(dos[e], vv)
                    p_buf[e, j] = p
                    dp_buf[e, j] = dp
                    dv_acc[pl.ds(r0, tq), :] += _dot_tn(p.astype(BF16), dos[e])
                    out.append(dls[e] + _fold_lanes(p * dp, jnp.add))
                return tuple(out)

            dls = lax.fori_loop(0, i + 1, sweep1, (jnp.zeros((tq, LANES), F32),) * 2)
            dls = [jnp.sum(d, axis=1, keepdims=True) for d in dls]

            def sweep2(j, dqs):
                r0 = pl.multiple_of(j * tq, tq)
                out = []
                for e, hl in enumerate(heads):
                    ds = p_buf[e, j] * (dp_buf[e, j] - dls[e])
                    dfr_ref[e, pl.ds(j, 1), :] += jnp.sum(ds, axis=0, keepdims=True)
                    dsb = ds.astype(BF16)
                    dk_acc[e, pl.ds(r0, tq), :] += _dot_tn(dsb, qs[e])
                    out.append(dqs[e] + _dot(dsb, k_ref[pl.ds(r0, tq), hl]))
                return tuple(out)

            dqs = lax.fori_loop(0, i + 1, sweep2, (jnp.zeros((tq, LANES), F32),) * 2)
            dq = jnp.where(mine_q[0], dqs[0], pltpu.roll(dqs[1], HEAD_DIM, 1)) * ATTN_SCALE
            dq_ref[pl.ds(q0, tq), :] = dq.astype(BF16)
            return 0

        lax.fori_loop(0, nq, q_step, 0)
        dk_ref[...] = jnp.where(lane_s < HEAD_DIM, dk_acc[0], pltpu.roll(dk_acc[1], HEAD_DIM, 1)).astype(BF16)
        dv_ref[...] = dv_acc[...].astype(BF16)

    wide = pl.BlockSpec((seq, 2 * LANES), lambda b, hp: (b, hp))
    col = pl.BlockSpec((seq, LANES), lambda b, hp: (b, hp))
    pair = pl.BlockSpec((None, seq, 2), lambda b, hp: (hp, b, 0))
    return pl.pallas_call(
        body, name="attn_bwd",
        out_shape=(jax.ShapeDtypeStruct((t, ATTN_WIDTH), BF16),) * 3 + (jax.ShapeDtypeStruct((N_HEADS, t // tq, tq), F32),),
        grid=(t // seq, hp_n),
        in_specs=[wide, wide, col, col, pair],
        out_specs=(col, col, col, pl.BlockSpec((2, nq, tq), lambda b, hp: (hp, b, 0))),
        scratch_shapes=[pltpu.VMEM((2, seq, LANES), F32), pltpu.VMEM((seq, LANES), F32),
                        pltpu.VMEM((2, nq, tq, tq), F32), pltpu.VMEM((2, nq, tq, tq), F32)],
        compiler_params=_cparams("parallel", "arbitrary"),
    )(qa, ka, v, do, lse4)


def _forget_bwd(dfc, fl, bf, seq):
    t = fl.shape[0]
    cb = min(256, seq)
    nb = seq // cb

    def body(dfc_ref, fl_ref, bf_ref, dfl_ref, db_ref):
        b = pl.program_id(0)
        ri = lax.broadcasted_iota(jnp.int32, (cb, cb), 0)
        ci = lax.broadcasted_iota(jnp.int32, (cb, cb), 1)
        tri = (ci >= ri).astype(BF16)
        carry = jnp.zeros((1, LANES), F32)
        dbs = jnp.zeros((1, LANES), F32)
        for blk in reversed(range(nb)):
            rs = slice(blk * cb, (blk + 1) * cb)
            dlf = _tri_dot(tri, -dfc_ref[rs, :]) + carry
            carry = dlf[0:1, :]
            dfl = dlf * _sigmoid(-(fl_ref[rs, :] + bf_ref[...]))
            dfl_ref[rs, :] = dfl.astype(BF16)
            dbs = dbs + jnp.sum(dfl, axis=0, keepdims=True)

        @pl.when(b == 0)
        def _():
            db_ref[...] = jnp.zeros_like(db_ref)

        db_ref[...] += dbs

    return pl.pallas_call(
        body, name="forget_bwd",
        out_shape=(jax.ShapeDtypeStruct((t, LANES), BF16), jax.ShapeDtypeStruct((1, LANES), F32)),
        grid=(t // seq,),
        in_specs=[pl.BlockSpec((seq, LANES), lambda b: (b, 0)), pl.BlockSpec((seq, LANES), lambda b: (b, 0)),
                  pl.BlockSpec((1, LANES), lambda b: (0, 0))],
        out_specs=(pl.BlockSpec((seq, LANES), lambda b: (b, 0)), pl.BlockSpec((1, LANES), lambda b: (0, 0))),
        compiler_params=_cparams("arbitrary"),
    )(dfc, fl, bf)


def _pool_bwd(dps, p, mix, scale, seq):
    t = dps.shape[0]

    def body(dps_ref, p_ref, mix_ref, sc_ref, du_ref, dmix_ref, dsc_ref):
        b = pl.program_id(0)

        @pl.when(b == 0)
        def _():
            dmix_ref[...] = jnp.zeros_like(dmix_ref)
            dsc_ref[...] = jnp.zeros_like(dsc_ref)

        tpos = lax.broadcasted_iota(jnp.int32, (seq, POOL_GROUP_DIM), 0)
        for g in range(POOL_GROUPS):
            sl = slice(g * POOL_GROUP_DIM, (g + 1) * POOL_GROUP_DIM)
            pb = p_ref[:, sl]
            dpsg = dps_ref[:, sl]
            pm = _dot(pb, mix_ref[g])
            dsc_ref[:, sl] += jnp.sum(dpsg * pm, axis=0, keepdims=True)
            dpm = (dpsg * sc_ref[:, sl]).astype(BF16)
            dmix_ref[g] += _dot_tn(pb, dpm)
            dp = _dot_nt(dpm, mix_ref[g])
            cnt = jnp.minimum(tpos + 1, POOL_WINDOWS[g]).astype(F32)
            s = dp / cnt
            for lvl in range(g + 1):
                d = 2 ** lvl
                s = s + jnp.where(tpos < seq - d, pltpu.roll(s, seq - d, 0), 0.0)
            du_ref[:, sl] = (s - dp).astype(BF16)

    return pl.pallas_call(
        body, name="pool_bwd",
        out_shape=(jax.ShapeDtypeStruct((t, POOL_WIDTH), BF16),
                   jax.ShapeDtypeStruct((POOL_GROUPS, POOL_GROUP_DIM, POOL_GROUP_DIM), F32),
                   jax.ShapeDtypeStruct((1, POOL_WIDTH), F32)),
        grid=(t // seq,),
        in_specs=[pl.BlockSpec((seq, POOL_WIDTH), lambda b: (b, 0)), pl.BlockSpec((seq, POOL_WIDTH), lambda b: (b, 0)),
                  pl.BlockSpec((POOL_GROUPS, POOL_GROUP_DIM, POOL_GROUP_DIM), lambda b: (0, 0, 0)),
                  pl.BlockSpec((1, POOL_WIDTH), lambda b: (0, 0))],
        out_specs=(pl.BlockSpec((seq, POOL_WIDTH), lambda b: (b, 0)),
                   pl.BlockSpec((POOL_GROUPS, POOL_GROUP_DIM, POOL_GROUP_DIM), lambda b: (0, 0, 0)),
                   pl.BlockSpec((1, POOL_WIDTH), lambda b: (0, 0))),
        compiler_params=_cparams("arbitrary"),
    )(dps, p, mix, scale)


def _in_bwd(du, dq, dk, dv, dg2, dfl, dx1, x, g, wu, wqkv, wg2, wft, tm):
    t, d = x.shape
    tm = min(tm, t)
    aw = ATTN_WIDTH

    def body(du_ref, dq_ref, dk_ref, dv_ref, dg2_ref, dfl_ref, dx1_ref, x_ref, g_ref, wu_ref, wqkv_ref, wg2_ref, wft_ref,
             dx_ref, dg_ref):
        i = pl.program_id(0)
        dh = _dot_nt(du_ref[...], wu_ref[...])
        dh += _dot_nt(dq_ref[...], wqkv_ref[:, 0:aw])
        dh += _dot_nt(dk_ref[...], wqkv_ref[:, aw:2 * aw])
        dh += _dot_nt(dv_ref[...], wqkv_ref[:, 2 * aw:3 * aw])
        dh += _dot_nt(dg2_ref[...], wg2_ref[...])
        dh += _dot(dfl_ref[...], wft_ref[...])
        dxn, dg = _rms_bwd(x_ref[...], g_ref[...], dh)
        dx_ref[...] = dx1_ref[...] + dxn

        @pl.when(i == 0)
        def _():
            dg_ref[...] = jnp.zeros_like(dg_ref)

        dg_ref[...] += dg

    row = lambda w: pl.BlockSpec((tm, w), lambda i: (i, 0))
    full = lambda a: pl.BlockSpec(a.shape, lambda i: (0, 0))
    return pl.pallas_call(
        body, name="in_bwd",
        out_shape=(jax.ShapeDtypeStruct((t, d), F32), jax.ShapeDtypeStruct((1, d), F32)),
        grid=(t // tm,),
        in_specs=[row(POOL_WIDTH), row(aw), row(aw), row(aw), row(2 * d), row(LANES), row(d), row(d),
                  pl.BlockSpec((1, d), lambda i: (0, 0)), full(wu), full(wqkv), full(wg2), full(wft)],
        out_specs=(row(d), pl.BlockSpec((1, d), lambda i: (0, 0))),
        compiler_params=_cparams("arbitrary"),
    )(du, dq, dk, dv, dg2, dfl, dx1, x, g, wu, wqkv, wg2, wft)


def _position():
    return lax.axis_index("x"), lax.axis_index("y"), lax.axis_index("c")


def _remote(src, dst, send_sem, recv_sem, device):
    return pltpu.make_async_remote_copy(src_ref=src, dst_ref=dst, send_sem=send_sem, recv_sem=recv_sem,
                                        device_id=device, device_id_type=MESH)


HBM = pl.BlockSpec(memory_space=pltpu.HBM)
SEM = pl.BlockSpec(memory_space=pltpu.SEMAPHORE)
DATAFLOW = pltpu.SideEffectType.DATAFLOW_SIDE_EFFECTING


def _copies_start(name, arrays, plan, m):
    n = len(arrays)
    arrays = [pltpu.with_memory_space_constraint(a, pltpu.HBM) for a in arrays]

    def body(*refs):
        ins, send_sem, recv_sem, token = refs[:n], refs[n], refs[n + 1], refs[2 * n + 2]
        for i, (src, dst, device, _) in enumerate(plan(ins, *_position())):
            _remote(src, dst, send_sem.at[i], recv_sem.at[i], device).start()
        token[...] = jnp.zeros_like(token)

    outs = pl.pallas_call(
        body, name=name,
        out_shape=(pltpu.SemaphoreType.DMA((m,)), pltpu.SemaphoreType.DMA((m,)),
                   *[pltpu.HBM(a.shape, a.dtype) for a in arrays], jax.ShapeDtypeStruct((8, LANES), F32)),
        in_specs=[HBM] * n, out_specs=(SEM, SEM, *[HBM] * n, pl.BlockSpec(memory_space=pltpu.VMEM)),
        input_output_aliases={i: i + 2 for i in range(n)},
        compiler_params=pltpu.CompilerParams(has_side_effects=DATAFLOW),
    )(*arrays)
    return (outs[0], outs[1]), list(outs[2:2 + n]), outs[2 + n]


def _copies_wait(name, sems, arrays, plan, after):
    n = len(arrays)

    def body(*refs):
        ins, send_sem, recv_sem = refs[:n], refs[n], refs[n + 1]
        for i, (src, dst, device, landing) in enumerate(plan(ins, *_position())):
            _remote(src, dst, send_sem.at[i], recv_sem.at[i], device).wait_send()
            _remote(landing, landing, send_sem.at[i], recv_sem.at[i], device).wait_recv()

    outs = pl.pallas_call(
        body, name=name,
        out_shape=tuple(pltpu.HBM(a.shape, a.dtype) for a in arrays),
        in_specs=[HBM] * n + [SEM, SEM, ANY], out_specs=tuple([HBM] * n),
        input_output_aliases={i: i for i in range(n)},
        compiler_params=pltpu.CompilerParams(has_side_effects=DATAFLOW),
    )(*arrays, sems[0], sems[1], after)
    return list(outs)


def _tie(x, token):
    return x if token is None else lax.optimization_barrier((x, token))[0]


def _other_chips(x, y):
    return [(1 - x, y), (x, 1 - y), (1 - x, 1 - y)]


def _gather_begin(tag, shards, token):
    n = len(shards)
    lands = [lax.empty((N_CHIPS,) + s.shape, s.dtype) for s in shards]

    def plan(refs, x, y, c):
        return [(refs[k].at[c], refs[n + k].at[2 * x + y, c], (ox, oy, c), refs[n + k].at[2 * ox + oy, c])
                for k in range(n) for ox, oy in _other_chips(x, y)]

    arrays = [_tie(a, token) for a in shards] + lands
    sems, thru, token = _copies_start(f"gather_{tag}_ici_start", arrays, plan, 3 * n)
    return dict(tag=tag, n=n, plan=plan, sems=sems, arrays=thru, token=token)


def _gather_forward(st, after):
    n, tag = st["n"], st["tag"]
    thru = _copies_wait(f"gather_{tag}_ici_wait", st["sems"], st["arrays"], st["plan"], after)

    def plan(refs, x, y, c):
        return [(refs[k].at[2 * ox + oy, c], refs[k].at[2 * ox + oy, c], (x, y, 1 - c), refs[k].at[2 * ox + oy, 1 - c])
                for k in range(n) for ox, oy in _other_chips(x, y)]

    sems, lands, token = _copies_start(f"gather_{tag}_fwd_start", thru[n:], plan, 3 * n)
    return dict(tag=tag, n=n, plan=plan, sems=sems, arrays=lands, token=token, shards=thru[:n])


def _gather_end(st, after):
    lands = _copies_wait(f"gather_{st['tag']}_fwd_wait", st["sems"], st["arrays"], st["plan"], after)
    me = 2 * lax.axis_index("x") + lax.axis_index("y")
    return [lax.dynamic_update_index_in_dim(g, s, me, 0) for g, s in zip(lands, st["shards"])]


def _add_keep_give(name, pos, a, a_keep, a_give, b, b_keep, b_give, steps):
    r, c = a.shape[-2:]

    def spec(arr, fn):
        lead = arr.ndim - 2
        return pl.BlockSpec((None,) * lead + (r, c), lambda i, p: tuple(fn(i, p)) + (0, 0))

    out_spec = pl.BlockSpec((None, r, c), lambda i, p: (i, 0, 0))

    def body(p_ref, ak_ref, bk_ref, ag_ref, bg_ref, keep_ref, give_ref):
        keep_ref[...] = ak_ref[...] + bk_ref[...].astype(F32)
        give_ref[...] = (ag_ref[...] + bg_ref[...].astype(F32)).astype(BF16)

    return pl.pallas_call(
        body, name=name,
        out_shape=(jax.ShapeDtypeStruct((steps, r, c), F32), jax.ShapeDtypeStruct((steps, r, c), BF16)),
        grid_spec=pltpu.PrefetchScalarGridSpec(
            num_scalar_prefetch=1, grid=(steps,),
            in_specs=[spec(a, a_keep), spec(b, b_keep), spec(a, a_give), spec(b, b_give)],
            out_specs=(out_spec, out_spec)),
        compiler_params=_cparams("parallel"),
    )(pos, a, b, a, b)


def _add_last(name, a, b):
    _, r, c = a.shape
    blk = pl.BlockSpec((None, r, c), lambda i: (0, 0, 0))

    def body(a_ref, b_ref, o_ref):
        o_ref[...] = a_ref[...] + b_ref[...].astype(F32)

    return pl.pallas_call(
        body, name=name, out_shape=jax.ShapeDtypeStruct((r, c), F32), grid=(1,), in_specs=[blk, blk],
        out_specs=pl.BlockSpec((r, c), lambda i: (0, 0)), compiler_params=_cparams("arbitrary"),
    )(a, b)


def _exchange_begin(tag, stage, gives, lands, peer_fn, extra):
    n = len(gives)

    def plan(refs, x, y, c):
        return [(refs[k], refs[n + k], peer_fn(x, y, c), refs[n + k]) for k in range(n)]

    sems, thru, token = _copies_start(f"rs{tag}_{stage}_start", gives + lands, plan, n)
    return dict(extra, tag=tag, n=n, stage=stage, plan=plan, sems=sems, arrays=thru, token=token)


def _reduce_begin(tag, grads):
    n = len(grads)
    lands = [lax.empty((N_CHIPS,) + g.shape[2:], F32) for g in grads]

    def plan(refs, x, y, c):
        return [(refs[k].at[j, 1 - c], refs[n + k].at[j], (x, y, 1 - c), refs[n + k].at[j])
                for k in range(n) for j in range(N_CHIPS)]

    sems, thru, token = _copies_start(f"rs{tag}_c_start", list(grads) + lands, plan, N_CHIPS * n)
    return dict(tag=tag, n=n, stage="c", plan=plan, sems=sems, arrays=thru, token=token)


def _reduce_advance(st, after):
    tag, n, stage = st["tag"], st["n"], st["stage"]
    thru = _copies_wait(f"rs{tag}_{stage}_wait", st["sems"], st["arrays"], st["plan"], after)
    first, recv = thru[:n], thru[n:]
    x, y, c = _position()
    if stage == "c":
        pos = jnp.stack([c, x]).astype(jnp.int32)
        sums = [_add_keep_give(
            f"rs{tag}_c_add{k}", pos,
            first[k], lambda i, p: (2 * p[1] + i, p[0]), lambda i, p: (2 * (1 - p[1]) + i, p[0]),
            recv[k], lambda i, p: (2 * p[1] + i,), lambda i, p: (2 * (1 - p[1]) + i,), 2) for k in range(n)]
        lands = [lax.empty(s[1].shape, BF16) for s in sums]
        return _exchange_begin(tag, "x", [s[1] for s in sums], lands, lambda x, y, c: (1 - x, y, c),
                               dict(keep=[s[0] for s in sums]))
    if stage == "x":
        pos = jnp.stack([y]).astype(jnp.int32)
        sums = [_add_keep_give(
            f"rs{tag}_x_add{k}", pos,
            st["keep"][k], lambda i, p: (p[0],), lambda i, p: (1 - p[0],),
            recv[k], lambda i, p: (p[0],), lambda i, p: (1 - p[0],), 1) for k in range(n)]
        lands = [lax.empty(s[1].shape, BF16) for s in sums]
        return _exchange_begin(tag, "y", [s[1] for s in sums], lands, lambda x, y, c: (x, 1 - y, c),
                               dict(keep=[s[0] for s in sums]))
    if stage == "y":
        mine = [_add_last(f"rs{tag}_y_add{k}", st["keep"][k], recv[k]) for k in range(n)]
        lands = [lax.empty(m.shape, F32) for m in mine]
        return _exchange_begin(tag, "swap", mine, lands, lambda x, y, c: (x, y, 1 - c), {})
    return dict(done=list(zip(first, recv)), token=None)


def _all_reduce_small(v):
    r = v.shape[0]

    def body(v_ref, out_ref, buf, send_sems, recv_sems, local_sem):
        x, y, c = _position()
        me, sibling = (x, y, c), (x, y, 1 - c)
        chips = [(1 - x, y), (x, 1 - y), (1 - x, 1 - y)]

        def rows(px, py, pc):
            return buf.at[pl.ds((4 * px + 2 * py + pc) * r, r), :]

        def copy(k, block, to, src=None):
            return _remote(rows(*block) if src is None else src, rows(*block), send_sems.at[k], recv_sems.at[k], to)

        mine = pltpu.make_async_copy(v_ref, rows(*me), local_sem)
        mine.start()
        first = [copy(0, me, sibling, src=v_ref)]
        first += [copy(1 + j, me, (*chip, c), src=v_ref) for j, chip in enumerate(chips)]
        for cp in first:
            cp.start()
        passed = [copy(4 + j, (*chip, c), sibling) for j, chip in enumerate(chips)]
        for j, chip in enumerate(chips):
            copy(1 + j, (*chip, c), me).wait_recv()
            passed[j].start()
        copy(0, sibling, me).wait_recv()
        for j, chip in enumerate(chips):
            copy(4 + j, (*chip, 1 - c), me).wait_recv()
        for cp in first + passed:
            cp.wait_send()
        mine.wait()
        acc = buf[0:r, :]
        for dev in range(1, N_DEV):
            acc = acc + buf[dev * r:(dev + 1) * r, :]
        out_ref[...] = acc

    return pl.pallas_call(
        body, name="all_reduce_small",
        out_shape=jax.ShapeDtypeStruct(v.shape, F32),
        in_specs=[pl.BlockSpec(memory_space=pltpu.VMEM)],
        out_specs=pl.BlockSpec(memory_space=pltpu.VMEM),
        scratch_shapes=[pltpu.VMEM((N_DEV * r, LANES), F32), pltpu.SemaphoreType.DMA((7,)),
                        pltpu.SemaphoreType.DMA((7,)), pltpu.SemaphoreType.DMA],
        compiler_params=pltpu.CompilerParams(has_side_effects=True, vmem_limit_bytes=VMEM_LIMIT_V7X),
    )(v)


def _adamw_update(w, gg, m, v):
    mn = ADAM_B1 * m + (1.0 - ADAM_B1) * gg
    vn = ADAM_B2 * v + (1.0 - ADAM_B2) * (gg * gg)
    m_hat = mn / (1.0 - ADAM_B1 ** ADAM_STEP)
    v_hat = vn / (1.0 - ADAM_B2 ** ADAM_STEP)
    return -ADAM_LR * (m_hat / (jnp.sqrt(v_hat) + ADAM_EPS) + ADAM_WD * w), mn, vn


def _adamw(name, w, g, m, v):
    def body(w_ref, g_ref, m_ref, v_ref, d_ref, mo_ref, vo_ref):
        d_ref[...], mo_ref[...], vo_ref[...] = _adamw_update(w_ref[...], g_ref[...], m_ref[...], v_ref[...])

    blk = pl.BlockSpec(w.shape, lambda i: (0, 0))
    return pl.pallas_call(
        body, name=name, out_shape=(jax.ShapeDtypeStruct(w.shape, F32),) * 3, grid=(1,),
        in_specs=[blk] * 4, out_specs=(blk,) * 3, compiler_params=_cparams("arbitrary"),
    )(w, g, m, v)


def _adamw_halves(name, pos_c, w, g_mine, g_other, m, v, tr):
    r, c = w.shape
    rh = r // 2
    tr = tr if rh % tr == 0 else rh
    nt = rh // tr

    def body(p_ref, w_ref, gm_ref, go_ref, m_ref, v_ref, g_ref, d_ref, mo_ref, vo_ref):
        gg = jnp.where(pl.program_id(0) == p_ref[0], gm_ref[...], go_ref[...])
        g_ref[...] = gg
        d_ref[...], mo_ref[...], vo_ref[...] = _adamw_update(w_ref[...], gg, m_ref[...], v_ref[...])

    full = pl.BlockSpec((tr, c), lambda h, i, p: (h * nt + i, 0))
    half = pl.BlockSpec((tr, c), lambda h, i, p: (i, 0))
    return pl.pallas_call(
        body, name=name, out_shape=(jax.ShapeDtypeStruct((r, c), F32),) * 4,
        grid_spec=pltpu.PrefetchScalarGridSpec(
            num_scalar_prefetch=1, grid=(2, nt),
            in_specs=[full, half, half, full, full], out_specs=(full,) * 4),
        compiler_params=_cparams("parallel", "parallel"),
    )(pos_c, w, g_mine, g_other, m, v)


def _col_sharded_to_comm(g):
    k, n = g.shape
    return g.reshape(2, k // 2, N_CHIPS, n // N_CHIPS).transpose(2, 0, 1, 3)


def _row_sharded_to_comm(g):
    r, c = g.shape
    return g.reshape(N_CHIPS, 2, r // (2 * N_CHIPS), c)


def _col_sharded_full(g):
    _, _, rh, c = g.shape
    return g.reshape(N_CHIPS, 2 * rh, c).transpose(1, 0, 2).reshape(2 * rh, N_CHIPS * c)


def _row_sharded_full(g):
    _, _, rh, c = g.shape
    return g.reshape(N_CHIPS * 2 * rh, c)


def _pack_small(g1, bfv, mix, scale, g2n, gf, extra=None):
    row8 = jnp.pad(bfv.reshape(1, N_HEADS), ((0, 0), (0, LANES - N_HEADS)))
    if extra is not None:
        row8 = row8 + jnp.pad(extra[:, :1], ((0, 0), (N_HEADS, LANES - N_HEADS - 1)))
    return jnp.concatenate([
        g1.reshape(8, LANES), jnp.pad(row8, ((0, 7), (0, 0))), mix.reshape(512, LANES),
        jnp.pad(scale.reshape(4, LANES), ((0, 4), (0, 0))), g2n.reshape(8, LANES), gf.reshape(8, LANES)], axis=0)


def _unpack_small(s, like):
    g1, bfv, mix, scale, g2n, gf = like
    return (s[0:8].reshape(g1.shape), s[8, :N_HEADS].reshape(bfv.shape), s[16:528].reshape(mix.shape),
            s[528:532].reshape(scale.shape), s[536:544].reshape(g2n.shape), s[544:552].reshape(gf.shape))


class _MeshLinks:
    def __init__(self, shards_in, shards_rest):
        self.gin = _gather_begin("in", shards_in, None)
        self.grest = _gather_begin("rest", shards_rest, self.gin["token"])
        self.token = self.grest["token"]
        self.groups = {}

    def tie(self, x):
        return _tie(x, self.token)

    def weights_in(self, after):
        st = _gather_forward(self.gin, after)
        (g,) = _gather_end(st, st["token"])
        return _col_sharded_full(g)

    def rest_forward(self, after):
        self.grest = _gather_forward(self.grest, after)
        self.token = self.grest["token"]

    def weights_rest(self, after):
        g = _gather_end(self.grest, after)
        return [_col_sharded_full(g[0]), _col_sharded_full(g[1])] + [_row_sharded_full(a) for a in g[2:]]

    def reduce_begin(self, tag, grads):
        self.groups[tag] = _reduce_begin(tag, grads)
        self.token = self.groups[tag]["token"]

    def advance(self, after):
        for tag, st in self.groups.items():
            if "done" not in st:
                self.groups[tag] = _reduce_advance(st, after)
                if self.groups[tag]["token"] is not None:
                    self.token = self.groups[tag]["token"]

    def reduced(self, tag):
        return self.groups[tag]["done"]


class _NoLinks:
    def __init__(self, w_in, rest):
        self.w_in, self.rest, self.grads = w_in, rest, {}

    def tie(self, x):
        return x

    def weights_in(self, after):
        return self.w_in

    def rest_forward(self, after):
        pass

    def weights_rest(self, after):
        return self.rest

    def reduce_begin(self, tag, grads):
        self.grads[tag] = grads

    def advance(self, after):
        pass


def _local_step(links, x, target, seq, norm1_g, b_forget, pool_mix, pool_scale, norm2_g, norm_f_g):
    t, d = x.shape
    tq = min(256, seq)
    aw = ATTN_WIDTH
    o_q, o_f, o_g = POOL_WIDTH, POOL_WIDTH + 3 * aw, POOL_WIDTH + 3 * aw + N_HEADS
    bf = jnp.pad(b_forget, ((0, 0), (0, LANES - N_HEADS)))
    mixb = pool_mix.astype(BF16)

    h = _norm_fwd("norm1_fwd", links.tie(x), norm1_g, 512)
    w_in = links.weights_in(h)
    wu = w_in[:, :o_q]
    wqkv = w_in[:, o_q:o_f]
    wf = jnp.pad(w_in[:, o_f:o_g], ((0, 0), (0, LANES - N_HEADS)))
    wg2 = w_in[:, o_g:]
    wft = wf.T
    u = _matmul("mm_u", h, wu, "nn", F32, 1024, 512, d)
    g2 = _matmul("mm_gates", h, wg2, "nn", F32, 1024, 512, d)
    fl, fcum = _forget_fwd(h, wf, bf, seq)
    qa, ka, v = _attn_prep(h, _head_blocks(wqkv[:, :aw]), _head_blocks(wqkv[:, aw:2 * aw]), wqkv[:, 2 * aw:], fcum, 512)
    p, ps = _pool_fwd(u, mixb, pool_scale, seq)
    links.rest_forward(ps)
    o, lse = _attn_fwd(links.tie(qa), ka, v, seq, tq)
    w_pool_out, w_attn_out, w_out, w_ffn_gate, w_ffn_up, w_ffn_down = links.weights_rest(o)
    merged, x1 = _merge_fwd(x, ps, o, g2, w_pool_out, w_attn_out, w_out, 256)
    h2, gt, up, act, x2 = _ffn_fwd(x1, norm2_g, w_ffn_gate, w_ffn_up, w_ffn_down, 1024, 256)
    loss, dx2, d_gf = _final_fwd_bwd(x2, target, norm_f_g, 512)

    dgt, dup, dx1, d_g2n = _ffn_bwd(dx2, x1, norm2_g, gt, up, w_ffn_gate, w_ffn_up, w_ffn_down, 1024, 256)
    d_wd = _matmul("dw_down", act, dx2, "tn", F32, 1408, 1024, 512)
    d_wg = _matmul("dw_gate", dgt, h2, "tn", F32, 1408, 1024, 512)
    d_wu = _matmul("dw_up", dup, h2, "tn", F32, 1408, 1024, 512)
    links.reduce_begin("a", [_row_sharded_to_comm(g) for g in (d_wg, d_wu, d_wd)])
    dpy, day, dg2, dps, da = _merge_bwd(links.tie(dx1), ps, o, g2, w_pool_out, w_attn_out, w_out, 256)
    links.advance(dps)
    d_wout = _matmul("dw_out", merged, dx1, "tn", F32, 1024, 1024, 512)
    d_wpo = _matmul("dw_pool_out", ps, dpy, "tn", F32, 512, 1024, 512)
    d_wao = _matmul("dw_attn_out", o, day, "tn", F32, 512, 1024, 512)
    dq, dk, dv, dfr = _attn_bwd(qa, ka, v, links.tie(da), lse, seq, tq)
    links.advance(dq)
    dfc = jnp.pad(dfr.reshape(N_HEADS, t).T, ((0, 0), (0, LANES - N_HEADS)))
    dfl, d_bf = _forget_bwd(dfc, fl, bf, seq)
    du, d_mix, d_scale = _pool_bwd(links.tie(dps), p, mixb, pool_scale, seq)
    d_wu_in = _matmul("dw_in_u", h, du, "tn", F32, 1024, 512, 512)
    d_wq = _matmul("dw_in_q", h, dq, "tn", F32, 1024, 512, 512)
    d_wk = _matmul("dw_in_k", h, dk, "tn", F32, 1024, 512, 512)
    d_wv = _matmul("dw_in_v", h, dv, "tn", F32, 1024, 512, 512)
    d_wf = _matmul("dw_in_f", h, dfl, "tn", F32, 1024, LANES, 512)
    d_wg2 = _matmul("dw_in_gates", h, dg2, "tn", F32, 1024, 1024, 512)
    d_win = jnp.concatenate([d_wu_in, d_wq, d_wk, d_wv, d_wf[:, :N_HEADS], d_wg2], axis=1)
    comm_b = [_col_sharded_to_comm(d_win), _col_sharded_to_comm(d_wpo), _col_sharded_to_comm(d_wao),
              _row_sharded_to_comm(d_wout)]
    links.advance(comm_b[0])
    links.reduce_begin("b", comm_b)
    dx, d_g1 = _in_bwd(du, dq, dk, dv, dg2, dfl, links.tie(dx1), x, norm1_g, wu, wqkv, wg2, wft, 256)
    links.advance(dx)
    small = (d_g1, d_bf[:, :N_HEADS], d_mix, d_scale, d_g2n, d_gf)
    return loss, dx, small


def kernel(x, norm1_g, w_in, b_forget, pool_mix, pool_scale, w_pool_out, w_attn_out, w_out, norm2_g, w_ffn_gate, w_ffn_up, w_ffn_down, norm_f_g, loss_target, m_norm1_g, m_w_in, m_b_forget, m_pool_mix, m_pool_scale, m_w_pool_out, m_w_attn_out, m_w_out, m_norm2_g, m_w_ffn_gate, m_w_ffn_up, m_w_ffn_down, m_norm_f_g, v_norm1_g, v_w_in, v_b_forget, v_pool_mix, v_pool_scale, v_w_pool_out, v_w_attn_out, v_w_out, v_norm2_g, v_w_ffn_gate, v_w_ffn_up, v_w_ffn_down, v_norm_f_g):
    nb, seq, d = x.shape
    group_a = ((w_ffn_gate, m_w_ffn_gate, v_w_ffn_gate, True, 9), (w_ffn_up, m_w_ffn_up, v_w_ffn_up, True, 10),
               (w_ffn_down, m_w_ffn_down, v_w_ffn_down, False, 11))
    group_b = ((w_in, m_w_in, v_w_in, False, 1), (w_pool_out, m_w_pool_out, v_w_pool_out, False, 5),
               (w_attn_out, m_w_attn_out, v_w_attn_out, False, 6), (w_out, m_w_out, v_w_out, False, 7))
    small_w = (norm1_g, b_forget, pool_mix, pool_scale, norm2_g, norm_f_g)
    small_m = (m_norm1_g, m_b_forget, m_pool_mix, m_pool_scale, m_norm2_g, m_norm_f_g)
    small_v = (v_norm1_g, v_b_forget, v_pool_mix, v_pool_scale, v_norm2_g, v_norm_f_g)
    small_pos = (0, 2, 3, 4, 8, 12)
    view = lambda a, tr: a[0].T if tr else a[0]
    unview = lambda a, tr, like: (a.T if tr else a).reshape(like.shape)

    def shard(w, tr):
        lw = view(w, tr).astype(BF16)
        return lw.reshape(2, lw.shape[0] // 2, lw.shape[1])

    links = _MeshLinks([shard(w_in, False)],
                       [shard(w_pool_out, False), shard(w_attn_out, False), shard(w_out, False),
                        shard(w_ffn_gate, True), shard(w_ffn_up, True), shard(w_ffn_down, False)])
    loss, dx, small_g = _local_step(
        links, x.reshape(nb * seq, d), loss_target.reshape(nb * seq, d), seq,
        norm1_g, b_forget, pool_mix[0], pool_scale, norm2_g, norm_f_g.reshape(1, d))

    grads, deltas, new_m, new_v = [None] * 13, [None] * 13, [None] * 13, [None] * 13
    pos_c = jnp.stack([lax.axis_index("c")]).astype(jnp.int32)

    def update(tag, group, tie):
        last = None
        for k, ((w, m, v, tr, pos), (mine, other)) in enumerate(zip(group, links.reduced(tag))):
            outs = _adamw_halves(f"adamw_{tag}{k}", pos_c, tie(view(w, tr)), mine, other, view(m, tr), view(v, tr), 256)
            grads[pos], deltas[pos], new_m[pos], new_v[pos] = (unview(a, tr, w) for a in outs)
            last = outs[1]
        return last

    last = update("a", group_a, links.tie)
    links.advance(last)
    small_sum = _all_reduce_small(links.tie(_pack_small(*small_g, extra=loss)))
    loss_out = small_sum[8, N_HEADS]
    dl, mn, vn = _adamw("adamw_small", _pack_small(*small_w), small_sum * _small_mask(), _pack_small(*small_m),
                        _pack_small(*small_v))
    for pos, g, a, b, e in zip(small_pos, _unpack_small(small_sum, small_w), _unpack_small(dl, small_w),
                               _unpack_small(mn, small_w), _unpack_small(vn, small_w)):
        grads[pos], deltas[pos], new_m[pos], new_v[pos] = g, a, b, e
    links.advance(dl)
    links.advance(links.token)
    update("b", group_b, lambda a: a)

    return (loss_out, dx.reshape(nb, seq, d), *grads, *deltas, *new_m, *new_v)


def _small_mask():
    rows = lax.broadcasted_iota(jnp.int32, (552, LANES), 0)
    lanes = lax.broadcasted_iota(jnp.int32, (552, LANES), 1)
    return jnp.where(jnp.logical_and(rows == 8, lanes == N_HEADS), 0.0, 1.0).astype(F32)
```

```python
import functools

import jax
import jax.numpy as jnp
from jax import lax
from jax.experimental import pallas as pl
from jax.experimental.pallas import tpu as pltpu

F32 = jnp.float32
BF16 = jnp.bfloat16

D_MODEL = 1024
POOL_WINDOWS = (2, 4, 8, 16)
POOL_GROUPS = 4
POOL_GROUP_DIM = 128
POOL_WIDTH = 512
HEAD_DIM = 64
N_HEADS = 8
ATTN_WIDTH = 512
D_FF = 2816
RMS_EPS = 1e-6
ATTN_SCALE = HEAD_DIM ** -0.5
NEG_BIG = -1e30

ADAM_LR = 0.001
ADAM_B1 = 0.9
ADAM_B2 = 0.999
ADAM_EPS = 1e-08
ADAM_WD = 0.01
ADAM_STEP = 10

LANES = 128
N_CHIPS = 4
N_DEV = 8
VMEM_LIMIT_V7X = 52 * 1024 * 1024
MESH = pl.DeviceIdType.MESH
ANY = pl.BlockSpec(memory_space=pl.ANY)


def _cparams(*sem):
    return pltpu.CompilerParams(dimension_semantics=sem if sem else None, vmem_limit_bytes=VMEM_LIMIT_V7X)


def _after(body, n_in, dep):
    if dep is None:
        return body

    def wrapped(*refs):
        body(*refs[:n_in], *refs[n_in + 1:])

    return wrapped


def _dep_args(dep):
    return ([], []) if dep is None else ([ANY], [dep])


def _dot(a, b):
    return lax.dot_general(a, b, (((1,), (0,)), ((), ())), preferred_element_type=F32)


def _dot_nt(a, b):
    return lax.dot_general(a, b, (((1,), (1,)), ((), ())), preferred_element_type=F32)


def _dot_tn(a, b):
    return lax.dot_general(a, b, (((0,), (0,)), ((), ())), preferred_element_type=F32)


def _sigmoid(x):
    return jax.nn.sigmoid(x)


def _rms_fwd(x, g):
    r = lax.rsqrt(jnp.mean(x * x, axis=-1, keepdims=True) + RMS_EPS)
    return (x * r) * g


def _rms_bwd(x, g, dy):
    r = lax.rsqrt(jnp.mean(x * x, axis=-1, keepdims=True) + RMS_EPS)
    xh = x * r
    dg = jnp.sum(dy * xh, axis=0, keepdims=True)
    dxh = dy * g
    dx = r * (dxh - xh * jnp.mean(dxh * xh, axis=-1, keepdims=True))
    return dx, dg


def _matmul(name, a, b, mode, out_dtype, tm, tn, tk):
    if mode == "nn":
        (m, k), (_, n) = a.shape, b.shape
    elif mode == "nt":
        (m, k), (n, _) = a.shape, b.shape
    else:
        (k, m), (_, n) = a.shape, b.shape
    tm, tn, tk = min(tm, m), min(tn, n), min(tk, k)
    assert m % tm == 0 and n % tn == 0 and k % tk == 0, (name, m, n, k, tm, tn, tk)
    nk = k // tk
    if mode == "tn":
        a_spec = pl.BlockSpec((tk, tm), lambda i, j, kk: (kk, i))
    else:
        a_spec = pl.BlockSpec((tm, tk), lambda i, j, kk: (i, kk))
    if mode == "nt":
        b_spec = pl.BlockSpec((tn, tk), lambda i, j, kk: (j, kk))
    else:
        b_spec = pl.BlockSpec((tk, tn), lambda i, j, kk: (kk, j))
    dot = {"nn": _dot, "nt": _dot_nt, "tn": _dot_tn}[mode]
    use_scratch = nk > 1 and out_dtype != F32

    def body(a_ref, b_ref, o_ref, *scratch):
        prod = dot(a_ref[...].astype(BF16), b_ref[...].astype(BF16))
        if nk == 1:
            o_ref[...] = prod.astype(out_dtype)
            return
        acc = scratch[0] if use_scratch else o_ref
        kk = pl.program_id(2)

        @pl.when(kk == 0)
        def _():
            acc[...] = prod

        @pl.when(kk > 0)
        def _():
            acc[...] += prod

        if use_scratch:
            @pl.when(kk == nk - 1)
            def _():
                o_ref[...] = acc[...].astype(out_dtype)

    return pl.pallas_call(
        body,
        name=name,
        out_shape=jax.ShapeDtypeStruct((m, n), out_dtype),
        grid=(m // tm, n // tn, nk),
        in_specs=[a_spec, b_spec],
        out_specs=pl.BlockSpec((tm, tn), lambda i, j, kk: (i, j)),
        scratch_shapes=[pltpu.VMEM((tm, tn), F32)] if use_scratch else [],
        compiler_params=_cparams("parallel", "parallel", "arbitrary"),
    )(a, b)


def _norm_fwd(name, x, g, tm):
    t, d = x.shape
    tm = min(tm, t)

    def body(x_ref, g_ref, h_ref):
        h_ref[...] = _rms_fwd(x_ref[...], g_ref[...]).astype(BF16)

    return pl.pallas_call(
        body, name=name, out_shape=jax.ShapeDtypeStruct((t, d), BF16), grid=(t // tm,),
        in_specs=[pl.BlockSpec((tm, d), lambda i: (i, 0)), pl.BlockSpec((1, d), lambda i: (0, 0))],
        out_specs=pl.BlockSpec((tm, d), lambda i: (i, 0)),
        compiler_params=_cparams("parallel"),
    )(x, g)


def _split3(x):
    hi = x.astype(BF16)
    r1 = x - hi.astype(F32)
    mid = r1.astype(BF16)
    lo = (r1 - mid.astype(F32)).astype(BF16)
    return hi, mid, lo


def _tri_dot(tri, x):
    hi, mid, lo = _split3(x)
    return _dot(tri, hi) + _dot(tri, mid) + _dot(tri, lo)


def _forget_fwd(h, wf, bf, seq):
    t, d = h.shape
    cb = min(256, seq)

    def body(h_ref, wf_ref, bf_ref, fl_ref, fc_ref):
        fl = _dot(h_ref[...], wf_ref[...])
        fl_ref[...] = fl
        xx = fl + bf_ref[...]
        lf = jnp.minimum(xx, 0.0) - jnp.log(1.0 + jnp.exp(-jnp.abs(xx)))
        ri = lax.broadcasted_iota(jnp.int32, (cb, cb), 0)
        ci = lax.broadcasted_iota(jnp.int32, (cb, cb), 1)
        tri = (ri >= ci).astype(BF16)
        carry = jnp.zeros((1, LANES), F32)
        for blk in range(seq // cb):
            cs = _tri_dot(tri, lf[blk * cb:(blk + 1) * cb]) + carry
            fc_ref[blk * cb:(blk + 1) * cb, :] = cs
            carry = cs[cb - 1:cb, :]

    return pl.pallas_call(
        body, name="forget_fwd",
        out_shape=(jax.ShapeDtypeStruct((t, LANES), F32), jax.ShapeDtypeStruct((t, LANES), F32)),
        grid=(t // seq,),
        in_specs=[pl.BlockSpec((seq, d), lambda b: (b, 0)), pl.BlockSpec((d, LANES), lambda b: (0, 0)),
                  pl.BlockSpec((1, LANES), lambda b: (0, 0))],
        out_specs=(pl.BlockSpec((seq, LANES), lambda b: (b, 0)), pl.BlockSpec((seq, LANES), lambda b: (b, 0))),
        compiler_params=_cparams("parallel"),
    )(h, wf, bf)


def _pool_fwd(u, mix, scale, seq):
    t = u.shape[0]

    def body(u_ref, mix_ref, sc_ref, p_ref, ps_ref):
        tpos = lax.broadcasted_iota(jnp.int32, (seq, POOL_GROUP_DIM), 0)
        for g in range(POOL_GROUPS):
            sl = slice(g * POOL_GROUP_DIM, (g + 1) * POOL_GROUP_DIM)
            ug = u_ref[:, sl]
            s = ug
            for lvl in range(g + 1):
                d = 2 ** lvl
                s = s + jnp.where(tpos >= d, pltpu.roll(s, d, 0), 0.0)
            cnt = jnp.minimum(tpos + 1, POOL_WINDOWS[g]).astype(F32)
            pb = (s / cnt - ug).astype(BF16)
            p_ref[:, sl] = pb
            ps_ref[:, sl] = (_dot(pb, mix_ref[g]) * sc_ref[:, sl]).astype(BF16)

    return pl.pallas_call(
        body, name="pool_fwd",
        out_shape=(jax.ShapeDtypeStruct((t, POOL_WIDTH), BF16), jax.ShapeDtypeStruct((t, POOL_WIDTH), BF16)),
        grid=(t // seq,),
        in_specs=[pl.BlockSpec((seq, POOL_WIDTH), lambda b: (b, 0)),
                  pl.BlockSpec((POOL_GROUPS, POOL_GROUP_DIM, POOL_GROUP_DIM), lambda b: (0, 0, 0)),
                  pl.BlockSpec((1, POOL_WIDTH), lambda b: (0, 0))],
        out_specs=(pl.BlockSpec((seq, POOL_WIDTH), lambda b: (b, 0)), pl.BlockSpec((seq, POOL_WIDTH), lambda b: (b, 0))),
        compiler_params=_cparams("parallel"),
    )(u, mix, scale)


def _aug_constants():
    w = N_HEADS * LANES
    rows = jnp.arange(3 * LANES)
    piece, head = rows // LANES, rows % LANES
    cols = jnp.arange(w)
    live = (head < N_HEADS)[:, None]
    pq = (live & (cols[None, :] == (head * LANES + HEAD_DIM + piece)[:, None])).astype(BF16)
    pk = -(live & (cols[None, :] == (head * LANES + HEAD_DIM + 3 + piece)[:, None])).astype(BF16)
    lane = cols % LANES
    oq = ((lane >= HEAD_DIM + 3) & (lane < HEAD_DIM + 6)).astype(F32)[None, :]
    ok = ((lane >= HEAD_DIM) & (lane < HEAD_DIM + 3)).astype(F32)[None, :]
    return pq, pk, oq, ok


def _head_blocks(w):
    d = w.shape[0]
    return jnp.pad(w.reshape(d, N_HEADS, HEAD_DIM), ((0, 0), (0, 0), (0, LANES - HEAD_DIM))).reshape(d, N_HEADS * LANES)


def _attn_prep(h, wq, wk, wv, fcum, tm):
    t, d = h.shape
    tm = min(tm, t)
    w = N_HEADS * LANES
    pq, pk, oq, ok = _aug_constants()

    def body(h_ref, wq_ref, wk_ref, wv_ref, f_ref, pq_ref, pk_ref, oq_ref, ok_ref, qa_ref, ka_ref, v_ref):
        hh = h_ref[...]
        fs = jnp.concatenate(_split3(f_ref[...]), axis=1)
        q = _dot(hh, wq_ref[...]).astype(BF16).astype(F32) * ATTN_SCALE
        qa_ref[...] = (q + _dot(fs, pq_ref[...]) + oq_ref[...]).astype(BF16)
        k = _dot(hh, wk_ref[...]).astype(BF16).astype(F32)
        ka_ref[...] = (k + _dot(fs, pk_ref[...]) + ok_ref[...]).astype(BF16)
        v_ref[...] = _dot(hh, wv_ref[...]).astype(BF16)

    row = lambda n: pl.BlockSpec((tm, n), lambda i: (i, 0))
    full = lambda a: pl.BlockSpec(a.shape, lambda i: (0, 0))
    return pl.pallas_call(
        body, name="attn_prep",
        out_shape=(jax.ShapeDtypeStruct((t, w), BF16), jax.ShapeDtypeStruct((t, w), BF16),
                   jax.ShapeDtypeStruct((t, ATTN_WIDTH), BF16)),
        grid=(t // tm,),
        in_specs=[row(d), full(wq), full(wk), full(wv), row(LANES), full(pq), full(pk), full(oq), full(ok)],
        out_specs=(row(w), row(w), row(ATTN_WIDTH)),
        compiler_params=_cparams("parallel"),
    )(h, wq, wk, wv, fcum, pq, pk, oq, ok)


def _fold_lanes(x, op):
    out = x[:, :LANES]
    for g in range(1, x.shape[1] // LANES):
        out = op(out, x[:, g * LANES:(g + 1) * LANES])
    return out


def _attn_fwd(qa, ka, v, seq, tq, dep=None):
    t = qa.shape[0]
    nq = seq // tq
    hp_n = N_HEADS // 2
    heads = [slice(e * LANES, (e + 1) * LANES) for e in range(2)]

    def body(q_ref, k_ref, v_ref, o_ref, lse_ref, s_buf):
        i = pl.program_id(2)
        diag_ok = lax.broadcasted_iota(jnp.int32, (tq, tq), 0) >= lax.broadcasted_iota(jnp.int32, (tq, tq), 1)
        qs = [q_ref[:, hl] for hl in heads]

        def sweep1(j, mxs):
            r0 = pl.multiple_of(j * tq, tq)
            out = []
            for e, hl in enumerate(heads):
                s = _dot_nt(qs[e], k_ref[pl.ds(r0, tq), hl])
                s = jnp.where(jnp.logical_or(diag_ok, j < i), s, NEG_BIG)
                s_buf[e, j] = s
                out.append(jnp.maximum(mxs[e], _fold_lanes(s, jnp.maximum)))
            return tuple(out)

        mxs = lax.fori_loop(0, i + 1, sweep1, (jnp.full((tq, LANES), NEG_BIG, F32),) * 2)
        ms = [jnp.max(mx, axis=1, keepdims=True) for mx in mxs]

        def sweep2(j, carry):
            r0 = pl.multiple_of(j * tq, tq)
            vv = v_ref[pl.ds(r0, tq), :]
            out = []
            for e in range(2):
                p = jnp.exp(s_buf[e, j] - ms[e])
                out += [carry[2 * e] + _fold_lanes(p, jnp.add), carry[2 * e + 1] + _dot(p.astype(BF16), vv)]
            return tuple(out)

        res = lax.fori_loop(0, i + 1, sweep2, (jnp.zeros((tq, LANES), F32),) * 4)
        outs = []
        for e in range(2):
            l = jnp.sum(res[2 * e], axis=1, keepdims=True)
            outs.append(res[2 * e + 1] / l)
            lse_ref[:, e:e + 1] = ms[e] + jnp.log(l)
        lane = lax.broadcasted_iota(jnp.int32, (tq, LANES), 1)
        o_ref[...] = jnp.where(lane < HEAD_DIM, outs[0], outs[1])

    dep_specs, dep_ops = _dep_args(dep)
    return pl.pallas_call(
        _after(body, 3, dep), name="attn_fwd",
        out_shape=(jax.ShapeDtypeStruct((t, ATTN_WIDTH), F32), jax.ShapeDtypeStruct((hp_n, t, 2), F32)),
        grid=(t // seq, hp_n, nq),
        in_specs=[pl.BlockSpec((tq, 2 * LANES), lambda b, hp, i: (b * nq + i, hp)),
                  pl.BlockSpec((seq, 2 * LANES), lambda b, hp, i: (b, hp)),
                  pl.BlockSpec((seq, LANES), lambda b, hp, i: (b, hp))] + dep_specs,
        out_specs=(pl.BlockSpec((tq, LANES), lambda b, hp, i: (b * nq + i, hp)),
                   pl.BlockSpec((None, tq, 2), lambda b, hp, i: (hp, b * nq + i, 0))),
        scratch_shapes=[pltpu.VMEM((2, nq, tq, tq), F32)],
        compiler_params=_cparams("parallel", "parallel", "arbitrary"),
    )(qa, ka, v, *dep_ops)


def _merge_fwd(x, ps, o, g2, wpo, wao, wout, tm):
    t, d = x.shape
    tm = min(tm, t)

    def body(x_ref, ps_ref, o_ref, gp_ref, ga_ref, wpo_ref, wao_ref, wout_ref, mg_ref, x1_ref):
        py = _dot(ps_ref[...], wpo_ref[...])
        ay = _dot(o_ref[...].astype(BF16), wao_ref[...])
        mb = (_sigmoid(gp_ref[...]) * py + _sigmoid(ga_ref[...]) * ay).astype(BF16)
        mg_ref[...] = mb
        x1_ref[...] = x_ref[...] + _dot(mb, wout_ref[...])

    row = lambda w: pl.BlockSpec((tm, w), lambda i: (i, 0))
    full = lambda a: pl.BlockSpec(a.shape, lambda i: (0, 0))
    return pl.pallas_call(
        body, name="merge_fwd",
        out_shape=(jax.ShapeDtypeStruct((t, d), BF16), jax.ShapeDtypeStruct((t, d), F32)),
        grid=(t // tm,),
        in_specs=[row(d), row(POOL_WIDTH), row(ATTN_WIDTH), pl.BlockSpec((tm, d), lambda i: (i, 0)),
                  pl.BlockSpec((tm, d), lambda i: (i, 1)), full(wpo), full(wao), full(wout)],
        out_specs=(row(d), row(d)),
        compiler_params=_cparams("parallel"),
    )(x, ps, o, g2, g2, wpo, wao, wout)


def _ffn_fwd(x1, g, wg, wu, wd, tm, tf):
    t, d = x1.shape
    f = wg.shape[0]
    tm = min(tm, t)
    nf = f // tf

    def body(x1_ref, g_ref, wg_ref, wu_ref, wd_ref, h2_ref, gt_ref, up_ref, act_ref, x2_ref):
        j = pl.program_id(1)

        @pl.when(j == 0)
        def _():
            h2_ref[...] = _rms_fwd(x1_ref[...], g_ref[...]).astype(BF16)

        h2 = h2_ref[...]
        gt = _dot_nt(h2, wg_ref[...])
        up = _dot_nt(h2, wu_ref[...])
        act = (gt * _sigmoid(gt) * up).astype(BF16)
        gt_ref[...] = gt
        up_ref[...] = up
        act_ref[...] = act
        prod = _dot(act, wd_ref[...])

        @pl.when(j == 0)
        def _():
            x2_ref[...] = prod

        @pl.when(j > 0)
        def _():
            x2_ref[...] += prod

        @pl.when(j == nf - 1)
        def _():
            x2_ref[...] += x1_ref[...]

    return pl.pallas_call(
        body, name="ffn_fwd",
        out_shape=(jax.ShapeDtypeStruct((t, d), BF16), jax.ShapeDtypeStruct((t, f), F32),
                   jax.ShapeDtypeStruct((t, f), F32), jax.ShapeDtypeStruct((t, f), BF16),
                   jax.ShapeDtypeStruct((t, d), F32)),
        grid=(t // tm, nf),
        in_specs=[pl.BlockSpec((tm, d), lambda i, j: (i, 0)), pl.BlockSpec((1, d), lambda i, j: (0, 0)),
                  pl.BlockSpec((tf, d), lambda i, j: (j, 0)), pl.BlockSpec((tf, d), lambda i, j: (j, 0)),
                  pl.BlockSpec((tf, d), lambda i, j: (j, 0))],
        out_specs=(pl.BlockSpec((tm, d), lambda i, j: (i, 0)), pl.BlockSpec((tm, tf), lambda i, j: (i, j)),
                   pl.BlockSpec((tm, tf), lambda i, j: (i, j)), pl.BlockSpec((tm, tf), lambda i, j: (i, j)),
                   pl.BlockSpec((tm, d), lambda i, j: (i, 0))),
        compiler_params=_cparams("parallel", "arbitrary"),
    )(x1, g, wg, wu, wd)


def _final_fwd_bwd(x2, target, g, tm):
    t, d = x2.shape
    tm = min(tm, t)

    def body(x_ref, t_ref, g_ref, loss_ref, dx_ref, dg_ref):
        i = pl.program_id(0)
        x = x_ref[...]
        gg = g_ref[...]
        err = _rms_fwd(x, gg) - t_ref[...]
        part = 0.5 * jnp.sum(jnp.mean(err * err, axis=-1, keepdims=True), axis=0, keepdims=True)
        dx, dg = _rms_bwd(x, gg, err * (1.0 / d))
        dx_ref[...] = dx

        @pl.when(i == 0)
        def _():
            loss_ref[...] = jnp.zeros_like(loss_ref)
            dg_ref[...] = jnp.zeros_like(dg_ref)

        loss_ref[...] += jnp.broadcast_to(part, loss_ref.shape)
        dg_ref[...] += dg

    return pl.pallas_call(
        body, name="final_fwd_bwd",
        out_shape=(jax.ShapeDtypeStruct((1, LANES), F32), jax.ShapeDtypeStruct((t, d), F32),
                   jax.ShapeDtypeStruct((1, d), F32)),
        grid=(t // tm,),
        in_specs=[pl.BlockSpec((tm, d), lambda i: (i, 0)), pl.BlockSpec((tm, d), lambda i: (i, 0)),
                  pl.BlockSpec((1, d), lambda i: (0, 0))],
        out_specs=(pl.BlockSpec((1, LANES), lambda i: (0, 0)), pl.BlockSpec((tm, d), lambda i: (i, 0)),
                   pl.BlockSpec((1, d), lambda i: (0, 0))),
        compiler_params=_cparams("arbitrary"),
    )(x2, target, g)


def _ffn_bwd(dx2, x1, g, gt, up, wg, wu, wd, tm, tf):
    t, d = dx2.shape
    f = gt.shape[1]
    tm = min(tm, t)
    nf = f // tf

    def body(dx2_ref, x1_ref, g_ref, gt_ref, up_ref, wg_ref, wu_ref, wd_ref, dgt_ref, dup_ref, dx1_ref, dg_ref, acc_ref):
        i, j = pl.program_id(0), pl.program_id(1)
        dact = _dot_nt(dx2_ref[...].astype(BF16), wd_ref[...])
        gtv = gt_ref[...]
        sg = _sigmoid(gtv)
        dup = (dact * (gtv * sg)).astype(BF16)
        dgt = (dact * up_ref[...] * (sg * (1.0 + gtv * (1.0 - sg)))).astype(BF16)
        dgt_ref[...] = dgt
        dup_ref[...] = dup
        contrib = _dot(dgt, wg_ref[...]) + _dot(dup, wu_ref[...])

        @pl.when(j == 0)
        def _():
            acc_ref[...] = contrib

        @pl.when(j > 0)
        def _():
            acc_ref[...] += contrib

        @pl.when(jnp.logical_and(i == 0, j == 0))
        def _():
            dg_ref[...] = jnp.zeros_like(dg_ref)

        @pl.when(j == nf - 1)
        def _():
            dxn, dg = _rms_bwd(x1_ref[...], g_ref[...], acc_ref[...])
            dx1_ref[...] = dx2_ref[...] + dxn
            dg_ref[...] += dg

    return pl.pallas_call(
        body, name="ffn_bwd",
        out_shape=(jax.ShapeDtypeStruct((t, f), BF16), jax.ShapeDtypeStruct((t, f), BF16),
                   jax.ShapeDtypeStruct((t, d), F32), jax.ShapeDtypeStruct((1, d), F32)),
        grid=(t // tm, nf),
        in_specs=[pl.BlockSpec((tm, d), lambda i, j: (i, 0)), pl.BlockSpec((tm, d), lambda i, j: (i, 0)),
                  pl.BlockSpec((1, d), lambda i, j: (0, 0)),
                  pl.BlockSpec((tm, tf), lambda i, j: (i, j)), pl.BlockSpec((tm, tf), lambda i, j: (i, j)),
                  pl.BlockSpec((tf, d), lambda i, j: (j, 0)), pl.BlockSpec((tf, d), lambda i, j: (j, 0)),
                  pl.BlockSpec((tf, d), lambda i, j: (j, 0))],
        out_specs=(pl.BlockSpec((tm, tf), lambda i, j: (i, j)), pl.BlockSpec((tm, tf), lambda i, j: (i, j)),
                   pl.BlockSpec((tm, d), lambda i, j: (i, 0)), pl.BlockSpec((1, d), lambda i, j: (0, 0))),
        scratch_shapes=[pltpu.VMEM((tm, d), F32)],
        compiler_params=_cparams("arbitrary", "arbitrary"),
    )(dx2, x1, g, gt, up, wg, wu, wd)


def _merge_bwd(dx1, ps, o, g2, wpo, wao, wout, tm, dep=None):
    t, d = dx1.shape
    tm = min(tm, t)

    def body(dx1_ref, ps_ref, o_ref, gp_ref, ga_ref, wpo_ref, wao_ref, wout_ref, dpy_ref, day_ref, dg2_ref, dps_ref, da_ref):
        dm = _dot_nt(dx1_ref[...].astype(BF16), wout_ref[...])
        py = _dot(ps_ref[...], wpo_ref[...])
        ay = _dot(o_ref[...].astype(BF16), wao_ref[...])
        sp = _sigmoid(gp_ref[...])
        sa = _sigmoid(ga_ref[...])
        dpy = (dm * sp).astype(BF16)
        day = (dm * sa).astype(BF16)
        dpy_ref[...] = dpy
        day_ref[...] = day
        dg2_ref[:, :d] = (dm * py * (sp * (1.0 - sp))).astype(BF16)
        dg2_ref[:, d:] = (dm * ay * (sa * (1.0 - sa))).astype(BF16)
        dps_ref[...] = _dot_nt(dpy, wpo_ref[...])
        da_ref[...] = _dot_nt(day, wao_ref[...]).astype(BF16)

    row = lambda w: pl.BlockSpec((tm, w), lambda i: (i, 0))
    full = lambda a: pl.BlockSpec(a.shape, lambda i: (0, 0))
    dep_specs, dep_ops = _dep_args(dep)
    return pl.pallas_call(
        _after(body, 8, dep), name="merge_bwd",
        out_shape=(jax.ShapeDtypeStruct((t, d), BF16), jax.ShapeDtypeStruct((t, d), BF16),
                   jax.ShapeDtypeStruct((t, 2 * d), BF16), jax.ShapeDtypeStruct((t, POOL_WIDTH), F32),
                   jax.ShapeDtypeStruct((t, ATTN_WIDTH), BF16)),
        grid=(t // tm,),
        in_specs=[row(d), row(POOL_WIDTH), row(ATTN_WIDTH), pl.BlockSpec((tm, d), lambda i: (i, 0)),
                  pl.BlockSpec((tm, d), lambda i: (i, 1)), full(wpo), full(wao), full(wout)] + dep_specs,
        out_specs=(row(d), row(d), row(2 * d), row(POOL_WIDTH), row(ATTN_WIDTH)),
        compiler_params=_cparams("parallel"),
    )(dx1, ps, o, g2, g2, wpo, wao, wout, *dep_ops)


def _attn_bwd(qa, ka, v, do, lse4, seq, tq, dep=None):
    t = qa.shape[0]
    nq = seq // tq
    hp_n = N_HEADS // 2
    heads = [slice(e * LANES, (e + 1) * LANES) for e in range(2)]

    def body(q_ref, k_ref, v_ref, do_ref, lse_ref, dq_ref, dk_ref, dv_ref, dfr_ref, dk_acc, dv_acc, p_buf, dp_buf):
        diag_ok = lax.broadcasted_iota(jnp.int32, (tq, tq), 0) >= lax.broadcasted_iota(jnp.int32, (tq, tq), 1)
        lane_q = lax.broadcasted_iota(jnp.int32, (tq, LANES), 1)
        lane_s = lax.broadcasted_iota(jnp.int32, (seq, LANES), 1)
        mine_q = [lane_q < HEAD_DIM, lane_q >= HEAD_DIM]
        dv_acc[...] = jnp.zeros_like(dv_acc)
        dk_acc[...] = jnp.zeros_like(dk_acc)
        dfr_ref[...] = jnp.zeros_like(dfr_ref)

        def q_step(i, _):
            q0 = pl.multiple_of(i * tq, tq)
            qs = [q_ref[pl.ds(q0, tq), hl] for hl in heads]
            dov = do_ref[pl.ds(q0, tq), :]
            dos = [jnp.where(mq, dov, jnp.zeros((), BF16)) for mq in mine_q]
            lss = [lse_ref[pl.ds(q0, tq), e:e + 1] for e in range(2)]

            def sweep1(j, dls):
                r0 = pl.multiple_of(j * tq, tq)
                vv = v_ref[pl.ds(r0, tq), :]
                out = []
                for e, hl in enumerate(heads):
                    s = _dot_nt(qs[e], k_ref[pl.ds(r0, tq), hl])
                    s = jnp.where(jnp.logical_or(diag_ok, j < i), s, NEG_BIG)
                    p = jnp.exp(s - lss[e])
                    dp = _dot_nt(dos[e], vv)
                    p_buf[e, j] = p
                    dp_buf[e, j] = dp
                    dv_acc[pl.ds(r0, tq), :] += _dot_tn(p.astype(BF16), dos[e])
                    out.append(dls[e] + _fold_lanes(p * dp, jnp.add))
                return tuple(out)

            dls = lax.fori_loop(0, i + 1, sweep1, (jnp.zeros((tq, LANES), F32),) * 2)
            dls = [jnp.sum(d, axis=1, keepdims=True) for d in dls]

            def sweep2(j, dqs):
                r0 = pl.multiple_of(j * tq, tq)
                out = []
                for e, hl in enumerate(heads):
                    ds = p_buf[e, j] * (dp_buf[e, j] - dls[e])
                    dfr_ref[e, pl.ds(j, 1), :] += jnp.sum(ds, axis=0, keepdims=True)
                    dsb = ds.astype(BF16)
                    dk_acc[e, pl.ds(r0, tq), :] += _dot_tn(dsb, qs[e])
                    out.append(dqs[e] + _dot(dsb, k_ref[pl.ds(r0, tq), hl]))
                return tuple(out)

            dqs = lax.fori_loop(0, i + 1, sweep2, (jnp.zeros((tq, LANES), F32),) * 2)
            dq = jnp.where(mine_q[0], dqs[0], pltpu.roll(dqs[1], HEAD_DIM, 1)) * ATTN_SCALE
            dq_ref[pl.ds(q0, tq), :] = dq.astype(BF16)
            return 0

        lax.fori_loop(0, nq, q_step, 0)
        dk_ref[...] = jnp.where(lane_s < HEAD_DIM, dk_acc[0], pltpu.roll(dk_acc[1], HEAD_DIM, 1)).astype(BF16)
        dv_ref[...] = dv_acc[...].astype(BF16)

    wide = pl.BlockSpec((seq, 2 * LANES), lambda b, hp: (b, hp))
    col = pl.BlockSpec((seq, LANES), lambda b, hp: (b, hp))
    pair = pl.BlockSpec((None, seq, 2), lambda b, hp: (hp, b, 0))
    dep_specs, dep_ops = _dep_args(dep)
    return pl.pallas_call(
        _after(body, 5, dep), name="attn_bwd",
        out_shape=(jax.ShapeDtypeStruct((t, ATTN_WIDTH), BF16),) * 3 + (jax.ShapeDtypeStruct((N_HEADS, t // tq, tq), F32),),
        grid=(t // seq, hp_n),
        in_specs=[wide, wide, col, col, pair] + dep_specs,
        out_specs=(col, col, col, pl.BlockSpec((2, nq, tq), lambda b, hp: (hp, b, 0))),
        scratch_shapes=[pltpu.VMEM((2, seq, LANES), F32), pltpu.VMEM((seq, LANES), F32),
                        pltpu.VMEM((2, nq, tq, tq), F32), pltpu.VMEM((2, nq, tq, tq), F32)],
        compiler_params=_cparams("parallel", "arbitrary"),
    )(qa, ka, v, do, lse4, *dep_ops)


def _forget_bwd(dfc, fl, bf, seq):
    t = fl.shape[0]
    cb = min(256, seq)
    nb = seq // cb

    def body(dfc_ref, fl_ref, bf_ref, dfl_ref, db_ref):
        b = pl.program_id(0)
        ri = lax.broadcasted_iota(jnp.int32, (cb, cb), 0)
        ci = lax.broadcasted_iota(jnp.int32, (cb, cb), 1)
        tri = (ci >= ri).astype(BF16)
        carry = jnp.zeros((1, LANES), F32)
        dbs = jnp.zeros((1, LANES), F32)
        for blk in reversed(range(nb)):
            rs = slice(blk * cb, (blk + 1) * cb)
            dlf = _tri_dot(tri, -dfc_ref[rs, :]) + carry
            carry = dlf[0:1, :]
            dfl = dlf * _sigmoid(-(fl_ref[rs, :] + bf_ref[...]))
            dfl_ref[rs, :] = dfl.astype(BF16)
            dbs = dbs + jnp.sum(dfl, axis=0, keepdims=True)

        @pl.when(b == 0)
        def _():
            db_ref[...] = jnp.zeros_like(db_ref)

        db_ref[...] += dbs

    return pl.pallas_call(
        body, name="forget_bwd",
        out_shape=(jax.ShapeDtypeStruct((t, LANES), BF16), jax.ShapeDtypeStruct((1, LANES), F32)),
        grid=(t // seq,),
        in_specs=[pl.BlockSpec((seq, LANES), lambda b: (b, 0)), pl.BlockSpec((seq, LANES), lambda b: (b, 0)),
                  pl.BlockSpec((1, LANES), lambda b: (0, 0))],
        out_specs=(pl.BlockSpec((seq, LANES), lambda b: (b, 0)), pl.BlockSpec((1, LANES), lambda b: (0, 0))),
        compiler_params=_cparams("arbitrary"),
    )(dfc, fl, bf)


def _pool_bwd(dps, p, mix, scale, seq):
    t = dps.shape[0]

    def body(dps_ref, p_ref, mix_ref, sc_ref, du_ref, dmix_ref, dsc_ref):
        b = pl.program_id(0)

        @pl.when(b == 0)
        def _():
            dmix_ref[...] = jnp.zeros_like(dmix_ref)
            dsc_ref[...] = jnp.zeros_like(dsc_ref)

        tpos = lax.broadcasted_iota(jnp.int32, (seq, POOL_GROUP_DIM), 0)
        for g in range(POOL_GROUPS):
            sl = slice(g * POOL_GROUP_DIM, (g + 1) * POOL_GROUP_DIM)
            pb = p_ref[:, sl]
            dpsg = dps_ref[:, sl]
            pm = _dot(pb, mix_ref[g])
            dsc_ref[:, sl] += jnp.sum(dpsg * pm, axis=0, keepdims=True)
            dpm = (dpsg * sc_ref[:, sl]).astype(BF16)
            dmix_ref[g] += _dot_tn(pb, dpm)
            dp = _dot_nt(dpm, mix_ref[g])
            cnt = jnp.minimum(tpos + 1, POOL_WINDOWS[g]).astype(F32)
            s = dp / cnt
            for lvl in range(g + 1):
                d = 2 ** lvl
                s = s + jnp.where(tpos < seq - d, pltpu.roll(s, seq - d, 0), 0.0)
            du_ref[:, sl] = (s - dp).astype(BF16)

    return pl.pallas_call(
        body, name="pool_bwd",
        out_shape=(jax.ShapeDtypeStruct((t, POOL_WIDTH), BF16),
                   jax.ShapeDtypeStruct((POOL_GROUPS, POOL_GROUP_DIM, POOL_GROUP_DIM), F32),
                   jax.ShapeDtypeStruct((1, POOL_WIDTH), F32)),
        grid=(t // seq,),
        in_specs=[pl.BlockSpec((seq, POOL_WIDTH), lambda b: (b, 0)), pl.BlockSpec((seq, POOL_WIDTH), lambda b: (b, 0)),
                  pl.BlockSpec((POOL_GROUPS, POOL_GROUP_DIM, POOL_GROUP_DIM), lambda b: (0, 0, 0)),
                  pl.BlockSpec((1, POOL_WIDTH), lambda b: (0, 0))],
        out_specs=(pl.BlockSpec((seq, POOL_WIDTH), lambda b: (b, 0)),
                   pl.BlockSpec((POOL_GROUPS, POOL_GROUP_DIM, POOL_GROUP_DIM), lambda b: (0, 0, 0)),
                   pl.BlockSpec((1, POOL_WIDTH), lambda b: (0, 0))),
        compiler_params=_cparams("arbitrary"),
    )(dps, p, mix, scale)


def _in_bwd(du, dq, dk, dv, dg2, dfl, dx1, x, g, wu, wqkv, wg2, wft, tm):
    t, d = x.shape
    tm = min(tm, t)
    aw = ATTN_WIDTH

    def body(du_ref, dq_ref, dk_ref, dv_ref, dg2_ref, dfl_ref, dx1_ref, x_ref, g_ref, wu_ref, wqkv_ref, wg2_ref, wft_ref,
             dx_ref, dg_ref):
        i = pl.program_id(0)
        dh = _dot_nt(du_ref[...], wu_ref[...])
        dh += _dot_nt(dq_ref[...], wqkv_ref[:, 0:aw])
        dh += _dot_nt(dk_ref[...], wqkv_ref[:, aw:2 * aw])
        dh += _dot_nt(dv_ref[...], wqkv_ref[:, 2 * aw:3 * aw])
        dh += _dot_nt(dg2_ref[...], wg2_ref[...])
        dh += _dot(dfl_ref[...], wft_ref[...])
        dxn, dg = _rms_bwd(x_ref[...], g_ref[...], dh)
        dx_ref[...] = dx1_ref[...] + dxn

        @pl.when(i == 0)
        def _():
            dg_ref[...] = jnp.zeros_like(dg_ref)

        dg_ref[...] += dg

    row = lambda w: pl.BlockSpec((tm, w), lambda i: (i, 0))
    full = lambda a: pl.BlockSpec(a.shape, lambda i: (0, 0))
    return pl.pallas_call(
        body, name="in_bwd",
        out_shape=(jax.ShapeDtypeStruct((t, d), F32), jax.ShapeDtypeStruct((1, d), F32)),
        grid=(t // tm,),
        in_specs=[row(POOL_WIDTH), row(aw), row(aw), row(aw), row(2 * d), row(LANES), row(d), row(d),
                  pl.BlockSpec((1, d), lambda i: (0, 0)), full(wu), full(wqkv), full(wg2), full(wft)],
        out_specs=(row(d), pl.BlockSpec((1, d), lambda i: (0, 0))),
        compiler_params=_cparams("arbitrary"),
    )(du, dq, dk, dv, dg2, dfl, dx1, x, g, wu, wqkv, wg2, wft)


def _position():
    return lax.axis_index("x"), lax.axis_index("y"), lax.axis_index("c")


def _remote(src, dst, send_sem, recv_sem, device):
    return pltpu.make_async_remote_copy(src_ref=src, dst_ref=dst, send_sem=send_sem, recv_sem=recv_sem,
                                        device_id=device, device_id_type=MESH)


HBM = pl.BlockSpec(memory_space=pltpu.HBM)
SEM = pl.BlockSpec(memory_space=pltpu.SEMAPHORE)
DATAFLOW = pltpu.SideEffectType.DATAFLOW_SIDE_EFFECTING


def _copies_start(name, arrays, plan, m, dep=None):
    n = len(arrays)
    arrays = [pltpu.with_memory_space_constraint(a, pltpu.HBM) for a in arrays]

    def body(*refs):
        ins, send_sem, recv_sem, token = refs[:n], refs[n], refs[n + 1], refs[2 * n + 2]
        for i, (src, dst, device, _) in enumerate(plan(ins, *_position())):
            _remote(src, dst, send_sem.at[i], recv_sem.at[i], device).start()
        token[...] = jnp.zeros_like(token)

    dep_specs, dep_ops = _dep_args(dep)
    outs = pl.pallas_call(
        _after(body, n, dep), name=name,
        out_shape=(pltpu.SemaphoreType.DMA((m,)), pltpu.SemaphoreType.DMA((m,)),
                   *[pltpu.HBM(a.shape, a.dtype) for a in arrays], jax.ShapeDtypeStruct((8, LANES), F32)),
        in_specs=[HBM] * n + dep_specs, out_specs=(SEM, SEM, *[HBM] * n, pl.BlockSpec(memory_space=pltpu.VMEM)),
        input_output_aliases={i: i + 2 for i in range(n)},
        compiler_params=pltpu.CompilerParams(has_side_effects=DATAFLOW),
    )(*arrays, *dep_ops)
    return (outs[0], outs[1]), list(outs[2:2 + n]), outs[2 + n]


def _copies_wait(name, sems, arrays, plan, after):
    n = len(arrays)
    afters = list(after) if isinstance(after, (list, tuple)) else [after]

    def body(*refs):
        ins, send_sem, recv_sem = refs[:n], refs[n], refs[n + 1]
        for i, (src, dst, device, landing) in enumerate(plan(ins, *_position())):
            _remote(src, dst, send_sem.at[i], recv_sem.at[i], device).wait_send()
            _remote(landing, landing, send_sem.at[i], recv_sem.at[i], device).wait_recv()

    outs = pl.pallas_call(
        body, name=name,
        out_shape=tuple(pltpu.HBM(a.shape, a.dtype) for a in arrays),
        in_specs=[HBM] * n + [SEM, SEM] + [ANY] * len(afters), out_specs=tuple([HBM] * n),
        input_output_aliases={i: i for i in range(n)},
        compiler_params=pltpu.CompilerParams(has_side_effects=DATAFLOW),
    )(*arrays, sems[0], sems[1], *afters)
    return list(outs)


def _tie(x, token):
    return x if token is None else x + token[0, 0]


def _other_chips(x, y):
    return [(1 - x, y), (x, 1 - y), (1 - x, 1 - y)]


def _gather_begin(tag, shards, token):
    n = len(shards)
    lands = [lax.empty((N_CHIPS,) + s.shape, s.dtype) for s in shards]

    def plan(refs, x, y, c):
        return [(refs[k].at[c], refs[n + k].at[2 * x + y, c], (ox, oy, c), refs[n + k].at[2 * ox + oy, c])
                for k in range(n) for ox, oy in _other_chips(x, y)]

    sems, thru, token = _copies_start(f"gather_{tag}_ici_start", list(shards) + lands, plan, 3 * n, dep=token)
    return dict(tag=tag, n=n, plan=plan, sems=sems, arrays=thru, token=token)


def _gather_forward(st, after):
    n, tag = st["n"], st["tag"]
    thru = _copies_wait(f"gather_{tag}_ici_wait", st["sems"], st["arrays"], st["plan"], after)

    def plan(refs, x, y, c):
        return [(refs[k].at[2 * ox + oy, c], refs[k].at[2 * ox + oy, c], (x, y, 1 - c), refs[k].at[2 * ox + oy, 1 - c])
                for k in range(n) for ox, oy in _other_chips(x, y)]

    sems, lands, token = _copies_start(f"gather_{tag}_fwd_start", thru[n:], plan, 3 * n)
    return dict(tag=tag, n=n, plan=plan, sems=sems, arrays=lands, token=token, shards=thru[:n])


def _gather_end(st, after):
    lands = _copies_wait(f"gather_{st['tag']}_fwd_wait", st["sems"], st["arrays"], st["plan"], after)
    me = 2 * lax.axis_index("x") + lax.axis_index("y")
    return [lax.dynamic_update_index_in_dim(g, s, me, 0) for g, s in zip(lands, st["shards"])]


def _add_keep_give(name, pos, a, a_keep, a_give, b, b_keep, b_give, steps):
    r, c = a.shape[-2:]

    def spec(arr, fn):
        lead = arr.ndim - 2
        return pl.BlockSpec((None,) * lead + (r, c), lambda i, p: tuple(fn(i, p)) + (0, 0))

    out_spec = pl.BlockSpec((None, r, c), lambda i, p: (i, 0, 0))

    def body(p_ref, ak_ref, bk_ref, ag_ref, bg_ref, keep_ref, give_ref):
        keep_ref[...] = ak_ref[...] + bk_ref[...].astype(F32)
        give_ref[...] = (ag_ref[...] + bg_ref[...].astype(F32)).astype(BF16)

    return pl.pallas_call(
        body, name=name,
        out_shape=(jax.ShapeDtypeStruct((steps, r, c), F32), jax.ShapeDtypeStruct((steps, r, c), BF16)),
        grid_spec=pltpu.PrefetchScalarGridSpec(
            num_scalar_prefetch=1, grid=(steps,),
            in_specs=[spec(a, a_keep), spec(b, b_keep), spec(a, a_give), spec(b, b_give)],
            out_specs=(out_spec, out_spec)),
        compiler_params=_cparams("parallel"),
    )(pos, a, b, a, b)


def _add_last(name, a, b):
    _, r, c = a.shape
    blk = pl.BlockSpec((None, r, c), lambda i: (0, 0, 0))

    def body(a_ref, b_ref, o_ref):
        o_ref[...] = a_ref[...] + b_ref[...].astype(F32)

    return pl.pallas_call(
        body, name=name, out_shape=jax.ShapeDtypeStruct((r, c), F32), grid=(1,), in_specs=[blk, blk],
        out_specs=pl.BlockSpec((r, c), lambda i: (0, 0)), compiler_params=_cparams("arbitrary"),
    )(a, b)


def _exchange_begin(tag, stage, gives, lands, peer_fn, extra):
    n = len(gives)

    def plan(refs, x, y, c):
        return [(refs[k], refs[n + k], peer_fn(x, y, c), refs[n + k]) for k in range(n)]

    sems, thru, token = _copies_start(f"rs{tag}_{stage}_start", gives + lands, plan, n)
    return dict(extra, tag=tag, n=n, stage=stage, plan=plan, sems=sems, arrays=thru, token=token)


def _reduce_begin(tag, grads):
    n = len(grads)
    lands = [lax.empty((N_CHIPS,) + g.shape[2:], F32) for g in grads]

    def plan(refs, x, y, c):
        return [(refs[k].at[j, 1 - c], refs[n + k].at[j], (x, y, 1 - c), refs[n + k].at[j])
                for k in range(n) for j in range(N_CHIPS)]

    sems, thru, token = _copies_start(f"rs{tag}_c_start", list(grads) + lands, plan, N_CHIPS * n)
    return dict(tag=tag, n=n, stage="c", plan=plan, sems=sems, arrays=thru, token=token)


def _reduce_advance(st, after):
    tag, n, stage = st["tag"], st["n"], st["stage"]
    thru = _copies_wait(f"rs{tag}_{stage}_wait", st["sems"], st["arrays"], st["plan"], after)
    first, recv = thru[:n], thru[n:]
    x, y, c = _position()
    if stage == "c":
        pos = jnp.stack([c, x]).astype(jnp.int32)
        sums = [_add_keep_give(
            f"rs{tag}_c_add{k}", pos,
            first[k], lambda i, p: (2 * p[1] + i, p[0]), lambda i, p: (2 * (1 - p[1]) + i, p[0]),
            recv[k], lambda i, p: (2 * p[1] + i,), lambda i, p: (2 * (1 - p[1]) + i,), 2) for k in range(n)]
        lands = [lax.empty(s[1].shape, BF16) for s in sums]
        return _exchange_begin(tag, "x", [s[1] for s in sums], lands, lambda x, y, c: (1 - x, y, c),
                               dict(keep=[s[0] for s in sums]))
    if stage == "x":
        pos = jnp.stack([y]).astype(jnp.int32)
        sums = [_add_keep_give(
            f"rs{tag}_x_add{k}", pos,
            st["keep"][k], lambda i, p: (p[0],), lambda i, p: (1 - p[0],),
            recv[k], lambda i, p: (p[0],), lambda i, p: (1 - p[0],), 1) for k in range(n)]
        lands = [lax.empty(s[1].shape, BF16) for s in sums]
        return _exchange_begin(tag, "y", [s[1] for s in sums], lands, lambda x, y, c: (x, 1 - y, c),
                               dict(keep=[s[0] for s in sums]))
    if stage == "y":
        mine = [_add_last(f"rs{tag}_y_add{k}", st["keep"][k], recv[k]) for k in range(n)]
        lands = [lax.empty(m.shape, F32) for m in mine]
        return _exchange_begin(tag, "swap", mine, lands, lambda x, y, c: (x, y, 1 - c), {})
    return dict(done=list(zip(first, recv)), token=None)


def _all_reduce_small(v):
    r = v.shape[0]

    def body(v_ref, out_ref, buf, send_sems, recv_sems, local_sem):
        x, y, c = _position()
        me, sibling = (x, y, c), (x, y, 1 - c)
        chips = [(1 - x, y), (x, 1 - y), (1 - x, 1 - y)]

        def rows(px, py, pc):
            return buf.at[pl.ds((4 * px + 2 * py + pc) * r, r), :]

        def copy(k, block, to, src=None):
            return _remote(rows(*block) if src is None else src, rows(*block), send_sems.at[k], recv_sems.at[k], to)

        mine = pltpu.make_async_copy(v_ref, rows(*me), local_sem)
        mine.start()
        first = [copy(0, me, sibling, src=v_ref)]
        first += [copy(1 + j, me, (*chip, c), src=v_ref) for j, chip in enumerate(chips)]
        for cp in first:
            cp.start()
        passed = [copy(4 + j, (*chip, c), sibling) for j, chip in enumerate(chips)]
        for j, chip in enumerate(chips):
            copy(1 + j, (*chip, c), me).wait_recv()
            passed[j].start()
        copy(0, sibling, me).wait_recv()
        for j, chip in enumerate(chips):
            copy(4 + j, (*chip, 1 - c), me).wait_recv()
        for cp in first + passed:
            cp.wait_send()
        mine.wait()
        acc = buf[0:r, :]
        for dev in range(1, N_DEV):
            acc = acc + buf[dev * r:(dev + 1) * r, :]
        out_ref[...] = acc

    return pl.pallas_call(
        body, name="all_reduce_small",
        out_shape=jax.ShapeDtypeStruct(v.shape, F32),
        in_specs=[pl.BlockSpec(memory_space=pltpu.VMEM)],
        out_specs=pl.BlockSpec(memory_space=pltpu.VMEM),
        scratch_shapes=[pltpu.VMEM((N_DEV * r, LANES), F32), pltpu.SemaphoreType.DMA((7,)),
                        pltpu.SemaphoreType.DMA((7,)), pltpu.SemaphoreType.DMA],
        compiler_params=pltpu.CompilerParams(has_side_effects=True, vmem_limit_bytes=VMEM_LIMIT_V7X),
    )(v)


def _adamw_update(w, gg, m, v):
    mn = ADAM_B1 * m + (1.0 - ADAM_B1) * gg
    vn = ADAM_B2 * v + (1.0 - ADAM_B2) * (gg * gg)
    m_hat = mn / (1.0 - ADAM_B1 ** ADAM_STEP)
    v_hat = vn / (1.0 - ADAM_B2 ** ADAM_STEP)
    return -ADAM_LR * (m_hat / (jnp.sqrt(v_hat) + ADAM_EPS) + ADAM_WD * w), mn, vn


def _adamw(name, w, g, m, v):
    def body(w_ref, g_ref, m_ref, v_ref, d_ref, mo_ref, vo_ref):
        d_ref[...], mo_ref[...], vo_ref[...] = _adamw_update(w_ref[...], g_ref[...], m_ref[...], v_ref[...])

    blk = pl.BlockSpec(w.shape, lambda i: (0, 0))
    return pl.pallas_call(
        body, name=name, out_shape=(jax.ShapeDtypeStruct(w.shape, F32),) * 3, grid=(1,),
        in_specs=[blk] * 4, out_specs=(blk,) * 3, compiler_params=_cparams("arbitrary"),
    )(w, g, m, v)


def _adamw_halves(name, pos_c, w, g_mine, g_other, m, v, tr, dep=None):
    r, c = w.shape
    rh = r // 2
    tr = tr if rh % tr == 0 else rh
    nt = rh // tr

    def body(p_ref, w_ref, gm_ref, go_ref, m_ref, v_ref, g_ref, d_ref, mo_ref, vo_ref):
        gg = jnp.where(pl.program_id(0) == p_ref[0], gm_ref[...], go_ref[...])
        g_ref[...] = gg
        d_ref[...], mo_ref[...], vo_ref[...] = _adamw_update(w_ref[...], gg, m_ref[...], v_ref[...])

    full = pl.BlockSpec((tr, c), lambda h, i, p: (h * nt + i, 0))
    half = pl.BlockSpec((tr, c), lambda h, i, p: (i, 0))
    dep_specs, dep_ops = _dep_args(dep)
    return pl.pallas_call(
        _after(body, 6, dep), name=name, out_shape=(jax.ShapeDtypeStruct((r, c), F32),) * 4,
        grid_spec=pltpu.PrefetchScalarGridSpec(
            num_scalar_prefetch=1, grid=(2, nt),
            in_specs=[full, half, half, full, full] + dep_specs, out_specs=(full,) * 4),
        compiler_params=_cparams("parallel", "parallel"),
    )(pos_c, w, g_mine, g_other, m, v, *dep_ops)


def _col_sharded_to_comm(g):
    k, n = g.shape
    return g.reshape(2, k // 2, N_CHIPS, n // N_CHIPS).transpose(2, 0, 1, 3)


def _row_sharded_to_comm(g):
    r, c = g.shape
    return g.reshape(N_CHIPS, 2, r // (2 * N_CHIPS), c)


def _col_sharded_full(g):
    _, _, rh, c = g.shape
    return g.reshape(N_CHIPS, 2 * rh, c).transpose(1, 0, 2).reshape(2 * rh, N_CHIPS * c)


def _row_sharded_full(g):
    _, _, rh, c = g.shape
    return g.reshape(N_CHIPS * 2 * rh, c)


def _pack_small(g1, bfv, mix, scale, g2n, gf, extra=None):
    row8 = jnp.pad(bfv.reshape(1, N_HEADS), ((0, 0), (0, LANES - N_HEADS)))
    if extra is not None:
        row8 = row8 + jnp.pad(extra[:, :1], ((0, 0), (N_HEADS, LANES - N_HEADS - 1)))
    return jnp.concatenate([
        g1.reshape(8, LANES), jnp.pad(row8, ((0, 7), (0, 0))), mix.reshape(512, LANES),
        jnp.pad(scale.reshape(4, LANES), ((0, 4), (0, 0))), g2n.reshape(8, LANES), gf.reshape(8, LANES)], axis=0)


def _unpack_small(s, like):
    g1, bfv, mix, scale, g2n, gf = like
    return (s[0:8].reshape(g1.shape), s[8, :N_HEADS].reshape(bfv.shape), s[16:528].reshape(mix.shape),
            s[528:532].reshape(scale.shape), s[536:544].reshape(g2n.shape), s[544:552].reshape(gf.shape))


class _MeshLinks:
    def __init__(self, shards_in, shards_rest):
        self.gin = _gather_begin("in", shards_in, None)
        self.grest = _gather_begin("rest", shards_rest, self.gin["token"])
        self.token = self.grest["token"]
        self.groups = {}

    def tie(self, x):
        return _tie(x, self.token)

    def weights_in(self, after):
        st = _gather_forward(self.gin, after)
        (g,) = _gather_end(st, st["token"])
        return _col_sharded_full(g)

    def rest_forward(self, after):
        self.grest = _gather_forward(self.grest, after)
        self.token = self.grest["token"]

    def weights_rest(self, after):
        g = _gather_end(self.grest, after)
        return [_col_sharded_full(g[0]), _col_sharded_full(g[1])] + [_row_sharded_full(a) for a in g[2:]]

    def reduce_begin(self, tag, grads):
        self.groups[tag] = _reduce_begin(tag, grads)
        self.token = self.groups[tag]["token"]

    def advance(self, after):
        for tag, st in self.groups.items():
            if "done" not in st:
                self.groups[tag] = _reduce_advance(st, after)
                if self.groups[tag]["token"] is not None:
                    self.token = self.groups[tag]["token"]

    def reduced(self, tag):
        return self.groups[tag]["done"]


class _NoLinks:
    token = None

    def __init__(self, w_in, rest):
        self.w_in, self.rest, self.grads = w_in, rest, {}

    def tie(self, x):
        return x

    def weights_in(self, after):
        return self.w_in

    def rest_forward(self, after):
        pass

    def weights_rest(self, after):
        return self.rest

    def reduce_begin(self, tag, grads):
        self.grads[tag] = grads

    def advance(self, after):
        pass


def _local_step(links, x, target, seq, norm1_g, b_forget, pool_mix, pool_scale, norm2_g, norm_f_g):
    t, d = x.shape
    tq = min(256, seq)
    aw = ATTN_WIDTH
    o_q, o_f, o_g = POOL_WIDTH, POOL_WIDTH + 3 * aw, POOL_WIDTH + 3 * aw + N_HEADS
    bf = jnp.pad(b_forget, ((0, 0), (0, LANES - N_HEADS)))
    mixb = pool_mix.astype(BF16)

    h = _norm_fwd("norm1_fwd", x, links.tie(norm1_g), 512)
    w_in = links.weights_in(h)
    wu = w_in[:, :o_q]
    wqkv = w_in[:, o_q:o_f]
    wf = jnp.pad(w_in[:, o_f:o_g], ((0, 0), (0, LANES - N_HEADS)))
    wg2 = w_in[:, o_g:]
    wft = wf.T
    u = _matmul("mm_u", h, wu, "nn", F32, 1024, 512, d)
    g2 = _matmul("mm_gates", h, wg2, "nn", F32, 1024, 512, d)
    fl, fcum = _forget_fwd(h, wf, bf, seq)
    qa, ka, v = _attn_prep(h, _head_blocks(wqkv[:, :aw]), _head_blocks(wqkv[:, aw:2 * aw]), wqkv[:, 2 * aw:], fcum, 512)
    p, ps = _pool_fwd(u, mixb, pool_scale, seq)
    links.rest_forward([ps, qa, g2])
    o, lse = _attn_fwd(qa, ka, v, seq, tq, dep=links.token)
    w_pool_out, w_attn_out, w_out, w_ffn_gate, w_ffn_up, w_ffn_down = links.weights_rest(o)
    merged, x1 = _merge_fwd(x, ps, o, g2, w_pool_out, w_attn_out, w_out, 256)
    h2, gt, up, act, x2 = _ffn_fwd(x1, norm2_g, w_ffn_gate, w_ffn_up, w_ffn_down, 1024, 256)
    loss, dx2, d_gf = _final_fwd_bwd(x2, target, norm_f_g, 512)

    dgt, dup, dx1, d_g2n = _ffn_bwd(dx2, x1, norm2_g, gt, up, w_ffn_gate, w_ffn_up, w_ffn_down, 1024, 256)
    d_wd = _matmul("dw_down", act, dx2, "tn", F32, 1408, 1024, 512)
    d_wg = _matmul("dw_gate", dgt, h2, "tn", F32, 1408, 1024, 512)
    d_wu = _matmul("dw_up", dup, h2, "tn", F32, 1408, 1024, 512)
    links.reduce_begin("a", [_row_sharded_to_comm(g) for g in (d_wg, d_wu, d_wd)])
    dpy, day, dg2, dps, da = _merge_bwd(dx1, ps, o, g2, w_pool_out, w_attn_out, w_out, 256, dep=links.token)
    links.advance(dps)
    d_wout = _matmul("dw_out", merged, dx1, "tn", F32, 1024, 1024, 512)
    d_wpo = _matmul("dw_pool_out", ps, dpy, "tn", F32, 512, 1024, 512)
    d_wao = _matmul("dw_attn_out", o, day, "tn", F32, 512, 1024, 512)
    dq, dk, dv, dfr = _attn_bwd(qa, ka, v, da, lse, seq, tq, dep=links.token)
    links.advance(dq)
    dfc = jnp.pad(dfr.reshape(N_HEADS, t).T, ((0, 0), (0, LANES - N_HEADS)))
    dfl, d_bf = _forget_bwd(dfc, fl, bf, seq)
    du, d_mix, d_scale = _pool_bwd(dps, p, mixb, links.tie(pool_scale), seq)
    d_wu_in = _matmul("dw_in_u", h, du, "tn", F32, 1024, 512, 512)
    d_wq = _matmul("dw_in_q", h, dq, "tn", F32, 1024, 512, 512)
    d_wk = _matmul("dw_in_k", h, dk, "tn", F32, 1024, 512, 512)
    d_wv = _matmul("dw_in_v", h, dv, "tn", F32, 1024, 512, 512)
    d_wf = _matmul("dw_in_f", h, dfl, "tn", F32, 1024, LANES, 512)
    d_wg2 = _matmul("dw_in_gates", h, dg2, "tn", F32, 1024, 1024, 512)
    d_win = jnp.concatenate([d_wu_in, d_wq, d_wk, d_wv, d_wf[:, :N_HEADS], d_wg2], axis=1)
    comm_b = [_col_sharded_to_comm(d_win), _col_sharded_to_comm(d_wpo), _col_sharded_to_comm(d_wao),
              _row_sharded_to_comm(d_wout)]
    links.advance(comm_b[0])
    links.reduce_begin("b", comm_b)
    dx, d_g1 = _in_bwd(du, dq, dk, dv, dg2, dfl, dx1, x, links.tie(norm1_g), wu, wqkv, wg2, wft, 256)
    links.advance(dx)
    small = (d_g1, d_bf[:, :N_HEADS], d_mix, d_scale, d_g2n, d_gf)
    return loss, dx, small


def kernel(x, norm1_g, w_in, b_forget, pool_mix, pool_scale, w_pool_out, w_attn_out, w_out, norm2_g, w_ffn_gate, w_ffn_up, w_ffn_down, norm_f_g, loss_target, m_norm1_g, m_w_in, m_b_forget, m_pool_mix, m_pool_scale, m_w_pool_out, m_w_attn_out, m_w_out, m_norm2_g, m_w_ffn_gate, m_w_ffn_up, m_w_ffn_down, m_norm_f_g, v_norm1_g, v_w_in, v_b_forget, v_pool_mix, v_pool_scale, v_w_pool_out, v_w_attn_out, v_w_out, v_norm2_g, v_w_ffn_gate, v_w_ffn_up, v_w_ffn_down, v_norm_f_g):
    nb, seq, d = x.shape
    group_a = ((w_ffn_gate, m_w_ffn_gate, v_w_ffn_gate, True, 9), (w_ffn_up, m_w_ffn_up, v_w_ffn_up, True, 10),
               (w_ffn_down, m_w_ffn_down, v_w_ffn_down, False, 11))
    group_b = ((w_in, m_w_in, v_w_in, False, 1), (w_pool_out, m_w_pool_out, v_w_pool_out, False, 5),
               (w_attn_out, m_w_attn_out, v_w_attn_out, False, 6), (w_out, m_w_out, v_w_out, False, 7))
    small_w = (norm1_g, b_forget, pool_mix, pool_scale, norm2_g, norm_f_g)
    small_m = (m_norm1_g, m_b_forget, m_pool_mix, m_pool_scale, m_norm2_g, m_norm_f_g)
    small_v = (v_norm1_g, v_b_forget, v_pool_mix, v_pool_scale, v_norm2_g, v_norm_f_g)
    small_pos = (0, 2, 3, 4, 8, 12)
    view = lambda a, tr: a[0].T if tr else a[0]
    unview = lambda a, tr, like: (a.T if tr else a).reshape(like.shape)

    def shard(w, tr):
        lw = view(w, tr).astype(BF16)
        return lw.reshape(2, lw.shape[0] // 2, lw.shape[1])

    links = _MeshLinks([shard(w_in, False)],
                       [shard(w_pool_out, False), shard(w_attn_out, False), shard(w_out, False),
                        shard(w_ffn_gate, True), shard(w_ffn_up, True), shard(w_ffn_down, False)])
    loss, dx, small_g = _local_step(
        links, x.reshape(nb * seq, d), loss_target.reshape(nb * seq, d), seq,
        norm1_g, b_forget, pool_mix[0], pool_scale, norm2_g, norm_f_g.reshape(1, d))

    grads, deltas, new_m, new_v = [None] * 13, [None] * 13, [None] * 13, [None] * 13
    pos_c = jnp.stack([lax.axis_index("c")]).astype(jnp.int32)

    def update(tag, group, dep):
        last = []
        for k, ((w, m, v, tr, pos), (mine, other)) in enumerate(zip(group, links.reduced(tag))):
            outs = _adamw_halves(f"adamw_{tag}{k}", pos_c, view(w, tr), mine, other, view(m, tr), view(v, tr), 256,
                                 dep=dep)
            grads[pos], deltas[pos], new_m[pos], new_v[pos] = (unview(a, tr, w) for a in outs)
            last.append(outs[1])
        return last

    last = update("a", group_a, links.token)
    links.advance(last)
    small_sum = _all_reduce_small(links.tie(_pack_small(*small_g, extra=loss)))
    loss_out = small_sum[8, N_HEADS]
    dl, mn, vn = _adamw("adamw_small", _pack_small(*small_w), small_sum * _small_mask(), _pack_small(*small_m),
                        _pack_small(*small_v))
    for pos, g, a, b, e in zip(small_pos, _unpack_small(small_sum, small_w), _unpack_small(dl, small_w),
                               _unpack_small(mn, small_w), _unpack_small(vn, small_w)):
        grads[pos], deltas[pos], new_m[pos], new_v[pos] = g, a, b, e
    links.advance(dl)
    links.advance(links.token)
    update("b", group_b, None)

    return (loss_out, dx.reshape(nb, seq, d), *grads, *deltas, *new_m, *new_v)


def _small_mask():
    rows = lax.broadcasted_iota(jnp.int32, (552, LANES), 0)
    lanes = lax.broadcasted_iota(jnp.int32, (552, LANES), 1)
    return jnp.where(jnp.logical_and(rows == 8, lanes == N_HEADS), 0.0, 1.0).astype(F32)
```

```python
import functools

import jax
import jax.numpy as jnp
from jax import lax
from jax.experimental import pallas as pl
from jax.experimental.pallas import tpu as pltpu

F32 = jnp.float32
BF16 = jnp.bfloat16

D_MODEL = 1024
POOL_WINDOWS = (2, 4, 8, 16)
POOL_GROUPS = 4
POOL_GROUP_DIM = 128
POOL_WIDTH = 512
HEAD_DIM = 64
N_HEADS = 8
ATTN_WIDTH = 512
D_FF = 2816
RMS_EPS = 1e-6
ATTN_SCALE = HEAD_DIM ** -0.5
NEG_BIG = -1e30

ADAM_LR = 0.001
ADAM_B1 = 0.9
ADAM_B2 = 0.999
ADAM_EPS = 1e-08
ADAM_WD = 0.01
ADAM_STEP = 10

LANES = 128
N_CHIPS = 4
N_DEV = 8
VMEM_LIMIT_V7X = 52 * 1024 * 1024
MESH = pl.DeviceIdType.MESH
ANY = pl.BlockSpec(memory_space=pl.ANY)


def _cparams(*sem):
    return pltpu.CompilerParams(dimension_semantics=sem if sem else None, vmem_limit_bytes=VMEM_LIMIT_V7X)


def _after(body, n_in, dep):
    if dep is None:
        return body

    def wrapped(*refs):
        body(*refs[:n_in], *refs[n_in + 1:])

    return wrapped


def _dep_args(dep):
    return ([], []) if dep is None else ([ANY], [dep])


def _dot(a, b):
    return lax.dot_general(a, b, (((1,), (0,)), ((), ())), preferred_element_type=F32)


def _dot_nt(a, b):
    return lax.dot_general(a, b, (((1,), (1,)), ((), ())), preferred_element_type=F32)


def _dot_tn(a, b):
    return lax.dot_general(a, b, (((0,), (0,)), ((), ())), preferred_element_type=F32)


def _sigmoid(x):
    return jax.nn.sigmoid(x)


def _rms_fwd(x, g):
    r = lax.rsqrt(jnp.mean(x * x, axis=-1, keepdims=True) + RMS_EPS)
    return (x * r) * g


def _rms_bwd(x, g, dy):
    r = lax.rsqrt(jnp.mean(x * x, axis=-1, keepdims=True) + RMS_EPS)
    xh = x * r
    dg = jnp.sum(dy * xh, axis=0, keepdims=True)
    dxh = dy * g
    dx = r * (dxh - xh * jnp.mean(dxh * xh, axis=-1, keepdims=True))
    return dx, dg


def _matmul(name, a, b, mode, out_dtype, tm, tn, tk, dep=None):
    if mode == "nn":
        (m, k), (_, n) = a.shape, b.shape
    elif mode == "nt":
        (m, k), (n, _) = a.shape, b.shape
    else:
        (k, m), (_, n) = a.shape, b.shape
    tm, tn, tk = min(tm, m), min(tn, n), min(tk, k)
    assert m % tm == 0 and n % tn == 0 and k % tk == 0, (name, m, n, k, tm, tn, tk)
    nk = k // tk
    if mode == "tn":
        a_spec = pl.BlockSpec((tk, tm), lambda i, j, kk: (kk, i))
    else:
        a_spec = pl.BlockSpec((tm, tk), lambda i, j, kk: (i, kk))
    if mode == "nt":
        b_spec = pl.BlockSpec((tn, tk), lambda i, j, kk: (j, kk))
    else:
        b_spec = pl.BlockSpec((tk, tn), lambda i, j, kk: (kk, j))
    dot = {"nn": _dot, "nt": _dot_nt, "tn": _dot_tn}[mode]
    use_scratch = nk > 1 and out_dtype != F32

    def body(a_ref, b_ref, o_ref, *scratch):
        prod = dot(a_ref[...].astype(BF16), b_ref[...].astype(BF16))
        if nk == 1:
            o_ref[...] = prod.astype(out_dtype)
            return
        acc = scratch[0] if use_scratch else o_ref
        kk = pl.program_id(2)

        @pl.when(kk == 0)
        def _():
            acc[...] = prod

        @pl.when(kk > 0)
        def _():
            acc[...] += prod

        if use_scratch:
            @pl.when(kk == nk - 1)
            def _():
                o_ref[...] = acc[...].astype(out_dtype)

    dep_specs, dep_ops = _dep_args(dep)
    return pl.pallas_call(
        _after(body, 2, dep),
        name=name,
        out_shape=jax.ShapeDtypeStruct((m, n), out_dtype),
        grid=(m // tm, n // tn, nk),
        in_specs=[a_spec, b_spec] + dep_specs,
        out_specs=pl.BlockSpec((tm, tn), lambda i, j, kk: (i, j)),
        scratch_shapes=[pltpu.VMEM((tm, tn), F32)] if use_scratch else [],
        compiler_params=_cparams("parallel", "parallel", "arbitrary"),
    )(a, b, *dep_ops)


def _norm_fwd(name, x, g, tm):
    t, d = x.shape
    tm = min(tm, t)

    def body(x_ref, g_ref, h_ref):
        h_ref[...] = _rms_fwd(x_ref[...], g_ref[...]).astype(BF16)

    return pl.pallas_call(
        body, name=name, out_shape=jax.ShapeDtypeStruct((t, d), BF16), grid=(t // tm,),
        in_specs=[pl.BlockSpec((tm, d), lambda i: (i, 0)), pl.BlockSpec((1, d), lambda i: (0, 0))],
        out_specs=pl.BlockSpec((tm, d), lambda i: (i, 0)),
        compiler_params=_cparams("parallel"),
    )(x, g)


def _split3(x):
    hi = x.astype(BF16)
    r1 = x - hi.astype(F32)
    mid = r1.astype(BF16)
    lo = (r1 - mid.astype(F32)).astype(BF16)
    return hi, mid, lo


def _tri_dot(tri, x):
    hi, mid, lo = _split3(x)
    return _dot(tri, hi) + _dot(tri, mid) + _dot(tri, lo)


def _forget_fwd(h, wf, bf, seq):
    t, d = h.shape
    cb = min(256, seq)

    def body(h_ref, wf_ref, bf_ref, fl_ref, fc_ref):
        fl = _dot(h_ref[...], wf_ref[...])
        fl_ref[...] = fl
        xx = fl + bf_ref[...]
        lf = jnp.minimum(xx, 0.0) - jnp.log(1.0 + jnp.exp(-jnp.abs(xx)))
        ri = lax.broadcasted_iota(jnp.int32, (cb, cb), 0)
        ci = lax.broadcasted_iota(jnp.int32, (cb, cb), 1)
        tri = (ri >= ci).astype(BF16)
        carry = jnp.zeros((1, LANES), F32)
        for blk in range(seq // cb):
            cs = _tri_dot(tri, lf[blk * cb:(blk + 1) * cb]) + carry
            fc_ref[blk * cb:(blk + 1) * cb, :] = cs
            carry = cs[cb - 1:cb, :]

    return pl.pallas_call(
        body, name="forget_fwd",
        out_shape=(jax.ShapeDtypeStruct((t, LANES), F32), jax.ShapeDtypeStruct((t, LANES), F32)),
        grid=(t // seq,),
        in_specs=[pl.BlockSpec((seq, d), lambda b: (b, 0)), pl.BlockSpec((d, LANES), lambda b: (0, 0)),
                  pl.BlockSpec((1, LANES), lambda b: (0, 0))],
        out_specs=(pl.BlockSpec((seq, LANES), lambda b: (b, 0)), pl.BlockSpec((seq, LANES), lambda b: (b, 0))),
        compiler_params=_cparams("parallel"),
    )(h, wf, bf)


def _pool_fwd(u, mix, scale, seq):
    t = u.shape[0]

    def body(u_ref, mix_ref, sc_ref, p_ref, ps_ref):
        tpos = lax.broadcasted_iota(jnp.int32, (seq, POOL_GROUP_DIM), 0)
        for g in range(POOL_GROUPS):
            sl = slice(g * POOL_GROUP_DIM, (g + 1) * POOL_GROUP_DIM)
            ug = u_ref[:, sl]
            s = ug
            for lvl in range(g + 1):
                d = 2 ** lvl
                s = s + jnp.where(tpos >= d, pltpu.roll(s, d, 0), 0.0)
            cnt = jnp.minimum(tpos + 1, POOL_WINDOWS[g]).astype(F32)
            pb = (s / cnt - ug).astype(BF16)
            p_ref[:, sl] = pb
            ps_ref[:, sl] = (_dot(pb, mix_ref[g]) * sc_ref[:, sl]).astype(BF16)

    return pl.pallas_call(
        body, name="pool_fwd",
        out_shape=(jax.ShapeDtypeStruct((t, POOL_WIDTH), BF16), jax.ShapeDtypeStruct((t, POOL_WIDTH), BF16)),
        grid=(t // seq,),
        in_specs=[pl.BlockSpec((seq, POOL_WIDTH), lambda b: (b, 0)),
                  pl.BlockSpec((POOL_GROUPS, POOL_GROUP_DIM, POOL_GROUP_DIM), lambda b: (0, 0, 0)),
                  pl.BlockSpec((1, POOL_WIDTH), lambda b: (0, 0))],
        out_specs=(pl.BlockSpec((seq, POOL_WIDTH), lambda b: (b, 0)), pl.BlockSpec((seq, POOL_WIDTH), lambda b: (b, 0))),
        compiler_params=_cparams("parallel"),
    )(u, mix, scale)


def _aug_constants():
    w = N_HEADS * LANES
    rows = jnp.arange(3 * LANES)
    piece, head = rows // LANES, rows % LANES
    cols = jnp.arange(w)
    live = (head < N_HEADS)[:, None]
    pq = (live & (cols[None, :] == (head * LANES + HEAD_DIM + piece)[:, None])).astype(BF16)
    pk = -(live & (cols[None, :] == (head * LANES + HEAD_DIM + 3 + piece)[:, None])).astype(BF16)
    lane = cols % LANES
    oq = ((lane >= HEAD_DIM + 3) & (lane < HEAD_DIM + 6)).astype(F32)[None, :]
    ok = ((lane >= HEAD_DIM) & (lane < HEAD_DIM + 3)).astype(F32)[None, :]
    return pq, pk, oq, ok


def _head_blocks(w):
    d = w.shape[0]
    return jnp.pad(w.reshape(d, N_HEADS, HEAD_DIM), ((0, 0), (0, 0), (0, LANES - HEAD_DIM))).reshape(d, N_HEADS * LANES)


def _attn_prep(h, wq, wk, wv, fcum, tm):
    t, d = h.shape
    tm = min(tm, t)
    w = N_HEADS * LANES
    pq, pk, oq, ok = _aug_constants()

    def body(h_ref, wq_ref, wk_ref, wv_ref, f_ref, pq_ref, pk_ref, oq_ref, ok_ref, qa_ref, ka_ref, v_ref):
        hh = h_ref[...]
        fs = jnp.concatenate(_split3(f_ref[...]), axis=1)
        q = _dot(hh, wq_ref[...]).astype(BF16).astype(F32) * ATTN_SCALE
        qa_ref[...] = (q + _dot(fs, pq_ref[...]) + oq_ref[...]).astype(BF16)
        k = _dot(hh, wk_ref[...]).astype(BF16).astype(F32)
        ka_ref[...] = (k + _dot(fs, pk_ref[...]) + ok_ref[...]).astype(BF16)
        v_ref[...] = _dot(hh, wv_ref[...]).astype(BF16)

    row = lambda n: pl.BlockSpec((tm, n), lambda i: (i, 0))
    full = lambda a: pl.BlockSpec(a.shape, lambda i: (0, 0))
    return pl.pallas_call(
        body, name="attn_prep",
        out_shape=(jax.ShapeDtypeStruct((t, w), BF16), jax.ShapeDtypeStruct((t, w), BF16),
                   jax.ShapeDtypeStruct((t, ATTN_WIDTH), BF16)),
        grid=(t // tm,),
        in_specs=[row(d), full(wq), full(wk), full(wv), row(LANES), full(pq), full(pk), full(oq), full(ok)],
        out_specs=(row(w), row(w), row(ATTN_WIDTH)),
        compiler_params=_cparams("parallel"),
    )(h, wq, wk, wv, fcum, pq, pk, oq, ok)


def _fold_lanes(x, op):
    out = x[:, :LANES]
    for g in range(1, x.shape[1] // LANES):
        out = op(out, x[:, g * LANES:(g + 1) * LANES])
    return out


def _attn_fwd(qa, ka, v, seq, tq, dep=None):
    t = qa.shape[0]
    nq = seq // tq
    hp_n = N_HEADS // 2
    heads = [slice(e * LANES, (e + 1) * LANES) for e in range(2)]

    def body(q_ref, k_ref, v_ref, o_ref, lse_ref, s_buf):
        i = pl.program_id(2)
        diag_ok = lax.broadcasted_iota(jnp.int32, (tq, tq), 0) >= lax.broadcasted_iota(jnp.int32, (tq, tq), 1)
        qs = [q_ref[:, hl] for hl in heads]

        def sweep1(j, mxs):
            r0 = pl.multiple_of(j * tq, tq)
            out = []
            for e, hl in enumerate(heads):
                s = _dot_nt(qs[e], k_ref[pl.ds(r0, tq), hl])
                s = jnp.where(jnp.logical_or(diag_ok, j < i), s, NEG_BIG)
                s_buf[e, j] = s
                out.append(jnp.maximum(mxs[e], _fold_lanes(s, jnp.maximum)))
            return tuple(out)

        mxs = lax.fori_loop(0, i + 1, sweep1, (jnp.full((tq, LANES), NEG_BIG, F32),) * 2)
        ms = [jnp.max(mx, axis=1, keepdims=True) for mx in mxs]

        def sweep2(j, carry):
            r0 = pl.multiple_of(j * tq, tq)
            vv = v_ref[pl.ds(r0, tq), :]
            out = []
            for e in range(2):
                p = jnp.exp(s_buf[e, j] - ms[e])
                out += [carry[2 * e] + _fold_lanes(p, jnp.add), carry[2 * e + 1] + _dot(p.astype(BF16), vv)]
            return tuple(out)

        res = lax.fori_loop(0, i + 1, sweep2, (jnp.zeros((tq, LANES), F32),) * 4)
        outs = []
        for e in range(2):
            l = jnp.sum(res[2 * e], axis=1, keepdims=True)
            outs.append(res[2 * e + 1] / l)
            lse_ref[:, e:e + 1] = ms[e] + jnp.log(l)
        lane = lax.broadcasted_iota(jnp.int32, (tq, LANES), 1)
        o_ref[...] = jnp.where(lane < HEAD_DIM, outs[0], outs[1])

    dep_specs, dep_ops = _dep_args(dep)
    return pl.pallas_call(
        _after(body, 3, dep), name="attn_fwd",
        out_shape=(jax.ShapeDtypeStruct((t, ATTN_WIDTH), F32), jax.ShapeDtypeStruct((hp_n, t, 2), F32)),
        grid=(t // seq, hp_n, nq),
        in_specs=[pl.BlockSpec((tq, 2 * LANES), lambda b, hp, i: (b * nq + i, hp)),
                  pl.BlockSpec((seq, 2 * LANES), lambda b, hp, i: (b, hp)),
                  pl.BlockSpec((seq, LANES), lambda b, hp, i: (b, hp))] + dep_specs,
        out_specs=(pl.BlockSpec((tq, LANES), lambda b, hp, i: (b * nq + i, hp)),
                   pl.BlockSpec((None, tq, 2), lambda b, hp, i: (hp, b * nq + i, 0))),
        scratch_shapes=[pltpu.VMEM((2, nq, tq, tq), F32)],
        compiler_params=_cparams("parallel", "parallel", "arbitrary"),
    )(qa, ka, v, *dep_ops)


def _merge_fwd(x, ps, o, g2, wpo, wao, wout, tm):
    t, d = x.shape
    tm = min(tm, t)

    def body(x_ref, ps_ref, o_ref, gp_ref, ga_ref, wpo_ref, wao_ref, wout_ref, mg_ref, x1_ref):
        py = _dot(ps_ref[...], wpo_ref[...])
        ay = _dot(o_ref[...].astype(BF16), wao_ref[...])
        mb = (_sigmoid(gp_ref[...].astype(F32)) * py + _sigmoid(ga_ref[...].astype(F32)) * ay).astype(BF16)
        mg_ref[...] = mb
        x1_ref[...] = x_ref[...] + _dot(mb, wout_ref[...])

    row = lambda w: pl.BlockSpec((tm, w), lambda i: (i, 0))
    full = lambda a: pl.BlockSpec(a.shape, lambda i: (0, 0))
    return pl.pallas_call(
        body, name="merge_fwd",
        out_shape=(jax.ShapeDtypeStruct((t, d), BF16), jax.ShapeDtypeStruct((t, d), F32)),
        grid=(t // tm,),
        in_specs=[row(d), row(POOL_WIDTH), row(ATTN_WIDTH), pl.BlockSpec((tm, d), lambda i: (i, 0)),
                  pl.BlockSpec((tm, d), lambda i: (i, 1)), full(wpo), full(wao), full(wout)],
        out_specs=(row(d), row(d)),
        compiler_params=_cparams("parallel"),
    )(x, ps, o, g2, g2, wpo, wao, wout)


def _ffn_fwd(x1, g, wg, wu, wd, tm, tf):
    t, d = x1.shape
    f = wg.shape[0]
    tm = min(tm, t)
    nf = f // tf

    def body(x1_ref, g_ref, wg_ref, wu_ref, wd_ref, h2_ref, gt_ref, up_ref, act_ref, x2_ref):
        j = pl.program_id(1)

        @pl.when(j == 0)
        def _():
            h2_ref[...] = _rms_fwd(x1_ref[...], g_ref[...]).astype(BF16)

        h2 = h2_ref[...]
        gt = _dot_nt(h2, wg_ref[...])
        up = _dot_nt(h2, wu_ref[...])
        act = (gt * _sigmoid(gt) * up).astype(BF16)
        gt_ref[...] = gt.astype(BF16)
        up_ref[...] = up.astype(BF16)
        act_ref[...] = act
        prod = _dot(act, wd_ref[...])

        @pl.when(j == 0)
        def _():
            x2_ref[...] = prod

        @pl.when(j > 0)
        def _():
            x2_ref[...] += prod

        @pl.when(j == nf - 1)
        def _():
            x2_ref[...] += x1_ref[...]

    return pl.pallas_call(
        body, name="ffn_fwd",
        out_shape=(jax.ShapeDtypeStruct((t, d), BF16), jax.ShapeDtypeStruct((t, f), BF16),
                   jax.ShapeDtypeStruct((t, f), BF16), jax.ShapeDtypeStruct((t, f), BF16),
                   jax.ShapeDtypeStruct((t, d), F32)),
        grid=(t // tm, nf),
        in_specs=[pl.BlockSpec((tm, d), lambda i, j: (i, 0)), pl.BlockSpec((1, d), lambda i, j: (0, 0)),
                  pl.BlockSpec((tf, d), lambda i, j: (j, 0)), pl.BlockSpec((tf, d), lambda i, j: (j, 0)),
                  pl.BlockSpec((tf, d), lambda i, j: (j, 0))],
        out_specs=(pl.BlockSpec((tm, d), lambda i, j: (i, 0)), pl.BlockSpec((tm, tf), lambda i, j: (i, j)),
                   pl.BlockSpec((tm, tf), lambda i, j: (i, j)), pl.BlockSpec((tm, tf), lambda i, j: (i, j)),
                   pl.BlockSpec((tm, d), lambda i, j: (i, 0))),
        compiler_params=_cparams("parallel", "arbitrary"),
    )(x1, g, wg, wu, wd)


def _final_fwd_bwd(x2, target, g, tm):
    t, d = x2.shape
    tm = min(tm, t)

    def body(x_ref, t_ref, g_ref, loss_ref, dx_ref, dg_ref):
        i = pl.program_id(0)
        x = x_ref[...]
        gg = g_ref[...]
        err = _rms_fwd(x, gg) - t_ref[...]
        part = 0.5 * jnp.sum(jnp.mean(err * err, axis=-1, keepdims=True), axis=0, keepdims=True)
        dx, dg = _rms_bwd(x, gg, err * (1.0 / d))
        dx_ref[...] = dx

        @pl.when(i == 0)
        def _():
            loss_ref[...] = jnp.zeros_like(loss_ref)
            dg_ref[...] = jnp.zeros_like(dg_ref)

        loss_ref[...] += jnp.broadcast_to(part, loss_ref.shape)
        dg_ref[...] += dg

    return pl.pallas_call(
        body, name="final_fwd_bwd",
        out_shape=(jax.ShapeDtypeStruct((1, LANES), F32), jax.ShapeDtypeStruct((t, d), F32),
                   jax.ShapeDtypeStruct((1, d), F32)),
        grid=(t // tm,),
        in_specs=[pl.BlockSpec((tm, d), lambda i: (i, 0)), pl.BlockSpec((tm, d), lambda i: (i, 0)),
                  pl.BlockSpec((1, d), lambda i: (0, 0))],
        out_specs=(pl.BlockSpec((1, LANES), lambda i: (0, 0)), pl.BlockSpec((tm, d), lambda i: (i, 0)),
                   pl.BlockSpec((1, d), lambda i: (0, 0))),
        compiler_params=_cparams("arbitrary"),
    )(x2, target, g)


def _ffn_bwd(dx2, x1, g, gt, up, wg, wu, wd, tm, tf):
    t, d = dx2.shape
    f = gt.shape[1]
    tm = min(tm, t)
    nf = f // tf

    def body(dx2_ref, x1_ref, g_ref, gt_ref, up_ref, wg_ref, wu_ref, wd_ref, dgt_ref, dup_ref, dx1_ref, dg_ref, acc_ref,
             dxb_ref):
        i, j = pl.program_id(0), pl.program_id(1)

        @pl.when(j == 0)
        def _():
            dxb_ref[...] = dx2_ref[...].astype(BF16)

        dact = _dot_nt(dxb_ref[...], wd_ref[...])
        gtv = gt_ref[...].astype(F32)
        sg = _sigmoid(gtv)
        dup = (dact * (gtv * sg)).astype(BF16)
        dgt = (dact * up_ref[...].astype(F32) * (sg * (1.0 + gtv * (1.0 - sg)))).astype(BF16)
        dgt_ref[...] = dgt
        dup_ref[...] = dup
        contrib = _dot(dgt, wg_ref[...]) + _dot(dup, wu_ref[...])

        @pl.when(j == 0)
        def _():
            acc_ref[...] = contrib

        @pl.when(j > 0)
        def _():
            acc_ref[...] += contrib

        @pl.when(jnp.logical_and(i == 0, j == 0))
        def _():
            dg_ref[...] = jnp.zeros_like(dg_ref)

        @pl.when(j == nf - 1)
        def _():
            dxn, dg = _rms_bwd(x1_ref[...], g_ref[...], acc_ref[...])
            dx1_ref[...] = dx2_ref[...] + dxn
            dg_ref[...] += dg

    return pl.pallas_call(
        body, name="ffn_bwd",
        out_shape=(jax.ShapeDtypeStruct((t, f), BF16), jax.ShapeDtypeStruct((t, f), BF16),
                   jax.ShapeDtypeStruct((t, d), F32), jax.ShapeDtypeStruct((1, d), F32)),
        grid=(t // tm, nf),
        in_specs=[pl.BlockSpec((tm, d), lambda i, j: (i, 0)), pl.BlockSpec((tm, d), lambda i, j: (i, 0)),
                  pl.BlockSpec((1, d), lambda i, j: (0, 0)),
                  pl.BlockSpec((tm, tf), lambda i, j: (i, j)), pl.BlockSpec((tm, tf), lambda i, j: (i, j)),
                  pl.BlockSpec((tf, d), lambda i, j: (j, 0)), pl.BlockSpec((tf, d), lambda i, j: (j, 0)),
                  pl.BlockSpec((tf, d), lambda i, j: (j, 0))],
        out_specs=(pl.BlockSpec((tm, tf), lambda i, j: (i, j)), pl.BlockSpec((tm, tf), lambda i, j: (i, j)),
                   pl.BlockSpec((tm, d), lambda i, j: (i, 0)), pl.BlockSpec((1, d), lambda i, j: (0, 0))),
        scratch_shapes=[pltpu.VMEM((tm, d), F32), pltpu.VMEM((tm, d), BF16)],
        compiler_params=_cparams("arbitrary", "arbitrary"),
    )(dx2, x1, g, gt, up, wg, wu, wd)


def _merge_bwd(dx1, ps, o, g2, wpo, wao, wout, tm, dep=None):
    t, d = dx1.shape
    tm = min(tm, t)

    def body(dx1_ref, ps_ref, o_ref, gp_ref, ga_ref, wpo_ref, wao_ref, wout_ref, dpy_ref, day_ref, dg2_ref, dps_ref, da_ref):
        dm = _dot_nt(dx1_ref[...].astype(BF16), wout_ref[...])
        py = _dot(ps_ref[...], wpo_ref[...])
        ay = _dot(o_ref[...].astype(BF16), wao_ref[...])
        sp = _sigmoid(gp_ref[...].astype(F32))
        sa = _sigmoid(ga_ref[...].astype(F32))
        dpy = (dm * sp).astype(BF16)
        day = (dm * sa).astype(BF16)
        dpy_ref[...] = dpy
        day_ref[...] = day
        dg2_ref[:, :d] = (dm * py * (sp * (1.0 - sp))).astype(BF16)
        dg2_ref[:, d:] = (dm * ay * (sa * (1.0 - sa))).astype(BF16)
        dps_ref[...] = _dot_nt(dpy, wpo_ref[...])
        da_ref[...] = _dot_nt(day, wao_ref[...]).astype(BF16)

    row = lambda w: pl.BlockSpec((tm, w), lambda i: (i, 0))
    full = lambda a: pl.BlockSpec(a.shape, lambda i: (0, 0))
    dep_specs, dep_ops = _dep_args(dep)
    return pl.pallas_call(
        _after(body, 8, dep), name="merge_bwd",
        out_shape=(jax.ShapeDtypeStruct((t, d), BF16), jax.ShapeDtypeStruct((t, d), BF16),
                   jax.ShapeDtypeStruct((t, 2 * d), BF16), jax.ShapeDtypeStruct((t, POOL_WIDTH), F32),
                   jax.ShapeDtypeStruct((t, ATTN_WIDTH), BF16)),
        grid=(t // tm,),
        in_specs=[row(d), row(POOL_WIDTH), row(ATTN_WIDTH), pl.BlockSpec((tm, d), lambda i: (i, 0)),
                  pl.BlockSpec((tm, d), lambda i: (i, 1)), full(wpo), full(wao), full(wout)] + dep_specs,
        out_specs=(row(d), row(d), row(2 * d), row(POOL_WIDTH), row(ATTN_WIDTH)),
        compiler_params=_cparams("parallel"),
    )(dx1, ps, o, g2, g2, wpo, wao, wout, *dep_ops)


def _attn_bwd(qa, ka, v, do, lse4, seq, tq, dep=None):
    t = qa.shape[0]
    nq = seq // tq
    hp_n = N_HEADS // 2
    heads = [slice(e * LANES, (e + 1) * LANES) for e in range(2)]

    def body(q_ref, k_ref, v_ref, do_ref, lse_ref, dq_ref, dk_ref, dv_ref, dfr_ref, dk_acc, dv_acc, p_buf, dp_buf):
        diag_ok = lax.broadcasted_iota(jnp.int32, (tq, tq), 0) >= lax.broadcasted_iota(jnp.int32, (tq, tq), 1)
        lane_q = lax.broadcasted_iota(jnp.int32, (tq, LANES), 1)
        lane_s = lax.broadcasted_iota(jnp.int32, (seq, LANES), 1)
        mine_q = [lane_q < HEAD_DIM, lane_q >= HEAD_DIM]
        dv_acc[...] = jnp.zeros_like(dv_acc)
        dk_acc[...] = jnp.zeros_like(dk_acc)
        dfr_ref[...] = jnp.zeros_like(dfr_ref)

        def q_step(i, _):
            q0 = pl.multiple_of(i * tq, tq)
            qs = [q_ref[pl.ds(q0, tq), hl] for hl in heads]
            dov = do_ref[pl.ds(q0, tq), :]
            dos = [jnp.where(mq, dov, jnp.zeros((), BF16)) for mq in mine_q]
            lss = [lse_ref[pl.ds(q0, tq), e:e + 1] for e in range(2)]

            def sweep1(j, dls):
                r0 = pl.multiple_of(j * tq, tq)
                vv = v_ref[pl.ds(r0, tq), :]
                out = []
                for e, hl in enumerate(heads):
                    s = _dot_nt(qs[e], k_ref[pl.ds(r0, tq), hl])
                    s = jnp.where(jnp.logical_or(diag_ok, j < i), s, NEG_BIG)
                    p = jnp.exp(s - lss[e])
                    dp = _dot_nt(dos[e], vv)
                    p_buf[e, j] = p
                    dp_buf[e, j] = dp
                    dv_acc[pl.ds(r0, tq), :] += _dot_tn(p.astype(BF16), dos[e])
                    out.append(dls[e] + _fold_lanes(p * dp, jnp.add))
                return tuple(out)

            dls = lax.fori_loop(0, i + 1, sweep1, (jnp.zeros((tq, LANES), F32),) * 2)
            dls = [jnp.sum(d, axis=1, keepdims=True) for d in dls]

            def sweep2(j, dqs):
                r0 = pl.multiple_of(j * tq, tq)
                out = []
                for e, hl in enumerate(heads):
                    ds = p_buf[e, j] * (dp_buf[e, j] - dls[e])
                    dfr_ref[e, pl.ds(j, 1), :] += jnp.sum(ds, axis=0, keepdims=True)
                    dsb = ds.astype(BF16)
                    dk_acc[e, pl.ds(r0, tq), :] += _dot_tn(dsb, qs[e])
                    out.append(dqs[e] + _dot(dsb, k_ref[pl.ds(r0, tq), hl]))
                return tuple(out)

            dqs = lax.fori_loop(0, i + 1, sweep2, (jnp.zeros((tq, LANES), F32),) * 2)
            dq = jnp.where(mine_q[0], dqs[0], pltpu.roll(dqs[1], HEAD_DIM, 1)) * ATTN_SCALE
            dq_ref[pl.ds(q0, tq), :] = dq.astype(BF16)
            return 0

        lax.fori_loop(0, nq, q_step, 0)
        dk_ref[...] = jnp.where(lane_s < HEAD_DIM, dk_acc[0], pltpu.roll(dk_acc[1], HEAD_DIM, 1)).astype(BF16)
        dv_ref[...] = dv_acc[...].astype(BF16)

    wide = pl.BlockSpec((seq, 2 * LANES), lambda b, hp: (b, hp))
    col = pl.BlockSpec((seq, LANES), lambda b, hp: (b, hp))
    pair = pl.BlockSpec((None, seq, 2), lambda b, hp: (hp, b, 0))
    dep_specs, dep_ops = _dep_args(dep)
    return pl.pallas_call(
        _after(body, 5, dep), name="attn_bwd",
        out_shape=(jax.ShapeDtypeStruct((t, ATTN_WIDTH), BF16),) * 3 + (jax.ShapeDtypeStruct((N_HEADS, t // tq, tq), F32),),
        grid=(t // seq, hp_n),
        in_specs=[wide, wide, col, col, pair] + dep_specs,
        out_specs=(col, col, col, pl.BlockSpec((2, nq, tq), lambda b, hp: (hp, b, 0))),
        scratch_shapes=[pltpu.VMEM((2, seq, LANES), F32), pltpu.VMEM((seq, LANES), F32),
                        pltpu.VMEM((2, nq, tq, tq), F32), pltpu.VMEM((2, nq, tq, tq), F32)],
        compiler_params=_cparams("parallel", "arbitrary"),
    )(qa, ka, v, do, lse4, *dep_ops)


def _forget_bwd(dfc, fl, bf, seq):
    t = fl.shape[0]
    cb = min(256, seq)
    nb = seq // cb

    def body(dfc_ref, fl_ref, bf_ref, dfl_ref, db_ref):
        b = pl.program_id(0)
        ri = lax.broadcasted_iota(jnp.int32, (cb, cb), 0)
        ci = lax.broadcasted_iota(jnp.int32, (cb, cb), 1)
        tri = (ci >= ri).astype(BF16)
        carry = jnp.zeros((1, LANES), F32)
        dbs = jnp.zeros((1, LANES), F32)
        for blk in reversed(range(nb)):
            rs = slice(blk * cb, (blk + 1) * cb)
            dlf = _tri_dot(tri, -dfc_ref[rs, :]) + carry
            carry = dlf[0:1, :]
            dfl = dlf * _sigmoid(-(fl_ref[rs, :] + bf_ref[...]))
            dfl_ref[rs, :] = dfl.astype(BF16)
            dbs = dbs + jnp.sum(dfl, axis=0, keepdims=True)

        @pl.when(b == 0)
        def _():
            db_ref[...] = jnp.zeros_like(db_ref)

        db_ref[...] += dbs

    return pl.pallas_call(
        body, name="forget_bwd",
        out_shape=(jax.ShapeDtypeStruct((t, LANES), BF16), jax.ShapeDtypeStruct((1, LANES), F32)),
        grid=(t // seq,),
        in_specs=[pl.BlockSpec((seq, LANES), lambda b: (b, 0)), pl.BlockSpec((seq, LANES), lambda b: (b, 0)),
                  pl.BlockSpec((1, LANES), lambda b: (0, 0))],
        out_specs=(pl.BlockSpec((seq, LANES), lambda b: (b, 0)), pl.BlockSpec((1, LANES), lambda b: (0, 0))),
        compiler_params=_cparams("arbitrary"),
    )(dfc, fl, bf)


def _pool_bwd(dps, p, mix, scale, seq):
    t = dps.shape[0]

    def body(dps_ref, p_ref, mix_ref, sc_ref, du_ref, dmix_ref, dsc_ref):
        b = pl.program_id(0)

        @pl.when(b == 0)
        def _():
            dmix_ref[...] = jnp.zeros_like(dmix_ref)
            dsc_ref[...] = jnp.zeros_like(dsc_ref)

        tpos = lax.broadcasted_iota(jnp.int32, (seq, POOL_GROUP_DIM), 0)
        for g in range(POOL_GROUPS):
            sl = slice(g * POOL_GROUP_DIM, (g + 1) * POOL_GROUP_DIM)
            pb = p_ref[:, sl]
            dpsg = dps_ref[:, sl]
            pm = _dot(pb, mix_ref[g])
            dsc_ref[:, sl] += jnp.sum(dpsg * pm, axis=0, keepdims=True)
            dpm = (dpsg * sc_ref[:, sl]).astype(BF16)
            dmix_ref[g] += _dot_tn(pb, dpm)
            dp = _dot_nt(dpm, mix_ref[g])
            cnt = jnp.minimum(tpos + 1, POOL_WINDOWS[g]).astype(F32)
            s = dp / cnt
            for lvl in range(g + 1):
                d = 2 ** lvl
                s = s + jnp.where(tpos < seq - d, pltpu.roll(s, seq - d, 0), 0.0)
            du_ref[:, sl] = (s - dp).astype(BF16)

    return pl.pallas_call(
        body, name="pool_bwd",
        out_shape=(jax.ShapeDtypeStruct((t, POOL_WIDTH), BF16),
                   jax.ShapeDtypeStruct((POOL_GROUPS, POOL_GROUP_DIM, POOL_GROUP_DIM), F32),
                   jax.ShapeDtypeStruct((1, POOL_WIDTH), F32)),
        grid=(t // seq,),
        in_specs=[pl.BlockSpec((seq, POOL_WIDTH), lambda b: (b, 0)), pl.BlockSpec((seq, POOL_WIDTH), lambda b: (b, 0)),
                  pl.BlockSpec((POOL_GROUPS, POOL_GROUP_DIM, POOL_GROUP_DIM), lambda b: (0, 0, 0)),
                  pl.BlockSpec((1, POOL_WIDTH), lambda b: (0, 0))],
        out_specs=(pl.BlockSpec((seq, POOL_WIDTH), lambda b: (b, 0)),
                   pl.BlockSpec((POOL_GROUPS, POOL_GROUP_DIM, POOL_GROUP_DIM), lambda b: (0, 0, 0)),
                   pl.BlockSpec((1, POOL_WIDTH), lambda b: (0, 0))),
        compiler_params=_cparams("arbitrary"),
    )(dps, p, mix, scale)


def _in_bwd(du, dq, dk, dv, dg2, dfl, dx1, x, g, wu, wqkv, wg2, wft, tm):
    t, d = x.shape
    tm = min(tm, t)
    aw = ATTN_WIDTH

    def body(du_ref, dq_ref, dk_ref, dv_ref, dg2_ref, dfl_ref, dx1_ref, x_ref, g_ref, wu_ref, wqkv_ref, wg2_ref, wft_ref,
             dx_ref, dg_ref):
        i = pl.program_id(0)
        dh = _dot_nt(du_ref[...], wu_ref[...])
        dh += _dot_nt(dq_ref[...], wqkv_ref[:, 0:aw])
        dh += _dot_nt(dk_ref[...], wqkv_ref[:, aw:2 * aw])
        dh += _dot_nt(dv_ref[...], wqkv_ref[:, 2 * aw:3 * aw])
        dh += _dot_nt(dg2_ref[...], wg2_ref[...])
        dh += _dot(dfl_ref[...], wft_ref[...])
        dxn, dg = _rms_bwd(x_ref[...], g_ref[...], dh)
        dx_ref[...] = dx1_ref[...] + dxn

        @pl.when(i == 0)
        def _():
            dg_ref[...] = jnp.zeros_like(dg_ref)

        dg_ref[...] += dg

    row = lambda w: pl.BlockSpec((tm, w), lambda i: (i, 0))
    full = lambda a: pl.BlockSpec(a.shape, lambda i: (0, 0))
    return pl.pallas_call(
        body, name="in_bwd",
        out_shape=(jax.ShapeDtypeStruct((t, d), F32), jax.ShapeDtypeStruct((1, d), F32)),
        grid=(t // tm,),
        in_specs=[row(POOL_WIDTH), row(aw), row(aw), row(aw), row(2 * d), row(LANES), row(d), row(d),
                  pl.BlockSpec((1, d), lambda i: (0, 0)), full(wu), full(wqkv), full(wg2), full(wft)],
        out_specs=(row(d), pl.BlockSpec((1, d), lambda i: (0, 0))),
        compiler_params=_cparams("arbitrary"),
    )(du, dq, dk, dv, dg2, dfl, dx1, x, g, wu, wqkv, wg2, wft)


def _position():
    return lax.axis_index("x"), lax.axis_index("y"), lax.axis_index("c")


def _remote(src, dst, send_sem, recv_sem, device):
    return pltpu.make_async_remote_copy(src_ref=src, dst_ref=dst, send_sem=send_sem, recv_sem=recv_sem,
                                        device_id=device, device_id_type=MESH)


HBM = pl.BlockSpec(memory_space=pltpu.HBM)
SEM = pl.BlockSpec(memory_space=pltpu.SEMAPHORE)
DATAFLOW = pltpu.SideEffectType.DATAFLOW_SIDE_EFFECTING


def _copies_start(name, arrays, plan, m, dep=None):
    n = len(arrays)
    arrays = [pltpu.with_memory_space_constraint(a, pltpu.HBM) for a in arrays]

    def body(*refs):
        ins, send_sem, recv_sem, token = refs[:n], refs[n], refs[n + 1], refs[2 * n + 2]
        for i, (src, dst, device, _) in enumerate(plan(ins, *_position())):
            _remote(src, dst, send_sem.at[i], recv_sem.at[i], device).start()
        token[...] = jnp.zeros_like(token)

    dep_specs, dep_ops = _dep_args(dep)
    outs = pl.pallas_call(
        _after(body, n, dep), name=name,
        out_shape=(pltpu.SemaphoreType.DMA((m,)), pltpu.SemaphoreType.DMA((m,)),
                   *[pltpu.HBM(a.shape, a.dtype) for a in arrays], jax.ShapeDtypeStruct((8, LANES), F32)),
        in_specs=[HBM] * n + dep_specs, out_specs=(SEM, SEM, *[HBM] * n, pl.BlockSpec(memory_space=pltpu.VMEM)),
        input_output_aliases={i: i + 2 for i in range(n)},
        compiler_params=pltpu.CompilerParams(has_side_effects=DATAFLOW),
    )(*arrays, *dep_ops)
    return (outs[0], outs[1]), list(outs[2:2 + n]), outs[2 + n]


def _copies_wait(name, sems, arrays, plan, after):
    n = len(arrays)
    afters = list(after) if isinstance(after, (list, tuple)) else [after]

    def body(*refs):
        ins, send_sem, recv_sem = refs[:n], refs[n], refs[n + 1]
        for i, (src, dst, device, landing) in enumerate(plan(ins, *_position())):
            _remote(src, dst, send_sem.at[i], recv_sem.at[i], device).wait_send()
            _remote(landing, landing, send_sem.at[i], recv_sem.at[i], device).wait_recv()

    outs = pl.pallas_call(
        body, name=name,
        out_shape=tuple(pltpu.HBM(a.shape, a.dtype) for a in arrays),
        in_specs=[HBM] * n + [SEM, SEM] + [ANY] * len(afters), out_specs=tuple([HBM] * n),
        input_output_aliases={i: i for i in range(n)},
        compiler_params=pltpu.CompilerParams(has_side_effects=DATAFLOW),
    )(*arrays, sems[0], sems[1], *afters)
    return list(outs)


def _tie(x, token):
    return x if token is None else x + token[0, 0]


def _other_chips(x, y):
    return [(1 - x, y), (x, 1 - y), (1 - x, 1 - y)]


def _gather_begin(tag, shards, token):
    n = len(shards)
    lands = [lax.empty((N_CHIPS,) + s.shape, s.dtype) for s in shards]

    def plan(refs, x, y, c):
        return [(refs[k].at[c], refs[n + k].at[2 * x + y, c], (ox, oy, c), refs[n + k].at[2 * ox + oy, c])
                for k in range(n) for ox, oy in _other_chips(x, y)]

    sems, thru, token = _copies_start(f"gather_{tag}_ici_start", list(shards) + lands, plan, 3 * n, dep=token)
    return dict(tag=tag, n=n, plan=plan, sems=sems, arrays=thru, token=token)


def _gather_forward(st, after):
    n, tag = st["n"], st["tag"]
    thru = _copies_wait(f"gather_{tag}_ici_wait", st["sems"], st["arrays"], st["plan"], after)

    def plan(refs, x, y, c):
        return [(refs[k].at[2 * ox + oy, c], refs[k].at[2 * ox + oy, c], (x, y, 1 - c), refs[k].at[2 * ox + oy, 1 - c])
                for k in range(n) for ox, oy in _other_chips(x, y)]

    sems, lands, token = _copies_start(f"gather_{tag}_fwd_start", thru[n:], plan, 3 * n)
    return dict(tag=tag, n=n, plan=plan, sems=sems, arrays=lands, token=token, shards=thru[:n])


def _gather_end(st, after):
    lands = _copies_wait(f"gather_{st['tag']}_fwd_wait", st["sems"], st["arrays"], st["plan"], after)
    me = 2 * lax.axis_index("x") + lax.axis_index("y")
    return [lax.dynamic_update_index_in_dim(g, s, me, 0) for g, s in zip(lands, st["shards"])]


def _add_keep_give(name, pos, a, a_keep, a_give, b, b_keep, b_give, steps):
    r, c = a.shape[-2:]

    def spec(arr, fn):
        lead = arr.ndim - 2
        return pl.BlockSpec((None,) * lead + (r, c), lambda i, p: tuple(fn(i, p)) + (0, 0))

    out_spec = pl.BlockSpec((None, r, c), lambda i, p: (i, 0, 0))

    def body(p_ref, ak_ref, bk_ref, ag_ref, bg_ref, keep_ref, give_ref):
        keep_ref[...] = ak_ref[...] + bk_ref[...].astype(F32)
        give_ref[...] = (ag_ref[...] + bg_ref[...].astype(F32)).astype(BF16)

    return pl.pallas_call(
        body, name=name,
        out_shape=(jax.ShapeDtypeStruct((steps, r, c), F32), jax.ShapeDtypeStruct((steps, r, c), BF16)),
        grid_spec=pltpu.PrefetchScalarGridSpec(
            num_scalar_prefetch=1, grid=(steps,),
            in_specs=[spec(a, a_keep), spec(b, b_keep), spec(a, a_give), spec(b, b_give)],
            out_specs=(out_spec, out_spec)),
        compiler_params=_cparams("parallel"),
    )(pos, a, b, a, b)


def _add_last(name, a, b):
    _, r, c = a.shape
    blk = pl.BlockSpec((None, r, c), lambda i: (0, 0, 0))

    def body(a_ref, b_ref, o_ref):
        o_ref[...] = a_ref[...] + b_ref[...].astype(F32)

    return pl.pallas_call(
        body, name=name, out_shape=jax.ShapeDtypeStruct((r, c), F32), grid=(1,), in_specs=[blk, blk],
        out_specs=pl.BlockSpec((r, c), lambda i: (0, 0)), compiler_params=_cparams("arbitrary"),
    )(a, b)


def _exchange_begin(tag, stage, gives, lands, peer_fn, extra):
    n = len(gives)

    def plan(refs, x, y, c):
        return [(refs[k], refs[n + k], peer_fn(x, y, c), refs[n + k]) for k in range(n)]

    sems, thru, token = _copies_start(f"rs{tag}_{stage}_start", gives + lands, plan, n)
    return dict(extra, tag=tag, n=n, stage=stage, plan=plan, sems=sems, arrays=thru, token=token)


def _reduce_begin(tag, grads):
    n = len(grads)
    lands = [lax.empty((N_CHIPS,) + g.shape[2:], F32) for g in grads]

    def plan(refs, x, y, c):
        return [(refs[k].at[j, 1 - c], refs[n + k].at[j], (x, y, 1 - c), refs[n + k].at[j])
                for k in range(n) for j in range(N_CHIPS)]

    sems, thru, token = _copies_start(f"rs{tag}_c_start", list(grads) + lands, plan, N_CHIPS * n)
    return dict(tag=tag, n=n, stage="c", plan=plan, sems=sems, arrays=thru, token=token)


def _reduce_advance(st, after):
    tag, n, stage = st["tag"], st["n"], st["stage"]
    thru = _copies_wait(f"rs{tag}_{stage}_wait", st["sems"], st["arrays"], st["plan"], after)
    first, recv = thru[:n], thru[n:]
    x, y, c = _position()
    if stage == "c":
        pos = jnp.stack([c, x]).astype(jnp.int32)
        sums = [_add_keep_give(
            f"rs{tag}_c_add{k}", pos,
            first[k], lambda i, p: (2 * p[1] + i, p[0]), lambda i, p: (2 * (1 - p[1]) + i, p[0]),
            recv[k], lambda i, p: (2 * p[1] + i,), lambda i, p: (2 * (1 - p[1]) + i,), 2) for k in range(n)]
        lands = [lax.empty(s[1].shape, BF16) for s in sums]
        return _exchange_begin(tag, "x", [s[1] for s in sums], lands, lambda x, y, c: (1 - x, y, c),
                               dict(keep=[s[0] for s in sums]))
    if stage == "x":
        pos = jnp.stack([y]).astype(jnp.int32)
        sums = [_add_keep_give(
            f"rs{tag}_x_add{k}", pos,
            st["keep"][k], lambda i, p: (p[0],), lambda i, p: (1 - p[0],),
            recv[k], lambda i, p: (p[0],), lambda i, p: (1 - p[0],), 1) for k in range(n)]
        lands = [lax.empty(s[1].shape, BF16) for s in sums]
        return _exchange_begin(tag, "y", [s[1] for s in sums], lands, lambda x, y, c: (x, 1 - y, c),
                               dict(keep=[s[0] for s in sums]))
    if stage == "y":
        mine = [_add_last(f"rs{tag}_y_add{k}", st["keep"][k], recv[k]) for k in range(n)]
        lands = [lax.empty(m.shape, F32) for m in mine]
        return _exchange_begin(tag, "swap", mine, lands, lambda x, y, c: (x, y, 1 - c), {})
    return dict(done=list(zip(first, recv)), token=None)


def _all_reduce_small(v):
    r = v.shape[0]

    def body(v_ref, out_ref, buf, send_sems, recv_sems, local_sem):
        x, y, c = _position()
        me, sibling = (x, y, c), (x, y, 1 - c)
        chips = [(1 - x, y), (x, 1 - y), (1 - x, 1 - y)]

        def rows(px, py, pc):
            return buf.at[pl.ds((4 * px + 2 * py + pc) * r, r), :]

        def copy(k, block, to, src=None):
            return _remote(rows(*block) if src is None else src, rows(*block), send_sems.at[k], recv_sems.at[k], to)

        mine = pltpu.make_async_copy(v_ref, rows(*me), local_sem)
        mine.start()
        first = [copy(0, me, sibling, src=v_ref)]
        first += [copy(1 + j, me, (*chip, c), src=v_ref) for j, chip in enumerate(chips)]
        for cp in first:
            cp.start()
        passed = [copy(4 + j, (*chip, c), sibling) for j, chip in enumerate(chips)]
        for j, chip in enumerate(chips):
            copy(1 + j, (*chip, c), me).wait_recv()
            passed[j].start()
        copy(0, sibling, me).wait_recv()
        for j, chip in enumerate(chips):
            copy(4 + j, (*chip, 1 - c), me).wait_recv()
        for cp in first + passed:
            cp.wait_send()
        mine.wait()
        acc = buf[0:r, :]
        for dev in range(1, N_DEV):
            acc = acc + buf[dev * r:(dev + 1) * r, :]
        out_ref[...] = acc

    return pl.pallas_call(
        body, name="all_reduce_small",
        out_shape=jax.ShapeDtypeStruct(v.shape, F32),
        in_specs=[pl.BlockSpec(memory_space=pltpu.VMEM)],
        out_specs=pl.BlockSpec(memory_space=pltpu.VMEM),
        scratch_shapes=[pltpu.VMEM((N_DEV * r, LANES), F32), pltpu.SemaphoreType.DMA((7,)),
                        pltpu.SemaphoreType.DMA((7,)), pltpu.SemaphoreType.DMA],
        compiler_params=pltpu.CompilerParams(has_side_effects=True, vmem_limit_bytes=VMEM_LIMIT_V7X),
    )(v)


def _adamw_update(w, gg, m, v):
    mn = ADAM_B1 * m + (1.0 - ADAM_B1) * gg
    vn = ADAM_B2 * v + (1.0 - ADAM_B2) * (gg * gg)
    m_hat = mn / (1.0 - ADAM_B1 ** ADAM_STEP)
    v_hat = vn / (1.0 - ADAM_B2 ** ADAM_STEP)
    return -ADAM_LR * (m_hat / (jnp.sqrt(v_hat) + ADAM_EPS) + ADAM_WD * w), mn, vn


def _adamw(name, w, g, m, v):
    def body(w_ref, g_ref, m_ref, v_ref, d_ref, mo_ref, vo_ref):
        d_ref[...], mo_ref[...], vo_ref[...] = _adamw_update(w_ref[...], g_ref[...], m_ref[...], v_ref[...])

    blk = pl.BlockSpec(w.shape, lambda i: (0, 0))
    return pl.pallas_call(
        body, name=name, out_shape=(jax.ShapeDtypeStruct(w.shape, F32),) * 3, grid=(1,),
        in_specs=[blk] * 4, out_specs=(blk,) * 3, compiler_params=_cparams("arbitrary"),
    )(w, g, m, v)


def _adamw_halves(name, pos_c, w, g_mine, g_other, m, v, tr, dep=None):
    r, c = w.shape
    rh = r // 2
    tr = tr if rh % tr == 0 else rh
    nt = rh // tr

    def body(p_ref, w_ref, gm_ref, go_ref, m_ref, v_ref, g_ref, d_ref, mo_ref, vo_ref):
        gg = jnp.where(pl.program_id(0) == p_ref[0], gm_ref[...], go_ref[...])
        g_ref[...] = gg
        d_ref[...], mo_ref[...], vo_ref[...] = _adamw_update(w_ref[...], gg, m_ref[...], v_ref[...])

    full = pl.BlockSpec((tr, c), lambda h, i, p: (h * nt + i, 0))
    half = pl.BlockSpec((tr, c), lambda h, i, p: (i, 0))
    dep_specs, dep_ops = _dep_args(dep)
    return pl.pallas_call(
        _after(body, 6, dep), name=name, out_shape=(jax.ShapeDtypeStruct((r, c), F32),) * 4,
        grid_spec=pltpu.PrefetchScalarGridSpec(
            num_scalar_prefetch=1, grid=(2, nt),
            in_specs=[full, half, half, full, full] + dep_specs, out_specs=(full,) * 4),
        compiler_params=_cparams("parallel", "parallel"),
    )(pos_c, w, g_mine, g_other, m, v, *dep_ops)


def _col_sharded_to_comm(g):
    k, n = g.shape
    return g.reshape(2, k // 2, N_CHIPS, n // N_CHIPS).transpose(2, 0, 1, 3)


def _row_sharded_to_comm(g):
    r, c = g.shape
    return g.reshape(N_CHIPS, 2, r // (2 * N_CHIPS), c)


def _col_sharded_full(g):
    _, _, rh, c = g.shape
    return g.reshape(N_CHIPS, 2 * rh, c).transpose(1, 0, 2).reshape(2 * rh, N_CHIPS * c)


def _row_sharded_full(g):
    _, _, rh, c = g.shape
    return g.reshape(N_CHIPS * 2 * rh, c)


def _pack_small(g1, bfv, mix, scale, g2n, gf, extra=None):
    row8 = jnp.pad(bfv.reshape(1, N_HEADS), ((0, 0), (0, LANES - N_HEADS)))
    if extra is not None:
        row8 = row8 + jnp.pad(extra[:, :1], ((0, 0), (N_HEADS, LANES - N_HEADS - 1)))
    return jnp.concatenate([
        g1.reshape(8, LANES), jnp.pad(row8, ((0, 7), (0, 0))), mix.reshape(512, LANES),
        jnp.pad(scale.reshape(4, LANES), ((0, 4), (0, 0))), g2n.reshape(8, LANES), gf.reshape(8, LANES)], axis=0)


def _unpack_small(s, like):
    g1, bfv, mix, scale, g2n, gf = like
    return (s[0:8].reshape(g1.shape), s[8, :N_HEADS].reshape(bfv.shape), s[16:528].reshape(mix.shape),
            s[528:532].reshape(scale.shape), s[536:544].reshape(g2n.shape), s[544:552].reshape(gf.shape))


class _MeshLinks:
    def __init__(self, shards_in, shards_rest):
        self.gin = _gather_begin("in", shards_in, None)
        self.grest = _gather_begin("rest", shards_rest, self.gin["token"])
        self.token = self.grest["token"]
        self.groups = {}

    def tie(self, x):
        return _tie(x, self.token)

    def weights_in(self, after):
        st = _gather_forward(self.gin, after)
        (g,) = _gather_end(st, st["token"])
        return _col_sharded_full(g)

    def rest_forward(self, after):
        self.grest = _gather_forward(self.grest, after)
        self.token = self.grest["token"]

    def weights_rest(self, after):
        g = _gather_end(self.grest, after)
        return [_col_sharded_full(g[0]), _col_sharded_full(g[1])] + [_row_sharded_full(a) for a in g[2:]]

    def reduce_begin(self, tag, grads):
        self.groups[tag] = _reduce_begin(tag, grads)
        self.token = self.groups[tag]["token"]

    def advance(self, after):
        for tag, st in self.groups.items():
            if "done" not in st:
                self.groups[tag] = _reduce_advance(st, after)
                if self.groups[tag]["token"] is not None:
                    self.token = self.groups[tag]["token"]

    def reduced(self, tag):
        return self.groups[tag]["done"]


class _NoLinks:
    token = None

    def __init__(self, w_in, rest):
        self.w_in, self.rest, self.grads = w_in, rest, {}

    def tie(self, x):
        return x

    def weights_in(self, after):
        return self.w_in

    def rest_forward(self, after):
        pass

    def weights_rest(self, after):
        return self.rest

    def reduce_begin(self, tag, grads):
        self.grads[tag] = grads

    def advance(self, after):
        pass


def _local_step(links, x, target, seq, norm1_g, b_forget, pool_mix, pool_scale, norm2_g, norm_f_g):
    t, d = x.shape
    tq = min(256, seq)
    aw = ATTN_WIDTH
    o_q, o_f, o_g = POOL_WIDTH, POOL_WIDTH + 3 * aw, POOL_WIDTH + 3 * aw + N_HEADS
    bf = jnp.pad(b_forget, ((0, 0), (0, LANES - N_HEADS)))
    mixb = pool_mix.astype(BF16)

    h = _norm_fwd("norm1_fwd", x, links.tie(norm1_g), 512)
    w_in = links.weights_in(h)
    wu = w_in[:, :o_q]
    wqkv = w_in[:, o_q:o_f]
    wf = jnp.pad(w_in[:, o_f:o_g], ((0, 0), (0, LANES - N_HEADS)))
    wg2 = w_in[:, o_g:]
    wft = wf.T
    u = _matmul("mm_u", h, wu, "nn", F32, 1024, 512, d)
    g2 = _matmul("mm_gates", h, wg2, "nn", BF16, 1024, 512, d)
    fl, fcum = _forget_fwd(h, wf, bf, seq)
    qa, ka, v = _attn_prep(h, _head_blocks(wqkv[:, :aw]), _head_blocks(wqkv[:, aw:2 * aw]), wqkv[:, 2 * aw:], fcum, 512)
    p, ps = _pool_fwd(u, mixb, pool_scale, seq)
    links.rest_forward([ps, qa, g2])
    o, lse = _attn_fwd(qa, ka, v, seq, tq, dep=links.token)
    w_pool_out, w_attn_out, w_out, w_ffn_gate, w_ffn_up, w_ffn_down = links.weights_rest(o)
    merged, x1 = _merge_fwd(x, ps, o, g2, w_pool_out, w_attn_out, w_out, 256)
    h2, gt, up, act, x2 = _ffn_fwd(x1, norm2_g, w_ffn_gate, w_ffn_up, w_ffn_down, 1024, 256)
    loss, dx2, d_gf = _final_fwd_bwd(x2, target, norm_f_g, 512)

    dgt, dup, dx1, d_g2n = _ffn_bwd(dx2, x1, norm2_g, gt, up, w_ffn_gate, w_ffn_up, w_ffn_down, 1024, 256)
    d_wd = _matmul("dw_down", act, dx2, "tn", F32, 1408, 1024, 512)
    d_wg = _matmul("dw_gate", dgt, h2, "tn", F32, 1408, 1024, 512)
    d_wu = _matmul("dw_up", dup, h2, "tn", F32, 1408, 1024, 512)
    links.reduce_begin("a", [_row_sharded_to_comm(g) for g in (d_wg, d_wu, d_wd)])
    dpy, day, dg2, dps, da = _merge_bwd(dx1, ps, o, g2, w_pool_out, w_attn_out, w_out, 256, dep=links.token)
    links.advance(dps)
    d_wout = _matmul("dw_out", merged, dx1, "tn", F32, 1024, 1024, 512)
    d_wpo = _matmul("dw_pool_out", ps, dpy, "tn", F32, 512, 1024, 512)
    d_wao = _matmul("dw_attn_out", o, day, "tn", F32, 512, 1024, 512)
    links.reduce_begin("m", [_col_sharded_to_comm(d_wpo), _col_sharded_to_comm(d_wao), _row_sharded_to_comm(d_wout)])
    dq, dk, dv, dfr = _attn_bwd(qa, ka, v, da, lse, seq, tq, dep=links.token)
    links.advance(dq)
    dfc = jnp.pad(dfr.reshape(N_HEADS, t).T, ((0, 0), (0, LANES - N_HEADS)))
    dfl, d_bf = _forget_bwd(dfc, fl, bf, seq)
    du, d_mix, d_scale = _pool_bwd(dps, p, mixb, links.tie(pool_scale), seq)
    d_wu_in = _matmul("dw_in_u", h, du, "tn", F32, 1024, 512, 512)
    d_wq = _matmul("dw_in_q", h, dq, "tn", F32, 1024, 512, 512)
    d_wk = _matmul("dw_in_k", h, dk, "tn", F32, 1024, 512, 512)
    d_wv = _matmul("dw_in_v", h, dv, "tn", F32, 1024, 512, 512)
    links.advance([d_wu_in, d_wq, d_wk, d_wv])
    d_wf = _matmul("dw_in_f", h, dfl, "tn", F32, 1024, LANES, 512)
    d_wg2 = _matmul("dw_in_gates", h, dg2, "tn", F32, 1024, 1024, 512, dep=links.token)
    d_win = jnp.concatenate([d_wu_in, d_wq, d_wk, d_wv, d_wf[:, :N_HEADS], d_wg2], axis=1)
    comm_b = [_col_sharded_to_comm(d_win)]
    links.advance(comm_b)
    links.reduce_begin("b", comm_b)
    dx, d_g1 = _in_bwd(du, dq, dk, dv, dg2, dfl, dx1, x, links.tie(norm1_g), wu, wqkv, wg2, wft, 256)
    links.advance(dx)
    small = (d_g1, d_bf[:, :N_HEADS], d_mix, d_scale, d_g2n, d_gf)
    return loss, dx, small


def kernel(x, norm1_g, w_in, b_forget, pool_mix, pool_scale, w_pool_out, w_attn_out, w_out, norm2_g, w_ffn_gate, w_ffn_up, w_ffn_down, norm_f_g, loss_target, m_norm1_g, m_w_in, m_b_forget, m_pool_mix, m_pool_scale, m_w_pool_out, m_w_attn_out, m_w_out, m_norm2_g, m_w_ffn_gate, m_w_ffn_up, m_w_ffn_down, m_norm_f_g, v_norm1_g, v_w_in, v_b_forget, v_pool_mix, v_pool_scale, v_w_pool_out, v_w_attn_out, v_w_out, v_norm2_g, v_w_ffn_gate, v_w_ffn_up, v_w_ffn_down, v_norm_f_g):
    nb, seq, d = x.shape
    group_a = ((w_ffn_gate, m_w_ffn_gate, v_w_ffn_gate, True, 9), (w_ffn_up, m_w_ffn_up, v_w_ffn_up, True, 10),
               (w_ffn_down, m_w_ffn_down, v_w_ffn_down, False, 11))
    group_m = ((w_pool_out, m_w_pool_out, v_w_pool_out, False, 5), (w_attn_out, m_w_attn_out, v_w_attn_out, False, 6),
               (w_out, m_w_out, v_w_out, False, 7))
    group_b = ((w_in, m_w_in, v_w_in, False, 1),)
    small_w = (norm1_g, b_forget, pool_mix, pool_scale, norm2_g, norm_f_g)
    small_m = (m_norm1_g, m_b_forget, m_pool_mix, m_pool_scale, m_norm2_g, m_norm_f_g)
    small_v = (v_norm1_g, v_b_forget, v_pool_mix, v_pool_scale, v_norm2_g, v_norm_f_g)
    small_pos = (0, 2, 3, 4, 8, 12)
    view = lambda a, tr: a[0].T if tr else a[0]
    unview = lambda a, tr, like: (a.T if tr else a).reshape(like.shape)

    def shard(w, tr):
        lw = view(w, tr).astype(BF16)
        return lw.reshape(2, lw.shape[0] // 2, lw.shape[1])

    links = _MeshLinks([shard(w_in, False)],
                       [shard(w_pool_out, False), shard(w_attn_out, False), shard(w_out, False),
                        shard(w_ffn_gate, True), shard(w_ffn_up, True), shard(w_ffn_down, False)])
    loss, dx, small_g = _local_step(
        links, x.reshape(nb * seq, d), loss_target.reshape(nb * seq, d), seq,
        norm1_g, b_forget, pool_mix[0], pool_scale, norm2_g, norm_f_g.reshape(1, d))

    grads, deltas, new_m, new_v = [None] * 13, [None] * 13, [None] * 13, [None] * 13
    pos_c = jnp.stack([lax.axis_index("c")]).astype(jnp.int32)

    def update(tag, group, dep):
        last = []
        for k, ((w, m, v, tr, pos), (mine, other)) in enumerate(zip(group, links.reduced(tag))):
            outs = _adamw_halves(f"adamw_{tag}{k}", pos_c, view(w, tr), mine, other, view(m, tr), view(v, tr), 256,
                                 dep=dep)
            grads[pos], deltas[pos], new_m[pos], new_v[pos] = (unview(a, tr, w) for a in outs)
            last.append(outs[1])
        return last

    last = update("a", group_a, links.token) + update("m", group_m, links.token)
    links.advance(last)
    small_sum = _all_reduce_small(links.tie(_pack_small(*small_g, extra=loss)))
    loss_out = small_sum[8, N_HEADS]
    dl, mn, vn = _adamw("adamw_small", _pack_small(*small_w), small_sum * _small_mask(), _pack_small(*small_m),
                        _pack_small(*small_v))
    for pos, g, a, b, e in zip(small_pos, _unpack_small(small_sum, small_w), _unpack_small(dl, small_w),
                               _unpack_small(mn, small_w), _unpack_small(vn, small_w)):
        grads[pos], deltas[pos], new_m[pos], new_v[pos] = g, a, b, e
    links.advance(dl)
    links.advance(links.token)
    update("b", group_b, None)

    return (loss_out, dx.reshape(nb, seq, d), *grads, *deltas, *new_m, *new_v)


def _small_mask():
    rows = lax.broadcasted_iota(jnp.int32, (552, LANES), 0)
    lanes = lax.broadcasted_iota(jnp.int32, (552, LANES), 1)
    return jnp.where(jnp.logical_and(rows == 8, lanes == N_HEADS), 0.0, 1.0).astype(F32)
```

```python
import functools

import jax
import jax.numpy as jnp
from jax import lax
from jax.experimental import pallas as pl
from jax.experimental.pallas import tpu as pltpu

F32 = jnp.float32
BF16 = jnp.bfloat16

D_MODEL = 1024
POOL_WINDOWS = (2, 4, 8, 16)
POOL_GROUPS = 4
POOL_GROUP_DIM = 128
POOL_WIDTH = 512
HEAD_DIM = 64
N_HEADS = 8
ATTN_WIDTH = 512
D_FF = 2816
RMS_EPS = 1e-6
ATTN_SCALE = HEAD_DIM ** -0.5
NEG_BIG = -1e30

ADAM_LR = 0.001
ADAM_B1 = 0.9
ADAM_B2 = 0.999
ADAM_EPS = 1e-08
ADAM_WD = 0.01
ADAM_STEP = 10

LANES = 128
N_CHIPS = 4
N_DEV = 8
VMEM_LIMIT_V7X = 52 * 1024 * 1024
MESH = pl.DeviceIdType.MESH
ANY = pl.BlockSpec(memory_space=pl.ANY)


def _cparams(*sem):
    return pltpu.CompilerParams(dimension_semantics=sem if sem else None, vmem_limit_bytes=VMEM_LIMIT_V7X)


def _dep_list(dep):
    return [] if dep is None else (list(dep) if isinstance(dep, (list, tuple)) else [dep])


def _after(body, n_in, dep):
    k = len(_dep_list(dep))
    if k == 0:
        return body

    def wrapped(*refs):
        body(*refs[:n_in], *refs[n_in + k:])

    return wrapped


def _dep_args(dep):
    deps = _dep_list(dep)
    return [ANY] * len(deps), deps


def _dot(a, b):
    return lax.dot_general(a, b, (((1,), (0,)), ((), ())), preferred_element_type=F32)


def _dot_nt(a, b):
    return lax.dot_general(a, b, (((1,), (1,)), ((), ())), preferred_element_type=F32)


def _dot_tn(a, b):
    return lax.dot_general(a, b, (((0,), (0,)), ((), ())), preferred_element_type=F32)


def _sigmoid(x):
    return jax.nn.sigmoid(x)


def _rms_fwd(x, g):
    r = lax.rsqrt(jnp.mean(x * x, axis=-1, keepdims=True) + RMS_EPS)
    return (x * r) * g


def _rms_bwd(x, g, dy):
    r = lax.rsqrt(jnp.mean(x * x, axis=-1, keepdims=True) + RMS_EPS)
    xh = x * r
    dg = jnp.sum(dy * xh, axis=0, keepdims=True)
    dxh = dy * g
    dx = r * (dxh - xh * jnp.mean(dxh * xh, axis=-1, keepdims=True))
    return dx, dg


def _matmul(name, a, b, mode, out_dtype, tm, tn, tk, dep=None):
    if mode == "nn":
        (m, k), (_, n) = a.shape, b.shape
    elif mode == "nt":
        (m, k), (n, _) = a.shape, b.shape
    else:
        (k, m), (_, n) = a.shape, b.shape
    tm, tn, tk = min(tm, m), min(tn, n), min(tk, k)
    assert m % tm == 0 and n % tn == 0 and k % tk == 0, (name, m, n, k, tm, tn, tk)
    nk = k // tk
    if mode == "tn":
        a_spec = pl.BlockSpec((tk, tm), lambda i, j, kk: (kk, i))
    else:
        a_spec = pl.BlockSpec((tm, tk), lambda i, j, kk: (i, kk))
    if mode == "nt":
        b_spec = pl.BlockSpec((tn, tk), lambda i, j, kk: (j, kk))
    else:
        b_spec = pl.BlockSpec((tk, tn), lambda i, j, kk: (kk, j))
    dot = {"nn": _dot, "nt": _dot_nt, "tn": _dot_tn}[mode]
    use_scratch = nk > 1 and out_dtype != F32

    def body(a_ref, b_ref, o_ref, *scratch):
        prod = dot(a_ref[...].astype(BF16), b_ref[...].astype(BF16))
        if nk == 1:
            o_ref[...] = prod.astype(out_dtype)
            return
        acc = scratch[0] if use_scratch else o_ref
        kk = pl.program_id(2)

        @pl.when(kk == 0)
        def _():
            acc[...] = prod

        @pl.when(kk > 0)
        def _():
            acc[...] += prod

        if use_scratch:
            @pl.when(kk == nk - 1)
            def _():
                o_ref[...] = acc[...].astype(out_dtype)

    dep_specs, dep_ops = _dep_args(dep)
    return pl.pallas_call(
        _after(body, 2, dep),
        name=name,
        out_shape=jax.ShapeDtypeStruct((m, n), out_dtype),
        grid=(m // tm, n // tn, nk),
        in_specs=[a_spec, b_spec] + dep_specs,
        out_specs=pl.BlockSpec((tm, tn), lambda i, j, kk: (i, j)),
        scratch_shapes=[pltpu.VMEM((tm, tn), F32)] if use_scratch else [],
        compiler_params=_cparams("parallel", "parallel", "arbitrary"),
    )(a, b, *dep_ops)


def _norm_fwd(name, x, g, tm):
    t, d = x.shape
    tm = min(tm, t)

    def body(x_ref, g_ref, h_ref):
        h_ref[...] = _rms_fwd(x_ref[...], g_ref[...]).astype(BF16)

    return pl.pallas_call(
        body, name=name, out_shape=jax.ShapeDtypeStruct((t, d), BF16), grid=(t // tm,),
        in_specs=[pl.BlockSpec((tm, d), lambda i: (i, 0)), pl.BlockSpec((1, d), lambda i: (0, 0))],
        out_specs=pl.BlockSpec((tm, d), lambda i: (i, 0)),
        compiler_params=_cparams("parallel"),
    )(x, g)


def _split3(x):
    hi = x.astype(BF16)
    r1 = x - hi.astype(F32)
    mid = r1.astype(BF16)
    lo = (r1 - mid.astype(F32)).astype(BF16)
    return hi, mid, lo


def _tri_dot(tri, x):
    hi, mid, lo = _split3(x)
    return _dot(tri, hi) + _dot(tri, mid) + _dot(tri, lo)


def _forget_fwd(h, wf, bf, seq):
    t, d = h.shape
    cb = min(256, seq)

    def body(h_ref, wf_ref, bf_ref, fl_ref, fc_ref):
        fl = _dot(h_ref[...], wf_ref[...])
        fl_ref[...] = fl
        xx = fl + bf_ref[...]
        lf = jnp.minimum(xx, 0.0) - jnp.log(1.0 + jnp.exp(-jnp.abs(xx)))
        ri = lax.broadcasted_iota(jnp.int32, (cb, cb), 0)
        ci = lax.broadcasted_iota(jnp.int32, (cb, cb), 1)
        tri = (ri >= ci).astype(BF16)
        carry = jnp.zeros((1, LANES), F32)
        for blk in range(seq // cb):
            cs = _tri_dot(tri, lf[blk * cb:(blk + 1) * cb]) + carry
            fc_ref[blk * cb:(blk + 1) * cb, :] = cs
            carry = cs[cb - 1:cb, :]

    return pl.pallas_call(
        body, name="forget_fwd",
        out_shape=(jax.ShapeDtypeStruct((t, LANES), F32), jax.ShapeDtypeStruct((t, LANES), F32)),
        grid=(t // seq,),
        in_specs=[pl.BlockSpec((seq, d), lambda b: (b, 0)), pl.BlockSpec((d, LANES), lambda b: (0, 0)),
                  pl.BlockSpec((1, LANES), lambda b: (0, 0))],
        out_specs=(pl.BlockSpec((seq, LANES), lambda b: (b, 0)), pl.BlockSpec((seq, LANES), lambda b: (b, 0))),
        compiler_params=_cparams("parallel"),
    )(h, wf, bf)


def _pool_fwd(u, mix, scale, seq):
    t = u.shape[0]

    def body(u_ref, mix_ref, sc_ref, p_ref, ps_ref):
        tpos = lax.broadcasted_iota(jnp.int32, (seq, POOL_GROUP_DIM), 0)
        for g in range(POOL_GROUPS):
            sl = slice(g * POOL_GROUP_DIM, (g + 1) * POOL_GROUP_DIM)
            ug = u_ref[:, sl]
            s = ug
            for lvl in range(g + 1):
                d = 2 ** lvl
                s = s + jnp.where(tpos >= d, pltpu.roll(s, d, 0), 0.0)
            cnt = jnp.minimum(tpos + 1, POOL_WINDOWS[g]).astype(F32)
            pb = (s / cnt - ug).astype(BF16)
            p_ref[:, sl] = pb
            ps_ref[:, sl] = (_dot(pb, mix_ref[g]) * sc_ref[:, sl]).astype(BF16)

    return pl.pallas_call(
        body, name="pool_fwd",
        out_shape=(jax.ShapeDtypeStruct((t, POOL_WIDTH), BF16), jax.ShapeDtypeStruct((t, POOL_WIDTH), BF16)),
        grid=(t // seq,),
        in_specs=[pl.BlockSpec((seq, POOL_WIDTH), lambda b: (b, 0)),
                  pl.BlockSpec((POOL_GROUPS, POOL_GROUP_DIM, POOL_GROUP_DIM), lambda b: (0, 0, 0)),
                  pl.BlockSpec((1, POOL_WIDTH), lambda b: (0, 0))],
        out_specs=(pl.BlockSpec((seq, POOL_WIDTH), lambda b: (b, 0)), pl.BlockSpec((seq, POOL_WIDTH), lambda b: (b, 0))),
        compiler_params=_cparams("parallel"),
    )(u, mix, scale)


def _aug_constants():
    w = N_HEADS * LANES
    rows = jnp.arange(3 * LANES)
    piece, head = rows // LANES, rows % LANES
    cols = jnp.arange(w)
    live = (head < N_HEADS)[:, None]
    pq = (live & (cols[None, :] == (head * LANES + HEAD_DIM + piece)[:, None])).astype(BF16)
    pk = -(live & (cols[None, :] == (head * LANES + HEAD_DIM + 3 + piece)[:, None])).astype(BF16)
    lane = cols % LANES
    oq = ((lane >= HEAD_DIM + 3) & (lane < HEAD_DIM + 6)).astype(F32)[None, :]
    ok = ((lane >= HEAD_DIM) & (lane < HEAD_DIM + 3)).astype(F32)[None, :]
    return pq, pk, oq, ok


def _head_blocks(w):
    d = w.shape[0]
    return jnp.pad(w.reshape(d, N_HEADS, HEAD_DIM), ((0, 0), (0, 0), (0, LANES - HEAD_DIM))).reshape(d, N_HEADS * LANES)


def _attn_prep(h, wq, wk, wv, fcum, tm):
    t, d = h.shape
    tm = min(tm, t)
    w = N_HEADS * LANES
    pq, pk, oq, ok = _aug_constants()

    def body(h_ref, wq_ref, wk_ref, wv_ref, f_ref, pq_ref, pk_ref, oq_ref, ok_ref, qa_ref, ka_ref, v_ref):
        hh = h_ref[...]
        fs = jnp.concatenate(_split3(f_ref[...]), axis=1)
        q = _dot(hh, wq_ref[...]).astype(BF16).astype(F32) * ATTN_SCALE
        qa_ref[...] = (q + _dot(fs, pq_ref[...]) + oq_ref[...]).astype(BF16)
        k = _dot(hh, wk_ref[...]).astype(BF16).astype(F32)
        ka_ref[...] = (k + _dot(fs, pk_ref[...]) + ok_ref[...]).astype(BF16)
        v_ref[...] = _dot(hh, wv_ref[...]).astype(BF16)

    row = lambda n: pl.BlockSpec((tm, n), lambda i: (i, 0))
    full = lambda a: pl.BlockSpec(a.shape, lambda i: (0, 0))
    return pl.pallas_call(
        body, name="attn_prep",
        out_shape=(jax.ShapeDtypeStruct((t, w), BF16), jax.ShapeDtypeStruct((t, w), BF16),
                   jax.ShapeDtypeStruct((t, ATTN_WIDTH), BF16)),
        grid=(t // tm,),
        in_specs=[row(d), full(wq), full(wk), full(wv), row(LANES), full(pq), full(pk), full(oq), full(ok)],
        out_specs=(row(w), row(w), row(ATTN_WIDTH)),
        compiler_params=_cparams("parallel"),
    )(h, wq, wk, wv, fcum, pq, pk, oq, ok)


def _fold_lanes(x, op):
    out = x[:, :LANES]
    for g in range(1, x.shape[1] // LANES):
        out = op(out, x[:, g * LANES:(g + 1) * LANES])
    return out


def _attn_fwd(qa, ka, v, seq, tq, dep=None):
    t = qa.shape[0]
    nq = seq // tq
    hp_n = N_HEADS // 2
    heads = [slice(e * LANES, (e + 1) * LANES) for e in range(2)]

    def body(q_ref, k_ref, v_ref, o_ref, lse_ref, s_buf):
        i = pl.program_id(2)
        diag_ok = lax.broadcasted_iota(jnp.int32, (tq, tq), 0) >= lax.broadcasted_iota(jnp.int32, (tq, tq), 1)
        qs = [q_ref[:, hl] for hl in heads]

        def sweep1(j, mxs):
            r0 = pl.multiple_of(j * tq, tq)
            out = []
            for e, hl in enumerate(heads):
                s = _dot_nt(qs[e], k_ref[pl.ds(r0, tq), hl])
                s = jnp.where(jnp.logical_or(diag_ok, j < i), s, NEG_BIG)
                s_buf[e, j] = s
                out.append(jnp.maximum(mxs[e], _fold_lanes(s, jnp.maximum)))
            return tuple(out)

        mxs = lax.fori_loop(0, i + 1, sweep1, (jnp.full((tq, LANES), NEG_BIG, F32),) * 2)
        ms = [jnp.max(mx, axis=1, keepdims=True) for mx in mxs]

        def sweep2(j, carry):
            r0 = pl.multiple_of(j * tq, tq)
            vv = v_ref[pl.ds(r0, tq), :]
            out = []
            for e in range(2):
                p = jnp.exp(s_buf[e, j] - ms[e])
                out += [carry[2 * e] + _fold_lanes(p, jnp.add), carry[2 * e + 1] + _dot(p.astype(BF16), vv)]
            return tuple(out)

        res = lax.fori_loop(0, i + 1, sweep2, (jnp.zeros((tq, LANES), F32),) * 4)
        outs = []
        for e in range(2):
            l = jnp.sum(res[2 * e], axis=1, keepdims=True)
            outs.append(res[2 * e + 1] / l)
            lse_ref[:, e:e + 1] = ms[e] + jnp.log(l)
        lane = lax.broadcasted_iota(jnp.int32, (tq, LANES), 1)
        o_ref[...] = jnp.where(lane < HEAD_DIM, outs[0], outs[1])

    dep_specs, dep_ops = _dep_args(dep)
    return pl.pallas_call(
        _after(body, 3, dep), name="attn_fwd",
        out_shape=(jax.ShapeDtypeStruct((t, ATTN_WIDTH), F32), jax.ShapeDtypeStruct((hp_n, t, 2), F32)),
        grid=(t // seq, hp_n, nq),
        in_specs=[pl.BlockSpec((tq, 2 * LANES), lambda b, hp, i: (b * nq + i, hp)),
                  pl.BlockSpec((seq, 2 * LANES), lambda b, hp, i: (b, hp)),
                  pl.BlockSpec((seq, LANES), lambda b, hp, i: (b, hp))] + dep_specs,
        out_specs=(pl.BlockSpec((tq, LANES), lambda b, hp, i: (b * nq + i, hp)),
                   pl.BlockSpec((None, tq, 2), lambda b, hp, i: (hp, b * nq + i, 0))),
        scratch_shapes=[pltpu.VMEM((2, nq, tq, tq), F32)],
        compiler_params=_cparams("parallel", "parallel", "arbitrary"),
    )(qa, ka, v, *dep_ops)


def _merge_fwd(x, ps, o, g2, wpo, wao, wout, tm):
    t, d = x.shape
    tm = min(tm, t)

    def body(x_ref, ps_ref, o_ref, gp_ref, ga_ref, wpo_ref, wao_ref, wout_ref, mg_ref, x1_ref):
        py = _dot(ps_ref[...], wpo_ref[...])
        ay = _dot(o_ref[...].astype(BF16), wao_ref[...])
        mb = (_sigmoid(gp_ref[...].astype(F32)) * py + _sigmoid(ga_ref[...].astype(F32)) * ay).astype(BF16)
        mg_ref[...] = mb
        x1_ref[...] = x_ref[...] + _dot(mb, wout_ref[...])

    row = lambda w: pl.BlockSpec((tm, w), lambda i: (i, 0))
    full = lambda a: pl.BlockSpec(a.shape, lambda i: (0, 0))
    return pl.pallas_call(
        body, name="merge_fwd",
        out_shape=(jax.ShapeDtypeStruct((t, d), BF16), jax.ShapeDtypeStruct((t, d), F32)),
        grid=(t // tm,),
        in_specs=[row(d), row(POOL_WIDTH), row(ATTN_WIDTH), pl.BlockSpec((tm, d), lambda i: (i, 0)),
                  pl.BlockSpec((tm, d), lambda i: (i, 1)), full(wpo), full(wao), full(wout)],
        out_specs=(row(d), row(d)),
        compiler_params=_cparams("parallel"),
    )(x, ps, o, g2, g2, wpo, wao, wout)


def _ffn_fwd(x1, g, wg, wu, wd, tm, tf):
    t, d = x1.shape
    f = wg.shape[0]
    tm = min(tm, t)
    nf = f // tf

    def body(x1_ref, g_ref, wg_ref, wu_ref, wd_ref, h2_ref, gt_ref, up_ref, act_ref, x2_ref):
        j = pl.program_id(1)

        @pl.when(j == 0)
        def _():
            h2_ref[...] = _rms_fwd(x1_ref[...], g_ref[...]).astype(BF16)

        h2 = h2_ref[...]
        gt = _dot_nt(h2, wg_ref[...])
        up = _dot_nt(h2, wu_ref[...])
        act = (gt * _sigmoid(gt) * up).astype(BF16)
        gt_ref[...] = gt.astype(BF16)
        up_ref[...] = up.astype(BF16)
        act_ref[...] = act
        prod = _dot(act, wd_ref[...])

        @pl.when(j == 0)
        def _():
            x2_ref[...] = prod

        @pl.when(j > 0)
        def _():
            x2_ref[...] += prod

        @pl.when(j == nf - 1)
        def _():
            x2_ref[...] += x1_ref[...]

    return pl.pallas_call(
        body, name="ffn_fwd",
        out_shape=(jax.ShapeDtypeStruct((t, d), BF16), jax.ShapeDtypeStruct((t, f), BF16),
                   jax.ShapeDtypeStruct((t, f), BF16), jax.ShapeDtypeStruct((t, f), BF16),
                   jax.ShapeDtypeStruct((t, d), F32)),
        grid=(t // tm, nf),
        in_specs=[pl.BlockSpec((tm, d), lambda i, j: (i, 0)), pl.BlockSpec((1, d), lambda i, j: (0, 0)),
                  pl.BlockSpec((tf, d), lambda i, j: (j, 0)), pl.BlockSpec((tf, d), lambda i, j: (j, 0)),
                  pl.BlockSpec((tf, d), lambda i, j: (j, 0))],
        out_specs=(pl.BlockSpec((tm, d), lambda i, j: (i, 0)), pl.BlockSpec((tm, tf), lambda i, j: (i, j)),
                   pl.BlockSpec((tm, tf), lambda i, j: (i, j)), pl.BlockSpec((tm, tf), lambda i, j: (i, j)),
                   pl.BlockSpec((tm, d), lambda i, j: (i, 0))),
        compiler_params=_cparams("parallel", "arbitrary"),
    )(x1, g, wg, wu, wd)


def _final_fwd_bwd(x2, target, g, tm):
    t, d = x2.shape
    tm = min(tm, t)

    def body(x_ref, t_ref, g_ref, loss_ref, dx_ref, dg_ref):
        i = pl.program_id(0)
        x = x_ref[...]
        gg = g_ref[...]
        err = _rms_fwd(x, gg) - t_ref[...]
        part = 0.5 * jnp.sum(jnp.mean(err * err, axis=-1, keepdims=True), axis=0, keepdims=True)
        dx, dg = _rms_bwd(x, gg, err * (1.0 / d))
        dx_ref[...] = dx

        @pl.when(i == 0)
        def _():
            loss_ref[...] = jnp.zeros_like(loss_ref)
            dg_ref[...] = jnp.zeros_like(dg_ref)

        loss_ref[...] += jnp.broadcast_to(part, loss_ref.shape)
        dg_ref[...] += dg

    return pl.pallas_call(
        body, name="final_fwd_bwd",
        out_shape=(jax.ShapeDtypeStruct((1, LANES), F32), jax.ShapeDtypeStruct((t, d), F32),
                   jax.ShapeDtypeStruct((1, d), F32)),
        grid=(t // tm,),
        in_specs=[pl.BlockSpec((tm, d), lambda i: (i, 0)), pl.BlockSpec((tm, d), lambda i: (i, 0)),
                  pl.BlockSpec((1, d), lambda i: (0, 0))],
        out_specs=(pl.BlockSpec((1, LANES), lambda i: (0, 0)), pl.BlockSpec((tm, d), lambda i: (i, 0)),
                   pl.BlockSpec((1, d), lambda i: (0, 0))),
        compiler_params=_cparams("arbitrary"),
    )(x2, target, g)


def _ffn_bwd(dx2, x1, g, gt, up, wg, wu, wd, tm, tf):
    t, d = dx2.shape
    f = gt.shape[1]
    tm = min(tm, t)
    nf = f // tf

    def body(dx2_ref, x1_ref, g_ref, gt_ref, up_ref, wg_ref, wu_ref, wd_ref, dgt_ref, dup_ref, dx1_ref, dg_ref, acc_ref,
             dxb_ref):
        i, j = pl.program_id(0), pl.program_id(1)

        @pl.when(j == 0)
        def _():
            dxb_ref[...] = dx2_ref[...].astype(BF16)

        dact = _dot_nt(dxb_ref[...], wd_ref[...])
        gtv = gt_ref[...].astype(F32)
        sg = _sigmoid(gtv)
        dup = (dact * (gtv * sg)).astype(BF16)
        dgt = (dact * up_ref[...].astype(F32) * (sg * (1.0 + gtv * (1.0 - sg)))).astype(BF16)
        dgt_ref[...] = dgt
        dup_ref[...] = dup
        contrib = _dot(dgt, wg_ref[...]) + _dot(dup, wu_ref[...])

        @pl.when(j == 0)
        def _():
            acc_ref[...] = contrib

        @pl.when(j > 0)
        def _():
            acc_ref[...] += contrib

        @pl.when(jnp.logical_and(i == 0, j == 0))
        def _():
            dg_ref[...] = jnp.zeros_like(dg_ref)

        @pl.when(j == nf - 1)
        def _():
            dxn, dg = _rms_bwd(x1_ref[...], g_ref[...], acc_ref[...])
            dx1_ref[...] = dx2_ref[...] + dxn
            dg_ref[...] += dg

    return pl.pallas_call(
        body, name="ffn_bwd",
        out_shape=(jax.ShapeDtypeStruct((t, f), BF16), jax.ShapeDtypeStruct((t, f), BF16),
                   jax.ShapeDtypeStruct((t, d), F32), jax.ShapeDtypeStruct((1, d), F32)),
        grid=(t // tm, nf),
        in_specs=[pl.BlockSpec((tm, d), lambda i, j: (i, 0)), pl.BlockSpec((tm, d), lambda i, j: (i, 0)),
                  pl.BlockSpec((1, d), lambda i, j: (0, 0)),
                  pl.BlockSpec((tm, tf), lambda i, j: (i, j)), pl.BlockSpec((tm, tf), lambda i, j: (i, j)),
                  pl.BlockSpec((tf, d), lambda i, j: (j, 0)), pl.BlockSpec((tf, d), lambda i, j: (j, 0)),
                  pl.BlockSpec((tf, d), lambda i, j: (j, 0))],
        out_specs=(pl.BlockSpec((tm, tf), lambda i, j: (i, j)), pl.BlockSpec((tm, tf), lambda i, j: (i, j)),
                   pl.BlockSpec((tm, d), lambda i, j: (i, 0)), pl.BlockSpec((1, d), lambda i, j: (0, 0))),
        scratch_shapes=[pltpu.VMEM((tm, d), F32), pltpu.VMEM((tm, d), BF16)],
        compiler_params=_cparams("arbitrary", "arbitrary"),
    )(dx2, x1, g, gt, up, wg, wu, wd)


def _merge_bwd(dx1, ps, o, g2, wpo, wao, wout, tm, dep=None):
    t, d = dx1.shape
    tm = min(tm, t)

    def body(dx1_ref, ps_ref, o_ref, gp_ref, ga_ref, wpo_ref, wao_ref, wout_ref, dpy_ref, day_ref, dg2_ref, dps_ref, da_ref):
        dm = _dot_nt(dx1_ref[...].astype(BF16), wout_ref[...])
        py = _dot(ps_ref[...], wpo_ref[...])
        ay = _dot(o_ref[...].astype(BF16), wao_ref[...])
        sp = _sigmoid(gp_ref[...].astype(F32))
        sa = _sigmoid(ga_ref[...].astype(F32))
        dpy = (dm * sp).astype(BF16)
        day = (dm * sa).astype(BF16)
        dpy_ref[...] = dpy
        day_ref[...] = day
        dg2_ref[:, :d] = (dm * py * (sp * (1.0 - sp))).astype(BF16)
        dg2_ref[:, d:] = (dm * ay * (sa * (1.0 - sa))).astype(BF16)
        dps_ref[...] = _dot_nt(dpy, wpo_ref[...])
        da_ref[...] = _dot_nt(day, wao_ref[...]).astype(BF16)

    row = lambda w: pl.BlockSpec((tm, w), lambda i: (i, 0))
    full = lambda a: pl.BlockSpec(a.shape, lambda i: (0, 0))
    dep_specs, dep_ops = _dep_args(dep)
    return pl.pallas_call(
        _after(body, 8, dep), name="merge_bwd",
        out_shape=(jax.ShapeDtypeStruct((t, d), BF16), jax.ShapeDtypeStruct((t, d), BF16),
                   jax.ShapeDtypeStruct((t, 2 * d), BF16), jax.ShapeDtypeStruct((t, POOL_WIDTH), F32),
                   jax.ShapeDtypeStruct((t, ATTN_WIDTH), BF16)),
        grid=(t // tm,),
        in_specs=[row(d), row(POOL_WIDTH), row(ATTN_WIDTH), pl.BlockSpec((tm, d), lambda i: (i, 0)),
                  pl.BlockSpec((tm, d), lambda i: (i, 1)), full(wpo), full(wao), full(wout)] + dep_specs,
        out_specs=(row(d), row(d), row(2 * d), row(POOL_WIDTH), row(ATTN_WIDTH)),
        compiler_params=_cparams("parallel"),
    )(dx1, ps, o, g2, g2, wpo, wao, wout, *dep_ops)


def _attn_bwd(qa, ka, v, do, lse4, seq, tq, dep=None):
    t = qa.shape[0]
    nq = seq // tq
    hp_n = N_HEADS // 2
    heads = [slice(e * LANES, (e + 1) * LANES) for e in range(2)]

    def body(q_ref, k_ref, v_ref, do_ref, lse_ref, dq_ref, dk_ref, dv_ref, dfr_ref, dk_acc, dv_acc, p_buf, dp_buf):
        diag_ok = lax.broadcasted_iota(jnp.int32, (tq, tq), 0) >= lax.broadcasted_iota(jnp.int32, (tq, tq), 1)
        lane_q = lax.broadcasted_iota(jnp.int32, (tq, LANES), 1)
        lane_s = lax.broadcasted_iota(jnp.int32, (seq, LANES), 1)
        mine_q = [lane_q < HEAD_DIM, lane_q >= HEAD_DIM]
        dv_acc[...] = jnp.zeros_like(dv_acc)
        dk_acc[...] = jnp.zeros_like(dk_acc)
        dfr_ref[...] = jnp.zeros_like(dfr_ref)

        def q_step(i, _):
            q0 = pl.multiple_of(i * tq, tq)
            qs = [q_ref[pl.ds(q0, tq), hl] for hl in heads]
            dov = do_ref[pl.ds(q0, tq), :]
            dos = [jnp.where(mq, dov, jnp.zeros((), BF16)) for mq in mine_q]
            lss = [lse_ref[pl.ds(q0, tq), e:e + 1] for e in range(2)]

            def sweep1(j, dls):
                r0 = pl.multiple_of(j * tq, tq)
                vv = v_ref[pl.ds(r0, tq), :]
                out = []
                for e, hl in enumerate(heads):
                    s = _dot_nt(qs[e], k_ref[pl.ds(r0, tq), hl])
                    s = jnp.where(jnp.logical_or(diag_ok, j < i), s, NEG_BIG)
                    p = jnp.exp(s - lss[e])
                    dp = _dot_nt(dos[e], vv)
                    p_buf[e, j] = p
                    dp_buf[e, j] = dp
                    dv_acc[pl.ds(r0, tq), :] += _dot_tn(p.astype(BF16), dos[e])
                    out.append(dls[e] + _fold_lanes(p * dp, jnp.add))
                return tuple(out)

            dls = lax.fori_loop(0, i + 1, sweep1, (jnp.zeros((tq, LANES), F32),) * 2)
            dls = [jnp.sum(d, axis=1, keepdims=True) for d in dls]

            def sweep2(j, dqs):
                r0 = pl.multiple_of(j * tq, tq)
                out = []
                for e, hl in enumerate(heads):
                    ds = p_buf[e, j] * (dp_buf[e, j] - dls[e])
                    dfr_ref[e, pl.ds(j, 1), :] += jnp.sum(ds, axis=0, keepdims=True)
                    dsb = ds.astype(BF16)
                    dk_acc[e, pl.ds(r0, tq), :] += _dot_tn(dsb, qs[e])
                    out.append(dqs[e] + _dot(dsb, k_ref[pl.ds(r0, tq), hl]))
                return tuple(out)

            dqs = lax.fori_loop(0, i + 1, sweep2, (jnp.zeros((tq, LANES), F32),) * 2)
            dq = jnp.where(mine_q[0], dqs[0], pltpu.roll(dqs[1], HEAD_DIM, 1)) * ATTN_SCALE
            dq_ref[pl.ds(q0, tq), :] = dq.astype(BF16)
            return 0

        lax.fori_loop(0, nq, q_step, 0)
        dk_ref[...] = jnp.where(lane_s < HEAD_DIM, dk_acc[0], pltpu.roll(dk_acc[1], HEAD_DIM, 1)).astype(BF16)
        dv_ref[...] = dv_acc[...].astype(BF16)

    wide = pl.BlockSpec((seq, 2 * LANES), lambda b, hp: (b, hp))
    col = pl.BlockSpec((seq, LANES), lambda b, hp: (b, hp))
    pair = pl.BlockSpec((None, seq, 2), lambda b, hp: (hp, b, 0))
    dep_specs, dep_ops = _dep_args(dep)
    return pl.pallas_call(
        _after(body, 5, dep), name="attn_bwd",
        out_shape=(jax.ShapeDtypeStruct((t, ATTN_WIDTH), BF16),) * 3 + (jax.ShapeDtypeStruct((N_HEADS, t // tq, tq), F32),),
        grid=(t // seq, hp_n),
        in_specs=[wide, wide, col, col, pair] + dep_specs,
        out_specs=(col, col, col, pl.BlockSpec((2, nq, tq), lambda b, hp: (hp, b, 0))),
        scratch_shapes=[pltpu.VMEM((2, seq, LANES), F32), pltpu.VMEM((seq, LANES), F32),
                        pltpu.VMEM((2, nq, tq, tq), F32), pltpu.VMEM((2, nq, tq, tq), F32)],
        compiler_params=_cparams("parallel", "arbitrary"),
    )(qa, ka, v, do, lse4, *dep_ops)


def _forget_bwd(dfc, fl, bf, seq):
    t = fl.shape[0]
    cb = min(256, seq)
    nb = seq // cb

    def body(dfc_ref, fl_ref, bf_ref, dfl_ref, db_ref):
        b = pl.program_id(0)
        ri = lax.broadcasted_iota(jnp.int32, (cb, cb), 0)
        ci = lax.broadcasted_iota(jnp.int32, (cb, cb), 1)
        tri = (ci >= ri).astype(BF16)
        carry = jnp.zeros((1, LANES), F32)
        dbs = jnp.zeros((1, LANES), F32)
        for blk in reversed(range(nb)):
            rs = slice(blk * cb, (blk + 1) * cb)
            dlf = _tri_dot(tri, -dfc_ref[rs, :]) + carry
            carry = dlf[0:1, :]
            dfl = dlf * _sigmoid(-(fl_ref[rs, :] + bf_ref[...]))
            dfl_ref[rs, :] = dfl.astype(BF16)
            dbs = dbs + jnp.sum(dfl, axis=0, keepdims=True)

        @pl.when(b == 0)
        def _():
            db_ref[...] = jnp.zeros_like(db_ref)

        db_ref[...] += dbs

    return pl.pallas_call(
        body, name="forget_bwd",
        out_shape=(jax.ShapeDtypeStruct((t, LANES), BF16), jax.ShapeDtypeStruct((1, LANES), F32)),
        grid=(t // seq,),
        in_specs=[pl.BlockSpec((seq, LANES), lambda b: (b, 0)), pl.BlockSpec((seq, LANES), lambda b: (b, 0)),
                  pl.BlockSpec((1, LANES), lambda b: (0, 0))],
        out_specs=(pl.BlockSpec((seq, LANES), lambda b: (b, 0)), pl.BlockSpec((1, LANES), lambda b: (0, 0))),
        compiler_params=_cparams("arbitrary"),
    )(dfc, fl, bf)


def _pool_bwd(dps, p, mix, scale, seq):
    t = dps.shape[0]

    def body(dps_ref, p_ref, mix_ref, sc_ref, du_ref, dmix_ref, dsc_ref):
        b = pl.program_id(0)

        @pl.when(b == 0)
        def _():
            dmix_ref[...] = jnp.zeros_like(dmix_ref)
            dsc_ref[...] = jnp.zeros_like(dsc_ref)

        tpos = lax.broadcasted_iota(jnp.int32, (seq, POOL_GROUP_DIM), 0)
        for g in range(POOL_GROUPS):
            sl = slice(g * POOL_GROUP_DIM, (g + 1) * POOL_GROUP_DIM)
            pb = p_ref[:, sl]
            dpsg = dps_ref[:, sl]
            pm = _dot(pb, mix_ref[g])
            dsc_ref[:, sl] += jnp.sum(dpsg * pm, axis=0, keepdims=True)
            dpm = (dpsg * sc_ref[:, sl]).astype(BF16)
            dmix_ref[g] += _dot_tn(pb, dpm)
            dp = _dot_nt(dpm, mix_ref[g])
            cnt = jnp.minimum(tpos + 1, POOL_WINDOWS[g]).astype(F32)
            s = dp / cnt
            for lvl in range(g + 1):
                d = 2 ** lvl
                s = s + jnp.where(tpos < seq - d, pltpu.roll(s, seq - d, 0), 0.0)
            du_ref[:, sl] = (s - dp).astype(BF16)

    return pl.pallas_call(
        body, name="pool_bwd",
        out_shape=(jax.ShapeDtypeStruct((t, POOL_WIDTH), BF16),
                   jax.ShapeDtypeStruct((POOL_GROUPS, POOL_GROUP_DIM, POOL_GROUP_DIM), F32),
                   jax.ShapeDtypeStruct((1, POOL_WIDTH), F32)),
        grid=(t // seq,),
        in_specs=[pl.BlockSpec((seq, POOL_WIDTH), lambda b: (b, 0)), pl.BlockSpec((seq, POOL_WIDTH), lambda b: (b, 0)),
                  pl.BlockSpec((POOL_GROUPS, POOL_GROUP_DIM, POOL_GROUP_DIM), lambda b: (0, 0, 0)),
                  pl.BlockSpec((1, POOL_WIDTH), lambda b: (0, 0))],
        out_specs=(pl.BlockSpec((seq, POOL_WIDTH), lambda b: (b, 0)),
                   pl.BlockSpec((POOL_GROUPS, POOL_GROUP_DIM, POOL_GROUP_DIM), lambda b: (0, 0, 0)),
                   pl.BlockSpec((1, POOL_WIDTH), lambda b: (0, 0))),
        compiler_params=_cparams("arbitrary"),
    )(dps, p, mix, scale)


def _in_bwd(du, dq, dk, dv, dg2, dfl, dx1, x, g, wu, wqkv, wg2, wft, tm):
    t, d = x.shape
    tm = min(tm, t)
    aw = ATTN_WIDTH

    def body(du_ref, dq_ref, dk_ref, dv_ref, dg2_ref, dfl_ref, dx1_ref, x_ref, g_ref, wu_ref, wqkv_ref, wg2_ref, wft_ref,
             dx_ref, dg_ref):
        i = pl.program_id(0)
        dh = _dot_nt(du_ref[...], wu_ref[...])
        dh += _dot_nt(dq_ref[...], wqkv_ref[:, 0:aw])
        dh += _dot_nt(dk_ref[...], wqkv_ref[:, aw:2 * aw])
        dh += _dot_nt(dv_ref[...], wqkv_ref[:, 2 * aw:3 * aw])
        dh += _dot_nt(dg2_ref[...], wg2_ref[...])
        dh += _dot(dfl_ref[...], wft_ref[...])
        dxn, dg = _rms_bwd(x_ref[...], g_ref[...], dh)
        dx_ref[...] = dx1_ref[...] + dxn

        @pl.when(i == 0)
        def _():
            dg_ref[...] = jnp.zeros_like(dg_ref)

        dg_ref[...] += dg

    row = lambda w: pl.BlockSpec((tm, w), lambda i: (i, 0))
    full = lambda a: pl.BlockSpec(a.shape, lambda i: (0, 0))
    return pl.pallas_call(
        body, name="in_bwd",
        out_shape=(jax.ShapeDtypeStruct((t, d), F32), jax.ShapeDtypeStruct((1, d), F32)),
        grid=(t // tm,),
        in_specs=[row(POOL_WIDTH), row(aw), row(aw), row(aw), row(2 * d), row(LANES), row(d), row(d),
                  pl.BlockSpec((1, d), lambda i: (0, 0)), full(wu), full(wqkv), full(wg2), full(wft)],
        out_specs=(row(d), pl.BlockSpec((1, d), lambda i: (0, 0))),
        compiler_params=_cparams("arbitrary"),
    )(du, dq, dk, dv, dg2, dfl, dx1, x, g, wu, wqkv, wg2, wft)


def _position():
    return lax.axis_index("x"), lax.axis_index("y"), lax.axis_index("c")


def _remote(src, dst, send_sem, recv_sem, device):
    return pltpu.make_async_remote_copy(src_ref=src, dst_ref=dst, send_sem=send_sem, recv_sem=recv_sem,
                                        device_id=device, device_id_type=MESH)


HBM = pl.BlockSpec(memory_space=pltpu.HBM)
SEM = pl.BlockSpec(memory_space=pltpu.SEMAPHORE)
DATAFLOW = pltpu.SideEffectType.DATAFLOW_SIDE_EFFECTING


def _copies_start(name, arrays, plan, m, dep=None):
    n = len(arrays)
    arrays = [pltpu.with_memory_space_constraint(a, pltpu.HBM) for a in arrays]

    def body(*refs):
        ins, send_sem, recv_sem, token = refs[:n], refs[n], refs[n + 1], refs[2 * n + 2]
        for i, (src, dst, device, _) in enumerate(plan(ins, *_position())):
            _remote(src, dst, send_sem.at[i], recv_sem.at[i], device).start()
        token[...] = jnp.zeros_like(token)

    dep_specs, dep_ops = _dep_args(dep)
    outs = pl.pallas_call(
        _after(body, n, dep), name=name,
        out_shape=(pltpu.SemaphoreType.DMA((m,)), pltpu.SemaphoreType.DMA((m,)),
                   *[pltpu.HBM(a.shape, a.dtype) for a in arrays], jax.ShapeDtypeStruct((8, LANES), F32)),
        in_specs=[HBM] * n + dep_specs, out_specs=(SEM, SEM, *[HBM] * n, pl.BlockSpec(memory_space=pltpu.VMEM)),
        input_output_aliases={i: i + 2 for i in range(n)},
        compiler_params=pltpu.CompilerParams(has_side_effects=DATAFLOW),
    )(*arrays, *dep_ops)
    return (outs[0], outs[1]), list(outs[2:2 + n]), outs[2 + n]


def _copies_wait(name, sems, arrays, plan, after):
    n = len(arrays)
    afters = list(after) if isinstance(after, (list, tuple)) else [after]

    def body(*refs):
        ins, send_sem, recv_sem = refs[:n], refs[n], refs[n + 1]
        for i, (src, dst, device, landing) in enumerate(plan(ins, *_position())):
            _remote(src, dst, send_sem.at[i], recv_sem.at[i], device).wait_send()
            _remote(landing, landing, send_sem.at[i], recv_sem.at[i], device).wait_recv()

    outs = pl.pallas_call(
        body, name=name,
        out_shape=tuple(pltpu.HBM(a.shape, a.dtype) for a in arrays),
        in_specs=[HBM] * n + [SEM, SEM] + [ANY] * len(afters), out_specs=tuple([HBM] * n),
        input_output_aliases={i: i for i in range(n)},
        compiler_params=pltpu.CompilerParams(has_side_effects=DATAFLOW),
    )(*arrays, sems[0], sems[1], *afters)
    return list(outs)


def _tie(x, dep):
    for token in _dep_list(dep):
        x = x + token[0, 0]
    return x


def _other_chips(x, y):
    return [(1 - x, y), (x, 1 - y), (1 - x, 1 - y)]


def _gather_begin(tag, shards, token):
    n = len(shards)
    lands = [lax.empty((N_CHIPS,) + s.shape, s.dtype) for s in shards]

    def plan(refs, x, y, c):
        return [(refs[k].at[c], refs[n + k].at[2 * x + y, c], (ox, oy, c), refs[n + k].at[2 * ox + oy, c])
                for k in range(n) for ox, oy in _other_chips(x, y)]

    sems, thru, token = _copies_start(f"gather_{tag}_ici_start", list(shards) + lands, plan, 3 * n, dep=token)
    return dict(tag=tag, n=n, plan=plan, sems=sems, arrays=thru, token=token)


def _gather_forward(st, after):
    n, tag = st["n"], st["tag"]
    thru = _copies_wait(f"gather_{tag}_ici_wait", st["sems"], st["arrays"], st["plan"], after)

    def plan(refs, x, y, c):
        return [(refs[k].at[2 * ox + oy, c], refs[k].at[2 * ox + oy, c], (x, y, 1 - c), refs[k].at[2 * ox + oy, 1 - c])
                for k in range(n) for ox, oy in _other_chips(x, y)]

    sems, lands, token = _copies_start(f"gather_{tag}_fwd_start", thru[n:], plan, 3 * n)
    return dict(tag=tag, n=n, plan=plan, sems=sems, arrays=lands, token=token, shards=thru[:n])


def _gather_end(st, after):
    lands = _copies_wait(f"gather_{st['tag']}_fwd_wait", st["sems"], st["arrays"], st["plan"], after)
    me = 2 * lax.axis_index("x") + lax.axis_index("y")
    return [lax.dynamic_update_index_in_dim(g, s, me, 0) for g, s in zip(lands, st["shards"])]


def _add_keep_give(name, pos, a, a_keep, a_give, b, b_keep, b_give, steps):
    r, c = a.shape[-2:]

    def spec(arr, fn):
        lead = arr.ndim - 2
        return pl.BlockSpec((None,) * lead + (r, c), lambda i, p: tuple(fn(i, p)) + (0, 0))

    out_spec = pl.BlockSpec((None, r, c), lambda i, p: (i, 0, 0))

    def body(p_ref, ak_ref, bk_ref, ag_ref, bg_ref, keep_ref, give_ref):
        keep_ref[...] = ak_ref[...] + bk_ref[...].astype(F32)
        give_ref[...] = (ag_ref[...] + bg_ref[...].astype(F32)).astype(BF16)

    return pl.pallas_call(
        body, name=name,
        out_shape=(jax.ShapeDtypeStruct((steps, r, c), F32), jax.ShapeDtypeStruct((steps, r, c), BF16)),
        grid_spec=pltpu.PrefetchScalarGridSpec(
            num_scalar_prefetch=1, grid=(steps,),
            in_specs=[spec(a, a_keep), spec(b, b_keep), spec(a, a_give), spec(b, b_give)],
            out_specs=(out_spec, out_spec)),
        compiler_params=_cparams("parallel"),
    )(pos, a, b, a, b)


def _add_last(name, a, b):
    _, r, c = a.shape
    blk = pl.BlockSpec((None, r, c), lambda i: (0, 0, 0))

    def body(a_ref, b_ref, o_ref):
        o_ref[...] = a_ref[...] + b_ref[...].astype(F32)

    return pl.pallas_call(
        body, name=name, out_shape=jax.ShapeDtypeStruct((r, c), F32), grid=(1,), in_specs=[blk, blk],
        out_specs=pl.BlockSpec((r, c), lambda i: (0, 0)), compiler_params=_cparams("arbitrary"),
    )(a, b)


def _exchange_begin(tag, stage, gives, lands, peer_fn, extra):
    n = len(gives)

    def plan(refs, x, y, c):
        return [(refs[k], refs[n + k], peer_fn(x, y, c), refs[n + k]) for k in range(n)]

    sems, thru, token = _copies_start(f"rs{tag}_{stage}_start", gives + lands, plan, n)
    return dict(extra, tag=tag, n=n, stage=stage, plan=plan, sems=sems, arrays=thru, token=token)


def _reduce_begin(tag, grads):
    n = len(grads)
    lands = [lax.empty((N_CHIPS,) + g.shape[2:], F32) for g in grads]

    def plan(refs, x, y, c):
        return [(refs[k].at[j, 1 - c], refs[n + k].at[j], (x, y, 1 - c), refs[n + k].at[j])
                for k in range(n) for j in range(N_CHIPS)]

    sems, thru, token = _copies_start(f"rs{tag}_c_start", list(grads) + lands, plan, N_CHIPS * n)
    return dict(tag=tag, n=n, stage="c", plan=plan, sems=sems, arrays=thru, token=token)


def _reduce_advance(st, after):
    tag, n, stage = st["tag"], st["n"], st["stage"]
    thru = _copies_wait(f"rs{tag}_{stage}_wait", st["sems"], st["arrays"], st["plan"], after)
    first, recv = thru[:n], thru[n:]
    x, y, c = _position()
    if stage == "c":
        pos = jnp.stack([c, x]).astype(jnp.int32)
        sums = [_add_keep_give(
            f"rs{tag}_c_add{k}", pos,
            first[k], lambda i, p: (2 * p[1] + i, p[0]), lambda i, p: (2 * (1 - p[1]) + i, p[0]),
            recv[k], lambda i, p: (2 * p[1] + i,), lambda i, p: (2 * (1 - p[1]) + i,), 2) for k in range(n)]
        lands = [lax.empty(s[1].shape, BF16) for s in sums]
        return _exchange_begin(tag, "x", [s[1] for s in sums], lands, lambda x, y, c: (1 - x, y, c),
                               dict(keep=[s[0] for s in sums]))
    if stage == "x":
        pos = jnp.stack([y]).astype(jnp.int32)
        sums = [_add_keep_give(
            f"rs{tag}_x_add{k}", pos,
            st["keep"][k], lambda i, p: (p[0],), lambda i, p: (1 - p[0],),
            recv[k], lambda i, p: (p[0],), lambda i, p: (1 - p[0],), 1) for k in range(n)]
        lands = [lax.empty(s[1].shape, BF16) for s in sums]
        return _exchange_begin(tag, "y", [s[1] for s in sums], lands, lambda x, y, c: (x, 1 - y, c),
                               dict(keep=[s[0] for s in sums]))
    if stage == "y":
        mine = [_add_last(f"rs{tag}_y_add{k}", st["keep"][k], recv[k]) for k in range(n)]
        lands = [lax.empty(m.shape, F32) for m in mine]
        return _exchange_begin(tag, "swap", mine, lands, lambda x, y, c: (x, y, 1 - c), {})
    return dict(done=list(zip(first, recv)), token=None)


def _all_reduce_small(v):
    r = v.shape[0]

    def body(v_ref, out_ref, buf, send_sems, recv_sems, local_sem):
        x, y, c = _position()
        me, sibling = (x, y, c), (x, y, 1 - c)
        chips = [(1 - x, y), (x, 1 - y), (1 - x, 1 - y)]

        def rows(px, py, pc):
            return buf.at[pl.ds((4 * px + 2 * py + pc) * r, r), :]

        def copy(k, block, to, src=None):
            return _remote(rows(*block) if src is None else src, rows(*block), send_sems.at[k], recv_sems.at[k], to)

        mine = pltpu.make_async_copy(v_ref, rows(*me), local_sem)
        mine.start()
        first = [copy(0, me, sibling, src=v_ref)]
        first += [copy(1 + j, me, (*chip, c), src=v_ref) for j, chip in enumerate(chips)]
        for cp in first:
            cp.start()
        passed = [copy(4 + j, (*chip, c), sibling) for j, chip in enumerate(chips)]
        for j, chip in enumerate(chips):
            copy(1 + j, (*chip, c), me).wait_recv()
            passed[j].start()
        copy(0, sibling, me).wait_recv()
        for j, chip in enumerate(chips):
            copy(4 + j, (*chip, 1 - c), me).wait_recv()
        for cp in first + passed:
            cp.wait_send()
        mine.wait()
        acc = buf[0:r, :]
        for dev in range(1, N_DEV):
            acc = acc + buf[dev * r:(dev + 1) * r, :]
        out_ref[...] = acc

    return pl.pallas_call(
        body, name="all_reduce_small",
        out_shape=jax.ShapeDtypeStruct(v.shape, F32),
        in_specs=[pl.BlockSpec(memory_space=pltpu.VMEM)],
        out_specs=pl.BlockSpec(memory_space=pltpu.VMEM),
        scratch_shapes=[pltpu.VMEM((N_DEV * r, LANES), F32), pltpu.SemaphoreType.DMA((7,)),
                        pltpu.SemaphoreType.DMA((7,)), pltpu.SemaphoreType.DMA],
        compiler_params=pltpu.CompilerParams(has_side_effects=True, vmem_limit_bytes=VMEM_LIMIT_V7X),
    )(v)


def _adamw_update(w, gg, m, v):
    mn = ADAM_B1 * m + (1.0 - ADAM_B1) * gg
    vn = ADAM_B2 * v + (1.0 - ADAM_B2) * (gg * gg)
    m_hat = mn / (1.0 - ADAM_B1 ** ADAM_STEP)
    v_hat = vn / (1.0 - ADAM_B2 ** ADAM_STEP)
    return -ADAM_LR * (m_hat / (jnp.sqrt(v_hat) + ADAM_EPS) + ADAM_WD * w), mn, vn


def _adamw(name, w, g, m, v):
    def body(w_ref, g_ref, m_ref, v_ref, d_ref, mo_ref, vo_ref):
        d_ref[...], mo_ref[...], vo_ref[...] = _adamw_update(w_ref[...], g_ref[...], m_ref[...], v_ref[...])

    blk = pl.BlockSpec(w.shape, lambda i: (0, 0))
    return pl.pallas_call(
        body, name=name, out_shape=(jax.ShapeDtypeStruct(w.shape, F32),) * 3, grid=(1,),
        in_specs=[blk] * 4, out_specs=(blk,) * 3, compiler_params=_cparams("arbitrary"),
    )(w, g, m, v)


def _adamw_halves(name, pos_c, w, g_mine, g_other, m, v, tr, dep=None):
    r, c = w.shape
    rh = r // 2
    tr = tr if rh % tr == 0 else rh
    nt = rh // tr

    def body(p_ref, w_ref, gm_ref, go_ref, m_ref, v_ref, g_ref, d_ref, mo_ref, vo_ref):
        gg = jnp.where(pl.program_id(0) == p_ref[0], gm_ref[...], go_ref[...])
        g_ref[...] = gg
        d_ref[...], mo_ref[...], vo_ref[...] = _adamw_update(w_ref[...], gg, m_ref[...], v_ref[...])

    full = pl.BlockSpec((tr, c), lambda h, i, p: (h * nt + i, 0))
    half = pl.BlockSpec((tr, c), lambda h, i, p: (i, 0))
    dep_specs, dep_ops = _dep_args(dep)
    return pl.pallas_call(
        _after(body, 6, dep), name=name, out_shape=(jax.ShapeDtypeStruct((r, c), F32),) * 4,
        grid_spec=pltpu.PrefetchScalarGridSpec(
            num_scalar_prefetch=1, grid=(2, nt),
            in_specs=[full, half, half, full, full] + dep_specs, out_specs=(full,) * 4),
        compiler_params=_cparams("parallel", "parallel"),
    )(pos_c, w, g_mine, g_other, m, v, *dep_ops)


def _col_sharded_to_comm(g):
    k, n = g.shape
    return g.reshape(2, k // 2, N_CHIPS, n // N_CHIPS).transpose(2, 0, 1, 3)


def _row_sharded_to_comm(g):
    r, c = g.shape
    return g.reshape(N_CHIPS, 2, r // (2 * N_CHIPS), c)


def _col_sharded_full(g):
    _, _, rh, c = g.shape
    return g.reshape(N_CHIPS, 2 * rh, c).transpose(1, 0, 2).reshape(2 * rh, N_CHIPS * c)


def _row_sharded_full(g):
    _, _, rh, c = g.shape
    return g.reshape(N_CHIPS * 2 * rh, c)


def _pack_small(g1, bfv, mix, scale, g2n, gf, extra=None):
    row8 = jnp.pad(bfv.reshape(1, N_HEADS), ((0, 0), (0, LANES - N_HEADS)))
    if extra is not None:
        row8 = row8 + jnp.pad(extra[:, :1], ((0, 0), (N_HEADS, LANES - N_HEADS - 1)))
    return jnp.concatenate([
        g1.reshape(8, LANES), jnp.pad(row8, ((0, 7), (0, 0))), mix.reshape(512, LANES),
        jnp.pad(scale.reshape(4, LANES), ((0, 4), (0, 0))), g2n.reshape(8, LANES), gf.reshape(8, LANES)], axis=0)


def _unpack_small(s, like):
    g1, bfv, mix, scale, g2n, gf = like
    return (s[0:8].reshape(g1.shape), s[8, :N_HEADS].reshape(bfv.shape), s[16:528].reshape(mix.shape),
            s[528:532].reshape(scale.shape), s[536:544].reshape(g2n.shape), s[544:552].reshape(gf.shape))


class _MeshLinks:
    def __init__(self, shards_in, shards_rest):
        self.gin = _gather_begin("in", shards_in, None)
        self.grest = _gather_begin("rest", shards_rest, self.gin["token"])
        self.tokens = {"gather": self.grest["token"]}
        self.groups = {}

    @property
    def token(self):
        return list(self.tokens.values())

    def tie(self, x):
        return _tie(x, self.token)

    def weights_in(self, after):
        st = _gather_forward(self.gin, after)
        (g,) = _gather_end(st, st["token"])
        return _col_sharded_full(g)

    def rest_forward(self, after):
        self.grest = _gather_forward(self.grest, after)
        self.tokens["gather"] = self.grest["token"]

    def weights_rest(self, after):
        g = _gather_end(self.grest, after)
        del self.tokens["gather"]
        return [_col_sharded_full(g[0]), _col_sharded_full(g[1])] + [_row_sharded_full(a) for a in g[2:]]

    def reduce_begin(self, tag, grads):
        self.groups[tag] = _reduce_begin(tag, grads)
        self.tokens[tag] = self.groups[tag]["token"]

    def advance(self, after):
        for tag, st in self.groups.items():
            if "done" not in st:
                self.groups[tag] = _reduce_advance(st, after)
                if self.groups[tag]["token"] is None:
                    del self.tokens[tag]
                else:
                    self.tokens[tag] = self.groups[tag]["token"]

    def reduced(self, tag):
        return self.groups[tag]["done"]


class _NoLinks:
    token = None

    def __init__(self, w_in, rest):
        self.w_in, self.rest, self.grads = w_in, rest, {}

    def tie(self, x):
        return x

    def weights_in(self, after):
        return self.w_in

    def rest_forward(self, after):
        pass

    def weights_rest(self, after):
        return self.rest

    def reduce_begin(self, tag, grads):
        self.grads[tag] = grads

    def advance(self, after):
        pass


def _local_step(links, x, target, seq, norm1_g, b_forget, pool_mix, pool_scale, norm2_g, norm_f_g):
    t, d = x.shape
    tq = min(256, seq)
    aw = ATTN_WIDTH
    o_q, o_f, o_g = POOL_WIDTH, POOL_WIDTH + 3 * aw, POOL_WIDTH + 3 * aw + N_HEADS
    bf = jnp.pad(b_forget, ((0, 0), (0, LANES - N_HEADS)))
    mixb = pool_mix.astype(BF16)

    h = _norm_fwd("norm1_fwd", x, links.tie(norm1_g), 512)
    w_in = links.weights_in(h)
    wu = w_in[:, :o_q]
    wqkv = w_in[:, o_q:o_f]
    wf = jnp.pad(w_in[:, o_f:o_g], ((0, 0), (0, LANES - N_HEADS)))
    wg2 = w_in[:, o_g:]
    wft = wf.T
    u = _matmul("mm_u", h, wu, "nn", F32, 1024, 512, d)
    g2 = _matmul("mm_gates", h, wg2, "nn", BF16, 1024, 512, d)
    fl, fcum = _forget_fwd(h, wf, bf, seq)
    qa, ka, v = _attn_prep(h, _head_blocks(wqkv[:, :aw]), _head_blocks(wqkv[:, aw:2 * aw]), wqkv[:, 2 * aw:], fcum, 512)
    p, ps = _pool_fwd(u, mixb, pool_scale, seq)
    links.rest_forward([ps, qa, g2])
    o, lse = _attn_fwd(qa, ka, v, seq, tq, dep=links.token)
    w_pool_out, w_attn_out, w_out, w_ffn_gate, w_ffn_up, w_ffn_down = links.weights_rest(o)
    merged, x1 = _merge_fwd(x, ps, o, g2, w_pool_out, w_attn_out, w_out, 256)
    h2, gt, up, act, x2 = _ffn_fwd(x1, norm2_g, w_ffn_gate, w_ffn_up, w_ffn_down, 1024, 256)
    loss, dx2, d_gf = _final_fwd_bwd(x2, target, norm_f_g, 512)

    dgt, dup, dx1, d_g2n = _ffn_bwd(dx2, x1, norm2_g, gt, up, w_ffn_gate, w_ffn_up, w_ffn_down, 1024, 256)
    d_wd = _matmul("dw_down", act, dx2, "tn", F32, 1408, 1024, 512)
    d_wg = _matmul("dw_gate", dgt, h2, "tn", F32, 1408, 1024, 512)
    d_wu = _matmul("dw_up", dup, h2, "tn", F32, 1408, 1024, 512)
    links.reduce_begin("a", [_row_sharded_to_comm(g) for g in (d_wg, d_wu, d_wd)])
    dpy, day, dg2, dps, da = _merge_bwd(dx1, ps, o, g2, w_pool_out, w_attn_out, w_out, 256, dep=links.token)
    links.advance(dps)
    d_wout = _matmul("dw_out", merged, dx1, "tn", F32, 1024, 1024, 512)
    d_wpo = _matmul("dw_pool_out", ps, dpy, "tn", F32, 512, 1024, 512)
    d_wao = _matmul("dw_attn_out", o, day, "tn", F32, 512, 1024, 512)
    links.reduce_begin("m", [_col_sharded_to_comm(d_wpo), _col_sharded_to_comm(d_wao), _row_sharded_to_comm(d_wout)])
    dq, dk, dv, dfr = _attn_bwd(qa, ka, v, da, lse, seq, tq, dep=links.token)
    links.advance(dq)
    dfc = jnp.pad(dfr.reshape(N_HEADS, t).T, ((0, 0), (0, LANES - N_HEADS)))
    dfl, d_bf = _forget_bwd(dfc, fl, bf, seq)
    du, d_mix, d_scale = _pool_bwd(dps, p, mixb, links.tie(pool_scale), seq)
    d_wu_in = _matmul("dw_in_u", h, du, "tn", F32, 1024, 512, 512)
    d_wq = _matmul("dw_in_q", h, dq, "tn", F32, 1024, 512, 512)
    d_wk = _matmul("dw_in_k", h, dk, "tn", F32, 1024, 512, 512)
    d_wv = _matmul("dw_in_v", h, dv, "tn", F32, 1024, 512, 512)
    links.advance([d_wu_in, d_wq, d_wk, d_wv])
    d_wf = _matmul("dw_in_f", h, dfl, "tn", F32, 1024, LANES, 512)
    d_wg2 = _matmul("dw_in_gates", h, dg2, "tn", F32, 1024, 1024, 512, dep=links.token)
    d_win = jnp.concatenate([d_wu_in, d_wq, d_wk, d_wv, d_wf[:, :N_HEADS], d_wg2], axis=1)
    comm_b = [_col_sharded_to_comm(d_win)]
    links.advance(comm_b)
    links.reduce_begin("b", comm_b)
    dx, d_g1 = _in_bwd(du, dq, dk, dv, dg2, dfl, dx1, x, links.tie(norm1_g), wu, wqkv, wg2, wft, 256)
    links.advance(dx)
    small = (d_g1, d_bf[:, :N_HEADS], d_mix, d_scale, d_g2n, d_gf)
    return loss, dx, small


def kernel(x, norm1_g, w_in, b_forget, pool_mix, pool_scale, w_pool_out, w_attn_out, w_out, norm2_g, w_ffn_gate, w_ffn_up, w_ffn_down, norm_f_g, loss_target, m_norm1_g, m_w_in, m_b_forget, m_pool_mix, m_pool_scale, m_w_pool_out, m_w_attn_out, m_w_out, m_norm2_g, m_w_ffn_gate, m_w_ffn_up, m_w_ffn_down, m_norm_f_g, v_norm1_g, v_w_in, v_b_forget, v_pool_mix, v_pool_scale, v_w_pool_out, v_w_attn_out, v_w_out, v_norm2_g, v_w_ffn_gate, v_w_ffn_up, v_w_ffn_down, v_norm_f_g):
    nb, seq, d = x.shape
    group_a = ((w_ffn_gate, m_w_ffn_gate, v_w_ffn_gate, True, 9), (w_ffn_up, m_w_ffn_up, v_w_ffn_up, True, 10),
               (w_ffn_down, m_w_ffn_down, v_w_ffn_down, False, 11))
    group_m = ((w_pool_out, m_w_pool_out, v_w_pool_out, False, 5), (w_attn_out, m_w_attn_out, v_w_attn_out, False, 6),
               (w_out, m_w_out, v_w_out, False, 7))
    group_b = ((w_in, m_w_in, v_w_in, False, 1),)
    small_w = (norm1_g, b_forget, pool_mix, pool_scale, norm2_g, norm_f_g)
    small_m = (m_norm1_g, m_b_forget, m_pool_mix, m_pool_scale, m_norm2_g, m_norm_f_g)
    small_v = (v_norm1_g, v_b_forget, v_pool_mix, v_pool_scale, v_norm2_g, v_norm_f_g)
    small_pos = (0, 2, 3, 4, 8, 12)
    view = lambda a, tr: a[0].T if tr else a[0]
    unview = lambda a, tr, like: (a.T if tr else a).reshape(like.shape)

    def shard(w, tr):
        lw = view(w, tr).astype(BF16)
        return lw.reshape(2, lw.shape[0] // 2, lw.shape[1])

    links = _MeshLinks([shard(w_in, False)],
                       [shard(w_pool_out, False), shard(w_attn_out, False), shard(w_out, False),
                        shard(w_ffn_gate, True), shard(w_ffn_up, True), shard(w_ffn_down, False)])
    loss, dx, small_g = _local_step(
        links, x.reshape(nb * seq, d), loss_target.reshape(nb * seq, d), seq,
        norm1_g, b_forget, pool_mix[0], pool_scale, norm2_g, norm_f_g.reshape(1, d))

    grads, deltas, new_m, new_v = [None] * 13, [None] * 13, [None] * 13, [None] * 13
    pos_c = jnp.stack([lax.axis_index("c")]).astype(jnp.int32)

    def update(tag, group, dep):
        last = []
        for k, ((w, m, v, tr, pos), (mine, other)) in enumerate(zip(group, links.reduced(tag))):
            outs = _adamw_halves(f"adamw_{tag}{k}", pos_c, view(w, tr), mine, other, view(m, tr), view(v, tr), 256,
                                 dep=dep)
            grads[pos], deltas[pos], new_m[pos], new_v[pos] = (unview(a, tr, w) for a in outs)
            last.append(outs[1])
        return last

    last = update("a", group_a, links.token) + update("m", group_m, links.token)
    links.advance(last)
    small_sum = _all_reduce_small(links.tie(_pack_small(*small_g, extra=loss)))
    loss_out = small_sum[8, N_HEADS]
    dl, mn, vn = _adamw("adamw_small", _pack_small(*small_w), small_sum * _small_mask(), _pack_small(*small_m),
                        _pack_small(*small_v))
    for pos, g, a, b, e in zip(small_pos, _unpack_small(small_sum, small_w), _unpack_small(dl, small_w),
                               _unpack_small(mn, small_w), _unpack_small(vn, small_w)):
        grads[pos], deltas[pos], new_m[pos], new_v[pos] = g, a, b, e
    links.advance(dl)
    links.advance(links.token)
    update("b", group_b, None)

    return (loss_out, dx.reshape(nb, seq, d), *grads, *deltas, *new_m, *new_v)


def _small_mask():
    rows = lax.broadcasted_iota(jnp.int32, (552, LANES), 0)
    lanes = lax.broadcasted_iota(jnp.int32, (552, LANES), 1)
    return jnp.where(jnp.logical_and(rows == 8, lanes == N_HEADS), 0.0, 1.0).astype(F32)
```

```python
import functools

import jax
import jax.numpy as jnp
from jax import lax
from jax.experimental import pallas as pl
from jax.experimental.pallas import tpu as pltpu

F32 = jnp.float32
BF16 = jnp.bfloat16

D_MODEL = 1024
POOL_WINDOWS = (2, 4, 8, 16)
POOL_GROUPS = 4
POOL_GROUP_DIM = 128
POOL_WIDTH = 512
HEAD_DIM = 64
N_HEADS = 8
ATTN_WIDTH = 512
D_FF = 2816
RMS_EPS = 1e-6
ATTN_SCALE = HEAD_DIM ** -0.5
NEG_BIG = -1e30

ADAM_LR = 0.001
ADAM_B1 = 0.9
ADAM_B2 = 0.999
ADAM_EPS = 1e-08
ADAM_WD = 0.01
ADAM_STEP = 10

LANES = 128
N_CHIPS = 4
N_DEV = 8
VMEM_LIMIT_V7X = 52 * 1024 * 1024
MESH = pl.DeviceIdType.MESH
ANY = pl.BlockSpec(memory_space=pl.ANY)


def _cparams(*sem):
    return pltpu.CompilerParams(dimension_semantics=sem if sem else None, vmem_limit_bytes=VMEM_LIMIT_V7X)


def _dep_list(dep):
    return [] if dep is None else (list(dep) if isinstance(dep, (list, tuple)) else [dep])


def _after(body, n_in, dep):
    k = len(_dep_list(dep))
    if k == 0:
        return body

    def wrapped(*refs):
        body(*refs[:n_in], *refs[n_in + k:])

    return wrapped


def _dep_args(dep):
    deps = _dep_list(dep)
    return [ANY] * len(deps), deps


def _dot(a, b):
    return lax.dot_general(a, b, (((1,), (0,)), ((), ())), preferred_element_type=F32)


def _dot_nt(a, b):
    return lax.dot_general(a, b, (((1,), (1,)), ((), ())), preferred_element_type=F32)


def _dot_tn(a, b):
    return lax.dot_general(a, b, (((0,), (0,)), ((), ())), preferred_element_type=F32)


def _sigmoid(x):
    return jax.nn.sigmoid(x)


def _rms_fwd(x, g):
    r = lax.rsqrt(jnp.mean(x * x, axis=-1, keepdims=True) + RMS_EPS)
    return (x * r) * g


def _rms_bwd(x, g, dy):
    r = lax.rsqrt(jnp.mean(x * x, axis=-1, keepdims=True) + RMS_EPS)
    xh = x * r
    dg = jnp.sum(dy * xh, axis=0, keepdims=True)
    dxh = dy * g
    dx = r * (dxh - xh * jnp.mean(dxh * xh, axis=-1, keepdims=True))
    return dx, dg


def _matmul(name, a, b, mode, out_dtype, tm, tn, tk, dep=None):
    if mode == "nn":
        (m, k), (_, n) = a.shape, b.shape
    elif mode == "nt":
        (m, k), (n, _) = a.shape, b.shape
    else:
        (k, m), (_, n) = a.shape, b.shape
    tm, tn, tk = min(tm, m), min(tn, n), min(tk, k)
    assert m % tm == 0 and n % tn == 0 and k % tk == 0, (name, m, n, k, tm, tn, tk)
    nk = k // tk
    if mode == "tn":
        a_spec = pl.BlockSpec((tk, tm), lambda i, j, kk: (kk, i))
    else:
        a_spec = pl.BlockSpec((tm, tk), lambda i, j, kk: (i, kk))
    if mode == "nt":
        b_spec = pl.BlockSpec((tn, tk), lambda i, j, kk: (j, kk))
    else:
        b_spec = pl.BlockSpec((tk, tn), lambda i, j, kk: (kk, j))
    dot = {"nn": _dot, "nt": _dot_nt, "tn": _dot_tn}[mode]
    use_scratch = nk > 1 and out_dtype != F32

    def body(a_ref, b_ref, o_ref, *scratch):
        prod = dot(a_ref[...].astype(BF16), b_ref[...].astype(BF16))
        if nk == 1:
            o_ref[...] = prod.astype(out_dtype)
            return
        acc = scratch[0] if use_scratch else o_ref
        kk = pl.program_id(2)

        @pl.when(kk == 0)
        def _():
            acc[...] = prod

        @pl.when(kk > 0)
        def _():
            acc[...] += prod

        if use_scratch:
            @pl.when(kk == nk - 1)
            def _():
                o_ref[...] = acc[...].astype(out_dtype)

    dep_specs, dep_ops = _dep_args(dep)
    return pl.pallas_call(
        _after(body, 2, dep),
        name=name,
        out_shape=jax.ShapeDtypeStruct((m, n), out_dtype),
        grid=(m // tm, n // tn, nk),
        in_specs=[a_spec, b_spec] + dep_specs,
        out_specs=pl.BlockSpec((tm, tn), lambda i, j, kk: (i, j)),
        scratch_shapes=[pltpu.VMEM((tm, tn), F32)] if use_scratch else [],
        compiler_params=_cparams("parallel", "parallel", "arbitrary"),
    )(a, b, *dep_ops)


def _norm_fwd(name, x, g, tm):
    t, d = x.shape
    tm = min(tm, t)

    def body(x_ref, g_ref, h_ref):
        h_ref[...] = _rms_fwd(x_ref[...], g_ref[...]).astype(BF16)

    return pl.pallas_call(
        body, name=name, out_shape=jax.ShapeDtypeStruct((t, d), BF16), grid=(t // tm,),
        in_specs=[pl.BlockSpec((tm, d), lambda i: (i, 0)), pl.BlockSpec((1, d), lambda i: (0, 0))],
        out_specs=pl.BlockSpec((tm, d), lambda i: (i, 0)),
        compiler_params=_cparams("parallel"),
    )(x, g)


def _split3(x):
    hi = x.astype(BF16)
    r1 = x - hi.astype(F32)
    mid = r1.astype(BF16)
    lo = (r1 - mid.astype(F32)).astype(BF16)
    return hi, mid, lo


def _tri_dot(tri, x):
    hi, mid, lo = _split3(x)
    return _dot(tri, hi) + _dot(tri, mid) + _dot(tri, lo)


def _forget_fwd(h, wf, bf, seq):
    t, d = h.shape
    cb = min(256, seq)

    def body(h_ref, wf_ref, bf_ref, fl_ref, fc_ref):
        fl = _dot(h_ref[...], wf_ref[...])
        fl_ref[...] = fl
        xx = fl + bf_ref[...]
        lf = jnp.minimum(xx, 0.0) - jnp.log(1.0 + jnp.exp(-jnp.abs(xx)))
        ri = lax.broadcasted_iota(jnp.int32, (cb, cb), 0)
        ci = lax.broadcasted_iota(jnp.int32, (cb, cb), 1)
        tri = (ri >= ci).astype(BF16)
        carry = jnp.zeros((1, LANES), F32)
        for blk in range(seq // cb):
            cs = _tri_dot(tri, lf[blk * cb:(blk + 1) * cb]) + carry
            fc_ref[blk * cb:(blk + 1) * cb, :] = cs
            carry = cs[cb - 1:cb, :]

    return pl.pallas_call(
        body, name="forget_fwd",
        out_shape=(jax.ShapeDtypeStruct((t, LANES), F32), jax.ShapeDtypeStruct((t, LANES), F32)),
        grid=(t // seq,),
        in_specs=[pl.BlockSpec((seq, d), lambda b: (b, 0)), pl.BlockSpec((d, LANES), lambda b: (0, 0)),
                  pl.BlockSpec((1, LANES), lambda b: (0, 0))],
        out_specs=(pl.BlockSpec((seq, LANES), lambda b: (b, 0)), pl.BlockSpec((seq, LANES), lambda b: (b, 0))),
        compiler_params=_cparams("parallel"),
    )(h, wf, bf)


def _pool_fwd(u, mix, scale, seq):
    t = u.shape[0]

    def body(u_ref, mix_ref, sc_ref, p_ref, ps_ref):
        tpos = lax.broadcasted_iota(jnp.int32, (seq, POOL_GROUP_DIM), 0)
        for g in range(POOL_GROUPS):
            sl = slice(g * POOL_GROUP_DIM, (g + 1) * POOL_GROUP_DIM)
            ug = u_ref[:, sl]
            s = ug
            for lvl in range(g + 1):
                d = 2 ** lvl
                s = s + jnp.where(tpos >= d, pltpu.roll(s, d, 0), 0.0)
            cnt = jnp.minimum(tpos + 1, POOL_WINDOWS[g]).astype(F32)
            pb = (s / cnt - ug).astype(BF16)
            p_ref[:, sl] = pb
            ps_ref[:, sl] = (_dot(pb, mix_ref[g]) * sc_ref[:, sl]).astype(BF16)

    return pl.pallas_call(
        body, name="pool_fwd",
        out_shape=(jax.ShapeDtypeStruct((t, POOL_WIDTH), BF16), jax.ShapeDtypeStruct((t, POOL_WIDTH), BF16)),
        grid=(t // seq,),
        in_specs=[pl.BlockSpec((seq, POOL_WIDTH), lambda b: (b, 0)),
                  pl.BlockSpec((POOL_GROUPS, POOL_GROUP_DIM, POOL_GROUP_DIM), lambda b: (0, 0, 0)),
                  pl.BlockSpec((1, POOL_WIDTH), lambda b: (0, 0))],
        out_specs=(pl.BlockSpec((seq, POOL_WIDTH), lambda b: (b, 0)), pl.BlockSpec((seq, POOL_WIDTH), lambda b: (b, 0))),
        compiler_params=_cparams("parallel"),
    )(u, mix, scale)


def _aug_constants():
    w = N_HEADS * LANES
    rows = jnp.arange(3 * LANES)
    piece, head = rows // LANES, rows % LANES
    cols = jnp.arange(w)
    live = (head < N_HEADS)[:, None]
    pq = (live & (cols[None, :] == (head * LANES + HEAD_DIM + piece)[:, None])).astype(BF16)
    pk = -(live & (cols[None, :] == (head * LANES + HEAD_DIM + 3 + piece)[:, None])).astype(BF16)
    lane = cols % LANES
    oq = ((lane >= HEAD_DIM + 3) & (lane < HEAD_DIM + 6)).astype(F32)[None, :]
    ok = ((lane >= HEAD_DIM) & (lane < HEAD_DIM + 3)).astype(F32)[None, :]
    return pq, pk, oq, ok


def _head_blocks(w):
    d = w.shape[0]
    return jnp.pad(w.reshape(d, N_HEADS, HEAD_DIM), ((0, 0), (0, 0), (0, LANES - HEAD_DIM))).reshape(d, N_HEADS * LANES)


def _attn_prep(h, wq, wk, wv, fcum, tm):
    t, d = h.shape
    tm = min(tm, t)
    w = N_HEADS * LANES
    pq, pk, oq, ok = _aug_constants()

    def body(h_ref, wq_ref, wk_ref, wv_ref, f_ref, pq_ref, pk_ref, oq_ref, ok_ref, qa_ref, ka_ref, v_ref):
        hh = h_ref[...]
        fs = jnp.concatenate(_split3(f_ref[...]), axis=1)
        q = _dot(hh, wq_ref[...]).astype(BF16).astype(F32) * ATTN_SCALE
        qa_ref[...] = (q + _dot(fs, pq_ref[...]) + oq_ref[...]).astype(BF16)
        k = _dot(hh, wk_ref[...]).astype(BF16).astype(F32)
        ka_ref[...] = (k + _dot(fs, pk_ref[...]) + ok_ref[...]).astype(BF16)
        v_ref[...] = _dot(hh, wv_ref[...]).astype(BF16)

    row = lambda n: pl.BlockSpec((tm, n), lambda i: (i, 0))
    full = lambda a: pl.BlockSpec(a.shape, lambda i: (0, 0))
    return pl.pallas_call(
        body, name="attn_prep",
        out_shape=(jax.ShapeDtypeStruct((t, w), BF16), jax.ShapeDtypeStruct((t, w), BF16),
                   jax.ShapeDtypeStruct((t, ATTN_WIDTH), BF16)),
        grid=(t // tm,),
        in_specs=[row(d), full(wq), full(wk), full(wv), row(LANES), full(pq), full(pk), full(oq), full(ok)],
        out_specs=(row(w), row(w), row(ATTN_WIDTH)),
        compiler_params=_cparams("parallel"),
    )(h, wq, wk, wv, fcum, pq, pk, oq, ok)


def _fold_lanes(x, op):
    out = x[:, :LANES]
    for g in range(1, x.shape[1] // LANES):
        out = op(out, x[:, g * LANES:(g + 1) * LANES])
    return out


def _attn_fwd(qa, ka, v, seq, tq, dep=None):
    t = qa.shape[0]
    nq = seq // tq
    hp_n = N_HEADS // 2
    heads = [slice(e * LANES, (e + 1) * LANES) for e in range(2)]

    def body(q_ref, k_ref, v_ref, o_ref, lse_ref, s_buf):
        i = pl.program_id(2)
        diag_ok = lax.broadcasted_iota(jnp.int32, (tq, tq), 0) >= lax.broadcasted_iota(jnp.int32, (tq, tq), 1)
        qs = [q_ref[:, hl] for hl in heads]

        def sweep1(j, mxs):
            r0 = pl.multiple_of(j * tq, tq)
            out = []
            for e, hl in enumerate(heads):
                s = _dot_nt(qs[e], k_ref[pl.ds(r0, tq), hl])
                s = jnp.where(jnp.logical_or(diag_ok, j < i), s, NEG_BIG)
                s_buf[e, j] = s
                out.append(jnp.maximum(mxs[e], _fold_lanes(s, jnp.maximum)))
            return tuple(out)

        mxs = lax.fori_loop(0, i + 1, sweep1, (jnp.full((tq, LANES), NEG_BIG, F32),) * 2)
        ms = [jnp.max(mx, axis=1, keepdims=True) for mx in mxs]

        def sweep2(j, carry):
            r0 = pl.multiple_of(j * tq, tq)
            vv = v_ref[pl.ds(r0, tq), :]
            out = []
            for e in range(2):
                p = jnp.exp(s_buf[e, j] - ms[e])
                out += [carry[2 * e] + _fold_lanes(p, jnp.add), carry[2 * e + 1] + _dot(p.astype(BF16), vv)]
            return tuple(out)

        res = lax.fori_loop(0, i + 1, sweep2, (jnp.zeros((tq, LANES), F32),) * 4)
        outs = []
        for e in range(2):
            l = jnp.sum(res[2 * e], axis=1, keepdims=True)
            outs.append(res[2 * e + 1] / l)
            lse_ref[:, e:e + 1] = ms[e] + jnp.log(l)
        lane = lax.broadcasted_iota(jnp.int32, (tq, LANES), 1)
        o_ref[...] = jnp.where(lane < HEAD_DIM, outs[0], outs[1])

    dep_specs, dep_ops = _dep_args(dep)
    return pl.pallas_call(
        _after(body, 3, dep), name="attn_fwd",
        out_shape=(jax.ShapeDtypeStruct((t, ATTN_WIDTH), F32), jax.ShapeDtypeStruct((hp_n, t, 2), F32)),
        grid=(t // seq, hp_n, nq),
        in_specs=[pl.BlockSpec((tq, 2 * LANES), lambda b, hp, i: (b * nq + i, hp)),
                  pl.BlockSpec((seq, 2 * LANES), lambda b, hp, i: (b, hp)),
                  pl.BlockSpec((seq, LANES), lambda b, hp, i: (b, hp))] + dep_specs,
        out_specs=(pl.BlockSpec((tq, LANES), lambda b, hp, i: (b * nq + i, hp)),
                   pl.BlockSpec((None, tq, 2), lambda b, hp, i: (hp, b * nq + i, 0))),
        scratch_shapes=[pltpu.VMEM((2, nq, tq, tq), F32)],
        compiler_params=_cparams("parallel", "parallel", "arbitrary"),
    )(qa, ka, v, *dep_ops)


def _merge_fwd(x, ps, o, g2, wpo, wao, wout, tm):
    t, d = x.shape
    tm = min(tm, t)

    def body(x_ref, ps_ref, o_ref, gp_ref, ga_ref, wpo_ref, wao_ref, wout_ref, mg_ref, x1_ref):
        py = _dot(ps_ref[...], wpo_ref[...])
        ay = _dot(o_ref[...].astype(BF16), wao_ref[...])
        mb = (_sigmoid(gp_ref[...].astype(F32)) * py + _sigmoid(ga_ref[...].astype(F32)) * ay).astype(BF16)
        mg_ref[...] = mb
        x1_ref[...] = x_ref[...] + _dot(mb, wout_ref[...])

    row = lambda w: pl.BlockSpec((tm, w), lambda i: (i, 0))
    full = lambda a: pl.BlockSpec(a.shape, lambda i: (0, 0))
    return pl.pallas_call(
        body, name="merge_fwd",
        out_shape=(jax.ShapeDtypeStruct((t, d), BF16), jax.ShapeDtypeStruct((t, d), F32)),
        grid=(t // tm,),
        in_specs=[row(d), row(POOL_WIDTH), row(ATTN_WIDTH), pl.BlockSpec((tm, d), lambda i: (i, 0)),
                  pl.BlockSpec((tm, d), lambda i: (i, 1)), full(wpo), full(wao), full(wout)],
        out_specs=(row(d), row(d)),
        compiler_params=_cparams("parallel"),
    )(x, ps, o, g2, g2, wpo, wao, wout)


def _ffn_fwd(x1, g, wg, wu, wd, tm, tf):
    t, d = x1.shape
    f = wg.shape[0]
    tm = min(tm, t)
    nf = f // tf

    def body(x1_ref, g_ref, wg_ref, wu_ref, wd_ref, h2_ref, gt_ref, up_ref, act_ref, x2_ref):
        j = pl.program_id(1)

        @pl.when(j == 0)
        def _():
            h2_ref[...] = _rms_fwd(x1_ref[...], g_ref[...]).astype(BF16)

        h2 = h2_ref[...]
        gt = _dot_nt(h2, wg_ref[...])
        up = _dot_nt(h2, wu_ref[...])
        sg = _sigmoid(gt)
        silu = gt * sg
        act = (silu * up).astype(BF16)
        gt_ref[...] = (up * (sg * (1.0 + gt * (1.0 - sg)))).astype(BF16)
        up_ref[...] = silu.astype(BF16)
        act_ref[...] = act
        prod = _dot(act, wd_ref[...])

        @pl.when(j == 0)
        def _():
            x2_ref[...] = prod

        @pl.when(j > 0)
        def _():
            x2_ref[...] += prod

        @pl.when(j == nf - 1)
        def _():
            x2_ref[...] += x1_ref[...]

    return pl.pallas_call(
        body, name="ffn_fwd",
        out_shape=(jax.ShapeDtypeStruct((t, d), BF16), jax.ShapeDtypeStruct((t, f), BF16),
                   jax.ShapeDtypeStruct((t, f), BF16), jax.ShapeDtypeStruct((t, f), BF16),
                   jax.ShapeDtypeStruct((t, d), F32)),
        grid=(t // tm, nf),
        in_specs=[pl.BlockSpec((tm, d), lambda i, j: (i, 0)), pl.BlockSpec((1, d), lambda i, j: (0, 0)),
                  pl.BlockSpec((tf, d), lambda i, j: (j, 0)), pl.BlockSpec((tf, d), lambda i, j: (j, 0)),
                  pl.BlockSpec((tf, d), lambda i, j: (j, 0))],
        out_specs=(pl.BlockSpec((tm, d), lambda i, j: (i, 0)), pl.BlockSpec((tm, tf), lambda i, j: (i, j)),
                   pl.BlockSpec((tm, tf), lambda i, j: (i, j)), pl.BlockSpec((tm, tf), lambda i, j: (i, j)),
                   pl.BlockSpec((tm, d), lambda i, j: (i, 0))),
        compiler_params=_cparams("parallel", "arbitrary"),
    )(x1, g, wg, wu, wd)


def _final_fwd_bwd(x2, target, g, tm):
    t, d = x2.shape
    tm = min(tm, t)

    def body(x_ref, t_ref, g_ref, loss_ref, dx_ref, dg_ref):
        i = pl.program_id(0)
        x = x_ref[...]
        gg = g_ref[...]
        err = _rms_fwd(x, gg) - t_ref[...]
        part = 0.5 * jnp.sum(jnp.mean(err * err, axis=-1, keepdims=True), axis=0, keepdims=True)
        dx, dg = _rms_bwd(x, gg, err * (1.0 / d))
        dx_ref[...] = dx

        @pl.when(i == 0)
        def _():
            loss_ref[...] = jnp.zeros_like(loss_ref)
            dg_ref[...] = jnp.zeros_like(dg_ref)

        loss_ref[...] += jnp.broadcast_to(part, loss_ref.shape)
        dg_ref[...] += dg

    return pl.pallas_call(
        body, name="final_fwd_bwd",
        out_shape=(jax.ShapeDtypeStruct((1, LANES), F32), jax.ShapeDtypeStruct((t, d), F32),
                   jax.ShapeDtypeStruct((1, d), F32)),
        grid=(t // tm,),
        in_specs=[pl.BlockSpec((tm, d), lambda i: (i, 0)), pl.BlockSpec((tm, d), lambda i: (i, 0)),
                  pl.BlockSpec((1, d), lambda i: (0, 0))],
        out_specs=(pl.BlockSpec((1, LANES), lambda i: (0, 0)), pl.BlockSpec((tm, d), lambda i: (i, 0)),
                   pl.BlockSpec((1, d), lambda i: (0, 0))),
        compiler_params=_cparams("arbitrary"),
    )(x2, target, g)


def _ffn_bwd(dx2, x1, g, gt, up, wg, wu, wd, tm, tf):
    t, d = dx2.shape
    f = gt.shape[1]
    tm = min(tm, t)
    nf = f // tf

    def body(dx2_ref, x1_ref, g_ref, gt_ref, up_ref, wg_ref, wu_ref, wd_ref, dgt_ref, dup_ref, dx1_ref, dg_ref, acc_ref,
             dxb_ref):
        i, j = pl.program_id(0), pl.program_id(1)

        @pl.when(j == 0)
        def _():
            dxb_ref[...] = dx2_ref[...].astype(BF16)

        dact = _dot_nt(dxb_ref[...], wd_ref[...])
        dgt = (dact * gt_ref[...].astype(F32)).astype(BF16)
        dup = (dact * up_ref[...].astype(F32)).astype(BF16)
        dgt_ref[...] = dgt
        dup_ref[...] = dup
        contrib = _dot(dgt, wg_ref[...]) + _dot(dup, wu_ref[...])

        @pl.when(j == 0)
        def _():
            acc_ref[...] = contrib

        @pl.when(j > 0)
        def _():
            acc_ref[...] += contrib

        @pl.when(jnp.logical_and(i == 0, j == 0))
        def _():
            dg_ref[...] = jnp.zeros_like(dg_ref)

        @pl.when(j == nf - 1)
        def _():
            dxn, dg = _rms_bwd(x1_ref[...], g_ref[...], acc_ref[...])
            dx1_ref[...] = dx2_ref[...] + dxn
            dg_ref[...] += dg

    return pl.pallas_call(
        body, name="ffn_bwd",
        out_shape=(jax.ShapeDtypeStruct((t, f), BF16), jax.ShapeDtypeStruct((t, f), BF16),
                   jax.ShapeDtypeStruct((t, d), F32), jax.ShapeDtypeStruct((1, d), F32)),
        grid=(t // tm, nf),
        in_specs=[pl.BlockSpec((tm, d), lambda i, j: (i, 0)), pl.BlockSpec((tm, d), lambda i, j: (i, 0)),
                  pl.BlockSpec((1, d), lambda i, j: (0, 0)),
                  pl.BlockSpec((tm, tf), lambda i, j: (i, j)), pl.BlockSpec((tm, tf), lambda i, j: (i, j)),
                  pl.BlockSpec((tf, d), lambda i, j: (j, 0)), pl.BlockSpec((tf, d), lambda i, j: (j, 0)),
                  pl.BlockSpec((tf, d), lambda i, j: (j, 0))],
        out_specs=(pl.BlockSpec((tm, tf), lambda i, j: (i, j)), pl.BlockSpec((tm, tf), lambda i, j: (i, j)),
                   pl.BlockSpec((tm, d), lambda i, j: (i, 0)), pl.BlockSpec((1, d), lambda i, j: (0, 0))),
        scratch_shapes=[pltpu.VMEM((tm, d), F32), pltpu.VMEM((tm, d), BF16)],
        compiler_params=_cparams("arbitrary", "arbitrary"),
    )(dx2, x1, g, gt, up, wg, wu, wd)


def _merge_bwd(dx1, ps, o, g2, wpo, wao, wout, tm, dep=None):
    t, d = dx1.shape
    tm = min(tm, t)

    def body(dx1_ref, ps_ref, o_ref, gp_ref, ga_ref, wpo_ref, wao_ref, wout_ref, dpy_ref, day_ref, dg2_ref, dps_ref, da_ref):
        dm = _dot_nt(dx1_ref[...].astype(BF16), wout_ref[...])
        py = _dot(ps_ref[...], wpo_ref[...])
        ay = _dot(o_ref[...].astype(BF16), wao_ref[...])
        sp = _sigmoid(gp_ref[...].astype(F32))
        sa = _sigmoid(ga_ref[...].astype(F32))
        dpy = (dm * sp).astype(BF16)
        day = (dm * sa).astype(BF16)
        dpy_ref[...] = dpy
        day_ref[...] = day
        dg2_ref[:, :d] = (dm * py * (sp * (1.0 - sp))).astype(BF16)
        dg2_ref[:, d:] = (dm * ay * (sa * (1.0 - sa))).astype(BF16)
        dps_ref[...] = _dot_nt(dpy, wpo_ref[...])
        da_ref[...] = _dot_nt(day, wao_ref[...]).astype(BF16)

    row = lambda w: pl.BlockSpec((tm, w), lambda i: (i, 0))
    full = lambda a: pl.BlockSpec(a.shape, lambda i: (0, 0))
    dep_specs, dep_ops = _dep_args(dep)
    return pl.pallas_call(
        _after(body, 8, dep), name="merge_bwd",
        out_shape=(jax.ShapeDtypeStruct((t, d), BF16), jax.ShapeDtypeStruct((t, d), BF16),
                   jax.ShapeDtypeStruct((t, 2 * d), BF16), jax.ShapeDtypeStruct((t, POOL_WIDTH), F32),
                   jax.ShapeDtypeStruct((t, ATTN_WIDTH), BF16)),
        grid=(t // tm,),
        in_specs=[row(d), row(POOL_WIDTH), row(ATTN_WIDTH), pl.BlockSpec((tm, d), lambda i: (i, 0)),
                  pl.BlockSpec((tm, d), lambda i: (i, 1)), full(wpo), full(wao), full(wout)] + dep_specs,
        out_specs=(row(d), row(d), row(2 * d), row(POOL_WIDTH), row(ATTN_WIDTH)),
        compiler_params=_cparams("parallel"),
    )(dx1, ps, o, g2, g2, wpo, wao, wout, *dep_ops)


def _attn_bwd(qa, ka, v, do, lse4, seq, tq, dep=None):
    t = qa.shape[0]
    nq = seq // tq
    hp_n = N_HEADS // 2
    heads = [slice(e * LANES, (e + 1) * LANES) for e in range(2)]

    def body(q_ref, k_ref, v_ref, do_ref, lse_ref, dq_ref, dk_ref, dv_ref, dfr_ref, dk_acc, dv_acc, p_buf, dp_buf):
        diag_ok = lax.broadcasted_iota(jnp.int32, (tq, tq), 0) >= lax.broadcasted_iota(jnp.int32, (tq, tq), 1)
        lane_q = lax.broadcasted_iota(jnp.int32, (tq, LANES), 1)
        lane_s = lax.broadcasted_iota(jnp.int32, (seq, LANES), 1)
        mine_q = [lane_q < HEAD_DIM, lane_q >= HEAD_DIM]
        dv_acc[...] = jnp.zeros_like(dv_acc)
        dk_acc[...] = jnp.zeros_like(dk_acc)
        dfr_ref[...] = jnp.zeros_like(dfr_ref)

        def q_step(i, _):
            q0 = pl.multiple_of(i * tq, tq)
            qs = [q_ref[pl.ds(q0, tq), hl] for hl in heads]
            dov = do_ref[pl.ds(q0, tq), :]
            dos = [jnp.where(mq, dov, jnp.zeros((), BF16)) for mq in mine_q]
            lss = [lse_ref[pl.ds(q0, tq), e:e + 1] for e in range(2)]

            def sweep1(j, dls):
                r0 = pl.multiple_of(j * tq, tq)
                vv = v_ref[pl.ds(r0, tq), :]
                out = []
                for e, hl in enumerate(heads):
                    s = _dot_nt(qs[e], k_ref[pl.ds(r0, tq), hl])
                    s = jnp.where(jnp.logical_or(diag_ok, j < i), s, NEG_BIG)
                    p = jnp.exp(s - lss[e])
                    dp = _dot_nt(dos[e], vv)
                    p_buf[e, j] = p
                    dp_buf[e, j] = dp
                    dv_acc[pl.ds(r0, tq), :] += _dot_tn(p.astype(BF16), dos[e])
                    out.append(dls[e] + _fold_lanes(p * dp, jnp.add))
                return tuple(out)

            dls = lax.fori_loop(0, i + 1, sweep1, (jnp.zeros((tq, LANES), F32),) * 2)
            dls = [jnp.sum(d, axis=1, keepdims=True) for d in dls]

            def sweep2(j, dqs):
                r0 = pl.multiple_of(j * tq, tq)
                out = []
                for e, hl in enumerate(heads):
                    ds = p_buf[e, j] * (dp_buf[e, j] - dls[e])
                    dfr_ref[e, pl.ds(j, 1), :] += jnp.sum(ds, axis=0, keepdims=True)
                    dsb = ds.astype(BF16)
                    dk_acc[e, pl.ds(r0, tq), :] += _dot_tn(dsb, qs[e])
                    out.append(dqs[e] + _dot(dsb, k_ref[pl.ds(r0, tq), hl]))
                return tuple(out)

            dqs = lax.fori_loop(0, i + 1, sweep2, (jnp.zeros((tq, LANES), F32),) * 2)
            dq = jnp.where(mine_q[0], dqs[0], pltpu.roll(dqs[1], HEAD_DIM, 1)) * ATTN_SCALE
            dq_ref[pl.ds(q0, tq), :] = dq.astype(BF16)
            return 0

        lax.fori_loop(0, nq, q_step, 0)
        dk_ref[...] = jnp.where(lane_s < HEAD_DIM, dk_acc[0], pltpu.roll(dk_acc[1], HEAD_DIM, 1)).astype(BF16)
        dv_ref[...] = dv_acc[...].astype(BF16)

    wide = pl.BlockSpec((seq, 2 * LANES), lambda b, hp: (b, hp))
    col = pl.BlockSpec((seq, LANES), lambda b, hp: (b, hp))
    pair = pl.BlockSpec((None, seq, 2), lambda b, hp: (hp, b, 0))
    dep_specs, dep_ops = _dep_args(dep)
    return pl.pallas_call(
        _after(body, 5, dep), name="attn_bwd",
        out_shape=(jax.ShapeDtypeStruct((t, ATTN_WIDTH), BF16),) * 3 + (jax.ShapeDtypeStruct((N_HEADS, t // tq, tq), F32),),
        grid=(t // seq, hp_n),
        in_specs=[wide, wide, col, col, pair] + dep_specs,
        out_specs=(col, col, col, pl.BlockSpec((2, nq, tq), lambda b, hp: (hp, b, 0))),
        scratch_shapes=[pltpu.VMEM((2, seq, LANES), F32), pltpu.VMEM((seq, LANES), F32),
                        pltpu.VMEM((2, nq, tq, tq), F32), pltpu.VMEM((2, nq, tq, tq), F32)],
        compiler_params=_cparams("parallel", "arbitrary"),
    )(qa, ka, v, do, lse4, *dep_ops)


def _forget_bwd(dfc, fl, bf, seq):
    t = fl.shape[0]
    cb = min(256, seq)
    nb = seq // cb

    def body(dfc_ref, fl_ref, bf_ref, dfl_ref, db_ref):
        b = pl.program_id(0)
        ri = lax.broadcasted_iota(jnp.int32, (cb, cb), 0)
        ci = lax.broadcasted_iota(jnp.int32, (cb, cb), 1)
        tri = (ci >= ri).astype(BF16)
        carry = jnp.zeros((1, LANES), F32)
        dbs = jnp.zeros((1, LANES), F32)
        for blk in reversed(range(nb)):
            rs = slice(blk * cb, (blk + 1) * cb)
            dlf = _tri_dot(tri, -dfc_ref[rs, :]) + carry
            carry = dlf[0:1, :]
            dfl = dlf * _sigmoid(-(fl_ref[rs, :] + bf_ref[...]))
            dfl_ref[rs, :] = dfl.astype(BF16)
            dbs = dbs + jnp.sum(dfl, axis=0, keepdims=True)

        @pl.when(b == 0)
        def _():
            db_ref[...] = jnp.zeros_like(db_ref)

        db_ref[...] += dbs

    return pl.pallas_call(
        body, name="forget_bwd",
        out_shape=(jax.ShapeDtypeStruct((t, LANES), BF16), jax.ShapeDtypeStruct((1, LANES), F32)),
        grid=(t // seq,),
        in_specs=[pl.BlockSpec((seq, LANES), lambda b: (b, 0)), pl.BlockSpec((seq, LANES), lambda b: (b, 0)),
                  pl.BlockSpec((1, LANES), lambda b: (0, 0))],
        out_specs=(pl.BlockSpec((seq, LANES), lambda b: (b, 0)), pl.BlockSpec((1, LANES), lambda b: (0, 0))),
        compiler_params=_cparams("arbitrary"),
    )(dfc, fl, bf)


def _pool_bwd(dps, p, mix, scale, seq):
    t = dps.shape[0]

    def body(dps_ref, p_ref, mix_ref, sc_ref, du_ref, dmix_ref, dsc_ref):
        b = pl.program_id(0)

        @pl.when(b == 0)
        def _():
            dmix_ref[...] = jnp.zeros_like(dmix_ref)
            dsc_ref[...] = jnp.zeros_like(dsc_ref)

        tpos = lax.broadcasted_iota(jnp.int32, (seq, POOL_GROUP_DIM), 0)
        for g in range(POOL_GROUPS):
            sl = slice(g * POOL_GROUP_DIM, (g + 1) * POOL_GROUP_DIM)
            pb = p_ref[:, sl]
            dpsg = dps_ref[:, sl]
            pm = _dot(pb, mix_ref[g])
            dsc_ref[:, sl] += jnp.sum(dpsg * pm, axis=0, keepdims=True)
            dpm = (dpsg * sc_ref[:, sl]).astype(BF16)
            dmix_ref[g] += _dot_tn(pb, dpm)
            dp = _dot_nt(dpm, mix_ref[g])
            cnt = jnp.minimum(tpos + 1, POOL_WINDOWS[g]).astype(F32)
            s = dp / cnt
            for lvl in range(g + 1):
                d = 2 ** lvl
                s = s + jnp.where(tpos < seq - d, pltpu.roll(s, seq - d, 0), 0.0)
            du_ref[:, sl] = (s - dp).astype(BF16)

    return pl.pallas_call(
        body, name="pool_bwd",
        out_shape=(jax.ShapeDtypeStruct((t, POOL_WIDTH), BF16),
                   jax.ShapeDtypeStruct((POOL_GROUPS, POOL_GROUP_DIM, POOL_GROUP_DIM), F32),
                   jax.ShapeDtypeStruct((1, POOL_WIDTH), F32)),
        grid=(t // seq,),
        in_specs=[pl.BlockSpec((seq, POOL_WIDTH), lambda b: (b, 0)), pl.BlockSpec((seq, POOL_WIDTH), lambda b: (b, 0)),
                  pl.BlockSpec((POOL_GROUPS, POOL_GROUP_DIM, POOL_GROUP_DIM), lambda b: (0, 0, 0)),
                  pl.BlockSpec((1, POOL_WIDTH), lambda b: (0, 0))],
        out_specs=(pl.BlockSpec((seq, POOL_WIDTH), lambda b: (b, 0)),
                   pl.BlockSpec((POOL_GROUPS, POOL_GROUP_DIM, POOL_GROUP_DIM), lambda b: (0, 0, 0)),
                   pl.BlockSpec((1, POOL_WIDTH), lambda b: (0, 0))),
        compiler_params=_cparams("arbitrary"),
    )(dps, p, mix, scale)


def _in_bwd(du, dq, dk, dv, dg2, dfl, dx1, x, g, wu, wqkv, wg2, wft, tm):
    t, d = x.shape
    tm = min(tm, t)
    aw = ATTN_WIDTH

    def body(du_ref, dq_ref, dk_ref, dv_ref, dg2_ref, dfl_ref, dx1_ref, x_ref, g_ref, wu_ref, wqkv_ref, wg2_ref, wft_ref,
             dx_ref, dg_ref):
        i = pl.program_id(0)
        dh = _dot_nt(du_ref[...], wu_ref[...])
        dh += _dot_nt(dq_ref[...], wqkv_ref[:, 0:aw])
        dh += _dot_nt(dk_ref[...], wqkv_ref[:, aw:2 * aw])
        dh += _dot_nt(dv_ref[...], wqkv_ref[:, 2 * aw:3 * aw])
        dh += _dot_nt(dg2_ref[...], wg2_ref[...])
        dh += _dot(dfl_ref[...], wft_ref[...])
        dxn, dg = _rms_bwd(x_ref[...], g_ref[...], dh)
        dx_ref[...] = dx1_ref[...] + dxn

        @pl.when(i == 0)
        def _():
            dg_ref[...] = jnp.zeros_like(dg_ref)

        dg_ref[...] += dg

    row = lambda w: pl.BlockSpec((tm, w), lambda i: (i, 0))
    full = lambda a: pl.BlockSpec(a.shape, lambda i: (0, 0))
    return pl.pallas_call(
        body, name="in_bwd",
        out_shape=(jax.ShapeDtypeStruct((t, d), F32), jax.ShapeDtypeStruct((1, d), F32)),
        grid=(t // tm,),
        in_specs=[row(POOL_WIDTH), row(aw), row(aw), row(aw), row(2 * d), row(LANES), row(d), row(d),
                  pl.BlockSpec((1, d), lambda i: (0, 0)), full(wu), full(wqkv), full(wg2), full(wft)],
        out_specs=(row(d), pl.BlockSpec((1, d), lambda i: (0, 0))),
        compiler_params=_cparams("arbitrary"),
    )(du, dq, dk, dv, dg2, dfl, dx1, x, g, wu, wqkv, wg2, wft)


def _position():
    return lax.axis_index("x"), lax.axis_index("y"), lax.axis_index("c")


def _remote(src, dst, send_sem, recv_sem, device):
    return pltpu.make_async_remote_copy(src_ref=src, dst_ref=dst, send_sem=send_sem, recv_sem=recv_sem,
                                        device_id=device, device_id_type=MESH)


HBM = pl.BlockSpec(memory_space=pltpu.HBM)
SEM = pl.BlockSpec(memory_space=pltpu.SEMAPHORE)
DATAFLOW = pltpu.SideEffectType.DATAFLOW_SIDE_EFFECTING


def _copies_start(name, arrays, plan, m, dep=None):
    n = len(arrays)
    arrays = [pltpu.with_memory_space_constraint(a, pltpu.HBM) for a in arrays]

    def body(*refs):
        ins, send_sem, recv_sem, token = refs[:n], refs[n], refs[n + 1], refs[2 * n + 2]
        for i, (src, dst, device, _) in enumerate(plan(ins, *_position())):
            _remote(src, dst, send_sem.at[i], recv_sem.at[i], device).start()
        token[...] = jnp.zeros_like(token)

    dep_specs, dep_ops = _dep_args(dep)
    outs = pl.pallas_call(
        _after(body, n, dep), name=name,
        out_shape=(pltpu.SemaphoreType.DMA((m,)), pltpu.SemaphoreType.DMA((m,)),
                   *[pltpu.HBM(a.shape, a.dtype) for a in arrays], jax.ShapeDtypeStruct((8, LANES), F32)),
        in_specs=[HBM] * n + dep_specs, out_specs=(SEM, SEM, *[HBM] * n, pl.BlockSpec(memory_space=pltpu.VMEM)),
        input_output_aliases={i: i + 2 for i in range(n)},
        compiler_params=pltpu.CompilerParams(has_side_effects=DATAFLOW),
    )(*arrays, *dep_ops)
    return (outs[0], outs[1]), list(outs[2:2 + n]), outs[2 + n]


def _copies_wait(name, sems, arrays, plan, after):
    n = len(arrays)
    afters = list(after) if isinstance(after, (list, tuple)) else [after]

    def body(*refs):
        ins, send_sem, recv_sem = refs[:n], refs[n], refs[n + 1]
        for i, (src, dst, device, landing) in enumerate(plan(ins, *_position())):
            _remote(src, dst, send_sem.at[i], recv_sem.at[i], device).wait_send()
            _remote(landing, landing, send_sem.at[i], recv_sem.at[i], device).wait_recv()

    outs = pl.pallas_call(
        body, name=name,
        out_shape=tuple(pltpu.HBM(a.shape, a.dtype) for a in arrays),
        in_specs=[HBM] * n + [SEM, SEM] + [ANY] * len(afters), out_specs=tuple([HBM] * n),
        input_output_aliases={i: i for i in range(n)},
        compiler_params=pltpu.CompilerParams(has_side_effects=DATAFLOW),
    )(*arrays, sems[0], sems[1], *afters)
    return list(outs)


def _tie(x, dep):
    for token in _dep_list(dep):
        x = x + token[0, 0]
    return x


def _other_chips(x, y):
    return [(1 - x, y), (x, 1 - y), (1 - x, 1 - y)]


def _gather_begin(tag, shards, token):
    n = len(shards)
    lands = [lax.empty((N_CHIPS,) + s.shape, s.dtype) for s in shards]

    def plan(refs, x, y, c):
        return [(refs[k].at[c], refs[n + k].at[2 * x + y, c], (ox, oy, c), refs[n + k].at[2 * ox + oy, c])
                for k in range(n) for ox, oy in _other_chips(x, y)]

    sems, thru, token = _copies_start(f"gather_{tag}_ici_start", list(shards) + lands, plan, 3 * n, dep=token)
    return dict(tag=tag, n=n, plan=plan, sems=sems, arrays=thru, token=token)


def _gather_forward(st, after):
    n, tag = st["n"], st["tag"]
    thru = _copies_wait(f"gather_{tag}_ici_wait", st["sems"], st["arrays"], st["plan"], after)

    def plan(refs, x, y, c):
        return [(refs[k].at[2 * ox + oy, c], refs[k].at[2 * ox + oy, c], (x, y, 1 - c), refs[k].at[2 * ox + oy, 1 - c])
                for k in range(n) for ox, oy in _other_chips(x, y)]

    sems, lands, token = _copies_start(f"gather_{tag}_fwd_start", thru[n:], plan, 3 * n)
    return dict(tag=tag, n=n, plan=plan, sems=sems, arrays=lands, token=token, shards=thru[:n])


def _gather_end(st, after):
    lands = _copies_wait(f"gather_{st['tag']}_fwd_wait", st["sems"], st["arrays"], st["plan"], after)
    me = 2 * lax.axis_index("x") + lax.axis_index("y")
    return [lax.dynamic_update_index_in_dim(g, s, me, 0) for g, s in zip(lands, st["shards"])]


def _add_keep_give(name, pos, a, a_keep, a_give, b, b_keep, b_give, steps):
    r, c = a.shape[-2:]

    def spec(arr, fn):
        lead = arr.ndim - 2
        return pl.BlockSpec((None,) * lead + (r, c), lambda i, p: tuple(fn(i, p)) + (0, 0))

    out_spec = pl.BlockSpec((None, r, c), lambda i, p: (i, 0, 0))

    def body(p_ref, ak_ref, bk_ref, ag_ref, bg_ref, keep_ref, give_ref):
        keep_ref[...] = ak_ref[...] + bk_ref[...].astype(F32)
        give_ref[...] = (ag_ref[...] + bg_ref[...].astype(F32)).astype(BF16)

    return pl.pallas_call(
        body, name=name,
        out_shape=(jax.ShapeDtypeStruct((steps, r, c), F32), jax.ShapeDtypeStruct((steps, r, c), BF16)),
        grid_spec=pltpu.PrefetchScalarGridSpec(
            num_scalar_prefetch=1, grid=(steps,),
            in_specs=[spec(a, a_keep), spec(b, b_keep), spec(a, a_give), spec(b, b_give)],
            out_specs=(out_spec, out_spec)),
        compiler_params=_cparams("parallel"),
    )(pos, a, b, a, b)


def _add_last(name, a, b):
    _, r, c = a.shape
    blk = pl.BlockSpec((None, r, c), lambda i: (0, 0, 0))

    def body(a_ref, b_ref, o_ref):
        o_ref[...] = a_ref[...] + b_ref[...].astype(F32)

    return pl.pallas_call(
        body, name=name, out_shape=jax.ShapeDtypeStruct((r, c), F32), grid=(1,), in_specs=[blk, blk],
        out_specs=pl.BlockSpec((r, c), lambda i: (0, 0)), compiler_params=_cparams("arbitrary"),
    )(a, b)


def _exchange_begin(tag, stage, gives, lands, peer_fn, extra):
    n = len(gives)

    def plan(refs, x, y, c):
        return [(refs[k], refs[n + k], peer_fn(x, y, c), refs[n + k]) for k in range(n)]

    sems, thru, token = _copies_start(f"rs{tag}_{stage}_start", gives + lands, plan, n)
    return dict(extra, tag=tag, n=n, stage=stage, plan=plan, sems=sems, arrays=thru, token=token)


def _reduce_begin(tag, grads):
    n = len(grads)
    lands = [lax.empty((N_CHIPS,) + g.shape[2:], F32) for g in grads]

    def plan(refs, x, y, c):
        return [(refs[k].at[j, 1 - c], refs[n + k].at[j], (x, y, 1 - c), refs[n + k].at[j])
                for k in range(n) for j in range(N_CHIPS)]

    sems, thru, token = _copies_start(f"rs{tag}_c_start", list(grads) + lands, plan, N_CHIPS * n)
    return dict(tag=tag, n=n, stage="c", plan=plan, sems=sems, arrays=thru, token=token)


def _reduce_advance(st, after):
    tag, n, stage = st["tag"], st["n"], st["stage"]
    thru = _copies_wait(f"rs{tag}_{stage}_wait", st["sems"], st["arrays"], st["plan"], after)
    first, recv = thru[:n], thru[n:]
    x, y, c = _position()
    if stage == "c":
        pos = jnp.stack([c, x]).astype(jnp.int32)
        sums = [_add_keep_give(
            f"rs{tag}_c_add{k}", pos,
            first[k], lambda i, p: (2 * p[1] + i, p[0]), lambda i, p: (2 * (1 - p[1]) + i, p[0]),
            recv[k], lambda i, p: (2 * p[1] + i,), lambda i, p: (2 * (1 - p[1]) + i,), 2) for k in range(n)]
        lands = [lax.empty(s[1].shape, BF16) for s in sums]
        return _exchange_begin(tag, "x", [s[1] for s in sums], lands, lambda x, y, c: (1 - x, y, c),
                               dict(keep=[s[0] for s in sums]))
    if stage == "x":
        pos = jnp.stack([y]).astype(jnp.int32)
        sums = [_add_keep_give(
            f"rs{tag}_x_add{k}", pos,
            st["keep"][k], lambda i, p: (p[0],), lambda i, p: (1 - p[0],),
            recv[k], lambda i, p: (p[0],), lambda i, p: (1 - p[0],), 1) for k in range(n)]
        lands = [lax.empty(s[1].shape, BF16) for s in sums]
        return _exchange_begin(tag, "y", [s[1] for s in sums], lands, lambda x, y, c: (x, 1 - y, c),
                               dict(keep=[s[0] for s in sums]))
    if stage == "y":
        mine = [_add_last(f"rs{tag}_y_add{k}", st["keep"][k], recv[k]) for k in range(n)]
        lands = [lax.empty(m.shape, F32) for m in mine]
        return _exchange_begin(tag, "swap", mine, lands, lambda x, y, c: (x, y, 1 - c), {})
    return dict(done=list(zip(first, recv)), token=None)


def _all_reduce_small(v):
    r = v.shape[0]

    def body(v_ref, out_ref, buf, send_sems, recv_sems, local_sem):
        x, y, c = _position()
        me, sibling = (x, y, c), (x, y, 1 - c)
        chips = [(1 - x, y), (x, 1 - y), (1 - x, 1 - y)]

        def rows(px, py, pc):
            return buf.at[pl.ds((4 * px + 2 * py + pc) * r, r), :]

        def copy(k, block, to, src=None):
            return _remote(rows(*block) if src is None else src, rows(*block), send_sems.at[k], recv_sems.at[k], to)

        mine = pltpu.make_async_copy(v_ref, rows(*me), local_sem)
        mine.start()
        first = [copy(0, me, sibling, src=v_ref)]
        first += [copy(1 + j, me, (*chip, c), src=v_ref) for j, chip in enumerate(chips)]
        for cp in first:
            cp.start()
        passed = [copy(4 + j, (*chip, c), sibling) for j, chip in enumerate(chips)]
        for j, chip in enumerate(chips):
            copy(1 + j, (*chip, c), me).wait_recv()
            passed[j].start()
        copy(0, sibling, me).wait_recv()
        for j, chip in enumerate(chips):
            copy(4 + j, (*chip, 1 - c), me).wait_recv()
        for cp in first + passed:
            cp.wait_send()
        mine.wait()
        acc = buf[0:r, :]
        for dev in range(1, N_DEV):
            acc = acc + buf[dev * r:(dev + 1) * r, :]
        out_ref[...] = acc

    return pl.pallas_call(
        body, name="all_reduce_small",
        out_shape=jax.ShapeDtypeStruct(v.shape, F32),
        in_specs=[pl.BlockSpec(memory_space=pltpu.VMEM)],
        out_specs=pl.BlockSpec(memory_space=pltpu.VMEM),
        scratch_shapes=[pltpu.VMEM((N_DEV * r, LANES), F32), pltpu.SemaphoreType.DMA((7,)),
                        pltpu.SemaphoreType.DMA((7,)), pltpu.SemaphoreType.DMA],
        compiler_params=pltpu.CompilerParams(has_side_effects=True, vmem_limit_bytes=VMEM_LIMIT_V7X),
    )(v)


def _adamw_update(w, gg, m, v):
    mn = ADAM_B1 * m + (1.0 - ADAM_B1) * gg
    vn = ADAM_B2 * v + (1.0 - ADAM_B2) * (gg * gg)
    m_hat = mn / (1.0 - ADAM_B1 ** ADAM_STEP)
    v_hat = vn / (1.0 - ADAM_B2 ** ADAM_STEP)
    return -ADAM_LR * (m_hat / (jnp.sqrt(v_hat) + ADAM_EPS) + ADAM_WD * w), mn, vn


def _adamw(name, w, g, m, v):
    def body(w_ref, g_ref, m_ref, v_ref, d_ref, mo_ref, vo_ref):
        d_ref[...], mo_ref[...], vo_ref[...] = _adamw_update(w_ref[...], g_ref[...], m_ref[...], v_ref[...])

    blk = pl.BlockSpec(w.shape, lambda i: (0, 0))
    return pl.pallas_call(
        body, name=name, out_shape=(jax.ShapeDtypeStruct(w.shape, F32),) * 3, grid=(1,),
        in_specs=[blk] * 4, out_specs=(blk,) * 3, compiler_params=_cparams("arbitrary"),
    )(w, g, m, v)


def _adamw_halves(name, pos_c, w, g_mine, g_other, m, v, tr, dep=None):
    r, c = w.shape
    rh = r // 2
    tr = tr if rh % tr == 0 else rh
    nt = rh // tr

    def body(p_ref, w_ref, gm_ref, go_ref, m_ref, v_ref, g_ref, d_ref, mo_ref, vo_ref):
        gg = jnp.where(pl.program_id(0) == p_ref[0], gm_ref[...], go_ref[...])
        g_ref[...] = gg
        d_ref[...], mo_ref[...], vo_ref[...] = _adamw_update(w_ref[...], gg, m_ref[...], v_ref[...])

    full = pl.BlockSpec((tr, c), lambda h, i, p: (h * nt + i, 0))
    half = pl.BlockSpec((tr, c), lambda h, i, p: (i, 0))
    dep_specs, dep_ops = _dep_args(dep)
    return pl.pallas_call(
        _after(body, 6, dep), name=name, out_shape=(jax.ShapeDtypeStruct((r, c), F32),) * 4,
        grid_spec=pltpu.PrefetchScalarGridSpec(
            num_scalar_prefetch=1, grid=(2, nt),
            in_specs=[full, half, half, full, full] + dep_specs, out_specs=(full,) * 4),
        compiler_params=_cparams("parallel", "parallel"),
    )(pos_c, w, g_mine, g_other, m, v, *dep_ops)


def _col_sharded_to_comm(g):
    k, n = g.shape
    return g.reshape(2, k // 2, N_CHIPS, n // N_CHIPS).transpose(2, 0, 1, 3)


def _row_sharded_to_comm(g):
    r, c = g.shape
    return g.reshape(N_CHIPS, 2, r // (2 * N_CHIPS), c)


def _col_sharded_full(g):
    _, _, rh, c = g.shape
    return g.reshape(N_CHIPS, 2 * rh, c).transpose(1, 0, 2).reshape(2 * rh, N_CHIPS * c)


def _row_sharded_full(g):
    _, _, rh, c = g.shape
    return g.reshape(N_CHIPS * 2 * rh, c)


def _pack_small(g1, bfv, mix, scale, g2n, gf, extra=None):
    row8 = jnp.pad(bfv.reshape(1, N_HEADS), ((0, 0), (0, LANES - N_HEADS)))
    if extra is not None:
        row8 = row8 + jnp.pad(extra[:, :1], ((0, 0), (N_HEADS, LANES - N_HEADS - 1)))
    return jnp.concatenate([
        g1.reshape(8, LANES), jnp.pad(row8, ((0, 7), (0, 0))), mix.reshape(512, LANES),
        jnp.pad(scale.reshape(4, LANES), ((0, 4), (0, 0))), g2n.reshape(8, LANES), gf.reshape(8, LANES)], axis=0)


def _unpack_small(s, like):
    g1, bfv, mix, scale, g2n, gf = like
    return (s[0:8].reshape(g1.shape), s[8, :N_HEADS].reshape(bfv.shape), s[16:528].reshape(mix.shape),
            s[528:532].reshape(scale.shape), s[536:544].reshape(g2n.shape), s[544:552].reshape(gf.shape))


class _MeshLinks:
    def __init__(self, shards_in, shards_rest):
        self.gin = _gather_begin("in", shards_in, None)
        self.grest = _gather_begin("rest", shards_rest, self.gin["token"])
        self.tokens = {"gather": self.grest["token"]}
        self.groups = {}

    @property
    def token(self):
        return list(self.tokens.values())

    def tie(self, x):
        return _tie(x, self.token)

    def weights_in(self, after):
        st = _gather_forward(self.gin, after)
        (g,) = _gather_end(st, st["token"])
        return _col_sharded_full(g)

    def rest_forward(self, after):
        self.grest = _gather_forward(self.grest, after)
        self.tokens["gather"] = self.grest["token"]

    def weights_rest(self, after):
        g = _gather_end(self.grest, after)
        del self.tokens["gather"]
        return [_col_sharded_full(g[0]), _col_sharded_full(g[1])] + [_row_sharded_full(a) for a in g[2:]]

    def reduce_begin(self, tag, grads):
        self.groups[tag] = _reduce_begin(tag, grads)
        self.tokens[tag] = self.groups[tag]["token"]

    def advance(self, after):
        for tag, st in self.groups.items():
            if "done" not in st:
                self.groups[tag] = _reduce_advance(st, after)
                if self.groups[tag]["token"] is None:
                    del self.tokens[tag]
                else:
                    self.tokens[tag] = self.groups[tag]["token"]

    def reduced(self, tag):
        return self.groups[tag]["done"]


class _NoLinks:
    token = None

    def __init__(self, w_in, rest):
        self.w_in, self.rest, self.grads = w_in, rest, {}

    def tie(self, x):
        return x

    def weights_in(self, after):
        return self.w_in

    def rest_forward(self, after):
        pass

    def weights_rest(self, after):
        return self.rest

    def reduce_begin(self, tag, grads):
        self.grads[tag] = grads

    def advance(self, after):
        pass


def _local_step(links, x, target, seq, norm1_g, b_forget, pool_mix, pool_scale, norm2_g, norm_f_g):
    t, d = x.shape
    tq = min(256, seq)
    aw = ATTN_WIDTH
    o_q, o_f, o_g = POOL_WIDTH, POOL_WIDTH + 3 * aw, POOL_WIDTH + 3 * aw + N_HEADS
    bf = jnp.pad(b_forget, ((0, 0), (0, LANES - N_HEADS)))
    mixb = pool_mix.astype(BF16)

    h = _norm_fwd("norm1_fwd", x, links.tie(norm1_g), 512)
    w_in = links.weights_in(h)
    wu = w_in[:, :o_q]
    wqkv = w_in[:, o_q:o_f]
    wf = jnp.pad(w_in[:, o_f:o_g], ((0, 0), (0, LANES - N_HEADS)))
    wg2 = w_in[:, o_g:]
    wft = wf.T
    u = _matmul("mm_u", h, wu, "nn", F32, 1024, 512, d)
    g2 = _matmul("mm_gates", h, wg2, "nn", BF16, 1024, 512, d)
    fl, fcum = _forget_fwd(h, wf, bf, seq)
    qa, ka, v = _attn_prep(h, _head_blocks(wqkv[:, :aw]), _head_blocks(wqkv[:, aw:2 * aw]), wqkv[:, 2 * aw:], fcum, 512)
    p, ps = _pool_fwd(u, mixb, pool_scale, seq)
    links.rest_forward([ps, qa, g2])
    o, lse = _attn_fwd(qa, ka, v, seq, tq, dep=links.token)
    w_pool_out, w_attn_out, w_out, w_ffn_gate, w_ffn_up, w_ffn_down = links.weights_rest(o)
    merged, x1 = _merge_fwd(x, ps, o, g2, w_pool_out, w_attn_out, w_out, 256)
    h2, gt, up, act, x2 = _ffn_fwd(x1, norm2_g, w_ffn_gate, w_ffn_up, w_ffn_down, 1024, 256)
    loss, dx2, d_gf = _final_fwd_bwd(x2, target, norm_f_g, 512)

    dgt, dup, dx1, d_g2n = _ffn_bwd(dx2, x1, norm2_g, gt, up, w_ffn_gate, w_ffn_up, w_ffn_down, 1024, 256)
    d_wd = _matmul("dw_down", act, dx2, "tn", F32, 1408, 1024, 1024)
    d_wg = _matmul("dw_gate", dgt, h2, "tn", F32, 1408, 1024, 1024)
    d_wu = _matmul("dw_up", dup, h2, "tn", F32, 1408, 1024, 1024)
    links.reduce_begin("a", [_row_sharded_to_comm(g) for g in (d_wg, d_wu, d_wd)])
    dpy, day, dg2, dps, da = _merge_bwd(dx1, ps, o, g2, w_pool_out, w_attn_out, w_out, 256, dep=links.token)
    links.advance(dps)
    d_wout = _matmul("dw_out", merged, dx1, "tn", F32, 1024, 1024, 1024)
    d_wpo = _matmul("dw_pool_out", ps, dpy, "tn", F32, 512, 1024, 1024)
    d_wao = _matmul("dw_attn_out", o, day, "tn", F32, 512, 1024, 1024)
    links.reduce_begin("m", [_col_sharded_to_comm(d_wpo), _col_sharded_to_comm(d_wao), _row_sharded_to_comm(d_wout)])
    dq, dk, dv, dfr = _attn_bwd(qa, ka, v, da, lse, seq, tq, dep=links.token)
    links.advance(dq)
    dfc = jnp.pad(dfr.reshape(N_HEADS, t).T, ((0, 0), (0, LANES - N_HEADS)))
    dfl, d_bf = _forget_bwd(dfc, fl, bf, seq)
    du, d_mix, d_scale = _pool_bwd(dps, p, mixb, links.tie(pool_scale), seq)
    d_wu_in = _matmul("dw_in_u", h, du, "tn", F32, 1024, 512, 1024)
    d_wq = _matmul("dw_in_q", h, dq, "tn", F32, 1024, 512, 1024)
    d_wk = _matmul("dw_in_k", h, dk, "tn", F32, 1024, 512, 1024)
    d_wv = _matmul("dw_in_v", h, dv, "tn", F32, 1024, 512, 1024)
    links.advance([d_wu_in, d_wq, d_wk, d_wv])
    d_wf = _matmul("dw_in_f", h, dfl, "tn", F32, 1024, LANES, 512)
    d_wg2 = _matmul("dw_in_gates", h, dg2, "tn", F32, 1024, 1024, 1024, dep=links.token)
    d_win = jnp.concatenate([d_wu_in, d_wq, d_wk, d_wv, d_wf[:, :N_HEADS], d_wg2], axis=1)
    comm_b = [_col_sharded_to_comm(d_win)]
    links.advance(comm_b)
    links.reduce_begin("b", comm_b)
    dx, d_g1 = _in_bwd(du, dq, dk, dv, dg2, dfl, dx1, x, links.tie(norm1_g), wu, wqkv, wg2, wft, 256)
    links.advance(dx)
    small = (d_g1, d_bf[:, :N_HEADS], d_mix, d_scale, d_g2n, d_gf)
    return loss, dx, small


def kernel(x, norm1_g, w_in, b_forget, pool_mix, pool_scale, w_pool_out, w_attn_out, w_out, norm2_g, w_ffn_gate, w_ffn_up, w_ffn_down, norm_f_g, loss_target, m_norm1_g, m_w_in, m_b_forget, m_pool_mix, m_pool_scale, m_w_pool_out, m_w_attn_out, m_w_out, m_norm2_g, m_w_ffn_gate, m_w_ffn_up, m_w_ffn_down, m_norm_f_g, v_norm1_g, v_w_in, v_b_forget, v_pool_mix, v_pool_scale, v_w_pool_out, v_w_attn_out, v_w_out, v_norm2_g, v_w_ffn_gate, v_w_ffn_up, v_w_ffn_down, v_norm_f_g):
    nb, seq, d = x.shape
    group_a = ((w_ffn_gate, m_w_ffn_gate, v_w_ffn_gate, True, 9), (w_ffn_up, m_w_ffn_up, v_w_ffn_up, True, 10),
               (w_ffn_down, m_w_ffn_down, v_w_ffn_down, False, 11))
    group_m = ((w_pool_out, m_w_pool_out, v_w_pool_out, False, 5), (w_attn_out, m_w_attn_out, v_w_attn_out, False, 6),
               (w_out, m_w_out, v_w_out, False, 7))
    group_b = ((w_in, m_w_in, v_w_in, False, 1),)
    small_w = (norm1_g, b_forget, pool_mix, pool_scale, norm2_g, norm_f_g)
    small_m = (m_norm1_g, m_b_forget, m_pool_mix, m_pool_scale, m_norm2_g, m_norm_f_g)
    small_v = (v_norm1_g, v_b_forget, v_pool_mix, v_pool_scale, v_norm2_g, v_norm_f_g)
    small_pos = (0, 2, 3, 4, 8, 12)
    view = lambda a, tr: a[0].T if tr else a[0]
    unview = lambda a, tr, like: (a.T if tr else a).reshape(like.shape)

    def shard(w, tr):
        lw = view(w, tr).astype(BF16)
        return lw.reshape(2, lw.shape[0] // 2, lw.shape[1])

    links = _MeshLinks([shard(w_in, False)],
                       [shard(w_pool_out, False), shard(w_attn_out, False), shard(w_out, False),
                        shard(w_ffn_gate, True), shard(w_ffn_up, True), shard(w_ffn_down, False)])
    loss, dx, small_g = _local_step(
        links, x.reshape(nb * seq, d), loss_target.reshape(nb * seq, d), seq,
        norm1_g, b_forget, pool_mix[0], pool_scale, norm2_g, norm_f_g.reshape(1, d))

    grads, deltas, new_m, new_v = [None] * 13, [None] * 13, [None] * 13, [None] * 13
    pos_c = jnp.stack([lax.axis_index("c")]).astype(jnp.int32)

    def update(tag, group, dep):
        last = []
        for k, ((w, m, v, tr, pos), (mine, other)) in enumerate(zip(group, links.reduced(tag))):
            outs = _adamw_halves(f"adamw_{tag}{k}", pos_c, view(w, tr), mine, other, view(m, tr), view(v, tr), 256,
                                 dep=dep)
            grads[pos], deltas[pos], new_m[pos], new_v[pos] = (unview(a, tr, w) for a in outs)
            last.append(outs[1])
        return last

    last = update("a", group_a, links.token) + update("m", group_m, links.token)
    links.advance(last)
    small_sum = _all_reduce_small(links.tie(_pack_small(*small_g, extra=loss)))
    loss_out = small_sum[8, N_HEADS]
    dl, mn, vn = _adamw("adamw_small", _pack_small(*small_w), small_sum * _small_mask(), _pack_small(*small_m),
                        _pack_small(*small_v))
    for pos, g, a, b, e in zip(small_pos, _unpack_small(small_sum, small_w), _unpack_small(dl, small_w),
                               _unpack_small(mn, small_w), _unpack_small(vn, small_w)):
        grads[pos], deltas[pos], new_m[pos], new_v[pos] = g, a, b, e
    links.advance(dl)
    links.advance(links.token)
    update("b", group_b, None)

    return (loss_out, dx.reshape(nb, seq, d), *grads, *deltas, *new_m, *new_v)


def _small_mask():
    rows = lax.broadcasted_iota(jnp.int32, (552, LANES), 0)
    lanes = lax.broadcasted_iota(jnp.int32, (552, LANES), 1)
    return jnp.where(jnp.logical_and(rows == 8, lanes == N_HEADS), 0.0, 1.0).astype(F32)
```

```python
import functools

import jax
import jax.numpy as jnp
from jax import lax
from jax.experimental import pallas as pl
from jax.experimental.pallas import tpu as pltpu

F32 = jnp.float32
BF16 = jnp.bfloat16

D_MODEL = 1024
POOL_WINDOWS = (2, 4, 8, 16)
POOL_GROUPS = 4
POOL_GROUP_DIM = 128
POOL_WIDTH = 512
HEAD_DIM = 64
N_HEADS = 8
ATTN_WIDTH = 512
D_FF = 2816
RMS_EPS = 1e-6
ATTN_SCALE = HEAD_DIM ** -0.5
NEG_BIG = -1e30

ADAM_LR = 0.001
ADAM_B1 = 0.9
ADAM_B2 = 0.999
ADAM_EPS = 1e-08
ADAM_WD = 0.01
ADAM_STEP = 10

LANES = 128
N_CHIPS = 4
N_DEV = 8
VMEM_LIMIT_V7X = 52 * 1024 * 1024
MESH = pl.DeviceIdType.MESH
ANY = pl.BlockSpec(memory_space=pl.ANY)


def _cparams(*sem):
    return pltpu.CompilerParams(dimension_semantics=sem if sem else None, vmem_limit_bytes=VMEM_LIMIT_V7X)


def _dep_list(dep):
    return [] if dep is None else (list(dep) if isinstance(dep, (list, tuple)) else [dep])


def _after(body, n_in, dep):
    k = len(_dep_list(dep))
    if k == 0:
        return body

    def wrapped(*refs):
        body(*refs[:n_in], *refs[n_in + k:])

    return wrapped


def _dep_args(dep):
    deps = _dep_list(dep)
    return [ANY] * len(deps), deps


def _dot(a, b):
    return lax.dot_general(a, b, (((1,), (0,)), ((), ())), preferred_element_type=F32)


def _dot_nt(a, b):
    return lax.dot_general(a, b, (((1,), (1,)), ((), ())), preferred_element_type=F32)


def _dot_tn(a, b):
    return lax.dot_general(a, b, (((0,), (0,)), ((), ())), preferred_element_type=F32)


def _sigmoid(x):
    return jax.nn.sigmoid(x)


def _rms_fwd(x, g):
    r = lax.rsqrt(jnp.mean(x * x, axis=-1, keepdims=True) + RMS_EPS)
    return (x * r) * g


def _rms_bwd(x, g, dy):
    r = lax.rsqrt(jnp.mean(x * x, axis=-1, keepdims=True) + RMS_EPS)
    xh = x * r
    dg = jnp.sum(dy * xh, axis=0, keepdims=True)
    dxh = dy * g
    dx = r * (dxh - xh * jnp.mean(dxh * xh, axis=-1, keepdims=True))
    return dx, dg


def _matmul(name, a, b, mode, out_dtype, tm, tn, tk, dep=None):
    if mode == "nn":
        (m, k), (_, n) = a.shape, b.shape
    elif mode == "nt":
        (m, k), (n, _) = a.shape, b.shape
    else:
        (k, m), (_, n) = a.shape, b.shape
    tm, tn, tk = min(tm, m), min(tn, n), min(tk, k)
    assert m % tm == 0 and n % tn == 0 and k % tk == 0, (name, m, n, k, tm, tn, tk)
    nk = k // tk
    if mode == "tn":
        a_spec = pl.BlockSpec((tk, tm), lambda i, j, kk: (kk, i))
    else:
        a_spec = pl.BlockSpec((tm, tk), lambda i, j, kk: (i, kk))
    if mode == "nt":
        b_spec = pl.BlockSpec((tn, tk), lambda i, j, kk: (j, kk))
    else:
        b_spec = pl.BlockSpec((tk, tn), lambda i, j, kk: (kk, j))
    dot = {"nn": _dot, "nt": _dot_nt, "tn": _dot_tn}[mode]
    use_scratch = nk > 1 and out_dtype != F32

    def body(a_ref, b_ref, o_ref, *scratch):
        prod = dot(a_ref[...].astype(BF16), b_ref[...].astype(BF16))
        if nk == 1:
            o_ref[...] = prod.astype(out_dtype)
            return
        acc = scratch[0] if use_scratch else o_ref
        kk = pl.program_id(2)

        @pl.when(kk == 0)
        def _():
            acc[...] = prod

        @pl.when(kk > 0)
        def _():
            acc[...] += prod

        if use_scratch:
            @pl.when(kk == nk - 1)
            def _():
                o_ref[...] = acc[...].astype(out_dtype)

    dep_specs, dep_ops = _dep_args(dep)
    return pl.pallas_call(
        _after(body, 2, dep),
        name=name,
        out_shape=jax.ShapeDtypeStruct((m, n), out_dtype),
        grid=(m // tm, n // tn, nk),
        in_specs=[a_spec, b_spec] + dep_specs,
        out_specs=pl.BlockSpec((tm, tn), lambda i, j, kk: (i, j)),
        scratch_shapes=[pltpu.VMEM((tm, tn), F32)] if use_scratch else [],
        compiler_params=_cparams("parallel", "parallel", "arbitrary"),
    )(a, b, *dep_ops)


def _norm_fwd(name, x, g, tm):
    t, d = x.shape
    tm = min(tm, t)

    def body(x_ref, g_ref, h_ref):
        h_ref[...] = _rms_fwd(x_ref[...], g_ref[...]).astype(BF16)

    return pl.pallas_call(
        body, name=name, out_shape=jax.ShapeDtypeStruct((t, d), BF16), grid=(t // tm,),
        in_specs=[pl.BlockSpec((tm, d), lambda i: (i, 0)), pl.BlockSpec((1, d), lambda i: (0, 0))],
        out_specs=pl.BlockSpec((tm, d), lambda i: (i, 0)),
        compiler_params=_cparams("parallel"),
    )(x, g)


def _split3(x):
    hi = x.astype(BF16)
    r1 = x - hi.astype(F32)
    mid = r1.astype(BF16)
    lo = (r1 - mid.astype(F32)).astype(BF16)
    return hi, mid, lo


def _tri_dot(tri, x):
    hi, mid, lo = _split3(x)
    return _dot(tri, hi) + _dot(tri, mid) + _dot(tri, lo)


def _forget_fwd(h, wf, bf, seq):
    t, d = h.shape
    cb = min(256, seq)

    def body(h_ref, wf_ref, bf_ref, fl_ref, fc_ref):
        fl = _dot(h_ref[...], wf_ref[...])
        fl_ref[...] = fl
        xx = fl + bf_ref[...]
        lf = jnp.minimum(xx, 0.0) - jnp.log(1.0 + jnp.exp(-jnp.abs(xx)))
        ri = lax.broadcasted_iota(jnp.int32, (cb, cb), 0)
        ci = lax.broadcasted_iota(jnp.int32, (cb, cb), 1)
        tri = (ri >= ci).astype(BF16)
        carry = jnp.zeros((1, LANES), F32)
        for blk in range(seq // cb):
            cs = _tri_dot(tri, lf[blk * cb:(blk + 1) * cb]) + carry
            fc_ref[blk * cb:(blk + 1) * cb, :] = cs
            carry = cs[cb - 1:cb, :]

    return pl.pallas_call(
        body, name="forget_fwd",
        out_shape=(jax.ShapeDtypeStruct((t, LANES), F32), jax.ShapeDtypeStruct((t, LANES), F32)),
        grid=(t // seq,),
        in_specs=[pl.BlockSpec((seq, d), lambda b: (b, 0)), pl.BlockSpec((d, LANES), lambda b: (0, 0)),
                  pl.BlockSpec((1, LANES), lambda b: (0, 0))],
        out_specs=(pl.BlockSpec((seq, LANES), lambda b: (b, 0)), pl.BlockSpec((seq, LANES), lambda b: (b, 0))),
        compiler_params=_cparams("parallel"),
    )(h, wf, bf)


def _pool_fwd(u, mix, scale, seq):
    t = u.shape[0]

    def body(u_ref, mix_ref, sc_ref, p_ref, ps_ref):
        tpos = lax.broadcasted_iota(jnp.int32, (seq, POOL_GROUP_DIM), 0)
        for g in range(POOL_GROUPS):
            sl = slice(g * POOL_GROUP_DIM, (g + 1) * POOL_GROUP_DIM)
            ug = u_ref[:, sl]
            s = ug
            for lvl in range(g + 1):
                d = 2 ** lvl
                s = s + jnp.where(tpos >= d, pltpu.roll(s, d, 0), 0.0)
            cnt = jnp.minimum(tpos + 1, POOL_WINDOWS[g]).astype(F32)
            pb = (s / cnt - ug).astype(BF16)
            p_ref[:, sl] = pb
            ps_ref[:, sl] = (_dot(pb, mix_ref[g]) * sc_ref[:, sl]).astype(BF16)

    return pl.pallas_call(
        body, name="pool_fwd",
        out_shape=(jax.ShapeDtypeStruct((t, POOL_WIDTH), BF16), jax.ShapeDtypeStruct((t, POOL_WIDTH), BF16)),
        grid=(t // seq,),
        in_specs=[pl.BlockSpec((seq, POOL_WIDTH), lambda b: (b, 0)),
                  pl.BlockSpec((POOL_GROUPS, POOL_GROUP_DIM, POOL_GROUP_DIM), lambda b: (0, 0, 0)),
                  pl.BlockSpec((1, POOL_WIDTH), lambda b: (0, 0))],
        out_specs=(pl.BlockSpec((seq, POOL_WIDTH), lambda b: (b, 0)), pl.BlockSpec((seq, POOL_WIDTH), lambda b: (b, 0))),
        compiler_params=_cparams("parallel"),
    )(u, mix, scale)


def _aug_constants():
    w = N_HEADS * LANES
    rows = jnp.arange(3 * LANES)
    piece, head = rows // LANES, rows % LANES
    cols = jnp.arange(w)
    live = (head < N_HEADS)[:, None]
    pq = (live & (cols[None, :] == (head * LANES + HEAD_DIM + piece)[:, None])).astype(BF16)
    pk = -(live & (cols[None, :] == (head * LANES + HEAD_DIM + 3 + piece)[:, None])).astype(BF16)
    lane = cols % LANES
    oq = ((lane >= HEAD_DIM + 3) & (lane < HEAD_DIM + 6)).astype(F32)[None, :]
    ok = ((lane >= HEAD_DIM) & (lane < HEAD_DIM + 3)).astype(F32)[None, :]
    return pq, pk, oq, ok


def _head_blocks(w):
    d = w.shape[0]
    return jnp.pad(w.reshape(d, N_HEADS, HEAD_DIM), ((0, 0), (0, 0), (0, LANES - HEAD_DIM))).reshape(d, N_HEADS * LANES)


def _attn_prep(h, wq, wk, wv, fcum, tm):
    t, d = h.shape
    tm = min(tm, t)
    w = N_HEADS * LANES
    pq, pk, oq, ok = _aug_constants()

    def body(h_ref, wq_ref, wk_ref, wv_ref, f_ref, pq_ref, pk_ref, oq_ref, ok_ref, qa_ref, ka_ref, v_ref):
        hh = h_ref[...]
        fs = jnp.concatenate(_split3(f_ref[...]), axis=1)
        q = _dot(hh, wq_ref[...]).astype(BF16).astype(F32) * ATTN_SCALE
        qa_ref[...] = (q + _dot(fs, pq_ref[...]) + oq_ref[...]).astype(BF16)
        k = _dot(hh, wk_ref[...]).astype(BF16).astype(F32)
        ka_ref[...] = (k + _dot(fs, pk_ref[...]) + ok_ref[...]).astype(BF16)
        v_ref[...] = _dot(hh, wv_ref[...]).astype(BF16)

    row = lambda n: pl.BlockSpec((tm, n), lambda i: (i, 0))
    full = lambda a: pl.BlockSpec(a.shape, lambda i: (0, 0))
    return pl.pallas_call(
        body, name="attn_prep",
        out_shape=(jax.ShapeDtypeStruct((t, w), BF16), jax.ShapeDtypeStruct((t, w), BF16),
                   jax.ShapeDtypeStruct((t, ATTN_WIDTH), BF16)),
        grid=(t // tm,),
        in_specs=[row(d), full(wq), full(wk), full(wv), row(LANES), full(pq), full(pk), full(oq), full(ok)],
        out_specs=(row(w), row(w), row(ATTN_WIDTH)),
        compiler_params=_cparams("parallel"),
    )(h, wq, wk, wv, fcum, pq, pk, oq, ok)


def _fold_lanes(x, op):
    out = x[:, :LANES]
    for g in range(1, x.shape[1] // LANES):
        out = op(out, x[:, g * LANES:(g + 1) * LANES])
    return out


def _attn_fwd(qa, ka, v, seq, tq, dep=None):
    t = qa.shape[0]
    nq = seq // tq
    hp_n = N_HEADS // 2
    heads = [slice(e * LANES, (e + 1) * LANES) for e in range(2)]

    def body(q_ref, k_ref, v_ref, o_ref, lse_ref, s_buf):
        i = pl.program_id(2)
        diag_ok = lax.broadcasted_iota(jnp.int32, (tq, tq), 0) >= lax.broadcasted_iota(jnp.int32, (tq, tq), 1)
        qs = [q_ref[:, hl] for hl in heads]

        def sweep1(j, mxs):
            r0 = pl.multiple_of(j * tq, tq)
            out = []
            for e, hl in enumerate(heads):
                s = _dot_nt(qs[e], k_ref[pl.ds(r0, tq), hl])
                s = jnp.where(jnp.logical_or(diag_ok, j < i), s, NEG_BIG)
                s_buf[e, j] = s
                out.append(jnp.maximum(mxs[e], _fold_lanes(s, jnp.maximum)))
            return tuple(out)

        mxs = lax.fori_loop(0, i + 1, sweep1, (jnp.full((tq, LANES), NEG_BIG, F32),) * 2)
        ms = [jnp.max(mx, axis=1, keepdims=True) for mx in mxs]

        def sweep2(j, carry):
            r0 = pl.multiple_of(j * tq, tq)
            vv = v_ref[pl.ds(r0, tq), :]
            out = []
            for e in range(2):
                p = jnp.exp(s_buf[e, j] - ms[e])
                out += [carry[2 * e] + _fold_lanes(p, jnp.add), carry[2 * e + 1] + _dot(p.astype(BF16), vv)]
            return tuple(out)

        res = lax.fori_loop(0, i + 1, sweep2, (jnp.zeros((tq, LANES), F32),) * 4)
        outs = []
        for e in range(2):
            l = jnp.sum(res[2 * e], axis=1, keepdims=True)
            outs.append(res[2 * e + 1] / l)
            lse_ref[:, e:e + 1] = ms[e] + jnp.log(l)
        lane = lax.broadcasted_iota(jnp.int32, (tq, LANES), 1)
        o_ref[...] = jnp.where(lane < HEAD_DIM, outs[0], outs[1])

    dep_specs, dep_ops = _dep_args(dep)
    return pl.pallas_call(
        _after(body, 3, dep), name="attn_fwd",
        out_shape=(jax.ShapeDtypeStruct((t, ATTN_WIDTH), F32), jax.ShapeDtypeStruct((hp_n, t, 2), F32)),
        grid=(t // seq, hp_n, nq),
        in_specs=[pl.BlockSpec((tq, 2 * LANES), lambda b, hp, i: (b * nq + i, hp)),
                  pl.BlockSpec((seq, 2 * LANES), lambda b, hp, i: (b, hp)),
                  pl.BlockSpec((seq, LANES), lambda b, hp, i: (b, hp))] + dep_specs,
        out_specs=(pl.BlockSpec((tq, LANES), lambda b, hp, i: (b * nq + i, hp)),
                   pl.BlockSpec((None, tq, 2), lambda b, hp, i: (hp, b * nq + i, 0))),
        scratch_shapes=[pltpu.VMEM((2, nq, tq, tq), F32)],
        compiler_params=_cparams("parallel", "parallel", "arbitrary"),
    )(qa, ka, v, *dep_ops)


def _merge_fwd(x, ps, o, g2, wpo, wao, wout, tm):
    t, d = x.shape
    tm = min(tm, t)

    def body(x_ref, ps_ref, o_ref, gp_ref, ga_ref, wpo_ref, wao_ref, wout_ref, mg_ref, x1_ref):
        py = _dot(ps_ref[...], wpo_ref[...])
        ay = _dot(o_ref[...].astype(BF16), wao_ref[...])
        mb = (_sigmoid(gp_ref[...].astype(F32)) * py + _sigmoid(ga_ref[...].astype(F32)) * ay).astype(BF16)
        mg_ref[...] = mb
        x1_ref[...] = x_ref[...] + _dot(mb, wout_ref[...])

    row = lambda w: pl.BlockSpec((tm, w), lambda i: (i, 0))
    full = lambda a: pl.BlockSpec(a.shape, lambda i: (0, 0))
    return pl.pallas_call(
        body, name="merge_fwd",
        out_shape=(jax.ShapeDtypeStruct((t, d), BF16), jax.ShapeDtypeStruct((t, d), F32)),
        grid=(t // tm,),
        in_specs=[row(d), row(POOL_WIDTH), row(ATTN_WIDTH), pl.BlockSpec((tm, d), lambda i: (i, 0)),
                  pl.BlockSpec((tm, d), lambda i: (i, 1)), full(wpo), full(wao), full(wout)],
        out_specs=(row(d), row(d)),
        compiler_params=_cparams("parallel"),
    )(x, ps, o, g2, g2, wpo, wao, wout)


def _ffn_fwd(x1, g, wg, wu, wd, tm, tf):
    t, d = x1.shape
    f = wg.shape[0]
    tm = min(tm, t)
    nf = f // tf

    def body(x1_ref, g_ref, wg_ref, wu_ref, wd_ref, h2_ref, gt_ref, up_ref, act_ref, x2_ref):
        j = pl.program_id(1)

        @pl.when(j == 0)
        def _():
            h2_ref[...] = _rms_fwd(x1_ref[...], g_ref[...]).astype(BF16)

        h2 = h2_ref[...]
        gt = _dot_nt(h2, wg_ref[...])
        up = _dot_nt(h2, wu_ref[...])
        sg = _sigmoid(gt)
        silu = gt * sg
        act = (silu * up).astype(BF16)
        gt_ref[...] = (up * (sg * (1.0 + gt * (1.0 - sg)))).astype(BF16)
        up_ref[...] = silu.astype(BF16)
        act_ref[...] = act
        prod = _dot(act, wd_ref[...])

        @pl.when(j == 0)
        def _():
            x2_ref[...] = prod

        @pl.when(j > 0)
        def _():
            x2_ref[...] += prod

        @pl.when(j == nf - 1)
        def _():
            x2_ref[...] += x1_ref[...]

    return pl.pallas_call(
        body, name="ffn_fwd",
        out_shape=(jax.ShapeDtypeStruct((t, d), BF16), jax.ShapeDtypeStruct((t, f), BF16),
                   jax.ShapeDtypeStruct((t, f), BF16), jax.ShapeDtypeStruct((t, f), BF16),
                   jax.ShapeDtypeStruct((t, d), F32)),
        grid=(t // tm, nf),
        in_specs=[pl.BlockSpec((tm, d), lambda i, j: (i, 0)), pl.BlockSpec((1, d), lambda i, j: (0, 0)),
                  pl.BlockSpec((tf, d), lambda i, j: (j, 0)), pl.BlockSpec((tf, d), lambda i, j: (j, 0)),
                  pl.BlockSpec((tf, d), lambda i, j: (j, 0))],
        out_specs=(pl.BlockSpec((tm, d), lambda i, j: (i, 0)), pl.BlockSpec((tm, tf), lambda i, j: (i, j)),
                   pl.BlockSpec((tm, tf), lambda i, j: (i, j)), pl.BlockSpec((tm, tf), lambda i, j: (i, j)),
                   pl.BlockSpec((tm, d), lambda i, j: (i, 0))),
        compiler_params=_cparams("parallel", "arbitrary"),
    )(x1, g, wg, wu, wd)


def _final_fwd_bwd(x2, target, g, tm):
    t, d = x2.shape
    tm = min(tm, t)

    def body(x_ref, t_ref, g_ref, loss_ref, dx_ref, dg_ref):
        i = pl.program_id(0)
        x = x_ref[...]
        gg = g_ref[...]
        err = _rms_fwd(x, gg) - t_ref[...]
        part = 0.5 * jnp.sum(jnp.mean(err * err, axis=-1, keepdims=True), axis=0, keepdims=True)
        dx, dg = _rms_bwd(x, gg, err * (1.0 / d))
        dx_ref[...] = dx

        @pl.when(i == 0)
        def _():
            loss_ref[...] = jnp.zeros_like(loss_ref)
            dg_ref[...] = jnp.zeros_like(dg_ref)

        loss_ref[...] += jnp.broadcast_to(part, loss_ref.shape)
        dg_ref[...] += dg

    return pl.pallas_call(
        body, name="final_fwd_bwd",
        out_shape=(jax.ShapeDtypeStruct((1, LANES), F32), jax.ShapeDtypeStruct((t, d), F32),
                   jax.ShapeDtypeStruct((1, d), F32)),
        grid=(t // tm,),
        in_specs=[pl.BlockSpec((tm, d), lambda i: (i, 0)), pl.BlockSpec((tm, d), lambda i: (i, 0)),
                  pl.BlockSpec((1, d), lambda i: (0, 0))],
        out_specs=(pl.BlockSpec((1, LANES), lambda i: (0, 0)), pl.BlockSpec((tm, d), lambda i: (i, 0)),
                   pl.BlockSpec((1, d), lambda i: (0, 0))),
        compiler_params=_cparams("arbitrary"),
    )(x2, target, g)


def _ffn_bwd(dx2, x1, g, gt, up, wg, wu, wd, tm, tf):
    t, d = dx2.shape
    f = gt.shape[1]
    tm = min(tm, t)
    nf = f // tf

    def body(dx2_ref, x1_ref, g_ref, gt_ref, up_ref, wg_ref, wu_ref, wd_ref, dgt_ref, dup_ref, dx1_ref, dg_ref, acc_ref,
             dxb_ref):
        i, j = pl.program_id(0), pl.program_id(1)

        @pl.when(j == 0)
        def _():
            dxb_ref[...] = dx2_ref[...].astype(BF16)

        dact = _dot_nt(dxb_ref[...], wd_ref[...])
        dgt = (dact * gt_ref[...].astype(F32)).astype(BF16)
        dup = (dact * up_ref[...].astype(F32)).astype(BF16)
        dgt_ref[...] = dgt
        dup_ref[...] = dup
        contrib = _dot(dgt, wg_ref[...]) + _dot(dup, wu_ref[...])

        @pl.when(j == 0)
        def _():
            acc_ref[...] = contrib

        @pl.when(j > 0)
        def _():
            acc_ref[...] += contrib

        @pl.when(jnp.logical_and(i == 0, j == 0))
        def _():
            dg_ref[...] = jnp.zeros_like(dg_ref)

        @pl.when(j == nf - 1)
        def _():
            dxn, dg = _rms_bwd(x1_ref[...], g_ref[...], acc_ref[...])
            dx1_ref[...] = dx2_ref[...] + dxn
            dg_ref[...] += dg

    return pl.pallas_call(
        body, name="ffn_bwd",
        out_shape=(jax.ShapeDtypeStruct((t, f), BF16), jax.ShapeDtypeStruct((t, f), BF16),
                   jax.ShapeDtypeStruct((t, d), F32), jax.ShapeDtypeStruct((1, d), F32)),
        grid=(t // tm, nf),
        in_specs=[pl.BlockSpec((tm, d), lambda i, j: (i, 0)), pl.BlockSpec((tm, d), lambda i, j: (i, 0)),
                  pl.BlockSpec((1, d), lambda i, j: (0, 0)),
                  pl.BlockSpec((tm, tf), lambda i, j: (i, j)), pl.BlockSpec((tm, tf), lambda i, j: (i, j)),
                  pl.BlockSpec((tf, d), lambda i, j: (j, 0)), pl.BlockSpec((tf, d), lambda i, j: (j, 0)),
                  pl.BlockSpec((tf, d), lambda i, j: (j, 0))],
        out_specs=(pl.BlockSpec((tm, tf), lambda i, j: (i, j)), pl.BlockSpec((tm, tf), lambda i, j: (i, j)),
                   pl.BlockSpec((tm, d), lambda i, j: (i, 0)), pl.BlockSpec((1, d), lambda i, j: (0, 0))),
        scratch_shapes=[pltpu.VMEM((tm, d), F32), pltpu.VMEM((tm, d), BF16)],
        compiler_params=_cparams("arbitrary", "arbitrary"),
    )(dx2, x1, g, gt, up, wg, wu, wd)


def _merge_bwd(dx1, ps, o, g2, wpo, wao, wout, tm, dep=None):
    t, d = dx1.shape
    tm = min(tm, t)

    def body(dx1_ref, ps_ref, o_ref, gp_ref, ga_ref, wpo_ref, wao_ref, wout_ref, dpy_ref, day_ref, dg2_ref, dps_ref, da_ref):
        dm = _dot_nt(dx1_ref[...].astype(BF16), wout_ref[...])
        py = _dot(ps_ref[...], wpo_ref[...])
        ay = _dot(o_ref[...].astype(BF16), wao_ref[...])
        sp = _sigmoid(gp_ref[...].astype(F32))
        sa = _sigmoid(ga_ref[...].astype(F32))
        dpy = (dm * sp).astype(BF16)
        day = (dm * sa).astype(BF16)
        dpy_ref[...] = dpy
        day_ref[...] = day
        dg2_ref[:, :d] = (dm * py * (sp * (1.0 - sp))).astype(BF16)
        dg2_ref[:, d:] = (dm * ay * (sa * (1.0 - sa))).astype(BF16)
        dps_ref[...] = _dot_nt(dpy, wpo_ref[...])
        da_ref[...] = _dot_nt(day, wao_ref[...]).astype(BF16)

    row = lambda w: pl.BlockSpec((tm, w), lambda i: (i, 0))
    full = lambda a: pl.BlockSpec(a.shape, lambda i: (0, 0))
    dep_specs, dep_ops = _dep_args(dep)
    return pl.pallas_call(
        _after(body, 8, dep), name="merge_bwd",
        out_shape=(jax.ShapeDtypeStruct((t, d), BF16), jax.ShapeDtypeStruct((t, d), BF16),
                   jax.ShapeDtypeStruct((t, 2 * d), BF16), jax.ShapeDtypeStruct((t, POOL_WIDTH), F32),
                   jax.ShapeDtypeStruct((t, ATTN_WIDTH), BF16)),
        grid=(t // tm,),
        in_specs=[row(d), row(POOL_WIDTH), row(ATTN_WIDTH), pl.BlockSpec((tm, d), lambda i: (i, 0)),
                  pl.BlockSpec((tm, d), lambda i: (i, 1)), full(wpo), full(wao), full(wout)] + dep_specs,
        out_specs=(row(d), row(d), row(2 * d), row(POOL_WIDTH), row(ATTN_WIDTH)),
        compiler_params=_cparams("parallel"),
    )(dx1, ps, o, g2, g2, wpo, wao, wout, *dep_ops)


def _attn_bwd(qa, ka, v, do, lse4, seq, tq, dep=None):
    t = qa.shape[0]
    nq = seq // tq
    hp_n = N_HEADS // 2
    heads = [slice(e * LANES, (e + 1) * LANES) for e in range(2)]

    def body(q_ref, k_ref, v_ref, do_ref, lse_ref, dq_ref, dk_ref, dv_ref, dfr_ref, dk_acc, dv_acc, p_buf, dp_buf):
        diag_ok = lax.broadcasted_iota(jnp.int32, (tq, tq), 0) >= lax.broadcasted_iota(jnp.int32, (tq, tq), 1)
        lane_q = lax.broadcasted_iota(jnp.int32, (tq, LANES), 1)
        lane_s = lax.broadcasted_iota(jnp.int32, (seq, LANES), 1)
        mine_q = [lane_q < HEAD_DIM, lane_q >= HEAD_DIM]
        dv_acc[...] = jnp.zeros_like(dv_acc)
        dk_acc[...] = jnp.zeros_like(dk_acc)
        dfr_ref[...] = jnp.zeros_like(dfr_ref)

        def q_step(i, _):
            q0 = pl.multiple_of(i * tq, tq)
            qs = [q_ref[pl.ds(q0, tq), hl] for hl in heads]
            dov = do_ref[pl.ds(q0, tq), :]
            dos = [jnp.where(mq, dov, jnp.zeros((), BF16)) for mq in mine_q]
            lss = [lse_ref[pl.ds(q0, tq), e:e + 1] for e in range(2)]

            def sweep1(j, dls):
                r0 = pl.multiple_of(j * tq, tq)
                vv = v_ref[pl.ds(r0, tq), :]
                out = []
                for e, hl in enumerate(heads):
                    s = _dot_nt(qs[e], k_ref[pl.ds(r0, tq), hl])
                    s = jnp.where(jnp.logical_or(diag_ok, j < i), s, NEG_BIG)
                    p = jnp.exp(s - lss[e])
                    dp = _dot_nt(dos[e], vv)
                    p_buf[e, j] = p
                    dp_buf[e, j] = dp
                    dv_acc[pl.ds(r0, tq), :] += _dot_tn(p.astype(BF16), dos[e])
                    out.append(dls[e] + _fold_lanes(p * dp, jnp.add))
                return tuple(out)

            dls = lax.fori_loop(0, i + 1, sweep1, (jnp.zeros((tq, LANES), F32),) * 2)
            dls = [jnp.sum(d, axis=1, keepdims=True) for d in dls]

            def sweep2(j, dqs):
                r0 = pl.multiple_of(j * tq, tq)
                out = []
                for e, hl in enumerate(heads):
                    ds = p_buf[e, j] * (dp_buf[e, j] - dls[e])
                    dfr_ref[e, pl.ds(j, 1), :] += jnp.sum(ds, axis=0, keepdims=True)
                    dsb = ds.astype(BF16)
                    dk_acc[e, pl.ds(r0, tq), :] += _dot_tn(dsb, qs[e])
                    out.append(dqs[e] + _dot(dsb, k_ref[pl.ds(r0, tq), hl]))
                return tuple(out)

            dqs = lax.fori_loop(0, i + 1, sweep2, (jnp.zeros((tq, LANES), F32),) * 2)
            dq = jnp.where(mine_q[0], dqs[0], pltpu.roll(dqs[1], HEAD_DIM, 1)) * ATTN_SCALE
            dq_ref[pl.ds(q0, tq), :] = dq.astype(BF16)
            return 0

        lax.fori_loop(0, nq, q_step, 0)
        dk_ref[...] = jnp.where(lane_s < HEAD_DIM, dk_acc[0], pltpu.roll(dk_acc[1], HEAD_DIM, 1)).astype(BF16)
        dv_ref[...] = dv_acc[...].astype(BF16)

    wide = pl.BlockSpec((seq, 2 * LANES), lambda b, hp: (b, hp))
    col = pl.BlockSpec((seq, LANES), lambda b, hp: (b, hp))
    pair = pl.BlockSpec((None, seq, 2), lambda b, hp: (hp, b, 0))
    dep_specs, dep_ops = _dep_args(dep)
    return pl.pallas_call(
        _after(body, 5, dep), name="attn_bwd",
        out_shape=(jax.ShapeDtypeStruct((t, ATTN_WIDTH), BF16),) * 3 + (jax.ShapeDtypeStruct((N_HEADS, t // tq, tq), F32),),
        grid=(t // seq, hp_n),
        in_specs=[wide, wide, col, col, pair] + dep_specs,
        out_specs=(col, col, col, pl.BlockSpec((2, nq, tq), lambda b, hp: (hp, b, 0))),
        scratch_shapes=[pltpu.VMEM((2, seq, LANES), F32), pltpu.VMEM((seq, LANES), F32),
                        pltpu.VMEM((2, nq, tq, tq), F32), pltpu.VMEM((2, nq, tq, tq), F32)],
        compiler_params=_cparams("parallel", "arbitrary"),
    )(qa, ka, v, do, lse4, *dep_ops)


def _forget_bwd(dfc, fl, bf, seq):
    t = fl.shape[0]
    cb = min(256, seq)
    nb = seq // cb

    def body(dfc_ref, fl_ref, bf_ref, dfl_ref, db_ref):
        b = pl.program_id(0)
        ri = lax.broadcasted_iota(jnp.int32, (cb, cb), 0)
        ci = lax.broadcasted_iota(jnp.int32, (cb, cb), 1)
        tri = (ci >= ri).astype(BF16)
        carry = jnp.zeros((1, LANES), F32)
        dbs = jnp.zeros((1, LANES), F32)
        for blk in reversed(range(nb)):
            rs = slice(blk * cb, (blk + 1) * cb)
            dlf = _tri_dot(tri, -dfc_ref[rs, :]) + carry
            carry = dlf[0:1, :]
            dfl = dlf * _sigmoid(-(fl_ref[rs, :] + bf_ref[...]))
            dfl_ref[rs, :] = dfl.astype(BF16)
            dbs = dbs + jnp.sum(dfl, axis=0, keepdims=True)

        @pl.when(b == 0)
        def _():
            db_ref[...] = jnp.zeros_like(db_ref)

        db_ref[...] += dbs

    return pl.pallas_call(
        body, name="forget_bwd",
        out_shape=(jax.ShapeDtypeStruct((t, LANES), BF16), jax.ShapeDtypeStruct((1, LANES), F32)),
        grid=(t // seq,),
        in_specs=[pl.BlockSpec((seq, LANES), lambda b: (b, 0)), pl.BlockSpec((seq, LANES), lambda b: (b, 0)),
                  pl.BlockSpec((1, LANES), lambda b: (0, 0))],
        out_specs=(pl.BlockSpec((seq, LANES), lambda b: (b, 0)), pl.BlockSpec((1, LANES), lambda b: (0, 0))),
        compiler_params=_cparams("arbitrary"),
    )(dfc, fl, bf)


def _pool_bwd(dps, p, mix, scale, seq):
    t = dps.shape[0]

    def body(dps_ref, p_ref, mix_ref, sc_ref, du_ref, dmix_ref, dsc_ref):
        b = pl.program_id(0)

        @pl.when(b == 0)
        def _():
            dmix_ref[...] = jnp.zeros_like(dmix_ref)
            dsc_ref[...] = jnp.zeros_like(dsc_ref)

        tpos = lax.broadcasted_iota(jnp.int32, (seq, POOL_GROUP_DIM), 0)
        for g in range(POOL_GROUPS):
            sl = slice(g * POOL_GROUP_DIM, (g + 1) * POOL_GROUP_DIM)
            pb = p_ref[:, sl]
            dpsg = dps_ref[:, sl]
            pm = _dot(pb, mix_ref[g])
            dsc_ref[:, sl] += jnp.sum(dpsg * pm, axis=0, keepdims=True)
            dpm = (dpsg * sc_ref[:, sl]).astype(BF16)
            dmix_ref[g] += _dot_tn(pb, dpm)
            dp = _dot_nt(dpm, mix_ref[g])
            cnt = jnp.minimum(tpos + 1, POOL_WINDOWS[g]).astype(F32)
            s = dp / cnt
            for lvl in range(g + 1):
                d = 2 ** lvl
                s = s + jnp.where(tpos < seq - d, pltpu.roll(s, seq - d, 0), 0.0)
            du_ref[:, sl] = (s - dp).astype(BF16)

    return pl.pallas_call(
        body, name="pool_bwd",
        out_shape=(jax.ShapeDtypeStruct((t, POOL_WIDTH), BF16),
                   jax.ShapeDtypeStruct((POOL_GROUPS, POOL_GROUP_DIM, POOL_GROUP_DIM), F32),
                   jax.ShapeDtypeStruct((1, POOL_WIDTH), F32)),
        grid=(t // seq,),
        in_specs=[pl.BlockSpec((seq, POOL_WIDTH), lambda b: (b, 0)), pl.BlockSpec((seq, POOL_WIDTH), lambda b: (b, 0)),
                  pl.BlockSpec((POOL_GROUPS, POOL_GROUP_DIM, POOL_GROUP_DIM), lambda b: (0, 0, 0)),
                  pl.BlockSpec((1, POOL_WIDTH), lambda b: (0, 0))],
        out_specs=(pl.BlockSpec((seq, POOL_WIDTH), lambda b: (b, 0)),
                   pl.BlockSpec((POOL_GROUPS, POOL_GROUP_DIM, POOL_GROUP_DIM), lambda b: (0, 0, 0)),
                   pl.BlockSpec((1, POOL_WIDTH), lambda b: (0, 0))),
        compiler_params=_cparams("arbitrary"),
    )(dps, p, mix, scale)


def _in_bwd(du, dq, dk, dv, dg2, dfl, dx1, x, g, wu, wqkv, wg2, wft, tm):
    t, d = x.shape
    tm = min(tm, t)
    aw = ATTN_WIDTH

    def body(du_ref, dq_ref, dk_ref, dv_ref, dg2_ref, dfl_ref, dx1_ref, x_ref, g_ref, wu_ref, wqkv_ref, wg2_ref, wft_ref,
             dx_ref, dg_ref):
        i = pl.program_id(0)
        dh = _dot_nt(du_ref[...], wu_ref[...])
        dh += _dot_nt(dq_ref[...], wqkv_ref[:, 0:aw])
        dh += _dot_nt(dk_ref[...], wqkv_ref[:, aw:2 * aw])
        dh += _dot_nt(dv_ref[...], wqkv_ref[:, 2 * aw:3 * aw])
        dh += _dot_nt(dg2_ref[...], wg2_ref[...])
        dh += _dot(dfl_ref[...], wft_ref[...])
        dxn, dg = _rms_bwd(x_ref[...], g_ref[...], dh)
        dx_ref[...] = dx1_ref[...] + dxn

        @pl.when(i == 0)
        def _():
            dg_ref[...] = jnp.zeros_like(dg_ref)

        dg_ref[...] += dg

    row = lambda w: pl.BlockSpec((tm, w), lambda i: (i, 0))
    full = lambda a: pl.BlockSpec(a.shape, lambda i: (0, 0))
    return pl.pallas_call(
        body, name="in_bwd",
        out_shape=(jax.ShapeDtypeStruct((t, d), F32), jax.ShapeDtypeStruct((1, d), F32)),
        grid=(t // tm,),
        in_specs=[row(POOL_WIDTH), row(aw), row(aw), row(aw), row(2 * d), row(LANES), row(d), row(d),
                  pl.BlockSpec((1, d), lambda i: (0, 0)), full(wu), full(wqkv), full(wg2), full(wft)],
        out_specs=(row(d), pl.BlockSpec((1, d), lambda i: (0, 0))),
        compiler_params=_cparams("arbitrary"),
    )(du, dq, dk, dv, dg2, dfl, dx1, x, g, wu, wqkv, wg2, wft)


def _position():
    return lax.axis_index("x"), lax.axis_index("y"), lax.axis_index("c")


def _remote(src, dst, send_sem, recv_sem, device):
    return pltpu.make_async_remote_copy(src_ref=src, dst_ref=dst, send_sem=send_sem, recv_sem=recv_sem,
                                        device_id=device, device_id_type=MESH)


HBM = pl.BlockSpec(memory_space=pltpu.HBM)
SEM = pl.BlockSpec(memory_space=pltpu.SEMAPHORE)
DATAFLOW = pltpu.SideEffectType.DATAFLOW_SIDE_EFFECTING


def _copies_start(name, arrays, plan, m, dep=None):
    n = len(arrays)
    arrays = [pltpu.with_memory_space_constraint(a, pltpu.HBM) for a in arrays]

    def body(*refs):
        ins, send_sem, recv_sem, token = refs[:n], refs[n], refs[n + 1], refs[2 * n + 2]
        for i, (src, dst, device, _) in enumerate(plan(ins, *_position())):
            _remote(src, dst, send_sem.at[i], recv_sem.at[i], device).start()
        token[...] = jnp.zeros_like(token)

    dep_specs, dep_ops = _dep_args(dep)
    outs = pl.pallas_call(
        _after(body, n, dep), name=name,
        out_shape=(pltpu.SemaphoreType.DMA((m,)), pltpu.SemaphoreType.DMA((m,)),
                   *[pltpu.HBM(a.shape, a.dtype) for a in arrays], jax.ShapeDtypeStruct((8, LANES), F32)),
        in_specs=[HBM] * n + dep_specs, out_specs=(SEM, SEM, *[HBM] * n, pl.BlockSpec(memory_space=pltpu.VMEM)),
        input_output_aliases={i: i + 2 for i in range(n)},
        compiler_params=pltpu.CompilerParams(has_side_effects=DATAFLOW),
    )(*arrays, *dep_ops)
    return (outs[0], outs[1]), list(outs[2:2 + n]), outs[2 + n]


def _copies_wait(name, sems, arrays, plan, after):
    n = len(arrays)
    afters = list(after) if isinstance(after, (list, tuple)) else [after]

    def body(*refs):
        ins, send_sem, recv_sem = refs[:n], refs[n], refs[n + 1]
        for i, (src, dst, device, landing) in enumerate(plan(ins, *_position())):
            _remote(src, dst, send_sem.at[i], recv_sem.at[i], device).wait_send()
            _remote(landing, landing, send_sem.at[i], recv_sem.at[i], device).wait_recv()

    outs = pl.pallas_call(
        body, name=name,
        out_shape=tuple(pltpu.HBM(a.shape, a.dtype) for a in arrays),
        in_specs=[HBM] * n + [SEM, SEM] + [ANY] * len(afters), out_specs=tuple([HBM] * n),
        input_output_aliases={i: i for i in range(n)},
        compiler_params=pltpu.CompilerParams(has_side_effects=DATAFLOW),
    )(*arrays, sems[0], sems[1], *afters)
    return list(outs)


def _tie(x, dep):
    for token in _dep_list(dep):
        x = x + token[0, 0]
    return x


def _other_chips(x, y):
    return [(1 - x, y), (x, 1 - y), (1 - x, 1 - y)]


def _gather_begin(tag, shards, token):
    n = len(shards)
    lands = [lax.empty((N_CHIPS,) + s.shape, s.dtype) for s in shards]

    def plan(refs, x, y, c):
        return [(refs[k].at[c], refs[n + k].at[2 * x + y, c], (ox, oy, c), refs[n + k].at[2 * ox + oy, c])
                for k in range(n) for ox, oy in _other_chips(x, y)]

    sems, thru, token = _copies_start(f"gather_{tag}_ici_start", list(shards) + lands, plan, 3 * n, dep=token)
    return dict(tag=tag, n=n, plan=plan, sems=sems, arrays=thru, token=token)


def _gather_forward(st, after):
    n, tag = st["n"], st["tag"]
    thru = _copies_wait(f"gather_{tag}_ici_wait", st["sems"], st["arrays"], st["plan"], after)

    def plan(refs, x, y, c):
        return [(refs[k].at[2 * ox + oy, c], refs[k].at[2 * ox + oy, c], (x, y, 1 - c), refs[k].at[2 * ox + oy, 1 - c])
                for k in range(n) for ox, oy in _other_chips(x, y)]

    sems, lands, token = _copies_start(f"gather_{tag}_fwd_start", thru[n:], plan, 3 * n)
    return dict(tag=tag, n=n, plan=plan, sems=sems, arrays=lands, token=token, shards=thru[:n])


def _gather_end(st, after):
    lands = _copies_wait(f"gather_{st['tag']}_fwd_wait", st["sems"], st["arrays"], st["plan"], after)
    me = 2 * lax.axis_index("x") + lax.axis_index("y")
    return [lax.dynamic_update_index_in_dim(g, s, me, 0) for g, s in zip(lands, st["shards"])]


def _add_keep_give(name, pos, a, a_keep, a_give, b, b_keep, b_give, steps):
    r, c = a.shape[-2:]

    def spec(arr, fn):
        lead = arr.ndim - 2
        return pl.BlockSpec((None,) * lead + (r, c), lambda i, p: tuple(fn(i, p)) + (0, 0))

    out_spec = pl.BlockSpec((None, r, c), lambda i, p: (i, 0, 0))

    def body(p_ref, ak_ref, bk_ref, ag_ref, bg_ref, keep_ref, give_ref):
        keep_ref[...] = ak_ref[...] + bk_ref[...].astype(F32)
        give_ref[...] = (ag_ref[...] + bg_ref[...].astype(F32)).astype(BF16)

    return pl.pallas_call(
        body, name=name,
        out_shape=(jax.ShapeDtypeStruct((steps, r, c), F32), jax.ShapeDtypeStruct((steps, r, c), BF16)),
        grid_spec=pltpu.PrefetchScalarGridSpec(
            num_scalar_prefetch=1, grid=(steps,),
            in_specs=[spec(a, a_keep), spec(b, b_keep), spec(a, a_give), spec(b, b_give)],
            out_specs=(out_spec, out_spec)),
        compiler_params=_cparams("parallel"),
    )(pos, a, b, a, b)


def _add_last(name, a, b):
    _, r, c = a.shape
    blk = pl.BlockSpec((None, r, c), lambda i: (0, 0, 0))

    def body(a_ref, b_ref, o_ref):
        o_ref[...] = a_ref[...] + b_ref[...].astype(F32)

    return pl.pallas_call(
        body, name=name, out_shape=jax.ShapeDtypeStruct((r, c), F32), grid=(1,), in_specs=[blk, blk],
        out_specs=pl.BlockSpec((r, c), lambda i: (0, 0)), compiler_params=_cparams("arbitrary"),
    )(a, b)


def _exchange_begin(tag, stage, gives, lands, peer_fn, extra):
    n = len(gives)

    def plan(refs, x, y, c):
        return [(refs[k], refs[n + k], peer_fn(x, y, c), refs[n + k]) for k in range(n)]

    sems, thru, token = _copies_start(f"rs{tag}_{stage}_start", gives + lands, plan, n)
    return dict(extra, tag=tag, n=n, stage=stage, plan=plan, sems=sems, arrays=thru, token=token)


def _reduce_begin(tag, grads):
    n = len(grads)
    lands = [lax.empty((N_CHIPS,) + g.shape[2:], F32) for g in grads]

    def plan(refs, x, y, c):
        return [(refs[k].at[j, 1 - c], refs[n + k].at[j], (x, y, 1 - c), refs[n + k].at[j])
                for k in range(n) for j in range(N_CHIPS)]

    sems, thru, token = _copies_start(f"rs{tag}_c_start", list(grads) + lands, plan, N_CHIPS * n)
    return dict(tag=tag, n=n, stage="c", plan=plan, sems=sems, arrays=thru, token=token)


def _reduce_advance(st, after):
    tag, n, stage = st["tag"], st["n"], st["stage"]
    thru = _copies_wait(f"rs{tag}_{stage}_wait", st["sems"], st["arrays"], st["plan"], after)
    first, recv = thru[:n], thru[n:]
    x, y, c = _position()
    if stage == "c":
        pos = jnp.stack([c, x]).astype(jnp.int32)
        sums = [_add_keep_give(
            f"rs{tag}_c_add{k}", pos,
            first[k], lambda i, p: (2 * p[1] + i, p[0]), lambda i, p: (2 * (1 - p[1]) + i, p[0]),
            recv[k], lambda i, p: (2 * p[1] + i,), lambda i, p: (2 * (1 - p[1]) + i,), 2) for k in range(n)]
        lands = [lax.empty(s[1].shape, BF16) for s in sums]
        return _exchange_begin(tag, "x", [s[1] for s in sums], lands, lambda x, y, c: (1 - x, y, c),
                               dict(keep=[s[0] for s in sums]))
    if stage == "x":
        pos = jnp.stack([y]).astype(jnp.int32)
        sums = [_add_keep_give(
            f"rs{tag}_x_add{k}", pos,
            st["keep"][k], lambda i, p: (p[0],), lambda i, p: (1 - p[0],),
            recv[k], lambda i, p: (p[0],), lambda i, p: (1 - p[0],), 1) for k in range(n)]
        lands = [lax.empty(s[1].shape, BF16) for s in sums]
        return _exchange_begin(tag, "y", [s[1] for s in sums], lands, lambda x, y, c: (x, 1 - y, c),
                               dict(keep=[s[0] for s in sums]))
    if stage == "y":
        mine = [_add_last(f"rs{tag}_y_add{k}", st["keep"][k], recv[k]) for k in range(n)]
        lands = [lax.empty(m.shape, F32) for m in mine]
        return _exchange_begin(tag, "swap", mine, lands, lambda x, y, c: (x, y, 1 - c), {})
    return dict(done=list(zip(first, recv)), token=None)


def _all_reduce_small(v):
    r = v.shape[0]

    def body(v_ref, out_ref, buf, send_sems, recv_sems, local_sem):
        x, y, c = _position()
        me, sibling = (x, y, c), (x, y, 1 - c)
        chips = [(1 - x, y), (x, 1 - y), (1 - x, 1 - y)]

        def rows(px, py, pc):
            return buf.at[pl.ds((4 * px + 2 * py + pc) * r, r), :]

        def copy(k, block, to, src=None):
            return _remote(rows(*block) if src is None else src, rows(*block), send_sems.at[k], recv_sems.at[k], to)

        mine = pltpu.make_async_copy(v_ref, rows(*me), local_sem)
        mine.start()
        first = [copy(0, me, sibling, src=v_ref)]
        first += [copy(1 + j, me, (*chip, c), src=v_ref) for j, chip in enumerate(chips)]
        for cp in first:
            cp.start()
        passed = [copy(4 + j, (*chip, c), sibling) for j, chip in enumerate(chips)]
        for j, chip in enumerate(chips):
            copy(1 + j, (*chip, c), me).wait_recv()
            passed[j].start()
        copy(0, sibling, me).wait_recv()
        for j, chip in enumerate(chips):
            copy(4 + j, (*chip, 1 - c), me).wait_recv()
        for cp in first + passed:
            cp.wait_send()
        mine.wait()
        acc = buf[0:r, :]
        for dev in range(1, N_DEV):
            acc = acc + buf[dev * r:(dev + 1) * r, :]
        out_ref[...] = acc

    return pl.pallas_call(
        body, name="all_reduce_small",
        out_shape=jax.ShapeDtypeStruct(v.shape, F32),
        in_specs=[pl.BlockSpec(memory_space=pltpu.VMEM)],
        out_specs=pl.BlockSpec(memory_space=pltpu.VMEM),
        scratch_shapes=[pltpu.VMEM((N_DEV * r, LANES), F32), pltpu.SemaphoreType.DMA((7,)),
                        pltpu.SemaphoreType.DMA((7,)), pltpu.SemaphoreType.DMA],
        compiler_params=pltpu.CompilerParams(has_side_effects=True, vmem_limit_bytes=VMEM_LIMIT_V7X),
    )(v)


def _adamw_update(w, gg, m, v):
    mn = ADAM_B1 * m + (1.0 - ADAM_B1) * gg
    vn = ADAM_B2 * v + (1.0 - ADAM_B2) * (gg * gg)
    m_hat = mn / (1.0 - ADAM_B1 ** ADAM_STEP)
    v_hat = vn / (1.0 - ADAM_B2 ** ADAM_STEP)
    return -ADAM_LR * (m_hat / (jnp.sqrt(v_hat) + ADAM_EPS) + ADAM_WD * w), mn, vn


def _adamw(name, w, g, m, v):
    def body(w_ref, g_ref, m_ref, v_ref, d_ref, mo_ref, vo_ref):
        d_ref[...], mo_ref[...], vo_ref[...] = _adamw_update(w_ref[...], g_ref[...], m_ref[...], v_ref[...])

    blk = pl.BlockSpec(w.shape, lambda i: (0, 0))
    return pl.pallas_call(
        body, name=name, out_shape=(jax.ShapeDtypeStruct(w.shape, F32),) * 3, grid=(1,),
        in_specs=[blk] * 4, out_specs=(blk,) * 3, compiler_params=_cparams("arbitrary"),
    )(w, g, m, v)


def _adamw_rows(name, w, g, m, v, tr):
    r, _, c = w.shape
    assert r % tr == 0

    def body(w_ref, g_ref, m_ref, v_ref, d_ref, mo_ref, vo_ref):
        d_ref[...], mo_ref[...], vo_ref[...] = _adamw_update(w_ref[...], g_ref[...], m_ref[...], v_ref[...])

    blk = pl.BlockSpec((tr, 1, c), lambda i: (i, 0, 0))
    return pl.pallas_call(
        body, name=name, out_shape=(jax.ShapeDtypeStruct(w.shape, F32),) * 3, grid=(r // tr,),
        in_specs=[blk] * 4, out_specs=(blk,) * 3, compiler_params=_cparams("parallel"),
    )(w, g, m, v)


def _adamw_halves(name, pos_c, w, g_mine, g_other, m, v, tr, dep=None):
    r, c = w.shape
    rh = r // 2
    tr = tr if rh % tr == 0 else rh
    nt = rh // tr

    def body(p_ref, w_ref, gm_ref, go_ref, m_ref, v_ref, g_ref, d_ref, mo_ref, vo_ref):
        gg = jnp.where(pl.program_id(0) == p_ref[0], gm_ref[...], go_ref[...])
        g_ref[...] = gg
        d_ref[...], mo_ref[...], vo_ref[...] = _adamw_update(w_ref[...], gg, m_ref[...], v_ref[...])

    full = pl.BlockSpec((tr, c), lambda h, i, p: (h * nt + i, 0))
    half = pl.BlockSpec((tr, c), lambda h, i, p: (i, 0))
    dep_specs, dep_ops = _dep_args(dep)
    return pl.pallas_call(
        _after(body, 6, dep), name=name, out_shape=(jax.ShapeDtypeStruct((r, c), F32),) * 4,
        grid_spec=pltpu.PrefetchScalarGridSpec(
            num_scalar_prefetch=1, grid=(2, nt),
            in_specs=[full, half, half, full, full] + dep_specs, out_specs=(full,) * 4),
        compiler_params=_cparams("parallel", "parallel"),
    )(pos_c, w, g_mine, g_other, m, v, *dep_ops)


def _col_sharded_to_comm(g):
    k, n = g.shape
    return g.reshape(2, k // 2, N_CHIPS, n // N_CHIPS).transpose(2, 0, 1, 3)


def _row_sharded_to_comm(g):
    r, c = g.shape
    return g.reshape(N_CHIPS, 2, r // (2 * N_CHIPS), c)


def _col_sharded_full(g):
    _, _, rh, c = g.shape
    return g.reshape(N_CHIPS, 2 * rh, c).transpose(1, 0, 2).reshape(2 * rh, N_CHIPS * c)


def _row_sharded_full(g):
    _, _, rh, c = g.shape
    return g.reshape(N_CHIPS * 2 * rh, c)


def _pack_small(g1, bfv, mix, scale, g2n, gf, extra=None):
    row8 = jnp.pad(bfv.reshape(1, N_HEADS), ((0, 0), (0, LANES - N_HEADS)))
    if extra is not None:
        row8 = row8 + jnp.pad(extra[:, :1], ((0, 0), (N_HEADS, LANES - N_HEADS - 1)))
    return jnp.concatenate([
        g1.reshape(8, LANES), jnp.pad(row8, ((0, 7), (0, 0))), mix.reshape(512, LANES),
        jnp.pad(scale.reshape(4, LANES), ((0, 4), (0, 0))), g2n.reshape(8, LANES), gf.reshape(8, LANES)], axis=0)


def _unpack_small(s, like):
    g1, bfv, mix, scale, g2n, gf = like
    return (s[0:8].reshape(g1.shape), s[8, :N_HEADS].reshape(bfv.shape), s[16:528].reshape(mix.shape),
            s[528:532].reshape(scale.shape), s[536:544].reshape(g2n.shape), s[544:552].reshape(gf.shape))


class _MeshLinks:
    def __init__(self, shards_in, shards_rest):
        self.gin = _gather_begin("in", shards_in, None)
        self.grest = _gather_begin("rest", shards_rest, self.gin["token"])
        self.tokens = {"gather": self.grest["token"]}
        self.groups = {}

    @property
    def token(self):
        return list(self.tokens.values())

    def tie(self, x):
        return _tie(x, self.token)

    def weights_in(self, after):
        st = _gather_forward(self.gin, after)
        (g,) = _gather_end(st, st["token"])
        return _col_sharded_full(g)

    def rest_forward(self, after):
        self.grest = _gather_forward(self.grest, after)
        self.tokens["gather"] = self.grest["token"]

    def weights_rest(self, after):
        g = _gather_end(self.grest, after)
        del self.tokens["gather"]
        return [_col_sharded_full(g[0]), _col_sharded_full(g[1])] + [_row_sharded_full(a) for a in g[2:]]

    def reduce_begin(self, tag, grads):
        self.groups[tag] = _reduce_begin(tag, grads)
        self.tokens[tag] = self.groups[tag]["token"]

    def advance(self, after):
        for tag, st in self.groups.items():
            if "done" not in st:
                self.groups[tag] = _reduce_advance(st, after)
                if self.groups[tag]["token"] is None:
                    del self.tokens[tag]
                else:
                    self.tokens[tag] = self.groups[tag]["token"]

    def reduced(self, tag):
        return self.groups[tag]["done"]


class _NoLinks:
    token = None

    def __init__(self, w_in, rest):
        self.w_in, self.rest, self.grads = w_in, rest, {}

    def tie(self, x):
        return x

    def weights_in(self, after):
        return self.w_in

    def rest_forward(self, after):
        pass

    def weights_rest(self, after):
        return self.rest

    def reduce_begin(self, tag, grads):
        self.grads[tag] = grads

    def advance(self, after):
        pass


def _local_step(links, x, target, seq, norm1_g, b_forget, pool_mix, pool_scale, norm2_g, norm_f_g):
    t, d = x.shape
    tq = min(256, seq)
    aw = ATTN_WIDTH
    o_q, o_f, o_g = POOL_WIDTH, POOL_WIDTH + 3 * aw, POOL_WIDTH + 3 * aw + N_HEADS
    bf = jnp.pad(b_forget, ((0, 0), (0, LANES - N_HEADS)))
    mixb = pool_mix.astype(BF16)

    h = _norm_fwd("norm1_fwd", x, links.tie(norm1_g), 512)
    w_in = links.weights_in(h)
    wu = w_in[:, :o_q]
    wqkv = w_in[:, o_q:o_f]
    wf = jnp.pad(w_in[:, o_f:o_g], ((0, 0), (0, LANES - N_HEADS)))
    wg2 = w_in[:, o_g:]
    wft = wf.T
    u = _matmul("mm_u", h, wu, "nn", F32, 1024, 512, d)
    g2 = _matmul("mm_gates", h, wg2, "nn", BF16, 1024, 512, d)
    fl, fcum = _forget_fwd(h, wf, bf, seq)
    qa, ka, v = _attn_prep(h, _head_blocks(wqkv[:, :aw]), _head_blocks(wqkv[:, aw:2 * aw]), wqkv[:, 2 * aw:], fcum, 512)
    p, ps = _pool_fwd(u, mixb, pool_scale, seq)
    links.rest_forward([ps, qa, g2])
    o, lse = _attn_fwd(qa, ka, v, seq, tq, dep=links.token)
    w_pool_out, w_attn_out, w_out, w_ffn_gate, w_ffn_up, w_ffn_down = links.weights_rest(o)
    merged, x1 = _merge_fwd(x, ps, o, g2, w_pool_out, w_attn_out, w_out, 256)
    h2, gt, up, act, x2 = _ffn_fwd(x1, norm2_g, w_ffn_gate, w_ffn_up, w_ffn_down, 1024, 256)
    loss, dx2, d_gf = _final_fwd_bwd(x2, target, norm_f_g, 512)

    dgt, dup, dx1, d_g2n = _ffn_bwd(dx2, x1, norm2_g, gt, up, w_ffn_gate, w_ffn_up, w_ffn_down, 1024, 256)
    d_wd = _matmul("dw_down", act, dx2, "tn", F32, 1408, 1024, 1024)
    d_wg = _matmul("dw_gate", dgt, h2, "tn", F32, 1408, 1024, 1024)
    d_wu = _matmul("dw_up", dup, h2, "tn", F32, 1408, 1024, 1024)
    links.reduce_begin("a", [_row_sharded_to_comm(g) for g in (d_wg, d_wu, d_wd)])
    dpy, day, dg2, dps, da = _merge_bwd(dx1, ps, o, g2, w_pool_out, w_attn_out, w_out, 256, dep=links.token)
    links.advance(dps)
    d_wout = _matmul("dw_out", merged, dx1, "tn", F32, 1024, 1024, 1024)
    d_wpo = _matmul("dw_pool_out", ps, dpy, "tn", F32, 512, 1024, 1024)
    d_wao = _matmul("dw_attn_out", o, day, "tn", F32, 512, 1024, 1024)
    links.reduce_begin("m", [_col_sharded_to_comm(d_wpo), _col_sharded_to_comm(d_wao), _row_sharded_to_comm(d_wout)])
    dq, dk, dv, dfr = _attn_bwd(qa, ka, v, da, lse, seq, tq, dep=links.token)
    links.advance(dq)
    dfc = jnp.pad(dfr.reshape(N_HEADS, t).T, ((0, 0), (0, LANES - N_HEADS)))
    dfl, d_bf = _forget_bwd(dfc, fl, bf, seq)
    du, d_mix, d_scale = _pool_bwd(dps, p, mixb, links.tie(pool_scale), seq)
    d_wu_in = _matmul("dw_in_u", h, du, "tn", F32, 1024, 512, 1024)
    d_wq = _matmul("dw_in_q", h, dq, "tn", F32, 1024, 512, 1024)
    d_wk = _matmul("dw_in_k", h, dk, "tn", F32, 1024, 512, 1024)
    d_wv = _matmul("dw_in_v", h, dv, "tn", F32, 1024, 512, 1024)
    links.advance([d_wu_in, d_wq, d_wk, d_wv])
    d_wf = _matmul("dw_in_f", h, dfl, "tn", F32, 1024, LANES, 512)
    d_wg2 = _matmul("dw_in_gates", h, dg2, "tn", F32, 1024, 1024, 1024, dep=links.token)
    d_win = jnp.concatenate([d_wu_in, d_wq, d_wk, d_wv, d_wf[:, :N_HEADS], d_wg2], axis=1)
    comm_b = [_col_sharded_to_comm(d_win)]
    links.advance(comm_b)
    links.reduce_begin("b", comm_b)
    dx, d_g1 = _in_bwd(du, dq, dk, dv, dg2, dfl, dx1, x, links.tie(norm1_g), wu, wqkv, wg2, wft, 256)
    links.advance(dx)
    small = (d_g1, d_bf[:, :N_HEADS], d_mix, d_scale, d_g2n, d_gf)
    return loss, dx, small


def kernel(x, norm1_g, w_in, b_forget, pool_mix, pool_scale, w_pool_out, w_attn_out, w_out, norm2_g, w_ffn_gate, w_ffn_up, w_ffn_down, norm_f_g, loss_target, m_norm1_g, m_w_in, m_b_forget, m_pool_mix, m_pool_scale, m_w_pool_out, m_w_attn_out, m_w_out, m_norm2_g, m_w_ffn_gate, m_w_ffn_up, m_w_ffn_down, m_norm_f_g, v_norm1_g, v_w_in, v_b_forget, v_pool_mix, v_pool_scale, v_w_pool_out, v_w_attn_out, v_w_out, v_norm2_g, v_w_ffn_gate, v_w_ffn_up, v_w_ffn_down, v_norm_f_g):
    nb, seq, d = x.shape
    group_a = ((w_ffn_gate, m_w_ffn_gate, v_w_ffn_gate, True, 9), (w_ffn_up, m_w_ffn_up, v_w_ffn_up, True, 10),
               (w_ffn_down, m_w_ffn_down, v_w_ffn_down, False, 11))
    group_m = ((w_pool_out, m_w_pool_out, v_w_pool_out, False, 5), (w_attn_out, m_w_attn_out, v_w_attn_out, False, 6),
               (w_out, m_w_out, v_w_out, False, 7))
    group_b = ((w_in, m_w_in, v_w_in, False, 1),)
    small_w = (norm1_g, b_forget, pool_mix, pool_scale, norm2_g, norm_f_g)
    small_m = (m_norm1_g, m_b_forget, m_pool_mix, m_pool_scale, m_norm2_g, m_norm_f_g)
    small_v = (v_norm1_g, v_b_forget, v_pool_mix, v_pool_scale, v_norm2_g, v_norm_f_g)
    small_pos = (0, 2, 3, 4, 8, 12)
    view = lambda a, tr: a[0].T if tr else a[0]
    unview = lambda a, tr, like: (a.T if tr else a).reshape(like.shape)

    def shard(w, tr):
        lw = view(w, tr).astype(BF16)
        return lw.reshape(2, lw.shape[0] // 2, lw.shape[1])

    links = _MeshLinks([shard(w_in, False)],
                       [shard(w_pool_out, False), shard(w_attn_out, False), shard(w_out, False),
                        shard(w_ffn_gate, True), shard(w_ffn_up, True), shard(w_ffn_down, False)])
    loss, dx, small_g = _local_step(
        links, x.reshape(nb * seq, d), loss_target.reshape(nb * seq, d), seq,
        norm1_g, b_forget, pool_mix[0], pool_scale, norm2_g, norm_f_g.reshape(1, d))

    grads, deltas, new_m, new_v = [None] * 13, [None] * 13, [None] * 13, [None] * 13
    pos_c = jnp.stack([lax.axis_index("c")]).astype(jnp.int32)

    def update(tag, group, dep):
        last = []
        for k, ((w, m, v, tr, pos), (mine, other)) in enumerate(zip(group, links.reduced(tag))):
            outs = _adamw_halves(f"adamw_{tag}{k}", pos_c, view(w, tr), mine, other, view(m, tr), view(v, tr), 256,
                                 dep=dep)
            grads[pos], deltas[pos], new_m[pos], new_v[pos] = (unview(a, tr, w) for a in outs)
            last.append(outs[1])
        return last

    last = update("a", group_a, links.token) + update("m", group_m, links.token)
    links.advance(last)
    small_sum = _all_reduce_small(links.tie(_pack_small(*small_g, extra=loss)))
    loss_out = small_sum[8, N_HEADS]
    dl, mn, vn = _adamw("adamw_small", _pack_small(*small_w), small_sum * _small_mask(), _pack_small(*small_m),
                        _pack_small(*small_v))
    for pos, g, a, b, e in zip(small_pos, _unpack_small(small_sum, small_w), _unpack_small(dl, small_w),
                               _unpack_small(mn, small_w), _unpack_small(vn, small_w)):
        grads[pos], deltas[pos], new_m[pos], new_v[pos] = g, a, b, e
    links.advance(dl)
    links.advance(links.token)
    (mine, other), = links.reduced("b")
    c = lax.axis_index("c")
    g_in = jnp.where(c == 0, jnp.concatenate([mine, other]), jnp.concatenate([other, mine]))
    cm = lambda a: jnp.transpose(a, (2, 0, 1))
    outs = _adamw_rows("adamw_b0", cm(w_in), cm(g_in[None]), cm(m_w_in), cm(v_w_in), 114)
    grads[1], deltas[1], new_m[1], new_v[1] = (jnp.transpose(a, (1, 2, 0)) for a in (cm(g_in[None]),) + tuple(outs))

    return (loss_out, dx.reshape(nb, seq, d), *grads, *deltas, *new_m, *new_v)


def _small_mask():
    rows = lax.broadcasted_iota(jnp.int32, (552, LANES), 0)
    lanes = lax.broadcasted_iota(jnp.int32, (552, LANES), 1)
    return jnp.where(jnp.logical_and(rows == 8, lanes == N_HEADS), 0.0, 1.0).astype(F32)
```

```python
import functools

import jax
import jax.numpy as jnp
from jax import lax
from jax.experimental import pallas as pl
from jax.experimental.pallas import tpu as pltpu

F32 = jnp.float32
BF16 = jnp.bfloat16

D_MODEL = 1024
POOL_WINDOWS = (2, 4, 8, 16)
POOL_GROUPS = 4
POOL_GROUP_DIM = 128
POOL_WIDTH = 512
HEAD_DIM = 64
N_HEADS = 8
ATTN_WIDTH = 512
D_FF = 2816
RMS_EPS = 1e-6
ATTN_SCALE = HEAD_DIM ** -0.5
NEG_BIG = -1e30

ADAM_LR = 0.001
ADAM_B1 = 0.9
ADAM_B2 = 0.999
ADAM_EPS = 1e-08
ADAM_WD = 0.01
ADAM_STEP = 10

LANES = 128
N_CHIPS = 4
N_DEV = 8
VMEM_LIMIT_V7X = 52 * 1024 * 1024
MESH = pl.DeviceIdType.MESH
ANY = pl.BlockSpec(memory_space=pl.ANY)


def _cparams(*sem):
    return pltpu.CompilerParams(dimension_semantics=sem if sem else None, vmem_limit_bytes=VMEM_LIMIT_V7X)


def _dep_list(dep):
    return [] if dep is None else (list(dep) if isinstance(dep, (list, tuple)) else [dep])


def _after(body, n_in, dep):
    k = len(_dep_list(dep))
    if k == 0:
        return body

    def wrapped(*refs):
        body(*refs[:n_in], *refs[n_in + k:])

    return wrapped


def _dep_args(dep):
    deps = _dep_list(dep)
    return [ANY] * len(deps), deps


def _dot(a, b):
    return lax.dot_general(a, b, (((1,), (0,)), ((), ())), preferred_element_type=F32)


def _dot_nt(a, b):
    return lax.dot_general(a, b, (((1,), (1,)), ((), ())), preferred_element_type=F32)


def _dot_tn(a, b):
    return lax.dot_general(a, b, (((0,), (0,)), ((), ())), preferred_element_type=F32)


def _sigmoid(x):
    return jax.nn.sigmoid(x)


def _rms_fwd(x, g):
    r = lax.rsqrt(jnp.mean(x * x, axis=-1, keepdims=True) + RMS_EPS)
    return (x * r) * g


def _rms_bwd(x, g, dy):
    r = lax.rsqrt(jnp.mean(x * x, axis=-1, keepdims=True) + RMS_EPS)
    xh = x * r
    dg = jnp.sum(dy * xh, axis=0, keepdims=True)
    dxh = dy * g
    dx = r * (dxh - xh * jnp.mean(dxh * xh, axis=-1, keepdims=True))
    return dx, dg


def _matmul(name, a, b, mode, out_dtype, tm, tn, tk, dep=None):
    if mode == "nn":
        (m, k), (_, n) = a.shape, b.shape
    elif mode == "nt":
        (m, k), (n, _) = a.shape, b.shape
    else:
        (k, m), (_, n) = a.shape, b.shape
    tm, tn, tk = min(tm, m), min(tn, n), min(tk, k)
    assert m % tm == 0 and n % tn == 0 and k % tk == 0, (name, m, n, k, tm, tn, tk)
    nk = k // tk
    if mode == "tn":
        a_spec = pl.BlockSpec((tk, tm), lambda i, j, kk: (kk, i))
    else:
        a_spec = pl.BlockSpec((tm, tk), lambda i, j, kk: (i, kk))
    if mode == "nt":
        b_spec = pl.BlockSpec((tn, tk), lambda i, j, kk: (j, kk))
    else:
        b_spec = pl.BlockSpec((tk, tn), lambda i, j, kk: (kk, j))
    dot = {"nn": _dot, "nt": _dot_nt, "tn": _dot_tn}[mode]
    use_scratch = nk > 1 and out_dtype != F32

    def body(a_ref, b_ref, o_ref, *scratch):
        prod = dot(a_ref[...].astype(BF16), b_ref[...].astype(BF16))
        if nk == 1:
            o_ref[...] = prod.astype(out_dtype)
            return
        acc = scratch[0] if use_scratch else o_ref
        kk = pl.program_id(2)

        @pl.when(kk == 0)
        def _():
            acc[...] = prod

        @pl.when(kk > 0)
        def _():
            acc[...] += prod

        if use_scratch:
            @pl.when(kk == nk - 1)
            def _():
                o_ref[...] = acc[...].astype(out_dtype)

    dep_specs, dep_ops = _dep_args(dep)
    return pl.pallas_call(
        _after(body, 2, dep),
        name=name,
        out_shape=jax.ShapeDtypeStruct((m, n), out_dtype),
        grid=(m // tm, n // tn, nk),
        in_specs=[a_spec, b_spec] + dep_specs,
        out_specs=pl.BlockSpec((tm, tn), lambda i, j, kk: (i, j)),
        scratch_shapes=[pltpu.VMEM((tm, tn), F32)] if use_scratch else [],
        compiler_params=_cparams("parallel", "parallel", "arbitrary"),
    )(a, b, *dep_ops)


def _norm_fwd(name, x, g, tm):
    t, d = x.shape
    tm = min(tm, t)

    def body(x_ref, g_ref, h_ref):
        h_ref[...] = _rms_fwd(x_ref[...], g_ref[...]).astype(BF16)

    return pl.pallas_call(
        body, name=name, out_shape=jax.ShapeDtypeStruct((t, d), BF16), grid=(t // tm,),
        in_specs=[pl.BlockSpec((tm, d), lambda i: (i, 0)), pl.BlockSpec((1, d), lambda i: (0, 0))],
        out_specs=pl.BlockSpec((tm, d), lambda i: (i, 0)),
        compiler_params=_cparams("parallel"),
    )(x, g)


def _split3(x):
    hi = x.astype(BF16)
    r1 = x - hi.astype(F32)
    mid = r1.astype(BF16)
    lo = (r1 - mid.astype(F32)).astype(BF16)
    return hi, mid, lo


def _tri_dot(tri, x):
    hi, mid, lo = _split3(x)
    return _dot(tri, hi) + _dot(tri, mid) + _dot(tri, lo)


def _forget_fwd(h, wf, bf, seq):
    t, d = h.shape
    cb = min(256, seq)

    def body(h_ref, wf_ref, bf_ref, fl_ref, fc_ref):
        fl = _dot(h_ref[...], wf_ref[...])
        fl_ref[...] = fl
        xx = fl + bf_ref[...]
        lf = jnp.minimum(xx, 0.0) - jnp.log(1.0 + jnp.exp(-jnp.abs(xx)))
        ri = lax.broadcasted_iota(jnp.int32, (cb, cb), 0)
        ci = lax.broadcasted_iota(jnp.int32, (cb, cb), 1)
        tri = (ri >= ci).astype(BF16)
        carry = jnp.zeros((1, LANES), F32)
        for blk in range(seq // cb):
            cs = _tri_dot(tri, lf[blk * cb:(blk + 1) * cb]) + carry
            fc_ref[blk * cb:(blk + 1) * cb, :] = cs
            carry = cs[cb - 1:cb, :]

    return pl.pallas_call(
        body, name="forget_fwd",
        out_shape=(jax.ShapeDtypeStruct((t, LANES), F32), jax.ShapeDtypeStruct((t, LANES), F32)),
        grid=(t // seq,),
        in_specs=[pl.BlockSpec((seq, d), lambda b: (b, 0)), pl.BlockSpec((d, LANES), lambda b: (0, 0)),
                  pl.BlockSpec((1, LANES), lambda b: (0, 0))],
        out_specs=(pl.BlockSpec((seq, LANES), lambda b: (b, 0)), pl.BlockSpec((seq, LANES), lambda b: (b, 0))),
        compiler_params=_cparams("parallel"),
    )(h, wf, bf)


def _pool_fwd(u, mix, scale, seq):
    t = u.shape[0]

    def body(u_ref, mix_ref, sc_ref, p_ref, ps_ref):
        tpos = lax.broadcasted_iota(jnp.int32, (seq, POOL_GROUP_DIM), 0)
        for g in range(POOL_GROUPS):
            sl = slice(g * POOL_GROUP_DIM, (g + 1) * POOL_GROUP_DIM)
            ug = u_ref[:, sl]
            s = ug
            for lvl in range(g + 1):
                d = 2 ** lvl
                s = s + jnp.where(tpos >= d, pltpu.roll(s, d, 0), 0.0)
            cnt = jnp.minimum(tpos + 1, POOL_WINDOWS[g]).astype(F32)
            pb = (s / cnt - ug).astype(BF16)
            p_ref[:, sl] = pb
            ps_ref[:, sl] = (_dot(pb, mix_ref[g]) * sc_ref[:, sl]).astype(BF16)

    return pl.pallas_call(
        body, name="pool_fwd",
        out_shape=(jax.ShapeDtypeStruct((t, POOL_WIDTH), BF16), jax.ShapeDtypeStruct((t, POOL_WIDTH), BF16)),
        grid=(t // seq,),
        in_specs=[pl.BlockSpec((seq, POOL_WIDTH), lambda b: (b, 0)),
                  pl.BlockSpec((POOL_GROUPS, POOL_GROUP_DIM, POOL_GROUP_DIM), lambda b: (0, 0, 0)),
                  pl.BlockSpec((1, POOL_WIDTH), lambda b: (0, 0))],
        out_specs=(pl.BlockSpec((seq, POOL_WIDTH), lambda b: (b, 0)), pl.BlockSpec((seq, POOL_WIDTH), lambda b: (b, 0))),
        compiler_params=_cparams("parallel"),
    )(u, mix, scale)


def _aug_constants():
    w = N_HEADS * LANES
    rows = jnp.arange(3 * LANES)
    piece, head = rows // LANES, rows % LANES
    cols = jnp.arange(w)
    live = (head < N_HEADS)[:, None]
    pq = (live & (cols[None, :] == (head * LANES + HEAD_DIM + piece)[:, None])).astype(BF16)
    pk = -(live & (cols[None, :] == (head * LANES + HEAD_DIM + 3 + piece)[:, None])).astype(BF16)
    lane = cols % LANES
    oq = ((lane >= HEAD_DIM + 3) & (lane < HEAD_DIM + 6)).astype(F32)[None, :]
    ok = ((lane >= HEAD_DIM) & (lane < HEAD_DIM + 3)).astype(F32)[None, :]
    return pq, pk, oq, ok


def _head_blocks(wt):
    d = wt.shape[1]
    return jnp.pad(wt.reshape(N_HEADS, HEAD_DIM, d), ((0, 0), (0, LANES - HEAD_DIM), (0, 0))).reshape(N_HEADS * LANES, d)


def _attn_prep(h, wq, wk, wv, fcum, tm):
    t, d = h.shape
    tm = min(tm, t)
    w = N_HEADS * LANES
    pq, pk, oq, ok = _aug_constants()

    def body(h_ref, wq_ref, wk_ref, wv_ref, f_ref, pq_ref, pk_ref, oq_ref, ok_ref, qa_ref, ka_ref, v_ref):
        hh = h_ref[...]
        fs = jnp.concatenate(_split3(f_ref[...]), axis=1)
        q = _dot_nt(hh, wq_ref[...]).astype(BF16).astype(F32) * ATTN_SCALE
        qa_ref[...] = (q + _dot(fs, pq_ref[...]) + oq_ref[...]).astype(BF16)
        k = _dot_nt(hh, wk_ref[...]).astype(BF16).astype(F32)
        ka_ref[...] = (k + _dot(fs, pk_ref[...]) + ok_ref[...]).astype(BF16)
        v_ref[...] = _dot_nt(hh, wv_ref[...]).astype(BF16)

    row = lambda n: pl.BlockSpec((tm, n), lambda i: (i, 0))
    full = lambda a: pl.BlockSpec(a.shape, lambda i: (0, 0))
    return pl.pallas_call(
        body, name="attn_prep",
        out_shape=(jax.ShapeDtypeStruct((t, w), BF16), jax.ShapeDtypeStruct((t, w), BF16),
                   jax.ShapeDtypeStruct((t, ATTN_WIDTH), BF16)),
        grid=(t // tm,),
        in_specs=[row(d), full(wq), full(wk), full(wv), row(LANES), full(pq), full(pk), full(oq), full(ok)],
        out_specs=(row(w), row(w), row(ATTN_WIDTH)),
        compiler_params=_cparams("parallel"),
    )(h, wq, wk, wv, fcum, pq, pk, oq, ok)


def _fold_lanes(x, op):
    out = x[:, :LANES]
    for g in range(1, x.shape[1] // LANES):
        out = op(out, x[:, g * LANES:(g + 1) * LANES])
    return out


def _attn_fwd(qa, ka, v, seq, tq, dep=None):
    t = qa.shape[0]
    nq = seq // tq
    hp_n = N_HEADS // 2
    heads = [slice(e * LANES, (e + 1) * LANES) for e in range(2)]

    def body(q_ref, k_ref, v_ref, o_ref, lse_ref, s_buf):
        i = pl.program_id(2)
        diag_ok = lax.broadcasted_iota(jnp.int32, (tq, tq), 0) >= lax.broadcasted_iota(jnp.int32, (tq, tq), 1)
        qs = [q_ref[:, hl] for hl in heads]

        def sweep1(j, mxs):
            r0 = pl.multiple_of(j * tq, tq)
            out = []
            for e, hl in enumerate(heads):
                s = _dot_nt(qs[e], k_ref[pl.ds(r0, tq), hl])
                s = jnp.where(jnp.logical_or(diag_ok, j < i), s, NEG_BIG)
                s_buf[e, j] = s
                out.append(jnp.maximum(mxs[e], _fold_lanes(s, jnp.maximum)))
            return tuple(out)

        mxs = lax.fori_loop(0, i + 1, sweep1, (jnp.full((tq, LANES), NEG_BIG, F32),) * 2)
        ms = [jnp.max(mx, axis=1, keepdims=True) for mx in mxs]

        def sweep2(j, carry):
            r0 = pl.multiple_of(j * tq, tq)
            vv = v_ref[pl.ds(r0, tq), :]
            out = []
            for e in range(2):
                p = jnp.exp(s_buf[e, j] - ms[e])
                out += [carry[2 * e] + _fold_lanes(p, jnp.add), carry[2 * e + 1] + _dot(p.astype(BF16), vv)]
            return tuple(out)

        res = lax.fori_loop(0, i + 1, sweep2, (jnp.zeros((tq, LANES), F32),) * 4)
        outs = []
        for e in range(2):
            l = jnp.sum(res[2 * e], axis=1, keepdims=True)
            outs.append(res[2 * e + 1] / l)
            lse_ref[:, e:e + 1] = ms[e] + jnp.log(l)
        lane = lax.broadcasted_iota(jnp.int32, (tq, LANES), 1)
        o_ref[...] = jnp.where(lane < HEAD_DIM, outs[0], outs[1])

    dep_specs, dep_ops = _dep_args(dep)
    return pl.pallas_call(
        _after(body, 3, dep), name="attn_fwd",
        out_shape=(jax.ShapeDtypeStruct((t, ATTN_WIDTH), F32), jax.ShapeDtypeStruct((hp_n, t, 2), F32)),
        grid=(t // seq, hp_n, nq),
        in_specs=[pl.BlockSpec((tq, 2 * LANES), lambda b, hp, i: (b * nq + i, hp)),
                  pl.BlockSpec((seq, 2 * LANES), lambda b, hp, i: (b, hp)),
                  pl.BlockSpec((seq, LANES), lambda b, hp, i: (b, hp))] + dep_specs,
        out_specs=(pl.BlockSpec((tq, LANES), lambda b, hp, i: (b * nq + i, hp)),
                   pl.BlockSpec((None, tq, 2), lambda b, hp, i: (hp, b * nq + i, 0))),
        scratch_shapes=[pltpu.VMEM((2, nq, tq, tq), F32)],
        compiler_params=_cparams("parallel", "parallel", "arbitrary"),
    )(qa, ka, v, *dep_ops)


def _merge_fwd(x, ps, o, g2, wpo, wao, wout, tm):
    t, d = x.shape
    tm = min(tm, t)

    def body(x_ref, ps_ref, o_ref, gp_ref, ga_ref, wpo_ref, wao_ref, wout_ref, mg_ref, x1_ref):
        py = _dot(ps_ref[...], wpo_ref[...])
        ay = _dot(o_ref[...].astype(BF16), wao_ref[...])
        mb = (_sigmoid(gp_ref[...].astype(F32)) * py + _sigmoid(ga_ref[...].astype(F32)) * ay).astype(BF16)
        mg_ref[...] = mb
        x1_ref[...] = x_ref[...] + _dot(mb, wout_ref[...])

    row = lambda w: pl.BlockSpec((tm, w), lambda i: (i, 0))
    full = lambda a: pl.BlockSpec(a.shape, lambda i: (0, 0))
    return pl.pallas_call(
        body, name="merge_fwd",
        out_shape=(jax.ShapeDtypeStruct((t, d), BF16), jax.ShapeDtypeStruct((t, d), F32)),
        grid=(t // tm,),
        in_specs=[row(d), row(POOL_WIDTH), row(ATTN_WIDTH), pl.BlockSpec((tm, d), lambda i: (i, 0)),
                  pl.BlockSpec((tm, d), lambda i: (i, 1)), full(wpo), full(wao), full(wout)],
        out_specs=(row(d), row(d)),
        compiler_params=_cparams("parallel"),
    )(x, ps, o, g2, g2, wpo, wao, wout)


def _ffn_fwd(x1, g, wg, wu, wd, tm, tf):
    t, d = x1.shape
    f = wg.shape[0]
    tm = min(tm, t)
    nf = f // tf

    def body(x1_ref, g_ref, wg_ref, wu_ref, wd_ref, h2_ref, gt_ref, up_ref, act_ref, x2_ref):
        j = pl.program_id(1)

        @pl.when(j == 0)
        def _():
            h2_ref[...] = _rms_fwd(x1_ref[...], g_ref[...]).astype(BF16)

        h2 = h2_ref[...]
        gt = _dot_nt(h2, wg_ref[...])
        up = _dot_nt(h2, wu_ref[...])
        sg = _sigmoid(gt)
        silu = gt * sg
        act = (silu * up).astype(BF16)
        gt_ref[...] = (up * (sg * (1.0 + gt * (1.0 - sg)))).astype(BF16)
        up_ref[...] = silu.astype(BF16)
        act_ref[...] = act
        prod = _dot(act, wd_ref[...])

        @pl.when(j == 0)
        def _():
            x2_ref[...] = prod

        @pl.when(j > 0)
        def _():
            x2_ref[...] += prod

        @pl.when(j == nf - 1)
        def _():
            x2_ref[...] += x1_ref[...]

    return pl.pallas_call(
        body, name="ffn_fwd",
        out_shape=(jax.ShapeDtypeStruct((t, d), BF16), jax.ShapeDtypeStruct((t, f), BF16),
                   jax.ShapeDtypeStruct((t, f), BF16), jax.ShapeDtypeStruct((t, f), BF16),
                   jax.ShapeDtypeStruct((t, d), F32)),
        grid=(t // tm, nf),
        in_specs=[pl.BlockSpec((tm, d), lambda i, j: (i, 0)), pl.BlockSpec((1, d), lambda i, j: (0, 0)),
                  pl.BlockSpec((tf, d), lambda i, j: (j, 0)), pl.BlockSpec((tf, d), lambda i, j: (j, 0)),
                  pl.BlockSpec((tf, d), lambda i, j: (j, 0))],
        out_specs=(pl.BlockSpec((tm, d), lambda i, j: (i, 0)), pl.BlockSpec((tm, tf), lambda i, j: (i, j)),
                   pl.BlockSpec((tm, tf), lambda i, j: (i, j)), pl.BlockSpec((tm, tf), lambda i, j: (i, j)),
                   pl.BlockSpec((tm, d), lambda i, j: (i, 0))),
        compiler_params=_cparams("parallel", "arbitrary"),
    )(x1, g, wg, wu, wd)


def _final_fwd_bwd(x2, target, g, tm):
    t, d = x2.shape
    tm = min(tm, t)

    def body(x_ref, t_ref, g_ref, loss_ref, dx_ref, dg_ref):
        i = pl.program_id(0)
        x = x_ref[...]
        gg = g_ref[...]
        err = _rms_fwd(x, gg) - t_ref[...]
        part = 0.5 * jnp.sum(jnp.mean(err * err, axis=-1, keepdims=True), axis=0, keepdims=True)
        dx, dg = _rms_bwd(x, gg, err * (1.0 / d))
        dx_ref[...] = dx

        @pl.when(i == 0)
        def _():
            loss_ref[...] = jnp.zeros_like(loss_ref)
            dg_ref[...] = jnp.zeros_like(dg_ref)

        loss_ref[...] += jnp.broadcast_to(part, loss_ref.shape)
        dg_ref[...] += dg

    return pl.pallas_call(
        body, name="final_fwd_bwd",
        out_shape=(jax.ShapeDtypeStruct((1, LANES), F32), jax.ShapeDtypeStruct((t, d), F32),
                   jax.ShapeDtypeStruct((1, d), F32)),
        grid=(t // tm,),
        in_specs=[pl.BlockSpec((tm, d), lambda i: (i, 0)), pl.BlockSpec((tm, d), lambda i: (i, 0)),
                  pl.BlockSpec((1, d), lambda i: (0, 0))],
        out_specs=(pl.BlockSpec((1, LANES), lambda i: (0, 0)), pl.BlockSpec((tm, d), lambda i: (i, 0)),
                   pl.BlockSpec((1, d), lambda i: (0, 0))),
        compiler_params=_cparams("arbitrary"),
    )(x2, target, g)


def _ffn_bwd(dx2, x1, g, gt, up, wg, wu, wd, tm, tf):
    t, d = dx2.shape
    f = gt.shape[1]
    tm = min(tm, t)
    nf = f // tf

    def body(dx2_ref, x1_ref, g_ref, gt_ref, up_ref, wg_ref, wu_ref, wd_ref, dgt_ref, dup_ref, dx1_ref, dg_ref, acc_ref,
             dxb_ref):
        i, j = pl.program_id(0), pl.program_id(1)

        @pl.when(j == 0)
        def _():
            dxb_ref[...] = dx2_ref[...].astype(BF16)

        dact = _dot_nt(dxb_ref[...], wd_ref[...])
        dgt = (dact * gt_ref[...].astype(F32)).astype(BF16)
        dup = (dact * up_ref[...].astype(F32)).astype(BF16)
        dgt_ref[...] = dgt
        dup_ref[...] = dup
        contrib = _dot(dgt, wg_ref[...]) + _dot(dup, wu_ref[...])

        @pl.when(j == 0)
        def _():
            acc_ref[...] = contrib

        @pl.when(j > 0)
        def _():
            acc_ref[...] += contrib

        @pl.when(jnp.logical_and(i == 0, j == 0))
        def _():
            dg_ref[...] = jnp.zeros_like(dg_ref)

        @pl.when(j == nf - 1)
        def _():
            dxn, dg = _rms_bwd(x1_ref[...], g_ref[...], acc_ref[...])
            dx1_ref[...] = dx2_ref[...] + dxn
            dg_ref[...] += dg

    return pl.pallas_call(
        body, name="ffn_bwd",
        out_shape=(jax.ShapeDtypeStruct((t, f), BF16), jax.ShapeDtypeStruct((t, f), BF16),
                   jax.ShapeDtypeStruct((t, d), F32), jax.ShapeDtypeStruct((1, d), F32)),
        grid=(t // tm, nf),
        in_specs=[pl.BlockSpec((tm, d), lambda i, j: (i, 0)), pl.BlockSpec((tm, d), lambda i, j: (i, 0)),
                  pl.BlockSpec((1, d), lambda i, j: (0, 0)),
                  pl.BlockSpec((tm, tf), lambda i, j: (i, j)), pl.BlockSpec((tm, tf), lambda i, j: (i, j)),
                  pl.BlockSpec((tf, d), lambda i, j: (j, 0)), pl.BlockSpec((tf, d), lambda i, j: (j, 0)),
                  pl.BlockSpec((tf, d), lambda i, j: (j, 0))],
        out_specs=(pl.BlockSpec((tm, tf), lambda i, j: (i, j)), pl.BlockSpec((tm, tf), lambda i, j: (i, j)),
                   pl.BlockSpec((tm, d), lambda i, j: (i, 0)), pl.BlockSpec((1, d), lambda i, j: (0, 0))),
        scratch_shapes=[pltpu.VMEM((tm, d), F32), pltpu.VMEM((tm, d), BF16)],
        compiler_params=_cparams("arbitrary", "arbitrary"),
    )(dx2, x1, g, gt, up, wg, wu, wd)


def _merge_bwd(dx1, ps, o, g2, wpo, wao, wout, tm, dep=None):
    t, d = dx1.shape
    tm = min(tm, t)

    def body(dx1_ref, ps_ref, o_ref, gp_ref, ga_ref, wpo_ref, wao_ref, wout_ref, dpy_ref, day_ref, dg2_ref, dps_ref, da_ref):
        dm = _dot_nt(dx1_ref[...].astype(BF16), wout_ref[...])
        py = _dot(ps_ref[...], wpo_ref[...])
        ay = _dot(o_ref[...].astype(BF16), wao_ref[...])
        sp = _sigmoid(gp_ref[...].astype(F32))
        sa = _sigmoid(ga_ref[...].astype(F32))
        dpy = (dm * sp).astype(BF16)
        day = (dm * sa).astype(BF16)
        dpy_ref[...] = dpy
        day_ref[...] = day
        dg2_ref[:, :d] = (dm * py * (sp * (1.0 - sp))).astype(BF16)
        dg2_ref[:, d:] = (dm * ay * (sa * (1.0 - sa))).astype(BF16)
        dps_ref[...] = _dot_nt(dpy, wpo_ref[...])
        da_ref[...] = _dot_nt(day, wao_ref[...]).astype(BF16)

    row = lambda w: pl.BlockSpec((tm, w), lambda i: (i, 0))
    full = lambda a: pl.BlockSpec(a.shape, lambda i: (0, 0))
    dep_specs, dep_ops = _dep_args(dep)
    return pl.pallas_call(
        _after(body, 8, dep), name="merge_bwd",
        out_shape=(jax.ShapeDtypeStruct((t, d), BF16), jax.ShapeDtypeStruct((t, d), BF16),
                   jax.ShapeDtypeStruct((t, 2 * d), BF16), jax.ShapeDtypeStruct((t, POOL_WIDTH), F32),
                   jax.ShapeDtypeStruct((t, ATTN_WIDTH), BF16)),
        grid=(t // tm,),
        in_specs=[row(d), row(POOL_WIDTH), row(ATTN_WIDTH), pl.BlockSpec((tm, d), lambda i: (i, 0)),
                  pl.BlockSpec((tm, d), lambda i: (i, 1)), full(wpo), full(wao), full(wout)] + dep_specs,
        out_specs=(row(d), row(d), row(2 * d), row(POOL_WIDTH), row(ATTN_WIDTH)),
        compiler_params=_cparams("parallel"),
    )(dx1, ps, o, g2, g2, wpo, wao, wout, *dep_ops)


def _attn_bwd(qa, ka, v, do, lse4, seq, tq, dep=None):
    t = qa.shape[0]
    nq = seq // tq
    hp_n = N_HEADS // 2
    heads = [slice(e * LANES, (e + 1) * LANES) for e in range(2)]

    def body(q_ref, k_ref, v_ref, do_ref, lse_ref, dq_ref, dk_ref, dv_ref, dfr_ref, dk_acc, dv_acc, p_buf, dp_buf):
        diag_ok = lax.broadcasted_iota(jnp.int32, (tq, tq), 0) >= lax.broadcasted_iota(jnp.int32, (tq, tq), 1)
        lane_q = lax.broadcasted_iota(jnp.int32, (tq, LANES), 1)
        lane_s = lax.broadcasted_iota(jnp.int32, (seq, LANES), 1)
        mine_q = [lane_q < HEAD_DIM, lane_q >= HEAD_DIM]
        dv_acc[...] = jnp.zeros_like(dv_acc)
        dk_acc[...] = jnp.zeros_like(dk_acc)
        dfr_ref[...] = jnp.zeros_like(dfr_ref)

        def q_step(i, _):
            q0 = pl.multiple_of(i * tq, tq)
            qs = [q_ref[pl.ds(q0, tq), hl] for hl in heads]
            dov = do_ref[pl.ds(q0, tq), :]
            dos = [jnp.where(mq, dov, jnp.zeros((), BF16)) for mq in mine_q]
            lss = [lse_ref[pl.ds(q0, tq), e:e + 1] for e in range(2)]

            def sweep1(j, dls):
                r0 = pl.multiple_of(j * tq, tq)
                vv = v_ref[pl.ds(r0, tq), :]
                out = []
                for e, hl in enumerate(heads):
                    s = _dot_nt(qs[e], k_ref[pl.ds(r0, tq), hl])
                    s = jnp.where(jnp.logical_or(diag_ok, j < i), s, NEG_BIG)
                    p = jnp.exp(s - lss[e])
                    dp = _dot_nt(dos[e], vv)
                    p_buf[e, j] = p
                    dp_buf[e, j] = dp
                    dv_acc[pl.ds(r0, tq), :] += _dot_tn(p.astype(BF16), dos[e])
                    out.append(dls[e] + _fold_lanes(p * dp, jnp.add))
                return tuple(out)

            dls = lax.fori_loop(0, i + 1, sweep1, (jnp.zeros((tq, LANES), F32),) * 2)
            dls = [jnp.sum(d, axis=1, keepdims=True) for d in dls]

            def sweep2(j, dqs):
                r0 = pl.multiple_of(j * tq, tq)
                out = []
                for e, hl in enumerate(heads):
                    ds = p_buf[e, j] * (dp_buf[e, j] - dls[e])
                    dfr_ref[e, pl.ds(j, 1), :] += jnp.sum(ds, axis=0, keepdims=True)
                    dsb = ds.astype(BF16)
                    dk_acc[e, pl.ds(r0, tq), :] += _dot_tn(dsb, qs[e])
                    out.append(dqs[e] + _dot(dsb, k_ref[pl.ds(r0, tq), hl]))
                return tuple(out)

            dqs = lax.fori_loop(0, i + 1, sweep2, (jnp.zeros((tq, LANES), F32),) * 2)
            dq = jnp.where(mine_q[0], dqs[0], pltpu.roll(dqs[1], HEAD_DIM, 1)) * ATTN_SCALE
            dq_ref[pl.ds(q0, tq), :] = dq.astype(BF16)
            return 0

        lax.fori_loop(0, nq, q_step, 0)
        dk_ref[...] = jnp.where(lane_s < HEAD_DIM, dk_acc[0], pltpu.roll(dk_acc[1], HEAD_DIM, 1)).astype(BF16)
        dv_ref[...] = dv_acc[...].astype(BF16)

    wide = pl.BlockSpec((seq, 2 * LANES), lambda b, hp: (b, hp))
    col = pl.BlockSpec((seq, LANES), lambda b, hp: (b, hp))
    pair = pl.BlockSpec((None, seq, 2), lambda b, hp: (hp, b, 0))
    dep_specs, dep_ops = _dep_args(dep)
    return pl.pallas_call(
        _after(body, 5, dep), name="attn_bwd",
        out_shape=(jax.ShapeDtypeStruct((t, ATTN_WIDTH), BF16),) * 3 + (jax.ShapeDtypeStruct((N_HEADS, t // tq, tq), F32),),
        grid=(t // seq, hp_n),
        in_specs=[wide, wide, col, col, pair] + dep_specs,
        out_specs=(col, col, col, pl.BlockSpec((2, nq, tq), lambda b, hp: (hp, b, 0))),
        scratch_shapes=[pltpu.VMEM((2, seq, LANES), F32), pltpu.VMEM((seq, LANES), F32),
                        pltpu.VMEM((2, nq, tq, tq), F32), pltpu.VMEM((2, nq, tq, tq), F32)],
        compiler_params=_cparams("parallel", "arbitrary"),
    )(qa, ka, v, do, lse4, *dep_ops)


def _forget_bwd(dfc, fl, bf, seq):
    t = fl.shape[0]
    cb = min(256, seq)
    nb = seq // cb

    def body(dfc_ref, fl_ref, bf_ref, dfl_ref, db_ref):
        b = pl.program_id(0)
        ri = lax.broadcasted_iota(jnp.int32, (cb, cb), 0)
        ci = lax.broadcasted_iota(jnp.int32, (cb, cb), 1)
        tri = (ci >= ri).astype(BF16)
        carry = jnp.zeros((1, LANES), F32)
        dbs = jnp.zeros((1, LANES), F32)
        for blk in reversed(range(nb)):
            rs = slice(blk * cb, (blk + 1) * cb)
            dlf = _tri_dot(tri, -dfc_ref[rs, :]) + carry
            carry = dlf[0:1, :]
            dfl = dlf * _sigmoid(-(fl_ref[rs, :] + bf_ref[...]))
            dfl_ref[rs, :] = dfl.astype(BF16)
            dbs = dbs + jnp.sum(dfl, axis=0, keepdims=True)

        @pl.when(b == 0)
        def _():
            db_ref[...] = jnp.zeros_like(db_ref)

        db_ref[...] += dbs

    return pl.pallas_call(
        body, name="forget_bwd",
        out_shape=(jax.ShapeDtypeStruct((t, LANES), BF16), jax.ShapeDtypeStruct((1, LANES), F32)),
        grid=(t // seq,),
        in_specs=[pl.BlockSpec((seq, LANES), lambda b: (b, 0)), pl.BlockSpec((seq, LANES), lambda b: (b, 0)),
                  pl.BlockSpec((1, LANES), lambda b: (0, 0))],
        out_specs=(pl.BlockSpec((seq, LANES), lambda b: (b, 0)), pl.BlockSpec((1, LANES), lambda b: (0, 0))),
        compiler_params=_cparams("arbitrary"),
    )(dfc, fl, bf)


def _pool_bwd(dps, p, mix, scale, seq):
    t = dps.shape[0]

    def body(dps_ref, p_ref, mix_ref, sc_ref, du_ref, dmix_ref, dsc_ref):
        b = pl.program_id(0)

        @pl.when(b == 0)
        def _():
            dmix_ref[...] = jnp.zeros_like(dmix_ref)
            dsc_ref[...] = jnp.zeros_like(dsc_ref)

        tpos = lax.broadcasted_iota(jnp.int32, (seq, POOL_GROUP_DIM), 0)
        for g in range(POOL_GROUPS):
            sl = slice(g * POOL_GROUP_DIM, (g + 1) * POOL_GROUP_DIM)
            pb = p_ref[:, sl]
            dpsg = dps_ref[:, sl]
            pm = _dot(pb, mix_ref[g])
            dsc_ref[:, sl] += jnp.sum(dpsg * pm, axis=0, keepdims=True)
            dpm = (dpsg * sc_ref[:, sl]).astype(BF16)
            dmix_ref[g] += _dot_tn(pb, dpm)
            dp = _dot_nt(dpm, mix_ref[g])
            cnt = jnp.minimum(tpos + 1, POOL_WINDOWS[g]).astype(F32)
            s = dp / cnt
            for lvl in range(g + 1):
                d = 2 ** lvl
                s = s + jnp.where(tpos < seq - d, pltpu.roll(s, seq - d, 0), 0.0)
            du_ref[:, sl] = (s - dp).astype(BF16)

    return pl.pallas_call(
        body, name="pool_bwd",
        out_shape=(jax.ShapeDtypeStruct((t, POOL_WIDTH), BF16),
                   jax.ShapeDtypeStruct((POOL_GROUPS, POOL_GROUP_DIM, POOL_GROUP_DIM), F32),
                   jax.ShapeDtypeStruct((1, POOL_WIDTH), F32)),
        grid=(t // seq,),
        in_specs=[pl.BlockSpec((seq, POOL_WIDTH), lambda b: (b, 0)), pl.BlockSpec((seq, POOL_WIDTH), lambda b: (b, 0)),
                  pl.BlockSpec((POOL_GROUPS, POOL_GROUP_DIM, POOL_GROUP_DIM), lambda b: (0, 0, 0)),
                  pl.BlockSpec((1, POOL_WIDTH), lambda b: (0, 0))],
        out_specs=(pl.BlockSpec((seq, POOL_WIDTH), lambda b: (b, 0)),
                   pl.BlockSpec((POOL_GROUPS, POOL_GROUP_DIM, POOL_GROUP_DIM), lambda b: (0, 0, 0)),
                   pl.BlockSpec((1, POOL_WIDTH), lambda b: (0, 0))),
        compiler_params=_cparams("arbitrary"),
    )(dps, p, mix, scale)


def _in_bwd(du, dq, dk, dv, dg2, dfl, dx1, x, g, wu, wqkv, wg2, wft, tm):
    t, d = x.shape
    tm = min(tm, t)
    aw = ATTN_WIDTH

    def body(du_ref, dq_ref, dk_ref, dv_ref, dg2_ref, dfl_ref, dx1_ref, x_ref, g_ref, wu_ref, wqkv_ref, wg2_ref, wft_ref,
             dx_ref, dg_ref):
        i = pl.program_id(0)
        dh = _dot(du_ref[...], wu_ref[...])
        dh += _dot(dq_ref[...], wqkv_ref[0:aw, :])
        dh += _dot(dk_ref[...], wqkv_ref[aw:2 * aw, :])
        dh += _dot(dv_ref[...], wqkv_ref[2 * aw:3 * aw, :])
        dh += _dot(dg2_ref[...], wg2_ref[...])
        dh += _dot(dfl_ref[...], wft_ref[...])
        dxn, dg = _rms_bwd(x_ref[...], g_ref[...], dh)
        dx_ref[...] = dx1_ref[...] + dxn

        @pl.when(i == 0)
        def _():
            dg_ref[...] = jnp.zeros_like(dg_ref)

        dg_ref[...] += dg

    row = lambda w: pl.BlockSpec((tm, w), lambda i: (i, 0))
    full = lambda a: pl.BlockSpec(a.shape, lambda i: (0, 0))
    return pl.pallas_call(
        body, name="in_bwd",
        out_shape=(jax.ShapeDtypeStruct((t, d), F32), jax.ShapeDtypeStruct((1, d), F32)),
        grid=(t // tm,),
        in_specs=[row(POOL_WIDTH), row(aw), row(aw), row(aw), row(2 * d), row(LANES), row(d), row(d),
                  pl.BlockSpec((1, d), lambda i: (0, 0)), full(wu), full(wqkv), full(wg2), full(wft)],
        out_specs=(row(d), pl.BlockSpec((1, d), lambda i: (0, 0))),
        compiler_params=_cparams("arbitrary"),
    )(du, dq, dk, dv, dg2, dfl, dx1, x, g, wu, wqkv, wg2, wft)


def _position():
    return lax.axis_index("x"), lax.axis_index("y"), lax.axis_index("c")


def _remote(src, dst, send_sem, recv_sem, device):
    return pltpu.make_async_remote_copy(src_ref=src, dst_ref=dst, send_sem=send_sem, recv_sem=recv_sem,
                                        device_id=device, device_id_type=MESH)


HBM = pl.BlockSpec(memory_space=pltpu.HBM)
SEM = pl.BlockSpec(memory_space=pltpu.SEMAPHORE)
DATAFLOW = pltpu.SideEffectType.DATAFLOW_SIDE_EFFECTING


def _copies_start(name, arrays, plan, m, dep=None):
    n = len(arrays)
    arrays = [pltpu.with_memory_space_constraint(a, pltpu.HBM) for a in arrays]

    def body(*refs):
        ins, send_sem, recv_sem, token = refs[:n], refs[n], refs[n + 1], refs[2 * n + 2]
        for i, (src, dst, device, _) in enumerate(plan(ins, *_position())):
            _remote(src, dst, send_sem.at[i], recv_sem.at[i], device).start()
        token[...] = jnp.zeros_like(token)

    dep_specs, dep_ops = _dep_args(dep)
    outs = pl.pallas_call(
        _after(body, n, dep), name=name,
        out_shape=(pltpu.SemaphoreType.DMA((m,)), pltpu.SemaphoreType.DMA((m,)),
                   *[pltpu.HBM(a.shape, a.dtype) for a in arrays], jax.ShapeDtypeStruct((8, LANES), F32)),
        in_specs=[HBM] * n + dep_specs, out_specs=(SEM, SEM, *[HBM] * n, pl.BlockSpec(memory_space=pltpu.VMEM)),
        input_output_aliases={i: i + 2 for i in range(n)},
        compiler_params=pltpu.CompilerParams(has_side_effects=DATAFLOW),
    )(*arrays, *dep_ops)
    return (outs[0], outs[1]), list(outs[2:2 + n]), outs[2 + n]


def _copies_wait(name, sems, arrays, plan, after):
    n = len(arrays)
    afters = list(after) if isinstance(after, (list, tuple)) else [after]

    def body(*refs):
        ins, send_sem, recv_sem = refs[:n], refs[n], refs[n + 1]
        for i, (src, dst, device, landing) in enumerate(plan(ins, *_position())):
            _remote(src, dst, send_sem.at[i], recv_sem.at[i], device).wait_send()
            _remote(landing, landing, send_sem.at[i], recv_sem.at[i], device).wait_recv()

    outs = pl.pallas_call(
        body, name=name,
        out_shape=tuple(pltpu.HBM(a.shape, a.dtype) for a in arrays),
        in_specs=[HBM] * n + [SEM, SEM] + [ANY] * len(afters), out_specs=tuple([HBM] * n),
        input_output_aliases={i: i for i in range(n)},
        compiler_params=pltpu.CompilerParams(has_side_effects=DATAFLOW),
    )(*arrays, sems[0], sems[1], *afters)
    return list(outs)


def _tie(x, dep):
    for token in _dep_list(dep):
        x = x + token[0, 0]
    return x


def _other_chips(x, y):
    return [(1 - x, y), (x, 1 - y), (1 - x, 1 - y)]


def _gather_begin(tag, shards, token):
    n = len(shards)
    lands = [lax.empty((N_CHIPS,) + s.shape, s.dtype) for s in shards]

    def plan(refs, x, y, c):
        return [(refs[k].at[c], refs[n + k].at[2 * x + y, c], (ox, oy, c), refs[n + k].at[2 * ox + oy, c])
                for k in range(n) for ox, oy in _other_chips(x, y)]

    sems, thru, token = _copies_start(f"gather_{tag}_ici_start", list(shards) + lands, plan, 3 * n, dep=token)
    return dict(tag=tag, n=n, plan=plan, sems=sems, arrays=thru, token=token)


def _gather_forward(st, after):
    n, tag = st["n"], st["tag"]
    thru = _copies_wait(f"gather_{tag}_ici_wait", st["sems"], st["arrays"], st["plan"], after)

    def plan(refs, x, y, c):
        return [(refs[k].at[2 * ox + oy, c], refs[k].at[2 * ox + oy, c], (x, y, 1 - c), refs[k].at[2 * ox + oy, 1 - c])
                for k in range(n) for ox, oy in _other_chips(x, y)]

    sems, lands, token = _copies_start(f"gather_{tag}_fwd_start", thru[n:], plan, 3 * n)
    return dict(tag=tag, n=n, plan=plan, sems=sems, arrays=lands, token=token, shards=thru[:n])


def _gather_end(st, after):
    lands = _copies_wait(f"gather_{st['tag']}_fwd_wait", st["sems"], st["arrays"], st["plan"], after)
    me = 2 * lax.axis_index("x") + lax.axis_index("y")
    return [lax.dynamic_update_index_in_dim(g, s, me, 0) for g, s in zip(lands, st["shards"])]


def _add_keep_give(name, pos, a, a_keep, a_give, b, b_keep, b_give, steps):
    r, c = b.shape[-2:]

    def spec(arr, fn):
        lead = arr.ndim - 2

        def index(i, p):
            idx = tuple(fn(i, p))
            return idx if len(idx) == arr.ndim else idx + (0, 0)

        return pl.BlockSpec((None,) * lead + (r, c), index)

    out_spec = pl.BlockSpec((None, r, c), lambda i, p: (i, 0, 0))

    def body(p_ref, ak_ref, bk_ref, ag_ref, bg_ref, keep_ref, give_ref):
        keep_ref[...] = ak_ref[...] + bk_ref[...].astype(F32)
        give_ref[...] = (ag_ref[...] + bg_ref[...].astype(F32)).astype(BF16)

    return pl.pallas_call(
        body, name=name,
        out_shape=(jax.ShapeDtypeStruct((steps, r, c), F32), jax.ShapeDtypeStruct((steps, r, c), BF16)),
        grid_spec=pltpu.PrefetchScalarGridSpec(
            num_scalar_prefetch=1, grid=(steps,),
            in_specs=[spec(a, a_keep), spec(b, b_keep), spec(a, a_give), spec(b, b_give)],
            out_specs=(out_spec, out_spec)),
        compiler_params=_cparams("parallel"),
    )(pos, a, b, a, b)


def _add_last(name, a, b):
    _, r, c = a.shape
    blk = pl.BlockSpec((None, r, c), lambda i: (0, 0, 0))

    def body(a_ref, b_ref, o_ref):
        o_ref[...] = a_ref[...] + b_ref[...].astype(F32)

    return pl.pallas_call(
        body, name=name, out_shape=jax.ShapeDtypeStruct((r, c), F32), grid=(1,), in_specs=[blk, blk],
        out_specs=pl.BlockSpec((r, c), lambda i: (0, 0)), compiler_params=_cparams("arbitrary"),
    )(a, b)


def _exchange_begin(tag, stage, gives, lands, peer_fn, extra):
    n = len(gives)

    def plan(refs, x, y, c):
        return [(refs[k], refs[n + k], peer_fn(x, y, c), refs[n + k]) for k in range(n)]

    sems, thru, token = _copies_start(f"rs{tag}_{stage}_start", gives + lands, plan, n)
    return dict(extra, tag=tag, n=n, stage=stage, plan=plan, sems=sems, arrays=thru, token=token)


def _reduce_begin(tag, grads, column_halves=False):
    n = len(grads)
    if column_halves:
        half = lambda ref, j, h: ref.at[j, :, pl.ds(pl.multiple_of(h * (ref.shape[2] // 2), LANES), ref.shape[2] // 2)]
        lands = [lax.empty((N_CHIPS, g.shape[1], g.shape[2] // 2), F32) for g in grads]
    else:
        half = lambda ref, j, h: ref.at[j, h]
        lands = [lax.empty((N_CHIPS,) + g.shape[2:], F32) for g in grads]

    def plan(refs, x, y, c):
        return [(half(refs[k], j, 1 - c), refs[n + k].at[j], (x, y, 1 - c), refs[n + k].at[j])
                for k in range(n) for j in range(N_CHIPS)]

    sems, thru, token = _copies_start(f"rs{tag}_c_start", list(grads) + lands, plan, N_CHIPS * n)
    return dict(tag=tag, n=n, stage="c", plan=plan, sems=sems, arrays=thru, token=token, column_halves=column_halves)


def _reduce_advance(st, after):
    tag, n, stage = st["tag"], st["n"], st["stage"]
    thru = _copies_wait(f"rs{tag}_{stage}_wait", st["sems"], st["arrays"], st["plan"], after)
    first, recv = thru[:n], thru[n:]
    x, y, c = _position()
    if stage == "c":
        pos = jnp.stack([c, x]).astype(jnp.int32)
        if st["column_halves"]:
            mine = lambda chip: (lambda i, p: (chip(p) + i, 0, p[0]))
        else:
            mine = lambda chip: (lambda i, p: (chip(p) + i, p[0]))
        sums = [_add_keep_give(
            f"rs{tag}_c_add{k}", pos,
            first[k], mine(lambda p: 2 * p[1]), mine(lambda p: 2 * (1 - p[1])),
            recv[k], lambda i, p: (2 * p[1] + i,), lambda i, p: (2 * (1 - p[1]) + i,), 2) for k in range(n)]
        lands = [lax.empty(s[1].shape, BF16) for s in sums]
        return _exchange_begin(tag, "x", [s[1] for s in sums], lands, lambda x, y, c: (1 - x, y, c),
                               dict(keep=[s[0] for s in sums]))
    if stage == "x":
        pos = jnp.stack([y]).astype(jnp.int32)
        sums = [_add_keep_give(
            f"rs{tag}_x_add{k}", pos,
            st["keep"][k], lambda i, p: (p[0],), lambda i, p: (1 - p[0],),
            recv[k], lambda i, p: (p[0],), lambda i, p: (1 - p[0],), 1) for k in range(n)]
        lands = [lax.empty(s[1].shape, BF16) for s in sums]
        return _exchange_begin(tag, "y", [s[1] for s in sums], lands, lambda x, y, c: (x, 1 - y, c),
                               dict(keep=[s[0] for s in sums]))
    if stage == "y":
        mine = [_add_last(f"rs{tag}_y_add{k}", st["keep"][k], recv[k]) for k in range(n)]
        lands = [lax.empty(m.shape, F32) for m in mine]
        return _exchange_begin(tag, "swap", mine, lands, lambda x, y, c: (x, y, 1 - c), {})
    return dict(done=list(zip(first, recv)), token=None)


def _all_reduce_small(v):
    r = v.shape[0]

    def body(v_ref, out_ref, buf, send_sems, recv_sems, local_sem):
        x, y, c = _position()
        me, sibling = (x, y, c), (x, y, 1 - c)
        chips = [(1 - x, y), (x, 1 - y), (1 - x, 1 - y)]

        def rows(px, py, pc):
            return buf.at[pl.ds((4 * px + 2 * py + pc) * r, r), :]

        def copy(k, block, to, src=None):
            return _remote(rows(*block) if src is None else src, rows(*block), send_sems.at[k], recv_sems.at[k], to)

        mine = pltpu.make_async_copy(v_ref, rows(*me), local_sem)
        mine.start()
        first = [copy(0, me, sibling, src=v_ref)]
        first += [copy(1 + j, me, (*chip, c), src=v_ref) for j, chip in enumerate(chips)]
        for cp in first:
            cp.start()
        passed = [copy(4 + j, (*chip, c), sibling) for j, chip in enumerate(chips)]
        for j, chip in enumerate(chips):
            copy(1 + j, (*chip, c), me).wait_recv()
            passed[j].start()
        copy(0, sibling, me).wait_recv()
        for j, chip in enumerate(chips):
            copy(4 + j, (*chip, 1 - c), me).wait_recv()
        for cp in first + passed:
            cp.wait_send()
        mine.wait()
        acc = buf[0:r, :]
        for dev in range(1, N_DEV):
            acc = acc + buf[dev * r:(dev + 1) * r, :]
        out_ref[...] = acc

    return pl.pallas_call(
        body, name="all_reduce_small",
        out_shape=jax.ShapeDtypeStruct(v.shape, F32),
        in_specs=[pl.BlockSpec(memory_space=pltpu.VMEM)],
        out_specs=pl.BlockSpec(memory_space=pltpu.VMEM),
        scratch_shapes=[pltpu.VMEM((N_DEV * r, LANES), F32), pltpu.SemaphoreType.DMA((7,)),
                        pltpu.SemaphoreType.DMA((7,)), pltpu.SemaphoreType.DMA],
        compiler_params=pltpu.CompilerParams(has_side_effects=True, vmem_limit_bytes=VMEM_LIMIT_V7X),
    )(v)


def _adamw_update(w, gg, m, v):
    mn = ADAM_B1 * m + (1.0 - ADAM_B1) * gg
    vn = ADAM_B2 * v + (1.0 - ADAM_B2) * (gg * gg)
    m_hat = mn / (1.0 - ADAM_B1 ** ADAM_STEP)
    v_hat = vn / (1.0 - ADAM_B2 ** ADAM_STEP)
    return -ADAM_LR * (m_hat / (jnp.sqrt(v_hat) + ADAM_EPS) + ADAM_WD * w), mn, vn


def _adamw(name, w, g, m, v):
    def body(w_ref, g_ref, m_ref, v_ref, d_ref, mo_ref, vo_ref):
        d_ref[...], mo_ref[...], vo_ref[...] = _adamw_update(w_ref[...], g_ref[...], m_ref[...], v_ref[...])

    blk = pl.BlockSpec(w.shape, lambda i: (0, 0))
    return pl.pallas_call(
        body, name=name, out_shape=(jax.ShapeDtypeStruct(w.shape, F32),) * 3, grid=(1,),
        in_specs=[blk] * 4, out_specs=(blk,) * 3, compiler_params=_cparams("arbitrary"),
    )(w, g, m, v)


def _adamw_rows(name, w, g, m, v, tr):
    r, _, c = w.shape
    assert r % tr == 0

    def body(w_ref, g_ref, m_ref, v_ref, d_ref, mo_ref, vo_ref):
        d_ref[...], mo_ref[...], vo_ref[...] = _adamw_update(w_ref[...], g_ref[...], m_ref[...], v_ref[...])

    blk = pl.BlockSpec((tr, 1, c), lambda i: (i, 0, 0))
    return pl.pallas_call(
        body, name=name, out_shape=(jax.ShapeDtypeStruct(w.shape, F32),) * 3, grid=(r // tr,),
        in_specs=[blk] * 4, out_specs=(blk,) * 3, compiler_params=_cparams("parallel"),
    )(w, g, m, v)


def _adamw_halves(name, pos_c, w, g_mine, g_other, m, v, tr, dep=None):
    r, c = w.shape
    rh = r // 2
    tr = tr if rh % tr == 0 else rh
    nt = rh // tr

    def body(p_ref, w_ref, gm_ref, go_ref, m_ref, v_ref, g_ref, d_ref, mo_ref, vo_ref):
        gg = jnp.where(pl.program_id(0) == p_ref[0], gm_ref[...], go_ref[...])
        g_ref[...] = gg
        d_ref[...], mo_ref[...], vo_ref[...] = _adamw_update(w_ref[...], gg, m_ref[...], v_ref[...])

    full = pl.BlockSpec((tr, c), lambda h, i, p: (h * nt + i, 0))
    half = pl.BlockSpec((tr, c), lambda h, i, p: (i, 0))
    dep_specs, dep_ops = _dep_args(dep)
    return pl.pallas_call(
        _after(body, 6, dep), name=name, out_shape=(jax.ShapeDtypeStruct((r, c), F32),) * 4,
        grid_spec=pltpu.PrefetchScalarGridSpec(
            num_scalar_prefetch=1, grid=(2, nt),
            in_specs=[full, half, half, full, full] + dep_specs, out_specs=(full,) * 4),
        compiler_params=_cparams("parallel", "parallel"),
    )(pos_c, w, g_mine, g_other, m, v, *dep_ops)


def _col_sharded_to_comm(g):
    k, n = g.shape
    return g.reshape(2, k // 2, N_CHIPS, n // N_CHIPS).transpose(2, 0, 1, 3)


def _row_sharded_to_comm(g):
    r, c = g.shape
    return g.reshape(N_CHIPS, 2, r // (2 * N_CHIPS), c)


def _col_sharded_full(g):
    _, _, rh, c = g.shape
    return g.reshape(N_CHIPS, 2 * rh, c).transpose(1, 0, 2).reshape(2 * rh, N_CHIPS * c)


def _row_sharded_full(g):
    _, _, rh, c = g.shape
    return g.reshape(N_CHIPS * 2 * rh, c)


def _pack_small(g1, bfv, mix, scale, g2n, gf, extra=None):
    row8 = jnp.pad(bfv.reshape(1, N_HEADS), ((0, 0), (0, LANES - N_HEADS)))
    if extra is not None:
        row8 = row8 + jnp.pad(extra[:, :1], ((0, 0), (N_HEADS, LANES - N_HEADS - 1)))
    return jnp.concatenate([
        g1.reshape(8, LANES), jnp.pad(row8, ((0, 7), (0, 0))), mix.reshape(512, LANES),
        jnp.pad(scale.reshape(4, LANES), ((0, 4), (0, 0))), g2n.reshape(8, LANES), gf.reshape(8, LANES)], axis=0)


def _unpack_small(s, like):
    g1, bfv, mix, scale, g2n, gf = like
    return (s[0:8].reshape(g1.shape), s[8, :N_HEADS].reshape(bfv.shape), s[16:528].reshape(mix.shape),
            s[528:532].reshape(scale.shape), s[536:544].reshape(g2n.shape), s[544:552].reshape(gf.shape))


class _MeshLinks:
    def __init__(self, shards_in, shards_rest):
        self.gin = _gather_begin("in", shards_in, None)
        self.grest = _gather_begin("rest", shards_rest, self.gin["token"])
        self.tokens = {"gather": self.grest["token"]}
        self.groups = {}

    @property
    def token(self):
        return list(self.tokens.values())

    def tie(self, x):
        return _tie(x, self.token)

    def weights_in(self, after):
        st = _gather_forward(self.gin, after)
        (g,) = _gather_end(st, st["token"])
        return g.transpose(0, 2, 1, 3).reshape(N_CHIPS * g.shape[2], 2 * g.shape[3])

    def rest_forward(self, after):
        self.grest = _gather_forward(self.grest, after)
        self.tokens["gather"] = self.grest["token"]

    def weights_rest(self, after):
        g = _gather_end(self.grest, after)
        del self.tokens["gather"]
        return [_col_sharded_full(g[0]), _col_sharded_full(g[1])] + [_row_sharded_full(a) for a in g[2:]]

    def reduce_begin(self, tag, grads, column_halves=False):
        self.groups[tag] = _reduce_begin(tag, grads, column_halves)
        self.tokens[tag] = self.groups[tag]["token"]

    def advance(self, after):
        for tag, st in self.groups.items():
            if "done" not in st:
                self.groups[tag] = _reduce_advance(st, after)
                if self.groups[tag]["token"] is None:
                    del self.tokens[tag]
                else:
                    self.tokens[tag] = self.groups[tag]["token"]

    def reduced(self, tag):
        return self.groups[tag]["done"]


class _NoLinks:
    token = None

    def __init__(self, w_in, rest):
        self.w_in, self.rest, self.grads = w_in, rest, {}

    def tie(self, x):
        return x

    def weights_in(self, after):
        return self.w_in

    def rest_forward(self, after):
        pass

    def weights_rest(self, after):
        return self.rest

    def reduce_begin(self, tag, grads, column_halves=False):
        self.grads[tag] = grads

    def advance(self, after):
        pass


def _local_step(links, x, target, seq, norm1_g, b_forget, pool_mix, pool_scale, norm2_g, norm_f_g):
    t, d = x.shape
    tq = min(256, seq)
    aw = ATTN_WIDTH
    o_q, o_f, o_g = POOL_WIDTH, POOL_WIDTH + 3 * aw, POOL_WIDTH + 3 * aw + N_HEADS
    bf = jnp.pad(b_forget, ((0, 0), (0, LANES - N_HEADS)))
    mixb = pool_mix.astype(BF16)

    h = _norm_fwd("norm1_fwd", x, links.tie(norm1_g), 512)
    w_in = links.weights_in(h)
    wu = w_in[:o_q]
    wqkv = w_in[o_q:o_f]
    wft = jnp.pad(w_in[o_f:o_g], ((0, LANES - N_HEADS), (0, 0)))
    wg2 = w_in[o_g:]
    wf = wft.T
    u = _matmul("mm_u", h, wu, "nt", F32, 1024, 512, d)
    g2 = _matmul("mm_gates", h, wg2, "nt", BF16, 1024, 512, d)
    fl, fcum = _forget_fwd(h, wf, bf, seq)
    qa, ka, v = _attn_prep(h, _head_blocks(wqkv[:aw]), _head_blocks(wqkv[aw:2 * aw]), wqkv[2 * aw:], fcum, 512)
    p, ps = _pool_fwd(u, mixb, pool_scale, seq)
    links.rest_forward([ps, qa, g2])
    o, lse = _attn_fwd(qa, ka, v, seq, tq, dep=links.token)
    w_pool_out, w_attn_out, w_out, w_ffn_gate, w_ffn_up, w_ffn_down = links.weights_rest(o)
    merged, x1 = _merge_fwd(x, ps, o, g2, w_pool_out, w_attn_out, w_out, 256)
    h2, gt, up, act, x2 = _ffn_fwd(x1, norm2_g, w_ffn_gate, w_ffn_up, w_ffn_down, 1024, 256)
    loss, dx2, d_gf = _final_fwd_bwd(x2, target, norm_f_g, 512)

    dgt, dup, dx1, d_g2n = _ffn_bwd(dx2, x1, norm2_g, gt, up, w_ffn_gate, w_ffn_up, w_ffn_down, 1024, 256)
    d_wd = _matmul("dw_down", act, dx2, "tn", F32, 1408, 1024, 1024)
    d_wg = _matmul("dw_gate", dgt, h2, "tn", F32, 1408, 1024, 1024)
    d_wu = _matmul("dw_up", dup, h2, "tn", F32, 1408, 1024, 1024)
    links.reduce_begin("a", [_row_sharded_to_comm(g) for g in (d_wg, d_wu, d_wd)])
    dpy, day, dg2, dps, da = _merge_bwd(dx1, ps, o, g2, w_pool_out, w_attn_out, w_out, 256, dep=links.token)
    links.advance(dps)
    d_wout = _matmul("dw_out", merged, dx1, "tn", F32, 1024, 1024, 1024)
    d_wpo = _matmul("dw_pool_out", ps, dpy, "tn", F32, 512, 1024, 1024)
    d_wao = _matmul("dw_attn_out", o, day, "tn", F32, 512, 1024, 1024)
    links.reduce_begin("m", [_col_sharded_to_comm(d_wpo), _col_sharded_to_comm(d_wao), _row_sharded_to_comm(d_wout)])
    dq, dk, dv, dfr = _attn_bwd(qa, ka, v, da, lse, seq, tq, dep=links.token)
    links.advance(dq)
    dfc = jnp.pad(dfr.reshape(N_HEADS, t).T, ((0, 0), (0, LANES - N_HEADS)))
    dfl, d_bf = _forget_bwd(dfc, fl, bf, seq)
    du, d_mix, d_scale = _pool_bwd(dps, p, mixb, links.tie(pool_scale), seq)
    d_wu_in = _matmul("dw_in_u", du, h, "tn", F32, 512, 1024, 1024)
    d_wq = _matmul("dw_in_q", dq, h, "tn", F32, 512, 1024, 1024)
    d_wk = _matmul("dw_in_k", dk, h, "tn", F32, 512, 1024, 1024)
    d_wv = _matmul("dw_in_v", dv, h, "tn", F32, 512, 1024, 1024)
    links.advance([d_wu_in, d_wq, d_wk, d_wv])
    d_wf = _matmul("dw_in_f", dfl, h, "tn", F32, LANES, 1024, 512)
    d_wg2 = _matmul("dw_in_gates", dg2, h, "tn", F32, 1024, 1024, 1024, dep=links.token)
    d_win = jnp.concatenate([d_wu_in, d_wq, d_wk, d_wv, d_wf[:N_HEADS], d_wg2], axis=0)
    comm_b = [d_win.reshape(N_CHIPS, d_win.shape[0] // N_CHIPS, d)]
    links.advance(comm_b)
    links.reduce_begin("b", comm_b, column_halves=True)
    dx, d_g1 = _in_bwd(du, dq, dk, dv, dg2, dfl, dx1, x, links.tie(norm1_g), wu, wqkv, wg2, wft, 256)
    links.advance(dx)
    small = (d_g1, d_bf[:, :N_HEADS], d_mix, d_scale, d_g2n, d_gf)
    return loss, dx, small


def kernel(x, norm1_g, w_in, b_forget, pool_mix, pool_scale, w_pool_out, w_attn_out, w_out, norm2_g, w_ffn_gate, w_ffn_up, w_ffn_down, norm_f_g, loss_target, m_norm1_g, m_w_in, m_b_forget, m_pool_mix, m_pool_scale, m_w_pool_out, m_w_attn_out, m_w_out, m_norm2_g, m_w_ffn_gate, m_w_ffn_up, m_w_ffn_down, m_norm_f_g, v_norm1_g, v_w_in, v_b_forget, v_pool_mix, v_pool_scale, v_w_pool_out, v_w_attn_out, v_w_out, v_norm2_g, v_w_ffn_gate, v_w_ffn_up, v_w_ffn_down, v_norm_f_g):
    nb, seq, d = x.shape
    group_a = ((w_ffn_gate, m_w_ffn_gate, v_w_ffn_gate, True, 9), (w_ffn_up, m_w_ffn_up, v_w_ffn_up, True, 10),
               (w_ffn_down, m_w_ffn_down, v_w_ffn_down, False, 11))
    group_m = ((w_pool_out, m_w_pool_out, v_w_pool_out, False, 5), (w_attn_out, m_w_attn_out, v_w_attn_out, False, 6),
               (w_out, m_w_out, v_w_out, False, 7))
    group_b = ((w_in, m_w_in, v_w_in, False, 1),)
    small_w = (norm1_g, b_forget, pool_mix, pool_scale, norm2_g, norm_f_g)
    small_m = (m_norm1_g, m_b_forget, m_pool_mix, m_pool_scale, m_norm2_g, m_norm_f_g)
    small_v = (v_norm1_g, v_b_forget, v_pool_mix, v_pool_scale, v_norm2_g, v_norm_f_g)
    small_pos = (0, 2, 3, 4, 8, 12)
    view = lambda a, tr: a[0].T if tr else a[0]
    unview = lambda a, tr, like: (a.T if tr else a).reshape(like.shape)

    def shard(w, tr):
        lw = view(w, tr).astype(BF16)
        return lw.reshape(2, lw.shape[0] // 2, lw.shape[1])

    cm = lambda a: jnp.transpose(a, (2, 0, 1))
    rows_in, _, cols_in = cm(w_in).shape
    shard_in = cm(w_in).astype(BF16).reshape(rows_in, 2, cols_in // 2).transpose(1, 0, 2)
    links = _MeshLinks([shard_in],
                       [shard(w_pool_out, False), shard(w_attn_out, False), shard(w_out, False),
                        shard(w_ffn_gate, True), shard(w_ffn_up, True), shard(w_ffn_down, False)])
    loss, dx, small_g = _local_step(
        links, x.reshape(nb * seq, d), loss_target.reshape(nb * seq, d), seq,
        norm1_g, b_forget, pool_mix[0], pool_scale, norm2_g, norm_f_g.reshape(1, d))

    grads, deltas, new_m, new_v = [None] * 13, [None] * 13, [None] * 13, [None] * 13
    pos_c = jnp.stack([lax.axis_index("c")]).astype(jnp.int32)

    def update(tag, group, dep):
        last = []
        for k, ((w, m, v, tr, pos), (mine, other)) in enumerate(zip(group, links.reduced(tag))):
            outs = _adamw_halves(f"adamw_{tag}{k}", pos_c, view(w, tr), mine, other, view(m, tr), view(v, tr), 256,
                                 dep=dep)
            grads[pos], deltas[pos], new_m[pos], new_v[pos] = (unview(a, tr, w) for a in outs)
            last.append(outs[1])
        return last

    last = update("a", group_a, links.token) + update("m", group_m, links.token)
    links.advance(last)
    small_sum = _all_reduce_small(links.tie(_pack_small(*small_g, extra=loss)))
    loss_out = small_sum[8, N_HEADS]
    dl, mn, vn = _adamw("adamw_small", _pack_small(*small_w), small_sum * _small_mask(), _pack_small(*small_m),
                        _pack_small(*small_v))
    for pos, g, a, b, e in zip(small_pos, _unpack_small(small_sum, small_w), _unpack_small(dl, small_w),
                               _unpack_small(mn, small_w), _unpack_small(vn, small_w)):
        grads[pos], deltas[pos], new_m[pos], new_v[pos] = g, a, b, e
    links.advance(dl)
    links.advance(links.token)
    (mine, other), = links.reduced("b")
    c = lax.axis_index("c")
    g_in = jnp.where(c == 0, jnp.concatenate([mine, other], axis=1), jnp.concatenate([other, mine], axis=1))
    g_in = g_in.reshape(rows_in, 1, cols_in)
    outs = _adamw_rows("adamw_b0", cm(w_in), g_in, cm(m_w_in), cm(v_w_in), 114)
    grads[1], deltas[1], new_m[1], new_v[1] = (jnp.transpose(a, (1, 2, 0)) for a in (g_in,) + tuple(outs))

    return (loss_out, dx.reshape(nb, seq, d), *grads, *deltas, *new_m, *new_v)


def _small_mask():
    rows = lax.broadcasted_iota(jnp.int32, (552, LANES), 0)
    lanes = lax.broadcasted_iota(jnp.int32, (552, LANES), 1)
    return jnp.where(jnp.logical_and(rows == 8, lanes == N_HEADS), 0.0, 1.0).astype(F32)
```

```python
import functools

import jax
import jax.numpy as jnp
from jax import lax
from jax.experimental import pallas as pl
from jax.experimental.pallas import tpu as pltpu

F32 = jnp.float32
BF16 = jnp.bfloat16

D_MODEL = 1024
POOL_WINDOWS = (2, 4, 8, 16)
POOL_GROUPS = 4
POOL_GROUP_DIM = 128
POOL_WIDTH = 512
HEAD_DIM = 64
N_HEADS = 8
ATTN_WIDTH = 512
D_FF = 2816
RMS_EPS = 1e-6
ATTN_SCALE = HEAD_DIM ** -0.5
NEG_BIG = -1e30

ADAM_LR = 0.001
ADAM_B1 = 0.9
ADAM_B2 = 0.999
ADAM_EPS = 1e-08
ADAM_WD = 0.01
ADAM_STEP = 10

LANES = 128
N_CHIPS = 4
N_DEV = 8
VMEM_LIMIT_V7X = 52 * 1024 * 1024
MESH = pl.DeviceIdType.MESH
ANY = pl.BlockSpec(memory_space=pl.ANY)


def _cparams(*sem):
    return pltpu.CompilerParams(dimension_semantics=sem if sem else None, vmem_limit_bytes=VMEM_LIMIT_V7X)


def _dep_list(dep):
    return [] if dep is None else (list(dep) if isinstance(dep, (list, tuple)) else [dep])


def _after(body, n_in, dep):
    k = len(_dep_list(dep))
    if k == 0:
        return body

    def wrapped(*refs):
        body(*refs[:n_in], *refs[n_in + k:])

    return wrapped


def _dep_args(dep):
    deps = _dep_list(dep)
    return [ANY] * len(deps), deps


def _dot(a, b):
    return lax.dot_general(a, b, (((1,), (0,)), ((), ())), preferred_element_type=F32)


def _dot_nt(a, b):
    return lax.dot_general(a, b, (((1,), (1,)), ((), ())), preferred_element_type=F32)


def _dot_tn(a, b):
    return lax.dot_general(a, b, (((0,), (0,)), ((), ())), preferred_element_type=F32)


def _sigmoid(x):
    return jax.nn.sigmoid(x)


def _rms_fwd(x, g):
    r = lax.rsqrt(jnp.mean(x * x, axis=-1, keepdims=True) + RMS_EPS)
    return (x * r) * g


def _rms_bwd(x, g, dy):
    r = lax.rsqrt(jnp.mean(x * x, axis=-1, keepdims=True) + RMS_EPS)
    xh = x * r
    dg = jnp.sum(dy * xh, axis=0, keepdims=True)
    dxh = dy * g
    dx = r * (dxh - xh * jnp.mean(dxh * xh, axis=-1, keepdims=True))
    return dx, dg


def _matmul(name, a, b, mode, out_dtype, tm, tn, tk, dep=None):
    if mode == "nn":
        (m, k), (_, n) = a.shape, b.shape
    elif mode == "nt":
        (m, k), (n, _) = a.shape, b.shape
    else:
        (k, m), (_, n) = a.shape, b.shape
    tm, tn, tk = min(tm, m), min(tn, n), min(tk, k)
    assert m % tm == 0 and n % tn == 0 and k % tk == 0, (name, m, n, k, tm, tn, tk)
    nk = k // tk
    if mode == "tn":
        a_spec = pl.BlockSpec((tk, tm), lambda i, j, kk: (kk, i))
    else:
        a_spec = pl.BlockSpec((tm, tk), lambda i, j, kk: (i, kk))
    if mode == "nt":
        b_spec = pl.BlockSpec((tn, tk), lambda i, j, kk: (j, kk))
    else:
        b_spec = pl.BlockSpec((tk, tn), lambda i, j, kk: (kk, j))
    dot = {"nn": _dot, "nt": _dot_nt, "tn": _dot_tn}[mode]
    use_scratch = nk > 1 and out_dtype != F32

    def body(a_ref, b_ref, o_ref, *scratch):
        prod = dot(a_ref[...].astype(BF16), b_ref[...].astype(BF16))
        if nk == 1:
            o_ref[...] = prod.astype(out_dtype)
            return
        acc = scratch[0] if use_scratch else o_ref
        kk = pl.program_id(2)

        @pl.when(kk == 0)
        def _():
            acc[...] = prod

        @pl.when(kk > 0)
        def _():
            acc[...] += prod

        if use_scratch:
            @pl.when(kk == nk - 1)
            def _():
                o_ref[...] = acc[...].astype(out_dtype)

    dep_specs, dep_ops = _dep_args(dep)
    return pl.pallas_call(
        _after(body, 2, dep),
        name=name,
        out_shape=jax.ShapeDtypeStruct((m, n), out_dtype),
        grid=(m // tm, n // tn, nk),
        in_specs=[a_spec, b_spec] + dep_specs,
        out_specs=pl.BlockSpec((tm, tn), lambda i, j, kk: (i, j)),
        scratch_shapes=[pltpu.VMEM((tm, tn), F32)] if use_scratch else [],
        compiler_params=_cparams("parallel", "parallel", "arbitrary"),
    )(a, b, *dep_ops)


def _norm_fwd(name, x, g, tm):
    t, d = x.shape
    tm = min(tm, t)

    def body(x_ref, g_ref, h_ref):
        h_ref[...] = _rms_fwd(x_ref[...], g_ref[...]).astype(BF16)

    return pl.pallas_call(
        body, name=name, out_shape=jax.ShapeDtypeStruct((t, d), BF16), grid=(t // tm,),
        in_specs=[pl.BlockSpec((tm, d), lambda i: (i, 0)), pl.BlockSpec((1, d), lambda i: (0, 0))],
        out_specs=pl.BlockSpec((tm, d), lambda i: (i, 0)),
        compiler_params=_cparams("parallel"),
    )(x, g)


def _split3(x):
    hi = x.astype(BF16)
    r1 = x - hi.astype(F32)
    mid = r1.astype(BF16)
    lo = (r1 - mid.astype(F32)).astype(BF16)
    return hi, mid, lo


def _tri_dot(tri, x):
    hi, mid, lo = _split3(x)
    return _dot(tri, hi) + _dot(tri, mid) + _dot(tri, lo)


def _forget_fwd(h, wf, bf, seq):
    t, d = h.shape
    cb = min(256, seq)

    def body(h_ref, wf_ref, bf_ref, fl_ref, fc_ref):
        fl = _dot(h_ref[...], wf_ref[...])
        fl_ref[...] = fl
        xx = fl + bf_ref[...]
        lf = jnp.minimum(xx, 0.0) - jnp.log(1.0 + jnp.exp(-jnp.abs(xx)))
        ri = lax.broadcasted_iota(jnp.int32, (cb, cb), 0)
        ci = lax.broadcasted_iota(jnp.int32, (cb, cb), 1)
        tri = (ri >= ci).astype(BF16)
        carry = jnp.zeros((1, LANES), F32)
        for blk in range(seq // cb):
            cs = _tri_dot(tri, lf[blk * cb:(blk + 1) * cb]) + carry
            fc_ref[blk * cb:(blk + 1) * cb, :] = cs
            carry = cs[cb - 1:cb, :]

    return pl.pallas_call(
        body, name="forget_fwd",
        out_shape=(jax.ShapeDtypeStruct((t, LANES), F32), jax.ShapeDtypeStruct((t, LANES), F32)),
        grid=(t // seq,),
        in_specs=[pl.BlockSpec((seq, d), lambda b: (b, 0)), pl.BlockSpec((d, LANES), lambda b: (0, 0)),
                  pl.BlockSpec((1, LANES), lambda b: (0, 0))],
        out_specs=(pl.BlockSpec((seq, LANES), lambda b: (b, 0)), pl.BlockSpec((seq, LANES), lambda b: (b, 0))),
        compiler_params=_cparams("parallel"),
    )(h, wf, bf)


def _pool_fwd(u, mix, scale, seq):
    t = u.shape[0]

    def body(u_ref, mix_ref, sc_ref, p_ref, ps_ref):
        tpos = lax.broadcasted_iota(jnp.int32, (seq, POOL_GROUP_DIM), 0)
        for g in range(POOL_GROUPS):
            sl = slice(g * POOL_GROUP_DIM, (g + 1) * POOL_GROUP_DIM)
            ug = u_ref[:, sl]
            s = ug
            for lvl in range(g + 1):
                d = 2 ** lvl
                s = s + jnp.where(tpos >= d, pltpu.roll(s, d, 0), 0.0)
            cnt = jnp.minimum(tpos + 1, POOL_WINDOWS[g]).astype(F32)
            pb = (s / cnt - ug).astype(BF16)
            p_ref[:, sl] = pb
            ps_ref[:, sl] = (_dot(pb, mix_ref[g]) * sc_ref[:, sl]).astype(BF16)

    return pl.pallas_call(
        body, name="pool_fwd",
        out_shape=(jax.ShapeDtypeStruct((t, POOL_WIDTH), BF16), jax.ShapeDtypeStruct((t, POOL_WIDTH), BF16)),
        grid=(t // seq,),
        in_specs=[pl.BlockSpec((seq, POOL_WIDTH), lambda b: (b, 0)),
                  pl.BlockSpec((POOL_GROUPS, POOL_GROUP_DIM, POOL_GROUP_DIM), lambda b: (0, 0, 0)),
                  pl.BlockSpec((1, POOL_WIDTH), lambda b: (0, 0))],
        out_specs=(pl.BlockSpec((seq, POOL_WIDTH), lambda b: (b, 0)), pl.BlockSpec((seq, POOL_WIDTH), lambda b: (b, 0))),
        compiler_params=_cparams("parallel"),
    )(u, mix, scale)


def _aug_constants():
    w = N_HEADS * LANES
    rows = jnp.arange(3 * LANES)
    piece, head = rows // LANES, rows % LANES
    cols = jnp.arange(w)
    live = (head < N_HEADS)[:, None]
    pq = (live & (cols[None, :] == (head * LANES + HEAD_DIM + piece)[:, None])).astype(BF16)
    pk = -(live & (cols[None, :] == (head * LANES + HEAD_DIM + 3 + piece)[:, None])).astype(BF16)
    lane = cols % LANES
    oq = ((lane >= HEAD_DIM + 3) & (lane < HEAD_DIM + 6)).astype(F32)[None, :]
    ok = ((lane >= HEAD_DIM) & (lane < HEAD_DIM + 3)).astype(F32)[None, :]
    return pq, pk, oq, ok


def _head_blocks(wt):
    d = wt.shape[1]
    return jnp.pad(wt.reshape(N_HEADS, HEAD_DIM, d), ((0, 0), (0, LANES - HEAD_DIM), (0, 0))).reshape(N_HEADS * LANES, d)


def _attn_prep(h, wq, wk, wv, fcum, tm):
    t, d = h.shape
    tm = min(tm, t)
    w = N_HEADS * LANES
    pq, pk, oq, ok = _aug_constants()

    def body(h_ref, wq_ref, wk_ref, wv_ref, f_ref, pq_ref, pk_ref, oq_ref, ok_ref, qa_ref, ka_ref, v_ref):
        hh = h_ref[...]
        fs = jnp.concatenate(_split3(f_ref[...]), axis=1)
        q = _dot_nt(hh, wq_ref[...]).astype(BF16).astype(F32) * ATTN_SCALE
        qa_ref[...] = (q + _dot(fs, pq_ref[...]) + oq_ref[...]).astype(BF16)
        k = _dot_nt(hh, wk_ref[...]).astype(BF16).astype(F32)
        ka_ref[...] = (k + _dot(fs, pk_ref[...]) + ok_ref[...]).astype(BF16)
        v_ref[...] = _dot_nt(hh, wv_ref[...]).astype(BF16)

    row = lambda n: pl.BlockSpec((tm, n), lambda i: (i, 0))
    full = lambda a: pl.BlockSpec(a.shape, lambda i: (0, 0))
    return pl.pallas_call(
        body, name="attn_prep",
        out_shape=(jax.ShapeDtypeStruct((t, w), BF16), jax.ShapeDtypeStruct((t, w), BF16),
                   jax.ShapeDtypeStruct((t, ATTN_WIDTH), BF16)),
        grid=(t // tm,),
        in_specs=[row(d), full(wq), full(wk), full(wv), row(LANES), full(pq), full(pk), full(oq), full(ok)],
        out_specs=(row(w), row(w), row(ATTN_WIDTH)),
        compiler_params=_cparams("parallel"),
    )(h, wq, wk, wv, fcum, pq, pk, oq, ok)


def _fold_lanes(x, op):
    out = x[:, :LANES]
    for g in range(1, x.shape[1] // LANES):
        out = op(out, x[:, g * LANES:(g + 1) * LANES])
    return out


def _attn_fwd(qa, ka, v, seq, tq, dep=None):
    t = qa.shape[0]
    nq = seq // tq
    hp_n = N_HEADS // 2
    heads = [slice(e * LANES, (e + 1) * LANES) for e in range(2)]

    def body(q_ref, k_ref, v_ref, o_ref, lse_ref, s_buf):
        i = pl.program_id(2)
        diag_ok = lax.broadcasted_iota(jnp.int32, (tq, tq), 0) >= lax.broadcasted_iota(jnp.int32, (tq, tq), 1)
        qs = [q_ref[:, hl] for hl in heads]

        def sweep1(j, mxs):
            r0 = pl.multiple_of(j * tq, tq)
            out = []
            for e, hl in enumerate(heads):
                s = _dot_nt(qs[e], k_ref[pl.ds(r0, tq), hl])
                s = jnp.where(jnp.logical_or(diag_ok, j < i), s, NEG_BIG)
                s_buf[e, j] = s
                out.append(jnp.maximum(mxs[e], _fold_lanes(s, jnp.maximum)))
            return tuple(out)

        mxs = lax.fori_loop(0, i + 1, sweep1, (jnp.full((tq, LANES), NEG_BIG, F32),) * 2)
        ms = [jnp.max(mx, axis=1, keepdims=True) for mx in mxs]

        def sweep2(j, carry):
            r0 = pl.multiple_of(j * tq, tq)
            vv = v_ref[pl.ds(r0, tq), :]
            out = []
            for e in range(2):
                p = jnp.exp(s_buf[e, j] - ms[e])
                out += [carry[2 * e] + _fold_lanes(p, jnp.add), carry[2 * e + 1] + _dot(p.astype(BF16), vv)]
            return tuple(out)

        res = lax.fori_loop(0, i + 1, sweep2, (jnp.zeros((tq, LANES), F32),) * 4)
        outs = []
        for e in range(2):
            l = jnp.sum(res[2 * e], axis=1, keepdims=True)
            outs.append(res[2 * e + 1] / l)
            lse_ref[:, e:e + 1] = ms[e] + jnp.log(l)
        lane = lax.broadcasted_iota(jnp.int32, (tq, LANES), 1)
        o_ref[...] = jnp.where(lane < HEAD_DIM, outs[0], outs[1])

    dep_specs, dep_ops = _dep_args(dep)
    return pl.pallas_call(
        _after(body, 3, dep), name="attn_fwd",
        out_shape=(jax.ShapeDtypeStruct((t, ATTN_WIDTH), F32), jax.ShapeDtypeStruct((hp_n, t, 2), F32)),
        grid=(t // seq, hp_n, nq),
        in_specs=[pl.BlockSpec((tq, 2 * LANES), lambda b, hp, i: (b * nq + i, hp)),
                  pl.BlockSpec((seq, 2 * LANES), lambda b, hp, i: (b, hp)),
                  pl.BlockSpec((seq, LANES), lambda b, hp, i: (b, hp))] + dep_specs,
        out_specs=(pl.BlockSpec((tq, LANES), lambda b, hp, i: (b * nq + i, hp)),
                   pl.BlockSpec((None, tq, 2), lambda b, hp, i: (hp, b * nq + i, 0))),
        scratch_shapes=[pltpu.VMEM((2, nq, tq, tq), F32)],
        compiler_params=_cparams("parallel", "parallel", "arbitrary"),
    )(qa, ka, v, *dep_ops)


def _merge_fwd(x, ps, o, g2, wpo, wao, wout, tm):
    t, d = x.shape
    tm = min(tm, t)

    def body(x_ref, ps_ref, o_ref, gp_ref, ga_ref, wpo_ref, wao_ref, wout_ref, mg_ref, x1_ref):
        py = _dot(ps_ref[...], wpo_ref[...])
        ay = _dot(o_ref[...].astype(BF16), wao_ref[...])
        mb = (_sigmoid(gp_ref[...].astype(F32)) * py + _sigmoid(ga_ref[...].astype(F32)) * ay).astype(BF16)
        mg_ref[...] = mb
        x1_ref[...] = x_ref[...] + _dot(mb, wout_ref[...])

    row = lambda w: pl.BlockSpec((tm, w), lambda i: (i, 0))
    full = lambda a: pl.BlockSpec(a.shape, lambda i: (0, 0))
    return pl.pallas_call(
        body, name="merge_fwd",
        out_shape=(jax.ShapeDtypeStruct((t, d), BF16), jax.ShapeDtypeStruct((t, d), F32)),
        grid=(t // tm,),
        in_specs=[row(d), row(POOL_WIDTH), row(ATTN_WIDTH), pl.BlockSpec((tm, d), lambda i: (i, 0)),
                  pl.BlockSpec((tm, d), lambda i: (i, 1)), full(wpo), full(wao), full(wout)],
        out_specs=(row(d), row(d)),
        compiler_params=_cparams("parallel"),
    )(x, ps, o, g2, g2, wpo, wao, wout)


def _ffn_fwd(x1, g, wg, wu, wd, tm, tf):
    t, d = x1.shape
    f = wg.shape[0]
    tm = min(tm, t)
    nf = f // tf

    def body(x1_ref, g_ref, wg_ref, wu_ref, wd_ref, h2_ref, gt_ref, up_ref, act_ref, x2_ref):
        j = pl.program_id(1)

        @pl.when(j == 0)
        def _():
            h2_ref[...] = _rms_fwd(x1_ref[...], g_ref[...]).astype(BF16)

        h2 = h2_ref[...]
        gt = _dot_nt(h2, wg_ref[...])
        up = _dot_nt(h2, wu_ref[...])
        sg = _sigmoid(gt)
        silu = gt * sg
        act = (silu * up).astype(BF16)
        gt_ref[...] = (up * (sg * (1.0 + gt * (1.0 - sg)))).astype(BF16)
        up_ref[...] = silu.astype(BF16)
        act_ref[...] = act
        prod = _dot(act, wd_ref[...])

        @pl.when(j == 0)
        def _():
            x2_ref[...] = prod

        @pl.when(j > 0)
        def _():
            x2_ref[...] += prod

        @pl.when(j == nf - 1)
        def _():
            x2_ref[...] += x1_ref[...]

    return pl.pallas_call(
        body, name="ffn_fwd",
        out_shape=(jax.ShapeDtypeStruct((t, d), BF16), jax.ShapeDtypeStruct((t, f), BF16),
                   jax.ShapeDtypeStruct((t, f), BF16), jax.ShapeDtypeStruct((t, f), BF16),
                   jax.ShapeDtypeStruct((t, d), F32)),
        grid=(t // tm, nf),
        in_specs=[pl.BlockSpec((tm, d), lambda i, j: (i, 0)), pl.BlockSpec((1, d), lambda i, j: (0, 0)),
                  pl.BlockSpec((tf, d), lambda i, j: (j, 0)), pl.BlockSpec((tf, d), lambda i, j: (j, 0)),
                  pl.BlockSpec((tf, d), lambda i, j: (j, 0))],
        out_specs=(pl.BlockSpec((tm, d), lambda i, j: (i, 0)), pl.BlockSpec((tm, tf), lambda i, j: (i, j)),
                   pl.BlockSpec((tm, tf), lambda i, j: (i, j)), pl.BlockSpec((tm, tf), lambda i, j: (i, j)),
                   pl.BlockSpec((tm, d), lambda i, j: (i, 0))),
        compiler_params=_cparams("parallel", "arbitrary"),
    )(x1, g, wg, wu, wd)


def _final_fwd_bwd(x2, target, g, tm):
    t, d = x2.shape
    tm = min(tm, t)

    def body(x_ref, t_ref, g_ref, loss_ref, dx_ref, dg_ref):
        i = pl.program_id(0)
        x = x_ref[...]
        gg = g_ref[...]
        err = _rms_fwd(x, gg) - t_ref[...]
        part = 0.5 * jnp.sum(jnp.mean(err * err, axis=-1, keepdims=True), axis=0, keepdims=True)
        dx, dg = _rms_bwd(x, gg, err * (1.0 / d))
        dx_ref[...] = dx

        @pl.when(i == 0)
        def _():
            loss_ref[...] = jnp.zeros_like(loss_ref)
            dg_ref[...] = jnp.zeros_like(dg_ref)

        loss_ref[...] += jnp.broadcast_to(part, loss_ref.shape)
        dg_ref[...] += dg

    return pl.pallas_call(
        body, name="final_fwd_bwd",
        out_shape=(jax.ShapeDtypeStruct((1, LANES), F32), jax.ShapeDtypeStruct((t, d), F32),
                   jax.ShapeDtypeStruct((1, d), F32)),
        grid=(t // tm,),
        in_specs=[pl.BlockSpec((tm, d), lambda i: (i, 0)), pl.BlockSpec((tm, d), lambda i: (i, 0)),
                  pl.BlockSpec((1, d), lambda i: (0, 0))],
        out_specs=(pl.BlockSpec((1, LANES), lambda i: (0, 0)), pl.BlockSpec((tm, d), lambda i: (i, 0)),
                   pl.BlockSpec((1, d), lambda i: (0, 0))),
        compiler_params=_cparams("arbitrary"),
    )(x2, target, g)


def _ffn_bwd(dx2, x1, g, gt, up, wg, wu, wd, tm, tf):
    t, d = dx2.shape
    f = gt.shape[1]
    tm = min(tm, t)
    nf = f // tf

    def body(dx2_ref, x1_ref, g_ref, gt_ref, up_ref, wg_ref, wu_ref, wd_ref, dgt_ref, dup_ref, dx1_ref, dg_ref, acc_ref,
             dxb_ref):
        i, j = pl.program_id(0), pl.program_id(1)

        @pl.when(j == 0)
        def _():
            dxb_ref[...] = dx2_ref[...].astype(BF16)

        dact = _dot_nt(dxb_ref[...], wd_ref[...])
        dgt = (dact * gt_ref[...].astype(F32)).astype(BF16)
        dup = (dact * up_ref[...].astype(F32)).astype(BF16)
        dgt_ref[...] = dgt
        dup_ref[...] = dup
        contrib = _dot(dgt, wg_ref[...]) + _dot(dup, wu_ref[...])

        @pl.when(j == 0)
        def _():
            acc_ref[...] = contrib

        @pl.when(j > 0)
        def _():
            acc_ref[...] += contrib

        @pl.when(jnp.logical_and(i == 0, j == 0))
        def _():
            dg_ref[...] = jnp.zeros_like(dg_ref)

        @pl.when(j == nf - 1)
        def _():
            dxn, dg = _rms_bwd(x1_ref[...], g_ref[...], acc_ref[...])
            dx1_ref[...] = dx2_ref[...] + dxn
            dg_ref[...] += dg

    return pl.pallas_call(
        body, name="ffn_bwd",
        out_shape=(jax.ShapeDtypeStruct((t, f), BF16), jax.ShapeDtypeStruct((t, f), BF16),
                   jax.ShapeDtypeStruct((t, d), F32), jax.ShapeDtypeStruct((1, d), F32)),
        grid=(t // tm, nf),
        in_specs=[pl.BlockSpec((tm, d), lambda i, j: (i, 0)), pl.BlockSpec((tm, d), lambda i, j: (i, 0)),
                  pl.BlockSpec((1, d), lambda i, j: (0, 0)),
                  pl.BlockSpec((tm, tf), lambda i, j: (i, j)), pl.BlockSpec((tm, tf), lambda i, j: (i, j)),
                  pl.BlockSpec((tf, d), lambda i, j: (j, 0)), pl.BlockSpec((tf, d), lambda i, j: (j, 0)),
                  pl.BlockSpec((tf, d), lambda i, j: (j, 0))],
        out_specs=(pl.BlockSpec((tm, tf), lambda i, j: (i, j)), pl.BlockSpec((tm, tf), lambda i, j: (i, j)),
                   pl.BlockSpec((tm, d), lambda i, j: (i, 0)), pl.BlockSpec((1, d), lambda i, j: (0, 0))),
        scratch_shapes=[pltpu.VMEM((tm, d), F32), pltpu.VMEM((tm, d), BF16)],
        compiler_params=_cparams("arbitrary", "arbitrary"),
    )(dx2, x1, g, gt, up, wg, wu, wd)


def _merge_bwd(dx1, ps, o, g2, wpo, wao, wout, tm, dep=None):
    t, d = dx1.shape
    tm = min(tm, t)

    def body(dx1_ref, ps_ref, o_ref, gp_ref, ga_ref, wpo_ref, wao_ref, wout_ref, dpy_ref, day_ref, dg2_ref, dps_ref, da_ref):
        dm = _dot_nt(dx1_ref[...].astype(BF16), wout_ref[...])
        py = _dot(ps_ref[...], wpo_ref[...])
        ay = _dot(o_ref[...].astype(BF16), wao_ref[...])
        sp = _sigmoid(gp_ref[...].astype(F32))
        sa = _sigmoid(ga_ref[...].astype(F32))
        dpy = (dm * sp).astype(BF16)
        day = (dm * sa).astype(BF16)
        dpy_ref[...] = dpy
        day_ref[...] = day
        dg2_ref[:, :d] = (dm * py * (sp * (1.0 - sp))).astype(BF16)
        dg2_ref[:, d:] = (dm * ay * (sa * (1.0 - sa))).astype(BF16)
        dps_ref[...] = _dot_nt(dpy, wpo_ref[...])
        da_ref[...] = _dot_nt(day, wao_ref[...]).astype(BF16)

    row = lambda w: pl.BlockSpec((tm, w), lambda i: (i, 0))
    full = lambda a: pl.BlockSpec(a.shape, lambda i: (0, 0))
    dep_specs, dep_ops = _dep_args(dep)
    return pl.pallas_call(
        _after(body, 8, dep), name="merge_bwd",
        out_shape=(jax.ShapeDtypeStruct((t, d), BF16), jax.ShapeDtypeStruct((t, d), BF16),
                   jax.ShapeDtypeStruct((t, 2 * d), BF16), jax.ShapeDtypeStruct((t, POOL_WIDTH), F32),
                   jax.ShapeDtypeStruct((t, ATTN_WIDTH), BF16)),
        grid=(t // tm,),
        in_specs=[row(d), row(POOL_WIDTH), row(ATTN_WIDTH), pl.BlockSpec((tm, d), lambda i: (i, 0)),
                  pl.BlockSpec((tm, d), lambda i: (i, 1)), full(wpo), full(wao), full(wout)] + dep_specs,
        out_specs=(row(d), row(d), row(2 * d), row(POOL_WIDTH), row(ATTN_WIDTH)),
        compiler_params=_cparams("parallel"),
    )(dx1, ps, o, g2, g2, wpo, wao, wout, *dep_ops)


def _attn_bwd(qa, ka, v, do, lse4, seq, tq, dep=None):
    t = qa.shape[0]
    nq = seq // tq
    hp_n = N_HEADS // 2
    heads = [slice(e * LANES, (e + 1) * LANES) for e in range(2)]

    def body(q_ref, k_ref, v_ref, do_ref, lse_ref, dq_ref, dk_ref, dv_ref, dfr_ref, dk_acc, dv_acc, p_buf, dp_buf):
        diag_ok = lax.broadcasted_iota(jnp.int32, (tq, tq), 0) >= lax.broadcasted_iota(jnp.int32, (tq, tq), 1)
        lane_q = lax.broadcasted_iota(jnp.int32, (tq, LANES), 1)
        lane_s = lax.broadcasted_iota(jnp.int32, (seq, LANES), 1)
        mine_q = [lane_q < HEAD_DIM, lane_q >= HEAD_DIM]
        dv_acc[...] = jnp.zeros_like(dv_acc)
        dk_acc[...] = jnp.zeros_like(dk_acc)
        dfr_ref[...] = jnp.zeros_like(dfr_ref)

        def q_step(i, _):
            q0 = pl.multiple_of(i * tq, tq)
            qs = [q_ref[pl.ds(q0, tq), hl] for hl in heads]
            dov = do_ref[pl.ds(q0, tq), :]
            dos = [jnp.where(mq, dov, jnp.zeros((), BF16)) for mq in mine_q]
            lss = [lse_ref[pl.ds(q0, tq), e:e + 1] for e in range(2)]

            def sweep1(j, dls):
                r0 = pl.multiple_of(j * tq, tq)
                vv = v_ref[pl.ds(r0, tq), :]
                out = []
                for e, hl in enumerate(heads):
                    s = _dot_nt(qs[e], k_ref[pl.ds(r0, tq), hl])
                    s = jnp.where(jnp.logical_or(diag_ok, j < i), s, NEG_BIG)
                    p = jnp.exp(s - lss[e])
                    dp = _dot_nt(dos[e], vv)
                    p_buf[e, j] = p
                    dp_buf[e, j] = dp
                    dv_acc[pl.ds(r0, tq), :] += _dot_tn(p.astype(BF16), dos[e])
                    out.append(dls[e] + _fold_lanes(p * dp, jnp.add))
                return tuple(out)

            dls = lax.fori_loop(0, i + 1, sweep1, (jnp.zeros((tq, LANES), F32),) * 2)
            dls = [jnp.sum(d, axis=1, keepdims=True) for d in dls]

            def sweep2(j, dqs):
                r0 = pl.multiple_of(j * tq, tq)
                out = []
                for e, hl in enumerate(heads):
                    ds = p_buf[e, j] * (dp_buf[e, j] - dls[e])
                    dfr_ref[e, pl.ds(j, 1), :] += jnp.sum(ds, axis=0, keepdims=True)
                    dsb = ds.astype(BF16)
                    dk_acc[e, pl.ds(r0, tq), :] += _dot_tn(dsb, qs[e])
                    out.append(dqs[e] + _dot(dsb, k_ref[pl.ds(r0, tq), hl]))
                return tuple(out)

            dqs = lax.fori_loop(0, i + 1, sweep2, (jnp.zeros((tq, LANES), F32),) * 2)
            dq = jnp.where(mine_q[0], dqs[0], pltpu.roll(dqs[1], HEAD_DIM, 1)) * ATTN_SCALE
            dq_ref[pl.ds(q0, tq), :] = dq.astype(BF16)
            return 0

        lax.fori_loop(0, nq, q_step, 0)
        dk_ref[...] = jnp.where(lane_s < HEAD_DIM, dk_acc[0], pltpu.roll(dk_acc[1], HEAD_DIM, 1)).astype(BF16)
        dv_ref[...] = dv_acc[...].astype(BF16)

    wide = pl.BlockSpec((seq, 2 * LANES), lambda b, hp: (b, hp))
    col = pl.BlockSpec((seq, LANES), lambda b, hp: (b, hp))
    pair = pl.BlockSpec((None, seq, 2), lambda b, hp: (hp, b, 0))
    dep_specs, dep_ops = _dep_args(dep)
    return pl.pallas_call(
        _after(body, 5, dep), name="attn_bwd",
        out_shape=(jax.ShapeDtypeStruct((t, ATTN_WIDTH), BF16),) * 3 + (jax.ShapeDtypeStruct((N_HEADS, t // tq, tq), F32),),
        grid=(t // seq, hp_n),
        in_specs=[wide, wide, col, col, pair] + dep_specs,
        out_specs=(col, col, col, pl.BlockSpec((2, nq, tq), lambda b, hp: (hp, b, 0))),
        scratch_shapes=[pltpu.VMEM((2, seq, LANES), F32), pltpu.VMEM((seq, LANES), F32),
                        pltpu.VMEM((2, nq, tq, tq), F32), pltpu.VMEM((2, nq, tq, tq), F32)],
        compiler_params=_cparams("parallel", "arbitrary"),
    )(qa, ka, v, do, lse4, *dep_ops)


def _forget_bwd(dfc, fl, bf, seq):
    t = fl.shape[0]
    cb = min(256, seq)
    nb = seq // cb

    def body(dfc_ref, fl_ref, bf_ref, dfl_ref, db_ref):
        b = pl.program_id(0)
        ri = lax.broadcasted_iota(jnp.int32, (cb, cb), 0)
        ci = lax.broadcasted_iota(jnp.int32, (cb, cb), 1)
        tri = (ci >= ri).astype(BF16)
        carry = jnp.zeros((1, LANES), F32)
        dbs = jnp.zeros((1, LANES), F32)
        for blk in reversed(range(nb)):
            rs = slice(blk * cb, (blk + 1) * cb)
            dlf = _tri_dot(tri, -dfc_ref[rs, :]) + carry
            carry = dlf[0:1, :]
            dfl = dlf * _sigmoid(-(fl_ref[rs, :] + bf_ref[...]))
            dfl_ref[rs, :] = dfl.astype(BF16)
            dbs = dbs + jnp.sum(dfl, axis=0, keepdims=True)

        @pl.when(b == 0)
        def _():
            db_ref[...] = jnp.zeros_like(db_ref)

        db_ref[...] += dbs

    return pl.pallas_call(
        body, name="forget_bwd",
        out_shape=(jax.ShapeDtypeStruct((t, LANES), BF16), jax.ShapeDtypeStruct((1, LANES), F32)),
        grid=(t // seq,),
        in_specs=[pl.BlockSpec((seq, LANES), lambda b: (b, 0)), pl.BlockSpec((seq, LANES), lambda b: (b, 0)),
                  pl.BlockSpec((1, LANES), lambda b: (0, 0))],
        out_specs=(pl.BlockSpec((seq, LANES), lambda b: (b, 0)), pl.BlockSpec((1, LANES), lambda b: (0, 0))),
        compiler_params=_cparams("arbitrary"),
    )(dfc, fl, bf)


def _pool_bwd(dps, p, mix, scale, seq):
    t = dps.shape[0]

    def body(dps_ref, p_ref, mix_ref, sc_ref, du_ref, dmix_ref, dsc_ref):
        b = pl.program_id(0)

        @pl.when(b == 0)
        def _():
            dmix_ref[...] = jnp.zeros_like(dmix_ref)
            dsc_ref[...] = jnp.zeros_like(dsc_ref)

        tpos = lax.broadcasted_iota(jnp.int32, (seq, POOL_GROUP_DIM), 0)
        for g in range(POOL_GROUPS):
            sl = slice(g * POOL_GROUP_DIM, (g + 1) * POOL_GROUP_DIM)
            pb = p_ref[:, sl]
            dpsg = dps_ref[:, sl]
            pm = _dot(pb, mix_ref[g])
            dsc_ref[:, sl] += jnp.sum(dpsg * pm, axis=0, keepdims=True)
            dpm = (dpsg * sc_ref[:, sl]).astype(BF16)
            dmix_ref[g] += _dot_tn(pb, dpm)
            dp = _dot_nt(dpm, mix_ref[g])
            cnt = jnp.minimum(tpos + 1, POOL_WINDOWS[g]).astype(F32)
            s = dp / cnt
            for lvl in range(g + 1):
                d = 2 ** lvl
                s = s + jnp.where(tpos < seq - d, pltpu.roll(s, seq - d, 0), 0.0)
            du_ref[:, sl] = (s - dp).astype(BF16)

    return pl.pallas_call(
        body, name="pool_bwd",
        out_shape=(jax.ShapeDtypeStruct((t, POOL_WIDTH), BF16),
                   jax.ShapeDtypeStruct((POOL_GROUPS, POOL_GROUP_DIM, POOL_GROUP_DIM), F32),
                   jax.ShapeDtypeStruct((1, POOL_WIDTH), F32)),
        grid=(t // seq,),
        in_specs=[pl.BlockSpec((seq, POOL_WIDTH), lambda b: (b, 0)), pl.BlockSpec((seq, POOL_WIDTH), lambda b: (b, 0)),
                  pl.BlockSpec((POOL_GROUPS, POOL_GROUP_DIM, POOL_GROUP_DIM), lambda b: (0, 0, 0)),
                  pl.BlockSpec((1, POOL_WIDTH), lambda b: (0, 0))],
        out_specs=(pl.BlockSpec((seq, POOL_WIDTH), lambda b: (b, 0)),
                   pl.BlockSpec((POOL_GROUPS, POOL_GROUP_DIM, POOL_GROUP_DIM), lambda b: (0, 0, 0)),
                   pl.BlockSpec((1, POOL_WIDTH), lambda b: (0, 0))),
        compiler_params=_cparams("arbitrary"),
    )(dps, p, mix, scale)


def _in_bwd(du, dq, dk, dv, dg2, dfl, dx1, x, g, wu, wqkv, wg2, wft, tm):
    t, d = x.shape
    tm = min(tm, t)
    aw = ATTN_WIDTH

    def body(du_ref, dq_ref, dk_ref, dv_ref, dg2_ref, dfl_ref, dx1_ref, x_ref, g_ref, wu_ref, wqkv_ref, wg2_ref, wft_ref,
             dx_ref, dg_ref):
        i = pl.program_id(0)
        dh = _dot(du_ref[...], wu_ref[...])
        dh += _dot(dq_ref[...], wqkv_ref[0:aw, :])
        dh += _dot(dk_ref[...], wqkv_ref[aw:2 * aw, :])
        dh += _dot(dv_ref[...], wqkv_ref[2 * aw:3 * aw, :])
        dh += _dot(dg2_ref[...], wg2_ref[...])
        dh += _dot(dfl_ref[...], wft_ref[...])
        dxn, dg = _rms_bwd(x_ref[...], g_ref[...], dh)
        dx_ref[...] = dx1_ref[...] + dxn

        @pl.when(i == 0)
        def _():
            dg_ref[...] = jnp.zeros_like(dg_ref)

        dg_ref[...] += dg

    row = lambda w: pl.BlockSpec((tm, w), lambda i: (i, 0))
    full = lambda a: pl.BlockSpec(a.shape, lambda i: (0, 0))
    return pl.pallas_call(
        body, name="in_bwd",
        out_shape=(jax.ShapeDtypeStruct((t, d), F32), jax.ShapeDtypeStruct((1, d), F32)),
        grid=(t // tm,),
        in_specs=[row(POOL_WIDTH), row(aw), row(aw), row(aw), row(2 * d), row(LANES), row(d), row(d),
                  pl.BlockSpec((1, d), lambda i: (0, 0)), full(wu), full(wqkv), full(wg2), full(wft)],
        out_specs=(row(d), pl.BlockSpec((1, d), lambda i: (0, 0))),
        compiler_params=_cparams("arbitrary"),
    )(du, dq, dk, dv, dg2, dfl, dx1, x, g, wu, wqkv, wg2, wft)


def _position():
    return lax.axis_index("x"), lax.axis_index("y"), lax.axis_index("c")


def _remote(src, dst, send_sem, recv_sem, device):
    return pltpu.make_async_remote_copy(src_ref=src, dst_ref=dst, send_sem=send_sem, recv_sem=recv_sem,
                                        device_id=device, device_id_type=MESH)


HBM = pl.BlockSpec(memory_space=pltpu.HBM)
SEM = pl.BlockSpec(memory_space=pltpu.SEMAPHORE)
DATAFLOW = pltpu.SideEffectType.DATAFLOW_SIDE_EFFECTING


def _copies_start(name, arrays, plan, m, dep=None):
    n = len(arrays)
    arrays = [pltpu.with_memory_space_constraint(a, pltpu.HBM) for a in arrays]

    def body(*refs):
        ins, send_sem, recv_sem, token = refs[:n], refs[n], refs[n + 1], refs[2 * n + 2]
        for i, (src, dst, device, _) in enumerate(plan(ins, *_position())):
            _remote(src, dst, send_sem.at[i], recv_sem.at[i], device).start()
        token[...] = jnp.zeros_like(token)

    dep_specs, dep_ops = _dep_args(dep)
    outs = pl.pallas_call(
        _after(body, n, dep), name=name,
        out_shape=(pltpu.SemaphoreType.DMA((m,)), pltpu.SemaphoreType.DMA((m,)),
                   *[pltpu.HBM(a.shape, a.dtype) for a in arrays], jax.ShapeDtypeStruct((8, LANES), F32)),
        in_specs=[HBM] * n + dep_specs, out_specs=(SEM, SEM, *[HBM] * n, pl.BlockSpec(memory_space=pltpu.VMEM)),
        input_output_aliases={i: i + 2 for i in range(n)},
        compiler_params=pltpu.CompilerParams(has_side_effects=DATAFLOW),
    )(*arrays, *dep_ops)
    return (outs[0], outs[1]), list(outs[2:2 + n]), outs[2 + n]


def _copies_wait(name, sems, arrays, plan, after):
    n = len(arrays)
    afters = list(after) if isinstance(after, (list, tuple)) else [after]

    def body(*refs):
        ins, send_sem, recv_sem = refs[:n], refs[n], refs[n + 1]
        for i, (src, dst, device, landing) in enumerate(plan(ins, *_position())):
            _remote(src, dst, send_sem.at[i], recv_sem.at[i], device).wait_send()
            _remote(landing, landing, send_sem.at[i], recv_sem.at[i], device).wait_recv()

    outs = pl.pallas_call(
        body, name=name,
        out_shape=tuple(pltpu.HBM(a.shape, a.dtype) for a in arrays),
        in_specs=[HBM] * n + [SEM, SEM] + [ANY] * len(afters), out_specs=tuple([HBM] * n),
        input_output_aliases={i: i for i in range(n)},
        compiler_params=pltpu.CompilerParams(has_side_effects=DATAFLOW),
    )(*arrays, sems[0], sems[1], *afters)
    return list(outs)


def _tie(x, dep):
    for token in _dep_list(dep):
        x = x + token[0, 0]
    return x


def _other_chips(x, y):
    return [(1 - x, y), (x, 1 - y), (1 - x, 1 - y)]


def _gather_begin(tag, shards, token):
    n = len(shards)
    lands = [lax.empty((N_CHIPS,) + s.shape, s.dtype) for s in shards]

    def plan(refs, x, y, c):
        return [(refs[k].at[c], refs[n + k].at[2 * x + y, c], (ox, oy, c), refs[n + k].at[2 * ox + oy, c])
                for k in range(n) for ox, oy in _other_chips(x, y)]

    sems, thru, token = _copies_start(f"gather_{tag}_ici_start", list(shards) + lands, plan, 3 * n, dep=token)
    return dict(tag=tag, n=n, plan=plan, sems=sems, arrays=thru, token=token)


def _gather_forward(st, after):
    n, tag = st["n"], st["tag"]
    thru = _copies_wait(f"gather_{tag}_ici_wait", st["sems"], st["arrays"], st["plan"], after)

    def plan(refs, x, y, c):
        return [(refs[k].at[2 * ox + oy, c], refs[k].at[2 * ox + oy, c], (x, y, 1 - c), refs[k].at[2 * ox + oy, 1 - c])
                for k in range(n) for ox, oy in _other_chips(x, y)]

    sems, lands, token = _copies_start(f"gather_{tag}_fwd_start", thru[n:], plan, 3 * n)
    return dict(tag=tag, n=n, plan=plan, sems=sems, arrays=lands, token=token, shards=thru[:n])


def _gather_end(st, after):
    lands = _copies_wait(f"gather_{st['tag']}_fwd_wait", st["sems"], st["arrays"], st["plan"], after)
    me = 2 * lax.axis_index("x") + lax.axis_index("y")
    return [lax.dynamic_update_index_in_dim(g, s, me, 0) for g, s in zip(lands, st["shards"])]


def _add_keep_give(name, pos, a, a_keep, a_give, b, b_keep, b_give, steps):
    r, c = b.shape[-2:]

    def spec(arr, fn):
        lead = arr.ndim - 2

        def index(i, p):
            idx = tuple(fn(i, p))
            return idx if len(idx) == arr.ndim else idx + (0, 0)

        return pl.BlockSpec((None,) * lead + (r, c), index)

    out_spec = pl.BlockSpec((None, r, c), lambda i, p: (i, 0, 0))

    def body(p_ref, ak_ref, bk_ref, ag_ref, bg_ref, keep_ref, give_ref):
        keep_ref[...] = ak_ref[...] + bk_ref[...].astype(F32)
        give_ref[...] = (ag_ref[...] + bg_ref[...].astype(F32)).astype(BF16)

    return pl.pallas_call(
        body, name=name,
        out_shape=(jax.ShapeDtypeStruct((steps, r, c), F32), jax.ShapeDtypeStruct((steps, r, c), BF16)),
        grid_spec=pltpu.PrefetchScalarGridSpec(
            num_scalar_prefetch=1, grid=(steps,),
            in_specs=[spec(a, a_keep), spec(b, b_keep), spec(a, a_give), spec(b, b_give)],
            out_specs=(out_spec, out_spec)),
        compiler_params=_cparams("parallel"),
    )(pos, a, b, a, b)


def _add_last(name, a, b):
    _, r, c = a.shape
    blk = pl.BlockSpec((None, r, c), lambda i: (0, 0, 0))

    def body(a_ref, b_ref, o_ref):
        o_ref[...] = a_ref[...] + b_ref[...].astype(F32)

    return pl.pallas_call(
        body, name=name, out_shape=jax.ShapeDtypeStruct((r, c), F32), grid=(1,), in_specs=[blk, blk],
        out_specs=pl.BlockSpec((r, c), lambda i: (0, 0)), compiler_params=_cparams("arbitrary"),
    )(a, b)


def _exchange_begin(tag, stage, gives, lands, peer_fn, extra):
    n = len(gives)

    def plan(refs, x, y, c):
        return [(refs[k], refs[n + k], peer_fn(x, y, c), refs[n + k]) for k in range(n)]

    sems, thru, token = _copies_start(f"rs{tag}_{stage}_start", gives + lands, plan, n)
    return dict(extra, tag=tag, n=n, stage=stage, plan=plan, sems=sems, arrays=thru, token=token)


def _reduce_begin(tag, grads, column_halves=False):
    n = len(grads)
    if column_halves:
        half = lambda ref, j, h: ref.at[j, :, pl.ds(pl.multiple_of(h * (ref.shape[2] // 2), LANES), ref.shape[2] // 2)]
        lands = [lax.empty((N_CHIPS, g.shape[1], g.shape[2] // 2), F32) for g in grads]
    else:
        half = lambda ref, j, h: ref.at[j, h]
        lands = [lax.empty((N_CHIPS,) + g.shape[2:], F32) for g in grads]

    def plan(refs, x, y, c):
        return [(half(refs[k], j, 1 - c), refs[n + k].at[j], (x, y, 1 - c), refs[n + k].at[j])
                for k in range(n) for j in range(N_CHIPS)]

    sems, thru, token = _copies_start(f"rs{tag}_c_start", list(grads) + lands, plan, N_CHIPS * n)
    return dict(tag=tag, n=n, stage="c", plan=plan, sems=sems, arrays=thru, token=token, column_halves=column_halves)


def _reduce_advance(st, after):
    tag, n, stage = st["tag"], st["n"], st["stage"]
    thru = _copies_wait(f"rs{tag}_{stage}_wait", st["sems"], st["arrays"], st["plan"], after)
    first, recv = thru[:n], thru[n:]
    x, y, c = _position()
    if stage == "c":
        pos = jnp.stack([c, x]).astype(jnp.int32)
        if st["column_halves"]:
            mine = lambda chip: (lambda i, p: (chip(p) + i, 0, p[0]))
        else:
            mine = lambda chip: (lambda i, p: (chip(p) + i, p[0]))
        sums = [_add_keep_give(
            f"rs{tag}_c_add{k}", pos,
            first[k], mine(lambda p: 2 * p[1]), mine(lambda p: 2 * (1 - p[1])),
            recv[k], lambda i, p: (2 * p[1] + i,), lambda i, p: (2 * (1 - p[1]) + i,), 2) for k in range(n)]
        lands = [lax.empty(s[1].shape, BF16) for s in sums]
        return _exchange_begin(tag, "x", [s[1] for s in sums], lands, lambda x, y, c: (1 - x, y, c),
                               dict(keep=[s[0] for s in sums]))
    if stage == "x":
        pos = jnp.stack([y]).astype(jnp.int32)
        sums = [_add_keep_give(
            f"rs{tag}_x_add{k}", pos,
            st["keep"][k], lambda i, p: (p[0],), lambda i, p: (1 - p[0],),
            recv[k], lambda i, p: (p[0],), lambda i, p: (1 - p[0],), 1) for k in range(n)]
        lands = [lax.empty(s[1].shape, BF16) for s in sums]
        return _exchange_begin(tag, "y", [s[1] for s in sums], lands, lambda x, y, c: (x, 1 - y, c),
                               dict(keep=[s[0] for s in sums]))
    if stage == "y":
        mine = [_add_last(f"rs{tag}_y_add{k}", st["keep"][k], recv[k]) for k in range(n)]
        lands = [lax.empty(m.shape, F32) for m in mine]
        return _exchange_begin(tag, "swap", mine, lands, lambda x, y, c: (x, y, 1 - c), {})
    return dict(done=list(zip(first, recv)), token=None)


def _all_reduce_small(v):
    r = v.shape[0]

    def body(v_ref, out_ref, buf, send_sems, recv_sems, local_sem):
        x, y, c = _position()
        me, sibling = (x, y, c), (x, y, 1 - c)
        chips = [(1 - x, y), (x, 1 - y), (1 - x, 1 - y)]

        def rows(px, py, pc):
            return buf.at[pl.ds((4 * px + 2 * py + pc) * r, r), :]

        def copy(k, block, to, src=None):
            return _remote(rows(*block) if src is None else src, rows(*block), send_sems.at[k], recv_sems.at[k], to)

        mine = pltpu.make_async_copy(v_ref, rows(*me), local_sem)
        mine.start()
        first = [copy(0, me, sibling, src=v_ref)]
        first += [copy(1 + j, me, (*chip, c), src=v_ref) for j, chip in enumerate(chips)]
        for cp in first:
            cp.start()
        passed = [copy(4 + j, (*chip, c), sibling) for j, chip in enumerate(chips)]
        for j, chip in enumerate(chips):
            copy(1 + j, (*chip, c), me).wait_recv()
            passed[j].start()
        copy(0, sibling, me).wait_recv()
        for j, chip in enumerate(chips):
            copy(4 + j, (*chip, 1 - c), me).wait_recv()
        for cp in first + passed:
            cp.wait_send()
        mine.wait()
        acc = buf[0:r, :]
        for dev in range(1, N_DEV):
            acc = acc + buf[dev * r:(dev + 1) * r, :]
        out_ref[...] = acc

    return pl.pallas_call(
        body, name="all_reduce_small",
        out_shape=jax.ShapeDtypeStruct(v.shape, F32),
        in_specs=[pl.BlockSpec(memory_space=pltpu.VMEM)],
        out_specs=pl.BlockSpec(memory_space=pltpu.VMEM),
        scratch_shapes=[pltpu.VMEM((N_DEV * r, LANES), F32), pltpu.SemaphoreType.DMA((7,)),
                        pltpu.SemaphoreType.DMA((7,)), pltpu.SemaphoreType.DMA],
        compiler_params=pltpu.CompilerParams(has_side_effects=True, vmem_limit_bytes=VMEM_LIMIT_V7X),
    )(v)


def _adamw_update(w, gg, m, v):
    mn = ADAM_B1 * m + (1.0 - ADAM_B1) * gg
    vn = ADAM_B2 * v + (1.0 - ADAM_B2) * (gg * gg)
    m_hat = mn / (1.0 - ADAM_B1 ** ADAM_STEP)
    v_hat = vn / (1.0 - ADAM_B2 ** ADAM_STEP)
    return -ADAM_LR * (m_hat / (jnp.sqrt(v_hat) + ADAM_EPS) + ADAM_WD * w), mn, vn


def _adamw(name, w, g, m, v):
    def body(w_ref, g_ref, m_ref, v_ref, d_ref, mo_ref, vo_ref):
        d_ref[...], mo_ref[...], vo_ref[...] = _adamw_update(w_ref[...], g_ref[...], m_ref[...], v_ref[...])

    blk = pl.BlockSpec(w.shape, lambda i: (0, 0))
    return pl.pallas_call(
        body, name=name, out_shape=(jax.ShapeDtypeStruct(w.shape, F32),) * 3, grid=(1,),
        in_specs=[blk] * 4, out_specs=(blk,) * 3, compiler_params=_cparams("arbitrary"),
    )(w, g, m, v)


def _rows_to_bf16_halves(name, w):
    r, _, c = w.shape
    ch = c // 2

    def body(w_ref, o_ref):
        wb = w_ref[:, 0, :].astype(BF16)
        o_ref[0] = wb[:, :ch]
        o_ref[1] = wb[:, ch:]

    return pl.pallas_call(
        body, name=name, out_shape=jax.ShapeDtypeStruct((2, r, ch), BF16), grid=(1,),
        in_specs=[pl.BlockSpec((r, 1, c), lambda i: (0, 0, 0))], out_specs=pl.BlockSpec((2, r, ch), lambda i: (0, 0, 0)),
        compiler_params=_cparams("arbitrary"),
    )(w)


def _adamw_rows(name, pos_c, w, g_mine, g_other, m, v):
    r, _, c = w.shape
    ch = c // 2

    def body(p_ref, w_ref, gm_ref, go_ref, m_ref, v_ref, g_ref, d_ref, mo_ref, vo_ref):
        gg = jnp.where(pl.program_id(0) == p_ref[0], gm_ref[...], go_ref[...])
        dl, mn, vn = _adamw_update(w_ref[:, 0, :], gg, m_ref[:, 0, :], v_ref[:, 0, :])
        g_ref[:, 0, :] = gg
        d_ref[:, 0, :] = dl
        mo_ref[:, 0, :] = mn
        vo_ref[:, 0, :] = vn

    rows = pl.BlockSpec((r, 1, ch), lambda h, p: (0, 0, h))
    half = pl.BlockSpec((r, ch), lambda h, p: (0, 0))
    return pl.pallas_call(
        body, name=name, out_shape=(jax.ShapeDtypeStruct(w.shape, F32),) * 4,
        grid_spec=pltpu.PrefetchScalarGridSpec(
            num_scalar_prefetch=1, grid=(2,), in_specs=[rows, half, half, rows, rows], out_specs=(rows,) * 4),
        compiler_params=_cparams("parallel"),
    )(pos_c, w, g_mine, g_other, m, v)


def _adamw_halves(name, pos_c, w, g_mine, g_other, m, v, tr, dep=None):
    r, c = w.shape
    rh = r // 2
    tr = tr if rh % tr == 0 else rh
    nt = rh // tr

    def body(p_ref, w_ref, gm_ref, go_ref, m_ref, v_ref, g_ref, d_ref, mo_ref, vo_ref):
        gg = jnp.where(pl.program_id(0) == p_ref[0], gm_ref[...], go_ref[...])
        g_ref[...] = gg
        d_ref[...], mo_ref[...], vo_ref[...] = _adamw_update(w_ref[...], gg, m_ref[...], v_ref[...])

    full = pl.BlockSpec((tr, c), lambda h, i, p: (h * nt + i, 0))
    half = pl.BlockSpec((tr, c), lambda h, i, p: (i, 0))
    dep_specs, dep_ops = _dep_args(dep)
    return pl.pallas_call(
        _after(body, 6, dep), name=name, out_shape=(jax.ShapeDtypeStruct((r, c), F32),) * 4,
        grid_spec=pltpu.PrefetchScalarGridSpec(
            num_scalar_prefetch=1, grid=(2, nt),
            in_specs=[full, half, half, full, full] + dep_specs, out_specs=(full,) * 4),
        compiler_params=_cparams("parallel", "parallel"),
    )(pos_c, w, g_mine, g_other, m, v, *dep_ops)


def _col_sharded_to_comm(g):
    k, n = g.shape
    return g.reshape(2, k // 2, N_CHIPS, n // N_CHIPS).transpose(2, 0, 1, 3)


def _row_sharded_to_comm(g):
    r, c = g.shape
    return g.reshape(N_CHIPS, 2, r // (2 * N_CHIPS), c)


def _col_sharded_full(g):
    _, _, rh, c = g.shape
    return g.reshape(N_CHIPS, 2 * rh, c).transpose(1, 0, 2).reshape(2 * rh, N_CHIPS * c)


def _row_sharded_full(g):
    _, _, rh, c = g.shape
    return g.reshape(N_CHIPS * 2 * rh, c)


def _pack_small(g1, bfv, mix, scale, g2n, gf, extra=None):
    row8 = jnp.pad(bfv.reshape(1, N_HEADS), ((0, 0), (0, LANES - N_HEADS)))
    if extra is not None:
        row8 = row8 + jnp.pad(extra[:, :1], ((0, 0), (N_HEADS, LANES - N_HEADS - 1)))
    return jnp.concatenate([
        g1.reshape(8, LANES), jnp.pad(row8, ((0, 7), (0, 0))), mix.reshape(512, LANES),
        jnp.pad(scale.reshape(4, LANES), ((0, 4), (0, 0))), g2n.reshape(8, LANES), gf.reshape(8, LANES)], axis=0)


def _unpack_small(s, like):
    g1, bfv, mix, scale, g2n, gf = like
    return (s[0:8].reshape(g1.shape), s[8, :N_HEADS].reshape(bfv.shape), s[16:528].reshape(mix.shape),
            s[528:532].reshape(scale.shape), s[536:544].reshape(g2n.shape), s[544:552].reshape(gf.shape))


class _MeshLinks:
    def __init__(self, shards_in, shards_rest):
        self.gin = _gather_begin("in", shards_in, None)
        self.grest = _gather_begin("rest", shards_rest, self.gin["token"])
        self.tokens = {"gather": self.grest["token"]}
        self.groups = {}

    @property
    def token(self):
        return list(self.tokens.values())

    def tie(self, x):
        return _tie(x, self.token)

    def weights_in(self, after):
        st = _gather_forward(self.gin, after)
        (g,) = _gather_end(st, st["token"])
        return g.transpose(0, 2, 1, 3).reshape(N_CHIPS * g.shape[2], 2 * g.shape[3])

    def rest_forward(self, after):
        self.grest = _gather_forward(self.grest, after)
        self.tokens["gather"] = self.grest["token"]

    def weights_rest(self, after):
        g = _gather_end(self.grest, after)
        del self.tokens["gather"]
        return [_col_sharded_full(g[0]), _col_sharded_full(g[1])] + [_row_sharded_full(a) for a in g[2:]]

    def reduce_begin(self, tag, grads, column_halves=False):
        self.groups[tag] = _reduce_begin(tag, grads, column_halves)
        self.tokens[tag] = self.groups[tag]["token"]

    def advance(self, after):
        for tag, st in self.groups.items():
            if "done" not in st:
                self.groups[tag] = _reduce_advance(st, after)
                if self.groups[tag]["token"] is None:
                    del self.tokens[tag]
                else:
                    self.tokens[tag] = self.groups[tag]["token"]

    def reduced(self, tag):
        return self.groups[tag]["done"]


class _NoLinks:
    token = None

    def __init__(self, w_in, rest):
        self.w_in, self.rest, self.grads = w_in, rest, {}

    def tie(self, x):
        return x

    def weights_in(self, after):
        return self.w_in

    def rest_forward(self, after):
        pass

    def weights_rest(self, after):
        return self.rest

    def reduce_begin(self, tag, grads, column_halves=False):
        self.grads[tag] = grads

    def advance(self, after):
        pass


def _local_step(links, x, target, seq, norm1_g, b_forget, pool_mix, pool_scale, norm2_g, norm_f_g):
    t, d = x.shape
    tq = min(256, seq)
    aw = ATTN_WIDTH
    o_q, o_f, o_g = POOL_WIDTH, POOL_WIDTH + 3 * aw, POOL_WIDTH + 3 * aw + N_HEADS
    bf = jnp.pad(b_forget, ((0, 0), (0, LANES - N_HEADS)))
    mixb = pool_mix.astype(BF16)

    h = _norm_fwd("norm1_fwd", x, links.tie(norm1_g), 512)
    w_in = links.weights_in(h)
    wu = w_in[:o_q]
    wqkv = w_in[o_q:o_f]
    wft = jnp.pad(w_in[o_f:o_g], ((0, LANES - N_HEADS), (0, 0)))
    wg2 = w_in[o_g:]
    wf = wft.T
    u = _matmul("mm_u", h, wu, "nt", F32, 1024, 512, d)
    g2 = _matmul("mm_gates", h, wg2, "nt", BF16, 1024, 512, d)
    fl, fcum = _forget_fwd(h, wf, bf, seq)
    qa, ka, v = _attn_prep(h, _head_blocks(wqkv[:aw]), _head_blocks(wqkv[aw:2 * aw]), wqkv[2 * aw:], fcum, 512)
    p, ps = _pool_fwd(u, mixb, pool_scale, seq)
    links.rest_forward([ps, qa, g2])
    o, lse = _attn_fwd(qa, ka, v, seq, tq, dep=links.token)
    w_pool_out, w_attn_out, w_out, w_ffn_gate, w_ffn_up, w_ffn_down = links.weights_rest(o)
    merged, x1 = _merge_fwd(x, ps, o, g2, w_pool_out, w_attn_out, w_out, 256)
    h2, gt, up, act, x2 = _ffn_fwd(x1, norm2_g, w_ffn_gate, w_ffn_up, w_ffn_down, 1024, 256)
    loss, dx2, d_gf = _final_fwd_bwd(x2, target, norm_f_g, 512)

    dgt, dup, dx1, d_g2n = _ffn_bwd(dx2, x1, norm2_g, gt, up, w_ffn_gate, w_ffn_up, w_ffn_down, 1024, 256)
    d_wd = _matmul("dw_down", act, dx2, "tn", F32, 1408, 1024, 1024)
    d_wg = _matmul("dw_gate", dgt, h2, "tn", F32, 1408, 1024, 1024)
    d_wu = _matmul("dw_up", dup, h2, "tn", F32, 1408, 1024, 1024)
    links.reduce_begin("a", [_row_sharded_to_comm(g) for g in (d_wg, d_wu, d_wd)])
    dpy, day, dg2, dps, da = _merge_bwd(dx1, ps, o, g2, w_pool_out, w_attn_out, w_out, 256, dep=links.token)
    links.advance(dps)
    d_wout = _matmul("dw_out", merged, dx1, "tn", F32, 1024, 1024, 1024)
    d_wpo = _matmul("dw_pool_out", ps, dpy, "tn", F32, 512, 1024, 1024)
    d_wao = _matmul("dw_attn_out", o, day, "tn", F32, 512, 1024, 1024)
    links.reduce_begin("m", [_col_sharded_to_comm(d_wpo), _col_sharded_to_comm(d_wao), _row_sharded_to_comm(d_wout)])
    dq, dk, dv, dfr = _attn_bwd(qa, ka, v, da, lse, seq, tq, dep=links.token)
    links.advance(dq)
    dfc = jnp.pad(dfr.reshape(N_HEADS, t).T, ((0, 0), (0, LANES - N_HEADS)))
    dfl, d_bf = _forget_bwd(dfc, fl, bf, seq)
    du, d_mix, d_scale = _pool_bwd(dps, p, mixb, links.tie(pool_scale), seq)
    d_wu_in = _matmul("dw_in_u", du, h, "tn", F32, 512, 1024, 1024)
    d_wq = _matmul("dw_in_q", dq, h, "tn", F32, 512, 1024, 1024)
    d_wk = _matmul("dw_in_k", dk, h, "tn", F32, 512, 1024, 1024)
    d_wv = _matmul("dw_in_v", dv, h, "tn", F32, 512, 1024, 1024)
    links.advance([d_wu_in, d_wq, d_wk, d_wv])
    d_wf = _matmul("dw_in_f", dfl, h, "tn", F32, LANES, 1024, 512)
    d_wg2 = _matmul("dw_in_gates", dg2, h, "tn", F32, 1024, 1024, 1024, dep=links.token)
    d_win = jnp.concatenate([d_wu_in, d_wq, d_wk, d_wv, d_wf[:N_HEADS], d_wg2], axis=0)
    comm_b = [d_win.reshape(N_CHIPS, d_win.shape[0] // N_CHIPS, d)]
    links.advance(comm_b)
    links.reduce_begin("b", comm_b, column_halves=True)
    dx, d_g1 = _in_bwd(du, dq, dk, dv, dg2, dfl, dx1, x, links.tie(norm1_g), wu, wqkv, wg2, wft, 256)
    links.advance(dx)
    small = (d_g1, d_bf[:, :N_HEADS], d_mix, d_scale, d_g2n, d_gf)
    return loss, dx, small


def kernel(x, norm1_g, w_in, b_forget, pool_mix, pool_scale, w_pool_out, w_attn_out, w_out, norm2_g, w_ffn_gate, w_ffn_up, w_ffn_down, norm_f_g, loss_target, m_norm1_g, m_w_in, m_b_forget, m_pool_mix, m_pool_scale, m_w_pool_out, m_w_attn_out, m_w_out, m_norm2_g, m_w_ffn_gate, m_w_ffn_up, m_w_ffn_down, m_norm_f_g, v_norm1_g, v_w_in, v_b_forget, v_pool_mix, v_pool_scale, v_w_pool_out, v_w_attn_out, v_w_out, v_norm2_g, v_w_ffn_gate, v_w_ffn_up, v_w_ffn_down, v_norm_f_g):
    nb, seq, d = x.shape
    group_a = ((w_ffn_gate, m_w_ffn_gate, v_w_ffn_gate, True, 9), (w_ffn_up, m_w_ffn_up, v_w_ffn_up, True, 10),
               (w_ffn_down, m_w_ffn_down, v_w_ffn_down, False, 11))
    group_m = ((w_pool_out, m_w_pool_out, v_w_pool_out, False, 5), (w_attn_out, m_w_attn_out, v_w_attn_out, False, 6),
               (w_out, m_w_out, v_w_out, False, 7))
    group_b = ((w_in, m_w_in, v_w_in, False, 1),)
    small_w = (norm1_g, b_forget, pool_mix, pool_scale, norm2_g, norm_f_g)
    small_m = (m_norm1_g, m_b_forget, m_pool_mix, m_pool_scale, m_norm2_g, m_norm_f_g)
    small_v = (v_norm1_g, v_b_forget, v_pool_mix, v_pool_scale, v_norm2_g, v_norm_f_g)
    small_pos = (0, 2, 3, 4, 8, 12)
    view = lambda a, tr: a[0].T if tr else a[0]
    unview = lambda a, tr, like: (a.T if tr else a).reshape(like.shape)

    def shard(w, tr):
        lw = view(w, tr).astype(BF16)
        return lw.reshape(2, lw.shape[0] // 2, lw.shape[1])

    cm = lambda a: jnp.transpose(a, (2, 0, 1))
    rows_in, _, cols_in = cm(w_in).shape
    shard_in = _rows_to_bf16_halves("w_in_to_bf16", cm(w_in))
    links = _MeshLinks([shard_in],
                       [shard(w_pool_out, False), shard(w_attn_out, False), shard(w_out, False),
                        shard(w_ffn_gate, True), shard(w_ffn_up, True), shard(w_ffn_down, False)])
    loss, dx, small_g = _local_step(
        links, x.reshape(nb * seq, d), loss_target.reshape(nb * seq, d), seq,
        norm1_g, b_forget, pool_mix[0], pool_scale, norm2_g, norm_f_g.reshape(1, d))

    grads, deltas, new_m, new_v = [None] * 13, [None] * 13, [None] * 13, [None] * 13
    pos_c = jnp.stack([lax.axis_index("c")]).astype(jnp.int32)

    def update(tag, group, dep):
        last = []
        for k, ((w, m, v, tr, pos), (mine, other)) in enumerate(zip(group, links.reduced(tag))):
            outs = _adamw_halves(f"adamw_{tag}{k}", pos_c, view(w, tr), mine, other, view(m, tr), view(v, tr), 256,
                                 dep=dep)
            grads[pos], deltas[pos], new_m[pos], new_v[pos] = (unview(a, tr, w) for a in outs)
            last.append(outs[1])
        return last

    last = update("a", group_a, links.token) + update("m", group_m, links.token)
    links.advance(last)
    small_sum = _all_reduce_small(links.tie(_pack_small(*small_g, extra=loss)))
    loss_out = small_sum[8, N_HEADS]
    dl, mn, vn = _adamw("adamw_small", _pack_small(*small_w), small_sum * _small_mask(), _pack_small(*small_m),
                        _pack_small(*small_v))
    for pos, g, a, b, e in zip(small_pos, _unpack_small(small_sum, small_w), _unpack_small(dl, small_w),
                               _unpack_small(mn, small_w), _unpack_small(vn, small_w)):
        grads[pos], deltas[pos], new_m[pos], new_v[pos] = g, a, b, e
    links.advance(dl)
    links.advance(links.token)
    (mine, other), = links.reduced("b")
    outs = _adamw_rows("adamw_b0", pos_c, cm(w_in), mine, other, cm(m_w_in), cm(v_w_in))
    grads[1], deltas[1], new_m[1], new_v[1] = (jnp.transpose(a, (1, 2, 0)) for a in outs)

    return (loss_out, dx.reshape(nb, seq, d), *grads, *deltas, *new_m, *new_v)


def _small_mask():
    rows = lax.broadcasted_iota(jnp.int32, (552, LANES), 0)
    lanes = lax.broadcasted_iota(jnp.int32, (552, LANES), 1)
    return jnp.where(jnp.logical_and(rows == 8, lanes == N_HEADS), 0.0, 1.0).astype(F32)
```

```python
import functools

import jax
import jax.numpy as jnp
from jax import lax
from jax.experimental import pallas as pl
from jax.experimental.pallas import tpu as pltpu

F32 = jnp.float32
BF16 = jnp.bfloat16

D_MODEL = 1024
POOL_WINDOWS = (2, 4, 8, 16)
POOL_GROUPS = 4
POOL_GROUP_DIM = 128
POOL_WIDTH = 512
HEAD_DIM = 64
N_HEADS = 8
ATTN_WIDTH = 512
D_FF = 2816
RMS_EPS = 1e-6
ATTN_SCALE = HEAD_DIM ** -0.5
NEG_BIG = -1e30

ADAM_LR = 0.001
ADAM_B1 = 0.9
ADAM_B2 = 0.999
ADAM_EPS = 1e-08
ADAM_WD = 0.01
ADAM_STEP = 10

LANES = 128
N_CHIPS = 4
N_DEV = 8
VMEM_LIMIT_V7X = 52 * 1024 * 1024
MESH = pl.DeviceIdType.MESH
ANY = pl.BlockSpec(memory_space=pl.ANY)


def _cparams(*sem):
    return pltpu.CompilerParams(dimension_semantics=sem if sem else None, vmem_limit_bytes=VMEM_LIMIT_V7X)


def _dep_list(dep):
    return [] if dep is None else (list(dep) if isinstance(dep, (list, tuple)) else [dep])


def _after(body, n_in, dep):
    k = len(_dep_list(dep))
    if k == 0:
        return body

    def wrapped(*refs):
        body(*refs[:n_in], *refs[n_in + k:])

    return wrapped


def _dep_args(dep):
    deps = _dep_list(dep)
    return [ANY] * len(deps), deps


def _dot(a, b):
    return lax.dot_general(a, b, (((1,), (0,)), ((), ())), preferred_element_type=F32)


def _dot_nt(a, b):
    return lax.dot_general(a, b, (((1,), (1,)), ((), ())), preferred_element_type=F32)


def _dot_tn(a, b):
    return lax.dot_general(a, b, (((0,), (0,)), ((), ())), preferred_element_type=F32)


def _sigmoid(x):
    return jax.nn.sigmoid(x)


def _rms_fwd(x, g):
    r = lax.rsqrt(jnp.mean(x * x, axis=-1, keepdims=True) + RMS_EPS)
    return (x * r) * g


def _rms_bwd(x, g, dy):
    r = lax.rsqrt(jnp.mean(x * x, axis=-1, keepdims=True) + RMS_EPS)
    xh = x * r
    dg = jnp.sum(dy * xh, axis=0, keepdims=True)
    dxh = dy * g
    dx = r * (dxh - xh * jnp.mean(dxh * xh, axis=-1, keepdims=True))
    return dx, dg


def _matmul(name, a, b, mode, out_dtype, tm, tn, tk, dep=None):
    if mode == "nn":
        (m, k), (_, n) = a.shape, b.shape
    elif mode == "nt":
        (m, k), (n, _) = a.shape, b.shape
    else:
        (k, m), (_, n) = a.shape, b.shape
    tm, tn, tk = min(tm, m), min(tn, n), min(tk, k)
    assert m % tm == 0 and n % tn == 0 and k % tk == 0, (name, m, n, k, tm, tn, tk)
    nk = k // tk
    if mode == "tn":
        a_spec = pl.BlockSpec((tk, tm), lambda i, j, kk: (kk, i))
    else:
        a_spec = pl.BlockSpec((tm, tk), lambda i, j, kk: (i, kk))
    if mode == "nt":
        b_spec = pl.BlockSpec((tn, tk), lambda i, j, kk: (j, kk))
    else:
        b_spec = pl.BlockSpec((tk, tn), lambda i, j, kk: (kk, j))
    dot = {"nn": _dot, "nt": _dot_nt, "tn": _dot_tn}[mode]
    use_scratch = nk > 1 and out_dtype != F32

    def body(a_ref, b_ref, o_ref, *scratch):
        prod = dot(a_ref[...].astype(BF16), b_ref[...].astype(BF16))
        if nk == 1:
            o_ref[...] = prod.astype(out_dtype)
            return
        acc = scratch[0] if use_scratch else o_ref
        kk = pl.program_id(2)

        @pl.when(kk == 0)
        def _():
            acc[...] = prod

        @pl.when(kk > 0)
        def _():
            acc[...] += prod

        if use_scratch:
            @pl.when(kk == nk - 1)
            def _():
                o_ref[...] = acc[...].astype(out_dtype)

    dep_specs, dep_ops = _dep_args(dep)
    return pl.pallas_call(
        _after(body, 2, dep),
        name=name,
        out_shape=jax.ShapeDtypeStruct((m, n), out_dtype),
        grid=(m // tm, n // tn, nk),
        in_specs=[a_spec, b_spec] + dep_specs,
        out_specs=pl.BlockSpec((tm, tn), lambda i, j, kk: (i, j)),
        scratch_shapes=[pltpu.VMEM((tm, tn), F32)] if use_scratch else [],
        compiler_params=_cparams("parallel", "parallel", "arbitrary"),
    )(a, b, *dep_ops)


def _norm_fwd(name, x, g, tm):
    t, d = x.shape
    tm = min(tm, t)

    def body(x_ref, g_ref, h_ref):
        h_ref[...] = _rms_fwd(x_ref[...], g_ref[...]).astype(BF16)

    return pl.pallas_call(
        body, name=name, out_shape=jax.ShapeDtypeStruct((t, d), BF16), grid=(t // tm,),
        in_specs=[pl.BlockSpec((tm, d), lambda i: (i, 0)), pl.BlockSpec((1, d), lambda i: (0, 0))],
        out_specs=pl.BlockSpec((tm, d), lambda i: (i, 0)),
        compiler_params=_cparams("parallel"),
    )(x, g)


def _split3(x):
    hi = x.astype(BF16)
    r1 = x - hi.astype(F32)
    mid = r1.astype(BF16)
    lo = (r1 - mid.astype(F32)).astype(BF16)
    return hi, mid, lo


def _tri_dot(tri, x):
    hi, mid, lo = _split3(x)
    return _dot(tri, hi) + _dot(tri, mid) + _dot(tri, lo)


def _forget_fwd(h, wf, bf, seq):
    t, d = h.shape
    cb = min(256, seq)

    def body(h_ref, wf_ref, bf_ref, fl_ref, fc_ref):
        fl = _dot(h_ref[...], wf_ref[...])
        fl_ref[...] = fl
        xx = fl + bf_ref[...]
        lf = jnp.minimum(xx, 0.0) - jnp.log(1.0 + jnp.exp(-jnp.abs(xx)))
        ri = lax.broadcasted_iota(jnp.int32, (cb, cb), 0)
        ci = lax.broadcasted_iota(jnp.int32, (cb, cb), 1)
        tri = (ri >= ci).astype(BF16)
        carry = jnp.zeros((1, LANES), F32)
        for blk in range(seq // cb):
            cs = _tri_dot(tri, lf[blk * cb:(blk + 1) * cb]) + carry
            fc_ref[blk * cb:(blk + 1) * cb, :] = cs
            carry = cs[cb - 1:cb, :]

    return pl.pallas_call(
        body, name="forget_fwd",
        out_shape=(jax.ShapeDtypeStruct((t, LANES), F32), jax.ShapeDtypeStruct((t, LANES), F32)),
        grid=(t // seq,),
        in_specs=[pl.BlockSpec((seq, d), lambda b: (b, 0)), pl.BlockSpec((d, LANES), lambda b: (0, 0)),
                  pl.BlockSpec((1, LANES), lambda b: (0, 0))],
        out_specs=(pl.BlockSpec((seq, LANES), lambda b: (b, 0)), pl.BlockSpec((seq, LANES), lambda b: (b, 0))),
        compiler_params=_cparams("parallel"),
    )(h, wf, bf)


def _pool_fwd(u, mix, scale, seq):
    t = u.shape[0]

    def body(u_ref, mix_ref, sc_ref, p_ref, ps_ref):
        tpos = lax.broadcasted_iota(jnp.int32, (seq, POOL_GROUP_DIM), 0)
        for g in range(POOL_GROUPS):
            sl = slice(g * POOL_GROUP_DIM, (g + 1) * POOL_GROUP_DIM)
            ug = u_ref[:, sl]
            s = ug
            for lvl in range(g + 1):
                d = 2 ** lvl
                s = s + jnp.where(tpos >= d, pltpu.roll(s, d, 0), 0.0)
            cnt = jnp.minimum(tpos + 1, POOL_WINDOWS[g]).astype(F32)
            pb = (s / cnt - ug).astype(BF16)
            p_ref[:, sl] = pb
            ps_ref[:, sl] = (_dot(pb, mix_ref[g]) * sc_ref[:, sl]).astype(BF16)

    return pl.pallas_call(
        body, name="pool_fwd",
        out_shape=(jax.ShapeDtypeStruct((t, POOL_WIDTH), BF16), jax.ShapeDtypeStruct((t, POOL_WIDTH), BF16)),
        grid=(t // seq,),
        in_specs=[pl.BlockSpec((seq, POOL_WIDTH), lambda b: (b, 0)),
                  pl.BlockSpec((POOL_GROUPS, POOL_GROUP_DIM, POOL_GROUP_DIM), lambda b: (0, 0, 0)),
                  pl.BlockSpec((1, POOL_WIDTH), lambda b: (0, 0))],
        out_specs=(pl.BlockSpec((seq, POOL_WIDTH), lambda b: (b, 0)), pl.BlockSpec((seq, POOL_WIDTH), lambda b: (b, 0))),
        compiler_params=_cparams("parallel"),
    )(u, mix, scale)


def _aug_constants():
    w = N_HEADS * LANES
    rows = jnp.arange(3 * LANES)
    piece, head = rows // LANES, rows % LANES
    cols = jnp.arange(w)
    live = (head < N_HEADS)[:, None]
    pq = (live & (cols[None, :] == (head * LANES + HEAD_DIM + piece)[:, None])).astype(BF16)
    pk = -(live & (cols[None, :] == (head * LANES + HEAD_DIM + 3 + piece)[:, None])).astype(BF16)
    lane = cols % LANES
    oq = ((lane >= HEAD_DIM + 3) & (lane < HEAD_DIM + 6)).astype(F32)[None, :]
    ok = ((lane >= HEAD_DIM) & (lane < HEAD_DIM + 3)).astype(F32)[None, :]
    return pq, pk, oq, ok


def _head_blocks(wt):
    d = wt.shape[1]
    return jnp.pad(wt.reshape(N_HEADS, HEAD_DIM, d), ((0, 0), (0, LANES - HEAD_DIM), (0, 0))).reshape(N_HEADS * LANES, d)


def _attn_prep(h, wq, wk, wv, fcum, tm):
    t, d = h.shape
    tm = min(tm, t)
    w = N_HEADS * LANES
    pq, pk, oq, ok = _aug_constants()

    def body(h_ref, wq_ref, wk_ref, wv_ref, f_ref, pq_ref, pk_ref, oq_ref, ok_ref, qa_ref, ka_ref, v_ref):
        hh = h_ref[...]
        fs = jnp.concatenate(_split3(f_ref[...]), axis=1)
        q = _dot_nt(hh, wq_ref[...]).astype(BF16).astype(F32) * ATTN_SCALE
        qa_ref[...] = (q + _dot(fs, pq_ref[...]) + oq_ref[...]).astype(BF16)
        k = _dot_nt(hh, wk_ref[...]).astype(BF16).astype(F32)
        ka_ref[...] = (k + _dot(fs, pk_ref[...]) + ok_ref[...]).astype(BF16)
        v_ref[...] = _dot_nt(hh, wv_ref[...]).astype(BF16)

    row = lambda n: pl.BlockSpec((tm, n), lambda i: (i, 0))
    full = lambda a: pl.BlockSpec(a.shape, lambda i: (0, 0))
    return pl.pallas_call(
        body, name="attn_prep",
        out_shape=(jax.ShapeDtypeStruct((t, w), BF16), jax.ShapeDtypeStruct((t, w), BF16),
                   jax.ShapeDtypeStruct((t, ATTN_WIDTH), BF16)),
        grid=(t // tm,),
        in_specs=[row(d), full(wq), full(wk), full(wv), row(LANES), full(pq), full(pk), full(oq), full(ok)],
        out_specs=(row(w), row(w), row(ATTN_WIDTH)),
        compiler_params=_cparams("parallel"),
    )(h, wq, wk, wv, fcum, pq, pk, oq, ok)


def _fold_lanes(x, op):
    out = x[:, :LANES]
    for g in range(1, x.shape[1] // LANES):
        out = op(out, x[:, g * LANES:(g + 1) * LANES])
    return out


def _attn_fwd(qa, ka, v, seq, tq, dep=None):
    t = qa.shape[0]
    nq = seq // tq
    hp_n = N_HEADS // 2
    heads = [slice(e * LANES, (e + 1) * LANES) for e in range(2)]

    def body(q_ref, k_ref, v_ref, o_ref, lse_ref, s_buf):
        i = pl.program_id(2)
        diag_ok = lax.broadcasted_iota(jnp.int32, (tq, tq), 0) >= lax.broadcasted_iota(jnp.int32, (tq, tq), 1)
        qs = [q_ref[:, hl] for hl in heads]

        def sweep1(j, mxs):
            r0 = pl.multiple_of(j * tq, tq)
            out = []
            for e, hl in enumerate(heads):
                s = _dot_nt(qs[e], k_ref[pl.ds(r0, tq), hl])
                s = jnp.where(jnp.logical_or(diag_ok, j < i), s, NEG_BIG)
                s_buf[e, j] = s
                out.append(jnp.maximum(mxs[e], _fold_lanes(s, jnp.maximum)))
            return tuple(out)

        mxs = lax.fori_loop(0, i + 1, sweep1, (jnp.full((tq, LANES), NEG_BIG, F32),) * 2)
        ms = [jnp.max(mx, axis=1, keepdims=True) for mx in mxs]

        def sweep2(j, carry):
            r0 = pl.multiple_of(j * tq, tq)
            vv = v_ref[pl.ds(r0, tq), :]
            out = []
            for e in range(2):
                p = jnp.exp(s_buf[e, j] - ms[e])
                out += [carry[2 * e] + _fold_lanes(p, jnp.add), carry[2 * e + 1] + _dot(p.astype(BF16), vv)]
            return tuple(out)

        res = lax.fori_loop(0, i + 1, sweep2, (jnp.zeros((tq, LANES), F32),) * 4)
        outs = []
        for e in range(2):
            l = jnp.sum(res[2 * e], axis=1, keepdims=True)
            outs.append(res[2 * e + 1] / l)
            lse_ref[:, e:e + 1] = ms[e] + jnp.log(l)
        lane = lax.broadcasted_iota(jnp.int32, (tq, LANES), 1)
        o_ref[...] = jnp.where(lane < HEAD_DIM, outs[0], outs[1])

    dep_specs, dep_ops = _dep_args(dep)
    return pl.pallas_call(
        _after(body, 3, dep), name="attn_fwd",
        out_shape=(jax.ShapeDtypeStruct((t, ATTN_WIDTH), F32), jax.ShapeDtypeStruct((hp_n, t, 2), F32)),
        grid=(t // seq, hp_n, nq),
        in_specs=[pl.BlockSpec((tq, 2 * LANES), lambda b, hp, i: (b * nq + i, hp)),
                  pl.BlockSpec((seq, 2 * LANES), lambda b, hp, i: (b, hp)),
                  pl.BlockSpec((seq, LANES), lambda b, hp, i: (b, hp))] + dep_specs,
        out_specs=(pl.BlockSpec((tq, LANES), lambda b, hp, i: (b * nq + i, hp)),
                   pl.BlockSpec((None, tq, 2), lambda b, hp, i: (hp, b * nq + i, 0))),
        scratch_shapes=[pltpu.VMEM((2, nq, tq, tq), F32)],
        compiler_params=_cparams("parallel", "parallel", "arbitrary"),
    )(qa, ka, v, *dep_ops)


def _merge_fwd(x, ps, o, g2, wpo, wao, wout, tm):
    t, d = x.shape
    tm = min(tm, t)

    def body(x_ref, ps_ref, o_ref, gp_ref, ga_ref, wpo_ref, wao_ref, wout_ref, mg_ref, x1_ref):
        py = _dot(ps_ref[...], wpo_ref[...])
        ay = _dot(o_ref[...].astype(BF16), wao_ref[...])
        mb = (_sigmoid(gp_ref[...].astype(F32)) * py + _sigmoid(ga_ref[...].astype(F32)) * ay).astype(BF16)
        mg_ref[...] = mb
        x1_ref[...] = x_ref[...] + _dot(mb, wout_ref[...])

    row = lambda w: pl.BlockSpec((tm, w), lambda i: (i, 0))
    full = lambda a: pl.BlockSpec(a.shape, lambda i: (0, 0))
    return pl.pallas_call(
        body, name="merge_fwd",
        out_shape=(jax.ShapeDtypeStruct((t, d), BF16), jax.ShapeDtypeStruct((t, d), F32)),
        grid=(t // tm,),
        in_specs=[row(d), row(POOL_WIDTH), row(ATTN_WIDTH), pl.BlockSpec((tm, d), lambda i: (i, 0)),
                  pl.BlockSpec((tm, d), lambda i: (i, 1)), full(wpo), full(wao), full(wout)],
        out_specs=(row(d), row(d)),
        compiler_params=_cparams("parallel"),
    )(x, ps, o, g2, g2, wpo, wao, wout)


def _ffn_fwd(x1, g, wg, wu, wd, tm, tf):
    t, d = x1.shape
    f = wg.shape[0]
    tm = min(tm, t)
    nf = f // tf

    def body(x1_ref, g_ref, wg_ref, wu_ref, wd_ref, h2_ref, gt_ref, up_ref, act_ref, x2_ref):
        j = pl.program_id(1)

        @pl.when(j == 0)
        def _():
            h2_ref[...] = _rms_fwd(x1_ref[...], g_ref[...]).astype(BF16)

        h2 = h2_ref[...]
        gt = _dot_nt(h2, wg_ref[...])
        up = _dot_nt(h2, wu_ref[...])
        sg = _sigmoid(gt)
        silu = gt * sg
        act = (silu * up).astype(BF16)
        gt_ref[...] = (up * (sg * (1.0 + gt * (1.0 - sg)))).astype(BF16)
        up_ref[...] = silu.astype(BF16)
        act_ref[...] = act
        prod = _dot(act, wd_ref[...])

        @pl.when(j == 0)
        def _():
            x2_ref[...] = prod

        @pl.when(j > 0)
        def _():
            x2_ref[...] += prod

        @pl.when(j == nf - 1)
        def _():
            x2_ref[...] += x1_ref[...]

    return pl.pallas_call(
        body, name="ffn_fwd",
        out_shape=(jax.ShapeDtypeStruct((t, d), BF16), jax.ShapeDtypeStruct((t, f), BF16),
                   jax.ShapeDtypeStruct((t, f), BF16), jax.ShapeDtypeStruct((t, f), BF16),
                   jax.ShapeDtypeStruct((t, d), F32)),
        grid=(t // tm, nf),
        in_specs=[pl.BlockSpec((tm, d), lambda i, j: (i, 0)), pl.BlockSpec((1, d), lambda i, j: (0, 0)),
                  pl.BlockSpec((tf, d), lambda i, j: (j, 0)), pl.BlockSpec((tf, d), lambda i, j: (j, 0)),
                  pl.BlockSpec((tf, d), lambda i, j: (j, 0))],
        out_specs=(pl.BlockSpec((tm, d), lambda i, j: (i, 0)), pl.BlockSpec((tm, tf), lambda i, j: (i, j)),
                   pl.BlockSpec((tm, tf), lambda i, j: (i, j)), pl.BlockSpec((tm, tf), lambda i, j: (i, j)),
                   pl.BlockSpec((tm, d), lambda i, j: (i, 0))),
        compiler_params=_cparams("parallel", "arbitrary"),
    )(x1, g, wg, wu, wd)


def _final_fwd_bwd(x2, target, g, tm):
    t, d = x2.shape
    tm = min(tm, t)

    def body(x_ref, t_ref, g_ref, loss_ref, dx_ref, dg_ref):
        i = pl.program_id(0)
        x = x_ref[...]
        gg = g_ref[...]
        err = _rms_fwd(x, gg) - t_ref[...]
        part = 0.5 * jnp.sum(jnp.mean(err * err, axis=-1, keepdims=True), axis=0, keepdims=True)
        dx, dg = _rms_bwd(x, gg, err * (1.0 / d))
        dx_ref[...] = dx

        @pl.when(i == 0)
        def _():
            loss_ref[...] = jnp.zeros_like(loss_ref)
            dg_ref[...] = jnp.zeros_like(dg_ref)

        loss_ref[...] += jnp.broadcast_to(part, loss_ref.shape)
        dg_ref[...] += dg

    return pl.pallas_call(
        body, name="final_fwd_bwd",
        out_shape=(jax.ShapeDtypeStruct((1, LANES), F32), jax.ShapeDtypeStruct((t, d), F32),
                   jax.ShapeDtypeStruct((1, d), F32)),
        grid=(t // tm,),
        in_specs=[pl.BlockSpec((tm, d), lambda i: (i, 0)), pl.BlockSpec((tm, d), lambda i: (i, 0)),
                  pl.BlockSpec((1, d), lambda i: (0, 0))],
        out_specs=(pl.BlockSpec((1, LANES), lambda i: (0, 0)), pl.BlockSpec((tm, d), lambda i: (i, 0)),
                   pl.BlockSpec((1, d), lambda i: (0, 0))),
        compiler_params=_cparams("arbitrary"),
    )(x2, target, g)


def _ffn_bwd(dx2, x1, g, gt, up, wg, wu, wd, tm, tf):
    t, d = dx2.shape
    f = gt.shape[1]
    tm = min(tm, t)
    nf = f // tf

    def body(dx2_ref, x1_ref, g_ref, gt_ref, up_ref, wg_ref, wu_ref, wd_ref, dgt_ref, dup_ref, dx1_ref, dg_ref, acc_ref,
             dxb_ref):
        i, j = pl.program_id(0), pl.program_id(1)

        @pl.when(j == 0)
        def _():
            dxb_ref[...] = dx2_ref[...].astype(BF16)

        dact = _dot_nt(dxb_ref[...], wd_ref[...])
        dgt = (dact * gt_ref[...].astype(F32)).astype(BF16)
        dup = (dact * up_ref[...].astype(F32)).astype(BF16)
        dgt_ref[...] = dgt
        dup_ref[...] = dup
        contrib = _dot(dgt, wg_ref[...]) + _dot(dup, wu_ref[...])

        @pl.when(j == 0)
        def _():
            acc_ref[...] = contrib

        @pl.when(j > 0)
        def _():
            acc_ref[...] += contrib

        @pl.when(jnp.logical_and(i == 0, j == 0))
        def _():
            dg_ref[...] = jnp.zeros_like(dg_ref)

        @pl.when(j == nf - 1)
        def _():
            dxn, dg = _rms_bwd(x1_ref[...], g_ref[...], acc_ref[...])
            dx1_ref[...] = dx2_ref[...] + dxn
            dg_ref[...] += dg

    return pl.pallas_call(
        body, name="ffn_bwd",
        out_shape=(jax.ShapeDtypeStruct((t, f), BF16), jax.ShapeDtypeStruct((t, f), BF16),
                   jax.ShapeDtypeStruct((t, d), F32), jax.ShapeDtypeStruct((1, d), F32)),
        grid=(t // tm, nf),
        in_specs=[pl.BlockSpec((tm, d), lambda i, j: (i, 0)), pl.BlockSpec((tm, d), lambda i, j: (i, 0)),
                  pl.BlockSpec((1, d), lambda i, j: (0, 0)),
                  pl.BlockSpec((tm, tf), lambda i, j: (i, j)), pl.BlockSpec((tm, tf), lambda i, j: (i, j)),
                  pl.BlockSpec((tf, d), lambda i, j: (j, 0)), pl.BlockSpec((tf, d), lambda i, j: (j, 0)),
                  pl.BlockSpec((tf, d), lambda i, j: (j, 0))],
        out_specs=(pl.BlockSpec((tm, tf), lambda i, j: (i, j)), pl.BlockSpec((tm, tf), lambda i, j: (i, j)),
                   pl.BlockSpec((tm, d), lambda i, j: (i, 0)), pl.BlockSpec((1, d), lambda i, j: (0, 0))),
        scratch_shapes=[pltpu.VMEM((tm, d), F32), pltpu.VMEM((tm, d), BF16)],
        compiler_params=_cparams("arbitrary", "arbitrary"),
    )(dx2, x1, g, gt, up, wg, wu, wd)


def _merge_bwd(dx1, ps, o, g2, wpo, wao, wout, tm, dep=None):
    t, d = dx1.shape
    tm = min(tm, t)

    def body(dx1_ref, ps_ref, o_ref, gp_ref, ga_ref, wpo_ref, wao_ref, wout_ref, dpy_ref, day_ref, dg2_ref, dps_ref, da_ref):
        dm = _dot_nt(dx1_ref[...].astype(BF16), wout_ref[...])
        py = _dot(ps_ref[...], wpo_ref[...])
        ay = _dot(o_ref[...].astype(BF16), wao_ref[...])
        sp = _sigmoid(gp_ref[...].astype(F32))
        sa = _sigmoid(ga_ref[...].astype(F32))
        dpy = (dm * sp).astype(BF16)
        day = (dm * sa).astype(BF16)
        dpy_ref[...] = dpy
        day_ref[...] = day
        dg2_ref[:, :d] = (dm * py * (sp * (1.0 - sp))).astype(BF16)
        dg2_ref[:, d:] = (dm * ay * (sa * (1.0 - sa))).astype(BF16)
        dps_ref[...] = _dot_nt(dpy, wpo_ref[...])
        da_ref[...] = _dot_nt(day, wao_ref[...]).astype(BF16)

    row = lambda w: pl.BlockSpec((tm, w), lambda i: (i, 0))
    full = lambda a: pl.BlockSpec(a.shape, lambda i: (0, 0))
    dep_specs, dep_ops = _dep_args(dep)
    return pl.pallas_call(
        _after(body, 8, dep), name="merge_bwd",
        out_shape=(jax.ShapeDtypeStruct((t, d), BF16), jax.ShapeDtypeStruct((t, d), BF16),
                   jax.ShapeDtypeStruct((t, 2 * d), BF16), jax.ShapeDtypeStruct((t, POOL_WIDTH), F32),
                   jax.ShapeDtypeStruct((t, ATTN_WIDTH), BF16)),
        grid=(t // tm,),
        in_specs=[row(d), row(POOL_WIDTH), row(ATTN_WIDTH), pl.BlockSpec((tm, d), lambda i: (i, 0)),
                  pl.BlockSpec((tm, d), lambda i: (i, 1)), full(wpo), full(wao), full(wout)] + dep_specs,
        out_specs=(row(d), row(d), row(2 * d), row(POOL_WIDTH), row(ATTN_WIDTH)),
        compiler_params=_cparams("parallel"),
    )(dx1, ps, o, g2, g2, wpo, wao, wout, *dep_ops)


def _attn_bwd(qa, ka, v, do, lse4, seq, tq, dep=None):
    t = qa.shape[0]
    nq = seq // tq
    hp_n = N_HEADS // 2
    heads = [slice(e * LANES, (e + 1) * LANES) for e in range(2)]

    def body(q_ref, k_ref, v_ref, do_ref, lse_ref, dq_ref, dk_ref, dv_ref, dfr_ref, dk_acc, dv_acc, p_buf, dp_buf):
        diag_ok = lax.broadcasted_iota(jnp.int32, (tq, tq), 0) >= lax.broadcasted_iota(jnp.int32, (tq, tq), 1)
        lane_q = lax.broadcasted_iota(jnp.int32, (tq, LANES), 1)
        lane_s = lax.broadcasted_iota(jnp.int32, (seq, LANES), 1)
        mine_q = [lane_q < HEAD_DIM, lane_q >= HEAD_DIM]
        dv_acc[...] = jnp.zeros_like(dv_acc)
        dk_acc[...] = jnp.zeros_like(dk_acc)
        dfr_ref[...] = jnp.zeros_like(dfr_ref)

        def q_step(i, _):
            q0 = pl.multiple_of(i * tq, tq)
            qs = [q_ref[pl.ds(q0, tq), hl] for hl in heads]
            dov = do_ref[pl.ds(q0, tq), :]
            dos = [jnp.where(mq, dov, jnp.zeros((), BF16)) for mq in mine_q]
            lss = [lse_ref[pl.ds(q0, tq), e:e + 1] for e in range(2)]

            def sweep1(j, dls):
                r0 = pl.multiple_of(j * tq, tq)
                vv = v_ref[pl.ds(r0, tq), :]
                out = []
                for e, hl in enumerate(heads):
                    s = _dot_nt(qs[e], k_ref[pl.ds(r0, tq), hl])
                    s = jnp.where(jnp.logical_or(diag_ok, j < i), s, NEG_BIG)
                    p = jnp.exp(s - lss[e])
                    dp = _dot_nt(dos[e], vv)
                    p_buf[e, j] = p
                    dp_buf[e, j] = dp
                    dv_acc[pl.ds(r0, tq), :] += _dot_tn(p.astype(BF16), dos[e])
                    out.append(dls[e] + _fold_lanes(p * dp, jnp.add))
                return tuple(out)

            dls = lax.fori_loop(0, i + 1, sweep1, (jnp.zeros((tq, LANES), F32),) * 2)
            dls = [jnp.sum(d, axis=1, keepdims=True) for d in dls]

            def sweep2(j, dqs):
                r0 = pl.multiple_of(j * tq, tq)
                out = []
                for e, hl in enumerate(heads):
                    ds = p_buf[e, j] * (dp_buf[e, j] - dls[e])
                    dfr_ref[e, pl.ds(j, 1), :] += jnp.sum(ds, axis=0, keepdims=True)
                    dsb = ds.astype(BF16)
                    dk_acc[e, pl.ds(r0, tq), :] += _dot_tn(dsb, qs[e])
                    out.append(dqs[e] + _dot(dsb, k_ref[pl.ds(r0, tq), hl]))
                return tuple(out)

            dqs = lax.fori_loop(0, i + 1, sweep2, (jnp.zeros((tq, LANES), F32),) * 2)
            dq = jnp.where(mine_q[0], dqs[0], pltpu.roll(dqs[1], HEAD_DIM, 1)) * ATTN_SCALE
            dq_ref[pl.ds(q0, tq), :] = dq.astype(BF16)
            return 0

        lax.fori_loop(0, nq, q_step, 0)
        dk_ref[...] = jnp.where(lane_s < HEAD_DIM, dk_acc[0], pltpu.roll(dk_acc[1], HEAD_DIM, 1)).astype(BF16)
        dv_ref[...] = dv_acc[...].astype(BF16)

    wide = pl.BlockSpec((seq, 2 * LANES), lambda b, hp: (b, hp))
    col = pl.BlockSpec((seq, LANES), lambda b, hp: (b, hp))
    pair = pl.BlockSpec((None, seq, 2), lambda b, hp: (hp, b, 0))
    dep_specs, dep_ops = _dep_args(dep)
    return pl.pallas_call(
        _after(body, 5, dep), name="attn_bwd",
        out_shape=(jax.ShapeDtypeStruct((t, ATTN_WIDTH), BF16),) * 3 + (jax.ShapeDtypeStruct((N_HEADS, t // tq, tq), F32),),
        grid=(t // seq, hp_n),
        in_specs=[wide, wide, col, col, pair] + dep_specs,
        out_specs=(col, col, col, pl.BlockSpec((2, nq, tq), lambda b, hp: (hp, b, 0))),
        scratch_shapes=[pltpu.VMEM((2, seq, LANES), F32), pltpu.VMEM((seq, LANES), F32),
                        pltpu.VMEM((2, nq, tq, tq), F32), pltpu.VMEM((2, nq, tq, tq), F32)],
        compiler_params=_cparams("parallel", "arbitrary"),
    )(qa, ka, v, do, lse4, *dep_ops)


def _forget_bwd(dfc, fl, bf, seq):
    t = fl.shape[0]
    cb = min(256, seq)
    nb = seq // cb

    def body(dfc_ref, fl_ref, bf_ref, dfl_ref, db_ref):
        b = pl.program_id(0)
        ri = lax.broadcasted_iota(jnp.int32, (cb, cb), 0)
        ci = lax.broadcasted_iota(jnp.int32, (cb, cb), 1)
        tri = (ci >= ri).astype(BF16)
        carry = jnp.zeros((1, LANES), F32)
        dbs = jnp.zeros((1, LANES), F32)
        for blk in reversed(range(nb)):
            rs = slice(blk * cb, (blk + 1) * cb)
            dlf = _tri_dot(tri, -dfc_ref[rs, :]) + carry
            carry = dlf[0:1, :]
            dfl = dlf * _sigmoid(-(fl_ref[rs, :] + bf_ref[...]))
            dfl_ref[rs, :] = dfl.astype(BF16)
            dbs = dbs + jnp.sum(dfl, axis=0, keepdims=True)

        @pl.when(b == 0)
        def _():
            db_ref[...] = jnp.zeros_like(db_ref)

        db_ref[...] += dbs

    return pl.pallas_call(
        body, name="forget_bwd",
        out_shape=(jax.ShapeDtypeStruct((t, LANES), BF16), jax.ShapeDtypeStruct((1, LANES), F32)),
        grid=(t // seq,),
        in_specs=[pl.BlockSpec((seq, LANES), lambda b: (b, 0)), pl.BlockSpec((seq, LANES), lambda b: (b, 0)),
                  pl.BlockSpec((1, LANES), lambda b: (0, 0))],
        out_specs=(pl.BlockSpec((seq, LANES), lambda b: (b, 0)), pl.BlockSpec((1, LANES), lambda b: (0, 0))),
        compiler_params=_cparams("arbitrary"),
    )(dfc, fl, bf)


def _pool_bwd(dps, p, mix, scale, seq):
    t = dps.shape[0]

    def body(dps_ref, p_ref, mix_ref, sc_ref, du_ref, dmix_ref, dsc_ref):
        b = pl.program_id(0)

        @pl.when(b == 0)
        def _():
            dmix_ref[...] = jnp.zeros_like(dmix_ref)
            dsc_ref[...] = jnp.zeros_like(dsc_ref)

        tpos = lax.broadcasted_iota(jnp.int32, (seq, POOL_GROUP_DIM), 0)
        for g in range(POOL_GROUPS):
            sl = slice(g * POOL_GROUP_DIM, (g + 1) * POOL_GROUP_DIM)
            pb = p_ref[:, sl]
            dpsg = dps_ref[:, sl]
            pm = _dot(pb, mix_ref[g])
            dsc_ref[:, sl] += jnp.sum(dpsg * pm, axis=0, keepdims=True)
            dpm = (dpsg * sc_ref[:, sl]).astype(BF16)
            dmix_ref[g] += _dot_tn(pb, dpm)
            dp = _dot_nt(dpm, mix_ref[g])
            cnt = jnp.minimum(tpos + 1, POOL_WINDOWS[g]).astype(F32)
            s = dp / cnt
            for lvl in range(g + 1):
                d = 2 ** lvl
                s = s + jnp.where(tpos < seq - d, pltpu.roll(s, seq - d, 0), 0.0)
            du_ref[:, sl] = (s - dp).astype(BF16)

    return pl.pallas_call(
        body, name="pool_bwd",
        out_shape=(jax.ShapeDtypeStruct((t, POOL_WIDTH), BF16),
                   jax.ShapeDtypeStruct((POOL_GROUPS, POOL_GROUP_DIM, POOL_GROUP_DIM), F32),
                   jax.ShapeDtypeStruct((1, POOL_WIDTH), F32)),
        grid=(t // seq,),
        in_specs=[pl.BlockSpec((seq, POOL_WIDTH), lambda b: (b, 0)), pl.BlockSpec((seq, POOL_WIDTH), lambda b: (b, 0)),
                  pl.BlockSpec((POOL_GROUPS, POOL_GROUP_DIM, POOL_GROUP_DIM), lambda b: (0, 0, 0)),
                  pl.BlockSpec((1, POOL_WIDTH), lambda b: (0, 0))],
        out_specs=(pl.BlockSpec((seq, POOL_WIDTH), lambda b: (b, 0)),
                   pl.BlockSpec((POOL_GROUPS, POOL_GROUP_DIM, POOL_GROUP_DIM), lambda b: (0, 0, 0)),
                   pl.BlockSpec((1, POOL_WIDTH), lambda b: (0, 0))),
        compiler_params=_cparams("arbitrary"),
    )(dps, p, mix, scale)


def _in_bwd(du, dq, dk, dv, dg2, dfl, dx1, x, g, wu, wqkv, wg2, wft, tm):
    t, d = x.shape
    tm = min(tm, t)
    aw = ATTN_WIDTH

    def body(du_ref, dq_ref, dk_ref, dv_ref, dg2_ref, dfl_ref, dx1_ref, x_ref, g_ref, wu_ref, wqkv_ref, wg2_ref, wft_ref,
             dx_ref, dg_ref):
        i = pl.program_id(0)
        dh = _dot(du_ref[...], wu_ref[...])
        dh += _dot(dq_ref[...], wqkv_ref[0:aw, :])
        dh += _dot(dk_ref[...], wqkv_ref[aw:2 * aw, :])
        dh += _dot(dv_ref[...], wqkv_ref[2 * aw:3 * aw, :])
        dh += _dot(dg2_ref[...], wg2_ref[...])
        dh += _dot(dfl_ref[...], wft_ref[...])
        dxn, dg = _rms_bwd(x_ref[...], g_ref[...], dh)
        dx_ref[...] = dx1_ref[...] + dxn

        @pl.when(i == 0)
        def _():
            dg_ref[...] = jnp.zeros_like(dg_ref)

        dg_ref[...] += dg

    row = lambda w: pl.BlockSpec((tm, w), lambda i: (i, 0))
    full = lambda a: pl.BlockSpec(a.shape, lambda i: (0, 0))
    return pl.pallas_call(
        body, name="in_bwd",
        out_shape=(jax.ShapeDtypeStruct((t, d), F32), jax.ShapeDtypeStruct((1, d), F32)),
        grid=(t // tm,),
        in_specs=[row(POOL_WIDTH), row(aw), row(aw), row(aw), row(2 * d), row(LANES), row(d), row(d),
                  pl.BlockSpec((1, d), lambda i: (0, 0)), full(wu), full(wqkv), full(wg2), full(wft)],
        out_specs=(row(d), pl.BlockSpec((1, d), lambda i: (0, 0))),
        compiler_params=_cparams("arbitrary"),
    )(du, dq, dk, dv, dg2, dfl, dx1, x, g, wu, wqkv, wg2, wft)


def _position():
    return lax.axis_index("x"), lax.axis_index("y"), lax.axis_index("c")


def _remote(src, dst, send_sem, recv_sem, device):
    return pltpu.make_async_remote_copy(src_ref=src, dst_ref=dst, send_sem=send_sem, recv_sem=recv_sem,
                                        device_id=device, device_id_type=MESH)


HBM = pl.BlockSpec(memory_space=pltpu.HBM)
SEM = pl.BlockSpec(memory_space=pltpu.SEMAPHORE)
DATAFLOW = pltpu.SideEffectType.DATAFLOW_SIDE_EFFECTING


def _copies_start(name, arrays, plan, m, dep=None):
    n = len(arrays)
    arrays = [pltpu.with_memory_space_constraint(a, pltpu.HBM) for a in arrays]

    def body(*refs):
        ins, send_sem, recv_sem, token = refs[:n], refs[n], refs[n + 1], refs[2 * n + 2]
        for i, (src, dst, device, _) in enumerate(plan(ins, *_position())):
            _remote(src, dst, send_sem.at[i], recv_sem.at[i], device).start()
        token[...] = jnp.zeros_like(token)

    dep_specs, dep_ops = _dep_args(dep)
    outs = pl.pallas_call(
        _after(body, n, dep), name=name,
        out_shape=(pltpu.SemaphoreType.DMA((m,)), pltpu.SemaphoreType.DMA((m,)),
                   *[pltpu.HBM(a.shape, a.dtype) for a in arrays], jax.ShapeDtypeStruct((8, LANES), F32)),
        in_specs=[HBM] * n + dep_specs, out_specs=(SEM, SEM, *[HBM] * n, pl.BlockSpec(memory_space=pltpu.VMEM)),
        input_output_aliases={i: i + 2 for i in range(n)},
        compiler_params=pltpu.CompilerParams(has_side_effects=DATAFLOW),
    )(*arrays, *dep_ops)
    return (outs[0], outs[1]), list(outs[2:2 + n]), outs[2 + n]


def _copies_wait(name, sems, arrays, plan, after):
    n = len(arrays)
    afters = list(after) if isinstance(after, (list, tuple)) else [after]

    def body(*refs):
        ins, send_sem, recv_sem = refs[:n], refs[n], refs[n + 1]
        for i, (src, dst, device, landing) in enumerate(plan(ins, *_position())):
            _remote(src, dst, send_sem.at[i], recv_sem.at[i], device).wait_send()
            _remote(landing, landing, send_sem.at[i], recv_sem.at[i], device).wait_recv()

    outs = pl.pallas_call(
        body, name=name,
        out_shape=tuple(pltpu.HBM(a.shape, a.dtype) for a in arrays),
        in_specs=[HBM] * n + [SEM, SEM] + [ANY] * len(afters), out_specs=tuple([HBM] * n),
        input_output_aliases={i: i for i in range(n)},
        compiler_params=pltpu.CompilerParams(has_side_effects=DATAFLOW),
    )(*arrays, sems[0], sems[1], *afters)
    return list(outs)


def _tie(x, dep):
    for token in _dep_list(dep):
        x = x + token[0, 0]
    return x


def _other_chips(x, y):
    return [(1 - x, y), (x, 1 - y), (1 - x, 1 - y)]


def _gather_begin(tag, shards, token, column_halves=False):
    n = len(shards)
    lands = [lax.empty((N_CHIPS,) + s.shape, s.dtype) for s in shards]
    if column_halves:
        cols = lambda ref, h: pl.ds(pl.multiple_of(h * (ref.shape[-1] // 2), LANES), ref.shape[-1] // 2)
        mine = lambda ref, h: ref.at[:, cols(ref, h)]
        landed = lambda ref, chip, h: ref.at[chip, :, cols(ref, h)]
    else:
        mine = lambda ref, h: ref.at[h]
        landed = lambda ref, chip, h: ref.at[chip, h]

    def plan(refs, x, y, c):
        return [(mine(refs[k], c), landed(refs[n + k], 2 * x + y, c), (ox, oy, c), landed(refs[n + k], 2 * ox + oy, c))
                for k in range(n) for ox, oy in _other_chips(x, y)]

    sems, thru, token = _copies_start(f"gather_{tag}_ici_start", list(shards) + lands, plan, 3 * n, dep=token)
    return dict(tag=tag, n=n, plan=plan, sems=sems, arrays=thru, token=token, landed=landed)


def _gather_forward(st, after):
    n, tag, landed = st["n"], st["tag"], st["landed"]
    thru = _copies_wait(f"gather_{tag}_ici_wait", st["sems"], st["arrays"], st["plan"], after)

    def plan(refs, x, y, c):
        return [(landed(refs[k], 2 * ox + oy, c), landed(refs[k], 2 * ox + oy, c), (x, y, 1 - c),
                 landed(refs[k], 2 * ox + oy, 1 - c))
                for k in range(n) for ox, oy in _other_chips(x, y)]

    sems, lands, token = _copies_start(f"gather_{tag}_fwd_start", thru[n:], plan, 3 * n)
    return dict(tag=tag, n=n, plan=plan, sems=sems, arrays=lands, token=token, shards=thru[:n])


def _gather_end(st, after):
    lands = _copies_wait(f"gather_{st['tag']}_fwd_wait", st["sems"], st["arrays"], st["plan"], after)
    me = 2 * lax.axis_index("x") + lax.axis_index("y")
    return [lax.dynamic_update_index_in_dim(g, s, me, 0) for g, s in zip(lands, st["shards"])]


def _add_keep_give(name, pos, a, a_keep, a_give, b, b_keep, b_give, steps):
    r, c = b.shape[-2:]

    def spec(arr, fn):
        lead = arr.ndim - 2

        def index(i, p):
            idx = tuple(fn(i, p))
            return idx if len(idx) == arr.ndim else idx + (0, 0)

        return pl.BlockSpec((None,) * lead + (r, c), index)

    out_spec = pl.BlockSpec((None, r, c), lambda i, p: (i, 0, 0))

    def body(p_ref, ak_ref, bk_ref, ag_ref, bg_ref, keep_ref, give_ref):
        keep_ref[...] = ak_ref[...] + bk_ref[...].astype(F32)
        give_ref[...] = (ag_ref[...] + bg_ref[...].astype(F32)).astype(BF16)

    return pl.pallas_call(
        body, name=name,
        out_shape=(jax.ShapeDtypeStruct((steps, r, c), F32), jax.ShapeDtypeStruct((steps, r, c), BF16)),
        grid_spec=pltpu.PrefetchScalarGridSpec(
            num_scalar_prefetch=1, grid=(steps,),
            in_specs=[spec(a, a_keep), spec(b, b_keep), spec(a, a_give), spec(b, b_give)],
            out_specs=(out_spec, out_spec)),
        compiler_params=_cparams("parallel"),
    )(pos, a, b, a, b)


def _add_last(name, a, b):
    _, r, c = a.shape
    blk = pl.BlockSpec((None, r, c), lambda i: (0, 0, 0))

    def body(a_ref, b_ref, o_ref):
        o_ref[...] = a_ref[...] + b_ref[...].astype(F32)

    return pl.pallas_call(
        body, name=name, out_shape=jax.ShapeDtypeStruct((r, c), F32), grid=(1,), in_specs=[blk, blk],
        out_specs=pl.BlockSpec((r, c), lambda i: (0, 0)), compiler_params=_cparams("arbitrary"),
    )(a, b)


def _exchange_begin(tag, stage, gives, lands, peer_fn, extra):
    n = len(gives)

    def plan(refs, x, y, c):
        return [(refs[k], refs[n + k], peer_fn(x, y, c), refs[n + k]) for k in range(n)]

    sems, thru, token = _copies_start(f"rs{tag}_{stage}_start", gives + lands, plan, n)
    return dict(extra, tag=tag, n=n, stage=stage, plan=plan, sems=sems, arrays=thru, token=token)


def _reduce_begin(tag, grads, column_halves=False):
    n = len(grads)
    if column_halves:
        half = lambda ref, j, h: ref.at[j, :, pl.ds(pl.multiple_of(h * (ref.shape[2] // 2), LANES), ref.shape[2] // 2)]
        lands = [lax.empty((N_CHIPS, g.shape[1], g.shape[2] // 2), F32) for g in grads]
    else:
        half = lambda ref, j, h: ref.at[j, h]
        lands = [lax.empty((N_CHIPS,) + g.shape[2:], F32) for g in grads]

    def plan(refs, x, y, c):
        return [(half(refs[k], j, 1 - c), refs[n + k].at[j], (x, y, 1 - c), refs[n + k].at[j])
                for k in range(n) for j in range(N_CHIPS)]

    sems, thru, token = _copies_start(f"rs{tag}_c_start", list(grads) + lands, plan, N_CHIPS * n)
    return dict(tag=tag, n=n, stage="c", plan=plan, sems=sems, arrays=thru, token=token, column_halves=column_halves)


def _reduce_advance(st, after):
    tag, n, stage = st["tag"], st["n"], st["stage"]
    thru = _copies_wait(f"rs{tag}_{stage}_wait", st["sems"], st["arrays"], st["plan"], after)
    first, recv = thru[:n], thru[n:]
    x, y, c = _position()
    if stage == "c":
        pos = jnp.stack([c, x]).astype(jnp.int32)
        if st["column_halves"]:
            mine = lambda chip: (lambda i, p: (chip(p) + i, 0, p[0]))
        else:
            mine = lambda chip: (lambda i, p: (chip(p) + i, p[0]))
        sums = [_add_keep_give(
            f"rs{tag}_c_add{k}", pos,
            first[k], mine(lambda p: 2 * p[1]), mine(lambda p: 2 * (1 - p[1])),
            recv[k], lambda i, p: (2 * p[1] + i,), lambda i, p: (2 * (1 - p[1]) + i,), 2) for k in range(n)]
        lands = [lax.empty(s[1].shape, BF16) for s in sums]
        return _exchange_begin(tag, "x", [s[1] for s in sums], lands, lambda x, y, c: (1 - x, y, c),
                               dict(keep=[s[0] for s in sums]))
    if stage == "x":
        pos = jnp.stack([y]).astype(jnp.int32)
        sums = [_add_keep_give(
            f"rs{tag}_x_add{k}", pos,
            st["keep"][k], lambda i, p: (p[0],), lambda i, p: (1 - p[0],),
            recv[k], lambda i, p: (p[0],), lambda i, p: (1 - p[0],), 1) for k in range(n)]
        lands = [lax.empty(s[1].shape, BF16) for s in sums]
        return _exchange_begin(tag, "y", [s[1] for s in sums], lands, lambda x, y, c: (x, 1 - y, c),
                               dict(keep=[s[0] for s in sums]))
    if stage == "y":
        mine = [_add_last(f"rs{tag}_y_add{k}", st["keep"][k], recv[k]) for k in range(n)]
        lands = [lax.empty(m.shape, F32) for m in mine]
        return _exchange_begin(tag, "swap", mine, lands, lambda x, y, c: (x, y, 1 - c), {})
    return dict(done=list(zip(first, recv)), token=None)


def _all_reduce_small(v):
    r = v.shape[0]

    def body(v_ref, out_ref, buf, send_sems, recv_sems, local_sem):
        x, y, c = _position()
        me, sibling = (x, y, c), (x, y, 1 - c)
        chips = [(1 - x, y), (x, 1 - y), (1 - x, 1 - y)]

        def rows(px, py, pc):
            return buf.at[pl.ds((4 * px + 2 * py + pc) * r, r), :]

        def copy(k, block, to, src=None):
            return _remote(rows(*block) if src is None else src, rows(*block), send_sems.at[k], recv_sems.at[k], to)

        mine = pltpu.make_async_copy(v_ref, rows(*me), local_sem)
        mine.start()
        first = [copy(0, me, sibling, src=v_ref)]
        first += [copy(1 + j, me, (*chip, c), src=v_ref) for j, chip in enumerate(chips)]
        for cp in first:
            cp.start()
        passed = [copy(4 + j, (*chip, c), sibling) for j, chip in enumerate(chips)]
        for j, chip in enumerate(chips):
            copy(1 + j, (*chip, c), me).wait_recv()
            passed[j].start()
        copy(0, sibling, me).wait_recv()
        for j, chip in enumerate(chips):
            copy(4 + j, (*chip, 1 - c), me).wait_recv()
        for cp in first + passed:
            cp.wait_send()
        mine.wait()
        acc = buf[0:r, :]
        for dev in range(1, N_DEV):
            acc = acc + buf[dev * r:(dev + 1) * r, :]
        out_ref[...] = acc

    return pl.pallas_call(
        body, name="all_reduce_small",
        out_shape=jax.ShapeDtypeStruct(v.shape, F32),
        in_specs=[pl.BlockSpec(memory_space=pltpu.VMEM)],
        out_specs=pl.BlockSpec(memory_space=pltpu.VMEM),
        scratch_shapes=[pltpu.VMEM((N_DEV * r, LANES), F32), pltpu.SemaphoreType.DMA((7,)),
                        pltpu.SemaphoreType.DMA((7,)), pltpu.SemaphoreType.DMA],
        compiler_params=pltpu.CompilerParams(has_side_effects=True, vmem_limit_bytes=VMEM_LIMIT_V7X),
    )(v)


def _adamw_update(w, gg, m, v):
    mn = ADAM_B1 * m + (1.0 - ADAM_B1) * gg
    vn = ADAM_B2 * v + (1.0 - ADAM_B2) * (gg * gg)
    m_hat = mn / (1.0 - ADAM_B1 ** ADAM_STEP)
    v_hat = vn / (1.0 - ADAM_B2 ** ADAM_STEP)
    return -ADAM_LR * (m_hat / (jnp.sqrt(v_hat) + ADAM_EPS) + ADAM_WD * w), mn, vn


def _adamw(name, w, g, m, v):
    def body(w_ref, g_ref, m_ref, v_ref, d_ref, mo_ref, vo_ref):
        d_ref[...], mo_ref[...], vo_ref[...] = _adamw_update(w_ref[...], g_ref[...], m_ref[...], v_ref[...])

    blk = pl.BlockSpec(w.shape, lambda i: (0, 0))
    return pl.pallas_call(
        body, name=name, out_shape=(jax.ShapeDtypeStruct(w.shape, F32),) * 3, grid=(1,),
        in_specs=[blk] * 4, out_specs=(blk,) * 3, compiler_params=_cparams("arbitrary"),
    )(w, g, m, v)


def _rows_to_bf16(name, w):
    r, _, c = w.shape

    def body(w_ref, o_ref):
        o_ref[...] = w_ref[:, 0, :].astype(BF16)

    return pl.pallas_call(
        body, name=name, out_shape=jax.ShapeDtypeStruct((r, c), BF16), grid=(1,),
        in_specs=[pl.BlockSpec((r, 1, c), lambda i: (0, 0, 0))], out_specs=pl.BlockSpec((r, c), lambda i: (0, 0)),
        compiler_params=_cparams("arbitrary"),
    )(w)


def _adamw_rows(name, pos_c, w, g_mine, g_other, m, v):
    r, _, c = w.shape
    ch = c // 2

    def body(p_ref, w_ref, gm_ref, go_ref, m_ref, v_ref, g_ref, d_ref, mo_ref, vo_ref):
        gg = jnp.where(pl.program_id(0) == p_ref[0], gm_ref[...], go_ref[...])
        dl, mn, vn = _adamw_update(w_ref[:, 0, :], gg, m_ref[:, 0, :], v_ref[:, 0, :])
        g_ref[:, 0, :] = gg
        d_ref[:, 0, :] = dl
        mo_ref[:, 0, :] = mn
        vo_ref[:, 0, :] = vn

    rows = pl.BlockSpec((r, 1, ch), lambda h, p: (0, 0, h))
    half = pl.BlockSpec((r, ch), lambda h, p: (0, 0))
    return pl.pallas_call(
        body, name=name, out_shape=(jax.ShapeDtypeStruct(w.shape, F32),) * 4,
        grid_spec=pltpu.PrefetchScalarGridSpec(
            num_scalar_prefetch=1, grid=(2,), in_specs=[rows, half, half, rows, rows], out_specs=(rows,) * 4),
        compiler_params=_cparams("parallel"),
    )(pos_c, w, g_mine, g_other, m, v)


def _adamw_halves(name, pos_c, w, g_mine, g_other, m, v, tr, dep=None):
    r, c = w.shape
    rh = r // 2
    tr = tr if rh % tr == 0 else rh
    nt = rh // tr

    def body(p_ref, w_ref, gm_ref, go_ref, m_ref, v_ref, g_ref, d_ref, mo_ref, vo_ref):
        gg = jnp.where(pl.program_id(0) == p_ref[0], gm_ref[...], go_ref[...])
        g_ref[...] = gg
        d_ref[...], mo_ref[...], vo_ref[...] = _adamw_update(w_ref[...], gg, m_ref[...], v_ref[...])

    full = pl.BlockSpec((tr, c), lambda h, i, p: (h * nt + i, 0))
    half = pl.BlockSpec((tr, c), lambda h, i, p: (i, 0))
    dep_specs, dep_ops = _dep_args(dep)
    return pl.pallas_call(
        _after(body, 6, dep), name=name, out_shape=(jax.ShapeDtypeStruct((r, c), F32),) * 4,
        grid_spec=pltpu.PrefetchScalarGridSpec(
            num_scalar_prefetch=1, grid=(2, nt),
            in_specs=[full, half, half, full, full] + dep_specs, out_specs=(full,) * 4),
        compiler_params=_cparams("parallel", "parallel"),
    )(pos_c, w, g_mine, g_other, m, v, *dep_ops)


def _col_sharded_to_comm(g):
    k, n = g.shape
    return g.reshape(2, k // 2, N_CHIPS, n // N_CHIPS).transpose(2, 0, 1, 3)


def _row_sharded_to_comm(g):
    r, c = g.shape
    return g.reshape(N_CHIPS, 2, r // (2 * N_CHIPS), c)


def _col_sharded_full(g):
    _, _, rh, c = g.shape
    return g.reshape(N_CHIPS, 2 * rh, c).transpose(1, 0, 2).reshape(2 * rh, N_CHIPS * c)


def _row_sharded_full(g):
    _, _, rh, c = g.shape
    return g.reshape(N_CHIPS * 2 * rh, c)


def _pack_small(g1, bfv, mix, scale, g2n, gf, extra=None):
    row8 = jnp.pad(bfv.reshape(1, N_HEADS), ((0, 0), (0, LANES - N_HEADS)))
    if extra is not None:
        row8 = row8 + jnp.pad(extra[:, :1], ((0, 0), (N_HEADS, LANES - N_HEADS - 1)))
    return jnp.concatenate([
        g1.reshape(8, LANES), jnp.pad(row8, ((0, 7), (0, 0))), mix.reshape(512, LANES),
        jnp.pad(scale.reshape(4, LANES), ((0, 4), (0, 0))), g2n.reshape(8, LANES), gf.reshape(8, LANES)], axis=0)


def _unpack_small(s, like):
    g1, bfv, mix, scale, g2n, gf = like
    return (s[0:8].reshape(g1.shape), s[8, :N_HEADS].reshape(bfv.shape), s[16:528].reshape(mix.shape),
            s[528:532].reshape(scale.shape), s[536:544].reshape(g2n.shape), s[544:552].reshape(gf.shape))


class _MeshLinks:
    def __init__(self, shards_in, shards_rest):
        self.gin = _gather_begin("in", shards_in, None, column_halves=True)
        self.grest = _gather_begin("rest", shards_rest, self.gin["token"])
        self.tokens = {"gather": self.grest["token"]}
        self.groups = {}

    @property
    def token(self):
        return list(self.tokens.values())

    def tie(self, x):
        return _tie(x, self.token)

    def weights_in(self, after):
        st = _gather_forward(self.gin, after)
        (g,) = _gather_end(st, st["token"])
        return g.reshape(N_CHIPS * g.shape[1], g.shape[2])

    def rest_forward(self, after):
        self.grest = _gather_forward(self.grest, after)
        self.tokens["gather"] = self.grest["token"]

    def weights_rest(self, after):
        g = _gather_end(self.grest, after)
        del self.tokens["gather"]
        return [_col_sharded_full(g[0]), _col_sharded_full(g[1])] + [_row_sharded_full(a) for a in g[2:]]

    def reduce_begin(self, tag, grads, column_halves=False):
        self.groups[tag] = _reduce_begin(tag, grads, column_halves)
        self.tokens[tag] = self.groups[tag]["token"]

    def advance(self, after):
        for tag, st in self.groups.items():
            if "done" not in st:
                self.groups[tag] = _reduce_advance(st, after)
                if self.groups[tag]["token"] is None:
                    del self.tokens[tag]
                else:
                    self.tokens[tag] = self.groups[tag]["token"]

    def reduced(self, tag):
        return self.groups[tag]["done"]


class _NoLinks:
    token = None

    def __init__(self, w_in, rest):
        self.w_in, self.rest, self.grads = w_in, rest, {}

    def tie(self, x):
        return x

    def weights_in(self, after):
        return self.w_in

    def rest_forward(self, after):
        pass

    def weights_rest(self, after):
        return self.rest

    def reduce_begin(self, tag, grads, column_halves=False):
        self.grads[tag] = grads

    def advance(self, after):
        pass


def _local_step(links, x, target, seq, norm1_g, b_forget, pool_mix, pool_scale, norm2_g, norm_f_g):
    t, d = x.shape
    tq = min(256, seq)
    aw = ATTN_WIDTH
    o_q, o_f, o_g = POOL_WIDTH, POOL_WIDTH + 3 * aw, POOL_WIDTH + 3 * aw + N_HEADS
    bf = jnp.pad(b_forget, ((0, 0), (0, LANES - N_HEADS)))
    mixb = pool_mix.astype(BF16)

    h = _norm_fwd("norm1_fwd", x, links.tie(norm1_g), 512)
    w_in = links.weights_in(h)
    wu = w_in[:o_q]
    wqkv = w_in[o_q:o_f]
    wft = jnp.pad(w_in[o_f:o_g], ((0, LANES - N_HEADS), (0, 0)))
    wg2 = w_in[o_g:]
    wf = wft.T
    u = _matmul("mm_u", h, wu, "nt", F32, 1024, 512, d)
    g2 = _matmul("mm_gates", h, wg2, "nt", BF16, 1024, 512, d)
    fl, fcum = _forget_fwd(h, wf, bf, seq)
    qa, ka, v = _attn_prep(h, _head_blocks(wqkv[:aw]), _head_blocks(wqkv[aw:2 * aw]), wqkv[2 * aw:], fcum, 512)
    p, ps = _pool_fwd(u, mixb, pool_scale, seq)
    links.rest_forward([ps, qa, g2])
    o, lse = _attn_fwd(qa, ka, v, seq, tq, dep=links.token)
    w_pool_out, w_attn_out, w_out, w_ffn_gate, w_ffn_up, w_ffn_down = links.weights_rest(o)
    merged, x1 = _merge_fwd(x, ps, o, g2, w_pool_out, w_attn_out, w_out, 256)
    h2, gt, up, act, x2 = _ffn_fwd(x1, norm2_g, w_ffn_gate, w_ffn_up, w_ffn_down, 1024, 256)
    loss, dx2, d_gf = _final_fwd_bwd(x2, target, norm_f_g, 512)

    dgt, dup, dx1, d_g2n = _ffn_bwd(dx2, x1, norm2_g, gt, up, w_ffn_gate, w_ffn_up, w_ffn_down, 1024, 256)
    d_wd = _matmul("dw_down", act, dx2, "tn", F32, 1408, 1024, 1024)
    d_wg = _matmul("dw_gate", dgt, h2, "tn", F32, 1408, 1024, 1024)
    d_wu = _matmul("dw_up", dup, h2, "tn", F32, 1408, 1024, 1024)
    links.reduce_begin("a", [_row_sharded_to_comm(g) for g in (d_wg, d_wu, d_wd)])
    dpy, day, dg2, dps, da = _merge_bwd(dx1, ps, o, g2, w_pool_out, w_attn_out, w_out, 256, dep=links.token)
    links.advance(dps)
    d_wout = _matmul("dw_out", merged, dx1, "tn", F32, 1024, 1024, 1024)
    d_wpo = _matmul("dw_pool_out", ps, dpy, "tn", F32, 512, 1024, 1024)
    d_wao = _matmul("dw_attn_out", o, day, "tn", F32, 512, 1024, 1024)
    links.reduce_begin("m", [_col_sharded_to_comm(d_wpo), _col_sharded_to_comm(d_wao), _row_sharded_to_comm(d_wout)])
    dq, dk, dv, dfr = _attn_bwd(qa, ka, v, da, lse, seq, tq, dep=links.token)
    links.advance(dq)
    dfc = jnp.pad(dfr.reshape(N_HEADS, t).T, ((0, 0), (0, LANES - N_HEADS)))
    dfl, d_bf = _forget_bwd(dfc, fl, bf, seq)
    du, d_mix, d_scale = _pool_bwd(dps, p, mixb, links.tie(pool_scale), seq)
    d_wu_in = _matmul("dw_in_u", du, h, "tn", F32, 512, 1024, 1024)
    d_wq = _matmul("dw_in_q", dq, h, "tn", F32, 512, 1024, 1024)
    d_wk = _matmul("dw_in_k", dk, h, "tn", F32, 512, 1024, 1024)
    d_wv = _matmul("dw_in_v", dv, h, "tn", F32, 512, 1024, 1024)
    links.advance([d_wu_in, d_wq, d_wk, d_wv])
    d_wf = _matmul("dw_in_f", dfl, h, "tn", F32, LANES, 1024, 512)
    d_wg2 = _matmul("dw_in_gates", dg2, h, "tn", F32, 1024, 1024, 1024, dep=links.token)
    d_win = jnp.concatenate([d_wu_in, d_wq, d_wk, d_wv, d_wf[:N_HEADS], d_wg2], axis=0)
    comm_b = [d_win.reshape(N_CHIPS, d_win.shape[0] // N_CHIPS, d)]
    links.advance(comm_b)
    links.reduce_begin("b", comm_b, column_halves=True)
    dx, d_g1 = _in_bwd(du, dq, dk, dv, dg2, dfl, dx1, x, links.tie(norm1_g), wu, wqkv, wg2, wft, 256)
    links.advance(dx)
    small = (d_g1, d_bf[:, :N_HEADS], d_mix, d_scale, d_g2n, d_gf)
    return loss, dx, small


def kernel(x, norm1_g, w_in, b_forget, pool_mix, pool_scale, w_pool_out, w_attn_out, w_out, norm2_g, w_ffn_gate, w_ffn_up, w_ffn_down, norm_f_g, loss_target, m_norm1_g, m_w_in, m_b_forget, m_pool_mix, m_pool_scale, m_w_pool_out, m_w_attn_out, m_w_out, m_norm2_g, m_w_ffn_gate, m_w_ffn_up, m_w_ffn_down, m_norm_f_g, v_norm1_g, v_w_in, v_b_forget, v_pool_mix, v_pool_scale, v_w_pool_out, v_w_attn_out, v_w_out, v_norm2_g, v_w_ffn_gate, v_w_ffn_up, v_w_ffn_down, v_norm_f_g):
    nb, seq, d = x.shape
    group_a = ((w_ffn_gate, m_w_ffn_gate, v_w_ffn_gate, True, 9), (w_ffn_up, m_w_ffn_up, v_w_ffn_up, True, 10),
               (w_ffn_down, m_w_ffn_down, v_w_ffn_down, False, 11))
    group_m = ((w_pool_out, m_w_pool_out, v_w_pool_out, False, 5), (w_attn_out, m_w_attn_out, v_w_attn_out, False, 6),
               (w_out, m_w_out, v_w_out, False, 7))
    group_b = ((w_in, m_w_in, v_w_in, False, 1),)
    small_w = (norm1_g, b_forget, pool_mix, pool_scale, norm2_g, norm_f_g)
    small_m = (m_norm1_g, m_b_forget, m_pool_mix, m_pool_scale, m_norm2_g, m_norm_f_g)
    small_v = (v_norm1_g, v_b_forget, v_pool_mix, v_pool_scale, v_norm2_g, v_norm_f_g)
    small_pos = (0, 2, 3, 4, 8, 12)
    view = lambda a, tr: a[0].T if tr else a[0]
    unview = lambda a, tr, like: (a.T if tr else a).reshape(like.shape)

    def shard(w, tr):
        lw = view(w, tr).astype(BF16)
        return lw.reshape(2, lw.shape[0] // 2, lw.shape[1])

    cm = lambda a: jnp.transpose(a, (2, 0, 1))
    shard_in = _rows_to_bf16("w_in_to_bf16", cm(w_in))
    links = _MeshLinks([shard_in],
                       [shard(w_pool_out, False), shard(w_attn_out, False), shard(w_out, False),
                        shard(w_ffn_gate, True), shard(w_ffn_up, True), shard(w_ffn_down, False)])
    loss, dx, small_g = _local_step(
        links, x.reshape(nb * seq, d), loss_target.reshape(nb * seq, d), seq,
        norm1_g, b_forget, pool_mix[0], pool_scale, norm2_g, norm_f_g.reshape(1, d))

    grads, deltas, new_m, new_v = [None] * 13, [None] * 13, [None] * 13, [None] * 13
    pos_c = jnp.stack([lax.axis_index("c")]).astype(jnp.int32)

    def update(tag, group, dep):
        last = []
        for k, ((w, m, v, tr, pos), (mine, other)) in enumerate(zip(group, links.reduced(tag))):
            outs = _adamw_halves(f"adamw_{tag}{k}", pos_c, view(w, tr), mine, other, view(m, tr), view(v, tr), 256,
                                 dep=dep)
            grads[pos], deltas[pos], new_m[pos], new_v[pos] = (unview(a, tr, w) for a in outs)
            last.append(outs[1])
        return last

    last = update("a", group_a, links.token) + update("m", group_m, links.token)
    links.advance(last)
    small_sum = _all_reduce_small(links.tie(_pack_small(*small_g, extra=loss)))
    loss_out = small_sum[8, N_HEADS]
    dl, mn, vn = _adamw("adamw_small", _pack_small(*small_w), small_sum * _small_mask(), _pack_small(*small_m),
                        _pack_small(*small_v))
    for pos, g, a, b, e in zip(small_pos, _unpack_small(small_sum, small_w), _unpack_small(dl, small_w),
                               _unpack_small(mn, small_w), _unpack_small(vn, small_w)):
        grads[pos], deltas[pos], new_m[pos], new_v[pos] = g, a, b, e
    links.advance(dl)
    links.advance(links.token)
    (mine, other), = links.reduced("b")
    outs = _adamw_rows("adamw_b0", pos_c, cm(w_in), mine, other, cm(m_w_in), cm(v_w_in))
    grads[1], deltas[1], new_m[1], new_v[1] = (jnp.transpose(a, (1, 2, 0)) for a in outs)

    return (loss_out, dx.reshape(nb, seq, d), *grads, *deltas, *new_m, *new_v)


def _small_mask():
    rows = lax.broadcasted_iota(jnp.int32, (552, LANES), 0)
    lanes = lax.broadcasted_iota(jnp.int32, (552, LANES), 1)
    return jnp.where(jnp.logical_and(rows == 8, lanes == N_HEADS), 0.0, 1.0).astype(F32)
```

```python
import functools

import jax
import jax.numpy as jnp
from jax import lax
from jax.experimental import pallas as pl
from jax.experimental.pallas import tpu as pltpu

F32 = jnp.float32
BF16 = jnp.bfloat16

D_MODEL = 1024
POOL_WINDOWS = (2, 4, 8, 16)
POOL_GROUPS = 4
POOL_GROUP_DIM = 128
POOL_WIDTH = 512
HEAD_DIM = 64
N_HEADS = 8
ATTN_WIDTH = 512
D_FF = 2816
RMS_EPS = 1e-6
ATTN_SCALE = HEAD_DIM ** -0.5
NEG_BIG = -1e30

ADAM_LR = 0.001
ADAM_B1 = 0.9
ADAM_B2 = 0.999
ADAM_EPS = 1e-08
ADAM_WD = 0.01
ADAM_STEP = 10

LANES = 128
N_CHIPS = 4
N_DEV = 8
VMEM_LIMIT_V7X = 52 * 1024 * 1024
MESH = pl.DeviceIdType.MESH
ANY = pl.BlockSpec(memory_space=pl.ANY)


def _cparams(*sem):
    return pltpu.CompilerParams(dimension_semantics=sem if sem else None, vmem_limit_bytes=VMEM_LIMIT_V7X)


def _dep_list(dep):
    return [] if dep is None else (list(dep) if isinstance(dep, (list, tuple)) else [dep])


def _after(body, n_in, dep):
    k = len(_dep_list(dep))
    if k == 0:
        return body

    def wrapped(*refs):
        body(*refs[:n_in], *refs[n_in + k:])

    return wrapped


def _dep_args(dep):
    deps = _dep_list(dep)
    return [ANY] * len(deps), deps


def _dot(a, b):
    return lax.dot_general(a, b, (((1,), (0,)), ((), ())), preferred_element_type=F32)


def _dot_nt(a, b):
    return lax.dot_general(a, b, (((1,), (1,)), ((), ())), preferred_element_type=F32)


def _dot_tn(a, b):
    return lax.dot_general(a, b, (((0,), (0,)), ((), ())), preferred_element_type=F32)


def _sigmoid(x):
    return jax.nn.sigmoid(x)


def _rms_fwd(x, g):
    r = lax.rsqrt(jnp.mean(x * x, axis=-1, keepdims=True) + RMS_EPS)
    return (x * r) * g


def _rms_bwd(x, g, dy):
    r = lax.rsqrt(jnp.mean(x * x, axis=-1, keepdims=True) + RMS_EPS)
    xh = x * r
    dg = jnp.sum(dy * xh, axis=0, keepdims=True)
    dxh = dy * g
    dx = r * (dxh - xh * jnp.mean(dxh * xh, axis=-1, keepdims=True))
    return dx, dg


def _matmul(name, a, b, mode, out_dtype, tm, tn, tk, dep=None):
    if mode == "nn":
        (m, k), (_, n) = a.shape, b.shape
    elif mode == "nt":
        (m, k), (n, _) = a.shape, b.shape
    else:
        (k, m), (_, n) = a.shape, b.shape
    tm, tn, tk = min(tm, m), min(tn, n), min(tk, k)
    assert m % tm == 0 and n % tn == 0 and k % tk == 0, (name, m, n, k, tm, tn, tk)
    nk = k // tk
    if mode == "tn":
        a_spec = pl.BlockSpec((tk, tm), lambda i, j, kk: (kk, i))
    else:
        a_spec = pl.BlockSpec((tm, tk), lambda i, j, kk: (i, kk))
    if mode == "nt":
        b_spec = pl.BlockSpec((tn, tk), lambda i, j, kk: (j, kk))
    else:
        b_spec = pl.BlockSpec((tk, tn), lambda i, j, kk: (kk, j))
    dot = {"nn": _dot, "nt": _dot_nt, "tn": _dot_tn}[mode]
    use_scratch = nk > 1 and out_dtype != F32

    def body(a_ref, b_ref, o_ref, *scratch):
        prod = dot(a_ref[...].astype(BF16), b_ref[...].astype(BF16))
        if nk == 1:
            o_ref[...] = prod.astype(out_dtype)
            return
        acc = scratch[0] if use_scratch else o_ref
        kk = pl.program_id(2)

        @pl.when(kk == 0)
        def _():
            acc[...] = prod

        @pl.when(kk > 0)
        def _():
            acc[...] += prod

        if use_scratch:
            @pl.when(kk == nk - 1)
            def _():
                o_ref[...] = acc[...].astype(out_dtype)

    dep_specs, dep_ops = _dep_args(dep)
    return pl.pallas_call(
        _after(body, 2, dep),
        name=name,
        out_shape=jax.ShapeDtypeStruct((m, n), out_dtype),
        grid=(m // tm, n // tn, nk),
        in_specs=[a_spec, b_spec] + dep_specs,
        out_specs=pl.BlockSpec((tm, tn), lambda i, j, kk: (i, j)),
        scratch_shapes=[pltpu.VMEM((tm, tn), F32)] if use_scratch else [],
        compiler_params=_cparams("parallel", "parallel", "arbitrary"),
    )(a, b, *dep_ops)


def _norm_fwd(name, x, g, tm):
    t, d = x.shape
    tm = min(tm, t)

    def body(x_ref, g_ref, h_ref):
        h_ref[...] = _rms_fwd(x_ref[...], g_ref[...]).astype(BF16)

    return pl.pallas_call(
        body, name=name, out_shape=jax.ShapeDtypeStruct((t, d), BF16), grid=(t // tm,),
        in_specs=[pl.BlockSpec((tm, d), lambda i: (i, 0)), pl.BlockSpec((1, d), lambda i: (0, 0))],
        out_specs=pl.BlockSpec((tm, d), lambda i: (i, 0)),
        compiler_params=_cparams("parallel"),
    )(x, g)


def _split3(x):
    hi = x.astype(BF16)
    r1 = x - hi.astype(F32)
    mid = r1.astype(BF16)
    lo = (r1 - mid.astype(F32)).astype(BF16)
    return hi, mid, lo


def _tri_dot(tri, x):
    hi, mid, lo = _split3(x)
    return _dot(tri, hi) + _dot(tri, mid) + _dot(tri, lo)


def _forget_fwd(h, wf, bf, seq):
    t, d = h.shape
    cb = min(256, seq)

    def body(h_ref, wf_ref, bf_ref, fl_ref, fc_ref):
        fl = _dot(h_ref[...], wf_ref[...])
        fl_ref[...] = fl
        xx = fl + bf_ref[...]
        lf = jnp.minimum(xx, 0.0) - jnp.log(1.0 + jnp.exp(-jnp.abs(xx)))
        ri = lax.broadcasted_iota(jnp.int32, (cb, cb), 0)
        ci = lax.broadcasted_iota(jnp.int32, (cb, cb), 1)
        tri = (ri >= ci).astype(BF16)
        carry = jnp.zeros((1, LANES), F32)
        for blk in range(seq // cb):
            cs = _tri_dot(tri, lf[blk * cb:(blk + 1) * cb]) + carry
            fc_ref[blk * cb:(blk + 1) * cb, :] = cs
            carry = cs[cb - 1:cb, :]

    return pl.pallas_call(
        body, name="forget_fwd",
        out_shape=(jax.ShapeDtypeStruct((t, LANES), F32), jax.ShapeDtypeStruct((t, LANES), F32)),
        grid=(t // seq,),
        in_specs=[pl.BlockSpec((seq, d), lambda b: (b, 0)), pl.BlockSpec((d, LANES), lambda b: (0, 0)),
                  pl.BlockSpec((1, LANES), lambda b: (0, 0))],
        out_specs=(pl.BlockSpec((seq, LANES), lambda b: (b, 0)), pl.BlockSpec((seq, LANES), lambda b: (b, 0))),
        compiler_params=_cparams("parallel"),
    )(h, wf, bf)


def _pool_fwd(u, mix, scale, seq):
    t = u.shape[0]

    def body(u_ref, mix_ref, sc_ref, p_ref, ps_ref):
        tpos = lax.broadcasted_iota(jnp.int32, (seq, POOL_GROUP_DIM), 0)
        for g in range(POOL_GROUPS):
            sl = slice(g * POOL_GROUP_DIM, (g + 1) * POOL_GROUP_DIM)
            ug = u_ref[:, sl]
            s = ug
            for lvl in range(g + 1):
                d = 2 ** lvl
                s = s + jnp.where(tpos >= d, pltpu.roll(s, d, 0), 0.0)
            cnt = jnp.minimum(tpos + 1, POOL_WINDOWS[g]).astype(F32)
            pb = (s / cnt - ug).astype(BF16)
            p_ref[:, sl] = pb
            ps_ref[:, sl] = (_dot(pb, mix_ref[g]) * sc_ref[:, sl]).astype(BF16)

    return pl.pallas_call(
        body, name="pool_fwd",
        out_shape=(jax.ShapeDtypeStruct((t, POOL_WIDTH), BF16), jax.ShapeDtypeStruct((t, POOL_WIDTH), BF16)),
        grid=(t // seq,),
        in_specs=[pl.BlockSpec((seq, POOL_WIDTH), lambda b: (b, 0)),
                  pl.BlockSpec((POOL_GROUPS, POOL_GROUP_DIM, POOL_GROUP_DIM), lambda b: (0, 0, 0)),
                  pl.BlockSpec((1, POOL_WIDTH), lambda b: (0, 0))],
        out_specs=(pl.BlockSpec((seq, POOL_WIDTH), lambda b: (b, 0)), pl.BlockSpec((seq, POOL_WIDTH), lambda b: (b, 0))),
        compiler_params=_cparams("parallel"),
    )(u, mix, scale)


def _aug_constants():
    w = N_HEADS * LANES
    rows = jnp.arange(3 * LANES)
    piece, head = rows // LANES, rows % LANES
    cols = jnp.arange(w)
    live = (head < N_HEADS)[:, None]
    pq = (live & (cols[None, :] == (head * LANES + HEAD_DIM + piece)[:, None])).astype(BF16)
    pk = -(live & (cols[None, :] == (head * LANES + HEAD_DIM + 3 + piece)[:, None])).astype(BF16)
    lane = cols % LANES
    oq = ((lane >= HEAD_DIM + 3) & (lane < HEAD_DIM + 6)).astype(F32)[None, :]
    ok = ((lane >= HEAD_DIM) & (lane < HEAD_DIM + 3)).astype(F32)[None, :]
    return pq, pk, oq, ok


def _head_blocks(wt):
    d = wt.shape[1]
    return jnp.pad(wt.reshape(N_HEADS, HEAD_DIM, d), ((0, 0), (0, LANES - HEAD_DIM), (0, 0))).reshape(N_HEADS * LANES, d)


def _attn_prep(h, wq, wk, wv, fcum, tm):
    t, d = h.shape
    tm = min(tm, t)
    w = N_HEADS * LANES
    pq, pk, oq, ok = _aug_constants()

    def body(h_ref, wq_ref, wk_ref, wv_ref, f_ref, pq_ref, pk_ref, oq_ref, ok_ref, qa_ref, ka_ref, v_ref):
        hh = h_ref[...]
        fs = jnp.concatenate(_split3(f_ref[...]), axis=1)
        q = _dot_nt(hh, wq_ref[...]).astype(BF16).astype(F32) * ATTN_SCALE
        qa_ref[...] = (q + _dot(fs, pq_ref[...]) + oq_ref[...]).astype(BF16)
        k = _dot_nt(hh, wk_ref[...]).astype(BF16).astype(F32)
        ka_ref[...] = (k + _dot(fs, pk_ref[...]) + ok_ref[...]).astype(BF16)
        v_ref[...] = _dot_nt(hh, wv_ref[...]).astype(BF16)

    row = lambda n: pl.BlockSpec((tm, n), lambda i: (i, 0))
    full = lambda a: pl.BlockSpec(a.shape, lambda i: (0, 0))
    return pl.pallas_call(
        body, name="attn_prep",
        out_shape=(jax.ShapeDtypeStruct((t, w), BF16), jax.ShapeDtypeStruct((t, w), BF16),
                   jax.ShapeDtypeStruct((t, ATTN_WIDTH), BF16)),
        grid=(t // tm,),
        in_specs=[row(d), full(wq), full(wk), full(wv), row(LANES), full(pq), full(pk), full(oq), full(ok)],
        out_specs=(row(w), row(w), row(ATTN_WIDTH)),
        compiler_params=_cparams("parallel"),
    )(h, wq, wk, wv, fcum, pq, pk, oq, ok)


def _fold_lanes(x, op):
    out = x[:, :LANES]
    for g in range(1, x.shape[1] // LANES):
        out = op(out, x[:, g * LANES:(g + 1) * LANES])
    return out


def _attn_fwd(qa, ka, v, seq, tq, dep=None):
    t = qa.shape[0]
    nq = seq // tq
    hp_n = N_HEADS // 2
    heads = [slice(e * LANES, (e + 1) * LANES) for e in range(2)]

    def body(q_ref, k_ref, v_ref, o_ref, lse_ref, s_buf):
        i = pl.program_id(2)
        diag_ok = lax.broadcasted_iota(jnp.int32, (tq, tq), 0) >= lax.broadcasted_iota(jnp.int32, (tq, tq), 1)
        qs = [q_ref[:, hl] for hl in heads]

        def sweep1(j, mxs):
            r0 = pl.multiple_of(j * tq, tq)
            out = []
            for e, hl in enumerate(heads):
                s = _dot_nt(qs[e], k_ref[pl.ds(r0, tq), hl])
                s = jnp.where(jnp.logical_or(diag_ok, j < i), s, NEG_BIG)
                s_buf[e, j] = s
                out.append(jnp.maximum(mxs[e], _fold_lanes(s, jnp.maximum)))
            return tuple(out)

        mxs = lax.fori_loop(0, i + 1, sweep1, (jnp.full((tq, LANES), NEG_BIG, F32),) * 2)
        ms = [jnp.max(mx, axis=1, keepdims=True) for mx in mxs]

        def sweep2(j, carry):
            r0 = pl.multiple_of(j * tq, tq)
            vv = v_ref[pl.ds(r0, tq), :]
            out = []
            for e in range(2):
                p = jnp.exp(s_buf[e, j] - ms[e])
                out += [carry[2 * e] + _fold_lanes(p, jnp.add), carry[2 * e + 1] + _dot(p.astype(BF16), vv)]
            return tuple(out)

        res = lax.fori_loop(0, i + 1, sweep2, (jnp.zeros((tq, LANES), F32),) * 4)
        outs = []
        for e in range(2):
            l = jnp.sum(res[2 * e], axis=1, keepdims=True)
            outs.append(res[2 * e + 1] / l)
            lse_ref[:, e:e + 1] = ms[e] + jnp.log(l)
        lane = lax.broadcasted_iota(jnp.int32, (tq, LANES), 1)
        o_ref[...] = jnp.where(lane < HEAD_DIM, outs[0], outs[1])

    dep_specs, dep_ops = _dep_args(dep)
    return pl.pallas_call(
        _after(body, 3, dep), name="attn_fwd",
        out_shape=(jax.ShapeDtypeStruct((t, ATTN_WIDTH), F32), jax.ShapeDtypeStruct((hp_n, t, 2), F32)),
        grid=(t // seq, hp_n, nq),
        in_specs=[pl.BlockSpec((tq, 2 * LANES), lambda b, hp, i: (b * nq + i, hp)),
                  pl.BlockSpec((seq, 2 * LANES), lambda b, hp, i: (b, hp)),
                  pl.BlockSpec((seq, LANES), lambda b, hp, i: (b, hp))] + dep_specs,
        out_specs=(pl.BlockSpec((tq, LANES), lambda b, hp, i: (b * nq + i, hp)),
                   pl.BlockSpec((None, tq, 2), lambda b, hp, i: (hp, b * nq + i, 0))),
        scratch_shapes=[pltpu.VMEM((2, nq, tq, tq), F32)],
        compiler_params=_cparams("parallel", "parallel", "arbitrary"),
    )(qa, ka, v, *dep_ops)


def _merge_fwd(x, ps, o, g2, wpo, wao, wout, tm):
    t, d = x.shape
    tm = min(tm, t)

    def body(x_ref, ps_ref, o_ref, gp_ref, ga_ref, wpo_ref, wao_ref, wout_ref, mg_ref, x1_ref):
        py = _dot(ps_ref[...], wpo_ref[...])
        ay = _dot(o_ref[...].astype(BF16), wao_ref[...])
        mb = (_sigmoid(gp_ref[...].astype(F32)) * py + _sigmoid(ga_ref[...].astype(F32)) * ay).astype(BF16)
        mg_ref[...] = mb
        x1_ref[...] = x_ref[...] + _dot(mb, wout_ref[...])

    row = lambda w: pl.BlockSpec((tm, w), lambda i: (i, 0))
    full = lambda a: pl.BlockSpec(a.shape, lambda i: (0, 0))
    return pl.pallas_call(
        body, name="merge_fwd",
        out_shape=(jax.ShapeDtypeStruct((t, d), BF16), jax.ShapeDtypeStruct((t, d), F32)),
        grid=(t // tm,),
        in_specs=[row(d), row(POOL_WIDTH), row(ATTN_WIDTH), pl.BlockSpec((tm, d), lambda i: (i, 0)),
                  pl.BlockSpec((tm, d), lambda i: (i, 1)), full(wpo), full(wao), full(wout)],
        out_specs=(row(d), row(d)),
        compiler_params=_cparams("parallel"),
    )(x, ps, o, g2, g2, wpo, wao, wout)


def _ffn_fwd(x1, g, wg, wu, wd, tm, tf):
    t, d = x1.shape
    f = wg.shape[0]
    tm = min(tm, t)
    nf = f // tf

    def body(x1_ref, g_ref, wg_ref, wu_ref, wd_ref, h2_ref, gt_ref, up_ref, act_ref, x2_ref):
        j = pl.program_id(1)

        @pl.when(j == 0)
        def _():
            h2_ref[...] = _rms_fwd(x1_ref[...], g_ref[...]).astype(BF16)

        h2 = h2_ref[...]
        gt = _dot_nt(h2, wg_ref[...])
        up = _dot_nt(h2, wu_ref[...])
        sg = _sigmoid(gt)
        silu = gt * sg
        act = (silu * up).astype(BF16)
        gt_ref[...] = (up * (sg * (1.0 + gt * (1.0 - sg)))).astype(BF16)
        up_ref[...] = silu.astype(BF16)
        act_ref[...] = act
        prod = _dot(act, wd_ref[...])

        @pl.when(j == 0)
        def _():
            x2_ref[...] = prod

        @pl.when(j > 0)
        def _():
            x2_ref[...] += prod

        @pl.when(j == nf - 1)
        def _():
            x2_ref[...] += x1_ref[...]

    return pl.pallas_call(
        body, name="ffn_fwd",
        out_shape=(jax.ShapeDtypeStruct((t, d), BF16), jax.ShapeDtypeStruct((t, f), BF16),
                   jax.ShapeDtypeStruct((t, f), BF16), jax.ShapeDtypeStruct((t, f), BF16),
                   jax.ShapeDtypeStruct((t, d), F32)),
        grid=(t // tm, nf),
        in_specs=[pl.BlockSpec((tm, d), lambda i, j: (i, 0)), pl.BlockSpec((1, d), lambda i, j: (0, 0)),
                  pl.BlockSpec((tf, d), lambda i, j: (j, 0)), pl.BlockSpec((tf, d), lambda i, j: (j, 0)),
                  pl.BlockSpec((tf, d), lambda i, j: (j, 0))],
        out_specs=(pl.BlockSpec((tm, d), lambda i, j: (i, 0)), pl.BlockSpec((tm, tf), lambda i, j: (i, j)),
                   pl.BlockSpec((tm, tf), lambda i, j: (i, j)), pl.BlockSpec((tm, tf), lambda i, j: (i, j)),
                   pl.BlockSpec((tm, d), lambda i, j: (i, 0))),
        compiler_params=_cparams("parallel", "arbitrary"),
    )(x1, g, wg, wu, wd)


def _final_fwd_bwd(x2, target, g, tm):
    t, d = x2.shape
    tm = min(tm, t)

    def body(x_ref, t_ref, g_ref, loss_ref, dx_ref, dg_ref):
        i = pl.program_id(0)
        x = x_ref[...]
        gg = g_ref[...]
        err = _rms_fwd(x, gg) - t_ref[...]
        part = 0.5 * jnp.sum(jnp.mean(err * err, axis=-1, keepdims=True), axis=0, keepdims=True)
        dx, dg = _rms_bwd(x, gg, err * (1.0 / d))
        dx_ref[...] = dx

        @pl.when(i == 0)
        def _():
            loss_ref[...] = jnp.zeros_like(loss_ref)
            dg_ref[...] = jnp.zeros_like(dg_ref)

        loss_ref[...] += jnp.broadcast_to(part, loss_ref.shape)
        dg_ref[...] += dg

    return pl.pallas_call(
        body, name="final_fwd_bwd",
        out_shape=(jax.ShapeDtypeStruct((1, LANES), F32), jax.ShapeDtypeStruct((t, d), F32),
                   jax.ShapeDtypeStruct((1, d), F32)),
        grid=(t // tm,),
        in_specs=[pl.BlockSpec((tm, d), lambda i: (i, 0)), pl.BlockSpec((tm, d), lambda i: (i, 0)),
                  pl.BlockSpec((1, d), lambda i: (0, 0))],
        out_specs=(pl.BlockSpec((1, LANES), lambda i: (0, 0)), pl.BlockSpec((tm, d), lambda i: (i, 0)),
                   pl.BlockSpec((1, d), lambda i: (0, 0))),
        compiler_params=_cparams("arbitrary"),
    )(x2, target, g)


def _ffn_bwd(dx2, x1, g, gt, up, wg, wu, wd, tm, tf):
    t, d = dx2.shape
    f = gt.shape[1]
    tm = min(tm, t)
    nf = f // tf

    def body(dx2_ref, x1_ref, g_ref, gt_ref, up_ref, wg_ref, wu_ref, wd_ref, dgt_ref, dup_ref, dx1_ref, dg_ref, acc_ref,
             dxb_ref):
        i, j = pl.program_id(0), pl.program_id(1)

        @pl.when(j == 0)
        def _():
            dxb_ref[...] = dx2_ref[...].astype(BF16)

        dact = _dot_nt(dxb_ref[...], wd_ref[...])
        dgt = (dact * gt_ref[...].astype(F32)).astype(BF16)
        dup = (dact * up_ref[...].astype(F32)).astype(BF16)
        dgt_ref[...] = dgt
        dup_ref[...] = dup
        contrib = _dot(dgt, wg_ref[...]) + _dot(dup, wu_ref[...])

        @pl.when(j == 0)
        def _():
            acc_ref[...] = contrib

        @pl.when(j > 0)
        def _():
            acc_ref[...] += contrib

        @pl.when(jnp.logical_and(i == 0, j == 0))
        def _():
            dg_ref[...] = jnp.zeros_like(dg_ref)

        @pl.when(j == nf - 1)
        def _():
            dxn, dg = _rms_bwd(x1_ref[...], g_ref[...], acc_ref[...])
            dx1_ref[...] = dx2_ref[...] + dxn
            dg_ref[...] += dg

    return pl.pallas_call(
        body, name="ffn_bwd",
        out_shape=(jax.ShapeDtypeStruct((t, f), BF16), jax.ShapeDtypeStruct((t, f), BF16),
                   jax.ShapeDtypeStruct((t, d), F32), jax.ShapeDtypeStruct((1, d), F32)),
        grid=(t // tm, nf),
        in_specs=[pl.BlockSpec((tm, d), lambda i, j: (i, 0)), pl.BlockSpec((tm, d), lambda i, j: (i, 0)),
                  pl.BlockSpec((1, d), lambda i, j: (0, 0)),
                  pl.BlockSpec((tm, tf), lambda i, j: (i, j)), pl.BlockSpec((tm, tf), lambda i, j: (i, j)),
                  pl.BlockSpec((tf, d), lambda i, j: (j, 0)), pl.BlockSpec((tf, d), lambda i, j: (j, 0)),
                  pl.BlockSpec((tf, d), lambda i, j: (j, 0))],
        out_specs=(pl.BlockSpec((tm, tf), lambda i, j: (i, j)), pl.BlockSpec((tm, tf), lambda i, j: (i, j)),
                   pl.BlockSpec((tm, d), lambda i, j: (i, 0)), pl.BlockSpec((1, d), lambda i, j: (0, 0))),
        scratch_shapes=[pltpu.VMEM((tm, d), F32), pltpu.VMEM((tm, d), BF16)],
        compiler_params=_cparams("arbitrary", "arbitrary"),
    )(dx2, x1, g, gt, up, wg, wu, wd)


def _merge_bwd(dx1, ps, o, g2, wpo, wao, wout, tm, dep=None):
    t, d = dx1.shape
    tm = min(tm, t)

    def body(dx1_ref, ps_ref, o_ref, gp_ref, ga_ref, wpo_ref, wao_ref, wout_ref, dpy_ref, day_ref, dg2_ref, dps_ref, da_ref):
        dm = _dot_nt(dx1_ref[...].astype(BF16), wout_ref[...])
        py = _dot(ps_ref[...], wpo_ref[...])
        ay = _dot(o_ref[...].astype(BF16), wao_ref[...])
        sp = _sigmoid(gp_ref[...].astype(F32))
        sa = _sigmoid(ga_ref[...].astype(F32))
        dpy = (dm * sp).astype(BF16)
        day = (dm * sa).astype(BF16)
        dpy_ref[...] = dpy
        day_ref[...] = day
        dg2_ref[:, :d] = (dm * py * (sp * (1.0 - sp))).astype(BF16)
        dg2_ref[:, d:] = (dm * ay * (sa * (1.0 - sa))).astype(BF16)
        dps_ref[...] = _dot_nt(dpy, wpo_ref[...])
        da_ref[...] = _dot_nt(day, wao_ref[...]).astype(BF16)

    row = lambda w: pl.BlockSpec((tm, w), lambda i: (i, 0))
    full = lambda a: pl.BlockSpec(a.shape, lambda i: (0, 0))
    dep_specs, dep_ops = _dep_args(dep)
    return pl.pallas_call(
        _after(body, 8, dep), name="merge_bwd",
        out_shape=(jax.ShapeDtypeStruct((t, d), BF16), jax.ShapeDtypeStruct((t, d), BF16),
                   jax.ShapeDtypeStruct((t, 2 * d), BF16), jax.ShapeDtypeStruct((t, POOL_WIDTH), F32),
                   jax.ShapeDtypeStruct((t, ATTN_WIDTH), BF16)),
        grid=(t // tm,),
        in_specs=[row(d), row(POOL_WIDTH), row(ATTN_WIDTH), pl.BlockSpec((tm, d), lambda i: (i, 0)),
                  pl.BlockSpec((tm, d), lambda i: (i, 1)), full(wpo), full(wao), full(wout)] + dep_specs,
        out_specs=(row(d), row(d), row(2 * d), row(POOL_WIDTH), row(ATTN_WIDTH)),
        compiler_params=_cparams("parallel"),
    )(dx1, ps, o, g2, g2, wpo, wao, wout, *dep_ops)


def _attn_bwd(qa, ka, v, do, lse4, seq, tq, dep=None):
    t = qa.shape[0]
    nq = seq // tq
    hp_n = N_HEADS // 2
    heads = [slice(e * LANES, (e + 1) * LANES) for e in range(2)]

    def body(q_ref, k_ref, v_ref, do_ref, lse_ref, dq_ref, dk_ref, dv_ref, dfr_ref, dk_acc, dv_acc, p_buf, dp_buf):
        diag_ok = lax.broadcasted_iota(jnp.int32, (tq, tq), 0) >= lax.broadcasted_iota(jnp.int32, (tq, tq), 1)
        lane_q = lax.broadcasted_iota(jnp.int32, (tq, LANES), 1)
        lane_s = lax.broadcasted_iota(jnp.int32, (seq, LANES), 1)
        mine_q = [lane_q < HEAD_DIM, lane_q >= HEAD_DIM]
        dv_acc[...] = jnp.zeros_like(dv_acc)
        dk_acc[...] = jnp.zeros_like(dk_acc)
        dfr_ref[...] = jnp.zeros_like(dfr_ref)

        def q_step(i, _):
            q0 = pl.multiple_of(i * tq, tq)
            qs = [q_ref[pl.ds(q0, tq), hl] for hl in heads]
            dov = do_ref[pl.ds(q0, tq), :]
            dos = [jnp.where(mq, dov, jnp.zeros((), BF16)) for mq in mine_q]
            lss = [lse_ref[pl.ds(q0, tq), e:e + 1] for e in range(2)]

            def sweep1(j, dls):
                r0 = pl.multiple_of(j * tq, tq)
                vv = v_ref[pl.ds(r0, tq), :]
                out = []
                for e, hl in enumerate(heads):
                    s = _dot_nt(qs[e], k_ref[pl.ds(r0, tq), hl])
                    s = jnp.where(jnp.logical_or(diag_ok, j < i), s, NEG_BIG)
                    p = jnp.exp(s - lss[e])
                    dp = _dot_nt(dos[e], vv)
                    p_buf[e, j] = p
                    dp_buf[e, j] = dp
                    dv_acc[pl.ds(r0, tq), :] += _dot_tn(p.astype(BF16), dos[e])
                    out.append(dls[e] + _fold_lanes(p * dp, jnp.add))
                return tuple(out)

            dls = lax.fori_loop(0, i + 1, sweep1, (jnp.zeros((tq, LANES), F32),) * 2)
            dls = [jnp.sum(d, axis=1, keepdims=True) for d in dls]

            def sweep2(j, dqs):
                r0 = pl.multiple_of(j * tq, tq)
                out = []
                for e, hl in enumerate(heads):
                    ds = p_buf[e, j] * (dp_buf[e, j] - dls[e])
                    dfr_ref[e, pl.ds(j, 1), :] += jnp.sum(ds, axis=0, keepdims=True)
                    dsb = ds.astype(BF16)
                    dk_acc[e, pl.ds(r0, tq), :] += _dot_tn(dsb, qs[e])
                    out.append(dqs[e] + _dot(dsb, k_ref[pl.ds(r0, tq), hl]))
                return tuple(out)

            dqs = lax.fori_loop(0, i + 1, sweep2, (jnp.zeros((tq, LANES), F32),) * 2)
            dq = jnp.where(mine_q[0], dqs[0], pltpu.roll(dqs[1], HEAD_DIM, 1)) * ATTN_SCALE
            dq_ref[pl.ds(q0, tq), :] = dq.astype(BF16)
            return 0

        lax.fori_loop(0, nq, q_step, 0)
        dk_ref[...] = jnp.where(lane_s < HEAD_DIM, dk_acc[0], pltpu.roll(dk_acc[1], HEAD_DIM, 1)).astype(BF16)
        dv_ref[...] = dv_acc[...].astype(BF16)

    wide = pl.BlockSpec((seq, 2 * LANES), lambda b, hp: (b, hp))
    col = pl.BlockSpec((seq, LANES), lambda b, hp: (b, hp))
    pair = pl.BlockSpec((None, seq, 2), lambda b, hp: (hp, b, 0))
    dep_specs, dep_ops = _dep_args(dep)
    return pl.pallas_call(
        _after(body, 5, dep), name="attn_bwd",
        out_shape=(jax.ShapeDtypeStruct((t, ATTN_WIDTH), BF16),) * 3 + (jax.ShapeDtypeStruct((N_HEADS, t // tq, tq), F32),),
        grid=(t // seq, hp_n),
        in_specs=[wide, wide, col, col, pair] + dep_specs,
        out_specs=(col, col, col, pl.BlockSpec((2, nq, tq), lambda b, hp: (hp, b, 0))),
        scratch_shapes=[pltpu.VMEM((2, seq, LANES), F32), pltpu.VMEM((seq, LANES), F32),
                        pltpu.VMEM((2, nq, tq, tq), F32), pltpu.VMEM((2, nq, tq, tq), F32)],
        compiler_params=_cparams("parallel", "arbitrary"),
    )(qa, ka, v, do, lse4, *dep_ops)


def _forget_bwd(dfc, fl, bf, seq):
    t = fl.shape[0]
    cb = min(256, seq)
    nb = seq // cb

    def body(dfc_ref, fl_ref, bf_ref, dfl_ref, db_ref):
        b = pl.program_id(0)
        ri = lax.broadcasted_iota(jnp.int32, (cb, cb), 0)
        ci = lax.broadcasted_iota(jnp.int32, (cb, cb), 1)
        tri = (ci >= ri).astype(BF16)
        carry = jnp.zeros((1, LANES), F32)
        dbs = jnp.zeros((1, LANES), F32)
        for blk in reversed(range(nb)):
            rs = slice(blk * cb, (blk + 1) * cb)
            dlf = _tri_dot(tri, -dfc_ref[rs, :]) + carry
            carry = dlf[0:1, :]
            dfl = dlf * _sigmoid(-(fl_ref[rs, :] + bf_ref[...]))
            dfl_ref[rs, :] = dfl.astype(BF16)
            dbs = dbs + jnp.sum(dfl, axis=0, keepdims=True)

        @pl.when(b == 0)
        def _():
            db_ref[...] = jnp.zeros_like(db_ref)

        db_ref[...] += dbs

    return pl.pallas_call(
        body, name="forget_bwd",
        out_shape=(jax.ShapeDtypeStruct((t, LANES), BF16), jax.ShapeDtypeStruct((1, LANES), F32)),
        grid=(t // seq,),
        in_specs=[pl.BlockSpec((seq, LANES), lambda b: (b, 0)), pl.BlockSpec((seq, LANES), lambda b: (b, 0)),
                  pl.BlockSpec((1, LANES), lambda b: (0, 0))],
        out_specs=(pl.BlockSpec((seq, LANES), lambda b: (b, 0)), pl.BlockSpec((1, LANES), lambda b: (0, 0))),
        compiler_params=_cparams("arbitrary"),
    )(dfc, fl, bf)


def _pool_bwd(dps, p, mix, scale, seq):
    t = dps.shape[0]

    def body(dps_ref, p_ref, mix_ref, sc_ref, du_ref, dmix_ref, dsc_ref):
        b = pl.program_id(0)

        @pl.when(b == 0)
        def _():
            dmix_ref[...] = jnp.zeros_like(dmix_ref)
            dsc_ref[...] = jnp.zeros_like(dsc_ref)

        tpos = lax.broadcasted_iota(jnp.int32, (seq, POOL_GROUP_DIM), 0)
        for g in range(POOL_GROUPS):
            sl = slice(g * POOL_GROUP_DIM, (g + 1) * POOL_GROUP_DIM)
            pb = p_ref[:, sl]
            dpsg = dps_ref[:, sl]
            pm = _dot(pb, mix_ref[g])
            dsc_ref[:, sl] += jnp.sum(dpsg * pm, axis=0, keepdims=True)
            dpm = (dpsg * sc_ref[:, sl]).astype(BF16)
            dmix_ref[g] += _dot_tn(pb, dpm)
            dp = _dot_nt(dpm, mix_ref[g])
            cnt = jnp.minimum(tpos + 1, POOL_WINDOWS[g]).astype(F32)
            s = dp / cnt
            for lvl in range(g + 1):
                d = 2 ** lvl
                s = s + jnp.where(tpos < seq - d, pltpu.roll(s, seq - d, 0), 0.0)
            du_ref[:, sl] = (s - dp).astype(BF16)

    return pl.pallas_call(
        body, name="pool_bwd",
        out_shape=(jax.ShapeDtypeStruct((t, POOL_WIDTH), BF16),
                   jax.ShapeDtypeStruct((POOL_GROUPS, POOL_GROUP_DIM, POOL_GROUP_DIM), F32),
                   jax.ShapeDtypeStruct((1, POOL_WIDTH), F32)),
        grid=(t // seq,),
        in_specs=[pl.BlockSpec((seq, POOL_WIDTH), lambda b: (b, 0)), pl.BlockSpec((seq, POOL_WIDTH), lambda b: (b, 0)),
                  pl.BlockSpec((POOL_GROUPS, POOL_GROUP_DIM, POOL_GROUP_DIM), lambda b: (0, 0, 0)),
                  pl.BlockSpec((1, POOL_WIDTH), lambda b: (0, 0))],
        out_specs=(pl.BlockSpec((seq, POOL_WIDTH), lambda b: (b, 0)),
                   pl.BlockSpec((POOL_GROUPS, POOL_GROUP_DIM, POOL_GROUP_DIM), lambda b: (0, 0, 0)),
                   pl.BlockSpec((1, POOL_WIDTH), lambda b: (0, 0))),
        compiler_params=_cparams("arbitrary"),
    )(dps, p, mix, scale)


def _in_bwd(du, dq, dk, dv, dg2, dfl, dx1, x, g, wu, wqkv, wg2, wft, tm):
    t, d = x.shape
    tm = min(tm, t)
    aw = ATTN_WIDTH

    def body(du_ref, dq_ref, dk_ref, dv_ref, dg2_ref, dfl_ref, dx1_ref, x_ref, g_ref, wu_ref, wqkv_ref, wg2_ref, wft_ref,
             dx_ref, dg_ref):
        i = pl.program_id(0)
        dh = _dot(du_ref[...], wu_ref[...])
        dh += _dot(dq_ref[...], wqkv_ref[0:aw, :])
        dh += _dot(dk_ref[...], wqkv_ref[aw:2 * aw, :])
        dh += _dot(dv_ref[...], wqkv_ref[2 * aw:3 * aw, :])
        dh += _dot(dg2_ref[...], wg2_ref[...])
        dh += _dot(dfl_ref[...], wft_ref[...])
        dxn, dg = _rms_bwd(x_ref[...], g_ref[...], dh)
        dx_ref[...] = dx1_ref[...] + dxn

        @pl.when(i == 0)
        def _():
            dg_ref[...] = jnp.zeros_like(dg_ref)

        dg_ref[...] += dg

    row = lambda w: pl.BlockSpec((tm, w), lambda i: (i, 0))
    full = lambda a: pl.BlockSpec(a.shape, lambda i: (0, 0))
    return pl.pallas_call(
        body, name="in_bwd",
        out_shape=(jax.ShapeDtypeStruct((t, d), F32), jax.ShapeDtypeStruct((1, d), F32)),
        grid=(t // tm,),
        in_specs=[row(POOL_WIDTH), row(aw), row(aw), row(aw), row(2 * d), row(LANES), row(d), row(d),
                  pl.BlockSpec((1, d), lambda i: (0, 0)), full(wu), full(wqkv), full(wg2), full(wft)],
        out_specs=(row(d), pl.BlockSpec((1, d), lambda i: (0, 0))),
        compiler_params=_cparams("arbitrary"),
    )(du, dq, dk, dv, dg2, dfl, dx1, x, g, wu, wqkv, wg2, wft)


def _position():
    return lax.axis_index("x"), lax.axis_index("y"), lax.axis_index("c")


def _remote(src, dst, send_sem, recv_sem, device):
    return pltpu.make_async_remote_copy(src_ref=src, dst_ref=dst, send_sem=send_sem, recv_sem=recv_sem,
                                        device_id=device, device_id_type=MESH)


HBM = pl.BlockSpec(memory_space=pltpu.HBM)
SEM = pl.BlockSpec(memory_space=pltpu.SEMAPHORE)
DATAFLOW = pltpu.SideEffectType.DATAFLOW_SIDE_EFFECTING


def _copies_start(name, arrays, plan, m, dep=None):
    n = len(arrays)
    arrays = [pltpu.with_memory_space_constraint(a, pltpu.HBM) for a in arrays]

    def body(*refs):
        ins, send_sem, recv_sem, token = refs[:n], refs[n], refs[n + 1], refs[2 * n + 2]
        for i, (src, dst, device, _) in enumerate(plan(ins, *_position())):
            _remote(src, dst, send_sem.at[i], recv_sem.at[i], device).start()
        token[...] = jnp.zeros_like(token)

    dep_specs, dep_ops = _dep_args(dep)
    outs = pl.pallas_call(
        _after(body, n, dep), name=name,
        out_shape=(pltpu.SemaphoreType.DMA((m,)), pltpu.SemaphoreType.DMA((m,)),
                   *[pltpu.HBM(a.shape, a.dtype) for a in arrays], jax.ShapeDtypeStruct((8, LANES), F32)),
        in_specs=[HBM] * n + dep_specs, out_specs=(SEM, SEM, *[HBM] * n, pl.BlockSpec(memory_space=pltpu.VMEM)),
        input_output_aliases={i: i + 2 for i in range(n)},
        compiler_params=pltpu.CompilerParams(has_side_effects=DATAFLOW),
    )(*arrays, *dep_ops)
    return (outs[0], outs[1]), list(outs[2:2 + n]), outs[2 + n]


def _copies_wait(name, sems, arrays, plan, after):
    n = len(arrays)
    afters = list(after) if isinstance(after, (list, tuple)) else [after]

    def body(*refs):
        ins, send_sem, recv_sem = refs[:n], refs[n], refs[n + 1]
        for i, (src, dst, device, landing) in enumerate(plan(ins, *_position())):
            _remote(src, dst, send_sem.at[i], recv_sem.at[i], device).wait_send()
            _remote(landing, landing, send_sem.at[i], recv_sem.at[i], device).wait_recv()

    outs = pl.pallas_call(
        body, name=name,
        out_shape=tuple(pltpu.HBM(a.shape, a.dtype) for a in arrays),
        in_specs=[HBM] * n + [SEM, SEM] + [ANY] * len(afters), out_specs=tuple([HBM] * n),
        input_output_aliases={i: i for i in range(n)},
        compiler_params=pltpu.CompilerParams(has_side_effects=DATAFLOW),
    )(*arrays, sems[0], sems[1], *afters)
    return list(outs)


def _tie(x, dep):
    for token in _dep_list(dep):
        x = x + token[0, 0]
    return x


def _other_chips(x, y):
    return [(1 - x, y), (x, 1 - y), (1 - x, 1 - y)]


def _gather_begin(tag, shards, token, column_halves=False):
    n = len(shards)
    lands = [lax.empty((N_CHIPS,) + s.shape, s.dtype) for s in shards]
    if column_halves:
        cols = lambda ref, h: pl.ds(pl.multiple_of(h * (ref.shape[-1] // 2), LANES), ref.shape[-1] // 2)
        mine = lambda ref, h: ref.at[:, cols(ref, h)]
        landed = lambda ref, chip, h: ref.at[chip, :, cols(ref, h)]
    else:
        mine = lambda ref, h: ref.at[h]
        landed = lambda ref, chip, h: ref.at[chip, h]

    def plan(refs, x, y, c):
        return [(mine(refs[k], c), landed(refs[n + k], 2 * x + y, c), (ox, oy, c), landed(refs[n + k], 2 * ox + oy, c))
                for k in range(n) for ox, oy in _other_chips(x, y)]

    sems, thru, token = _copies_start(f"gather_{tag}_ici_start", list(shards) + lands, plan, 3 * n, dep=token)
    return dict(tag=tag, n=n, plan=plan, sems=sems, arrays=thru, token=token, landed=landed)


def _gather_forward(st, after):
    n, tag, landed = st["n"], st["tag"], st["landed"]
    thru = _copies_wait(f"gather_{tag}_ici_wait", st["sems"], st["arrays"], st["plan"], after)

    def plan(refs, x, y, c):
        return [(landed(refs[k], 2 * ox + oy, c), landed(refs[k], 2 * ox + oy, c), (x, y, 1 - c),
                 landed(refs[k], 2 * ox + oy, 1 - c))
                for k in range(n) for ox, oy in _other_chips(x, y)]

    sems, lands, token = _copies_start(f"gather_{tag}_fwd_start", thru[n:], plan, 3 * n)
    return dict(tag=tag, n=n, plan=plan, sems=sems, arrays=lands, token=token, shards=thru[:n])


def _gather_end(st, after, merge=True):
    lands = _copies_wait(f"gather_{st['tag']}_fwd_wait", st["sems"], st["arrays"], st["plan"], after)
    if not merge:
        return lands, st["shards"]
    me = 2 * lax.axis_index("x") + lax.axis_index("y")
    return [lax.dynamic_update_index_in_dim(g, s, me, 0) for g, s in zip(lands, st["shards"])]


def _add_keep_give(name, pos, a, a_keep, a_give, b, b_keep, b_give, steps):
    r, c = b.shape[-2:]

    def spec(arr, fn):
        lead = arr.ndim - 2

        def index(i, p):
            idx = tuple(fn(i, p))
            return idx if len(idx) == arr.ndim else idx + (0, 0)

        return pl.BlockSpec((None,) * lead + (r, c), index)

    out_spec = pl.BlockSpec((None, r, c), lambda i, p: (i, 0, 0))

    def body(p_ref, ak_ref, bk_ref, ag_ref, bg_ref, keep_ref, give_ref):
        keep_ref[...] = ak_ref[...] + bk_ref[...].astype(F32)
        give_ref[...] = (ag_ref[...] + bg_ref[...].astype(F32)).astype(BF16)

    return pl.pallas_call(
        body, name=name,
        out_shape=(jax.ShapeDtypeStruct((steps, r, c), F32), jax.ShapeDtypeStruct((steps, r, c), BF16)),
        grid_spec=pltpu.PrefetchScalarGridSpec(
            num_scalar_prefetch=1, grid=(steps,),
            in_specs=[spec(a, a_keep), spec(b, b_keep), spec(a, a_give), spec(b, b_give)],
            out_specs=(out_spec, out_spec)),
        compiler_params=_cparams("parallel"),
    )(pos, a, b, a, b)


def _add_last(name, a, b):
    _, r, c = a.shape
    blk = pl.BlockSpec((None, r, c), lambda i: (0, 0, 0))

    def body(a_ref, b_ref, o_ref):
        o_ref[...] = a_ref[...] + b_ref[...].astype(F32)

    return pl.pallas_call(
        body, name=name, out_shape=jax.ShapeDtypeStruct((r, c), F32), grid=(1,), in_specs=[blk, blk],
        out_specs=pl.BlockSpec((r, c), lambda i: (0, 0)), compiler_params=_cparams("arbitrary"),
    )(a, b)


def _exchange_begin(tag, stage, gives, lands, peer_fn, extra):
    n = len(gives)

    def plan(refs, x, y, c):
        return [(refs[k], refs[n + k], peer_fn(x, y, c), refs[n + k]) for k in range(n)]

    sems, thru, token = _copies_start(f"rs{tag}_{stage}_start", gives + lands, plan, n)
    return dict(extra, tag=tag, n=n, stage=stage, plan=plan, sems=sems, arrays=thru, token=token)


def _reduce_begin(tag, grads, column_halves=False):
    n = len(grads)
    if column_halves:
        half = lambda ref, j, h: ref.at[j, :, pl.ds(pl.multiple_of(h * (ref.shape[2] // 2), LANES), ref.shape[2] // 2)]
        lands = [lax.empty((N_CHIPS, g.shape[1], g.shape[2] // 2), F32) for g in grads]
    else:
        half = lambda ref, j, h: ref.at[j, h]
        lands = [lax.empty((N_CHIPS,) + g.shape[2:], F32) for g in grads]

    def plan(refs, x, y, c):
        return [(half(refs[k], j, 1 - c), refs[n + k].at[j], (x, y, 1 - c), refs[n + k].at[j])
                for k in range(n) for j in range(N_CHIPS)]

    sems, thru, token = _copies_start(f"rs{tag}_c_start", list(grads) + lands, plan, N_CHIPS * n)
    return dict(tag=tag, n=n, stage="c", plan=plan, sems=sems, arrays=thru, token=token, column_halves=column_halves)


def _reduce_advance(st, after):
    tag, n, stage = st["tag"], st["n"], st["stage"]
    thru = _copies_wait(f"rs{tag}_{stage}_wait", st["sems"], st["arrays"], st["plan"], after)
    first, recv = thru[:n], thru[n:]
    x, y, c = _position()
    if stage == "c":
        pos = jnp.stack([c, x]).astype(jnp.int32)
        if st["column_halves"]:
            mine = lambda chip: (lambda i, p: (chip(p) + i, 0, p[0]))
        else:
            mine = lambda chip: (lambda i, p: (chip(p) + i, p[0]))
        sums = [_add_keep_give(
            f"rs{tag}_c_add{k}", pos,
            first[k], mine(lambda p: 2 * p[1]), mine(lambda p: 2 * (1 - p[1])),
            recv[k], lambda i, p: (2 * p[1] + i,), lambda i, p: (2 * (1 - p[1]) + i,), 2) for k in range(n)]
        lands = [lax.empty(s[1].shape, BF16) for s in sums]
        return _exchange_begin(tag, "x", [s[1] for s in sums], lands, lambda x, y, c: (1 - x, y, c),
                               dict(keep=[s[0] for s in sums]))
    if stage == "x":
        pos = jnp.stack([y]).astype(jnp.int32)
        sums = [_add_keep_give(
            f"rs{tag}_x_add{k}", pos,
            st["keep"][k], lambda i, p: (p[0],), lambda i, p: (1 - p[0],),
            recv[k], lambda i, p: (p[0],), lambda i, p: (1 - p[0],), 1) for k in range(n)]
        lands = [lax.empty(s[1].shape, BF16) for s in sums]
        return _exchange_begin(tag, "y", [s[1] for s in sums], lands, lambda x, y, c: (x, 1 - y, c),
                               dict(keep=[s[0] for s in sums]))
    if stage == "y":
        mine = [_add_last(f"rs{tag}_y_add{k}", st["keep"][k], recv[k]) for k in range(n)]
        lands = [lax.empty(m.shape, F32) for m in mine]
        return _exchange_begin(tag, "swap", mine, lands, lambda x, y, c: (x, y, 1 - c), {})
    return dict(done=list(zip(first, recv)), token=None)


def _all_reduce_small(v):
    r = v.shape[0]

    def body(v_ref, out_ref, buf, send_sems, recv_sems, local_sem):
        x, y, c = _position()
        me, sibling = (x, y, c), (x, y, 1 - c)
        chips = [(1 - x, y), (x, 1 - y), (1 - x, 1 - y)]

        def rows(px, py, pc):
            return buf.at[pl.ds((4 * px + 2 * py + pc) * r, r), :]

        def copy(k, block, to, src=None):
            return _remote(rows(*block) if src is None else src, rows(*block), send_sems.at[k], recv_sems.at[k], to)

        mine = pltpu.make_async_copy(v_ref, rows(*me), local_sem)
        mine.start()
        first = [copy(0, me, sibling, src=v_ref)]
        first += [copy(1 + j, me, (*chip, c), src=v_ref) for j, chip in enumerate(chips)]
        for cp in first:
            cp.start()
        passed = [copy(4 + j, (*chip, c), sibling) for j, chip in enumerate(chips)]
        for j, chip in enumerate(chips):
            copy(1 + j, (*chip, c), me).wait_recv()
            passed[j].start()
        copy(0, sibling, me).wait_recv()
        for j, chip in enumerate(chips):
            copy(4 + j, (*chip, 1 - c), me).wait_recv()
        for cp in first + passed:
            cp.wait_send()
        mine.wait()
        acc = buf[0:r, :]
        for dev in range(1, N_DEV):
            acc = acc + buf[dev * r:(dev + 1) * r, :]
        out_ref[...] = acc

    return pl.pallas_call(
        body, name="all_reduce_small",
        out_shape=jax.ShapeDtypeStruct(v.shape, F32),
        in_specs=[pl.BlockSpec(memory_space=pltpu.VMEM)],
        out_specs=pl.BlockSpec(memory_space=pltpu.VMEM),
        scratch_shapes=[pltpu.VMEM((N_DEV * r, LANES), F32), pltpu.SemaphoreType.DMA((7,)),
                        pltpu.SemaphoreType.DMA((7,)), pltpu.SemaphoreType.DMA],
        compiler_params=pltpu.CompilerParams(has_side_effects=True, vmem_limit_bytes=VMEM_LIMIT_V7X),
    )(v)


def _adamw_update(w, gg, m, v):
    mn = ADAM_B1 * m + (1.0 - ADAM_B1) * gg
    vn = ADAM_B2 * v + (1.0 - ADAM_B2) * (gg * gg)
    m_hat = mn / (1.0 - ADAM_B1 ** ADAM_STEP)
    v_hat = vn / (1.0 - ADAM_B2 ** ADAM_STEP)
    return -ADAM_LR * (m_hat / (jnp.sqrt(v_hat) + ADAM_EPS) + ADAM_WD * w), mn, vn


def _adamw(name, w, g, m, v):
    def body(w_ref, g_ref, m_ref, v_ref, d_ref, mo_ref, vo_ref):
        d_ref[...], mo_ref[...], vo_ref[...] = _adamw_update(w_ref[...], g_ref[...], m_ref[...], v_ref[...])

    blk = pl.BlockSpec(w.shape, lambda i: (0, 0))
    return pl.pallas_call(
        body, name=name, out_shape=(jax.ShapeDtypeStruct(w.shape, F32),) * 3, grid=(1,),
        in_specs=[blk] * 4, out_specs=(blk,) * 3, compiler_params=_cparams("arbitrary"),
    )(w, g, m, v)


def _rows_to_bf16(name, w):
    r, _, c = w.shape

    def body(w_ref, o_ref):
        o_ref[...] = w_ref[:, 0, :].astype(BF16)

    return pl.pallas_call(
        body, name=name, out_shape=jax.ShapeDtypeStruct((r, c), BF16), grid=(1,),
        in_specs=[pl.BlockSpec((r, 1, c), lambda i: (0, 0, 0))], out_specs=pl.BlockSpec((r, c), lambda i: (0, 0)),
        compiler_params=_cparams("arbitrary"),
    )(w)


def _adamw_rows(name, pos_c, w, g_mine, g_other, m, v):
    r, _, c = w.shape
    ch = c // 2

    def body(p_ref, w_ref, gm_ref, go_ref, m_ref, v_ref, g_ref, d_ref, mo_ref, vo_ref):
        gg = jnp.where(pl.program_id(0) == p_ref[0], gm_ref[...], go_ref[...])
        dl, mn, vn = _adamw_update(w_ref[:, 0, :], gg, m_ref[:, 0, :], v_ref[:, 0, :])
        g_ref[:, 0, :] = gg
        d_ref[:, 0, :] = dl
        mo_ref[:, 0, :] = mn
        vo_ref[:, 0, :] = vn

    rows = pl.BlockSpec((r, 1, ch), lambda h, p: (0, 0, h))
    half = pl.BlockSpec((r, ch), lambda h, p: (0, 0))
    return pl.pallas_call(
        body, name=name, out_shape=(jax.ShapeDtypeStruct(w.shape, F32),) * 4,
        grid_spec=pltpu.PrefetchScalarGridSpec(
            num_scalar_prefetch=1, grid=(2,), in_specs=[rows, half, half, rows, rows], out_specs=(rows,) * 4),
        compiler_params=_cparams("parallel"),
    )(pos_c, w, g_mine, g_other, m, v)


def _adamw_halves(name, pos_c, w, g_mine, g_other, m, v, tr, dep=None):
    r, c = w.shape
    rh = r // 2
    tr = tr if rh % tr == 0 else rh
    nt = rh // tr

    def body(p_ref, w_ref, gm_ref, go_ref, m_ref, v_ref, g_ref, d_ref, mo_ref, vo_ref):
        gg = jnp.where(pl.program_id(0) == p_ref[0], gm_ref[...], go_ref[...])
        g_ref[...] = gg
        d_ref[...], mo_ref[...], vo_ref[...] = _adamw_update(w_ref[...], gg, m_ref[...], v_ref[...])

    full = pl.BlockSpec((tr, c), lambda h, i, p: (h * nt + i, 0))
    half = pl.BlockSpec((tr, c), lambda h, i, p: (i, 0))
    dep_specs, dep_ops = _dep_args(dep)
    return pl.pallas_call(
        _after(body, 6, dep), name=name, out_shape=(jax.ShapeDtypeStruct((r, c), F32),) * 4,
        grid_spec=pltpu.PrefetchScalarGridSpec(
            num_scalar_prefetch=1, grid=(2, nt),
            in_specs=[full, half, half, full, full] + dep_specs, out_specs=(full,) * 4),
        compiler_params=_cparams("parallel", "parallel"),
    )(pos_c, w, g_mine, g_other, m, v, *dep_ops)


def _col_sharded_to_comm(g):
    k, n = g.shape
    return g.reshape(2, k // 2, N_CHIPS, n // N_CHIPS).transpose(2, 0, 1, 3)


def _row_sharded_to_comm(g):
    r, c = g.shape
    return g.reshape(N_CHIPS, 2, r // (2 * N_CHIPS), c)


def _col_sharded_full(g):
    _, _, rh, c = g.shape
    return g.reshape(N_CHIPS, 2 * rh, c).transpose(1, 0, 2).reshape(2 * rh, N_CHIPS * c)


def _row_sharded_full(g):
    _, _, rh, c = g.shape
    return g.reshape(N_CHIPS * 2 * rh, c)


def _chip_rows(w3, start, stop, own=None, me=None):
    r = w3.shape[1]
    parts = []
    for chip in range(N_CHIPS):
        lo, hi = max(start - chip * r, 0), min(stop - chip * r, r)
        if lo < hi:
            part = w3[chip, lo:hi]
            parts.append(part if own is None else jnp.where(me == chip, own[lo:hi], part))
    return parts[0] if len(parts) == 1 else jnp.concatenate(parts, axis=0)


def _pack_small(g1, bfv, mix, scale, g2n, gf, extra=None):
    row8 = jnp.pad(bfv.reshape(1, N_HEADS), ((0, 0), (0, LANES - N_HEADS)))
    if extra is not None:
        row8 = row8 + jnp.pad(extra[:, :1], ((0, 0), (N_HEADS, LANES - N_HEADS - 1)))
    return jnp.concatenate([
        g1.reshape(8, LANES), jnp.pad(row8, ((0, 7), (0, 0))), mix.reshape(512, LANES),
        jnp.pad(scale.reshape(4, LANES), ((0, 4), (0, 0))), g2n.reshape(8, LANES), gf.reshape(8, LANES)], axis=0)


def _unpack_small(s, like):
    g1, bfv, mix, scale, g2n, gf = like
    return (s[0:8].reshape(g1.shape), s[8, :N_HEADS].reshape(bfv.shape), s[16:528].reshape(mix.shape),
            s[528:532].reshape(scale.shape), s[536:544].reshape(g2n.shape), s[544:552].reshape(gf.shape))


class _MeshLinks:
    def __init__(self, shards_in, shards_rest):
        self.gin = _gather_begin("in", shards_in, None, column_halves=True)
        self.grest = _gather_begin("rest", shards_rest, self.gin["token"])
        self.tokens = {"gather": self.grest["token"]}
        self.groups = {}

    @property
    def token(self):
        return list(self.tokens.values())

    def tie(self, x):
        return _tie(x, self.token)

    def weights_in(self, after):
        st = _gather_forward(self.gin, after)
        (g,), (own,) = _gather_end(st, st["token"], merge=False)
        return g, own, 2 * lax.axis_index("x") + lax.axis_index("y")

    def rest_forward(self, after):
        self.grest = _gather_forward(self.grest, after)
        self.tokens["gather"] = self.grest["token"]

    def weights_rest(self, after):
        g = _gather_end(self.grest, after)
        del self.tokens["gather"]
        return [_col_sharded_full(g[0]), _col_sharded_full(g[1])] + [_row_sharded_full(a) for a in g[2:]]

    def reduce_begin(self, tag, grads, column_halves=False):
        self.groups[tag] = _reduce_begin(tag, grads, column_halves)
        self.tokens[tag] = self.groups[tag]["token"]

    def advance(self, after):
        for tag, st in self.groups.items():
            if "done" not in st:
                self.groups[tag] = _reduce_advance(st, after)
                if self.groups[tag]["token"] is None:
                    del self.tokens[tag]
                else:
                    self.tokens[tag] = self.groups[tag]["token"]

    def reduced(self, tag):
        return self.groups[tag]["done"]


class _NoLinks:
    token = None

    def __init__(self, w_in, rest):
        self.w_in, self.rest, self.grads = w_in, rest, {}

    def tie(self, x):
        return x

    def weights_in(self, after):
        return self.w_in, None, None

    def rest_forward(self, after):
        pass

    def weights_rest(self, after):
        return self.rest

    def reduce_begin(self, tag, grads, column_halves=False):
        self.grads[tag] = grads

    def advance(self, after):
        pass


def _local_step(links, x, target, seq, norm1_g, b_forget, pool_mix, pool_scale, norm2_g, norm_f_g):
    t, d = x.shape
    tq = min(256, seq)
    aw = ATTN_WIDTH
    o_q, o_f, o_g = POOL_WIDTH, POOL_WIDTH + 3 * aw, POOL_WIDTH + 3 * aw + N_HEADS
    bf = jnp.pad(b_forget, ((0, 0), (0, LANES - N_HEADS)))
    mixb = pool_mix.astype(BF16)

    h = _norm_fwd("norm1_fwd", x, links.tie(norm1_g), 512)
    w_in, own, me = links.weights_in(h)
    wu = _chip_rows(w_in, 0, o_q, own, me)
    wqkv = _chip_rows(w_in, o_q, o_f, own, me)
    wft = jnp.pad(_chip_rows(w_in, o_f, o_g, own, me), ((0, LANES - N_HEADS), (0, 0)))
    wg2 = _chip_rows(w_in, o_g, N_CHIPS * w_in.shape[1], own, me)
    wf = wft.T
    u = _matmul("mm_u", h, wu, "nt", F32, 1024, 512, d)
    g2 = _matmul("mm_gates", h, wg2, "nt", BF16, 1024, 512, d)
    fl, fcum = _forget_fwd(h, wf, bf, seq)
    qa, ka, v = _attn_prep(h, _head_blocks(wqkv[:aw]), _head_blocks(wqkv[aw:2 * aw]), wqkv[2 * aw:], fcum, 512)
    p, ps = _pool_fwd(u, mixb, pool_scale, seq)
    links.rest_forward([ps, qa, g2])
    o, lse = _attn_fwd(qa, ka, v, seq, tq, dep=links.token)
    w_pool_out, w_attn_out, w_out, w_ffn_gate, w_ffn_up, w_ffn_down = links.weights_rest(o)
    merged, x1 = _merge_fwd(x, ps, o, g2, w_pool_out, w_attn_out, w_out, 256)
    h2, gt, up, act, x2 = _ffn_fwd(x1, norm2_g, w_ffn_gate, w_ffn_up, w_ffn_down, 1024, 256)
    loss, dx2, d_gf = _final_fwd_bwd(x2, target, norm_f_g, 512)

    dgt, dup, dx1, d_g2n = _ffn_bwd(dx2, x1, norm2_g, gt, up, w_ffn_gate, w_ffn_up, w_ffn_down, 1024, 256)
    d_wd = _matmul("dw_down", act, dx2, "tn", F32, 1408, 1024, 1024)
    d_wg = _matmul("dw_gate", dgt, h2, "tn", F32, 1408, 1024, 1024)
    d_wu = _matmul("dw_up", dup, h2, "tn", F32, 1408, 1024, 1024)
    links.reduce_begin("a", [_row_sharded_to_comm(g) for g in (d_wg, d_wu, d_wd)])
    dpy, day, dg2, dps, da = _merge_bwd(dx1, ps, o, g2, w_pool_out, w_attn_out, w_out, 256, dep=links.token)
    links.advance(dps)
    d_wout = _matmul("dw_out", merged, dx1, "tn", F32, 1024, 1024, 1024)
    d_wpo = _matmul("dw_pool_out", ps, dpy, "tn", F32, 512, 1024, 1024)
    d_wao = _matmul("dw_attn_out", o, day, "tn", F32, 512, 1024, 1024)
    links.reduce_begin("m", [_col_sharded_to_comm(d_wpo), _col_sharded_to_comm(d_wao), _row_sharded_to_comm(d_wout)])
    dq, dk, dv, dfr = _attn_bwd(qa, ka, v, da, lse, seq, tq, dep=links.token)
    links.advance(dq)
    dfc = jnp.pad(dfr.reshape(N_HEADS, t).T, ((0, 0), (0, LANES - N_HEADS)))
    dfl, d_bf = _forget_bwd(dfc, fl, bf, seq)
    du, d_mix, d_scale = _pool_bwd(dps, p, mixb, links.tie(pool_scale), seq)
    d_wu_in = _matmul("dw_in_u", du, h, "tn", F32, 512, 1024, 1024)
    d_wq = _matmul("dw_in_q", dq, h, "tn", F32, 512, 1024, 1024)
    d_wk = _matmul("dw_in_k", dk, h, "tn", F32, 512, 1024, 1024)
    d_wv = _matmul("dw_in_v", dv, h, "tn", F32, 512, 1024, 1024)
    links.advance([d_wu_in, d_wq, d_wk, d_wv])
    d_wf = _matmul("dw_in_f", dfl, h, "tn", F32, LANES, 1024, 512)
    d_wg2 = _matmul("dw_in_gates", dg2, h, "tn", F32, 1024, 1024, 1024, dep=links.token)
    d_win = jnp.concatenate([d_wu_in, d_wq, d_wk, d_wv, d_wf[:N_HEADS], d_wg2], axis=0)
    comm_b = [d_win.reshape(N_CHIPS, d_win.shape[0] // N_CHIPS, d)]
    links.advance(comm_b)
    links.reduce_begin("b", comm_b, column_halves=True)
    dx, d_g1 = _in_bwd(du, dq, dk, dv, dg2, dfl, dx1, x, links.tie(norm1_g), wu, wqkv, wg2, wft, 256)
    links.advance(dx)
    small = (d_g1, d_bf[:, :N_HEADS], d_mix, d_scale, d_g2n, d_gf)
    return loss, dx, small


def kernel(x, norm1_g, w_in, b_forget, pool_mix, pool_scale, w_pool_out, w_attn_out, w_out, norm2_g, w_ffn_gate, w_ffn_up, w_ffn_down, norm_f_g, loss_target, m_norm1_g, m_w_in, m_b_forget, m_pool_mix, m_pool_scale, m_w_pool_out, m_w_attn_out, m_w_out, m_norm2_g, m_w_ffn_gate, m_w_ffn_up, m_w_ffn_down, m_norm_f_g, v_norm1_g, v_w_in, v_b_forget, v_pool_mix, v_pool_scale, v_w_pool_out, v_w_attn_out, v_w_out, v_norm2_g, v_w_ffn_gate, v_w_ffn_up, v_w_ffn_down, v_norm_f_g):
    nb, seq, d = x.shape
    group_a = ((w_ffn_gate, m_w_ffn_gate, v_w_ffn_gate, True, 9), (w_ffn_up, m_w_ffn_up, v_w_ffn_up, True, 10),
               (w_ffn_down, m_w_ffn_down, v_w_ffn_down, False, 11))
    group_m = ((w_pool_out, m_w_pool_out, v_w_pool_out, False, 5), (w_attn_out, m_w_attn_out, v_w_attn_out, False, 6),
               (w_out, m_w_out, v_w_out, False, 7))
    group_b = ((w_in, m_w_in, v_w_in, False, 1),)
    small_w = (norm1_g, b_forget, pool_mix, pool_scale, norm2_g, norm_f_g)
    small_m = (m_norm1_g, m_b_forget, m_pool_mix, m_pool_scale, m_norm2_g, m_norm_f_g)
    small_v = (v_norm1_g, v_b_forget, v_pool_mix, v_pool_scale, v_norm2_g, v_norm_f_g)
    small_pos = (0, 2, 3, 4, 8, 12)
    view = lambda a, tr: a[0].T if tr else a[0]
    unview = lambda a, tr, like: (a.T if tr else a).reshape(like.shape)

    def shard(w, tr):
        lw = view(w, tr).astype(BF16)
        return lw.reshape(2, lw.shape[0] // 2, lw.shape[1])

    cm = lambda a: jnp.transpose(a, (2, 0, 1))
    shard_in = _rows_to_bf16("w_in_to_bf16", cm(w_in))
    links = _MeshLinks([shard_in],
                       [shard(w_pool_out, False), shard(w_attn_out, False), shard(w_out, False),
                        shard(w_ffn_gate, True), shard(w_ffn_up, True), shard(w_ffn_down, False)])
    loss, dx, small_g = _local_step(
        links, x.reshape(nb * seq, d), loss_target.reshape(nb * seq, d), seq,
        norm1_g, b_forget, pool_mix[0], pool_scale, norm2_g, norm_f_g.reshape(1, d))

    grads, deltas, new_m, new_v = [None] * 13, [None] * 13, [None] * 13, [None] * 13
    pos_c = jnp.stack([lax.axis_index("c")]).astype(jnp.int32)

    def update(tag, group, dep):
        last = []
        for k, ((w, m, v, tr, pos), (mine, other)) in enumerate(zip(group, links.reduced(tag))):
            outs = _adamw_halves(f"adamw_{tag}{k}", pos_c, view(w, tr), mine, other, view(m, tr), view(v, tr), 256,
                                 dep=dep)
            grads[pos], deltas[pos], new_m[pos], new_v[pos] = (unview(a, tr, w) for a in outs)
            last.append(outs[1])
        return last

    last = update("a", group_a, links.token) + update("m", group_m, links.token)
    links.advance(last)
    small_sum = _all_reduce_small(links.tie(_pack_small(*small_g, extra=loss)))
    loss_out = small_sum[8, N_HEADS]
    dl, mn, vn = _adamw("adamw_small", _pack_small(*small_w), small_sum * _small_mask(), _pack_small(*small_m),
                        _pack_small(*small_v))
    for pos, g, a, b, e in zip(small_pos, _unpack_small(small_sum, small_w), _unpack_small(dl, small_w),
                               _unpack_small(mn, small_w), _unpack_small(vn, small_w)):
        grads[pos], deltas[pos], new_m[pos], new_v[pos] = g, a, b, e
    links.advance(dl)
    links.advance(links.token)
    (mine, other), = links.reduced("b")
    outs = _adamw_rows("adamw_b0", pos_c, cm(w_in), mine, other, cm(m_w_in), cm(v_w_in))
    grads[1], deltas[1], new_m[1], new_v[1] = (jnp.transpose(a, (1, 2, 0)) for a in outs)

    return (loss_out, dx.reshape(nb, seq, d), *grads, *deltas, *new_m, *new_v)


def _small_mask():
    rows = lax.broadcasted_iota(jnp.int32, (552, LANES), 0)
    lanes = lax.broadcasted_iota(jnp.int32, (552, LANES), 1)
    return jnp.where(jnp.logical_and(rows == 8, lanes == N_HEADS), 0.0, 1.0).astype(F32)
```

```python
import functools

import jax
import jax.numpy as jnp
from jax import lax
from jax.experimental import pallas as pl
from jax.experimental.pallas import tpu as pltpu

F32 = jnp.float32
BF16 = jnp.bfloat16

D_MODEL = 1024
POOL_WINDOWS = (2, 4, 8, 16)
POOL_GROUPS = 4
POOL_GROUP_DIM = 128
POOL_WIDTH = 512
HEAD_DIM = 64
N_HEADS = 8
ATTN_WIDTH = 512
D_FF = 2816
RMS_EPS = 1e-6
ATTN_SCALE = HEAD_DIM ** -0.5
NEG_BIG = -1e30

ADAM_LR = 0.001
ADAM_B1 = 0.9
ADAM_B2 = 0.999
ADAM_EPS = 1e-08
ADAM_WD = 0.01
ADAM_STEP = 10

LANES = 128
N_CHIPS = 4
N_DEV = 8
VMEM_LIMIT_V7X = 52 * 1024 * 1024
MESH = pl.DeviceIdType.MESH
ANY = pl.BlockSpec(memory_space=pl.ANY)


def _cparams(*sem):
    return pltpu.CompilerParams(dimension_semantics=sem if sem else None, vmem_limit_bytes=VMEM_LIMIT_V7X)


def _dep_list(dep):
    return [] if dep is None else (list(dep) if isinstance(dep, (list, tuple)) else [dep])


def _after(body, n_in, dep):
    k = len(_dep_list(dep))
    if k == 0:
        return body

    def wrapped(*refs):
        body(*refs[:n_in], *refs[n_in + k:])

    return wrapped


def _dep_args(dep):
    deps = _dep_list(dep)
    return [ANY] * len(deps), deps


def _dot(a, b):
    return lax.dot_general(a, b, (((1,), (0,)), ((), ())), preferred_element_type=F32)


def _dot_nt(a, b):
    return lax.dot_general(a, b, (((1,), (1,)), ((), ())), preferred_element_type=F32)


def _dot_tn(a, b):
    return lax.dot_general(a, b, (((0,), (0,)), ((), ())), preferred_element_type=F32)


def _sigmoid(x):
    return jax.nn.sigmoid(x)


def _rms_fwd(x, g):
    r = lax.rsqrt(jnp.mean(x * x, axis=-1, keepdims=True) + RMS_EPS)
    return (x * r) * g


def _rms_bwd(x, g, dy):
    r = lax.rsqrt(jnp.mean(x * x, axis=-1, keepdims=True) + RMS_EPS)
    xh = x * r
    dg = jnp.sum(dy * xh, axis=0, keepdims=True)
    dxh = dy * g
    dx = r * (dxh - xh * jnp.mean(dxh * xh, axis=-1, keepdims=True))
    return dx, dg


def _matmul(name, a, b, mode, out_dtype, tm, tn, tk, dep=None):
    if mode == "nn":
        (m, k), (_, n) = a.shape, b.shape
    elif mode == "nt":
        (m, k), (n, _) = a.shape, b.shape
    else:
        (k, m), (_, n) = a.shape, b.shape
    tm, tn, tk = min(tm, m), min(tn, n), min(tk, k)
    assert m % tm == 0 and n % tn == 0 and k % tk == 0, (name, m, n, k, tm, tn, tk)
    nk = k // tk
    if mode == "tn":
        a_spec = pl.BlockSpec((tk, tm), lambda i, j, kk: (kk, i))
    else:
        a_spec = pl.BlockSpec((tm, tk), lambda i, j, kk: (i, kk))
    if mode == "nt":
        b_spec = pl.BlockSpec((tn, tk), lambda i, j, kk: (j, kk))
    else:
        b_spec = pl.BlockSpec((tk, tn), lambda i, j, kk: (kk, j))
    dot = {"nn": _dot, "nt": _dot_nt, "tn": _dot_tn}[mode]
    use_scratch = nk > 1 and out_dtype != F32

    def body(a_ref, b_ref, o_ref, *scratch):
        prod = dot(a_ref[...].astype(BF16), b_ref[...].astype(BF16))
        if nk == 1:
            o_ref[...] = prod.astype(out_dtype)
            return
        acc = scratch[0] if use_scratch else o_ref
        kk = pl.program_id(2)

        @pl.when(kk == 0)
        def _():
            acc[...] = prod

        @pl.when(kk > 0)
        def _():
            acc[...] += prod

        if use_scratch:
            @pl.when(kk == nk - 1)
            def _():
                o_ref[...] = acc[...].astype(out_dtype)

    dep_specs, dep_ops = _dep_args(dep)
    return pl.pallas_call(
        _after(body, 2, dep),
        name=name,
        out_shape=jax.ShapeDtypeStruct((m, n), out_dtype),
        grid=(m // tm, n // tn, nk),
        in_specs=[a_spec, b_spec] + dep_specs,
        out_specs=pl.BlockSpec((tm, tn), lambda i, j, kk: (i, j)),
        scratch_shapes=[pltpu.VMEM((tm, tn), F32)] if use_scratch else [],
        compiler_params=_cparams("parallel", "parallel", "arbitrary"),
    )(a, b, *dep_ops)


def _norm_fwd(name, x, g, tm):
    t, d = x.shape
    tm = min(tm, t)

    def body(x_ref, g_ref, h_ref):
        h_ref[...] = _rms_fwd(x_ref[...], g_ref[...]).astype(BF16)

    return pl.pallas_call(
        body, name=name, out_shape=jax.ShapeDtypeStruct((t, d), BF16), grid=(t // tm,),
        in_specs=[pl.BlockSpec((tm, d), lambda i: (i, 0)), pl.BlockSpec((1, d), lambda i: (0, 0))],
        out_specs=pl.BlockSpec((tm, d), lambda i: (i, 0)),
        compiler_params=_cparams("parallel"),
    )(x, g)


def _split3(x):
    hi = x.astype(BF16)
    r1 = x - hi.astype(F32)
    mid = r1.astype(BF16)
    lo = (r1 - mid.astype(F32)).astype(BF16)
    return hi, mid, lo


def _tri_dot(tri, x):
    hi, mid, lo = _split3(x)
    return _dot(tri, hi) + _dot(tri, mid) + _dot(tri, lo)


def _forget_fwd(h, wf, bf, seq):
    t, d = h.shape
    cb = min(256, seq)

    def body(h_ref, wf_ref, bf_ref, fl_ref, fc_ref):
        fl = _dot(h_ref[...], wf_ref[...])
        fl_ref[...] = fl
        xx = fl + bf_ref[...]
        lf = jnp.minimum(xx, 0.0) - jnp.log(1.0 + jnp.exp(-jnp.abs(xx)))
        ri = lax.broadcasted_iota(jnp.int32, (cb, cb), 0)
        ci = lax.broadcasted_iota(jnp.int32, (cb, cb), 1)
        tri = (ri >= ci).astype(BF16)
        carry = jnp.zeros((1, LANES), F32)
        for blk in range(seq // cb):
            cs = _tri_dot(tri, lf[blk * cb:(blk + 1) * cb]) + carry
            fc_ref[blk * cb:(blk + 1) * cb, :] = cs
            carry = cs[cb - 1:cb, :]

    return pl.pallas_call(
        body, name="forget_fwd",
        out_shape=(jax.ShapeDtypeStruct((t, LANES), F32), jax.ShapeDtypeStruct((t, LANES), F32)),
        grid=(t // seq,),
        in_specs=[pl.BlockSpec((seq, d), lambda b: (b, 0)), pl.BlockSpec((d, LANES), lambda b: (0, 0)),
                  pl.BlockSpec((1, LANES), lambda b: (0, 0))],
        out_specs=(pl.BlockSpec((seq, LANES), lambda b: (b, 0)), pl.BlockSpec((seq, LANES), lambda b: (b, 0))),
        compiler_params=_cparams("parallel"),
    )(h, wf, bf)


def _pool_fwd(u, mix, scale, seq):
    t = u.shape[0]

    def body(u_ref, mix_ref, sc_ref, p_ref, ps_ref):
        tpos = lax.broadcasted_iota(jnp.int32, (seq, POOL_GROUP_DIM), 0)
        for g in range(POOL_GROUPS):
            sl = slice(g * POOL_GROUP_DIM, (g + 1) * POOL_GROUP_DIM)
            ug = u_ref[:, sl]
            s = ug
            for lvl in range(g + 1):
                d = 2 ** lvl
                s = s + jnp.where(tpos >= d, pltpu.roll(s, d, 0), 0.0)
            cnt = jnp.minimum(tpos + 1, POOL_WINDOWS[g]).astype(F32)
            pb = (s / cnt - ug).astype(BF16)
            p_ref[:, sl] = pb
            ps_ref[:, sl] = (_dot(pb, mix_ref[g]) * sc_ref[:, sl]).astype(BF16)

    return pl.pallas_call(
        body, name="pool_fwd",
        out_shape=(jax.ShapeDtypeStruct((t, POOL_WIDTH), BF16), jax.ShapeDtypeStruct((t, POOL_WIDTH), BF16)),
        grid=(t // seq,),
        in_specs=[pl.BlockSpec((seq, POOL_WIDTH), lambda b: (b, 0)),
                  pl.BlockSpec((POOL_GROUPS, POOL_GROUP_DIM, POOL_GROUP_DIM), lambda b: (0, 0, 0)),
                  pl.BlockSpec((1, POOL_WIDTH), lambda b: (0, 0))],
        out_specs=(pl.BlockSpec((seq, POOL_WIDTH), lambda b: (b, 0)), pl.BlockSpec((seq, POOL_WIDTH), lambda b: (b, 0))),
        compiler_params=_cparams("parallel"),
    )(u, mix, scale)


def _aug_constants():
    w = N_HEADS * LANES
    rows = jnp.arange(3 * LANES)
    piece, head = rows // LANES, rows % LANES
    cols = jnp.arange(w)
    live = (head < N_HEADS)[:, None]
    pq = (live & (cols[None, :] == (head * LANES + HEAD_DIM + piece)[:, None])).astype(BF16)
    pk = -(live & (cols[None, :] == (head * LANES + HEAD_DIM + 3 + piece)[:, None])).astype(BF16)
    lane = cols % LANES
    oq = ((lane >= HEAD_DIM + 3) & (lane < HEAD_DIM + 6)).astype(F32)[None, :]
    ok = ((lane >= HEAD_DIM) & (lane < HEAD_DIM + 3)).astype(F32)[None, :]
    return pq, pk, oq, ok


def _head_blocks(wt):
    d = wt.shape[1]
    return jnp.pad(wt.reshape(N_HEADS, HEAD_DIM, d), ((0, 0), (0, LANES - HEAD_DIM), (0, 0))).reshape(N_HEADS * LANES, d)


def _attn_prep(h, wq, wk, wv, fcum, tm):
    t, d = h.shape
    tm = min(tm, t)
    w = N_HEADS * LANES
    pq, pk, oq, ok = _aug_constants()

    def body(h_ref, wq_ref, wk_ref, wv_ref, f_ref, pq_ref, pk_ref, oq_ref, ok_ref, qa_ref, ka_ref, v_ref):
        hh = h_ref[...]
        fs = jnp.concatenate(_split3(f_ref[...]), axis=1)
        q = _dot_nt(hh, wq_ref[...]).astype(BF16).astype(F32) * ATTN_SCALE
        qa_ref[...] = (q + _dot(fs, pq_ref[...]) + oq_ref[...]).astype(BF16)
        k = _dot_nt(hh, wk_ref[...]).astype(BF16).astype(F32)
        ka_ref[...] = (k + _dot(fs, pk_ref[...]) + ok_ref[...]).astype(BF16)
        v_ref[...] = _dot_nt(hh, wv_ref[...]).astype(BF16)

    row = lambda n: pl.BlockSpec((tm, n), lambda i: (i, 0))
    full = lambda a: pl.BlockSpec(a.shape, lambda i: (0, 0))
    return pl.pallas_call(
        body, name="attn_prep",
        out_shape=(jax.ShapeDtypeStruct((t, w), BF16), jax.ShapeDtypeStruct((t, w), BF16),
                   jax.ShapeDtypeStruct((t, ATTN_WIDTH), BF16)),
        grid=(t // tm,),
        in_specs=[row(d), full(wq), full(wk), full(wv), row(LANES), full(pq), full(pk), full(oq), full(ok)],
        out_specs=(row(w), row(w), row(ATTN_WIDTH)),
        compiler_params=_cparams("parallel"),
    )(h, wq, wk, wv, fcum, pq, pk, oq, ok)


def _fold_lanes(x, op):
    out = x[:, :LANES]
    for g in range(1, x.shape[1] // LANES):
        out = op(out, x[:, g * LANES:(g + 1) * LANES])
    return out


def _attn_fwd(qa, ka, v, seq, tq, dep=None):
    t = qa.shape[0]
    nq = seq // tq
    hp_n = N_HEADS // 2
    heads = [slice(e * LANES, (e + 1) * LANES) for e in range(2)]

    def body(q_ref, k_ref, v_ref, o_ref, lse_ref, s_buf):
        i = pl.program_id(2)
        diag_ok = lax.broadcasted_iota(jnp.int32, (tq, tq), 0) >= lax.broadcasted_iota(jnp.int32, (tq, tq), 1)
        qs = [q_ref[:, hl] for hl in heads]

        def sweep1(j, mxs):
            r0 = pl.multiple_of(j * tq, tq)
            out = []
            for e, hl in enumerate(heads):
                s = _dot_nt(qs[e], k_ref[pl.ds(r0, tq), hl])
                s = jnp.where(jnp.logical_or(diag_ok, j < i), s, NEG_BIG)
                s_buf[e, j] = s
                out.append(jnp.maximum(mxs[e], _fold_lanes(s, jnp.maximum)))
            return tuple(out)

        mxs = lax.fori_loop(0, i + 1, sweep1, (jnp.full((tq, LANES), NEG_BIG, F32),) * 2)
        ms = [jnp.max(mx, axis=1, keepdims=True) for mx in mxs]

        def sweep2(j, carry):
            r0 = pl.multiple_of(j * tq, tq)
            vv = v_ref[pl.ds(r0, tq), :]
            out = []
            for e in range(2):
                p = jnp.exp(s_buf[e, j] - ms[e])
                out += [carry[2 * e] + _fold_lanes(p, jnp.add), carry[2 * e + 1] + _dot(p.astype(BF16), vv)]
            return tuple(out)

        res = lax.fori_loop(0, i + 1, sweep2, (jnp.zeros((tq, LANES), F32),) * 4)
        outs = []
        for e in range(2):
            l = jnp.sum(res[2 * e], axis=1, keepdims=True)
            outs.append(res[2 * e + 1] / l)
            lse_ref[:, e:e + 1] = ms[e] + jnp.log(l)
        lane = lax.broadcasted_iota(jnp.int32, (tq, LANES), 1)
        o_ref[...] = jnp.where(lane < HEAD_DIM, outs[0], outs[1])

    dep_specs, dep_ops = _dep_args(dep)
    return pl.pallas_call(
        _after(body, 3, dep), name="attn_fwd",
        out_shape=(jax.ShapeDtypeStruct((t, ATTN_WIDTH), F32), jax.ShapeDtypeStruct((hp_n, t, 2), F32)),
        grid=(t // seq, hp_n, nq),
        in_specs=[pl.BlockSpec((tq, 2 * LANES), lambda b, hp, i: (b * nq + i, hp)),
                  pl.BlockSpec((seq, 2 * LANES), lambda b, hp, i: (b, hp)),
                  pl.BlockSpec((seq, LANES), lambda b, hp, i: (b, hp))] + dep_specs,
        out_specs=(pl.BlockSpec((tq, LANES), lambda b, hp, i: (b * nq + i, hp)),
                   pl.BlockSpec((None, tq, 2), lambda b, hp, i: (hp, b * nq + i, 0))),
        scratch_shapes=[pltpu.VMEM((2, nq, tq, tq), F32)],
        compiler_params=_cparams("parallel", "parallel", "arbitrary"),
    )(qa, ka, v, *dep_ops)


def _merge_fwd(x, ps, o, g2, wpo, wao, wout, tm):
    t, d = x.shape
    tm = min(tm, t)

    def body(x_ref, ps_ref, o_ref, gp_ref, ga_ref, wpo_ref, wao_ref, wout_ref, mg_ref, x1_ref):
        py = _dot(ps_ref[...], wpo_ref[...])
        ay = _dot(o_ref[...].astype(BF16), wao_ref[...])
        mb = (_sigmoid(gp_ref[...].astype(F32)) * py + _sigmoid(ga_ref[...].astype(F32)) * ay).astype(BF16)
        mg_ref[...] = mb
        x1_ref[...] = x_ref[...] + _dot(mb, wout_ref[...])

    row = lambda w: pl.BlockSpec((tm, w), lambda i: (i, 0))
    full = lambda a: pl.BlockSpec(a.shape, lambda i: (0, 0))
    return pl.pallas_call(
        body, name="merge_fwd",
        out_shape=(jax.ShapeDtypeStruct((t, d), BF16), jax.ShapeDtypeStruct((t, d), F32)),
        grid=(t // tm,),
        in_specs=[row(d), row(POOL_WIDTH), row(ATTN_WIDTH), pl.BlockSpec((tm, d), lambda i: (i, 0)),
                  pl.BlockSpec((tm, d), lambda i: (i, 1)), full(wpo), full(wao), full(wout)],
        out_specs=(row(d), row(d)),
        compiler_params=_cparams("parallel"),
    )(x, ps, o, g2, g2, wpo, wao, wout)


def _ffn_fwd(x1, g, wg, wu, wd, tm, tf):
    t, d = x1.shape
    f = wg.shape[0]
    tm = min(tm, t)
    nf = f // tf

    def body(x1_ref, g_ref, wg_ref, wu_ref, wd_ref, h2_ref, gt_ref, up_ref, act_ref, x2_ref):
        j = pl.program_id(1)

        @pl.when(j == 0)
        def _():
            h2_ref[...] = _rms_fwd(x1_ref[...], g_ref[...]).astype(BF16)

        h2 = h2_ref[...]
        gt = _dot_nt(h2, wg_ref[...])
        up = _dot_nt(h2, wu_ref[...])
        sg = _sigmoid(gt)
        silu = gt * sg
        act = (silu * up).astype(BF16)
        gt_ref[...] = (up * (sg * (1.0 + gt * (1.0 - sg)))).astype(BF16)
        up_ref[...] = silu.astype(BF16)
        act_ref[...] = act
        prod = _dot(act, wd_ref[...])

        @pl.when(j == 0)
        def _():
            x2_ref[...] = prod

        @pl.when(j > 0)
        def _():
            x2_ref[...] += prod

        @pl.when(j == nf - 1)
        def _():
            x2_ref[...] += x1_ref[...]

    return pl.pallas_call(
        body, name="ffn_fwd",
        out_shape=(jax.ShapeDtypeStruct((t, d), BF16), jax.ShapeDtypeStruct((t, f), BF16),
                   jax.ShapeDtypeStruct((t, f), BF16), jax.ShapeDtypeStruct((t, f), BF16),
                   jax.ShapeDtypeStruct((t, d), F32)),
        grid=(t // tm, nf),
        in_specs=[pl.BlockSpec((tm, d), lambda i, j: (i, 0)), pl.BlockSpec((1, d), lambda i, j: (0, 0)),
                  pl.BlockSpec((tf, d), lambda i, j: (j, 0)), pl.BlockSpec((tf, d), lambda i, j: (j, 0)),
                  pl.BlockSpec((tf, d), lambda i, j: (j, 0))],
        out_specs=(pl.BlockSpec((tm, d), lambda i, j: (i, 0)), pl.BlockSpec((tm, tf), lambda i, j: (i, j)),
                   pl.BlockSpec((tm, tf), lambda i, j: (i, j)), pl.BlockSpec((tm, tf), lambda i, j: (i, j)),
                   pl.BlockSpec((tm, d), lambda i, j: (i, 0))),
        compiler_params=_cparams("parallel", "arbitrary"),
    )(x1, g, wg, wu, wd)


def _final_fwd_bwd(x2, target, g, tm):
    t, d = x2.shape
    tm = min(tm, t)

    def body(x_ref, t_ref, g_ref, loss_ref, dx_ref, dg_ref):
        i = pl.program_id(0)
        x = x_ref[...]
        gg = g_ref[...]
        err = _rms_fwd(x, gg) - t_ref[...]
        part = 0.5 * jnp.sum(jnp.mean(err * err, axis=-1, keepdims=True), axis=0, keepdims=True)
        dx, dg = _rms_bwd(x, gg, err * (1.0 / d))
        dx_ref[...] = dx

        @pl.when(i == 0)
        def _():
            loss_ref[...] = jnp.zeros_like(loss_ref)
            dg_ref[...] = jnp.zeros_like(dg_ref)

        loss_ref[...] += jnp.broadcast_to(part, loss_ref.shape)
        dg_ref[...] += dg

    return pl.pallas_call(
        body, name="final_fwd_bwd",
        out_shape=(jax.ShapeDtypeStruct((1, LANES), F32), jax.ShapeDtypeStruct((t, d), F32),
                   jax.ShapeDtypeStruct((1, d), F32)),
        grid=(t // tm,),
        in_specs=[pl.BlockSpec((tm, d), lambda i: (i, 0)), pl.BlockSpec((tm, d), lambda i: (i, 0)),
                  pl.BlockSpec((1, d), lambda i: (0, 0))],
        out_specs=(pl.BlockSpec((1, LANES), lambda i: (0, 0)), pl.BlockSpec((tm, d), lambda i: (i, 0)),
                   pl.BlockSpec((1, d), lambda i: (0, 0))),
        compiler_params=_cparams("arbitrary"),
    )(x2, target, g)


def _ffn_bwd(dx2, x1, g, gt, up, wg, wu, wd, tm, tf):
    t, d = dx2.shape
    f = gt.shape[1]
    tm = min(tm, t)
    nf = f // tf

    def body(dx2_ref, x1_ref, g_ref, gt_ref, up_ref, wg_ref, wu_ref, wd_ref, dgt_ref, dup_ref, dx1_ref, dg_ref, acc_ref,
             dxb_ref):
        i, j = pl.program_id(0), pl.program_id(1)

        @pl.when(j == 0)
        def _():
            dxb_ref[...] = dx2_ref[...].astype(BF16)

        dact = _dot_nt(dxb_ref[...], wd_ref[...])
        dgt = (dact * gt_ref[...].astype(F32)).astype(BF16)
        dup = (dact * up_ref[...].astype(F32)).astype(BF16)
        dgt_ref[...] = dgt
        dup_ref[...] = dup
        contrib = _dot(dgt, wg_ref[...]) + _dot(dup, wu_ref[...])

        @pl.when(j == 0)
        def _():
            acc_ref[...] = contrib

        @pl.when(j > 0)
        def _():
            acc_ref[...] += contrib

        @pl.when(jnp.logical_and(i == 0, j == 0))
        def _():
            dg_ref[...] = jnp.zeros_like(dg_ref)

        @pl.when(j == nf - 1)
        def _():
            dxn, dg = _rms_bwd(x1_ref[...], g_ref[...], acc_ref[...])
            dx1_ref[...] = dx2_ref[...] + dxn
            dg_ref[...] += dg

    return pl.pallas_call(
        body, name="ffn_bwd",
        out_shape=(jax.ShapeDtypeStruct((t, f), BF16), jax.ShapeDtypeStruct((t, f), BF16),
                   jax.ShapeDtypeStruct((t, d), F32), jax.ShapeDtypeStruct((1, d), F32)),
        grid=(t // tm, nf),
        in_specs=[pl.BlockSpec((tm, d), lambda i, j: (i, 0)), pl.BlockSpec((tm, d), lambda i, j: (i, 0)),
                  pl.BlockSpec((1, d), lambda i, j: (0, 0)),
                  pl.BlockSpec((tm, tf), lambda i, j: (i, j)), pl.BlockSpec((tm, tf), lambda i, j: (i, j)),
                  pl.BlockSpec((tf, d), lambda i, j: (j, 0)), pl.BlockSpec((tf, d), lambda i, j: (j, 0)),
                  pl.BlockSpec((tf, d), lambda i, j: (j, 0))],
        out_specs=(pl.BlockSpec((tm, tf), lambda i, j: (i, j)), pl.BlockSpec((tm, tf), lambda i, j: (i, j)),
                   pl.BlockSpec((tm, d), lambda i, j: (i, 0)), pl.BlockSpec((1, d), lambda i, j: (0, 0))),
        scratch_shapes=[pltpu.VMEM((tm, d), F32), pltpu.VMEM((tm, d), BF16)],
        compiler_params=_cparams("arbitrary", "arbitrary"),
    )(dx2, x1, g, gt, up, wg, wu, wd)


def _merge_bwd(dx1, ps, o, g2, wpo, wao, wout, tm, dep=None):
    t, d = dx1.shape
    tm = min(tm, t)

    def body(dx1_ref, ps_ref, o_ref, gp_ref, ga_ref, wpo_ref, wao_ref, wout_ref, dpy_ref, day_ref, dg2_ref, dps_ref, da_ref):
        dm = _dot_nt(dx1_ref[...].astype(BF16), wout_ref[...])
        py = _dot(ps_ref[...], wpo_ref[...])
        ay = _dot(o_ref[...].astype(BF16), wao_ref[...])
        sp = _sigmoid(gp_ref[...].astype(F32))
        sa = _sigmoid(ga_ref[...].astype(F32))
        dpy = (dm * sp).astype(BF16)
        day = (dm * sa).astype(BF16)
        dpy_ref[...] = dpy
        day_ref[...] = day
        dg2_ref[:, :d] = (dm * py * (sp * (1.0 - sp))).astype(BF16)
        dg2_ref[:, d:] = (dm * ay * (sa * (1.0 - sa))).astype(BF16)
        dps_ref[...] = _dot_nt(dpy, wpo_ref[...])
        da_ref[...] = _dot_nt(day, wao_ref[...]).astype(BF16)

    row = lambda w: pl.BlockSpec((tm, w), lambda i: (i, 0))
    full = lambda a: pl.BlockSpec(a.shape, lambda i: (0, 0))
    dep_specs, dep_ops = _dep_args(dep)
    return pl.pallas_call(
        _after(body, 8, dep), name="merge_bwd",
        out_shape=(jax.ShapeDtypeStruct((t, d), BF16), jax.ShapeDtypeStruct((t, d), BF16),
                   jax.ShapeDtypeStruct((t, 2 * d), BF16), jax.ShapeDtypeStruct((t, POOL_WIDTH), F32),
                   jax.ShapeDtypeStruct((t, ATTN_WIDTH), BF16)),
        grid=(t // tm,),
        in_specs=[row(d), row(POOL_WIDTH), row(ATTN_WIDTH), pl.BlockSpec((tm, d), lambda i: (i, 0)),
                  pl.BlockSpec((tm, d), lambda i: (i, 1)), full(wpo), full(wao), full(wout)] + dep_specs,
        out_specs=(row(d), row(d), row(2 * d), row(POOL_WIDTH), row(ATTN_WIDTH)),
        compiler_params=_cparams("parallel"),
    )(dx1, ps, o, g2, g2, wpo, wao, wout, *dep_ops)


def _attn_bwd(qa, ka, v, do, lse4, seq, tq, dep=None):
    t = qa.shape[0]
    nq = seq // tq
    hp_n = N_HEADS // 2
    heads = [slice(e * LANES, (e + 1) * LANES) for e in range(2)]

    def body(q_ref, k_ref, v_ref, do_ref, lse_ref, dq_ref, dk_ref, dv_ref, dfr_ref, dk_acc, dv_acc, p_buf, dp_buf):
        diag_ok = lax.broadcasted_iota(jnp.int32, (tq, tq), 0) >= lax.broadcasted_iota(jnp.int32, (tq, tq), 1)
        lane_q = lax.broadcasted_iota(jnp.int32, (tq, LANES), 1)
        lane_s = lax.broadcasted_iota(jnp.int32, (seq, LANES), 1)
        mine_q = [lane_q < HEAD_DIM, lane_q >= HEAD_DIM]
        dv_acc[...] = jnp.zeros_like(dv_acc)
        dk_acc[...] = jnp.zeros_like(dk_acc)
        dfr_ref[...] = jnp.zeros_like(dfr_ref)

        def q_step(i, _):
            q0 = pl.multiple_of(i * tq, tq)
            qs = [q_ref[pl.ds(q0, tq), hl] for hl in heads]
            dov = do_ref[pl.ds(q0, tq), :]
            dos = [jnp.where(mq, dov, jnp.zeros((), BF16)) for mq in mine_q]
            lss = [lse_ref[pl.ds(q0, tq), e:e + 1] for e in range(2)]

            def sweep1(j, dls):
                r0 = pl.multiple_of(j * tq, tq)
                vv = v_ref[pl.ds(r0, tq), :]
                out = []
                for e, hl in enumerate(heads):
                    s = _dot_nt(qs[e], k_ref[pl.ds(r0, tq), hl])
                    s = jnp.where(jnp.logical_or(diag_ok, j < i), s, NEG_BIG)
                    p = jnp.exp(s - lss[e])
                    dp = _dot_nt(dos[e], vv)
                    p_buf[e, j] = p
                    dp_buf[e, j] = dp
                    dv_acc[pl.ds(r0, tq), :] += _dot_tn(p.astype(BF16), dos[e])
                    out.append(dls[e] + _fold_lanes(p * dp, jnp.add))
                return tuple(out)

            dls = lax.fori_loop(0, i + 1, sweep1, (jnp.zeros((tq, LANES), F32),) * 2)
            dls = [jnp.sum(d, axis=1, keepdims=True) for d in dls]

            def sweep2(j, dqs):
                r0 = pl.multiple_of(j * tq, tq)
                out = []
                for e, hl in enumerate(heads):
                    ds = p_buf[e, j] * (dp_buf[e, j] - dls[e])
                    dfr_ref[e, pl.ds(j, 1), :] += jnp.sum(ds, axis=0, keepdims=True)
                    dsb = ds.astype(BF16)
                    dk_acc[e, pl.ds(r0, tq), :] += _dot_tn(dsb, qs[e])
                    out.append(dqs[e] + _dot(dsb, k_ref[pl.ds(r0, tq), hl]))
                return tuple(out)

            dqs = lax.fori_loop(0, i + 1, sweep2, (jnp.zeros((tq, LANES), F32),) * 2)
            dq = jnp.where(mine_q[0], dqs[0], pltpu.roll(dqs[1], HEAD_DIM, 1)) * ATTN_SCALE
            dq_ref[pl.ds(q0, tq), :] = dq.astype(BF16)
            return 0

        lax.fori_loop(0, nq, q_step, 0)
        dk_ref[...] = jnp.where(lane_s < HEAD_DIM, dk_acc[0], pltpu.roll(dk_acc[1], HEAD_DIM, 1)).astype(BF16)
        dv_ref[...] = dv_acc[...].astype(BF16)

    wide = pl.BlockSpec((seq, 2 * LANES), lambda b, hp: (b, hp))
    col = pl.BlockSpec((seq, LANES), lambda b, hp: (b, hp))
    pair = pl.BlockSpec((None, seq, 2), lambda b, hp: (hp, b, 0))
    dep_specs, dep_ops = _dep_args(dep)
    return pl.pallas_call(
        _after(body, 5, dep), name="attn_bwd",
        out_shape=(jax.ShapeDtypeStruct((t, ATTN_WIDTH), BF16),) * 3 + (jax.ShapeDtypeStruct((N_HEADS, t // tq, tq), F32),),
        grid=(t // seq, hp_n),
        in_specs=[wide, wide, col, col, pair] + dep_specs,
        out_specs=(col, col, col, pl.BlockSpec((2, nq, tq), lambda b, hp: (hp, b, 0))),
        scratch_shapes=[pltpu.VMEM((2, seq, LANES), F32), pltpu.VMEM((seq, LANES), F32),
                        pltpu.VMEM((2, nq, tq, tq), F32), pltpu.VMEM((2, nq, tq, tq), F32)],
        compiler_params=_cparams("parallel", "arbitrary"),
    )(qa, ka, v, do, lse4, *dep_ops)


def _forget_bwd(dfc, fl, bf, seq):
    t = fl.shape[0]
    cb = min(256, seq)
    nb = seq // cb

    def body(dfc_ref, fl_ref, bf_ref, dfl_ref, db_ref):
        b = pl.program_id(0)
        ri = lax.broadcasted_iota(jnp.int32, (cb, cb), 0)
        ci = lax.broadcasted_iota(jnp.int32, (cb, cb), 1)
        tri = (ci >= ri).astype(BF16)
        carry = jnp.zeros((1, LANES), F32)
        dbs = jnp.zeros((1, LANES), F32)
        for blk in reversed(range(nb)):
            rs = slice(blk * cb, (blk + 1) * cb)
            dlf = _tri_dot(tri, -dfc_ref[rs, :]) + carry
            carry = dlf[0:1, :]
            dfl = dlf * _sigmoid(-(fl_ref[rs, :] + bf_ref[...]))
            dfl_ref[rs, :] = dfl.astype(BF16)
            dbs = dbs + jnp.sum(dfl, axis=0, keepdims=True)

        @pl.when(b == 0)
        def _():
            db_ref[...] = jnp.zeros_like(db_ref)

        db_ref[...] += dbs

    return pl.pallas_call(
        body, name="forget_bwd",
        out_shape=(jax.ShapeDtypeStruct((t, LANES), BF16), jax.ShapeDtypeStruct((1, LANES), F32)),
        grid=(t // seq,),
        in_specs=[pl.BlockSpec((seq, LANES), lambda b: (b, 0)), pl.BlockSpec((seq, LANES), lambda b: (b, 0)),
                  pl.BlockSpec((1, LANES), lambda b: (0, 0))],
        out_specs=(pl.BlockSpec((seq, LANES), lambda b: (b, 0)), pl.BlockSpec((1, LANES), lambda b: (0, 0))),
        compiler_params=_cparams("arbitrary"),
    )(dfc, fl, bf)


def _pool_bwd(dps, p, mix, scale, seq):
    t = dps.shape[0]

    def body(dps_ref, p_ref, mix_ref, sc_ref, du_ref, dmix_ref, dsc_ref):
        b = pl.program_id(0)

        @pl.when(b == 0)
        def _():
            dmix_ref[...] = jnp.zeros_like(dmix_ref)
            dsc_ref[...] = jnp.zeros_like(dsc_ref)

        tpos = lax.broadcasted_iota(jnp.int32, (seq, POOL_GROUP_DIM), 0)
        for g in range(POOL_GROUPS):
            sl = slice(g * POOL_GROUP_DIM, (g + 1) * POOL_GROUP_DIM)
            pb = p_ref[:, sl]
            dpsg = dps_ref[:, sl]
            pm = _dot(pb, mix_ref[g])
            dsc_ref[:, sl] += jnp.sum(dpsg * pm, axis=0, keepdims=True)
            dpm = (dpsg * sc_ref[:, sl]).astype(BF16)
            dmix_ref[g] += _dot_tn(pb, dpm)
            dp = _dot_nt(dpm, mix_ref[g])
            cnt = jnp.minimum(tpos + 1, POOL_WINDOWS[g]).astype(F32)
            s = dp / cnt
            for lvl in range(g + 1):
                d = 2 ** lvl
                s = s + jnp.where(tpos < seq - d, pltpu.roll(s, seq - d, 0), 0.0)
            du_ref[:, sl] = (s - dp).astype(BF16)

    return pl.pallas_call(
        body, name="pool_bwd",
        out_shape=(jax.ShapeDtypeStruct((t, POOL_WIDTH), BF16),
                   jax.ShapeDtypeStruct((POOL_GROUPS, POOL_GROUP_DIM, POOL_GROUP_DIM), F32),
                   jax.ShapeDtypeStruct((1, POOL_WIDTH), F32)),
        grid=(t // seq,),
        in_specs=[pl.BlockSpec((seq, POOL_WIDTH), lambda b: (b, 0)), pl.BlockSpec((seq, POOL_WIDTH), lambda b: (b, 0)),
                  pl.BlockSpec((POOL_GROUPS, POOL_GROUP_DIM, POOL_GROUP_DIM), lambda b: (0, 0, 0)),
                  pl.BlockSpec((1, POOL_WIDTH), lambda b: (0, 0))],
        out_specs=(pl.BlockSpec((seq, POOL_WIDTH), lambda b: (b, 0)),
                   pl.BlockSpec((POOL_GROUPS, POOL_GROUP_DIM, POOL_GROUP_DIM), lambda b: (0, 0, 0)),
                   pl.BlockSpec((1, POOL_WIDTH), lambda b: (0, 0))),
        compiler_params=_cparams("arbitrary"),
    )(dps, p, mix, scale)


def _in_bwd(du, dq, dk, dv, dg2, dfl, dx1, x, g, wu, wqkv, wg2, wft, tm):
    t, d = x.shape
    tm = min(tm, t)
    aw = ATTN_WIDTH

    def body(du_ref, dq_ref, dk_ref, dv_ref, dg2_ref, dfl_ref, dx1_ref, x_ref, g_ref, wu_ref, wqkv_ref, wg2_ref, wft_ref,
             dx_ref, dg_ref):
        i = pl.program_id(0)
        dh = _dot(du_ref[...], wu_ref[...])
        dh += _dot(dq_ref[...], wqkv_ref[0:aw, :])
        dh += _dot(dk_ref[...], wqkv_ref[aw:2 * aw, :])
        dh += _dot(dv_ref[...], wqkv_ref[2 * aw:3 * aw, :])
        dh += _dot(dg2_ref[...], wg2_ref[...])
        dh += _dot(dfl_ref[...], wft_ref[...])
        dxn, dg = _rms_bwd(x_ref[...], g_ref[...], dh)
        dx_ref[...] = dx1_ref[...] + dxn

        @pl.when(i == 0)
        def _():
            dg_ref[...] = jnp.zeros_like(dg_ref)

        dg_ref[...] += dg

    row = lambda w: pl.BlockSpec((tm, w), lambda i: (i, 0))
    full = lambda a: pl.BlockSpec(a.shape, lambda i: (0, 0))
    return pl.pallas_call(
        body, name="in_bwd",
        out_shape=(jax.ShapeDtypeStruct((t, d), F32), jax.ShapeDtypeStruct((1, d), F32)),
        grid=(t // tm,),
        in_specs=[row(POOL_WIDTH), row(aw), row(aw), row(aw), row(2 * d), row(LANES), row(d), row(d),
                  pl.BlockSpec((1, d), lambda i: (0, 0)), full(wu), full(wqkv), full(wg2), full(wft)],
        out_specs=(row(d), pl.BlockSpec((1, d), lambda i: (0, 0))),
        compiler_params=_cparams("arbitrary"),
    )(du, dq, dk, dv, dg2, dfl, dx1, x, g, wu, wqkv, wg2, wft)


def _position():
    return lax.axis_index("x"), lax.axis_index("y"), lax.axis_index("c")


def _remote(src, dst, send_sem, recv_sem, device):
    return pltpu.make_async_remote_copy(src_ref=src, dst_ref=dst, send_sem=send_sem, recv_sem=recv_sem,
                                        device_id=device, device_id_type=MESH)


HBM = pl.BlockSpec(memory_space=pltpu.HBM)
SEM = pl.BlockSpec(memory_space=pltpu.SEMAPHORE)
DATAFLOW = pltpu.SideEffectType.DATAFLOW_SIDE_EFFECTING


def _copies_start(name, arrays, plan, m, dep=None):
    n = len(arrays)
    arrays = [pltpu.with_memory_space_constraint(a, pltpu.HBM) for a in arrays]

    def body(*refs):
        ins, send_sem, recv_sem, token = refs[:n], refs[n], refs[n + 1], refs[2 * n + 2]
        for i, (src, dst, device, _) in enumerate(plan(ins, *_position())):
            _remote(src, dst, send_sem.at[i], recv_sem.at[i], device).start()
        token[...] = jnp.zeros_like(token)

    dep_specs, dep_ops = _dep_args(dep)
    outs = pl.pallas_call(
        _after(body, n, dep), name=name,
        out_shape=(pltpu.SemaphoreType.DMA((m,)), pltpu.SemaphoreType.DMA((m,)),
                   *[pltpu.HBM(a.shape, a.dtype) for a in arrays], jax.ShapeDtypeStruct((8, LANES), F32)),
        in_specs=[HBM] * n + dep_specs, out_specs=(SEM, SEM, *[HBM] * n, pl.BlockSpec(memory_space=pltpu.VMEM)),
        input_output_aliases={i: i + 2 for i in range(n)},
        compiler_params=pltpu.CompilerParams(has_side_effects=DATAFLOW),
    )(*arrays, *dep_ops)
    return (outs[0], outs[1]), list(outs[2:2 + n]), outs[2 + n]


def _copies_wait(name, sems, arrays, plan, after):
    n = len(arrays)
    afters = list(after) if isinstance(after, (list, tuple)) else [after]

    def body(*refs):
        ins, send_sem, recv_sem = refs[:n], refs[n], refs[n + 1]
        for i, (src, dst, device, landing) in enumerate(plan(ins, *_position())):
            _remote(src, dst, send_sem.at[i], recv_sem.at[i], device).wait_send()
            _remote(landing, landing, send_sem.at[i], recv_sem.at[i], device).wait_recv()

    outs = pl.pallas_call(
        body, name=name,
        out_shape=tuple(pltpu.HBM(a.shape, a.dtype) for a in arrays),
        in_specs=[HBM] * n + [SEM, SEM] + [ANY] * len(afters), out_specs=tuple([HBM] * n),
        input_output_aliases={i: i for i in range(n)},
        compiler_params=pltpu.CompilerParams(has_side_effects=DATAFLOW),
    )(*arrays, sems[0], sems[1], *afters)
    return list(outs)


def _tie(x, dep):
    for token in _dep_list(dep):
        x = x + token[0, 0]
    return x


def _other_chips(x, y):
    return [(1 - x, y), (x, 1 - y), (1 - x, 1 - y)]


def _gather_begin(tag, shards, token, column_halves=False):
    n = len(shards)
    lands = [lax.empty((N_CHIPS,) + s.shape, s.dtype) for s in shards]
    if column_halves:
        cols = lambda ref, h: pl.ds(pl.multiple_of(h * (ref.shape[-1] // 2), LANES), ref.shape[-1] // 2)
        mine = lambda ref, h: ref.at[:, cols(ref, h)]
        landed = lambda ref, chip, h: ref.at[chip, :, cols(ref, h)]
    else:
        mine = lambda ref, h: ref.at[h]
        landed = lambda ref, chip, h: ref.at[chip, h]

    def plan(refs, x, y, c):
        return [(mine(refs[k], c), landed(refs[n + k], 2 * x + y, c), (ox, oy, c), landed(refs[n + k], 2 * ox + oy, c))
                for k in range(n) for ox, oy in _other_chips(x, y)]

    sems, thru, token = _copies_start(f"gather_{tag}_ici_start", list(shards) + lands, plan, 3 * n, dep=token)
    return dict(tag=tag, n=n, plan=plan, sems=sems, arrays=thru, token=token, landed=landed)


def _gather_forward(st, after):
    n, tag, landed = st["n"], st["tag"], st["landed"]
    thru = _copies_wait(f"gather_{tag}_ici_wait", st["sems"], st["arrays"], st["plan"], after)

    def plan(refs, x, y, c):
        return [(landed(refs[k], 2 * ox + oy, c), landed(refs[k], 2 * ox + oy, c), (x, y, 1 - c),
                 landed(refs[k], 2 * ox + oy, 1 - c))
                for k in range(n) for ox, oy in _other_chips(x, y)]

    sems, lands, token = _copies_start(f"gather_{tag}_fwd_start", thru[n:], plan, 3 * n)
    return dict(tag=tag, n=n, plan=plan, sems=sems, arrays=lands, token=token, shards=thru[:n])


def _gather_end(st, after, merge=True):
    lands = _copies_wait(f"gather_{st['tag']}_fwd_wait", st["sems"], st["arrays"], st["plan"], after)
    if not merge:
        return lands, st["shards"]
    me = 2 * lax.axis_index("x") + lax.axis_index("y")
    return [lax.dynamic_update_index_in_dim(g, s, me, 0) for g, s in zip(lands, st["shards"])]


def _add_keep_give(name, pos, a, a_keep, a_give, b, b_keep, b_give, steps):
    r, c = b.shape[-2:]

    def spec(arr, fn):
        lead = arr.ndim - 2

        def index(i, p):
            idx = tuple(fn(i, p))
            return idx if len(idx) == arr.ndim else idx + (0, 0)

        return pl.BlockSpec((None,) * lead + (r, c), index)

    out_spec = pl.BlockSpec((None, r, c), lambda i, p: (i, 0, 0))

    def body(p_ref, ak_ref, bk_ref, ag_ref, bg_ref, keep_ref, give_ref):
        keep_ref[...] = ak_ref[...] + bk_ref[...].astype(F32)
        give_ref[...] = (ag_ref[...] + bg_ref[...].astype(F32)).astype(BF16)

    return pl.pallas_call(
        body, name=name,
        out_shape=(jax.ShapeDtypeStruct((steps, r, c), F32), jax.ShapeDtypeStruct((steps, r, c), BF16)),
        grid_spec=pltpu.PrefetchScalarGridSpec(
            num_scalar_prefetch=1, grid=(steps,),
            in_specs=[spec(a, a_keep), spec(b, b_keep), spec(a, a_give), spec(b, b_give)],
            out_specs=(out_spec, out_spec)),
        compiler_params=_cparams("parallel"),
    )(pos, a, b, a, b)


def _add_last(name, a, b):
    _, r, c = a.shape
    blk = pl.BlockSpec((None, r, c), lambda i: (0, 0, 0))

    def body(a_ref, b_ref, o_ref):
        o_ref[...] = a_ref[...] + b_ref[...].astype(F32)

    return pl.pallas_call(
        body, name=name, out_shape=jax.ShapeDtypeStruct((r, c), F32), grid=(1,), in_specs=[blk, blk],
        out_specs=pl.BlockSpec((r, c), lambda i: (0, 0)), compiler_params=_cparams("arbitrary"),
    )(a, b)


def _exchange_begin(tag, stage, gives, lands, peer_fn, extra):
    n = len(gives)

    def plan(refs, x, y, c):
        return [(refs[k], refs[n + k], peer_fn(x, y, c), refs[n + k]) for k in range(n)]

    sems, thru, token = _copies_start(f"rs{tag}_{stage}_start", gives + lands, plan, n)
    return dict(extra, tag=tag, n=n, stage=stage, plan=plan, sems=sems, arrays=thru, token=token)


def _reduce_begin(tag, grads, column_halves=False):
    n = len(grads)
    if column_halves:
        half = lambda ref, j, h: ref.at[j, :, pl.ds(pl.multiple_of(h * (ref.shape[2] // 2), LANES), ref.shape[2] // 2)]
        lands = [lax.empty((N_CHIPS, g.shape[1], g.shape[2] // 2), F32) for g in grads]
    else:
        half = lambda ref, j, h: ref.at[j, h]
        lands = [lax.empty((N_CHIPS,) + g.shape[2:], F32) for g in grads]

    def plan(refs, x, y, c):
        return [(half(refs[k], j, 1 - c), refs[n + k].at[j], (x, y, 1 - c), refs[n + k].at[j])
                for k in range(n) for j in range(N_CHIPS)]

    sems, thru, token = _copies_start(f"rs{tag}_c_start", list(grads) + lands, plan, N_CHIPS * n)
    return dict(tag=tag, n=n, stage="c", plan=plan, sems=sems, arrays=thru, token=token, column_halves=column_halves)


def _reduce_advance(st, after):
    tag, n, stage = st["tag"], st["n"], st["stage"]
    thru = _copies_wait(f"rs{tag}_{stage}_wait", st["sems"], st["arrays"], st["plan"], after)
    first, recv = thru[:n], thru[n:]
    x, y, c = _position()
    if stage == "c":
        pos = jnp.stack([c, x]).astype(jnp.int32)
        if st["column_halves"]:
            mine = lambda chip: (lambda i, p: (chip(p) + i, 0, p[0]))
        else:
            mine = lambda chip: (lambda i, p: (chip(p) + i, p[0]))
        sums = [_add_keep_give(
            f"rs{tag}_c_add{k}", pos,
            first[k], mine(lambda p: 2 * p[1]), mine(lambda p: 2 * (1 - p[1])),
            recv[k], lambda i, p: (2 * p[1] + i,), lambda i, p: (2 * (1 - p[1]) + i,), 2) for k in range(n)]
        lands = [lax.empty(s[1].shape, BF16) for s in sums]
        return _exchange_begin(tag, "x", [s[1] for s in sums], lands, lambda x, y, c: (1 - x, y, c),
                               dict(keep=[s[0] for s in sums]))
    if stage == "x":
        pos = jnp.stack([y]).astype(jnp.int32)
        sums = [_add_keep_give(
            f"rs{tag}_x_add{k}", pos,
            st["keep"][k], lambda i, p: (p[0],), lambda i, p: (1 - p[0],),
            recv[k], lambda i, p: (p[0],), lambda i, p: (1 - p[0],), 1) for k in range(n)]
        lands = [lax.empty(s[1].shape, BF16) for s in sums]
        return _exchange_begin(tag, "y", [s[1] for s in sums], lands, lambda x, y, c: (x, 1 - y, c),
                               dict(keep=[s[0] for s in sums]))
    if stage == "y":
        mine = [_add_last(f"rs{tag}_y_add{k}", st["keep"][k], recv[k]) for k in range(n)]
        lands = [lax.empty(m.shape, F32) for m in mine]
        return _exchange_begin(tag, "swap", mine, lands, lambda x, y, c: (x, y, 1 - c), {})
    return dict(done=list(zip(first, recv)), token=None)


def _all_reduce_small(v):
    r = v.shape[0]

    def body(v_ref, out_ref, buf, send_sems, recv_sems, local_sem):
        x, y, c = _position()
        me, sibling = (x, y, c), (x, y, 1 - c)
        chips = [(1 - x, y), (x, 1 - y), (1 - x, 1 - y)]

        def rows(px, py, pc):
            return buf.at[pl.ds((4 * px + 2 * py + pc) * r, r), :]

        def copy(k, block, to, src=None):
            return _remote(rows(*block) if src is None else src, rows(*block), send_sems.at[k], recv_sems.at[k], to)

        mine = pltpu.make_async_copy(v_ref, rows(*me), local_sem)
        mine.start()
        first = [copy(0, me, sibling, src=v_ref)]
        first += [copy(1 + j, me, (*chip, c), src=v_ref) for j, chip in enumerate(chips)]
        for cp in first:
            cp.start()
        passed = [copy(4 + j, (*chip, c), sibling) for j, chip in enumerate(chips)]
        for j, chip in enumerate(chips):
            copy(1 + j, (*chip, c), me).wait_recv()
            passed[j].start()
        copy(0, sibling, me).wait_recv()
        for j, chip in enumerate(chips):
            copy(4 + j, (*chip, 1 - c), me).wait_recv()
        for cp in first + passed:
            cp.wait_send()
        mine.wait()
        acc = buf[0:r, :]
        for dev in range(1, N_DEV):
            acc = acc + buf[dev * r:(dev + 1) * r, :]
        out_ref[...] = acc

    return pl.pallas_call(
        body, name="all_reduce_small",
        out_shape=jax.ShapeDtypeStruct(v.shape, F32),
        in_specs=[pl.BlockSpec(memory_space=pltpu.VMEM)],
        out_specs=pl.BlockSpec(memory_space=pltpu.VMEM),
        scratch_shapes=[pltpu.VMEM((N_DEV * r, LANES), F32), pltpu.SemaphoreType.DMA((7,)),
                        pltpu.SemaphoreType.DMA((7,)), pltpu.SemaphoreType.DMA],
        compiler_params=pltpu.CompilerParams(has_side_effects=True, vmem_limit_bytes=VMEM_LIMIT_V7X),
    )(v)


def _adamw_update(w, gg, m, v):
    mn = ADAM_B1 * m + (1.0 - ADAM_B1) * gg
    vn = ADAM_B2 * v + (1.0 - ADAM_B2) * (gg * gg)
    m_hat = mn / (1.0 - ADAM_B1 ** ADAM_STEP)
    v_hat = vn / (1.0 - ADAM_B2 ** ADAM_STEP)
    return -ADAM_LR * (m_hat / (jnp.sqrt(v_hat) + ADAM_EPS) + ADAM_WD * w), mn, vn


def _adamw(name, w, g, m, v):
    def body(w_ref, g_ref, m_ref, v_ref, d_ref, mo_ref, vo_ref):
        d_ref[...], mo_ref[...], vo_ref[...] = _adamw_update(w_ref[...], g_ref[...], m_ref[...], v_ref[...])

    blk = pl.BlockSpec(w.shape, lambda i: (0, 0))
    return pl.pallas_call(
        body, name=name, out_shape=(jax.ShapeDtypeStruct(w.shape, F32),) * 3, grid=(1,),
        in_specs=[blk] * 4, out_specs=(blk,) * 3, compiler_params=_cparams("arbitrary"),
    )(w, g, m, v)


def _rows_to_bf16(name, w):
    r, _, c = w.shape

    def body(w_ref, o_ref):
        o_ref[...] = w_ref[:, 0, :].astype(BF16)

    return pl.pallas_call(
        body, name=name, out_shape=jax.ShapeDtypeStruct((r, c), BF16), grid=(1,),
        in_specs=[pl.BlockSpec((r, 1, c), lambda i: (0, 0, 0))], out_specs=pl.BlockSpec((r, c), lambda i: (0, 0)),
        compiler_params=_cparams("arbitrary"),
    )(w)


def _adamw_rows(name, pos_c, w, g_mine, g_other, m, v):
    r, _, c = w.shape
    ch = c // 2

    def body(p_ref, w_ref, gm_ref, go_ref, m_ref, v_ref, g_ref, d_ref, mo_ref, vo_ref):
        gg = jnp.where(pl.program_id(0) == p_ref[0], gm_ref[...], go_ref[...])
        dl, mn, vn = _adamw_update(w_ref[:, 0, :], gg, m_ref[:, 0, :], v_ref[:, 0, :])
        g_ref[:, 0, :] = gg
        d_ref[:, 0, :] = dl
        mo_ref[:, 0, :] = mn
        vo_ref[:, 0, :] = vn

    rows = pl.BlockSpec((r, 1, ch), lambda h, p: (0, 0, h))
    half = pl.BlockSpec((r, ch), lambda h, p: (0, 0))
    return pl.pallas_call(
        body, name=name, out_shape=(jax.ShapeDtypeStruct(w.shape, F32),) * 4,
        grid_spec=pltpu.PrefetchScalarGridSpec(
            num_scalar_prefetch=1, grid=(2,), in_specs=[rows, half, half, rows, rows], out_specs=(rows,) * 4),
        compiler_params=_cparams("parallel"),
    )(pos_c, w, g_mine, g_other, m, v)


def _adamw_halves(name, pos_c, w, g_mine, g_other, m, v, tr, dep=None):
    r, c = w.shape
    rh = r // 2
    tr = tr if rh % tr == 0 else rh
    nt = rh // tr

    def body(p_ref, w_ref, gm_ref, go_ref, m_ref, v_ref, g_ref, d_ref, mo_ref, vo_ref):
        gg = jnp.where(pl.program_id(0) == p_ref[0], gm_ref[...], go_ref[...])
        g_ref[...] = gg
        d_ref[...], mo_ref[...], vo_ref[...] = _adamw_update(w_ref[...], gg, m_ref[...], v_ref[...])

    full = pl.BlockSpec((tr, c), lambda h, i, p: (h * nt + i, 0))
    half = pl.BlockSpec((tr, c), lambda h, i, p: (i, 0))
    dep_specs, dep_ops = _dep_args(dep)
    return pl.pallas_call(
        _after(body, 6, dep), name=name, out_shape=(jax.ShapeDtypeStruct((r, c), F32),) * 4,
        grid_spec=pltpu.PrefetchScalarGridSpec(
            num_scalar_prefetch=1, grid=(2, nt),
            in_specs=[full, half, half, full, full] + dep_specs, out_specs=(full,) * 4),
        compiler_params=_cparams("parallel", "parallel"),
    )(pos_c, w, g_mine, g_other, m, v, *dep_ops)


def _col_sharded_to_comm(g):
    k, n = g.shape
    return g.reshape(2, k // 2, N_CHIPS, n // N_CHIPS).transpose(2, 0, 1, 3)


def _row_sharded_to_comm(g):
    r, c = g.shape
    return g.reshape(N_CHIPS, 2, r // (2 * N_CHIPS), c)


def _col_sharded_full(g):
    _, _, rh, c = g.shape
    return g.reshape(N_CHIPS, 2 * rh, c).transpose(1, 0, 2).reshape(2 * rh, N_CHIPS * c)


def _row_sharded_full(g):
    _, _, rh, c = g.shape
    return g.reshape(N_CHIPS * 2 * rh, c)


def _chip_rows(w3, start, stop, own=None, me=None):
    r = w3.shape[1]
    parts = []
    for chip in range(N_CHIPS):
        lo, hi = max(start - chip * r, 0), min(stop - chip * r, r)
        if lo < hi:
            part = w3[chip, lo:hi]
            parts.append(part if own is None else jnp.where(me == chip, own[lo:hi], part))
    return parts[0] if len(parts) == 1 else jnp.concatenate(parts, axis=0)


def _pack_small(g1, bfv, mix, scale, g2n, gf, extra=None):
    row8 = jnp.pad(bfv.reshape(1, N_HEADS), ((0, 0), (0, LANES - N_HEADS)))
    if extra is not None:
        row8 = row8 + jnp.pad(extra[:, :1], ((0, 0), (N_HEADS, LANES - N_HEADS - 1)))
    return jnp.concatenate([
        g1.reshape(8, LANES), jnp.pad(row8, ((0, 7), (0, 0))), mix.reshape(512, LANES),
        jnp.pad(scale.reshape(4, LANES), ((0, 4), (0, 0))), g2n.reshape(8, LANES), gf.reshape(8, LANES)], axis=0)


def _unpack_small(s, like):
    g1, bfv, mix, scale, g2n, gf = like
    return (s[0:8].reshape(g1.shape), s[8, :N_HEADS].reshape(bfv.shape), s[16:528].reshape(mix.shape),
            s[528:532].reshape(scale.shape), s[536:544].reshape(g2n.shape), s[544:552].reshape(gf.shape))


class _MeshLinks:
    def __init__(self, shards_in, shards_rest):
        self.gin = _gather_begin("in", shards_in, None, column_halves=True)
        self.grest = _gather_begin("rest", shards_rest, self.gin["token"])
        self.tokens = {"gather": self.grest["token"]}
        self.groups = {}

    @property
    def token(self):
        return list(self.tokens.values())

    def tie(self, x):
        return _tie(x, self.token)

    def weights_in(self, after):
        st = _gather_forward(self.gin, after)
        (g,), (own,) = _gather_end(st, st["token"], merge=False)
        return g, own, 2 * lax.axis_index("x") + lax.axis_index("y")

    def rest_forward(self, after):
        self.grest = _gather_forward(self.grest, after)
        self.tokens["gather"] = self.grest["token"]

    def weights_rest(self, after):
        g = _gather_end(self.grest, after)
        del self.tokens["gather"]
        return [_col_sharded_full(g[0]), _col_sharded_full(g[1])] + [_row_sharded_full(a) for a in g[2:]]

    def reduce_begin(self, tag, grads, column_halves=False):
        self.groups[tag] = _reduce_begin(tag, grads, column_halves)
        self.tokens[tag] = self.groups[tag]["token"]

    def advance(self, after):
        for tag, st in self.groups.items():
            if "done" not in st:
                self.groups[tag] = _reduce_advance(st, after)
                if self.groups[tag]["token"] is None:
                    del self.tokens[tag]
                else:
                    self.tokens[tag] = self.groups[tag]["token"]

    def reduced(self, tag):
        return self.groups[tag]["done"]


class _NoLinks:
    token = None

    def __init__(self, w_in, rest):
        self.w_in, self.rest, self.grads = w_in, rest, {}

    def tie(self, x):
        return x

    def weights_in(self, after):
        return self.w_in, None, None

    def rest_forward(self, after):
        pass

    def weights_rest(self, after):
        return self.rest

    def reduce_begin(self, tag, grads, column_halves=False):
        self.grads[tag] = grads

    def advance(self, after):
        pass


def _local_step(links, x, target, seq, norm1_g, b_forget, pool_mix, pool_scale, norm2_g, norm_f_g):
    t, d = x.shape
    tq = min(256, seq)
    aw = ATTN_WIDTH
    o_q, o_f, o_g = POOL_WIDTH, POOL_WIDTH + 3 * aw, POOL_WIDTH + 3 * aw + N_HEADS
    bf = jnp.pad(b_forget, ((0, 0), (0, LANES - N_HEADS)))
    mixb = pool_mix.astype(BF16)

    h = _norm_fwd("norm1_fwd", x, links.tie(norm1_g), 512)
    w_in, own, me = links.weights_in(h)
    wu = _chip_rows(w_in, 0, o_q, own, me)
    wqkv = _chip_rows(w_in, o_q, o_f, own, me)
    wft = jnp.pad(_chip_rows(w_in, o_f, o_g, own, me), ((0, LANES - N_HEADS), (0, 0)))
    wg2 = _chip_rows(w_in, o_g, N_CHIPS * w_in.shape[1], own, me)
    wf = wft.T
    u = _matmul("mm_u", h, wu, "nt", F32, 1024, 512, d)
    g2 = _matmul("mm_gates", h, wg2, "nt", BF16, 1024, 512, d)
    fl, fcum = _forget_fwd(h, wf, bf, seq)
    qa, ka, v = _attn_prep(h, _head_blocks(wqkv[:aw]), _head_blocks(wqkv[aw:2 * aw]), wqkv[2 * aw:], fcum, 1024)
    p, ps = _pool_fwd(u, mixb, pool_scale, seq)
    links.rest_forward([ps, qa, g2])
    o, lse = _attn_fwd(qa, ka, v, seq, tq, dep=links.token)
    w_pool_out, w_attn_out, w_out, w_ffn_gate, w_ffn_up, w_ffn_down = links.weights_rest(o)
    merged, x1 = _merge_fwd(x, ps, o, g2, w_pool_out, w_attn_out, w_out, 512)
    h2, gt, up, act, x2 = _ffn_fwd(x1, norm2_g, w_ffn_gate, w_ffn_up, w_ffn_down, 1024, 256)
    loss, dx2, d_gf = _final_fwd_bwd(x2, target, norm_f_g, 512)

    dgt, dup, dx1, d_g2n = _ffn_bwd(dx2, x1, norm2_g, gt, up, w_ffn_gate, w_ffn_up, w_ffn_down, 1024, 256)
    d_wd = _matmul("dw_down", act, dx2, "tn", F32, 1408, 1024, 1024)
    d_wg = _matmul("dw_gate", dgt, h2, "tn", F32, 1408, 1024, 1024)
    d_wu = _matmul("dw_up", dup, h2, "tn", F32, 1408, 1024, 1024)
    links.reduce_begin("a", [_row_sharded_to_comm(g) for g in (d_wg, d_wu, d_wd)])
    dpy, day, dg2, dps, da = _merge_bwd(dx1, ps, o, g2, w_pool_out, w_attn_out, w_out, 512, dep=links.token)
    links.advance(dps)
    d_wout = _matmul("dw_out", merged, dx1, "tn", F32, 1024, 1024, 1024)
    d_wpo = _matmul("dw_pool_out", ps, dpy, "tn", F32, 512, 1024, 1024)
    d_wao = _matmul("dw_attn_out", o, day, "tn", F32, 512, 1024, 1024)
    links.reduce_begin("m", [_col_sharded_to_comm(d_wpo), _col_sharded_to_comm(d_wao), _row_sharded_to_comm(d_wout)])
    dq, dk, dv, dfr = _attn_bwd(qa, ka, v, da, lse, seq, tq, dep=links.token)
    links.advance(dq)
    dfc = jnp.pad(dfr.reshape(N_HEADS, t).T, ((0, 0), (0, LANES - N_HEADS)))
    dfl, d_bf = _forget_bwd(dfc, fl, bf, seq)
    du, d_mix, d_scale = _pool_bwd(dps, p, mixb, links.tie(pool_scale), seq)
    d_wu_in = _matmul("dw_in_u", du, h, "tn", F32, 512, 1024, 1024)
    d_wq = _matmul("dw_in_q", dq, h, "tn", F32, 512, 1024, 1024)
    d_wk = _matmul("dw_in_k", dk, h, "tn", F32, 512, 1024, 1024)
    d_wv = _matmul("dw_in_v", dv, h, "tn", F32, 512, 1024, 1024)
    links.advance([d_wu_in, d_wq, d_wk, d_wv])
    d_wf = _matmul("dw_in_f", dfl, h, "tn", F32, LANES, 1024, 512)
    d_wg2 = _matmul("dw_in_gates", dg2, h, "tn", F32, 1024, 1024, 1024, dep=links.token)
    d_win = jnp.concatenate([d_wu_in, d_wq, d_wk, d_wv, d_wf[:N_HEADS], d_wg2], axis=0)
    comm_b = [d_win.reshape(N_CHIPS, d_win.shape[0] // N_CHIPS, d)]
    links.advance(comm_b)
    links.reduce_begin("b", comm_b, column_halves=True)
    dx, d_g1 = _in_bwd(du, dq, dk, dv, dg2, dfl, dx1, x, links.tie(norm1_g), wu, wqkv, wg2, wft, 512)
    links.advance(dx)
    small = (d_g1, d_bf[:, :N_HEADS], d_mix, d_scale, d_g2n, d_gf)
    return loss, dx, small


def kernel(x, norm1_g, w_in, b_forget, pool_mix, pool_scale, w_pool_out, w_attn_out, w_out, norm2_g, w_ffn_gate, w_ffn_up, w_ffn_down, norm_f_g, loss_target, m_norm1_g, m_w_in, m_b_forget, m_pool_mix, m_pool_scale, m_w_pool_out, m_w_attn_out, m_w_out, m_norm2_g, m_w_ffn_gate, m_w_ffn_up, m_w_ffn_down, m_norm_f_g, v_norm1_g, v_w_in, v_b_forget, v_pool_mix, v_pool_scale, v_w_pool_out, v_w_attn_out, v_w_out, v_norm2_g, v_w_ffn_gate, v_w_ffn_up, v_w_ffn_down, v_norm_f_g):
    nb, seq, d = x.shape
    group_a = ((w_ffn_gate, m_w_ffn_gate, v_w_ffn_gate, True, 9), (w_ffn_up, m_w_ffn_up, v_w_ffn_up, True, 10),
               (w_ffn_down, m_w_ffn_down, v_w_ffn_down, False, 11))
    group_m = ((w_pool_out, m_w_pool_out, v_w_pool_out, False, 5), (w_attn_out, m_w_attn_out, v_w_attn_out, False, 6),
               (w_out, m_w_out, v_w_out, False, 7))
    group_b = ((w_in, m_w_in, v_w_in, False, 1),)
    small_w = (norm1_g, b_forget, pool_mix, pool_scale, norm2_g, norm_f_g)
    small_m = (m_norm1_g, m_b_forget, m_pool_mix, m_pool_scale, m_norm2_g, m_norm_f_g)
    small_v = (v_norm1_g, v_b_forget, v_pool_mix, v_pool_scale, v_norm2_g, v_norm_f_g)
    small_pos = (0, 2, 3, 4, 8, 12)
    view = lambda a, tr: a[0].T if tr else a[0]
    unview = lambda a, tr, like: (a.T if tr else a).reshape(like.shape)

    def shard(w, tr):
        lw = view(w, tr).astype(BF16)
        return lw.reshape(2, lw.shape[0] // 2, lw.shape[1])

    cm = lambda a: jnp.transpose(a, (2, 0, 1))
    shard_in = _rows_to_bf16("w_in_to_bf16", cm(w_in))
    links = _MeshLinks([shard_in],
                       [shard(w_pool_out, False), shard(w_attn_out, False), shard(w_out, False),
                        shard(w_ffn_gate, True), shard(w_ffn_up, True), shard(w_ffn_down, False)])
    loss, dx, small_g = _local_step(
        links, x.reshape(nb * seq, d), loss_target.reshape(nb * seq, d), seq,
        norm1_g, b_forget, pool_mix[0], pool_scale, norm2_g, norm_f_g.reshape(1, d))

    grads, deltas, new_m, new_v = [None] * 13, [None] * 13, [None] * 13, [None] * 13
    pos_c = jnp.stack([lax.axis_index("c")]).astype(jnp.int32)

    def update(tag, group, dep):
        last = []
        for k, ((w, m, v, tr, pos), (mine, other)) in enumerate(zip(group, links.reduced(tag))):
            outs = _adamw_halves(f"adamw_{tag}{k}", pos_c, view(w, tr), mine, other, view(m, tr), view(v, tr), 256,
                                 dep=dep)
            grads[pos], deltas[pos], new_m[pos], new_v[pos] = (unview(a, tr, w) for a in outs)
            last.append(outs[1])
        return last

    last = update("a", group_a, links.token) + update("m", group_m, links.token)
    links.advance(last)
    small_sum = _all_reduce_small(links.tie(_pack_small(*small_g, extra=loss)))
    loss_out = small_sum[8, N_HEADS]
    dl, mn, vn = _adamw("adamw_small", _pack_small(*small_w), small_sum * _small_mask(), _pack_small(*small_m),
                        _pack_small(*small_v))
    for pos, g, a, b, e in zip(small_pos, _unpack_small(small_sum, small_w), _unpack_small(dl, small_w),
                               _unpack_small(mn, small_w), _unpack_small(vn, small_w)):
        grads[pos], deltas[pos], new_m[pos], new_v[pos] = g, a, b, e
    links.advance(dl)
    links.advance(links.token)
    (mine, other), = links.reduced("b")
    outs = _adamw_rows("adamw_b0", pos_c, cm(w_in), mine, other, cm(m_w_in), cm(v_w_in))
    grads[1], deltas[1], new_m[1], new_v[1] = (jnp.transpose(a, (1, 2, 0)) for a in outs)

    return (loss_out, dx.reshape(nb, seq, d), *grads, *deltas, *new_m, *new_v)


def _small_mask():
    rows = lax.broadcasted_iota(jnp.int32, (552, LANES), 0)
    lanes = lax.broadcasted_iota(jnp.int32, (552, LANES), 1)
    return jnp.where(jnp.logical_and(rows == 8, lanes == N_HEADS), 0.0, 1.0).astype(F32)
```

```python
import functools

import jax
import jax.numpy as jnp
from jax import lax
from jax.experimental import pallas as pl
from jax.experimental.pallas import tpu as pltpu

F32 = jnp.float32
BF16 = jnp.bfloat16

D_MODEL = 1024
POOL_WINDOWS = (2, 4, 8, 16)
POOL_GROUPS = 4
POOL_GROUP_DIM = 128
POOL_WIDTH = 512
HEAD_DIM = 64
N_HEADS = 8
ATTN_WIDTH = 512
D_FF = 2816
RMS_EPS = 1e-6
ATTN_SCALE = HEAD_DIM ** -0.5
NEG_BIG = -1e30

ADAM_LR = 0.001
ADAM_B1 = 0.9
ADAM_B2 = 0.999
ADAM_EPS = 1e-08
ADAM_WD = 0.01
ADAM_STEP = 10

LANES = 128
N_CHIPS = 4
N_DEV = 8
VMEM_LIMIT_V7X = 52 * 1024 * 1024
MESH = pl.DeviceIdType.MESH
ANY = pl.BlockSpec(memory_space=pl.ANY)


def _cparams(*sem):
    return pltpu.CompilerParams(dimension_semantics=sem if sem else None, vmem_limit_bytes=VMEM_LIMIT_V7X)


def _dep_list(dep):
    return [] if dep is None else (list(dep) if isinstance(dep, (list, tuple)) else [dep])


def _after(body, n_in, dep):
    k = len(_dep_list(dep))
    if k == 0:
        return body

    def wrapped(*refs):
        body(*refs[:n_in], *refs[n_in + k:])

    return wrapped


def _dep_args(dep):
    deps = _dep_list(dep)
    return [ANY] * len(deps), deps


def _dot(a, b):
    return lax.dot_general(a, b, (((1,), (0,)), ((), ())), preferred_element_type=F32)


def _dot_nt(a, b):
    return lax.dot_general(a, b, (((1,), (1,)), ((), ())), preferred_element_type=F32)


def _dot_tn(a, b):
    return lax.dot_general(a, b, (((0,), (0,)), ((), ())), preferred_element_type=F32)


def _sigmoid(x):
    return jax.nn.sigmoid(x)


def _rms_fwd(x, g):
    r = lax.rsqrt(jnp.mean(x * x, axis=-1, keepdims=True) + RMS_EPS)
    return (x * r) * g


def _rms_bwd(x, g, dy):
    r = lax.rsqrt(jnp.mean(x * x, axis=-1, keepdims=True) + RMS_EPS)
    xh = x * r
    dg = jnp.sum(dy * xh, axis=0, keepdims=True)
    dxh = dy * g
    dx = r * (dxh - xh * jnp.mean(dxh * xh, axis=-1, keepdims=True))
    return dx, dg


def _matmul(name, a, b, mode, out_dtype, tm, tn, tk, dep=None):
    if mode == "nn":
        (m, k), (_, n) = a.shape, b.shape
    elif mode == "nt":
        (m, k), (n, _) = a.shape, b.shape
    else:
        (k, m), (_, n) = a.shape, b.shape
    tm, tn, tk = min(tm, m), min(tn, n), min(tk, k)
    assert m % tm == 0 and n % tn == 0 and k % tk == 0, (name, m, n, k, tm, tn, tk)
    nk = k // tk
    if mode == "tn":
        a_spec = pl.BlockSpec((tk, tm), lambda i, j, kk: (kk, i))
    else:
        a_spec = pl.BlockSpec((tm, tk), lambda i, j, kk: (i, kk))
    if mode == "nt":
        b_spec = pl.BlockSpec((tn, tk), lambda i, j, kk: (j, kk))
    else:
        b_spec = pl.BlockSpec((tk, tn), lambda i, j, kk: (kk, j))
    dot = {"nn": _dot, "nt": _dot_nt, "tn": _dot_tn}[mode]
    use_scratch = nk > 1 and out_dtype != F32

    def body(a_ref, b_ref, o_ref, *scratch):
        prod = dot(a_ref[...].astype(BF16), b_ref[...].astype(BF16))
        if nk == 1:
            o_ref[...] = prod.astype(out_dtype)
            return
        acc = scratch[0] if use_scratch else o_ref
        kk = pl.program_id(2)

        @pl.when(kk == 0)
        def _():
            acc[...] = prod

        @pl.when(kk > 0)
        def _():
            acc[...] += prod

        if use_scratch:
            @pl.when(kk == nk - 1)
            def _():
                o_ref[...] = acc[...].astype(out_dtype)

    dep_specs, dep_ops = _dep_args(dep)
    return pl.pallas_call(
        _after(body, 2, dep),
        name=name,
        out_shape=jax.ShapeDtypeStruct((m, n), out_dtype),
        grid=(m // tm, n // tn, nk),
        in_specs=[a_spec, b_spec] + dep_specs,
        out_specs=pl.BlockSpec((tm, tn), lambda i, j, kk: (i, j)),
        scratch_shapes=[pltpu.VMEM((tm, tn), F32)] if use_scratch else [],
        compiler_params=_cparams("parallel", "parallel", "arbitrary"),
    )(a, b, *dep_ops)


def _norm_fwd(name, x, g, tm):
    t, d = x.shape
    tm = min(tm, t)

    def body(x_ref, g_ref, h_ref):
        h_ref[...] = _rms_fwd(x_ref[...], g_ref[...]).astype(BF16)

    return pl.pallas_call(
        body, name=name, out_shape=jax.ShapeDtypeStruct((t, d), BF16), grid=(t // tm,),
        in_specs=[pl.BlockSpec((tm, d), lambda i: (i, 0)), pl.BlockSpec((1, d), lambda i: (0, 0))],
        out_specs=pl.BlockSpec((tm, d), lambda i: (i, 0)),
        compiler_params=_cparams("parallel"),
    )(x, g)


def _split3(x):
    hi = x.astype(BF16)
    r1 = x - hi.astype(F32)
    mid = r1.astype(BF16)
    lo = (r1 - mid.astype(F32)).astype(BF16)
    return hi, mid, lo


def _tri_dot(tri, x):
    hi, mid, lo = _split3(x)
    return _dot(tri, hi) + _dot(tri, mid) + _dot(tri, lo)


def _forget_fwd(h, wf, bf, seq):
    t, d = h.shape
    cb = min(256, seq)

    def body(h_ref, wf_ref, bf_ref, fl_ref, fc_ref):
        fl = _dot(h_ref[...], wf_ref[...])
        fl_ref[...] = fl
        xx = fl + bf_ref[...]
        lf = jnp.minimum(xx, 0.0) - jnp.log(1.0 + jnp.exp(-jnp.abs(xx)))
        ri = lax.broadcasted_iota(jnp.int32, (cb, cb), 0)
        ci = lax.broadcasted_iota(jnp.int32, (cb, cb), 1)
        tri = (ri >= ci).astype(BF16)
        carry = jnp.zeros((1, LANES), F32)
        for blk in range(seq // cb):
            cs = _tri_dot(tri, lf[blk * cb:(blk + 1) * cb]) + carry
            fc_ref[blk * cb:(blk + 1) * cb, :] = cs
            carry = cs[cb - 1:cb, :]

    return pl.pallas_call(
        body, name="forget_fwd",
        out_shape=(jax.ShapeDtypeStruct((t, LANES), F32), jax.ShapeDtypeStruct((t, LANES), F32)),
        grid=(t // seq,),
        in_specs=[pl.BlockSpec((seq, d), lambda b: (b, 0)), pl.BlockSpec((d, LANES), lambda b: (0, 0)),
                  pl.BlockSpec((1, LANES), lambda b: (0, 0))],
        out_specs=(pl.BlockSpec((seq, LANES), lambda b: (b, 0)), pl.BlockSpec((seq, LANES), lambda b: (b, 0))),
        compiler_params=_cparams("parallel"),
    )(h, wf, bf)


def _pool_fwd(u, mix, scale, seq):
    t = u.shape[0]

    def body(u_ref, mix_ref, sc_ref, p_ref, ps_ref):
        tpos = lax.broadcasted_iota(jnp.int32, (seq, POOL_GROUP_DIM), 0)
        for g in range(POOL_GROUPS):
            sl = slice(g * POOL_GROUP_DIM, (g + 1) * POOL_GROUP_DIM)
            ug = u_ref[:, sl]
            s = ug
            for lvl in range(g + 1):
                d = 2 ** lvl
                s = s + jnp.where(tpos >= d, pltpu.roll(s, d, 0), 0.0)
            cnt = jnp.minimum(tpos + 1, POOL_WINDOWS[g]).astype(F32)
            pb = (s / cnt - ug).astype(BF16)
            p_ref[:, sl] = pb
            ps_ref[:, sl] = (_dot(pb, mix_ref[g]) * sc_ref[:, sl]).astype(BF16)

    return pl.pallas_call(
        body, name="pool_fwd",
        out_shape=(jax.ShapeDtypeStruct((t, POOL_WIDTH), BF16), jax.ShapeDtypeStruct((t, POOL_WIDTH), BF16)),
        grid=(t // seq,),
        in_specs=[pl.BlockSpec((seq, POOL_WIDTH), lambda b: (b, 0)),
                  pl.BlockSpec((POOL_GROUPS, POOL_GROUP_DIM, POOL_GROUP_DIM), lambda b: (0, 0, 0)),
                  pl.BlockSpec((1, POOL_WIDTH), lambda b: (0, 0))],
        out_specs=(pl.BlockSpec((seq, POOL_WIDTH), lambda b: (b, 0)), pl.BlockSpec((seq, POOL_WIDTH), lambda b: (b, 0))),
        compiler_params=_cparams("parallel"),
    )(u, mix, scale)


def _aug_constants():
    w = N_HEADS * LANES
    rows = jnp.arange(3 * LANES)
    piece, head = rows // LANES, rows % LANES
    cols = jnp.arange(w)
    live = (head < N_HEADS)[:, None]
    pq = (live & (cols[None, :] == (head * LANES + HEAD_DIM + piece)[:, None])).astype(BF16)
    pk = -(live & (cols[None, :] == (head * LANES + HEAD_DIM + 3 + piece)[:, None])).astype(BF16)
    lane = cols % LANES
    oq = ((lane >= HEAD_DIM + 3) & (lane < HEAD_DIM + 6)).astype(F32)[None, :]
    ok = ((lane >= HEAD_DIM) & (lane < HEAD_DIM + 3)).astype(F32)[None, :]
    return pq, pk, oq, ok


def _head_blocks(wt):
    d = wt.shape[1]
    return jnp.pad(wt.reshape(N_HEADS, HEAD_DIM, d), ((0, 0), (0, LANES - HEAD_DIM), (0, 0))).reshape(N_HEADS * LANES, d)


def _attn_prep(h, wq, wk, wv, fcum, tm):
    t, d = h.shape
    tm = min(tm, t)
    w = N_HEADS * LANES
    pq, pk, oq, ok = _aug_constants()

    def body(h_ref, wq_ref, wk_ref, wv_ref, f_ref, pq_ref, pk_ref, oq_ref, ok_ref, qa_ref, ka_ref, v_ref):
        hh = h_ref[...]
        fs = jnp.concatenate(_split3(f_ref[...]), axis=1)
        q = _dot_nt(hh, wq_ref[...]).astype(BF16).astype(F32) * ATTN_SCALE
        qa_ref[...] = (q + _dot(fs, pq_ref[...]) + oq_ref[...]).astype(BF16)
        k = _dot_nt(hh, wk_ref[...]).astype(BF16).astype(F32)
        ka_ref[...] = (k + _dot(fs, pk_ref[...]) + ok_ref[...]).astype(BF16)
        v_ref[...] = _dot_nt(hh, wv_ref[...]).astype(BF16)

    row = lambda n: pl.BlockSpec((tm, n), lambda i: (i, 0))
    full = lambda a: pl.BlockSpec(a.shape, lambda i: (0, 0))
    return pl.pallas_call(
        body, name="attn_prep",
        out_shape=(jax.ShapeDtypeStruct((t, w), BF16), jax.ShapeDtypeStruct((t, w), BF16),
                   jax.ShapeDtypeStruct((t, ATTN_WIDTH), BF16)),
        grid=(t // tm,),
        in_specs=[row(d), full(wq), full(wk), full(wv), row(LANES), full(pq), full(pk), full(oq), full(ok)],
        out_specs=(row(w), row(w), row(ATTN_WIDTH)),
        compiler_params=_cparams("parallel"),
    )(h, wq, wk, wv, fcum, pq, pk, oq, ok)


def _fold_lanes(x, op):
    out = x[:, :LANES]
    for g in range(1, x.shape[1] // LANES):
        out = op(out, x[:, g * LANES:(g + 1) * LANES])
    return out


def _causal_sweep(i, tile, carry):
    def pair(jj, c):
        return tile(2 * jj + 1, tile(2 * jj, c, False), False)

    carry = lax.fori_loop(0, i // 2, pair, carry)
    carry = lax.cond(i % 2 == 1, lambda c: tile(i - 1, c, False), lambda c: c, carry)
    return tile(i, carry, True)


def _attn_fwd(qa, ka, v, seq, tq, dep=None):
    t = qa.shape[0]
    nq = seq // tq
    hp_n = N_HEADS // 2
    heads = [slice(e * LANES, (e + 1) * LANES) for e in range(2)]

    def body(q_ref, k_ref, v_ref, o_ref, lse_ref, s_buf):
        i = pl.program_id(2)
        diag_ok = lax.broadcasted_iota(jnp.int32, (tq, tq), 0) >= lax.broadcasted_iota(jnp.int32, (tq, tq), 1)
        qs = [q_ref[:, hl] for hl in heads]

        def sweep1(j, mxs, diagonal):
            r0 = pl.multiple_of(j * tq, tq)
            out = []
            for e, hl in enumerate(heads):
                s = _dot_nt(qs[e], k_ref[pl.ds(r0, tq), hl])
                if diagonal:
                    s = jnp.where(diag_ok, s, NEG_BIG)
                s_buf[e, j] = s
                out.append(jnp.maximum(mxs[e], _fold_lanes(s, jnp.maximum)))
            return tuple(out)

        mxs = _causal_sweep(i, sweep1, (jnp.full((tq, LANES), NEG_BIG, F32),) * 2)
        ms = [jnp.max(mx, axis=1, keepdims=True) for mx in mxs]

        def sweep2(j, carry, diagonal):
            r0 = pl.multiple_of(j * tq, tq)
            vv = v_ref[pl.ds(r0, tq), :]
            out = []
            for e in range(2):
                p = jnp.exp(s_buf[e, j] - ms[e])
                out += [carry[2 * e] + _fold_lanes(p, jnp.add), carry[2 * e + 1] + _dot(p.astype(BF16), vv)]
            return tuple(out)

        res = _causal_sweep(i, sweep2, (jnp.zeros((tq, LANES), F32),) * 4)
        outs = []
        for e in range(2):
            l = jnp.sum(res[2 * e], axis=1, keepdims=True)
            outs.append(res[2 * e + 1] / l)
            lse_ref[:, e:e + 1] = ms[e] + jnp.log(l)
        lane = lax.broadcasted_iota(jnp.int32, (tq, LANES), 1)
        o_ref[...] = jnp.where(lane < HEAD_DIM, outs[0], outs[1])

    dep_specs, dep_ops = _dep_args(dep)
    return pl.pallas_call(
        _after(body, 3, dep), name="attn_fwd",
        out_shape=(jax.ShapeDtypeStruct((t, ATTN_WIDTH), F32), jax.ShapeDtypeStruct((hp_n, t, 2), F32)),
        grid=(t // seq, hp_n, nq),
        in_specs=[pl.BlockSpec((tq, 2 * LANES), lambda b, hp, i: (b * nq + i, hp)),
                  pl.BlockSpec((seq, 2 * LANES), lambda b, hp, i: (b, hp)),
                  pl.BlockSpec((seq, LANES), lambda b, hp, i: (b, hp))] + dep_specs,
        out_specs=(pl.BlockSpec((tq, LANES), lambda b, hp, i: (b * nq + i, hp)),
                   pl.BlockSpec((None, tq, 2), lambda b, hp, i: (hp, b * nq + i, 0))),
        scratch_shapes=[pltpu.VMEM((2, nq, tq, tq), F32)],
        compiler_params=_cparams("parallel", "parallel", "arbitrary"),
    )(qa, ka, v, *dep_ops)


def _merge_fwd(x, ps, o, g2, wpo, wao, wout, tm):
    t, d = x.shape
    tm = min(tm, t)

    def body(x_ref, ps_ref, o_ref, gp_ref, ga_ref, wpo_ref, wao_ref, wout_ref, mg_ref, x1_ref):
        py = _dot(ps_ref[...], wpo_ref[...])
        ay = _dot(o_ref[...].astype(BF16), wao_ref[...])
        mb = (_sigmoid(gp_ref[...].astype(F32)) * py + _sigmoid(ga_ref[...].astype(F32)) * ay).astype(BF16)
        mg_ref[...] = mb
        x1_ref[...] = x_ref[...] + _dot(mb, wout_ref[...])

    row = lambda w: pl.BlockSpec((tm, w), lambda i: (i, 0))
    full = lambda a: pl.BlockSpec(a.shape, lambda i: (0, 0))
    return pl.pallas_call(
        body, name="merge_fwd",
        out_shape=(jax.ShapeDtypeStruct((t, d), BF16), jax.ShapeDtypeStruct((t, d), F32)),
        grid=(t // tm,),
        in_specs=[row(d), row(POOL_WIDTH), row(ATTN_WIDTH), pl.BlockSpec((tm, d), lambda i: (i, 0)),
                  pl.BlockSpec((tm, d), lambda i: (i, 1)), full(wpo), full(wao), full(wout)],
        out_specs=(row(d), row(d)),
        compiler_params=_cparams("parallel"),
    )(x, ps, o, g2, g2, wpo, wao, wout)


def _ffn_fwd(x1, g, wg, wu, wd, tm, tf):
    t, d = x1.shape
    f = wg.shape[0]
    tm = min(tm, t)
    nf = f // tf

    def body(x1_ref, g_ref, wg_ref, wu_ref, wd_ref, h2_ref, gt_ref, up_ref, act_ref, x2_ref):
        j = pl.program_id(1)

        @pl.when(j == 0)
        def _():
            h2_ref[...] = _rms_fwd(x1_ref[...], g_ref[...]).astype(BF16)

        h2 = h2_ref[...]
        gt = _dot_nt(h2, wg_ref[...])
        up = _dot_nt(h2, wu_ref[...])
        sg = _sigmoid(gt)
        silu = gt * sg
        act = (silu * up).astype(BF16)
        gt_ref[...] = (up * (sg * (1.0 + gt * (1.0 - sg)))).astype(BF16)
        up_ref[...] = silu.astype(BF16)
        act_ref[...] = act
        prod = _dot(act, wd_ref[...])

        @pl.when(j == 0)
        def _():
            x2_ref[...] = prod

        @pl.when(j > 0)
        def _():
            x2_ref[...] += prod

        @pl.when(j == nf - 1)
        def _():
            x2_ref[...] += x1_ref[...]

    return pl.pallas_call(
        body, name="ffn_fwd",
        out_shape=(jax.ShapeDtypeStruct((t, d), BF16), jax.ShapeDtypeStruct((t, f), BF16),
                   jax.ShapeDtypeStruct((t, f), BF16), jax.ShapeDtypeStruct((t, f), BF16),
                   jax.ShapeDtypeStruct((t, d), F32)),
        grid=(t // tm, nf),
        in_specs=[pl.BlockSpec((tm, d), lambda i, j: (i, 0)), pl.BlockSpec((1, d), lambda i, j: (0, 0)),
                  pl.BlockSpec((tf, d), lambda i, j: (j, 0)), pl.BlockSpec((tf, d), lambda i, j: (j, 0)),
                  pl.BlockSpec((tf, d), lambda i, j: (j, 0))],
        out_specs=(pl.BlockSpec((tm, d), lambda i, j: (i, 0)), pl.BlockSpec((tm, tf), lambda i, j: (i, j)),
                   pl.BlockSpec((tm, tf), lambda i, j: (i, j)), pl.BlockSpec((tm, tf), lambda i, j: (i, j)),
                   pl.BlockSpec((tm, d), lambda i, j: (i, 0))),
        compiler_params=_cparams("parallel", "arbitrary"),
    )(x1, g, wg, wu, wd)


def _final_fwd_bwd(x2, target, g, tm):
    t, d = x2.shape
    tm = min(tm, t)

    def body(x_ref, t_ref, g_ref, loss_ref, dx_ref, dg_ref):
        i = pl.program_id(0)
        x = x_ref[...]
        gg = g_ref[...]
        err = _rms_fwd(x, gg) - t_ref[...]
        part = 0.5 * jnp.sum(jnp.mean(err * err, axis=-1, keepdims=True), axis=0, keepdims=True)
        dx, dg = _rms_bwd(x, gg, err * (1.0 / d))
        dx_ref[...] = dx

        @pl.when(i == 0)
        def _():
            loss_ref[...] = jnp.zeros_like(loss_ref)
            dg_ref[...] = jnp.zeros_like(dg_ref)

        loss_ref[...] += jnp.broadcast_to(part, loss_ref.shape)
        dg_ref[...] += dg

    return pl.pallas_call(
        body, name="final_fwd_bwd",
        out_shape=(jax.ShapeDtypeStruct((1, LANES), F32), jax.ShapeDtypeStruct((t, d), F32),
                   jax.ShapeDtypeStruct((1, d), F32)),
        grid=(t // tm,),
        in_specs=[pl.BlockSpec((tm, d), lambda i: (i, 0)), pl.BlockSpec((tm, d), lambda i: (i, 0)),
                  pl.BlockSpec((1, d), lambda i: (0, 0))],
        out_specs=(pl.BlockSpec((1, LANES), lambda i: (0, 0)), pl.BlockSpec((tm, d), lambda i: (i, 0)),
                   pl.BlockSpec((1, d), lambda i: (0, 0))),
        compiler_params=_cparams("arbitrary"),
    )(x2, target, g)


def _ffn_bwd(dx2, x1, g, gt, up, wg, wu, wd, tm, tf):
    t, d = dx2.shape
    f = gt.shape[1]
    tm = min(tm, t)
    nf = f // tf

    def body(dx2_ref, x1_ref, g_ref, gt_ref, up_ref, wg_ref, wu_ref, wd_ref, dgt_ref, dup_ref, dx1_ref, dg_ref, acc_ref,
             dxb_ref):
        i, j = pl.program_id(0), pl.program_id(1)

        @pl.when(j == 0)
        def _():
            dxb_ref[...] = dx2_ref[...].astype(BF16)

        dact = _dot_nt(dxb_ref[...], wd_ref[...])
        dgt = (dact * gt_ref[...].astype(F32)).astype(BF16)
        dup = (dact * up_ref[...].astype(F32)).astype(BF16)
        dgt_ref[...] = dgt
        dup_ref[...] = dup
        contrib = _dot(dgt, wg_ref[...]) + _dot(dup, wu_ref[...])

        @pl.when(j == 0)
        def _():
            acc_ref[...] = contrib

        @pl.when(j > 0)
        def _():
            acc_ref[...] += contrib

        @pl.when(jnp.logical_and(i == 0, j == 0))
        def _():
            dg_ref[...] = jnp.zeros_like(dg_ref)

        @pl.when(j == nf - 1)
        def _():
            dxn, dg = _rms_bwd(x1_ref[...], g_ref[...], acc_ref[...])
            dx1_ref[...] = dx2_ref[...] + dxn
            dg_ref[...] += dg

    return pl.pallas_call(
        body, name="ffn_bwd",
        out_shape=(jax.ShapeDtypeStruct((t, f), BF16), jax.ShapeDtypeStruct((t, f), BF16),
                   jax.ShapeDtypeStruct((t, d), F32), jax.ShapeDtypeStruct((1, d), F32)),
        grid=(t // tm, nf),
        in_specs=[pl.BlockSpec((tm, d), lambda i, j: (i, 0)), pl.BlockSpec((tm, d), lambda i, j: (i, 0)),
                  pl.BlockSpec((1, d), lambda i, j: (0, 0)),
                  pl.BlockSpec((tm, tf), lambda i, j: (i, j)), pl.BlockSpec((tm, tf), lambda i, j: (i, j)),
                  pl.BlockSpec((tf, d), lambda i, j: (j, 0)), pl.BlockSpec((tf, d), lambda i, j: (j, 0)),
                  pl.BlockSpec((tf, d), lambda i, j: (j, 0))],
        out_specs=(pl.BlockSpec((tm, tf), lambda i, j: (i, j)), pl.BlockSpec((tm, tf), lambda i, j: (i, j)),
                   pl.BlockSpec((tm, d), lambda i, j: (i, 0)), pl.BlockSpec((1, d), lambda i, j: (0, 0))),
        scratch_shapes=[pltpu.VMEM((tm, d), F32), pltpu.VMEM((tm, d), BF16)],
        compiler_params=_cparams("arbitrary", "arbitrary"),
    )(dx2, x1, g, gt, up, wg, wu, wd)


def _merge_bwd(dx1, ps, o, g2, wpo, wao, wout, tm, dep=None):
    t, d = dx1.shape
    tm = min(tm, t)

    def body(dx1_ref, ps_ref, o_ref, gp_ref, ga_ref, wpo_ref, wao_ref, wout_ref, dpy_ref, day_ref, dg2_ref, dps_ref, da_ref):
        dm = _dot_nt(dx1_ref[...].astype(BF16), wout_ref[...])
        py = _dot(ps_ref[...], wpo_ref[...])
        ay = _dot(o_ref[...].astype(BF16), wao_ref[...])
        sp = _sigmoid(gp_ref[...].astype(F32))
        sa = _sigmoid(ga_ref[...].astype(F32))
        dpy = (dm * sp).astype(BF16)
        day = (dm * sa).astype(BF16)
        dpy_ref[...] = dpy
        day_ref[...] = day
        dg2_ref[:, :d] = (dm * py * (sp * (1.0 - sp))).astype(BF16)
        dg2_ref[:, d:] = (dm * ay * (sa * (1.0 - sa))).astype(BF16)
        dps_ref[...] = _dot_nt(dpy, wpo_ref[...])
        da_ref[...] = _dot_nt(day, wao_ref[...]).astype(BF16)

    row = lambda w: pl.BlockSpec((tm, w), lambda i: (i, 0))
    full = lambda a: pl.BlockSpec(a.shape, lambda i: (0, 0))
    dep_specs, dep_ops = _dep_args(dep)
    return pl.pallas_call(
        _after(body, 8, dep), name="merge_bwd",
        out_shape=(jax.ShapeDtypeStruct((t, d), BF16), jax.ShapeDtypeStruct((t, d), BF16),
                   jax.ShapeDtypeStruct((t, 2 * d), BF16), jax.ShapeDtypeStruct((t, POOL_WIDTH), F32),
                   jax.ShapeDtypeStruct((t, ATTN_WIDTH), BF16)),
        grid=(t // tm,),
        in_specs=[row(d), row(POOL_WIDTH), row(ATTN_WIDTH), pl.BlockSpec((tm, d), lambda i: (i, 0)),
                  pl.BlockSpec((tm, d), lambda i: (i, 1)), full(wpo), full(wao), full(wout)] + dep_specs,
        out_specs=(row(d), row(d), row(2 * d), row(POOL_WIDTH), row(ATTN_WIDTH)),
        compiler_params=_cparams("parallel"),
    )(dx1, ps, o, g2, g2, wpo, wao, wout, *dep_ops)


def _attn_bwd(qa, ka, v, do, lse4, seq, tq, dep=None):
    t = qa.shape[0]
    nq = seq // tq
    hp_n = N_HEADS // 2
    heads = [slice(e * LANES, (e + 1) * LANES) for e in range(2)]

    def body(q_ref, k_ref, v_ref, do_ref, lse_ref, dq_ref, dk_ref, dv_ref, dfr_ref, dk_acc, dv_acc, p_buf, dp_buf):
        diag_ok = lax.broadcasted_iota(jnp.int32, (tq, tq), 0) >= lax.broadcasted_iota(jnp.int32, (tq, tq), 1)
        lane_q = lax.broadcasted_iota(jnp.int32, (tq, LANES), 1)
        lane_s = lax.broadcasted_iota(jnp.int32, (seq, LANES), 1)
        mine_q = [lane_q < HEAD_DIM, lane_q >= HEAD_DIM]
        dv_acc[...] = jnp.zeros_like(dv_acc)
        dk_acc[...] = jnp.zeros_like(dk_acc)
        dfr_ref[...] = jnp.zeros_like(dfr_ref)

        def q_step(i, _):
            q0 = pl.multiple_of(i * tq, tq)
            qs = [q_ref[pl.ds(q0, tq), hl] for hl in heads]
            dov = do_ref[pl.ds(q0, tq), :]
            dos = [jnp.where(mq, dov, jnp.zeros((), BF16)) for mq in mine_q]
            lss = [lse_ref[pl.ds(q0, tq), e:e + 1] for e in range(2)]

            def sweep1(j, dls, diagonal):
                r0 = pl.multiple_of(j * tq, tq)
                vv = v_ref[pl.ds(r0, tq), :]
                out = []
                for e, hl in enumerate(heads):
                    s = _dot_nt(qs[e], k_ref[pl.ds(r0, tq), hl])
                    if diagonal:
                        s = jnp.where(diag_ok, s, NEG_BIG)
                    p = jnp.exp(s - lss[e])
                    dp = _dot_nt(dos[e], vv)
                    p_buf[e, j] = p
                    dp_buf[e, j] = dp
                    dv_acc[pl.ds(r0, tq), :] += _dot_tn(p.astype(BF16), dos[e])
                    out.append(dls[e] + _fold_lanes(p * dp, jnp.add))
                return tuple(out)

            dls = _causal_sweep(i, sweep1, (jnp.zeros((tq, LANES), F32),) * 2)
            dls = [jnp.sum(d, axis=1, keepdims=True) for d in dls]

            def sweep2(j, dqs, diagonal):
                r0 = pl.multiple_of(j * tq, tq)
                out = []
                for e, hl in enumerate(heads):
                    ds = p_buf[e, j] * (dp_buf[e, j] - dls[e])
                    dfr_ref[e, pl.ds(j, 1), :] += jnp.sum(ds, axis=0, keepdims=True)
                    dsb = ds.astype(BF16)
                    dk_acc[e, pl.ds(r0, tq), :] += _dot_tn(dsb, qs[e])
                    out.append(dqs[e] + _dot(dsb, k_ref[pl.ds(r0, tq), hl]))
                return tuple(out)

            dqs = _causal_sweep(i, sweep2, (jnp.zeros((tq, LANES), F32),) * 2)
            dq = jnp.where(mine_q[0], dqs[0], pltpu.roll(dqs[1], HEAD_DIM, 1)) * ATTN_SCALE
            dq_ref[pl.ds(q0, tq), :] = dq.astype(BF16)
            return 0

        lax.fori_loop(0, nq, q_step, 0)
        dk_ref[...] = jnp.where(lane_s < HEAD_DIM, dk_acc[0], pltpu.roll(dk_acc[1], HEAD_DIM, 1)).astype(BF16)
        dv_ref[...] = dv_acc[...].astype(BF16)

    wide = pl.BlockSpec((seq, 2 * LANES), lambda b, hp: (b, hp))
    col = pl.BlockSpec((seq, LANES), lambda b, hp: (b, hp))
    pair = pl.BlockSpec((None, seq, 2), lambda b, hp: (hp, b, 0))
    dep_specs, dep_ops = _dep_args(dep)
    return pl.pallas_call(
        _after(body, 5, dep), name="attn_bwd",
        out_shape=(jax.ShapeDtypeStruct((t, ATTN_WIDTH), BF16),) * 3 + (jax.ShapeDtypeStruct((N_HEADS, t // tq, tq), F32),),
        grid=(t // seq, hp_n),
        in_specs=[wide, wide, col, col, pair] + dep_specs,
        out_specs=(col, col, col, pl.BlockSpec((2, nq, tq), lambda b, hp: (hp, b, 0))),
        scratch_shapes=[pltpu.VMEM((2, seq, LANES), F32), pltpu.VMEM((seq, LANES), F32),
                        pltpu.VMEM((2, nq, tq, tq), F32), pltpu.VMEM((2, nq, tq, tq), F32)],
        compiler_params=_cparams("parallel", "arbitrary"),
    )(qa, ka, v, do, lse4, *dep_ops)


def _forget_bwd(dfc, fl, bf, seq):
    t = fl.shape[0]
    cb = min(256, seq)
    nb = seq // cb

    def body(dfc_ref, fl_ref, bf_ref, dfl_ref, db_ref):
        b = pl.program_id(0)
        ri = lax.broadcasted_iota(jnp.int32, (cb, cb), 0)
        ci = lax.broadcasted_iota(jnp.int32, (cb, cb), 1)
        tri = (ci >= ri).astype(BF16)
        carry = jnp.zeros((1, LANES), F32)
        dbs = jnp.zeros((1, LANES), F32)
        for blk in reversed(range(nb)):
            rs = slice(blk * cb, (blk + 1) * cb)
            dlf = _tri_dot(tri, -dfc_ref[rs, :]) + carry
            carry = dlf[0:1, :]
            dfl = dlf * _sigmoid(-(fl_ref[rs, :] + bf_ref[...]))
            dfl_ref[rs, :] = dfl.astype(BF16)
            dbs = dbs + jnp.sum(dfl, axis=0, keepdims=True)

        @pl.when(b == 0)
        def _():
            db_ref[...] = jnp.zeros_like(db_ref)

        db_ref[...] += dbs

    return pl.pallas_call(
        body, name="forget_bwd",
        out_shape=(jax.ShapeDtypeStruct((t, LANES), BF16), jax.ShapeDtypeStruct((1, LANES), F32)),
        grid=(t // seq,),
        in_specs=[pl.BlockSpec((seq, LANES), lambda b: (b, 0)), pl.BlockSpec((seq, LANES), lambda b: (b, 0)),
                  pl.BlockSpec((1, LANES), lambda b: (0, 0))],
        out_specs=(pl.BlockSpec((seq, LANES), lambda b: (b, 0)), pl.BlockSpec((1, LANES), lambda b: (0, 0))),
        compiler_params=_cparams("arbitrary"),
    )(dfc, fl, bf)


def _pool_bwd(dps, p, mix, scale, seq):
    t = dps.shape[0]

    def body(dps_ref, p_ref, mix_ref, sc_ref, du_ref, dmix_ref, dsc_ref):
        b = pl.program_id(0)

        @pl.when(b == 0)
        def _():
            dmix_ref[...] = jnp.zeros_like(dmix_ref)
            dsc_ref[...] = jnp.zeros_like(dsc_ref)

        tpos = lax.broadcasted_iota(jnp.int32, (seq, POOL_GROUP_DIM), 0)
        for g in range(POOL_GROUPS):
            sl = slice(g * POOL_GROUP_DIM, (g + 1) * POOL_GROUP_DIM)
            pb = p_ref[:, sl]
            dpsg = dps_ref[:, sl]
            pm = _dot(pb, mix_ref[g])
            dsc_ref[:, sl] += jnp.sum(dpsg * pm, axis=0, keepdims=True)
            dpm = (dpsg * sc_ref[:, sl]).astype(BF16)
            dmix_ref[g] += _dot_tn(pb, dpm)
            dp = _dot_nt(dpm, mix_ref[g])
            cnt = jnp.minimum(tpos + 1, POOL_WINDOWS[g]).astype(F32)
            s = dp / cnt
            for lvl in range(g + 1):
                d = 2 ** lvl
                s = s + jnp.where(tpos < seq - d, pltpu.roll(s, seq - d, 0), 0.0)
            du_ref[:, sl] = (s - dp).astype(BF16)

    return pl.pallas_call(
        body, name="pool_bwd",
        out_shape=(jax.ShapeDtypeStruct((t, POOL_WIDTH), BF16),
                   jax.ShapeDtypeStruct((POOL_GROUPS, POOL_GROUP_DIM, POOL_GROUP_DIM), F32),
                   jax.ShapeDtypeStruct((1, POOL_WIDTH), F32)),
        grid=(t // seq,),
        in_specs=[pl.BlockSpec((seq, POOL_WIDTH), lambda b: (b, 0)), pl.BlockSpec((seq, POOL_WIDTH), lambda b: (b, 0)),
                  pl.BlockSpec((POOL_GROUPS, POOL_GROUP_DIM, POOL_GROUP_DIM), lambda b: (0, 0, 0)),
                  pl.BlockSpec((1, POOL_WIDTH), lambda b: (0, 0))],
        out_specs=(pl.BlockSpec((seq, POOL_WIDTH), lambda b: (b, 0)),
                   pl.BlockSpec((POOL_GROUPS, POOL_GROUP_DIM, POOL_GROUP_DIM), lambda b: (0, 0, 0)),
                   pl.BlockSpec((1, POOL_WIDTH), lambda b: (0, 0))),
        compiler_params=_cparams("arbitrary"),
    )(dps, p, mix, scale)


def _in_bwd(du, dq, dk, dv, dg2, dfl, dx1, x, g, wu, wqkv, wg2, wft, tm):
    t, d = x.shape
    tm = min(tm, t)
    aw = ATTN_WIDTH

    def body(du_ref, dq_ref, dk_ref, dv_ref, dg2_ref, dfl_ref, dx1_ref, x_ref, g_ref, wu_ref, wqkv_ref, wg2_ref, wft_ref,
             dx_ref, dg_ref):
        i = pl.program_id(0)
        dh = _dot(du_ref[...], wu_ref[...])
        dh += _dot(dq_ref[...], wqkv_ref[0:aw, :])
        dh += _dot(dk_ref[...], wqkv_ref[aw:2 * aw, :])
        dh += _dot(dv_ref[...], wqkv_ref[2 * aw:3 * aw, :])
        dh += _dot(dg2_ref[...], wg2_ref[...])
        dh += _dot(dfl_ref[...], wft_ref[...])
        dxn, dg = _rms_bwd(x_ref[...], g_ref[...], dh)
        dx_ref[...] = dx1_ref[...] + dxn

        @pl.when(i == 0)
        def _():
            dg_ref[...] = jnp.zeros_like(dg_ref)

        dg_ref[...] += dg

    row = lambda w: pl.BlockSpec((tm, w), lambda i: (i, 0))
    full = lambda a: pl.BlockSpec(a.shape, lambda i: (0, 0))
    return pl.pallas_call(
        body, name="in_bwd",
        out_shape=(jax.ShapeDtypeStruct((t, d), F32), jax.ShapeDtypeStruct((1, d), F32)),
        grid=(t // tm,),
        in_specs=[row(POOL_WIDTH), row(aw), row(aw), row(aw), row(2 * d), row(LANES), row(d), row(d),
                  pl.BlockSpec((1, d), lambda i: (0, 0)), full(wu), full(wqkv), full(wg2), full(wft)],
        out_specs=(row(d), pl.BlockSpec((1, d), lambda i: (0, 0))),
        compiler_params=_cparams("arbitrary"),
    )(du, dq, dk, dv, dg2, dfl, dx1, x, g, wu, wqkv, wg2, wft)


def _position():
    return lax.axis_index("x"), lax.axis_index("y"), lax.axis_index("c")


def _remote(src, dst, send_sem, recv_sem, device):
    return pltpu.make_async_remote_copy(src_ref=src, dst_ref=dst, send_sem=send_sem, recv_sem=recv_sem,
                                        device_id=device, device_id_type=MESH)


HBM = pl.BlockSpec(memory_space=pltpu.HBM)
SEM = pl.BlockSpec(memory_space=pltpu.SEMAPHORE)
DATAFLOW = pltpu.SideEffectType.DATAFLOW_SIDE_EFFECTING


def _copies_start(name, arrays, plan, m, dep=None):
    n = len(arrays)
    arrays = [pltpu.with_memory_space_constraint(a, pltpu.HBM) for a in arrays]

    def body(*refs):
        ins, send_sem, recv_sem, token = refs[:n], refs[n], refs[n + 1], refs[2 * n + 2]
        for i, (src, dst, device, _) in enumerate(plan(ins, *_position())):
            _remote(src, dst, send_sem.at[i], recv_sem.at[i], device).start()
        token[...] = jnp.zeros_like(token)

    dep_specs, dep_ops = _dep_args(dep)
    outs = pl.pallas_call(
        _after(body, n, dep), name=name,
        out_shape=(pltpu.SemaphoreType.DMA((m,)), pltpu.SemaphoreType.DMA((m,)),
                   *[pltpu.HBM(a.shape, a.dtype) for a in arrays], jax.ShapeDtypeStruct((8, LANES), F32)),
        in_specs=[HBM] * n + dep_specs, out_specs=(SEM, SEM, *[HBM] * n, pl.BlockSpec(memory_space=pltpu.VMEM)),
        input_output_aliases={i: i + 2 for i in range(n)},
        compiler_params=pltpu.CompilerParams(has_side_effects=DATAFLOW),
    )(*arrays, *dep_ops)
    return (outs[0], outs[1]), list(outs[2:2 + n]), outs[2 + n]


def _copies_wait(name, sems, arrays, plan, after):
    n = len(arrays)
    afters = list(after) if isinstance(after, (list, tuple)) else [after]

    def body(*refs):
        ins, send_sem, recv_sem = refs[:n], refs[n], refs[n + 1]
        for i, (src, dst, device, landing) in enumerate(plan(ins, *_position())):
            _remote(src, dst, send_sem.at[i], recv_sem.at[i], device).wait_send()
            _remote(landing, landing, send_sem.at[i], recv_sem.at[i], device).wait_recv()

    outs = pl.pallas_call(
        body, name=name,
        out_shape=tuple(pltpu.HBM(a.shape, a.dtype) for a in arrays),
        in_specs=[HBM] * n + [SEM, SEM] + [ANY] * len(afters), out_specs=tuple([HBM] * n),
        input_output_aliases={i: i for i in range(n)},
        compiler_params=pltpu.CompilerParams(has_side_effects=DATAFLOW),
    )(*arrays, sems[0], sems[1], *afters)
    return list(outs)


def _tie(x, dep):
    for token in _dep_list(dep):
        x = x + token[0, 0]
    return x


def _other_chips(x, y):
    return [(1 - x, y), (x, 1 - y), (1 - x, 1 - y)]


def _gather_begin(tag, shards, token, column_halves=False):
    n = len(shards)
    lands = [lax.empty((N_CHIPS,) + s.shape, s.dtype) for s in shards]
    if column_halves:
        cols = lambda ref, h: pl.ds(pl.multiple_of(h * (ref.shape[-1] // 2), LANES), ref.shape[-1] // 2)
        mine = lambda ref, h: ref.at[:, cols(ref, h)]
        landed = lambda ref, chip, h: ref.at[chip, :, cols(ref, h)]
    else:
        mine = lambda ref, h: ref.at[h]
        landed = lambda ref, chip, h: ref.at[chip, h]

    def plan(refs, x, y, c):
        return [(mine(refs[k], c), landed(refs[n + k], 2 * x + y, c), (ox, oy, c), landed(refs[n + k], 2 * ox + oy, c))
                for k in range(n) for ox, oy in _other_chips(x, y)]

    sems, thru, token = _copies_start(f"gather_{tag}_ici_start", list(shards) + lands, plan, 3 * n, dep=token)
    return dict(tag=tag, n=n, plan=plan, sems=sems, arrays=thru, token=token, landed=landed)


def _gather_forward(st, after):
    n, tag, landed = st["n"], st["tag"], st["landed"]
    thru = _copies_wait(f"gather_{tag}_ici_wait", st["sems"], st["arrays"], st["plan"], after)

    def plan(refs, x, y, c):
        return [(landed(refs[k], 2 * ox + oy, c), landed(refs[k], 2 * ox + oy, c), (x, y, 1 - c),
                 landed(refs[k], 2 * ox + oy, 1 - c))
                for k in range(n) for ox, oy in _other_chips(x, y)]

    sems, lands, token = _copies_start(f"gather_{tag}_fwd_start", thru[n:], plan, 3 * n)
    return dict(tag=tag, n=n, plan=plan, sems=sems, arrays=lands, token=token, shards=thru[:n])


def _gather_end(st, after, merge=True):
    lands = _copies_wait(f"gather_{st['tag']}_fwd_wait", st["sems"], st["arrays"], st["plan"], after)
    if not merge:
        return lands, st["shards"]
    me = 2 * lax.axis_index("x") + lax.axis_index("y")
    return [lax.dynamic_update_index_in_dim(g, s, me, 0) for g, s in zip(lands, st["shards"])]


def _add_keep_give(name, pos, a, a_keep, a_give, b, b_keep, b_give, steps):
    r, c = b.shape[-2:]

    def spec(arr, fn):
        lead = arr.ndim - 2

        def index(i, p):
            idx = tuple(fn(i, p))
            return idx if len(idx) == arr.ndim else idx + (0, 0)

        return pl.BlockSpec((None,) * lead + (r, c), index)

    out_spec = pl.BlockSpec((None, r, c), lambda i, p: (i, 0, 0))

    def body(p_ref, ak_ref, bk_ref, ag_ref, bg_ref, keep_ref, give_ref):
        keep_ref[...] = ak_ref[...] + bk_ref[...].astype(F32)
        give_ref[...] = (ag_ref[...] + bg_ref[...].astype(F32)).astype(BF16)

    return pl.pallas_call(
        body, name=name,
        out_shape=(jax.ShapeDtypeStruct((steps, r, c), F32), jax.ShapeDtypeStruct((steps, r, c), BF16)),
        grid_spec=pltpu.PrefetchScalarGridSpec(
            num_scalar_prefetch=1, grid=(steps,),
            in_specs=[spec(a, a_keep), spec(b, b_keep), spec(a, a_give), spec(b, b_give)],
            out_specs=(out_spec, out_spec)),
        compiler_params=_cparams("parallel"),
    )(pos, a, b, a, b)


def _add_last(name, a, b):
    _, r, c = a.shape
    blk = pl.BlockSpec((None, r, c), lambda i: (0, 0, 0))

    def body(a_ref, b_ref, o_ref):
        o_ref[...] = a_ref[...] + b_ref[...].astype(F32)

    return pl.pallas_call(
        body, name=name, out_shape=jax.ShapeDtypeStruct((r, c), F32), grid=(1,), in_specs=[blk, blk],
        out_specs=pl.BlockSpec((r, c), lambda i: (0, 0)), compiler_params=_cparams("arbitrary"),
    )(a, b)


def _exchange_begin(tag, stage, gives, lands, peer_fn, extra):
    n = len(gives)

    def plan(refs, x, y, c):
        return [(refs[k], refs[n + k], peer_fn(x, y, c), refs[n + k]) for k in range(n)]

    sems, thru, token = _copies_start(f"rs{tag}_{stage}_start", gives + lands, plan, n)
    return dict(extra, tag=tag, n=n, stage=stage, plan=plan, sems=sems, arrays=thru, token=token)


def _reduce_begin(tag, grads, column_halves=False):
    n = len(grads)
    if column_halves:
        half = lambda ref, j, h: ref.at[j, :, pl.ds(pl.multiple_of(h * (ref.shape[2] // 2), LANES), ref.shape[2] // 2)]
        lands = [lax.empty((N_CHIPS, g.shape[1], g.shape[2] // 2), F32) for g in grads]
    else:
        half = lambda ref, j, h: ref.at[j, h]
        lands = [lax.empty((N_CHIPS,) + g.shape[2:], F32) for g in grads]

    def plan(refs, x, y, c):
        return [(half(refs[k], j, 1 - c), refs[n + k].at[j], (x, y, 1 - c), refs[n + k].at[j])
                for k in range(n) for j in range(N_CHIPS)]

    sems, thru, token = _copies_start(f"rs{tag}_c_start", list(grads) + lands, plan, N_CHIPS * n)
    return dict(tag=tag, n=n, stage="c", plan=plan, sems=sems, arrays=thru, token=token, column_halves=column_halves)


def _reduce_advance(st, after):
    tag, n, stage = st["tag"], st["n"], st["stage"]
    thru = _copies_wait(f"rs{tag}_{stage}_wait", st["sems"], st["arrays"], st["plan"], after)
    first, recv = thru[:n], thru[n:]
    x, y, c = _position()
    if stage == "c":
        pos = jnp.stack([c, x]).astype(jnp.int32)
        if st["column_halves"]:
            mine = lambda chip: (lambda i, p: (chip(p) + i, 0, p[0]))
        else:
            mine = lambda chip: (lambda i, p: (chip(p) + i, p[0]))
        sums = [_add_keep_give(
            f"rs{tag}_c_add{k}", pos,
            first[k], mine(lambda p: 2 * p[1]), mine(lambda p: 2 * (1 - p[1])),
            recv[k], lambda i, p: (2 * p[1] + i,), lambda i, p: (2 * (1 - p[1]) + i,), 2) for k in range(n)]
        lands = [lax.empty(s[1].shape, BF16) for s in sums]
        return _exchange_begin(tag, "x", [s[1] for s in sums], lands, lambda x, y, c: (1 - x, y, c),
                               dict(keep=[s[0] for s in sums]))
    if stage == "x":
        pos = jnp.stack([y]).astype(jnp.int32)
        sums = [_add_keep_give(
            f"rs{tag}_x_add{k}", pos,
            st["keep"][k], lambda i, p: (p[0],), lambda i, p: (1 - p[0],),
            recv[k], lambda i, p: (p[0],), lambda i, p: (1 - p[0],), 1) for k in range(n)]
        lands = [lax.empty(s[1].shape, BF16) for s in sums]
        return _exchange_begin(tag, "y", [s[1] for s in sums], lands, lambda x, y, c: (x, 1 - y, c),
                               dict(keep=[s[0] for s in sums]))
    if stage == "y":
        mine = [_add_last(f"rs{tag}_y_add{k}", st["keep"][k], recv[k]) for k in range(n)]
        lands = [lax.empty(m.shape, F32) for m in mine]
        return _exchange_begin(tag, "swap", mine, lands, lambda x, y, c: (x, y, 1 - c), {})
    return dict(done=list(zip(first, recv)), token=None)


def _all_reduce_small(v):
    r = v.shape[0]

    def body(v_ref, out_ref, buf, send_sems, recv_sems, local_sem):
        x, y, c = _position()
        me, sibling = (x, y, c), (x, y, 1 - c)
        chips = [(1 - x, y), (x, 1 - y), (1 - x, 1 - y)]

        def rows(px, py, pc):
            return buf.at[pl.ds((4 * px + 2 * py + pc) * r, r), :]

        def copy(k, block, to, src=None):
            return _remote(rows(*block) if src is None else src, rows(*block), send_sems.at[k], recv_sems.at[k], to)

        mine = pltpu.make_async_copy(v_ref, rows(*me), local_sem)
        mine.start()
        first = [copy(0, me, sibling, src=v_ref)]
        first += [copy(1 + j, me, (*chip, c), src=v_ref) for j, chip in enumerate(chips)]
        for cp in first:
            cp.start()
        passed = [copy(4 + j, (*chip, c), sibling) for j, chip in enumerate(chips)]
        for j, chip in enumerate(chips):
            copy(1 + j, (*chip, c), me).wait_recv()
            passed[j].start()
        copy(0, sibling, me).wait_recv()
        for j, chip in enumerate(chips):
            copy(4 + j, (*chip, 1 - c), me).wait_recv()
        for cp in first + passed:
            cp.wait_send()
        mine.wait()
        acc = buf[0:r, :]
        for dev in range(1, N_DEV):
            acc = acc + buf[dev * r:(dev + 1) * r, :]
        out_ref[...] = acc

    return pl.pallas_call(
        body, name="all_reduce_small",
        out_shape=jax.ShapeDtypeStruct(v.shape, F32),
        in_specs=[pl.BlockSpec(memory_space=pltpu.VMEM)],
        out_specs=pl.BlockSpec(memory_space=pltpu.VMEM),
        scratch_shapes=[pltpu.VMEM((N_DEV * r, LANES), F32), pltpu.SemaphoreType.DMA((7,)),
                        pltpu.SemaphoreType.DMA((7,)), pltpu.SemaphoreType.DMA],
        compiler_params=pltpu.CompilerParams(has_side_effects=True, vmem_limit_bytes=VMEM_LIMIT_V7X),
    )(v)


def _adamw_update(w, gg, m, v):
    mn = ADAM_B1 * m + (1.0 - ADAM_B1) * gg
    vn = ADAM_B2 * v + (1.0 - ADAM_B2) * (gg * gg)
    m_hat = mn / (1.0 - ADAM_B1 ** ADAM_STEP)
    v_hat = vn / (1.0 - ADAM_B2 ** ADAM_STEP)
    return -ADAM_LR * (m_hat / (jnp.sqrt(v_hat) + ADAM_EPS) + ADAM_WD * w), mn, vn


def _adamw(name, w, g, m, v):
    def body(w_ref, g_ref, m_ref, v_ref, d_ref, mo_ref, vo_ref):
        d_ref[...], mo_ref[...], vo_ref[...] = _adamw_update(w_ref[...], g_ref[...], m_ref[...], v_ref[...])

    blk = pl.BlockSpec(w.shape, lambda i: (0, 0))
    return pl.pallas_call(
        body, name=name, out_shape=(jax.ShapeDtypeStruct(w.shape, F32),) * 3, grid=(1,),
        in_specs=[blk] * 4, out_specs=(blk,) * 3, compiler_params=_cparams("arbitrary"),
    )(w, g, m, v)


def _rows_to_bf16(name, w):
    r, _, c = w.shape

    def body(w_ref, o_ref):
        o_ref[...] = w_ref[:, 0, :].astype(BF16)

    return pl.pallas_call(
        body, name=name, out_shape=jax.ShapeDtypeStruct((r, c), BF16), grid=(1,),
        in_specs=[pl.BlockSpec((r, 1, c), lambda i: (0, 0, 0))], out_specs=pl.BlockSpec((r, c), lambda i: (0, 0)),
        compiler_params=_cparams("arbitrary"),
    )(w)


def _adamw_rows(name, pos_c, w, g_mine, g_other, m, v):
    r, _, c = w.shape
    ch = c // 2

    def body(p_ref, w_ref, gm_ref, go_ref, m_ref, v_ref, g_ref, d_ref, mo_ref, vo_ref):
        gg = jnp.where(pl.program_id(0) == p_ref[0], gm_ref[...], go_ref[...])
        dl, mn, vn = _adamw_update(w_ref[:, 0, :], gg, m_ref[:, 0, :], v_ref[:, 0, :])
        g_ref[:, 0, :] = gg
        d_ref[:, 0, :] = dl
        mo_ref[:, 0, :] = mn
        vo_ref[:, 0, :] = vn

    rows = pl.BlockSpec((r, 1, ch), lambda h, p: (0, 0, h))
    half = pl.BlockSpec((r, ch), lambda h, p: (0, 0))
    return pl.pallas_call(
        body, name=name, out_shape=(jax.ShapeDtypeStruct(w.shape, F32),) * 4,
        grid_spec=pltpu.PrefetchScalarGridSpec(
            num_scalar_prefetch=1, grid=(2,), in_specs=[rows, half, half, rows, rows], out_specs=(rows,) * 4),
        compiler_params=_cparams("parallel"),
    )(pos_c, w, g_mine, g_other, m, v)


def _adamw_halves(name, pos_c, w, g_mine, g_other, m, v, tr, dep=None):
    r, c = w.shape
    rh = r // 2
    tr = tr if rh % tr == 0 else rh
    nt = rh // tr

    def body(p_ref, w_ref, gm_ref, go_ref, m_ref, v_ref, g_ref, d_ref, mo_ref, vo_ref):
        gg = jnp.where(pl.program_id(0) == p_ref[0], gm_ref[...], go_ref[...])
        g_ref[...] = gg
        d_ref[...], mo_ref[...], vo_ref[...] = _adamw_update(w_ref[...], gg, m_ref[...], v_ref[...])

    full = pl.BlockSpec((tr, c), lambda h, i, p: (h * nt + i, 0))
    half = pl.BlockSpec((tr, c), lambda h, i, p: (i, 0))
    dep_specs, dep_ops = _dep_args(dep)
    return pl.pallas_call(
        _after(body, 6, dep), name=name, out_shape=(jax.ShapeDtypeStruct((r, c), F32),) * 4,
        grid_spec=pltpu.PrefetchScalarGridSpec(
            num_scalar_prefetch=1, grid=(2, nt),
            in_specs=[full, half, half, full, full] + dep_specs, out_specs=(full,) * 4),
        compiler_params=_cparams("parallel", "parallel"),
    )(pos_c, w, g_mine, g_other, m, v, *dep_ops)


def _col_sharded_to_comm(g):
    k, n = g.shape
    return g.reshape(2, k // 2, N_CHIPS, n // N_CHIPS).transpose(2, 0, 1, 3)


def _row_sharded_to_comm(g):
    r, c = g.shape
    return g.reshape(N_CHIPS, 2, r // (2 * N_CHIPS), c)


def _col_sharded_full(g):
    _, _, rh, c = g.shape
    return g.reshape(N_CHIPS, 2 * rh, c).transpose(1, 0, 2).reshape(2 * rh, N_CHIPS * c)


def _row_sharded_full(g):
    _, _, rh, c = g.shape
    return g.reshape(N_CHIPS * 2 * rh, c)


def _chip_rows(w3, start, stop, own=None, me=None):
    r = w3.shape[1]
    parts = []
    for chip in range(N_CHIPS):
        lo, hi = max(start - chip * r, 0), min(stop - chip * r, r)
        if lo < hi:
            part = w3[chip, lo:hi]
            parts.append(part if own is None else jnp.where(me == chip, own[lo:hi], part))
    return parts[0] if len(parts) == 1 else jnp.concatenate(parts, axis=0)


def _pack_small(g1, bfv, mix, scale, g2n, gf, extra=None):
    row8 = jnp.pad(bfv.reshape(1, N_HEADS), ((0, 0), (0, LANES - N_HEADS)))
    if extra is not None:
        row8 = row8 + jnp.pad(extra[:, :1], ((0, 0), (N_HEADS, LANES - N_HEADS - 1)))
    return jnp.concatenate([
        g1.reshape(8, LANES), jnp.pad(row8, ((0, 7), (0, 0))), mix.reshape(512, LANES),
        jnp.pad(scale.reshape(4, LANES), ((0, 4), (0, 0))), g2n.reshape(8, LANES), gf.reshape(8, LANES)], axis=0)


def _unpack_small(s, like):
    g1, bfv, mix, scale, g2n, gf = like
    return (s[0:8].reshape(g1.shape), s[8, :N_HEADS].reshape(bfv.shape), s[16:528].reshape(mix.shape),
            s[528:532].reshape(scale.shape), s[536:544].reshape(g2n.shape), s[544:552].reshape(gf.shape))


class _MeshLinks:
    def __init__(self, shards_in, shards_rest):
        self.gin = _gather_begin("in", shards_in, None, column_halves=True)
        self.grest = _gather_begin("rest", shards_rest, self.gin["token"])
        self.tokens = {"gather": self.grest["token"]}
        self.groups = {}

    @property
    def token(self):
        return list(self.tokens.values())

    def tie(self, x):
        return _tie(x, self.token)

    def weights_in(self, after):
        st = _gather_forward(self.gin, after)
        (g,), (own,) = _gather_end(st, st["token"], merge=False)
        return g, own, 2 * lax.axis_index("x") + lax.axis_index("y")

    def rest_forward(self, after):
        self.grest = _gather_forward(self.grest, after)
        self.tokens["gather"] = self.grest["token"]

    def weights_rest(self, after):
        g = _gather_end(self.grest, after)
        del self.tokens["gather"]
        return [_col_sharded_full(g[0]), _col_sharded_full(g[1])] + [_row_sharded_full(a) for a in g[2:]]

    def reduce_begin(self, tag, grads, column_halves=False):
        self.groups[tag] = _reduce_begin(tag, grads, column_halves)
        self.tokens[tag] = self.groups[tag]["token"]

    def advance(self, after):
        for tag, st in self.groups.items():
            if "done" not in st:
                self.groups[tag] = _reduce_advance(st, after)
                if self.groups[tag]["token"] is None:
                    del self.tokens[tag]
                else:
                    self.tokens[tag] = self.groups[tag]["token"]

    def reduced(self, tag):
        return self.groups[tag]["done"]


class _NoLinks:
    token = None

    def __init__(self, w_in, rest):
        self.w_in, self.rest, self.grads = w_in, rest, {}

    def tie(self, x):
        return x

    def weights_in(self, after):
        return self.w_in, None, None

    def rest_forward(self, after):
        pass

    def weights_rest(self, after):
        return self.rest

    def reduce_begin(self, tag, grads, column_halves=False):
        self.grads[tag] = grads

    def advance(self, after):
        pass


def _local_step(links, x, target, seq, norm1_g, b_forget, pool_mix, pool_scale, norm2_g, norm_f_g):
    t, d = x.shape
    tq = min(256, seq)
    aw = ATTN_WIDTH
    o_q, o_f, o_g = POOL_WIDTH, POOL_WIDTH + 3 * aw, POOL_WIDTH + 3 * aw + N_HEADS
    bf = jnp.pad(b_forget, ((0, 0), (0, LANES - N_HEADS)))
    mixb = pool_mix.astype(BF16)

    h = _norm_fwd("norm1_fwd", x, links.tie(norm1_g), 512)
    w_in, own, me = links.weights_in(h)
    wu = _chip_rows(w_in, 0, o_q, own, me)
    wqkv = _chip_rows(w_in, o_q, o_f, own, me)
    wft = jnp.pad(_chip_rows(w_in, o_f, o_g, own, me), ((0, LANES - N_HEADS), (0, 0)))
    wg2 = _chip_rows(w_in, o_g, N_CHIPS * w_in.shape[1], own, me)
    wf = wft.T
    u = _matmul("mm_u", h, wu, "nt", F32, 1024, 512, d)
    g2 = _matmul("mm_gates", h, wg2, "nt", BF16, 1024, 512, d)
    fl, fcum = _forget_fwd(h, wf, bf, seq)
    qa, ka, v = _attn_prep(h, _head_blocks(wqkv[:aw]), _head_blocks(wqkv[aw:2 * aw]), wqkv[2 * aw:], fcum, 1024)
    p, ps = _pool_fwd(u, mixb, pool_scale, seq)
    links.rest_forward([ps, qa, g2])
    o, lse = _attn_fwd(qa, ka, v, seq, tq, dep=links.token)
    w_pool_out, w_attn_out, w_out, w_ffn_gate, w_ffn_up, w_ffn_down = links.weights_rest(o)
    merged, x1 = _merge_fwd(x, ps, o, g2, w_pool_out, w_attn_out, w_out, 512)
    h2, gt, up, act, x2 = _ffn_fwd(x1, norm2_g, w_ffn_gate, w_ffn_up, w_ffn_down, 1024, 256)
    loss, dx2, d_gf = _final_fwd_bwd(x2, target, norm_f_g, 512)

    dgt, dup, dx1, d_g2n = _ffn_bwd(dx2, x1, norm2_g, gt, up, w_ffn_gate, w_ffn_up, w_ffn_down, 1024, 256)
    d_wd = _matmul("dw_down", act, dx2, "tn", F32, 1408, 1024, 1024)
    d_wg = _matmul("dw_gate", dgt, h2, "tn", F32, 1408, 1024, 1024)
    d_wu = _matmul("dw_up", dup, h2, "tn", F32, 1408, 1024, 1024)
    links.reduce_begin("a", [_row_sharded_to_comm(g) for g in (d_wg, d_wu, d_wd)])
    dpy, day, dg2, dps, da = _merge_bwd(dx1, ps, o, g2, w_pool_out, w_attn_out, w_out, 512, dep=links.token)
    links.advance(dps)
    d_wout = _matmul("dw_out", merged, dx1, "tn", F32, 1024, 1024, 1024)
    d_wpo = _matmul("dw_pool_out", ps, dpy, "tn", F32, 512, 1024, 1024)
    d_wao = _matmul("dw_attn_out", o, day, "tn", F32, 512, 1024, 1024)
    links.reduce_begin("m", [_col_sharded_to_comm(d_wpo), _col_sharded_to_comm(d_wao), _row_sharded_to_comm(d_wout)])
    dq, dk, dv, dfr = _attn_bwd(qa, ka, v, da, lse, seq, tq, dep=links.token)
    links.advance(dq)
    dfc = jnp.pad(dfr.reshape(N_HEADS, t).T, ((0, 0), (0, LANES - N_HEADS)))
    dfl, d_bf = _forget_bwd(dfc, fl, bf, seq)
    du, d_mix, d_scale = _pool_bwd(dps, p, mixb, links.tie(pool_scale), seq)
    d_wu_in = _matmul("dw_in_u", du, h, "tn", F32, 512, 1024, 1024)
    d_wq = _matmul("dw_in_q", dq, h, "tn", F32, 512, 1024, 1024)
    d_wk = _matmul("dw_in_k", dk, h, "tn", F32, 512, 1024, 1024)
    d_wv = _matmul("dw_in_v", dv, h, "tn", F32, 512, 1024, 1024)
    links.advance([d_wu_in, d_wq, d_wk, d_wv])
    d_wf = _matmul("dw_in_f", dfl, h, "tn", F32, LANES, 1024, 512)
    d_wg2 = _matmul("dw_in_gates", dg2, h, "tn", F32, 1024, 1024, 1024, dep=links.token)
    d_win = jnp.concatenate([d_wu_in, d_wq, d_wk, d_wv, d_wf[:N_HEADS], d_wg2], axis=0)
    comm_b = [d_win.reshape(N_CHIPS, d_win.shape[0] // N_CHIPS, d)]
    links.advance(comm_b)
    links.reduce_begin("b", comm_b, column_halves=True)
    dx, d_g1 = _in_bwd(du, dq, dk, dv, dg2, dfl, dx1, x, links.tie(norm1_g), wu, wqkv, wg2, wft, 512)
    links.advance(dx)
    small = (d_g1, d_bf[:, :N_HEADS], d_mix, d_scale, d_g2n, d_gf)
    return loss, dx, small


def kernel(x, norm1_g, w_in, b_forget, pool_mix, pool_scale, w_pool_out, w_attn_out, w_out, norm2_g, w_ffn_gate, w_ffn_up, w_ffn_down, norm_f_g, loss_target, m_norm1_g, m_w_in, m_b_forget, m_pool_mix, m_pool_scale, m_w_pool_out, m_w_attn_out, m_w_out, m_norm2_g, m_w_ffn_gate, m_w_ffn_up, m_w_ffn_down, m_norm_f_g, v_norm1_g, v_w_in, v_b_forget, v_pool_mix, v_pool_scale, v_w_pool_out, v_w_attn_out, v_w_out, v_norm2_g, v_w_ffn_gate, v_w_ffn_up, v_w_ffn_down, v_norm_f_g):
    nb, seq, d = x.shape
    group_a = ((w_ffn_gate, m_w_ffn_gate, v_w_ffn_gate, True, 9), (w_ffn_up, m_w_ffn_up, v_w_ffn_up, True, 10),
               (w_ffn_down, m_w_ffn_down, v_w_ffn_down, False, 11))
    group_m = ((w_pool_out, m_w_pool_out, v_w_pool_out, False, 5), (w_attn_out, m_w_attn_out, v_w_attn_out, False, 6),
               (w_out, m_w_out, v_w_out, False, 7))
    group_b = ((w_in, m_w_in, v_w_in, False, 1),)
    small_w = (norm1_g, b_forget, pool_mix, pool_scale, norm2_g, norm_f_g)
    small_m = (m_norm1_g, m_b_forget, m_pool_mix, m_pool_scale, m_norm2_g, m_norm_f_g)
    small_v = (v_norm1_g, v_b_forget, v_pool_mix, v_pool_scale, v_norm2_g, v_norm_f_g)
    small_pos = (0, 2, 3, 4, 8, 12)
    view = lambda a, tr: a[0].T if tr else a[0]
    unview = lambda a, tr, like: (a.T if tr else a).reshape(like.shape)

    def shard(w, tr):
        lw = view(w, tr).astype(BF16)
        return lw.reshape(2, lw.shape[0] // 2, lw.shape[1])

    cm = lambda a: jnp.transpose(a, (2, 0, 1))
    shard_in = _rows_to_bf16("w_in_to_bf16", cm(w_in))
    links = _MeshLinks([shard_in],
                       [shard(w_pool_out, False), shard(w_attn_out, False), shard(w_out, False),
                        shard(w_ffn_gate, True), shard(w_ffn_up, True), shard(w_ffn_down, False)])
    loss, dx, small_g = _local_step(
        links, x.reshape(nb * seq, d), loss_target.reshape(nb * seq, d), seq,
        norm1_g, b_forget, pool_mix[0], pool_scale, norm2_g, norm_f_g.reshape(1, d))

    grads, deltas, new_m, new_v = [None] * 13, [None] * 13, [None] * 13, [None] * 13
    pos_c = jnp.stack([lax.axis_index("c")]).astype(jnp.int32)

    def update(tag, group, dep):
        last = []
        for k, ((w, m, v, tr, pos), (mine, other)) in enumerate(zip(group, links.reduced(tag))):
            outs = _adamw_halves(f"adamw_{tag}{k}", pos_c, view(w, tr), mine, other, view(m, tr), view(v, tr), 256,
                                 dep=dep)
            grads[pos], deltas[pos], new_m[pos], new_v[pos] = (unview(a, tr, w) for a in outs)
            last.append(outs[1])
        return last

    last = update("a", group_a, links.token) + update("m", group_m, links.token)
    links.advance(last)
    small_sum = _all_reduce_small(links.tie(_pack_small(*small_g, extra=loss)))
    loss_out = small_sum[8, N_HEADS]
    dl, mn, vn = _adamw("adamw_small", _pack_small(*small_w), small_sum * _small_mask(), _pack_small(*small_m),
                        _pack_small(*small_v))
    for pos, g, a, b, e in zip(small_pos, _unpack_small(small_sum, small_w), _unpack_small(dl, small_w),
                               _unpack_small(mn, small_w), _unpack_small(vn, small_w)):
        grads[pos], deltas[pos], new_m[pos], new_v[pos] = g, a, b, e
    links.advance(dl)
    links.advance(links.token)
    (mine, other), = links.reduced("b")
    outs = _adamw_rows("adamw_b0", pos_c, cm(w_in), mine, other, cm(m_w_in), cm(v_w_in))
    grads[1], deltas[1], new_m[1], new_v[1] = (jnp.transpose(a, (1, 2, 0)) for a in outs)

    return (loss_out, dx.reshape(nb, seq, d), *grads, *deltas, *new_m, *new_v)


def _small_mask():
    rows = lax.broadcasted_iota(jnp.int32, (552, LANES), 0)
    lanes = lax.broadcasted_iota(jnp.int32, (552, LANES), 1)
    return jnp.where(jnp.logical_and(rows == 8, lanes == N_HEADS), 0.0, 1.0).astype(F32)
```

```python
import functools

import jax
import jax.numpy as jnp
from jax import lax
from jax.experimental import pallas as pl
from jax.experimental.pallas import tpu as pltpu

F32 = jnp.float32
BF16 = jnp.bfloat16

D_MODEL = 1024
POOL_WINDOWS = (2, 4, 8, 16)
POOL_GROUPS = 4
POOL_GROUP_DIM = 128
POOL_WIDTH = 512
HEAD_DIM = 64
N_HEADS = 8
ATTN_WIDTH = 512
D_FF = 2816
RMS_EPS = 1e-6
ATTN_SCALE = HEAD_DIM ** -0.5
NEG_BIG = -1e30

ADAM_LR = 0.001
ADAM_B1 = 0.9
ADAM_B2 = 0.999
ADAM_EPS = 1e-08
ADAM_WD = 0.01
ADAM_STEP = 10

LANES = 128
N_CHIPS = 4
N_DEV = 8
VMEM_LIMIT_V7X = 52 * 1024 * 1024
MESH = pl.DeviceIdType.MESH
ANY = pl.BlockSpec(memory_space=pl.ANY)


def _cparams(*sem):
    return pltpu.CompilerParams(dimension_semantics=sem if sem else None, vmem_limit_bytes=VMEM_LIMIT_V7X)


def _dep_list(dep):
    return [] if dep is None else (list(dep) if isinstance(dep, (list, tuple)) else [dep])


def _after(body, n_in, dep):
    k = len(_dep_list(dep))
    if k == 0:
        return body

    def wrapped(*refs):
        body(*refs[:n_in], *refs[n_in + k:])

    return wrapped


def _dep_args(dep):
    deps = _dep_list(dep)
    return [ANY] * len(deps), deps


def _dot(a, b):
    return lax.dot_general(a, b, (((1,), (0,)), ((), ())), preferred_element_type=F32)


def _dot_nt(a, b):
    return lax.dot_general(a, b, (((1,), (1,)), ((), ())), preferred_element_type=F32)


def _dot_tn(a, b):
    return lax.dot_general(a, b, (((0,), (0,)), ((), ())), preferred_element_type=F32)


def _sigmoid(x):
    return jax.nn.sigmoid(x)


def _rms_fwd(x, g):
    r = lax.rsqrt(jnp.mean(x * x, axis=-1, keepdims=True) + RMS_EPS)
    return (x * r) * g


def _rms_bwd(x, g, dy):
    r = lax.rsqrt(jnp.mean(x * x, axis=-1, keepdims=True) + RMS_EPS)
    xh = x * r
    dg = jnp.sum(dy * xh, axis=0, keepdims=True)
    dxh = dy * g
    dx = r * (dxh - xh * jnp.mean(dxh * xh, axis=-1, keepdims=True))
    return dx, dg


def _matmul(name, a, b, mode, out_dtype, tm, tn, tk, dep=None):
    if mode == "nn":
        (m, k), (_, n) = a.shape, b.shape
    elif mode == "nt":
        (m, k), (n, _) = a.shape, b.shape
    else:
        (k, m), (_, n) = a.shape, b.shape
    tm, tn, tk = min(tm, m), min(tn, n), min(tk, k)
    assert m % tm == 0 and n % tn == 0 and k % tk == 0, (name, m, n, k, tm, tn, tk)
    nk = k // tk
    if mode == "tn":
        a_spec = pl.BlockSpec((tk, tm), lambda i, j, kk: (kk, i))
    else:
        a_spec = pl.BlockSpec((tm, tk), lambda i, j, kk: (i, kk))
    if mode == "nt":
        b_spec = pl.BlockSpec((tn, tk), lambda i, j, kk: (j, kk))
    else:
        b_spec = pl.BlockSpec((tk, tn), lambda i, j, kk: (kk, j))
    dot = {"nn": _dot, "nt": _dot_nt, "tn": _dot_tn}[mode]
    use_scratch = nk > 1 and out_dtype != F32

    def body(a_ref, b_ref, o_ref, *scratch):
        prod = dot(a_ref[...].astype(BF16), b_ref[...].astype(BF16))
        if nk == 1:
            o_ref[...] = prod.astype(out_dtype)
            return
        acc = scratch[0] if use_scratch else o_ref
        kk = pl.program_id(2)

        @pl.when(kk == 0)
        def _():
            acc[...] = prod

        @pl.when(kk > 0)
        def _():
            acc[...] += prod

        if use_scratch:
            @pl.when(kk == nk - 1)
            def _():
                o_ref[...] = acc[...].astype(out_dtype)

    dep_specs, dep_ops = _dep_args(dep)
    return pl.pallas_call(
        _after(body, 2, dep),
        name=name,
        out_shape=jax.ShapeDtypeStruct((m, n), out_dtype),
        grid=(m // tm, n // tn, nk),
        in_specs=[a_spec, b_spec] + dep_specs,
        out_specs=pl.BlockSpec((tm, tn), lambda i, j, kk: (i, j)),
        scratch_shapes=[pltpu.VMEM((tm, tn), F32)] if use_scratch else [],
        compiler_params=_cparams("parallel", "parallel", "arbitrary"),
    )(a, b, *dep_ops)


def _norm_fwd(name, x, g, tm):
    t, d = x.shape
    tm = min(tm, t)

    def body(x_ref, g_ref, h_ref):
        h_ref[...] = _rms_fwd(x_ref[...], g_ref[...]).astype(BF16)

    return pl.pallas_call(
        body, name=name, out_shape=jax.ShapeDtypeStruct((t, d), BF16), grid=(t // tm,),
        in_specs=[pl.BlockSpec((tm, d), lambda i: (i, 0)), pl.BlockSpec((1, d), lambda i: (0, 0))],
        out_specs=pl.BlockSpec((tm, d), lambda i: (i, 0)),
        compiler_params=_cparams("parallel"),
    )(x, g)


def _split3(x):
    hi = x.astype(BF16)
    r1 = x - hi.astype(F32)
    mid = r1.astype(BF16)
    lo = (r1 - mid.astype(F32)).astype(BF16)
    return hi, mid, lo


def _tri_dot(tri, x):
    hi, mid, lo = _split3(x)
    return _dot(tri, hi) + _dot(tri, mid) + _dot(tri, lo)


def _forget_fwd(h, wf, bf, seq):
    t, d = h.shape
    cb = min(256, seq)

    def body(h_ref, wf_ref, bf_ref, fl_ref, fc_ref):
        fl = _dot(h_ref[...], wf_ref[...])
        fl_ref[...] = fl
        xx = fl + bf_ref[...]
        lf = jnp.minimum(xx, 0.0) - jnp.log(1.0 + jnp.exp(-jnp.abs(xx)))
        ri = lax.broadcasted_iota(jnp.int32, (cb, cb), 0)
        ci = lax.broadcasted_iota(jnp.int32, (cb, cb), 1)
        tri = (ri >= ci).astype(BF16)
        carry = jnp.zeros((1, LANES), F32)
        for blk in range(seq // cb):
            cs = _tri_dot(tri, lf[blk * cb:(blk + 1) * cb]) + carry
            fc_ref[blk * cb:(blk + 1) * cb, :] = cs
            carry = cs[cb - 1:cb, :]

    return pl.pallas_call(
        body, name="forget_fwd",
        out_shape=(jax.ShapeDtypeStruct((t, LANES), F32), jax.ShapeDtypeStruct((t, LANES), F32)),
        grid=(t // seq,),
        in_specs=[pl.BlockSpec((seq, d), lambda b: (b, 0)), pl.BlockSpec((d, LANES), lambda b: (0, 0)),
                  pl.BlockSpec((1, LANES), lambda b: (0, 0))],
        out_specs=(pl.BlockSpec((seq, LANES), lambda b: (b, 0)), pl.BlockSpec((seq, LANES), lambda b: (b, 0))),
        compiler_params=_cparams("parallel"),
    )(h, wf, bf)


def _pool_fwd(u, mix, scale, seq):
    t = u.shape[0]

    def body(u_ref, mix_ref, sc_ref, p_ref, ps_ref):
        tpos = lax.broadcasted_iota(jnp.int32, (seq, POOL_GROUP_DIM), 0)
        for g in range(POOL_GROUPS):
            sl = slice(g * POOL_GROUP_DIM, (g + 1) * POOL_GROUP_DIM)
            ug = u_ref[:, sl]
            s = ug
            for lvl in range(g + 1):
                d = 2 ** lvl
                s = s + jnp.where(tpos >= d, pltpu.roll(s, d, 0), 0.0)
            cnt = jnp.minimum(tpos + 1, POOL_WINDOWS[g]).astype(F32)
            pb = (s / cnt - ug).astype(BF16)
            p_ref[:, sl] = pb
            ps_ref[:, sl] = (_dot(pb, mix_ref[g]) * sc_ref[:, sl]).astype(BF16)

    return pl.pallas_call(
        body, name="pool_fwd",
        out_shape=(jax.ShapeDtypeStruct((t, POOL_WIDTH), BF16), jax.ShapeDtypeStruct((t, POOL_WIDTH), BF16)),
        grid=(t // seq,),
        in_specs=[pl.BlockSpec((seq, POOL_WIDTH), lambda b: (b, 0)),
                  pl.BlockSpec((POOL_GROUPS, POOL_GROUP_DIM, POOL_GROUP_DIM), lambda b: (0, 0, 0)),
                  pl.BlockSpec((1, POOL_WIDTH), lambda b: (0, 0))],
        out_specs=(pl.BlockSpec((seq, POOL_WIDTH), lambda b: (b, 0)), pl.BlockSpec((seq, POOL_WIDTH), lambda b: (b, 0))),
        compiler_params=_cparams("parallel"),
    )(u, mix, scale)


def _aug_constants():
    w = N_HEADS * LANES
    rows = jnp.arange(3 * LANES)
    piece, head = rows // LANES, rows % LANES
    cols = jnp.arange(w)
    live = (head < N_HEADS)[:, None]
    pq = (live & (cols[None, :] == (head * LANES + HEAD_DIM + piece)[:, None])).astype(BF16)
    pk = -(live & (cols[None, :] == (head * LANES + HEAD_DIM + 3 + piece)[:, None])).astype(BF16)
    lane = cols % LANES
    oq = ((lane >= HEAD_DIM + 3) & (lane < HEAD_DIM + 6)).astype(F32)[None, :]
    ok = ((lane >= HEAD_DIM) & (lane < HEAD_DIM + 3)).astype(F32)[None, :]
    return pq, pk, oq, ok


def _head_blocks(wt):
    d = wt.shape[1]
    return jnp.pad(wt.reshape(N_HEADS, HEAD_DIM, d), ((0, 0), (0, LANES - HEAD_DIM), (0, 0))).reshape(N_HEADS * LANES, d)


def _attn_prep(h, wq, wk, wv, fcum, tm):
    t, d = h.shape
    tm = min(tm, t)
    w = N_HEADS * LANES
    pq, pk, oq, ok = _aug_constants()

    def body(h_ref, wq_ref, wk_ref, wv_ref, f_ref, pq_ref, pk_ref, oq_ref, ok_ref, qa_ref, ka_ref, v_ref):
        hh = h_ref[...]
        fs = jnp.concatenate(_split3(f_ref[...]), axis=1)
        q = _dot_nt(hh, wq_ref[...]).astype(BF16).astype(F32) * ATTN_SCALE
        qa_ref[...] = (q + _dot(fs, pq_ref[...]) + oq_ref[...]).astype(BF16)
        k = _dot_nt(hh, wk_ref[...]).astype(BF16).astype(F32)
        ka_ref[...] = (k + _dot(fs, pk_ref[...]) + ok_ref[...]).astype(BF16)
        v_ref[...] = _dot_nt(hh, wv_ref[...]).astype(BF16)

    row = lambda n: pl.BlockSpec((tm, n), lambda i: (i, 0))
    full = lambda a: pl.BlockSpec(a.shape, lambda i: (0, 0))
    return pl.pallas_call(
        body, name="attn_prep",
        out_shape=(jax.ShapeDtypeStruct((t, w), BF16), jax.ShapeDtypeStruct((t, w), BF16),
                   jax.ShapeDtypeStruct((t, ATTN_WIDTH), BF16)),
        grid=(t // tm,),
        in_specs=[row(d), full(wq), full(wk), full(wv), row(LANES), full(pq), full(pk), full(oq), full(ok)],
        out_specs=(row(w), row(w), row(ATTN_WIDTH)),
        compiler_params=_cparams("parallel"),
    )(h, wq, wk, wv, fcum, pq, pk, oq, ok)


def _fold_lanes(x, op):
    out = x[:, :LANES]
    for g in range(1, x.shape[1] // LANES):
        out = op(out, x[:, g * LANES:(g + 1) * LANES])
    return out


def _causal_sweep(i, tile, carry):
    def pair(jj, c):
        return tile(2 * jj + 1, tile(2 * jj, c, False), False)

    carry = lax.fori_loop(0, i // 2, pair, carry)
    return lax.cond(i % 2 == 1, lambda c: tile(i, tile(i - 1, c, False), True), lambda c: tile(i, c, True), carry)


def _attn_fwd(qa, ka, v, seq, tq, dep=None):
    t = qa.shape[0]
    nq = seq // tq
    hp_n = N_HEADS // 2
    heads = [slice(e * LANES, (e + 1) * LANES) for e in range(2)]

    def body(q_ref, k_ref, v_ref, o_ref, lse_ref, s_buf):
        i = pl.program_id(2)
        diag_ok = lax.broadcasted_iota(jnp.int32, (tq, tq), 0) >= lax.broadcasted_iota(jnp.int32, (tq, tq), 1)
        qs = [q_ref[:, hl] for hl in heads]

        def sweep1(j, mxs, diagonal):
            r0 = pl.multiple_of(j * tq, tq)
            out = []
            for e, hl in enumerate(heads):
                s = _dot_nt(qs[e], k_ref[pl.ds(r0, tq), hl])
                if diagonal:
                    s = jnp.where(diag_ok, s, NEG_BIG)
                s_buf[e, j] = s
                out.append(jnp.maximum(mxs[e], _fold_lanes(s, jnp.maximum)))
            return tuple(out)

        mxs = _causal_sweep(i, sweep1, (jnp.full((tq, LANES), NEG_BIG, F32),) * 2)
        ms = [jnp.max(mx, axis=1, keepdims=True) for mx in mxs]

        def sweep2(j, carry, diagonal):
            r0 = pl.multiple_of(j * tq, tq)
            vv = v_ref[pl.ds(r0, tq), :]
            out = []
            for e in range(2):
                p = jnp.exp(s_buf[e, j] - ms[e])
                out += [carry[2 * e] + _fold_lanes(p, jnp.add), carry[2 * e + 1] + _dot(p.astype(BF16), vv)]
            return tuple(out)

        res = _causal_sweep(i, sweep2, (jnp.zeros((tq, LANES), F32),) * 4)
        outs = []
        for e in range(2):
            l = jnp.sum(res[2 * e], axis=1, keepdims=True)
            outs.append(res[2 * e + 1] / l)
            lse_ref[:, e:e + 1] = ms[e] + jnp.log(l)
        lane = lax.broadcasted_iota(jnp.int32, (tq, LANES), 1)
        o_ref[...] = jnp.where(lane < HEAD_DIM, outs[0], outs[1])

    dep_specs, dep_ops = _dep_args(dep)
    return pl.pallas_call(
        _after(body, 3, dep), name="attn_fwd",
        out_shape=(jax.ShapeDtypeStruct((t, ATTN_WIDTH), F32), jax.ShapeDtypeStruct((hp_n, t, 2), F32)),
        grid=(t // seq, hp_n, nq),
        in_specs=[pl.BlockSpec((tq, 2 * LANES), lambda b, hp, i: (b * nq + i, hp)),
                  pl.BlockSpec((seq, 2 * LANES), lambda b, hp, i: (b, hp)),
                  pl.BlockSpec((seq, LANES), lambda b, hp, i: (b, hp))] + dep_specs,
        out_specs=(pl.BlockSpec((tq, LANES), lambda b, hp, i: (b * nq + i, hp)),
                   pl.BlockSpec((None, tq, 2), lambda b, hp, i: (hp, b * nq + i, 0))),
        scratch_shapes=[pltpu.VMEM((2, nq, tq, tq), F32)],
        compiler_params=_cparams("parallel", "parallel", "arbitrary"),
    )(qa, ka, v, *dep_ops)


def _merge_fwd(x, ps, o, g2, wpo, wao, wout, tm):
    t, d = x.shape
    tm = min(tm, t)

    def body(x_ref, ps_ref, o_ref, gp_ref, ga_ref, wpo_ref, wao_ref, wout_ref, mg_ref, x1_ref):
        py = _dot(ps_ref[...], wpo_ref[...])
        ay = _dot(o_ref[...].astype(BF16), wao_ref[...])
        mb = (_sigmoid(gp_ref[...].astype(F32)) * py + _sigmoid(ga_ref[...].astype(F32)) * ay).astype(BF16)
        mg_ref[...] = mb
        x1_ref[...] = x_ref[...] + _dot(mb, wout_ref[...])

    row = lambda w: pl.BlockSpec((tm, w), lambda i: (i, 0))
    full = lambda a: pl.BlockSpec(a.shape, lambda i: (0, 0))
    return pl.pallas_call(
        body, name="merge_fwd",
        out_shape=(jax.ShapeDtypeStruct((t, d), BF16), jax.ShapeDtypeStruct((t, d), F32)),
        grid=(t // tm,),
        in_specs=[row(d), row(POOL_WIDTH), row(ATTN_WIDTH), pl.BlockSpec((tm, d), lambda i: (i, 0)),
                  pl.BlockSpec((tm, d), lambda i: (i, 1)), full(wpo), full(wao), full(wout)],
        out_specs=(row(d), row(d)),
        compiler_params=_cparams("parallel"),
    )(x, ps, o, g2, g2, wpo, wao, wout)


def _ffn_fwd(x1, g, wg, wu, wd, tm, tf):
    t, d = x1.shape
    f = wg.shape[0]
    tm = min(tm, t)
    nf = f // tf

    def body(x1_ref, g_ref, wg_ref, wu_ref, wd_ref, h2_ref, gt_ref, up_ref, act_ref, x2_ref):
        j = pl.program_id(1)

        @pl.when(j == 0)
        def _():
            h2_ref[...] = _rms_fwd(x1_ref[...], g_ref[...]).astype(BF16)

        h2 = h2_ref[...]
        gt = _dot_nt(h2, wg_ref[...])
        up = _dot_nt(h2, wu_ref[...])
        sg = _sigmoid(gt)
        silu = gt * sg
        act = (silu * up).astype(BF16)
        gt_ref[...] = (up * (sg * (1.0 + gt * (1.0 - sg)))).astype(BF16)
        up_ref[...] = silu.astype(BF16)
        act_ref[...] = act
        prod = _dot(act, wd_ref[...])

        @pl.when(j == 0)
        def _():
            x2_ref[...] = prod

        @pl.when(j > 0)
        def _():
            x2_ref[...] += prod

        @pl.when(j == nf - 1)
        def _():
            x2_ref[...] += x1_ref[...]

    return pl.pallas_call(
        body, name="ffn_fwd",
        out_shape=(jax.ShapeDtypeStruct((t, d), BF16), jax.ShapeDtypeStruct((t, f), BF16),
                   jax.ShapeDtypeStruct((t, f), BF16), jax.ShapeDtypeStruct((t, f), BF16),
                   jax.ShapeDtypeStruct((t, d), F32)),
        grid=(t // tm, nf),
        in_specs=[pl.BlockSpec((tm, d), lambda i, j: (i, 0)), pl.BlockSpec((1, d), lambda i, j: (0, 0)),
                  pl.BlockSpec((tf, d), lambda i, j: (j, 0)), pl.BlockSpec((tf, d), lambda i, j: (j, 0)),
                  pl.BlockSpec((tf, d), lambda i, j: (j, 0))],
        out_specs=(pl.BlockSpec((tm, d), lambda i, j: (i, 0)), pl.BlockSpec((tm, tf), lambda i, j: (i, j)),
                   pl.BlockSpec((tm, tf), lambda i, j: (i, j)), pl.BlockSpec((tm, tf), lambda i, j: (i, j)),
                   pl.BlockSpec((tm, d), lambda i, j: (i, 0))),
        compiler_params=_cparams("parallel", "arbitrary"),
    )(x1, g, wg, wu, wd)


def _final_fwd_bwd(x2, target, g, tm):
    t, d = x2.shape
    tm = min(tm, t)

    def body(x_ref, t_ref, g_ref, loss_ref, dx_ref, dg_ref):
        i = pl.program_id(0)
        x = x_ref[...]
        gg = g_ref[...]
        err = _rms_fwd(x, gg) - t_ref[...]
        part = 0.5 * jnp.sum(jnp.mean(err * err, axis=-1, keepdims=True), axis=0, keepdims=True)
        dx, dg = _rms_bwd(x, gg, err * (1.0 / d))
        dx_ref[...] = dx

        @pl.when(i == 0)
        def _():
            loss_ref[...] = jnp.zeros_like(loss_ref)
            dg_ref[...] = jnp.zeros_like(dg_ref)

        loss_ref[...] += jnp.broadcast_to(part, loss_ref.shape)
        dg_ref[...] += dg

    return pl.pallas_call(
        body, name="final_fwd_bwd",
        out_shape=(jax.ShapeDtypeStruct((1, LANES), F32), jax.ShapeDtypeStruct((t, d), F32),
                   jax.ShapeDtypeStruct((1, d), F32)),
        grid=(t // tm,),
        in_specs=[pl.BlockSpec((tm, d), lambda i: (i, 0)), pl.BlockSpec((tm, d), lambda i: (i, 0)),
                  pl.BlockSpec((1, d), lambda i: (0, 0))],
        out_specs=(pl.BlockSpec((1, LANES), lambda i: (0, 0)), pl.BlockSpec((tm, d), lambda i: (i, 0)),
                   pl.BlockSpec((1, d), lambda i: (0, 0))),
        compiler_params=_cparams("arbitrary"),
    )(x2, target, g)


def _ffn_bwd(dx2, x1, g, gt, up, wg, wu, wd, tm, tf):
    t, d = dx2.shape
    f = gt.shape[1]
    tm = min(tm, t)
    nf = f // tf

    def body(dx2_ref, x1_ref, g_ref, gt_ref, up_ref, wg_ref, wu_ref, wd_ref, dgt_ref, dup_ref, dx1_ref, dg_ref, acc_ref,
             dxb_ref):
        i, j = pl.program_id(0), pl.program_id(1)

        @pl.when(j == 0)
        def _():
            dxb_ref[...] = dx2_ref[...].astype(BF16)

        dact = _dot_nt(dxb_ref[...], wd_ref[...])
        dgt = (dact * gt_ref[...].astype(F32)).astype(BF16)
        dup = (dact * up_ref[...].astype(F32)).astype(BF16)
        dgt_ref[...] = dgt
        dup_ref[...] = dup
        contrib = _dot(dgt, wg_ref[...]) + _dot(dup, wu_ref[...])

        @pl.when(j == 0)
        def _():
            acc_ref[...] = contrib

        @pl.when(j > 0)
        def _():
            acc_ref[...] += contrib

        @pl.when(jnp.logical_and(i == 0, j == 0))
        def _():
            dg_ref[...] = jnp.zeros_like(dg_ref)

        @pl.when(j == nf - 1)
        def _():
            dxn, dg = _rms_bwd(x1_ref[...], g_ref[...], acc_ref[...])
            dx1_ref[...] = dx2_ref[...] + dxn
            dg_ref[...] += dg

    return pl.pallas_call(
        body, name="ffn_bwd",
        out_shape=(jax.ShapeDtypeStruct((t, f), BF16), jax.ShapeDtypeStruct((t, f), BF16),
                   jax.ShapeDtypeStruct((t, d), F32), jax.ShapeDtypeStruct((1, d), F32)),
        grid=(t // tm, nf),
        in_specs=[pl.BlockSpec((tm, d), lambda i, j: (i, 0)), pl.BlockSpec((tm, d), lambda i, j: (i, 0)),
                  pl.BlockSpec((1, d), lambda i, j: (0, 0)),
                  pl.BlockSpec((tm, tf), lambda i, j: (i, j)), pl.BlockSpec((tm, tf), lambda i, j: (i, j)),
                  pl.BlockSpec((tf, d), lambda i, j: (j, 0)), pl.BlockSpec((tf, d), lambda i, j: (j, 0)),
                  pl.BlockSpec((tf, d), lambda i, j: (j, 0))],
        out_specs=(pl.BlockSpec((tm, tf), lambda i, j: (i, j)), pl.BlockSpec((tm, tf), lambda i, j: (i, j)),
                   pl.BlockSpec((tm, d), lambda i, j: (i, 0)), pl.BlockSpec((1, d), lambda i, j: (0, 0))),
        scratch_shapes=[pltpu.VMEM((tm, d), F32), pltpu.VMEM((tm, d), BF16)],
        compiler_params=_cparams("arbitrary", "arbitrary"),
    )(dx2, x1, g, gt, up, wg, wu, wd)


def _merge_bwd(dx1, ps, o, g2, wpo, wao, wout, tm, dep=None):
    t, d = dx1.shape
    tm = min(tm, t)

    def body(dx1_ref, ps_ref, o_ref, gp_ref, ga_ref, wpo_ref, wao_ref, wout_ref, dpy_ref, day_ref, dg2_ref, dps_ref, da_ref):
        dm = _dot_nt(dx1_ref[...].astype(BF16), wout_ref[...])
        py = _dot(ps_ref[...], wpo_ref[...])
        ay = _dot(o_ref[...].astype(BF16), wao_ref[...])
        sp = _sigmoid(gp_ref[...].astype(F32))
        sa = _sigmoid(ga_ref[...].astype(F32))
        dpy = (dm * sp).astype(BF16)
        day = (dm * sa).astype(BF16)
        dpy_ref[...] = dpy
        day_ref[...] = day
        dg2_ref[:, :d] = (dm * py * (sp * (1.0 - sp))).astype(BF16)
        dg2_ref[:, d:] = (dm * ay * (sa * (1.0 - sa))).astype(BF16)
        dps_ref[...] = _dot_nt(dpy, wpo_ref[...])
        da_ref[...] = _dot_nt(day, wao_ref[...]).astype(BF16)

    row = lambda w: pl.BlockSpec((tm, w), lambda i: (i, 0))
    full = lambda a: pl.BlockSpec(a.shape, lambda i: (0, 0))
    dep_specs, dep_ops = _dep_args(dep)
    return pl.pallas_call(
        _after(body, 8, dep), name="merge_bwd",
        out_shape=(jax.ShapeDtypeStruct((t, d), BF16), jax.ShapeDtypeStruct((t, d), BF16),
                   jax.ShapeDtypeStruct((t, 2 * d), BF16), jax.ShapeDtypeStruct((t, POOL_WIDTH), F32),
                   jax.ShapeDtypeStruct((t, ATTN_WIDTH), BF16)),
        grid=(t // tm,),
        in_specs=[row(d), row(POOL_WIDTH), row(ATTN_WIDTH), pl.BlockSpec((tm, d), lambda i: (i, 0)),
                  pl.BlockSpec((tm, d), lambda i: (i, 1)), full(wpo), full(wao), full(wout)] + dep_specs,
        out_specs=(row(d), row(d), row(2 * d), row(POOL_WIDTH), row(ATTN_WIDTH)),
        compiler_params=_cparams("parallel"),
    )(dx1, ps, o, g2, g2, wpo, wao, wout, *dep_ops)


def _attn_bwd(qa, ka, v, do, lse4, seq, tq, dep=None):
    t = qa.shape[0]
    nq = seq // tq
    hp_n = N_HEADS // 2
    heads = [slice(e * LANES, (e + 1) * LANES) for e in range(2)]

    def body(q_ref, k_ref, v_ref, do_ref, lse_ref, dq_ref, dk_ref, dv_ref, dfr_ref, dk_acc, dv_acc, p_buf, dp_buf):
        diag_ok = lax.broadcasted_iota(jnp.int32, (tq, tq), 0) >= lax.broadcasted_iota(jnp.int32, (tq, tq), 1)
        lane_q = lax.broadcasted_iota(jnp.int32, (tq, LANES), 1)
        lane_s = lax.broadcasted_iota(jnp.int32, (seq, LANES), 1)
        mine_q = [lane_q < HEAD_DIM, lane_q >= HEAD_DIM]
        dv_acc[...] = jnp.zeros_like(dv_acc)
        dk_acc[...] = jnp.zeros_like(dk_acc)
        dfr_ref[...] = jnp.zeros_like(dfr_ref)

        def q_step(i, _):
            q0 = pl.multiple_of(i * tq, tq)
            qs = [q_ref[pl.ds(q0, tq), hl] for hl in heads]
            dov = do_ref[pl.ds(q0, tq), :]
            dos = [jnp.where(mq, dov, jnp.zeros((), BF16)) for mq in mine_q]
            lss = [lse_ref[pl.ds(q0, tq), e:e + 1] for e in range(2)]

            def sweep1(j, dls, diagonal):
                r0 = pl.multiple_of(j * tq, tq)
                vv = v_ref[pl.ds(r0, tq), :]
                out = []
                for e, hl in enumerate(heads):
                    s = _dot_nt(qs[e], k_ref[pl.ds(r0, tq), hl])
                    if diagonal:
                        s = jnp.where(diag_ok, s, NEG_BIG)
                    p = jnp.exp(s - lss[e])
                    dp = _dot_nt(dos[e], vv)
                    p_buf[e, j] = p
                    dp_buf[e, j] = dp
                    dv_acc[pl.ds(r0, tq), :] += _dot_tn(p.astype(BF16), dos[e])
                    out.append(dls[e] + _fold_lanes(p * dp, jnp.add))
                return tuple(out)

            dls = _causal_sweep(i, sweep1, (jnp.zeros((tq, LANES), F32),) * 2)
            dls = [jnp.sum(d, axis=1, keepdims=True) for d in dls]

            def sweep2(j, dqs, diagonal):
                r0 = pl.multiple_of(j * tq, tq)
                out = []
                for e, hl in enumerate(heads):
                    ds = p_buf[e, j] * (dp_buf[e, j] - dls[e])
                    dfr_ref[e, pl.ds(j, 1), :] += jnp.sum(ds, axis=0, keepdims=True)
                    dsb = ds.astype(BF16)
                    dk_acc[e, pl.ds(r0, tq), :] += _dot_tn(dsb, qs[e])
                    out.append(dqs[e] + _dot(dsb, k_ref[pl.ds(r0, tq), hl]))
                return tuple(out)

            dqs = _causal_sweep(i, sweep2, (jnp.zeros((tq, LANES), F32),) * 2)
            dq = jnp.where(mine_q[0], dqs[0], pltpu.roll(dqs[1], HEAD_DIM, 1)) * ATTN_SCALE
            dq_ref[pl.ds(q0, tq), :] = dq.astype(BF16)
            return 0

        lax.fori_loop(0, nq, q_step, 0)
        dk_ref[...] = jnp.where(lane_s < HEAD_DIM, dk_acc[0], pltpu.roll(dk_acc[1], HEAD_DIM, 1)).astype(BF16)
        dv_ref[...] = dv_acc[...].astype(BF16)

    wide = pl.BlockSpec((seq, 2 * LANES), lambda b, hp: (b, hp))
    col = pl.BlockSpec((seq, LANES), lambda b, hp: (b, hp))
    pair = pl.BlockSpec((None, seq, 2), lambda b, hp: (hp, b, 0))
    dep_specs, dep_ops = _dep_args(dep)
    return pl.pallas_call(
        _after(body, 5, dep), name="attn_bwd",
        out_shape=(jax.ShapeDtypeStruct((t, ATTN_WIDTH), BF16),) * 3 + (jax.ShapeDtypeStruct((N_HEADS, t // tq, tq), F32),),
        grid=(t // seq, hp_n),
        in_specs=[wide, wide, col, col, pair] + dep_specs,
        out_specs=(col, col, col, pl.BlockSpec((2, nq, tq), lambda b, hp: (hp, b, 0))),
        scratch_shapes=[pltpu.VMEM((2, seq, LANES), F32), pltpu.VMEM((seq, LANES), F32),
                        pltpu.VMEM((2, nq, tq, tq), F32), pltpu.VMEM((2, nq, tq, tq), F32)],
        compiler_params=_cparams("parallel", "arbitrary"),
    )(qa, ka, v, do, lse4, *dep_ops)


def _forget_bwd(dfc, fl, bf, seq):
    t = fl.shape[0]
    cb = min(256, seq)
    nb = seq // cb

    def body(dfc_ref, fl_ref, bf_ref, dfl_ref, db_ref):
        b = pl.program_id(0)
        ri = lax.broadcasted_iota(jnp.int32, (cb, cb), 0)
        ci = lax.broadcasted_iota(jnp.int32, (cb, cb), 1)
        tri = (ci >= ri).astype(BF16)
        carry = jnp.zeros((1, LANES), F32)
        dbs = jnp.zeros((1, LANES), F32)
        for blk in reversed(range(nb)):
            rs = slice(blk * cb, (blk + 1) * cb)
            dlf = _tri_dot(tri, -dfc_ref[rs, :]) + carry
            carry = dlf[0:1, :]
            dfl = dlf * _sigmoid(-(fl_ref[rs, :] + bf_ref[...]))
            dfl_ref[rs, :] = dfl.astype(BF16)
            dbs = dbs + jnp.sum(dfl, axis=0, keepdims=True)

        @pl.when(b == 0)
        def _():
            db_ref[...] = jnp.zeros_like(db_ref)

        db_ref[...] += dbs

    return pl.pallas_call(
        body, name="forget_bwd",
        out_shape=(jax.ShapeDtypeStruct((t, LANES), BF16), jax.ShapeDtypeStruct((1, LANES), F32)),
        grid=(t // seq,),
        in_specs=[pl.BlockSpec((seq, LANES), lambda b: (b, 0)), pl.BlockSpec((seq, LANES), lambda b: (b, 0)),
                  pl.BlockSpec((1, LANES), lambda b: (0, 0))],
        out_specs=(pl.BlockSpec((seq, LANES), lambda b: (b, 0)), pl.BlockSpec((1, LANES), lambda b: (0, 0))),
        compiler_params=_cparams("arbitrary"),
    )(dfc, fl, bf)


def _pool_bwd(dps, p, mix, scale, seq):
    t = dps.shape[0]

    def body(dps_ref, p_ref, mix_ref, sc_ref, du_ref, dmix_ref, dsc_ref):
        b = pl.program_id(0)

        @pl.when(b == 0)
        def _():
            dmix_ref[...] = jnp.zeros_like(dmix_ref)
            dsc_ref[...] = jnp.zeros_like(dsc_ref)

        tpos = lax.broadcasted_iota(jnp.int32, (seq, POOL_GROUP_DIM), 0)
        for g in range(POOL_GROUPS):
            sl = slice(g * POOL_GROUP_DIM, (g + 1) * POOL_GROUP_DIM)
            pb = p_ref[:, sl]
            dpsg = dps_ref[:, sl]
            pm = _dot(pb, mix_ref[g])
            dsc_ref[:, sl] += jnp.sum(dpsg * pm, axis=0, keepdims=True)
            dpm = (dpsg * sc_ref[:, sl]).astype(BF16)
            dmix_ref[g] += _dot_tn(pb, dpm)
            dp = _dot_nt(dpm, mix_ref[g])
            cnt = jnp.minimum(tpos + 1, POOL_WINDOWS[g]).astype(F32)
            s = dp / cnt
            for lvl in range(g + 1):
                d = 2 ** lvl
                s = s + jnp.where(tpos < seq - d, pltpu.roll(s, seq - d, 0), 0.0)
            du_ref[:, sl] = (s - dp).astype(BF16)

    return pl.pallas_call(
        body, name="pool_bwd",
        out_shape=(jax.ShapeDtypeStruct((t, POOL_WIDTH), BF16),
                   jax.ShapeDtypeStruct((POOL_GROUPS, POOL_GROUP_DIM, POOL_GROUP_DIM), F32),
                   jax.ShapeDtypeStruct((1, POOL_WIDTH), F32)),
        grid=(t // seq,),
        in_specs=[pl.BlockSpec((seq, POOL_WIDTH), lambda b: (b, 0)), pl.BlockSpec((seq, POOL_WIDTH), lambda b: (b, 0)),
                  pl.BlockSpec((POOL_GROUPS, POOL_GROUP_DIM, POOL_GROUP_DIM), lambda b: (0, 0, 0)),
                  pl.BlockSpec((1, POOL_WIDTH), lambda b: (0, 0))],
        out_specs=(pl.BlockSpec((seq, POOL_WIDTH), lambda b: (b, 0)),
                   pl.BlockSpec((POOL_GROUPS, POOL_GROUP_DIM, POOL_GROUP_DIM), lambda b: (0, 0, 0)),
                   pl.BlockSpec((1, POOL_WIDTH), lambda b: (0, 0))),
        compiler_params=_cparams("arbitrary"),
    )(dps, p, mix, scale)


def _in_bwd(du, dq, dk, dv, dg2, dfl, dx1, x, g, wu, wqkv, wg2, wft, tm):
    t, d = x.shape
    tm = min(tm, t)
    aw = ATTN_WIDTH

    def body(du_ref, dq_ref, dk_ref, dv_ref, dg2_ref, dfl_ref, dx1_ref, x_ref, g_ref, wu_ref, wqkv_ref, wg2_ref, wft_ref,
             dx_ref, dg_ref):
        i = pl.program_id(0)
        dh = _dot(du_ref[...], wu_ref[...])
        dh += _dot(dq_ref[...], wqkv_ref[0:aw, :])
        dh += _dot(dk_ref[...], wqkv_ref[aw:2 * aw, :])
        dh += _dot(dv_ref[...], wqkv_ref[2 * aw:3 * aw, :])
        dh += _dot(dg2_ref[...], wg2_ref[...])
        dh += _dot(dfl_ref[...], wft_ref[...])
        dxn, dg = _rms_bwd(x_ref[...], g_ref[...], dh)
        dx_ref[...] = dx1_ref[...] + dxn

        @pl.when(i == 0)
        def _():
            dg_ref[...] = jnp.zeros_like(dg_ref)

        dg_ref[...] += dg

    row = lambda w: pl.BlockSpec((tm, w), lambda i: (i, 0))
    full = lambda a: pl.BlockSpec(a.shape, lambda i: (0, 0))
    return pl.pallas_call(
        body, name="in_bwd",
        out_shape=(jax.ShapeDtypeStruct((t, d), F32), jax.ShapeDtypeStruct((1, d), F32)),
        grid=(t // tm,),
        in_specs=[row(POOL_WIDTH), row(aw), row(aw), row(aw), row(2 * d), row(LANES), row(d), row(d),
                  pl.BlockSpec((1, d), lambda i: (0, 0)), full(wu), full(wqkv), full(wg2), full(wft)],
        out_specs=(row(d), pl.BlockSpec((1, d), lambda i: (0, 0))),
        compiler_params=_cparams("arbitrary"),
    )(du, dq, dk, dv, dg2, dfl, dx1, x, g, wu, wqkv, wg2, wft)


def _position():
    return lax.axis_index("x"), lax.axis_index("y"), lax.axis_index("c")


def _remote(src, dst, send_sem, recv_sem, device):
    return pltpu.make_async_remote_copy(src_ref=src, dst_ref=dst, send_sem=send_sem, recv_sem=recv_sem,
                                        device_id=device, device_id_type=MESH)


HBM = pl.BlockSpec(memory_space=pltpu.HBM)
SEM = pl.BlockSpec(memory_space=pltpu.SEMAPHORE)
DATAFLOW = pltpu.SideEffectType.DATAFLOW_SIDE_EFFECTING


def _copies_start(name, arrays, plan, m, dep=None):
    n = len(arrays)
    arrays = [pltpu.with_memory_space_constraint(a, pltpu.HBM) for a in arrays]

    def body(*refs):
        ins, send_sem, recv_sem, token = refs[:n], refs[n], refs[n + 1], refs[2 * n + 2]
        for i, (src, dst, device, _) in enumerate(plan(ins, *_position())):
            _remote(src, dst, send_sem.at[i], recv_sem.at[i], device).start()
        token[...] = jnp.zeros_like(token)

    dep_specs, dep_ops = _dep_args(dep)
    outs = pl.pallas_call(
        _after(body, n, dep), name=name,
        out_shape=(pltpu.SemaphoreType.DMA((m,)), pltpu.SemaphoreType.DMA((m,)),
                   *[pltpu.HBM(a.shape, a.dtype) for a in arrays], jax.ShapeDtypeStruct((8, LANES), F32)),
        in_specs=[HBM] * n + dep_specs, out_specs=(SEM, SEM, *[HBM] * n, pl.BlockSpec(memory_space=pltpu.VMEM)),
        input_output_aliases={i: i + 2 for i in range(n)},
        compiler_params=pltpu.CompilerParams(has_side_effects=DATAFLOW),
    )(*arrays, *dep_ops)
    return (outs[0], outs[1]), list(outs[2:2 + n]), outs[2 + n]


def _copies_wait(name, sems, arrays, plan, after):
    n = len(arrays)
    afters = list(after) if isinstance(after, (list, tuple)) else [after]

    def body(*refs):
        ins, send_sem, recv_sem = refs[:n], refs[n], refs[n + 1]
        for i, (src, dst, device, landing) in enumerate(plan(ins, *_position())):
            _remote(src, dst, send_sem.at[i], recv_sem.at[i], device).wait_send()
            _remote(landing, landing, send_sem.at[i], recv_sem.at[i], device).wait_recv()

    outs = pl.pallas_call(
        body, name=name,
        out_shape=tuple(pltpu.HBM(a.shape, a.dtype) for a in arrays),
        in_specs=[HBM] * n + [SEM, SEM] + [ANY] * len(afters), out_specs=tuple([HBM] * n),
        input_output_aliases={i: i for i in range(n)},
        compiler_params=pltpu.CompilerParams(has_side_effects=DATAFLOW),
    )(*arrays, sems[0], sems[1], *afters)
    return list(outs)


def _tie(x, dep):
    for token in _dep_list(dep):
        x = x + token[0, 0]
    return x


def _other_chips(x, y):
    return [(1 - x, y), (x, 1 - y), (1 - x, 1 - y)]


def _gather_begin(tag, shards, token, column_halves=False):
    n = len(shards)
    lands = [lax.empty((N_CHIPS,) + s.shape, s.dtype) for s in shards]
    if column_halves:
        cols = lambda ref, h: pl.ds(pl.multiple_of(h * (ref.shape[-1] // 2), LANES), ref.shape[-1] // 2)
        mine = lambda ref, h: ref.at[:, cols(ref, h)]
        landed = lambda ref, chip, h: ref.at[chip, :, cols(ref, h)]
    else:
        mine = lambda ref, h: ref.at[h]
        landed = lambda ref, chip, h: ref.at[chip, h]

    def plan(refs, x, y, c):
        return [(mine(refs[k], c), landed(refs[n + k], 2 * x + y, c), (ox, oy, c), landed(refs[n + k], 2 * ox + oy, c))
                for k in range(n) for ox, oy in _other_chips(x, y)]

    sems, thru, token = _copies_start(f"gather_{tag}_ici_start", list(shards) + lands, plan, 3 * n, dep=token)
    return dict(tag=tag, n=n, plan=plan, sems=sems, arrays=thru, token=token, landed=landed)


def _gather_forward(st, after):
    n, tag, landed = st["n"], st["tag"], st["landed"]
    thru = _copies_wait(f"gather_{tag}_ici_wait", st["sems"], st["arrays"], st["plan"], after)

    def plan(refs, x, y, c):
        return [(landed(refs[k], 2 * ox + oy, c), landed(refs[k], 2 * ox + oy, c), (x, y, 1 - c),
                 landed(refs[k], 2 * ox + oy, 1 - c))
                for k in range(n) for ox, oy in _other_chips(x, y)]

    sems, lands, token = _copies_start(f"gather_{tag}_fwd_start", thru[n:], plan, 3 * n)
    return dict(tag=tag, n=n, plan=plan, sems=sems, arrays=lands, token=token, shards=thru[:n])


def _gather_end(st, after, merge=True):
    lands = _copies_wait(f"gather_{st['tag']}_fwd_wait", st["sems"], st["arrays"], st["plan"], after)
    if not merge:
        return lands, st["shards"]
    me = 2 * lax.axis_index("x") + lax.axis_index("y")
    return [lax.dynamic_update_index_in_dim(g, s, me, 0) for g, s in zip(lands, st["shards"])]


def _add_keep_give(name, pos, a, a_keep, a_give, b, b_keep, b_give, steps):
    r, c = b.shape[-2:]

    def spec(arr, fn):
        lead = arr.ndim - 2

        def index(i, p):
            idx = tuple(fn(i, p))
            return idx if len(idx) == arr.ndim else idx + (0, 0)

        return pl.BlockSpec((None,) * lead + (r, c), index)

    out_spec = pl.BlockSpec((None, r, c), lambda i, p: (i, 0, 0))

    def body(p_ref, ak_ref, bk_ref, ag_ref, bg_ref, keep_ref, give_ref):
        keep_ref[...] = ak_ref[...] + bk_ref[...].astype(F32)
        give_ref[...] = (ag_ref[...] + bg_ref[...].astype(F32)).astype(BF16)

    return pl.pallas_call(
        body, name=name,
        out_shape=(jax.ShapeDtypeStruct((steps, r, c), F32), jax.ShapeDtypeStruct((steps, r, c), BF16)),
        grid_spec=pltpu.PrefetchScalarGridSpec(
            num_scalar_prefetch=1, grid=(steps,),
            in_specs=[spec(a, a_keep), spec(b, b_keep), spec(a, a_give), spec(b, b_give)],
            out_specs=(out_spec, out_spec)),
        compiler_params=_cparams("parallel"),
    )(pos, a, b, a, b)


def _add_last(name, a, b):
    _, r, c = a.shape
    blk = pl.BlockSpec((None, r, c), lambda i: (0, 0, 0))

    def body(a_ref, b_ref, o_ref):
        o_ref[...] = a_ref[...] + b_ref[...].astype(F32)

    return pl.pallas_call(
        body, name=name, out_shape=jax.ShapeDtypeStruct((r, c), F32), grid=(1,), in_specs=[blk, blk],
        out_specs=pl.BlockSpec((r, c), lambda i: (0, 0)), compiler_params=_cparams("arbitrary"),
    )(a, b)


def _exchange_begin(tag, stage, gives, lands, peer_fn, extra):
    n = len(gives)

    def plan(refs, x, y, c):
        return [(refs[k], refs[n + k], peer_fn(x, y, c), refs[n + k]) for k in range(n)]

    sems, thru, token = _copies_start(f"rs{tag}_{stage}_start", gives + lands, plan, n)
    return dict(extra, tag=tag, n=n, stage=stage, plan=plan, sems=sems, arrays=thru, token=token)


def _reduce_begin(tag, grads, column_halves=False):
    n = len(grads)
    if column_halves:
        half = lambda ref, j, h: ref.at[j, :, pl.ds(pl.multiple_of(h * (ref.shape[2] // 2), LANES), ref.shape[2] // 2)]
        lands = [lax.empty((N_CHIPS, g.shape[1], g.shape[2] // 2), F32) for g in grads]
    else:
        half = lambda ref, j, h: ref.at[j, h]
        lands = [lax.empty((N_CHIPS,) + g.shape[2:], F32) for g in grads]

    def plan(refs, x, y, c):
        return [(half(refs[k], j, 1 - c), refs[n + k].at[j], (x, y, 1 - c), refs[n + k].at[j])
                for k in range(n) for j in range(N_CHIPS)]

    sems, thru, token = _copies_start(f"rs{tag}_c_start", list(grads) + lands, plan, N_CHIPS * n)
    return dict(tag=tag, n=n, stage="c", plan=plan, sems=sems, arrays=thru, token=token, column_halves=column_halves)


def _reduce_advance(st, after):
    tag, n, stage = st["tag"], st["n"], st["stage"]
    thru = _copies_wait(f"rs{tag}_{stage}_wait", st["sems"], st["arrays"], st["plan"], after)
    first, recv = thru[:n], thru[n:]
    x, y, c = _position()
    if stage == "c":
        pos = jnp.stack([c, x]).astype(jnp.int32)
        if st["column_halves"]:
            mine = lambda chip: (lambda i, p: (chip(p) + i, 0, p[0]))
        else:
            mine = lambda chip: (lambda i, p: (chip(p) + i, p[0]))
        sums = [_add_keep_give(
            f"rs{tag}_c_add{k}", pos,
            first[k], mine(lambda p: 2 * p[1]), mine(lambda p: 2 * (1 - p[1])),
            recv[k], lambda i, p: (2 * p[1] + i,), lambda i, p: (2 * (1 - p[1]) + i,), 2) for k in range(n)]
        lands = [lax.empty(s[1].shape, BF16) for s in sums]
        return _exchange_begin(tag, "x", [s[1] for s in sums], lands, lambda x, y, c: (1 - x, y, c),
                               dict(keep=[s[0] for s in sums]))
    if stage == "x":
        pos = jnp.stack([y]).astype(jnp.int32)
        sums = [_add_keep_give(
            f"rs{tag}_x_add{k}", pos,
            st["keep"][k], lambda i, p: (p[0],), lambda i, p: (1 - p[0],),
            recv[k], lambda i, p: (p[0],), lambda i, p: (1 - p[0],), 1) for k in range(n)]
        lands = [lax.empty(s[1].shape, BF16) for s in sums]
        return _exchange_begin(tag, "y", [s[1] for s in sums], lands, lambda x, y, c: (x, 1 - y, c),
                               dict(keep=[s[0] for s in sums]))
    if stage == "y":
        mine = [_add_last(f"rs{tag}_y_add{k}", st["keep"][k], recv[k]) for k in range(n)]
        lands = [lax.empty(m.shape, F32) for m in mine]
        return _exchange_begin(tag, "swap", mine, lands, lambda x, y, c: (x, y, 1 - c), {})
    return dict(done=list(zip(first, recv)), token=None)


def _all_reduce_small(v):
    r = v.shape[0]

    def body(v_ref, out_ref, buf, send_sems, recv_sems, local_sem):
        x, y, c = _position()
        me, sibling = (x, y, c), (x, y, 1 - c)
        chips = [(1 - x, y), (x, 1 - y), (1 - x, 1 - y)]

        def rows(px, py, pc):
            return buf.at[pl.ds((4 * px + 2 * py + pc) * r, r), :]

        def copy(k, block, to, src=None):
            return _remote(rows(*block) if src is None else src, rows(*block), send_sems.at[k], recv_sems.at[k], to)

        mine = pltpu.make_async_copy(v_ref, rows(*me), local_sem)
        mine.start()
        first = [copy(0, me, sibling, src=v_ref)]
        first += [copy(1 + j, me, (*chip, c), src=v_ref) for j, chip in enumerate(chips)]
        for cp in first:
            cp.start()
        passed = [copy(4 + j, (*chip, c), sibling) for j, chip in enumerate(chips)]
        for j, chip in enumerate(chips):
            copy(1 + j, (*chip, c), me).wait_recv()
            passed[j].start()
        copy(0, sibling, me).wait_recv()
        for j, chip in enumerate(chips):
            copy(4 + j, (*chip, 1 - c), me).wait_recv()
        for cp in first + passed:
            cp.wait_send()
        mine.wait()
        acc = buf[0:r, :]
        for dev in range(1, N_DEV):
            acc = acc + buf[dev * r:(dev + 1) * r, :]
        out_ref[...] = acc

    return pl.pallas_call(
        body, name="all_reduce_small",
        out_shape=jax.ShapeDtypeStruct(v.shape, F32),
        in_specs=[pl.BlockSpec(memory_space=pltpu.VMEM)],
        out_specs=pl.BlockSpec(memory_space=pltpu.VMEM),
        scratch_shapes=[pltpu.VMEM((N_DEV * r, LANES), F32), pltpu.SemaphoreType.DMA((7,)),
                        pltpu.SemaphoreType.DMA((7,)), pltpu.SemaphoreType.DMA],
        compiler_params=pltpu.CompilerParams(has_side_effects=True, vmem_limit_bytes=VMEM_LIMIT_V7X),
    )(v)


def _adamw_update(w, gg, m, v):
    mn = ADAM_B1 * m + (1.0 - ADAM_B1) * gg
    vn = ADAM_B2 * v + (1.0 - ADAM_B2) * (gg * gg)
    m_hat = mn / (1.0 - ADAM_B1 ** ADAM_STEP)
    v_hat = vn / (1.0 - ADAM_B2 ** ADAM_STEP)
    return -ADAM_LR * (m_hat / (jnp.sqrt(v_hat) + ADAM_EPS) + ADAM_WD * w), mn, vn


def _adamw(name, w, g, m, v):
    def body(w_ref, g_ref, m_ref, v_ref, d_ref, mo_ref, vo_ref):
        d_ref[...], mo_ref[...], vo_ref[...] = _adamw_update(w_ref[...], g_ref[...], m_ref[...], v_ref[...])

    blk = pl.BlockSpec(w.shape, lambda i: (0, 0))
    return pl.pallas_call(
        body, name=name, out_shape=(jax.ShapeDtypeStruct(w.shape, F32),) * 3, grid=(1,),
        in_specs=[blk] * 4, out_specs=(blk,) * 3, compiler_params=_cparams("arbitrary"),
    )(w, g, m, v)


def _rows_to_bf16(name, w):
    r, _, c = w.shape

    def body(w_ref, o_ref):
        o_ref[...] = w_ref[:, 0, :].astype(BF16)

    return pl.pallas_call(
        body, name=name, out_shape=jax.ShapeDtypeStruct((r, c), BF16), grid=(1,),
        in_specs=[pl.BlockSpec((r, 1, c), lambda i: (0, 0, 0))], out_specs=pl.BlockSpec((r, c), lambda i: (0, 0)),
        compiler_params=_cparams("arbitrary"),
    )(w)


def _adamw_rows(name, pos_c, w, g_mine, g_other, m, v):
    r, _, c = w.shape
    ch = c // 2

    def body(p_ref, w_ref, gm_ref, go_ref, m_ref, v_ref, g_ref, d_ref, mo_ref, vo_ref):
        gg = jnp.where(pl.program_id(0) == p_ref[0], gm_ref[...], go_ref[...])
        dl, mn, vn = _adamw_update(w_ref[:, 0, :], gg, m_ref[:, 0, :], v_ref[:, 0, :])
        g_ref[:, 0, :] = gg
        d_ref[:, 0, :] = dl
        mo_ref[:, 0, :] = mn
        vo_ref[:, 0, :] = vn

    rows = pl.BlockSpec((r, 1, ch), lambda h, p: (0, 0, h))
    half = pl.BlockSpec((r, ch), lambda h, p: (0, 0))
    return pl.pallas_call(
        body, name=name, out_shape=(jax.ShapeDtypeStruct(w.shape, F32),) * 4,
        grid_spec=pltpu.PrefetchScalarGridSpec(
            num_scalar_prefetch=1, grid=(2,), in_specs=[rows, half, half, rows, rows], out_specs=(rows,) * 4),
        compiler_params=_cparams("parallel"),
    )(pos_c, w, g_mine, g_other, m, v)


def _adamw_halves(name, pos_c, w, g_mine, g_other, m, v, tr, dep=None):
    r, c = w.shape
    rh = r // 2
    tr = tr if rh % tr == 0 else rh
    nt = rh // tr

    def body(p_ref, w_ref, gm_ref, go_ref, m_ref, v_ref, g_ref, d_ref, mo_ref, vo_ref):
        gg = jnp.where(pl.program_id(0) == p_ref[0], gm_ref[...], go_ref[...])
        g_ref[...] = gg
        d_ref[...], mo_ref[...], vo_ref[...] = _adamw_update(w_ref[...], gg, m_ref[...], v_ref[...])

    full = pl.BlockSpec((tr, c), lambda h, i, p: (h * nt + i, 0))
    half = pl.BlockSpec((tr, c), lambda h, i, p: (i, 0))
    dep_specs, dep_ops = _dep_args(dep)
    return pl.pallas_call(
        _after(body, 6, dep), name=name, out_shape=(jax.ShapeDtypeStruct((r, c), F32),) * 4,
        grid_spec=pltpu.PrefetchScalarGridSpec(
            num_scalar_prefetch=1, grid=(2, nt),
            in_specs=[full, half, half, full, full] + dep_specs, out_specs=(full,) * 4),
        compiler_params=_cparams("parallel", "parallel"),
    )(pos_c, w, g_mine, g_other, m, v, *dep_ops)


def _col_sharded_to_comm(g):
    k, n = g.shape
    return g.reshape(2, k // 2, N_CHIPS, n // N_CHIPS).transpose(2, 0, 1, 3)


def _row_sharded_to_comm(g):
    r, c = g.shape
    return g.reshape(N_CHIPS, 2, r // (2 * N_CHIPS), c)


def _col_sharded_full(g):
    _, _, rh, c = g.shape
    return g.reshape(N_CHIPS, 2 * rh, c).transpose(1, 0, 2).reshape(2 * rh, N_CHIPS * c)


def _row_sharded_full(g):
    _, _, rh, c = g.shape
    return g.reshape(N_CHIPS * 2 * rh, c)


def _chip_rows(w3, start, stop, own=None, me=None):
    r = w3.shape[1]
    parts = []
    for chip in range(N_CHIPS):
        lo, hi = max(start - chip * r, 0), min(stop - chip * r, r)
        if lo < hi:
            part = w3[chip, lo:hi]
            parts.append(part if own is None else jnp.where(me == chip, own[lo:hi], part))
    return parts[0] if len(parts) == 1 else jnp.concatenate(parts, axis=0)


def _pack_small(g1, bfv, mix, scale, g2n, gf, extra=None):
    row8 = jnp.pad(bfv.reshape(1, N_HEADS), ((0, 0), (0, LANES - N_HEADS)))
    if extra is not None:
        row8 = row8 + jnp.pad(extra[:, :1], ((0, 0), (N_HEADS, LANES - N_HEADS - 1)))
    return jnp.concatenate([
        g1.reshape(8, LANES), jnp.pad(row8, ((0, 7), (0, 0))), mix.reshape(512, LANES),
        jnp.pad(scale.reshape(4, LANES), ((0, 4), (0, 0))), g2n.reshape(8, LANES), gf.reshape(8, LANES)], axis=0)


def _unpack_small(s, like):
    g1, bfv, mix, scale, g2n, gf = like
    return (s[0:8].reshape(g1.shape), s[8, :N_HEADS].reshape(bfv.shape), s[16:528].reshape(mix.shape),
            s[528:532].reshape(scale.shape), s[536:544].reshape(g2n.shape), s[544:552].reshape(gf.shape))


class _MeshLinks:
    def __init__(self, shards_in, shards_rest):
        self.gin = _gather_begin("in", shards_in, None, column_halves=True)
        self.grest = _gather_begin("rest", shards_rest, self.gin["token"])
        self.tokens = {"gather": self.grest["token"]}
        self.groups = {}

    @property
    def token(self):
        return list(self.tokens.values())

    def tie(self, x):
        return _tie(x, self.token)

    def weights_in(self, after):
        st = _gather_forward(self.gin, after)
        (g,), (own,) = _gather_end(st, st["token"], merge=False)
        return g, own, 2 * lax.axis_index("x") + lax.axis_index("y")

    def rest_forward(self, after):
        self.grest = _gather_forward(self.grest, after)
        self.tokens["gather"] = self.grest["token"]

    def weights_rest(self, after):
        g = _gather_end(self.grest, after)
        del self.tokens["gather"]
        return [_col_sharded_full(g[0]), _col_sharded_full(g[1])] + [_row_sharded_full(a) for a in g[2:]]

    def reduce_begin(self, tag, grads, column_halves=False):
        self.groups[tag] = _reduce_begin(tag, grads, column_halves)
        self.tokens[tag] = self.groups[tag]["token"]

    def advance(self, after):
        for tag, st in self.groups.items():
            if "done" not in st:
                self.groups[tag] = _reduce_advance(st, after)
                if self.groups[tag]["token"] is None:
                    del self.tokens[tag]
                else:
                    self.tokens[tag] = self.groups[tag]["token"]

    def reduced(self, tag):
        return self.groups[tag]["done"]


class _NoLinks:
    token = None

    def __init__(self, w_in, rest):
        self.w_in, self.rest, self.grads = w_in, rest, {}

    def tie(self, x):
        return x

    def weights_in(self, after):
        return self.w_in, None, None

    def rest_forward(self, after):
        pass

    def weights_rest(self, after):
        return self.rest

    def reduce_begin(self, tag, grads, column_halves=False):
        self.grads[tag] = grads

    def advance(self, after):
        pass


def _local_step(links, x, target, seq, norm1_g, b_forget, pool_mix, pool_scale, norm2_g, norm_f_g):
    t, d = x.shape
    tq = min(256, seq)
    aw = ATTN_WIDTH
    o_q, o_f, o_g = POOL_WIDTH, POOL_WIDTH + 3 * aw, POOL_WIDTH + 3 * aw + N_HEADS
    bf = jnp.pad(b_forget, ((0, 0), (0, LANES - N_HEADS)))
    mixb = pool_mix.astype(BF16)

    h = _norm_fwd("norm1_fwd", x, links.tie(norm1_g), 512)
    w_in, own, me = links.weights_in(h)
    wu = _chip_rows(w_in, 0, o_q, own, me)
    wqkv = _chip_rows(w_in, o_q, o_f, own, me)
    wft = jnp.pad(_chip_rows(w_in, o_f, o_g, own, me), ((0, LANES - N_HEADS), (0, 0)))
    wg2 = _chip_rows(w_in, o_g, N_CHIPS * w_in.shape[1], own, me)
    wf = wft.T
    u = _matmul("mm_u", h, wu, "nt", F32, 1024, 512, d)
    g2 = _matmul("mm_gates", h, wg2, "nt", BF16, 1024, 512, d)
    fl, fcum = _forget_fwd(h, wf, bf, seq)
    qa, ka, v = _attn_prep(h, _head_blocks(wqkv[:aw]), _head_blocks(wqkv[aw:2 * aw]), wqkv[2 * aw:], fcum, 1024)
    p, ps = _pool_fwd(u, mixb, pool_scale, seq)
    links.rest_forward([ps, qa, g2])
    o, lse = _attn_fwd(qa, ka, v, seq, tq, dep=links.token)
    w_pool_out, w_attn_out, w_out, w_ffn_gate, w_ffn_up, w_ffn_down = links.weights_rest(o)
    merged, x1 = _merge_fwd(x, ps, o, g2, w_pool_out, w_attn_out, w_out, 512)
    h2, gt, up, act, x2 = _ffn_fwd(x1, norm2_g, w_ffn_gate, w_ffn_up, w_ffn_down, 1024, 256)
    loss, dx2, d_gf = _final_fwd_bwd(x2, target, norm_f_g, 512)

    dgt, dup, dx1, d_g2n = _ffn_bwd(dx2, x1, norm2_g, gt, up, w_ffn_gate, w_ffn_up, w_ffn_down, 1024, 256)
    d_wd = _matmul("dw_down", act, dx2, "tn", F32, 1408, 1024, 1024)
    d_wg = _matmul("dw_gate", dgt, h2, "tn", F32, 1408, 1024, 1024)
    d_wu = _matmul("dw_up", dup, h2, "tn", F32, 1408, 1024, 1024)
    links.reduce_begin("a", [_row_sharded_to_comm(g) for g in (d_wg, d_wu, d_wd)])
    dpy, day, dg2, dps, da = _merge_bwd(dx1, ps, o, g2, w_pool_out, w_attn_out, w_out, 512, dep=links.token)
    links.advance(dps)
    d_wout = _matmul("dw_out", merged, dx1, "tn", F32, 1024, 1024, 1024)
    d_wpo = _matmul("dw_pool_out", ps, dpy, "tn", F32, 512, 1024, 1024)
    d_wao = _matmul("dw_attn_out", o, day, "tn", F32, 512, 1024, 1024)
    links.reduce_begin("m", [_col_sharded_to_comm(d_wpo), _col_sharded_to_comm(d_wao), _row_sharded_to_comm(d_wout)])
    dq, dk, dv, dfr = _attn_bwd(qa, ka, v, da, lse, seq, tq, dep=links.token)
    links.advance(dq)
    dfc = jnp.pad(dfr.reshape(N_HEADS, t).T, ((0, 0), (0, LANES - N_HEADS)))
    dfl, d_bf = _forget_bwd(dfc, fl, bf, seq)
    du, d_mix, d_scale = _pool_bwd(dps, p, mixb, links.tie(pool_scale), seq)
    d_wu_in = _matmul("dw_in_u", du, h, "tn", F32, 512, 1024, 1024)
    d_wq = _matmul("dw_in_q", dq, h, "tn", F32, 512, 1024, 1024)
    d_wk = _matmul("dw_in_k", dk, h, "tn", F32, 512, 1024, 1024)
    d_wv = _matmul("dw_in_v", dv, h, "tn", F32, 512, 1024, 1024)
    links.advance([d_wu_in, d_wq, d_wk, d_wv])
    d_wf = _matmul("dw_in_f", dfl, h, "tn", F32, LANES, 1024, 512)
    d_wg2 = _matmul("dw_in_gates", dg2, h, "tn", F32, 1024, 1024, 1024, dep=links.token)
    d_win = jnp.concatenate([d_wu_in, d_wq, d_wk, d_wv, d_wf[:N_HEADS], d_wg2], axis=0)
    comm_b = [d_win.reshape(N_CHIPS, d_win.shape[0] // N_CHIPS, d)]
    links.advance(comm_b)
    links.reduce_begin("b", comm_b, column_halves=True)
    dx, d_g1 = _in_bwd(du, dq, dk, dv, dg2, dfl, dx1, x, links.tie(norm1_g), wu, wqkv, wg2, wft, 512)
    links.advance(dx)
    small = (d_g1, d_bf[:, :N_HEADS], d_mix, d_scale, d_g2n, d_gf)
    return loss, dx, small


def kernel(x, norm1_g, w_in, b_forget, pool_mix, pool_scale, w_pool_out, w_attn_out, w_out, norm2_g, w_ffn_gate, w_ffn_up, w_ffn_down, norm_f_g, loss_target, m_norm1_g, m_w_in, m_b_forget, m_pool_mix, m_pool_scale, m_w_pool_out, m_w_attn_out, m_w_out, m_norm2_g, m_w_ffn_gate, m_w_ffn_up, m_w_ffn_down, m_norm_f_g, v_norm1_g, v_w_in, v_b_forget, v_pool_mix, v_pool_scale, v_w_pool_out, v_w_attn_out, v_w_out, v_norm2_g, v_w_ffn_gate, v_w_ffn_up, v_w_ffn_down, v_norm_f_g):
    nb, seq, d = x.shape
    group_a = ((w_ffn_gate, m_w_ffn_gate, v_w_ffn_gate, True, 9), (w_ffn_up, m_w_ffn_up, v_w_ffn_up, True, 10),
               (w_ffn_down, m_w_ffn_down, v_w_ffn_down, False, 11))
    group_m = ((w_pool_out, m_w_pool_out, v_w_pool_out, False, 5), (w_attn_out, m_w_attn_out, v_w_attn_out, False, 6),
               (w_out, m_w_out, v_w_out, False, 7))
    group_b = ((w_in, m_w_in, v_w_in, False, 1),)
    small_w = (norm1_g, b_forget, pool_mix, pool_scale, norm2_g, norm_f_g)
    small_m = (m_norm1_g, m_b_forget, m_pool_mix, m_pool_scale, m_norm2_g, m_norm_f_g)
    small_v = (v_norm1_g, v_b_forget, v_pool_mix, v_pool_scale, v_norm2_g, v_norm_f_g)
    small_pos = (0, 2, 3, 4, 8, 12)
    view = lambda a, tr: a[0].T if tr else a[0]
    unview = lambda a, tr, like: (a.T if tr else a).reshape(like.shape)

    def shard(w, tr):
        lw = view(w, tr).astype(BF16)
        return lw.reshape(2, lw.shape[0] // 2, lw.shape[1])

    cm = lambda a: jnp.transpose(a, (2, 0, 1))
    shard_in = _rows_to_bf16("w_in_to_bf16", cm(w_in))
    links = _MeshLinks([shard_in],
                       [shard(w_pool_out, False), shard(w_attn_out, False), shard(w_out, False),
                        shard(w_ffn_gate, True), shard(w_ffn_up, True), shard(w_ffn_down, False)])
    loss, dx, small_g = _local_step(
        links, x.reshape(nb * seq, d), loss_target.reshape(nb * seq, d), seq,
        norm1_g, b_forget, pool_mix[0], pool_scale, norm2_g, norm_f_g.reshape(1, d))

    grads, deltas, new_m, new_v = [None] * 13, [None] * 13, [None] * 13, [None] * 13
    pos_c = jnp.stack([lax.axis_index("c")]).astype(jnp.int32)

    def update(tag, group, dep):
        last = []
        for k, ((w, m, v, tr, pos), (mine, other)) in enumerate(zip(group, links.reduced(tag))):
            outs = _adamw_halves(f"adamw_{tag}{k}", pos_c, view(w, tr), mine, other, view(m, tr), view(v, tr), 256,
                                 dep=dep)
            grads[pos], deltas[pos], new_m[pos], new_v[pos] = (unview(a, tr, w) for a in outs)
            last.append(outs[1])
        return last

    last = update("a", group_a, links.token) + update("m", group_m, links.token)
    links.advance(last)
    small_sum = _all_reduce_small(links.tie(_pack_small(*small_g, extra=loss)))
    loss_out = small_sum[8, N_HEADS]
    dl, mn, vn = _adamw("adamw_small", _pack_small(*small_w), small_sum * _small_mask(), _pack_small(*small_m),
                        _pack_small(*small_v))
    for pos, g, a, b, e in zip(small_pos, _unpack_small(small_sum, small_w), _unpack_small(dl, small_w),
                               _unpack_small(mn, small_w), _unpack_small(vn, small_w)):
        grads[pos], deltas[pos], new_m[pos], new_v[pos] = g, a, b, e
    links.advance(dl)
    links.advance(links.token)
    (mine, other), = links.reduced("b")
    outs = _adamw_rows("adamw_b0", pos_c, cm(w_in), mine, other, cm(m_w_in), cm(v_w_in))
    grads[1], deltas[1], new_m[1], new_v[1] = (jnp.transpose(a, (1, 2, 0)) for a in outs)

    return (loss_out, dx.reshape(nb, seq, d), *grads, *deltas, *new_m, *new_v)


def _small_mask():
    rows = lax.broadcasted_iota(jnp.int32, (552, LANES), 0)
    lanes = lax.broadcasted_iota(jnp.int32, (552, LANES), 1)
    return jnp.where(jnp.logical_and(rows == 8, lanes == N_HEADS), 0.0, 1.0).astype(F32)
```

```python
import functools

import jax
import jax.numpy as jnp
from jax import lax
from jax.experimental import pallas as pl
from jax.experimental.pallas import tpu as pltpu

F32 = jnp.float32
BF16 = jnp.bfloat16

D_MODEL = 1024
POOL_WINDOWS = (2, 4, 8, 16)
POOL_GROUPS = 4
POOL_GROUP_DIM = 128
POOL_WIDTH = 512
HEAD_DIM = 64
N_HEADS = 8
ATTN_WIDTH = 512
D_FF = 2816
RMS_EPS = 1e-6
ATTN_SCALE = HEAD_DIM ** -0.5
NEG_BIG = -1e30

ADAM_LR = 0.001
ADAM_B1 = 0.9
ADAM_B2 = 0.999
ADAM_EPS = 1e-08
ADAM_WD = 0.01
ADAM_STEP = 10

LANES = 128
N_CHIPS = 4
N_DEV = 8
VMEM_LIMIT_V7X = 52 * 1024 * 1024
MESH = pl.DeviceIdType.MESH
ANY = pl.BlockSpec(memory_space=pl.ANY)


def _cparams(*sem):
    return pltpu.CompilerParams(dimension_semantics=sem if sem else None, vmem_limit_bytes=VMEM_LIMIT_V7X)


def _dep_list(dep):
    return [] if dep is None else (list(dep) if isinstance(dep, (list, tuple)) else [dep])


def _after(body, n_in, dep):
    k = len(_dep_list(dep))
    if k == 0:
        return body

    def wrapped(*refs):
        body(*refs[:n_in], *refs[n_in + k:])

    return wrapped


def _dep_args(dep):
    deps = _dep_list(dep)
    return [ANY] * len(deps), deps


def _dot(a, b):
    return lax.dot_general(a, b, (((1,), (0,)), ((), ())), preferred_element_type=F32)


def _dot_nt(a, b):
    return lax.dot_general(a, b, (((1,), (1,)), ((), ())), preferred_element_type=F32)


def _dot_tn(a, b):
    return lax.dot_general(a, b, (((0,), (0,)), ((), ())), preferred_element_type=F32)


def _sigmoid(x):
    return jax.nn.sigmoid(x)


def _rms_fwd(x, g):
    r = lax.rsqrt(jnp.mean(x * x, axis=-1, keepdims=True) + RMS_EPS)
    return (x * r) * g


def _rms_bwd(x, g, dy):
    r = lax.rsqrt(jnp.mean(x * x, axis=-1, keepdims=True) + RMS_EPS)
    xh = x * r
    dg = jnp.sum(dy * xh, axis=0, keepdims=True)
    dxh = dy * g
    dx = r * (dxh - xh * jnp.mean(dxh * xh, axis=-1, keepdims=True))
    return dx, dg


def _matmul(name, a, b, mode, out_dtype, tm, tn, tk, dep=None):
    if mode == "nn":
        (m, k), (_, n) = a.shape, b.shape
    elif mode == "nt":
        (m, k), (n, _) = a.shape, b.shape
    else:
        (k, m), (_, n) = a.shape, b.shape
    tm, tn, tk = min(tm, m), min(tn, n), min(tk, k)
    assert m % tm == 0 and n % tn == 0 and k % tk == 0, (name, m, n, k, tm, tn, tk)
    nk = k // tk
    if mode == "tn":
        a_spec = pl.BlockSpec((tk, tm), lambda i, j, kk: (kk, i))
    else:
        a_spec = pl.BlockSpec((tm, tk), lambda i, j, kk: (i, kk))
    if mode == "nt":
        b_spec = pl.BlockSpec((tn, tk), lambda i, j, kk: (j, kk))
    else:
        b_spec = pl.BlockSpec((tk, tn), lambda i, j, kk: (kk, j))
    dot = {"nn": _dot, "nt": _dot_nt, "tn": _dot_tn}[mode]
    use_scratch = nk > 1 and out_dtype != F32

    def body(a_ref, b_ref, o_ref, *scratch):
        prod = dot(a_ref[...].astype(BF16), b_ref[...].astype(BF16))
        if nk == 1:
            o_ref[...] = prod.astype(out_dtype)
            return
        acc = scratch[0] if use_scratch else o_ref
        kk = pl.program_id(2)

        @pl.when(kk == 0)
        def _():
            acc[...] = prod

        @pl.when(kk > 0)
        def _():
            acc[...] += prod

        if use_scratch:
            @pl.when(kk == nk - 1)
            def _():
                o_ref[...] = acc[...].astype(out_dtype)

    dep_specs, dep_ops = _dep_args(dep)
    return pl.pallas_call(
        _after(body, 2, dep),
        name=name,
        out_shape=jax.ShapeDtypeStruct((m, n), out_dtype),
        grid=(m // tm, n // tn, nk),
        in_specs=[a_spec, b_spec] + dep_specs,
        out_specs=pl.BlockSpec((tm, tn), lambda i, j, kk: (i, j)),
        scratch_shapes=[pltpu.VMEM((tm, tn), F32)] if use_scratch else [],
        compiler_params=_cparams("parallel", "parallel", "arbitrary"),
    )(a, b, *dep_ops)


def _norm_fwd(name, x, g, tm):
    t, d = x.shape
    tm = min(tm, t)

    def body(x_ref, g_ref, h_ref):
        h_ref[...] = _rms_fwd(x_ref[...], g_ref[...]).astype(BF16)

    return pl.pallas_call(
        body, name=name, out_shape=jax.ShapeDtypeStruct((t, d), BF16), grid=(t // tm,),
        in_specs=[pl.BlockSpec((tm, d), lambda i: (i, 0)), pl.BlockSpec((1, d), lambda i: (0, 0))],
        out_specs=pl.BlockSpec((tm, d), lambda i: (i, 0)),
        compiler_params=_cparams("parallel"),
    )(x, g)


def _split3(x):
    hi = x.astype(BF16)
    r1 = x - hi.astype(F32)
    mid = r1.astype(BF16)
    lo = (r1 - mid.astype(F32)).astype(BF16)
    return hi, mid, lo


def _tri_dot(tri, x):
    hi, mid, lo = _split3(x)
    return _dot(tri, hi) + _dot(tri, mid) + _dot(tri, lo)


def _forget_fwd(h, wf, bf, seq):
    t, d = h.shape
    cb = min(256, seq)

    def body(h_ref, wf_ref, bf_ref, fl_ref, fc_ref):
        fl = _dot(h_ref[...], wf_ref[...])
        fl_ref[...] = fl
        xx = fl + bf_ref[...]
        lf = jnp.minimum(xx, 0.0) - jnp.log(1.0 + jnp.exp(-jnp.abs(xx)))
        ri = lax.broadcasted_iota(jnp.int32, (cb, cb), 0)
        ci = lax.broadcasted_iota(jnp.int32, (cb, cb), 1)
        tri = (ri >= ci).astype(BF16)
        carry = jnp.zeros((1, LANES), F32)
        for blk in range(seq // cb):
            cs = _tri_dot(tri, lf[blk * cb:(blk + 1) * cb]) + carry
            fc_ref[blk * cb:(blk + 1) * cb, :] = cs
            carry = cs[cb - 1:cb, :]

    return pl.pallas_call(
        body, name="forget_fwd",
        out_shape=(jax.ShapeDtypeStruct((t, LANES), F32), jax.ShapeDtypeStruct((t, LANES), F32)),
        grid=(t // seq,),
        in_specs=[pl.BlockSpec((seq, d), lambda b: (b, 0)), pl.BlockSpec((d, LANES), lambda b: (0, 0)),
                  pl.BlockSpec((1, LANES), lambda b: (0, 0))],
        out_specs=(pl.BlockSpec((seq, LANES), lambda b: (b, 0)), pl.BlockSpec((seq, LANES), lambda b: (b, 0))),
        compiler_params=_cparams("parallel"),
    )(h, wf, bf)


def _pool_fwd(u, mix, scale, seq):
    t = u.shape[0]

    def body(u_ref, mix_ref, sc_ref, p_ref, ps_ref):
        tpos = lax.broadcasted_iota(jnp.int32, (seq, POOL_GROUP_DIM), 0)
        for g in range(POOL_GROUPS):
            sl = slice(g * POOL_GROUP_DIM, (g + 1) * POOL_GROUP_DIM)
            ug = u_ref[:, sl]
            s = ug
            for lvl in range(g + 1):
                d = 2 ** lvl
                s = s + jnp.where(tpos >= d, pltpu.roll(s, d, 0), 0.0)
            cnt = jnp.minimum(tpos + 1, POOL_WINDOWS[g]).astype(F32)
            pb = (s / cnt - ug).astype(BF16)
            p_ref[:, sl] = pb
            ps_ref[:, sl] = (_dot(pb, mix_ref[g]) * sc_ref[:, sl]).astype(BF16)

    return pl.pallas_call(
        body, name="pool_fwd",
        out_shape=(jax.ShapeDtypeStruct((t, POOL_WIDTH), BF16), jax.ShapeDtypeStruct((t, POOL_WIDTH), BF16)),
        grid=(t // seq,),
        in_specs=[pl.BlockSpec((seq, POOL_WIDTH), lambda b: (b, 0)),
                  pl.BlockSpec((POOL_GROUPS, POOL_GROUP_DIM, POOL_GROUP_DIM), lambda b: (0, 0, 0)),
                  pl.BlockSpec((1, POOL_WIDTH), lambda b: (0, 0))],
        out_specs=(pl.BlockSpec((seq, POOL_WIDTH), lambda b: (b, 0)), pl.BlockSpec((seq, POOL_WIDTH), lambda b: (b, 0))),
        compiler_params=_cparams("parallel"),
    )(u, mix, scale)


def _aug_constants():
    w = N_HEADS * LANES
    rows = jnp.arange(3 * LANES)
    piece, head = rows // LANES, rows % LANES
    cols = jnp.arange(w)
    live = (head < N_HEADS)[:, None]
    pq = (live & (cols[None, :] == (head * LANES + HEAD_DIM + piece)[:, None])).astype(BF16)
    pk = -(live & (cols[None, :] == (head * LANES + HEAD_DIM + 3 + piece)[:, None])).astype(BF16)
    lane = cols % LANES
    oq = ((lane >= HEAD_DIM + 3) & (lane < HEAD_DIM + 6)).astype(F32)[None, :]
    ok = ((lane >= HEAD_DIM) & (lane < HEAD_DIM + 3)).astype(F32)[None, :]
    return pq, pk, oq, ok


def _head_blocks(wt):
    d = wt.shape[1]
    return jnp.pad(wt.reshape(N_HEADS, HEAD_DIM, d), ((0, 0), (0, LANES - HEAD_DIM), (0, 0))).reshape(N_HEADS * LANES, d)


def _attn_prep(h, wq, wk, wv, fcum, tm):
    t, d = h.shape
    tm = min(tm, t)
    w = N_HEADS * LANES
    pq, pk, oq, ok = _aug_constants()

    def body(h_ref, wq_ref, wk_ref, wv_ref, f_ref, pq_ref, pk_ref, oq_ref, ok_ref, qa_ref, ka_ref, v_ref):
        hh = h_ref[...]
        fs = jnp.concatenate(_split3(f_ref[...]), axis=1)
        q = _dot_nt(hh, wq_ref[...]).astype(BF16).astype(F32) * ATTN_SCALE
        qa_ref[...] = (q + _dot(fs, pq_ref[...]) + oq_ref[...]).astype(BF16)
        k = _dot_nt(hh, wk_ref[...]).astype(BF16).astype(F32)
        ka_ref[...] = (k + _dot(fs, pk_ref[...]) + ok_ref[...]).astype(BF16)
        v_ref[...] = _dot_nt(hh, wv_ref[...]).astype(BF16)

    row = lambda n: pl.BlockSpec((tm, n), lambda i: (i, 0))
    full = lambda a: pl.BlockSpec(a.shape, lambda i: (0, 0))
    return pl.pallas_call(
        body, name="attn_prep",
        out_shape=(jax.ShapeDtypeStruct((t, w), BF16), jax.ShapeDtypeStruct((t, w), BF16),
                   jax.ShapeDtypeStruct((t, ATTN_WIDTH), BF16)),
        grid=(t // tm,),
        in_specs=[row(d), full(wq), full(wk), full(wv), row(LANES), full(pq), full(pk), full(oq), full(ok)],
        out_specs=(row(w), row(w), row(ATTN_WIDTH)),
        compiler_params=_cparams("parallel"),
    )(h, wq, wk, wv, fcum, pq, pk, oq, ok)


def _fold_lanes(x, op):
    out = x[:, :LANES]
    for g in range(1, x.shape[1] // LANES):
        out = op(out, x[:, g * LANES:(g + 1) * LANES])
    return out


def _causal_sweep(i, tile, carry):
    def pair(jj, c):
        return tile(2 * jj + 1, tile(2 * jj, c, False), False)

    carry = lax.fori_loop(0, i // 2, pair, carry)
    return lax.cond(i % 2 == 1, lambda c: tile(i, tile(i - 1, c, False), True), lambda c: tile(i, c, True), carry)


def _attn_fwd(qa, ka, v, seq, tq, dep=None):
    t = qa.shape[0]
    nq = seq // tq
    hp_n = N_HEADS // 2
    heads = [slice(e * LANES, (e + 1) * LANES) for e in range(2)]

    def body(q_ref, k_ref, v_ref, o_ref, lse_ref, s_buf):
        i = pl.program_id(2)
        diag_ok = lax.broadcasted_iota(jnp.int32, (tq, tq), 0) >= lax.broadcasted_iota(jnp.int32, (tq, tq), 1)
        qs = [q_ref[:, hl] for hl in heads]

        def sweep1(j, mxs, diagonal):
            r0 = pl.multiple_of(j * tq, tq)
            out = []
            for e, hl in enumerate(heads):
                s = _dot_nt(qs[e], k_ref[pl.ds(r0, tq), hl])
                if diagonal:
                    s = jnp.where(diag_ok, s, NEG_BIG)
                s_buf[e, j] = s
                out.append(jnp.maximum(mxs[e], _fold_lanes(s, jnp.maximum)))
            return tuple(out)

        mxs = _causal_sweep(i, sweep1, (jnp.full((tq, LANES), NEG_BIG, F32),) * 2)
        ms = [jnp.max(mx, axis=1, keepdims=True) for mx in mxs]

        def sweep2(j, carry, diagonal):
            r0 = pl.multiple_of(j * tq, tq)
            vv = v_ref[pl.ds(r0, tq), :]
            out = []
            for e in range(2):
                p = jnp.exp(s_buf[e, j] - ms[e])
                out += [carry[2 * e] + _fold_lanes(p, jnp.add), carry[2 * e + 1] + _dot(p.astype(BF16), vv)]
            return tuple(out)

        res = _causal_sweep(i, sweep2, (jnp.zeros((tq, LANES), F32),) * 4)
        outs = []
        for e in range(2):
            l = jnp.sum(res[2 * e], axis=1, keepdims=True)
            outs.append(res[2 * e + 1] / l)
            lse_ref[:, e:e + 1] = ms[e] + jnp.log(l)
        lane = lax.broadcasted_iota(jnp.int32, (tq, LANES), 1)
        o_ref[...] = jnp.where(lane < HEAD_DIM, outs[0], outs[1])

    dep_specs, dep_ops = _dep_args(dep)
    return pl.pallas_call(
        _after(body, 3, dep), name="attn_fwd",
        out_shape=(jax.ShapeDtypeStruct((t, ATTN_WIDTH), F32), jax.ShapeDtypeStruct((hp_n, t, 2), F32)),
        grid=(t // seq, hp_n, nq),
        in_specs=[pl.BlockSpec((tq, 2 * LANES), lambda b, hp, i: (b * nq + i, hp)),
                  pl.BlockSpec((seq, 2 * LANES), lambda b, hp, i: (b, hp)),
                  pl.BlockSpec((seq, LANES), lambda b, hp, i: (b, hp))] + dep_specs,
        out_specs=(pl.BlockSpec((tq, LANES), lambda b, hp, i: (b * nq + i, hp)),
                   pl.BlockSpec((None, tq, 2), lambda b, hp, i: (hp, b * nq + i, 0))),
        scratch_shapes=[pltpu.VMEM((2, nq, tq, tq), F32)],
        compiler_params=_cparams("parallel", "parallel", "arbitrary"),
    )(qa, ka, v, *dep_ops)


def _merge_fwd(x, ps, o, g2, wpo, wao, wout, tm):
    t, d = x.shape
    tm = min(tm, t)

    def body(x_ref, ps_ref, o_ref, gp_ref, ga_ref, wpo_ref, wao_ref, wout_ref, mg_ref, x1_ref):
        py = _dot(ps_ref[...], wpo_ref[...])
        ay = _dot(o_ref[...].astype(BF16), wao_ref[...])
        mb = (_sigmoid(gp_ref[...].astype(F32)) * py + _sigmoid(ga_ref[...].astype(F32)) * ay).astype(BF16)
        mg_ref[...] = mb
        x1_ref[...] = x_ref[...] + _dot(mb, wout_ref[...])

    row = lambda w: pl.BlockSpec((tm, w), lambda i: (i, 0))
    full = lambda a: pl.BlockSpec(a.shape, lambda i: (0, 0))
    return pl.pallas_call(
        body, name="merge_fwd",
        out_shape=(jax.ShapeDtypeStruct((t, d), BF16), jax.ShapeDtypeStruct((t, d), F32)),
        grid=(t // tm,),
        in_specs=[row(d), row(POOL_WIDTH), row(ATTN_WIDTH), pl.BlockSpec((tm, d), lambda i: (i, 0)),
                  pl.BlockSpec((tm, d), lambda i: (i, 1)), full(wpo), full(wao), full(wout)],
        out_specs=(row(d), row(d)),
        compiler_params=_cparams("parallel"),
    )(x, ps, o, g2, g2, wpo, wao, wout)


def _ffn_fwd(x1, g, wg, wu, wd, tm, tf):
    t, d = x1.shape
    f = wg.shape[0]
    tm = min(tm, t)
    nf = f // tf

    def body(x1_ref, g_ref, wg_ref, wu_ref, wd_ref, h2_ref, gt_ref, up_ref, act_ref, x2_ref):
        j = pl.program_id(1)

        @pl.when(j == 0)
        def _():
            h2_ref[...] = _rms_fwd(x1_ref[...], g_ref[...]).astype(BF16)

        h2 = h2_ref[...]
        gt = _dot_nt(h2, wg_ref[...])
        up = _dot_nt(h2, wu_ref[...])
        sg = _sigmoid(gt)
        silu = gt * sg
        act = (silu * up).astype(BF16)
        gt_ref[...] = (up * (sg * (1.0 + gt * (1.0 - sg)))).astype(BF16)
        up_ref[...] = silu.astype(BF16)
        act_ref[...] = act
        prod = _dot(act, wd_ref[...])

        @pl.when(j == 0)
        def _():
            x2_ref[...] = prod

        @pl.when(j > 0)
        def _():
            x2_ref[...] += prod

        @pl.when(j == nf - 1)
        def _():
            x2_ref[...] += x1_ref[...]

    return pl.pallas_call(
        body, name="ffn_fwd",
        out_shape=(jax.ShapeDtypeStruct((t, d), BF16), jax.ShapeDtypeStruct((t, f), BF16),
                   jax.ShapeDtypeStruct((t, f), BF16), jax.ShapeDtypeStruct((t, f), BF16),
                   jax.ShapeDtypeStruct((t, d), F32)),
        grid=(t // tm, nf),
        in_specs=[pl.BlockSpec((tm, d), lambda i, j: (i, 0)), pl.BlockSpec((1, d), lambda i, j: (0, 0)),
                  pl.BlockSpec((tf, d), lambda i, j: (j, 0)), pl.BlockSpec((tf, d), lambda i, j: (j, 0)),
                  pl.BlockSpec((tf, d), lambda i, j: (j, 0))],
        out_specs=(pl.BlockSpec((tm, d), lambda i, j: (i, 0)), pl.BlockSpec((tm, tf), lambda i, j: (i, j)),
                   pl.BlockSpec((tm, tf), lambda i, j: (i, j)), pl.BlockSpec((tm, tf), lambda i, j: (i, j)),
                   pl.BlockSpec((tm, d), lambda i, j: (i, 0))),
        compiler_params=_cparams("parallel", "arbitrary"),
    )(x1, g, wg, wu, wd)


def _final_fwd_bwd(x2, target, g, tm):
    t, d = x2.shape
    tm = min(tm, t)

    def body(x_ref, t_ref, g_ref, loss_ref, dx_ref, dg_ref):
        i = pl.program_id(0)
        x = x_ref[...]
        gg = g_ref[...]
        err = _rms_fwd(x, gg) - t_ref[...]
        part = 0.5 * jnp.sum(jnp.mean(err * err, axis=-1, keepdims=True), axis=0, keepdims=True)
        dx, dg = _rms_bwd(x, gg, err * (1.0 / d))
        dx_ref[...] = dx

        @pl.when(i == 0)
        def _():
            loss_ref[...] = jnp.zeros_like(loss_ref)
            dg_ref[...] = jnp.zeros_like(dg_ref)

        loss_ref[...] += jnp.broadcast_to(part, loss_ref.shape)
        dg_ref[...] += dg

    return pl.pallas_call(
        body, name="final_fwd_bwd",
        out_shape=(jax.ShapeDtypeStruct((1, LANES), F32), jax.ShapeDtypeStruct((t, d), F32),
                   jax.ShapeDtypeStruct((1, d), F32)),
        grid=(t // tm,),
        in_specs=[pl.BlockSpec((tm, d), lambda i: (i, 0)), pl.BlockSpec((tm, d), lambda i: (i, 0)),
                  pl.BlockSpec((1, d), lambda i: (0, 0))],
        out_specs=(pl.BlockSpec((1, LANES), lambda i: (0, 0)), pl.BlockSpec((tm, d), lambda i: (i, 0)),
                   pl.BlockSpec((1, d), lambda i: (0, 0))),
        compiler_params=_cparams("arbitrary"),
    )(x2, target, g)


def _ffn_bwd(dx2, x1, g, gt, up, wg, wu, wd, tm, tf):
    t, d = dx2.shape
    f = gt.shape[1]
    tm = min(tm, t)
    nf = f // tf
    wgu = jnp.concatenate([wg.reshape(nf, tf, d), wu.reshape(nf, tf, d)], axis=1).reshape(2 * f, d)

    def body(dx2_ref, x1_ref, g_ref, gt_ref, up_ref, wgu_ref, wd_ref, dgt_ref, dup_ref, dx1_ref, dg_ref, acc_ref,
             dxb_ref):
        i, j = pl.program_id(0), pl.program_id(1)

        @pl.when(j == 0)
        def _():
            dxb_ref[...] = dx2_ref[...].astype(BF16)

        dact = _dot_nt(dxb_ref[...], wd_ref[...])
        dgt = (dact * gt_ref[...].astype(F32)).astype(BF16)
        dup = (dact * up_ref[...].astype(F32)).astype(BF16)
        dgt_ref[...] = dgt
        dup_ref[...] = dup
        contrib = _dot(jnp.concatenate([dgt, dup], axis=1), wgu_ref[...])

        @pl.when(j == 0)
        def _():
            acc_ref[...] = contrib

        @pl.when(j > 0)
        def _():
            acc_ref[...] += contrib

        @pl.when(jnp.logical_and(i == 0, j == 0))
        def _():
            dg_ref[...] = jnp.zeros_like(dg_ref)

        @pl.when(j == nf - 1)
        def _():
            dxn, dg = _rms_bwd(x1_ref[...], g_ref[...], acc_ref[...])
            dx1_ref[...] = dx2_ref[...] + dxn
            dg_ref[...] += dg

    return pl.pallas_call(
        body, name="ffn_bwd",
        out_shape=(jax.ShapeDtypeStruct((t, f), BF16), jax.ShapeDtypeStruct((t, f), BF16),
                   jax.ShapeDtypeStruct((t, d), F32), jax.ShapeDtypeStruct((1, d), F32)),
        grid=(t // tm, nf),
        in_specs=[pl.BlockSpec((tm, d), lambda i, j: (i, 0)), pl.BlockSpec((tm, d), lambda i, j: (i, 0)),
                  pl.BlockSpec((1, d), lambda i, j: (0, 0)),
                  pl.BlockSpec((tm, tf), lambda i, j: (i, j)), pl.BlockSpec((tm, tf), lambda i, j: (i, j)),
                  pl.BlockSpec((2 * tf, d), lambda i, j: (j, 0)), pl.BlockSpec((tf, d), lambda i, j: (j, 0))],
        out_specs=(pl.BlockSpec((tm, tf), lambda i, j: (i, j)), pl.BlockSpec((tm, tf), lambda i, j: (i, j)),
                   pl.BlockSpec((tm, d), lambda i, j: (i, 0)), pl.BlockSpec((1, d), lambda i, j: (0, 0))),
        scratch_shapes=[pltpu.VMEM((tm, d), F32), pltpu.VMEM((tm, d), BF16)],
        compiler_params=_cparams("arbitrary", "arbitrary"),
    )(dx2, x1, g, gt, up, wgu, wd)


def _merge_bwd(dx1, ps, o, g2, wpo, wao, wout, tm, dep=None):
    t, d = dx1.shape
    tm = min(tm, t)

    def body(dx1_ref, ps_ref, o_ref, gp_ref, ga_ref, wpo_ref, wao_ref, wout_ref, dpy_ref, day_ref, dg2_ref, dps_ref, da_ref):
        dm = _dot_nt(dx1_ref[...].astype(BF16), wout_ref[...])
        py = _dot(ps_ref[...], wpo_ref[...])
        ay = _dot(o_ref[...].astype(BF16), wao_ref[...])
        sp = _sigmoid(gp_ref[...].astype(F32))
        sa = _sigmoid(ga_ref[...].astype(F32))
        dpy = (dm * sp).astype(BF16)
        day = (dm * sa).astype(BF16)
        dpy_ref[...] = dpy
        day_ref[...] = day
        dg2_ref[:, :d] = (dm * py * (sp * (1.0 - sp))).astype(BF16)
        dg2_ref[:, d:] = (dm * ay * (sa * (1.0 - sa))).astype(BF16)
        dps_ref[...] = _dot_nt(dpy, wpo_ref[...])
        da_ref[...] = _dot_nt(day, wao_ref[...]).astype(BF16)

    row = lambda w: pl.BlockSpec((tm, w), lambda i: (i, 0))
    full = lambda a: pl.BlockSpec(a.shape, lambda i: (0, 0))
    dep_specs, dep_ops = _dep_args(dep)
    return pl.pallas_call(
        _after(body, 8, dep), name="merge_bwd",
        out_shape=(jax.ShapeDtypeStruct((t, d), BF16), jax.ShapeDtypeStruct((t, d), BF16),
                   jax.ShapeDtypeStruct((t, 2 * d), BF16), jax.ShapeDtypeStruct((t, POOL_WIDTH), F32),
                   jax.ShapeDtypeStruct((t, ATTN_WIDTH), BF16)),
        grid=(t // tm,),
        in_specs=[row(d), row(POOL_WIDTH), row(ATTN_WIDTH), pl.BlockSpec((tm, d), lambda i: (i, 0)),
                  pl.BlockSpec((tm, d), lambda i: (i, 1)), full(wpo), full(wao), full(wout)] + dep_specs,
        out_specs=(row(d), row(d), row(2 * d), row(POOL_WIDTH), row(ATTN_WIDTH)),
        compiler_params=_cparams("parallel"),
    )(dx1, ps, o, g2, g2, wpo, wao, wout, *dep_ops)


def _attn_bwd(qa, ka, v, do, lse4, seq, tq, dep=None):
    t = qa.shape[0]
    nq = seq // tq
    hp_n = N_HEADS // 2
    heads = [slice(e * LANES, (e + 1) * LANES) for e in range(2)]

    def body(q_ref, k_ref, v_ref, do_ref, lse_ref, dq_ref, dk_ref, dv_ref, dfr_ref, dk_acc, dv_acc, p_buf, dp_buf):
        diag_ok = lax.broadcasted_iota(jnp.int32, (tq, tq), 0) >= lax.broadcasted_iota(jnp.int32, (tq, tq), 1)
        lane_q = lax.broadcasted_iota(jnp.int32, (tq, LANES), 1)
        lane_s = lax.broadcasted_iota(jnp.int32, (seq, LANES), 1)
        mine_q = [lane_q < HEAD_DIM, lane_q >= HEAD_DIM]
        dv_acc[...] = jnp.zeros_like(dv_acc)
        dk_acc[...] = jnp.zeros_like(dk_acc)
        dfr_ref[...] = jnp.zeros_like(dfr_ref)

        def q_step(i, _):
            q0 = pl.multiple_of(i * tq, tq)
            qs = [q_ref[pl.ds(q0, tq), hl] for hl in heads]
            dov = do_ref[pl.ds(q0, tq), :]
            dos = [jnp.where(mq, dov, jnp.zeros((), BF16)) for mq in mine_q]
            lss = [lse_ref[pl.ds(q0, tq), e:e + 1] for e in range(2)]

            def sweep1(j, dls, diagonal):
                r0 = pl.multiple_of(j * tq, tq)
                vv = v_ref[pl.ds(r0, tq), :]
                out = []
                for e, hl in enumerate(heads):
                    s = _dot_nt(qs[e], k_ref[pl.ds(r0, tq), hl])
                    if diagonal:
                        s = jnp.where(diag_ok, s, NEG_BIG)
                    p = jnp.exp(s - lss[e])
                    dp = _dot_nt(dos[e], vv)
                    p_buf[e, j] = p
                    dp_buf[e, j] = dp
                    dv_acc[pl.ds(r0, tq), :] += _dot_tn(p.astype(BF16), dos[e])
                    out.append(dls[e] + _fold_lanes(p * dp, jnp.add))
                return tuple(out)

            dls = _causal_sweep(i, sweep1, (jnp.zeros((tq, LANES), F32),) * 2)
            dls = [jnp.sum(d, axis=1, keepdims=True) for d in dls]

            def sweep2(j, dqs, diagonal):
                r0 = pl.multiple_of(j * tq, tq)
                out = []
                for e, hl in enumerate(heads):
                    ds = p_buf[e, j] * (dp_buf[e, j] - dls[e])
                    dfr_ref[e, pl.ds(j, 1), :] += jnp.sum(ds, axis=0, keepdims=True)
                    dsb = ds.astype(BF16)
                    dk_acc[e, pl.ds(r0, tq), :] += _dot_tn(dsb, qs[e])
                    out.append(dqs[e] + _dot(dsb, k_ref[pl.ds(r0, tq), hl]))
                return tuple(out)

            dqs = _causal_sweep(i, sweep2, (jnp.zeros((tq, LANES), F32),) * 2)
            dq = jnp.where(mine_q[0], dqs[0], pltpu.roll(dqs[1], HEAD_DIM, 1)) * ATTN_SCALE
            dq_ref[pl.ds(q0, tq), :] = dq.astype(BF16)
            return 0

        lax.fori_loop(0, nq, q_step, 0)
        dk_ref[...] = jnp.where(lane_s < HEAD_DIM, dk_acc[0], pltpu.roll(dk_acc[1], HEAD_DIM, 1)).astype(BF16)
        dv_ref[...] = dv_acc[...].astype(BF16)

    wide = pl.BlockSpec((seq, 2 * LANES), lambda b, hp: (b, hp))
    col = pl.BlockSpec((seq, LANES), lambda b, hp: (b, hp))
    pair = pl.BlockSpec((None, seq, 2), lambda b, hp: (hp, b, 0))
    dep_specs, dep_ops = _dep_args(dep)
    return pl.pallas_call(
        _after(body, 5, dep), name="attn_bwd",
        out_shape=(jax.ShapeDtypeStruct((t, ATTN_WIDTH), BF16),) * 3 + (jax.ShapeDtypeStruct((N_HEADS, t // tq, tq), F32),),
        grid=(t // seq, hp_n),
        in_specs=[wide, wide, col, col, pair] + dep_specs,
        out_specs=(col, col, col, pl.BlockSpec((2, nq, tq), lambda b, hp: (hp, b, 0))),
        scratch_shapes=[pltpu.VMEM((2, seq, LANES), F32), pltpu.VMEM((seq, LANES), F32),
                        pltpu.VMEM((2, nq, tq, tq), F32), pltpu.VMEM((2, nq, tq, tq), F32)],
        compiler_params=_cparams("parallel", "arbitrary"),
    )(qa, ka, v, do, lse4, *dep_ops)


def _forget_bwd(dfc, fl, bf, seq):
    t = fl.shape[0]
    cb = min(256, seq)
    nb = seq // cb

    def body(dfc_ref, fl_ref, bf_ref, dfl_ref, db_ref):
        b = pl.program_id(0)
        ri = lax.broadcasted_iota(jnp.int32, (cb, cb), 0)
        ci = lax.broadcasted_iota(jnp.int32, (cb, cb), 1)
        tri = (ci >= ri).astype(BF16)
        carry = jnp.zeros((1, LANES), F32)
        dbs = jnp.zeros((1, LANES), F32)
        for blk in reversed(range(nb)):
            rs = slice(blk * cb, (blk + 1) * cb)
            dlf = _tri_dot(tri, -dfc_ref[rs, :]) + carry
            carry = dlf[0:1, :]
            dfl = dlf * _sigmoid(-(fl_ref[rs, :] + bf_ref[...]))
            dfl_ref[rs, :] = dfl.astype(BF16)
            dbs = dbs + jnp.sum(dfl, axis=0, keepdims=True)

        @pl.when(b == 0)
        def _():
            db_ref[...] = jnp.zeros_like(db_ref)

        db_ref[...] += dbs

    return pl.pallas_call(
        body, name="forget_bwd",
        out_shape=(jax.ShapeDtypeStruct((t, LANES), BF16), jax.ShapeDtypeStruct((1, LANES), F32)),
        grid=(t // seq,),
        in_specs=[pl.BlockSpec((seq, LANES), lambda b: (b, 0)), pl.BlockSpec((seq, LANES), lambda b: (b, 0)),
                  pl.BlockSpec((1, LANES), lambda b: (0, 0))],
        out_specs=(pl.BlockSpec((seq, LANES), lambda b: (b, 0)), pl.BlockSpec((1, LANES), lambda b: (0, 0))),
        compiler_params=_cparams("arbitrary"),
    )(dfc, fl, bf)


def _pool_bwd(dps, p, mix, scale, seq):
    t = dps.shape[0]

    def body(dps_ref, p_ref, mix_ref, sc_ref, du_ref, dmix_ref, dsc_ref):
        b = pl.program_id(0)

        @pl.when(b == 0)
        def _():
            dmix_ref[...] = jnp.zeros_like(dmix_ref)
            dsc_ref[...] = jnp.zeros_like(dsc_ref)

        tpos = lax.broadcasted_iota(jnp.int32, (seq, POOL_GROUP_DIM), 0)
        for g in range(POOL_GROUPS):
            sl = slice(g * POOL_GROUP_DIM, (g + 1) * POOL_GROUP_DIM)
            pb = p_ref[:, sl]
            dpsg = dps_ref[:, sl]
            pm = _dot(pb, mix_ref[g])
            dsc_ref[:, sl] += jnp.sum(dpsg * pm, axis=0, keepdims=True)
            dpm = (dpsg * sc_ref[:, sl]).astype(BF16)
            dmix_ref[g] += _dot_tn(pb, dpm)
            dp = _dot_nt(dpm, mix_ref[g])
            cnt = jnp.minimum(tpos + 1, POOL_WINDOWS[g]).astype(F32)
            s = dp / cnt
            for lvl in range(g + 1):
                d = 2 ** lvl
                s = s + jnp.where(tpos < seq - d, pltpu.roll(s, seq - d, 0), 0.0)
            du_ref[:, sl] = (s - dp).astype(BF16)

    return pl.pallas_call(
        body, name="pool_bwd",
        out_shape=(jax.ShapeDtypeStruct((t, POOL_WIDTH), BF16),
                   jax.ShapeDtypeStruct((POOL_GROUPS, POOL_GROUP_DIM, POOL_GROUP_DIM), F32),
                   jax.ShapeDtypeStruct((1, POOL_WIDTH), F32)),
        grid=(t // seq,),
        in_specs=[pl.BlockSpec((seq, POOL_WIDTH), lambda b: (b, 0)), pl.BlockSpec((seq, POOL_WIDTH), lambda b: (b, 0)),
                  pl.BlockSpec((POOL_GROUPS, POOL_GROUP_DIM, POOL_GROUP_DIM), lambda b: (0, 0, 0)),
                  pl.BlockSpec((1, POOL_WIDTH), lambda b: (0, 0))],
        out_specs=(pl.BlockSpec((seq, POOL_WIDTH), lambda b: (b, 0)),
                   pl.BlockSpec((POOL_GROUPS, POOL_GROUP_DIM, POOL_GROUP_DIM), lambda b: (0, 0, 0)),
                   pl.BlockSpec((1, POOL_WIDTH), lambda b: (0, 0))),
        compiler_params=_cparams("arbitrary"),
    )(dps, p, mix, scale)


def _in_bwd(du, dq, dk, dv, dg2, dfl, dx1, x, g, wu, wqkv, wg2, wft, tm):
    t, d = x.shape
    tm = min(tm, t)
    aw = ATTN_WIDTH

    def body(du_ref, dq_ref, dk_ref, dv_ref, dg2_ref, dfl_ref, dx1_ref, x_ref, g_ref, wu_ref, wqkv_ref, wg2_ref, wft_ref,
             dx_ref, dg_ref):
        i = pl.program_id(0)
        dh = _dot(du_ref[...], wu_ref[...])
        dh += _dot(dq_ref[...], wqkv_ref[0:aw, :])
        dh += _dot(dk_ref[...], wqkv_ref[aw:2 * aw, :])
        dh += _dot(dv_ref[...], wqkv_ref[2 * aw:3 * aw, :])
        dh += _dot(dg2_ref[...], wg2_ref[...])
        dh += _dot(dfl_ref[...], wft_ref[...])
        dxn, dg = _rms_bwd(x_ref[...], g_ref[...], dh)
        dx_ref[...] = dx1_ref[...] + dxn

        @pl.when(i == 0)
        def _():
            dg_ref[...] = jnp.zeros_like(dg_ref)

        dg_ref[...] += dg

    row = lambda w: pl.BlockSpec((tm, w), lambda i: (i, 0))
    full = lambda a: pl.BlockSpec(a.shape, lambda i: (0, 0))
    return pl.pallas_call(
        body, name="in_bwd",
        out_shape=(jax.ShapeDtypeStruct((t, d), F32), jax.ShapeDtypeStruct((1, d), F32)),
        grid=(t // tm,),
        in_specs=[row(POOL_WIDTH), row(aw), row(aw), row(aw), row(2 * d), row(LANES), row(d), row(d),
                  pl.BlockSpec((1, d), lambda i: (0, 0)), full(wu), full(wqkv), full(wg2), full(wft)],
        out_specs=(row(d), pl.BlockSpec((1, d), lambda i: (0, 0))),
        compiler_params=_cparams("arbitrary"),
    )(du, dq, dk, dv, dg2, dfl, dx1, x, g, wu, wqkv, wg2, wft)


def _position():
    return lax.axis_index("x"), lax.axis_index("y"), lax.axis_index("c")


def _remote(src, dst, send_sem, recv_sem, device):
    return pltpu.make_async_remote_copy(src_ref=src, dst_ref=dst, send_sem=send_sem, recv_sem=recv_sem,
                                        device_id=device, device_id_type=MESH)


HBM = pl.BlockSpec(memory_space=pltpu.HBM)
SEM = pl.BlockSpec(memory_space=pltpu.SEMAPHORE)
DATAFLOW = pltpu.SideEffectType.DATAFLOW_SIDE_EFFECTING


def _copies_start(name, arrays, plan, m, dep=None):
    n = len(arrays)
    arrays = [pltpu.with_memory_space_constraint(a, pltpu.HBM) for a in arrays]

    def body(*refs):
        ins, send_sem, recv_sem, token = refs[:n], refs[n], refs[n + 1], refs[2 * n + 2]
        for i, (src, dst, device, _) in enumerate(plan(ins, *_position())):
            _remote(src, dst, send_sem.at[i], recv_sem.at[i], device).start()
        token[...] = jnp.zeros_like(token)

    dep_specs, dep_ops = _dep_args(dep)
    outs = pl.pallas_call(
        _after(body, n, dep), name=name,
        out_shape=(pltpu.SemaphoreType.DMA((m,)), pltpu.SemaphoreType.DMA((m,)),
                   *[pltpu.HBM(a.shape, a.dtype) for a in arrays], jax.ShapeDtypeStruct((8, LANES), F32)),
        in_specs=[HBM] * n + dep_specs, out_specs=(SEM, SEM, *[HBM] * n, pl.BlockSpec(memory_space=pltpu.VMEM)),
        input_output_aliases={i: i + 2 for i in range(n)},
        compiler_params=pltpu.CompilerParams(has_side_effects=DATAFLOW),
    )(*arrays, *dep_ops)
    return (outs[0], outs[1]), list(outs[2:2 + n]), outs[2 + n]


def _copies_wait(name, sems, arrays, plan, after):
    n = len(arrays)
    afters = list(after) if isinstance(after, (list, tuple)) else [after]

    def body(*refs):
        ins, send_sem, recv_sem = refs[:n], refs[n], refs[n + 1]
        for i, (src, dst, device, landing) in enumerate(plan(ins, *_position())):
            _remote(src, dst, send_sem.at[i], recv_sem.at[i], device).wait_send()
            _remote(landing, landing, send_sem.at[i], recv_sem.at[i], device).wait_recv()

    outs = pl.pallas_call(
        body, name=name,
        out_shape=tuple(pltpu.HBM(a.shape, a.dtype) for a in arrays),
        in_specs=[HBM] * n + [SEM, SEM] + [ANY] * len(afters), out_specs=tuple([HBM] * n),
        input_output_aliases={i: i for i in range(n)},
        compiler_params=pltpu.CompilerParams(has_side_effects=DATAFLOW),
    )(*arrays, sems[0], sems[1], *afters)
    return list(outs)


def _tie(x, dep):
    for token in _dep_list(dep):
        x = x + token[0, 0]
    return x


def _other_chips(x, y):
    return [(1 - x, y), (x, 1 - y), (1 - x, 1 - y)]


def _gather_begin(tag, shards, token, column_halves=False):
    n = len(shards)
    lands = [lax.empty((N_CHIPS,) + s.shape, s.dtype) for s in shards]
    if column_halves:
        cols = lambda ref, h: pl.ds(pl.multiple_of(h * (ref.shape[-1] // 2), LANES), ref.shape[-1] // 2)
        mine = lambda ref, h: ref.at[:, cols(ref, h)]
        landed = lambda ref, chip, h: ref.at[chip, :, cols(ref, h)]
    else:
        mine = lambda ref, h: ref.at[h]
        landed = lambda ref, chip, h: ref.at[chip, h]

    def plan(refs, x, y, c):
        return [(mine(refs[k], c), landed(refs[n + k], 2 * x + y, c), (ox, oy, c), landed(refs[n + k], 2 * ox + oy, c))
                for k in range(n) for ox, oy in _other_chips(x, y)]

    sems, thru, token = _copies_start(f"gather_{tag}_ici_start", list(shards) + lands, plan, 3 * n, dep=token)
    return dict(tag=tag, n=n, plan=plan, sems=sems, arrays=thru, token=token, landed=landed)


def _gather_forward(st, after):
    n, tag, landed = st["n"], st["tag"], st["landed"]
    thru = _copies_wait(f"gather_{tag}_ici_wait", st["sems"], st["arrays"], st["plan"], after)

    def plan(refs, x, y, c):
        return [(landed(refs[k], 2 * ox + oy, c), landed(refs[k], 2 * ox + oy, c), (x, y, 1 - c),
                 landed(refs[k], 2 * ox + oy, 1 - c))
                for k in range(n) for ox, oy in _other_chips(x, y)]

    sems, lands, token = _copies_start(f"gather_{tag}_fwd_start", thru[n:], plan, 3 * n)
    return dict(tag=tag, n=n, plan=plan, sems=sems, arrays=lands, token=token, shards=thru[:n])


def _gather_end(st, after, merge=True):
    lands = _copies_wait(f"gather_{st['tag']}_fwd_wait", st["sems"], st["arrays"], st["plan"], after)
    if not merge:
        return lands, st["shards"]
    me = 2 * lax.axis_index("x") + lax.axis_index("y")
    return [lax.dynamic_update_index_in_dim(g, s, me, 0) for g, s in zip(lands, st["shards"])]


def _add_keep_give(name, pos, a, a_keep, a_give, b, b_keep, b_give, steps):
    r, c = b.shape[-2:]

    def spec(arr, fn):
        lead = arr.ndim - 2

        def index(i, p):
            idx = tuple(fn(i, p))
            return idx if len(idx) == arr.ndim else idx + (0, 0)

        return pl.BlockSpec((None,) * lead + (r, c), index)

    out_spec = pl.BlockSpec((None, r, c), lambda i, p: (i, 0, 0))

    def body(p_ref, ak_ref, bk_ref, ag_ref, bg_ref, keep_ref, give_ref):
        keep_ref[...] = ak_ref[...] + bk_ref[...].astype(F32)
        give_ref[...] = (ag_ref[...] + bg_ref[...].astype(F32)).astype(BF16)

    return pl.pallas_call(
        body, name=name,
        out_shape=(jax.ShapeDtypeStruct((steps, r, c), F32), jax.ShapeDtypeStruct((steps, r, c), BF16)),
        grid_spec=pltpu.PrefetchScalarGridSpec(
            num_scalar_prefetch=1, grid=(steps,),
            in_specs=[spec(a, a_keep), spec(b, b_keep), spec(a, a_give), spec(b, b_give)],
            out_specs=(out_spec, out_spec)),
        compiler_params=_cparams("parallel"),
    )(pos, a, b, a, b)


def _add_last(name, a, b):
    _, r, c = a.shape
    blk = pl.BlockSpec((None, r, c), lambda i: (0, 0, 0))

    def body(a_ref, b_ref, o_ref):
        o_ref[...] = a_ref[...] + b_ref[...].astype(F32)

    return pl.pallas_call(
        body, name=name, out_shape=jax.ShapeDtypeStruct((r, c), F32), grid=(1,), in_specs=[blk, blk],
        out_specs=pl.BlockSpec((r, c), lambda i: (0, 0)), compiler_params=_cparams("arbitrary"),
    )(a, b)


def _exchange_begin(tag, stage, gives, lands, peer_fn, extra):
    n = len(gives)

    def plan(refs, x, y, c):
        return [(refs[k], refs[n + k], peer_fn(x, y, c), refs[n + k]) for k in range(n)]

    sems, thru, token = _copies_start(f"rs{tag}_{stage}_start", gives + lands, plan, n)
    return dict(extra, tag=tag, n=n, stage=stage, plan=plan, sems=sems, arrays=thru, token=token)


def _reduce_begin(tag, grads, column_halves=False):
    n = len(grads)
    if column_halves:
        half = lambda ref, j, h: ref.at[j, :, pl.ds(pl.multiple_of(h * (ref.shape[2] // 2), LANES), ref.shape[2] // 2)]
        lands = [lax.empty((N_CHIPS, g.shape[1], g.shape[2] // 2), F32) for g in grads]
    else:
        half = lambda ref, j, h: ref.at[j, h]
        lands = [lax.empty((N_CHIPS,) + g.shape[2:], F32) for g in grads]

    def plan(refs, x, y, c):
        return [(half(refs[k], j, 1 - c), refs[n + k].at[j], (x, y, 1 - c), refs[n + k].at[j])
                for k in range(n) for j in range(N_CHIPS)]

    sems, thru, token = _copies_start(f"rs{tag}_c_start", list(grads) + lands, plan, N_CHIPS * n)
    return dict(tag=tag, n=n, stage="c", plan=plan, sems=sems, arrays=thru, token=token, column_halves=column_halves)


def _reduce_advance(st, after):
    tag, n, stage = st["tag"], st["n"], st["stage"]
    thru = _copies_wait(f"rs{tag}_{stage}_wait", st["sems"], st["arrays"], st["plan"], after)
    first, recv = thru[:n], thru[n:]
    x, y, c = _position()
    if stage == "c":
        pos = jnp.stack([c, x]).astype(jnp.int32)
        if st["column_halves"]:
            mine = lambda chip: (lambda i, p: (chip(p) + i, 0, p[0]))
        else:
            mine = lambda chip: (lambda i, p: (chip(p) + i, p[0]))
        sums = [_add_keep_give(
            f"rs{tag}_c_add{k}", pos,
            first[k], mine(lambda p: 2 * p[1]), mine(lambda p: 2 * (1 - p[1])),
            recv[k], lambda i, p: (2 * p[1] + i,), lambda i, p: (2 * (1 - p[1]) + i,), 2) for k in range(n)]
        lands = [lax.empty(s[1].shape, BF16) for s in sums]
        return _exchange_begin(tag, "x", [s[1] for s in sums], lands, lambda x, y, c: (1 - x, y, c),
                               dict(keep=[s[0] for s in sums]))
    if stage == "x":
        pos = jnp.stack([y]).astype(jnp.int32)
        sums = [_add_keep_give(
            f"rs{tag}_x_add{k}", pos,
            st["keep"][k], lambda i, p: (p[0],), lambda i, p: (1 - p[0],),
            recv[k], lambda i, p: (p[0],), lambda i, p: (1 - p[0],), 1) for k in range(n)]
        lands = [lax.empty(s[1].shape, BF16) for s in sums]
        return _exchange_begin(tag, "y", [s[1] for s in sums], lands, lambda x, y, c: (x, 1 - y, c),
                               dict(keep=[s[0] for s in sums]))
    if stage == "y":
        mine = [_add_last(f"rs{tag}_y_add{k}", st["keep"][k], recv[k]) for k in range(n)]
        lands = [lax.empty(m.shape, F32) for m in mine]
        return _exchange_begin(tag, "swap", mine, lands, lambda x, y, c: (x, y, 1 - c), {})
    return dict(done=list(zip(first, recv)), token=None)


def _small_begin(v, dep):
    land = lax.empty((N_DEV,) + v.shape, F32)
    flips = [(fx, fy, fc) for fx in (0, 1) for fy in (0, 1) for fc in (0, 1)][1:]

    def plan(refs, x, y, c):
        copies = []
        for fx, fy, fc in flips:
            px, py, pc = (1 - x if fx else x), (1 - y if fy else y), (1 - c if fc else c)
            copies.append((refs[0], refs[1].at[4 * x + 2 * y + c], (px, py, pc), refs[1].at[4 * px + 2 * py + pc]))
        return copies

    sems, thru, token = _copies_start("small_start", [v, land], plan, len(flips), dep=dep)
    return dict(plan=plan, sems=sems, arrays=thru, token=token)


def _small_end(st, after):
    own, land = _copies_wait("small_wait", st["sems"], st["arrays"], st["plan"], after)
    x, y, c = _position()
    me = jnp.stack([4 * x + 2 * y + c]).astype(jnp.int32)

    def body(me_ref, own_ref, land_ref, out_ref):
        term = lambda dev: jnp.where(me_ref[0] == dev, own_ref[...], land_ref[dev])
        acc = term(0)
        for dev in range(1, N_DEV):
            acc = acc + term(dev)
        out_ref[...] = acc

    return pl.pallas_call(
        body, name="small_sum", out_shape=jax.ShapeDtypeStruct(own.shape, F32),
        grid_spec=pltpu.PrefetchScalarGridSpec(
            num_scalar_prefetch=1, grid=(1,),
            in_specs=[pl.BlockSpec(own.shape, lambda i, m: (0, 0)), pl.BlockSpec(land.shape, lambda i, m: (0, 0, 0))],
            out_specs=pl.BlockSpec(own.shape, lambda i, m: (0, 0))),
        compiler_params=_cparams("arbitrary"),
    )(me, own, land)


def _all_reduce_small(v):
    r = v.shape[0]

    def body(v_ref, out_ref, buf, send_sems, recv_sems, local_sem):
        x, y, c = _position()
        me, sibling = (x, y, c), (x, y, 1 - c)
        chips = [(1 - x, y), (x, 1 - y), (1 - x, 1 - y)]

        def rows(px, py, pc):
            return buf.at[pl.ds((4 * px + 2 * py + pc) * r, r), :]

        def copy(k, block, to, src=None):
            return _remote(rows(*block) if src is None else src, rows(*block), send_sems.at[k], recv_sems.at[k], to)

        mine = pltpu.make_async_copy(v_ref, rows(*me), local_sem)
        mine.start()
        first = [copy(0, me, sibling, src=v_ref)]
        first += [copy(1 + j, me, (*chip, c), src=v_ref) for j, chip in enumerate(chips)]
        for cp in first:
            cp.start()
        passed = [copy(4 + j, (*chip, c), sibling) for j, chip in enumerate(chips)]
        for j, chip in enumerate(chips):
            copy(1 + j, (*chip, c), me).wait_recv()
            passed[j].start()
        copy(0, sibling, me).wait_recv()
        for j, chip in enumerate(chips):
            copy(4 + j, (*chip, 1 - c), me).wait_recv()
        for cp in first + passed:
            cp.wait_send()
        mine.wait()
        acc = buf[0:r, :]
        for dev in range(1, N_DEV):
            acc = acc + buf[dev * r:(dev + 1) * r, :]
        out_ref[...] = acc

    return pl.pallas_call(
        body, name="all_reduce_small",
        out_shape=jax.ShapeDtypeStruct(v.shape, F32),
        in_specs=[pl.BlockSpec(memory_space=pltpu.VMEM)],
        out_specs=pl.BlockSpec(memory_space=pltpu.VMEM),
        scratch_shapes=[pltpu.VMEM((N_DEV * r, LANES), F32), pltpu.SemaphoreType.DMA((7,)),
                        pltpu.SemaphoreType.DMA((7,)), pltpu.SemaphoreType.DMA],
        compiler_params=pltpu.CompilerParams(has_side_effects=True, vmem_limit_bytes=VMEM_LIMIT_V7X),
    )(v)


def _adamw_update(w, gg, m, v):
    mn = ADAM_B1 * m + (1.0 - ADAM_B1) * gg
    vn = ADAM_B2 * v + (1.0 - ADAM_B2) * (gg * gg)
    m_hat = mn / (1.0 - ADAM_B1 ** ADAM_STEP)
    v_hat = vn / (1.0 - ADAM_B2 ** ADAM_STEP)
    return -ADAM_LR * (m_hat / (jnp.sqrt(v_hat) + ADAM_EPS) + ADAM_WD * w), mn, vn


def _adamw(name, w, g, m, v):
    def body(w_ref, g_ref, m_ref, v_ref, d_ref, mo_ref, vo_ref):
        d_ref[...], mo_ref[...], vo_ref[...] = _adamw_update(w_ref[...], g_ref[...], m_ref[...], v_ref[...])

    blk = pl.BlockSpec(w.shape, lambda i: (0, 0))
    return pl.pallas_call(
        body, name=name, out_shape=(jax.ShapeDtypeStruct(w.shape, F32),) * 3, grid=(1,),
        in_specs=[blk] * 4, out_specs=(blk,) * 3, compiler_params=_cparams("arbitrary"),
    )(w, g, m, v)


def _rows_to_bf16(name, w):
    r, _, c = w.shape

    def body(w_ref, o_ref):
        o_ref[...] = w_ref[:, 0, :].astype(BF16)

    return pl.pallas_call(
        body, name=name, out_shape=jax.ShapeDtypeStruct((r, c), BF16), grid=(1,),
        in_specs=[pl.BlockSpec((r, 1, c), lambda i: (0, 0, 0))], out_specs=pl.BlockSpec((r, c), lambda i: (0, 0)),
        compiler_params=_cparams("arbitrary"),
    )(w)


def _adamw_rows(name, pos_c, w, g_mine, g_other, m, v):
    r, _, c = w.shape
    ch = c // 2

    def body(p_ref, w_ref, gm_ref, go_ref, m_ref, v_ref, g_ref, d_ref, mo_ref, vo_ref):
        gg = jnp.where(pl.program_id(0) == p_ref[0], gm_ref[...], go_ref[...])
        dl, mn, vn = _adamw_update(w_ref[:, 0, :], gg, m_ref[:, 0, :], v_ref[:, 0, :])
        g_ref[:, 0, :] = gg
        d_ref[:, 0, :] = dl
        mo_ref[:, 0, :] = mn
        vo_ref[:, 0, :] = vn

    rows = pl.BlockSpec((r, 1, ch), lambda h, p: (0, 0, h))
    half = pl.BlockSpec((r, ch), lambda h, p: (0, 0))
    return pl.pallas_call(
        body, name=name, out_shape=(jax.ShapeDtypeStruct(w.shape, F32),) * 4,
        grid_spec=pltpu.PrefetchScalarGridSpec(
            num_scalar_prefetch=1, grid=(2,), in_specs=[rows, half, half, rows, rows], out_specs=(rows,) * 4),
        compiler_params=_cparams("parallel"),
    )(pos_c, w, g_mine, g_other, m, v)


def _adamw_halves(name, pos_c, w, g_mine, g_other, m, v, tr, dep=None):
    r, c = w.shape
    rh = r // 2
    tr = tr if rh % tr == 0 else rh
    nt = rh // tr

    def body(p_ref, w_ref, gm_ref, go_ref, m_ref, v_ref, g_ref, d_ref, mo_ref, vo_ref):
        gg = jnp.where(pl.program_id(0) == p_ref[0], gm_ref[...], go_ref[...])
        g_ref[...] = gg
        d_ref[...], mo_ref[...], vo_ref[...] = _adamw_update(w_ref[...], gg, m_ref[...], v_ref[...])

    full = pl.BlockSpec((tr, c), lambda h, i, p: (h * nt + i, 0))
    half = pl.BlockSpec((tr, c), lambda h, i, p: (i, 0))
    dep_specs, dep_ops = _dep_args(dep)
    return pl.pallas_call(
        _after(body, 6, dep), name=name, out_shape=(jax.ShapeDtypeStruct((r, c), F32),) * 4,
        grid_spec=pltpu.PrefetchScalarGridSpec(
            num_scalar_prefetch=1, grid=(2, nt),
            in_specs=[full, half, half, full, full] + dep_specs, out_specs=(full,) * 4),
        compiler_params=_cparams("parallel", "parallel"),
    )(pos_c, w, g_mine, g_other, m, v, *dep_ops)


def _col_sharded_to_comm(g):
    k, n = g.shape
    return g.reshape(2, k // 2, N_CHIPS, n // N_CHIPS).transpose(2, 0, 1, 3)


def _row_sharded_to_comm(g):
    r, c = g.shape
    return g.reshape(N_CHIPS, 2, r // (2 * N_CHIPS), c)


def _col_sharded_full(g):
    _, _, rh, c = g.shape
    return g.reshape(N_CHIPS, 2 * rh, c).transpose(1, 0, 2).reshape(2 * rh, N_CHIPS * c)


def _row_sharded_full(g):
    _, _, rh, c = g.shape
    return g.reshape(N_CHIPS * 2 * rh, c)


def _chip_rows(w3, start, stop, own=None, me=None):
    r = w3.shape[1]
    parts = []
    for chip in range(N_CHIPS):
        lo, hi = max(start - chip * r, 0), min(stop - chip * r, r)
        if lo < hi:
            part = w3[chip, lo:hi]
            parts.append(part if own is None else jnp.where(me == chip, own[lo:hi], part))
    return parts[0] if len(parts) == 1 else jnp.concatenate(parts, axis=0)


def _pack_small(g1, bfv, mix, scale, g2n, gf, extra=None):
    row8 = jnp.pad(bfv.reshape(1, N_HEADS), ((0, 0), (0, LANES - N_HEADS)))
    if extra is not None:
        row8 = row8 + jnp.pad(extra[:, :1], ((0, 0), (N_HEADS, LANES - N_HEADS - 1)))
    return jnp.concatenate([
        g1.reshape(8, LANES), jnp.pad(row8, ((0, 7), (0, 0))), mix.reshape(512, LANES),
        jnp.pad(scale.reshape(4, LANES), ((0, 4), (0, 0))), g2n.reshape(8, LANES), gf.reshape(8, LANES)], axis=0)


def _unpack_small(s, like):
    g1, bfv, mix, scale, g2n, gf = like
    return (s[0:8].reshape(g1.shape), s[8, :N_HEADS].reshape(bfv.shape), s[16:528].reshape(mix.shape),
            s[528:532].reshape(scale.shape), s[536:544].reshape(g2n.shape), s[544:552].reshape(gf.shape))


class _MeshLinks:
    def __init__(self, shards_in, shards_rest):
        self.gin = _gather_begin("in", shards_in, None, column_halves=True)
        self.grest = _gather_begin("rest", shards_rest, self.gin["token"])
        self.tokens = {"gather": self.grest["token"]}
        self.groups = {}

    @property
    def token(self):
        return list(self.tokens.values())

    def tie(self, x):
        return _tie(x, self.token)

    def weights_in(self, after):
        st = _gather_forward(self.gin, after)
        (g,), (own,) = _gather_end(st, st["token"], merge=False)
        return g, own, 2 * lax.axis_index("x") + lax.axis_index("y")

    def rest_forward(self, after):
        self.grest = _gather_forward(self.grest, after)
        self.tokens["gather"] = self.grest["token"]

    def weights_rest(self, after):
        g = _gather_end(self.grest, after)
        del self.tokens["gather"]
        return [_col_sharded_full(g[0]), _col_sharded_full(g[1])] + [_row_sharded_full(a) for a in g[2:]]

    def reduce_begin(self, tag, grads, column_halves=False):
        self.groups[tag] = _reduce_begin(tag, grads, column_halves)
        self.tokens[tag] = self.groups[tag]["token"]

    def advance(self, after):
        for tag, st in self.groups.items():
            if "done" not in st:
                self.groups[tag] = _reduce_advance(st, after)
                if self.groups[tag]["token"] is None:
                    del self.tokens[tag]
                else:
                    self.tokens[tag] = self.groups[tag]["token"]

    def reduced(self, tag):
        return self.groups[tag]["done"]


class _NoLinks:
    token = None

    def __init__(self, w_in, rest):
        self.w_in, self.rest, self.grads = w_in, rest, {}

    def tie(self, x):
        return x

    def weights_in(self, after):
        return self.w_in, None, None

    def rest_forward(self, after):
        pass

    def weights_rest(self, after):
        return self.rest

    def reduce_begin(self, tag, grads, column_halves=False):
        self.grads[tag] = grads

    def advance(self, after):
        pass


def _local_step(links, x, target, seq, norm1_g, b_forget, pool_mix, pool_scale, norm2_g, norm_f_g):
    t, d = x.shape
    tq = min(256, seq)
    aw = ATTN_WIDTH
    o_q, o_f, o_g = POOL_WIDTH, POOL_WIDTH + 3 * aw, POOL_WIDTH + 3 * aw + N_HEADS
    bf = jnp.pad(b_forget, ((0, 0), (0, LANES - N_HEADS)))
    mixb = pool_mix.astype(BF16)

    h = _norm_fwd("norm1_fwd", x, links.tie(norm1_g), 512)
    w_in, own, me = links.weights_in(h)
    wu = _chip_rows(w_in, 0, o_q, own, me)
    wqkv = _chip_rows(w_in, o_q, o_f, own, me)
    wft = jnp.pad(_chip_rows(w_in, o_f, o_g, own, me), ((0, LANES - N_HEADS), (0, 0)))
    wg2 = _chip_rows(w_in, o_g, N_CHIPS * w_in.shape[1], own, me)
    wf = wft.T
    u = _matmul("mm_u", h, wu, "nt", F32, 1024, 512, d)
    g2 = _matmul("mm_gates", h, wg2, "nt", BF16, 1024, 1024, d)
    fl, fcum = _forget_fwd(h, wf, bf, seq)
    qa, ka, v = _attn_prep(h, _head_blocks(wqkv[:aw]), _head_blocks(wqkv[aw:2 * aw]), wqkv[2 * aw:], fcum, 1024)
    p, ps = _pool_fwd(u, mixb, pool_scale, seq)
    links.rest_forward([ps, qa, g2])
    o, lse = _attn_fwd(qa, ka, v, seq, tq, dep=links.token)
    w_pool_out, w_attn_out, w_out, w_ffn_gate, w_ffn_up, w_ffn_down = links.weights_rest(o)
    merged, x1 = _merge_fwd(x, ps, o, g2, w_pool_out, w_attn_out, w_out, 512)
    h2, gt, up, act, x2 = _ffn_fwd(x1, norm2_g, w_ffn_gate, w_ffn_up, w_ffn_down, 1024, 256)
    loss, dx2, d_gf = _final_fwd_bwd(x2, target, norm_f_g, 512)

    dgt, dup, dx1, d_g2n = _ffn_bwd(dx2, x1, norm2_g, gt, up, w_ffn_gate, w_ffn_up, w_ffn_down, 1024, 256)
    d_wd = _matmul("dw_down", act, dx2, "tn", F32, 1408, 1024, 1024)
    d_wg = _matmul("dw_gate", dgt, h2, "tn", F32, 1408, 1024, 1024)
    d_wu = _matmul("dw_up", dup, h2, "tn", F32, 1408, 1024, 1024)
    links.reduce_begin("a", [_row_sharded_to_comm(g) for g in (d_wg, d_wu, d_wd)])
    dpy, day, dg2, dps, da = _merge_bwd(dx1, ps, o, g2, w_pool_out, w_attn_out, w_out, 512, dep=links.token)
    links.advance(dps)
    d_wout = _matmul("dw_out", merged, dx1, "tn", F32, 1024, 1024, 1024)
    d_wpo = _matmul("dw_pool_out", ps, dpy, "tn", F32, 512, 1024, 1024)
    d_wao = _matmul("dw_attn_out", o, day, "tn", F32, 512, 1024, 1024)
    links.reduce_begin("m", [_col_sharded_to_comm(d_wpo), _col_sharded_to_comm(d_wao), _row_sharded_to_comm(d_wout)])
    dq, dk, dv, dfr = _attn_bwd(qa, ka, v, da, lse, seq, tq, dep=links.token)
    links.advance(dq)
    dfc = jnp.pad(dfr.reshape(N_HEADS, t).T, ((0, 0), (0, LANES - N_HEADS)))
    dfl, d_bf = _forget_bwd(dfc, fl, bf, seq)
    du, d_mix, d_scale = _pool_bwd(dps, p, mixb, links.tie(pool_scale), seq)
    d_wu_in = _matmul("dw_in_u", du, h, "tn", F32, 512, 1024, 1024)
    d_wq = _matmul("dw_in_q", dq, h, "tn", F32, 512, 1024, 1024)
    d_wk = _matmul("dw_in_k", dk, h, "tn", F32, 512, 1024, 1024)
    d_wv = _matmul("dw_in_v", dv, h, "tn", F32, 512, 1024, 1024)
    links.advance([d_wu_in, d_wq, d_wk, d_wv])
    d_wf = _matmul("dw_in_f", dfl, h, "tn", F32, LANES, 1024, 512)
    d_wg2 = _matmul("dw_in_gates", dg2, h, "tn", F32, 1024, 1024, 1024, dep=links.token)
    d_win = jnp.concatenate([d_wu_in, d_wq, d_wk, d_wv, d_wf[:N_HEADS], d_wg2], axis=0)
    comm_b = [d_win.reshape(N_CHIPS, d_win.shape[0] // N_CHIPS, d)]
    links.advance(comm_b)
    links.reduce_begin("b", comm_b, column_halves=True)
    dx, d_g1 = _in_bwd(du, dq, dk, dv, dg2, dfl, dx1, x, links.tie(norm1_g), wu, wqkv, wg2, wft, 512)
    links.advance(dx)
    small = (d_g1, d_bf[:, :N_HEADS], d_mix, d_scale, d_g2n, d_gf)
    return loss, dx, small


def kernel(x, norm1_g, w_in, b_forget, pool_mix, pool_scale, w_pool_out, w_attn_out, w_out, norm2_g, w_ffn_gate, w_ffn_up, w_ffn_down, norm_f_g, loss_target, m_norm1_g, m_w_in, m_b_forget, m_pool_mix, m_pool_scale, m_w_pool_out, m_w_attn_out, m_w_out, m_norm2_g, m_w_ffn_gate, m_w_ffn_up, m_w_ffn_down, m_norm_f_g, v_norm1_g, v_w_in, v_b_forget, v_pool_mix, v_pool_scale, v_w_pool_out, v_w_attn_out, v_w_out, v_norm2_g, v_w_ffn_gate, v_w_ffn_up, v_w_ffn_down, v_norm_f_g):
    nb, seq, d = x.shape
    group_a = ((w_ffn_gate, m_w_ffn_gate, v_w_ffn_gate, True, 9), (w_ffn_up, m_w_ffn_up, v_w_ffn_up, True, 10),
               (w_ffn_down, m_w_ffn_down, v_w_ffn_down, False, 11))
    group_m = ((w_pool_out, m_w_pool_out, v_w_pool_out, False, 5), (w_attn_out, m_w_attn_out, v_w_attn_out, False, 6),
               (w_out, m_w_out, v_w_out, False, 7))
    group_b = ((w_in, m_w_in, v_w_in, False, 1),)
    small_w = (norm1_g, b_forget, pool_mix, pool_scale, norm2_g, norm_f_g)
    small_m = (m_norm1_g, m_b_forget, m_pool_mix, m_pool_scale, m_norm2_g, m_norm_f_g)
    small_v = (v_norm1_g, v_b_forget, v_pool_mix, v_pool_scale, v_norm2_g, v_norm_f_g)
    small_pos = (0, 2, 3, 4, 8, 12)
    view = lambda a, tr: a[0].T if tr else a[0]
    unview = lambda a, tr, like: (a.T if tr else a).reshape(like.shape)

    def shard(w, tr):
        lw = view(w, tr).astype(BF16)
        return lw.reshape(2, lw.shape[0] // 2, lw.shape[1])

    cm = lambda a: jnp.transpose(a, (2, 0, 1))
    shard_in = _rows_to_bf16("w_in_to_bf16", cm(w_in))
    links = _MeshLinks([shard_in],
                       [shard(w_pool_out, False), shard(w_attn_out, False), shard(w_out, False),
                        shard(w_ffn_gate, True), shard(w_ffn_up, True), shard(w_ffn_down, False)])
    loss, dx, small_g = _local_step(
        links, x.reshape(nb * seq, d), loss_target.reshape(nb * seq, d), seq,
        norm1_g, b_forget, pool_mix[0], pool_scale, norm2_g, norm_f_g.reshape(1, d))

    grads, deltas, new_m, new_v = [None] * 13, [None] * 13, [None] * 13, [None] * 13
    pos_c = jnp.stack([lax.axis_index("c")]).astype(jnp.int32)

    def update(tag, group, dep):
        last = []
        for k, ((w, m, v, tr, pos), (mine, other)) in enumerate(zip(group, links.reduced(tag))):
            outs = _adamw_halves(f"adamw_{tag}{k}", pos_c, view(w, tr), mine, other, view(m, tr), view(v, tr), 256,
                                 dep=dep)
            grads[pos], deltas[pos], new_m[pos], new_v[pos] = (unview(a, tr, w) for a in outs)
            last.append(outs[1])
        return last

    small_state = _small_begin(_pack_small(*small_g, extra=loss), links.token)
    links.tokens["small"] = small_state["token"]
    last = update("a", group_a, links.token) + update("m", group_m, links.token)
    links.advance(last)
    del links.tokens["small"]
    small_sum = _small_end(small_state, last)
    loss_out = small_sum[8, N_HEADS]
    dl, mn, vn = _adamw("adamw_small", _pack_small(*small_w), small_sum * _small_mask(), _pack_small(*small_m),
                        _pack_small(*small_v))
    for pos, g, a, b, e in zip(small_pos, _unpack_small(small_sum, small_w), _unpack_small(dl, small_w),
                               _unpack_small(mn, small_w), _unpack_small(vn, small_w)):
        grads[pos], deltas[pos], new_m[pos], new_v[pos] = g, a, b, e
    links.advance(dl)
    links.advance(links.token)
    (mine, other), = links.reduced("b")
    outs = _adamw_rows("adamw_b0", pos_c, cm(w_in), mine, other, cm(m_w_in), cm(v_w_in))
    grads[1], deltas[1], new_m[1], new_v[1] = (jnp.transpose(a, (1, 2, 0)) for a in outs)

    return (loss_out, dx.reshape(nb, seq, d), *grads, *deltas, *new_m, *new_v)


def _small_mask():
    rows = lax.broadcasted_iota(jnp.int32, (552, LANES), 0)
    lanes = lax.broadcasted_iota(jnp.int32, (552, LANES), 1)
    return jnp.where(jnp.logical_and(rows == 8, lanes == N_HEADS), 0.0, 1.0).astype(F32)
```

```python
import functools

import jax
import jax.numpy as jnp
from jax import lax
from jax.experimental import pallas as pl
from jax.experimental.pallas import tpu as pltpu

F32 = jnp.float32
BF16 = jnp.bfloat16

D_MODEL = 1024
POOL_WINDOWS = (2, 4, 8, 16)
POOL_GROUPS = 4
POOL_GROUP_DIM = 128
POOL_WIDTH = 512
HEAD_DIM = 64
N_HEADS = 8
ATTN_WIDTH = 512
D_FF = 2816
RMS_EPS = 1e-6
ATTN_SCALE = HEAD_DIM ** -0.5
NEG_BIG = -1e30

ADAM_LR = 0.001
ADAM_B1 = 0.9
ADAM_B2 = 0.999
ADAM_EPS = 1e-08
ADAM_WD = 0.01
ADAM_STEP = 10

LANES = 128
N_CHIPS = 4
N_DEV = 8
VMEM_LIMIT_V7X = 52 * 1024 * 1024
MESH = pl.DeviceIdType.MESH
ANY = pl.BlockSpec(memory_space=pl.ANY)


def _cparams(*sem):
    return pltpu.CompilerParams(dimension_semantics=sem if sem else None, vmem_limit_bytes=VMEM_LIMIT_V7X)


def _dep_list(dep):
    return [] if dep is None else (list(dep) if isinstance(dep, (list, tuple)) else [dep])


def _after(body, n_in, dep):
    k = len(_dep_list(dep))
    if k == 0:
        return body

    def wrapped(*refs):
        body(*refs[:n_in], *refs[n_in + k:])

    return wrapped


def _dep_args(dep):
    deps = _dep_list(dep)
    return [ANY] * len(deps), deps


def _dot(a, b):
    return lax.dot_general(a, b, (((1,), (0,)), ((), ())), preferred_element_type=F32)


def _dot_nt(a, b):
    return lax.dot_general(a, b, (((1,), (1,)), ((), ())), preferred_element_type=F32)


def _dot_tn(a, b):
    return lax.dot_general(a, b, (((0,), (0,)), ((), ())), preferred_element_type=F32)


def _sigmoid(x):
    return jax.nn.sigmoid(x)


def _rms_fwd(x, g):
    r = lax.rsqrt(jnp.mean(x * x, axis=-1, keepdims=True) + RMS_EPS)
    return (x * r) * g


def _rms_bwd(x, g, dy):
    r = lax.rsqrt(jnp.mean(x * x, axis=-1, keepdims=True) + RMS_EPS)
    xh = x * r
    dg = jnp.sum(dy * xh, axis=0, keepdims=True)
    dxh = dy * g
    dx = r * (dxh - xh * jnp.mean(dxh * xh, axis=-1, keepdims=True))
    return dx, dg


def _matmul(name, a, b, mode, out_dtype, tm, tn, tk, dep=None):
    if mode == "nn":
        (m, k), (_, n) = a.shape, b.shape
    elif mode == "nt":
        (m, k), (n, _) = a.shape, b.shape
    else:
        (k, m), (_, n) = a.shape, b.shape
    tm, tn, tk = min(tm, m), min(tn, n), min(tk, k)
    assert m % tm == 0 and n % tn == 0 and k % tk == 0, (name, m, n, k, tm, tn, tk)
    nk = k // tk
    if mode == "tn":
        a_spec = pl.BlockSpec((tk, tm), lambda i, j, kk: (kk, i))
    else:
        a_spec = pl.BlockSpec((tm, tk), lambda i, j, kk: (i, kk))
    if mode == "nt":
        b_spec = pl.BlockSpec((tn, tk), lambda i, j, kk: (j, kk))
    else:
        b_spec = pl.BlockSpec((tk, tn), lambda i, j, kk: (kk, j))
    dot = {"nn": _dot, "nt": _dot_nt, "tn": _dot_tn}[mode]
    use_scratch = nk > 1 and out_dtype != F32

    def body(a_ref, b_ref, o_ref, *scratch):
        prod = dot(a_ref[...].astype(BF16), b_ref[...].astype(BF16))
        if nk == 1:
            o_ref[...] = prod.astype(out_dtype)
            return
        acc = scratch[0] if use_scratch else o_ref
        kk = pl.program_id(2)

        @pl.when(kk == 0)
        def _():
            acc[...] = prod

        @pl.when(kk > 0)
        def _():
            acc[...] += prod

        if use_scratch:
            @pl.when(kk == nk - 1)
            def _():
                o_ref[...] = acc[...].astype(out_dtype)

    dep_specs, dep_ops = _dep_args(dep)
    return pl.pallas_call(
        _after(body, 2, dep),
        name=name,
        out_shape=jax.ShapeDtypeStruct((m, n), out_dtype),
        grid=(m // tm, n // tn, nk),
        in_specs=[a_spec, b_spec] + dep_specs,
        out_specs=pl.BlockSpec((tm, tn), lambda i, j, kk: (i, j)),
        scratch_shapes=[pltpu.VMEM((tm, tn), F32)] if use_scratch else [],
        compiler_params=_cparams("parallel", "parallel", "arbitrary"),
    )(a, b, *dep_ops)


def _norm_fwd(name, x, g, tm):
    t, d = x.shape
    tm = min(tm, t)

    def body(x_ref, g_ref, h_ref):
        h_ref[...] = _rms_fwd(x_ref[...], g_ref[...]).astype(BF16)

    return pl.pallas_call(
        body, name=name, out_shape=jax.ShapeDtypeStruct((t, d), BF16), grid=(t // tm,),
        in_specs=[pl.BlockSpec((tm, d), lambda i: (i, 0)), pl.BlockSpec((1, d), lambda i: (0, 0))],
        out_specs=pl.BlockSpec((tm, d), lambda i: (i, 0)),
        compiler_params=_cparams("parallel"),
    )(x, g)


def _split3(x):
    hi = x.astype(BF16)
    r1 = x - hi.astype(F32)
    mid = r1.astype(BF16)
    lo = (r1 - mid.astype(F32)).astype(BF16)
    return hi, mid, lo


def _tri_dot(tri, x):
    hi, mid, lo = _split3(x)
    return _dot(tri, hi) + _dot(tri, mid) + _dot(tri, lo)


def _forget_fwd(h, wf, bf, seq):
    t, d = h.shape
    cb = min(256, seq)

    def body(h_ref, wf_ref, bf_ref, fl_ref, fc_ref):
        fl = _dot(h_ref[...], wf_ref[...])
        fl_ref[...] = fl
        xx = fl + bf_ref[...]
        lf = jnp.minimum(xx, 0.0) - jnp.log(1.0 + jnp.exp(-jnp.abs(xx)))
        ri = lax.broadcasted_iota(jnp.int32, (cb, cb), 0)
        ci = lax.broadcasted_iota(jnp.int32, (cb, cb), 1)
        tri = (ri >= ci).astype(BF16)
        carry = jnp.zeros((1, LANES), F32)
        for blk in range(seq // cb):
            cs = _tri_dot(tri, lf[blk * cb:(blk + 1) * cb]) + carry
            fc_ref[blk * cb:(blk + 1) * cb, :] = cs
            carry = cs[cb - 1:cb, :]

    return pl.pallas_call(
        body, name="forget_fwd",
        out_shape=(jax.ShapeDtypeStruct((t, LANES), F32), jax.ShapeDtypeStruct((t, LANES), F32)),
        grid=(t // seq,),
        in_specs=[pl.BlockSpec((seq, d), lambda b: (b, 0)), pl.BlockSpec((d, LANES), lambda b: (0, 0)),
                  pl.BlockSpec((1, LANES), lambda b: (0, 0))],
        out_specs=(pl.BlockSpec((seq, LANES), lambda b: (b, 0)), pl.BlockSpec((seq, LANES), lambda b: (b, 0))),
        compiler_params=_cparams("parallel"),
    )(h, wf, bf)


def _pool_fwd(u, mix, scale, seq):
    t = u.shape[0]

    def body(u_ref, mix_ref, sc_ref, p_ref, ps_ref):
        tpos = lax.broadcasted_iota(jnp.int32, (seq, POOL_GROUP_DIM), 0)
        for g in range(POOL_GROUPS):
            sl = slice(g * POOL_GROUP_DIM, (g + 1) * POOL_GROUP_DIM)
            ug = u_ref[:, sl]
            s = ug
            for lvl in range(g + 1):
                d = 2 ** lvl
                s = s + jnp.where(tpos >= d, pltpu.roll(s, d, 0), 0.0)
            cnt = jnp.minimum(tpos + 1, POOL_WINDOWS[g]).astype(F32)
            pb = (s / cnt - ug).astype(BF16)
            p_ref[:, sl] = pb
            ps_ref[:, sl] = (_dot(pb, mix_ref[g]) * sc_ref[:, sl]).astype(BF16)

    return pl.pallas_call(
        body, name="pool_fwd",
        out_shape=(jax.ShapeDtypeStruct((t, POOL_WIDTH), BF16), jax.ShapeDtypeStruct((t, POOL_WIDTH), BF16)),
        grid=(t // seq,),
        in_specs=[pl.BlockSpec((seq, POOL_WIDTH), lambda b: (b, 0)),
                  pl.BlockSpec((POOL_GROUPS, POOL_GROUP_DIM, POOL_GROUP_DIM), lambda b: (0, 0, 0)),
                  pl.BlockSpec((1, POOL_WIDTH), lambda b: (0, 0))],
        out_specs=(pl.BlockSpec((seq, POOL_WIDTH), lambda b: (b, 0)), pl.BlockSpec((seq, POOL_WIDTH), lambda b: (b, 0))),
        compiler_params=_cparams("parallel"),
    )(u, mix, scale)


def _aug_constants():
    w = N_HEADS * LANES
    rows = jnp.arange(3 * LANES)
    piece, head = rows // LANES, rows % LANES
    cols = jnp.arange(w)
    live = (head < N_HEADS)[:, None]
    pq = (live & (cols[None, :] == (head * LANES + HEAD_DIM + piece)[:, None])).astype(BF16)
    pk = -(live & (cols[None, :] == (head * LANES + HEAD_DIM + 3 + piece)[:, None])).astype(BF16)
    lane = cols % LANES
    oq = ((lane >= HEAD_DIM + 3) & (lane < HEAD_DIM + 6)).astype(F32)[None, :]
    ok = ((lane >= HEAD_DIM) & (lane < HEAD_DIM + 3)).astype(F32)[None, :]
    return pq, pk, oq, ok


def _head_blocks(wt):
    d = wt.shape[1]
    return jnp.pad(wt.reshape(N_HEADS, HEAD_DIM, d), ((0, 0), (0, LANES - HEAD_DIM), (0, 0))).reshape(N_HEADS * LANES, d)


def _attn_prep(h, wq, wk, wv, fcum, tm):
    t, d = h.shape
    tm = min(tm, t)
    w = N_HEADS * LANES
    pq, pk, oq, ok = _aug_constants()

    def body(h_ref, wq_ref, wk_ref, wv_ref, f_ref, pq_ref, pk_ref, oq_ref, ok_ref, qa_ref, ka_ref, v_ref):
        hh = h_ref[...]
        fs = jnp.concatenate(_split3(f_ref[...]), axis=1)
        q = _dot_nt(hh, wq_ref[...]).astype(BF16).astype(F32) * ATTN_SCALE
        qa_ref[...] = (q + _dot(fs, pq_ref[...]) + oq_ref[...]).astype(BF16)
        k = _dot_nt(hh, wk_ref[...]).astype(BF16).astype(F32)
        ka_ref[...] = (k + _dot(fs, pk_ref[...]) + ok_ref[...]).astype(BF16)
        v_ref[...] = _dot_nt(hh, wv_ref[...]).astype(BF16)

    row = lambda n: pl.BlockSpec((tm, n), lambda i: (i, 0))
    full = lambda a: pl.BlockSpec(a.shape, lambda i: (0, 0))
    return pl.pallas_call(
        body, name="attn_prep",
        out_shape=(jax.ShapeDtypeStruct((t, w), BF16), jax.ShapeDtypeStruct((t, w), BF16),
                   jax.ShapeDtypeStruct((t, ATTN_WIDTH), BF16)),
        grid=(t // tm,),
        in_specs=[row(d), full(wq), full(wk), full(wv), row(LANES), full(pq), full(pk), full(oq), full(ok)],
        out_specs=(row(w), row(w), row(ATTN_WIDTH)),
        compiler_params=_cparams("parallel"),
    )(h, wq, wk, wv, fcum, pq, pk, oq, ok)


def _fold_lanes(x, op):
    out = x[:, :LANES]
    for g in range(1, x.shape[1] // LANES):
        out = op(out, x[:, g * LANES:(g + 1) * LANES])
    return out


def _causal_sweep(i, tile, carry):
    def pair(jj, c):
        return tile(2 * jj + 1, tile(2 * jj, c, False), False)

    carry = lax.fori_loop(0, i // 2, pair, carry)
    return lax.cond(i % 2 == 1, lambda c: tile(i, tile(i - 1, c, False), True), lambda c: tile(i, c, True), carry)


def _attn_fwd(qa, ka, v, seq, tq, dep=None):
    t = qa.shape[0]
    nq = seq // tq
    hp_n = N_HEADS // 2
    heads = [slice(e * LANES, (e + 1) * LANES) for e in range(2)]

    def body(q_ref, k_ref, v_ref, o_ref, lse_ref, s_buf):
        i = pl.program_id(2)
        diag_ok = lax.broadcasted_iota(jnp.int32, (tq, tq), 0) >= lax.broadcasted_iota(jnp.int32, (tq, tq), 1)
        qs = [q_ref[:, hl] for hl in heads]

        def sweep1(j, mxs, diagonal):
            r0 = pl.multiple_of(j * tq, tq)
            out = []
            for e, hl in enumerate(heads):
                s = _dot_nt(qs[e], k_ref[pl.ds(r0, tq), hl])
                if diagonal:
                    s = jnp.where(diag_ok, s, NEG_BIG)
                s_buf[e, j] = s
                out.append(jnp.maximum(mxs[e], _fold_lanes(s, jnp.maximum)))
            return tuple(out)

        mxs = _causal_sweep(i, sweep1, (jnp.full((tq, LANES), NEG_BIG, F32),) * 2)
        ms = [jnp.max(mx, axis=1, keepdims=True) for mx in mxs]

        def sweep2(j, carry, diagonal):
            r0 = pl.multiple_of(j * tq, tq)
            vv = v_ref[pl.ds(r0, tq), :]
            out = []
            for e in range(2):
                p = jnp.exp(s_buf[e, j] - ms[e])
                out += [carry[2 * e] + _fold_lanes(p, jnp.add), carry[2 * e + 1] + _dot(p.astype(BF16), vv)]
            return tuple(out)

        res = _causal_sweep(i, sweep2, (jnp.zeros((tq, LANES), F32),) * 4)
        outs = []
        for e in range(2):
            l = jnp.sum(res[2 * e], axis=1, keepdims=True)
            outs.append(res[2 * e + 1] / l)
            lse_ref[:, e:e + 1] = ms[e] + jnp.log(l)
        lane = lax.broadcasted_iota(jnp.int32, (tq, LANES), 1)
        o_ref[...] = jnp.where(lane < HEAD_DIM, outs[0], outs[1])

    dep_specs, dep_ops = _dep_args(dep)
    return pl.pallas_call(
        _after(body, 3, dep), name="attn_fwd",
        out_shape=(jax.ShapeDtypeStruct((t, ATTN_WIDTH), F32), jax.ShapeDtypeStruct((hp_n, t, 2), F32)),
        grid=(t // seq, hp_n, nq),
        in_specs=[pl.BlockSpec((tq, 2 * LANES), lambda b, hp, i: (b * nq + i, hp)),
                  pl.BlockSpec((seq, 2 * LANES), lambda b, hp, i: (b, hp)),
                  pl.BlockSpec((seq, LANES), lambda b, hp, i: (b, hp))] + dep_specs,
        out_specs=(pl.BlockSpec((tq, LANES), lambda b, hp, i: (b * nq + i, hp)),
                   pl.BlockSpec((None, tq, 2), lambda b, hp, i: (hp, b * nq + i, 0))),
        scratch_shapes=[pltpu.VMEM((2, nq, tq, tq), F32)],
        compiler_params=_cparams("parallel", "parallel", "arbitrary"),
    )(qa, ka, v, *dep_ops)


def _merge_fwd(x, ps, o, g2, wpo, wao, wout, tm):
    t, d = x.shape
    tm = min(tm, t)

    def body(x_ref, ps_ref, o_ref, gp_ref, ga_ref, wpo_ref, wao_ref, wout_ref, mg_ref, x1_ref):
        py = _dot(ps_ref[...], wpo_ref[...])
        ay = _dot(o_ref[...].astype(BF16), wao_ref[...])
        mb = (_sigmoid(gp_ref[...].astype(F32)) * py + _sigmoid(ga_ref[...].astype(F32)) * ay).astype(BF16)
        mg_ref[...] = mb
        x1_ref[...] = x_ref[...] + _dot(mb, wout_ref[...])

    row = lambda w: pl.BlockSpec((tm, w), lambda i: (i, 0))
    full = lambda a: pl.BlockSpec(a.shape, lambda i: (0, 0))
    return pl.pallas_call(
        body, name="merge_fwd",
        out_shape=(jax.ShapeDtypeStruct((t, d), BF16), jax.ShapeDtypeStruct((t, d), F32)),
        grid=(t // tm,),
        in_specs=[row(d), row(POOL_WIDTH), row(ATTN_WIDTH), pl.BlockSpec((tm, d), lambda i: (i, 0)),
                  pl.BlockSpec((tm, d), lambda i: (i, 1)), full(wpo), full(wao), full(wout)],
        out_specs=(row(d), row(d)),
        compiler_params=_cparams("parallel"),
    )(x, ps, o, g2, g2, wpo, wao, wout)


def _ffn_fwd(x1, g, wg, wu, wd, tm, tf):
    t, d = x1.shape
    f = wg.shape[0]
    tm = min(tm, t)
    nf = f // tf
    rows = min(512, tm)

    def body(x1_ref, g_ref, wg_ref, wu_ref, wd_ref, h2_ref, gt_ref, up_ref, act_ref, x2_ref):
        j = pl.program_id(1)

        @pl.when(j == 0)
        def _():
            h2_ref[...] = _rms_fwd(x1_ref[...], g_ref[...]).astype(BF16)

            x2_ref[...] = x1_ref[...]

        for r0 in range(0, tm, rows):
            rs = slice(r0, r0 + rows)
            h2 = h2_ref[rs, :]
            gt = _dot_nt(h2, wg_ref[...])
            up = _dot_nt(h2, wu_ref[...])
            sg = _sigmoid(gt)
            silu = gt * sg
            act = (silu * up).astype(BF16)
            gt_ref[rs, :] = (up * (sg * (1.0 + gt * (1.0 - sg)))).astype(BF16)
            up_ref[rs, :] = silu.astype(BF16)
            act_ref[rs, :] = act
            x2_ref[rs, :] += _dot(act, wd_ref[...])

    return pl.pallas_call(
        body, name="ffn_fwd",
        out_shape=(jax.ShapeDtypeStruct((t, d), BF16), jax.ShapeDtypeStruct((t, f), BF16),
                   jax.ShapeDtypeStruct((t, f), BF16), jax.ShapeDtypeStruct((t, f), BF16),
                   jax.ShapeDtypeStruct((t, d), F32)),
        grid=(t // tm, nf),
        in_specs=[pl.BlockSpec((tm, d), lambda i, j: (i, 0)), pl.BlockSpec((1, d), lambda i, j: (0, 0)),
                  pl.BlockSpec((tf, d), lambda i, j: (j, 0)), pl.BlockSpec((tf, d), lambda i, j: (j, 0)),
                  pl.BlockSpec((tf, d), lambda i, j: (j, 0))],
        out_specs=(pl.BlockSpec((tm, d), lambda i, j: (i, 0)), pl.BlockSpec((tm, tf), lambda i, j: (i, j)),
                   pl.BlockSpec((tm, tf), lambda i, j: (i, j)), pl.BlockSpec((tm, tf), lambda i, j: (i, j)),
                   pl.BlockSpec((tm, d), lambda i, j: (i, 0))),
        compiler_params=_cparams("parallel", "arbitrary"),
    )(x1, g, wg, wu, wd)


def _final_fwd_bwd(x2, target, g, tm):
    t, d = x2.shape
    tm = min(tm, t)

    def body(x_ref, t_ref, g_ref, loss_ref, dx_ref, dg_ref):
        i = pl.program_id(0)
        x = x_ref[...]
        gg = g_ref[...]
        err = _rms_fwd(x, gg) - t_ref[...]
        part = 0.5 * jnp.sum(jnp.mean(err * err, axis=-1, keepdims=True), axis=0, keepdims=True)
        dx, dg = _rms_bwd(x, gg, err * (1.0 / d))
        dx_ref[...] = dx

        @pl.when(i == 0)
        def _():
            loss_ref[...] = jnp.zeros_like(loss_ref)
            dg_ref[...] = jnp.zeros_like(dg_ref)

        loss_ref[...] += jnp.broadcast_to(part, loss_ref.shape)
        dg_ref[...] += dg

    return pl.pallas_call(
        body, name="final_fwd_bwd",
        out_shape=(jax.ShapeDtypeStruct((1, LANES), F32), jax.ShapeDtypeStruct((t, d), F32),
                   jax.ShapeDtypeStruct((1, d), F32)),
        grid=(t // tm,),
        in_specs=[pl.BlockSpec((tm, d), lambda i: (i, 0)), pl.BlockSpec((tm, d), lambda i: (i, 0)),
                  pl.BlockSpec((1, d), lambda i: (0, 0))],
        out_specs=(pl.BlockSpec((1, LANES), lambda i: (0, 0)), pl.BlockSpec((tm, d), lambda i: (i, 0)),
                   pl.BlockSpec((1, d), lambda i: (0, 0))),
        compiler_params=_cparams("arbitrary"),
    )(x2, target, g)


def _ffn_bwd(dx2, x1, g, gt, up, wg, wu, wd, tm, tf):
    t, d = dx2.shape
    f = gt.shape[1]
    tm = min(tm, t)
    nf = f // tf
    wgu = jnp.concatenate([wg.reshape(nf, tf, d), wu.reshape(nf, tf, d)], axis=1).reshape(2 * f, d)
    rows = min(256, tm)

    def body(dx2_ref, x1_ref, g_ref, gt_ref, up_ref, wgu_ref, wd_ref, dgt_ref, dup_ref, dx1_ref, dg_ref, acc_ref,
             dxb_ref):
        i, j = pl.program_id(0), pl.program_id(1)

        @pl.when(j == 0)
        def _():
            dxb_ref[...] = dx2_ref[...].astype(BF16)
            acc_ref[...] = jnp.zeros_like(acc_ref)

        for r0 in range(0, tm, rows):
            rs = slice(r0, r0 + rows)
            dact = _dot_nt(dxb_ref[rs, :], wd_ref[...])
            dgt = (dact * gt_ref[rs, :].astype(F32)).astype(BF16)
            dup = (dact * up_ref[rs, :].astype(F32)).astype(BF16)
            dgt_ref[rs, :] = dgt
            dup_ref[rs, :] = dup
            acc_ref[rs, :] += _dot(jnp.concatenate([dgt, dup], axis=1), wgu_ref[...])

        @pl.when(jnp.logical_and(i == 0, j == 0))
        def _():
            dg_ref[...] = jnp.zeros_like(dg_ref)

        @pl.when(j == nf - 1)
        def _():
            dxn, dg = _rms_bwd(x1_ref[...], g_ref[...], acc_ref[...])
            dx1_ref[...] = dx2_ref[...] + dxn
            dg_ref[...] += dg

    return pl.pallas_call(
        body, name="ffn_bwd",
        out_shape=(jax.ShapeDtypeStruct((t, f), BF16), jax.ShapeDtypeStruct((t, f), BF16),
                   jax.ShapeDtypeStruct((t, d), F32), jax.ShapeDtypeStruct((1, d), F32)),
        grid=(t // tm, nf),
        in_specs=[pl.BlockSpec((tm, d), lambda i, j: (i, 0)), pl.BlockSpec((tm, d), lambda i, j: (i, 0)),
                  pl.BlockSpec((1, d), lambda i, j: (0, 0)),
                  pl.BlockSpec((tm, tf), lambda i, j: (i, j)), pl.BlockSpec((tm, tf), lambda i, j: (i, j)),
                  pl.BlockSpec((2 * tf, d), lambda i, j: (j, 0)), pl.BlockSpec((tf, d), lambda i, j: (j, 0))],
        out_specs=(pl.BlockSpec((tm, tf), lambda i, j: (i, j)), pl.BlockSpec((tm, tf), lambda i, j: (i, j)),
                   pl.BlockSpec((tm, d), lambda i, j: (i, 0)), pl.BlockSpec((1, d), lambda i, j: (0, 0))),
        scratch_shapes=[pltpu.VMEM((tm, d), F32), pltpu.VMEM((tm, d), BF16)],
        compiler_params=_cparams("arbitrary", "arbitrary"),
    )(dx2, x1, g, gt, up, wgu, wd)


def _merge_bwd(dx1, ps, o, g2, wpo, wao, wout, tm, dep=None):
    t, d = dx1.shape
    tm = min(tm, t)

    def body(dx1_ref, ps_ref, o_ref, gp_ref, ga_ref, wpo_ref, wao_ref, wout_ref, dpy_ref, day_ref, dg2_ref, dps_ref, da_ref):
        dm = _dot_nt(dx1_ref[...].astype(BF16), wout_ref[...])
        py = _dot(ps_ref[...], wpo_ref[...])
        ay = _dot(o_ref[...].astype(BF16), wao_ref[...])
        sp = _sigmoid(gp_ref[...].astype(F32))
        sa = _sigmoid(ga_ref[...].astype(F32))
        dpy = (dm * sp).astype(BF16)
        day = (dm * sa).astype(BF16)
        dpy_ref[...] = dpy
        day_ref[...] = day
        dg2_ref[:, :d] = (dm * py * (sp * (1.0 - sp))).astype(BF16)
        dg2_ref[:, d:] = (dm * ay * (sa * (1.0 - sa))).astype(BF16)
        dps_ref[...] = _dot_nt(dpy, wpo_ref[...])
        da_ref[...] = _dot_nt(day, wao_ref[...]).astype(BF16)

    row = lambda w: pl.BlockSpec((tm, w), lambda i: (i, 0))
    full = lambda a: pl.BlockSpec(a.shape, lambda i: (0, 0))
    dep_specs, dep_ops = _dep_args(dep)
    return pl.pallas_call(
        _after(body, 8, dep), name="merge_bwd",
        out_shape=(jax.ShapeDtypeStruct((t, d), BF16), jax.ShapeDtypeStruct((t, d), BF16),
                   jax.ShapeDtypeStruct((t, 2 * d), BF16), jax.ShapeDtypeStruct((t, POOL_WIDTH), F32),
                   jax.ShapeDtypeStruct((t, ATTN_WIDTH), BF16)),
        grid=(t // tm,),
        in_specs=[row(d), row(POOL_WIDTH), row(ATTN_WIDTH), pl.BlockSpec((tm, d), lambda i: (i, 0)),
                  pl.BlockSpec((tm, d), lambda i: (i, 1)), full(wpo), full(wao), full(wout)] + dep_specs,
        out_specs=(row(d), row(d), row(2 * d), row(POOL_WIDTH), row(ATTN_WIDTH)),
        compiler_params=_cparams("parallel"),
    )(dx1, ps, o, g2, g2, wpo, wao, wout, *dep_ops)


def _attn_bwd(qa, ka, v, do, lse4, seq, tq, dep=None):
    t = qa.shape[0]
    nq = seq // tq
    hp_n = N_HEADS // 2
    heads = [slice(e * LANES, (e + 1) * LANES) for e in range(2)]

    def body(q_ref, k_ref, v_ref, do_ref, lse_ref, dq_ref, dk_ref, dv_ref, dfr_ref, dk_acc, dv_acc, p_buf, dp_buf):
        diag_ok = lax.broadcasted_iota(jnp.int32, (tq, tq), 0) >= lax.broadcasted_iota(jnp.int32, (tq, tq), 1)
        lane_q = lax.broadcasted_iota(jnp.int32, (tq, LANES), 1)
        lane_s = lax.broadcasted_iota(jnp.int32, (seq, LANES), 1)
        mine_q = [lane_q < HEAD_DIM, lane_q >= HEAD_DIM]
        dv_acc[...] = jnp.zeros_like(dv_acc)
        dk_acc[...] = jnp.zeros_like(dk_acc)
        dfr_ref[...] = jnp.zeros_like(dfr_ref)

        def q_step(i, _):
            q0 = pl.multiple_of(i * tq, tq)
            qs = [q_ref[pl.ds(q0, tq), hl] for hl in heads]
            dov = do_ref[pl.ds(q0, tq), :]
            dos = [jnp.where(mq, dov, jnp.zeros((), BF16)) for mq in mine_q]
            lss = [lse_ref[pl.ds(q0, tq), e:e + 1] for e in range(2)]

            def sweep1(j, dls, diagonal):
                r0 = pl.multiple_of(j * tq, tq)
                vv = v_ref[pl.ds(r0, tq), :]
                out = []
                for e, hl in enumerate(heads):
                    s = _dot_nt(qs[e], k_ref[pl.ds(r0, tq), hl])
                    if diagonal:
                        s = jnp.where(diag_ok, s, NEG_BIG)
                    p = jnp.exp(s - lss[e])
                    dp = _dot_nt(dos[e], vv)
                    p_buf[e, j] = p
                    dp_buf[e, j] = dp
                    dv_acc[pl.ds(r0, tq), :] += _dot_tn(p.astype(BF16), dos[e])
                    out.append(dls[e] + _fold_lanes(p * dp, jnp.add))
                return tuple(out)

            dls = _causal_sweep(i, sweep1, (jnp.zeros((tq, LANES), F32),) * 2)
            dls = [jnp.sum(d, axis=1, keepdims=True) for d in dls]

            def sweep2(j, dqs, diagonal):
                r0 = pl.multiple_of(j * tq, tq)
                out = []
                for e, hl in enumerate(heads):
                    ds = p_buf[e, j] * (dp_buf[e, j] - dls[e])
                    dfr_ref[e, pl.ds(j, 1), :] += jnp.sum(ds, axis=0, keepdims=True)
                    dsb = ds.astype(BF16)
                    dk_acc[e, pl.ds(r0, tq), :] += _dot_tn(dsb, qs[e])
                    out.append(dqs[e] + _dot(dsb, k_ref[pl.ds(r0, tq), hl]))
                return tuple(out)

            dqs = _causal_sweep(i, sweep2, (jnp.zeros((tq, LANES), F32),) * 2)
            dq = jnp.where(mine_q[0], dqs[0], pltpu.roll(dqs[1], HEAD_DIM, 1)) * ATTN_SCALE
            dq_ref[pl.ds(q0, tq), :] = dq.astype(BF16)
            return 0

        lax.fori_loop(0, nq, q_step, 0)
        dk_ref[...] = jnp.where(lane_s < HEAD_DIM, dk_acc[0], pltpu.roll(dk_acc[1], HEAD_DIM, 1)).astype(BF16)
        dv_ref[...] = dv_acc[...].astype(BF16)

    wide = pl.BlockSpec((seq, 2 * LANES), lambda b, hp: (b, hp))
    col = pl.BlockSpec((seq, LANES), lambda b, hp: (b, hp))
    pair = pl.BlockSpec((None, seq, 2), lambda b, hp: (hp, b, 0))
    dep_specs, dep_ops = _dep_args(dep)
    return pl.pallas_call(
        _after(body, 5, dep), name="attn_bwd",
        out_shape=(jax.ShapeDtypeStruct((t, ATTN_WIDTH), BF16),) * 3 + (jax.ShapeDtypeStruct((N_HEADS, t // tq, tq), F32),),
        grid=(t // seq, hp_n),
        in_specs=[wide, wide, col, col, pair] + dep_specs,
        out_specs=(col, col, col, pl.BlockSpec((2, nq, tq), lambda b, hp: (hp, b, 0))),
        scratch_shapes=[pltpu.VMEM((2, seq, LANES), F32), pltpu.VMEM((seq, LANES), F32),
                        pltpu.VMEM((2, nq, tq, tq), F32), pltpu.VMEM((2, nq, tq, tq), F32)],
        compiler_params=_cparams("parallel", "arbitrary"),
    )(qa, ka, v, do, lse4, *dep_ops)


def _forget_bwd(dfc, fl, bf, seq):
    t = fl.shape[0]
    cb = min(256, seq)
    nb = seq // cb

    def body(dfc_ref, fl_ref, bf_ref, dfl_ref, db_ref):
        b = pl.program_id(0)
        ri = lax.broadcasted_iota(jnp.int32, (cb, cb), 0)
        ci = lax.broadcasted_iota(jnp.int32, (cb, cb), 1)
        tri = (ci >= ri).astype(BF16)
        carry = jnp.zeros((1, LANES), F32)
        dbs = jnp.zeros((1, LANES), F32)
        for blk in reversed(range(nb)):
            rs = slice(blk * cb, (blk + 1) * cb)
            dlf = _tri_dot(tri, -dfc_ref[rs, :]) + carry
            carry = dlf[0:1, :]
            dfl = dlf * _sigmoid(-(fl_ref[rs, :] + bf_ref[...]))
            dfl_ref[rs, :] = dfl.astype(BF16)
            dbs = dbs + jnp.sum(dfl, axis=0, keepdims=True)

        @pl.when(b == 0)
        def _():
            db_ref[...] = jnp.zeros_like(db_ref)

        db_ref[...] += dbs

    return pl.pallas_call(
        body, name="forget_bwd",
        out_shape=(jax.ShapeDtypeStruct((t, LANES), BF16), jax.ShapeDtypeStruct((1, LANES), F32)),
        grid=(t // seq,),
        in_specs=[pl.BlockSpec((seq, LANES), lambda b: (b, 0)), pl.BlockSpec((seq, LANES), lambda b: (b, 0)),
                  pl.BlockSpec((1, LANES), lambda b: (0, 0))],
        out_specs=(pl.BlockSpec((seq, LANES), lambda b: (b, 0)), pl.BlockSpec((1, LANES), lambda b: (0, 0))),
        compiler_params=_cparams("arbitrary"),
    )(dfc, fl, bf)


def _pool_bwd(dps, p, mix, scale, seq):
    t = dps.shape[0]

    def body(dps_ref, p_ref, mix_ref, sc_ref, du_ref, dmix_ref, dsc_ref):
        b = pl.program_id(0)

        @pl.when(b == 0)
        def _():
            dmix_ref[...] = jnp.zeros_like(dmix_ref)
            dsc_ref[...] = jnp.zeros_like(dsc_ref)

        tpos = lax.broadcasted_iota(jnp.int32, (seq, POOL_GROUP_DIM), 0)
        for g in range(POOL_GROUPS):
            sl = slice(g * POOL_GROUP_DIM, (g + 1) * POOL_GROUP_DIM)
            pb = p_ref[:, sl]
            dpsg = dps_ref[:, sl]
            pm = _dot(pb, mix_ref[g])
            dsc_ref[:, sl] += jnp.sum(dpsg * pm, axis=0, keepdims=True)
            dpm = (dpsg * sc_ref[:, sl]).astype(BF16)
            dmix_ref[g] += _dot_tn(pb, dpm)
            dp = _dot_nt(dpm, mix_ref[g])
            cnt = jnp.minimum(tpos + 1, POOL_WINDOWS[g]).astype(F32)
            s = dp / cnt
            for lvl in range(g + 1):
                d = 2 ** lvl
                s = s + jnp.where(tpos < seq - d, pltpu.roll(s, seq - d, 0), 0.0)
            du_ref[:, sl] = (s - dp).astype(BF16)

    return pl.pallas_call(
        body, name="pool_bwd",
        out_shape=(jax.ShapeDtypeStruct((t, POOL_WIDTH), BF16),
                   jax.ShapeDtypeStruct((POOL_GROUPS, POOL_GROUP_DIM, POOL_GROUP_DIM), F32),
                   jax.ShapeDtypeStruct((1, POOL_WIDTH), F32)),
        grid=(t // seq,),
        in_specs=[pl.BlockSpec((seq, POOL_WIDTH), lambda b: (b, 0)), pl.BlockSpec((seq, POOL_WIDTH), lambda b: (b, 0)),
                  pl.BlockSpec((POOL_GROUPS, POOL_GROUP_DIM, POOL_GROUP_DIM), lambda b: (0, 0, 0)),
                  pl.BlockSpec((1, POOL_WIDTH), lambda b: (0, 0))],
        out_specs=(pl.BlockSpec((seq, POOL_WIDTH), lambda b: (b, 0)),
                   pl.BlockSpec((POOL_GROUPS, POOL_GROUP_DIM, POOL_GROUP_DIM), lambda b: (0, 0, 0)),
                   pl.BlockSpec((1, POOL_WIDTH), lambda b: (0, 0))),
        compiler_params=_cparams("arbitrary"),
    )(dps, p, mix, scale)


def _in_bwd(du, dq, dk, dv, dg2, dfl, dx1, x, g, wu, wqkv, wg2, wft, tm):
    t, d = x.shape
    tm = min(tm, t)
    aw = ATTN_WIDTH

    def body(du_ref, dq_ref, dk_ref, dv_ref, dg2_ref, dfl_ref, dx1_ref, x_ref, g_ref, wu_ref, wqkv_ref, wg2_ref, wft_ref,
             dx_ref, dg_ref):
        i = pl.program_id(0)
        dh = _dot(du_ref[...], wu_ref[...])
        dh += _dot(dq_ref[...], wqkv_ref[0:aw, :])
        dh += _dot(dk_ref[...], wqkv_ref[aw:2 * aw, :])
        dh += _dot(dv_ref[...], wqkv_ref[2 * aw:3 * aw, :])
        dh += _dot(dg2_ref[...], wg2_ref[...])
        dh += _dot(dfl_ref[...], wft_ref[...])
        dxn, dg = _rms_bwd(x_ref[...], g_ref[...], dh)
        dx_ref[...] = dx1_ref[...] + dxn

        @pl.when(i == 0)
        def _():
            dg_ref[...] = jnp.zeros_like(dg_ref)

        dg_ref[...] += dg

    row = lambda w: pl.BlockSpec((tm, w), lambda i: (i, 0))
    full = lambda a: pl.BlockSpec(a.shape, lambda i: (0, 0))
    return pl.pallas_call(
        body, name="in_bwd",
        out_shape=(jax.ShapeDtypeStruct((t, d), F32), jax.ShapeDtypeStruct((1, d), F32)),
        grid=(t // tm,),
        in_specs=[row(POOL_WIDTH), row(aw), row(aw), row(aw), row(2 * d), row(LANES), row(d), row(d),
                  pl.BlockSpec((1, d), lambda i: (0, 0)), full(wu), full(wqkv), full(wg2), full(wft)],
        out_specs=(row(d), pl.BlockSpec((1, d), lambda i: (0, 0))),
        compiler_params=_cparams("arbitrary"),
    )(du, dq, dk, dv, dg2, dfl, dx1, x, g, wu, wqkv, wg2, wft)


def _position():
    return lax.axis_index("x"), lax.axis_index("y"), lax.axis_index("c")


def _remote(src, dst, send_sem, recv_sem, device):
    return pltpu.make_async_remote_copy(src_ref=src, dst_ref=dst, send_sem=send_sem, recv_sem=recv_sem,
                                        device_id=device, device_id_type=MESH)


HBM = pl.BlockSpec(memory_space=pltpu.HBM)
SEM = pl.BlockSpec(memory_space=pltpu.SEMAPHORE)
DATAFLOW = pltpu.SideEffectType.DATAFLOW_SIDE_EFFECTING


def _copies_start(name, arrays, plan, m, dep=None):
    n = len(arrays)
    arrays = [pltpu.with_memory_space_constraint(a, pltpu.HBM) for a in arrays]

    def body(*refs):
        ins, send_sem, recv_sem, token = refs[:n], refs[n], refs[n + 1], refs[2 * n + 2]
        for i, (src, dst, device, _) in enumerate(plan(ins, *_position())):
            _remote(src, dst, send_sem.at[i], recv_sem.at[i], device).start()
        token[...] = jnp.zeros_like(token)

    dep_specs, dep_ops = _dep_args(dep)
    outs = pl.pallas_call(
        _after(body, n, dep), name=name,
        out_shape=(pltpu.SemaphoreType.DMA((m,)), pltpu.SemaphoreType.DMA((m,)),
                   *[pltpu.HBM(a.shape, a.dtype) for a in arrays], jax.ShapeDtypeStruct((8, LANES), F32)),
        in_specs=[HBM] * n + dep_specs, out_specs=(SEM, SEM, *[HBM] * n, pl.BlockSpec(memory_space=pltpu.VMEM)),
        input_output_aliases={i: i + 2 for i in range(n)},
        compiler_params=pltpu.CompilerParams(has_side_effects=DATAFLOW),
    )(*arrays, *dep_ops)
    return (outs[0], outs[1]), list(outs[2:2 + n]), outs[2 + n]


def _copies_wait(name, sems, arrays, plan, after):
    n = len(arrays)
    afters = list(after) if isinstance(after, (list, tuple)) else [after]

    def body(*refs):
        ins, send_sem, recv_sem = refs[:n], refs[n], refs[n + 1]
        for i, (src, dst, device, landing) in enumerate(plan(ins, *_position())):
            _remote(src, dst, send_sem.at[i], recv_sem.at[i], device).wait_send()
            _remote(landing, landing, send_sem.at[i], recv_sem.at[i], device).wait_recv()

    outs = pl.pallas_call(
        body, name=name,
        out_shape=tuple(pltpu.HBM(a.shape, a.dtype) for a in arrays),
        in_specs=[HBM] * n + [SEM, SEM] + [ANY] * len(afters), out_specs=tuple([HBM] * n),
        input_output_aliases={i: i for i in range(n)},
        compiler_params=pltpu.CompilerParams(has_side_effects=DATAFLOW),
    )(*arrays, sems[0], sems[1], *afters)
    return list(outs)


def _tie(x, dep):
    for token in _dep_list(dep):
        x = x + token[0, 0]
    return x


def _other_chips(x, y):
    return [(1 - x, y), (x, 1 - y), (1 - x, 1 - y)]


def _gather_begin(tag, shards, token, column_halves=False):
    n = len(shards)
    lands = [lax.empty((N_CHIPS,) + s.shape, s.dtype) for s in shards]
    if column_halves:
        cols = lambda ref, h: pl.ds(pl.multiple_of(h * (ref.shape[-1] // 2), LANES), ref.shape[-1] // 2)
        mine = lambda ref, h: ref.at[:, cols(ref, h)]
        landed = lambda ref, chip, h: ref.at[chip, :, cols(ref, h)]
    else:
        mine = lambda ref, h: ref.at[h]
        landed = lambda ref, chip, h: ref.at[chip, h]

    def plan(refs, x, y, c):
        return [(mine(refs[k], c), landed(refs[n + k], 2 * x + y, c), (ox, oy, c), landed(refs[n + k], 2 * ox + oy, c))
                for k in range(n) for ox, oy in _other_chips(x, y)]

    sems, thru, token = _copies_start(f"gather_{tag}_ici_start", list(shards) + lands, plan, 3 * n, dep=token)
    return dict(tag=tag, n=n, plan=plan, sems=sems, arrays=thru, token=token, landed=landed)


def _gather_forward(st, after):
    n, tag, landed = st["n"], st["tag"], st["landed"]
    thru = _copies_wait(f"gather_{tag}_ici_wait", st["sems"], st["arrays"], st["plan"], after)

    def plan(refs, x, y, c):
        return [(landed(refs[k], 2 * ox + oy, c), landed(refs[k], 2 * ox + oy, c), (x, y, 1 - c),
                 landed(refs[k], 2 * ox + oy, 1 - c))
                for k in range(n) for ox, oy in _other_chips(x, y)]

    sems, lands, token = _copies_start(f"gather_{tag}_fwd_start", thru[n:], plan, 3 * n)
    return dict(tag=tag, n=n, plan=plan, sems=sems, arrays=lands, token=token, shards=thru[:n])


def _gather_end(st, after, merge=True):
    lands = _copies_wait(f"gather_{st['tag']}_fwd_wait", st["sems"], st["arrays"], st["plan"], after)
    if not merge:
        return lands, st["shards"]
    me = 2 * lax.axis_index("x") + lax.axis_index("y")
    return [lax.dynamic_update_index_in_dim(g, s, me, 0) for g, s in zip(lands, st["shards"])]


def _add_keep_give(name, pos, a, a_keep, a_give, b, b_keep, b_give, steps):
    r, c = b.shape[-2:]

    def spec(arr, fn):
        lead = arr.ndim - 2

        def index(i, p):
            idx = tuple(fn(i, p))
            return idx if len(idx) == arr.ndim else idx + (0, 0)

        return pl.BlockSpec((None,) * lead + (r, c), index)

    out_spec = pl.BlockSpec((None, r, c), lambda i, p: (i, 0, 0))

    def body(p_ref, ak_ref, bk_ref, ag_ref, bg_ref, keep_ref, give_ref):
        keep_ref[...] = ak_ref[...] + bk_ref[...].astype(F32)
        give_ref[...] = (ag_ref[...] + bg_ref[...].astype(F32)).astype(BF16)

    return pl.pallas_call(
        body, name=name,
        out_shape=(jax.ShapeDtypeStruct((steps, r, c), F32), jax.ShapeDtypeStruct((steps, r, c), BF16)),
        grid_spec=pltpu.PrefetchScalarGridSpec(
            num_scalar_prefetch=1, grid=(steps,),
            in_specs=[spec(a, a_keep), spec(b, b_keep), spec(a, a_give), spec(b, b_give)],
            out_specs=(out_spec, out_spec)),
        compiler_params=_cparams("parallel"),
    )(pos, a, b, a, b)


def _add_last(name, a, b):
    _, r, c = a.shape
    blk = pl.BlockSpec((None, r, c), lambda i: (0, 0, 0))

    def body(a_ref, b_ref, o_ref):
        o_ref[...] = a_ref[...] + b_ref[...].astype(F32)

    return pl.pallas_call(
        body, name=name, out_shape=jax.ShapeDtypeStruct((r, c), F32), grid=(1,), in_specs=[blk, blk],
        out_specs=pl.BlockSpec((r, c), lambda i: (0, 0)), compiler_params=_cparams("arbitrary"),
    )(a, b)


def _exchange_begin(tag, stage, gives, lands, peer_fn, extra):
    n = len(gives)

    def plan(refs, x, y, c):
        return [(refs[k], refs[n + k], peer_fn(x, y, c), refs[n + k]) for k in range(n)]

    sems, thru, token = _copies_start(f"rs{tag}_{stage}_start", gives + lands, plan, n)
    return dict(extra, tag=tag, n=n, stage=stage, plan=plan, sems=sems, arrays=thru, token=token)


def _reduce_begin(tag, grads, column_halves=False):
    n = len(grads)
    if column_halves:
        half = lambda ref, j, h: ref.at[j, :, pl.ds(pl.multiple_of(h * (ref.shape[2] // 2), LANES), ref.shape[2] // 2)]
        lands = [lax.empty((N_CHIPS, g.shape[1], g.shape[2] // 2), F32) for g in grads]
    else:
        half = lambda ref, j, h: ref.at[j, h]
        lands = [lax.empty((N_CHIPS,) + g.shape[2:], F32) for g in grads]

    def plan(refs, x, y, c):
        return [(half(refs[k], j, 1 - c), refs[n + k].at[j], (x, y, 1 - c), refs[n + k].at[j])
                for k in range(n) for j in range(N_CHIPS)]

    sems, thru, token = _copies_start(f"rs{tag}_c_start", list(grads) + lands, plan, N_CHIPS * n)
    return dict(tag=tag, n=n, stage="c", plan=plan, sems=sems, arrays=thru, token=token, column_halves=column_halves)


def _reduce_advance(st, after):
    tag, n, stage = st["tag"], st["n"], st["stage"]
    thru = _copies_wait(f"rs{tag}_{stage}_wait", st["sems"], st["arrays"], st["plan"], after)
    first, recv = thru[:n], thru[n:]
    x, y, c = _position()
    if stage == "c":
        pos = jnp.stack([c, x]).astype(jnp.int32)
        if st["column_halves"]:
            mine = lambda chip: (lambda i, p: (chip(p) + i, 0, p[0]))
        else:
            mine = lambda chip: (lambda i, p: (chip(p) + i, p[0]))
        sums = [_add_keep_give(
            f"rs{tag}_c_add{k}", pos,
            first[k], mine(lambda p: 2 * p[1]), mine(lambda p: 2 * (1 - p[1])),
            recv[k], lambda i, p: (2 * p[1] + i,), lambda i, p: (2 * (1 - p[1]) + i,), 2) for k in range(n)]
        lands = [lax.empty(s[1].shape, BF16) for s in sums]
        return _exchange_begin(tag, "x", [s[1] for s in sums], lands, lambda x, y, c: (1 - x, y, c),
                               dict(keep=[s[0] for s in sums]))
    if stage == "x":
        pos = jnp.stack([y]).astype(jnp.int32)
        sums = [_add_keep_give(
            f"rs{tag}_x_add{k}", pos,
            st["keep"][k], lambda i, p: (p[0],), lambda i, p: (1 - p[0],),
            recv[k], lambda i, p: (p[0],), lambda i, p: (1 - p[0],), 1) for k in range(n)]
        lands = [lax.empty(s[1].shape, BF16) for s in sums]
        return _exchange_begin(tag, "y", [s[1] for s in sums], lands, lambda x, y, c: (x, 1 - y, c),
                               dict(keep=[s[0] for s in sums]))
    if stage == "y":
        mine = [_add_last(f"rs{tag}_y_add{k}", st["keep"][k], recv[k]) for k in range(n)]
        lands = [lax.empty(m.shape, F32) for m in mine]
        return _exchange_begin(tag, "swap", mine, lands, lambda x, y, c: (x, y, 1 - c), {})
    return dict(done=list(zip(first, recv)), token=None)


def _small_begin(v, dep):
    land = lax.empty((N_DEV,) + v.shape, F32)
    flips = [(fx, fy, fc) for fx in (0, 1) for fy in (0, 1) for fc in (0, 1)][1:]

    def plan(refs, x, y, c):
        copies = []
        for fx, fy, fc in flips:
            px, py, pc = (1 - x if fx else x), (1 - y if fy else y), (1 - c if fc else c)
            copies.append((refs[0], refs[1].at[4 * x + 2 * y + c], (px, py, pc), refs[1].at[4 * px + 2 * py + pc]))
        return copies

    sems, thru, token = _copies_start("small_start", [v, land], plan, len(flips), dep=dep)
    return dict(plan=plan, sems=sems, arrays=thru, token=token)


def _small_end(st, after):
    own, land = _copies_wait("small_wait", st["sems"], st["arrays"], st["plan"], after)
    x, y, c = _position()
    me = jnp.stack([4 * x + 2 * y + c]).astype(jnp.int32)

    def body(me_ref, own_ref, land_ref, out_ref):
        term = lambda dev: jnp.where(me_ref[0] == dev, own_ref[...], land_ref[dev])
        acc = term(0)
        for dev in range(1, N_DEV):
            acc = acc + term(dev)
        out_ref[...] = acc

    return pl.pallas_call(
        body, name="small_sum", out_shape=jax.ShapeDtypeStruct(own.shape, F32),
        grid_spec=pltpu.PrefetchScalarGridSpec(
            num_scalar_prefetch=1, grid=(1,),
            in_specs=[pl.BlockSpec(own.shape, lambda i, m: (0, 0)), pl.BlockSpec(land.shape, lambda i, m: (0, 0, 0))],
            out_specs=pl.BlockSpec(own.shape, lambda i, m: (0, 0))),
        compiler_params=_cparams("arbitrary"),
    )(me, own, land)


def _all_reduce_small(v):
    r = v.shape[0]

    def body(v_ref, out_ref, buf, send_sems, recv_sems, local_sem):
        x, y, c = _position()
        me, sibling = (x, y, c), (x, y, 1 - c)
        chips = [(1 - x, y), (x, 1 - y), (1 - x, 1 - y)]

        def rows(px, py, pc):
            return buf.at[pl.ds((4 * px + 2 * py + pc) * r, r), :]

        def copy(k, block, to, src=None):
            return _remote(rows(*block) if src is None else src, rows(*block), send_sems.at[k], recv_sems.at[k], to)

        mine = pltpu.make_async_copy(v_ref, rows(*me), local_sem)
        mine.start()
        first = [copy(0, me, sibling, src=v_ref)]
        first += [copy(1 + j, me, (*chip, c), src=v_ref) for j, chip in enumerate(chips)]
        for cp in first:
            cp.start()
        passed = [copy(4 + j, (*chip, c), sibling) for j, chip in enumerate(chips)]
        for j, chip in enumerate(chips):
            copy(1 + j, (*chip, c), me).wait_recv()
            passed[j].start()
        copy(0, sibling, me).wait_recv()
        for j, chip in enumerate(chips):
            copy(4 + j, (*chip, 1 - c), me).wait_recv()
        for cp in first + passed:
            cp.wait_send()
        mine.wait()
        acc = buf[0:r, :]
        for dev in range(1, N_DEV):
            acc = acc + buf[dev * r:(dev + 1) * r, :]
        out_ref[...] = acc

    return pl.pallas_call(
        body, name="all_reduce_small",
        out_shape=jax.ShapeDtypeStruct(v.shape, F32),
        in_specs=[pl.BlockSpec(memory_space=pltpu.VMEM)],
        out_specs=pl.BlockSpec(memory_space=pltpu.VMEM),
        scratch_shapes=[pltpu.VMEM((N_DEV * r, LANES), F32), pltpu.SemaphoreType.DMA((7,)),
                        pltpu.SemaphoreType.DMA((7,)), pltpu.SemaphoreType.DMA],
        compiler_params=pltpu.CompilerParams(has_side_effects=True, vmem_limit_bytes=VMEM_LIMIT_V7X),
    )(v)


def _adamw_update(w, gg, m, v):
    mn = ADAM_B1 * m + (1.0 - ADAM_B1) * gg
    vn = ADAM_B2 * v + (1.0 - ADAM_B2) * (gg * gg)
    m_hat = mn / (1.0 - ADAM_B1 ** ADAM_STEP)
    v_hat = vn / (1.0 - ADAM_B2 ** ADAM_STEP)
    return -ADAM_LR * (m_hat / (jnp.sqrt(v_hat) + ADAM_EPS) + ADAM_WD * w), mn, vn


def _adamw(name, w, g, m, v):
    def body(w_ref, g_ref, m_ref, v_ref, d_ref, mo_ref, vo_ref):
        d_ref[...], mo_ref[...], vo_ref[...] = _adamw_update(w_ref[...], g_ref[...], m_ref[...], v_ref[...])

    blk = pl.BlockSpec(w.shape, lambda i: (0, 0))
    return pl.pallas_call(
        body, name=name, out_shape=(jax.ShapeDtypeStruct(w.shape, F32),) * 3, grid=(1,),
        in_specs=[blk] * 4, out_specs=(blk,) * 3, compiler_params=_cparams("arbitrary"),
    )(w, g, m, v)


def _rows_to_bf16(name, w):
    r, _, c = w.shape

    def body(w_ref, o_ref):
        o_ref[...] = w_ref[:, 0, :].astype(BF16)

    return pl.pallas_call(
        body, name=name, out_shape=jax.ShapeDtypeStruct((r, c), BF16), grid=(1,),
        in_specs=[pl.BlockSpec((r, 1, c), lambda i: (0, 0, 0))], out_specs=pl.BlockSpec((r, c), lambda i: (0, 0)),
        compiler_params=_cparams("arbitrary"),
    )(w)


def _adamw_rows(name, pos_c, w, g_mine, g_other, m, v):
    r, _, c = w.shape
    ch = c // 2

    def body(p_ref, w_ref, gm_ref, go_ref, m_ref, v_ref, g_ref, d_ref, mo_ref, vo_ref):
        gg = jnp.where(pl.program_id(0) == p_ref[0], gm_ref[...], go_ref[...])
        dl, mn, vn = _adamw_update(w_ref[:, 0, :], gg, m_ref[:, 0, :], v_ref[:, 0, :])
        g_ref[:, 0, :] = gg
        d_ref[:, 0, :] = dl
        mo_ref[:, 0, :] = mn
        vo_ref[:, 0, :] = vn

    rows = pl.BlockSpec((r, 1, ch), lambda h, p: (0, 0, h))
    half = pl.BlockSpec((r, ch), lambda h, p: (0, 0))
    return pl.pallas_call(
        body, name=name, out_shape=(jax.ShapeDtypeStruct(w.shape, F32),) * 4,
        grid_spec=pltpu.PrefetchScalarGridSpec(
            num_scalar_prefetch=1, grid=(2,), in_specs=[rows, half, half, rows, rows], out_specs=(rows,) * 4),
        compiler_params=_cparams("parallel"),
    )(pos_c, w, g_mine, g_other, m, v)


def _adamw_halves(name, pos_c, w, g_mine, g_other, m, v, tr, dep=None):
    r, c = w.shape
    rh = r // 2
    tr = tr if rh % tr == 0 else rh
    nt = rh // tr

    def body(p_ref, w_ref, gm_ref, go_ref, m_ref, v_ref, g_ref, d_ref, mo_ref, vo_ref):
        gg = jnp.where(pl.program_id(0) == p_ref[0], gm_ref[...], go_ref[...])
        g_ref[...] = gg
        d_ref[...], mo_ref[...], vo_ref[...] = _adamw_update(w_ref[...], gg, m_ref[...], v_ref[...])

    full = pl.BlockSpec((tr, c), lambda h, i, p: (h * nt + i, 0))
    half = pl.BlockSpec((tr, c), lambda h, i, p: (i, 0))
    dep_specs, dep_ops = _dep_args(dep)
    return pl.pallas_call(
        _after(body, 6, dep), name=name, out_shape=(jax.ShapeDtypeStruct((r, c), F32),) * 4,
        grid_spec=pltpu.PrefetchScalarGridSpec(
            num_scalar_prefetch=1, grid=(2, nt),
            in_specs=[full, half, half, full, full] + dep_specs, out_specs=(full,) * 4),
        compiler_params=_cparams("parallel", "parallel"),
    )(pos_c, w, g_mine, g_other, m, v, *dep_ops)


def _col_sharded_to_comm(g):
    k, n = g.shape
    return g.reshape(2, k // 2, N_CHIPS, n // N_CHIPS).transpose(2, 0, 1, 3)


def _row_sharded_to_comm(g):
    r, c = g.shape
    return g.reshape(N_CHIPS, 2, r // (2 * N_CHIPS), c)


def _col_sharded_full(g):
    _, _, rh, c = g.shape
    return g.reshape(N_CHIPS, 2 * rh, c).transpose(1, 0, 2).reshape(2 * rh, N_CHIPS * c)


def _row_sharded_full(g):
    _, _, rh, c = g.shape
    return g.reshape(N_CHIPS * 2 * rh, c)


def _chip_rows(w3, start, stop, own=None, me=None):
    r = w3.shape[1]
    parts = []
    for chip in range(N_CHIPS):
        lo, hi = max(start - chip * r, 0), min(stop - chip * r, r)
        if lo < hi:
            part = w3[chip, lo:hi]
            parts.append(part if own is None else jnp.where(me == chip, own[lo:hi], part))
    return parts[0] if len(parts) == 1 else jnp.concatenate(parts, axis=0)


def _pack_small(g1, bfv, mix, scale, g2n, gf, extra=None):
    row8 = jnp.pad(bfv.reshape(1, N_HEADS), ((0, 0), (0, LANES - N_HEADS)))
    if extra is not None:
        row8 = row8 + jnp.pad(extra[:, :1], ((0, 0), (N_HEADS, LANES - N_HEADS - 1)))
    return jnp.concatenate([
        g1.reshape(8, LANES), jnp.pad(row8, ((0, 7), (0, 0))), mix.reshape(512, LANES),
        jnp.pad(scale.reshape(4, LANES), ((0, 4), (0, 0))), g2n.reshape(8, LANES), gf.reshape(8, LANES)], axis=0)


def _unpack_small(s, like):
    g1, bfv, mix, scale, g2n, gf = like
    return (s[0:8].reshape(g1.shape), s[8, :N_HEADS].reshape(bfv.shape), s[16:528].reshape(mix.shape),
            s[528:532].reshape(scale.shape), s[536:544].reshape(g2n.shape), s[544:552].reshape(gf.shape))


class _MeshLinks:
    def __init__(self, shards_in, shards_rest):
        self.gin = _gather_begin("in", shards_in, None, column_halves=True)
        self.grest = _gather_begin("rest", shards_rest, self.gin["token"])
        self.tokens = {"gather": self.grest["token"]}
        self.groups = {}

    @property
    def token(self):
        return list(self.tokens.values())

    def tie(self, x):
        return _tie(x, self.token)

    def weights_in(self, after):
        st = _gather_forward(self.gin, after)
        (g,), (own,) = _gather_end(st, st["token"], merge=False)
        return g, own, 2 * lax.axis_index("x") + lax.axis_index("y")

    def rest_forward(self, after):
        self.grest = _gather_forward(self.grest, after)
        self.tokens["gather"] = self.grest["token"]

    def weights_rest(self, after):
        g = _gather_end(self.grest, after)
        del self.tokens["gather"]
        return [_col_sharded_full(g[0]), _col_sharded_full(g[1])] + [_row_sharded_full(a) for a in g[2:]]

    def reduce_begin(self, tag, grads, column_halves=False):
        self.groups[tag] = _reduce_begin(tag, grads, column_halves)
        self.tokens[tag] = self.groups[tag]["token"]

    def advance(self, after):
        for tag, st in self.groups.items():
            if "done" not in st:
                self.groups[tag] = _reduce_advance(st, after)
                if self.groups[tag]["token"] is None:
                    del self.tokens[tag]
                else:
                    self.tokens[tag] = self.groups[tag]["token"]

    def reduced(self, tag):
        return self.groups[tag]["done"]


class _NoLinks:
    token = None

    def __init__(self, w_in, rest):
        self.w_in, self.rest, self.grads = w_in, rest, {}

    def tie(self, x):
        return x

    def weights_in(self, after):
        return self.w_in, None, None

    def rest_forward(self, after):
        pass

    def weights_rest(self, after):
        return self.rest

    def reduce_begin(self, tag, grads, column_halves=False):
        self.grads[tag] = grads

    def advance(self, after):
        pass


def _local_step(links, x, target, seq, norm1_g, b_forget, pool_mix, pool_scale, norm2_g, norm_f_g):
    t, d = x.shape
    tq = min(256, seq)
    aw = ATTN_WIDTH
    o_q, o_f, o_g = POOL_WIDTH, POOL_WIDTH + 3 * aw, POOL_WIDTH + 3 * aw + N_HEADS
    bf = jnp.pad(b_forget, ((0, 0), (0, LANES - N_HEADS)))
    mixb = pool_mix.astype(BF16)

    h = _norm_fwd("norm1_fwd", x, links.tie(norm1_g), 512)
    w_in, own, me = links.weights_in(h)
    wu = _chip_rows(w_in, 0, o_q, own, me)
    wqkv = _chip_rows(w_in, o_q, o_f, own, me)
    wft = jnp.pad(_chip_rows(w_in, o_f, o_g, own, me), ((0, LANES - N_HEADS), (0, 0)))
    wg2 = _chip_rows(w_in, o_g, N_CHIPS * w_in.shape[1], own, me)
    wf = wft.T
    u = _matmul("mm_u", h, wu, "nt", F32, 1024, 512, d)
    g2 = _matmul("mm_gates", h, wg2, "nt", BF16, 1024, 1024, d)
    fl, fcum = _forget_fwd(h, wf, bf, seq)
    qa, ka, v = _attn_prep(h, _head_blocks(wqkv[:aw]), _head_blocks(wqkv[aw:2 * aw]), wqkv[2 * aw:], fcum, 1024)
    p, ps = _pool_fwd(u, mixb, pool_scale, seq)
    links.rest_forward([ps, qa, g2])
    o, lse = _attn_fwd(qa, ka, v, seq, tq, dep=links.token)
    w_pool_out, w_attn_out, w_out, w_ffn_gate, w_ffn_up, w_ffn_down = links.weights_rest(o)
    merged, x1 = _merge_fwd(x, ps, o, g2, w_pool_out, w_attn_out, w_out, 512)
    h2, gt, up, act, x2 = _ffn_fwd(x1, norm2_g, w_ffn_gate, w_ffn_up, w_ffn_down, 1024, 256)
    loss, dx2, d_gf = _final_fwd_bwd(x2, target, norm_f_g, 512)

    dgt, dup, dx1, d_g2n = _ffn_bwd(dx2, x1, norm2_g, gt, up, w_ffn_gate, w_ffn_up, w_ffn_down, 1024, 256)
    d_wd = _matmul("dw_down", act, dx2, "tn", F32, 1408, 1024, 1024)
    d_wg = _matmul("dw_gate", dgt, h2, "tn", F32, 1408, 1024, 1024)
    d_wu = _matmul("dw_up", dup, h2, "tn", F32, 1408, 1024, 1024)
    links.reduce_begin("a", [_row_sharded_to_comm(g) for g in (d_wg, d_wu, d_wd)])
    dpy, day, dg2, dps, da = _merge_bwd(dx1, ps, o, g2, w_pool_out, w_attn_out, w_out, 512, dep=links.token)
    links.advance(dps)
    d_wout = _matmul("dw_out", merged, dx1, "tn", F32, 1024, 1024, 1024)
    d_wpo = _matmul("dw_pool_out", ps, dpy, "tn", F32, 512, 1024, 1024)
    d_wao = _matmul("dw_attn_out", o, day, "tn", F32, 512, 1024, 1024)
    links.reduce_begin("m", [_col_sharded_to_comm(d_wpo), _col_sharded_to_comm(d_wao), _row_sharded_to_comm(d_wout)])
    dq, dk, dv, dfr = _attn_bwd(qa, ka, v, da, lse, seq, tq, dep=links.token)
    links.advance(dq)
    dfc = jnp.pad(dfr.reshape(N_HEADS, t).T, ((0, 0), (0, LANES - N_HEADS)))
    dfl, d_bf = _forget_bwd(dfc, fl, bf, seq)
    du, d_mix, d_scale = _pool_bwd(dps, p, mixb, links.tie(pool_scale), seq)
    d_wu_in = _matmul("dw_in_u", du, h, "tn", F32, 512, 1024, 1024)
    d_wq = _matmul("dw_in_q", dq, h, "tn", F32, 512, 1024, 1024)
    d_wk = _matmul("dw_in_k", dk, h, "tn", F32, 512, 1024, 1024)
    d_wv = _matmul("dw_in_v", dv, h, "tn", F32, 512, 1024, 1024)
    links.advance([d_wu_in, d_wq, d_wk, d_wv])
    d_wf = _matmul("dw_in_f", dfl, h, "tn", F32, LANES, 1024, 512)
    d_wg2 = _matmul("dw_in_gates", dg2, h, "tn", F32, 1024, 1024, 1024, dep=links.token)
    d_win = jnp.concatenate([d_wu_in, d_wq, d_wk, d_wv, d_wf[:N_HEADS], d_wg2], axis=0)
    comm_b = [d_win.reshape(N_CHIPS, d_win.shape[0] // N_CHIPS, d)]
    links.advance(comm_b)
    links.reduce_begin("b", comm_b, column_halves=True)
    dx, d_g1 = _in_bwd(du, dq, dk, dv, dg2, dfl, dx1, x, links.tie(norm1_g), wu, wqkv, wg2, wft, 512)
    links.advance(dx)
    small = (d_g1, d_bf[:, :N_HEADS], d_mix, d_scale, d_g2n, d_gf)
    return loss, dx, small


def kernel(x, norm1_g, w_in, b_forget, pool_mix, pool_scale, w_pool_out, w_attn_out, w_out, norm2_g, w_ffn_gate, w_ffn_up, w_ffn_down, norm_f_g, loss_target, m_norm1_g, m_w_in, m_b_forget, m_pool_mix, m_pool_scale, m_w_pool_out, m_w_attn_out, m_w_out, m_norm2_g, m_w_ffn_gate, m_w_ffn_up, m_w_ffn_down, m_norm_f_g, v_norm1_g, v_w_in, v_b_forget, v_pool_mix, v_pool_scale, v_w_pool_out, v_w_attn_out, v_w_out, v_norm2_g, v_w_ffn_gate, v_w_ffn_up, v_w_ffn_down, v_norm_f_g):
    nb, seq, d = x.shape
    group_a = ((w_ffn_gate, m_w_ffn_gate, v_w_ffn_gate, True, 9), (w_ffn_up, m_w_ffn_up, v_w_ffn_up, True, 10),
               (w_ffn_down, m_w_ffn_down, v_w_ffn_down, False, 11))
    group_m = ((w_pool_out, m_w_pool_out, v_w_pool_out, False, 5), (w_attn_out, m_w_attn_out, v_w_attn_out, False, 6),
               (w_out, m_w_out, v_w_out, False, 7))
    group_b = ((w_in, m_w_in, v_w_in, False, 1),)
    small_w = (norm1_g, b_forget, pool_mix, pool_scale, norm2_g, norm_f_g)
    small_m = (m_norm1_g, m_b_forget, m_pool_mix, m_pool_scale, m_norm2_g, m_norm_f_g)
    small_v = (v_norm1_g, v_b_forget, v_pool_mix, v_pool_scale, v_norm2_g, v_norm_f_g)
    small_pos = (0, 2, 3, 4, 8, 12)
    view = lambda a, tr: a[0].T if tr else a[0]
    unview = lambda a, tr, like: (a.T if tr else a).reshape(like.shape)

    def shard(w, tr):
        lw = view(w, tr).astype(BF16)
        return lw.reshape(2, lw.shape[0] // 2, lw.shape[1])

    cm = lambda a: jnp.transpose(a, (2, 0, 1))
    shard_in = _rows_to_bf16("w_in_to_bf16", cm(w_in))
    links = _MeshLinks([shard_in],
                       [shard(w_pool_out, False), shard(w_attn_out, False), shard(w_out, False),
                        shard(w_ffn_gate, True), shard(w_ffn_up, True), shard(w_ffn_down, False)])
    loss, dx, small_g = _local_step(
        links, x.reshape(nb * seq, d), loss_target.reshape(nb * seq, d), seq,
        norm1_g, b_forget, pool_mix[0], pool_scale, norm2_g, norm_f_g.reshape(1, d))

    grads, deltas, new_m, new_v = [None] * 13, [None] * 13, [None] * 13, [None] * 13
    pos_c = jnp.stack([lax.axis_index("c")]).astype(jnp.int32)

    def update(tag, group, dep):
        last = []
        for k, ((w, m, v, tr, pos), (mine, other)) in enumerate(zip(group, links.reduced(tag))):
            outs = _adamw_halves(f"adamw_{tag}{k}", pos_c, view(w, tr), mine, other, view(m, tr), view(v, tr), 256,
                                 dep=dep)
            grads[pos], deltas[pos], new_m[pos], new_v[pos] = (unview(a, tr, w) for a in outs)
            last.append(outs[1])
        return last

    small_state = _small_begin(_pack_small(*small_g, extra=loss), links.token)
    links.tokens["small"] = small_state["token"]
    last = update("a", group_a, links.token) + update("m", group_m, links.token)
    links.advance(last)
    del links.tokens["small"]
    small_sum = _small_end(small_state, last)
    loss_out = small_sum[8, N_HEADS]
    dl, mn, vn = _adamw("adamw_small", _pack_small(*small_w), small_sum * _small_mask(), _pack_small(*small_m),
                        _pack_small(*small_v))
    for pos, g, a, b, e in zip(small_pos, _unpack_small(small_sum, small_w), _unpack_small(dl, small_w),
                               _unpack_small(mn, small_w), _unpack_small(vn, small_w)):
        grads[pos], deltas[pos], new_m[pos], new_v[pos] = g, a, b, e
    links.advance(dl)
    links.advance(links.token)
    (mine, other), = links.reduced("b")
    outs = _adamw_rows("adamw_b0", pos_c, cm(w_in), mine, other, cm(m_w_in), cm(v_w_in))
    grads[1], deltas[1], new_m[1], new_v[1] = (jnp.transpose(a, (1, 2, 0)) for a in outs)

    return (loss_out, dx.reshape(nb, seq, d), *grads, *deltas, *new_m, *new_v)


def _small_mask():
    rows = lax.broadcasted_iota(jnp.int32, (552, LANES), 0)
    lanes = lax.broadcasted_iota(jnp.int32, (552, LANES), 1)
    return jnp.where(jnp.logical_and(rows == 8, lanes == N_HEADS), 0.0, 1.0).astype(F32)
```

```python
import functools

import jax
import jax.numpy as jnp
from jax import lax
from jax.experimental import pallas as pl
from jax.experimental.pallas import tpu as pltpu

F32 = jnp.float32
BF16 = jnp.bfloat16

D_MODEL = 1024
POOL_WINDOWS = (2, 4, 8, 16)
POOL_GROUPS = 4
POOL_GROUP_DIM = 128
POOL_WIDTH = 512
HEAD_DIM = 64
N_HEADS = 8
ATTN_WIDTH = 512
D_FF = 2816
RMS_EPS = 1e-6
ATTN_SCALE = HEAD_DIM ** -0.5
NEG_BIG = -1e30

ADAM_LR = 0.001
ADAM_B1 = 0.9
ADAM_B2 = 0.999
ADAM_EPS = 1e-08
ADAM_WD = 0.01
ADAM_STEP = 10

LANES = 128
N_CHIPS = 4
N_DEV = 8
VMEM_LIMIT_V7X = 52 * 1024 * 1024
ROW_CHUNK = 256
MESH = pl.DeviceIdType.MESH
ANY = pl.BlockSpec(memory_space=pl.ANY)


def _cparams(*sem):
    return pltpu.CompilerParams(dimension_semantics=sem if sem else None, vmem_limit_bytes=VMEM_LIMIT_V7X)


def _dep_list(dep):
    return [] if dep is None else (list(dep) if isinstance(dep, (list, tuple)) else [dep])


def _after(body, n_in, dep):
    k = len(_dep_list(dep))
    if k == 0:
        return body

    def wrapped(*refs):
        body(*refs[:n_in], *refs[n_in + k:])

    return wrapped


def _dep_args(dep):
    deps = _dep_list(dep)
    return [ANY] * len(deps), deps


def _dot(a, b):
    return lax.dot_general(a, b, (((1,), (0,)), ((), ())), preferred_element_type=F32)


def _dot_nt(a, b):
    return lax.dot_general(a, b, (((1,), (1,)), ((), ())), preferred_element_type=F32)


def _dot_tn(a, b):
    return lax.dot_general(a, b, (((0,), (0,)), ((), ())), preferred_element_type=F32)


def _sigmoid(x):
    return jax.nn.sigmoid(x)


def _rms_fwd(x, g):
    r = lax.rsqrt(jnp.mean(x * x, axis=-1, keepdims=True) + RMS_EPS)
    return (x * r) * g


def _rms_bwd(x, g, dy):
    r = lax.rsqrt(jnp.mean(x * x, axis=-1, keepdims=True) + RMS_EPS)
    xh = x * r
    dg = jnp.sum(dy * xh, axis=0, keepdims=True)
    dxh = dy * g
    dx = r * (dxh - xh * jnp.mean(dxh * xh, axis=-1, keepdims=True))
    return dx, dg


def _matmul(name, a, b, mode, out_dtype, tm, tn, tk, dep=None):
    if mode == "nn":
        (m, k), (_, n) = a.shape, b.shape
    elif mode == "nt":
        (m, k), (n, _) = a.shape, b.shape
    else:
        (k, m), (_, n) = a.shape, b.shape
    tm, tn, tk = min(tm, m), min(tn, n), min(tk, k)
    assert m % tm == 0 and n % tn == 0 and k % tk == 0, (name, m, n, k, tm, tn, tk)
    nk = k // tk
    if mode == "tn":
        a_spec = pl.BlockSpec((tk, tm), lambda i, j, kk: (kk, i))
    else:
        a_spec = pl.BlockSpec((tm, tk), lambda i, j, kk: (i, kk))
    if mode == "nt":
        b_spec = pl.BlockSpec((tn, tk), lambda i, j, kk: (j, kk))
    else:
        b_spec = pl.BlockSpec((tk, tn), lambda i, j, kk: (kk, j))
    dot = {"nn": _dot, "nt": _dot_nt, "tn": _dot_tn}[mode]
    use_scratch = nk > 1 and out_dtype != F32

    def body(a_ref, b_ref, o_ref, *scratch):
        if nk == 1 and mode != "tn":
            rows = min(ROW_CHUNK, tm)
            bb = b_ref[...].astype(BF16)
            for r0 in range(0, tm, rows):
                o_ref[r0:r0 + rows, :] = dot(a_ref[r0:r0 + rows, :].astype(BF16), bb).astype(out_dtype)
            return
        prod = dot(a_ref[...].astype(BF16), b_ref[...].astype(BF16))
        if nk == 1:
            o_ref[...] = prod.astype(out_dtype)
            return
        acc = scratch[0] if use_scratch else o_ref
        kk = pl.program_id(2)

        @pl.when(kk == 0)
        def _():
            acc[...] = prod

        @pl.when(kk > 0)
        def _():
            acc[...] += prod

        if use_scratch:
            @pl.when(kk == nk - 1)
            def _():
                o_ref[...] = acc[...].astype(out_dtype)

    dep_specs, dep_ops = _dep_args(dep)
    return pl.pallas_call(
        _after(body, 2, dep),
        name=name,
        out_shape=jax.ShapeDtypeStruct((m, n), out_dtype),
        grid=(m // tm, n // tn, nk),
        in_specs=[a_spec, b_spec] + dep_specs,
        out_specs=pl.BlockSpec((tm, tn), lambda i, j, kk: (i, j)),
        scratch_shapes=[pltpu.VMEM((tm, tn), F32)] if use_scratch else [],
        compiler_params=_cparams("parallel", "parallel", "arbitrary"),
    )(a, b, *dep_ops)


def _norm_fwd(name, x, g, tm):
    t, d = x.shape
    tm = min(tm, t)

    def body(x_ref, g_ref, h_ref):
        h_ref[...] = _rms_fwd(x_ref[...], g_ref[...]).astype(BF16)

    return pl.pallas_call(
        body, name=name, out_shape=jax.ShapeDtypeStruct((t, d), BF16), grid=(t // tm,),
        in_specs=[pl.BlockSpec((tm, d), lambda i: (i, 0)), pl.BlockSpec((1, d), lambda i: (0, 0))],
        out_specs=pl.BlockSpec((tm, d), lambda i: (i, 0)),
        compiler_params=_cparams("parallel"),
    )(x, g)


def _split3(x):
    hi = x.astype(BF16)
    r1 = x - hi.astype(F32)
    mid = r1.astype(BF16)
    lo = (r1 - mid.astype(F32)).astype(BF16)
    return hi, mid, lo


def _tri_dot(tri, x):
    hi, mid, lo = _split3(x)
    return _dot(tri, hi) + _dot(tri, mid) + _dot(tri, lo)


def _forget_fwd(h, wf, bf, seq):
    t, d = h.shape
    cb = min(256, seq)

    def body(h_ref, wf_ref, bf_ref, fl_ref, fc_ref):
        fl = _dot(h_ref[...], wf_ref[...])
        fl_ref[...] = fl
        xx = fl + bf_ref[...]
        lf = jnp.minimum(xx, 0.0) - jnp.log(1.0 + jnp.exp(-jnp.abs(xx)))
        ri = lax.broadcasted_iota(jnp.int32, (cb, cb), 0)
        ci = lax.broadcasted_iota(jnp.int32, (cb, cb), 1)
        tri = (ri >= ci).astype(BF16)
        carry = jnp.zeros((1, LANES), F32)
        for blk in range(seq // cb):
            cs = _tri_dot(tri, lf[blk * cb:(blk + 1) * cb]) + carry
            fc_ref[blk * cb:(blk + 1) * cb, :] = cs
            carry = cs[cb - 1:cb, :]

    return pl.pallas_call(
        body, name="forget_fwd",
        out_shape=(jax.ShapeDtypeStruct((t, LANES), F32), jax.ShapeDtypeStruct((t, LANES), F32)),
        grid=(t // seq,),
        in_specs=[pl.BlockSpec((seq, d), lambda b: (b, 0)), pl.BlockSpec((d, LANES), lambda b: (0, 0)),
                  pl.BlockSpec((1, LANES), lambda b: (0, 0))],
        out_specs=(pl.BlockSpec((seq, LANES), lambda b: (b, 0)), pl.BlockSpec((seq, LANES), lambda b: (b, 0))),
        compiler_params=_cparams("parallel"),
    )(h, wf, bf)


def _pool_fwd(u, mix, scale, seq):
    t = u.shape[0]

    def body(u_ref, mix_ref, sc_ref, p_ref, ps_ref):
        tpos = lax.broadcasted_iota(jnp.int32, (seq, POOL_GROUP_DIM), 0)
        for g in range(POOL_GROUPS):
            sl = slice(g * POOL_GROUP_DIM, (g + 1) * POOL_GROUP_DIM)
            ug = u_ref[:, sl]
            s = ug
            for lvl in range(g + 1):
                d = 2 ** lvl
                s = s + jnp.where(tpos >= d, pltpu.roll(s, d, 0), 0.0)
            cnt = jnp.minimum(tpos + 1, POOL_WINDOWS[g]).astype(F32)
            pb = (s / cnt - ug).astype(BF16)
            p_ref[:, sl] = pb
            ps_ref[:, sl] = (_dot(pb, mix_ref[g]) * sc_ref[:, sl]).astype(BF16)

    return pl.pallas_call(
        body, name="pool_fwd",
        out_shape=(jax.ShapeDtypeStruct((t, POOL_WIDTH), BF16), jax.ShapeDtypeStruct((t, POOL_WIDTH), BF16)),
        grid=(t // seq,),
        in_specs=[pl.BlockSpec((seq, POOL_WIDTH), lambda b: (b, 0)),
                  pl.BlockSpec((POOL_GROUPS, POOL_GROUP_DIM, POOL_GROUP_DIM), lambda b: (0, 0, 0)),
                  pl.BlockSpec((1, POOL_WIDTH), lambda b: (0, 0))],
        out_specs=(pl.BlockSpec((seq, POOL_WIDTH), lambda b: (b, 0)), pl.BlockSpec((seq, POOL_WIDTH), lambda b: (b, 0))),
        compiler_params=_cparams("parallel"),
    )(u, mix, scale)


def _aug_constants():
    w = N_HEADS * LANES
    rows = jnp.arange(3 * LANES)
    piece, head = rows // LANES, rows % LANES
    cols = jnp.arange(w)
    live = (head < N_HEADS)[:, None]
    pq = (live & (cols[None, :] == (head * LANES + HEAD_DIM + piece)[:, None])).astype(BF16)
    pk = -(live & (cols[None, :] == (head * LANES + HEAD_DIM + 3 + piece)[:, None])).astype(BF16)
    lane = cols % LANES
    oq = ((lane >= HEAD_DIM + 3) & (lane < HEAD_DIM + 6)).astype(F32)[None, :]
    ok = ((lane >= HEAD_DIM) & (lane < HEAD_DIM + 3)).astype(F32)[None, :]
    return pq, pk, oq, ok


def _head_blocks(wt):
    d = wt.shape[1]
    return jnp.pad(wt.reshape(N_HEADS, HEAD_DIM, d), ((0, 0), (0, LANES - HEAD_DIM), (0, 0))).reshape(N_HEADS * LANES, d)


def _attn_prep(h, wq, wk, wv, fcum, tm):
    t, d = h.shape
    tm = min(tm, t)
    rows = min(ROW_CHUNK, tm)
    w = N_HEADS * LANES
    pq, pk, oq, ok = _aug_constants()

    def body(h_ref, wq_ref, wk_ref, wv_ref, f_ref, pq_ref, pk_ref, oq_ref, ok_ref, qa_ref, ka_ref, v_ref):
        for r0 in range(0, tm, rows):
            rs = slice(r0, r0 + rows)
            hh = h_ref[rs, :]
            fs = jnp.concatenate(_split3(f_ref[rs, :]), axis=1)
            q = _dot_nt(hh, wq_ref[...]).astype(BF16).astype(F32) * ATTN_SCALE
            qa_ref[rs, :] = (q + _dot(fs, pq_ref[...]) + oq_ref[...]).astype(BF16)
            k = _dot_nt(hh, wk_ref[...]).astype(BF16).astype(F32)
            ka_ref[rs, :] = (k + _dot(fs, pk_ref[...]) + ok_ref[...]).astype(BF16)
            v_ref[rs, :] = _dot_nt(hh, wv_ref[...]).astype(BF16)

    row = lambda n: pl.BlockSpec((tm, n), lambda i: (i, 0))
    full = lambda a: pl.BlockSpec(a.shape, lambda i: (0, 0))
    return pl.pallas_call(
        body, name="attn_prep",
        out_shape=(jax.ShapeDtypeStruct((t, w), BF16), jax.ShapeDtypeStruct((t, w), BF16),
                   jax.ShapeDtypeStruct((t, ATTN_WIDTH), BF16)),
        grid=(t // tm,),
        in_specs=[row(d), full(wq), full(wk), full(wv), row(LANES), full(pq), full(pk), full(oq), full(ok)],
        out_specs=(row(w), row(w), row(ATTN_WIDTH)),
        compiler_params=_cparams("parallel"),
    )(h, wq, wk, wv, fcum, pq, pk, oq, ok)


def _fold_lanes(x, op):
    out = x[:, :LANES]
    for g in range(1, x.shape[1] // LANES):
        out = op(out, x[:, g * LANES:(g + 1) * LANES])
    return out


def _causal_sweep(i, tile, carry):
    def pair(jj, c):
        return tile(2 * jj + 1, tile(2 * jj, c, False), False)

    carry = lax.fori_loop(0, i // 2, pair, carry)
    return lax.cond(i % 2 == 1, lambda c: tile(i, tile(i - 1, c, False), True), lambda c: tile(i, c, True), carry)


def _attn_fwd(qa, ka, v, seq, tq, dep=None):
    t = qa.shape[0]
    nq = seq // tq
    hp_n = N_HEADS // 2
    heads = [slice(e * LANES, (e + 1) * LANES) for e in range(2)]

    def body(q_ref, k_ref, v_ref, o_ref, lse_ref, s_buf):
        i = pl.program_id(2)
        diag_ok = lax.broadcasted_iota(jnp.int32, (tq, tq), 0) >= lax.broadcasted_iota(jnp.int32, (tq, tq), 1)
        qs = [q_ref[:, hl] for hl in heads]

        def sweep1(j, mxs, diagonal):
            r0 = pl.multiple_of(j * tq, tq)
            out = []
            for e, hl in enumerate(heads):
                s = _dot_nt(qs[e], k_ref[pl.ds(r0, tq), hl])
                if diagonal:
                    s = jnp.where(diag_ok, s, NEG_BIG)
                s_buf[e, j] = s
                out.append(jnp.maximum(mxs[e], _fold_lanes(s, jnp.maximum)))
            return tuple(out)

        mxs = _causal_sweep(i, sweep1, (jnp.full((tq, LANES), NEG_BIG, F32),) * 2)
        ms = [jnp.max(mx, axis=1, keepdims=True) for mx in mxs]

        def sweep2(j, carry, diagonal):
            r0 = pl.multiple_of(j * tq, tq)
            vv = v_ref[pl.ds(r0, tq), :]
            out = []
            for e in range(2):
                p = jnp.exp(s_buf[e, j] - ms[e])
                out += [carry[2 * e] + _fold_lanes(p, jnp.add), carry[2 * e + 1] + _dot(p.astype(BF16), vv)]
            return tuple(out)

        res = _causal_sweep(i, sweep2, (jnp.zeros((tq, LANES), F32),) * 4)
        outs = []
        for e in range(2):
            l = jnp.sum(res[2 * e], axis=1, keepdims=True)
            outs.append(res[2 * e + 1] / l)
            lse_ref[:, e:e + 1] = ms[e] + jnp.log(l)
        lane = lax.broadcasted_iota(jnp.int32, (tq, LANES), 1)
        o_ref[...] = jnp.where(lane < HEAD_DIM, outs[0], outs[1])

    dep_specs, dep_ops = _dep_args(dep)
    return pl.pallas_call(
        _after(body, 3, dep), name="attn_fwd",
        out_shape=(jax.ShapeDtypeStruct((t, ATTN_WIDTH), F32), jax.ShapeDtypeStruct((hp_n, t, 2), F32)),
        grid=(t // seq, hp_n, nq),
        in_specs=[pl.BlockSpec((tq, 2 * LANES), lambda b, hp, i: (b * nq + i, hp)),
                  pl.BlockSpec((seq, 2 * LANES), lambda b, hp, i: (b, hp)),
                  pl.BlockSpec((seq, LANES), lambda b, hp, i: (b, hp))] + dep_specs,
        out_specs=(pl.BlockSpec((tq, LANES), lambda b, hp, i: (b * nq + i, hp)),
                   pl.BlockSpec((None, tq, 2), lambda b, hp, i: (hp, b * nq + i, 0))),
        scratch_shapes=[pltpu.VMEM((2, nq, tq, tq), F32)],
        compiler_params=_cparams("parallel", "parallel", "arbitrary"),
    )(qa, ka, v, *dep_ops)


def _merge_fwd(x, ps, o, g2, wpo, wao, wout, tm):
    t, d = x.shape
    tm = min(tm, t)
    rows = min(ROW_CHUNK, tm)

    def body(x_ref, ps_ref, o_ref, gp_ref, ga_ref, wpo_ref, wao_ref, wout_ref, mg_ref, x1_ref):
        for r0 in range(0, tm, rows):
            rs = slice(r0, r0 + rows)
            py = _dot(ps_ref[rs, :], wpo_ref[...])
            ay = _dot(o_ref[rs, :].astype(BF16), wao_ref[...])
            mb = (_sigmoid(gp_ref[rs, :].astype(F32)) * py + _sigmoid(ga_ref[rs, :].astype(F32)) * ay).astype(BF16)
            mg_ref[rs, :] = mb
            x1_ref[rs, :] = x_ref[rs, :] + _dot(mb, wout_ref[...])

    row = lambda w: pl.BlockSpec((tm, w), lambda i: (i, 0))
    full = lambda a: pl.BlockSpec(a.shape, lambda i: (0, 0))
    return pl.pallas_call(
        body, name="merge_fwd",
        out_shape=(jax.ShapeDtypeStruct((t, d), BF16), jax.ShapeDtypeStruct((t, d), F32)),
        grid=(t // tm,),
        in_specs=[row(d), row(POOL_WIDTH), row(ATTN_WIDTH), pl.BlockSpec((tm, d), lambda i: (i, 0)),
                  pl.BlockSpec((tm, d), lambda i: (i, 1)), full(wpo), full(wao), full(wout)],
        out_specs=(row(d), row(d)),
        compiler_params=_cparams("parallel"),
    )(x, ps, o, g2, g2, wpo, wao, wout)


def _ffn_fwd(x1, g, wg, wu, wd, tm, tf):
    t, d = x1.shape
    f = wg.shape[0]
    tm = min(tm, t)
    nf = f // tf
    rows = min(512, tm)

    def body(x1_ref, g_ref, wg_ref, wu_ref, wd_ref, h2_ref, gt_ref, up_ref, act_ref, x2_ref):
        j = pl.program_id(1)

        @pl.when(j == 0)
        def _():
            h2_ref[...] = _rms_fwd(x1_ref[...], g_ref[...]).astype(BF16)

            x2_ref[...] = x1_ref[...]

        for r0 in range(0, tm, rows):
            rs = slice(r0, r0 + rows)
            h2 = h2_ref[rs, :]
            gt = _dot_nt(h2, wg_ref[...])
            up = _dot_nt(h2, wu_ref[...])
            sg = _sigmoid(gt)
            silu = gt * sg
            act = (silu * up).astype(BF16)
            gt_ref[rs, :] = (up * (sg * (1.0 + gt * (1.0 - sg)))).astype(BF16)
            up_ref[rs, :] = silu.astype(BF16)
            act_ref[rs, :] = act
            x2_ref[rs, :] += _dot(act, wd_ref[...])

    return pl.pallas_call(
        body, name="ffn_fwd",
        out_shape=(jax.ShapeDtypeStruct((t, d), BF16), jax.ShapeDtypeStruct((t, f), BF16),
                   jax.ShapeDtypeStruct((t, f), BF16), jax.ShapeDtypeStruct((t, f), BF16),
                   jax.ShapeDtypeStruct((t, d), F32)),
        grid=(t // tm, nf),
        in_specs=[pl.BlockSpec((tm, d), lambda i, j: (i, 0)), pl.BlockSpec((1, d), lambda i, j: (0, 0)),
                  pl.BlockSpec((tf, d), lambda i, j: (j, 0)), pl.BlockSpec((tf, d), lambda i, j: (j, 0)),
                  pl.BlockSpec((tf, d), lambda i, j: (j, 0))],
        out_specs=(pl.BlockSpec((tm, d), lambda i, j: (i, 0)), pl.BlockSpec((tm, tf), lambda i, j: (i, j)),
                   pl.BlockSpec((tm, tf), lambda i, j: (i, j)), pl.BlockSpec((tm, tf), lambda i, j: (i, j)),
                   pl.BlockSpec((tm, d), lambda i, j: (i, 0))),
        compiler_params=_cparams("parallel", "arbitrary"),
    )(x1, g, wg, wu, wd)


def _final_fwd_bwd(x2, target, g, tm):
    t, d = x2.shape
    tm = min(tm, t)

    def body(x_ref, t_ref, g_ref, loss_ref, dx_ref, dg_ref):
        i = pl.program_id(0)
        x = x_ref[...]
        gg = g_ref[...]
        err = _rms_fwd(x, gg) - t_ref[...]
        part = 0.5 * jnp.sum(jnp.mean(err * err, axis=-1, keepdims=True), axis=0, keepdims=True)
        dx, dg = _rms_bwd(x, gg, err * (1.0 / d))
        dx_ref[...] = dx

        @pl.when(i == 0)
        def _():
            loss_ref[...] = jnp.zeros_like(loss_ref)
            dg_ref[...] = jnp.zeros_like(dg_ref)

        loss_ref[...] += jnp.broadcast_to(part, loss_ref.shape)
        dg_ref[...] += dg

    return pl.pallas_call(
        body, name="final_fwd_bwd",
        out_shape=(jax.ShapeDtypeStruct((1, LANES), F32), jax.ShapeDtypeStruct((t, d), F32),
                   jax.ShapeDtypeStruct((1, d), F32)),
        grid=(t // tm,),
        in_specs=[pl.BlockSpec((tm, d), lambda i: (i, 0)), pl.BlockSpec((tm, d), lambda i: (i, 0)),
                  pl.BlockSpec((1, d), lambda i: (0, 0))],
        out_specs=(pl.BlockSpec((1, LANES), lambda i: (0, 0)), pl.BlockSpec((tm, d), lambda i: (i, 0)),
                   pl.BlockSpec((1, d), lambda i: (0, 0))),
        compiler_params=_cparams("arbitrary"),
    )(x2, target, g)


def _ffn_bwd(dx2, x1, g, gt, up, wg, wu, wd, tm, tf):
    t, d = dx2.shape
    f = gt.shape[1]
    tm = min(tm, t)
    nf = f // tf
    wgu = jnp.concatenate([wg.reshape(nf, tf, d), wu.reshape(nf, tf, d)], axis=1).reshape(2 * f, d)
    rows = min(256, tm)

    def body(dx2_ref, x1_ref, g_ref, gt_ref, up_ref, wgu_ref, wd_ref, dgt_ref, dup_ref, dx1_ref, dg_ref, acc_ref,
             dxb_ref):
        i, j = pl.program_id(0), pl.program_id(1)

        @pl.when(j == 0)
        def _():
            dxb_ref[...] = dx2_ref[...].astype(BF16)
            acc_ref[...] = jnp.zeros_like(acc_ref)

        for r0 in range(0, tm, rows):
            rs = slice(r0, r0 + rows)
            dact = _dot_nt(dxb_ref[rs, :], wd_ref[...])
            dgt = (dact * gt_ref[rs, :].astype(F32)).astype(BF16)
            dup = (dact * up_ref[rs, :].astype(F32)).astype(BF16)
            dgt_ref[rs, :] = dgt
            dup_ref[rs, :] = dup
            acc_ref[rs, :] += _dot(jnp.concatenate([dgt, dup], axis=1), wgu_ref[...])

        @pl.when(jnp.logical_and(i == 0, j == 0))
        def _():
            dg_ref[...] = jnp.zeros_like(dg_ref)

        @pl.when(j == nf - 1)
        def _():
            dxn, dg = _rms_bwd(x1_ref[...], g_ref[...], acc_ref[...])
            dx1_ref[...] = dx2_ref[...] + dxn
            dg_ref[...] += dg

    return pl.pallas_call(
        body, name="ffn_bwd",
        out_shape=(jax.ShapeDtypeStruct((t, f), BF16), jax.ShapeDtypeStruct((t, f), BF16),
                   jax.ShapeDtypeStruct((t, d), F32), jax.ShapeDtypeStruct((1, d), F32)),
        grid=(t // tm, nf),
        in_specs=[pl.BlockSpec((tm, d), lambda i, j: (i, 0)), pl.BlockSpec((tm, d), lambda i, j: (i, 0)),
                  pl.BlockSpec((1, d), lambda i, j: (0, 0)),
                  pl.BlockSpec((tm, tf), lambda i, j: (i, j)), pl.BlockSpec((tm, tf), lambda i, j: (i, j)),
                  pl.BlockSpec((2 * tf, d), lambda i, j: (j, 0)), pl.BlockSpec((tf, d), lambda i, j: (j, 0))],
        out_specs=(pl.BlockSpec((tm, tf), lambda i, j: (i, j)), pl.BlockSpec((tm, tf), lambda i, j: (i, j)),
                   pl.BlockSpec((tm, d), lambda i, j: (i, 0)), pl.BlockSpec((1, d), lambda i, j: (0, 0))),
        scratch_shapes=[pltpu.VMEM((tm, d), F32), pltpu.VMEM((tm, d), BF16)],
        compiler_params=_cparams("arbitrary", "arbitrary"),
    )(dx2, x1, g, gt, up, wgu, wd)


def _merge_bwd(dx1, ps, o, g2, wpo, wao, wout, tm, dep=None):
    t, d = dx1.shape
    tm = min(tm, t)
    rows = min(ROW_CHUNK, tm)

    def body(dx1_ref, ps_ref, o_ref, gp_ref, ga_ref, wpo_ref, wao_ref, wout_ref, dpy_ref, day_ref, dg2_ref, dps_ref, da_ref):
        for r0 in range(0, tm, rows):
            rs = slice(r0, r0 + rows)
            dm = _dot_nt(dx1_ref[rs, :].astype(BF16), wout_ref[...])
            py = _dot(ps_ref[rs, :], wpo_ref[...])
            ay = _dot(o_ref[rs, :].astype(BF16), wao_ref[...])
            sp = _sigmoid(gp_ref[rs, :].astype(F32))
            sa = _sigmoid(ga_ref[rs, :].astype(F32))
            dpy = (dm * sp).astype(BF16)
            day = (dm * sa).astype(BF16)
            dpy_ref[rs, :] = dpy
            day_ref[rs, :] = day
            dg2_ref[rs, :d] = (dm * py * (sp * (1.0 - sp))).astype(BF16)
            dg2_ref[rs, d:] = (dm * ay * (sa * (1.0 - sa))).astype(BF16)
            dps_ref[rs, :] = _dot_nt(dpy, wpo_ref[...])
            da_ref[rs, :] = _dot_nt(day, wao_ref[...]).astype(BF16)

    row = lambda w: pl.BlockSpec((tm, w), lambda i: (i, 0))
    full = lambda a: pl.BlockSpec(a.shape, lambda i: (0, 0))
    dep_specs, dep_ops = _dep_args(dep)
    return pl.pallas_call(
        _after(body, 8, dep), name="merge_bwd",
        out_shape=(jax.ShapeDtypeStruct((t, d), BF16), jax.ShapeDtypeStruct((t, d), BF16),
                   jax.ShapeDtypeStruct((t, 2 * d), BF16), jax.ShapeDtypeStruct((t, POOL_WIDTH), F32),
                   jax.ShapeDtypeStruct((t, ATTN_WIDTH), BF16)),
        grid=(t // tm,),
        in_specs=[row(d), row(POOL_WIDTH), row(ATTN_WIDTH), pl.BlockSpec((tm, d), lambda i: (i, 0)),
                  pl.BlockSpec((tm, d), lambda i: (i, 1)), full(wpo), full(wao), full(wout)] + dep_specs,
        out_specs=(row(d), row(d), row(2 * d), row(POOL_WIDTH), row(ATTN_WIDTH)),
        compiler_params=_cparams("parallel"),
    )(dx1, ps, o, g2, g2, wpo, wao, wout, *dep_ops)


def _attn_bwd(qa, ka, v, do, lse4, seq, tq, dep=None):
    t = qa.shape[0]
    nq = seq // tq
    hp_n = N_HEADS // 2
    heads = [slice(e * LANES, (e + 1) * LANES) for e in range(2)]

    def body(q_ref, k_ref, v_ref, do_ref, lse_ref, dq_ref, dk_ref, dv_ref, dfr_ref, dk_acc, dv_acc, p_buf, dp_buf):
        diag_ok = lax.broadcasted_iota(jnp.int32, (tq, tq), 0) >= lax.broadcasted_iota(jnp.int32, (tq, tq), 1)
        lane_q = lax.broadcasted_iota(jnp.int32, (tq, LANES), 1)
        lane_s = lax.broadcasted_iota(jnp.int32, (seq, LANES), 1)
        mine_q = [lane_q < HEAD_DIM, lane_q >= HEAD_DIM]
        dv_acc[...] = jnp.zeros_like(dv_acc)
        dk_acc[...] = jnp.zeros_like(dk_acc)
        dfr_ref[...] = jnp.zeros_like(dfr_ref)

        def q_step(i, _):
            q0 = pl.multiple_of(i * tq, tq)
            qs = [q_ref[pl.ds(q0, tq), hl] for hl in heads]
            dov = do_ref[pl.ds(q0, tq), :]
            dos = [jnp.where(mq, dov, jnp.zeros((), BF16)) for mq in mine_q]
            lss = [lse_ref[pl.ds(q0, tq), e:e + 1] for e in range(2)]

            def sweep1(j, dls, diagonal):
                r0 = pl.multiple_of(j * tq, tq)
                vv = v_ref[pl.ds(r0, tq), :]
                out = []
                for e, hl in enumerate(heads):
                    s = _dot_nt(qs[e], k_ref[pl.ds(r0, tq), hl])
                    if diagonal:
                        s = jnp.where(diag_ok, s, NEG_BIG)
                    p = jnp.exp(s - lss[e])
                    dp = _dot_nt(dos[e], vv)
                    p_buf[e, j] = p
                    dp_buf[e, j] = dp
                    dv_acc[pl.ds(r0, tq), :] += _dot_tn(p.astype(BF16), dos[e])
                    out.append(dls[e] + _fold_lanes(p * dp, jnp.add))
                return tuple(out)

            dls = _causal_sweep(i, sweep1, (jnp.zeros((tq, LANES), F32),) * 2)
            dls = [jnp.sum(d, axis=1, keepdims=True) for d in dls]

            def sweep2(j, dqs, diagonal):
                r0 = pl.multiple_of(j * tq, tq)
                out = []
                for e, hl in enumerate(heads):
                    ds = p_buf[e, j] * (dp_buf[e, j] - dls[e])
                    dfr_ref[e, pl.ds(j, 1), :] += jnp.sum(ds, axis=0, keepdims=True)
                    dsb = ds.astype(BF16)
                    dk_acc[e, pl.ds(r0, tq), :] += _dot_tn(dsb, qs[e])
                    out.append(dqs[e] + _dot(dsb, k_ref[pl.ds(r0, tq), hl]))
                return tuple(out)

            dqs = _causal_sweep(i, sweep2, (jnp.zeros((tq, LANES), F32),) * 2)
            dq = jnp.where(mine_q[0], dqs[0], pltpu.roll(dqs[1], HEAD_DIM, 1)) * ATTN_SCALE
            dq_ref[pl.ds(q0, tq), :] = dq.astype(BF16)
            return 0

        lax.fori_loop(0, nq, q_step, 0)
        dk_ref[...] = jnp.where(lane_s < HEAD_DIM, dk_acc[0], pltpu.roll(dk_acc[1], HEAD_DIM, 1)).astype(BF16)
        dv_ref[...] = dv_acc[...].astype(BF16)

    wide = pl.BlockSpec((seq, 2 * LANES), lambda b, hp: (b, hp))
    col = pl.BlockSpec((seq, LANES), lambda b, hp: (b, hp))
    pair = pl.BlockSpec((None, seq, 2), lambda b, hp: (hp, b, 0))
    dep_specs, dep_ops = _dep_args(dep)
    return pl.pallas_call(
        _after(body, 5, dep), name="attn_bwd",
        out_shape=(jax.ShapeDtypeStruct((t, ATTN_WIDTH), BF16),) * 3 + (jax.ShapeDtypeStruct((N_HEADS, t // tq, tq), F32),),
        grid=(t // seq, hp_n),
        in_specs=[wide, wide, col, col, pair] + dep_specs,
        out_specs=(col, col, col, pl.BlockSpec((2, nq, tq), lambda b, hp: (hp, b, 0))),
        scratch_shapes=[pltpu.VMEM((2, seq, LANES), F32), pltpu.VMEM((seq, LANES), F32),
                        pltpu.VMEM((2, nq, tq, tq), F32), pltpu.VMEM((2, nq, tq, tq), F32)],
        compiler_params=_cparams("parallel", "arbitrary"),
    )(qa, ka, v, do, lse4, *dep_ops)


def _forget_bwd(dfc, fl, bf, seq):
    t = fl.shape[0]
    cb = min(256, seq)
    nb = seq // cb

    def body(dfc_ref, fl_ref, bf_ref, dfl_ref, db_ref):
        b = pl.program_id(0)
        ri = lax.broadcasted_iota(jnp.int32, (cb, cb), 0)
        ci = lax.broadcasted_iota(jnp.int32, (cb, cb), 1)
        tri = (ci >= ri).astype(BF16)
        carry = jnp.zeros((1, LANES), F32)
        dbs = jnp.zeros((1, LANES), F32)
        for blk in reversed(range(nb)):
            rs = slice(blk * cb, (blk + 1) * cb)
            dlf = _tri_dot(tri, -dfc_ref[rs, :]) + carry
            carry = dlf[0:1, :]
            dfl = dlf * _sigmoid(-(fl_ref[rs, :] + bf_ref[...]))
            dfl_ref[rs, :] = dfl.astype(BF16)
            dbs = dbs + jnp.sum(dfl, axis=0, keepdims=True)

        @pl.when(b == 0)
        def _():
            db_ref[...] = jnp.zeros_like(db_ref)

        db_ref[...] += dbs

    return pl.pallas_call(
        body, name="forget_bwd",
        out_shape=(jax.ShapeDtypeStruct((t, LANES), BF16), jax.ShapeDtypeStruct((1, LANES), F32)),
        grid=(t // seq,),
        in_specs=[pl.BlockSpec((seq, LANES), lambda b: (b, 0)), pl.BlockSpec((seq, LANES), lambda b: (b, 0)),
                  pl.BlockSpec((1, LANES), lambda b: (0, 0))],
        out_specs=(pl.BlockSpec((seq, LANES), lambda b: (b, 0)), pl.BlockSpec((1, LANES), lambda b: (0, 0))),
        compiler_params=_cparams("arbitrary"),
    )(dfc, fl, bf)


def _pool_bwd(dps, p, mix, scale, seq):
    t = dps.shape[0]

    def body(dps_ref, p_ref, mix_ref, sc_ref, du_ref, dmix_ref, dsc_ref):
        b = pl.program_id(0)

        @pl.when(b == 0)
        def _():
            dmix_ref[...] = jnp.zeros_like(dmix_ref)
            dsc_ref[...] = jnp.zeros_like(dsc_ref)

        tpos = lax.broadcasted_iota(jnp.int32, (seq, POOL_GROUP_DIM), 0)
        for g in range(POOL_GROUPS):
            sl = slice(g * POOL_GROUP_DIM, (g + 1) * POOL_GROUP_DIM)
            pb = p_ref[:, sl]
            dpsg = dps_ref[:, sl]
            pm = _dot(pb, mix_ref[g])
            dsc_ref[:, sl] += jnp.sum(dpsg * pm, axis=0, keepdims=True)
            dpm = (dpsg * sc_ref[:, sl]).astype(BF16)
            dmix_ref[g] += _dot_tn(pb, dpm)
            dp = _dot_nt(dpm, mix_ref[g])
            cnt = jnp.minimum(tpos + 1, POOL_WINDOWS[g]).astype(F32)
            s = dp / cnt
            for lvl in range(g + 1):
                d = 2 ** lvl
                s = s + jnp.where(tpos < seq - d, pltpu.roll(s, seq - d, 0), 0.0)
            du_ref[:, sl] = (s - dp).astype(BF16)

    return pl.pallas_call(
        body, name="pool_bwd",
        out_shape=(jax.ShapeDtypeStruct((t, POOL_WIDTH), BF16),
                   jax.ShapeDtypeStruct((POOL_GROUPS, POOL_GROUP_DIM, POOL_GROUP_DIM), F32),
                   jax.ShapeDtypeStruct((1, POOL_WIDTH), F32)),
        grid=(t // seq,),
        in_specs=[pl.BlockSpec((seq, POOL_WIDTH), lambda b: (b, 0)), pl.BlockSpec((seq, POOL_WIDTH), lambda b: (b, 0)),
                  pl.BlockSpec((POOL_GROUPS, POOL_GROUP_DIM, POOL_GROUP_DIM), lambda b: (0, 0, 0)),
                  pl.BlockSpec((1, POOL_WIDTH), lambda b: (0, 0))],
        out_specs=(pl.BlockSpec((seq, POOL_WIDTH), lambda b: (b, 0)),
                   pl.BlockSpec((POOL_GROUPS, POOL_GROUP_DIM, POOL_GROUP_DIM), lambda b: (0, 0, 0)),
                   pl.BlockSpec((1, POOL_WIDTH), lambda b: (0, 0))),
        compiler_params=_cparams("arbitrary"),
    )(dps, p, mix, scale)


def _in_bwd(du, dq, dk, dv, dg2, dfl, dx1, x, g, wu, wqkv, wg2, wft, tm):
    t, d = x.shape
    tm = min(tm, t)
    rows = min(ROW_CHUNK, tm)
    aw = ATTN_WIDTH

    def body(du_ref, dq_ref, dk_ref, dv_ref, dg2_ref, dfl_ref, dx1_ref, x_ref, g_ref, wu_ref, wqkv_ref, wg2_ref, wft_ref,
             dx_ref, dg_ref):
        i = pl.program_id(0)

        @pl.when(i == 0)
        def _():
            dg_ref[...] = jnp.zeros_like(dg_ref)

        for r0 in range(0, tm, rows):
            rs = slice(r0, r0 + rows)
            dh = _dot(du_ref[rs, :], wu_ref[...])
            dh += _dot(dq_ref[rs, :], wqkv_ref[0:aw, :])
            dh += _dot(dk_ref[rs, :], wqkv_ref[aw:2 * aw, :])
            dh += _dot(dv_ref[rs, :], wqkv_ref[2 * aw:3 * aw, :])
            dh += _dot(dg2_ref[rs, :], wg2_ref[...])
            dh += _dot(dfl_ref[rs, :], wft_ref[...])
            dxn, dg = _rms_bwd(x_ref[rs, :], g_ref[...], dh)
            dx_ref[rs, :] = dx1_ref[rs, :] + dxn
            dg_ref[...] += dg

    row = lambda w: pl.BlockSpec((tm, w), lambda i: (i, 0))
    full = lambda a: pl.BlockSpec(a.shape, lambda i: (0, 0))
    return pl.pallas_call(
        body, name="in_bwd",
        out_shape=(jax.ShapeDtypeStruct((t, d), F32), jax.ShapeDtypeStruct((1, d), F32)),
        grid=(t // tm,),
        in_specs=[row(POOL_WIDTH), row(aw), row(aw), row(aw), row(2 * d), row(LANES), row(d), row(d),
                  pl.BlockSpec((1, d), lambda i: (0, 0)), full(wu), full(wqkv), full(wg2), full(wft)],
        out_specs=(row(d), pl.BlockSpec((1, d), lambda i: (0, 0))),
        compiler_params=_cparams("arbitrary"),
    )(du, dq, dk, dv, dg2, dfl, dx1, x, g, wu, wqkv, wg2, wft)


def _position():
    return lax.axis_index("x"), lax.axis_index("y"), lax.axis_index("c")


def _remote(src, dst, send_sem, recv_sem, device):
    return pltpu.make_async_remote_copy(src_ref=src, dst_ref=dst, send_sem=send_sem, recv_sem=recv_sem,
                                        device_id=device, device_id_type=MESH)


HBM = pl.BlockSpec(memory_space=pltpu.HBM)
SEM = pl.BlockSpec(memory_space=pltpu.SEMAPHORE)
DATAFLOW = pltpu.SideEffectType.DATAFLOW_SIDE_EFFECTING


def _copies_start(name, arrays, plan, m, dep=None):
    n = len(arrays)
    arrays = [pltpu.with_memory_space_constraint(a, pltpu.HBM) for a in arrays]

    def body(*refs):
        ins, send_sem, recv_sem, token = refs[:n], refs[n], refs[n + 1], refs[2 * n + 2]
        for i, (src, dst, device, _) in enumerate(plan(ins, *_position())):
            _remote(src, dst, send_sem.at[i], recv_sem.at[i], device).start()
        token[...] = jnp.zeros_like(token)

    dep_specs, dep_ops = _dep_args(dep)
    outs = pl.pallas_call(
        _after(body, n, dep), name=name,
        out_shape=(pltpu.SemaphoreType.DMA((m,)), pltpu.SemaphoreType.DMA((m,)),
                   *[pltpu.HBM(a.shape, a.dtype) for a in arrays], jax.ShapeDtypeStruct((8, LANES), F32)),
        in_specs=[HBM] * n + dep_specs, out_specs=(SEM, SEM, *[HBM] * n, pl.BlockSpec(memory_space=pltpu.VMEM)),
        input_output_aliases={i: i + 2 for i in range(n)},
        compiler_params=pltpu.CompilerParams(has_side_effects=DATAFLOW),
    )(*arrays, *dep_ops)
    return (outs[0], outs[1]), list(outs[2:2 + n]), outs[2 + n]


def _copies_wait(name, sems, arrays, plan, after):
    n = len(arrays)
    afters = list(after) if isinstance(after, (list, tuple)) else [after]

    def body(*refs):
        ins, send_sem, recv_sem = refs[:n], refs[n], refs[n + 1]
        for i, (src, dst, device, landing) in enumerate(plan(ins, *_position())):
            _remote(src, dst, send_sem.at[i], recv_sem.at[i], device).wait_send()
            _remote(landing, landing, send_sem.at[i], recv_sem.at[i], device).wait_recv()

    outs = pl.pallas_call(
        body, name=name,
        out_shape=tuple(pltpu.HBM(a.shape, a.dtype) for a in arrays),
        in_specs=[HBM] * n + [SEM, SEM] + [ANY] * len(afters), out_specs=tuple([HBM] * n),
        input_output_aliases={i: i for i in range(n)},
        compiler_params=pltpu.CompilerParams(has_side_effects=DATAFLOW),
    )(*arrays, sems[0], sems[1], *afters)
    return list(outs)


def _tie(x, dep):
    for token in _dep_list(dep):
        x = x + token[0, 0]
    return x


def _other_chips(x, y):
    return [(1 - x, y), (x, 1 - y), (1 - x, 1 - y)]


def _gather_begin(tag, shards, token, column_halves=False):
    n = len(shards)
    lands = [lax.empty((N_CHIPS,) + s.shape, s.dtype) for s in shards]
    if column_halves:
        cols = lambda ref, h: pl.ds(pl.multiple_of(h * (ref.shape[-1] // 2), LANES), ref.shape[-1] // 2)
        mine = lambda ref, h: ref.at[:, cols(ref, h)]
        landed = lambda ref, chip, h: ref.at[chip, :, cols(ref, h)]
    else:
        mine = lambda ref, h: ref.at[h]
        landed = lambda ref, chip, h: ref.at[chip, h]

    def plan(refs, x, y, c):
        return [(mine(refs[k], c), landed(refs[n + k], 2 * x + y, c), (ox, oy, c), landed(refs[n + k], 2 * ox + oy, c))
                for k in range(n) for ox, oy in _other_chips(x, y)]

    sems, thru, token = _copies_start(f"gather_{tag}_ici_start", list(shards) + lands, plan, 3 * n, dep=token)
    return dict(tag=tag, n=n, plan=plan, sems=sems, arrays=thru, token=token, landed=landed)


def _gather_forward(st, after):
    n, tag, landed = st["n"], st["tag"], st["landed"]
    thru = _copies_wait(f"gather_{tag}_ici_wait", st["sems"], st["arrays"], st["plan"], after)

    def plan(refs, x, y, c):
        return [(landed(refs[k], 2 * ox + oy, c), landed(refs[k], 2 * ox + oy, c), (x, y, 1 - c),
                 landed(refs[k], 2 * ox + oy, 1 - c))
                for k in range(n) for ox, oy in _other_chips(x, y)]

    sems, lands, token = _copies_start(f"gather_{tag}_fwd_start", thru[n:], plan, 3 * n)
    return dict(tag=tag, n=n, plan=plan, sems=sems, arrays=lands, token=token, shards=thru[:n])


def _gather_end(st, after, merge=True):
    lands = _copies_wait(f"gather_{st['tag']}_fwd_wait", st["sems"], st["arrays"], st["plan"], after)
    if not merge:
        return lands, st["shards"]
    me = 2 * lax.axis_index("x") + lax.axis_index("y")
    return [lax.dynamic_update_index_in_dim(g, s, me, 0) for g, s in zip(lands, st["shards"])]


def _add_keep_give(name, pos, a, a_keep, a_give, b, b_keep, b_give, steps):
    r, c = b.shape[-2:]

    def spec(arr, fn):
        lead = arr.ndim - 2

        def index(i, p):
            idx = tuple(fn(i, p))
            return idx if len(idx) == arr.ndim else idx + (0, 0)

        return pl.BlockSpec((None,) * lead + (r, c), index)

    out_spec = pl.BlockSpec((None, r, c), lambda i, p: (i, 0, 0))

    def body(p_ref, ak_ref, bk_ref, ag_ref, bg_ref, keep_ref, give_ref):
        keep_ref[...] = ak_ref[...] + bk_ref[...].astype(F32)
        give_ref[...] = (ag_ref[...] + bg_ref[...].astype(F32)).astype(BF16)

    return pl.pallas_call(
        body, name=name,
        out_shape=(jax.ShapeDtypeStruct((steps, r, c), F32), jax.ShapeDtypeStruct((steps, r, c), BF16)),
        grid_spec=pltpu.PrefetchScalarGridSpec(
            num_scalar_prefetch=1, grid=(steps,),
            in_specs=[spec(a, a_keep), spec(b, b_keep), spec(a, a_give), spec(b, b_give)],
            out_specs=(out_spec, out_spec)),
        compiler_params=_cparams("parallel"),
    )(pos, a, b, a, b)


def _add_last(name, a, b):
    _, r, c = a.shape
    blk = pl.BlockSpec((None, r, c), lambda i: (0, 0, 0))

    def body(a_ref, b_ref, o_ref):
        o_ref[...] = a_ref[...] + b_ref[...].astype(F32)

    return pl.pallas_call(
        body, name=name, out_shape=jax.ShapeDtypeStruct((r, c), F32), grid=(1,), in_specs=[blk, blk],
        out_specs=pl.BlockSpec((r, c), lambda i: (0, 0)), compiler_params=_cparams("arbitrary"),
    )(a, b)


def _exchange_begin(tag, stage, gives, lands, peer_fn, extra):
    n = len(gives)

    def plan(refs, x, y, c):
        return [(refs[k], refs[n + k], peer_fn(x, y, c), refs[n + k]) for k in range(n)]

    sems, thru, token = _copies_start(f"rs{tag}_{stage}_start", gives + lands, plan, n)
    return dict(extra, tag=tag, n=n, stage=stage, plan=plan, sems=sems, arrays=thru, token=token)


def _reduce_begin(tag, grads, column_halves=False):
    n = len(grads)
    if column_halves:
        half = lambda ref, j, h: ref.at[j, :, pl.ds(pl.multiple_of(h * (ref.shape[2] // 2), LANES), ref.shape[2] // 2)]
        lands = [lax.empty((N_CHIPS, g.shape[1], g.shape[2] // 2), F32) for g in grads]
    else:
        half = lambda ref, j, h: ref.at[j, h]
        lands = [lax.empty((N_CHIPS,) + g.shape[2:], F32) for g in grads]

    def plan(refs, x, y, c):
        return [(half(refs[k], j, 1 - c), refs[n + k].at[j], (x, y, 1 - c), refs[n + k].at[j])
                for k in range(n) for j in range(N_CHIPS)]

    sems, thru, token = _copies_start(f"rs{tag}_c_start", list(grads) + lands, plan, N_CHIPS * n)
    return dict(tag=tag, n=n, stage="c", plan=plan, sems=sems, arrays=thru, token=token, column_halves=column_halves)


def _reduce_advance(st, after):
    tag, n, stage = st["tag"], st["n"], st["stage"]
    thru = _copies_wait(f"rs{tag}_{stage}_wait", st["sems"], st["arrays"], st["plan"], after)
    first, recv = thru[:n], thru[n:]
    x, y, c = _position()
    if stage == "c":
        pos = jnp.stack([c, x]).astype(jnp.int32)
        if st["column_halves"]:
            mine = lambda chip: (lambda i, p: (chip(p) + i, 0, p[0]))
        else:
            mine = lambda chip: (lambda i, p: (chip(p) + i, p[0]))
        sums = [_add_keep_give(
            f"rs{tag}_c_add{k}", pos,
            first[k], mine(lambda p: 2 * p[1]), mine(lambda p: 2 * (1 - p[1])),
            recv[k], lambda i, p: (2 * p[1] + i,), lambda i, p: (2 * (1 - p[1]) + i,), 2) for k in range(n)]
        lands = [lax.empty(s[1].shape, BF16) for s in sums]
        return _exchange_begin(tag, "x", [s[1] for s in sums], lands, lambda x, y, c: (1 - x, y, c),
                               dict(keep=[s[0] for s in sums]))
    if stage == "x":
        pos = jnp.stack([y]).astype(jnp.int32)
        sums = [_add_keep_give(
            f"rs{tag}_x_add{k}", pos,
            st["keep"][k], lambda i, p: (p[0],), lambda i, p: (1 - p[0],),
            recv[k], lambda i, p: (p[0],), lambda i, p: (1 - p[0],), 1) for k in range(n)]
        lands = [lax.empty(s[1].shape, BF16) for s in sums]
        return _exchange_begin(tag, "y", [s[1] for s in sums], lands, lambda x, y, c: (x, 1 - y, c),
                               dict(keep=[s[0] for s in sums]))
    if stage == "y":
        mine = [_add_last(f"rs{tag}_y_add{k}", st["keep"][k], recv[k]) for k in range(n)]
        lands = [lax.empty(m.shape, F32) for m in mine]
        return _exchange_begin(tag, "swap", mine, lands, lambda x, y, c: (x, y, 1 - c), {})
    return dict(done=list(zip(first, recv)), token=None)


def _small_begin(v, dep):
    land = lax.empty((N_DEV,) + v.shape, F32)
    flips = [(fx, fy, fc) for fx in (0, 1) for fy in (0, 1) for fc in (0, 1)][1:]

    def plan(refs, x, y, c):
        copies = []
        for fx, fy, fc in flips:
            px, py, pc = (1 - x if fx else x), (1 - y if fy else y), (1 - c if fc else c)
            copies.append((refs[0], refs[1].at[4 * x + 2 * y + c], (px, py, pc), refs[1].at[4 * px + 2 * py + pc]))
        return copies

    sems, thru, token = _copies_start("small_start", [v, land], plan, len(flips), dep=dep)
    return dict(plan=plan, sems=sems, arrays=thru, token=token)


def _small_end(st, after):
    own, land = _copies_wait("small_wait", st["sems"], st["arrays"], st["plan"], after)
    x, y, c = _position()
    me = jnp.stack([4 * x + 2 * y + c]).astype(jnp.int32)

    def body(me_ref, own_ref, land_ref, out_ref):
        term = lambda dev: jnp.where(me_ref[0] == dev, own_ref[...], land_ref[dev])
        acc = term(0)
        for dev in range(1, N_DEV):
            acc = acc + term(dev)
        out_ref[...] = acc

    return pl.pallas_call(
        body, name="small_sum", out_shape=jax.ShapeDtypeStruct(own.shape, F32),
        grid_spec=pltpu.PrefetchScalarGridSpec(
            num_scalar_prefetch=1, grid=(1,),
            in_specs=[pl.BlockSpec(own.shape, lambda i, m: (0, 0)), pl.BlockSpec(land.shape, lambda i, m: (0, 0, 0))],
            out_specs=pl.BlockSpec(own.shape, lambda i, m: (0, 0))),
        compiler_params=_cparams("arbitrary"),
    )(me, own, land)


def _all_reduce_small(v):
    r = v.shape[0]

    def body(v_ref, out_ref, buf, send_sems, recv_sems, local_sem):
        x, y, c = _position()
        me, sibling = (x, y, c), (x, y, 1 - c)
        chips = [(1 - x, y), (x, 1 - y), (1 - x, 1 - y)]

        def rows(px, py, pc):
            return buf.at[pl.ds((4 * px + 2 * py + pc) * r, r), :]

        def copy(k, block, to, src=None):
            return _remote(rows(*block) if src is None else src, rows(*block), send_sems.at[k], recv_sems.at[k], to)

        mine = pltpu.make_async_copy(v_ref, rows(*me), local_sem)
        mine.start()
        first = [copy(0, me, sibling, src=v_ref)]
        first += [copy(1 + j, me, (*chip, c), src=v_ref) for j, chip in enumerate(chips)]
        for cp in first:
            cp.start()
        passed = [copy(4 + j, (*chip, c), sibling) for j, chip in enumerate(chips)]
        for j, chip in enumerate(chips):
            copy(1 + j, (*chip, c), me).wait_recv()
            passed[j].start()
        copy(0, sibling, me).wait_recv()
        for j, chip in enumerate(chips):
            copy(4 + j, (*chip, 1 - c), me).wait_recv()
        for cp in first + passed:
            cp.wait_send()
        mine.wait()
        acc = buf[0:r, :]
        for dev in range(1, N_DEV):
            acc = acc + buf[dev * r:(dev + 1) * r, :]
        out_ref[...] = acc

    return pl.pallas_call(
        body, name="all_reduce_small",
        out_shape=jax.ShapeDtypeStruct(v.shape, F32),
        in_specs=[pl.BlockSpec(memory_space=pltpu.VMEM)],
        out_specs=pl.BlockSpec(memory_space=pltpu.VMEM),
        scratch_shapes=[pltpu.VMEM((N_DEV * r, LANES), F32), pltpu.SemaphoreType.DMA((7,)),
                        pltpu.SemaphoreType.DMA((7,)), pltpu.SemaphoreType.DMA],
        compiler_params=pltpu.CompilerParams(has_side_effects=True, vmem_limit_bytes=VMEM_LIMIT_V7X),
    )(v)


def _adamw_update(w, gg, m, v):
    mn = ADAM_B1 * m + (1.0 - ADAM_B1) * gg
    vn = ADAM_B2 * v + (1.0 - ADAM_B2) * (gg * gg)
    m_hat = mn / (1.0 - ADAM_B1 ** ADAM_STEP)
    v_hat = vn / (1.0 - ADAM_B2 ** ADAM_STEP)
    return -ADAM_LR * (m_hat / (jnp.sqrt(v_hat) + ADAM_EPS) + ADAM_WD * w), mn, vn


def _adamw(name, w, g, m, v):
    def body(w_ref, g_ref, m_ref, v_ref, d_ref, mo_ref, vo_ref):
        d_ref[...], mo_ref[...], vo_ref[...] = _adamw_update(w_ref[...], g_ref[...], m_ref[...], v_ref[...])

    blk = pl.BlockSpec(w.shape, lambda i: (0, 0))
    return pl.pallas_call(
        body, name=name, out_shape=(jax.ShapeDtypeStruct(w.shape, F32),) * 3, grid=(1,),
        in_specs=[blk] * 4, out_specs=(blk,) * 3, compiler_params=_cparams("arbitrary"),
    )(w, g, m, v)


def _rows_to_bf16(name, w):
    r, _, c = w.shape

    def body(w_ref, o_ref):
        o_ref[...] = w_ref[:, 0, :].astype(BF16)

    return pl.pallas_call(
        body, name=name, out_shape=jax.ShapeDtypeStruct((r, c), BF16), grid=(1,),
        in_specs=[pl.BlockSpec((r, 1, c), lambda i: (0, 0, 0))], out_specs=pl.BlockSpec((r, c), lambda i: (0, 0)),
        compiler_params=_cparams("arbitrary"),
    )(w)


def _adamw_rows(name, pos_c, w, g_mine, g_other, m, v):
    r, _, c = w.shape
    ch = c // 2

    def body(p_ref, w_ref, gm_ref, go_ref, m_ref, v_ref, g_ref, d_ref, mo_ref, vo_ref):
        gg = jnp.where(pl.program_id(0) == p_ref[0], gm_ref[...], go_ref[...])
        dl, mn, vn = _adamw_update(w_ref[:, 0, :], gg, m_ref[:, 0, :], v_ref[:, 0, :])
        g_ref[:, 0, :] = gg
        d_ref[:, 0, :] = dl
        mo_ref[:, 0, :] = mn
        vo_ref[:, 0, :] = vn

    rows = pl.BlockSpec((r, 1, ch), lambda h, p: (0, 0, h))
    half = pl.BlockSpec((r, ch), lambda h, p: (0, 0))
    return pl.pallas_call(
        body, name=name, out_shape=(jax.ShapeDtypeStruct(w.shape, F32),) * 4,
        grid_spec=pltpu.PrefetchScalarGridSpec(
            num_scalar_prefetch=1, grid=(2,), in_specs=[rows, half, half, rows, rows], out_specs=(rows,) * 4),
        compiler_params=_cparams("parallel"),
    )(pos_c, w, g_mine, g_other, m, v)


def _adamw_halves(name, pos_c, w, g_mine, g_other, m, v, tr, dep=None):
    r, c = w.shape
    rh = r // 2
    tr = tr if rh % tr == 0 else rh
    nt = rh // tr

    def body(p_ref, w_ref, gm_ref, go_ref, m_ref, v_ref, g_ref, d_ref, mo_ref, vo_ref):
        gg = jnp.where(pl.program_id(0) == p_ref[0], gm_ref[...], go_ref[...])
        g_ref[...] = gg
        d_ref[...], mo_ref[...], vo_ref[...] = _adamw_update(w_ref[...], gg, m_ref[...], v_ref[...])

    full = pl.BlockSpec((tr, c), lambda h, i, p: (h * nt + i, 0))
    half = pl.BlockSpec((tr, c), lambda h, i, p: (i, 0))
    dep_specs, dep_ops = _dep_args(dep)
    return pl.pallas_call(
        _after(body, 6, dep), name=name, out_shape=(jax.ShapeDtypeStruct((r, c), F32),) * 4,
        grid_spec=pltpu.PrefetchScalarGridSpec(
            num_scalar_prefetch=1, grid=(2, nt),
            in_specs=[full, half, half, full, full] + dep_specs, out_specs=(full,) * 4),
        compiler_params=_cparams("parallel", "parallel"),
    )(pos_c, w, g_mine, g_other, m, v, *dep_ops)


def _col_sharded_to_comm(g):
    k, n = g.shape
    return g.reshape(2, k // 2, N_CHIPS, n // N_CHIPS).transpose(2, 0, 1, 3)


def _row_sharded_to_comm(g):
    r, c = g.shape
    return g.reshape(N_CHIPS, 2, r // (2 * N_CHIPS), c)


def _col_sharded_full(g):
    _, _, rh, c = g.shape
    return g.reshape(N_CHIPS, 2 * rh, c).transpose(1, 0, 2).reshape(2 * rh, N_CHIPS * c)


def _row_sharded_full(g):
    _, _, rh, c = g.shape
    return g.reshape(N_CHIPS * 2 * rh, c)


def _chip_rows(w3, start, stop, own=None, me=None):
    r = w3.shape[1]
    parts = []
    for chip in range(N_CHIPS):
        lo, hi = max(start - chip * r, 0), min(stop - chip * r, r)
        if lo < hi:
            part = w3[chip, lo:hi]
            parts.append(part if own is None else jnp.where(me == chip, own[lo:hi], part))
    return parts[0] if len(parts) == 1 else jnp.concatenate(parts, axis=0)


def _pack_small(g1, bfv, mix, scale, g2n, gf, extra=None):
    row8 = jnp.pad(bfv.reshape(1, N_HEADS), ((0, 0), (0, LANES - N_HEADS)))
    if extra is not None:
        row8 = row8 + jnp.pad(extra[:, :1], ((0, 0), (N_HEADS, LANES - N_HEADS - 1)))
    return jnp.concatenate([
        g1.reshape(8, LANES), jnp.pad(row8, ((0, 7), (0, 0))), mix.reshape(512, LANES),
        jnp.pad(scale.reshape(4, LANES), ((0, 4), (0, 0))), g2n.reshape(8, LANES), gf.reshape(8, LANES)], axis=0)


def _unpack_small(s, like):
    g1, bfv, mix, scale, g2n, gf = like
    return (s[0:8].reshape(g1.shape), s[8, :N_HEADS].reshape(bfv.shape), s[16:528].reshape(mix.shape),
            s[528:532].reshape(scale.shape), s[536:544].reshape(g2n.shape), s[544:552].reshape(gf.shape))


class _MeshLinks:
    def __init__(self, shards_in, shards_rest):
        self.gin = _gather_begin("in", shards_in, None, column_halves=True)
        self.grest = _gather_begin("rest", shards_rest, self.gin["token"])
        self.tokens = {"gather": self.grest["token"]}
        self.groups = {}

    @property
    def token(self):
        return list(self.tokens.values())

    def tie(self, x):
        return _tie(x, self.token)

    def weights_in(self, after):
        st = _gather_forward(self.gin, after)
        (g,), (own,) = _gather_end(st, st["token"], merge=False)
        return g, own, 2 * lax.axis_index("x") + lax.axis_index("y")

    def rest_forward(self, after):
        self.grest = _gather_forward(self.grest, after)
        self.tokens["gather"] = self.grest["token"]

    def weights_rest(self, after):
        g = _gather_end(self.grest, after)
        del self.tokens["gather"]
        return [_col_sharded_full(g[0]), _col_sharded_full(g[1])] + [_row_sharded_full(a) for a in g[2:]]

    def reduce_begin(self, tag, grads, column_halves=False):
        self.groups[tag] = _reduce_begin(tag, grads, column_halves)
        self.tokens[tag] = self.groups[tag]["token"]

    def advance(self, after):
        for tag, st in self.groups.items():
            if "done" not in st:
                self.groups[tag] = _reduce_advance(st, after)
                if self.groups[tag]["token"] is None:
                    del self.tokens[tag]
                else:
                    self.tokens[tag] = self.groups[tag]["token"]

    def reduced(self, tag):
        return self.groups[tag]["done"]


class _NoLinks:
    token = None

    def __init__(self, w_in, rest):
        self.w_in, self.rest, self.grads = w_in, rest, {}

    def tie(self, x):
        return x

    def weights_in(self, after):
        return self.w_in, None, None

    def rest_forward(self, after):
        pass

    def weights_rest(self, after):
        return self.rest

    def reduce_begin(self, tag, grads, column_halves=False):
        self.grads[tag] = grads

    def advance(self, after):
        pass


def _local_step(links, x, target, seq, norm1_g, b_forget, pool_mix, pool_scale, norm2_g, norm_f_g):
    t, d = x.shape
    tq = min(256, seq)
    aw = ATTN_WIDTH
    o_q, o_f, o_g = POOL_WIDTH, POOL_WIDTH + 3 * aw, POOL_WIDTH + 3 * aw + N_HEADS
    bf = jnp.pad(b_forget, ((0, 0), (0, LANES - N_HEADS)))
    mixb = pool_mix.astype(BF16)

    h = _norm_fwd("norm1_fwd", x, links.tie(norm1_g), 512)
    w_in, own, me = links.weights_in(h)
    wu = _chip_rows(w_in, 0, o_q, own, me)
    wqkv = _chip_rows(w_in, o_q, o_f, own, me)
    wft = jnp.pad(_chip_rows(w_in, o_f, o_g, own, me), ((0, LANES - N_HEADS), (0, 0)))
    wg2 = _chip_rows(w_in, o_g, N_CHIPS * w_in.shape[1], own, me)
    wf = wft.T
    u = _matmul("mm_u", h, wu, "nt", F32, 1024, 512, d)
    g2 = _matmul("mm_gates", h, wg2, "nt", BF16, 1024, 1024, d)
    fl, fcum = _forget_fwd(h, wf, bf, seq)
    qa, ka, v = _attn_prep(h, _head_blocks(wqkv[:aw]), _head_blocks(wqkv[aw:2 * aw]), wqkv[2 * aw:], fcum, 1024)
    p, ps = _pool_fwd(u, mixb, pool_scale, seq)
    links.rest_forward([ps, qa, g2])
    o, lse = _attn_fwd(qa, ka, v, seq, tq, dep=links.token)
    w_pool_out, w_attn_out, w_out, w_ffn_gate, w_ffn_up, w_ffn_down = links.weights_rest(o)
    merged, x1 = _merge_fwd(x, ps, o, g2, w_pool_out, w_attn_out, w_out, 512)
    h2, gt, up, act, x2 = _ffn_fwd(x1, norm2_g, w_ffn_gate, w_ffn_up, w_ffn_down, 1024, 256)
    loss, dx2, d_gf = _final_fwd_bwd(x2, target, norm_f_g, 512)

    dgt, dup, dx1, d_g2n = _ffn_bwd(dx2, x1, norm2_g, gt, up, w_ffn_gate, w_ffn_up, w_ffn_down, 1024, 256)
    d_wd = _matmul("dw_down", act, dx2, "tn", F32, 1408, 1024, 1024)
    d_wg = _matmul("dw_gate", dgt, h2, "tn", F32, 1408, 1024, 1024)
    d_wu = _matmul("dw_up", dup, h2, "tn", F32, 1408, 1024, 1024)
    links.reduce_begin("a", [_row_sharded_to_comm(g) for g in (d_wg, d_wu, d_wd)])
    dpy, day, dg2, dps, da = _merge_bwd(dx1, ps, o, g2, w_pool_out, w_attn_out, w_out, 512, dep=links.token)
    links.advance(dps)
    d_wout = _matmul("dw_out", merged, dx1, "tn", F32, 1024, 1024, 1024)
    d_wpo = _matmul("dw_pool_out", ps, dpy, "tn", F32, 512, 1024, 1024)
    d_wao = _matmul("dw_attn_out", o, day, "tn", F32, 512, 1024, 1024)
    links.reduce_begin("m", [_col_sharded_to_comm(d_wpo), _col_sharded_to_comm(d_wao), _row_sharded_to_comm(d_wout)])
    dq, dk, dv, dfr = _attn_bwd(qa, ka, v, da, lse, seq, tq, dep=links.token)
    links.advance(dq)
    dfc = jnp.pad(dfr.reshape(N_HEADS, t).T, ((0, 0), (0, LANES - N_HEADS)))
    dfl, d_bf = _forget_bwd(dfc, fl, bf, seq)
    du, d_mix, d_scale = _pool_bwd(dps, p, mixb, links.tie(pool_scale), seq)
    d_wu_in = _matmul("dw_in_u", du, h, "tn", F32, 512, 1024, 1024)
    d_wq = _matmul("dw_in_q", dq, h, "tn", F32, 512, 1024, 1024)
    d_wk = _matmul("dw_in_k", dk, h, "tn", F32, 512, 1024, 1024)
    d_wv = _matmul("dw_in_v", dv, h, "tn", F32, 512, 1024, 1024)
    links.advance([d_wu_in, d_wq, d_wk, d_wv])
    d_wf = _matmul("dw_in_f", dfl, h, "tn", F32, LANES, 1024, 512)
    d_wg2 = _matmul("dw_in_gates", dg2, h, "tn", F32, 1024, 1024, 1024, dep=links.token)
    d_win = jnp.concatenate([d_wu_in, d_wq, d_wk, d_wv, d_wf[:N_HEADS], d_wg2], axis=0)
    comm_b = [d_win.reshape(N_CHIPS, d_win.shape[0] // N_CHIPS, d)]
    links.advance(comm_b)
    links.reduce_begin("b", comm_b, column_halves=True)
    dx, d_g1 = _in_bwd(du, dq, dk, dv, dg2, dfl, dx1, x, links.tie(norm1_g), wu, wqkv, wg2, wft, 512)
    links.advance(dx)
    small = (d_g1, d_bf[:, :N_HEADS], d_mix, d_scale, d_g2n, d_gf)
    return loss, dx, small


def kernel(x, norm1_g, w_in, b_forget, pool_mix, pool_scale, w_pool_out, w_attn_out, w_out, norm2_g, w_ffn_gate, w_ffn_up, w_ffn_down, norm_f_g, loss_target, m_norm1_g, m_w_in, m_b_forget, m_pool_mix, m_pool_scale, m_w_pool_out, m_w_attn_out, m_w_out, m_norm2_g, m_w_ffn_gate, m_w_ffn_up, m_w_ffn_down, m_norm_f_g, v_norm1_g, v_w_in, v_b_forget, v_pool_mix, v_pool_scale, v_w_pool_out, v_w_attn_out, v_w_out, v_norm2_g, v_w_ffn_gate, v_w_ffn_up, v_w_ffn_down, v_norm_f_g):
    nb, seq, d = x.shape
    group_a = ((w_ffn_gate, m_w_ffn_gate, v_w_ffn_gate, True, 9), (w_ffn_up, m_w_ffn_up, v_w_ffn_up, True, 10),
               (w_ffn_down, m_w_ffn_down, v_w_ffn_down, False, 11))
    group_m = ((w_pool_out, m_w_pool_out, v_w_pool_out, False, 5), (w_attn_out, m_w_attn_out, v_w_attn_out, False, 6),
               (w_out, m_w_out, v_w_out, False, 7))
    group_b = ((w_in, m_w_in, v_w_in, False, 1),)
    small_w = (norm1_g, b_forget, pool_mix, pool_scale, norm2_g, norm_f_g)
    small_m = (m_norm1_g, m_b_forget, m_pool_mix, m_pool_scale, m_norm2_g, m_norm_f_g)
    small_v = (v_norm1_g, v_b_forget, v_pool_mix, v_pool_scale, v_norm2_g, v_norm_f_g)
    small_pos = (0, 2, 3, 4, 8, 12)
    view = lambda a, tr: a[0].T if tr else a[0]
    unview = lambda a, tr, like: (a.T if tr else a).reshape(like.shape)

    def shard(w, tr):
        lw = view(w, tr).astype(BF16)
        return lw.reshape(2, lw.shape[0] // 2, lw.shape[1])

    cm = lambda a: jnp.transpose(a, (2, 0, 1))
    shard_in = _rows_to_bf16("w_in_to_bf16", cm(w_in))
    links = _MeshLinks([shard_in],
                       [shard(w_pool_out, False), shard(w_attn_out, False), shard(w_out, False),
                        shard(w_ffn_gate, True), shard(w_ffn_up, True), shard(w_ffn_down, False)])
    loss, dx, small_g = _local_step(
        links, x.reshape(nb * seq, d), loss_target.reshape(nb * seq, d), seq,
        norm1_g, b_forget, pool_mix[0], pool_scale, norm2_g, norm_f_g.reshape(1, d))

    grads, deltas, new_m, new_v = [None] * 13, [None] * 13, [None] * 13, [None] * 13
    pos_c = jnp.stack([lax.axis_index("c")]).astype(jnp.int32)

    def update(tag, group, dep):
        last = []
        for k, ((w, m, v, tr, pos), (mine, other)) in enumerate(zip(group, links.reduced(tag))):
            outs = _adamw_halves(f"adamw_{tag}{k}", pos_c, view(w, tr), mine, other, view(m, tr), view(v, tr), 256,
                                 dep=dep)
            grads[pos], deltas[pos], new_m[pos], new_v[pos] = (unview(a, tr, w) for a in outs)
            last.append(outs[1])
        return last

    small_state = _small_begin(_pack_small(*small_g, extra=loss), links.token)
    links.tokens["small"] = small_state["token"]
    last = update("a", group_a, links.token) + update("m", group_m, links.token)
    links.advance(last)
    del links.tokens["small"]
    small_sum = _small_end(small_state, last)
    loss_out = small_sum[8, N_HEADS]
    dl, mn, vn = _adamw("adamw_small", _pack_small(*small_w), small_sum * _small_mask(), _pack_small(*small_m),
                        _pack_small(*small_v))
    for pos, g, a, b, e in zip(small_pos, _unpack_small(small_sum, small_w), _unpack_small(dl, small_w),
                               _unpack_small(mn, small_w), _unpack_small(vn, small_w)):
        grads[pos], deltas[pos], new_m[pos], new_v[pos] = g, a, b, e
    links.advance(dl)
    links.advance(links.token)
    (mine, other), = links.reduced("b")
    outs = _adamw_rows("adamw_b0", pos_c, cm(w_in), mine, other, cm(m_w_in), cm(v_w_in))
    grads[1], deltas[1], new_m[1], new_v[1] = (jnp.transpose(a, (1, 2, 0)) for a in outs)

    return (loss_out, dx.reshape(nb, seq, d), *grads, *deltas, *new_m, *new_v)


def _small_mask():
    rows = lax.broadcasted_iota(jnp.int32, (552, LANES), 0)
    lanes = lax.broadcasted_iota(jnp.int32, (552, LANES), 1)
    return jnp.where(jnp.logical_and(rows == 8, lanes == N_HEADS), 0.0, 1.0).astype(F32)
```

```python
import functools

import jax
import jax.numpy as jnp
from jax import lax
from jax.experimental import pallas as pl
from jax.experimental.pallas import tpu as pltpu

F32 = jnp.float32
BF16 = jnp.bfloat16

D_MODEL = 1024
POOL_WINDOWS = (2, 4, 8, 16)
POOL_GROUPS = 4
POOL_GROUP_DIM = 128
POOL_WIDTH = 512
HEAD_DIM = 64
N_HEADS = 8
ATTN_WIDTH = 512
D_FF = 2816
RMS_EPS = 1e-6
ATTN_SCALE = HEAD_DIM ** -0.5
NEG_BIG = -1e30

ADAM_LR = 0.001
ADAM_B1 = 0.9
ADAM_B2 = 0.999
ADAM_EPS = 1e-08
ADAM_WD = 0.01
ADAM_STEP = 10

LANES = 128
N_CHIPS = 4
N_DEV = 8
VMEM_LIMIT_V7X = 52 * 1024 * 1024
ROW_CHUNK = 256
MESH = pl.DeviceIdType.MESH
ANY = pl.BlockSpec(memory_space=pl.ANY)


def _cparams(*sem):
    return pltpu.CompilerParams(dimension_semantics=sem if sem else None, vmem_limit_bytes=VMEM_LIMIT_V7X)


def _dep_list(dep):
    return [] if dep is None else (list(dep) if isinstance(dep, (list, tuple)) else [dep])


def _after(body, n_in, dep):
    k = len(_dep_list(dep))
    if k == 0:
        return body

    def wrapped(*refs):
        body(*refs[:n_in], *refs[n_in + k:])

    return wrapped


def _dep_args(dep):
    deps = _dep_list(dep)
    return [ANY] * len(deps), deps


def _dot(a, b):
    return lax.dot_general(a, b, (((1,), (0,)), ((), ())), preferred_element_type=F32)


def _dot_nt(a, b):
    return lax.dot_general(a, b, (((1,), (1,)), ((), ())), preferred_element_type=F32)


def _dot_tn(a, b):
    return lax.dot_general(a, b, (((0,), (0,)), ((), ())), preferred_element_type=F32)


def _sigmoid(x):
    return jax.nn.sigmoid(x)


def _rms_fwd(x, g):
    r = lax.rsqrt(jnp.mean(x * x, axis=-1, keepdims=True) + RMS_EPS)
    return (x * r) * g


def _rms_bwd(x, g, dy):
    r = lax.rsqrt(jnp.mean(x * x, axis=-1, keepdims=True) + RMS_EPS)
    xh = x * r
    dg = jnp.sum(dy * xh, axis=0, keepdims=True)
    dxh = dy * g
    dx = r * (dxh - xh * jnp.mean(dxh * xh, axis=-1, keepdims=True))
    return dx, dg


def _matmul(name, a, b, mode, out_dtype, tm, tn, tk, dep=None):
    if mode == "nn":
        (m, k), (_, n) = a.shape, b.shape
    elif mode == "nt":
        (m, k), (n, _) = a.shape, b.shape
    else:
        (k, m), (_, n) = a.shape, b.shape
    tm, tn, tk = min(tm, m), min(tn, n), min(tk, k)
    assert m % tm == 0 and n % tn == 0 and k % tk == 0, (name, m, n, k, tm, tn, tk)
    nk = k // tk
    if mode == "tn":
        a_spec = pl.BlockSpec((tk, tm), lambda i, j, kk: (kk, i))
    else:
        a_spec = pl.BlockSpec((tm, tk), lambda i, j, kk: (i, kk))
    if mode == "nt":
        b_spec = pl.BlockSpec((tn, tk), lambda i, j, kk: (j, kk))
    else:
        b_spec = pl.BlockSpec((tk, tn), lambda i, j, kk: (kk, j))
    dot = {"nn": _dot, "nt": _dot_nt, "tn": _dot_tn}[mode]
    use_scratch = nk > 1 and out_dtype != F32

    def body(a_ref, b_ref, o_ref, *scratch):
        if nk == 1 and mode != "tn":
            rows = min(ROW_CHUNK, tm)
            bb = b_ref[...].astype(BF16)
            for r0 in range(0, tm, rows):
                o_ref[r0:r0 + rows, :] = dot(a_ref[r0:r0 + rows, :].astype(BF16), bb).astype(out_dtype)
            return
        prod = dot(a_ref[...].astype(BF16), b_ref[...].astype(BF16))
        if nk == 1:
            o_ref[...] = prod.astype(out_dtype)
            return
        acc = scratch[0] if use_scratch else o_ref
        kk = pl.program_id(2)

        @pl.when(kk == 0)
        def _():
            acc[...] = prod

        @pl.when(kk > 0)
        def _():
            acc[...] += prod

        if use_scratch:
            @pl.when(kk == nk - 1)
            def _():
                o_ref[...] = acc[...].astype(out_dtype)

    dep_specs, dep_ops = _dep_args(dep)
    return pl.pallas_call(
        _after(body, 2, dep),
        name=name,
        out_shape=jax.ShapeDtypeStruct((m, n), out_dtype),
        grid=(m // tm, n // tn, nk),
        in_specs=[a_spec, b_spec] + dep_specs,
        out_specs=pl.BlockSpec((tm, tn), lambda i, j, kk: (i, j)),
        scratch_shapes=[pltpu.VMEM((tm, tn), F32)] if use_scratch else [],
        compiler_params=_cparams("parallel", "parallel", "arbitrary"),
    )(a, b, *dep_ops)


def _norm_fwd(name, x, g, tm):
    t, d = x.shape
    tm = min(tm, t)

    def body(x_ref, g_ref, h_ref):
        h_ref[...] = _rms_fwd(x_ref[...], g_ref[...]).astype(BF16)

    return pl.pallas_call(
        body, name=name, out_shape=jax.ShapeDtypeStruct((t, d), BF16), grid=(t // tm,),
        in_specs=[pl.BlockSpec((tm, d), lambda i: (i, 0)), pl.BlockSpec((1, d), lambda i: (0, 0))],
        out_specs=pl.BlockSpec((tm, d), lambda i: (i, 0)),
        compiler_params=_cparams("parallel"),
    )(x, g)


def _split3(x):
    hi = x.astype(BF16)
    r1 = x - hi.astype(F32)
    mid = r1.astype(BF16)
    lo = (r1 - mid.astype(F32)).astype(BF16)
    return hi, mid, lo


def _tri_dot(tri, x):
    hi, mid, lo = _split3(x)
    return _dot(tri, hi) + _dot(tri, mid) + _dot(tri, lo)


def _forget_fwd(h, wf, bf, seq):
    t, d = h.shape
    cb = min(256, seq)

    def body(h_ref, wf_ref, bf_ref, fl_ref, fc_ref):
        fl = _dot(h_ref[...], wf_ref[...])
        fl_ref[...] = fl
        xx = fl + bf_ref[...]
        lf = jnp.minimum(xx, 0.0) - jnp.log(1.0 + jnp.exp(-jnp.abs(xx)))
        ri = lax.broadcasted_iota(jnp.int32, (cb, cb), 0)
        ci = lax.broadcasted_iota(jnp.int32, (cb, cb), 1)
        tri = (ri >= ci).astype(BF16)
        carry = jnp.zeros((1, LANES), F32)
        for blk in range(seq // cb):
            cs = _tri_dot(tri, lf[blk * cb:(blk + 1) * cb]) + carry
            fc_ref[blk * cb:(blk + 1) * cb, :] = cs
            carry = cs[cb - 1:cb, :]

    return pl.pallas_call(
        body, name="forget_fwd",
        out_shape=(jax.ShapeDtypeStruct((t, LANES), F32), jax.ShapeDtypeStruct((t, LANES), F32)),
        grid=(t // seq,),
        in_specs=[pl.BlockSpec((seq, d), lambda b: (b, 0)), pl.BlockSpec((d, LANES), lambda b: (0, 0)),
                  pl.BlockSpec((1, LANES), lambda b: (0, 0))],
        out_specs=(pl.BlockSpec((seq, LANES), lambda b: (b, 0)), pl.BlockSpec((seq, LANES), lambda b: (b, 0))),
        compiler_params=_cparams("parallel"),
    )(h, wf, bf)


def _pool_fwd(u, mix, scale, seq):
    t = u.shape[0]

    def body(u_ref, mix_ref, sc_ref, p_ref, ps_ref):
        tpos = lax.broadcasted_iota(jnp.int32, (seq, POOL_GROUP_DIM), 0)
        for g in range(POOL_GROUPS):
            sl = slice(g * POOL_GROUP_DIM, (g + 1) * POOL_GROUP_DIM)
            ug = u_ref[:, sl]
            s = ug
            for lvl in range(g + 1):
                d = 2 ** lvl
                s = s + jnp.where(tpos >= d, pltpu.roll(s, d, 0), 0.0)
            cnt = jnp.minimum(tpos + 1, POOL_WINDOWS[g]).astype(F32)
            pb = (s / cnt - ug).astype(BF16)
            p_ref[:, sl] = pb
            ps_ref[:, sl] = (_dot(pb, mix_ref[g]) * sc_ref[:, sl]).astype(BF16)

    return pl.pallas_call(
        body, name="pool_fwd",
        out_shape=(jax.ShapeDtypeStruct((t, POOL_WIDTH), BF16), jax.ShapeDtypeStruct((t, POOL_WIDTH), BF16)),
        grid=(t // seq,),
        in_specs=[pl.BlockSpec((seq, POOL_WIDTH), lambda b: (b, 0)),
                  pl.BlockSpec((POOL_GROUPS, POOL_GROUP_DIM, POOL_GROUP_DIM), lambda b: (0, 0, 0)),
                  pl.BlockSpec((1, POOL_WIDTH), lambda b: (0, 0))],
        out_specs=(pl.BlockSpec((seq, POOL_WIDTH), lambda b: (b, 0)), pl.BlockSpec((seq, POOL_WIDTH), lambda b: (b, 0))),
        compiler_params=_cparams("parallel"),
    )(u, mix, scale)


def _aug_constants():
    w = N_HEADS * LANES
    rows = jnp.arange(3 * LANES)
    piece, head = rows // LANES, rows % LANES
    cols = jnp.arange(w)
    live = (head < N_HEADS)[:, None]
    pq = (live & (cols[None, :] == (head * LANES + HEAD_DIM + piece)[:, None])).astype(BF16)
    pk = -(live & (cols[None, :] == (head * LANES + HEAD_DIM + 3 + piece)[:, None])).astype(BF16)
    lane = cols % LANES
    oq = ((lane >= HEAD_DIM + 3) & (lane < HEAD_DIM + 6)).astype(F32)[None, :]
    ok = ((lane >= HEAD_DIM) & (lane < HEAD_DIM + 3)).astype(F32)[None, :]
    return pq, pk, oq, ok


def _head_blocks(wt):
    d = wt.shape[1]
    return jnp.pad(wt.reshape(N_HEADS, HEAD_DIM, d), ((0, 0), (0, LANES - HEAD_DIM), (0, 0))).reshape(N_HEADS * LANES, d)


def _attn_prep(h, wq, wk, wv, fcum, tm):
    t, d = h.shape
    tm = min(tm, t)
    rows = min(ROW_CHUNK, tm)
    w = N_HEADS * LANES
    pq, pk, oq, ok = _aug_constants()

    def body(h_ref, wq_ref, wk_ref, wv_ref, f_ref, pq_ref, pk_ref, oq_ref, ok_ref, qa_ref, ka_ref, v_ref):
        for r0 in range(0, tm, rows):
            rs = slice(r0, r0 + rows)
            hh = h_ref[rs, :]
            fs = jnp.concatenate(_split3(f_ref[rs, :]), axis=1)
            q = _dot_nt(hh, wq_ref[...]).astype(BF16).astype(F32) * ATTN_SCALE
            qa_ref[rs, :] = (q + _dot(fs, pq_ref[...]) + oq_ref[...]).astype(BF16)
            k = _dot_nt(hh, wk_ref[...]).astype(BF16).astype(F32)
            ka_ref[rs, :] = (k + _dot(fs, pk_ref[...]) + ok_ref[...]).astype(BF16)
            v_ref[rs, :] = _dot_nt(hh, wv_ref[...]).astype(BF16)

    row = lambda n: pl.BlockSpec((tm, n), lambda i: (i, 0))
    full = lambda a: pl.BlockSpec(a.shape, lambda i: (0, 0))
    return pl.pallas_call(
        body, name="attn_prep",
        out_shape=(jax.ShapeDtypeStruct((t, w), BF16), jax.ShapeDtypeStruct((t, w), BF16),
                   jax.ShapeDtypeStruct((t, ATTN_WIDTH), BF16)),
        grid=(t // tm,),
        in_specs=[row(d), full(wq), full(wk), full(wv), row(LANES), full(pq), full(pk), full(oq), full(ok)],
        out_specs=(row(w), row(w), row(ATTN_WIDTH)),
        compiler_params=_cparams("parallel"),
    )(h, wq, wk, wv, fcum, pq, pk, oq, ok)


def _fold_lanes(x, op):
    out = x[:, :LANES]
    for g in range(1, x.shape[1] // LANES):
        out = op(out, x[:, g * LANES:(g + 1) * LANES])
    return out


def _causal_sweep(i, tile, carry):
    def pair(jj, c):
        return tile(2 * jj + 1, tile(2 * jj, c, False), False)

    carry = lax.fori_loop(0, i // 2, pair, carry)
    return lax.cond(i % 2 == 1, lambda c: tile(i, tile(i - 1, c, False), True), lambda c: tile(i, c, True), carry)


def _attn_fwd(qa, ka, v, seq, tq, dep=None):
    t = qa.shape[0]
    nq = seq // tq
    hp_n = N_HEADS // 2
    heads = [slice(e * LANES, (e + 1) * LANES) for e in range(2)]

    def body(q_ref, k_ref, v_ref, o_ref, lse_ref, s_buf):
        i = pl.program_id(2)
        diag_ok = lax.broadcasted_iota(jnp.int32, (tq, tq), 0) >= lax.broadcasted_iota(jnp.int32, (tq, tq), 1)
        qs = [q_ref[:, hl] for hl in heads]

        def sweep1(j, mxs, diagonal):
            r0 = pl.multiple_of(j * tq, tq)
            out = []
            for e, hl in enumerate(heads):
                s = _dot_nt(qs[e], k_ref[pl.ds(r0, tq), hl])
                if diagonal:
                    s = jnp.where(diag_ok, s, NEG_BIG)
                s_buf[e, j] = s
                out.append(jnp.maximum(mxs[e], _fold_lanes(s, jnp.maximum)))
            return tuple(out)

        mxs = _causal_sweep(i, sweep1, (jnp.full((tq, LANES), NEG_BIG, F32),) * 2)
        ms = [jnp.max(mx, axis=1, keepdims=True) for mx in mxs]

        def sweep2(j, carry, diagonal):
            r0 = pl.multiple_of(j * tq, tq)
            vv = v_ref[pl.ds(r0, tq), :]
            out = []
            for e in range(2):
                p = jnp.exp(s_buf[e, j] - ms[e])
                out += [carry[2 * e] + _fold_lanes(p, jnp.add), carry[2 * e + 1] + _dot(p.astype(BF16), vv)]
            return tuple(out)

        res = _causal_sweep(i, sweep2, (jnp.zeros((tq, LANES), F32),) * 4)
        outs = []
        for e in range(2):
            l = jnp.sum(res[2 * e], axis=1, keepdims=True)
            outs.append(res[2 * e + 1] / l)
            lse_ref[:, e:e + 1] = ms[e] + jnp.log(l)
        lane = lax.broadcasted_iota(jnp.int32, (tq, LANES), 1)
        o_ref[...] = jnp.where(lane < HEAD_DIM, outs[0], outs[1])

    dep_specs, dep_ops = _dep_args(dep)
    return pl.pallas_call(
        _after(body, 3, dep), name="attn_fwd",
        out_shape=(jax.ShapeDtypeStruct((t, ATTN_WIDTH), F32), jax.ShapeDtypeStruct((hp_n, t, 2), F32)),
        grid=(t // seq, hp_n, nq),
        in_specs=[pl.BlockSpec((tq, 2 * LANES), lambda b, hp, i: (b * nq + i, hp)),
                  pl.BlockSpec((seq, 2 * LANES), lambda b, hp, i: (b, hp)),
                  pl.BlockSpec((seq, LANES), lambda b, hp, i: (b, hp))] + dep_specs,
        out_specs=(pl.BlockSpec((tq, LANES), lambda b, hp, i: (b * nq + i, hp)),
                   pl.BlockSpec((None, tq, 2), lambda b, hp, i: (hp, b * nq + i, 0))),
        scratch_shapes=[pltpu.VMEM((2, nq, tq, tq), F32)],
        compiler_params=_cparams("parallel", "parallel", "arbitrary"),
    )(qa, ka, v, *dep_ops)


def _merge_fwd(x, ps, o, g2, wpo, wao, wout, tm):
    t, d = x.shape
    tm = min(tm, t)
    rows = min(ROW_CHUNK, tm)

    def body(x_ref, ps_ref, o_ref, gp_ref, ga_ref, wpo_ref, wao_ref, wout_ref, mg_ref, x1_ref):
        for r0 in range(0, tm, rows):
            rs = slice(r0, r0 + rows)
            py = _dot(ps_ref[rs, :], wpo_ref[...])
            ay = _dot(o_ref[rs, :].astype(BF16), wao_ref[...])
            mb = (_sigmoid(gp_ref[rs, :].astype(F32)) * py + _sigmoid(ga_ref[rs, :].astype(F32)) * ay).astype(BF16)
            mg_ref[rs, :] = mb
            x1_ref[rs, :] = x_ref[rs, :] + _dot(mb, wout_ref[...])

    row = lambda w: pl.BlockSpec((tm, w), lambda i: (i, 0))
    full = lambda a: pl.BlockSpec(a.shape, lambda i: (0, 0))
    return pl.pallas_call(
        body, name="merge_fwd",
        out_shape=(jax.ShapeDtypeStruct((t, d), BF16), jax.ShapeDtypeStruct((t, d), F32)),
        grid=(t // tm,),
        in_specs=[row(d), row(POOL_WIDTH), row(ATTN_WIDTH), pl.BlockSpec((tm, d), lambda i: (i, 0)),
                  pl.BlockSpec((tm, d), lambda i: (i, 1)), full(wpo), full(wao), full(wout)],
        out_specs=(row(d), row(d)),
        compiler_params=_cparams("parallel"),
    )(x, ps, o, g2, g2, wpo, wao, wout)


def _ffn_fwd(x1, g, wg, wu, wd, tm, tf):
    t, d = x1.shape
    f = wg.shape[0]
    tm = min(tm, t)
    nf = f // tf
    rows = min(512, tm)

    def body(x1_ref, g_ref, wg_ref, wu_ref, wd_ref, h2_ref, gt_ref, up_ref, act_ref, x2_ref):
        j = pl.program_id(1)

        @pl.when(j == 0)
        def _():
            h2_ref[...] = _rms_fwd(x1_ref[...], g_ref[...]).astype(BF16)

            x2_ref[...] = x1_ref[...]

        for r0 in range(0, tm, rows):
            rs = slice(r0, r0 + rows)
            h2 = h2_ref[rs, :]
            gt = _dot_nt(h2, wg_ref[...])
            up = _dot_nt(h2, wu_ref[...])
            sg = _sigmoid(gt)
            silu = gt * sg
            act = (silu * up).astype(BF16)
            gt_ref[rs, :] = (up * (sg * (1.0 + gt * (1.0 - sg)))).astype(BF16)
            up_ref[rs, :] = silu.astype(BF16)
            act_ref[rs, :] = act
            x2_ref[rs, :] += _dot(act, wd_ref[...])

    return pl.pallas_call(
        body, name="ffn_fwd",
        out_shape=(jax.ShapeDtypeStruct((t, d), BF16), jax.ShapeDtypeStruct((t, f), BF16),
                   jax.ShapeDtypeStruct((t, f), BF16), jax.ShapeDtypeStruct((t, f), BF16),
                   jax.ShapeDtypeStruct((t, d), F32)),
        grid=(t // tm, nf),
        in_specs=[pl.BlockSpec((tm, d), lambda i, j: (i, 0)), pl.BlockSpec((1, d), lambda i, j: (0, 0)),
                  pl.BlockSpec((tf, d), lambda i, j: (j, 0)), pl.BlockSpec((tf, d), lambda i, j: (j, 0)),
                  pl.BlockSpec((tf, d), lambda i, j: (j, 0))],
        out_specs=(pl.BlockSpec((tm, d), lambda i, j: (i, 0)), pl.BlockSpec((tm, tf), lambda i, j: (i, j)),
                   pl.BlockSpec((tm, tf), lambda i, j: (i, j)), pl.BlockSpec((tm, tf), lambda i, j: (i, j)),
                   pl.BlockSpec((tm, d), lambda i, j: (i, 0))),
        compiler_params=_cparams("parallel", "arbitrary"),
    )(x1, g, wg, wu, wd)


def _final_fwd_bwd(x2, target, g, tm):
    t, d = x2.shape
    tm = min(tm, t)

    def body(x_ref, t_ref, g_ref, loss_ref, dx_ref, dg_ref):
        i = pl.program_id(0)
        x = x_ref[...]
        gg = g_ref[...]
        err = _rms_fwd(x, gg) - t_ref[...]
        part = 0.5 * jnp.sum(jnp.mean(err * err, axis=-1, keepdims=True), axis=0, keepdims=True)
        dx, dg = _rms_bwd(x, gg, err * (1.0 / d))
        dx_ref[...] = dx

        @pl.when(i == 0)
        def _():
            loss_ref[...] = jnp.zeros_like(loss_ref)
            dg_ref[...] = jnp.zeros_like(dg_ref)

        loss_ref[...] += jnp.broadcast_to(part, loss_ref.shape)
        dg_ref[...] += dg

    return pl.pallas_call(
        body, name="final_fwd_bwd",
        out_shape=(jax.ShapeDtypeStruct((1, LANES), F32), jax.ShapeDtypeStruct((t, d), F32),
                   jax.ShapeDtypeStruct((1, d), F32)),
        grid=(t // tm,),
        in_specs=[pl.BlockSpec((tm, d), lambda i: (i, 0)), pl.BlockSpec((tm, d), lambda i: (i, 0)),
                  pl.BlockSpec((1, d), lambda i: (0, 0))],
        out_specs=(pl.BlockSpec((1, LANES), lambda i: (0, 0)), pl.BlockSpec((tm, d), lambda i: (i, 0)),
                   pl.BlockSpec((1, d), lambda i: (0, 0))),
        compiler_params=_cparams("arbitrary"),
    )(x2, target, g)


def _ffn_bwd(dx2, x1, g, gt, up, wg, wu, wd, tm, tf):
    t, d = dx2.shape
    f = gt.shape[1]
    tm = min(tm, t)
    nf = f // tf
    wgu = jnp.concatenate([wg.reshape(nf, tf, d), wu.reshape(nf, tf, d)], axis=1).reshape(2 * f, d)
    rows = min(256, tm)

    def body(dx2_ref, x1_ref, g_ref, gt_ref, up_ref, wgu_ref, wd_ref, dgt_ref, dup_ref, dx1_ref, dg_ref, acc_ref,
             dxb_ref):
        i, j = pl.program_id(0), pl.program_id(1)

        @pl.when(j == 0)
        def _():
            dxb_ref[...] = dx2_ref[...].astype(BF16)
            acc_ref[...] = jnp.zeros_like(acc_ref)

        for r0 in range(0, tm, rows):
            rs = slice(r0, r0 + rows)
            dact = _dot_nt(dxb_ref[rs, :], wd_ref[...])
            dgt = (dact * gt_ref[rs, :].astype(F32)).astype(BF16)
            dup = (dact * up_ref[rs, :].astype(F32)).astype(BF16)
            dgt_ref[rs, :] = dgt
            dup_ref[rs, :] = dup
            acc_ref[rs, :] += _dot(jnp.concatenate([dgt, dup], axis=1), wgu_ref[...])

        @pl.when(jnp.logical_and(i == 0, j == 0))
        def _():
            dg_ref[...] = jnp.zeros_like(dg_ref)

        @pl.when(j == nf - 1)
        def _():
            dxn, dg = _rms_bwd(x1_ref[...], g_ref[...], acc_ref[...])
            dx1_ref[...] = dx2_ref[...] + dxn
            dg_ref[...] += dg

    return pl.pallas_call(
        body, name="ffn_bwd",
        out_shape=(jax.ShapeDtypeStruct((t, f), BF16), jax.ShapeDtypeStruct((t, f), BF16),
                   jax.ShapeDtypeStruct((t, d), F32), jax.ShapeDtypeStruct((1, d), F32)),
        grid=(t // tm, nf),
        in_specs=[pl.BlockSpec((tm, d), lambda i, j: (i, 0)), pl.BlockSpec((tm, d), lambda i, j: (i, 0)),
                  pl.BlockSpec((1, d), lambda i, j: (0, 0)),
                  pl.BlockSpec((tm, tf), lambda i, j: (i, j)), pl.BlockSpec((tm, tf), lambda i, j: (i, j)),
                  pl.BlockSpec((2 * tf, d), lambda i, j: (j, 0)), pl.BlockSpec((tf, d), lambda i, j: (j, 0))],
        out_specs=(pl.BlockSpec((tm, tf), lambda i, j: (i, j)), pl.BlockSpec((tm, tf), lambda i, j: (i, j)),
                   pl.BlockSpec((tm, d), lambda i, j: (i, 0)), pl.BlockSpec((1, d), lambda i, j: (0, 0))),
        scratch_shapes=[pltpu.VMEM((tm, d), F32), pltpu.VMEM((tm, d), BF16)],
        compiler_params=_cparams("arbitrary", "arbitrary"),
    )(dx2, x1, g, gt, up, wgu, wd)


def _merge_bwd(dx1, ps, o, g2, wpo, wao, wout, tm, dep=None):
    t, d = dx1.shape
    tm = min(tm, t)
    rows = min(ROW_CHUNK, tm)

    def body(dx1_ref, ps_ref, o_ref, gp_ref, ga_ref, wpo_ref, wao_ref, wout_ref, dpy_ref, day_ref, dg2_ref, dps_ref, da_ref):
        for r0 in range(0, tm, rows):
            rs = slice(r0, r0 + rows)
            dm = _dot_nt(dx1_ref[rs, :].astype(BF16), wout_ref[...])
            py = _dot(ps_ref[rs, :], wpo_ref[...])
            ay = _dot(o_ref[rs, :].astype(BF16), wao_ref[...])
            sp = _sigmoid(gp_ref[rs, :].astype(F32))
            sa = _sigmoid(ga_ref[rs, :].astype(F32))
            dpy = (dm * sp).astype(BF16)
            day = (dm * sa).astype(BF16)
            dpy_ref[rs, :] = dpy
            day_ref[rs, :] = day
            dg2_ref[rs, :d] = (dm * py * (sp * (1.0 - sp))).astype(BF16)
            dg2_ref[rs, d:] = (dm * ay * (sa * (1.0 - sa))).astype(BF16)
            dps_ref[rs, :] = _dot_nt(dpy, wpo_ref[...])
            da_ref[rs, :] = _dot_nt(day, wao_ref[...]).astype(BF16)

    row = lambda w: pl.BlockSpec((tm, w), lambda i: (i, 0))
    full = lambda a: pl.BlockSpec(a.shape, lambda i: (0, 0))
    dep_specs, dep_ops = _dep_args(dep)
    return pl.pallas_call(
        _after(body, 8, dep), name="merge_bwd",
        out_shape=(jax.ShapeDtypeStruct((t, d), BF16), jax.ShapeDtypeStruct((t, d), BF16),
                   jax.ShapeDtypeStruct((t, 2 * d), BF16), jax.ShapeDtypeStruct((t, POOL_WIDTH), F32),
                   jax.ShapeDtypeStruct((t, ATTN_WIDTH), BF16)),
        grid=(t // tm,),
        in_specs=[row(d), row(POOL_WIDTH), row(ATTN_WIDTH), pl.BlockSpec((tm, d), lambda i: (i, 0)),
                  pl.BlockSpec((tm, d), lambda i: (i, 1)), full(wpo), full(wao), full(wout)] + dep_specs,
        out_specs=(row(d), row(d), row(2 * d), row(POOL_WIDTH), row(ATTN_WIDTH)),
        compiler_params=_cparams("parallel"),
    )(dx1, ps, o, g2, g2, wpo, wao, wout, *dep_ops)


def _attn_bwd(qa, ka, v, do, lse4, seq, tq, dep=None):
    t = qa.shape[0]
    nq = seq // tq
    hp_n = N_HEADS // 2
    heads = [slice(e * LANES, (e + 1) * LANES) for e in range(2)]

    def body(q_ref, k_ref, v_ref, do_ref, lse_ref, dq_ref, dk_ref, dv_ref, dfr_ref, dk_acc, dv_acc, p_buf, dp_buf):
        diag_ok = lax.broadcasted_iota(jnp.int32, (tq, tq), 0) >= lax.broadcasted_iota(jnp.int32, (tq, tq), 1)
        lane_q = lax.broadcasted_iota(jnp.int32, (tq, LANES), 1)
        mine_q = [lane_q < HEAD_DIM, lane_q >= HEAD_DIM]
        dv_acc[...] = jnp.zeros_like(dv_acc)
        dk_acc[...] = jnp.zeros_like(dk_acc)
        dfr_ref[...] = jnp.zeros_like(dfr_ref)
        transposed = lambda a: a.astype(F32).T.astype(BF16)

        def q_step(i, _):
            q0 = pl.multiple_of(i * tq, tq)
            qs = [q_ref[pl.ds(q0, tq), hl] for hl in heads]
            dov = do_ref[pl.ds(q0, tq), :]
            dos = [jnp.where(mq, dov, jnp.zeros((), BF16)) for mq in mine_q]
            qts = [transposed(q) for q in qs]
            dots = [transposed(a) for a in dos]
            lss = [lse_ref[pl.ds(q0, tq), e:e + 1] for e in range(2)]

            def sweep1(j, dls, diagonal):
                r0 = pl.multiple_of(j * tq, tq)
                vv = v_ref[pl.ds(r0, tq), :]
                out = []
                for e, hl in enumerate(heads):
                    s = _dot_nt(qs[e], k_ref[pl.ds(r0, tq), hl])
                    if diagonal:
                        s = jnp.where(diag_ok, s, NEG_BIG)
                    p = jnp.exp(s - lss[e])
                    dp = _dot_nt(dos[e], vv)
                    p_buf[e, j] = p
                    dp_buf[e, j] = dp
                    dv_acc[j] += _dot(dots[e], p.astype(BF16))
                    out.append(dls[e] + _fold_lanes(p * dp, jnp.add))
                return tuple(out)

            dls = _causal_sweep(i, sweep1, (jnp.zeros((tq, LANES), F32),) * 2)
            dls = [jnp.sum(d, axis=1, keepdims=True) for d in dls]

            def sweep2(j, dqs, diagonal):
                r0 = pl.multiple_of(j * tq, tq)
                out = []
                for e, hl in enumerate(heads):
                    ds = p_buf[e, j] * (dp_buf[e, j] - dls[e])
                    dfr_ref[e, pl.ds(j, 1), :] += jnp.sum(ds, axis=0, keepdims=True)
                    dsb = ds.astype(BF16)
                    dk_acc[e, j] += _dot(qts[e], dsb)
                    out.append(dqs[e] + _dot(dsb, k_ref[pl.ds(r0, tq), hl]))
                return tuple(out)

            dqs = _causal_sweep(i, sweep2, (jnp.zeros((tq, LANES), F32),) * 2)
            dq = jnp.where(mine_q[0], dqs[0], pltpu.roll(dqs[1], HEAD_DIM, 1)) * ATTN_SCALE
            dq_ref[pl.ds(q0, tq), :] = dq.astype(BF16)
            return 0

        lax.fori_loop(0, nq, q_step, 0)
        for j in range(nq):
            rs = slice(j * tq, (j + 1) * tq)
            dk = jnp.where(mine_q[0], dk_acc[0, j].T, pltpu.roll(dk_acc[1, j].T, HEAD_DIM, 1))
            dk_ref[rs, :] = dk.astype(BF16)
            dv_ref[rs, :] = dv_acc[j].T.astype(BF16)

    wide = pl.BlockSpec((seq, 2 * LANES), lambda b, hp: (b, hp))
    col = pl.BlockSpec((seq, LANES), lambda b, hp: (b, hp))
    pair = pl.BlockSpec((None, seq, 2), lambda b, hp: (hp, b, 0))
    dep_specs, dep_ops = _dep_args(dep)
    return pl.pallas_call(
        _after(body, 5, dep), name="attn_bwd",
        out_shape=(jax.ShapeDtypeStruct((t, ATTN_WIDTH), BF16),) * 3 + (jax.ShapeDtypeStruct((N_HEADS, t // tq, tq), F32),),
        grid=(t // seq, hp_n),
        in_specs=[wide, wide, col, col, pair] + dep_specs,
        out_specs=(col, col, col, pl.BlockSpec((2, nq, tq), lambda b, hp: (hp, b, 0))),
        scratch_shapes=[pltpu.VMEM((2, nq, LANES, tq), F32), pltpu.VMEM((nq, LANES, tq), F32),
                        pltpu.VMEM((2, nq, tq, tq), F32), pltpu.VMEM((2, nq, tq, tq), F32)],
        compiler_params=_cparams("parallel", "arbitrary"),
    )(qa, ka, v, do, lse4, *dep_ops)


def _forget_bwd(dfc, fl, bf, seq):
    t = fl.shape[0]
    cb = min(256, seq)
    nb = seq // cb

    def body(dfc_ref, fl_ref, bf_ref, dfl_ref, db_ref):
        b = pl.program_id(0)
        ri = lax.broadcasted_iota(jnp.int32, (cb, cb), 0)
        ci = lax.broadcasted_iota(jnp.int32, (cb, cb), 1)
        tri = (ci >= ri).astype(BF16)
        carry = jnp.zeros((1, LANES), F32)
        dbs = jnp.zeros((1, LANES), F32)
        for blk in reversed(range(nb)):
            rs = slice(blk * cb, (blk + 1) * cb)
            dlf = _tri_dot(tri, -dfc_ref[rs, :]) + carry
            carry = dlf[0:1, :]
            dfl = dlf * _sigmoid(-(fl_ref[rs, :] + bf_ref[...]))
            dfl_ref[rs, :] = dfl.astype(BF16)
            dbs = dbs + jnp.sum(dfl, axis=0, keepdims=True)

        @pl.when(b == 0)
        def _():
            db_ref[...] = jnp.zeros_like(db_ref)

        db_ref[...] += dbs

    return pl.pallas_call(
        body, name="forget_bwd",
        out_shape=(jax.ShapeDtypeStruct((t, LANES), BF16), jax.ShapeDtypeStruct((1, LANES), F32)),
        grid=(t // seq,),
        in_specs=[pl.BlockSpec((seq, LANES), lambda b: (b, 0)), pl.BlockSpec((seq, LANES), lambda b: (b, 0)),
                  pl.BlockSpec((1, LANES), lambda b: (0, 0))],
        out_specs=(pl.BlockSpec((seq, LANES), lambda b: (b, 0)), pl.BlockSpec((1, LANES), lambda b: (0, 0))),
        compiler_params=_cparams("arbitrary"),
    )(dfc, fl, bf)


def _pool_bwd(dps, p, mix, scale, seq):
    t = dps.shape[0]

    def body(dps_ref, p_ref, mix_ref, sc_ref, du_ref, dmix_ref, dsc_ref):
        b = pl.program_id(0)

        @pl.when(b == 0)
        def _():
            dmix_ref[...] = jnp.zeros_like(dmix_ref)
            dsc_ref[...] = jnp.zeros_like(dsc_ref)

        tpos = lax.broadcasted_iota(jnp.int32, (seq, POOL_GROUP_DIM), 0)
        for g in range(POOL_GROUPS):
            sl = slice(g * POOL_GROUP_DIM, (g + 1) * POOL_GROUP_DIM)
            pb = p_ref[:, sl]
            dpsg = dps_ref[:, sl]
            pm = _dot(pb, mix_ref[g])
            dsc_ref[:, sl] += jnp.sum(dpsg * pm, axis=0, keepdims=True)
            dpm = (dpsg * sc_ref[:, sl]).astype(BF16)
            dmix_ref[g] += _dot_tn(pb, dpm)
            dp = _dot_nt(dpm, mix_ref[g])
            cnt = jnp.minimum(tpos + 1, POOL_WINDOWS[g]).astype(F32)
            s = dp / cnt
            for lvl in range(g + 1):
                d = 2 ** lvl
                s = s + jnp.where(tpos < seq - d, pltpu.roll(s, seq - d, 0), 0.0)
            du_ref[:, sl] = (s - dp).astype(BF16)

    return pl.pallas_call(
        body, name="pool_bwd",
        out_shape=(jax.ShapeDtypeStruct((t, POOL_WIDTH), BF16),
                   jax.ShapeDtypeStruct((POOL_GROUPS, POOL_GROUP_DIM, POOL_GROUP_DIM), F32),
                   jax.ShapeDtypeStruct((1, POOL_WIDTH), F32)),
        grid=(t // seq,),
        in_specs=[pl.BlockSpec((seq, POOL_WIDTH), lambda b: (b, 0)), pl.BlockSpec((seq, POOL_WIDTH), lambda b: (b, 0)),
                  pl.BlockSpec((POOL_GROUPS, POOL_GROUP_DIM, POOL_GROUP_DIM), lambda b: (0, 0, 0)),
                  pl.BlockSpec((1, POOL_WIDTH), lambda b: (0, 0))],
        out_specs=(pl.BlockSpec((seq, POOL_WIDTH), lambda b: (b, 0)),
                   pl.BlockSpec((POOL_GROUPS, POOL_GROUP_DIM, POOL_GROUP_DIM), lambda b: (0, 0, 0)),
                   pl.BlockSpec((1, POOL_WIDTH), lambda b: (0, 0))),
        compiler_params=_cparams("arbitrary"),
    )(dps, p, mix, scale)


def _in_bwd(du, dq, dk, dv, dg2, dfl, dx1, x, g, wu, wqkv, wg2, wft, tm):
    t, d = x.shape
    tm = min(tm, t)
    rows = min(ROW_CHUNK, tm)
    aw = ATTN_WIDTH

    def body(du_ref, dq_ref, dk_ref, dv_ref, dg2_ref, dfl_ref, dx1_ref, x_ref, g_ref, wu_ref, wqkv_ref, wg2_ref, wft_ref,
             dx_ref, dg_ref):
        i = pl.program_id(0)

        @pl.when(i == 0)
        def _():
            dg_ref[...] = jnp.zeros_like(dg_ref)

        for r0 in range(0, tm, rows):
            rs = slice(r0, r0 + rows)
            dh = _dot(du_ref[rs, :], wu_ref[...])
            dh += _dot(dq_ref[rs, :], wqkv_ref[0:aw, :])
            dh += _dot(dk_ref[rs, :], wqkv_ref[aw:2 * aw, :])
            dh += _dot(dv_ref[rs, :], wqkv_ref[2 * aw:3 * aw, :])
            dh += _dot(dg2_ref[rs, :], wg2_ref[...])
            dh += _dot(dfl_ref[rs, :], wft_ref[...])
            dxn, dg = _rms_bwd(x_ref[rs, :], g_ref[...], dh)
            dx_ref[rs, :] = dx1_ref[rs, :] + dxn
            dg_ref[...] += dg

    row = lambda w: pl.BlockSpec((tm, w), lambda i: (i, 0))
    full = lambda a: pl.BlockSpec(a.shape, lambda i: (0, 0))
    return pl.pallas_call(
        body, name="in_bwd",
        out_shape=(jax.ShapeDtypeStruct((t, d), F32), jax.ShapeDtypeStruct((1, d), F32)),
        grid=(t // tm,),
        in_specs=[row(POOL_WIDTH), row(aw), row(aw), row(aw), row(2 * d), row(LANES), row(d), row(d),
                  pl.BlockSpec((1, d), lambda i: (0, 0)), full(wu), full(wqkv), full(wg2), full(wft)],
        out_specs=(row(d), pl.BlockSpec((1, d), lambda i: (0, 0))),
        compiler_params=_cparams("arbitrary"),
    )(du, dq, dk, dv, dg2, dfl, dx1, x, g, wu, wqkv, wg2, wft)


def _position():
    return lax.axis_index("x"), lax.axis_index("y"), lax.axis_index("c")


def _remote(src, dst, send_sem, recv_sem, device):
    return pltpu.make_async_remote_copy(src_ref=src, dst_ref=dst, send_sem=send_sem, recv_sem=recv_sem,
                                        device_id=device, device_id_type=MESH)


HBM = pl.BlockSpec(memory_space=pltpu.HBM)
SEM = pl.BlockSpec(memory_space=pltpu.SEMAPHORE)
DATAFLOW = pltpu.SideEffectType.DATAFLOW_SIDE_EFFECTING


def _copies_start(name, arrays, plan, m, dep=None):
    n = len(arrays)
    arrays = [pltpu.with_memory_space_constraint(a, pltpu.HBM) for a in arrays]

    def body(*refs):
        ins, send_sem, recv_sem, token = refs[:n], refs[n], refs[n + 1], refs[2 * n + 2]
        for i, (src, dst, device, _) in enumerate(plan(ins, *_position())):
            _remote(src, dst, send_sem.at[i], recv_sem.at[i], device).start()
        token[...] = jnp.zeros_like(token)

    dep_specs, dep_ops = _dep_args(dep)
    outs = pl.pallas_call(
        _after(body, n, dep), name=name,
        out_shape=(pltpu.SemaphoreType.DMA((m,)), pltpu.SemaphoreType.DMA((m,)),
                   *[pltpu.HBM(a.shape, a.dtype) for a in arrays], jax.ShapeDtypeStruct((8, LANES), F32)),
        in_specs=[HBM] * n + dep_specs, out_specs=(SEM, SEM, *[HBM] * n, pl.BlockSpec(memory_space=pltpu.VMEM)),
        input_output_aliases={i: i + 2 for i in range(n)},
        compiler_params=pltpu.CompilerParams(has_side_effects=DATAFLOW),
    )(*arrays, *dep_ops)
    return (outs[0], outs[1]), list(outs[2:2 + n]), outs[2 + n]


def _copies_wait(name, sems, arrays, plan, after):
    n = len(arrays)
    afters = list(after) if isinstance(after, (list, tuple)) else [after]

    def body(*refs):
        ins, send_sem, recv_sem = refs[:n], refs[n], refs[n + 1]
        for i, (src, dst, device, landing) in enumerate(plan(ins, *_position())):
            _remote(src, dst, send_sem.at[i], recv_sem.at[i], device).wait_send()
            _remote(landing, landing, send_sem.at[i], recv_sem.at[i], device).wait_recv()

    outs = pl.pallas_call(
        body, name=name,
        out_shape=tuple(pltpu.HBM(a.shape, a.dtype) for a in arrays),
        in_specs=[HBM] * n + [SEM, SEM] + [ANY] * len(afters), out_specs=tuple([HBM] * n),
        input_output_aliases={i: i for i in range(n)},
        compiler_params=pltpu.CompilerParams(has_side_effects=DATAFLOW),
    )(*arrays, sems[0], sems[1], *afters)
    return list(outs)


def _tie(x, dep):
    for token in _dep_list(dep):
        x = x + token[0, 0]
    return x


def _other_chips(x, y):
    return [(1 - x, y), (x, 1 - y), (1 - x, 1 - y)]


def _gather_begin(tag, shards, token, column_halves=False):
    n = len(shards)
    lands = [lax.empty((N_CHIPS,) + s.shape, s.dtype) for s in shards]
    if column_halves:
        cols = lambda ref, h: pl.ds(pl.multiple_of(h * (ref.shape[-1] // 2), LANES), ref.shape[-1] // 2)
        mine = lambda ref, h: ref.at[:, cols(ref, h)]
        landed = lambda ref, chip, h: ref.at[chip, :, cols(ref, h)]
    else:
        mine = lambda ref, h: ref.at[h]
        landed = lambda ref, chip, h: ref.at[chip, h]

    def plan(refs, x, y, c):
        return [(mine(refs[k], c), landed(refs[n + k], 2 * x + y, c), (ox, oy, c), landed(refs[n + k], 2 * ox + oy, c))
                for k in range(n) for ox, oy in _other_chips(x, y)]

    sems, thru, token = _copies_start(f"gather_{tag}_ici_start", list(shards) + lands, plan, 3 * n, dep=token)
    return dict(tag=tag, n=n, plan=plan, sems=sems, arrays=thru, token=token, landed=landed)


def _gather_forward(st, after):
    n, tag, landed = st["n"], st["tag"], st["landed"]
    thru = _copies_wait(f"gather_{tag}_ici_wait", st["sems"], st["arrays"], st["plan"], after)

    def plan(refs, x, y, c):
        return [(landed(refs[k], 2 * ox + oy, c), landed(refs[k], 2 * ox + oy, c), (x, y, 1 - c),
                 landed(refs[k], 2 * ox + oy, 1 - c))
                for k in range(n) for ox, oy in _other_chips(x, y)]

    sems, lands, token = _copies_start(f"gather_{tag}_fwd_start", thru[n:], plan, 3 * n)
    return dict(tag=tag, n=n, plan=plan, sems=sems, arrays=lands, token=token, shards=thru[:n])


def _gather_end(st, after, merge=True):
    lands = _copies_wait(f"gather_{st['tag']}_fwd_wait", st["sems"], st["arrays"], st["plan"], after)
    if not merge:
        return lands, st["shards"]
    me = 2 * lax.axis_index("x") + lax.axis_index("y")
    return [lax.dynamic_update_index_in_dim(g, s, me, 0) for g, s in zip(lands, st["shards"])]


def _add_keep_give(name, pos, a, a_keep, a_give, b, b_keep, b_give, steps):
    r, c = b.shape[-2:]

    def spec(arr, fn):
        lead = arr.ndim - 2

        def index(i, p):
            idx = tuple(fn(i, p))
            return idx if len(idx) == arr.ndim else idx + (0, 0)

        return pl.BlockSpec((None,) * lead + (r, c), index)

    out_spec = pl.BlockSpec((None, r, c), lambda i, p: (i, 0, 0))

    def body(p_ref, ak_ref, bk_ref, ag_ref, bg_ref, keep_ref, give_ref):
        keep_ref[...] = ak_ref[...] + bk_ref[...].astype(F32)
        give_ref[...] = (ag_ref[...] + bg_ref[...].astype(F32)).astype(BF16)

    return pl.pallas_call(
        body, name=name,
        out_shape=(jax.ShapeDtypeStruct((steps, r, c), F32), jax.ShapeDtypeStruct((steps, r, c), BF16)),
        grid_spec=pltpu.PrefetchScalarGridSpec(
            num_scalar_prefetch=1, grid=(steps,),
            in_specs=[spec(a, a_keep), spec(b, b_keep), spec(a, a_give), spec(b, b_give)],
            out_specs=(out_spec, out_spec)),
        compiler_params=_cparams("parallel"),
    )(pos, a, b, a, b)


def _add_last(name, a, b):
    _, r, c = a.shape
    blk = pl.BlockSpec((None, r, c), lambda i: (0, 0, 0))

    def body(a_ref, b_ref, o_ref):
        o_ref[...] = a_ref[...] + b_ref[...].astype(F32)

    return pl.pallas_call(
        body, name=name, out_shape=jax.ShapeDtypeStruct((r, c), F32), grid=(1,), in_specs=[blk, blk],
        out_specs=pl.BlockSpec((r, c), lambda i: (0, 0)), compiler_params=_cparams("arbitrary"),
    )(a, b)


def _exchange_begin(tag, stage, gives, lands, peer_fn, extra):
    n = len(gives)

    def plan(refs, x, y, c):
        return [(refs[k], refs[n + k], peer_fn(x, y, c), refs[n + k]) for k in range(n)]

    sems, thru, token = _copies_start(f"rs{tag}_{stage}_start", gives + lands, plan, n)
    return dict(extra, tag=tag, n=n, stage=stage, plan=plan, sems=sems, arrays=thru, token=token)


def _reduce_begin(tag, grads, column_halves=False):
    n = len(grads)
    if column_halves:
        half = lambda ref, j, h: ref.at[j, :, pl.ds(pl.multiple_of(h * (ref.shape[2] // 2), LANES), ref.shape[2] // 2)]
        lands = [lax.empty((N_CHIPS, g.shape[1], g.shape[2] // 2), F32) for g in grads]
    else:
        half = lambda ref, j, h: ref.at[j, h]
        lands = [lax.empty((N_CHIPS,) + g.shape[2:], F32) for g in grads]

    def plan(refs, x, y, c):
        return [(half(refs[k], j, 1 - c), refs[n + k].at[j], (x, y, 1 - c), refs[n + k].at[j])
                for k in range(n) for j in range(N_CHIPS)]

    sems, thru, token = _copies_start(f"rs{tag}_c_start", list(grads) + lands, plan, N_CHIPS * n)
    return dict(tag=tag, n=n, stage="c", plan=plan, sems=sems, arrays=thru, token=token, column_halves=column_halves)


def _reduce_advance(st, after):
    tag, n, stage = st["tag"], st["n"], st["stage"]
    thru = _copies_wait(f"rs{tag}_{stage}_wait", st["sems"], st["arrays"], st["plan"], after)
    first, recv = thru[:n], thru[n:]
    x, y, c = _position()
    if stage == "c":
        pos = jnp.stack([c, x]).astype(jnp.int32)
        if st["column_halves"]:
            mine = lambda chip: (lambda i, p: (chip(p) + i, 0, p[0]))
        else:
            mine = lambda chip: (lambda i, p: (chip(p) + i, p[0]))
        sums = [_add_keep_give(
            f"rs{tag}_c_add{k}", pos,
            first[k], mine(lambda p: 2 * p[1]), mine(lambda p: 2 * (1 - p[1])),
            recv[k], lambda i, p: (2 * p[1] + i,), lambda i, p: (2 * (1 - p[1]) + i,), 2) for k in range(n)]
        lands = [lax.empty(s[1].shape, BF16) for s in sums]
        return _exchange_begin(tag, "x", [s[1] for s in sums], lands, lambda x, y, c: (1 - x, y, c),
                               dict(keep=[s[0] for s in sums]))
    if stage == "x":
        pos = jnp.stack([y]).astype(jnp.int32)
        sums = [_add_keep_give(
            f"rs{tag}_x_add{k}", pos,
            st["keep"][k], lambda i, p: (p[0],), lambda i, p: (1 - p[0],),
            recv[k], lambda i, p: (p[0],), lambda i, p: (1 - p[0],), 1) for k in range(n)]
        lands = [lax.empty(s[1].shape, BF16) for s in sums]
        return _exchange_begin(tag, "y", [s[1] for s in sums], lands, lambda x, y, c: (x, 1 - y, c),
                               dict(keep=[s[0] for s in sums]))
    if stage == "y":
        mine = [_add_last(f"rs{tag}_y_add{k}", st["keep"][k], recv[k]) for k in range(n)]
        lands = [lax.empty(m.shape, F32) for m in mine]
        return _exchange_begin(tag, "swap", mine, lands, lambda x, y, c: (x, y, 1 - c), {})
    return dict(done=list(zip(first, recv)), token=None)


def _small_begin(v, dep):
    land = lax.empty((N_DEV,) + v.shape, F32)
    flips = [(fx, fy, fc) for fx in (0, 1) for fy in (0, 1) for fc in (0, 1)][1:]

    def plan(refs, x, y, c):
        copies = []
        for fx, fy, fc in flips:
            px, py, pc = (1 - x if fx else x), (1 - y if fy else y), (1 - c if fc else c)
            copies.append((refs[0], refs[1].at[4 * x + 2 * y + c], (px, py, pc), refs[1].at[4 * px + 2 * py + pc]))
        return copies

    sems, thru, token = _copies_start("small_start", [v, land], plan, len(flips), dep=dep)
    return dict(plan=plan, sems=sems, arrays=thru, token=token)


def _small_end(st, after):
    own, land = _copies_wait("small_wait", st["sems"], st["arrays"], st["plan"], after)
    x, y, c = _position()
    me = jnp.stack([4 * x + 2 * y + c]).astype(jnp.int32)

    def body(me_ref, own_ref, land_ref, out_ref):
        term = lambda dev: jnp.where(me_ref[0] == dev, own_ref[...], land_ref[dev])
        acc = term(0)
        for dev in range(1, N_DEV):
            acc = acc + term(dev)
        out_ref[...] = acc

    return pl.pallas_call(
        body, name="small_sum", out_shape=jax.ShapeDtypeStruct(own.shape, F32),
        grid_spec=pltpu.PrefetchScalarGridSpec(
            num_scalar_prefetch=1, grid=(1,),
            in_specs=[pl.BlockSpec(own.shape, lambda i, m: (0, 0)), pl.BlockSpec(land.shape, lambda i, m: (0, 0, 0))],
            out_specs=pl.BlockSpec(own.shape, lambda i, m: (0, 0))),
        compiler_params=_cparams("arbitrary"),
    )(me, own, land)


def _all_reduce_small(v):
    r = v.shape[0]

    def body(v_ref, out_ref, buf, send_sems, recv_sems, local_sem):
        x, y, c = _position()
        me, sibling = (x, y, c), (x, y, 1 - c)
        chips = [(1 - x, y), (x, 1 - y), (1 - x, 1 - y)]

        def rows(px, py, pc):
            return buf.at[pl.ds((4 * px + 2 * py + pc) * r, r), :]

        def copy(k, block, to, src=None):
            return _remote(rows(*block) if src is None else src, rows(*block), send_sems.at[k], recv_sems.at[k], to)

        mine = pltpu.make_async_copy(v_ref, rows(*me), local_sem)
        mine.start()
        first = [copy(0, me, sibling, src=v_ref)]
        first += [copy(1 + j, me, (*chip, c), src=v_ref) for j, chip in enumerate(chips)]
        for cp in first:
            cp.start()
        passed = [copy(4 + j, (*chip, c), sibling) for j, chip in enumerate(chips)]
        for j, chip in enumerate(chips):
            copy(1 + j, (*chip, c), me).wait_recv()
            passed[j].start()
        copy(0, sibling, me).wait_recv()
        for j, chip in enumerate(chips):
            copy(4 + j, (*chip, 1 - c), me).wait_recv()
        for cp in first + passed:
            cp.wait_send()
        mine.wait()
        acc = buf[0:r, :]
        for dev in range(1, N_DEV):
            acc = acc + buf[dev * r:(dev + 1) * r, :]
        out_ref[...] = acc

    return pl.pallas_call(
        body, name="all_reduce_small",
        out_shape=jax.ShapeDtypeStruct(v.shape, F32),
        in_specs=[pl.BlockSpec(memory_space=pltpu.VMEM)],
        out_specs=pl.BlockSpec(memory_space=pltpu.VMEM),
        scratch_shapes=[pltpu.VMEM((N_DEV * r, LANES), F32), pltpu.SemaphoreType.DMA((7,)),
                        pltpu.SemaphoreType.DMA((7,)), pltpu.SemaphoreType.DMA],
        compiler_params=pltpu.CompilerParams(has_side_effects=True, vmem_limit_bytes=VMEM_LIMIT_V7X),
    )(v)


def _adamw_update(w, gg, m, v):
    mn = ADAM_B1 * m + (1.0 - ADAM_B1) * gg
    vn = ADAM_B2 * v + (1.0 - ADAM_B2) * (gg * gg)
    m_hat = mn / (1.0 - ADAM_B1 ** ADAM_STEP)
    v_hat = vn / (1.0 - ADAM_B2 ** ADAM_STEP)
    return -ADAM_LR * (m_hat / (jnp.sqrt(v_hat) + ADAM_EPS) + ADAM_WD * w), mn, vn


def _adamw(name, w, g, m, v):
    def body(w_ref, g_ref, m_ref, v_ref, d_ref, mo_ref, vo_ref):
        d_ref[...], mo_ref[...], vo_ref[...] = _adamw_update(w_ref[...], g_ref[...], m_ref[...], v_ref[...])

    blk = pl.BlockSpec(w.shape, lambda i: (0, 0))
    return pl.pallas_call(
        body, name=name, out_shape=(jax.ShapeDtypeStruct(w.shape, F32),) * 3, grid=(1,),
        in_specs=[blk] * 4, out_specs=(blk,) * 3, compiler_params=_cparams("arbitrary"),
    )(w, g, m, v)


def _rows_to_bf16(name, w):
    r, _, c = w.shape

    def body(w_ref, o_ref):
        o_ref[...] = w_ref[:, 0, :].astype(BF16)

    return pl.pallas_call(
        body, name=name, out_shape=jax.ShapeDtypeStruct((r, c), BF16), grid=(1,),
        in_specs=[pl.BlockSpec((r, 1, c), lambda i: (0, 0, 0))], out_specs=pl.BlockSpec((r, c), lambda i: (0, 0)),
        compiler_params=_cparams("arbitrary"),
    )(w)


def _adamw_rows(name, pos_c, w, g_mine, g_other, m, v):
    r, _, c = w.shape
    ch = c // 2

    def body(p_ref, w_ref, gm_ref, go_ref, m_ref, v_ref, g_ref, d_ref, mo_ref, vo_ref):
        gg = jnp.where(pl.program_id(0) == p_ref[0], gm_ref[...], go_ref[...])
        dl, mn, vn = _adamw_update(w_ref[:, 0, :], gg, m_ref[:, 0, :], v_ref[:, 0, :])
        g_ref[:, 0, :] = gg
        d_ref[:, 0, :] = dl
        mo_ref[:, 0, :] = mn
        vo_ref[:, 0, :] = vn

    rows = pl.BlockSpec((r, 1, ch), lambda h, p: (0, 0, h))
    half = pl.BlockSpec((r, ch), lambda h, p: (0, 0))
    return pl.pallas_call(
        body, name=name, out_shape=(jax.ShapeDtypeStruct(w.shape, F32),) * 4,
        grid_spec=pltpu.PrefetchScalarGridSpec(
            num_scalar_prefetch=1, grid=(2,), in_specs=[rows, half, half, rows, rows], out_specs=(rows,) * 4),
        compiler_params=_cparams("parallel"),
    )(pos_c, w, g_mine, g_other, m, v)


def _adamw_halves(name, pos_c, w, g_mine, g_other, m, v, tr, dep=None):
    r, c = w.shape
    rh = r // 2
    tr = tr if rh % tr == 0 else rh
    nt = rh // tr

    def body(p_ref, w_ref, gm_ref, go_ref, m_ref, v_ref, g_ref, d_ref, mo_ref, vo_ref):
        gg = jnp.where(pl.program_id(0) == p_ref[0], gm_ref[...], go_ref[...])
        g_ref[...] = gg
        d_ref[...], mo_ref[...], vo_ref[...] = _adamw_update(w_ref[...], gg, m_ref[...], v_ref[...])

    full = pl.BlockSpec((tr, c), lambda h, i, p: (h * nt + i, 0))
    half = pl.BlockSpec((tr, c), lambda h, i, p: (i, 0))
    dep_specs, dep_ops = _dep_args(dep)
    return pl.pallas_call(
        _after(body, 6, dep), name=name, out_shape=(jax.ShapeDtypeStruct((r, c), F32),) * 4,
        grid_spec=pltpu.PrefetchScalarGridSpec(
            num_scalar_prefetch=1, grid=(2, nt),
            in_specs=[full, half, half, full, full] + dep_specs, out_specs=(full,) * 4),
        compiler_params=_cparams("parallel", "parallel"),
    )(pos_c, w, g_mine, g_other, m, v, *dep_ops)


def _col_sharded_to_comm(g):
    k, n = g.shape
    return g.reshape(2, k // 2, N_CHIPS, n // N_CHIPS).transpose(2, 0, 1, 3)


def _row_sharded_to_comm(g):
    r, c = g.shape
    return g.reshape(N_CHIPS, 2, r // (2 * N_CHIPS), c)


def _col_sharded_full(g):
    _, _, rh, c = g.shape
    return g.reshape(N_CHIPS, 2 * rh, c).transpose(1, 0, 2).reshape(2 * rh, N_CHIPS * c)


def _row_sharded_full(g):
    _, _, rh, c = g.shape
    return g.reshape(N_CHIPS * 2 * rh, c)


def _chip_rows(w3, start, stop, own=None, me=None):
    r = w3.shape[1]
    parts = []
    for chip in range(N_CHIPS):
        lo, hi = max(start - chip * r, 0), min(stop - chip * r, r)
        if lo < hi:
            part = w3[chip, lo:hi]
            parts.append(part if own is None else jnp.where(me == chip, own[lo:hi], part))
    return parts[0] if len(parts) == 1 else jnp.concatenate(parts, axis=0)


def _pack_small(g1, bfv, mix, scale, g2n, gf, extra=None):
    row8 = jnp.pad(bfv.reshape(1, N_HEADS), ((0, 0), (0, LANES - N_HEADS)))
    if extra is not None:
        row8 = row8 + jnp.pad(extra[:, :1], ((0, 0), (N_HEADS, LANES - N_HEADS - 1)))
    return jnp.concatenate([
        g1.reshape(8, LANES), jnp.pad(row8, ((0, 7), (0, 0))), mix.reshape(512, LANES),
        jnp.pad(scale.reshape(4, LANES), ((0, 4), (0, 0))), g2n.reshape(8, LANES), gf.reshape(8, LANES)], axis=0)


def _unpack_small(s, like):
    g1, bfv, mix, scale, g2n, gf = like
    return (s[0:8].reshape(g1.shape), s[8, :N_HEADS].reshape(bfv.shape), s[16:528].reshape(mix.shape),
            s[528:532].reshape(scale.shape), s[536:544].reshape(g2n.shape), s[544:552].reshape(gf.shape))


class _MeshLinks:
    def __init__(self, shards_in, shards_rest):
        self.gin = _gather_begin("in", shards_in, None, column_halves=True)
        self.grest = _gather_begin("rest", shards_rest, self.gin["token"])
        self.tokens = {"gather": self.grest["token"]}
        self.groups = {}

    @property
    def token(self):
        return list(self.tokens.values())

    def tie(self, x):
        return _tie(x, self.token)

    def weights_in(self, after):
        st = _gather_forward(self.gin, after)
        (g,), (own,) = _gather_end(st, st["token"], merge=False)
        return g, own, 2 * lax.axis_index("x") + lax.axis_index("y")

    def rest_forward(self, after):
        self.grest = _gather_forward(self.grest, after)
        self.tokens["gather"] = self.grest["token"]

    def weights_rest(self, after):
        g = _gather_end(self.grest, after)
        del self.tokens["gather"]
        return [_col_sharded_full(g[0]), _col_sharded_full(g[1])] + [_row_sharded_full(a) for a in g[2:]]

    def reduce_begin(self, tag, grads, column_halves=False):
        self.groups[tag] = _reduce_begin(tag, grads, column_halves)
        self.tokens[tag] = self.groups[tag]["token"]

    def advance(self, after):
        for tag, st in self.groups.items():
            if "done" not in st:
                self.groups[tag] = _reduce_advance(st, after)
                if self.groups[tag]["token"] is None:
                    del self.tokens[tag]
                else:
                    self.tokens[tag] = self.groups[tag]["token"]

    def reduced(self, tag):
        return self.groups[tag]["done"]


class _NoLinks:
    token = None

    def __init__(self, w_in, rest):
        self.w_in, self.rest, self.grads = w_in, rest, {}

    def tie(self, x):
        return x

    def weights_in(self, after):
        return self.w_in, None, None

    def rest_forward(self, after):
        pass

    def weights_rest(self, after):
        return self.rest

    def reduce_begin(self, tag, grads, column_halves=False):
        self.grads[tag] = grads

    def advance(self, after):
        pass


def _local_step(links, x, target, seq, norm1_g, b_forget, pool_mix, pool_scale, norm2_g, norm_f_g):
    t, d = x.shape
    tq = min(256, seq)
    aw = ATTN_WIDTH
    o_q, o_f, o_g = POOL_WIDTH, POOL_WIDTH + 3 * aw, POOL_WIDTH + 3 * aw + N_HEADS
    bf = jnp.pad(b_forget, ((0, 0), (0, LANES - N_HEADS)))
    mixb = pool_mix.astype(BF16)

    h = _norm_fwd("norm1_fwd", x, links.tie(norm1_g), 512)
    w_in, own, me = links.weights_in(h)
    wu = _chip_rows(w_in, 0, o_q, own, me)
    wqkv = _chip_rows(w_in, o_q, o_f, own, me)
    wft = jnp.pad(_chip_rows(w_in, o_f, o_g, own, me), ((0, LANES - N_HEADS), (0, 0)))
    wg2 = _chip_rows(w_in, o_g, N_CHIPS * w_in.shape[1], own, me)
    wf = wft.T
    u = _matmul("mm_u", h, wu, "nt", F32, 1024, 512, d)
    g2 = _matmul("mm_gates", h, wg2, "nt", BF16, 1024, 1024, d)
    fl, fcum = _forget_fwd(h, wf, bf, seq)
    qa, ka, v = _attn_prep(h, _head_blocks(wqkv[:aw]), _head_blocks(wqkv[aw:2 * aw]), wqkv[2 * aw:], fcum, 1024)
    p, ps = _pool_fwd(u, mixb, pool_scale, seq)
    links.rest_forward([ps, qa, g2])
    o, lse = _attn_fwd(qa, ka, v, seq, tq, dep=links.token)
    w_pool_out, w_attn_out, w_out, w_ffn_gate, w_ffn_up, w_ffn_down = links.weights_rest(o)
    merged, x1 = _merge_fwd(x, ps, o, g2, w_pool_out, w_attn_out, w_out, 512)
    h2, gt, up, act, x2 = _ffn_fwd(x1, norm2_g, w_ffn_gate, w_ffn_up, w_ffn_down, 1024, 256)
    loss, dx2, d_gf = _final_fwd_bwd(x2, target, norm_f_g, 512)

    dgt, dup, dx1, d_g2n = _ffn_bwd(dx2, x1, norm2_g, gt, up, w_ffn_gate, w_ffn_up, w_ffn_down, 1024, 256)
    d_wd = _matmul("dw_down", act, dx2, "tn", F32, 1408, 1024, 1024)
    d_wg = _matmul("dw_gate", dgt, h2, "tn", F32, 1408, 1024, 1024)
    d_wu = _matmul("dw_up", dup, h2, "tn", F32, 1408, 1024, 1024)
    links.reduce_begin("a", [_row_sharded_to_comm(g) for g in (d_wg, d_wu, d_wd)])
    dpy, day, dg2, dps, da = _merge_bwd(dx1, ps, o, g2, w_pool_out, w_attn_out, w_out, 512, dep=links.token)
    links.advance(dps)
    d_wout = _matmul("dw_out", merged, dx1, "tn", F32, 1024, 1024, 1024)
    d_wpo = _matmul("dw_pool_out", ps, dpy, "tn", F32, 512, 1024, 1024)
    d_wao = _matmul("dw_attn_out", o, day, "tn", F32, 512, 1024, 1024)
    links.reduce_begin("m", [_col_sharded_to_comm(d_wpo), _col_sharded_to_comm(d_wao), _row_sharded_to_comm(d_wout)])
    dq, dk, dv, dfr = _attn_bwd(qa, ka, v, da, lse, seq, tq, dep=links.token)
    links.advance(dq)
    dfc = jnp.pad(dfr.reshape(N_HEADS, t).T, ((0, 0), (0, LANES - N_HEADS)))
    dfl, d_bf = _forget_bwd(dfc, fl, bf, seq)
    du, d_mix, d_scale = _pool_bwd(dps, p, mixb, links.tie(pool_scale), seq)
    d_wu_in = _matmul("dw_in_u", du, h, "tn", F32, 512, 1024, 1024)
    d_wq = _matmul("dw_in_q", dq, h, "tn", F32, 512, 1024, 1024)
    d_wk = _matmul("dw_in_k", dk, h, "tn", F32, 512, 1024, 1024)
    d_wv = _matmul("dw_in_v", dv, h, "tn", F32, 512, 1024, 1024)
    links.advance([d_wu_in, d_wq, d_wk, d_wv])
    d_wf = _matmul("dw_in_f", dfl, h, "tn", F32, LANES, 1024, 512)
    d_wg2 = _matmul("dw_in_gates", dg2, h, "tn", F32, 1024, 1024, 1024, dep=links.token)
    d_win = jnp.concatenate([d_wu_in, d_wq, d_wk, d_wv, d_wf[:N_HEADS], d_wg2], axis=0)
    comm_b = [d_win.reshape(N_CHIPS, d_win.shape[0] // N_CHIPS, d)]
    links.advance(comm_b)
    links.reduce_begin("b", comm_b, column_halves=True)
    dx, d_g1 = _in_bwd(du, dq, dk, dv, dg2, dfl, dx1, x, links.tie(norm1_g), wu, wqkv, wg2, wft, 512)
    links.advance(dx)
    small = (d_g1, d_bf[:, :N_HEADS], d_mix, d_scale, d_g2n, d_gf)
    return loss, dx, small


def kernel(x, norm1_g, w_in, b_forget, pool_mix, pool_scale, w_pool_out, w_attn_out, w_out, norm2_g, w_ffn_gate, w_ffn_up, w_ffn_down, norm_f_g, loss_target, m_norm1_g, m_w_in, m_b_forget, m_pool_mix, m_pool_scale, m_w_pool_out, m_w_attn_out, m_w_out, m_norm2_g, m_w_ffn_gate, m_w_ffn_up, m_w_ffn_down, m_norm_f_g, v_norm1_g, v_w_in, v_b_forget, v_pool_mix, v_pool_scale, v_w_pool_out, v_w_attn_out, v_w_out, v_norm2_g, v_w_ffn_gate, v_w_ffn_up, v_w_ffn_down, v_norm_f_g):
    nb, seq, d = x.shape
    group_a = ((w_ffn_gate, m_w_ffn_gate, v_w_ffn_gate, True, 9), (w_ffn_up, m_w_ffn_up, v_w_ffn_up, True, 10),
               (w_ffn_down, m_w_ffn_down, v_w_ffn_down, False, 11))
    group_m = ((w_pool_out, m_w_pool_out, v_w_pool_out, False, 5), (w_attn_out, m_w_attn_out, v_w_attn_out, False, 6),
               (w_out, m_w_out, v_w_out, False, 7))
    group_b = ((w_in, m_w_in, v_w_in, False, 1),)
    small_w = (norm1_g, b_forget, pool_mix, pool_scale, norm2_g, norm_f_g)
    small_m = (m_norm1_g, m_b_forget, m_pool_mix, m_pool_scale, m_norm2_g, m_norm_f_g)
    small_v = (v_norm1_g, v_b_forget, v_pool_mix, v_pool_scale, v_norm2_g, v_norm_f_g)
    small_pos = (0, 2, 3, 4, 8, 12)
    view = lambda a, tr: a[0].T if tr else a[0]
    unview = lambda a, tr, like: (a.T if tr else a).reshape(like.shape)

    def shard(w, tr):
        lw = view(w, tr).astype(BF16)
        return lw.reshape(2, lw.shape[0] // 2, lw.shape[1])

    cm = lambda a: jnp.transpose(a, (2, 0, 1))
    shard_in = _rows_to_bf16("w_in_to_bf16", cm(w_in))
    links = _MeshLinks([shard_in],
                       [shard(w_pool_out, False), shard(w_attn_out, False), shard(w_out, False),
                        shard(w_ffn_gate, True), shard(w_ffn_up, True), shard(w_ffn_down, False)])
    loss, dx, small_g = _local_step(
        links, x.reshape(nb * seq, d), loss_target.reshape(nb * seq, d), seq,
        norm1_g, b_forget, pool_mix[0], pool_scale, norm2_g, norm_f_g.reshape(1, d))

    grads, deltas, new_m, new_v = [None] * 13, [None] * 13, [None] * 13, [None] * 13
    pos_c = jnp.stack([lax.axis_index("c")]).astype(jnp.int32)

    def update(tag, group, dep):
        last = []
        for k, ((w, m, v, tr, pos), (mine, other)) in enumerate(zip(group, links.reduced(tag))):
            outs = _adamw_halves(f"adamw_{tag}{k}", pos_c, view(w, tr), mine, other, view(m, tr), view(v, tr), 256,
                                 dep=dep)
            grads[pos], deltas[pos], new_m[pos], new_v[pos] = (unview(a, tr, w) for a in outs)
            last.append(outs[1])
        return last

    small_state = _small_begin(_pack_small(*small_g, extra=loss), links.token)
    links.tokens["small"] = small_state["token"]
    last = update("a", group_a, links.token) + update("m", group_m, links.token)
    links.advance(last)
    del links.tokens["small"]
    small_sum = _small_end(small_state, last)
    loss_out = small_sum[8, N_HEADS]
    dl, mn, vn = _adamw("adamw_small", _pack_small(*small_w), small_sum * _small_mask(), _pack_small(*small_m),
                        _pack_small(*small_v))
    for pos, g, a, b, e in zip(small_pos, _unpack_small(small_sum, small_w), _unpack_small(dl, small_w),
                               _unpack_small(mn, small_w), _unpack_small(vn, small_w)):
        grads[pos], deltas[pos], new_m[pos], new_v[pos] = g, a, b, e
    links.advance(dl)
    links.advance(links.token)
    (mine, other), = links.reduced("b")
    outs = _adamw_rows("adamw_b0", pos_c, cm(w_in), mine, other, cm(m_w_in), cm(v_w_in))
    grads[1], deltas[1], new_m[1], new_v[1] = (jnp.transpose(a, (1, 2, 0)) for a in outs)

    return (loss_out, dx.reshape(nb, seq, d), *grads, *deltas, *new_m, *new_v)


def _small_mask():
    rows = lax.broadcasted_iota(jnp.int32, (552, LANES), 0)
    lanes = lax.broadcasted_iota(jnp.int32, (552, LANES), 1)
    return jnp.where(jnp.logical_and(rows == 8, lanes == N_HEADS), 0.0, 1.0).astype(F32)
```

```python
import functools

import jax
import jax.numpy as jnp
from jax import lax
from jax.experimental import pallas as pl
from jax.experimental.pallas import tpu as pltpu

F32 = jnp.float32
BF16 = jnp.bfloat16

D_MODEL = 1024
POOL_WINDOWS = (2, 4, 8, 16)
POOL_GROUPS = 4
POOL_GROUP_DIM = 128
POOL_WIDTH = 512
HEAD_DIM = 64
N_HEADS = 8
ATTN_WIDTH = 512
D_FF = 2816
RMS_EPS = 1e-6
ATTN_SCALE = HEAD_DIM ** -0.5
NEG_BIG = -1e30

ADAM_LR = 0.001
ADAM_B1 = 0.9
ADAM_B2 = 0.999
ADAM_EPS = 1e-08
ADAM_WD = 0.01
ADAM_STEP = 10

LANES = 128
N_CHIPS = 4
N_DEV = 8
VMEM_LIMIT_V7X = 52 * 1024 * 1024
ROW_CHUNK = 256
MESH = pl.DeviceIdType.MESH
ANY = pl.BlockSpec(memory_space=pl.ANY)


def _cparams(*sem):
    return pltpu.CompilerParams(dimension_semantics=sem if sem else None, vmem_limit_bytes=VMEM_LIMIT_V7X)


def _dep_list(dep):
    return [] if dep is None else (list(dep) if isinstance(dep, (list, tuple)) else [dep])


def _after(body, n_in, dep):
    k = len(_dep_list(dep))
    if k == 0:
        return body

    def wrapped(*refs):
        body(*refs[:n_in], *refs[n_in + k:])

    return wrapped


def _dep_args(dep):
    deps = _dep_list(dep)
    return [ANY] * len(deps), deps


def _dot(a, b):
    return lax.dot_general(a, b, (((1,), (0,)), ((), ())), preferred_element_type=F32)


def _dot_nt(a, b):
    return lax.dot_general(a, b, (((1,), (1,)), ((), ())), preferred_element_type=F32)


def _dot_tn(a, b):
    return lax.dot_general(a, b, (((0,), (0,)), ((), ())), preferred_element_type=F32)


def _sigmoid(x):
    return jax.nn.sigmoid(x)


def _rms_fwd(x, g):
    r = lax.rsqrt(jnp.mean(x * x, axis=-1, keepdims=True) + RMS_EPS)
    return (x * r) * g


def _rms_bwd(x, g, dy):
    r = lax.rsqrt(jnp.mean(x * x, axis=-1, keepdims=True) + RMS_EPS)
    xh = x * r
    dg = jnp.sum(dy * xh, axis=0, keepdims=True)
    dxh = dy * g
    dx = r * (dxh - xh * jnp.mean(dxh * xh, axis=-1, keepdims=True))
    return dx, dg


def _matmul(name, a, b, mode, out_dtype, tm, tn, tk, dep=None):
    if mode == "nn":
        (m, k), (_, n) = a.shape, b.shape
    elif mode == "nt":
        (m, k), (n, _) = a.shape, b.shape
    else:
        (k, m), (_, n) = a.shape, b.shape
    tm, tn, tk = min(tm, m), min(tn, n), min(tk, k)
    assert m % tm == 0 and n % tn == 0 and k % tk == 0, (name, m, n, k, tm, tn, tk)
    nk = k // tk
    if mode == "tn":
        a_spec = pl.BlockSpec((tk, tm), lambda i, j, kk: (kk, i))
    else:
        a_spec = pl.BlockSpec((tm, tk), lambda i, j, kk: (i, kk))
    if mode == "nt":
        b_spec = pl.BlockSpec((tn, tk), lambda i, j, kk: (j, kk))
    else:
        b_spec = pl.BlockSpec((tk, tn), lambda i, j, kk: (kk, j))
    dot = {"nn": _dot, "nt": _dot_nt, "tn": _dot_tn}[mode]
    use_scratch = nk > 1 and out_dtype != F32

    def body(a_ref, b_ref, o_ref, *scratch):
        if nk == 1 and mode != "tn":
            rows = min(ROW_CHUNK, tm)
            bb = b_ref[...].astype(BF16)
            for r0 in range(0, tm, rows):
                o_ref[r0:r0 + rows, :] = dot(a_ref[r0:r0 + rows, :].astype(BF16), bb).astype(out_dtype)
            return
        prod = dot(a_ref[...].astype(BF16), b_ref[...].astype(BF16))
        if nk == 1:
            o_ref[...] = prod.astype(out_dtype)
            return
        acc = scratch[0] if use_scratch else o_ref
        kk = pl.program_id(2)

        @pl.when(kk == 0)
        def _():
            acc[...] = prod

        @pl.when(kk > 0)
        def _():
            acc[...] += prod

        if use_scratch:
            @pl.when(kk == nk - 1)
            def _():
                o_ref[...] = acc[...].astype(out_dtype)

    dep_specs, dep_ops = _dep_args(dep)
    return pl.pallas_call(
        _after(body, 2, dep),
        name=name,
        out_shape=jax.ShapeDtypeStruct((m, n), out_dtype),
        grid=(m // tm, n // tn, nk),
        in_specs=[a_spec, b_spec] + dep_specs,
        out_specs=pl.BlockSpec((tm, tn), lambda i, j, kk: (i, j)),
        scratch_shapes=[pltpu.VMEM((tm, tn), F32)] if use_scratch else [],
        compiler_params=_cparams("parallel", "parallel", "arbitrary"),
    )(a, b, *dep_ops)


def _norm_fwd(name, x, g, tm):
    t, d = x.shape
    tm = min(tm, t)

    def body(x_ref, g_ref, h_ref):
        h_ref[...] = _rms_fwd(x_ref[...], g_ref[...]).astype(BF16)

    return pl.pallas_call(
        body, name=name, out_shape=jax.ShapeDtypeStruct((t, d), BF16), grid=(t // tm,),
        in_specs=[pl.BlockSpec((tm, d), lambda i: (i, 0)), pl.BlockSpec((1, d), lambda i: (0, 0))],
        out_specs=pl.BlockSpec((tm, d), lambda i: (i, 0)),
        compiler_params=_cparams("parallel"),
    )(x, g)


def _split3(x):
    hi = x.astype(BF16)
    r1 = x - hi.astype(F32)
    mid = r1.astype(BF16)
    lo = (r1 - mid.astype(F32)).astype(BF16)
    return hi, mid, lo


def _tri_dot(tri, x):
    hi, mid, lo = _split3(x)
    return _dot(tri, hi) + _dot(tri, mid) + _dot(tri, lo)


def _forget_fwd(h, wf, bf, seq):
    t, d = h.shape
    cb = min(256, seq)

    def body(h_ref, wf_ref, bf_ref, fl_ref, fc_ref):
        fl = _dot(h_ref[...], wf_ref[...])
        fl_ref[...] = fl
        xx = fl + bf_ref[...]
        lf = jnp.minimum(xx, 0.0) - jnp.log(1.0 + jnp.exp(-jnp.abs(xx)))
        ri = lax.broadcasted_iota(jnp.int32, (cb, cb), 0)
        ci = lax.broadcasted_iota(jnp.int32, (cb, cb), 1)
        tri = (ri >= ci).astype(BF16)
        carry = jnp.zeros((1, LANES), F32)
        for blk in range(seq // cb):
            cs = _tri_dot(tri, lf[blk * cb:(blk + 1) * cb]) + carry
            fc_ref[blk * cb:(blk + 1) * cb, :] = cs
            carry = cs[cb - 1:cb, :]

    return pl.pallas_call(
        body, name="forget_fwd",
        out_shape=(jax.ShapeDtypeStruct((t, LANES), F32), jax.ShapeDtypeStruct((t, LANES), F32)),
        grid=(t // seq,),
        in_specs=[pl.BlockSpec((seq, d), lambda b: (b, 0)), pl.BlockSpec((d, LANES), lambda b: (0, 0)),
                  pl.BlockSpec((1, LANES), lambda b: (0, 0))],
        out_specs=(pl.BlockSpec((seq, LANES), lambda b: (b, 0)), pl.BlockSpec((seq, LANES), lambda b: (b, 0))),
        compiler_params=_cparams("parallel"),
    )(h, wf, bf)


def _pool_fwd(u, mix, scale, seq):
    t = u.shape[0]

    def body(u_ref, mix_ref, sc_ref, p_ref, ps_ref):
        tpos = lax.broadcasted_iota(jnp.int32, (seq, POOL_GROUP_DIM), 0)
        for g in range(POOL_GROUPS):
            sl = slice(g * POOL_GROUP_DIM, (g + 1) * POOL_GROUP_DIM)
            ug = u_ref[:, sl]
            s = ug
            for lvl in range(g + 1):
                d = 2 ** lvl
                s = s + jnp.where(tpos >= d, pltpu.roll(s, d, 0), 0.0)
            cnt = jnp.minimum(tpos + 1, POOL_WINDOWS[g]).astype(F32)
            pb = (s / cnt - ug).astype(BF16)
            p_ref[:, sl] = pb
            ps_ref[:, sl] = (_dot(pb, mix_ref[g]) * sc_ref[:, sl]).astype(BF16)

    return pl.pallas_call(
        body, name="pool_fwd",
        out_shape=(jax.ShapeDtypeStruct((t, POOL_WIDTH), BF16), jax.ShapeDtypeStruct((t, POOL_WIDTH), BF16)),
        grid=(t // seq,),
        in_specs=[pl.BlockSpec((seq, POOL_WIDTH), lambda b: (b, 0)),
                  pl.BlockSpec((POOL_GROUPS, POOL_GROUP_DIM, POOL_GROUP_DIM), lambda b: (0, 0, 0)),
                  pl.BlockSpec((1, POOL_WIDTH), lambda b: (0, 0))],
        out_specs=(pl.BlockSpec((seq, POOL_WIDTH), lambda b: (b, 0)), pl.BlockSpec((seq, POOL_WIDTH), lambda b: (b, 0))),
        compiler_params=_cparams("parallel"),
    )(u, mix, scale)


def _aug_constants():
    w = N_HEADS * LANES
    rows = jnp.arange(3 * LANES)
    piece, head = rows // LANES, rows % LANES
    cols = jnp.arange(w)
    live = (head < N_HEADS)[:, None]
    pq = (live & (cols[None, :] == (head * LANES + HEAD_DIM + piece)[:, None])).astype(BF16)
    pk = -(live & (cols[None, :] == (head * LANES + HEAD_DIM + 3 + piece)[:, None])).astype(BF16)
    lane = cols % LANES
    oq = ((lane >= HEAD_DIM + 3) & (lane < HEAD_DIM + 6)).astype(F32)[None, :]
    ok = ((lane >= HEAD_DIM) & (lane < HEAD_DIM + 3)).astype(F32)[None, :]
    return pq, pk, oq, ok


def _head_blocks(wt):
    d = wt.shape[1]
    return jnp.pad(wt.reshape(N_HEADS, HEAD_DIM, d), ((0, 0), (0, LANES - HEAD_DIM), (0, 0))).reshape(N_HEADS * LANES, d)


def _attn_prep(h, wq, wk, wv, fcum, tm):
    t, d = h.shape
    tm = min(tm, t)
    rows = min(ROW_CHUNK, tm)
    w = N_HEADS * LANES
    pq, pk, oq, ok = _aug_constants()

    def body(h_ref, wq_ref, wk_ref, wv_ref, f_ref, pq_ref, pk_ref, oq_ref, ok_ref, qa_ref, ka_ref, v_ref):
        for r0 in range(0, tm, rows):
            rs = slice(r0, r0 + rows)
            hh = h_ref[rs, :]
            fs = jnp.concatenate(_split3(f_ref[rs, :]), axis=1)
            q = _dot_nt(hh, wq_ref[...]).astype(BF16).astype(F32) * ATTN_SCALE
            qa_ref[rs, :] = (q + _dot(fs, pq_ref[...]) + oq_ref[...]).astype(BF16)
            k = _dot_nt(hh, wk_ref[...]).astype(BF16).astype(F32)
            ka_ref[rs, :] = (k + _dot(fs, pk_ref[...]) + ok_ref[...]).astype(BF16)
            v_ref[rs, :] = _dot_nt(hh, wv_ref[...]).astype(BF16)

    row = lambda n: pl.BlockSpec((tm, n), lambda i: (i, 0))
    full = lambda a: pl.BlockSpec(a.shape, lambda i: (0, 0))
    return pl.pallas_call(
        body, name="attn_prep",
        out_shape=(jax.ShapeDtypeStruct((t, w), BF16), jax.ShapeDtypeStruct((t, w), BF16),
                   jax.ShapeDtypeStruct((t, ATTN_WIDTH), BF16)),
        grid=(t // tm,),
        in_specs=[row(d), full(wq), full(wk), full(wv), row(LANES), full(pq), full(pk), full(oq), full(ok)],
        out_specs=(row(w), row(w), row(ATTN_WIDTH)),
        compiler_params=_cparams("parallel"),
    )(h, wq, wk, wv, fcum, pq, pk, oq, ok)


def _fold_lanes(x, op):
    out = x[:, :LANES]
    for g in range(1, x.shape[1] // LANES):
        out = op(out, x[:, g * LANES:(g + 1) * LANES])
    return out


def _causal_sweep(i, tile, carry):
    def quad(jj, c):
        for u in range(4):
            c = tile(4 * jj + u, c, False)
        return c

    carry = lax.fori_loop(0, i // 4, quad, carry)
    base = 4 * (i // 4)
    carry = lax.cond(i % 4 >= 2, lambda c: tile(base + 1, tile(base, c, False), False), lambda c: c, carry)
    return lax.cond(i % 2 == 1, lambda c: tile(i, tile(i - 1, c, False), True), lambda c: tile(i, c, True), carry)


def _attn_fwd(qa, ka, v, seq, tq, dep=None):
    t = qa.shape[0]
    nq = seq // tq
    hp_n = N_HEADS // 2
    heads = [slice(e * LANES, (e + 1) * LANES) for e in range(2)]

    def body(q_ref, k_ref, v_ref, o_ref, lse_ref, s_buf):
        i = pl.program_id(2)
        diag_ok = lax.broadcasted_iota(jnp.int32, (tq, tq), 0) >= lax.broadcasted_iota(jnp.int32, (tq, tq), 1)
        qs = [q_ref[:, hl] for hl in heads]

        def sweep1(j, mxs, diagonal):
            r0 = pl.multiple_of(j * tq, tq)
            out = []
            for e, hl in enumerate(heads):
                s = _dot_nt(qs[e], k_ref[pl.ds(r0, tq), hl])
                if diagonal:
                    s = jnp.where(diag_ok, s, NEG_BIG)
                s_buf[e, j] = s
                out.append(jnp.maximum(mxs[e], _fold_lanes(s, jnp.maximum)))
            return tuple(out)

        mxs = _causal_sweep(i, sweep1, (jnp.full((tq, LANES), NEG_BIG, F32),) * 2)
        ms = [jnp.max(mx, axis=1, keepdims=True) for mx in mxs]

        def sweep2(j, carry, diagonal):
            r0 = pl.multiple_of(j * tq, tq)
            vv = v_ref[pl.ds(r0, tq), :]
            out = []
            for e in range(2):
                p = jnp.exp(s_buf[e, j] - ms[e])
                out += [carry[2 * e] + _fold_lanes(p, jnp.add), carry[2 * e + 1] + _dot(p.astype(BF16), vv)]
            return tuple(out)

        res = _causal_sweep(i, sweep2, (jnp.zeros((tq, LANES), F32),) * 4)
        outs = []
        for e in range(2):
            l = jnp.sum(res[2 * e], axis=1, keepdims=True)
            outs.append(res[2 * e + 1] / l)
            lse_ref[:, e:e + 1] = ms[e] + jnp.log(l)
        lane = lax.broadcasted_iota(jnp.int32, (tq, LANES), 1)
        o_ref[...] = jnp.where(lane < HEAD_DIM, outs[0], outs[1])

    dep_specs, dep_ops = _dep_args(dep)
    return pl.pallas_call(
        _after(body, 3, dep), name="attn_fwd",
        out_shape=(jax.ShapeDtypeStruct((t, ATTN_WIDTH), F32), jax.ShapeDtypeStruct((hp_n, t, 2), F32)),
        grid=(t // seq, hp_n, nq),
        in_specs=[pl.BlockSpec((tq, 2 * LANES), lambda b, hp, i: (b * nq + i, hp)),
                  pl.BlockSpec((seq, 2 * LANES), lambda b, hp, i: (b, hp)),
                  pl.BlockSpec((seq, LANES), lambda b, hp, i: (b, hp))] + dep_specs,
        out_specs=(pl.BlockSpec((tq, LANES), lambda b, hp, i: (b * nq + i, hp)),
                   pl.BlockSpec((None, tq, 2), lambda b, hp, i: (hp, b * nq + i, 0))),
        scratch_shapes=[pltpu.VMEM((2, nq, tq, tq), F32)],
        compiler_params=_cparams("parallel", "parallel", "arbitrary"),
    )(qa, ka, v, *dep_ops)


def _merge_fwd(x, ps, o, g2, wpo, wao, wout, tm):
    t, d = x.shape
    tm = min(tm, t)
    rows = min(ROW_CHUNK, tm)

    def body(x_ref, ps_ref, o_ref, gp_ref, ga_ref, wpo_ref, wao_ref, wout_ref, mg_ref, x1_ref):
        for r0 in range(0, tm, rows):
            rs = slice(r0, r0 + rows)
            py = _dot(ps_ref[rs, :], wpo_ref[...])
            ay = _dot(o_ref[rs, :].astype(BF16), wao_ref[...])
            mb = (_sigmoid(gp_ref[rs, :].astype(F32)) * py + _sigmoid(ga_ref[rs, :].astype(F32)) * ay).astype(BF16)
            mg_ref[rs, :] = mb
            x1_ref[rs, :] = x_ref[rs, :] + _dot(mb, wout_ref[...])

    row = lambda w: pl.BlockSpec((tm, w), lambda i: (i, 0))
    full = lambda a: pl.BlockSpec(a.shape, lambda i: (0, 0))
    return pl.pallas_call(
        body, name="merge_fwd",
        out_shape=(jax.ShapeDtypeStruct((t, d), BF16), jax.ShapeDtypeStruct((t, d), F32)),
        grid=(t // tm,),
        in_specs=[row(d), row(POOL_WIDTH), row(ATTN_WIDTH), pl.BlockSpec((tm, d), lambda i: (i, 0)),
                  pl.BlockSpec((tm, d), lambda i: (i, 1)), full(wpo), full(wao), full(wout)],
        out_specs=(row(d), row(d)),
        compiler_params=_cparams("parallel"),
    )(x, ps, o, g2, g2, wpo, wao, wout)


def _ffn_fwd(x1, g, wg, wu, wd, tm, tf):
    t, d = x1.shape
    f = wg.shape[0]
    tm = min(tm, t)
    nf = f // tf
    rows = min(512, tm)

    def body(x1_ref, g_ref, wg_ref, wu_ref, wd_ref, h2_ref, gt_ref, up_ref, act_ref, x2_ref):
        j = pl.program_id(1)

        @pl.when(j == 0)
        def _():
            h2_ref[...] = _rms_fwd(x1_ref[...], g_ref[...]).astype(BF16)

            x2_ref[...] = x1_ref[...]

        for r0 in range(0, tm, rows):
            rs = slice(r0, r0 + rows)
            h2 = h2_ref[rs, :]
            gt = _dot_nt(h2, wg_ref[...])
            up = _dot_nt(h2, wu_ref[...])
            sg = _sigmoid(gt)
            silu = gt * sg
            act = (silu * up).astype(BF16)
            gt_ref[rs, :] = (up * (sg * (1.0 + gt * (1.0 - sg)))).astype(BF16)
            up_ref[rs, :] = silu.astype(BF16)
            act_ref[rs, :] = act
            x2_ref[rs, :] += _dot(act, wd_ref[...])

    return pl.pallas_call(
        body, name="ffn_fwd",
        out_shape=(jax.ShapeDtypeStruct((t, d), BF16), jax.ShapeDtypeStruct((t, f), BF16),
                   jax.ShapeDtypeStruct((t, f), BF16), jax.ShapeDtypeStruct((t, f), BF16),
                   jax.ShapeDtypeStruct((t, d), F32)),
        grid=(t // tm, nf),
        in_specs=[pl.BlockSpec((tm, d), lambda i, j: (i, 0)), pl.BlockSpec((1, d), lambda i, j: (0, 0)),
                  pl.BlockSpec((tf, d), lambda i, j: (j, 0)), pl.BlockSpec((tf, d), lambda i, j: (j, 0)),
                  pl.BlockSpec((tf, d), lambda i, j: (j, 0))],
        out_specs=(pl.BlockSpec((tm, d), lambda i, j: (i, 0)), pl.BlockSpec((tm, tf), lambda i, j: (i, j)),
                   pl.BlockSpec((tm, tf), lambda i, j: (i, j)), pl.BlockSpec((tm, tf), lambda i, j: (i, j)),
                   pl.BlockSpec((tm, d), lambda i, j: (i, 0))),
        compiler_params=_cparams("parallel", "arbitrary"),
    )(x1, g, wg, wu, wd)


def _final_fwd_bwd(x2, target, g, tm):
    t, d = x2.shape
    tm = min(tm, t)

    def body(x_ref, t_ref, g_ref, loss_ref, dx_ref, dg_ref):
        i = pl.program_id(0)
        x = x_ref[...]
        gg = g_ref[...]
        err = _rms_fwd(x, gg) - t_ref[...]
        part = 0.5 * jnp.sum(jnp.mean(err * err, axis=-1, keepdims=True), axis=0, keepdims=True)
        dx, dg = _rms_bwd(x, gg, err * (1.0 / d))
        dx_ref[...] = dx

        @pl.when(i == 0)
        def _():
            loss_ref[...] = jnp.zeros_like(loss_ref)
            dg_ref[...] = jnp.zeros_like(dg_ref)

        loss_ref[...] += jnp.broadcast_to(part, loss_ref.shape)
        dg_ref[...] += dg

    return pl.pallas_call(
        body, name="final_fwd_bwd",
        out_shape=(jax.ShapeDtypeStruct((1, LANES), F32), jax.ShapeDtypeStruct((t, d), F32),
                   jax.ShapeDtypeStruct((1, d), F32)),
        grid=(t // tm,),
        in_specs=[pl.BlockSpec((tm, d), lambda i: (i, 0)), pl.BlockSpec((tm, d), lambda i: (i, 0)),
                  pl.BlockSpec((1, d), lambda i: (0, 0))],
        out_specs=(pl.BlockSpec((1, LANES), lambda i: (0, 0)), pl.BlockSpec((tm, d), lambda i: (i, 0)),
                   pl.BlockSpec((1, d), lambda i: (0, 0))),
        compiler_params=_cparams("arbitrary"),
    )(x2, target, g)


def _ffn_bwd(dx2, x1, g, gt, up, wg, wu, wd, tm, tf):
    t, d = dx2.shape
    f = gt.shape[1]
    tm = min(tm, t)
    nf = f // tf
    wgu = jnp.concatenate([wg.reshape(nf, tf, d), wu.reshape(nf, tf, d)], axis=1).reshape(2 * f, d)
    rows = min(256, tm)

    def body(dx2_ref, x1_ref, g_ref, gt_ref, up_ref, wgu_ref, wd_ref, dgt_ref, dup_ref, dx1_ref, dg_ref, acc_ref,
             dxb_ref):
        i, j = pl.program_id(0), pl.program_id(1)

        @pl.when(j == 0)
        def _():
            dxb_ref[...] = dx2_ref[...].astype(BF16)
            acc_ref[...] = jnp.zeros_like(acc_ref)

        for r0 in range(0, tm, rows):
            rs = slice(r0, r0 + rows)
            dact = _dot_nt(dxb_ref[rs, :], wd_ref[...])
            dgt = (dact * gt_ref[rs, :].astype(F32)).astype(BF16)
            dup = (dact * up_ref[rs, :].astype(F32)).astype(BF16)
            dgt_ref[rs, :] = dgt
            dup_ref[rs, :] = dup
            acc_ref[rs, :] += _dot(jnp.concatenate([dgt, dup], axis=1), wgu_ref[...])

        @pl.when(jnp.logical_and(i == 0, j == 0))
        def _():
            dg_ref[...] = jnp.zeros_like(dg_ref)

        @pl.when(j == nf - 1)
        def _():
            dxn, dg = _rms_bwd(x1_ref[...], g_ref[...], acc_ref[...])
            dx1_ref[...] = dx2_ref[...] + dxn
            dg_ref[...] += dg

    return pl.pallas_call(
        body, name="ffn_bwd",
        out_shape=(jax.ShapeDtypeStruct((t, f), BF16), jax.ShapeDtypeStruct((t, f), BF16),
                   jax.ShapeDtypeStruct((t, d), F32), jax.ShapeDtypeStruct((1, d), F32)),
        grid=(t // tm, nf),
        in_specs=[pl.BlockSpec((tm, d), lambda i, j: (i, 0)), pl.BlockSpec((tm, d), lambda i, j: (i, 0)),
                  pl.BlockSpec((1, d), lambda i, j: (0, 0)),
                  pl.BlockSpec((tm, tf), lambda i, j: (i, j)), pl.BlockSpec((tm, tf), lambda i, j: (i, j)),
                  pl.BlockSpec((2 * tf, d), lambda i, j: (j, 0)), pl.BlockSpec((tf, d), lambda i, j: (j, 0))],
        out_specs=(pl.BlockSpec((tm, tf), lambda i, j: (i, j)), pl.BlockSpec((tm, tf), lambda i, j: (i, j)),
                   pl.BlockSpec((tm, d), lambda i, j: (i, 0)), pl.BlockSpec((1, d), lambda i, j: (0, 0))),
        scratch_shapes=[pltpu.VMEM((tm, d), F32), pltpu.VMEM((tm, d), BF16)],
        compiler_params=_cparams("arbitrary", "arbitrary"),
    )(dx2, x1, g, gt, up, wgu, wd)


def _merge_bwd(dx1, ps, o, g2, wpo, wao, wout, tm, dep=None):
    t, d = dx1.shape
    tm = min(tm, t)
    rows = min(ROW_CHUNK, tm)

    def body(dx1_ref, ps_ref, o_ref, gp_ref, ga_ref, wpo_ref, wao_ref, wout_ref, dpy_ref, day_ref, dg2_ref, dps_ref, da_ref):
        for r0 in range(0, tm, rows):
            rs = slice(r0, r0 + rows)
            dm = _dot_nt(dx1_ref[rs, :].astype(BF16), wout_ref[...])
            py = _dot(ps_ref[rs, :], wpo_ref[...])
            ay = _dot(o_ref[rs, :].astype(BF16), wao_ref[...])
            sp = _sigmoid(gp_ref[rs, :].astype(F32))
            sa = _sigmoid(ga_ref[rs, :].astype(F32))
            dpy = (dm * sp).astype(BF16)
            day = (dm * sa).astype(BF16)
            dpy_ref[rs, :] = dpy
            day_ref[rs, :] = day
            dg2_ref[rs, :d] = (dm * py * (sp * (1.0 - sp))).astype(BF16)
            dg2_ref[rs, d:] = (dm * ay * (sa * (1.0 - sa))).astype(BF16)
            dps_ref[rs, :] = _dot_nt(dpy, wpo_ref[...])
            da_ref[rs, :] = _dot_nt(day, wao_ref[...]).astype(BF16)

    row = lambda w: pl.BlockSpec((tm, w), lambda i: (i, 0))
    full = lambda a: pl.BlockSpec(a.shape, lambda i: (0, 0))
    dep_specs, dep_ops = _dep_args(dep)
    return pl.pallas_call(
        _after(body, 8, dep), name="merge_bwd",
        out_shape=(jax.ShapeDtypeStruct((t, d), BF16), jax.ShapeDtypeStruct((t, d), BF16),
                   jax.ShapeDtypeStruct((t, 2 * d), BF16), jax.ShapeDtypeStruct((t, POOL_WIDTH), F32),
                   jax.ShapeDtypeStruct((t, ATTN_WIDTH), BF16)),
        grid=(t // tm,),
        in_specs=[row(d), row(POOL_WIDTH), row(ATTN_WIDTH), pl.BlockSpec((tm, d), lambda i: (i, 0)),
                  pl.BlockSpec((tm, d), lambda i: (i, 1)), full(wpo), full(wao), full(wout)] + dep_specs,
        out_specs=(row(d), row(d), row(2 * d), row(POOL_WIDTH), row(ATTN_WIDTH)),
        compiler_params=_cparams("parallel"),
    )(dx1, ps, o, g2, g2, wpo, wao, wout, *dep_ops)


def _attn_bwd(qa, ka, v, do, lse4, seq, tq, dep=None):
    t = qa.shape[0]
    nq = seq // tq
    hp_n = N_HEADS // 2
    heads = [slice(e * LANES, (e + 1) * LANES) for e in range(2)]

    def body(q_ref, k_ref, v_ref, do_ref, lse_ref, dq_ref, dk_ref, dv_ref, dfr_ref, dk_acc, dv_acc, p_buf, dp_buf):
        diag_ok = lax.broadcasted_iota(jnp.int32, (tq, tq), 0) >= lax.broadcasted_iota(jnp.int32, (tq, tq), 1)
        lane_q = lax.broadcasted_iota(jnp.int32, (tq, LANES), 1)
        mine_q = [lane_q < HEAD_DIM, lane_q >= HEAD_DIM]
        dv_acc[...] = jnp.zeros_like(dv_acc)
        dk_acc[...] = jnp.zeros_like(dk_acc)
        dfr_ref[...] = jnp.zeros_like(dfr_ref)
        transposed = lambda a: a.astype(F32).T.astype(BF16)

        def q_step(i, _):
            q0 = pl.multiple_of(i * tq, tq)
            qs = [q_ref[pl.ds(q0, tq), hl] for hl in heads]
            dov = do_ref[pl.ds(q0, tq), :]
            dos = [jnp.where(mq, dov, jnp.zeros((), BF16)) for mq in mine_q]
            qts = [transposed(q) for q in qs]
            dots = [transposed(a) for a in dos]
            lss = [lse_ref[pl.ds(q0, tq), e:e + 1] for e in range(2)]

            def sweep1(j, dls, diagonal):
                r0 = pl.multiple_of(j * tq, tq)
                vv = v_ref[pl.ds(r0, tq), :]
                out = []
                for e, hl in enumerate(heads):
                    s = _dot_nt(qs[e], k_ref[pl.ds(r0, tq), hl])
                    if diagonal:
                        s = jnp.where(diag_ok, s, NEG_BIG)
                    p = jnp.exp(s - lss[e])
                    dp = _dot_nt(dos[e], vv)
                    p_buf[e, j] = p
                    dp_buf[e, j] = dp
                    dv_acc[j] += _dot(dots[e], p.astype(BF16))
                    out.append(dls[e] + _fold_lanes(p * dp, jnp.add))
                return tuple(out)

            dls = _causal_sweep(i, sweep1, (jnp.zeros((tq, LANES), F32),) * 2)
            dls = [jnp.sum(d, axis=1, keepdims=True) for d in dls]

            def sweep2(j, dqs, diagonal):
                r0 = pl.multiple_of(j * tq, tq)
                out = []
                for e, hl in enumerate(heads):
                    ds = p_buf[e, j] * (dp_buf[e, j] - dls[e])
                    dfr_ref[e, pl.ds(j, 1), :] += jnp.sum(ds, axis=0, keepdims=True)
                    dsb = ds.astype(BF16)
                    dk_acc[e, j] += _dot(qts[e], dsb)
                    out.append(dqs[e] + _dot(dsb, k_ref[pl.ds(r0, tq), hl]))
                return tuple(out)

            dqs = _causal_sweep(i, sweep2, (jnp.zeros((tq, LANES), F32),) * 2)
            dq = jnp.where(mine_q[0], dqs[0], pltpu.roll(dqs[1], HEAD_DIM, 1)) * ATTN_SCALE
            dq_ref[pl.ds(q0, tq), :] = dq.astype(BF16)
            return 0

        lax.fori_loop(0, nq, q_step, 0)
        for j in range(nq):
            rs = slice(j * tq, (j + 1) * tq)
            dk = jnp.where(mine_q[0], dk_acc[0, j].T, pltpu.roll(dk_acc[1, j].T, HEAD_DIM, 1))
            dk_ref[rs, :] = dk.astype(BF16)
            dv_ref[rs, :] = dv_acc[j].T.astype(BF16)

    wide = pl.BlockSpec((seq, 2 * LANES), lambda b, hp: (b, hp))
    col = pl.BlockSpec((seq, LANES), lambda b, hp: (b, hp))
    pair = pl.BlockSpec((None, seq, 2), lambda b, hp: (hp, b, 0))
    dep_specs, dep_ops = _dep_args(dep)
    return pl.pallas_call(
        _after(body, 5, dep), name="attn_bwd",
        out_shape=(jax.ShapeDtypeStruct((t, ATTN_WIDTH), BF16),) * 3 + (jax.ShapeDtypeStruct((N_HEADS, t // tq, tq), F32),),
        grid=(t // seq, hp_n),
        in_specs=[wide, wide, col, col, pair] + dep_specs,
        out_specs=(col, col, col, pl.BlockSpec((2, nq, tq), lambda b, hp: (hp, b, 0))),
        scratch_shapes=[pltpu.VMEM((2, nq, LANES, tq), F32), pltpu.VMEM((nq, LANES, tq), F32),
                        pltpu.VMEM((2, nq, tq, tq), F32), pltpu.VMEM((2, nq, tq, tq), F32)],
        compiler_params=_cparams("parallel", "arbitrary"),
    )(qa, ka, v, do, lse4, *dep_ops)


def _forget_bwd(dfc, fl, bf, seq):
    t = fl.shape[0]
    cb = min(256, seq)
    nb = seq // cb

    def body(dfc_ref, fl_ref, bf_ref, dfl_ref, db_ref):
        b = pl.program_id(0)
        ri = lax.broadcasted_iota(jnp.int32, (cb, cb), 0)
        ci = lax.broadcasted_iota(jnp.int32, (cb, cb), 1)
        tri = (ci >= ri).astype(BF16)
        carry = jnp.zeros((1, LANES), F32)
        dbs = jnp.zeros((1, LANES), F32)
        for blk in reversed(range(nb)):
            rs = slice(blk * cb, (blk + 1) * cb)
            dlf = _tri_dot(tri, -dfc_ref[rs, :]) + carry
            carry = dlf[0:1, :]
            dfl = dlf * _sigmoid(-(fl_ref[rs, :] + bf_ref[...]))
            dfl_ref[rs, :] = dfl.astype(BF16)
            dbs = dbs + jnp.sum(dfl, axis=0, keepdims=True)

        @pl.when(b == 0)
        def _():
            db_ref[...] = jnp.zeros_like(db_ref)

        db_ref[...] += dbs

    return pl.pallas_call(
        body, name="forget_bwd",
        out_shape=(jax.ShapeDtypeStruct((t, LANES), BF16), jax.ShapeDtypeStruct((1, LANES), F32)),
        grid=(t // seq,),
        in_specs=[pl.BlockSpec((seq, LANES), lambda b: (b, 0)), pl.BlockSpec((seq, LANES), lambda b: (b, 0)),
                  pl.BlockSpec((1, LANES), lambda b: (0, 0))],
        out_specs=(pl.BlockSpec((seq, LANES), lambda b: (b, 0)), pl.BlockSpec((1, LANES), lambda b: (0, 0))),
        compiler_params=_cparams("arbitrary"),
    )(dfc, fl, bf)


def _pool_bwd(dps, p, mix, scale, seq):
    t = dps.shape[0]

    def body(dps_ref, p_ref, mix_ref, sc_ref, du_ref, dmix_ref, dsc_ref):
        b = pl.program_id(0)

        @pl.when(b == 0)
        def _():
            dmix_ref[...] = jnp.zeros_like(dmix_ref)
            dsc_ref[...] = jnp.zeros_like(dsc_ref)

        tpos = lax.broadcasted_iota(jnp.int32, (seq, POOL_GROUP_DIM), 0)
        for g in range(POOL_GROUPS):
            sl = slice(g * POOL_GROUP_DIM, (g + 1) * POOL_GROUP_DIM)
            pb = p_ref[:, sl]
            dpsg = dps_ref[:, sl]
            pm = _dot(pb, mix_ref[g])
            dsc_ref[:, sl] += jnp.sum(dpsg * pm, axis=0, keepdims=True)
            dpm = (dpsg * sc_ref[:, sl]).astype(BF16)
            dmix_ref[g] += _dot_tn(pb, dpm)
            dp = _dot_nt(dpm, mix_ref[g])
            cnt = jnp.minimum(tpos + 1, POOL_WINDOWS[g]).astype(F32)
            s = dp / cnt
            for lvl in range(g + 1):
                d = 2 ** lvl
                s = s + jnp.where(tpos < seq - d, pltpu.roll(s, seq - d, 0), 0.0)
            du_ref[:, sl] = (s - dp).astype(BF16)

    return pl.pallas_call(
        body, name="pool_bwd",
        out_shape=(jax.ShapeDtypeStruct((t, POOL_WIDTH), BF16),
                   jax.ShapeDtypeStruct((POOL_GROUPS, POOL_GROUP_DIM, POOL_GROUP_DIM), F32),
                   jax.ShapeDtypeStruct((1, POOL_WIDTH), F32)),
        grid=(t // seq,),
        in_specs=[pl.BlockSpec((seq, POOL_WIDTH), lambda b: (b, 0)), pl.BlockSpec((seq, POOL_WIDTH), lambda b: (b, 0)),
                  pl.BlockSpec((POOL_GROUPS, POOL_GROUP_DIM, POOL_GROUP_DIM), lambda b: (0, 0, 0)),
                  pl.BlockSpec((1, POOL_WIDTH), lambda b: (0, 0))],
        out_specs=(pl.BlockSpec((seq, POOL_WIDTH), lambda b: (b, 0)),
                   pl.BlockSpec((POOL_GROUPS, POOL_GROUP_DIM, POOL_GROUP_DIM), lambda b: (0, 0, 0)),
                   pl.BlockSpec((1, POOL_WIDTH), lambda b: (0, 0))),
        compiler_params=_cparams("arbitrary"),
    )(dps, p, mix, scale)


def _in_bwd(du, dq, dk, dv, dg2, dfl, dx1, x, g, wu, wqkv, wg2, wft, tm):
    t, d = x.shape
    tm = min(tm, t)
    rows = min(ROW_CHUNK, tm)
    aw = ATTN_WIDTH

    def body(du_ref, dq_ref, dk_ref, dv_ref, dg2_ref, dfl_ref, dx1_ref, x_ref, g_ref, wu_ref, wqkv_ref, wg2_ref, wft_ref,
             dx_ref, dg_ref):
        i = pl.program_id(0)

        @pl.when(i == 0)
        def _():
            dg_ref[...] = jnp.zeros_like(dg_ref)

        for r0 in range(0, tm, rows):
            rs = slice(r0, r0 + rows)
            dh = _dot(du_ref[rs, :], wu_ref[...])
            dh += _dot(dq_ref[rs, :], wqkv_ref[0:aw, :])
            dh += _dot(dk_ref[rs, :], wqkv_ref[aw:2 * aw, :])
            dh += _dot(dv_ref[rs, :], wqkv_ref[2 * aw:3 * aw, :])
            dh += _dot(dg2_ref[rs, :], wg2_ref[...])
            dh += _dot(dfl_ref[rs, :], wft_ref[...])
            dxn, dg = _rms_bwd(x_ref[rs, :], g_ref[...], dh)
            dx_ref[rs, :] = dx1_ref[rs, :] + dxn
            dg_ref[...] += dg

    row = lambda w: pl.BlockSpec((tm, w), lambda i: (i, 0))
    full = lambda a: pl.BlockSpec(a.shape, lambda i: (0, 0))
    return pl.pallas_call(
        body, name="in_bwd",
        out_shape=(jax.ShapeDtypeStruct((t, d), F32), jax.ShapeDtypeStruct((1, d), F32)),
        grid=(t // tm,),
        in_specs=[row(POOL_WIDTH), row(aw), row(aw), row(aw), row(2 * d), row(LANES), row(d), row(d),
                  pl.BlockSpec((1, d), lambda i: (0, 0)), full(wu), full(wqkv), full(wg2), full(wft)],
        out_specs=(row(d), pl.BlockSpec((1, d), lambda i: (0, 0))),
        compiler_params=_cparams("arbitrary"),
    )(du, dq, dk, dv, dg2, dfl, dx1, x, g, wu, wqkv, wg2, wft)


def _position():
    return lax.axis_index("x"), lax.axis_index("y"), lax.axis_index("c")


def _remote(src, dst, send_sem, recv_sem, device):
    return pltpu.make_async_remote_copy(src_ref=src, dst_ref=dst, send_sem=send_sem, recv_sem=recv_sem,
                                        device_id=device, device_id_type=MESH)


HBM = pl.BlockSpec(memory_space=pltpu.HBM)
SEM = pl.BlockSpec(memory_space=pltpu.SEMAPHORE)
DATAFLOW = pltpu.SideEffectType.DATAFLOW_SIDE_EFFECTING


def _copies_start(name, arrays, plan, m, dep=None):
    n = len(arrays)
    arrays = [pltpu.with_memory_space_constraint(a, pltpu.HBM) for a in arrays]

    def body(*refs):
        ins, send_sem, recv_sem, token = refs[:n], refs[n], refs[n + 1], refs[2 * n + 2]
        for i, (src, dst, device, _) in enumerate(plan(ins, *_position())):
            _remote(src, dst, send_sem.at[i], recv_sem.at[i], device).start()
        token[...] = jnp.zeros_like(token)

    dep_specs, dep_ops = _dep_args(dep)
    outs = pl.pallas_call(
        _after(body, n, dep), name=name,
        out_shape=(pltpu.SemaphoreType.DMA((m,)), pltpu.SemaphoreType.DMA((m,)),
                   *[pltpu.HBM(a.shape, a.dtype) for a in arrays], jax.ShapeDtypeStruct((8, LANES), F32)),
        in_specs=[HBM] * n + dep_specs, out_specs=(SEM, SEM, *[HBM] * n, pl.BlockSpec(memory_space=pltpu.VMEM)),
        input_output_aliases={i: i + 2 for i in range(n)},
        compiler_params=pltpu.CompilerParams(has_side_effects=DATAFLOW),
    )(*arrays, *dep_ops)
    return (outs[0], outs[1]), list(outs[2:2 + n]), outs[2 + n]


def _copies_wait(name, sems, arrays, plan, after):
    n = len(arrays)
    afters = list(after) if isinstance(after, (list, tuple)) else [after]

    def body(*refs):
        ins, send_sem, recv_sem = refs[:n], refs[n], refs[n + 1]
        for i, (src, dst, device, landing) in enumerate(plan(ins, *_position())):
            _remote(src, dst, send_sem.at[i], recv_sem.at[i], device).wait_send()
            _remote(landing, landing, send_sem.at[i], recv_sem.at[i], device).wait_recv()

    outs = pl.pallas_call(
        body, name=name,
        out_shape=tuple(pltpu.HBM(a.shape, a.dtype) for a in arrays),
        in_specs=[HBM] * n + [SEM, SEM] + [ANY] * len(afters), out_specs=tuple([HBM] * n),
        input_output_aliases={i: i for i in range(n)},
        compiler_params=pltpu.CompilerParams(has_side_effects=DATAFLOW),
    )(*arrays, sems[0], sems[1], *afters)
    return list(outs)


def _tie(x, dep):
    for token in _dep_list(dep):
        x = x + token[0, 0]
    return x


def _other_chips(x, y):
    return [(1 - x, y), (x, 1 - y), (1 - x, 1 - y)]


def _gather_begin(tag, shards, token, column_halves=False):
    n = len(shards)
    lands = [lax.empty((N_CHIPS,) + s.shape, s.dtype) for s in shards]
    if column_halves:
        cols = lambda ref, h: pl.ds(pl.multiple_of(h * (ref.shape[-1] // 2), LANES), ref.shape[-1] // 2)
        mine = lambda ref, h: ref.at[:, cols(ref, h)]
        landed = lambda ref, chip, h: ref.at[chip, :, cols(ref, h)]
    else:
        mine = lambda ref, h: ref.at[h]
        landed = lambda ref, chip, h: ref.at[chip, h]

    def plan(refs, x, y, c):
        return [(mine(refs[k], c), landed(refs[n + k], 2 * x + y, c), (ox, oy, c), landed(refs[n + k], 2 * ox + oy, c))
                for k in range(n) for ox, oy in _other_chips(x, y)]

    sems, thru, token = _copies_start(f"gather_{tag}_ici_start", list(shards) + lands, plan, 3 * n, dep=token)
    return dict(tag=tag, n=n, plan=plan, sems=sems, arrays=thru, token=token, landed=landed)


def _gather_forward(st, after):
    n, tag, landed = st["n"], st["tag"], st["landed"]
    thru = _copies_wait(f"gather_{tag}_ici_wait", st["sems"], st["arrays"], st["plan"], after)

    def plan(refs, x, y, c):
        return [(landed(refs[k], 2 * ox + oy, c), landed(refs[k], 2 * ox + oy, c), (x, y, 1 - c),
                 landed(refs[k], 2 * ox + oy, 1 - c))
                for k in range(n) for ox, oy in _other_chips(x, y)]

    sems, lands, token = _copies_start(f"gather_{tag}_fwd_start", thru[n:], plan, 3 * n)
    return dict(tag=tag, n=n, plan=plan, sems=sems, arrays=lands, token=token, shards=thru[:n])


def _gather_end(st, after, merge=True):
    lands = _copies_wait(f"gather_{st['tag']}_fwd_wait", st["sems"], st["arrays"], st["plan"], after)
    if not merge:
        return lands, st["shards"]
    me = 2 * lax.axis_index("x") + lax.axis_index("y")
    return [lax.dynamic_update_index_in_dim(g, s, me, 0) for g, s in zip(lands, st["shards"])]


def _add_keep_give(name, pos, a, a_keep, a_give, b, b_keep, b_give, steps):
    r, c = b.shape[-2:]

    def spec(arr, fn):
        lead = arr.ndim - 2

        def index(i, p):
            idx = tuple(fn(i, p))
            return idx if len(idx) == arr.ndim else idx + (0, 0)

        return pl.BlockSpec((None,) * lead + (r, c), index)

    out_spec = pl.BlockSpec((None, r, c), lambda i, p: (i, 0, 0))

    def body(p_ref, ak_ref, bk_ref, ag_ref, bg_ref, keep_ref, give_ref):
        keep_ref[...] = ak_ref[...] + bk_ref[...].astype(F32)
        give_ref[...] = (ag_ref[...] + bg_ref[...].astype(F32)).astype(BF16)

    return pl.pallas_call(
        body, name=name,
        out_shape=(jax.ShapeDtypeStruct((steps, r, c), F32), jax.ShapeDtypeStruct((steps, r, c), BF16)),
        grid_spec=pltpu.PrefetchScalarGridSpec(
            num_scalar_prefetch=1, grid=(steps,),
            in_specs=[spec(a, a_keep), spec(b, b_keep), spec(a, a_give), spec(b, b_give)],
            out_specs=(out_spec, out_spec)),
        compiler_params=_cparams("parallel"),
    )(pos, a, b, a, b)


def _add_last(name, a, b):
    _, r, c = a.shape
    blk = pl.BlockSpec((None, r, c), lambda i: (0, 0, 0))

    def body(a_ref, b_ref, o_ref):
        o_ref[...] = a_ref[...] + b_ref[...].astype(F32)

    return pl.pallas_call(
        body, name=name, out_shape=jax.ShapeDtypeStruct((r, c), F32), grid=(1,), in_specs=[blk, blk],
        out_specs=pl.BlockSpec((r, c), lambda i: (0, 0)), compiler_params=_cparams("arbitrary"),
    )(a, b)


def _exchange_begin(tag, stage, gives, lands, peer_fn, extra):
    n = len(gives)

    def plan(refs, x, y, c):
        return [(refs[k], refs[n + k], peer_fn(x, y, c), refs[n + k]) for k in range(n)]

    sems, thru, token = _copies_start(f"rs{tag}_{stage}_start", gives + lands, plan, n)
    return dict(extra, tag=tag, n=n, stage=stage, plan=plan, sems=sems, arrays=thru, token=token)


def _reduce_begin(tag, grads, column_halves=False):
    n = len(grads)
    if column_halves:
        half = lambda ref, j, h: ref.at[j, :, pl.ds(pl.multiple_of(h * (ref.shape[2] // 2), LANES), ref.shape[2] // 2)]
        lands = [lax.empty((N_CHIPS, g.shape[1], g.shape[2] // 2), F32) for g in grads]
    else:
        half = lambda ref, j, h: ref.at[j, h]
        lands = [lax.empty((N_CHIPS,) + g.shape[2:], F32) for g in grads]

    def plan(refs, x, y, c):
        return [(half(refs[k], j, 1 - c), refs[n + k].at[j], (x, y, 1 - c), refs[n + k].at[j])
                for k in range(n) for j in range(N_CHIPS)]

    sems, thru, token = _copies_start(f"rs{tag}_c_start", list(grads) + lands, plan, N_CHIPS * n)
    return dict(tag=tag, n=n, stage="c", plan=plan, sems=sems, arrays=thru, token=token, column_halves=column_halves)


def _reduce_advance(st, after):
    tag, n, stage = st["tag"], st["n"], st["stage"]
    thru = _copies_wait(f"rs{tag}_{stage}_wait", st["sems"], st["arrays"], st["plan"], after)
    first, recv = thru[:n], thru[n:]
    x, y, c = _position()
    if stage == "c":
        pos = jnp.stack([c, x]).astype(jnp.int32)
        if st["column_halves"]:
            mine = lambda chip: (lambda i, p: (chip(p) + i, 0, p[0]))
        else:
            mine = lambda chip: (lambda i, p: (chip(p) + i, p[0]))
        sums = [_add_keep_give(
            f"rs{tag}_c_add{k}", pos,
            first[k], mine(lambda p: 2 * p[1]), mine(lambda p: 2 * (1 - p[1])),
            recv[k], lambda i, p: (2 * p[1] + i,), lambda i, p: (2 * (1 - p[1]) + i,), 2) for k in range(n)]
        lands = [lax.empty(s[1].shape, BF16) for s in sums]
        return _exchange_begin(tag, "x", [s[1] for s in sums], lands, lambda x, y, c: (1 - x, y, c),
                               dict(keep=[s[0] for s in sums]))
    if stage == "x":
        pos = jnp.stack([y]).astype(jnp.int32)
        sums = [_add_keep_give(
            f"rs{tag}_x_add{k}", pos,
            st["keep"][k], lambda i, p: (p[0],), lambda i, p: (1 - p[0],),
            recv[k], lambda i, p: (p[0],), lambda i, p: (1 - p[0],), 1) for k in range(n)]
        lands = [lax.empty(s[1].shape, BF16) for s in sums]
        return _exchange_begin(tag, "y", [s[1] for s in sums], lands, lambda x, y, c: (x, 1 - y, c),
                               dict(keep=[s[0] for s in sums]))
    if stage == "y":
        mine = [_add_last(f"rs{tag}_y_add{k}", st["keep"][k], recv[k]) for k in range(n)]
        lands = [lax.empty(m.shape, F32) for m in mine]
        return _exchange_begin(tag, "swap", mine, lands, lambda x, y, c: (x, y, 1 - c), {})
    return dict(done=list(zip(first, recv)), token=None)


def _small_begin(v, dep):
    land = lax.empty((N_DEV,) + v.shape, F32)
    flips = [(fx, fy, fc) for fx in (0, 1) for fy in (0, 1) for fc in (0, 1)][1:]

    def plan(refs, x, y, c):
        copies = []
        for fx, fy, fc in flips:
            px, py, pc = (1 - x if fx else x), (1 - y if fy else y), (1 - c if fc else c)
            copies.append((refs[0], refs[1].at[4 * x + 2 * y + c], (px, py, pc), refs[1].at[4 * px + 2 * py + pc]))
        return copies

    sems, thru, token = _copies_start("small_start", [v, land], plan, len(flips), dep=dep)
    return dict(plan=plan, sems=sems, arrays=thru, token=token)


def _small_end(st, after):
    own, land = _copies_wait("small_wait", st["sems"], st["arrays"], st["plan"], after)
    x, y, c = _position()
    me = jnp.stack([4 * x + 2 * y + c]).astype(jnp.int32)

    def body(me_ref, own_ref, land_ref, out_ref):
        term = lambda dev: jnp.where(me_ref[0] == dev, own_ref[...], land_ref[dev])
        acc = term(0)
        for dev in range(1, N_DEV):
            acc = acc + term(dev)
        out_ref[...] = acc

    return pl.pallas_call(
        body, name="small_sum", out_shape=jax.ShapeDtypeStruct(own.shape, F32),
        grid_spec=pltpu.PrefetchScalarGridSpec(
            num_scalar_prefetch=1, grid=(1,),
            in_specs=[pl.BlockSpec(own.shape, lambda i, m: (0, 0)), pl.BlockSpec(land.shape, lambda i, m: (0, 0, 0))],
            out_specs=pl.BlockSpec(own.shape, lambda i, m: (0, 0))),
        compiler_params=_cparams("arbitrary"),
    )(me, own, land)


def _all_reduce_small(v):
    r = v.shape[0]

    def body(v_ref, out_ref, buf, send_sems, recv_sems, local_sem):
        x, y, c = _position()
        me, sibling = (x, y, c), (x, y, 1 - c)
        chips = [(1 - x, y), (x, 1 - y), (1 - x, 1 - y)]

        def rows(px, py, pc):
            return buf.at[pl.ds((4 * px + 2 * py + pc) * r, r), :]

        def copy(k, block, to, src=None):
            return _remote(rows(*block) if src is None else src, rows(*block), send_sems.at[k], recv_sems.at[k], to)

        mine = pltpu.make_async_copy(v_ref, rows(*me), local_sem)
        mine.start()
        first = [copy(0, me, sibling, src=v_ref)]
        first += [copy(1 + j, me, (*chip, c), src=v_ref) for j, chip in enumerate(chips)]
        for cp in first:
            cp.start()
        passed = [copy(4 + j, (*chip, c), sibling) for j, chip in enumerate(chips)]
        for j, chip in enumerate(chips):
            copy(1 + j, (*chip, c), me).wait_recv()
            passed[j].start()
        copy(0, sibling, me).wait_recv()
        for j, chip in enumerate(chips):
            copy(4 + j, (*chip, 1 - c), me).wait_recv()
        for cp in first + passed:
            cp.wait_send()
        mine.wait()
        acc = buf[0:r, :]
        for dev in range(1, N_DEV):
            acc = acc + buf[dev * r:(dev + 1) * r, :]
        out_ref[...] = acc

    return pl.pallas_call(
        body, name="all_reduce_small",
        out_shape=jax.ShapeDtypeStruct(v.shape, F32),
        in_specs=[pl.BlockSpec(memory_space=pltpu.VMEM)],
        out_specs=pl.BlockSpec(memory_space=pltpu.VMEM),
        scratch_shapes=[pltpu.VMEM((N_DEV * r, LANES), F32), pltpu.SemaphoreType.DMA((7,)),
                        pltpu.SemaphoreType.DMA((7,)), pltpu.SemaphoreType.DMA],
        compiler_params=pltpu.CompilerParams(has_side_effects=True, vmem_limit_bytes=VMEM_LIMIT_V7X),
    )(v)


def _adamw_update(w, gg, m, v):
    mn = ADAM_B1 * m + (1.0 - ADAM_B1) * gg
    vn = ADAM_B2 * v + (1.0 - ADAM_B2) * (gg * gg)
    m_hat = mn / (1.0 - ADAM_B1 ** ADAM_STEP)
    v_hat = vn / (1.0 - ADAM_B2 ** ADAM_STEP)
    return -ADAM_LR * (m_hat / (jnp.sqrt(v_hat) + ADAM_EPS) + ADAM_WD * w), mn, vn


def _adamw(name, w, g, m, v):
    def body(w_ref, g_ref, m_ref, v_ref, d_ref, mo_ref, vo_ref):
        d_ref[...], mo_ref[...], vo_ref[...] = _adamw_update(w_ref[...], g_ref[...], m_ref[...], v_ref[...])

    blk = pl.BlockSpec(w.shape, lambda i: (0, 0))
    return pl.pallas_call(
        body, name=name, out_shape=(jax.ShapeDtypeStruct(w.shape, F32),) * 3, grid=(1,),
        in_specs=[blk] * 4, out_specs=(blk,) * 3, compiler_params=_cparams("arbitrary"),
    )(w, g, m, v)


def _rows_to_bf16(name, w):
    r, _, c = w.shape

    def body(w_ref, o_ref):
        o_ref[...] = w_ref[:, 0, :].astype(BF16)

    return pl.pallas_call(
        body, name=name, out_shape=jax.ShapeDtypeStruct((r, c), BF16), grid=(1,),
        in_specs=[pl.BlockSpec((r, 1, c), lambda i: (0, 0, 0))], out_specs=pl.BlockSpec((r, c), lambda i: (0, 0)),
        compiler_params=_cparams("arbitrary"),
    )(w)


def _adamw_rows(name, pos_c, w, g_mine, g_other, m, v):
    r, _, c = w.shape
    ch = c // 2

    def body(p_ref, w_ref, gm_ref, go_ref, m_ref, v_ref, g_ref, d_ref, mo_ref, vo_ref):
        gg = jnp.where(pl.program_id(0) == p_ref[0], gm_ref[...], go_ref[...])
        dl, mn, vn = _adamw_update(w_ref[:, 0, :], gg, m_ref[:, 0, :], v_ref[:, 0, :])
        g_ref[:, 0, :] = gg
        d_ref[:, 0, :] = dl
        mo_ref[:, 0, :] = mn
        vo_ref[:, 0, :] = vn

    rows = pl.BlockSpec((r, 1, ch), lambda h, p: (0, 0, h))
    half = pl.BlockSpec((r, ch), lambda h, p: (0, 0))
    return pl.pallas_call(
        body, name=name, out_shape=(jax.ShapeDtypeStruct(w.shape, F32),) * 4,
        grid_spec=pltpu.PrefetchScalarGridSpec(
            num_scalar_prefetch=1, grid=(2,), in_specs=[rows, half, half, rows, rows], out_specs=(rows,) * 4),
        compiler_params=_cparams("parallel"),
    )(pos_c, w, g_mine, g_other, m, v)


def _adamw_halves(name, pos_c, w, g_mine, g_other, m, v, tr, dep=None):
    r, c = w.shape
    rh = r // 2
    tr = tr if rh % tr == 0 else rh
    nt = rh // tr

    def body(p_ref, w_ref, gm_ref, go_ref, m_ref, v_ref, g_ref, d_ref, mo_ref, vo_ref):
        gg = jnp.where(pl.program_id(0) == p_ref[0], gm_ref[...], go_ref[...])
        g_ref[...] = gg
        d_ref[...], mo_ref[...], vo_ref[...] = _adamw_update(w_ref[...], gg, m_ref[...], v_ref[...])

    full = pl.BlockSpec((tr, c), lambda h, i, p: (h * nt + i, 0))
    half = pl.BlockSpec((tr, c), lambda h, i, p: (i, 0))
    dep_specs, dep_ops = _dep_args(dep)
    return pl.pallas_call(
        _after(body, 6, dep), name=name, out_shape=(jax.ShapeDtypeStruct((r, c), F32),) * 4,
        grid_spec=pltpu.PrefetchScalarGridSpec(
            num_scalar_prefetch=1, grid=(2, nt),
            in_specs=[full, half, half, full, full] + dep_specs, out_specs=(full,) * 4),
        compiler_params=_cparams("parallel", "parallel"),
    )(pos_c, w, g_mine, g_other, m, v, *dep_ops)


def _col_sharded_to_comm(g):
    k, n = g.shape
    return g.reshape(2, k // 2, N_CHIPS, n // N_CHIPS).transpose(2, 0, 1, 3)


def _row_sharded_to_comm(g):
    r, c = g.shape
    return g.reshape(N_CHIPS, 2, r // (2 * N_CHIPS), c)


def _col_sharded_full(g):
    _, _, rh, c = g.shape
    return g.reshape(N_CHIPS, 2 * rh, c).transpose(1, 0, 2).reshape(2 * rh, N_CHIPS * c)


def _row_sharded_full(g):
    _, _, rh, c = g.shape
    return g.reshape(N_CHIPS * 2 * rh, c)


def _chip_rows(w3, start, stop, own=None, me=None):
    r = w3.shape[1]
    parts = []
    for chip in range(N_CHIPS):
        lo, hi = max(start - chip * r, 0), min(stop - chip * r, r)
        if lo < hi:
            part = w3[chip, lo:hi]
            parts.append(part if own is None else jnp.where(me == chip, own[lo:hi], part))
    return parts[0] if len(parts) == 1 else jnp.concatenate(parts, axis=0)


def _pack_small(g1, bfv, mix, scale, g2n, gf, extra=None):
    row8 = jnp.pad(bfv.reshape(1, N_HEADS), ((0, 0), (0, LANES - N_HEADS)))
    if extra is not None:
        row8 = row8 + jnp.pad(extra[:, :1], ((0, 0), (N_HEADS, LANES - N_HEADS - 1)))
    return jnp.concatenate([
        g1.reshape(8, LANES), jnp.pad(row8, ((0, 7), (0, 0))), mix.reshape(512, LANES),
        jnp.pad(scale.reshape(4, LANES), ((0, 4), (0, 0))), g2n.reshape(8, LANES), gf.reshape(8, LANES)], axis=0)


def _unpack_small(s, like):
    g1, bfv, mix, scale, g2n, gf = like
    return (s[0:8].reshape(g1.shape), s[8, :N_HEADS].reshape(bfv.shape), s[16:528].reshape(mix.shape),
            s[528:532].reshape(scale.shape), s[536:544].reshape(g2n.shape), s[544:552].reshape(gf.shape))


class _MeshLinks:
    def __init__(self, shards_in, shards_rest):
        self.gin = _gather_begin("in", shards_in, None, column_halves=True)
        self.grest = _gather_begin("rest", shards_rest, self.gin["token"])
        self.tokens = {"gather": self.grest["token"]}
        self.groups = {}

    @property
    def token(self):
        return list(self.tokens.values())

    def tie(self, x):
        return _tie(x, self.token)

    def weights_in(self, after):
        st = _gather_forward(self.gin, after)
        (g,), (own,) = _gather_end(st, st["token"], merge=False)
        return g, own, 2 * lax.axis_index("x") + lax.axis_index("y")

    def rest_forward(self, after):
        self.grest = _gather_forward(self.grest, after)
        self.tokens["gather"] = self.grest["token"]

    def weights_rest(self, after):
        g = _gather_end(self.grest, after)
        del self.tokens["gather"]
        return [_col_sharded_full(g[0]), _col_sharded_full(g[1])] + [_row_sharded_full(a) for a in g[2:]]

    def reduce_begin(self, tag, grads, column_halves=False):
        self.groups[tag] = _reduce_begin(tag, grads, column_halves)
        self.tokens[tag] = self.groups[tag]["token"]

    def advance(self, after):
        for tag, st in self.groups.items():
            if "done" not in st:
                self.groups[tag] = _reduce_advance(st, after)
                if self.groups[tag]["token"] is None:
                    del self.tokens[tag]
                else:
                    self.tokens[tag] = self.groups[tag]["token"]

    def reduced(self, tag):
        return self.groups[tag]["done"]


class _NoLinks:
    token = None

    def __init__(self, w_in, rest):
        self.w_in, self.rest, self.grads = w_in, rest, {}

    def tie(self, x):
        return x

    def weights_in(self, after):
        return self.w_in, None, None

    def rest_forward(self, after):
        pass

    def weights_rest(self, after):
        return self.rest

    def reduce_begin(self, tag, grads, column_halves=False):
        self.grads[tag] = grads

    def advance(self, after):
        pass


def _local_step(links, x, target, seq, norm1_g, b_forget, pool_mix, pool_scale, norm2_g, norm_f_g):
    t, d = x.shape
    tq = min(256, seq)
    aw = ATTN_WIDTH
    o_q, o_f, o_g = POOL_WIDTH, POOL_WIDTH + 3 * aw, POOL_WIDTH + 3 * aw + N_HEADS
    bf = jnp.pad(b_forget, ((0, 0), (0, LANES - N_HEADS)))
    mixb = pool_mix.astype(BF16)

    h = _norm_fwd("norm1_fwd", x, links.tie(norm1_g), 512)
    w_in, own, me = links.weights_in(h)
    wu = _chip_rows(w_in, 0, o_q, own, me)
    wqkv = _chip_rows(w_in, o_q, o_f, own, me)
    wft = jnp.pad(_chip_rows(w_in, o_f, o_g, own, me), ((0, LANES - N_HEADS), (0, 0)))
    wg2 = _chip_rows(w_in, o_g, N_CHIPS * w_in.shape[1], own, me)
    wf = wft.T
    u = _matmul("mm_u", h, wu, "nt", F32, 1024, 512, d)
    g2 = _matmul("mm_gates", h, wg2, "nt", BF16, 1024, 1024, d)
    fl, fcum = _forget_fwd(h, wf, bf, seq)
    qa, ka, v = _attn_prep(h, _head_blocks(wqkv[:aw]), _head_blocks(wqkv[aw:2 * aw]), wqkv[2 * aw:], fcum, 1024)
    p, ps = _pool_fwd(u, mixb, pool_scale, seq)
    links.rest_forward([ps, qa, g2])
    o, lse = _attn_fwd(qa, ka, v, seq, tq, dep=links.token)
    w_pool_out, w_attn_out, w_out, w_ffn_gate, w_ffn_up, w_ffn_down = links.weights_rest(o)
    merged, x1 = _merge_fwd(x, ps, o, g2, w_pool_out, w_attn_out, w_out, 512)
    h2, gt, up, act, x2 = _ffn_fwd(x1, norm2_g, w_ffn_gate, w_ffn_up, w_ffn_down, 1024, 256)
    loss, dx2, d_gf = _final_fwd_bwd(x2, target, norm_f_g, 512)

    dgt, dup, dx1, d_g2n = _ffn_bwd(dx2, x1, norm2_g, gt, up, w_ffn_gate, w_ffn_up, w_ffn_down, 1024, 256)
    d_wd = _matmul("dw_down", act, dx2, "tn", F32, 1408, 1024, 1024)
    d_wg = _matmul("dw_gate", dgt, h2, "tn", F32, 1408, 1024, 1024)
    d_wu = _matmul("dw_up", dup, h2, "tn", F32, 1408, 1024, 1024)
    links.reduce_begin("a", [_row_sharded_to_comm(g) for g in (d_wg, d_wu, d_wd)])
    dpy, day, dg2, dps, da = _merge_bwd(dx1, ps, o, g2, w_pool_out, w_attn_out, w_out, 512, dep=links.token)
    links.advance(dps)
    d_wout = _matmul("dw_out", merged, dx1, "tn", F32, 1024, 1024, 1024)
    d_wpo = _matmul("dw_pool_out", ps, dpy, "tn", F32, 512, 1024, 1024)
    d_wao = _matmul("dw_attn_out", o, day, "tn", F32, 512, 1024, 1024)
    links.reduce_begin("m", [_col_sharded_to_comm(d_wpo), _col_sharded_to_comm(d_wao), _row_sharded_to_comm(d_wout)])
    dq, dk, dv, dfr = _attn_bwd(qa, ka, v, da, lse, seq, tq, dep=links.token)
    links.advance(dq)
    dfc = jnp.pad(dfr.reshape(N_HEADS, t).T, ((0, 0), (0, LANES - N_HEADS)))
    dfl, d_bf = _forget_bwd(dfc, fl, bf, seq)
    du, d_mix, d_scale = _pool_bwd(dps, p, mixb, links.tie(pool_scale), seq)
    d_wu_in = _matmul("dw_in_u", du, h, "tn", F32, 512, 1024, 1024)
    d_wq = _matmul("dw_in_q", dq, h, "tn", F32, 512, 1024, 1024)
    d_wk = _matmul("dw_in_k", dk, h, "tn", F32, 512, 1024, 1024)
    d_wv = _matmul("dw_in_v", dv, h, "tn", F32, 512, 1024, 1024)
    links.advance([d_wu_in, d_wq, d_wk, d_wv])
    d_wf = _matmul("dw_in_f", dfl, h, "tn", F32, LANES, 1024, 512)
    d_wg2 = _matmul("dw_in_gates", dg2, h, "tn", F32, 1024, 1024, 1024, dep=links.token)
    d_win = jnp.concatenate([d_wu_in, d_wq, d_wk, d_wv, d_wf[:N_HEADS], d_wg2], axis=0)
    comm_b = [d_win.reshape(N_CHIPS, d_win.shape[0] // N_CHIPS, d)]
    links.advance(comm_b)
    links.reduce_begin("b", comm_b, column_halves=True)
    dx, d_g1 = _in_bwd(du, dq, dk, dv, dg2, dfl, dx1, x, links.tie(norm1_g), wu, wqkv, wg2, wft, 512)
    links.advance(dx)
    small = (d_g1, d_bf[:, :N_HEADS], d_mix, d_scale, d_g2n, d_gf)
    return loss, dx, small


def kernel(x, norm1_g, w_in, b_forget, pool_mix, pool_scale, w_pool_out, w_attn_out, w_out, norm2_g, w_ffn_gate, w_ffn_up, w_ffn_down, norm_f_g, loss_target, m_norm1_g, m_w_in, m_b_forget, m_pool_mix, m_pool_scale, m_w_pool_out, m_w_attn_out, m_w_out, m_norm2_g, m_w_ffn_gate, m_w_ffn_up, m_w_ffn_down, m_norm_f_g, v_norm1_g, v_w_in, v_b_forget, v_pool_mix, v_pool_scale, v_w_pool_out, v_w_attn_out, v_w_out, v_norm2_g, v_w_ffn_gate, v_w_ffn_up, v_w_ffn_down, v_norm_f_g):
    nb, seq, d = x.shape
    group_a = ((w_ffn_gate, m_w_ffn_gate, v_w_ffn_gate, True, 9), (w_ffn_up, m_w_ffn_up, v_w_ffn_up, True, 10),
               (w_ffn_down, m_w_ffn_down, v_w_ffn_down, False, 11))
    group_m = ((w_pool_out, m_w_pool_out, v_w_pool_out, False, 5), (w_attn_out, m_w_attn_out, v_w_attn_out, False, 6),
               (w_out, m_w_out, v_w_out, False, 7))
    group_b = ((w_in, m_w_in, v_w_in, False, 1),)
    small_w = (norm1_g, b_forget, pool_mix, pool_scale, norm2_g, norm_f_g)
    small_m = (m_norm1_g, m_b_forget, m_pool_mix, m_pool_scale, m_norm2_g, m_norm_f_g)
    small_v = (v_norm1_g, v_b_forget, v_pool_mix, v_pool_scale, v_norm2_g, v_norm_f_g)
    small_pos = (0, 2, 3, 4, 8, 12)
    view = lambda a, tr: a[0].T if tr else a[0]
    unview = lambda a, tr, like: (a.T if tr else a).reshape(like.shape)

    def shard(w, tr):
        lw = view(w, tr).astype(BF16)
        return lw.reshape(2, lw.shape[0] // 2, lw.shape[1])

    cm = lambda a: jnp.transpose(a, (2, 0, 1))
    shard_in = _rows_to_bf16("w_in_to_bf16", cm(w_in))
    links = _MeshLinks([shard_in],
                       [shard(w_pool_out, False), shard(w_attn_out, False), shard(w_out, False),
                        shard(w_ffn_gate, True), shard(w_ffn_up, True), shard(w_ffn_down, False)])
    loss, dx, small_g = _local_step(
        links, x.reshape(nb * seq, d), loss_target.reshape(nb * seq, d), seq,
        norm1_g, b_forget, pool_mix[0], pool_scale, norm2_g, norm_f_g.reshape(1, d))

    grads, deltas, new_m, new_v = [None] * 13, [None] * 13, [None] * 13, [None] * 13
    pos_c = jnp.stack([lax.axis_index("c")]).astype(jnp.int32)

    def update(tag, group, dep):
        last = []
        for k, ((w, m, v, tr, pos), (mine, other)) in enumerate(zip(group, links.reduced(tag))):
            outs = _adamw_halves(f"adamw_{tag}{k}", pos_c, view(w, tr), mine, other, view(m, tr), view(v, tr), 256,
                                 dep=dep)
            grads[pos], deltas[pos], new_m[pos], new_v[pos] = (unview(a, tr, w) for a in outs)
            last.append(outs[1])
        return last

    small_state = _small_begin(_pack_small(*small_g, extra=loss), links.token)
    links.tokens["small"] = small_state["token"]
    last = update("a", group_a, links.token) + update("m", group_m, links.token)
    links.advance(last)
    del links.tokens["small"]
    small_sum = _small_end(small_state, last)
    loss_out = small_sum[8, N_HEADS]
    dl, mn, vn = _adamw("adamw_small", _pack_small(*small_w), small_sum * _small_mask(), _pack_small(*small_m),
                        _pack_small(*small_v))
    for pos, g, a, b, e in zip(small_pos, _unpack_small(small_sum, small_w), _unpack_small(dl, small_w),
                               _unpack_small(mn, small_w), _unpack_small(vn, small_w)):
        grads[pos], deltas[pos], new_m[pos], new_v[pos] = g, a, b, e
    links.advance(dl)
    links.advance(links.token)
    (mine, other), = links.reduced("b")
    outs = _adamw_rows("adamw_b0", pos_c, cm(w_in), mine, other, cm(m_w_in), cm(v_w_in))
    grads[1], deltas[1], new_m[1], new_v[1] = (jnp.transpose(a, (1, 2, 0)) for a in outs)

    return (loss_out, dx.reshape(nb, seq, d), *grads, *deltas, *new_m, *new_v)


def _small_mask():
    rows = lax.broadcasted_iota(jnp.int32, (552, LANES), 0)
    lanes = lax.broadcasted_iota(jnp.int32, (552, LANES), 1)
    return jnp.where(jnp.logical_and(rows == 8, lanes == N_HEADS), 0.0, 1.0).astype(F32)
```

```python
import functools

import jax
import jax.numpy as jnp
from jax import lax
from jax.experimental import pallas as pl
from jax.experimental.pallas import tpu as pltpu

F32 = jnp.float32
BF16 = jnp.bfloat16

D_MODEL = 1024
POOL_WINDOWS = (2, 4, 8, 16)
POOL_GROUPS = 4
POOL_GROUP_DIM = 128
POOL_WIDTH = 512
HEAD_DIM = 64
N_HEADS = 8
ATTN_WIDTH = 512
D_FF = 2816
RMS_EPS = 1e-6
ATTN_SCALE = HEAD_DIM ** -0.5
NEG_BIG = -1e30

ADAM_LR = 0.001
ADAM_B1 = 0.9
ADAM_B2 = 0.999
ADAM_EPS = 1e-08
ADAM_WD = 0.01
ADAM_STEP = 10

LANES = 128
N_CHIPS = 4
N_DEV = 8
VMEM_LIMIT_V7X = 52 * 1024 * 1024
ROW_CHUNK = 256
MESH = pl.DeviceIdType.MESH
ANY = pl.BlockSpec(memory_space=pl.ANY)


def _cparams(*sem):
    return pltpu.CompilerParams(dimension_semantics=sem if sem else None, vmem_limit_bytes=VMEM_LIMIT_V7X)


def _dep_list(dep):
    return [] if dep is None else (list(dep) if isinstance(dep, (list, tuple)) else [dep])


def _after(body, n_in, dep):
    k = len(_dep_list(dep))
    if k == 0:
        return body

    def wrapped(*refs):
        body(*refs[:n_in], *refs[n_in + k:])

    return wrapped


def _dep_args(dep):
    deps = _dep_list(dep)
    return [ANY] * len(deps), deps


def _dot(a, b):
    return lax.dot_general(a, b, (((1,), (0,)), ((), ())), preferred_element_type=F32)


def _dot_nt(a, b):
    return lax.dot_general(a, b, (((1,), (1,)), ((), ())), preferred_element_type=F32)


def _dot_tn(a, b):
    return lax.dot_general(a, b, (((0,), (0,)), ((), ())), preferred_element_type=F32)


def _sigmoid(x):
    return jax.nn.sigmoid(x)


def _rms_fwd(x, g):
    r = lax.rsqrt(jnp.mean(x * x, axis=-1, keepdims=True) + RMS_EPS)
    return (x * r) * g


def _rms_bwd(x, g, dy):
    r = lax.rsqrt(jnp.mean(x * x, axis=-1, keepdims=True) + RMS_EPS)
    xh = x * r
    dg = jnp.sum(dy * xh, axis=0, keepdims=True)
    dxh = dy * g
    dx = r * (dxh - xh * jnp.mean(dxh * xh, axis=-1, keepdims=True))
    return dx, dg


def _matmul(name, a, b, mode, out_dtype, tm, tn, tk, dep=None):
    if mode == "nn":
        (m, k), (_, n) = a.shape, b.shape
    elif mode == "nt":
        (m, k), (n, _) = a.shape, b.shape
    else:
        (k, m), (_, n) = a.shape, b.shape
    tm, tn, tk = min(tm, m), min(tn, n), min(tk, k)
    assert m % tm == 0 and n % tn == 0 and k % tk == 0, (name, m, n, k, tm, tn, tk)
    nk = k // tk
    if mode == "tn":
        a_spec = pl.BlockSpec((tk, tm), lambda i, j, kk: (kk, i))
    else:
        a_spec = pl.BlockSpec((tm, tk), lambda i, j, kk: (i, kk))
    if mode == "nt":
        b_spec = pl.BlockSpec((tn, tk), lambda i, j, kk: (j, kk))
    else:
        b_spec = pl.BlockSpec((tk, tn), lambda i, j, kk: (kk, j))
    dot = {"nn": _dot, "nt": _dot_nt, "tn": _dot_tn}[mode]
    use_scratch = nk > 1 and out_dtype != F32

    def body(a_ref, b_ref, o_ref, *scratch):
        if nk == 1 and mode != "tn":
            rows = min(ROW_CHUNK, tm)
            bb = b_ref[...].astype(BF16)
            for r0 in range(0, tm, rows):
                o_ref[r0:r0 + rows, :] = dot(a_ref[r0:r0 + rows, :].astype(BF16), bb).astype(out_dtype)
            return
        prod = dot(a_ref[...].astype(BF16), b_ref[...].astype(BF16))
        if nk == 1:
            o_ref[...] = prod.astype(out_dtype)
            return
        acc = scratch[0] if use_scratch else o_ref
        kk = pl.program_id(2)

        @pl.when(kk == 0)
        def _():
            acc[...] = prod

        @pl.when(kk > 0)
        def _():
            acc[...] += prod

        if use_scratch:
            @pl.when(kk == nk - 1)
            def _():
                o_ref[...] = acc[...].astype(out_dtype)

    dep_specs, dep_ops = _dep_args(dep)
    return pl.pallas_call(
        _after(body, 2, dep),
        name=name,
        out_shape=jax.ShapeDtypeStruct((m, n), out_dtype),
        grid=(m // tm, n // tn, nk),
        in_specs=[a_spec, b_spec] + dep_specs,
        out_specs=pl.BlockSpec((tm, tn), lambda i, j, kk: (i, j)),
        scratch_shapes=[pltpu.VMEM((tm, tn), F32)] if use_scratch else [],
        compiler_params=_cparams("parallel", "parallel", "arbitrary"),
    )(a, b, *dep_ops)


def _norm_fwd(name, x, g, tm):
    t, d = x.shape
    tm = min(tm, t)

    def body(x_ref, g_ref, h_ref):
        h_ref[...] = _rms_fwd(x_ref[...], g_ref[...]).astype(BF16)

    return pl.pallas_call(
        body, name=name, out_shape=jax.ShapeDtypeStruct((t, d), BF16), grid=(t // tm,),
        in_specs=[pl.BlockSpec((tm, d), lambda i: (i, 0)), pl.BlockSpec((1, d), lambda i: (0, 0))],
        out_specs=pl.BlockSpec((tm, d), lambda i: (i, 0)),
        compiler_params=_cparams("parallel"),
    )(x, g)


def _split3(x):
    hi = x.astype(BF16)
    r1 = x - hi.astype(F32)
    mid = r1.astype(BF16)
    lo = (r1 - mid.astype(F32)).astype(BF16)
    return hi, mid, lo


def _tri_dot(tri, x):
    hi, mid, lo = _split3(x)
    return _dot(tri, hi) + _dot(tri, mid) + _dot(tri, lo)


def _forget_fwd(h, wf, bf, seq):
    t, d = h.shape
    cb = min(256, seq)

    def body(h_ref, wf_ref, bf_ref, fl_ref, fc_ref):
        fl = _dot(h_ref[...], wf_ref[...])
        fl_ref[...] = fl
        xx = fl + bf_ref[...]
        lf = jnp.minimum(xx, 0.0) - jnp.log(1.0 + jnp.exp(-jnp.abs(xx)))
        ri = lax.broadcasted_iota(jnp.int32, (cb, cb), 0)
        ci = lax.broadcasted_iota(jnp.int32, (cb, cb), 1)
        tri = (ri >= ci).astype(BF16)
        carry = jnp.zeros((1, LANES), F32)
        for blk in range(seq // cb):
            cs = _tri_dot(tri, lf[blk * cb:(blk + 1) * cb]) + carry
            fc_ref[blk * cb:(blk + 1) * cb, :] = cs
            carry = cs[cb - 1:cb, :]

    return pl.pallas_call(
        body, name="forget_fwd",
        out_shape=(jax.ShapeDtypeStruct((t, LANES), F32), jax.ShapeDtypeStruct((t, LANES), F32)),
        grid=(t // seq,),
        in_specs=[pl.BlockSpec((seq, d), lambda b: (b, 0)), pl.BlockSpec((d, LANES), lambda b: (0, 0)),
                  pl.BlockSpec((1, LANES), lambda b: (0, 0))],
        out_specs=(pl.BlockSpec((seq, LANES), lambda b: (b, 0)), pl.BlockSpec((seq, LANES), lambda b: (b, 0))),
        compiler_params=_cparams("parallel"),
    )(h, wf, bf)


def _pool_fwd(u, mix, scale, seq):
    t = u.shape[0]

    def body(u_ref, mix_ref, sc_ref, p_ref, ps_ref):
        tpos = lax.broadcasted_iota(jnp.int32, (seq, POOL_GROUP_DIM), 0)
        for g in range(POOL_GROUPS):
            sl = slice(g * POOL_GROUP_DIM, (g + 1) * POOL_GROUP_DIM)
            ug = u_ref[:, sl]
            s = ug
            for lvl in range(g + 1):
                d = 2 ** lvl
                s = s + jnp.where(tpos >= d, pltpu.roll(s, d, 0), 0.0)
            cnt = jnp.minimum(tpos + 1, POOL_WINDOWS[g]).astype(F32)
            pb = (s / cnt - ug).astype(BF16)
            p_ref[:, sl] = pb
            ps_ref[:, sl] = (_dot(pb, mix_ref[g]) * sc_ref[:, sl]).astype(BF16)

    return pl.pallas_call(
        body, name="pool_fwd",
        out_shape=(jax.ShapeDtypeStruct((t, POOL_WIDTH), BF16), jax.ShapeDtypeStruct((t, POOL_WIDTH), BF16)),
        grid=(t // seq,),
        in_specs=[pl.BlockSpec((seq, POOL_WIDTH), lambda b: (b, 0)),
                  pl.BlockSpec((POOL_GROUPS, POOL_GROUP_DIM, POOL_GROUP_DIM), lambda b: (0, 0, 0)),
                  pl.BlockSpec((1, POOL_WIDTH), lambda b: (0, 0))],
        out_specs=(pl.BlockSpec((seq, POOL_WIDTH), lambda b: (b, 0)), pl.BlockSpec((seq, POOL_WIDTH), lambda b: (b, 0))),
        compiler_params=_cparams("parallel"),
    )(u, mix, scale)


def _aug_constants():
    w = N_HEADS * LANES
    rows = jnp.arange(3 * LANES)
    piece, head = rows // LANES, rows % LANES
    cols = jnp.arange(w)
    live = (head < N_HEADS)[:, None]
    pq = (live & (cols[None, :] == (head * LANES + HEAD_DIM + piece)[:, None])).astype(BF16)
    pk = -(live & (cols[None, :] == (head * LANES + HEAD_DIM + 3 + piece)[:, None])).astype(BF16)
    lane = cols % LANES
    oq = ((lane >= HEAD_DIM + 3) & (lane < HEAD_DIM + 6)).astype(F32)[None, :]
    ok = ((lane >= HEAD_DIM) & (lane < HEAD_DIM + 3)).astype(F32)[None, :]
    return pq, pk, oq, ok


def _head_blocks(wt):
    d = wt.shape[1]
    return jnp.pad(wt.reshape(N_HEADS, HEAD_DIM, d), ((0, 0), (0, LANES - HEAD_DIM), (0, 0))).reshape(N_HEADS * LANES, d)


def _attn_prep(h, wq, wk, wv, fcum, tm):
    t, d = h.shape
    tm = min(tm, t)
    rows = min(ROW_CHUNK, tm)
    w = N_HEADS * LANES
    pq, pk, oq, ok = _aug_constants()

    def body(h_ref, wq_ref, wk_ref, wv_ref, f_ref, pq_ref, pk_ref, oq_ref, ok_ref, qa_ref, ka_ref, v_ref):
        for r0 in range(0, tm, rows):
            rs = slice(r0, r0 + rows)
            hh = h_ref[rs, :]
            fs = jnp.concatenate(_split3(f_ref[rs, :]), axis=1)
            q = _dot_nt(hh, wq_ref[...]).astype(BF16).astype(F32) * ATTN_SCALE
            qa_ref[rs, :] = (q + _dot(fs, pq_ref[...]) + oq_ref[...]).astype(BF16)
            k = _dot_nt(hh, wk_ref[...]).astype(BF16).astype(F32)
            ka_ref[rs, :] = (k + _dot(fs, pk_ref[...]) + ok_ref[...]).astype(BF16)
            v_ref[rs, :] = _dot_nt(hh, wv_ref[...]).astype(BF16)

    row = lambda n: pl.BlockSpec((tm, n), lambda i: (i, 0))
    full = lambda a: pl.BlockSpec(a.shape, lambda i: (0, 0))
    return pl.pallas_call(
        body, name="attn_prep",
        out_shape=(jax.ShapeDtypeStruct((t, w), BF16), jax.ShapeDtypeStruct((t, w), BF16),
                   jax.ShapeDtypeStruct((t, ATTN_WIDTH), BF16)),
        grid=(t // tm,),
        in_specs=[row(d), full(wq), full(wk), full(wv), row(LANES), full(pq), full(pk), full(oq), full(ok)],
        out_specs=(row(w), row(w), row(ATTN_WIDTH)),
        compiler_params=_cparams("parallel"),
    )(h, wq, wk, wv, fcum, pq, pk, oq, ok)


def _fold_lanes(x, op):
    out = x[:, :LANES]
    for g in range(1, x.shape[1] // LANES):
        out = op(out, x[:, g * LANES:(g + 1) * LANES])
    return out


def _causal_sweep(i, tile, carry):
    def quad(jj, c):
        for u in range(4):
            c = tile(4 * jj + u, c, False)
        return c

    carry = lax.fori_loop(0, i // 4, quad, carry)
    base = 4 * (i // 4)
    carry = lax.cond(i % 4 >= 2, lambda c: tile(base + 1, tile(base, c, False), False), lambda c: c, carry)
    return lax.cond(i % 2 == 1, lambda c: tile(i, tile(i - 1, c, False), True), lambda c: tile(i, c, True), carry)


def _attn_fwd(qa, ka, v, seq, tq, dep=None):
    t = qa.shape[0]
    nq = seq // tq
    hp_n = N_HEADS // 2
    heads = [slice(e * LANES, (e + 1) * LANES) for e in range(2)]

    def body(q_ref, k_ref, v_ref, o_ref, lse_ref, s_buf):
        i = pl.program_id(2)
        diag_ok = lax.broadcasted_iota(jnp.int32, (tq, tq), 0) >= lax.broadcasted_iota(jnp.int32, (tq, tq), 1)
        qs = [q_ref[:, hl] for hl in heads]

        def sweep1(j, mxs, diagonal):
            r0 = pl.multiple_of(j * tq, tq)
            out = []
            for e, hl in enumerate(heads):
                s = _dot_nt(qs[e], k_ref[pl.ds(r0, tq), hl])
                if diagonal:
                    s = jnp.where(diag_ok, s, NEG_BIG)
                s_buf[e, j] = s
                out.append(jnp.maximum(mxs[e], _fold_lanes(s, jnp.maximum)))
            return tuple(out)

        mxs = _causal_sweep(i, sweep1, (jnp.full((tq, LANES), NEG_BIG, F32),) * 2)
        ms = [jnp.max(mx, axis=1, keepdims=True) for mx in mxs]

        def sweep2(j, carry, diagonal):
            r0 = pl.multiple_of(j * tq, tq)
            vv = v_ref[pl.ds(r0, tq), :]
            out = []
            for e in range(2):
                p = jnp.exp(s_buf[e, j] - ms[e])
                out += [carry[2 * e] + _fold_lanes(p, jnp.add), carry[2 * e + 1] + _dot(p.astype(BF16), vv)]
            return tuple(out)

        res = _causal_sweep(i, sweep2, (jnp.zeros((tq, LANES), F32),) * 4)
        outs = []
        for e in range(2):
            l = jnp.sum(res[2 * e], axis=1, keepdims=True)
            outs.append(res[2 * e + 1] / l)
            lse_ref[:, e:e + 1] = ms[e] + jnp.log(l)
        lane = lax.broadcasted_iota(jnp.int32, (tq, LANES), 1)
        o_ref[...] = jnp.where(lane < HEAD_DIM, outs[0], outs[1])

    dep_specs, dep_ops = _dep_args(dep)
    return pl.pallas_call(
        _after(body, 3, dep), name="attn_fwd",
        out_shape=(jax.ShapeDtypeStruct((t, ATTN_WIDTH), F32), jax.ShapeDtypeStruct((hp_n, t, 2), F32)),
        grid=(t // seq, hp_n, nq),
        in_specs=[pl.BlockSpec((tq, 2 * LANES), lambda b, hp, i: (b * nq + i, hp)),
                  pl.BlockSpec((seq, 2 * LANES), lambda b, hp, i: (b, hp)),
                  pl.BlockSpec((seq, LANES), lambda b, hp, i: (b, hp))] + dep_specs,
        out_specs=(pl.BlockSpec((tq, LANES), lambda b, hp, i: (b * nq + i, hp)),
                   pl.BlockSpec((None, tq, 2), lambda b, hp, i: (hp, b * nq + i, 0))),
        scratch_shapes=[pltpu.VMEM((2, nq, tq, tq), F32)],
        compiler_params=_cparams("parallel", "parallel", "arbitrary"),
    )(qa, ka, v, *dep_ops)


def _merge_fwd(x, ps, o, g2, wpo, wao, wout, tm):
    t, d = x.shape
    tm = min(tm, t)
    rows = min(ROW_CHUNK, tm)

    def body(x_ref, ps_ref, o_ref, gp_ref, ga_ref, wpo_ref, wao_ref, wout_ref, mg_ref, x1_ref):
        for r0 in range(0, tm, rows):
            rs = slice(r0, r0 + rows)
            py = _dot(ps_ref[rs, :], wpo_ref[...])
            ay = _dot(o_ref[rs, :].astype(BF16), wao_ref[...])
            mb = (_sigmoid(gp_ref[rs, :].astype(F32)) * py + _sigmoid(ga_ref[rs, :].astype(F32)) * ay).astype(BF16)
            mg_ref[rs, :] = mb
            x1_ref[rs, :] = x_ref[rs, :] + _dot(mb, wout_ref[...])

    row = lambda w: pl.BlockSpec((tm, w), lambda i: (i, 0))
    full = lambda a: pl.BlockSpec(a.shape, lambda i: (0, 0))
    return pl.pallas_call(
        body, name="merge_fwd",
        out_shape=(jax.ShapeDtypeStruct((t, d), BF16), jax.ShapeDtypeStruct((t, d), F32)),
        grid=(t // tm,),
        in_specs=[row(d), row(POOL_WIDTH), row(ATTN_WIDTH), pl.BlockSpec((tm, d), lambda i: (i, 0)),
                  pl.BlockSpec((tm, d), lambda i: (i, 1)), full(wpo), full(wao), full(wout)],
        out_specs=(row(d), row(d)),
        compiler_params=_cparams("parallel"),
    )(x, ps, o, g2, g2, wpo, wao, wout)


def _ffn_fwd(x1, g, wg, wu, wd, tm, tf):
    t, d = x1.shape
    f = wg.shape[0]
    tm = min(tm, t)
    nf = f // tf
    rows = min(512, tm)

    def body(x1_ref, g_ref, wg_ref, wu_ref, wd_ref, h2_ref, gt_ref, up_ref, act_ref, x2_ref):
        j = pl.program_id(1)

        @pl.when(j == 0)
        def _():
            h2_ref[...] = _rms_fwd(x1_ref[...], g_ref[...]).astype(BF16)

            x2_ref[...] = x1_ref[...]

        for r0 in range(0, tm, rows):
            rs = slice(r0, r0 + rows)
            h2 = h2_ref[rs, :]
            gt = _dot_nt(h2, wg_ref[...])
            up = _dot_nt(h2, wu_ref[...])
            sg = _sigmoid(gt)
            silu = gt * sg
            act = (silu * up).astype(BF16)
            gt_ref[rs, :] = (up * (sg * (1.0 + gt * (1.0 - sg)))).astype(BF16)
            up_ref[rs, :] = silu.astype(BF16)
            act_ref[rs, :] = act
            x2_ref[rs, :] += _dot(act, wd_ref[...])

    return pl.pallas_call(
        body, name="ffn_fwd",
        out_shape=(jax.ShapeDtypeStruct((t, d), BF16), jax.ShapeDtypeStruct((t, f), BF16),
                   jax.ShapeDtypeStruct((t, f), BF16), jax.ShapeDtypeStruct((t, f), BF16),
                   jax.ShapeDtypeStruct((t, d), F32)),
        grid=(t // tm, nf),
        in_specs=[pl.BlockSpec((tm, d), lambda i, j: (i, 0)), pl.BlockSpec((1, d), lambda i, j: (0, 0)),
                  pl.BlockSpec((tf, d), lambda i, j: (j, 0)), pl.BlockSpec((tf, d), lambda i, j: (j, 0)),
                  pl.BlockSpec((tf, d), lambda i, j: (j, 0))],
        out_specs=(pl.BlockSpec((tm, d), lambda i, j: (i, 0)), pl.BlockSpec((tm, tf), lambda i, j: (i, j)),
                   pl.BlockSpec((tm, tf), lambda i, j: (i, j)), pl.BlockSpec((tm, tf), lambda i, j: (i, j)),
                   pl.BlockSpec((tm, d), lambda i, j: (i, 0))),
        compiler_params=_cparams("parallel", "arbitrary"),
    )(x1, g, wg, wu, wd)


def _final_fwd_bwd(x2, target, g, tm):
    t, d = x2.shape
    tm = min(tm, t)

    def body(x_ref, t_ref, g_ref, loss_ref, dx_ref, dg_ref):
        i = pl.program_id(0)
        x = x_ref[...]
        gg = g_ref[...]
        err = _rms_fwd(x, gg) - t_ref[...]
        part = 0.5 * jnp.sum(jnp.mean(err * err, axis=-1, keepdims=True), axis=0, keepdims=True)
        dx, dg = _rms_bwd(x, gg, err * (1.0 / d))
        dx_ref[...] = dx

        @pl.when(i == 0)
        def _():
            loss_ref[...] = jnp.zeros_like(loss_ref)
            dg_ref[...] = jnp.zeros_like(dg_ref)

        loss_ref[...] += jnp.broadcast_to(part, loss_ref.shape)
        dg_ref[...] += dg

    return pl.pallas_call(
        body, name="final_fwd_bwd",
        out_shape=(jax.ShapeDtypeStruct((1, LANES), F32), jax.ShapeDtypeStruct((t, d), F32),
                   jax.ShapeDtypeStruct((1, d), F32)),
        grid=(t // tm,),
        in_specs=[pl.BlockSpec((tm, d), lambda i: (i, 0)), pl.BlockSpec((tm, d), lambda i: (i, 0)),
                  pl.BlockSpec((1, d), lambda i: (0, 0))],
        out_specs=(pl.BlockSpec((1, LANES), lambda i: (0, 0)), pl.BlockSpec((tm, d), lambda i: (i, 0)),
                   pl.BlockSpec((1, d), lambda i: (0, 0))),
        compiler_params=_cparams("arbitrary"),
    )(x2, target, g)


def _ffn_bwd(dx2, x1, g, gt, up, wg, wu, wd, tm, tf):
    t, d = dx2.shape
    f = gt.shape[1]
    tm = min(tm, t)
    nf = f // tf
    wgu = jnp.concatenate([wg.reshape(nf, tf, d), wu.reshape(nf, tf, d)], axis=1).reshape(2 * f, d)
    rows = min(256, tm)

    def body(dx2_ref, x1_ref, g_ref, gt_ref, up_ref, wgu_ref, wd_ref, dgt_ref, dup_ref, dx1_ref, dg_ref, acc_ref,
             dxb_ref):
        i, j = pl.program_id(0), pl.program_id(1)

        @pl.when(j == 0)
        def _():
            dxb_ref[...] = dx2_ref[...].astype(BF16)
            acc_ref[...] = jnp.zeros_like(acc_ref)

        for r0 in range(0, tm, rows):
            rs = slice(r0, r0 + rows)
            dact = _dot_nt(dxb_ref[rs, :], wd_ref[...])
            dgt = (dact * gt_ref[rs, :].astype(F32)).astype(BF16)
            dup = (dact * up_ref[rs, :].astype(F32)).astype(BF16)
            dgt_ref[rs, :] = dgt
            dup_ref[rs, :] = dup
            acc_ref[rs, :] += _dot(jnp.concatenate([dgt, dup], axis=1), wgu_ref[...])

        @pl.when(jnp.logical_and(i == 0, j == 0))
        def _():
            dg_ref[...] = jnp.zeros_like(dg_ref)

        @pl.when(j == nf - 1)
        def _():
            dxn, dg = _rms_bwd(x1_ref[...], g_ref[...], acc_ref[...])
            dx1_ref[...] = dx2_ref[...] + dxn
            dg_ref[...] += dg

    return pl.pallas_call(
        body, name="ffn_bwd",
        out_shape=(jax.ShapeDtypeStruct((t, f), BF16), jax.ShapeDtypeStruct((t, f), BF16),
                   jax.ShapeDtypeStruct((t, d), F32), jax.ShapeDtypeStruct((1, d), F32)),
        grid=(t // tm, nf),
        in_specs=[pl.BlockSpec((tm, d), lambda i, j: (i, 0)), pl.BlockSpec((tm, d), lambda i, j: (i, 0)),
                  pl.BlockSpec((1, d), lambda i, j: (0, 0)),
                  pl.BlockSpec((tm, tf), lambda i, j: (i, j)), pl.BlockSpec((tm, tf), lambda i, j: (i, j)),
                  pl.BlockSpec((2 * tf, d), lambda i, j: (j, 0)), pl.BlockSpec((tf, d), lambda i, j: (j, 0))],
        out_specs=(pl.BlockSpec((tm, tf), lambda i, j: (i, j)), pl.BlockSpec((tm, tf), lambda i, j: (i, j)),
                   pl.BlockSpec((tm, d), lambda i, j: (i, 0)), pl.BlockSpec((1, d), lambda i, j: (0, 0))),
        scratch_shapes=[pltpu.VMEM((tm, d), F32), pltpu.VMEM((tm, d), BF16)],
        compiler_params=_cparams("arbitrary", "arbitrary"),
    )(dx2, x1, g, gt, up, wgu, wd)


def _merge_bwd(dx1, ps, o, g2, wpo, wao, wout, tm, dep=None):
    t, d = dx1.shape
    tm = min(tm, t)
    rows = min(ROW_CHUNK, tm)

    def body(dx1_ref, ps_ref, o_ref, gp_ref, ga_ref, wpo_ref, wao_ref, wout_ref, dpy_ref, day_ref, dg2_ref, dps_ref, da_ref):
        for r0 in range(0, tm, rows):
            rs = slice(r0, r0 + rows)
            dm = _dot_nt(dx1_ref[rs, :].astype(BF16), wout_ref[...])
            py = _dot(ps_ref[rs, :], wpo_ref[...])
            ay = _dot(o_ref[rs, :].astype(BF16), wao_ref[...])
            sp = _sigmoid(gp_ref[rs, :].astype(F32))
            sa = _sigmoid(ga_ref[rs, :].astype(F32))
            dpy = (dm * sp).astype(BF16)
            day = (dm * sa).astype(BF16)
            dpy_ref[rs, :] = dpy
            day_ref[rs, :] = day
            dg2_ref[rs, :d] = (dm * py * (sp * (1.0 - sp))).astype(BF16)
            dg2_ref[rs, d:] = (dm * ay * (sa * (1.0 - sa))).astype(BF16)
            dps_ref[rs, :] = _dot_nt(dpy, wpo_ref[...])
            da_ref[rs, :] = _dot_nt(day, wao_ref[...]).astype(BF16)

    row = lambda w: pl.BlockSpec((tm, w), lambda i: (i, 0))
    full = lambda a: pl.BlockSpec(a.shape, lambda i: (0, 0))
    dep_specs, dep_ops = _dep_args(dep)
    return pl.pallas_call(
        _after(body, 8, dep), name="merge_bwd",
        out_shape=(jax.ShapeDtypeStruct((t, d), BF16), jax.ShapeDtypeStruct((t, d), BF16),
                   jax.ShapeDtypeStruct((t, 2 * d), BF16), jax.ShapeDtypeStruct((t, POOL_WIDTH), F32),
                   jax.ShapeDtypeStruct((t, ATTN_WIDTH), BF16)),
        grid=(t // tm,),
        in_specs=[row(d), row(POOL_WIDTH), row(ATTN_WIDTH), pl.BlockSpec((tm, d), lambda i: (i, 0)),
                  pl.BlockSpec((tm, d), lambda i: (i, 1)), full(wpo), full(wao), full(wout)] + dep_specs,
        out_specs=(row(d), row(d), row(2 * d), row(POOL_WIDTH), row(ATTN_WIDTH)),
        compiler_params=_cparams("parallel"),
    )(dx1, ps, o, g2, g2, wpo, wao, wout, *dep_ops)


def _attn_bwd(qa, ka, v, do, lse4, seq, tq, dep=None):
    t = qa.shape[0]
    nq = seq // tq
    hp_n = N_HEADS // 2
    heads = [slice(e * LANES, (e + 1) * LANES) for e in range(2)]

    def body(q_ref, k_ref, v_ref, do_ref, lse_ref, dq_ref, dk_ref, dv_ref, dfr_ref, dk_acc, dv_acc, p_buf, dp_buf):
        diag_ok = lax.broadcasted_iota(jnp.int32, (tq, tq), 0) >= lax.broadcasted_iota(jnp.int32, (tq, tq), 1)
        lane_q = lax.broadcasted_iota(jnp.int32, (tq, LANES), 1)
        mine_q = [lane_q < HEAD_DIM, lane_q >= HEAD_DIM]
        dv_acc[...] = jnp.zeros_like(dv_acc)
        dk_acc[...] = jnp.zeros_like(dk_acc)
        dfr_ref[...] = jnp.zeros_like(dfr_ref)
        transposed = lambda a: a.astype(F32).T.astype(BF16)

        def q_step(i, _):
            q0 = pl.multiple_of(i * tq, tq)
            qs = [q_ref[pl.ds(q0, tq), hl] for hl in heads]
            dov = do_ref[pl.ds(q0, tq), :]
            dos = [jnp.where(mq, dov, jnp.zeros((), BF16)) for mq in mine_q]
            qts = [transposed(q) for q in qs]
            dots = [transposed(a) for a in dos]
            lss = [lse_ref[pl.ds(q0, tq), e:e + 1] for e in range(2)]

            def sweep1(j, dls, diagonal):
                r0 = pl.multiple_of(j * tq, tq)
                vv = v_ref[pl.ds(r0, tq), :]
                out = []
                for e, hl in enumerate(heads):
                    s = _dot_nt(qs[e], k_ref[pl.ds(r0, tq), hl])
                    if diagonal:
                        s = jnp.where(diag_ok, s, NEG_BIG)
                    p = jnp.exp(s - lss[e])
                    dp = _dot_nt(dos[e], vv)
                    p_buf[e, j] = p
                    dp_buf[e, j] = dp
                    dv_acc[j] += _dot(dots[e], p.astype(BF16))
                    out.append(dls[e] + _fold_lanes(p * dp, jnp.add))
                return tuple(out)

            dls = _causal_sweep(i, sweep1, (jnp.zeros((tq, LANES), F32),) * 2)
            dls = [jnp.sum(d, axis=1, keepdims=True) for d in dls]

            def sweep2(j, dqs, diagonal):
                r0 = pl.multiple_of(j * tq, tq)
                out = []
                for e, hl in enumerate(heads):
                    ds = p_buf[e, j] * (dp_buf[e, j] - dls[e])
                    dfr_ref[e, pl.ds(j, 1), :] += jnp.sum(ds, axis=0, keepdims=True)
                    dsb = ds.astype(BF16)
                    dk_acc[e, j] += _dot(qts[e], dsb)
                    out.append(dqs[e] + _dot(dsb, k_ref[pl.ds(r0, tq), hl]))
                return tuple(out)

            dqs = _causal_sweep(i, sweep2, (jnp.zeros((tq, LANES), F32),) * 2)
            dq = jnp.where(mine_q[0], dqs[0], pltpu.roll(dqs[1], HEAD_DIM, 1)) * ATTN_SCALE
            dq_ref[pl.ds(q0, tq), :] = dq.astype(BF16)
            return 0

        lax.fori_loop(0, nq, q_step, 0)
        for j in range(nq):
            rs = slice(j * tq, (j + 1) * tq)
            dk = jnp.where(mine_q[0], dk_acc[0, j].T, pltpu.roll(dk_acc[1, j].T, HEAD_DIM, 1))
            dk_ref[rs, :] = dk.astype(BF16)
            dv_ref[rs, :] = dv_acc[j].T.astype(BF16)

    wide = pl.BlockSpec((seq, 2 * LANES), lambda b, hp: (b, hp))
    col = pl.BlockSpec((seq, LANES), lambda b, hp: (b, hp))
    pair = pl.BlockSpec((None, seq, 2), lambda b, hp: (hp, b, 0))
    dep_specs, dep_ops = _dep_args(dep)
    return pl.pallas_call(
        _after(body, 5, dep), name="attn_bwd",
        out_shape=(jax.ShapeDtypeStruct((t, ATTN_WIDTH), BF16),) * 3 + (jax.ShapeDtypeStruct((N_HEADS, t // tq, tq), F32),),
        grid=(t // seq, hp_n),
        in_specs=[wide, wide, col, col, pair] + dep_specs,
        out_specs=(col, col, col, pl.BlockSpec((2, nq, tq), lambda b, hp: (hp, b, 0))),
        scratch_shapes=[pltpu.VMEM((2, nq, LANES, tq), F32), pltpu.VMEM((nq, LANES, tq), F32),
                        pltpu.VMEM((2, nq, tq, tq), F32), pltpu.VMEM((2, nq, tq, tq), F32)],
        compiler_params=_cparams("parallel", "arbitrary"),
    )(qa, ka, v, do, lse4, *dep_ops)


def _forget_bwd(dfc, fl, bf, seq):
    t = fl.shape[0]
    cb = min(256, seq)
    nb = seq // cb

    def body(dfc_ref, fl_ref, bf_ref, dfl_ref, db_ref):
        b = pl.program_id(0)
        ri = lax.broadcasted_iota(jnp.int32, (cb, cb), 0)
        ci = lax.broadcasted_iota(jnp.int32, (cb, cb), 1)
        tri = (ci >= ri).astype(BF16)
        carry = jnp.zeros((1, LANES), F32)
        dbs = jnp.zeros((1, LANES), F32)
        for blk in reversed(range(nb)):
            rs = slice(blk * cb, (blk + 1) * cb)
            dlf = _tri_dot(tri, -dfc_ref[rs, :]) + carry
            carry = dlf[0:1, :]
            dfl = dlf * _sigmoid(-(fl_ref[rs, :] + bf_ref[...]))
            dfl_ref[rs, :] = dfl.astype(BF16)
            dbs = dbs + jnp.sum(dfl, axis=0, keepdims=True)

        @pl.when(b == 0)
        def _():
            db_ref[...] = jnp.zeros_like(db_ref)

        db_ref[...] += dbs

    return pl.pallas_call(
        body, name="forget_bwd",
        out_shape=(jax.ShapeDtypeStruct((t, LANES), BF16), jax.ShapeDtypeStruct((1, LANES), F32)),
        grid=(t // seq,),
        in_specs=[pl.BlockSpec((seq, LANES), lambda b: (b, 0)), pl.BlockSpec((seq, LANES), lambda b: (b, 0)),
                  pl.BlockSpec((1, LANES), lambda b: (0, 0))],
        out_specs=(pl.BlockSpec((seq, LANES), lambda b: (b, 0)), pl.BlockSpec((1, LANES), lambda b: (0, 0))),
        compiler_params=_cparams("arbitrary"),
    )(dfc, fl, bf)


def _pool_bwd(dps, p, mix, scale, seq):
    t = dps.shape[0]

    def body(dps_ref, p_ref, mix_ref, sc_ref, du_ref, dmix_ref, dsc_ref):
        b = pl.program_id(0)

        @pl.when(b == 0)
        def _():
            dmix_ref[...] = jnp.zeros_like(dmix_ref)
            dsc_ref[...] = jnp.zeros_like(dsc_ref)

        tpos = lax.broadcasted_iota(jnp.int32, (seq, POOL_GROUP_DIM), 0)
        for g in range(POOL_GROUPS):
            sl = slice(g * POOL_GROUP_DIM, (g + 1) * POOL_GROUP_DIM)
            pb = p_ref[:, sl]
            dpsg = dps_ref[:, sl]
            pm = _dot(pb, mix_ref[g])
            dsc_ref[:, sl] += jnp.sum(dpsg * pm, axis=0, keepdims=True)
            dpm = (dpsg * sc_ref[:, sl]).astype(BF16)
            dmix_ref[g] += _dot_tn(pb, dpm)
            dp = _dot_nt(dpm, mix_ref[g])
            cnt = jnp.minimum(tpos + 1, POOL_WINDOWS[g]).astype(F32)
            s = dp / cnt
            for lvl in range(g + 1):
                d = 2 ** lvl
                s = s + jnp.where(tpos < seq - d, pltpu.roll(s, seq - d, 0), 0.0)
            du_ref[:, sl] = (s - dp).astype(BF16)

    return pl.pallas_call(
        body, name="pool_bwd",
        out_shape=(jax.ShapeDtypeStruct((t, POOL_WIDTH), BF16),
                   jax.ShapeDtypeStruct((POOL_GROUPS, POOL_GROUP_DIM, POOL_GROUP_DIM), F32),
                   jax.ShapeDtypeStruct((1, POOL_WIDTH), F32)),
        grid=(t // seq,),
        in_specs=[pl.BlockSpec((seq, POOL_WIDTH), lambda b: (b, 0)), pl.BlockSpec((seq, POOL_WIDTH), lambda b: (b, 0)),
                  pl.BlockSpec((POOL_GROUPS, POOL_GROUP_DIM, POOL_GROUP_DIM), lambda b: (0, 0, 0)),
                  pl.BlockSpec((1, POOL_WIDTH), lambda b: (0, 0))],
        out_specs=(pl.BlockSpec((seq, POOL_WIDTH), lambda b: (b, 0)),
                   pl.BlockSpec((POOL_GROUPS, POOL_GROUP_DIM, POOL_GROUP_DIM), lambda b: (0, 0, 0)),
                   pl.BlockSpec((1, POOL_WIDTH), lambda b: (0, 0))),
        compiler_params=_cparams("arbitrary"),
    )(dps, p, mix, scale)


def _in_bwd(du, dq, dk, dv, dg2, dfl, dx1, x, g, wu, wqkv, wg2, wft, tm):
    t, d = x.shape
    tm = min(tm, t)
    rows = min(ROW_CHUNK, tm)
    aw = ATTN_WIDTH

    def body(du_ref, dq_ref, dk_ref, dv_ref, dg2_ref, dfl_ref, dx1_ref, x_ref, g_ref, wu_ref, wqkv_ref, wg2_ref, wft_ref,
             dx_ref, dg_ref):
        i = pl.program_id(0)

        @pl.when(i == 0)
        def _():
            dg_ref[...] = jnp.zeros_like(dg_ref)

        for r0 in range(0, tm, rows):
            rs = slice(r0, r0 + rows)
            dh = _dot(du_ref[rs, :], wu_ref[...])
            dh += _dot(dq_ref[rs, :], wqkv_ref[0:aw, :])
            dh += _dot(dk_ref[rs, :], wqkv_ref[aw:2 * aw, :])
            dh += _dot(dv_ref[rs, :], wqkv_ref[2 * aw:3 * aw, :])
            dh += _dot(dg2_ref[rs, :], wg2_ref[...])
            dh += _dot(dfl_ref[rs, :], wft_ref[...])
            dxn, dg = _rms_bwd(x_ref[rs, :], g_ref[...], dh)
            dx_ref[rs, :] = dx1_ref[rs, :] + dxn
            dg_ref[...] += dg

    row = lambda w: pl.BlockSpec((tm, w), lambda i: (i, 0))
    full = lambda a: pl.BlockSpec(a.shape, lambda i: (0, 0))
    return pl.pallas_call(
        body, name="in_bwd",
        out_shape=(jax.ShapeDtypeStruct((t, d), F32), jax.ShapeDtypeStruct((1, d), F32)),
        grid=(t // tm,),
        in_specs=[row(POOL_WIDTH), row(aw), row(aw), row(aw), row(2 * d), row(LANES), row(d), row(d),
                  pl.BlockSpec((1, d), lambda i: (0, 0)), full(wu), full(wqkv), full(wg2), full(wft)],
        out_specs=(row(d), pl.BlockSpec((1, d), lambda i: (0, 0))),
        compiler_params=_cparams("arbitrary"),
    )(du, dq, dk, dv, dg2, dfl, dx1, x, g, wu, wqkv, wg2, wft)


def _position():
    return lax.axis_index("x"), lax.axis_index("y"), lax.axis_index("c")


def _remote(src, dst, send_sem, recv_sem, device):
    return pltpu.make_async_remote_copy(src_ref=src, dst_ref=dst, send_sem=send_sem, recv_sem=recv_sem,
                                        device_id=device, device_id_type=MESH)


HBM = pl.BlockSpec(memory_space=pltpu.HBM)
SEM = pl.BlockSpec(memory_space=pltpu.SEMAPHORE)
DATAFLOW = pltpu.SideEffectType.DATAFLOW_SIDE_EFFECTING


def _copies_start(name, arrays, plan, m, dep=None):
    n = len(arrays)
    arrays = [pltpu.with_memory_space_constraint(a, pltpu.HBM) for a in arrays]

    def body(*refs):
        ins, send_sem, recv_sem, token = refs[:n], refs[n], refs[n + 1], refs[2 * n + 2]
        for i, (src, dst, device, _) in enumerate(plan(ins, *_position())):
            _remote(src, dst, send_sem.at[i], recv_sem.at[i], device).start()
        token[...] = jnp.zeros_like(token)

    dep_specs, dep_ops = _dep_args(dep)
    outs = pl.pallas_call(
        _after(body, n, dep), name=name,
        out_shape=(pltpu.SemaphoreType.DMA((m,)), pltpu.SemaphoreType.DMA((m,)),
                   *[pltpu.HBM(a.shape, a.dtype) for a in arrays], jax.ShapeDtypeStruct((8, LANES), F32)),
        in_specs=[HBM] * n + dep_specs, out_specs=(SEM, SEM, *[HBM] * n, pl.BlockSpec(memory_space=pltpu.VMEM)),
        input_output_aliases={i: i + 2 for i in range(n)},
        compiler_params=pltpu.CompilerParams(has_side_effects=DATAFLOW),
    )(*arrays, *dep_ops)
    return (outs[0], outs[1]), list(outs[2:2 + n]), outs[2 + n]


def _copies_wait(name, sems, arrays, plan, after):
    n = len(arrays)
    afters = list(after) if isinstance(after, (list, tuple)) else [after]

    def body(*refs):
        ins, send_sem, recv_sem = refs[:n], refs[n], refs[n + 1]
        for i, (src, dst, device, landing) in enumerate(plan(ins, *_position())):
            _remote(src, dst, send_sem.at[i], recv_sem.at[i], device).wait_send()
            _remote(landing, landing, send_sem.at[i], recv_sem.at[i], device).wait_recv()

    outs = pl.pallas_call(
        body, name=name,
        out_shape=tuple(pltpu.HBM(a.shape, a.dtype) for a in arrays),
        in_specs=[HBM] * n + [SEM, SEM] + [ANY] * len(afters), out_specs=tuple([HBM] * n),
        input_output_aliases={i: i for i in range(n)},
        compiler_params=pltpu.CompilerParams(has_side_effects=DATAFLOW),
    )(*arrays, sems[0], sems[1], *afters)
    return list(outs)


def _tie(x, dep):
    for token in _dep_list(dep):
        x = x + token[0, 0]
    return x


def _other_chips(x, y):
    return [(1 - x, y), (x, 1 - y), (1 - x, 1 - y)]


def _gather_begin(tag, shards, token, column_halves=False):
    n = len(shards)
    lands = [lax.empty((N_CHIPS,) + s.shape, s.dtype) for s in shards]
    if column_halves:
        cols = lambda ref, h: pl.ds(pl.multiple_of(h * (ref.shape[-1] // 2), LANES), ref.shape[-1] // 2)
        mine = lambda ref, h: ref.at[:, cols(ref, h)]
        landed = lambda ref, chip, h: ref.at[chip, :, cols(ref, h)]
    else:
        mine = lambda ref, h: ref.at[h]
        landed = lambda ref, chip, h: ref.at[chip, h]

    def plan(refs, x, y, c):
        return [(mine(refs[k], c), landed(refs[n + k], 2 * x + y, c), (ox, oy, c), landed(refs[n + k], 2 * ox + oy, c))
                for k in range(n) for ox, oy in _other_chips(x, y)]

    sems, thru, token = _copies_start(f"gather_{tag}_ici_start", list(shards) + lands, plan, 3 * n, dep=token)
    return dict(tag=tag, n=n, plan=plan, sems=sems, arrays=thru, token=token, landed=landed)


def _gather_forward(st, after):
    n, tag, landed = st["n"], st["tag"], st["landed"]
    thru = _copies_wait(f"gather_{tag}_ici_wait", st["sems"], st["arrays"], st["plan"], after)

    def plan(refs, x, y, c):
        return [(landed(refs[k], 2 * ox + oy, c), landed(refs[k], 2 * ox + oy, c), (x, y, 1 - c),
                 landed(refs[k], 2 * ox + oy, 1 - c))
                for k in range(n) for ox, oy in _other_chips(x, y)]

    sems, lands, token = _copies_start(f"gather_{tag}_fwd_start", thru[n:], plan, 3 * n)
    return dict(tag=tag, n=n, plan=plan, sems=sems, arrays=lands, token=token, shards=thru[:n])


def _gather_end(st, after, merge=True):
    lands = _copies_wait(f"gather_{st['tag']}_fwd_wait", st["sems"], st["arrays"], st["plan"], after)
    if not merge:
        return lands, st["shards"]
    me = 2 * lax.axis_index("x") + lax.axis_index("y")
    return [lax.dynamic_update_index_in_dim(g, s, me, 0) for g, s in zip(lands, st["shards"])]


def _add_keep_give(name, pos, a, a_keep, a_give, b, b_keep, b_give, steps):
    r, c = b.shape[-2:]

    def spec(arr, fn):
        lead = arr.ndim - 2

        def index(i, p):
            idx = tuple(fn(i, p))
            return idx if len(idx) == arr.ndim else idx + (0, 0)

        return pl.BlockSpec((None,) * lead + (r, c), index)

    out_spec = pl.BlockSpec((None, r, c), lambda i, p: (i, 0, 0))

    def body(p_ref, ak_ref, bk_ref, ag_ref, bg_ref, keep_ref, give_ref):
        keep_ref[...] = ak_ref[...] + bk_ref[...].astype(F32)
        give_ref[...] = (ag_ref[...] + bg_ref[...].astype(F32)).astype(BF16)

    return pl.pallas_call(
        body, name=name,
        out_shape=(jax.ShapeDtypeStruct((steps, r, c), F32), jax.ShapeDtypeStruct((steps, r, c), BF16)),
        grid_spec=pltpu.PrefetchScalarGridSpec(
            num_scalar_prefetch=1, grid=(steps,),
            in_specs=[spec(a, a_keep), spec(b, b_keep), spec(a, a_give), spec(b, b_give)],
            out_specs=(out_spec, out_spec)),
        compiler_params=_cparams("parallel"),
    )(pos, a, b, a, b)


def _add_last(name, a, b):
    _, r, c = a.shape
    blk = pl.BlockSpec((None, r, c), lambda i: (0, 0, 0))

    def body(a_ref, b_ref, o_ref):
        o_ref[...] = a_ref[...] + b_ref[...].astype(F32)

    return pl.pallas_call(
        body, name=name, out_shape=jax.ShapeDtypeStruct((r, c), F32), grid=(1,), in_specs=[blk, blk],
        out_specs=pl.BlockSpec((r, c), lambda i: (0, 0)), compiler_params=_cparams("arbitrary"),
    )(a, b)


def _exchange_part(gives, lands, peer_fn):
    n = len(gives)

    def plan(refs, x, y, c):
        return [(refs[k], refs[n + k], peer_fn(x, y, c), refs[n + k]) for k in range(n)]

    return list(gives) + list(lands), plan, n


def _join_parts(parts):
    offsets, total = [], 0
    for arrays, _, _ in parts:
        offsets.append(total)
        total += len(arrays)

    def plan(refs, x, y, c):
        copies = []
        for (arrays, part_plan, _), off in zip(parts, offsets):
            copies += part_plan(refs[off:off + len(arrays)], x, y, c)
        return copies

    return [a for arrays, _, _ in parts for a in arrays], plan, sum(m for _, _, m in parts)


def _reduce_begin(tag, grads, column_halves=False):
    n = len(grads)
    if column_halves:
        half = lambda ref, j, h: ref.at[j, :, pl.ds(pl.multiple_of(h * (ref.shape[2] // 2), LANES), ref.shape[2] // 2)]
        lands = [lax.empty((N_CHIPS, g.shape[1], g.shape[2] // 2), F32) for g in grads]
    else:
        half = lambda ref, j, h: ref.at[j, h]
        lands = [lax.empty((N_CHIPS,) + g.shape[2:], F32) for g in grads]

    def plan(refs, x, y, c):
        return [(half(refs[k], j, 1 - c), refs[n + k].at[j], (x, y, 1 - c), refs[n + k].at[j])
                for k in range(n) for j in range(N_CHIPS)]

    return dict(tag=tag, n=n, stage="c", grads=list(grads), column_halves=column_halves,
                part=(list(grads) + lands, plan, N_CHIPS * n))


def _reduce_next(st, thru):
    tag, n, stage = st["tag"], st["n"], st["stage"]
    first, recv = thru[:n], thru[n:]
    x, y, c = _position()
    if stage == "c":
        pos = jnp.stack([c, x]).astype(jnp.int32)
        if st["column_halves"]:
            mine = lambda chip: (lambda i, p: (chip(p) + i, 0, p[0]))
        else:
            mine = lambda chip: (lambda i, p: (chip(p) + i, p[0]))
        sums = [_add_keep_give(
            f"rs{tag}_c_add{k}", pos,
            first[k], mine(lambda p: 2 * p[1]), mine(lambda p: 2 * (1 - p[1])),
            recv[k], lambda i, p: (2 * p[1] + i,), lambda i, p: (2 * (1 - p[1]) + i,), 2) for k in range(n)]
        lands = [lax.empty(s[1].shape, BF16) for s in sums]
        return dict(tag=tag, n=n, stage="x", keep=[s[0] for s in sums],
                    part=_exchange_part([s[1] for s in sums], lands, lambda x, y, c: (1 - x, y, c)))
    if stage == "x":
        pos = jnp.stack([y]).astype(jnp.int32)
        sums = [_add_keep_give(
            f"rs{tag}_x_add{k}", pos,
            st["keep"][k], lambda i, p: (p[0],), lambda i, p: (1 - p[0],),
            recv[k], lambda i, p: (p[0],), lambda i, p: (1 - p[0],), 1) for k in range(n)]
        lands = [lax.empty(s[1].shape, BF16) for s in sums]
        return dict(tag=tag, n=n, stage="y", keep=[s[0] for s in sums],
                    part=_exchange_part([s[1] for s in sums], lands, lambda x, y, c: (x, 1 - y, c)))
    if stage == "y":
        mine = [_add_last(f"rs{tag}_y_add{k}", st["keep"][k], recv[k]) for k in range(n)]
        lands = [lax.empty(m.shape, F32) for m in mine]
        return dict(tag=tag, n=n, stage="swap", part=_exchange_part(mine, lands, lambda x, y, c: (x, y, 1 - c)))
    return dict(tag=tag, done=list(zip(first, recv)))


def _small_begin(v):
    land = lax.empty((N_DEV,) + v.shape, F32)
    flips = [(fx, fy, fc) for fx in (0, 1) for fy in (0, 1) for fc in (0, 1)][1:]

    def plan(refs, x, y, c):
        copies = []
        for fx, fy, fc in flips:
            px, py, pc = (1 - x if fx else x), (1 - y if fy else y), (1 - c if fc else c)
            copies.append((refs[0], refs[1].at[4 * x + 2 * y + c], (px, py, pc), refs[1].at[4 * px + 2 * py + pc]))
        return copies

    return dict(tag="small", n=1, stage="swap", grads=[v], part=([v, land], plan, len(flips)))


def _small_sum(own, land):
    x, y, c = _position()
    me = jnp.stack([4 * x + 2 * y + c]).astype(jnp.int32)

    def body(me_ref, own_ref, land_ref, out_ref):
        term = lambda dev: jnp.where(me_ref[0] == dev, own_ref[...], land_ref[dev])
        acc = term(0)
        for dev in range(1, N_DEV):
            acc = acc + term(dev)
        out_ref[...] = acc

    return pl.pallas_call(
        body, name="small_sum", out_shape=jax.ShapeDtypeStruct(own.shape, F32),
        grid_spec=pltpu.PrefetchScalarGridSpec(
            num_scalar_prefetch=1, grid=(1,),
            in_specs=[pl.BlockSpec(own.shape, lambda i, m: (0, 0)), pl.BlockSpec(land.shape, lambda i, m: (0, 0, 0))],
            out_specs=pl.BlockSpec(own.shape, lambda i, m: (0, 0))),
        compiler_params=_cparams("arbitrary"),
    )(me, own, land)


def _adamw_update(w, gg, m, v):
    mn = ADAM_B1 * m + (1.0 - ADAM_B1) * gg
    vn = ADAM_B2 * v + (1.0 - ADAM_B2) * (gg * gg)
    m_hat = mn / (1.0 - ADAM_B1 ** ADAM_STEP)
    v_hat = vn / (1.0 - ADAM_B2 ** ADAM_STEP)
    return -ADAM_LR * (m_hat / (jnp.sqrt(v_hat) + ADAM_EPS) + ADAM_WD * w), mn, vn


def _adamw(name, w, g, m, v):
    def body(w_ref, g_ref, m_ref, v_ref, d_ref, mo_ref, vo_ref):
        d_ref[...], mo_ref[...], vo_ref[...] = _adamw_update(w_ref[...], g_ref[...], m_ref[...], v_ref[...])

    blk = pl.BlockSpec(w.shape, lambda i: (0, 0))
    return pl.pallas_call(
        body, name=name, out_shape=(jax.ShapeDtypeStruct(w.shape, F32),) * 3, grid=(1,),
        in_specs=[blk] * 4, out_specs=(blk,) * 3, compiler_params=_cparams("arbitrary"),
    )(w, g, m, v)


def _rows_to_bf16(name, w):
    r, _, c = w.shape

    def body(w_ref, o_ref):
        o_ref[...] = w_ref[:, 0, :].astype(BF16)

    return pl.pallas_call(
        body, name=name, out_shape=jax.ShapeDtypeStruct((r, c), BF16), grid=(1,),
        in_specs=[pl.BlockSpec((r, 1, c), lambda i: (0, 0, 0))], out_specs=pl.BlockSpec((r, c), lambda i: (0, 0)),
        compiler_params=_cparams("arbitrary"),
    )(w)


def _adamw_rows(name, pos_c, w, g_mine, g_other, m, v):
    r, _, c = w.shape
    ch = c // 2

    def body(p_ref, w_ref, gm_ref, go_ref, m_ref, v_ref, g_ref, d_ref, mo_ref, vo_ref):
        gg = jnp.where(pl.program_id(0) == p_ref[0], gm_ref[...], go_ref[...])
        dl, mn, vn = _adamw_update(w_ref[:, 0, :], gg, m_ref[:, 0, :], v_ref[:, 0, :])
        g_ref[:, 0, :] = gg
        d_ref[:, 0, :] = dl
        mo_ref[:, 0, :] = mn
        vo_ref[:, 0, :] = vn

    rows = pl.BlockSpec((r, 1, ch), lambda h, p: (0, 0, h))
    half = pl.BlockSpec((r, ch), lambda h, p: (0, 0))
    return pl.pallas_call(
        body, name=name, out_shape=(jax.ShapeDtypeStruct(w.shape, F32),) * 4,
        grid_spec=pltpu.PrefetchScalarGridSpec(
            num_scalar_prefetch=1, grid=(2,), in_specs=[rows, half, half, rows, rows], out_specs=(rows,) * 4),
        compiler_params=_cparams("parallel"),
    )(pos_c, w, g_mine, g_other, m, v)


def _adamw_halves(name, pos_c, w, g_mine, g_other, m, v, tr, dep=None):
    r, c = w.shape
    rh = r // 2
    tr = tr if rh % tr == 0 else rh
    nt = rh // tr

    def body(p_ref, w_ref, gm_ref, go_ref, m_ref, v_ref, g_ref, d_ref, mo_ref, vo_ref):
        gg = jnp.where(pl.program_id(0) == p_ref[0], gm_ref[...], go_ref[...])
        g_ref[...] = gg
        d_ref[...], mo_ref[...], vo_ref[...] = _adamw_update(w_ref[...], gg, m_ref[...], v_ref[...])

    full = pl.BlockSpec((tr, c), lambda h, i, p: (h * nt + i, 0))
    half = pl.BlockSpec((tr, c), lambda h, i, p: (i, 0))
    dep_specs, dep_ops = _dep_args(dep)
    return pl.pallas_call(
        _after(body, 6, dep), name=name, out_shape=(jax.ShapeDtypeStruct((r, c), F32),) * 4,
        grid_spec=pltpu.PrefetchScalarGridSpec(
            num_scalar_prefetch=1, grid=(2, nt),
            in_specs=[full, half, half, full, full] + dep_specs, out_specs=(full,) * 4),
        compiler_params=_cparams("parallel", "parallel"),
    )(pos_c, w, g_mine, g_other, m, v, *dep_ops)


def _col_sharded_to_comm(g):
    k, n = g.shape
    return g.reshape(2, k // 2, N_CHIPS, n // N_CHIPS).transpose(2, 0, 1, 3)


def _row_sharded_to_comm(g):
    r, c = g.shape
    return g.reshape(N_CHIPS, 2, r // (2 * N_CHIPS), c)


def _col_sharded_full(g):
    _, _, rh, c = g.shape
    return g.reshape(N_CHIPS, 2 * rh, c).transpose(1, 0, 2).reshape(2 * rh, N_CHIPS * c)


def _row_sharded_full(g):
    _, _, rh, c = g.shape
    return g.reshape(N_CHIPS * 2 * rh, c)


def _chip_rows(w3, start, stop, own=None, me=None):
    r = w3.shape[1]
    parts = []
    for chip in range(N_CHIPS):
        lo, hi = max(start - chip * r, 0), min(stop - chip * r, r)
        if lo < hi:
            part = w3[chip, lo:hi]
            parts.append(part if own is None else jnp.where(me == chip, own[lo:hi], part))
    return parts[0] if len(parts) == 1 else jnp.concatenate(parts, axis=0)


def _pack_small(g1, bfv, mix, scale, g2n, gf, extra=None):
    row8 = jnp.pad(bfv.reshape(1, N_HEADS), ((0, 0), (0, LANES - N_HEADS)))
    if extra is not None:
        row8 = row8 + jnp.pad(extra[:, :1], ((0, 0), (N_HEADS, LANES - N_HEADS - 1)))
    return jnp.concatenate([
        g1.reshape(8, LANES), jnp.pad(row8, ((0, 7), (0, 0))), mix.reshape(512, LANES),
        jnp.pad(scale.reshape(4, LANES), ((0, 4), (0, 0))), g2n.reshape(8, LANES), gf.reshape(8, LANES)], axis=0)


def _unpack_small(s, like):
    g1, bfv, mix, scale, g2n, gf = like
    return (s[0:8].reshape(g1.shape), s[8, :N_HEADS].reshape(bfv.shape), s[16:528].reshape(mix.shape),
            s[528:532].reshape(scale.shape), s[536:544].reshape(g2n.shape), s[544:552].reshape(gf.shape))


class _MeshLinks:
    def __init__(self, shards_in, shards_rest):
        self.gin = _gather_begin("in", shards_in, None, column_halves=True)
        self.grest = _gather_begin("rest", shards_rest, self.gin["token"])
        self.tokens = {"gather": self.grest["token"]}
        self.groups, self.flight, self.slot = {}, None, 0

    @property
    def token(self):
        return list(self.tokens.values())

    def tie(self, x):
        return _tie(x, self.token)

    def weights_in(self, after):
        st = _gather_forward(self.gin, after)
        (g,), (own,) = _gather_end(st, st["token"], merge=False)
        return g, own, 2 * lax.axis_index("x") + lax.axis_index("y")

    def rest_forward(self, after):
        self.grest = _gather_forward(self.grest, after)
        self.tokens["gather"] = self.grest["token"]

    def weights_rest(self, after):
        g = _gather_end(self.grest, after)
        del self.tokens["gather"]
        return [_col_sharded_full(g[0]), _col_sharded_full(g[1])] + [_row_sharded_full(a) for a in g[2:]]

    def advance(self, after, begin=()):
        slot = self.slot
        self.slot += 1
        if self.flight is not None:
            tags, sems, parts = self.flight
            arrays, plan, _ = _join_parts(parts)
            thru = _copies_wait(f"slot{slot}_wait", sems, arrays, plan, after)
            for tag, part in zip(tags, parts):
                self.groups[tag] = _reduce_next(self.groups[tag], thru[:len(part[0])])
                thru = thru[len(part[0]):]
        for st in begin:
            self.groups[st["tag"]] = st
        live = [(tag, st["part"]) for tag, st in self.groups.items() if "part" in st]
        self.flight = None
        self.tokens.pop("reduce", None)
        if live:
            arrays, plan, m = _join_parts([part for _, part in live])
            sems, thru, token = _copies_start(f"slot{slot}_start", arrays, plan, m)
            parts = []
            for _, (part_arrays, part_plan, part_m) in live:
                parts.append((thru[:len(part_arrays)], part_plan, part_m))
                thru = thru[len(part_arrays):]
            self.flight = ([tag for tag, _ in live], sems, parts)
            self.tokens["reduce"] = token

    def reduced(self, tag):
        return self.groups[tag]["done"]


class _NoLinks:
    token = None

    def __init__(self, w_in, rest):
        self.w_in, self.rest, self.grads = w_in, rest, {}

    def tie(self, x):
        return x

    def weights_in(self, after):
        return self.w_in, None, None

    def rest_forward(self, after):
        pass

    def weights_rest(self, after):
        return self.rest

    def advance(self, after, begin=()):
        for st in begin:
            self.grads[st["tag"]] = st["grads"]


def _local_step(links, x, target, seq, norm1_g, b_forget, pool_mix, pool_scale, norm2_g, norm_f_g):
    t, d = x.shape
    tq = min(256, seq)
    aw = ATTN_WIDTH
    o_q, o_f, o_g = POOL_WIDTH, POOL_WIDTH + 3 * aw, POOL_WIDTH + 3 * aw + N_HEADS
    bf = jnp.pad(b_forget, ((0, 0), (0, LANES - N_HEADS)))
    mixb = pool_mix.astype(BF16)

    h = _norm_fwd("norm1_fwd", x, links.tie(norm1_g), 512)
    w_in, own, me = links.weights_in(h)
    wu = _chip_rows(w_in, 0, o_q, own, me)
    wqkv = _chip_rows(w_in, o_q, o_f, own, me)
    wft = jnp.pad(_chip_rows(w_in, o_f, o_g, own, me), ((0, LANES - N_HEADS), (0, 0)))
    wg2 = _chip_rows(w_in, o_g, N_CHIPS * w_in.shape[1], own, me)
    wf = wft.T
    u = _matmul("mm_u", h, wu, "nt", F32, 1024, 512, d)
    g2 = _matmul("mm_gates", h, wg2, "nt", BF16, 1024, 1024, d)
    fl, fcum = _forget_fwd(h, wf, bf, seq)
    qa, ka, v = _attn_prep(h, _head_blocks(wqkv[:aw]), _head_blocks(wqkv[aw:2 * aw]), wqkv[2 * aw:], fcum, 1024)
    p, ps = _pool_fwd(u, mixb, pool_scale, seq)
    links.rest_forward([ps, qa, g2])
    o, lse = _attn_fwd(qa, ka, v, seq, tq, dep=links.token)
    w_pool_out, w_attn_out, w_out, w_ffn_gate, w_ffn_up, w_ffn_down = links.weights_rest(o)
    merged, x1 = _merge_fwd(x, ps, o, g2, w_pool_out, w_attn_out, w_out, 512)
    h2, gt, up, act, x2 = _ffn_fwd(x1, norm2_g, w_ffn_gate, w_ffn_up, w_ffn_down, 1024, 256)
    loss, dx2, d_gf = _final_fwd_bwd(x2, target, norm_f_g, 512)

    dgt, dup, dx1, d_g2n = _ffn_bwd(dx2, x1, norm2_g, gt, up, w_ffn_gate, w_ffn_up, w_ffn_down, 1024, 256)
    d_wd = _matmul("dw_down", act, dx2, "tn", F32, 1408, 1024, 1024)
    d_wg = _matmul("dw_gate", dgt, h2, "tn", F32, 1408, 1024, 1024)
    d_wu = _matmul("dw_up", dup, h2, "tn", F32, 1408, 1024, 1024)
    links.advance(None, begin=[_reduce_begin("a", [_row_sharded_to_comm(g) for g in (d_wg, d_wu, d_wd)])])
    dpy, day, dg2, dps, da = _merge_bwd(dx1, ps, o, g2, w_pool_out, w_attn_out, w_out, 512, dep=links.token)
    links.advance(dps)
    d_wout = _matmul("dw_out", merged, dx1, "tn", F32, 1024, 1024, 1024)
    d_wpo = _matmul("dw_pool_out", ps, dpy, "tn", F32, 512, 1024, 1024)
    d_wao = _matmul("dw_attn_out", o, day, "tn", F32, 512, 1024, 1024)
    dq, dk, dv, dfr = _attn_bwd(qa, ka, v, da, lse, seq, tq, dep=links.token)
    links.advance(dq, begin=[_reduce_begin(
        "m", [_col_sharded_to_comm(d_wpo), _col_sharded_to_comm(d_wao), _row_sharded_to_comm(d_wout)])])
    dfc = jnp.pad(dfr.reshape(N_HEADS, t).T, ((0, 0), (0, LANES - N_HEADS)))
    dfl, d_bf = _forget_bwd(dfc, fl, bf, seq)
    du, d_mix, d_scale = _pool_bwd(dps, p, mixb, links.tie(pool_scale), seq)
    d_wu_in = _matmul("dw_in_u", du, h, "tn", F32, 512, 1024, 1024)
    d_wq = _matmul("dw_in_q", dq, h, "tn", F32, 512, 1024, 1024)
    d_wk = _matmul("dw_in_k", dk, h, "tn", F32, 512, 1024, 1024)
    d_wv = _matmul("dw_in_v", dv, h, "tn", F32, 512, 1024, 1024)
    links.advance([d_wu_in, d_wq, d_wk, d_wv])
    d_wf = _matmul("dw_in_f", dfl, h, "tn", F32, LANES, 1024, 512)
    d_wg2 = _matmul("dw_in_gates", dg2, h, "tn", F32, 1024, 1024, 1024, dep=links.token)
    d_win = jnp.concatenate([d_wu_in, d_wq, d_wk, d_wv, d_wf[:N_HEADS], d_wg2], axis=0)
    comm_b = [d_win.reshape(N_CHIPS, d_win.shape[0] // N_CHIPS, d)]
    links.advance(comm_b, begin=[_reduce_begin("b", comm_b, column_halves=True)])
    dx, d_g1 = _in_bwd(du, dq, dk, dv, dg2, dfl, dx1, x, links.tie(norm1_g), wu, wqkv, wg2, wft, 512)
    small = (d_g1, d_bf[:, :N_HEADS], d_mix, d_scale, d_g2n, d_gf)
    return loss, dx, small


def kernel(x, norm1_g, w_in, b_forget, pool_mix, pool_scale, w_pool_out, w_attn_out, w_out, norm2_g, w_ffn_gate, w_ffn_up, w_ffn_down, norm_f_g, loss_target, m_norm1_g, m_w_in, m_b_forget, m_pool_mix, m_pool_scale, m_w_pool_out, m_w_attn_out, m_w_out, m_norm2_g, m_w_ffn_gate, m_w_ffn_up, m_w_ffn_down, m_norm_f_g, v_norm1_g, v_w_in, v_b_forget, v_pool_mix, v_pool_scale, v_w_pool_out, v_w_attn_out, v_w_out, v_norm2_g, v_w_ffn_gate, v_w_ffn_up, v_w_ffn_down, v_norm_f_g):
    nb, seq, d = x.shape
    group_a = ((w_ffn_gate, m_w_ffn_gate, v_w_ffn_gate, True, 9), (w_ffn_up, m_w_ffn_up, v_w_ffn_up, True, 10),
               (w_ffn_down, m_w_ffn_down, v_w_ffn_down, False, 11))
    group_m = ((w_pool_out, m_w_pool_out, v_w_pool_out, False, 5), (w_attn_out, m_w_attn_out, v_w_attn_out, False, 6),
               (w_out, m_w_out, v_w_out, False, 7))
    group_b = ((w_in, m_w_in, v_w_in, False, 1),)
    small_w = (norm1_g, b_forget, pool_mix, pool_scale, norm2_g, norm_f_g)
    small_m = (m_norm1_g, m_b_forget, m_pool_mix, m_pool_scale, m_norm2_g, m_norm_f_g)
    small_v = (v_norm1_g, v_b_forget, v_pool_mix, v_pool_scale, v_norm2_g, v_norm_f_g)
    small_pos = (0, 2, 3, 4, 8, 12)
    view = lambda a, tr: a[0].T if tr else a[0]
    unview = lambda a, tr, like: (a.T if tr else a).reshape(like.shape)

    def shard(w, tr):
        lw = view(w, tr).astype(BF16)
        return lw.reshape(2, lw.shape[0] // 2, lw.shape[1])

    cm = lambda a: jnp.transpose(a, (2, 0, 1))
    shard_in = _rows_to_bf16("w_in_to_bf16", cm(w_in))
    links = _MeshLinks([shard_in],
                       [shard(w_pool_out, False), shard(w_attn_out, False), shard(w_out, False),
                        shard(w_ffn_gate, True), shard(w_ffn_up, True), shard(w_ffn_down, False)])
    loss, dx, small_g = _local_step(
        links, x.reshape(nb * seq, d), loss_target.reshape(nb * seq, d), seq,
        norm1_g, b_forget, pool_mix[0], pool_scale, norm2_g, norm_f_g.reshape(1, d))

    grads, deltas, new_m, new_v = [None] * 13, [None] * 13, [None] * 13, [None] * 13
    pos_c = jnp.stack([lax.axis_index("c")]).astype(jnp.int32)

    def update(tag, group, dep):
        last = []
        for k, ((w, m, v, tr, pos), (mine, other)) in enumerate(zip(group, links.reduced(tag))):
            outs = _adamw_halves(f"adamw_{tag}{k}", pos_c, view(w, tr), mine, other, view(m, tr), view(v, tr), 256,
                                 dep=dep)
            grads[pos], deltas[pos], new_m[pos], new_v[pos] = (unview(a, tr, w) for a in outs)
            last.append(outs[1])
        return last

    links.advance(dx, begin=[_small_begin(_pack_small(*small_g, extra=loss))])
    last = update("a", group_a, links.token)
    links.advance(last)
    small_sum = _small_sum(*links.reduced("small")[0])
    last = update("m", group_m, links.token)
    loss_out = small_sum[8, N_HEADS]
    dl, mn, vn = _adamw("adamw_small", _pack_small(*small_w), small_sum * _small_mask(), _pack_small(*small_m),
                        _pack_small(*small_v))
    for pos, g, a, b, e in zip(small_pos, _unpack_small(small_sum, small_w), _unpack_small(dl, small_w),
                               _unpack_small(mn, small_w), _unpack_small(vn, small_w)):
        grads[pos], deltas[pos], new_m[pos], new_v[pos] = g, a, b, e
    links.advance(last + [dl])
    links.advance(links.token)
    (mine, other), = links.reduced("b")
    outs = _adamw_rows("adamw_b0", pos_c, cm(w_in), mine, other, cm(m_w_in), cm(v_w_in))
    grads[1], deltas[1], new_m[1], new_v[1] = (jnp.transpose(a, (1, 2, 0)) for a in outs)

    return (loss_out, dx.reshape(nb, seq, d), *grads, *deltas, *new_m, *new_v)


def _small_mask():
    rows = lax.broadcasted_iota(jnp.int32, (552, LANES), 0)
    lanes = lax.broadcasted_iota(jnp.int32, (552, LANES), 1)
    return jnp.where(jnp.logical_and(rows == 8, lanes == N_HEADS), 0.0, 1.0).astype(F32)
```

```python
import jax
import jax.numpy as jnp
from jax import lax
from jax.experimental import pallas as pl
from jax.experimental.pallas import tpu as pltpu

F32 = jnp.float32
BF16 = jnp.bfloat16

D_MODEL = 1024
POOL_WINDOWS = (2, 4, 8, 16)
POOL_GROUPS = 4
POOL_GROUP_DIM = 128
POOL_WIDTH = 512
HEAD_DIM = 64
N_HEADS = 8
ATTN_WIDTH = 512
D_FF = 2816
RMS_EPS = 1e-6
ATTN_SCALE = HEAD_DIM ** -0.5
NEG_BIG = -1e30

ADAM_LR = 0.001
ADAM_B1 = 0.9
ADAM_B2 = 0.999
ADAM_EPS = 1e-08
ADAM_WD = 0.01
ADAM_STEP = 10

LANES = 128
N_CHIPS = 4
N_DEV = 8
VMEM_LIMIT_V7X = 52 * 1024 * 1024
ROW_CHUNK = 256
MESH = pl.DeviceIdType.MESH
ANY = pl.BlockSpec(memory_space=pl.ANY)


def _cparams(*sem):
    return pltpu.CompilerParams(dimension_semantics=sem if sem else None, vmem_limit_bytes=VMEM_LIMIT_V7X)


def _dep_list(dep):
    return [] if dep is None else (list(dep) if isinstance(dep, (list, tuple)) else [dep])


def _after(body, n_in, dep):
    k = len(_dep_list(dep))
    if k == 0:
        return body

    def wrapped(*refs):
        body(*refs[:n_in], *refs[n_in + k:])

    return wrapped


def _dep_args(dep):
    deps = _dep_list(dep)
    return [ANY] * len(deps), deps


def _dot(a, b):
    return lax.dot_general(a, b, (((1,), (0,)), ((), ())), preferred_element_type=F32)


def _dot_nt(a, b):
    return lax.dot_general(a, b, (((1,), (1,)), ((), ())), preferred_element_type=F32)


def _dot_tn(a, b):
    return lax.dot_general(a, b, (((0,), (0,)), ((), ())), preferred_element_type=F32)


def _sigmoid(x):
    return jax.nn.sigmoid(x)


def _rms_fwd(x, g):
    r = lax.rsqrt(jnp.mean(x * x, axis=-1, keepdims=True) + RMS_EPS)
    return (x * r) * g


def _rms_bwd(x, g, dy):
    r = lax.rsqrt(jnp.mean(x * x, axis=-1, keepdims=True) + RMS_EPS)
    xh = x * r
    dg = jnp.sum(dy * xh, axis=0, keepdims=True)
    dxh = dy * g
    dx = r * (dxh - xh * jnp.mean(dxh * xh, axis=-1, keepdims=True))
    return dx, dg


def _matmul(name, a, b, mode, out_dtype, tm, tn, tk, dep=None):
    if mode == "nn":
        (m, k), (_, n) = a.shape, b.shape
    elif mode == "nt":
        (m, k), (n, _) = a.shape, b.shape
    else:
        (k, m), (_, n) = a.shape, b.shape
    tm, tn, tk = min(tm, m), min(tn, n), min(tk, k)
    assert m % tm == 0 and n % tn == 0 and k % tk == 0, (name, m, n, k, tm, tn, tk)
    nk = k // tk
    if mode == "tn":
        a_spec = pl.BlockSpec((tk, tm), lambda i, j, kk: (kk, i))
    else:
        a_spec = pl.BlockSpec((tm, tk), lambda i, j, kk: (i, kk))
    if mode == "nt":
        b_spec = pl.BlockSpec((tn, tk), lambda i, j, kk: (j, kk))
    else:
        b_spec = pl.BlockSpec((tk, tn), lambda i, j, kk: (kk, j))
    dot = {"nn": _dot, "nt": _dot_nt, "tn": _dot_tn}[mode]
    use_scratch = nk > 1 and out_dtype != F32

    def body(a_ref, b_ref, o_ref, *scratch):
        if nk == 1 and mode != "tn":
            rows = min(ROW_CHUNK, tm)
            bb = b_ref[...].astype(BF16)
            for r0 in range(0, tm, rows):
                o_ref[r0:r0 + rows, :] = dot(a_ref[r0:r0 + rows, :].astype(BF16), bb).astype(out_dtype)
            return
        prod = dot(a_ref[...].astype(BF16), b_ref[...].astype(BF16))
        if nk == 1:
            o_ref[...] = prod.astype(out_dtype)
            return
        acc = scratch[0] if use_scratch else o_ref
        kk = pl.program_id(2)

        @pl.when(kk == 0)
        def _():
            acc[...] = prod

        @pl.when(kk > 0)
        def _():
            acc[...] += prod

        if use_scratch:
            @pl.when(kk == nk - 1)
            def _():
                o_ref[...] = acc[...].astype(out_dtype)

    dep_specs, dep_ops = _dep_args(dep)
    return pl.pallas_call(
        _after(body, 2, dep),
        name=name,
        out_shape=jax.ShapeDtypeStruct((m, n), out_dtype),
        grid=(m // tm, n // tn, nk),
        in_specs=[a_spec, b_spec] + dep_specs,
        out_specs=pl.BlockSpec((tm, tn), lambda i, j, kk: (i, j)),
        scratch_shapes=[pltpu.VMEM((tm, tn), F32)] if use_scratch else [],
        compiler_params=_cparams("parallel", "parallel", "arbitrary"),
    )(a, b, *dep_ops)


def _norm_fwd(name, x, g, tm):
    t, d = x.shape
    tm = min(tm, t)

    def body(x_ref, g_ref, h_ref):
        h_ref[...] = _rms_fwd(x_ref[...], g_ref[...]).astype(BF16)

    return pl.pallas_call(
        body, name=name, out_shape=jax.ShapeDtypeStruct((t, d), BF16), grid=(t // tm,),
        in_specs=[pl.BlockSpec((tm, d), lambda i: (i, 0)), pl.BlockSpec((1, d), lambda i: (0, 0))],
        out_specs=pl.BlockSpec((tm, d), lambda i: (i, 0)),
        compiler_params=_cparams("parallel"),
    )(x, g)


def _split3(x):
    hi = x.astype(BF16)
    r1 = x - hi.astype(F32)
    mid = r1.astype(BF16)
    lo = (r1 - mid.astype(F32)).astype(BF16)
    return hi, mid, lo


def _tri_dot(tri, x):
    hi, mid, lo = _split3(x)
    return _dot(tri, hi) + _dot(tri, mid) + _dot(tri, lo)


def _forget_fwd(h, wf, bf, seq):
    t, d = h.shape
    cb = min(256, seq)

    def body(h_ref, wf_ref, bf_ref, fl_ref, fc_ref):
        fl = _dot(h_ref[...], wf_ref[...])
        fl_ref[...] = fl
        xx = fl + bf_ref[...]
        lf = jnp.minimum(xx, 0.0) - jnp.log(1.0 + jnp.exp(-jnp.abs(xx)))
        ri = lax.broadcasted_iota(jnp.int32, (cb, cb), 0)
        ci = lax.broadcasted_iota(jnp.int32, (cb, cb), 1)
        tri = (ri >= ci).astype(BF16)
        carry = jnp.zeros((1, LANES), F32)
        for blk in range(seq // cb):
            cs = _tri_dot(tri, lf[blk * cb:(blk + 1) * cb]) + carry
            fc_ref[blk * cb:(blk + 1) * cb, :] = cs
            carry = cs[cb - 1:cb, :]

    return pl.pallas_call(
        body, name="forget_fwd",
        out_shape=(jax.ShapeDtypeStruct((t, LANES), F32), jax.ShapeDtypeStruct((t, LANES), F32)),
        grid=(t // seq,),
        in_specs=[pl.BlockSpec((seq, d), lambda b: (b, 0)), pl.BlockSpec((d, LANES), lambda b: (0, 0)),
                  pl.BlockSpec((1, LANES), lambda b: (0, 0))],
        out_specs=(pl.BlockSpec((seq, LANES), lambda b: (b, 0)), pl.BlockSpec((seq, LANES), lambda b: (b, 0))),
        compiler_params=_cparams("parallel"),
    )(h, wf, bf)


def _pool_fwd(u, mix, scale, seq):
    t = u.shape[0]

    def body(u_ref, mix_ref, sc_ref, p_ref, ps_ref):
        tpos = lax.broadcasted_iota(jnp.int32, (seq, POOL_GROUP_DIM), 0)
        for g in range(POOL_GROUPS):
            sl = slice(g * POOL_GROUP_DIM, (g + 1) * POOL_GROUP_DIM)
            ug = u_ref[:, sl]
            s = ug
            for lvl in range(g + 1):
                d = 2 ** lvl
                s = s + jnp.where(tpos >= d, pltpu.roll(s, d, 0), 0.0)
            cnt = jnp.minimum(tpos + 1, POOL_WINDOWS[g]).astype(F32)
            pb = (s / cnt - ug).astype(BF16)
            p_ref[:, sl] = pb
            ps_ref[:, sl] = (_dot(pb, mix_ref[g]) * sc_ref[:, sl]).astype(BF16)

    return pl.pallas_call(
        body, name="pool_fwd",
        out_shape=(jax.ShapeDtypeStruct((t, POOL_WIDTH), BF16), jax.ShapeDtypeStruct((t, POOL_WIDTH), BF16)),
        grid=(t // seq,),
        in_specs=[pl.BlockSpec((seq, POOL_WIDTH), lambda b: (b, 0)),
                  pl.BlockSpec((POOL_GROUPS, POOL_GROUP_DIM, POOL_GROUP_DIM), lambda b: (0, 0, 0)),
                  pl.BlockSpec((1, POOL_WIDTH), lambda b: (0, 0))],
        out_specs=(pl.BlockSpec((seq, POOL_WIDTH), lambda b: (b, 0)), pl.BlockSpec((seq, POOL_WIDTH), lambda b: (b, 0))),
        compiler_params=_cparams("parallel"),
    )(u, mix, scale)


def _aug_constants():
    w = N_HEADS * LANES
    rows = jnp.arange(3 * LANES)
    piece, head = rows // LANES, rows % LANES
    cols = jnp.arange(w)
    live = (head < N_HEADS)[:, None]
    pq = (live & (cols[None, :] == (head * LANES + HEAD_DIM + piece)[:, None])).astype(BF16)
    pk = -(live & (cols[None, :] == (head * LANES + HEAD_DIM + 3 + piece)[:, None])).astype(BF16)
    lane = cols % LANES
    oq = ((lane >= HEAD_DIM + 3) & (lane < HEAD_DIM + 6)).astype(F32)[None, :]
    ok = ((lane >= HEAD_DIM) & (lane < HEAD_DIM + 3)).astype(F32)[None, :]
    return pq, pk, oq, ok


def _head_blocks(wt):
    d = wt.shape[1]
    return jnp.pad(wt.reshape(N_HEADS, HEAD_DIM, d), ((0, 0), (0, LANES - HEAD_DIM), (0, 0))).reshape(N_HEADS * LANES, d)


def _attn_prep(h, wq, wk, wv, fcum, tm):
    t, d = h.shape
    tm = min(tm, t)
    rows = min(ROW_CHUNK, tm)
    w = N_HEADS * LANES
    pq, pk, oq, ok = _aug_constants()

    def body(h_ref, wq_ref, wk_ref, wv_ref, f_ref, pq_ref, pk_ref, oq_ref, ok_ref, qa_ref, ka_ref, v_ref):
        for r0 in range(0, tm, rows):
            rs = slice(r0, r0 + rows)
            hh = h_ref[rs, :]
            fs = jnp.concatenate(_split3(f_ref[rs, :]), axis=1)
            q = _dot_nt(hh, wq_ref[...]).astype(BF16).astype(F32) * ATTN_SCALE
            qa_ref[rs, :] = (q + _dot(fs, pq_ref[...]) + oq_ref[...]).astype(BF16)
            k = _dot_nt(hh, wk_ref[...]).astype(BF16).astype(F32)
            ka_ref[rs, :] = (k + _dot(fs, pk_ref[...]) + ok_ref[...]).astype(BF16)
            v_ref[rs, :] = _dot_nt(hh, wv_ref[...]).astype(BF16)

    row = lambda n: pl.BlockSpec((tm, n), lambda i: (i, 0))
    full = lambda a: pl.BlockSpec(a.shape, lambda i: (0, 0))
    return pl.pallas_call(
        body, name="attn_prep",
        out_shape=(jax.ShapeDtypeStruct((t, w), BF16), jax.ShapeDtypeStruct((t, w), BF16),
                   jax.ShapeDtypeStruct((t, ATTN_WIDTH), BF16)),
        grid=(t // tm,),
        in_specs=[row(d), full(wq), full(wk), full(wv), row(LANES), full(pq), full(pk), full(oq), full(ok)],
        out_specs=(row(w), row(w), row(ATTN_WIDTH)),
        compiler_params=_cparams("parallel"),
    )(h, wq, wk, wv, fcum, pq, pk, oq, ok)


def _fold_lanes(x, op):
    out = x[:, :LANES]
    for g in range(1, x.shape[1] // LANES):
        out = op(out, x[:, g * LANES:(g + 1) * LANES])
    return out


def _causal_sweep(i, tile, carry):
    def quad(jj, c):
        for u in range(4):
            c = tile(4 * jj + u, c, False)
        return c

    carry = lax.fori_loop(0, i // 4, quad, carry)
    base = 4 * (i // 4)
    carry = lax.cond(i % 4 >= 2, lambda c: tile(base + 1, tile(base, c, False), False), lambda c: c, carry)
    return lax.cond(i % 2 == 1, lambda c: tile(i, tile(i - 1, c, False), True), lambda c: tile(i, c, True), carry)


def _attn_fwd(qa, ka, v, seq, tq, dep=None):
    t = qa.shape[0]
    nq = seq // tq
    hp_n = N_HEADS // 2
    heads = [slice(e * LANES, (e + 1) * LANES) for e in range(2)]

    def body(q_ref, k_ref, v_ref, o_ref, lse_ref, s_buf):
        i = pl.program_id(2)
        diag_ok = lax.broadcasted_iota(jnp.int32, (tq, tq), 0) >= lax.broadcasted_iota(jnp.int32, (tq, tq), 1)
        qs = [q_ref[:, hl] for hl in heads]

        def sweep1(j, mxs, diagonal):
            r0 = pl.multiple_of(j * tq, tq)
            out = []
            for e, hl in enumerate(heads):
                s = _dot_nt(qs[e], k_ref[pl.ds(r0, tq), hl])
                if diagonal:
                    s = jnp.where(diag_ok, s, NEG_BIG)
                s_buf[e, j] = s
                out.append(jnp.maximum(mxs[e], _fold_lanes(s, jnp.maximum)))
            return tuple(out)

        mxs = _causal_sweep(i, sweep1, (jnp.full((tq, LANES), NEG_BIG, F32),) * 2)
        ms = [jnp.max(mx, axis=1, keepdims=True) for mx in mxs]

        def sweep2(j, carry, diagonal):
            r0 = pl.multiple_of(j * tq, tq)
            vv = v_ref[pl.ds(r0, tq), :]
            out = []
            for e in range(2):
                p = jnp.exp(s_buf[e, j] - ms[e])
                out += [carry[2 * e] + _fold_lanes(p, jnp.add), carry[2 * e + 1] + _dot(p.astype(BF16), vv)]
            return tuple(out)

        res = _causal_sweep(i, sweep2, (jnp.zeros((tq, LANES), F32),) * 4)
        outs = []
        for e in range(2):
            l = jnp.sum(res[2 * e], axis=1, keepdims=True)
            outs.append(res[2 * e + 1] / l)
            lse_ref[:, e:e + 1] = ms[e] + jnp.log(l)
        lane = lax.broadcasted_iota(jnp.int32, (tq, LANES), 1)
        o_ref[...] = jnp.where(lane < HEAD_DIM, outs[0], outs[1])

    dep_specs, dep_ops = _dep_args(dep)
    return pl.pallas_call(
        _after(body, 3, dep), name="attn_fwd",
        out_shape=(jax.ShapeDtypeStruct((t, ATTN_WIDTH), F32), jax.ShapeDtypeStruct((hp_n, t, 2), F32)),
        grid=(t // seq, hp_n, nq),
        in_specs=[pl.BlockSpec((tq, 2 * LANES), lambda b, hp, i: (b * nq + i, hp)),
                  pl.BlockSpec((seq, 2 * LANES), lambda b, hp, i: (b, hp)),
                  pl.BlockSpec((seq, LANES), lambda b, hp, i: (b, hp))] + dep_specs,
        out_specs=(pl.BlockSpec((tq, LANES), lambda b, hp, i: (b * nq + i, hp)),
                   pl.BlockSpec((None, tq, 2), lambda b, hp, i: (hp, b * nq + i, 0))),
        scratch_shapes=[pltpu.VMEM((2, nq, tq, tq), F32)],
        compiler_params=_cparams("parallel", "parallel", "arbitrary"),
    )(qa, ka, v, *dep_ops)


def _merge_fwd(x, ps, o, g2, wpo, wao, wout, tm):
    t, d = x.shape
    tm = min(tm, t)
    rows = min(ROW_CHUNK, tm)

    def body(x_ref, ps_ref, o_ref, gp_ref, ga_ref, wpo_ref, wao_ref, wout_ref, mg_ref, x1_ref):
        for r0 in range(0, tm, rows):
            rs = slice(r0, r0 + rows)
            py = _dot(ps_ref[rs, :], wpo_ref[...])
            ay = _dot(o_ref[rs, :].astype(BF16), wao_ref[...])
            mb = (_sigmoid(gp_ref[rs, :].astype(F32)) * py + _sigmoid(ga_ref[rs, :].astype(F32)) * ay).astype(BF16)
            mg_ref[rs, :] = mb
            x1_ref[rs, :] = x_ref[rs, :] + _dot(mb, wout_ref[...])

    row = lambda w: pl.BlockSpec((tm, w), lambda i: (i, 0))
    full = lambda a: pl.BlockSpec(a.shape, lambda i: (0, 0))
    return pl.pallas_call(
        body, name="merge_fwd",
        out_shape=(jax.ShapeDtypeStruct((t, d), BF16), jax.ShapeDtypeStruct((t, d), F32)),
        grid=(t // tm,),
        in_specs=[row(d), row(POOL_WIDTH), row(ATTN_WIDTH), pl.BlockSpec((tm, d), lambda i: (i, 0)),
                  pl.BlockSpec((tm, d), lambda i: (i, 1)), full(wpo), full(wao), full(wout)],
        out_specs=(row(d), row(d)),
        compiler_params=_cparams("parallel"),
    )(x, ps, o, g2, g2, wpo, wao, wout)


def _ffn_fwd(x1, g, wg, wu, wd, tm, tf):
    t, d = x1.shape
    f = wg.shape[0]
    tm = min(tm, t)
    nf = f // tf
    rows = min(512, tm)

    def body(x1_ref, g_ref, wg_ref, wu_ref, wd_ref, h2_ref, gt_ref, up_ref, act_ref, x2_ref):
        j = pl.program_id(1)

        @pl.when(j == 0)
        def _():
            h2_ref[...] = _rms_fwd(x1_ref[...], g_ref[...]).astype(BF16)

            x2_ref[...] = x1_ref[...]

        for r0 in range(0, tm, rows):
            rs = slice(r0, r0 + rows)
            h2 = h2_ref[rs, :]
            gt = _dot_nt(h2, wg_ref[...])
            up = _dot_nt(h2, wu_ref[...])
            sg = _sigmoid(gt)
            silu = gt * sg
            act = (silu * up).astype(BF16)
            gt_ref[rs, :] = (up * (sg * (1.0 + gt * (1.0 - sg)))).astype(BF16)
            up_ref[rs, :] = silu.astype(BF16)
            act_ref[rs, :] = act
            x2_ref[rs, :] += _dot(act, wd_ref[...])

    return pl.pallas_call(
        body, name="ffn_fwd",
        out_shape=(jax.ShapeDtypeStruct((t, d), BF16), jax.ShapeDtypeStruct((t, f), BF16),
                   jax.ShapeDtypeStruct((t, f), BF16), jax.ShapeDtypeStruct((t, f), BF16),
                   jax.ShapeDtypeStruct((t, d), F32)),
        grid=(t // tm, nf),
        in_specs=[pl.BlockSpec((tm, d), lambda i, j: (i, 0)), pl.BlockSpec((1, d), lambda i, j: (0, 0)),
                  pl.BlockSpec((tf, d), lambda i, j: (j, 0)), pl.BlockSpec((tf, d), lambda i, j: (j, 0)),
                  pl.BlockSpec((tf, d), lambda i, j: (j, 0))],
        out_specs=(pl.BlockSpec((tm, d), lambda i, j: (i, 0)), pl.BlockSpec((tm, tf), lambda i, j: (i, j)),
                   pl.BlockSpec((tm, tf), lambda i, j: (i, j)), pl.BlockSpec((tm, tf), lambda i, j: (i, j)),
                   pl.BlockSpec((tm, d), lambda i, j: (i, 0))),
        compiler_params=_cparams("parallel", "arbitrary"),
    )(x1, g, wg, wu, wd)


def _final_fwd_bwd(x2, target, g, tm):
    t, d = x2.shape
    tm = min(tm, t)

    def body(x_ref, t_ref, g_ref, loss_ref, dx_ref, dg_ref):
        i = pl.program_id(0)
        x = x_ref[...]
        gg = g_ref[...]
        err = _rms_fwd(x, gg) - t_ref[...]
        part = 0.5 * jnp.sum(jnp.mean(err * err, axis=-1, keepdims=True), axis=0, keepdims=True)
        dx, dg = _rms_bwd(x, gg, err * (1.0 / d))
        dx_ref[...] = dx

        @pl.when(i == 0)
        def _():
            loss_ref[...] = jnp.zeros_like(loss_ref)
            dg_ref[...] = jnp.zeros_like(dg_ref)

        loss_ref[...] += jnp.broadcast_to(part, loss_ref.shape)
        dg_ref[...] += dg

    return pl.pallas_call(
        body, name="final_fwd_bwd",
        out_shape=(jax.ShapeDtypeStruct((1, LANES), F32), jax.ShapeDtypeStruct((t, d), F32),
                   jax.ShapeDtypeStruct((1, d), F32)),
        grid=(t // tm,),
        in_specs=[pl.BlockSpec((tm, d), lambda i: (i, 0)), pl.BlockSpec((tm, d), lambda i: (i, 0)),
                  pl.BlockSpec((1, d), lambda i: (0, 0))],
        out_specs=(pl.BlockSpec((1, LANES), lambda i: (0, 0)), pl.BlockSpec((tm, d), lambda i: (i, 0)),
                   pl.BlockSpec((1, d), lambda i: (0, 0))),
        compiler_params=_cparams("arbitrary"),
    )(x2, target, g)


def _ffn_bwd(dx2, x1, g, gt, up, wg, wu, wd, tm, tf):
    t, d = dx2.shape
    f = gt.shape[1]
    tm = min(tm, t)
    nf = f // tf
    wgu = jnp.concatenate([wg.reshape(nf, tf, d), wu.reshape(nf, tf, d)], axis=1).reshape(2 * f, d)
    rows = min(256, tm)

    def body(dx2_ref, x1_ref, g_ref, gt_ref, up_ref, wgu_ref, wd_ref, dgt_ref, dup_ref, dx1_ref, dg_ref, acc_ref,
             dxb_ref):
        i, j = pl.program_id(0), pl.program_id(1)

        @pl.when(j == 0)
        def _():
            dxb_ref[...] = dx2_ref[...].astype(BF16)
            acc_ref[...] = jnp.zeros_like(acc_ref)

        for r0 in range(0, tm, rows):
            rs = slice(r0, r0 + rows)
            dact = _dot_nt(dxb_ref[rs, :], wd_ref[...])
            dgt = (dact * gt_ref[rs, :].astype(F32)).astype(BF16)
            dup = (dact * up_ref[rs, :].astype(F32)).astype(BF16)
            dgt_ref[rs, :] = dgt
            dup_ref[rs, :] = dup
            acc_ref[rs, :] += _dot(jnp.concatenate([dgt, dup], axis=1), wgu_ref[...])

        @pl.when(jnp.logical_and(i == 0, j == 0))
        def _():
            dg_ref[...] = jnp.zeros_like(dg_ref)

        @pl.when(j == nf - 1)
        def _():
            dxn, dg = _rms_bwd(x1_ref[...], g_ref[...], acc_ref[...])
            dx1_ref[...] = dx2_ref[...] + dxn
            dg_ref[...] += dg

    return pl.pallas_call(
        body, name="ffn_bwd",
        out_shape=(jax.ShapeDtypeStruct((t, f), BF16), jax.ShapeDtypeStruct((t, f), BF16),
                   jax.ShapeDtypeStruct((t, d), F32), jax.ShapeDtypeStruct((1, d), F32)),
        grid=(t // tm, nf),
        in_specs=[pl.BlockSpec((tm, d), lambda i, j: (i, 0)), pl.BlockSpec((tm, d), lambda i, j: (i, 0)),
                  pl.BlockSpec((1, d), lambda i, j: (0, 0)),
                  pl.BlockSpec((tm, tf), lambda i, j: (i, j)), pl.BlockSpec((tm, tf), lambda i, j: (i, j)),
                  pl.BlockSpec((2 * tf, d), lambda i, j: (j, 0)), pl.BlockSpec((tf, d), lambda i, j: (j, 0))],
        out_specs=(pl.BlockSpec((tm, tf), lambda i, j: (i, j)), pl.BlockSpec((tm, tf), lambda i, j: (i, j)),
                   pl.BlockSpec((tm, d), lambda i, j: (i, 0)), pl.BlockSpec((1, d), lambda i, j: (0, 0))),
        scratch_shapes=[pltpu.VMEM((tm, d), F32), pltpu.VMEM((tm, d), BF16)],
        compiler_params=_cparams("arbitrary", "arbitrary"),
    )(dx2, x1, g, gt, up, wgu, wd)


def _merge_bwd(dx1, ps, o, g2, wpo, wao, wout, tm, dep=None):
    t, d = dx1.shape
    tm = min(tm, t)
    rows = min(ROW_CHUNK, tm)

    def body(dx1_ref, ps_ref, o_ref, gp_ref, ga_ref, wpo_ref, wao_ref, wout_ref, dpy_ref, day_ref, dg2_ref, dps_ref, da_ref):
        for r0 in range(0, tm, rows):
            rs = slice(r0, r0 + rows)
            dm = _dot_nt(dx1_ref[rs, :].astype(BF16), wout_ref[...])
            py = _dot(ps_ref[rs, :], wpo_ref[...])
            ay = _dot(o_ref[rs, :].astype(BF16), wao_ref[...])
            sp = _sigmoid(gp_ref[rs, :].astype(F32))
            sa = _sigmoid(ga_ref[rs, :].astype(F32))
            dpy = (dm * sp).astype(BF16)
            day = (dm * sa).astype(BF16)
            dpy_ref[rs, :] = dpy
            day_ref[rs, :] = day
            dg2_ref[rs, :d] = (dm * py * (sp * (1.0 - sp))).astype(BF16)
            dg2_ref[rs, d:] = (dm * ay * (sa * (1.0 - sa))).astype(BF16)
            dps_ref[rs, :] = _dot_nt(dpy, wpo_ref[...])
            da_ref[rs, :] = _dot_nt(day, wao_ref[...]).astype(BF16)

    row = lambda w: pl.BlockSpec((tm, w), lambda i: (i, 0))
    full = lambda a: pl.BlockSpec(a.shape, lambda i: (0, 0))
    dep_specs, dep_ops = _dep_args(dep)
    return pl.pallas_call(
        _after(body, 8, dep), name="merge_bwd",
        out_shape=(jax.ShapeDtypeStruct((t, d), BF16), jax.ShapeDtypeStruct((t, d), BF16),
                   jax.ShapeDtypeStruct((t, 2 * d), BF16), jax.ShapeDtypeStruct((t, POOL_WIDTH), F32),
                   jax.ShapeDtypeStruct((t, ATTN_WIDTH), BF16)),
        grid=(t // tm,),
        in_specs=[row(d), row(POOL_WIDTH), row(ATTN_WIDTH), pl.BlockSpec((tm, d), lambda i: (i, 0)),
                  pl.BlockSpec((tm, d), lambda i: (i, 1)), full(wpo), full(wao), full(wout)] + dep_specs,
        out_specs=(row(d), row(d), row(2 * d), row(POOL_WIDTH), row(ATTN_WIDTH)),
        compiler_params=_cparams("parallel"),
    )(dx1, ps, o, g2, g2, wpo, wao, wout, *dep_ops)


def _attn_bwd(qa, ka, v, do, lse4, seq, tq, dep=None):
    t = qa.shape[0]
    nq = seq // tq
    hp_n = N_HEADS // 2
    heads = [slice(e * LANES, (e + 1) * LANES) for e in range(2)]

    def body(q_ref, k_ref, v_ref, do_ref, lse_ref, dq_ref, dk_ref, dv_ref, dfr_ref, dk_acc, dv_acc, p_buf, dp_buf):
        diag_ok = lax.broadcasted_iota(jnp.int32, (tq, tq), 0) >= lax.broadcasted_iota(jnp.int32, (tq, tq), 1)
        lane_q = lax.broadcasted_iota(jnp.int32, (tq, LANES), 1)
        mine_q = [lane_q < HEAD_DIM, lane_q >= HEAD_DIM]
        dv_acc[...] = jnp.zeros_like(dv_acc)
        dk_acc[...] = jnp.zeros_like(dk_acc)
        dfr_ref[...] = jnp.zeros_like(dfr_ref)
        transposed = lambda a: a.astype(F32).T.astype(BF16)

        def q_step(i, _):
            q0 = pl.multiple_of(i * tq, tq)
            qs = [q_ref[pl.ds(q0, tq), hl] for hl in heads]
            dov = do_ref[pl.ds(q0, tq), :]
            dos = [jnp.where(mq, dov, jnp.zeros((), BF16)) for mq in mine_q]
            qts = [transposed(q) for q in qs]
            dots = [transposed(a) for a in dos]
            lss = [lse_ref[pl.ds(q0, tq), e:e + 1] for e in range(2)]

            def sweep1(j, dls, diagonal):
                r0 = pl.multiple_of(j * tq, tq)
                vv = v_ref[pl.ds(r0, tq), :]
                out = []
                for e, hl in enumerate(heads):
                    s = _dot_nt(qs[e], k_ref[pl.ds(r0, tq), hl])
                    if diagonal:
                        s = jnp.where(diag_ok, s, NEG_BIG)
                    p = jnp.exp(s - lss[e])
                    dp = _dot_nt(dos[e], vv)
                    p_buf[e, j] = p
                    dp_buf[e, j] = dp
                    dv_acc[j] += _dot(dots[e], p.astype(BF16))
                    out.append(dls[e] + _fold_lanes(p * dp, jnp.add))
                return tuple(out)

            dls = _causal_sweep(i, sweep1, (jnp.zeros((tq, LANES), F32),) * 2)
            dls = [jnp.sum(d, axis=1, keepdims=True) for d in dls]

            def sweep2(j, dqs, diagonal):
                r0 = pl.multiple_of(j * tq, tq)
                out = []
                for e, hl in enumerate(heads):
                    ds = p_buf[e, j] * (dp_buf[e, j] - dls[e])
                    dfr_ref[e, pl.ds(j, 1), :] += jnp.sum(ds, axis=0, keepdims=True)
                    dsb = ds.astype(BF16)
                    dk_acc[e, j] += _dot(qts[e], dsb)
                    out.append(dqs[e] + _dot(dsb, k_ref[pl.ds(r0, tq), hl]))
                return tuple(out)

            dqs = _causal_sweep(i, sweep2, (jnp.zeros((tq, LANES), F32),) * 2)
            dq = jnp.where(mine_q[0], dqs[0], pltpu.roll(dqs[1], HEAD_DIM, 1)) * ATTN_SCALE
            dq_ref[pl.ds(q0, tq), :] = dq.astype(BF16)
            return 0

        lax.fori_loop(0, nq, q_step, 0)
        for j in range(nq):
            rs = slice(j * tq, (j + 1) * tq)
            dk = jnp.where(mine_q[0], dk_acc[0, j].T, pltpu.roll(dk_acc[1, j].T, HEAD_DIM, 1))
            dk_ref[rs, :] = dk.astype(BF16)
            dv_ref[rs, :] = dv_acc[j].T.astype(BF16)

    wide = pl.BlockSpec((seq, 2 * LANES), lambda b, hp: (b, hp))
    col = pl.BlockSpec((seq, LANES), lambda b, hp: (b, hp))
    pair = pl.BlockSpec((None, seq, 2), lambda b, hp: (hp, b, 0))
    dep_specs, dep_ops = _dep_args(dep)
    return pl.pallas_call(
        _after(body, 5, dep), name="attn_bwd",
        out_shape=(jax.ShapeDtypeStruct((t, ATTN_WIDTH), BF16),) * 3 + (jax.ShapeDtypeStruct((N_HEADS, t // tq, tq), F32),),
        grid=(t // seq, hp_n),
        in_specs=[wide, wide, col, col, pair] + dep_specs,
        out_specs=(col, col, col, pl.BlockSpec((2, nq, tq), lambda b, hp: (hp, b, 0))),
        scratch_shapes=[pltpu.VMEM((2, nq, LANES, tq), F32), pltpu.VMEM((nq, LANES, tq), F32),
                        pltpu.VMEM((2, nq, tq, tq), F32), pltpu.VMEM((2, nq, tq, tq), F32)],
        compiler_params=_cparams("parallel", "arbitrary"),
    )(qa, ka, v, do, lse4, *dep_ops)


def _forget_bwd(dfc, fl, bf, seq):
    t = fl.shape[0]
    cb = min(256, seq)
    nb = seq // cb

    def body(dfc_ref, fl_ref, bf_ref, dfl_ref, db_ref):
        b = pl.program_id(0)
        ri = lax.broadcasted_iota(jnp.int32, (cb, cb), 0)
        ci = lax.broadcasted_iota(jnp.int32, (cb, cb), 1)
        tri = (ci >= ri).astype(BF16)
        carry = jnp.zeros((1, LANES), F32)
        dbs = jnp.zeros((1, LANES), F32)
        for blk in reversed(range(nb)):
            rs = slice(blk * cb, (blk + 1) * cb)
            dlf = _tri_dot(tri, -dfc_ref[rs, :]) + carry
            carry = dlf[0:1, :]
            dfl = dlf * _sigmoid(-(fl_ref[rs, :] + bf_ref[...]))
            dfl_ref[rs, :] = dfl.astype(BF16)
            dbs = dbs + jnp.sum(dfl, axis=0, keepdims=True)

        @pl.when(b == 0)
        def _():
            db_ref[...] = jnp.zeros_like(db_ref)

        db_ref[...] += dbs

    return pl.pallas_call(
        body, name="forget_bwd",
        out_shape=(jax.ShapeDtypeStruct((t, LANES), BF16), jax.ShapeDtypeStruct((1, LANES), F32)),
        grid=(t // seq,),
        in_specs=[pl.BlockSpec((seq, LANES), lambda b: (b, 0)), pl.BlockSpec((seq, LANES), lambda b: (b, 0)),
                  pl.BlockSpec((1, LANES), lambda b: (0, 0))],
        out_specs=(pl.BlockSpec((seq, LANES), lambda b: (b, 0)), pl.BlockSpec((1, LANES), lambda b: (0, 0))),
        compiler_params=_cparams("arbitrary"),
    )(dfc, fl, bf)


def _pool_bwd(dps, p, mix, scale, seq):
    t = dps.shape[0]

    def body(dps_ref, p_ref, mix_ref, sc_ref, du_ref, dmix_ref, dsc_ref):
        b = pl.program_id(0)

        @pl.when(b == 0)
        def _():
            dmix_ref[...] = jnp.zeros_like(dmix_ref)
            dsc_ref[...] = jnp.zeros_like(dsc_ref)

        tpos = lax.broadcasted_iota(jnp.int32, (seq, POOL_GROUP_DIM), 0)
        for g in range(POOL_GROUPS):
            sl = slice(g * POOL_GROUP_DIM, (g + 1) * POOL_GROUP_DIM)
            pb = p_ref[:, sl]
            dpsg = dps_ref[:, sl]
            pm = _dot(pb, mix_ref[g])
            dsc_ref[:, sl] += jnp.sum(dpsg * pm, axis=0, keepdims=True)
            dpm = (dpsg * sc_ref[:, sl]).astype(BF16)
            dmix_ref[g] += _dot_tn(pb, dpm)
            dp = _dot_nt(dpm, mix_ref[g])
            cnt = jnp.minimum(tpos + 1, POOL_WINDOWS[g]).astype(F32)
            s = dp / cnt
            for lvl in range(g + 1):
                d = 2 ** lvl
                s = s + jnp.where(tpos < seq - d, pltpu.roll(s, seq - d, 0), 0.0)
            du_ref[:, sl] = (s - dp).astype(BF16)

    return pl.pallas_call(
        body, name="pool_bwd",
        out_shape=(jax.ShapeDtypeStruct((t, POOL_WIDTH), BF16),
                   jax.ShapeDtypeStruct((POOL_GROUPS, POOL_GROUP_DIM, POOL_GROUP_DIM), F32),
                   jax.ShapeDtypeStruct((1, POOL_WIDTH), F32)),
        grid=(t // seq,),
        in_specs=[pl.BlockSpec((seq, POOL_WIDTH), lambda b: (b, 0)), pl.BlockSpec((seq, POOL_WIDTH), lambda b: (b, 0)),
                  pl.BlockSpec((POOL_GROUPS, POOL_GROUP_DIM, POOL_GROUP_DIM), lambda b: (0, 0, 0)),
                  pl.BlockSpec((1, POOL_WIDTH), lambda b: (0, 0))],
        out_specs=(pl.BlockSpec((seq, POOL_WIDTH), lambda b: (b, 0)),
                   pl.BlockSpec((POOL_GROUPS, POOL_GROUP_DIM, POOL_GROUP_DIM), lambda b: (0, 0, 0)),
                   pl.BlockSpec((1, POOL_WIDTH), lambda b: (0, 0))),
        compiler_params=_cparams("arbitrary"),
    )(dps, p, mix, scale)


def _in_bwd(du, dq, dk, dv, dg2, dfl, dx1, x, g, wu, wqkv, wg2, wft, tm):
    t, d = x.shape
    tm = min(tm, t)
    rows = min(ROW_CHUNK, tm)
    aw = ATTN_WIDTH

    def body(du_ref, dq_ref, dk_ref, dv_ref, dg2_ref, dfl_ref, dx1_ref, x_ref, g_ref, wu_ref, wqkv_ref, wg2_ref, wft_ref,
             dx_ref, dg_ref):
        i = pl.program_id(0)

        @pl.when(i == 0)
        def _():
            dg_ref[...] = jnp.zeros_like(dg_ref)

        for r0 in range(0, tm, rows):
            rs = slice(r0, r0 + rows)
            dh = _dot(du_ref[rs, :], wu_ref[...])
            dh += _dot(dq_ref[rs, :], wqkv_ref[0:aw, :])
            dh += _dot(dk_ref[rs, :], wqkv_ref[aw:2 * aw, :])
            dh += _dot(dv_ref[rs, :], wqkv_ref[2 * aw:3 * aw, :])
            dh += _dot(dg2_ref[rs, :], wg2_ref[...])
            dh += _dot(dfl_ref[rs, :], wft_ref[...])
            dxn, dg = _rms_bwd(x_ref[rs, :], g_ref[...], dh)
            dx_ref[rs, :] = dx1_ref[rs, :] + dxn
            dg_ref[...] += dg

    row = lambda w: pl.BlockSpec((tm, w), lambda i: (i, 0))
    full = lambda a: pl.BlockSpec(a.shape, lambda i: (0, 0))
    return pl.pallas_call(
        body, name="in_bwd",
        out_shape=(jax.ShapeDtypeStruct((t, d), F32), jax.ShapeDtypeStruct((1, d), F32)),
        grid=(t // tm,),
        in_specs=[row(POOL_WIDTH), row(aw), row(aw), row(aw), row(2 * d), row(LANES), row(d), row(d),
                  pl.BlockSpec((1, d), lambda i: (0, 0)), full(wu), full(wqkv), full(wg2), full(wft)],
        out_specs=(row(d), pl.BlockSpec((1, d), lambda i: (0, 0))),
        compiler_params=_cparams("arbitrary"),
    )(du, dq, dk, dv, dg2, dfl, dx1, x, g, wu, wqkv, wg2, wft)


def _position():
    return lax.axis_index("x"), lax.axis_index("y"), lax.axis_index("c")


def _remote(src, dst, send_sem, recv_sem, device):
    return pltpu.make_async_remote_copy(src_ref=src, dst_ref=dst, send_sem=send_sem, recv_sem=recv_sem,
                                        device_id=device, device_id_type=MESH)


HBM = pl.BlockSpec(memory_space=pltpu.HBM)
SEM = pl.BlockSpec(memory_space=pltpu.SEMAPHORE)
DATAFLOW = pltpu.SideEffectType.DATAFLOW_SIDE_EFFECTING


def _copies_start(name, arrays, plan, m, dep=None):
    n = len(arrays)
    arrays = [pltpu.with_memory_space_constraint(a, pltpu.HBM) for a in arrays]

    def body(*refs):
        ins, send_sem, recv_sem, token = refs[:n], refs[n], refs[n + 1], refs[2 * n + 2]
        for i, (src, dst, device, _) in enumerate(plan(ins, *_position())):
            _remote(src, dst, send_sem.at[i], recv_sem.at[i], device).start()
        token[...] = jnp.zeros_like(token)

    dep_specs, dep_ops = _dep_args(dep)
    outs = pl.pallas_call(
        _after(body, n, dep), name=name,
        out_shape=(pltpu.SemaphoreType.DMA((m,)), pltpu.SemaphoreType.DMA((m,)),
                   *[pltpu.HBM(a.shape, a.dtype) for a in arrays], jax.ShapeDtypeStruct((8, LANES), F32)),
        in_specs=[HBM] * n + dep_specs, out_specs=(SEM, SEM, *[HBM] * n, pl.BlockSpec(memory_space=pltpu.VMEM)),
        input_output_aliases={i: i + 2 for i in range(n)},
        compiler_params=pltpu.CompilerParams(has_side_effects=DATAFLOW),
    )(*arrays, *dep_ops)
    return (outs[0], outs[1]), list(outs[2:2 + n]), outs[2 + n]


def _copies_wait(name, sems, arrays, plan, after):
    n = len(arrays)
    afters = list(after) if isinstance(after, (list, tuple)) else [after]

    def body(*refs):
        ins, send_sem, recv_sem = refs[:n], refs[n], refs[n + 1]
        for i, (src, dst, device, landing) in enumerate(plan(ins, *_position())):
            _remote(src, dst, send_sem.at[i], recv_sem.at[i], device).wait_send()
            _remote(landing, landing, send_sem.at[i], recv_sem.at[i], device).wait_recv()

    outs = pl.pallas_call(
        body, name=name,
        out_shape=tuple(pltpu.HBM(a.shape, a.dtype) for a in arrays),
        in_specs=[HBM] * n + [SEM, SEM] + [ANY] * len(afters), out_specs=tuple([HBM] * n),
        input_output_aliases={i: i for i in range(n)},
        compiler_params=pltpu.CompilerParams(has_side_effects=DATAFLOW),
    )(*arrays, sems[0], sems[1], *afters)
    return list(outs)


def _tie(x, dep):
    for token in _dep_list(dep):
        x = x + token[0, 0]
    return x


def _other_chips(x, y):
    return [(1 - x, y), (x, 1 - y), (1 - x, 1 - y)]


def _gather_begin(tag, shards, token, column_halves=False):
    n = len(shards)
    lands = [lax.empty((N_CHIPS,) + s.shape, s.dtype) for s in shards]
    if column_halves:
        cols = lambda ref, h: pl.ds(pl.multiple_of(h * (ref.shape[-1] // 2), LANES), ref.shape[-1] // 2)
        mine = lambda ref, h: ref.at[:, cols(ref, h)]
        landed = lambda ref, chip, h: ref.at[chip, :, cols(ref, h)]
    else:
        mine = lambda ref, h: ref.at[h]
        landed = lambda ref, chip, h: ref.at[chip, h]

    def plan(refs, x, y, c):
        return [(mine(refs[k], c), landed(refs[n + k], 2 * x + y, c), (ox, oy, c), landed(refs[n + k], 2 * ox + oy, c))
                for k in range(n) for ox, oy in _other_chips(x, y)]

    sems, thru, token = _copies_start(f"gather_{tag}_ici_start", list(shards) + lands, plan, 3 * n, dep=token)
    return dict(tag=tag, n=n, plan=plan, sems=sems, arrays=thru, token=token, landed=landed)


def _gather_forward(st, after):
    n, tag, landed = st["n"], st["tag"], st["landed"]
    thru = _copies_wait(f"gather_{tag}_ici_wait", st["sems"], st["arrays"], st["plan"], after)

    def plan(refs, x, y, c):
        return [(landed(refs[k], 2 * ox + oy, c), landed(refs[k], 2 * ox + oy, c), (x, y, 1 - c),
                 landed(refs[k], 2 * ox + oy, 1 - c))
                for k in range(n) for ox, oy in _other_chips(x, y)]

    sems, lands, token = _copies_start(f"gather_{tag}_fwd_start", thru[n:], plan, 3 * n)
    return dict(tag=tag, n=n, plan=plan, sems=sems, arrays=lands, token=token, shards=thru[:n])


def _gather_end(st, after, merge=True):
    lands = _copies_wait(f"gather_{st['tag']}_fwd_wait", st["sems"], st["arrays"], st["plan"], after)
    if not merge:
        return lands, st["shards"]
    me = 2 * lax.axis_index("x") + lax.axis_index("y")
    return [lax.dynamic_update_index_in_dim(g, s, me, 0) for g, s in zip(lands, st["shards"])]


def _add_keep_give(name, pos, a, a_keep, a_give, b, b_keep, b_give, steps):
    r, c = b.shape[-2:]

    def spec(arr, fn):
        lead = arr.ndim - 2

        def index(i, p):
            idx = tuple(fn(i, p))
            return idx if len(idx) == arr.ndim else idx + (0, 0)

        return pl.BlockSpec((None,) * lead + (r, c), index)

    out_spec = pl.BlockSpec((None, r, c), lambda i, p: (i, 0, 0))

    def body(p_ref, ak_ref, bk_ref, ag_ref, bg_ref, keep_ref, give_ref):
        keep_ref[...] = ak_ref[...] + bk_ref[...].astype(F32)
        give_ref[...] = (ag_ref[...] + bg_ref[...].astype(F32)).astype(BF16)

    return pl.pallas_call(
        body, name=name,
        out_shape=(jax.ShapeDtypeStruct((steps, r, c), F32), jax.ShapeDtypeStruct((steps, r, c), BF16)),
        grid_spec=pltpu.PrefetchScalarGridSpec(
            num_scalar_prefetch=1, grid=(steps,),
            in_specs=[spec(a, a_keep), spec(b, b_keep), spec(a, a_give), spec(b, b_give)],
            out_specs=(out_spec, out_spec)),
        compiler_params=_cparams("parallel"),
    )(pos, a, b, a, b)


def _add_last(name, a, b):
    _, r, c = a.shape
    blk = pl.BlockSpec((None, r, c), lambda i: (0, 0, 0))

    def body(a_ref, b_ref, o_ref):
        o_ref[...] = a_ref[...] + b_ref[...].astype(F32)

    return pl.pallas_call(
        body, name=name, out_shape=jax.ShapeDtypeStruct((r, c), F32), grid=(1,), in_specs=[blk, blk],
        out_specs=pl.BlockSpec((r, c), lambda i: (0, 0)), compiler_params=_cparams("arbitrary"),
    )(a, b)


def _exchange_part(gives, lands, peer_fn):
    n = len(gives)

    def plan(refs, x, y, c):
        return [(refs[k], refs[n + k], peer_fn(x, y, c), refs[n + k]) for k in range(n)]

    return list(gives) + list(lands), plan, n


def _join_parts(parts):
    offsets, total = [], 0
    for arrays, _, _ in parts:
        offsets.append(total)
        total += len(arrays)

    def plan(refs, x, y, c):
        copies = []
        for (arrays, part_plan, _), off in zip(parts, offsets):
            copies += part_plan(refs[off:off + len(arrays)], x, y, c)
        return copies

    return [a for arrays, _, _ in parts for a in arrays], plan, sum(m for _, _, m in parts)


def _reduce_begin(tag, grads, column_halves=False):
    n = len(grads)
    if column_halves:
        half = lambda ref, j, h: ref.at[j, :, pl.ds(pl.multiple_of(h * (ref.shape[2] // 2), LANES), ref.shape[2] // 2)]
        lands = [lax.empty((N_CHIPS, g.shape[1], g.shape[2] // 2), F32) for g in grads]
    else:
        half = lambda ref, j, h: ref.at[j, h]
        lands = [lax.empty((N_CHIPS,) + g.shape[2:], F32) for g in grads]

    def plan(refs, x, y, c):
        return [(half(refs[k], j, 1 - c), refs[n + k].at[j], (x, y, 1 - c), refs[n + k].at[j])
                for k in range(n) for j in range(N_CHIPS)]

    return dict(tag=tag, n=n, stage="c", grads=list(grads), column_halves=column_halves,
                part=(list(grads) + lands, plan, N_CHIPS * n))


def _reduce_next(st, thru):
    tag, n, stage = st["tag"], st["n"], st["stage"]
    first, recv = thru[:n], thru[n:]
    x, y, c = _position()
    if stage == "c":
        pos = jnp.stack([c, x]).astype(jnp.int32)
        if st["column_halves"]:
            mine = lambda chip: (lambda i, p: (chip(p) + i, 0, p[0]))
        else:
            mine = lambda chip: (lambda i, p: (chip(p) + i, p[0]))
        sums = [_add_keep_give(
            f"rs{tag}_c_add{k}", pos,
            first[k], mine(lambda p: 2 * p[1]), mine(lambda p: 2 * (1 - p[1])),
            recv[k], lambda i, p: (2 * p[1] + i,), lambda i, p: (2 * (1 - p[1]) + i,), 2) for k in range(n)]
        lands = [lax.empty(s[1].shape, BF16) for s in sums]
        return dict(tag=tag, n=n, stage="x", keep=[s[0] for s in sums],
                    part=_exchange_part([s[1] for s in sums], lands, lambda x, y, c: (1 - x, y, c)))
    if stage == "x":
        pos = jnp.stack([y]).astype(jnp.int32)
        sums = [_add_keep_give(
            f"rs{tag}_x_add{k}", pos,
            st["keep"][k], lambda i, p: (p[0],), lambda i, p: (1 - p[0],),
            recv[k], lambda i, p: (p[0],), lambda i, p: (1 - p[0],), 1) for k in range(n)]
        lands = [lax.empty(s[1].shape, BF16) for s in sums]
        return dict(tag=tag, n=n, stage="y", keep=[s[0] for s in sums],
                    part=_exchange_part([s[1] for s in sums], lands, lambda x, y, c: (x, 1 - y, c)))
    if stage == "y":
        mine = [_add_last(f"rs{tag}_y_add{k}", st["keep"][k], recv[k]) for k in range(n)]
        lands = [lax.empty(m.shape, F32) for m in mine]
        return dict(tag=tag, n=n, stage="swap", part=_exchange_part(mine, lands, lambda x, y, c: (x, y, 1 - c)))
    return dict(tag=tag, done=list(zip(first, recv)))


def _small_begin(tag, v):
    land = lax.empty((N_DEV,) + v.shape, F32)
    flips = [(fx, fy, fc) for fx in (0, 1) for fy in (0, 1) for fc in (0, 1)][1:]

    def plan(refs, x, y, c):
        copies = []
        for fx, fy, fc in flips:
            px, py, pc = (1 - x if fx else x), (1 - y if fy else y), (1 - c if fc else c)
            copies.append((refs[0], refs[1].at[4 * x + 2 * y + c], (px, py, pc), refs[1].at[4 * px + 2 * py + pc]))
        return copies

    return dict(tag=tag, n=1, stage="swap", grads=[v], part=([v, land], plan, len(flips)))


def _small_sum(name, own, land):
    x, y, c = _position()
    me = jnp.stack([4 * x + 2 * y + c]).astype(jnp.int32)

    def body(me_ref, own_ref, land_ref, out_ref):
        term = lambda dev: jnp.where(me_ref[0] == dev, own_ref[...], land_ref[dev])
        acc = term(0)
        for dev in range(1, N_DEV):
            acc = acc + term(dev)
        out_ref[...] = acc

    return pl.pallas_call(
        body, name=name, out_shape=jax.ShapeDtypeStruct(own.shape, F32),
        grid_spec=pltpu.PrefetchScalarGridSpec(
            num_scalar_prefetch=1, grid=(1,),
            in_specs=[pl.BlockSpec(own.shape, lambda i, m: (0, 0)), pl.BlockSpec(land.shape, lambda i, m: (0, 0, 0))],
            out_specs=pl.BlockSpec(own.shape, lambda i, m: (0, 0))),
        compiler_params=_cparams("arbitrary"),
    )(me, own, land)


def _adamw_update(w, gg, m, v):
    mn = ADAM_B1 * m + (1.0 - ADAM_B1) * gg
    vn = ADAM_B2 * v + (1.0 - ADAM_B2) * (gg * gg)
    m_hat = mn / (1.0 - ADAM_B1 ** ADAM_STEP)
    v_hat = vn / (1.0 - ADAM_B2 ** ADAM_STEP)
    return -ADAM_LR * (m_hat / (jnp.sqrt(v_hat) + ADAM_EPS) + ADAM_WD * w), mn, vn


def _adamw(name, w, g, m, v):
    def body(w_ref, g_ref, m_ref, v_ref, d_ref, mo_ref, vo_ref):
        d_ref[...], mo_ref[...], vo_ref[...] = _adamw_update(w_ref[...], g_ref[...], m_ref[...], v_ref[...])

    blk = pl.BlockSpec(w.shape, lambda i: (0, 0))
    return pl.pallas_call(
        body, name=name, out_shape=(jax.ShapeDtypeStruct(w.shape, F32),) * 3, grid=(1,),
        in_specs=[blk] * 4, out_specs=(blk,) * 3, compiler_params=_cparams("arbitrary"),
    )(w, g, m, v)


def _rows_to_bf16(name, w):
    r, _, c = w.shape

    def body(w_ref, o_ref):
        o_ref[...] = w_ref[:, 0, :].astype(BF16)

    return pl.pallas_call(
        body, name=name, out_shape=jax.ShapeDtypeStruct((r, c), BF16), grid=(1,),
        in_specs=[pl.BlockSpec((r, 1, c), lambda i: (0, 0, 0))], out_specs=pl.BlockSpec((r, c), lambda i: (0, 0)),
        compiler_params=_cparams("arbitrary"),
    )(w)


def _adamw_rows(name, pos_c, w, g_mine, g_other, m, v):
    r, _, c = w.shape
    ch = c // 2

    def body(p_ref, w_ref, gm_ref, go_ref, m_ref, v_ref, g_ref, d_ref, mo_ref, vo_ref):
        gg = jnp.where(pl.program_id(0) == p_ref[0], gm_ref[...], go_ref[...])
        dl, mn, vn = _adamw_update(w_ref[:, 0, :], gg, m_ref[:, 0, :], v_ref[:, 0, :])
        g_ref[:, 0, :] = gg
        d_ref[:, 0, :] = dl
        mo_ref[:, 0, :] = mn
        vo_ref[:, 0, :] = vn

    rows = pl.BlockSpec((r, 1, ch), lambda h, p: (0, 0, h))
    half = pl.BlockSpec((r, ch), lambda h, p: (0, 0))
    return pl.pallas_call(
        body, name=name, out_shape=(jax.ShapeDtypeStruct(w.shape, F32),) * 4,
        grid_spec=pltpu.PrefetchScalarGridSpec(
            num_scalar_prefetch=1, grid=(2,), in_specs=[rows, half, half, rows, rows], out_specs=(rows,) * 4),
        compiler_params=_cparams("parallel"),
    )(pos_c, w, g_mine, g_other, m, v)


def _adamw_halves(name, pos_c, w, g_mine, g_other, m, v, tr, dep=None):
    r, c = w.shape
    rh = r // 2
    tr = tr if rh % tr == 0 else rh
    nt = rh // tr

    def body(p_ref, w_ref, gm_ref, go_ref, m_ref, v_ref, g_ref, d_ref, mo_ref, vo_ref):
        gg = jnp.where(pl.program_id(0) == p_ref[0], gm_ref[...], go_ref[...])
        g_ref[...] = gg
        d_ref[...], mo_ref[...], vo_ref[...] = _adamw_update(w_ref[...], gg, m_ref[...], v_ref[...])

    full = pl.BlockSpec((tr, c), lambda h, i, p: (h * nt + i, 0))
    half = pl.BlockSpec((tr, c), lambda h, i, p: (i, 0))
    dep_specs, dep_ops = _dep_args(dep)
    return pl.pallas_call(
        _after(body, 6, dep), name=name, out_shape=(jax.ShapeDtypeStruct((r, c), F32),) * 4,
        grid_spec=pltpu.PrefetchScalarGridSpec(
            num_scalar_prefetch=1, grid=(2, nt),
            in_specs=[full, half, half, full, full] + dep_specs, out_specs=(full,) * 4),
        compiler_params=_cparams("parallel", "parallel"),
    )(pos_c, w, g_mine, g_other, m, v, *dep_ops)


def _col_sharded_to_comm(g):
    k, n = g.shape
    return g.reshape(2, k // 2, N_CHIPS, n // N_CHIPS).transpose(2, 0, 1, 3)


def _row_sharded_to_comm(g):
    r, c = g.shape
    return g.reshape(N_CHIPS, 2, r // (2 * N_CHIPS), c)


def _col_sharded_full(g):
    _, _, rh, c = g.shape
    return g.reshape(N_CHIPS, 2 * rh, c).transpose(1, 0, 2).reshape(2 * rh, N_CHIPS * c)


def _row_sharded_full(g):
    _, _, rh, c = g.shape
    return g.reshape(N_CHIPS * 2 * rh, c)


def _chip_rows(w3, start, stop, own=None, me=None):
    r = w3.shape[1]
    parts = []
    for chip in range(N_CHIPS):
        lo, hi = max(start - chip * r, 0), min(stop - chip * r, r)
        if lo < hi:
            part = w3[chip, lo:hi]
            parts.append(part if own is None else jnp.where(me == chip, own[lo:hi], part))
    return parts[0] if len(parts) == 1 else jnp.concatenate(parts, axis=0)


def _pack_small(g1, bfv, mix, scale, g2n, gf, extra=None):
    row8 = jnp.pad(bfv.reshape(1, N_HEADS), ((0, 0), (0, LANES - N_HEADS)))
    if extra is not None:
        row8 = row8 + jnp.pad(extra[:, :1], ((0, 0), (N_HEADS, LANES - N_HEADS - 1)))
    return jnp.concatenate([
        g1.reshape(8, LANES), jnp.pad(row8, ((0, 7), (0, 0))), mix.reshape(512, LANES),
        jnp.pad(scale.reshape(4, LANES), ((0, 4), (0, 0))), g2n.reshape(8, LANES), gf.reshape(8, LANES)], axis=0)


def _unpack_small(s, like):
    g1, bfv, mix, scale, g2n, gf = like
    return (s[0:8].reshape(g1.shape), s[8, :N_HEADS].reshape(bfv.shape), s[16:528].reshape(mix.shape),
            s[528:532].reshape(scale.shape), s[536:544].reshape(g2n.shape), s[544:552].reshape(gf.shape))


class _MeshLinks:
    def __init__(self, shards_in, shards_rest):
        self.gin = _gather_begin("in", shards_in, None, column_halves=True)
        self.grest = _gather_begin("rest", shards_rest, self.gin["token"])
        self.tokens = {"gather": self.grest["token"]}
        self.groups, self.flight, self.slot = {}, None, 0

    @property
    def token(self):
        return list(self.tokens.values())

    def tie(self, x):
        return _tie(x, self.token)

    def weights_in(self, after):
        st = _gather_forward(self.gin, after)
        (g,), (own,) = _gather_end(st, st["token"], merge=False)
        return g, own, 2 * lax.axis_index("x") + lax.axis_index("y")

    def rest_forward(self, after):
        self.grest = _gather_forward(self.grest, after)
        self.tokens["gather"] = self.grest["token"]

    def weights_rest(self, after):
        g = _gather_end(self.grest, after)
        del self.tokens["gather"]
        return [_col_sharded_full(g[0]), _col_sharded_full(g[1])] + [_row_sharded_full(a) for a in g[2:]]

    def advance(self, after, begin=()):
        slot = self.slot
        self.slot += 1
        if self.flight is not None:
            tags, sems, parts = self.flight
            arrays, plan, _ = _join_parts(parts)
            thru = _copies_wait(f"slot{slot}_wait", sems, arrays, plan, after)
            for tag, part in zip(tags, parts):
                self.groups[tag] = _reduce_next(self.groups[tag], thru[:len(part[0])])
                thru = thru[len(part[0]):]
        for st in begin:
            self.groups[st["tag"]] = st
        live = [(tag, st["part"]) for tag, st in self.groups.items() if "part" in st]
        self.flight = None
        self.tokens.pop("reduce", None)
        if live:
            arrays, plan, m = _join_parts([part for _, part in live])
            sems, thru, token = _copies_start(f"slot{slot}_start", arrays, plan, m)
            parts = []
            for _, (part_arrays, part_plan, part_m) in live:
                parts.append((thru[:len(part_arrays)], part_plan, part_m))
                thru = thru[len(part_arrays):]
            self.flight = ([tag for tag, _ in live], sems, parts)
            self.tokens["reduce"] = token

    def reduced(self, tag):
        return self.groups[tag]["done"]


class _NoLinks:
    token = None

    def __init__(self, w_in, rest):
        self.w_in, self.rest, self.grads = w_in, rest, {}

    def tie(self, x):
        return x

    def weights_in(self, after):
        return self.w_in, None, None

    def rest_forward(self, after):
        pass

    def weights_rest(self, after):
        return self.rest

    def advance(self, after, begin=()):
        for st in begin:
            self.grads[st["tag"]] = st["grads"]


def _local_step(links, x, target, seq, norm1_g, b_forget, pool_mix, pool_scale, norm2_g, norm_f_g):
    t, d = x.shape
    tq = min(256, seq)
    aw = ATTN_WIDTH
    o_q, o_f, o_g = POOL_WIDTH, POOL_WIDTH + 3 * aw, POOL_WIDTH + 3 * aw + N_HEADS
    bf = jnp.pad(b_forget, ((0, 0), (0, LANES - N_HEADS)))
    mixb = pool_mix.astype(BF16)

    h = _norm_fwd("norm1_fwd", x, links.tie(norm1_g), 512)
    w_in, own, me = links.weights_in(h)
    wu = _chip_rows(w_in, 0, o_q, own, me)
    wqkv = _chip_rows(w_in, o_q, o_f, own, me)
    wft = jnp.pad(_chip_rows(w_in, o_f, o_g, own, me), ((0, LANES - N_HEADS), (0, 0)))
    wg2 = _chip_rows(w_in, o_g, N_CHIPS * w_in.shape[1], own, me)
    wf = wft.T
    u = _matmul("mm_u", h, wu, "nt", F32, 1024, 512, d)
    g2 = _matmul("mm_gates", h, wg2, "nt", BF16, 1024, 1024, d)
    fl, fcum = _forget_fwd(h, wf, bf, seq)
    qa, ka, v = _attn_prep(h, _head_blocks(wqkv[:aw]), _head_blocks(wqkv[aw:2 * aw]), wqkv[2 * aw:], fcum, 1024)
    p, ps = _pool_fwd(u, mixb, pool_scale, seq)
    links.rest_forward([ps, qa, g2])
    o, lse = _attn_fwd(qa, ka, v, seq, tq, dep=links.token)
    w_pool_out, w_attn_out, w_out, w_ffn_gate, w_ffn_up, w_ffn_down = links.weights_rest(o)
    merged, x1 = _merge_fwd(x, ps, o, g2, w_pool_out, w_attn_out, w_out, 512)
    h2, gt, up, act, x2 = _ffn_fwd(x1, norm2_g, w_ffn_gate, w_ffn_up, w_ffn_down, 1024, 256)
    loss, dx2, d_gf = _final_fwd_bwd(x2, target, norm_f_g, 512)

    dgt, dup, dx1, d_g2n = _ffn_bwd(dx2, x1, norm2_g, gt, up, w_ffn_gate, w_ffn_up, w_ffn_down, 1024, 256)
    d_wd = _matmul("dw_down", act, dx2, "tn", F32, 1408, 1024, 1024)
    d_wg = _matmul("dw_gate", dgt, h2, "tn", F32, 1408, 1024, 1024)
    d_wu = _matmul("dw_up", dup, h2, "tn", F32, 1408, 1024, 1024)
    links.advance(None, begin=[_reduce_begin("a", [_row_sharded_to_comm(g) for g in (d_wg, d_wu, d_wd)])])
    dpy, day, dg2, dps, da = _merge_bwd(dx1, ps, o, g2, w_pool_out, w_attn_out, w_out, 512, dep=links.token)
    links.advance(dps)
    d_wout = _matmul("dw_out", merged, dx1, "tn", F32, 1024, 1024, 1024)
    d_wpo = _matmul("dw_pool_out", ps, dpy, "tn", F32, 512, 1024, 1024)
    d_wao = _matmul("dw_attn_out", o, day, "tn", F32, 512, 1024, 1024)
    dq, dk, dv, dfr = _attn_bwd(qa, ka, v, da, lse, seq, tq, dep=links.token)
    links.advance(dq, begin=[_reduce_begin(
        "m", [_col_sharded_to_comm(d_wpo), _col_sharded_to_comm(d_wao), _row_sharded_to_comm(d_wout)])])
    dfc = jnp.pad(dfr.reshape(N_HEADS, t).T, ((0, 0), (0, LANES - N_HEADS)))
    dfl, d_bf = _forget_bwd(dfc, fl, bf, seq)
    du, d_mix, d_scale = _pool_bwd(dps, p, mixb, links.tie(pool_scale), seq)
    d_wu_in = _matmul("dw_in_u", du, h, "tn", F32, 512, 1024, 1024)
    d_wq = _matmul("dw_in_q", dq, h, "tn", F32, 512, 1024, 1024, dep=links.token)
    d_wk = _matmul("dw_in_k", dk, h, "tn", F32, 512, 1024, 1024, dep=links.token)
    d_wv = _matmul("dw_in_v", dv, h, "tn", F32, 512, 1024, 1024, dep=links.token)
    links.advance([d_wu_in, d_wq, d_wk, d_wv])
    d_wf = _matmul("dw_in_f", dfl, h, "tn", F32, LANES, 1024, 512)
    d_wg2 = _matmul("dw_in_gates", dg2, h, "tn", F32, 1024, 1024, 1024, dep=links.token)
    d_win = jnp.concatenate([d_wu_in, d_wq, d_wk, d_wv, d_wf[:N_HEADS], d_wg2], axis=0)
    comm_b = [d_win.reshape(N_CHIPS, d_win.shape[0] // N_CHIPS, d)]
    small = (jnp.zeros_like(norm1_g), d_bf[:, :N_HEADS], d_mix, d_scale, d_g2n, d_gf)
    links.advance(comm_b, begin=[_reduce_begin("b", comm_b, column_halves=True),
                                 _small_begin("small", _pack_small(*small, extra=loss))])
    dx, d_g1 = _in_bwd(du, dq, dk, dv, dg2, dfl, dx1, x, links.tie(norm1_g), wu, wqkv, wg2, wft, 512)
    return loss, dx, (d_g1,) + small[1:]


def kernel(x, norm1_g, w_in, b_forget, pool_mix, pool_scale, w_pool_out, w_attn_out, w_out, norm2_g, w_ffn_gate, w_ffn_up, w_ffn_down, norm_f_g, loss_target, m_norm1_g, m_w_in, m_b_forget, m_pool_mix, m_pool_scale, m_w_pool_out, m_w_attn_out, m_w_out, m_norm2_g, m_w_ffn_gate, m_w_ffn_up, m_w_ffn_down, m_norm_f_g, v_norm1_g, v_w_in, v_b_forget, v_pool_mix, v_pool_scale, v_w_pool_out, v_w_attn_out, v_w_out, v_norm2_g, v_w_ffn_gate, v_w_ffn_up, v_w_ffn_down, v_norm_f_g):
    nb, seq, d = x.shape
    group_a = ((w_ffn_gate, m_w_ffn_gate, v_w_ffn_gate, True, 9), (w_ffn_up, m_w_ffn_up, v_w_ffn_up, True, 10),
               (w_ffn_down, m_w_ffn_down, v_w_ffn_down, False, 11))
    group_m = ((w_pool_out, m_w_pool_out, v_w_pool_out, False, 5), (w_attn_out, m_w_attn_out, v_w_attn_out, False, 6),
               (w_out, m_w_out, v_w_out, False, 7))
    group_b = ((w_in, m_w_in, v_w_in, False, 1),)
    small_w = (norm1_g, b_forget, pool_mix, pool_scale, norm2_g, norm_f_g)
    small_m = (m_norm1_g, m_b_forget, m_pool_mix, m_pool_scale, m_norm2_g, m_norm_f_g)
    small_v = (v_norm1_g, v_b_forget, v_pool_mix, v_pool_scale, v_norm2_g, v_norm_f_g)
    small_pos = (0, 2, 3, 4, 8, 12)
    view = lambda a, tr: a[0].T if tr else a[0]
    unview = lambda a, tr, like: (a.T if tr else a).reshape(like.shape)

    def shard(w, tr):
        lw = view(w, tr).astype(BF16)
        return lw.reshape(2, lw.shape[0] // 2, lw.shape[1])

    cm = lambda a: jnp.transpose(a, (2, 0, 1))
    shard_in = _rows_to_bf16("w_in_to_bf16", cm(w_in))
    links = _MeshLinks([shard_in],
                       [shard(w_pool_out, False), shard(w_attn_out, False), shard(w_out, False),
                        shard(w_ffn_gate, True), shard(w_ffn_up, True), shard(w_ffn_down, False)])
    loss, dx, small_g = _local_step(
        links, x.reshape(nb * seq, d), loss_target.reshape(nb * seq, d), seq,
        norm1_g, b_forget, pool_mix[0], pool_scale, norm2_g, norm_f_g.reshape(1, d))

    grads, deltas, new_m, new_v = [None] * 13, [None] * 13, [None] * 13, [None] * 13
    pos_c = jnp.stack([lax.axis_index("c")]).astype(jnp.int32)

    def update(tag, group, dep):
        last = []
        for k, ((w, m, v, tr, pos), (mine, other)) in enumerate(zip(group, links.reduced(tag))):
            outs = _adamw_halves(f"adamw_{tag}{k}", pos_c, view(w, tr), mine, other, view(m, tr), view(v, tr), 256,
                                 dep=dep)
            grads[pos], deltas[pos], new_m[pos], new_v[pos] = (unview(a, tr, w) for a in outs)
            last.append(outs[1])
        return last

    links.advance(dx, begin=[_small_begin("g1", small_g[0].reshape(8, LANES))])
    last = update("a", group_a, links.token)
    links.advance(last)
    small_sum = jnp.concatenate([_small_sum("g1_sum", *links.reduced("g1")[0]),
                                 _small_sum("small_sum", *links.reduced("small")[0])[8:]], axis=0)
    last = update("m", group_m, links.token)
    loss_out = small_sum[8, N_HEADS]
    dl, mn, vn = _adamw("adamw_small", _pack_small(*small_w), small_sum * _small_mask(), _pack_small(*small_m),
                        _pack_small(*small_v))
    for pos, g, a, b, e in zip(small_pos, _unpack_small(small_sum, small_w), _unpack_small(dl, small_w),
                               _unpack_small(mn, small_w), _unpack_small(vn, small_w)):
        grads[pos], deltas[pos], new_m[pos], new_v[pos] = g, a, b, e
    links.advance(last + [dl])
    links.advance(links.token)
    (mine, other), = links.reduced("b")
    outs = _adamw_rows("adamw_b0", pos_c, cm(w_in), mine, other, cm(m_w_in), cm(v_w_in))
    grads[1], deltas[1], new_m[1], new_v[1] = (jnp.transpose(a, (1, 2, 0)) for a in outs)

    return (loss_out, dx.reshape(nb, seq, d), *grads, *deltas, *new_m, *new_v)


def _small_mask():
    rows = lax.broadcasted_iota(jnp.int32, (552, LANES), 0)
    lanes = lax.broadcasted_iota(jnp.int32, (552, LANES), 1)
    return jnp.where(jnp.logical_and(rows == 8, lanes == N_HEADS), 0.0, 1.0).astype(F32)
```

```python
import jax
import jax.numpy as jnp
from jax import lax
from jax.experimental import pallas as pl
from jax.experimental.pallas import tpu as pltpu

F32 = jnp.float32
BF16 = jnp.bfloat16

D_MODEL = 1024
POOL_WINDOWS = (2, 4, 8, 16)
POOL_GROUPS = 4
POOL_GROUP_DIM = 128
POOL_WIDTH = 512
HEAD_DIM = 64
N_HEADS = 8
ATTN_WIDTH = 512
D_FF = 2816
RMS_EPS = 1e-6
ATTN_SCALE = HEAD_DIM ** -0.5
NEG_BIG = -1e30

ADAM_LR = 0.001
ADAM_B1 = 0.9
ADAM_B2 = 0.999
ADAM_EPS = 1e-08
ADAM_WD = 0.01
ADAM_STEP = 10

LANES = 128
N_CHIPS = 4
N_DEV = 8
VMEM_LIMIT_V7X = 52 * 1024 * 1024
ROW_CHUNK = 256
MESH = pl.DeviceIdType.MESH
ANY = pl.BlockSpec(memory_space=pl.ANY)


def _cparams(*sem):
    return pltpu.CompilerParams(dimension_semantics=sem if sem else None, vmem_limit_bytes=VMEM_LIMIT_V7X)


def _dep_list(dep):
    return [] if dep is None else (list(dep) if isinstance(dep, (list, tuple)) else [dep])


def _after(body, n_in, dep):
    k = len(_dep_list(dep))
    if k == 0:
        return body

    def wrapped(*refs):
        body(*refs[:n_in], *refs[n_in + k:])

    return wrapped


def _dep_args(dep):
    deps = _dep_list(dep)
    return [ANY] * len(deps), deps


def _dot(a, b):
    return lax.dot_general(a, b, (((1,), (0,)), ((), ())), preferred_element_type=F32)


def _dot_nt(a, b):
    return lax.dot_general(a, b, (((1,), (1,)), ((), ())), preferred_element_type=F32)


def _dot_tn(a, b):
    return lax.dot_general(a, b, (((0,), (0,)), ((), ())), preferred_element_type=F32)


def _sigmoid(x):
    return jax.nn.sigmoid(x)


def _rms_fwd(x, g):
    r = lax.rsqrt(jnp.mean(x * x, axis=-1, keepdims=True) + RMS_EPS)
    return (x * r) * g


def _rms_bwd(x, g, dy):
    r = lax.rsqrt(jnp.mean(x * x, axis=-1, keepdims=True) + RMS_EPS)
    xh = x * r
    dg = jnp.sum(dy * xh, axis=0, keepdims=True)
    dxh = dy * g
    dx = r * (dxh - xh * jnp.mean(dxh * xh, axis=-1, keepdims=True))
    return dx, dg


def _matmul(name, a, b, mode, out_dtype, tm, tn, tk, dep=None):
    if mode == "nn":
        (m, k), (_, n) = a.shape, b.shape
    elif mode == "nt":
        (m, k), (n, _) = a.shape, b.shape
    else:
        (k, m), (_, n) = a.shape, b.shape
    tm, tn, tk = min(tm, m), min(tn, n), min(tk, k)
    assert m % tm == 0 and n % tn == 0 and k % tk == 0, (name, m, n, k, tm, tn, tk)
    nk = k // tk
    if mode == "tn":
        a_spec = pl.BlockSpec((tk, tm), lambda i, j, kk: (kk, i))
    else:
        a_spec = pl.BlockSpec((tm, tk), lambda i, j, kk: (i, kk))
    if mode == "nt":
        b_spec = pl.BlockSpec((tn, tk), lambda i, j, kk: (j, kk))
    else:
        b_spec = pl.BlockSpec((tk, tn), lambda i, j, kk: (kk, j))
    dot = {"nn": _dot, "nt": _dot_nt, "tn": _dot_tn}[mode]
    use_scratch = nk > 1 and out_dtype != F32

    def body(a_ref, b_ref, o_ref, *scratch):
        if nk == 1 and mode != "tn":
            rows = min(ROW_CHUNK, tm)
            bb = b_ref[...].astype(BF16)
            for r0 in range(0, tm, rows):
                o_ref[r0:r0 + rows, :] = dot(a_ref[r0:r0 + rows, :].astype(BF16), bb).astype(out_dtype)
            return
        prod = dot(a_ref[...].astype(BF16), b_ref[...].astype(BF16))
        if nk == 1:
            o_ref[...] = prod.astype(out_dtype)
            return
        acc = scratch[0] if use_scratch else o_ref
        kk = pl.program_id(2)

        @pl.when(kk == 0)
        def _():
            acc[...] = prod

        @pl.when(kk > 0)
        def _():
            acc[...] += prod

        if use_scratch:
            @pl.when(kk == nk - 1)
            def _():
                o_ref[...] = acc[...].astype(out_dtype)

    dep_specs, dep_ops = _dep_args(dep)
    return pl.pallas_call(
        _after(body, 2, dep),
        name=name,
        out_shape=jax.ShapeDtypeStruct((m, n), out_dtype),
        grid=(m // tm, n // tn, nk),
        in_specs=[a_spec, b_spec] + dep_specs,
        out_specs=pl.BlockSpec((tm, tn), lambda i, j, kk: (i, j)),
        scratch_shapes=[pltpu.VMEM((tm, tn), F32)] if use_scratch else [],
        compiler_params=_cparams("parallel", "parallel", "arbitrary"),
    )(a, b, *dep_ops)


def _norm_fwd(name, x, g, tm):
    t, d = x.shape
    tm = min(tm, t)

    def body(x_ref, g_ref, h_ref):
        h_ref[...] = _rms_fwd(x_ref[...], g_ref[...]).astype(BF16)

    return pl.pallas_call(
        body, name=name, out_shape=jax.ShapeDtypeStruct((t, d), BF16), grid=(t // tm,),
        in_specs=[pl.BlockSpec((tm, d), lambda i: (i, 0)), pl.BlockSpec((1, d), lambda i: (0, 0))],
        out_specs=pl.BlockSpec((tm, d), lambda i: (i, 0)),
        compiler_params=_cparams("parallel"),
    )(x, g)


def _split3(x):
    hi = x.astype(BF16)
    r1 = x - hi.astype(F32)
    mid = r1.astype(BF16)
    lo = (r1 - mid.astype(F32)).astype(BF16)
    return hi, mid, lo


def _tri_dot(tri, x):
    hi, mid, lo = _split3(x)
    return _dot(tri, hi) + _dot(tri, mid) + _dot(tri, lo)


def _forget_fwd(h, wf, bf, seq):
    t, d = h.shape
    cb = min(256, seq)

    def body(h_ref, wf_ref, bf_ref, fl_ref, fc_ref):
        fl = _dot(h_ref[...], wf_ref[...])
        fl_ref[...] = fl
        xx = fl + bf_ref[...]
        lf = jnp.minimum(xx, 0.0) - jnp.log(1.0 + jnp.exp(-jnp.abs(xx)))
        ri = lax.broadcasted_iota(jnp.int32, (cb, cb), 0)
        ci = lax.broadcasted_iota(jnp.int32, (cb, cb), 1)
        tri = (ri >= ci).astype(BF16)
        carry = jnp.zeros((1, LANES), F32)
        for blk in range(seq // cb):
            cs = _tri_dot(tri, lf[blk * cb:(blk + 1) * cb]) + carry
            fc_ref[blk * cb:(blk + 1) * cb, :] = cs
            carry = cs[cb - 1:cb, :]

    return pl.pallas_call(
        body, name="forget_fwd",
        out_shape=(jax.ShapeDtypeStruct((t, LANES), F32), jax.ShapeDtypeStruct((t, LANES), F32)),
        grid=(t // seq,),
        in_specs=[pl.BlockSpec((seq, d), lambda b: (b, 0)), pl.BlockSpec((d, LANES), lambda b: (0, 0)),
                  pl.BlockSpec((1, LANES), lambda b: (0, 0))],
        out_specs=(pl.BlockSpec((seq, LANES), lambda b: (b, 0)), pl.BlockSpec((seq, LANES), lambda b: (b, 0))),
        compiler_params=_cparams("parallel"),
    )(h, wf, bf)


def _pool_fwd(u, mix, scale, seq):
    t = u.shape[0]

    def body(u_ref, mix_ref, sc_ref, p_ref, ps_ref):
        tpos = lax.broadcasted_iota(jnp.int32, (seq, POOL_GROUP_DIM), 0)
        for g in range(POOL_GROUPS):
            sl = slice(g * POOL_GROUP_DIM, (g + 1) * POOL_GROUP_DIM)
            ug = u_ref[:, sl]
            s = ug
            for lvl in range(g + 1):
                d = 2 ** lvl
                s = s + jnp.where(tpos >= d, pltpu.roll(s, d, 0), 0.0)
            cnt = jnp.minimum(tpos + 1, POOL_WINDOWS[g]).astype(F32)
            pb = (s / cnt - ug).astype(BF16)
            p_ref[:, sl] = pb
            ps_ref[:, sl] = (_dot(pb, mix_ref[g]) * sc_ref[:, sl]).astype(BF16)

    return pl.pallas_call(
        body, name="pool_fwd",
        out_shape=(jax.ShapeDtypeStruct((t, POOL_WIDTH), BF16), jax.ShapeDtypeStruct((t, POOL_WIDTH), BF16)),
        grid=(t // seq,),
        in_specs=[pl.BlockSpec((seq, POOL_WIDTH), lambda b: (b, 0)),
                  pl.BlockSpec((POOL_GROUPS, POOL_GROUP_DIM, POOL_GROUP_DIM), lambda b: (0, 0, 0)),
                  pl.BlockSpec((1, POOL_WIDTH), lambda b: (0, 0))],
        out_specs=(pl.BlockSpec((seq, POOL_WIDTH), lambda b: (b, 0)), pl.BlockSpec((seq, POOL_WIDTH), lambda b: (b, 0))),
        compiler_params=_cparams("parallel"),
    )(u, mix, scale)


def _aug_constants():
    w = N_HEADS * LANES
    rows = jnp.arange(3 * LANES)
    piece, head = rows // LANES, rows % LANES
    cols = jnp.arange(w)
    live = (head < N_HEADS)[:, None]
    pq = (live & (cols[None, :] == (head * LANES + HEAD_DIM + piece)[:, None])).astype(BF16)
    pk = -(live & (cols[None, :] == (head * LANES + HEAD_DIM + 3 + piece)[:, None])).astype(BF16)
    lane = cols % LANES
    oq = ((lane >= HEAD_DIM + 3) & (lane < HEAD_DIM + 6)).astype(F32)[None, :]
    ok = ((lane >= HEAD_DIM) & (lane < HEAD_DIM + 3)).astype(F32)[None, :]
    return pq, pk, oq, ok


def _head_blocks(wt):
    d = wt.shape[1]
    return jnp.pad(wt.reshape(N_HEADS, HEAD_DIM, d), ((0, 0), (0, LANES - HEAD_DIM), (0, 0))).reshape(N_HEADS * LANES, d)


def _attn_prep(h, wq, wk, wv, fcum, tm):
    t, d = h.shape
    tm = min(tm, t)
    rows = min(ROW_CHUNK, tm)
    w = N_HEADS * LANES
    pq, pk, oq, ok = _aug_constants()

    def body(h_ref, wq_ref, wk_ref, wv_ref, f_ref, pq_ref, pk_ref, oq_ref, ok_ref, qa_ref, ka_ref, v_ref):
        for r0 in range(0, tm, rows):
            rs = slice(r0, r0 + rows)
            hh = h_ref[rs, :]
            fs = jnp.concatenate(_split3(f_ref[rs, :]), axis=1)
            q = _dot_nt(hh, wq_ref[...]).astype(BF16).astype(F32) * ATTN_SCALE
            qa_ref[rs, :] = (q + _dot(fs, pq_ref[...]) + oq_ref[...]).astype(BF16)
            k = _dot_nt(hh, wk_ref[...]).astype(BF16).astype(F32)
            ka_ref[rs, :] = (k + _dot(fs, pk_ref[...]) + ok_ref[...]).astype(BF16)
            v_ref[rs, :] = _dot_nt(hh, wv_ref[...]).astype(BF16)

    row = lambda n: pl.BlockSpec((tm, n), lambda i: (i, 0))
    full = lambda a: pl.BlockSpec(a.shape, lambda i: (0, 0))
    return pl.pallas_call(
        body, name="attn_prep",
        out_shape=(jax.ShapeDtypeStruct((t, w), BF16), jax.ShapeDtypeStruct((t, w), BF16),
                   jax.ShapeDtypeStruct((t, ATTN_WIDTH), BF16)),
        grid=(t // tm,),
        in_specs=[row(d), full(wq), full(wk), full(wv), row(LANES), full(pq), full(pk), full(oq), full(ok)],
        out_specs=(row(w), row(w), row(ATTN_WIDTH)),
        compiler_params=_cparams("parallel"),
    )(h, wq, wk, wv, fcum, pq, pk, oq, ok)


def _fold_lanes(x, op):
    out = x[:, :LANES]
    for g in range(1, x.shape[1] // LANES):
        out = op(out, x[:, g * LANES:(g + 1) * LANES])
    return out


def _causal_sweep(i, tile, carry):
    def quad(jj, c):
        for u in range(4):
            c = tile(4 * jj + u, c, False)
        return c

    carry = lax.fori_loop(0, i // 4, quad, carry)
    base = 4 * (i // 4)
    carry = lax.cond(i % 4 >= 2, lambda c: tile(base + 1, tile(base, c, False), False), lambda c: c, carry)
    return lax.cond(i % 2 == 1, lambda c: tile(i, tile(i - 1, c, False), True), lambda c: tile(i, c, True), carry)


def _attn_fwd(qa, ka, v, seq, tq, dep=None):
    t = qa.shape[0]
    nq = seq // tq
    hp_n = N_HEADS // 2
    heads = [slice(e * LANES, (e + 1) * LANES) for e in range(2)]

    def body(q_ref, k_ref, v_ref, o_ref, lse_ref, s_buf):
        i = pl.program_id(2)
        diag_ok = lax.broadcasted_iota(jnp.int32, (tq, tq), 0) >= lax.broadcasted_iota(jnp.int32, (tq, tq), 1)
        qs = [q_ref[:, hl] for hl in heads]

        def sweep1(j, mxs, diagonal):
            r0 = pl.multiple_of(j * tq, tq)
            out = []
            for e, hl in enumerate(heads):
                s = _dot_nt(qs[e], k_ref[pl.ds(r0, tq), hl])
                if diagonal:
                    s = jnp.where(diag_ok, s, NEG_BIG)
                s_buf[e, j] = s
                out.append(jnp.maximum(mxs[e], _fold_lanes(s, jnp.maximum)))
            return tuple(out)

        mxs = _causal_sweep(i, sweep1, (jnp.full((tq, LANES), NEG_BIG, F32),) * 2)
        ms = [jnp.max(mx, axis=1, keepdims=True) for mx in mxs]

        def sweep2(j, carry, diagonal):
            r0 = pl.multiple_of(j * tq, tq)
            vv = v_ref[pl.ds(r0, tq), :]
            out = []
            for e in range(2):
                p = jnp.exp(s_buf[e, j] - ms[e])
                out += [carry[2 * e] + _fold_lanes(p, jnp.add), carry[2 * e + 1] + _dot(p.astype(BF16), vv)]
            return tuple(out)

        res = _causal_sweep(i, sweep2, (jnp.zeros((tq, LANES), F32),) * 4)
        outs = []
        for e in range(2):
            l = jnp.sum(res[2 * e], axis=1, keepdims=True)
            outs.append(res[2 * e + 1] / l)
            lse_ref[:, e:e + 1] = ms[e] + jnp.log(l)
        lane = lax.broadcasted_iota(jnp.int32, (tq, LANES), 1)
        o_ref[...] = jnp.where(lane < HEAD_DIM, outs[0], outs[1])

    dep_specs, dep_ops = _dep_args(dep)
    return pl.pallas_call(
        _after(body, 3, dep), name="attn_fwd",
        out_shape=(jax.ShapeDtypeStruct((t, ATTN_WIDTH), F32), jax.ShapeDtypeStruct((hp_n, t, 2), F32)),
        grid=(t // seq, hp_n, nq),
        in_specs=[pl.BlockSpec((tq, 2 * LANES), lambda b, hp, i: (b * nq + i, hp)),
                  pl.BlockSpec((seq, 2 * LANES), lambda b, hp, i: (b, hp)),
                  pl.BlockSpec((seq, LANES), lambda b, hp, i: (b, hp))] + dep_specs,
        out_specs=(pl.BlockSpec((tq, LANES), lambda b, hp, i: (b * nq + i, hp)),
                   pl.BlockSpec((None, tq, 2), lambda b, hp, i: (hp, b * nq + i, 0))),
        scratch_shapes=[pltpu.VMEM((2, nq, tq, tq), F32)],
        compiler_params=_cparams("parallel", "parallel", "arbitrary"),
    )(qa, ka, v, *dep_ops)


def _merge_fwd(x, ps, o, g2, wpo, wao, wout, tm):
    t, d = x.shape
    tm = min(tm, t)
    rows = min(ROW_CHUNK, tm)

    def body(x_ref, ps_ref, o_ref, gp_ref, ga_ref, wpo_ref, wao_ref, wout_ref, mg_ref, x1_ref):
        for r0 in range(0, tm, rows):
            rs = slice(r0, r0 + rows)
            py = _dot(ps_ref[rs, :], wpo_ref[...])
            ay = _dot(o_ref[rs, :].astype(BF16), wao_ref[...])
            mb = (_sigmoid(gp_ref[rs, :].astype(F32)) * py + _sigmoid(ga_ref[rs, :].astype(F32)) * ay).astype(BF16)
            mg_ref[rs, :] = mb
            x1_ref[rs, :] = x_ref[rs, :] + _dot(mb, wout_ref[...])

    row = lambda w: pl.BlockSpec((tm, w), lambda i: (i, 0))
    full = lambda a: pl.BlockSpec(a.shape, lambda i: (0, 0))
    return pl.pallas_call(
        body, name="merge_fwd",
        out_shape=(jax.ShapeDtypeStruct((t, d), BF16), jax.ShapeDtypeStruct((t, d), F32)),
        grid=(t // tm,),
        in_specs=[row(d), row(POOL_WIDTH), row(ATTN_WIDTH), pl.BlockSpec((tm, d), lambda i: (i, 0)),
                  pl.BlockSpec((tm, d), lambda i: (i, 1)), full(wpo), full(wao), full(wout)],
        out_specs=(row(d), row(d)),
        compiler_params=_cparams("parallel"),
    )(x, ps, o, g2, g2, wpo, wao, wout)


def _ffn_fwd(x1, g, wg, wu, wd, tm, tf):
    t, d = x1.shape
    f = wg.shape[0]
    tm = min(tm, t)
    nf = f // tf
    rows = min(512, tm)

    def body(x1_ref, g_ref, wg_ref, wu_ref, wd_ref, h2_ref, gt_ref, up_ref, act_ref, x2_ref):
        j = pl.program_id(1)

        @pl.when(j == 0)
        def _():
            h2_ref[...] = _rms_fwd(x1_ref[...], g_ref[...]).astype(BF16)

            x2_ref[...] = x1_ref[...]

        for r0 in range(0, tm, rows):
            rs = slice(r0, r0 + rows)
            h2 = h2_ref[rs, :]
            gt = _dot_nt(h2, wg_ref[...])
            up = _dot_nt(h2, wu_ref[...])
            sg = _sigmoid(gt)
            silu = gt * sg
            act = (silu * up).astype(BF16)
            gt_ref[rs, :] = (up * (sg * (1.0 + gt * (1.0 - sg)))).astype(BF16)
            up_ref[rs, :] = silu.astype(BF16)
            act_ref[rs, :] = act
            x2_ref[rs, :] += _dot(act, wd_ref[...])

    return pl.pallas_call(
        body, name="ffn_fwd",
        out_shape=(jax.ShapeDtypeStruct((t, d), BF16), jax.ShapeDtypeStruct((t, f), BF16),
                   jax.ShapeDtypeStruct((t, f), BF16), jax.ShapeDtypeStruct((t, f), BF16),
                   jax.ShapeDtypeStruct((t, d), F32)),
        grid=(t // tm, nf),
        in_specs=[pl.BlockSpec((tm, d), lambda i, j: (i, 0)), pl.BlockSpec((1, d), lambda i, j: (0, 0)),
                  pl.BlockSpec((tf, d), lambda i, j: (j, 0)), pl.BlockSpec((tf, d), lambda i, j: (j, 0)),
                  pl.BlockSpec((tf, d), lambda i, j: (j, 0))],
        out_specs=(pl.BlockSpec((tm, d), lambda i, j: (i, 0)), pl.BlockSpec((tm, tf), lambda i, j: (i, j)),
                   pl.BlockSpec((tm, tf), lambda i, j: (i, j)), pl.BlockSpec((tm, tf), lambda i, j: (i, j)),
                   pl.BlockSpec((tm, d), lambda i, j: (i, 0))),
        compiler_params=_cparams("parallel", "arbitrary"),
    )(x1, g, wg, wu, wd)


def _final_fwd_bwd(x2, target, g, tm):
    t, d = x2.shape
    tm = min(tm, t)

    def body(x_ref, t_ref, g_ref, loss_ref, dx_ref, dg_ref):
        i = pl.program_id(0)
        x = x_ref[...]
        gg = g_ref[...]
        err = _rms_fwd(x, gg) - t_ref[...]
        part = 0.5 * jnp.sum(jnp.mean(err * err, axis=-1, keepdims=True), axis=0, keepdims=True)
        dx, dg = _rms_bwd(x, gg, err * (1.0 / d))
        dx_ref[...] = dx

        @pl.when(i == 0)
        def _():
            loss_ref[...] = jnp.zeros_like(loss_ref)
            dg_ref[...] = jnp.zeros_like(dg_ref)

        loss_ref[...] += jnp.broadcast_to(part, loss_ref.shape)
        dg_ref[...] += dg

    return pl.pallas_call(
        body, name="final_fwd_bwd",
        out_shape=(jax.ShapeDtypeStruct((1, LANES), F32), jax.ShapeDtypeStruct((t, d), F32),
                   jax.ShapeDtypeStruct((1, d), F32)),
        grid=(t // tm,),
        in_specs=[pl.BlockSpec((tm, d), lambda i: (i, 0)), pl.BlockSpec((tm, d), lambda i: (i, 0)),
                  pl.BlockSpec((1, d), lambda i: (0, 0))],
        out_specs=(pl.BlockSpec((1, LANES), lambda i: (0, 0)), pl.BlockSpec((tm, d), lambda i: (i, 0)),
                   pl.BlockSpec((1, d), lambda i: (0, 0))),
        compiler_params=_cparams("arbitrary"),
    )(x2, target, g)


def _ffn_bwd(dx2, x1, g, gt, up, wg, wu, wd, tm, tf):
    t, d = dx2.shape
    f = gt.shape[1]
    tm = min(tm, t)
    nf = f // tf
    wgu = jnp.concatenate([wg.reshape(nf, tf, d), wu.reshape(nf, tf, d)], axis=1).reshape(2 * f, d)
    rows = min(256, tm)

    def body(dx2_ref, x1_ref, g_ref, gt_ref, up_ref, wgu_ref, wd_ref, dgt_ref, dup_ref, dx1_ref, dg_ref, acc_ref,
             dxb_ref):
        i, j = pl.program_id(0), pl.program_id(1)

        @pl.when(j == 0)
        def _():
            dxb_ref[...] = dx2_ref[...].astype(BF16)
            acc_ref[...] = jnp.zeros_like(acc_ref)

        for r0 in range(0, tm, rows):
            rs = slice(r0, r0 + rows)
            dact = _dot_nt(dxb_ref[rs, :], wd_ref[...])
            dgt = (dact * gt_ref[rs, :].astype(F32)).astype(BF16)
            dup = (dact * up_ref[rs, :].astype(F32)).astype(BF16)
            dgt_ref[rs, :] = dgt
            dup_ref[rs, :] = dup
            acc_ref[rs, :] += _dot(jnp.concatenate([dgt, dup], axis=1), wgu_ref[...])

        @pl.when(jnp.logical_and(i == 0, j == 0))
        def _():
            dg_ref[...] = jnp.zeros_like(dg_ref)

        @pl.when(j == nf - 1)
        def _():
            dxn, dg = _rms_bwd(x1_ref[...], g_ref[...], acc_ref[...])
            dx1_ref[...] = dx2_ref[...] + dxn
            dg_ref[...] += dg

    return pl.pallas_call(
        body, name="ffn_bwd",
        out_shape=(jax.ShapeDtypeStruct((t, f), BF16), jax.ShapeDtypeStruct((t, f), BF16),
                   jax.ShapeDtypeStruct((t, d), F32), jax.ShapeDtypeStruct((1, d), F32)),
        grid=(t // tm, nf),
        in_specs=[pl.BlockSpec((tm, d), lambda i, j: (i, 0)), pl.BlockSpec((tm, d), lambda i, j: (i, 0)),
                  pl.BlockSpec((1, d), lambda i, j: (0, 0)),
                  pl.BlockSpec((tm, tf), lambda i, j: (i, j)), pl.BlockSpec((tm, tf), lambda i, j: (i, j)),
                  pl.BlockSpec((2 * tf, d), lambda i, j: (j, 0)), pl.BlockSpec((tf, d), lambda i, j: (j, 0))],
        out_specs=(pl.BlockSpec((tm, tf), lambda i, j: (i, j)), pl.BlockSpec((tm, tf), lambda i, j: (i, j)),
                   pl.BlockSpec((tm, d), lambda i, j: (i, 0)), pl.BlockSpec((1, d), lambda i, j: (0, 0))),
        scratch_shapes=[pltpu.VMEM((tm, d), F32), pltpu.VMEM((tm, d), BF16)],
        compiler_params=_cparams("arbitrary", "arbitrary"),
    )(dx2, x1, g, gt, up, wgu, wd)


def _merge_bwd(dx1, ps, o, g2, wpo, wao, wout, tm, dep=None):
    t, d = dx1.shape
    tm = min(tm, t)
    rows = min(ROW_CHUNK, tm)

    def body(dx1_ref, ps_ref, o_ref, gp_ref, ga_ref, wpo_ref, wao_ref, wout_ref, dpy_ref, day_ref, dg2_ref, dps_ref, da_ref):
        for r0 in range(0, tm, rows):
            rs = slice(r0, r0 + rows)
            dm = _dot_nt(dx1_ref[rs, :].astype(BF16), wout_ref[...])
            py = _dot(ps_ref[rs, :], wpo_ref[...])
            ay = _dot(o_ref[rs, :].astype(BF16), wao_ref[...])
            sp = _sigmoid(gp_ref[rs, :].astype(F32))
            sa = _sigmoid(ga_ref[rs, :].astype(F32))
            dpy = (dm * sp).astype(BF16)
            day = (dm * sa).astype(BF16)
            dpy_ref[rs, :] = dpy
            day_ref[rs, :] = day
            dg2_ref[rs, :d] = (dm * py * (sp * (1.0 - sp))).astype(BF16)
            dg2_ref[rs, d:] = (dm * ay * (sa * (1.0 - sa))).astype(BF16)
            dps_ref[rs, :] = _dot_nt(dpy, wpo_ref[...])
            da_ref[rs, :] = _dot_nt(day, wao_ref[...]).astype(BF16)

    row = lambda w: pl.BlockSpec((tm, w), lambda i: (i, 0))
    full = lambda a: pl.BlockSpec(a.shape, lambda i: (0, 0))
    dep_specs, dep_ops = _dep_args(dep)
    return pl.pallas_call(
        _after(body, 8, dep), name="merge_bwd",
        out_shape=(jax.ShapeDtypeStruct((t, d), BF16), jax.ShapeDtypeStruct((t, d), BF16),
                   jax.ShapeDtypeStruct((t, 2 * d), BF16), jax.ShapeDtypeStruct((t, POOL_WIDTH), F32),
                   jax.ShapeDtypeStruct((t, ATTN_WIDTH), BF16)),
        grid=(t // tm,),
        in_specs=[row(d), row(POOL_WIDTH), row(ATTN_WIDTH), pl.BlockSpec((tm, d), lambda i: (i, 0)),
                  pl.BlockSpec((tm, d), lambda i: (i, 1)), full(wpo), full(wao), full(wout)] + dep_specs,
        out_specs=(row(d), row(d), row(2 * d), row(POOL_WIDTH), row(ATTN_WIDTH)),
        compiler_params=_cparams("parallel"),
    )(dx1, ps, o, g2, g2, wpo, wao, wout, *dep_ops)


def _attn_bwd(qa, ka, v, do, lse4, seq, tq, dep=None):
    t = qa.shape[0]
    nq = seq // tq
    hp_n = N_HEADS // 2
    heads = [slice(e * LANES, (e + 1) * LANES) for e in range(2)]

    def body(q_ref, k_ref, v_ref, do_ref, lse_ref, dq_ref, dk_ref, dv_ref, dfr_ref, dk_acc, dv_acc, p_buf, dp_buf):
        diag_ok = lax.broadcasted_iota(jnp.int32, (tq, tq), 0) >= lax.broadcasted_iota(jnp.int32, (tq, tq), 1)
        lane_q = lax.broadcasted_iota(jnp.int32, (tq, LANES), 1)
        mine_q = [lane_q < HEAD_DIM, lane_q >= HEAD_DIM]
        dv_acc[...] = jnp.zeros_like(dv_acc)
        dk_acc[...] = jnp.zeros_like(dk_acc)
        dfr_ref[...] = jnp.zeros_like(dfr_ref)
        transposed = lambda a: a.astype(F32).T.astype(BF16)

        def q_step(i, _):
            q0 = pl.multiple_of(i * tq, tq)
            qs = [q_ref[pl.ds(q0, tq), hl] for hl in heads]
            dov = do_ref[pl.ds(q0, tq), :]
            dos = [jnp.where(mq, dov, jnp.zeros((), BF16)) for mq in mine_q]
            qts = [transposed(q) for q in qs]
            dots = [transposed(a) for a in dos]
            lss = [lse_ref[pl.ds(q0, tq), e:e + 1] for e in range(2)]

            def sweep1(j, dls, diagonal):
                r0 = pl.multiple_of(j * tq, tq)
                vv = v_ref[pl.ds(r0, tq), :]
                out = []
                for e, hl in enumerate(heads):
                    s = _dot_nt(qs[e], k_ref[pl.ds(r0, tq), hl])
                    if diagonal:
                        s = jnp.where(diag_ok, s, NEG_BIG)
                    p = jnp.exp(s - lss[e])
                    dp = _dot_nt(dos[e], vv)
                    p_buf[e, j] = p
                    dp_buf[e, j] = dp
                    dv_acc[j] += _dot(dots[e], p.astype(BF16))
                    out.append(dls[e] + _fold_lanes(p * dp, jnp.add))
                return tuple(out)

            dls = _causal_sweep(i, sweep1, (jnp.zeros((tq, LANES), F32),) * 2)
            dls = [jnp.sum(d, axis=1, keepdims=True) for d in dls]

            def sweep2(j, dqs, diagonal):
                r0 = pl.multiple_of(j * tq, tq)
                out = []
                for e, hl in enumerate(heads):
                    ds = p_buf[e, j] * (dp_buf[e, j] - dls[e])
                    dfr_ref[e, pl.ds(j, 1), :] += jnp.sum(ds, axis=0, keepdims=True)
                    dsb = ds.astype(BF16)
                    dk_acc[e, j] += _dot(qts[e], dsb)
                    out.append(dqs[e] + _dot(dsb, k_ref[pl.ds(r0, tq), hl]))
                return tuple(out)

            dqs = _causal_sweep(i, sweep2, (jnp.zeros((tq, LANES), F32),) * 2)
            dq = jnp.where(mine_q[0], dqs[0], pltpu.roll(dqs[1], HEAD_DIM, 1)) * ATTN_SCALE
            dq_ref[pl.ds(q0, tq), :] = dq.astype(BF16)
            return 0

        lax.fori_loop(0, nq, q_step, 0)
        for j in range(nq):
            rs = slice(j * tq, (j + 1) * tq)
            dk = jnp.where(mine_q[0], dk_acc[0, j].T, pltpu.roll(dk_acc[1, j].T, HEAD_DIM, 1))
            dk_ref[rs, :] = dk.astype(BF16)
            dv_ref[rs, :] = dv_acc[j].T.astype(BF16)

    wide = pl.BlockSpec((seq, 2 * LANES), lambda b, hp: (b, hp))
    col = pl.BlockSpec((seq, LANES), lambda b, hp: (b, hp))
    pair = pl.BlockSpec((None, seq, 2), lambda b, hp: (hp, b, 0))
    dep_specs, dep_ops = _dep_args(dep)
    return pl.pallas_call(
        _after(body, 5, dep), name="attn_bwd",
        out_shape=(jax.ShapeDtypeStruct((t, ATTN_WIDTH), BF16),) * 3 + (jax.ShapeDtypeStruct((N_HEADS, t // tq, tq), F32),),
        grid=(t // seq, hp_n),
        in_specs=[wide, wide, col, col, pair] + dep_specs,
        out_specs=(col, col, col, pl.BlockSpec((2, nq, tq), lambda b, hp: (hp, b, 0))),
        scratch_shapes=[pltpu.VMEM((2, nq, LANES, tq), F32), pltpu.VMEM((nq, LANES, tq), F32),
                        pltpu.VMEM((2, nq, tq, tq), F32), pltpu.VMEM((2, nq, tq, tq), F32)],
        compiler_params=_cparams("parallel", "arbitrary"),
    )(qa, ka, v, do, lse4, *dep_ops)


def _forget_bwd(dfc, fl, bf, seq):
    t = fl.shape[0]
    cb = min(256, seq)
    nb = seq // cb

    def body(dfc_ref, fl_ref, bf_ref, dfl_ref, db_ref):
        b = pl.program_id(0)
        ri = lax.broadcasted_iota(jnp.int32, (cb, cb), 0)
        ci = lax.broadcasted_iota(jnp.int32, (cb, cb), 1)
        tri = (ci >= ri).astype(BF16)
        carry = jnp.zeros((1, LANES), F32)
        dbs = jnp.zeros((1, LANES), F32)
        for blk in reversed(range(nb)):
            rs = slice(blk * cb, (blk + 1) * cb)
            dlf = _tri_dot(tri, -dfc_ref[rs, :]) + carry
            carry = dlf[0:1, :]
            dfl = dlf * _sigmoid(-(fl_ref[rs, :] + bf_ref[...]))
            dfl_ref[rs, :] = dfl.astype(BF16)
            dbs = dbs + jnp.sum(dfl, axis=0, keepdims=True)

        @pl.when(b == 0)
        def _():
            db_ref[...] = jnp.zeros_like(db_ref)

        db_ref[...] += dbs

    return pl.pallas_call(
        body, name="forget_bwd",
        out_shape=(jax.ShapeDtypeStruct((t, LANES), BF16), jax.ShapeDtypeStruct((1, LANES), F32)),
        grid=(t // seq,),
        in_specs=[pl.BlockSpec((seq, LANES), lambda b: (b, 0)), pl.BlockSpec((seq, LANES), lambda b: (b, 0)),
                  pl.BlockSpec((1, LANES), lambda b: (0, 0))],
        out_specs=(pl.BlockSpec((seq, LANES), lambda b: (b, 0)), pl.BlockSpec((1, LANES), lambda b: (0, 0))),
        compiler_params=_cparams("arbitrary"),
    )(dfc, fl, bf)


def _pool_bwd(dps, p, mix, scale, seq):
    t = dps.shape[0]

    def body(dps_ref, p_ref, mix_ref, sc_ref, du_ref, dmix_ref, dsc_ref):
        b = pl.program_id(0)

        @pl.when(b == 0)
        def _():
            dmix_ref[...] = jnp.zeros_like(dmix_ref)
            dsc_ref[...] = jnp.zeros_like(dsc_ref)

        tpos = lax.broadcasted_iota(jnp.int32, (seq, POOL_GROUP_DIM), 0)
        for g in range(POOL_GROUPS):
            sl = slice(g * POOL_GROUP_DIM, (g + 1) * POOL_GROUP_DIM)
            pb = p_ref[:, sl]
            dpsg = dps_ref[:, sl]
            pm = _dot(pb, mix_ref[g])
            dsc_ref[:, sl] += jnp.sum(dpsg * pm, axis=0, keepdims=True)
            dpm = (dpsg * sc_ref[:, sl]).astype(BF16)
            dmix_ref[g] += _dot_tn(pb, dpm)
            dp = _dot_nt(dpm, mix_ref[g])
            cnt = jnp.minimum(tpos + 1, POOL_WINDOWS[g]).astype(F32)
            s = dp / cnt
            for lvl in range(g + 1):
                d = 2 ** lvl
                s = s + jnp.where(tpos < seq - d, pltpu.roll(s, seq - d, 0), 0.0)
            du_ref[:, sl] = (s - dp).astype(BF16)

    return pl.pallas_call(
        body, name="pool_bwd",
        out_shape=(jax.ShapeDtypeStruct((t, POOL_WIDTH), BF16),
                   jax.ShapeDtypeStruct((POOL_GROUPS, POOL_GROUP_DIM, POOL_GROUP_DIM), F32),
                   jax.ShapeDtypeStruct((1, POOL_WIDTH), F32)),
        grid=(t // seq,),
        in_specs=[pl.BlockSpec((seq, POOL_WIDTH), lambda b: (b, 0)), pl.BlockSpec((seq, POOL_WIDTH), lambda b: (b, 0)),
                  pl.BlockSpec((POOL_GROUPS, POOL_GROUP_DIM, POOL_GROUP_DIM), lambda b: (0, 0, 0)),
                  pl.BlockSpec((1, POOL_WIDTH), lambda b: (0, 0))],
        out_specs=(pl.BlockSpec((seq, POOL_WIDTH), lambda b: (b, 0)),
                   pl.BlockSpec((POOL_GROUPS, POOL_GROUP_DIM, POOL_GROUP_DIM), lambda b: (0, 0, 0)),
                   pl.BlockSpec((1, POOL_WIDTH), lambda b: (0, 0))),
        compiler_params=_cparams("arbitrary"),
    )(dps, p, mix, scale)


def _in_bwd(du, dq, dk, dv, dg2, dfl, dx1, x, g, wu, wqkv, wg2, wft, tm):
    t, d = x.shape
    tm = min(tm, t)
    rows = min(ROW_CHUNK, tm)
    aw = ATTN_WIDTH

    def body(du_ref, dq_ref, dk_ref, dv_ref, dg2_ref, dfl_ref, dx1_ref, x_ref, g_ref, wu_ref, wqkv_ref, wg2_ref, wft_ref,
             dx_ref, dg_ref):
        i = pl.program_id(0)

        @pl.when(i == 0)
        def _():
            dg_ref[...] = jnp.zeros_like(dg_ref)

        for r0 in range(0, tm, rows):
            rs = slice(r0, r0 + rows)
            dh = _dot(du_ref[rs, :], wu_ref[...])
            dh += _dot(dq_ref[rs, :], wqkv_ref[0:aw, :])
            dh += _dot(dk_ref[rs, :], wqkv_ref[aw:2 * aw, :])
            dh += _dot(dv_ref[rs, :], wqkv_ref[2 * aw:3 * aw, :])
            dh += _dot(dg2_ref[rs, :], wg2_ref[...])
            dh += _dot(dfl_ref[rs, :], wft_ref[...])
            dxn, dg = _rms_bwd(x_ref[rs, :], g_ref[...], dh)
            dx_ref[rs, :] = dx1_ref[rs, :] + dxn
            dg_ref[...] += dg

    row = lambda w: pl.BlockSpec((tm, w), lambda i: (i, 0))
    full = lambda a: pl.BlockSpec(a.shape, lambda i: (0, 0))
    return pl.pallas_call(
        body, name="in_bwd",
        out_shape=(jax.ShapeDtypeStruct((t, d), F32), jax.ShapeDtypeStruct((1, d), F32)),
        grid=(t // tm,),
        in_specs=[row(POOL_WIDTH), row(aw), row(aw), row(aw), row(2 * d), row(LANES), row(d), row(d),
                  pl.BlockSpec((1, d), lambda i: (0, 0)), full(wu), full(wqkv), full(wg2), full(wft)],
        out_specs=(row(d), pl.BlockSpec((1, d), lambda i: (0, 0))),
        compiler_params=_cparams("arbitrary"),
    )(du, dq, dk, dv, dg2, dfl, dx1, x, g, wu, wqkv, wg2, wft)


def _position():
    return lax.axis_index("x"), lax.axis_index("y"), lax.axis_index("c")


def _remote(src, dst, send_sem, recv_sem, device):
    return pltpu.make_async_remote_copy(src_ref=src, dst_ref=dst, send_sem=send_sem, recv_sem=recv_sem,
                                        device_id=device, device_id_type=MESH)


HBM = pl.BlockSpec(memory_space=pltpu.HBM)
SEM = pl.BlockSpec(memory_space=pltpu.SEMAPHORE)
DATAFLOW = pltpu.SideEffectType.DATAFLOW_SIDE_EFFECTING


def _copies_start(name, arrays, plan, m, dep=None):
    n = len(arrays)
    arrays = [pltpu.with_memory_space_constraint(a, pltpu.HBM) for a in arrays]

    def body(*refs):
        ins, send_sem, recv_sem, token = refs[:n], refs[n], refs[n + 1], refs[2 * n + 2]
        for i, (src, dst, device, _) in enumerate(plan(ins, *_position())):
            _remote(src, dst, send_sem.at[i], recv_sem.at[i], device).start()
        token[...] = jnp.zeros_like(token)

    dep_specs, dep_ops = _dep_args(dep)
    outs = pl.pallas_call(
        _after(body, n, dep), name=name,
        out_shape=(pltpu.SemaphoreType.DMA((m,)), pltpu.SemaphoreType.DMA((m,)),
                   *[pltpu.HBM(a.shape, a.dtype) for a in arrays], jax.ShapeDtypeStruct((8, LANES), F32)),
        in_specs=[HBM] * n + dep_specs, out_specs=(SEM, SEM, *[HBM] * n, pl.BlockSpec(memory_space=pltpu.VMEM)),
        input_output_aliases={i: i + 2 for i in range(n)},
        compiler_params=pltpu.CompilerParams(has_side_effects=DATAFLOW),
    )(*arrays, *dep_ops)
    return (outs[0], outs[1]), list(outs[2:2 + n]), outs[2 + n]


def _copies_wait(name, sems, arrays, plan, after):
    n = len(arrays)
    afters = list(after) if isinstance(after, (list, tuple)) else [after]

    def body(*refs):
        ins, send_sem, recv_sem = refs[:n], refs[n], refs[n + 1]
        for i, (src, dst, device, landing) in enumerate(plan(ins, *_position())):
            _remote(src, dst, send_sem.at[i], recv_sem.at[i], device).wait_send()
            _remote(landing, landing, send_sem.at[i], recv_sem.at[i], device).wait_recv()

    outs = pl.pallas_call(
        body, name=name,
        out_shape=tuple(pltpu.HBM(a.shape, a.dtype) for a in arrays),
        in_specs=[HBM] * n + [SEM, SEM] + [ANY] * len(afters), out_specs=tuple([HBM] * n),
        input_output_aliases={i: i for i in range(n)},
        compiler_params=pltpu.CompilerParams(has_side_effects=DATAFLOW),
    )(*arrays, sems[0], sems[1], *afters)
    return list(outs)


def _tie(x, dep):
    for token in _dep_list(dep):
        x = x + token[0, 0]
    return x


def _other_chips(x, y):
    return [(1 - x, y), (x, 1 - y), (1 - x, 1 - y)]


def _gather_begin(tag, shards, token, column_halves=False):
    n = len(shards)
    lands = [lax.empty((N_CHIPS,) + s.shape, s.dtype) for s in shards]
    if column_halves:
        cols = lambda ref, h: pl.ds(pl.multiple_of(h * (ref.shape[-1] // 2), LANES), ref.shape[-1] // 2)
        mine = lambda ref, h: ref.at[:, cols(ref, h)]
        landed = lambda ref, chip, h: ref.at[chip, :, cols(ref, h)]
    else:
        mine = lambda ref, h: ref.at[h]
        landed = lambda ref, chip, h: ref.at[chip, h]

    def plan(refs, x, y, c):
        return [(mine(refs[k], c), landed(refs[n + k], 2 * x + y, c), (ox, oy, c), landed(refs[n + k], 2 * ox + oy, c))
                for k in range(n) for ox, oy in _other_chips(x, y)]

    sems, thru, token = _copies_start(f"gather_{tag}_ici_start", list(shards) + lands, plan, 3 * n, dep=token)
    return dict(tag=tag, n=n, plan=plan, sems=sems, arrays=thru, token=token, landed=landed)


def _gather_forward(st, after):
    n, tag, landed = st["n"], st["tag"], st["landed"]
    thru = _copies_wait(f"gather_{tag}_ici_wait", st["sems"], st["arrays"], st["plan"], after)

    def plan(refs, x, y, c):
        return [(landed(refs[k], 2 * ox + oy, c), landed(refs[k], 2 * ox + oy, c), (x, y, 1 - c),
                 landed(refs[k], 2 * ox + oy, 1 - c))
                for k in range(n) for ox, oy in _other_chips(x, y)]

    sems, lands, token = _copies_start(f"gather_{tag}_fwd_start", thru[n:], plan, 3 * n)
    return dict(tag=tag, n=n, plan=plan, sems=sems, arrays=lands, token=token, shards=thru[:n])


def _gather_end(st, after, merge=True):
    lands = _copies_wait(f"gather_{st['tag']}_fwd_wait", st["sems"], st["arrays"], st["plan"], after)
    if not merge:
        return lands, st["shards"]
    me = 2 * lax.axis_index("x") + lax.axis_index("y")
    return [lax.dynamic_update_index_in_dim(g, s, me, 0) for g, s in zip(lands, st["shards"])]


def _add_keep_give(name, pos, a, a_keep, a_give, b, b_keep, b_give, steps):
    r, c = b.shape[-2:]

    def spec(arr, fn):
        lead = arr.ndim - 2

        def index(i, p):
            idx = tuple(fn(i, p))
            return idx if len(idx) == arr.ndim else idx + (0, 0)

        return pl.BlockSpec((None,) * lead + (r, c), index)

    out_spec = pl.BlockSpec((None, r, c), lambda i, p: (i, 0, 0))

    def body(p_ref, ak_ref, bk_ref, ag_ref, bg_ref, keep_ref, give_ref):
        keep_ref[...] = ak_ref[...] + bk_ref[...].astype(F32)
        give_ref[...] = (ag_ref[...] + bg_ref[...].astype(F32)).astype(BF16)

    return pl.pallas_call(
        body, name=name,
        out_shape=(jax.ShapeDtypeStruct((steps, r, c), F32), jax.ShapeDtypeStruct((steps, r, c), BF16)),
        grid_spec=pltpu.PrefetchScalarGridSpec(
            num_scalar_prefetch=1, grid=(steps,),
            in_specs=[spec(a, a_keep), spec(b, b_keep), spec(a, a_give), spec(b, b_give)],
            out_specs=(out_spec, out_spec)),
        compiler_params=_cparams("parallel"),
    )(pos, a, b, a, b)


def _add_last(name, a, b):
    _, r, c = a.shape
    blk = pl.BlockSpec((None, r, c), lambda i: (0, 0, 0))

    def body(a_ref, b_ref, o_ref):
        o_ref[...] = a_ref[...] + b_ref[...].astype(F32)

    return pl.pallas_call(
        body, name=name, out_shape=jax.ShapeDtypeStruct((r, c), F32), grid=(1,), in_specs=[blk, blk],
        out_specs=pl.BlockSpec((r, c), lambda i: (0, 0)), compiler_params=_cparams("arbitrary"),
    )(a, b)


def _exchange_part(gives, lands, peer_fn):
    n = len(gives)

    def plan(refs, x, y, c):
        return [(refs[k], refs[n + k], peer_fn(x, y, c), refs[n + k]) for k in range(n)]

    return list(gives) + list(lands), plan, n


def _join_parts(parts):
    offsets, total = [], 0
    for arrays, _, _ in parts:
        offsets.append(total)
        total += len(arrays)

    def plan(refs, x, y, c):
        copies = []
        for (arrays, part_plan, _), off in zip(parts, offsets):
            copies += part_plan(refs[off:off + len(arrays)], x, y, c)
        return copies

    return [a for arrays, _, _ in parts for a in arrays], plan, sum(m for _, _, m in parts)


def _reduce_begin(tag, grads, column_halves=False):
    n = len(grads)
    if column_halves:
        half = lambda ref, j, h: ref.at[j, :, pl.ds(pl.multiple_of(h * (ref.shape[2] // 2), LANES), ref.shape[2] // 2)]
        lands = [lax.empty((N_CHIPS, g.shape[1], g.shape[2] // 2), F32) for g in grads]
    else:
        half = lambda ref, j, h: ref.at[j, h]
        lands = [lax.empty((N_CHIPS,) + g.shape[2:], F32) for g in grads]

    def plan(refs, x, y, c):
        return [(half(refs[k], j, 1 - c), refs[n + k].at[j], (x, y, 1 - c), refs[n + k].at[j])
                for k in range(n) for j in range(N_CHIPS)]

    return dict(tag=tag, n=n, stage="c", grads=list(grads), column_halves=column_halves,
                part=(list(grads) + lands, plan, N_CHIPS * n))


def _reduce_next(st, thru):
    tag, n, stage = st["tag"], st["n"], st["stage"]
    first, recv = thru[:n], thru[n:]
    x, y, c = _position()
    if stage == "c":
        pos = jnp.stack([c, x]).astype(jnp.int32)
        if st["column_halves"]:
            mine = lambda chip: (lambda i, p: (chip(p) + i, 0, p[0]))
        else:
            mine = lambda chip: (lambda i, p: (chip(p) + i, p[0]))
        sums = [_add_keep_give(
            f"rs{tag}_c_add{k}", pos,
            first[k], mine(lambda p: 2 * p[1]), mine(lambda p: 2 * (1 - p[1])),
            recv[k], lambda i, p: (2 * p[1] + i,), lambda i, p: (2 * (1 - p[1]) + i,), 2) for k in range(n)]
        lands = [lax.empty(s[1].shape, BF16) for s in sums]
        return dict(tag=tag, n=n, stage="x", keep=[s[0] for s in sums],
                    part=_exchange_part([s[1] for s in sums], lands, lambda x, y, c: (1 - x, y, c)))
    if stage == "x":
        pos = jnp.stack([y]).astype(jnp.int32)
        sums = [_add_keep_give(
            f"rs{tag}_x_add{k}", pos,
            st["keep"][k], lambda i, p: (p[0],), lambda i, p: (1 - p[0],),
            recv[k], lambda i, p: (p[0],), lambda i, p: (1 - p[0],), 1) for k in range(n)]
        lands = [lax.empty(s[1].shape, BF16) for s in sums]
        return dict(tag=tag, n=n, stage="y", keep=[s[0] for s in sums],
                    part=_exchange_part([s[1] for s in sums], lands, lambda x, y, c: (x, 1 - y, c)))
    if stage == "y":
        mine = [_add_last(f"rs{tag}_y_add{k}", st["keep"][k], recv[k]) for k in range(n)]
        lands = [lax.empty(m.shape, F32) for m in mine]
        return dict(tag=tag, n=n, stage="swap", part=_exchange_part(mine, lands, lambda x, y, c: (x, y, 1 - c)))
    return dict(tag=tag, done=list(zip(first, recv)))


def _small_begin(tag, v):
    land = lax.empty((N_DEV,) + v.shape, F32)
    flips = [(fx, fy, fc) for fx in (0, 1) for fy in (0, 1) for fc in (0, 1)][1:]

    def plan(refs, x, y, c):
        copies = []
        for fx, fy, fc in flips:
            px, py, pc = (1 - x if fx else x), (1 - y if fy else y), (1 - c if fc else c)
            copies.append((refs[0], refs[1].at[4 * x + 2 * y + c], (px, py, pc), refs[1].at[4 * px + 2 * py + pc]))
        return copies

    return dict(tag=tag, n=1, stage="swap", grads=[v], part=([v, land], plan, len(flips)))


def _small_sum(name, own, land):
    x, y, c = _position()
    me = jnp.stack([4 * x + 2 * y + c]).astype(jnp.int32)

    def body(me_ref, own_ref, land_ref, out_ref):
        term = lambda dev: jnp.where(me_ref[0] == dev, own_ref[...], land_ref[dev])
        acc = term(0)
        for dev in range(1, N_DEV):
            acc = acc + term(dev)
        out_ref[...] = acc

    return pl.pallas_call(
        body, name=name, out_shape=jax.ShapeDtypeStruct(own.shape, F32),
        grid_spec=pltpu.PrefetchScalarGridSpec(
            num_scalar_prefetch=1, grid=(1,),
            in_specs=[pl.BlockSpec(own.shape, lambda i, m: (0, 0)), pl.BlockSpec(land.shape, lambda i, m: (0, 0, 0))],
            out_specs=pl.BlockSpec(own.shape, lambda i, m: (0, 0))),
        compiler_params=_cparams("arbitrary"),
    )(me, own, land)


def _adamw_update(w, gg, m, v):
    mn = ADAM_B1 * m + (1.0 - ADAM_B1) * gg
    vn = ADAM_B2 * v + (1.0 - ADAM_B2) * (gg * gg)
    m_hat = mn / (1.0 - ADAM_B1 ** ADAM_STEP)
    v_hat = vn / (1.0 - ADAM_B2 ** ADAM_STEP)
    return -ADAM_LR * (m_hat / (jnp.sqrt(v_hat) + ADAM_EPS) + ADAM_WD * w), mn, vn


def _adamw(name, w, g, m, v):
    def body(w_ref, g_ref, m_ref, v_ref, d_ref, mo_ref, vo_ref):
        d_ref[...], mo_ref[...], vo_ref[...] = _adamw_update(w_ref[...], g_ref[...], m_ref[...], v_ref[...])

    blk = pl.BlockSpec(w.shape, lambda i: (0, 0))
    return pl.pallas_call(
        body, name=name, out_shape=(jax.ShapeDtypeStruct(w.shape, F32),) * 3, grid=(1,),
        in_specs=[blk] * 4, out_specs=(blk,) * 3, compiler_params=_cparams("arbitrary"),
    )(w, g, m, v)


def _rows_to_bf16(name, w):
    r, _, c = w.shape

    def body(w_ref, o_ref):
        o_ref[...] = w_ref[:, 0, :].astype(BF16)

    return pl.pallas_call(
        body, name=name, out_shape=jax.ShapeDtypeStruct((r, c), BF16), grid=(1,),
        in_specs=[pl.BlockSpec((r, 1, c), lambda i: (0, 0, 0))], out_specs=pl.BlockSpec((r, c), lambda i: (0, 0)),
        compiler_params=_cparams("arbitrary"),
    )(w)


def _adamw_rows(name, pos_c, w, g_mine, g_other, m, v):
    r, _, c = w.shape
    ch = c // 2

    def body(p_ref, w_ref, gm_ref, go_ref, m_ref, v_ref, g_ref, d_ref, mo_ref, vo_ref):
        gg = jnp.where(pl.program_id(0) == p_ref[0], gm_ref[...], go_ref[...])
        dl, mn, vn = _adamw_update(w_ref[:, 0, :], gg, m_ref[:, 0, :], v_ref[:, 0, :])
        g_ref[:, 0, :] = gg
        d_ref[:, 0, :] = dl
        mo_ref[:, 0, :] = mn
        vo_ref[:, 0, :] = vn

    rows = pl.BlockSpec((r, 1, ch), lambda h, p: (0, 0, h))
    half = pl.BlockSpec((r, ch), lambda h, p: (0, 0))
    return pl.pallas_call(
        body, name=name, out_shape=(jax.ShapeDtypeStruct(w.shape, F32),) * 4,
        grid_spec=pltpu.PrefetchScalarGridSpec(
            num_scalar_prefetch=1, grid=(2,), in_specs=[rows, half, half, rows, rows], out_specs=(rows,) * 4),
        compiler_params=_cparams("parallel"),
    )(pos_c, w, g_mine, g_other, m, v)


def _adamw_halves(name, pos_c, w, g_mine, g_other, m, v, tr, dep=None):
    r, c = w.shape
    rh = r // 2
    tr = tr if rh % tr == 0 else rh
    nt = rh // tr

    def body(p_ref, w_ref, gm_ref, go_ref, m_ref, v_ref, g_ref, d_ref, mo_ref, vo_ref):
        gg = jnp.where(pl.program_id(0) == p_ref[0], gm_ref[...], go_ref[...])
        g_ref[...] = gg
        d_ref[...], mo_ref[...], vo_ref[...] = _adamw_update(w_ref[...], gg, m_ref[...], v_ref[...])

    full = pl.BlockSpec((tr, c), lambda h, i, p: (h * nt + i, 0))
    half = pl.BlockSpec((tr, c), lambda h, i, p: (i, 0))
    dep_specs, dep_ops = _dep_args(dep)
    return pl.pallas_call(
        _after(body, 6, dep), name=name, out_shape=(jax.ShapeDtypeStruct((r, c), F32),) * 4,
        grid_spec=pltpu.PrefetchScalarGridSpec(
            num_scalar_prefetch=1, grid=(2, nt),
            in_specs=[full, half, half, full, full] + dep_specs, out_specs=(full,) * 4),
        compiler_params=_cparams("parallel", "parallel"),
    )(pos_c, w, g_mine, g_other, m, v, *dep_ops)


def _col_sharded_to_comm(g):
    k, n = g.shape
    return g.reshape(2, k // 2, N_CHIPS, n // N_CHIPS).transpose(2, 0, 1, 3)


def _row_sharded_to_comm(g):
    r, c = g.shape
    return g.reshape(N_CHIPS, 2, r // (2 * N_CHIPS), c)


def _col_sharded_full(g):
    _, _, rh, c = g.shape
    return g.reshape(N_CHIPS, 2 * rh, c).transpose(1, 0, 2).reshape(2 * rh, N_CHIPS * c)


def _row_sharded_full(g):
    _, _, rh, c = g.shape
    return g.reshape(N_CHIPS * 2 * rh, c)


def _chip_rows(w3, start, stop, own=None, me=None):
    r = w3.shape[1]
    parts = []
    for chip in range(N_CHIPS):
        lo, hi = max(start - chip * r, 0), min(stop - chip * r, r)
        if lo < hi:
            part = w3[chip, lo:hi]
            parts.append(part if own is None else jnp.where(me == chip, own[lo:hi], part))
    return parts[0] if len(parts) == 1 else jnp.concatenate(parts, axis=0)


def _pack_small(g1, bfv, mix, scale, g2n, gf, extra=None):
    row8 = jnp.pad(bfv.reshape(1, N_HEADS), ((0, 0), (0, LANES - N_HEADS)))
    if extra is not None:
        row8 = row8 + jnp.pad(extra[:, :1], ((0, 0), (N_HEADS, LANES - N_HEADS - 1)))
    return jnp.concatenate([
        g1.reshape(8, LANES), jnp.pad(row8, ((0, 7), (0, 0))), mix.reshape(512, LANES),
        jnp.pad(scale.reshape(4, LANES), ((0, 4), (0, 0))), g2n.reshape(8, LANES), gf.reshape(8, LANES)], axis=0)


def _unpack_small(s, like):
    g1, bfv, mix, scale, g2n, gf = like
    return (s[0:8].reshape(g1.shape), s[8, :N_HEADS].reshape(bfv.shape), s[16:528].reshape(mix.shape),
            s[528:532].reshape(scale.shape), s[536:544].reshape(g2n.shape), s[544:552].reshape(gf.shape))


class _MeshLinks:
    def __init__(self, shards_in, shards_rest):
        self.gin = _gather_begin("in", shards_in, None, column_halves=True)
        self.grest = _gather_begin("rest", shards_rest, self.gin["token"])
        self.tokens = {"gather": self.grest["token"]}
        self.groups, self.flight, self.slot = {}, None, 0

    @property
    def token(self):
        return list(self.tokens.values())

    def tie(self, x):
        return _tie(x, self.token)

    def weights_in(self, after):
        st = _gather_forward(self.gin, after)
        (g,), (own,) = _gather_end(st, st["token"], merge=False)
        return g, own, 2 * lax.axis_index("x") + lax.axis_index("y")

    def rest_forward(self, after):
        self.grest = _gather_forward(self.grest, after)
        self.tokens["gather"] = self.grest["token"]

    def weights_rest(self, after):
        g = _gather_end(self.grest, after)
        del self.tokens["gather"]
        return [_col_sharded_full(g[0]), _col_sharded_full(g[1])] + [_row_sharded_full(a) for a in g[2:]]

    def advance(self, after, begin=()):
        slot = self.slot
        self.slot += 1
        if self.flight is not None:
            tags, sems, parts = self.flight
            arrays, plan, _ = _join_parts(parts)
            thru = _copies_wait(f"slot{slot}_wait", sems, arrays, plan, after)
            for tag, part in zip(tags, parts):
                self.groups[tag] = _reduce_next(self.groups[tag], thru[:len(part[0])])
                thru = thru[len(part[0]):]
        for st in begin:
            self.groups[st["tag"]] = st
        live = [(tag, st["part"]) for tag, st in self.groups.items() if "part" in st]
        self.flight = None
        self.tokens.pop("reduce", None)
        if live:
            arrays, plan, m = _join_parts([part for _, part in live])
            sems, thru, token = _copies_start(f"slot{slot}_start", arrays, plan, m)
            parts = []
            for _, (part_arrays, part_plan, part_m) in live:
                parts.append((thru[:len(part_arrays)], part_plan, part_m))
                thru = thru[len(part_arrays):]
            self.flight = ([tag for tag, _ in live], sems, parts)
            self.tokens["reduce"] = token

    def reduced(self, tag):
        return self.groups[tag]["done"]


class _NoLinks:
    token = None

    def __init__(self, w_in, rest):
        self.w_in, self.rest, self.grads = w_in, rest, {}

    def tie(self, x):
        return x

    def weights_in(self, after):
        return self.w_in, None, None

    def rest_forward(self, after):
        pass

    def weights_rest(self, after):
        return self.rest

    def advance(self, after, begin=()):
        for st in begin:
            self.grads[st["tag"]] = st["grads"]


def _local_step(links, x, target, seq, norm1_g, b_forget, pool_mix, pool_scale, norm2_g, norm_f_g, between=None):
    t, d = x.shape
    tq = min(256, seq)
    aw = ATTN_WIDTH
    o_q, o_f, o_g = POOL_WIDTH, POOL_WIDTH + 3 * aw, POOL_WIDTH + 3 * aw + N_HEADS
    bf = jnp.pad(b_forget, ((0, 0), (0, LANES - N_HEADS)))
    mixb = pool_mix.astype(BF16)

    h = _norm_fwd("norm1_fwd", x, links.tie(norm1_g), 512)
    w_in, own, me = links.weights_in(h)
    wu = _chip_rows(w_in, 0, o_q, own, me)
    wqkv = _chip_rows(w_in, o_q, o_f, own, me)
    wft = jnp.pad(_chip_rows(w_in, o_f, o_g, own, me), ((0, LANES - N_HEADS), (0, 0)))
    wg2 = _chip_rows(w_in, o_g, N_CHIPS * w_in.shape[1], own, me)
    wf = wft.T
    u = _matmul("mm_u", h, wu, "nt", F32, 1024, 512, d)
    g2 = _matmul("mm_gates", h, wg2, "nt", BF16, 1024, 1024, d)
    fl, fcum = _forget_fwd(h, wf, bf, seq)
    qa, ka, v = _attn_prep(h, _head_blocks(wqkv[:aw]), _head_blocks(wqkv[aw:2 * aw]), wqkv[2 * aw:], fcum, 1024)
    p, ps = _pool_fwd(u, mixb, pool_scale, seq)
    links.rest_forward([ps, qa, g2])
    o, lse = _attn_fwd(qa, ka, v, seq, tq, dep=links.token)
    w_pool_out, w_attn_out, w_out, w_ffn_gate, w_ffn_up, w_ffn_down = links.weights_rest(o)
    merged, x1 = _merge_fwd(x, ps, o, g2, w_pool_out, w_attn_out, w_out, 512)
    h2, gt, up, act, x2 = _ffn_fwd(x1, norm2_g, w_ffn_gate, w_ffn_up, w_ffn_down, 1024, 256)
    loss, dx2, d_gf = _final_fwd_bwd(x2, target, norm_f_g, 512)

    dgt, dup, dx1, d_g2n = _ffn_bwd(dx2, x1, norm2_g, gt, up, w_ffn_gate, w_ffn_up, w_ffn_down, 1024, 256)
    d_wd = _matmul("dw_down", act, dx2, "tn", F32, 1408, 1024, 1024)
    d_wg = _matmul("dw_gate", dgt, h2, "tn", F32, 1408, 1024, 1024)
    d_wu = _matmul("dw_up", dup, h2, "tn", F32, 1408, 1024, 1024)
    links.advance(None, begin=[_reduce_begin("a", [_row_sharded_to_comm(g) for g in (d_wg, d_wu, d_wd)])])
    dpy, day, dg2, dps, da = _merge_bwd(dx1, ps, o, g2, w_pool_out, w_attn_out, w_out, 512, dep=links.token)
    links.advance(dps)
    d_wout = _matmul("dw_out", merged, dx1, "tn", F32, 1024, 1024, 1024)
    d_wpo = _matmul("dw_pool_out", ps, dpy, "tn", F32, 512, 1024, 1024)
    d_wao = _matmul("dw_attn_out", o, day, "tn", F32, 512, 1024, 1024)
    dq, dk, dv, dfr = _attn_bwd(qa, ka, v, da, lse, seq, tq, dep=links.token)
    links.advance(dq, begin=[_reduce_begin(
        "m", [_col_sharded_to_comm(d_wpo), _col_sharded_to_comm(d_wao), _row_sharded_to_comm(d_wout)])])
    dfc = jnp.pad(dfr.reshape(N_HEADS, t).T, ((0, 0), (0, LANES - N_HEADS)))
    dfl, d_bf = _forget_bwd(dfc, fl, bf, seq)
    du, d_mix, d_scale = _pool_bwd(dps, p, mixb, links.tie(pool_scale), seq)
    d_wu_in = _matmul("dw_in_u", du, h, "tn", F32, 512, 1024, 1024)
    d_wq = _matmul("dw_in_q", dq, h, "tn", F32, 512, 1024, 1024, dep=links.token)
    d_wk = _matmul("dw_in_k", dk, h, "tn", F32, 512, 1024, 1024, dep=links.token)
    d_wv = _matmul("dw_in_v", dv, h, "tn", F32, 512, 1024, 1024, dep=links.token)
    links.advance([d_wu_in, d_wq, d_wk, d_wv])
    d_wf = _matmul("dw_in_f", dfl, h, "tn", F32, LANES, 1024, 512)
    d_wg2 = _matmul("dw_in_gates", dg2, h, "tn", F32, 1024, 1024, 1024, dep=links.token)
    d_win = jnp.concatenate([d_wu_in, d_wq, d_wk, d_wv, d_wf[:N_HEADS], d_wg2], axis=0)
    comm_b = [d_win.reshape(N_CHIPS, d_win.shape[0] // N_CHIPS, d)]
    small = (jnp.zeros_like(norm1_g), d_bf[:, :N_HEADS], d_mix, d_scale, d_g2n, d_gf)
    links.advance(comm_b, begin=[_reduce_begin("b", comm_b, column_halves=True),
                                 _small_begin("small", _pack_small(*small, extra=loss))])
    if between is not None:
        between()
    dx, d_g1 = _in_bwd(du, dq, dk, dv, dg2, dfl, dx1, x, links.tie(norm1_g), wu, wqkv, wg2, wft, 512)
    return loss, dx, (d_g1,) + small[1:]


def kernel(x, norm1_g, w_in, b_forget, pool_mix, pool_scale, w_pool_out, w_attn_out, w_out, norm2_g, w_ffn_gate, w_ffn_up, w_ffn_down, norm_f_g, loss_target, m_norm1_g, m_w_in, m_b_forget, m_pool_mix, m_pool_scale, m_w_pool_out, m_w_attn_out, m_w_out, m_norm2_g, m_w_ffn_gate, m_w_ffn_up, m_w_ffn_down, m_norm_f_g, v_norm1_g, v_w_in, v_b_forget, v_pool_mix, v_pool_scale, v_w_pool_out, v_w_attn_out, v_w_out, v_norm2_g, v_w_ffn_gate, v_w_ffn_up, v_w_ffn_down, v_norm_f_g):
    nb, seq, d = x.shape
    group_a = ((w_ffn_gate, m_w_ffn_gate, v_w_ffn_gate, True, 9), (w_ffn_up, m_w_ffn_up, v_w_ffn_up, True, 10),
               (w_ffn_down, m_w_ffn_down, v_w_ffn_down, False, 11))
    group_m = ((w_pool_out, m_w_pool_out, v_w_pool_out, False, 5), (w_attn_out, m_w_attn_out, v_w_attn_out, False, 6),
               (w_out, m_w_out, v_w_out, False, 7))
    group_b = ((w_in, m_w_in, v_w_in, False, 1),)
    small_w = (norm1_g, b_forget, pool_mix, pool_scale, norm2_g, norm_f_g)
    small_m = (m_norm1_g, m_b_forget, m_pool_mix, m_pool_scale, m_norm2_g, m_norm_f_g)
    small_v = (v_norm1_g, v_b_forget, v_pool_mix, v_pool_scale, v_norm2_g, v_norm_f_g)
    small_pos = (0, 2, 3, 4, 8, 12)
    view = lambda a, tr: a[0].T if tr else a[0]
    unview = lambda a, tr, like: (a.T if tr else a).reshape(like.shape)

    def shard(w, tr):
        lw = view(w, tr).astype(BF16)
        return lw.reshape(2, lw.shape[0] // 2, lw.shape[1])

    cm = lambda a: jnp.transpose(a, (2, 0, 1))
    shard_in = _rows_to_bf16("w_in_to_bf16", cm(w_in))
    links = _MeshLinks([shard_in],
                       [shard(w_pool_out, False), shard(w_attn_out, False), shard(w_out, False),
                        shard(w_ffn_gate, True), shard(w_ffn_up, True), shard(w_ffn_down, False)])
    grads, deltas, new_m, new_v = [None] * 13, [None] * 13, [None] * 13, [None] * 13
    pos_c = jnp.stack([lax.axis_index("c")]).astype(jnp.int32)

    def update(tag, group, dep):
        last = []
        for k, ((w, m, v, tr, pos), (mine, other)) in enumerate(zip(group, links.reduced(tag))):
            outs = _adamw_halves(f"adamw_{tag}{k}", pos_c, view(w, tr), mine, other, view(m, tr), view(v, tr), 256,
                                 dep=dep)
            grads[pos], deltas[pos], new_m[pos], new_v[pos] = (unview(a, tr, w) for a in outs)
            last.append(outs[1])
        return last

    def update_a():
        links.advance(update("a", group_a, links.token))

    loss, dx, small_g = _local_step(
        links, x.reshape(nb * seq, d), loss_target.reshape(nb * seq, d), seq,
        norm1_g, b_forget, pool_mix[0], pool_scale, norm2_g, norm_f_g.reshape(1, d), between=update_a)

    links.advance(dx, begin=[_small_begin("g1", small_g[0].reshape(8, LANES))])
    last = update("m", group_m, links.token)
    small_rest = _small_sum("small_sum", *links.reduced("small")[0])
    links.advance(last + [small_rest])
    small_sum = jnp.concatenate([_small_sum("g1_sum", *links.reduced("g1")[0]), small_rest[8:]], axis=0)
    loss_out = small_sum[8, N_HEADS]
    dl, mn, vn = _adamw("adamw_small", _pack_small(*small_w), small_sum * _small_mask(), _pack_small(*small_m),
                        _pack_small(*small_v))
    for pos, g, a, b, e in zip(small_pos, _unpack_small(small_sum, small_w), _unpack_small(dl, small_w),
                               _unpack_small(mn, small_w), _unpack_small(vn, small_w)):
        grads[pos], deltas[pos], new_m[pos], new_v[pos] = g, a, b, e
    links.advance([dl])
    (mine, other), = links.reduced("b")
    outs = _adamw_rows("adamw_b0", pos_c, cm(w_in), mine, other, cm(m_w_in), cm(v_w_in))
    grads[1], deltas[1], new_m[1], new_v[1] = (jnp.transpose(a, (1, 2, 0)) for a in outs)

    return (loss_out, dx.reshape(nb, seq, d), *grads, *deltas, *new_m, *new_v)


def _small_mask():
    rows = lax.broadcasted_iota(jnp.int32, (552, LANES), 0)
    lanes = lax.broadcasted_iota(jnp.int32, (552, LANES), 1)
    return jnp.where(jnp.logical_and(rows == 8, lanes == N_HEADS), 0.0, 1.0).astype(F32)
```

```python
import jax
import jax.numpy as jnp
from jax import lax
from jax.experimental import pallas as pl
from jax.experimental.pallas import tpu as pltpu

F32 = jnp.float32
BF16 = jnp.bfloat16

D_MODEL = 1024
POOL_WINDOWS = (2, 4, 8, 16)
POOL_GROUPS = 4
POOL_GROUP_DIM = 128
POOL_WIDTH = 512
HEAD_DIM = 64
N_HEADS = 8
ATTN_WIDTH = 512
D_FF = 2816
RMS_EPS = 1e-6
ATTN_SCALE = HEAD_DIM ** -0.5
NEG_BIG = -1e30

ADAM_LR = 0.001
ADAM_B1 = 0.9
ADAM_B2 = 0.999
ADAM_EPS = 1e-08
ADAM_WD = 0.01
ADAM_STEP = 10

LANES = 128
N_CHIPS = 4
N_DEV = 8
VMEM_LIMIT_V7X = 52 * 1024 * 1024
ROW_CHUNK = 256
MESH = pl.DeviceIdType.MESH
ANY = pl.BlockSpec(memory_space=pl.ANY)


def _cparams(*sem):
    return pltpu.CompilerParams(dimension_semantics=sem if sem else None, vmem_limit_bytes=VMEM_LIMIT_V7X)


def _dep_list(dep):
    return [] if dep is None else (list(dep) if isinstance(dep, (list, tuple)) else [dep])


def _after(body, n_in, dep):
    k = len(_dep_list(dep))
    if k == 0:
        return body

    def wrapped(*refs):
        body(*refs[:n_in], *refs[n_in + k:])

    return wrapped


def _dep_args(dep):
    deps = _dep_list(dep)
    return [ANY] * len(deps), deps


def _dot(a, b):
    return lax.dot_general(a, b, (((1,), (0,)), ((), ())), preferred_element_type=F32)


def _dot_nt(a, b):
    return lax.dot_general(a, b, (((1,), (1,)), ((), ())), preferred_element_type=F32)


def _dot_tn(a, b):
    return lax.dot_general(a, b, (((0,), (0,)), ((), ())), preferred_element_type=F32)


def _sigmoid(x):
    return jax.nn.sigmoid(x)


def _rms_fwd(x, g):
    r = lax.rsqrt(jnp.mean(x * x, axis=-1, keepdims=True) + RMS_EPS)
    return (x * r) * g


def _rms_bwd(x, g, dy):
    r = lax.rsqrt(jnp.mean(x * x, axis=-1, keepdims=True) + RMS_EPS)
    xh = x * r
    dg = jnp.sum(dy * xh, axis=0, keepdims=True)
    dxh = dy * g
    dx = r * (dxh - xh * jnp.mean(dxh * xh, axis=-1, keepdims=True))
    return dx, dg


def _matmul(name, a, b, mode, out_dtype, tm, tn, tk, dep=None):
    if mode == "nn":
        (m, k), (_, n) = a.shape, b.shape
    elif mode == "nt":
        (m, k), (n, _) = a.shape, b.shape
    else:
        (k, m), (_, n) = a.shape, b.shape
    tm, tn, tk = min(tm, m), min(tn, n), min(tk, k)
    assert m % tm == 0 and n % tn == 0 and k % tk == 0, (name, m, n, k, tm, tn, tk)
    nk = k // tk
    if mode == "tn":
        a_spec = pl.BlockSpec((tk, tm), lambda i, j, kk: (kk, i))
    else:
        a_spec = pl.BlockSpec((tm, tk), lambda i, j, kk: (i, kk))
    if mode == "nt":
        b_spec = pl.BlockSpec((tn, tk), lambda i, j, kk: (j, kk))
    else:
        b_spec = pl.BlockSpec((tk, tn), lambda i, j, kk: (kk, j))
    dot = {"nn": _dot, "nt": _dot_nt, "tn": _dot_tn}[mode]
    use_scratch = nk > 1 and out_dtype != F32

    def body(a_ref, b_ref, o_ref, *scratch):
        if nk == 1 and mode != "tn":
            rows = min(ROW_CHUNK, tm)
            bb = b_ref[...].astype(BF16)
            for r0 in range(0, tm, rows):
                o_ref[r0:r0 + rows, :] = dot(a_ref[r0:r0 + rows, :].astype(BF16), bb).astype(out_dtype)
            return
        prod = dot(a_ref[...].astype(BF16), b_ref[...].astype(BF16))
        if nk == 1:
            o_ref[...] = prod.astype(out_dtype)
            return
        acc = scratch[0] if use_scratch else o_ref
        kk = pl.program_id(2)

        @pl.when(kk == 0)
        def _():
            acc[...] = prod

        @pl.when(kk > 0)
        def _():
            acc[...] += prod

        if use_scratch:
            @pl.when(kk == nk - 1)
            def _():
                o_ref[...] = acc[...].astype(out_dtype)

    dep_specs, dep_ops = _dep_args(dep)
    return pl.pallas_call(
        _after(body, 2, dep),
        name=name,
        out_shape=jax.ShapeDtypeStruct((m, n), out_dtype),
        grid=(m // tm, n // tn, nk),
        in_specs=[a_spec, b_spec] + dep_specs,
        out_specs=pl.BlockSpec((tm, tn), lambda i, j, kk: (i, j)),
        scratch_shapes=[pltpu.VMEM((tm, tn), F32)] if use_scratch else [],
        compiler_params=_cparams("parallel", "parallel", "arbitrary"),
    )(a, b, *dep_ops)


def _norm_fwd(name, x, g, tm):
    t, d = x.shape
    tm = min(tm, t)

    def body(x_ref, g_ref, h_ref):
        h_ref[...] = _rms_fwd(x_ref[...], g_ref[...]).astype(BF16)

    return pl.pallas_call(
        body, name=name, out_shape=jax.ShapeDtypeStruct((t, d), BF16), grid=(t // tm,),
        in_specs=[pl.BlockSpec((tm, d), lambda i: (i, 0)), pl.BlockSpec((1, d), lambda i: (0, 0))],
        out_specs=pl.BlockSpec((tm, d), lambda i: (i, 0)),
        compiler_params=_cparams("parallel"),
    )(x, g)


def _split3(x):
    hi = x.astype(BF16)
    r1 = x - hi.astype(F32)
    mid = r1.astype(BF16)
    lo = (r1 - mid.astype(F32)).astype(BF16)
    return hi, mid, lo


def _tri_dot(tri, x):
    hi, mid, lo = _split3(x)
    return _dot(tri, hi) + _dot(tri, mid) + _dot(tri, lo)


def _forget_fwd(h, wf, bf, seq):
    t, d = h.shape
    cb = min(256, seq)

    def body(h_ref, wf_ref, bf_ref, fl_ref, fc_ref):
        fl = _dot(h_ref[...], wf_ref[...])
        fl_ref[...] = fl
        xx = fl + bf_ref[...]
        lf = jnp.minimum(xx, 0.0) - jnp.log(1.0 + jnp.exp(-jnp.abs(xx)))
        ri = lax.broadcasted_iota(jnp.int32, (cb, cb), 0)
        ci = lax.broadcasted_iota(jnp.int32, (cb, cb), 1)
        tri = (ri >= ci).astype(BF16)
        carry = jnp.zeros((1, LANES), F32)
        for blk in range(seq // cb):
            cs = _tri_dot(tri, lf[blk * cb:(blk + 1) * cb]) + carry
            fc_ref[blk * cb:(blk + 1) * cb, :] = cs
            carry = cs[cb - 1:cb, :]

    return pl.pallas_call(
        body, name="forget_fwd",
        out_shape=(jax.ShapeDtypeStruct((t, LANES), F32), jax.ShapeDtypeStruct((t, LANES), F32)),
        grid=(t // seq,),
        in_specs=[pl.BlockSpec((seq, d), lambda b: (b, 0)), pl.BlockSpec((d, LANES), lambda b: (0, 0)),
                  pl.BlockSpec((1, LANES), lambda b: (0, 0))],
        out_specs=(pl.BlockSpec((seq, LANES), lambda b: (b, 0)), pl.BlockSpec((seq, LANES), lambda b: (b, 0))),
        compiler_params=_cparams("parallel"),
    )(h, wf, bf)


def _pool_fwd(u, mix, scale, seq):
    t = u.shape[0]

    def body(u_ref, mix_ref, sc_ref, p_ref, ps_ref):
        tpos = lax.broadcasted_iota(jnp.int32, (seq, POOL_GROUP_DIM), 0)
        for g in range(POOL_GROUPS):
            sl = slice(g * POOL_GROUP_DIM, (g + 1) * POOL_GROUP_DIM)
            ug = u_ref[:, sl]
            s = ug
            for lvl in range(g + 1):
                d = 2 ** lvl
                s = s + jnp.where(tpos >= d, pltpu.roll(s, d, 0), 0.0)
            cnt = jnp.minimum(tpos + 1, POOL_WINDOWS[g]).astype(F32)
            pb = (s / cnt - ug).astype(BF16)
            p_ref[:, sl] = pb
            ps_ref[:, sl] = (_dot(pb, mix_ref[g]) * sc_ref[:, sl]).astype(BF16)

    return pl.pallas_call(
        body, name="pool_fwd",
        out_shape=(jax.ShapeDtypeStruct((t, POOL_WIDTH), BF16), jax.ShapeDtypeStruct((t, POOL_WIDTH), BF16)),
        grid=(t // seq,),
        in_specs=[pl.BlockSpec((seq, POOL_WIDTH), lambda b: (b, 0)),
                  pl.BlockSpec((POOL_GROUPS, POOL_GROUP_DIM, POOL_GROUP_DIM), lambda b: (0, 0, 0)),
                  pl.BlockSpec((1, POOL_WIDTH), lambda b: (0, 0))],
        out_specs=(pl.BlockSpec((seq, POOL_WIDTH), lambda b: (b, 0)), pl.BlockSpec((seq, POOL_WIDTH), lambda b: (b, 0))),
        compiler_params=_cparams("parallel"),
    )(u, mix, scale)


def _aug_constants():
    w = N_HEADS * LANES
    rows = jnp.arange(3 * LANES)
    piece, head = rows // LANES, rows % LANES
    cols = jnp.arange(w)
    live = (head < N_HEADS)[:, None]
    pq = (live & (cols[None, :] == (head * LANES + HEAD_DIM + piece)[:, None])).astype(BF16)
    pk = -(live & (cols[None, :] == (head * LANES + HEAD_DIM + 3 + piece)[:, None])).astype(BF16)
    lane = cols % LANES
    oq = ((lane >= HEAD_DIM + 3) & (lane < HEAD_DIM + 6)).astype(F32)[None, :]
    ok = ((lane >= HEAD_DIM) & (lane < HEAD_DIM + 3)).astype(F32)[None, :]
    return pq, pk, oq, ok


def _head_blocks(wt):
    d = wt.shape[1]
    return jnp.pad(wt.reshape(N_HEADS, HEAD_DIM, d), ((0, 0), (0, LANES - HEAD_DIM), (0, 0))).reshape(N_HEADS * LANES, d)


def _attn_prep(h, wq, wk, wv, fcum, tm):
    t, d = h.shape
    tm = min(tm, t)
    rows = min(ROW_CHUNK, tm)
    w = N_HEADS * LANES
    pq, pk, oq, ok = _aug_constants()

    def body(h_ref, wq_ref, wk_ref, wv_ref, f_ref, pq_ref, pk_ref, oq_ref, ok_ref, qa_ref, ka_ref, v_ref):
        for r0 in range(0, tm, rows):
            rs = slice(r0, r0 + rows)
            hh = h_ref[rs, :]
            fs = jnp.concatenate(_split3(f_ref[rs, :]), axis=1)
            q = _dot_nt(hh, wq_ref[...]).astype(BF16).astype(F32) * ATTN_SCALE
            qa_ref[rs, :] = (q + _dot(fs, pq_ref[...]) + oq_ref[...]).astype(BF16)
            k = _dot_nt(hh, wk_ref[...]).astype(BF16).astype(F32)
            ka_ref[rs, :] = (k + _dot(fs, pk_ref[...]) + ok_ref[...]).astype(BF16)
            v_ref[rs, :] = _dot_nt(hh, wv_ref[...]).astype(BF16)

    row = lambda n: pl.BlockSpec((tm, n), lambda i: (i, 0))
    full = lambda a: pl.BlockSpec(a.shape, lambda i: (0, 0))
    return pl.pallas_call(
        body, name="attn_prep",
        out_shape=(jax.ShapeDtypeStruct((t, w), BF16), jax.ShapeDtypeStruct((t, w), BF16),
                   jax.ShapeDtypeStruct((t, ATTN_WIDTH), BF16)),
        grid=(t // tm,),
        in_specs=[row(d), full(wq), full(wk), full(wv), row(LANES), full(pq), full(pk), full(oq), full(ok)],
        out_specs=(row(w), row(w), row(ATTN_WIDTH)),
        compiler_params=_cparams("parallel"),
    )(h, wq, wk, wv, fcum, pq, pk, oq, ok)


def _fold_lanes(x, op):
    out = x[:, :LANES]
    for g in range(1, x.shape[1] // LANES):
        out = op(out, x[:, g * LANES:(g + 1) * LANES])
    return out


def _causal_sweep(i, tile, carry):
    def quad(jj, c):
        for u in range(4):
            c = tile(4 * jj + u, c, False)
        return c

    carry = lax.fori_loop(0, i // 4, quad, carry)
    base = 4 * (i // 4)
    carry = lax.cond(i % 4 >= 2, lambda c: tile(base + 1, tile(base, c, False), False), lambda c: c, carry)
    return lax.cond(i % 2 == 1, lambda c: tile(i, tile(i - 1, c, False), True), lambda c: tile(i, c, True), carry)


def _attn_fwd(qa, ka, v, seq, tq, dep=None):
    t = qa.shape[0]
    nq = seq // tq
    hp_n = N_HEADS // 2
    heads = [slice(e * LANES, (e + 1) * LANES) for e in range(2)]

    def body(q_ref, k_ref, v_ref, o_ref, lse_ref, s_buf):
        i = pl.program_id(2)
        diag_ok = lax.broadcasted_iota(jnp.int32, (tq, tq), 0) >= lax.broadcasted_iota(jnp.int32, (tq, tq), 1)
        qs = [q_ref[:, hl] for hl in heads]

        def sweep1(j, mxs, diagonal):
            r0 = pl.multiple_of(j * tq, tq)
            out = []
            for e, hl in enumerate(heads):
                s = _dot_nt(qs[e], k_ref[pl.ds(r0, tq), hl])
                if diagonal:
                    s = jnp.where(diag_ok, s, NEG_BIG)
                s_buf[e, j] = s
                out.append(jnp.maximum(mxs[e], _fold_lanes(s, jnp.maximum)))
            return tuple(out)

        mxs = _causal_sweep(i, sweep1, (jnp.full((tq, LANES), NEG_BIG, F32),) * 2)
        ms = [jnp.max(mx, axis=1, keepdims=True) for mx in mxs]

        def sweep2(j, carry, diagonal):
            r0 = pl.multiple_of(j * tq, tq)
            vv = v_ref[pl.ds(r0, tq), :]
            out = []
            for e in range(2):
                p = jnp.exp(s_buf[e, j] - ms[e])
                out += [carry[2 * e] + _fold_lanes(p, jnp.add), carry[2 * e + 1] + _dot(p.astype(BF16), vv)]
            return tuple(out)

        res = _causal_sweep(i, sweep2, (jnp.zeros((tq, LANES), F32),) * 4)
        outs = []
        for e in range(2):
            l = jnp.sum(res[2 * e], axis=1, keepdims=True)
            outs.append(res[2 * e + 1] / l)
            lse_ref[:, e:e + 1] = ms[e] + jnp.log(l)
        lane = lax.broadcasted_iota(jnp.int32, (tq, LANES), 1)
        o_ref[...] = jnp.where(lane < HEAD_DIM, outs[0], outs[1])

    dep_specs, dep_ops = _dep_args(dep)
    return pl.pallas_call(
        _after(body, 3, dep), name="attn_fwd",
        out_shape=(jax.ShapeDtypeStruct((t, ATTN_WIDTH), F32), jax.ShapeDtypeStruct((hp_n, t, 2), F32)),
        grid=(t // seq, hp_n, nq),
        in_specs=[pl.BlockSpec((tq, 2 * LANES), lambda b, hp, i: (b * nq + i, hp)),
                  pl.BlockSpec((seq, 2 * LANES), lambda b, hp, i: (b, hp)),
                  pl.BlockSpec((seq, LANES), lambda b, hp, i: (b, hp))] + dep_specs,
        out_specs=(pl.BlockSpec((tq, LANES), lambda b, hp, i: (b * nq + i, hp)),
                   pl.BlockSpec((None, tq, 2), lambda b, hp, i: (hp, b * nq + i, 0))),
        scratch_shapes=[pltpu.VMEM((2, nq, tq, tq), F32)],
        compiler_params=_cparams("parallel", "parallel", "arbitrary"),
    )(qa, ka, v, *dep_ops)


def _merge_fwd(x, ps, o, g2, wpo, wao, wout, tm):
    t, d = x.shape
    tm = min(tm, t)
    rows = min(ROW_CHUNK, tm)

    def body(x_ref, ps_ref, o_ref, gp_ref, ga_ref, wpo_ref, wao_ref, wout_ref, mg_ref, x1_ref):
        for r0 in range(0, tm, rows):
            rs = slice(r0, r0 + rows)
            py = _dot(ps_ref[rs, :], wpo_ref[...])
            ay = _dot(o_ref[rs, :].astype(BF16), wao_ref[...])
            mb = (_sigmoid(gp_ref[rs, :].astype(F32)) * py + _sigmoid(ga_ref[rs, :].astype(F32)) * ay).astype(BF16)
            mg_ref[rs, :] = mb
            x1_ref[rs, :] = x_ref[rs, :] + _dot(mb, wout_ref[...])

    row = lambda w: pl.BlockSpec((tm, w), lambda i: (i, 0))
    full = lambda a: pl.BlockSpec(a.shape, lambda i: (0, 0))
    return pl.pallas_call(
        body, name="merge_fwd",
        out_shape=(jax.ShapeDtypeStruct((t, d), BF16), jax.ShapeDtypeStruct((t, d), F32)),
        grid=(t // tm,),
        in_specs=[row(d), row(POOL_WIDTH), row(ATTN_WIDTH), pl.BlockSpec((tm, d), lambda i: (i, 0)),
                  pl.BlockSpec((tm, d), lambda i: (i, 1)), full(wpo), full(wao), full(wout)],
        out_specs=(row(d), row(d)),
        compiler_params=_cparams("parallel"),
    )(x, ps, o, g2, g2, wpo, wao, wout)


def _ffn_fwd(x1, g, wg, wu, wd, tm, tf):
    t, d = x1.shape
    f = wg.shape[0]
    tm = min(tm, t)
    nf = f // tf
    rows = min(512, tm)

    def body(x1_ref, g_ref, wg_ref, wu_ref, wd_ref, h2_ref, gt_ref, up_ref, act_ref, x2_ref):
        j = pl.program_id(1)

        @pl.when(j == 0)
        def _():
            h2_ref[...] = _rms_fwd(x1_ref[...], g_ref[...]).astype(BF16)

            x2_ref[...] = x1_ref[...]

        for r0 in range(0, tm, rows):
            rs = slice(r0, r0 + rows)
            h2 = h2_ref[rs, :]
            gt = _dot_nt(h2, wg_ref[...])
            up = _dot_nt(h2, wu_ref[...])
            sg = _sigmoid(gt)
            silu = gt * sg
            act = (silu * up).astype(BF16)
            gt_ref[rs, :] = (up * (sg * (1.0 + gt * (1.0 - sg)))).astype(BF16)
            up_ref[rs, :] = silu.astype(BF16)
            act_ref[rs, :] = act
            x2_ref[rs, :] += _dot(act, wd_ref[...])

    return pl.pallas_call(
        body, name="ffn_fwd",
        out_shape=(jax.ShapeDtypeStruct((t, d), BF16), jax.ShapeDtypeStruct((t, f), BF16),
                   jax.ShapeDtypeStruct((t, f), BF16), jax.ShapeDtypeStruct((t, f), BF16),
                   jax.ShapeDtypeStruct((t, d), F32)),
        grid=(t // tm, nf),
        in_specs=[pl.BlockSpec((tm, d), lambda i, j: (i, 0)), pl.BlockSpec((1, d), lambda i, j: (0, 0)),
                  pl.BlockSpec((tf, d), lambda i, j: (j, 0)), pl.BlockSpec((tf, d), lambda i, j: (j, 0)),
                  pl.BlockSpec((tf, d), lambda i, j: (j, 0))],
        out_specs=(pl.BlockSpec((tm, d), lambda i, j: (i, 0)), pl.BlockSpec((tm, tf), lambda i, j: (i, j)),
                   pl.BlockSpec((tm, tf), lambda i, j: (i, j)), pl.BlockSpec((tm, tf), lambda i, j: (i, j)),
                   pl.BlockSpec((tm, d), lambda i, j: (i, 0))),
        compiler_params=_cparams("parallel", "arbitrary"),
    )(x1, g, wg, wu, wd)


def _final_fwd_bwd(x2, target, g, tm):
    t, d = x2.shape
    tm = min(tm, t)

    def body(x_ref, t_ref, g_ref, loss_ref, dx_ref, dg_ref):
        i = pl.program_id(0)
        x = x_ref[...]
        gg = g_ref[...]
        err = _rms_fwd(x, gg) - t_ref[...]
        part = 0.5 * jnp.sum(jnp.mean(err * err, axis=-1, keepdims=True), axis=0, keepdims=True)
        dx, dg = _rms_bwd(x, gg, err * (1.0 / d))
        dx_ref[...] = dx

        @pl.when(i == 0)
        def _():
            loss_ref[...] = jnp.zeros_like(loss_ref)
            dg_ref[...] = jnp.zeros_like(dg_ref)

        loss_ref[...] += jnp.broadcast_to(part, loss_ref.shape)
        dg_ref[...] += dg

    return pl.pallas_call(
        body, name="final_fwd_bwd",
        out_shape=(jax.ShapeDtypeStruct((1, LANES), F32), jax.ShapeDtypeStruct((t, d), F32),
                   jax.ShapeDtypeStruct((1, d), F32)),
        grid=(t // tm,),
        in_specs=[pl.BlockSpec((tm, d), lambda i: (i, 0)), pl.BlockSpec((tm, d), lambda i: (i, 0)),
                  pl.BlockSpec((1, d), lambda i: (0, 0))],
        out_specs=(pl.BlockSpec((1, LANES), lambda i: (0, 0)), pl.BlockSpec((tm, d), lambda i: (i, 0)),
                   pl.BlockSpec((1, d), lambda i: (0, 0))),
        compiler_params=_cparams("arbitrary"),
    )(x2, target, g)


def _ffn_bwd(dx2, x1, g, gt, up, wg, wu, wd, tm, tf):
    t, d = dx2.shape
    f = gt.shape[1]
    tm = min(tm, t)
    nf = f // tf
    wgu = jnp.concatenate([wg.reshape(nf, tf, d), wu.reshape(nf, tf, d)], axis=1).reshape(2 * f, d)
    rows = min(256, tm)

    def body(dx2_ref, x1_ref, g_ref, gt_ref, up_ref, wgu_ref, wd_ref, dgt_ref, dup_ref, dx1_ref, dg_ref, acc_ref,
             dxb_ref):
        i, j = pl.program_id(0), pl.program_id(1)

        @pl.when(j == 0)
        def _():
            dxb_ref[...] = dx2_ref[...].astype(BF16)
            acc_ref[...] = jnp.zeros_like(acc_ref)

        for r0 in range(0, tm, rows):
            rs = slice(r0, r0 + rows)
            dact = _dot_nt(dxb_ref[rs, :], wd_ref[...])
            dgt = (dact * gt_ref[rs, :].astype(F32)).astype(BF16)
            dup = (dact * up_ref[rs, :].astype(F32)).astype(BF16)
            dgt_ref[rs, :] = dgt
            dup_ref[rs, :] = dup
            acc_ref[rs, :] += _dot(jnp.concatenate([dgt, dup], axis=1), wgu_ref[...])

        @pl.when(jnp.logical_and(i == 0, j == 0))
        def _():
            dg_ref[...] = jnp.zeros_like(dg_ref)

        @pl.when(j == nf - 1)
        def _():
            dxn, dg = _rms_bwd(x1_ref[...], g_ref[...], acc_ref[...])
            dx1_ref[...] = dx2_ref[...] + dxn
            dg_ref[...] += dg

    return pl.pallas_call(
        body, name="ffn_bwd",
        out_shape=(jax.ShapeDtypeStruct((t, f), BF16), jax.ShapeDtypeStruct((t, f), BF16),
                   jax.ShapeDtypeStruct((t, d), F32), jax.ShapeDtypeStruct((1, d), F32)),
        grid=(t // tm, nf),
        in_specs=[pl.BlockSpec((tm, d), lambda i, j: (i, 0)), pl.BlockSpec((tm, d), lambda i, j: (i, 0)),
                  pl.BlockSpec((1, d), lambda i, j: (0, 0)),
                  pl.BlockSpec((tm, tf), lambda i, j: (i, j)), pl.BlockSpec((tm, tf), lambda i, j: (i, j)),
                  pl.BlockSpec((2 * tf, d), lambda i, j: (j, 0)), pl.BlockSpec((tf, d), lambda i, j: (j, 0))],
        out_specs=(pl.BlockSpec((tm, tf), lambda i, j: (i, j)), pl.BlockSpec((tm, tf), lambda i, j: (i, j)),
                   pl.BlockSpec((tm, d), lambda i, j: (i, 0)), pl.BlockSpec((1, d), lambda i, j: (0, 0))),
        scratch_shapes=[pltpu.VMEM((tm, d), F32), pltpu.VMEM((tm, d), BF16)],
        compiler_params=_cparams("arbitrary", "arbitrary"),
    )(dx2, x1, g, gt, up, wgu, wd)


def _merge_bwd(dx1, ps, o, g2, wpo, wao, wout, tm, dep=None):
    t, d = dx1.shape
    tm = min(tm, t)
    rows = min(ROW_CHUNK, tm)

    def body(dx1_ref, ps_ref, o_ref, gp_ref, ga_ref, wpo_ref, wao_ref, wout_ref, dpy_ref, day_ref, dg2_ref, dps_ref, da_ref):
        for r0 in range(0, tm, rows):
            rs = slice(r0, r0 + rows)
            dm = _dot_nt(dx1_ref[rs, :].astype(BF16), wout_ref[...])
            py = _dot(ps_ref[rs, :], wpo_ref[...])
            ay = _dot(o_ref[rs, :].astype(BF16), wao_ref[...])
            sp = _sigmoid(gp_ref[rs, :].astype(F32))
            sa = _sigmoid(ga_ref[rs, :].astype(F32))
            dpy = (dm * sp).astype(BF16)
            day = (dm * sa).astype(BF16)
            dpy_ref[rs, :] = dpy
            day_ref[rs, :] = day
            dg2_ref[rs, :d] = (dm * py * (sp * (1.0 - sp))).astype(BF16)
            dg2_ref[rs, d:] = (dm * ay * (sa * (1.0 - sa))).astype(BF16)
            dps_ref[rs, :] = _dot_nt(dpy, wpo_ref[...])
            da_ref[rs, :] = _dot_nt(day, wao_ref[...]).astype(BF16)

    row = lambda w: pl.BlockSpec((tm, w), lambda i: (i, 0))
    full = lambda a: pl.BlockSpec(a.shape, lambda i: (0, 0))
    dep_specs, dep_ops = _dep_args(dep)
    return pl.pallas_call(
        _after(body, 8, dep), name="merge_bwd",
        out_shape=(jax.ShapeDtypeStruct((t, d), BF16), jax.ShapeDtypeStruct((t, d), BF16),
                   jax.ShapeDtypeStruct((t, 2 * d), BF16), jax.ShapeDtypeStruct((t, POOL_WIDTH), F32),
                   jax.ShapeDtypeStruct((t, ATTN_WIDTH), BF16)),
        grid=(t // tm,),
        in_specs=[row(d), row(POOL_WIDTH), row(ATTN_WIDTH), pl.BlockSpec((tm, d), lambda i: (i, 0)),
                  pl.BlockSpec((tm, d), lambda i: (i, 1)), full(wpo), full(wao), full(wout)] + dep_specs,
        out_specs=(row(d), row(d), row(2 * d), row(POOL_WIDTH), row(ATTN_WIDTH)),
        compiler_params=_cparams("parallel"),
    )(dx1, ps, o, g2, g2, wpo, wao, wout, *dep_ops)


def _attn_bwd(qa, ka, v, do, lse4, seq, tq, dep=None):
    t = qa.shape[0]
    nq = seq // tq
    hp_n = N_HEADS // 2
    heads = [slice(e * LANES, (e + 1) * LANES) for e in range(2)]

    def body(q_ref, k_ref, v_ref, do_ref, lse_ref, dq_ref, dk_ref, dv_ref, dfr_ref, dk_acc, dv_acc, p_buf, dp_buf):
        diag_ok = lax.broadcasted_iota(jnp.int32, (tq, tq), 0) >= lax.broadcasted_iota(jnp.int32, (tq, tq), 1)
        lane_q = lax.broadcasted_iota(jnp.int32, (tq, LANES), 1)
        mine_q = [lane_q < HEAD_DIM, lane_q >= HEAD_DIM]
        dv_acc[...] = jnp.zeros_like(dv_acc)
        dk_acc[...] = jnp.zeros_like(dk_acc)
        dfr_ref[...] = jnp.zeros_like(dfr_ref)
        transposed = lambda a: a.astype(F32).T.astype(BF16)

        def q_step(i, _):
            q0 = pl.multiple_of(i * tq, tq)
            qs = [q_ref[pl.ds(q0, tq), hl] for hl in heads]
            dov = do_ref[pl.ds(q0, tq), :]
            dos = [jnp.where(mq, dov, jnp.zeros((), BF16)) for mq in mine_q]
            qts = [transposed(q) for q in qs]
            dots = [transposed(a) for a in dos]
            lss = [lse_ref[pl.ds(q0, tq), e:e + 1] for e in range(2)]

            def sweep1(j, dls, diagonal):
                r0 = pl.multiple_of(j * tq, tq)
                vv = v_ref[pl.ds(r0, tq), :]
                out = []
                for e, hl in enumerate(heads):
                    s = _dot_nt(qs[e], k_ref[pl.ds(r0, tq), hl])
                    if diagonal:
                        s = jnp.where(diag_ok, s, NEG_BIG)
                    p = jnp.exp(s - lss[e])
                    dp = _dot_nt(dos[e], vv)
                    p_buf[e, j] = p
                    dp_buf[e, j] = dp
                    dv_acc[j] += _dot(dots[e], p.astype(BF16))
                    out.append(dls[e] + _fold_lanes(p * dp, jnp.add))
                return tuple(out)

            dls = _causal_sweep(i, sweep1, (jnp.zeros((tq, LANES), F32),) * 2)
            dls = [jnp.sum(d, axis=1, keepdims=True) for d in dls]

            def sweep2(j, dqs, diagonal):
                r0 = pl.multiple_of(j * tq, tq)
                out = []
                for e, hl in enumerate(heads):
                    ds = p_buf[e, j] * (dp_buf[e, j] - dls[e])
                    dfr_ref[e, pl.ds(j, 1), :] += jnp.sum(ds, axis=0, keepdims=True)
                    dsb = ds.astype(BF16)
                    dk_acc[e, j] += _dot(qts[e], dsb)
                    out.append(dqs[e] + _dot(dsb, k_ref[pl.ds(r0, tq), hl]))
                return tuple(out)

            dqs = _causal_sweep(i, sweep2, (jnp.zeros((tq, LANES), F32),) * 2)
            dq = jnp.where(mine_q[0], dqs[0], pltpu.roll(dqs[1], HEAD_DIM, 1)) * ATTN_SCALE
            dq_ref[pl.ds(q0, tq), :] = dq.astype(BF16)
            return 0

        lax.fori_loop(0, nq, q_step, 0)
        for j in range(nq):
            rs = slice(j * tq, (j + 1) * tq)
            dk = jnp.where(mine_q[0], dk_acc[0, j].T, pltpu.roll(dk_acc[1, j].T, HEAD_DIM, 1))
            dk_ref[rs, :] = dk.astype(BF16)
            dv_ref[rs, :] = dv_acc[j].T.astype(BF16)

    wide = pl.BlockSpec((seq, 2 * LANES), lambda b, hp: (b, hp))
    col = pl.BlockSpec((seq, LANES), lambda b, hp: (b, hp))
    pair = pl.BlockSpec((None, seq, 2), lambda b, hp: (hp, b, 0))
    dep_specs, dep_ops = _dep_args(dep)
    return pl.pallas_call(
        _after(body, 5, dep), name="attn_bwd",
        out_shape=(jax.ShapeDtypeStruct((t, ATTN_WIDTH), BF16),) * 3 + (jax.ShapeDtypeStruct((N_HEADS, t // tq, tq), F32),),
        grid=(t // seq, hp_n),
        in_specs=[wide, wide, col, col, pair] + dep_specs,
        out_specs=(col, col, col, pl.BlockSpec((2, nq, tq), lambda b, hp: (hp, b, 0))),
        scratch_shapes=[pltpu.VMEM((2, nq, LANES, tq), F32), pltpu.VMEM((nq, LANES, tq), F32),
                        pltpu.VMEM((2, nq, tq, tq), F32), pltpu.VMEM((2, nq, tq, tq), F32)],
        compiler_params=_cparams("parallel", "arbitrary"),
    )(qa, ka, v, do, lse4, *dep_ops)


def _forget_bwd(dfc, fl, bf, seq):
    t = fl.shape[0]
    cb = min(256, seq)
    nb = seq // cb

    def body(dfc_ref, fl_ref, bf_ref, dfl_ref, db_ref):
        b = pl.program_id(0)
        ri = lax.broadcasted_iota(jnp.int32, (cb, cb), 0)
        ci = lax.broadcasted_iota(jnp.int32, (cb, cb), 1)
        tri = (ci >= ri).astype(BF16)
        carry = jnp.zeros((1, LANES), F32)
        dbs = jnp.zeros((1, LANES), F32)
        for blk in reversed(range(nb)):
            rs = slice(blk * cb, (blk + 1) * cb)
            dlf = _tri_dot(tri, -dfc_ref[rs, :]) + carry
            carry = dlf[0:1, :]
            dfl = dlf * _sigmoid(-(fl_ref[rs, :] + bf_ref[...]))
            dfl_ref[rs, :] = dfl.astype(BF16)
            dbs = dbs + jnp.sum(dfl, axis=0, keepdims=True)

        @pl.when(b == 0)
        def _():
            db_ref[...] = jnp.zeros_like(db_ref)

        db_ref[...] += dbs

    return pl.pallas_call(
        body, name="forget_bwd",
        out_shape=(jax.ShapeDtypeStruct((t, LANES), BF16), jax.ShapeDtypeStruct((1, LANES), F32)),
        grid=(t // seq,),
        in_specs=[pl.BlockSpec((seq, LANES), lambda b: (b, 0)), pl.BlockSpec((seq, LANES), lambda b: (b, 0)),
                  pl.BlockSpec((1, LANES), lambda b: (0, 0))],
        out_specs=(pl.BlockSpec((seq, LANES), lambda b: (b, 0)), pl.BlockSpec((1, LANES), lambda b: (0, 0))),
        compiler_params=_cparams("arbitrary"),
    )(dfc, fl, bf)


def _pool_bwd(dps, p, mix, scale, seq):
    t = dps.shape[0]

    def body(dps_ref, p_ref, mix_ref, sc_ref, du_ref, dmix_ref, dsc_ref):
        b = pl.program_id(0)

        @pl.when(b == 0)
        def _():
            dmix_ref[...] = jnp.zeros_like(dmix_ref)
            dsc_ref[...] = jnp.zeros_like(dsc_ref)

        tpos = lax.broadcasted_iota(jnp.int32, (seq, POOL_GROUP_DIM), 0)
        for g in range(POOL_GROUPS):
            sl = slice(g * POOL_GROUP_DIM, (g + 1) * POOL_GROUP_DIM)
            pb = p_ref[:, sl]
            dpsg = dps_ref[:, sl]
            pm = _dot(pb, mix_ref[g])
            dsc_ref[:, sl] += jnp.sum(dpsg * pm, axis=0, keepdims=True)
            dpm = (dpsg * sc_ref[:, sl]).astype(BF16)
            dmix_ref[g] += _dot_tn(pb, dpm)
            dp = _dot_nt(dpm, mix_ref[g])
            cnt = jnp.minimum(tpos + 1, POOL_WINDOWS[g]).astype(F32)
            s = dp / cnt
            for lvl in range(g + 1):
                d = 2 ** lvl
                s = s + jnp.where(tpos < seq - d, pltpu.roll(s, seq - d, 0), 0.0)
            du_ref[:, sl] = (s - dp).astype(BF16)

    return pl.pallas_call(
        body, name="pool_bwd",
        out_shape=(jax.ShapeDtypeStruct((t, POOL_WIDTH), BF16),
                   jax.ShapeDtypeStruct((POOL_GROUPS, POOL_GROUP_DIM, POOL_GROUP_DIM), F32),
                   jax.ShapeDtypeStruct((1, POOL_WIDTH), F32)),
        grid=(t // seq,),
        in_specs=[pl.BlockSpec((seq, POOL_WIDTH), lambda b: (b, 0)), pl.BlockSpec((seq, POOL_WIDTH), lambda b: (b, 0)),
                  pl.BlockSpec((POOL_GROUPS, POOL_GROUP_DIM, POOL_GROUP_DIM), lambda b: (0, 0, 0)),
                  pl.BlockSpec((1, POOL_WIDTH), lambda b: (0, 0))],
        out_specs=(pl.BlockSpec((seq, POOL_WIDTH), lambda b: (b, 0)),
                   pl.BlockSpec((POOL_GROUPS, POOL_GROUP_DIM, POOL_GROUP_DIM), lambda b: (0, 0, 0)),
                   pl.BlockSpec((1, POOL_WIDTH), lambda b: (0, 0))),
        compiler_params=_cparams("arbitrary"),
    )(dps, p, mix, scale)


def _in_bwd(du, dq, dk, dv, dg2, dfl, dx1, x, g, wu, wqkv, wg2, wft, tm):
    t, d = x.shape
    tm = min(tm, t)
    rows = min(ROW_CHUNK, tm)
    aw = ATTN_WIDTH

    def body(du_ref, dq_ref, dk_ref, dv_ref, dg2_ref, dfl_ref, dx1_ref, x_ref, g_ref, wu_ref, wqkv_ref, wg2_ref, wft_ref,
             dx_ref, dg_ref):
        i = pl.program_id(0)

        @pl.when(i == 0)
        def _():
            dg_ref[...] = jnp.zeros_like(dg_ref)

        for r0 in range(0, tm, rows):
            rs = slice(r0, r0 + rows)
            dh = _dot(du_ref[rs, :], wu_ref[...])
            dh += _dot(dq_ref[rs, :], wqkv_ref[0:aw, :])
            dh += _dot(dk_ref[rs, :], wqkv_ref[aw:2 * aw, :])
            dh += _dot(dv_ref[rs, :], wqkv_ref[2 * aw:3 * aw, :])
            dh += _dot(dg2_ref[rs, :], wg2_ref[...])
            dh += _dot(dfl_ref[rs, :], wft_ref[...])
            dxn, dg = _rms_bwd(x_ref[rs, :], g_ref[...], dh)
            dx_ref[rs, :] = dx1_ref[rs, :] + dxn
            dg_ref[...] += dg

    row = lambda w: pl.BlockSpec((tm, w), lambda i: (i, 0))
    full = lambda a: pl.BlockSpec(a.shape, lambda i: (0, 0))
    return pl.pallas_call(
        body, name="in_bwd",
        out_shape=(jax.ShapeDtypeStruct((t, d), F32), jax.ShapeDtypeStruct((1, d), F32)),
        grid=(t // tm,),
        in_specs=[row(POOL_WIDTH), row(aw), row(aw), row(aw), row(2 * d), row(LANES), row(d), row(d),
                  pl.BlockSpec((1, d), lambda i: (0, 0)), full(wu), full(wqkv), full(wg2), full(wft)],
        out_specs=(row(d), pl.BlockSpec((1, d), lambda i: (0, 0))),
        compiler_params=_cparams("arbitrary"),
    )(du, dq, dk, dv, dg2, dfl, dx1, x, g, wu, wqkv, wg2, wft)


def _position():
    return lax.axis_index("x"), lax.axis_index("y"), lax.axis_index("c")


def _remote(src, dst, send_sem, recv_sem, device):
    return pltpu.make_async_remote_copy(src_ref=src, dst_ref=dst, send_sem=send_sem, recv_sem=recv_sem,
                                        device_id=device, device_id_type=MESH)


HBM = pl.BlockSpec(memory_space=pltpu.HBM)
SEM = pl.BlockSpec(memory_space=pltpu.SEMAPHORE)
DATAFLOW = pltpu.SideEffectType.DATAFLOW_SIDE_EFFECTING


def _copies_start(name, arrays, plan, m, dep=None):
    n = len(arrays)
    arrays = [pltpu.with_memory_space_constraint(a, pltpu.HBM) for a in arrays]

    def body(*refs):
        ins, send_sem, recv_sem, token = refs[:n], refs[n], refs[n + 1], refs[2 * n + 2]
        for i, (src, dst, device, _) in enumerate(plan(ins, *_position())):
            _remote(src, dst, send_sem.at[i], recv_sem.at[i], device).start()
        token[...] = jnp.zeros_like(token)

    dep_specs, dep_ops = _dep_args(dep)
    outs = pl.pallas_call(
        _after(body, n, dep), name=name,
        out_shape=(pltpu.SemaphoreType.DMA((m,)), pltpu.SemaphoreType.DMA((m,)),
                   *[pltpu.HBM(a.shape, a.dtype) for a in arrays], jax.ShapeDtypeStruct((8, LANES), F32)),
        in_specs=[HBM] * n + dep_specs, out_specs=(SEM, SEM, *[HBM] * n, pl.BlockSpec(memory_space=pltpu.VMEM)),
        input_output_aliases={i: i + 2 for i in range(n)},
        compiler_params=pltpu.CompilerParams(has_side_effects=DATAFLOW),
    )(*arrays, *dep_ops)
    return (outs[0], outs[1]), list(outs[2:2 + n]), outs[2 + n]


def _copies_wait(name, sems, arrays, plan, after):
    n = len(arrays)
    afters = list(after) if isinstance(after, (list, tuple)) else [after]

    def body(*refs):
        ins, send_sem, recv_sem = refs[:n], refs[n], refs[n + 1]
        for i, (src, dst, device, landing) in enumerate(plan(ins, *_position())):
            _remote(src, dst, send_sem.at[i], recv_sem.at[i], device).wait_send()
            _remote(landing, landing, send_sem.at[i], recv_sem.at[i], device).wait_recv()

    outs = pl.pallas_call(
        body, name=name,
        out_shape=tuple(pltpu.HBM(a.shape, a.dtype) for a in arrays),
        in_specs=[HBM] * n + [SEM, SEM] + [ANY] * len(afters), out_specs=tuple([HBM] * n),
        input_output_aliases={i: i for i in range(n)},
        compiler_params=pltpu.CompilerParams(has_side_effects=DATAFLOW),
    )(*arrays, sems[0], sems[1], *afters)
    return list(outs)


def _tie(x, dep):
    for token in _dep_list(dep):
        x = x + token[0, 0]
    return x


def _other_chips(x, y):
    return [(1 - x, y), (x, 1 - y), (1 - x, 1 - y)]


def _gather_begin(tag, shards, token, column_halves=False):
    n = len(shards)
    lands = [lax.empty((N_CHIPS,) + s.shape, s.dtype) for s in shards]
    if column_halves:
        cols = lambda ref, h: pl.ds(pl.multiple_of(h * (ref.shape[-1] // 2), LANES), ref.shape[-1] // 2)
        mine = lambda ref, h: ref.at[:, cols(ref, h)]
        landed = lambda ref, chip, h: ref.at[chip, :, cols(ref, h)]
    else:
        mine = lambda ref, h: ref.at[h]
        landed = lambda ref, chip, h: ref.at[chip, h]

    def plan(refs, x, y, c):
        return [(mine(refs[k], c), landed(refs[n + k], 2 * x + y, c), (ox, oy, c), landed(refs[n + k], 2 * ox + oy, c))
                for k in range(n) for ox, oy in _other_chips(x, y)]

    sems, thru, token = _copies_start(f"gather_{tag}_ici_start", list(shards) + lands, plan, 3 * n, dep=token)
    return dict(tag=tag, n=n, plan=plan, sems=sems, arrays=thru, token=token, landed=landed)


def _gather_forward(st, after):
    n, tag, landed = st["n"], st["tag"], st["landed"]
    thru = _copies_wait(f"gather_{tag}_ici_wait", st["sems"], st["arrays"], st["plan"], after)

    def plan(refs, x, y, c):
        return [(landed(refs[k], 2 * ox + oy, c), landed(refs[k], 2 * ox + oy, c), (x, y, 1 - c),
                 landed(refs[k], 2 * ox + oy, 1 - c))
                for k in range(n) for ox, oy in _other_chips(x, y)]

    sems, lands, token = _copies_start(f"gather_{tag}_fwd_start", thru[n:], plan, 3 * n)
    return dict(tag=tag, n=n, plan=plan, sems=sems, arrays=lands, token=token, shards=thru[:n])


def _gather_end(st, after, merge=True):
    lands = _copies_wait(f"gather_{st['tag']}_fwd_wait", st["sems"], st["arrays"], st["plan"], after)
    if not merge:
        return lands, st["shards"]
    me = 2 * lax.axis_index("x") + lax.axis_index("y")
    return [lax.dynamic_update_index_in_dim(g, s, me, 0) for g, s in zip(lands, st["shards"])]


def _add_keep_give(name, pos, a, a_keep, a_give, b, b_keep, b_give, steps):
    r, c = b.shape[-2:]

    def spec(arr, fn):
        lead = arr.ndim - 2

        def index(i, p):
            idx = tuple(fn(i, p))
            return idx if len(idx) == arr.ndim else idx + (0, 0)

        return pl.BlockSpec((None,) * lead + (r, c), index)

    out_spec = pl.BlockSpec((None, r, c), lambda i, p: (i, 0, 0))

    def body(p_ref, ak_ref, bk_ref, ag_ref, bg_ref, keep_ref, give_ref):
        keep_ref[...] = ak_ref[...] + bk_ref[...].astype(F32)
        give_ref[...] = (ag_ref[...] + bg_ref[...].astype(F32)).astype(BF16)

    return pl.pallas_call(
        body, name=name,
        out_shape=(jax.ShapeDtypeStruct((steps, r, c), F32), jax.ShapeDtypeStruct((steps, r, c), BF16)),
        grid_spec=pltpu.PrefetchScalarGridSpec(
            num_scalar_prefetch=1, grid=(steps,),
            in_specs=[spec(a, a_keep), spec(b, b_keep), spec(a, a_give), spec(b, b_give)],
            out_specs=(out_spec, out_spec)),
        compiler_params=_cparams("parallel"),
    )(pos, a, b, a, b)


def _add_last(name, a, b):
    _, r, c = a.shape
    blk = pl.BlockSpec((None, r, c), lambda i: (0, 0, 0))

    def body(a_ref, b_ref, o_ref):
        o_ref[...] = a_ref[...] + b_ref[...].astype(F32)

    return pl.pallas_call(
        body, name=name, out_shape=jax.ShapeDtypeStruct((r, c), F32), grid=(1,), in_specs=[blk, blk],
        out_specs=pl.BlockSpec((r, c), lambda i: (0, 0)), compiler_params=_cparams("arbitrary"),
    )(a, b)


def _exchange_part(gives, lands, peer_fn):
    n = len(gives)

    def plan(refs, x, y, c):
        return [(refs[k], refs[n + k], peer_fn(x, y, c), refs[n + k]) for k in range(n)]

    return list(gives) + list(lands), plan, n


def _join_parts(parts):
    offsets, total = [], 0
    for arrays, _, _ in parts:
        offsets.append(total)
        total += len(arrays)

    def plan(refs, x, y, c):
        copies = []
        for (arrays, part_plan, _), off in zip(parts, offsets):
            copies += part_plan(refs[off:off + len(arrays)], x, y, c)
        return copies

    return [a for arrays, _, _ in parts for a in arrays], plan, sum(m for _, _, m in parts)


def _reduce_begin(tag, grads, column_halves=False):
    n = len(grads)
    if column_halves:
        half = lambda ref, j, h: ref.at[j, :, pl.ds(pl.multiple_of(h * (ref.shape[2] // 2), LANES), ref.shape[2] // 2)]
        lands = [lax.empty((N_CHIPS, g.shape[1], g.shape[2] // 2), F32) for g in grads]
    else:
        half = lambda ref, j, h: ref.at[j, h]
        lands = [lax.empty((N_CHIPS,) + g.shape[2:], F32) for g in grads]

    def plan(refs, x, y, c):
        return [(half(refs[k], j, 1 - c), refs[n + k].at[j], (x, y, 1 - c), refs[n + k].at[j])
                for k in range(n) for j in range(N_CHIPS)]

    return dict(tag=tag, n=n, stage="c", grads=list(grads), column_halves=column_halves,
                part=(list(grads) + lands, plan, N_CHIPS * n))


def _reduce_next(st, thru):
    tag, n, stage = st["tag"], st["n"], st["stage"]
    first, recv = thru[:n], thru[n:]
    x, y, c = _position()
    if stage == "c":
        pos = jnp.stack([c, x]).astype(jnp.int32)
        if st["column_halves"]:
            mine = lambda chip: (lambda i, p: (chip(p) + i, 0, p[0]))
        else:
            mine = lambda chip: (lambda i, p: (chip(p) + i, p[0]))
        sums = [_add_keep_give(
            f"rs{tag}_c_add{k}", pos,
            first[k], mine(lambda p: 2 * p[1]), mine(lambda p: 2 * (1 - p[1])),
            recv[k], lambda i, p: (2 * p[1] + i,), lambda i, p: (2 * (1 - p[1]) + i,), 2) for k in range(n)]
        lands = [lax.empty(s[1].shape, BF16) for s in sums]
        return dict(tag=tag, n=n, stage="x", keep=[s[0] for s in sums],
                    part=_exchange_part([s[1] for s in sums], lands, lambda x, y, c: (1 - x, y, c)))
    if stage == "x":
        pos = jnp.stack([y]).astype(jnp.int32)
        sums = [_add_keep_give(
            f"rs{tag}_x_add{k}", pos,
            st["keep"][k], lambda i, p: (p[0],), lambda i, p: (1 - p[0],),
            recv[k], lambda i, p: (p[0],), lambda i, p: (1 - p[0],), 1) for k in range(n)]
        lands = [lax.empty(s[1].shape, BF16) for s in sums]
        return dict(tag=tag, n=n, stage="y", keep=[s[0] for s in sums],
                    part=_exchange_part([s[1] for s in sums], lands, lambda x, y, c: (x, 1 - y, c)))
    if stage == "y":
        mine = [_add_last(f"rs{tag}_y_add{k}", st["keep"][k], recv[k]) for k in range(n)]
        lands = [lax.empty(m.shape, F32) for m in mine]
        return dict(tag=tag, n=n, stage="swap", part=_exchange_part(mine, lands, lambda x, y, c: (x, y, 1 - c)))
    return dict(tag=tag, done=list(zip(first, recv)))


def _small_begin(tag, v):
    land = lax.empty((N_DEV,) + v.shape, F32)
    flips = [(fx, fy, fc) for fx in (0, 1) for fy in (0, 1) for fc in (0, 1)][1:]

    def plan(refs, x, y, c):
        copies = []
        for fx, fy, fc in flips:
            px, py, pc = (1 - x if fx else x), (1 - y if fy else y), (1 - c if fc else c)
            copies.append((refs[0], refs[1].at[4 * x + 2 * y + c], (px, py, pc), refs[1].at[4 * px + 2 * py + pc]))
        return copies

    return dict(tag=tag, n=1, stage="swap", grads=[v], part=([v, land], plan, len(flips)))


def _small_sum(name, own, land):
    x, y, c = _position()
    me = jnp.stack([4 * x + 2 * y + c]).astype(jnp.int32)

    def body(me_ref, own_ref, land_ref, out_ref):
        term = lambda dev: jnp.where(me_ref[0] == dev, own_ref[...], land_ref[dev])
        acc = term(0)
        for dev in range(1, N_DEV):
            acc = acc + term(dev)
        out_ref[...] = acc

    return pl.pallas_call(
        body, name=name, out_shape=jax.ShapeDtypeStruct(own.shape, F32),
        grid_spec=pltpu.PrefetchScalarGridSpec(
            num_scalar_prefetch=1, grid=(1,),
            in_specs=[pl.BlockSpec(own.shape, lambda i, m: (0, 0)), pl.BlockSpec(land.shape, lambda i, m: (0, 0, 0))],
            out_specs=pl.BlockSpec(own.shape, lambda i, m: (0, 0))),
        compiler_params=_cparams("arbitrary"),
    )(me, own, land)


def _adamw_update(w, gg, m, v):
    mn = ADAM_B1 * m + (1.0 - ADAM_B1) * gg
    vn = ADAM_B2 * v + (1.0 - ADAM_B2) * (gg * gg)
    m_hat = mn / (1.0 - ADAM_B1 ** ADAM_STEP)
    v_hat = vn / (1.0 - ADAM_B2 ** ADAM_STEP)
    return -ADAM_LR * (m_hat / (jnp.sqrt(v_hat) + ADAM_EPS) + ADAM_WD * w), mn, vn


def _adamw(name, w, g, m, v):
    def body(w_ref, g_ref, m_ref, v_ref, d_ref, mo_ref, vo_ref):
        d_ref[...], mo_ref[...], vo_ref[...] = _adamw_update(w_ref[...], g_ref[...], m_ref[...], v_ref[...])

    blk = pl.BlockSpec(w.shape, lambda i: (0, 0))
    return pl.pallas_call(
        body, name=name, out_shape=(jax.ShapeDtypeStruct(w.shape, F32),) * 3, grid=(1,),
        in_specs=[blk] * 4, out_specs=(blk,) * 3, compiler_params=_cparams("arbitrary"),
    )(w, g, m, v)


def _rows_to_bf16(name, w):
    r, _, c = w.shape

    def body(w_ref, o_ref):
        o_ref[...] = w_ref[:, 0, :].astype(BF16)

    return pl.pallas_call(
        body, name=name, out_shape=jax.ShapeDtypeStruct((r, c), BF16), grid=(1,),
        in_specs=[pl.BlockSpec((r, 1, c), lambda i: (0, 0, 0))], out_specs=pl.BlockSpec((r, c), lambda i: (0, 0)),
        compiler_params=_cparams("arbitrary"),
    )(w)


def _adamw_rows(name, pos_c, w, g_mine, g_other, m, v):
    r, _, c = w.shape
    ch = c // 2

    def body(p_ref, w_ref, gm_ref, go_ref, m_ref, v_ref, g_ref, d_ref, mo_ref, vo_ref):
        gg = jnp.where(pl.program_id(0) == p_ref[0], gm_ref[...], go_ref[...])
        dl, mn, vn = _adamw_update(w_ref[:, 0, :], gg, m_ref[:, 0, :], v_ref[:, 0, :])
        g_ref[:, 0, :] = gg
        d_ref[:, 0, :] = dl
        mo_ref[:, 0, :] = mn
        vo_ref[:, 0, :] = vn

    rows = pl.BlockSpec((r, 1, ch), lambda h, p: (0, 0, h))
    half = pl.BlockSpec((r, ch), lambda h, p: (0, 0))
    return pl.pallas_call(
        body, name=name, out_shape=(jax.ShapeDtypeStruct(w.shape, F32),) * 4,
        grid_spec=pltpu.PrefetchScalarGridSpec(
            num_scalar_prefetch=1, grid=(2,), in_specs=[rows, half, half, rows, rows], out_specs=(rows,) * 4),
        compiler_params=_cparams("parallel"),
    )(pos_c, w, g_mine, g_other, m, v)


def _adamw_halves(name, pos_c, w, g_mine, g_other, m, v, tr, dep=None):
    r, c = w.shape
    rh = r // 2
    tr = tr if rh % tr == 0 else rh
    nt = rh // tr

    def body(p_ref, w_ref, gm_ref, go_ref, m_ref, v_ref, g_ref, d_ref, mo_ref, vo_ref):
        gg = jnp.where(pl.program_id(0) == p_ref[0], gm_ref[...], go_ref[...])
        g_ref[...] = gg
        d_ref[...], mo_ref[...], vo_ref[...] = _adamw_update(w_ref[...], gg, m_ref[...], v_ref[...])

    full = pl.BlockSpec((tr, c), lambda h, i, p: (h * nt + i, 0))
    half = pl.BlockSpec((tr, c), lambda h, i, p: (i, 0))
    dep_specs, dep_ops = _dep_args(dep)
    return pl.pallas_call(
        _after(body, 6, dep), name=name, out_shape=(jax.ShapeDtypeStruct((r, c), F32),) * 4,
        grid_spec=pltpu.PrefetchScalarGridSpec(
            num_scalar_prefetch=1, grid=(2, nt),
            in_specs=[full, half, half, full, full] + dep_specs, out_specs=(full,) * 4),
        compiler_params=_cparams("parallel", "parallel"),
    )(pos_c, w, g_mine, g_other, m, v, *dep_ops)


def _col_sharded_to_comm(g):
    k, n = g.shape
    return g.reshape(2, k // 2, N_CHIPS, n // N_CHIPS).transpose(2, 0, 1, 3)


def _row_sharded_to_comm(g):
    r, c = g.shape
    return g.reshape(N_CHIPS, 2, r // (2 * N_CHIPS), c)


def _col_sharded_full(g):
    _, _, rh, c = g.shape
    return g.reshape(N_CHIPS, 2 * rh, c).transpose(1, 0, 2).reshape(2 * rh, N_CHIPS * c)


def _row_sharded_full(g):
    _, _, rh, c = g.shape
    return g.reshape(N_CHIPS * 2 * rh, c)


def _chip_rows(w3, start, stop, own=None, me=None):
    r = w3.shape[1]
    parts = []
    for chip in range(N_CHIPS):
        lo, hi = max(start - chip * r, 0), min(stop - chip * r, r)
        if lo < hi:
            part = w3[chip, lo:hi]
            parts.append(part if own is None else jnp.where(me == chip, own[lo:hi], part))
    return parts[0] if len(parts) == 1 else jnp.concatenate(parts, axis=0)


def _pack_small(g1, bfv, mix, scale, g2n, gf, extra=None):
    row8 = jnp.pad(bfv.reshape(1, N_HEADS), ((0, 0), (0, LANES - N_HEADS)))
    if extra is not None:
        row8 = row8 + jnp.pad(extra[:, :1], ((0, 0), (N_HEADS, LANES - N_HEADS - 1)))
    return jnp.concatenate([
        g1.reshape(8, LANES), jnp.pad(row8, ((0, 7), (0, 0))), mix.reshape(512, LANES),
        jnp.pad(scale.reshape(4, LANES), ((0, 4), (0, 0))), g2n.reshape(8, LANES), gf.reshape(8, LANES)], axis=0)


def _unpack_small(s, like):
    g1, bfv, mix, scale, g2n, gf = like
    return (s[0:8].reshape(g1.shape), s[8, :N_HEADS].reshape(bfv.shape), s[16:528].reshape(mix.shape),
            s[528:532].reshape(scale.shape), s[536:544].reshape(g2n.shape), s[544:552].reshape(gf.shape))


class _MeshLinks:
    def __init__(self, shards_in, shards_rest):
        self.gin = _gather_begin("in", shards_in, None, column_halves=True)
        self.grest = _gather_begin("rest", shards_rest, self.gin["token"])
        self.tokens = {"gather": self.grest["token"]}
        self.groups, self.flight, self.slot = {}, None, 0

    @property
    def token(self):
        return list(self.tokens.values())

    def tie(self, x):
        return _tie(x, self.token)

    def weights_in(self, after):
        st = _gather_forward(self.gin, after)
        (g,), (own,) = _gather_end(st, st["token"], merge=False)
        return g, own, 2 * lax.axis_index("x") + lax.axis_index("y")

    def rest_forward(self, after):
        self.grest = _gather_forward(self.grest, after)
        self.tokens["gather"] = self.grest["token"]

    def weights_rest(self, after):
        g = _gather_end(self.grest, after)
        del self.tokens["gather"]
        return [_col_sharded_full(g[0]), _col_sharded_full(g[1])] + [_row_sharded_full(a) for a in g[2:]]

    def advance(self, after, begin=()):
        slot = self.slot
        self.slot += 1
        if self.flight is not None:
            tags, sems, parts = self.flight
            arrays, plan, _ = _join_parts(parts)
            thru = _copies_wait(f"slot{slot}_wait", sems, arrays, plan, after)
            for tag, part in zip(tags, parts):
                self.groups[tag] = _reduce_next(self.groups[tag], thru[:len(part[0])])
                thru = thru[len(part[0]):]
        for st in begin:
            self.groups[st["tag"]] = st
        live = [(tag, st["part"]) for tag, st in self.groups.items() if "part" in st]
        self.flight = None
        self.tokens.pop("reduce", None)
        if live:
            arrays, plan, m = _join_parts([part for _, part in live])
            sems, thru, token = _copies_start(f"slot{slot}_start", arrays, plan, m)
            parts = []
            for _, (part_arrays, part_plan, part_m) in live:
                parts.append((thru[:len(part_arrays)], part_plan, part_m))
                thru = thru[len(part_arrays):]
            self.flight = ([tag for tag, _ in live], sems, parts)
            self.tokens["reduce"] = token

    def reduced(self, tag):
        return self.groups[tag]["done"]


class _NoLinks:
    token = None

    def __init__(self, w_in, rest):
        self.w_in, self.rest, self.grads = w_in, rest, {}

    def tie(self, x):
        return x

    def weights_in(self, after):
        return self.w_in, None, None

    def rest_forward(self, after):
        pass

    def weights_rest(self, after):
        return self.rest

    def advance(self, after, begin=()):
        for st in begin:
            self.grads[st["tag"]] = st["grads"]


def _local_step(links, x, target, seq, norm1_g, b_forget, pool_mix, pool_scale, norm2_g, norm_f_g, between=None):
    t, d = x.shape
    tq = min(256, seq)
    aw = ATTN_WIDTH
    o_q, o_f, o_g = POOL_WIDTH, POOL_WIDTH + 3 * aw, POOL_WIDTH + 3 * aw + N_HEADS
    bf = jnp.pad(b_forget, ((0, 0), (0, LANES - N_HEADS)))
    mixb = pool_mix.astype(BF16)

    h = _norm_fwd("norm1_fwd", x, links.tie(norm1_g), 512)
    w_in, own, me = links.weights_in(h)
    wu = _chip_rows(w_in, 0, o_q, own, me)
    wqkv = _chip_rows(w_in, o_q, o_f, own, me)
    wft = jnp.pad(_chip_rows(w_in, o_f, o_g, own, me), ((0, LANES - N_HEADS), (0, 0)))
    wg2 = _chip_rows(w_in, o_g, N_CHIPS * w_in.shape[1], own, me)
    wf = wft.T
    u = _matmul("mm_u", h, wu, "nt", F32, 1024, 512, d)
    g2 = _matmul("mm_gates", h, wg2, "nt", BF16, 1024, 1024, d)
    fl, fcum = _forget_fwd(h, wf, bf, seq)
    qa, ka, v = _attn_prep(h, _head_blocks(wqkv[:aw]), _head_blocks(wqkv[aw:2 * aw]), wqkv[2 * aw:], fcum, 1024)
    p, ps = _pool_fwd(u, mixb, pool_scale, seq)
    links.rest_forward([ps, qa, g2])
    o, lse = _attn_fwd(qa, ka, v, seq, tq, dep=links.token)
    w_pool_out, w_attn_out, w_out, w_ffn_gate, w_ffn_up, w_ffn_down = links.weights_rest(o)
    merged, x1 = _merge_fwd(x, ps, o, g2, w_pool_out, w_attn_out, w_out, 512)
    h2, gt, up, act, x2 = _ffn_fwd(x1, norm2_g, w_ffn_gate, w_ffn_up, w_ffn_down, 1024, 256)
    loss, dx2, d_gf = _final_fwd_bwd(x2, target, norm_f_g, 512)

    dgt, dup, dx1, d_g2n = _ffn_bwd(dx2, x1, norm2_g, gt, up, w_ffn_gate, w_ffn_up, w_ffn_down, 1024, 256)
    d_wd = _matmul("dw_down", act, dx2, "tn", F32, 1408, 1024, 1024)
    d_wg = _matmul("dw_gate", dgt, h2, "tn", F32, 1408, 1024, 1024)
    d_wu = _matmul("dw_up", dup, h2, "tn", F32, 1408, 1024, 1024)
    links.advance(None, begin=[_reduce_begin("a", [_row_sharded_to_comm(g) for g in (d_wg, d_wu, d_wd)])])
    dpy, day, dg2, dps, da = _merge_bwd(dx1, ps, o, g2, w_pool_out, w_attn_out, w_out, 512, dep=links.token)
    links.advance(dps)
    d_wout = _matmul("dw_out", merged, dx1, "tn", F32, 1024, 1024, 1024)
    d_wpo = _matmul("dw_pool_out", ps, dpy, "tn", F32, 512, 1024, 1024)
    d_wao = _matmul("dw_attn_out", o, day, "tn", F32, 512, 1024, 1024)
    dq, dk, dv, dfr = _attn_bwd(qa, ka, v, da, lse, seq, tq, dep=links.token)
    links.advance(dq, begin=[_reduce_begin(
        "m", [_col_sharded_to_comm(d_wpo), _col_sharded_to_comm(d_wao), _row_sharded_to_comm(d_wout)])])
    dfc = jnp.pad(dfr.reshape(N_HEADS, t).T, ((0, 0), (0, LANES - N_HEADS)))
    dfl, d_bf = _forget_bwd(dfc, fl, bf, seq)
    du, d_mix, d_scale = _pool_bwd(dps, p, mixb, links.tie(pool_scale), seq)
    d_wu_in = _matmul("dw_in_u", du, h, "tn", F32, 512, 1024, 1024)
    d_wq = _matmul("dw_in_q", dq, h, "tn", F32, 512, 1024, 1024, dep=links.token)
    d_wk = _matmul("dw_in_k", dk, h, "tn", F32, 512, 1024, 1024, dep=links.token)
    d_wv = _matmul("dw_in_v", dv, h, "tn", F32, 512, 1024, 1024, dep=links.token)
    links.advance([d_wu_in, d_wq, d_wk, d_wv])
    d_wf = _matmul("dw_in_f", dfl, h, "tn", F32, LANES, 1024, 512)
    d_wg2 = _matmul("dw_in_gates", dg2, h, "tn", F32, 1024, 1024, 1024, dep=links.token)
    d_win = jnp.concatenate([d_wu_in, d_wq, d_wk, d_wv, d_wf[:N_HEADS], d_wg2], axis=0)
    comm_b = [d_win.reshape(N_CHIPS, d_win.shape[0] // N_CHIPS, d)]
    small = (jnp.zeros_like(norm1_g), d_bf[:, :N_HEADS], d_mix, d_scale, d_g2n, d_gf)
    links.advance(comm_b, begin=[_reduce_begin("b", comm_b, column_halves=True),
                                 _small_begin("small", _pack_small(*small, extra=loss))])
    if between is not None:
        between()
    dx, d_g1 = _in_bwd(du, dq, dk, dv, dg2, dfl, dx1, x, links.tie(norm1_g), wu, wqkv, wg2, wft, 512)
    return loss, dx, (d_g1,) + small[1:]


def kernel(x, norm1_g, w_in, b_forget, pool_mix, pool_scale, w_pool_out, w_attn_out, w_out, norm2_g, w_ffn_gate, w_ffn_up, w_ffn_down, norm_f_g, loss_target, m_norm1_g, m_w_in, m_b_forget, m_pool_mix, m_pool_scale, m_w_pool_out, m_w_attn_out, m_w_out, m_norm2_g, m_w_ffn_gate, m_w_ffn_up, m_w_ffn_down, m_norm_f_g, v_norm1_g, v_w_in, v_b_forget, v_pool_mix, v_pool_scale, v_w_pool_out, v_w_attn_out, v_w_out, v_norm2_g, v_w_ffn_gate, v_w_ffn_up, v_w_ffn_down, v_norm_f_g):
    nb, seq, d = x.shape
    group_a = ((w_ffn_gate, m_w_ffn_gate, v_w_ffn_gate, True, 9), (w_ffn_up, m_w_ffn_up, v_w_ffn_up, True, 10),
               (w_ffn_down, m_w_ffn_down, v_w_ffn_down, False, 11))
    group_m = ((w_pool_out, m_w_pool_out, v_w_pool_out, False, 5), (w_attn_out, m_w_attn_out, v_w_attn_out, False, 6),
               (w_out, m_w_out, v_w_out, False, 7))
    group_b = ((w_in, m_w_in, v_w_in, False, 1),)
    small_w = (norm1_g, b_forget, pool_mix, pool_scale, norm2_g, norm_f_g)
    small_m = (m_norm1_g, m_b_forget, m_pool_mix, m_pool_scale, m_norm2_g, m_norm_f_g)
    small_v = (v_norm1_g, v_b_forget, v_pool_mix, v_pool_scale, v_norm2_g, v_norm_f_g)
    small_pos = (0, 2, 3, 4, 8, 12)
    view = lambda a, tr: a[0].T if tr else a[0]
    unview = lambda a, tr, like: (a.T if tr else a).reshape(like.shape)

    def shard(w, tr):
        lw = view(w, tr).astype(BF16)
        return lw.reshape(2, lw.shape[0] // 2, lw.shape[1])

    cm = lambda a: jnp.transpose(a, (2, 0, 1))
    shard_in = _rows_to_bf16("w_in_to_bf16", cm(w_in))
    links = _MeshLinks([shard_in],
                       [shard(w_pool_out, False), shard(w_attn_out, False), shard(w_out, False),
                        shard(w_ffn_gate, True), shard(w_ffn_up, True), shard(w_ffn_down, False)])
    grads, deltas, new_m, new_v = [None] * 13, [None] * 13, [None] * 13, [None] * 13
    pos_c = jnp.stack([lax.axis_index("c")]).astype(jnp.int32)

    def update(tag, group, dep, members=(0, 1, 2)):
        last = []
        reduced = links.reduced(tag)
        for k in members:
            (w, m, v, tr, pos), (mine, other) = group[k], reduced[k]
            outs = _adamw_halves(f"adamw_{tag}{k}", pos_c, view(w, tr), mine, other, view(m, tr), view(v, tr), 256,
                                 dep=dep)
            grads[pos], deltas[pos], new_m[pos], new_v[pos] = (unview(a, tr, w) for a in outs)
            last.append(outs[1])
        return last

    def update_a():
        links.advance(update("a", group_a, links.token, members=(0, 1)))

    loss, dx, small_g = _local_step(
        links, x.reshape(nb * seq, d), loss_target.reshape(nb * seq, d), seq,
        norm1_g, b_forget, pool_mix[0], pool_scale, norm2_g, norm_f_g.reshape(1, d), between=update_a)

    links.advance(dx, begin=[_small_begin("g1", small_g[0].reshape(8, LANES))])
    last = update("m", group_m, links.token) + update("a", group_a, links.token, members=(2,))
    small_rest = _small_sum("small_sum", *links.reduced("small")[0])
    links.advance(last + [small_rest])
    small_sum = jnp.concatenate([_small_sum("g1_sum", *links.reduced("g1")[0]), small_rest[8:]], axis=0)
    loss_out = small_sum[8, N_HEADS]
    dl, mn, vn = _adamw("adamw_small", _pack_small(*small_w), small_sum * _small_mask(), _pack_small(*small_m),
                        _pack_small(*small_v))
    for pos, g, a, b, e in zip(small_pos, _unpack_small(small_sum, small_w), _unpack_small(dl, small_w),
                               _unpack_small(mn, small_w), _unpack_small(vn, small_w)):
        grads[pos], deltas[pos], new_m[pos], new_v[pos] = g, a, b, e
    links.advance([dl])
    (mine, other), = links.reduced("b")
    outs = _adamw_rows("adamw_b0", pos_c, cm(w_in), mine, other, cm(m_w_in), cm(v_w_in))
    grads[1], deltas[1], new_m[1], new_v[1] = (jnp.transpose(a, (1, 2, 0)) for a in outs)

    return (loss_out, dx.reshape(nb, seq, d), *grads, *deltas, *new_m, *new_v)


def _small_mask():
    rows = lax.broadcasted_iota(jnp.int32, (552, LANES), 0)
    lanes = lax.broadcasted_iota(jnp.int32, (552, LANES), 1)
    return jnp.where(jnp.logical_and(rows == 8, lanes == N_HEADS), 0.0, 1.0).astype(F32)
```

```python
import jax
import jax.numpy as jnp
from jax import lax
from jax.experimental import pallas as pl
from jax.experimental.pallas import tpu as pltpu

F32 = jnp.float32
BF16 = jnp.bfloat16

D_MODEL = 1024
POOL_WINDOWS = (2, 4, 8, 16)
POOL_GROUPS = 4
POOL_GROUP_DIM = 128
POOL_WIDTH = 512
HEAD_DIM = 64
N_HEADS = 8
ATTN_WIDTH = 512
D_FF = 2816
RMS_EPS = 1e-6
ATTN_SCALE = HEAD_DIM ** -0.5
NEG_BIG = -1e30

ADAM_LR = 0.001
ADAM_B1 = 0.9
ADAM_B2 = 0.999
ADAM_EPS = 1e-08
ADAM_WD = 0.01
ADAM_STEP = 10

LANES = 128
N_CHIPS = 4
N_DEV = 8
VMEM_LIMIT_V7X = 52 * 1024 * 1024
ROW_CHUNK = 256
MESH = pl.DeviceIdType.MESH
ANY = pl.BlockSpec(memory_space=pl.ANY)


def _cparams(*sem):
    return pltpu.CompilerParams(dimension_semantics=sem if sem else None, vmem_limit_bytes=VMEM_LIMIT_V7X)


def _dep_list(dep):
    return [] if dep is None else (list(dep) if isinstance(dep, (list, tuple)) else [dep])


def _after(body, n_in, dep):
    k = len(_dep_list(dep))
    if k == 0:
        return body

    def wrapped(*refs):
        body(*refs[:n_in], *refs[n_in + k:])

    return wrapped


def _dep_args(dep):
    deps = _dep_list(dep)
    return [ANY] * len(deps), deps


def _dot(a, b):
    return lax.dot_general(a, b, (((1,), (0,)), ((), ())), preferred_element_type=F32)


def _dot_nt(a, b):
    return lax.dot_general(a, b, (((1,), (1,)), ((), ())), preferred_element_type=F32)


def _dot_tn(a, b):
    return lax.dot_general(a, b, (((0,), (0,)), ((), ())), preferred_element_type=F32)


def _sigmoid(x):
    return jax.nn.sigmoid(x)


def _rms_fwd(x, g):
    r = lax.rsqrt(jnp.mean(x * x, axis=-1, keepdims=True) + RMS_EPS)
    return (x * r) * g


def _rms_bwd(x, g, dy):
    r = lax.rsqrt(jnp.mean(x * x, axis=-1, keepdims=True) + RMS_EPS)
    xh = x * r
    dg = jnp.sum(dy * xh, axis=0, keepdims=True)
    dxh = dy * g
    dx = r * (dxh - xh * jnp.mean(dxh * xh, axis=-1, keepdims=True))
    return dx, dg


def _matmul(name, a, b, mode, out_dtype, tm, tn, tk, dep=None):
    if mode == "nn":
        (m, k), (_, n) = a.shape, b.shape
    elif mode == "nt":
        (m, k), (n, _) = a.shape, b.shape
    else:
        (k, m), (_, n) = a.shape, b.shape
    tm, tn, tk = min(tm, m), min(tn, n), min(tk, k)
    assert m % tm == 0 and n % tn == 0 and k % tk == 0, (name, m, n, k, tm, tn, tk)
    nk = k // tk
    if mode == "tn":
        a_spec = pl.BlockSpec((tk, tm), lambda i, j, kk: (kk, i))
    else:
        a_spec = pl.BlockSpec((tm, tk), lambda i, j, kk: (i, kk))
    if mode == "nt":
        b_spec = pl.BlockSpec((tn, tk), lambda i, j, kk: (j, kk))
    else:
        b_spec = pl.BlockSpec((tk, tn), lambda i, j, kk: (kk, j))
    dot = {"nn": _dot, "nt": _dot_nt, "tn": _dot_tn}[mode]
    use_scratch = nk > 1 and out_dtype != F32

    def body(a_ref, b_ref, o_ref, *scratch):
        if nk == 1 and mode != "tn":
            rows = min(ROW_CHUNK, tm)
            bb = b_ref[...].astype(BF16)
            for r0 in range(0, tm, rows):
                o_ref[r0:r0 + rows, :] = dot(a_ref[r0:r0 + rows, :].astype(BF16), bb).astype(out_dtype)
            return
        prod = dot(a_ref[...].astype(BF16), b_ref[...].astype(BF16))
        if nk == 1:
            o_ref[...] = prod.astype(out_dtype)
            return
        acc = scratch[0] if use_scratch else o_ref
        kk = pl.program_id(2)

        @pl.when(kk == 0)
        def _():
            acc[...] = prod

        @pl.when(kk > 0)
        def _():
            acc[...] += prod

        if use_scratch:
            @pl.when(kk == nk - 1)
            def _():
                o_ref[...] = acc[...].astype(out_dtype)

    dep_specs, dep_ops = _dep_args(dep)
    return pl.pallas_call(
        _after(body, 2, dep),
        name=name,
        out_shape=jax.ShapeDtypeStruct((m, n), out_dtype),
        grid=(m // tm, n // tn, nk),
        in_specs=[a_spec, b_spec] + dep_specs,
        out_specs=pl.BlockSpec((tm, tn), lambda i, j, kk: (i, j)),
        scratch_shapes=[pltpu.VMEM((tm, tn), F32)] if use_scratch else [],
        compiler_params=_cparams("parallel", "parallel", "arbitrary"),
    )(a, b, *dep_ops)


def _norm_fwd(name, x, g, tm):
    t, d = x.shape
    tm = min(tm, t)

    def body(x_ref, g_ref, h_ref):
        h_ref[...] = _rms_fwd(x_ref[...], g_ref[...]).astype(BF16)

    return pl.pallas_call(
        body, name=name, out_shape=jax.ShapeDtypeStruct((t, d), BF16), grid=(t // tm,),
        in_specs=[pl.BlockSpec((tm, d), lambda i: (i, 0)), pl.BlockSpec((1, d), lambda i: (0, 0))],
        out_specs=pl.BlockSpec((tm, d), lambda i: (i, 0)),
        compiler_params=_cparams("parallel"),
    )(x, g)


def _split3(x):
    hi = x.astype(BF16)
    r1 = x - hi.astype(F32)
    mid = r1.astype(BF16)
    lo = (r1 - mid.astype(F32)).astype(BF16)
    return hi, mid, lo


def _tri_dot(tri, x):
    hi, mid, lo = _split3(x)
    return _dot(tri, hi) + _dot(tri, mid) + _dot(tri, lo)


def _forget_fwd(h, wf, bf, seq):
    t, d = h.shape
    cb = min(256, seq)

    def body(h_ref, wf_ref, bf_ref, fl_ref, fc_ref):
        fl = _dot(h_ref[...], wf_ref[...])
        fl_ref[...] = fl
        xx = fl + bf_ref[...]
        lf = jnp.minimum(xx, 0.0) - jnp.log(1.0 + jnp.exp(-jnp.abs(xx)))
        ri = lax.broadcasted_iota(jnp.int32, (cb, cb), 0)
        ci = lax.broadcasted_iota(jnp.int32, (cb, cb), 1)
        tri = (ri >= ci).astype(BF16)
        carry = jnp.zeros((1, LANES), F32)
        for blk in range(seq // cb):
            cs = _tri_dot(tri, lf[blk * cb:(blk + 1) * cb]) + carry
            fc_ref[blk * cb:(blk + 1) * cb, :] = cs
            carry = cs[cb - 1:cb, :]

    return pl.pallas_call(
        body, name="forget_fwd",
        out_shape=(jax.ShapeDtypeStruct((t, LANES), F32), jax.ShapeDtypeStruct((t, LANES), F32)),
        grid=(t // seq,),
        in_specs=[pl.BlockSpec((seq, d), lambda b: (b, 0)), pl.BlockSpec((d, LANES), lambda b: (0, 0)),
                  pl.BlockSpec((1, LANES), lambda b: (0, 0))],
        out_specs=(pl.BlockSpec((seq, LANES), lambda b: (b, 0)), pl.BlockSpec((seq, LANES), lambda b: (b, 0))),
        compiler_params=_cparams("parallel"),
    )(h, wf, bf)


def _pool_fwd(u, mix, scale, seq):
    t = u.shape[0]

    def body(u_ref, mix_ref, sc_ref, p_ref, ps_ref):
        tpos = lax.broadcasted_iota(jnp.int32, (seq, POOL_GROUP_DIM), 0)
        for g in range(POOL_GROUPS):
            sl = slice(g * POOL_GROUP_DIM, (g + 1) * POOL_GROUP_DIM)
            ug = u_ref[:, sl]
            s = ug
            for lvl in range(g + 1):
                d = 2 ** lvl
                s = s + jnp.where(tpos >= d, pltpu.roll(s, d, 0), 0.0)
            cnt = jnp.minimum(tpos + 1, POOL_WINDOWS[g]).astype(F32)
            pb = (s / cnt - ug).astype(BF16)
            p_ref[:, sl] = pb
            ps_ref[:, sl] = (_dot(pb, mix_ref[g]) * sc_ref[:, sl]).astype(BF16)

    return pl.pallas_call(
        body, name="pool_fwd",
        out_shape=(jax.ShapeDtypeStruct((t, POOL_WIDTH), BF16), jax.ShapeDtypeStruct((t, POOL_WIDTH), BF16)),
        grid=(t // seq,),
        in_specs=[pl.BlockSpec((seq, POOL_WIDTH), lambda b: (b, 0)),
                  pl.BlockSpec((POOL_GROUPS, POOL_GROUP_DIM, POOL_GROUP_DIM), lambda b: (0, 0, 0)),
                  pl.BlockSpec((1, POOL_WIDTH), lambda b: (0, 0))],
        out_specs=(pl.BlockSpec((seq, POOL_WIDTH), lambda b: (b, 0)), pl.BlockSpec((seq, POOL_WIDTH), lambda b: (b, 0))),
        compiler_params=_cparams("parallel"),
    )(u, mix, scale)


def _aug_constants():
    w = N_HEADS * LANES
    rows = jnp.arange(3 * LANES)
    piece, head = rows // LANES, rows % LANES
    cols = jnp.arange(w)
    live = (head < N_HEADS)[:, None]
    pq = (live & (cols[None, :] == (head * LANES + HEAD_DIM + piece)[:, None])).astype(BF16)
    pk = -(live & (cols[None, :] == (head * LANES + HEAD_DIM + 3 + piece)[:, None])).astype(BF16)
    lane = cols % LANES
    oq = ((lane >= HEAD_DIM + 3) & (lane < HEAD_DIM + 6)).astype(F32)[None, :]
    ok = ((lane >= HEAD_DIM) & (lane < HEAD_DIM + 3)).astype(F32)[None, :]
    return pq, pk, oq, ok


def _head_blocks(wt):
    d = wt.shape[1]
    return jnp.pad(wt.reshape(N_HEADS, HEAD_DIM, d), ((0, 0), (0, LANES - HEAD_DIM), (0, 0))).reshape(N_HEADS * LANES, d)


def _attn_prep(h, wq, wk, wv, fcum, tm):
    t, d = h.shape
    tm = min(tm, t)
    rows = min(ROW_CHUNK, tm)
    w = N_HEADS * LANES
    pq, pk, oq, ok = _aug_constants()

    def body(h_ref, wq_ref, wk_ref, wv_ref, f_ref, pq_ref, pk_ref, oq_ref, ok_ref, qa_ref, ka_ref, v_ref):
        for r0 in range(0, tm, rows):
            rs = slice(r0, r0 + rows)
            hh = h_ref[rs, :]
            fs = jnp.concatenate(_split3(f_ref[rs, :]), axis=1)
            q = _dot_nt(hh, wq_ref[...]).astype(BF16).astype(F32) * ATTN_SCALE
            qa_ref[rs, :] = (q + _dot(fs, pq_ref[...]) + oq_ref[...]).astype(BF16)
            k = _dot_nt(hh, wk_ref[...]).astype(BF16).astype(F32)
            ka_ref[rs, :] = (k + _dot(fs, pk_ref[...]) + ok_ref[...]).astype(BF16)
            v_ref[rs, :] = _dot_nt(hh, wv_ref[...]).astype(BF16)

    row = lambda n: pl.BlockSpec((tm, n), lambda i: (i, 0))
    full = lambda a: pl.BlockSpec(a.shape, lambda i: (0, 0))
    return pl.pallas_call(
        body, name="attn_prep",
        out_shape=(jax.ShapeDtypeStruct((t, w), BF16), jax.ShapeDtypeStruct((t, w), BF16),
                   jax.ShapeDtypeStruct((t, ATTN_WIDTH), BF16)),
        grid=(t // tm,),
        in_specs=[row(d), full(wq), full(wk), full(wv), row(LANES), full(pq), full(pk), full(oq), full(ok)],
        out_specs=(row(w), row(w), row(ATTN_WIDTH)),
        compiler_params=_cparams("parallel"),
    )(h, wq, wk, wv, fcum, pq, pk, oq, ok)


def _fold_lanes(x, op):
    out = x[:, :LANES]
    for g in range(1, x.shape[1] // LANES):
        out = op(out, x[:, g * LANES:(g + 1) * LANES])
    return out


def _causal_sweep(i, tile, carry):
    def quad(jj, c):
        for u in range(4):
            c = tile(4 * jj + u, c, False)
        return c

    carry = lax.fori_loop(0, i // 4, quad, carry)
    base = 4 * (i // 4)
    carry = lax.cond(i % 4 >= 2, lambda c: tile(base + 1, tile(base, c, False), False), lambda c: c, carry)
    return lax.cond(i % 2 == 1, lambda c: tile(i, tile(i - 1, c, False), True), lambda c: tile(i, c, True), carry)


def _attn_fwd(qa, ka, v, seq, tq, dep=None):
    t = qa.shape[0]
    nq = seq // tq
    hp_n = N_HEADS // 2
    heads = [slice(e * LANES, (e + 1) * LANES) for e in range(2)]

    def body(q_ref, k_ref, v_ref, o_ref, lse_ref, s_buf):
        i = pl.program_id(2)
        diag_ok = lax.broadcasted_iota(jnp.int32, (tq, tq), 0) >= lax.broadcasted_iota(jnp.int32, (tq, tq), 1)
        qs = [q_ref[:, hl] for hl in heads]

        def sweep1(j, mxs, diagonal):
            r0 = pl.multiple_of(j * tq, tq)
            out = []
            for e, hl in enumerate(heads):
                s = _dot_nt(qs[e], k_ref[pl.ds(r0, tq), hl])
                if diagonal:
                    s = jnp.where(diag_ok, s, NEG_BIG)
                s_buf[e, j] = s
                out.append(jnp.maximum(mxs[e], _fold_lanes(s, jnp.maximum)))
            return tuple(out)

        mxs = _causal_sweep(i, sweep1, (jnp.full((tq, LANES), NEG_BIG, F32),) * 2)
        ms = [jnp.max(mx, axis=1, keepdims=True) for mx in mxs]

        def sweep2(j, carry, diagonal):
            r0 = pl.multiple_of(j * tq, tq)
            vv = v_ref[pl.ds(r0, tq), :]
            out = []
            for e in range(2):
                p = jnp.exp(s_buf[e, j] - ms[e])
                out += [carry[2 * e] + _fold_lanes(p, jnp.add), carry[2 * e + 1] + _dot(p.astype(BF16), vv)]
            return tuple(out)

        res = _causal_sweep(i, sweep2, (jnp.zeros((tq, LANES), F32),) * 4)
        outs = []
        for e in range(2):
            l = jnp.sum(res[2 * e], axis=1, keepdims=True)
            outs.append(res[2 * e + 1] / l)
            lse_ref[:, e:e + 1] = ms[e] + jnp.log(l)
        lane = lax.broadcasted_iota(jnp.int32, (tq, LANES), 1)
        o_ref[...] = jnp.where(lane < HEAD_DIM, outs[0], outs[1])

    dep_specs, dep_ops = _dep_args(dep)
    return pl.pallas_call(
        _after(body, 3, dep), name="attn_fwd",
        out_shape=(jax.ShapeDtypeStruct((t, ATTN_WIDTH), F32), jax.ShapeDtypeStruct((hp_n, t, 2), F32)),
        grid=(t // seq, hp_n, nq),
        in_specs=[pl.BlockSpec((tq, 2 * LANES), lambda b, hp, i: (b * nq + i, hp)),
                  pl.BlockSpec((seq, 2 * LANES), lambda b, hp, i: (b, hp)),
                  pl.BlockSpec((seq, LANES), lambda b, hp, i: (b, hp))] + dep_specs,
        out_specs=(pl.BlockSpec((tq, LANES), lambda b, hp, i: (b * nq + i, hp)),
                   pl.BlockSpec((None, tq, 2), lambda b, hp, i: (hp, b * nq + i, 0))),
        scratch_shapes=[pltpu.VMEM((2, nq, tq, tq), F32)],
        compiler_params=_cparams("parallel", "parallel", "arbitrary"),
    )(qa, ka, v, *dep_ops)


def _merge_fwd(x, ps, o, g2, wpo, wao, wout, tm):
    t, d = x.shape
    tm = min(tm, t)
    rows = min(ROW_CHUNK, tm)

    def body(x_ref, ps_ref, o_ref, gp_ref, ga_ref, wpo_ref, wao_ref, wout_ref, mg_ref, x1_ref):
        for r0 in range(0, tm, rows):
            rs = slice(r0, r0 + rows)
            py = _dot(ps_ref[rs, :], wpo_ref[...])
            ay = _dot(o_ref[rs, :].astype(BF16), wao_ref[...])
            mb = (_sigmoid(gp_ref[rs, :].astype(F32)) * py + _sigmoid(ga_ref[rs, :].astype(F32)) * ay).astype(BF16)
            mg_ref[rs, :] = mb
            x1_ref[rs, :] = x_ref[rs, :] + _dot(mb, wout_ref[...])

    row = lambda w: pl.BlockSpec((tm, w), lambda i: (i, 0))
    full = lambda a: pl.BlockSpec(a.shape, lambda i: (0, 0))
    return pl.pallas_call(
        body, name="merge_fwd",
        out_shape=(jax.ShapeDtypeStruct((t, d), BF16), jax.ShapeDtypeStruct((t, d), F32)),
        grid=(t // tm,),
        in_specs=[row(d), row(POOL_WIDTH), row(ATTN_WIDTH), pl.BlockSpec((tm, d), lambda i: (i, 0)),
                  pl.BlockSpec((tm, d), lambda i: (i, 1)), full(wpo), full(wao), full(wout)],
        out_specs=(row(d), row(d)),
        compiler_params=_cparams("parallel"),
    )(x, ps, o, g2, g2, wpo, wao, wout)


def _ffn_fwd(x1, g, wg, wu, wd, tm, tf):
    t, d = x1.shape
    f = wg.shape[0]
    tm = min(tm, t)
    nf = f // tf
    rows = min(512, tm)

    def body(x1_ref, g_ref, wg_ref, wu_ref, wd_ref, h2_ref, gt_ref, up_ref, act_ref, x2_ref):
        j = pl.program_id(1)

        @pl.when(j == 0)
        def _():
            h2_ref[...] = _rms_fwd(x1_ref[...], g_ref[...]).astype(BF16)

            x2_ref[...] = x1_ref[...]

        for r0 in range(0, tm, rows):
            rs = slice(r0, r0 + rows)
            h2 = h2_ref[rs, :]
            gt = _dot_nt(h2, wg_ref[...])
            up = _dot_nt(h2, wu_ref[...])
            sg = _sigmoid(gt)
            silu = gt * sg
            act = (silu * up).astype(BF16)
            gt_ref[rs, :] = (up * (sg * (1.0 + gt * (1.0 - sg)))).astype(BF16)
            up_ref[rs, :] = silu.astype(BF16)
            act_ref[rs, :] = act
            x2_ref[rs, :] += _dot(act, wd_ref[...])

    return pl.pallas_call(
        body, name="ffn_fwd",
        out_shape=(jax.ShapeDtypeStruct((t, d), BF16), jax.ShapeDtypeStruct((t, f), BF16),
                   jax.ShapeDtypeStruct((t, f), BF16), jax.ShapeDtypeStruct((t, f), BF16),
                   jax.ShapeDtypeStruct((t, d), F32)),
        grid=(t // tm, nf),
        in_specs=[pl.BlockSpec((tm, d), lambda i, j: (i, 0)), pl.BlockSpec((1, d), lambda i, j: (0, 0)),
                  pl.BlockSpec((tf, d), lambda i, j: (j, 0)), pl.BlockSpec((tf, d), lambda i, j: (j, 0)),
                  pl.BlockSpec((tf, d), lambda i, j: (j, 0))],
        out_specs=(pl.BlockSpec((tm, d), lambda i, j: (i, 0)), pl.BlockSpec((tm, tf), lambda i, j: (i, j)),
                   pl.BlockSpec((tm, tf), lambda i, j: (i, j)), pl.BlockSpec((tm, tf), lambda i, j: (i, j)),
                   pl.BlockSpec((tm, d), lambda i, j: (i, 0))),
        compiler_params=_cparams("parallel", "arbitrary"),
    )(x1, g, wg, wu, wd)


def _final_fwd_bwd(x2, target, g, tm):
    t, d = x2.shape
    tm = min(tm, t)

    def body(x_ref, t_ref, g_ref, loss_ref, dx_ref, dg_ref):
        i = pl.program_id(0)
        x = x_ref[...]
        gg = g_ref[...]
        err = _rms_fwd(x, gg) - t_ref[...]
        part = 0.5 * jnp.sum(jnp.mean(err * err, axis=-1, keepdims=True), axis=0, keepdims=True)
        dx, dg = _rms_bwd(x, gg, err * (1.0 / d))
        dx_ref[...] = dx

        @pl.when(i == 0)
        def _():
            loss_ref[...] = jnp.zeros_like(loss_ref)
            dg_ref[...] = jnp.zeros_like(dg_ref)

        loss_ref[...] += jnp.broadcast_to(part, loss_ref.shape)
        dg_ref[...] += dg

    return pl.pallas_call(
        body, name="final_fwd_bwd",
        out_shape=(jax.ShapeDtypeStruct((1, LANES), F32), jax.ShapeDtypeStruct((t, d), F32),
                   jax.ShapeDtypeStruct((1, d), F32)),
        grid=(t // tm,),
        in_specs=[pl.BlockSpec((tm, d), lambda i: (i, 0)), pl.BlockSpec((tm, d), lambda i: (i, 0)),
                  pl.BlockSpec((1, d), lambda i: (0, 0))],
        out_specs=(pl.BlockSpec((1, LANES), lambda i: (0, 0)), pl.BlockSpec((tm, d), lambda i: (i, 0)),
                   pl.BlockSpec((1, d), lambda i: (0, 0))),
        compiler_params=_cparams("arbitrary"),
    )(x2, target, g)


def _ffn_bwd(dx2, x1, g, gt, up, wg, wu, wd, tm, tf):
    t, d = dx2.shape
    f = gt.shape[1]
    tm = min(tm, t)
    nf = f // tf
    wgu = jnp.concatenate([wg.reshape(nf, tf, d), wu.reshape(nf, tf, d)], axis=1).reshape(2 * f, d)
    rows = min(256, tm)

    def body(dx2_ref, x1_ref, g_ref, gt_ref, up_ref, wgu_ref, wd_ref, dgt_ref, dup_ref, dx1_ref, dg_ref, acc_ref,
             dxb_ref):
        i, j = pl.program_id(0), pl.program_id(1)

        @pl.when(j == 0)
        def _():
            dxb_ref[...] = dx2_ref[...].astype(BF16)
            acc_ref[...] = jnp.zeros_like(acc_ref)

        for r0 in range(0, tm, rows):
            rs = slice(r0, r0 + rows)
            dact = _dot_nt(dxb_ref[rs, :], wd_ref[...])
            dgt = (dact * gt_ref[rs, :].astype(F32)).astype(BF16)
            dup = (dact * up_ref[rs, :].astype(F32)).astype(BF16)
            dgt_ref[rs, :] = dgt
            dup_ref[rs, :] = dup
            acc_ref[rs, :] += _dot(jnp.concatenate([dgt, dup], axis=1), wgu_ref[...])

        @pl.when(jnp.logical_and(i == 0, j == 0))
        def _():
            dg_ref[...] = jnp.zeros_like(dg_ref)

        @pl.when(j == nf - 1)
        def _():
            dxn, dg = _rms_bwd(x1_ref[...], g_ref[...], acc_ref[...])
            dx1_ref[...] = dx2_ref[...] + dxn
            dg_ref[...] += dg

    return pl.pallas_call(
        body, name="ffn_bwd",
        out_shape=(jax.ShapeDtypeStruct((t, f), BF16), jax.ShapeDtypeStruct((t, f), BF16),
                   jax.ShapeDtypeStruct((t, d), F32), jax.ShapeDtypeStruct((1, d), F32)),
        grid=(t // tm, nf),
        in_specs=[pl.BlockSpec((tm, d), lambda i, j: (i, 0)), pl.BlockSpec((tm, d), lambda i, j: (i, 0)),
                  pl.BlockSpec((1, d), lambda i, j: (0, 0)),
                  pl.BlockSpec((tm, tf), lambda i, j: (i, j)), pl.BlockSpec((tm, tf), lambda i, j: (i, j)),
                  pl.BlockSpec((2 * tf, d), lambda i, j: (j, 0)), pl.BlockSpec((tf, d), lambda i, j: (j, 0))],
        out_specs=(pl.BlockSpec((tm, tf), lambda i, j: (i, j)), pl.BlockSpec((tm, tf), lambda i, j: (i, j)),
                   pl.BlockSpec((tm, d), lambda i, j: (i, 0)), pl.BlockSpec((1, d), lambda i, j: (0, 0))),
        scratch_shapes=[pltpu.VMEM((tm, d), F32), pltpu.VMEM((tm, d), BF16)],
        compiler_params=_cparams("arbitrary", "arbitrary"),
    )(dx2, x1, g, gt, up, wgu, wd)


def _merge_bwd(dx1, ps, o, g2, wpo, wao, wout, tm, dep=None):
    t, d = dx1.shape
    tm = min(tm, t)
    rows = min(ROW_CHUNK, tm)

    def body(dx1_ref, ps_ref, o_ref, gp_ref, ga_ref, wpo_ref, wao_ref, wout_ref, dpy_ref, day_ref, dg2_ref, dps_ref, da_ref):
        for r0 in range(0, tm, rows):
            rs = slice(r0, r0 + rows)
            dm = _dot_nt(dx1_ref[rs, :].astype(BF16), wout_ref[...])
            py = _dot(ps_ref[rs, :], wpo_ref[...])
            ay = _dot(o_ref[rs, :].astype(BF16), wao_ref[...])
            sp = _sigmoid(gp_ref[rs, :].astype(F32))
            sa = _sigmoid(ga_ref[rs, :].astype(F32))
            dpy = (dm * sp).astype(BF16)
            day = (dm * sa).astype(BF16)
            dpy_ref[rs, :] = dpy
            day_ref[rs, :] = day
            dg2_ref[rs, :d] = (dm * py * (sp * (1.0 - sp))).astype(BF16)
            dg2_ref[rs, d:] = (dm * ay * (sa * (1.0 - sa))).astype(BF16)
            dps_ref[rs, :] = _dot_nt(dpy, wpo_ref[...])
            da_ref[rs, :] = _dot_nt(day, wao_ref[...]).astype(BF16)

    row = lambda w: pl.BlockSpec((tm, w), lambda i: (i, 0))
    full = lambda a: pl.BlockSpec(a.shape, lambda i: (0, 0))
    dep_specs, dep_ops = _dep_args(dep)
    return pl.pallas_call(
        _after(body, 8, dep), name="merge_bwd",
        out_shape=(jax.ShapeDtypeStruct((t, d), BF16), jax.ShapeDtypeStruct((t, d), BF16),
                   jax.ShapeDtypeStruct((t, 2 * d), BF16), jax.ShapeDtypeStruct((t, POOL_WIDTH), F32),
                   jax.ShapeDtypeStruct((t, ATTN_WIDTH), BF16)),
        grid=(t // tm,),
        in_specs=[row(d), row(POOL_WIDTH), row(ATTN_WIDTH), pl.BlockSpec((tm, d), lambda i: (i, 0)),
                  pl.BlockSpec((tm, d), lambda i: (i, 1)), full(wpo), full(wao), full(wout)] + dep_specs,
        out_specs=(row(d), row(d), row(2 * d), row(POOL_WIDTH), row(ATTN_WIDTH)),
        compiler_params=_cparams("parallel"),
    )(dx1, ps, o, g2, g2, wpo, wao, wout, *dep_ops)


def _attn_bwd(qa, ka, v, do, lse4, seq, tq, dep=None):
    t = qa.shape[0]
    nq = seq // tq
    hp_n = N_HEADS // 2
    heads = [slice(e * LANES, (e + 1) * LANES) for e in range(2)]

    def body(q_ref, k_ref, v_ref, do_ref, lse_ref, dq_ref, dk_ref, dv_ref, dfr_ref, dk_acc, dv_acc, p_buf, dp_buf):
        diag_ok = lax.broadcasted_iota(jnp.int32, (tq, tq), 0) >= lax.broadcasted_iota(jnp.int32, (tq, tq), 1)
        lane_q = lax.broadcasted_iota(jnp.int32, (tq, LANES), 1)
        mine_q = [lane_q < HEAD_DIM, lane_q >= HEAD_DIM]
        dv_acc[...] = jnp.zeros_like(dv_acc)
        dk_acc[...] = jnp.zeros_like(dk_acc)
        dfr_ref[...] = jnp.zeros_like(dfr_ref)
        transposed = lambda a: a.astype(F32).T.astype(BF16)

        def q_step(i, _):
            q0 = pl.multiple_of(i * tq, tq)
            qs = [q_ref[pl.ds(q0, tq), hl] for hl in heads]
            dov = do_ref[pl.ds(q0, tq), :]
            dos = [jnp.where(mq, dov, jnp.zeros((), BF16)) for mq in mine_q]
            qts = [transposed(q) for q in qs]
            dots = [transposed(a) for a in dos]
            lss = [lse_ref[pl.ds(q0, tq), e:e + 1] for e in range(2)]

            def sweep1(j, dls, diagonal):
                r0 = pl.multiple_of(j * tq, tq)
                vv = v_ref[pl.ds(r0, tq), :]
                out = []
                for e, hl in enumerate(heads):
                    s = _dot_nt(qs[e], k_ref[pl.ds(r0, tq), hl])
                    if diagonal:
                        s = jnp.where(diag_ok, s, NEG_BIG)
                    p = jnp.exp(s - lss[e])
                    dp = _dot_nt(dos[e], vv)
                    p_buf[e, j] = p
                    dp_buf[e, j] = dp
                    dv_acc[j] += _dot(dots[e], p.astype(BF16))
                    out.append(dls[e] + _fold_lanes(p * dp, jnp.add))
                return tuple(out)

            dls = _causal_sweep(i, sweep1, (jnp.zeros((tq, LANES), F32),) * 2)
            dls = [jnp.sum(d, axis=1, keepdims=True) for d in dls]

            def sweep2(j, dqs, diagonal):
                r0 = pl.multiple_of(j * tq, tq)
                out = []
                for e, hl in enumerate(heads):
                    ds = p_buf[e, j] * (dp_buf[e, j] - dls[e])
                    dfr_ref[e, pl.ds(j, 1), :] += jnp.sum(ds, axis=0, keepdims=True)
                    dsb = ds.astype(BF16)
                    dk_acc[e, j] += _dot(qts[e], dsb)
                    out.append(dqs[e] + _dot(dsb, k_ref[pl.ds(r0, tq), hl]))
                return tuple(out)

            dqs = _causal_sweep(i, sweep2, (jnp.zeros((tq, LANES), F32),) * 2)
            dq = jnp.where(mine_q[0], dqs[0], pltpu.roll(dqs[1], HEAD_DIM, 1)) * ATTN_SCALE
            dq_ref[pl.ds(q0, tq), :] = dq.astype(BF16)
            return 0

        lax.fori_loop(0, nq, q_step, 0)
        for j in range(nq):
            rs = slice(j * tq, (j + 1) * tq)
            dk = jnp.where(mine_q[0], dk_acc[0, j].T, pltpu.roll(dk_acc[1, j].T, HEAD_DIM, 1))
            dk_ref[rs, :] = dk.astype(BF16)
            dv_ref[rs, :] = dv_acc[j].T.astype(BF16)

    wide = pl.BlockSpec((seq, 2 * LANES), lambda b, hp: (b, hp))
    col = pl.BlockSpec((seq, LANES), lambda b, hp: (b, hp))
    pair = pl.BlockSpec((None, seq, 2), lambda b, hp: (hp, b, 0))
    dep_specs, dep_ops = _dep_args(dep)
    return pl.pallas_call(
        _after(body, 5, dep), name="attn_bwd",
        out_shape=(jax.ShapeDtypeStruct((t, ATTN_WIDTH), BF16),) * 3 + (jax.ShapeDtypeStruct((N_HEADS, t // tq, tq), F32),),
        grid=(t // seq, hp_n),
        in_specs=[wide, wide, col, col, pair] + dep_specs,
        out_specs=(col, col, col, pl.BlockSpec((2, nq, tq), lambda b, hp: (hp, b, 0))),
        scratch_shapes=[pltpu.VMEM((2, nq, LANES, tq), F32), pltpu.VMEM((nq, LANES, tq), F32),
                        pltpu.VMEM((2, nq, tq, tq), F32), pltpu.VMEM((2, nq, tq, tq), F32)],
        compiler_params=_cparams("parallel", "arbitrary"),
    )(qa, ka, v, do, lse4, *dep_ops)


def _forget_bwd(dfc, fl, bf, seq):
    t = fl.shape[0]
    cb = min(256, seq)
    nb = seq // cb

    def body(dfc_ref, fl_ref, bf_ref, dfl_ref, db_ref):
        b = pl.program_id(0)
        ri = lax.broadcasted_iota(jnp.int32, (cb, cb), 0)
        ci = lax.broadcasted_iota(jnp.int32, (cb, cb), 1)
        tri = (ci >= ri).astype(BF16)
        carry = jnp.zeros((1, LANES), F32)
        dbs = jnp.zeros((1, LANES), F32)
        for blk in reversed(range(nb)):
            rs = slice(blk * cb, (blk + 1) * cb)
            dlf = _tri_dot(tri, -dfc_ref[rs, :]) + carry
            carry = dlf[0:1, :]
            dfl = dlf * _sigmoid(-(fl_ref[rs, :] + bf_ref[...]))
            dfl_ref[rs, :] = dfl.astype(BF16)
            dbs = dbs + jnp.sum(dfl, axis=0, keepdims=True)

        @pl.when(b == 0)
        def _():
            db_ref[...] = jnp.zeros_like(db_ref)

        db_ref[...] += dbs

    return pl.pallas_call(
        body, name="forget_bwd",
        out_shape=(jax.ShapeDtypeStruct((t, LANES), BF16), jax.ShapeDtypeStruct((1, LANES), F32)),
        grid=(t // seq,),
        in_specs=[pl.BlockSpec((seq, LANES), lambda b: (b, 0)), pl.BlockSpec((seq, LANES), lambda b: (b, 0)),
                  pl.BlockSpec((1, LANES), lambda b: (0, 0))],
        out_specs=(pl.BlockSpec((seq, LANES), lambda b: (b, 0)), pl.BlockSpec((1, LANES), lambda b: (0, 0))),
        compiler_params=_cparams("arbitrary"),
    )(dfc, fl, bf)


def _pool_bwd(dps, p, mix, scale, seq):
    t = dps.shape[0]

    def body(dps_ref, p_ref, mix_ref, sc_ref, du_ref, dmix_ref, dsc_ref):
        b = pl.program_id(0)

        @pl.when(b == 0)
        def _():
            dmix_ref[...] = jnp.zeros_like(dmix_ref)
            dsc_ref[...] = jnp.zeros_like(dsc_ref)

        tpos = lax.broadcasted_iota(jnp.int32, (seq, POOL_GROUP_DIM), 0)
        for g in range(POOL_GROUPS):
            sl = slice(g * POOL_GROUP_DIM, (g + 1) * POOL_GROUP_DIM)
            pb = p_ref[:, sl]
            dpsg = dps_ref[:, sl]
            pm = _dot(pb, mix_ref[g])
            dsc_ref[:, sl] += jnp.sum(dpsg * pm, axis=0, keepdims=True)
            dpm = (dpsg * sc_ref[:, sl]).astype(BF16)
            dmix_ref[g] += _dot_tn(pb, dpm)
            dp = _dot_nt(dpm, mix_ref[g])
            cnt = jnp.minimum(tpos + 1, POOL_WINDOWS[g]).astype(F32)
            s = dp / cnt
            for lvl in range(g + 1):
                d = 2 ** lvl
                s = s + jnp.where(tpos < seq - d, pltpu.roll(s, seq - d, 0), 0.0)
            du_ref[:, sl] = (s - dp).astype(BF16)

    return pl.pallas_call(
        body, name="pool_bwd",
        out_shape=(jax.ShapeDtypeStruct((t, POOL_WIDTH), BF16),
                   jax.ShapeDtypeStruct((POOL_GROUPS, POOL_GROUP_DIM, POOL_GROUP_DIM), F32),
                   jax.ShapeDtypeStruct((1, POOL_WIDTH), F32)),
        grid=(t // seq,),
        in_specs=[pl.BlockSpec((seq, POOL_WIDTH), lambda b: (b, 0)), pl.BlockSpec((seq, POOL_WIDTH), lambda b: (b, 0)),
                  pl.BlockSpec((POOL_GROUPS, POOL_GROUP_DIM, POOL_GROUP_DIM), lambda b: (0, 0, 0)),
                  pl.BlockSpec((1, POOL_WIDTH), lambda b: (0, 0))],
        out_specs=(pl.BlockSpec((seq, POOL_WIDTH), lambda b: (b, 0)),
                   pl.BlockSpec((POOL_GROUPS, POOL_GROUP_DIM, POOL_GROUP_DIM), lambda b: (0, 0, 0)),
                   pl.BlockSpec((1, POOL_WIDTH), lambda b: (0, 0))),
        compiler_params=_cparams("arbitrary"),
    )(dps, p, mix, scale)


def _in_bwd(du, dq, dk, dv, dg2, dfl, dx1, x, g, wu, wqkv, wg2, wft, tm):
    t, d = x.shape
    tm = min(tm, t)
    rows = min(ROW_CHUNK, tm)
    aw = ATTN_WIDTH

    def body(du_ref, dq_ref, dk_ref, dv_ref, dg2_ref, dfl_ref, dx1_ref, x_ref, g_ref, wu_ref, wqkv_ref, wg2_ref, wft_ref,
             dx_ref, dg_ref):
        i = pl.program_id(0)

        @pl.when(i == 0)
        def _():
            dg_ref[...] = jnp.zeros_like(dg_ref)

        for r0 in range(0, tm, rows):
            rs = slice(r0, r0 + rows)
            dh = _dot(du_ref[rs, :], wu_ref[...])
            dh += _dot(dq_ref[rs, :], wqkv_ref[0:aw, :])
            dh += _dot(dk_ref[rs, :], wqkv_ref[aw:2 * aw, :])
            dh += _dot(dv_ref[rs, :], wqkv_ref[2 * aw:3 * aw, :])
            dh += _dot(dg2_ref[rs, :], wg2_ref[...])
            dh += _dot(dfl_ref[rs, :], wft_ref[...])
            dxn, dg = _rms_bwd(x_ref[rs, :], g_ref[...], dh)
            dx_ref[rs, :] = dx1_ref[rs, :] + dxn
            dg_ref[...] += dg

    row = lambda w: pl.BlockSpec((tm, w), lambda i: (i, 0))
    full = lambda a: pl.BlockSpec(a.shape, lambda i: (0, 0))
    return pl.pallas_call(
        body, name="in_bwd",
        out_shape=(jax.ShapeDtypeStruct((t, d), F32), jax.ShapeDtypeStruct((1, d), F32)),
        grid=(t // tm,),
        in_specs=[row(POOL_WIDTH), row(aw), row(aw), row(aw), row(2 * d), row(LANES), row(d), row(d),
                  pl.BlockSpec((1, d), lambda i: (0, 0)), full(wu), full(wqkv), full(wg2), full(wft)],
        out_specs=(row(d), pl.BlockSpec((1, d), lambda i: (0, 0))),
        compiler_params=_cparams("arbitrary"),
    )(du, dq, dk, dv, dg2, dfl, dx1, x, g, wu, wqkv, wg2, wft)


def _position():
    return lax.axis_index("x"), lax.axis_index("y"), lax.axis_index("c")


def _remote(src, dst, send_sem, recv_sem, device):
    return pltpu.make_async_remote_copy(src_ref=src, dst_ref=dst, send_sem=send_sem, recv_sem=recv_sem,
                                        device_id=device, device_id_type=MESH)


HBM = pl.BlockSpec(memory_space=pltpu.HBM)
SEM = pl.BlockSpec(memory_space=pltpu.SEMAPHORE)
DATAFLOW = pltpu.SideEffectType.DATAFLOW_SIDE_EFFECTING


def _copies_start(name, arrays, plan, m, dep=None):
    n = len(arrays)
    arrays = [pltpu.with_memory_space_constraint(a, pltpu.HBM) for a in arrays]

    def body(*refs):
        ins, send_sem, recv_sem, token = refs[:n], refs[n], refs[n + 1], refs[2 * n + 2]
        for i, (src, dst, device, _) in enumerate(plan(ins, *_position())):
            _remote(src, dst, send_sem.at[i], recv_sem.at[i], device).start()
        token[...] = jnp.zeros_like(token)

    dep_specs, dep_ops = _dep_args(dep)
    outs = pl.pallas_call(
        _after(body, n, dep), name=name,
        out_shape=(pltpu.SemaphoreType.DMA((m,)), pltpu.SemaphoreType.DMA((m,)),
                   *[pltpu.HBM(a.shape, a.dtype) for a in arrays], jax.ShapeDtypeStruct((8, LANES), F32)),
        in_specs=[HBM] * n + dep_specs, out_specs=(SEM, SEM, *[HBM] * n, pl.BlockSpec(memory_space=pltpu.VMEM)),
        input_output_aliases={i: i + 2 for i in range(n)},
        compiler_params=pltpu.CompilerParams(has_side_effects=DATAFLOW),
    )(*arrays, *dep_ops)
    return (outs[0], outs[1]), list(outs[2:2 + n]), outs[2 + n]


def _copies_wait(name, sems, arrays, plan, after):
    n = len(arrays)
    afters = list(after) if isinstance(after, (list, tuple)) else [after]

    def body(*refs):
        ins, send_sem, recv_sem = refs[:n], refs[n], refs[n + 1]
        for i, (src, dst, device, landing) in enumerate(plan(ins, *_position())):
            _remote(src, dst, send_sem.at[i], recv_sem.at[i], device).wait_send()
            _remote(landing, landing, send_sem.at[i], recv_sem.at[i], device).wait_recv()

    outs = pl.pallas_call(
        body, name=name,
        out_shape=tuple(pltpu.HBM(a.shape, a.dtype) for a in arrays),
        in_specs=[HBM] * n + [SEM, SEM] + [ANY] * len(afters), out_specs=tuple([HBM] * n),
        input_output_aliases={i: i for i in range(n)},
        compiler_params=pltpu.CompilerParams(has_side_effects=DATAFLOW),
    )(*arrays, sems[0], sems[1], *afters)
    return list(outs)


def _tie(x, dep):
    for token in _dep_list(dep):
        x = x + token[0, 0]
    return x


def _other_chips(x, y):
    return [(1 - x, y), (x, 1 - y), (1 - x, 1 - y)]


def _gather_begin(tag, shards, token, column_halves=False):
    n = len(shards)
    lands = [lax.empty((N_CHIPS,) + s.shape, s.dtype) for s in shards]
    if column_halves:
        cols = lambda ref, h: pl.ds(pl.multiple_of(h * (ref.shape[-1] // 2), LANES), ref.shape[-1] // 2)
        mine = lambda ref, h: ref.at[:, cols(ref, h)]
        landed = lambda ref, chip, h: ref.at[chip, :, cols(ref, h)]
    else:
        mine = lambda ref, h: ref.at[h]
        landed = lambda ref, chip, h: ref.at[chip, h]

    def plan(refs, x, y, c):
        return [(mine(refs[k], c), landed(refs[n + k], 2 * x + y, c), (ox, oy, c), landed(refs[n + k], 2 * ox + oy, c))
                for k in range(n) for ox, oy in _other_chips(x, y)]

    sems, thru, token = _copies_start(f"gather_{tag}_ici_start", list(shards) + lands, plan, 3 * n, dep=token)
    return dict(tag=tag, n=n, plan=plan, sems=sems, arrays=thru, token=token, landed=landed)


def _gather_forward(st, after):
    n, tag, landed = st["n"], st["tag"], st["landed"]
    thru = _copies_wait(f"gather_{tag}_ici_wait", st["sems"], st["arrays"], st["plan"], after)

    def plan(refs, x, y, c):
        return [(landed(refs[k], 2 * ox + oy, c), landed(refs[k], 2 * ox + oy, c), (x, y, 1 - c),
                 landed(refs[k], 2 * ox + oy, 1 - c))
                for k in range(n) for ox, oy in _other_chips(x, y)]

    sems, lands, token = _copies_start(f"gather_{tag}_fwd_start", thru[n:], plan, 3 * n)
    return dict(tag=tag, n=n, plan=plan, sems=sems, arrays=lands, token=token, shards=thru[:n])


def _gather_end(st, after, merge=True):
    lands = _copies_wait(f"gather_{st['tag']}_fwd_wait", st["sems"], st["arrays"], st["plan"], after)
    if not merge:
        return lands, st["shards"]
    me = 2 * lax.axis_index("x") + lax.axis_index("y")
    return [lax.dynamic_update_index_in_dim(g, s, me, 0) for g, s in zip(lands, st["shards"])]


def _add_keep_give(name, pos, a, a_keep, a_give, b, b_keep, b_give, steps):
    r, c = b.shape[-2:]

    def spec(arr, fn):
        lead = arr.ndim - 2

        def index(i, p):
            idx = tuple(fn(i, p))
            return idx if len(idx) == arr.ndim else idx + (0, 0)

        return pl.BlockSpec((None,) * lead + (r, c), index)

    out_spec = pl.BlockSpec((None, r, c), lambda i, p: (i, 0, 0))

    def body(p_ref, ak_ref, bk_ref, ag_ref, bg_ref, keep_ref, give_ref):
        keep_ref[...] = ak_ref[...] + bk_ref[...].astype(F32)
        give_ref[...] = (ag_ref[...] + bg_ref[...].astype(F32)).astype(BF16)

    return pl.pallas_call(
        body, name=name,
        out_shape=(jax.ShapeDtypeStruct((steps, r, c), F32), jax.ShapeDtypeStruct((steps, r, c), BF16)),
        grid_spec=pltpu.PrefetchScalarGridSpec(
            num_scalar_prefetch=1, grid=(steps,),
            in_specs=[spec(a, a_keep), spec(b, b_keep), spec(a, a_give), spec(b, b_give)],
            out_specs=(out_spec, out_spec)),
        compiler_params=_cparams("parallel"),
    )(pos, a, b, a, b)


def _add_last(name, a, b):
    _, r, c = a.shape
    blk = pl.BlockSpec((None, r, c), lambda i: (0, 0, 0))

    def body(a_ref, b_ref, o_ref):
        o_ref[...] = a_ref[...] + b_ref[...].astype(F32)

    return pl.pallas_call(
        body, name=name, out_shape=jax.ShapeDtypeStruct((r, c), F32), grid=(1,), in_specs=[blk, blk],
        out_specs=pl.BlockSpec((r, c), lambda i: (0, 0)), compiler_params=_cparams("arbitrary"),
    )(a, b)


def _exchange_part(gives, lands, peer_fn):
    n = len(gives)

    def plan(refs, x, y, c):
        return [(refs[k], refs[n + k], peer_fn(x, y, c), refs[n + k]) for k in range(n)]

    return list(gives) + list(lands), plan, n


def _join_parts(parts):
    offsets, total = [], 0
    for arrays, _, _ in parts:
        offsets.append(total)
        total += len(arrays)

    def plan(refs, x, y, c):
        copies = []
        for (arrays, part_plan, _), off in zip(parts, offsets):
            copies += part_plan(refs[off:off + len(arrays)], x, y, c)
        return copies

    return [a for arrays, _, _ in parts for a in arrays], plan, sum(m for _, _, m in parts)


def _reduce_begin(tag, grads, column_halves=False):
    n = len(grads)
    if column_halves:
        half = lambda ref, j, h: ref.at[j, :, pl.ds(pl.multiple_of(h * (ref.shape[2] // 2), LANES), ref.shape[2] // 2)]
        lands = [lax.empty((N_CHIPS, g.shape[1], g.shape[2] // 2), F32) for g in grads]
    else:
        half = lambda ref, j, h: ref.at[j, h]
        lands = [lax.empty((N_CHIPS,) + g.shape[2:], F32) for g in grads]

    def plan(refs, x, y, c):
        return [(half(refs[k], j, 1 - c), refs[n + k].at[j], (x, y, 1 - c), refs[n + k].at[j])
                for k in range(n) for j in range(N_CHIPS)]

    return dict(tag=tag, n=n, stage="c", grads=list(grads), column_halves=column_halves,
                part=(list(grads) + lands, plan, N_CHIPS * n))


def _reduce_next(st, thru):
    tag, n, stage = st["tag"], st["n"], st["stage"]
    first, recv = thru[:n], thru[n:]
    x, y, c = _position()
    if stage == "c":
        pos = jnp.stack([c, x]).astype(jnp.int32)
        if st["column_halves"]:
            mine = lambda chip: (lambda i, p: (chip(p) + i, 0, p[0]))
        else:
            mine = lambda chip: (lambda i, p: (chip(p) + i, p[0]))
        sums = [_add_keep_give(
            f"rs{tag}_c_add{k}", pos,
            first[k], mine(lambda p: 2 * p[1]), mine(lambda p: 2 * (1 - p[1])),
            recv[k], lambda i, p: (2 * p[1] + i,), lambda i, p: (2 * (1 - p[1]) + i,), 2) for k in range(n)]
        lands = [lax.empty(s[1].shape, BF16) for s in sums]
        return dict(tag=tag, n=n, stage="x", keep=[s[0] for s in sums],
                    part=_exchange_part([s[1] for s in sums], lands, lambda x, y, c: (1 - x, y, c)))
    if stage == "x":
        pos = jnp.stack([y]).astype(jnp.int32)
        sums = [_add_keep_give(
            f"rs{tag}_x_add{k}", pos,
            st["keep"][k], lambda i, p: (p[0],), lambda i, p: (1 - p[0],),
            recv[k], lambda i, p: (p[0],), lambda i, p: (1 - p[0],), 1) for k in range(n)]
        lands = [lax.empty(s[1].shape, BF16) for s in sums]
        return dict(tag=tag, n=n, stage="y", keep=[s[0] for s in sums],
                    part=_exchange_part([s[1] for s in sums], lands, lambda x, y, c: (x, 1 - y, c)))
    if stage == "y":
        mine = [_add_last(f"rs{tag}_y_add{k}", st["keep"][k], recv[k]) for k in range(n)]
        lands = [lax.empty(m.shape, F32) for m in mine]
        return dict(tag=tag, n=n, stage="swap", part=_exchange_part(mine, lands, lambda x, y, c: (x, y, 1 - c)))
    return dict(tag=tag, done=list(zip(first, recv)))


def _small_begin(tag, v):
    land = lax.empty((N_DEV,) + v.shape, F32)
    flips = [(fx, fy, fc) for fx in (0, 1) for fy in (0, 1) for fc in (0, 1)][1:]

    def plan(refs, x, y, c):
        copies = []
        for fx, fy, fc in flips:
            px, py, pc = (1 - x if fx else x), (1 - y if fy else y), (1 - c if fc else c)
            copies.append((refs[0], refs[1].at[4 * x + 2 * y + c], (px, py, pc), refs[1].at[4 * px + 2 * py + pc]))
        return copies

    return dict(tag=tag, n=1, stage="swap", grads=[v], part=([v, land], plan, len(flips)))


def _small_sum(name, own, land):
    x, y, c = _position()
    me = jnp.stack([4 * x + 2 * y + c]).astype(jnp.int32)

    def body(me_ref, own_ref, land_ref, out_ref):
        term = lambda dev: jnp.where(me_ref[0] == dev, own_ref[...], land_ref[dev])
        acc = term(0)
        for dev in range(1, N_DEV):
            acc = acc + term(dev)
        out_ref[...] = acc

    return pl.pallas_call(
        body, name=name, out_shape=jax.ShapeDtypeStruct(own.shape, F32),
        grid_spec=pltpu.PrefetchScalarGridSpec(
            num_scalar_prefetch=1, grid=(1,),
            in_specs=[pl.BlockSpec(own.shape, lambda i, m: (0, 0)), pl.BlockSpec(land.shape, lambda i, m: (0, 0, 0))],
            out_specs=pl.BlockSpec(own.shape, lambda i, m: (0, 0))),
        compiler_params=_cparams("arbitrary"),
    )(me, own, land)


def _adamw_update(w, gg, m, v):
    mn = ADAM_B1 * m + (1.0 - ADAM_B1) * gg
    vn = ADAM_B2 * v + (1.0 - ADAM_B2) * (gg * gg)
    m_hat = mn / (1.0 - ADAM_B1 ** ADAM_STEP)
    v_hat = vn / (1.0 - ADAM_B2 ** ADAM_STEP)
    return -ADAM_LR * (m_hat / (jnp.sqrt(v_hat) + ADAM_EPS) + ADAM_WD * w), mn, vn


def _adamw(name, w, g, m, v):
    def body(w_ref, g_ref, m_ref, v_ref, d_ref, mo_ref, vo_ref):
        d_ref[...], mo_ref[...], vo_ref[...] = _adamw_update(w_ref[...], g_ref[...], m_ref[...], v_ref[...])

    blk = pl.BlockSpec(w.shape, lambda i: (0, 0))
    return pl.pallas_call(
        body, name=name, out_shape=(jax.ShapeDtypeStruct(w.shape, F32),) * 3, grid=(1,),
        in_specs=[blk] * 4, out_specs=(blk,) * 3, compiler_params=_cparams("arbitrary"),
    )(w, g, m, v)


def _rows_to_bf16(name, w):
    r, _, c = w.shape

    def body(w_ref, o_ref):
        o_ref[...] = w_ref[:, 0, :].astype(BF16)

    return pl.pallas_call(
        body, name=name, out_shape=jax.ShapeDtypeStruct((r, c), BF16), grid=(1,),
        in_specs=[pl.BlockSpec((r, 1, c), lambda i: (0, 0, 0))], out_specs=pl.BlockSpec((r, c), lambda i: (0, 0)),
        compiler_params=_cparams("arbitrary"),
    )(w)


def _adamw_rows(name, pos_c, w, g_mine, g_other, m, v):
    r, _, c = w.shape
    ch = c // 2

    def body(p_ref, w_ref, gm_ref, go_ref, m_ref, v_ref, g_ref, d_ref, mo_ref, vo_ref):
        gg = jnp.where(pl.program_id(0) == p_ref[0], gm_ref[...], go_ref[...])
        dl, mn, vn = _adamw_update(w_ref[:, 0, :], gg, m_ref[:, 0, :], v_ref[:, 0, :])
        g_ref[:, 0, :] = gg
        d_ref[:, 0, :] = dl
        mo_ref[:, 0, :] = mn
        vo_ref[:, 0, :] = vn

    rows = pl.BlockSpec((r, 1, ch), lambda h, p: (0, 0, h))
    half = pl.BlockSpec((r, ch), lambda h, p: (0, 0))
    return pl.pallas_call(
        body, name=name, out_shape=(jax.ShapeDtypeStruct(w.shape, F32),) * 4,
        grid_spec=pltpu.PrefetchScalarGridSpec(
            num_scalar_prefetch=1, grid=(2,), in_specs=[rows, half, half, rows, rows], out_specs=(rows,) * 4),
        compiler_params=_cparams("parallel"),
    )(pos_c, w, g_mine, g_other, m, v)


def _adamw_halves(name, pos_c, w, g_mine, g_other, m, v, tr, dep=None):
    r, c = w.shape
    rh = r // 2
    tr = tr if rh % tr == 0 else rh
    nt = rh // tr

    def body(p_ref, w_ref, gm_ref, go_ref, m_ref, v_ref, g_ref, d_ref, mo_ref, vo_ref):
        gg = jnp.where(pl.program_id(0) == p_ref[0], gm_ref[...], go_ref[...])
        g_ref[...] = gg
        d_ref[...], mo_ref[...], vo_ref[...] = _adamw_update(w_ref[...], gg, m_ref[...], v_ref[...])

    full = pl.BlockSpec((tr, c), lambda h, i, p: (h * nt + i, 0))
    half = pl.BlockSpec((tr, c), lambda h, i, p: (i, 0))
    dep_specs, dep_ops = _dep_args(dep)
    return pl.pallas_call(
        _after(body, 6, dep), name=name, out_shape=(jax.ShapeDtypeStruct((r, c), F32),) * 4,
        grid_spec=pltpu.PrefetchScalarGridSpec(
            num_scalar_prefetch=1, grid=(2, nt),
            in_specs=[full, half, half, full, full] + dep_specs, out_specs=(full,) * 4),
        compiler_params=_cparams("parallel", "parallel"),
    )(pos_c, w, g_mine, g_other, m, v, *dep_ops)


def _col_sharded_to_comm(g):
    k, n = g.shape
    return g.reshape(2, k // 2, N_CHIPS, n // N_CHIPS).transpose(2, 0, 1, 3)


def _row_sharded_to_comm(g):
    r, c = g.shape
    return g.reshape(N_CHIPS, 2, r // (2 * N_CHIPS), c)


def _col_sharded_full(g):
    _, _, rh, c = g.shape
    return g.reshape(N_CHIPS, 2 * rh, c).transpose(1, 0, 2).reshape(2 * rh, N_CHIPS * c)


def _row_sharded_full(g):
    _, _, rh, c = g.shape
    return g.reshape(N_CHIPS * 2 * rh, c)


def _chip_rows(w3, start, stop, own=None, me=None):
    r = w3.shape[1]
    parts = []
    for chip in range(N_CHIPS):
        lo, hi = max(start - chip * r, 0), min(stop - chip * r, r)
        if lo < hi:
            part = w3[chip, lo:hi]
            parts.append(part if own is None else jnp.where(me == chip, own[lo:hi], part))
    return parts[0] if len(parts) == 1 else jnp.concatenate(parts, axis=0)


def _pack_small(g1, bfv, mix, scale, g2n, gf, extra=None):
    row8 = jnp.pad(bfv.reshape(1, N_HEADS), ((0, 0), (0, LANES - N_HEADS)))
    if extra is not None:
        row8 = row8 + jnp.pad(extra[:, :1], ((0, 0), (N_HEADS, LANES - N_HEADS - 1)))
    return jnp.concatenate([
        g1.reshape(8, LANES), jnp.pad(row8, ((0, 7), (0, 0))), mix.reshape(512, LANES),
        jnp.pad(scale.reshape(4, LANES), ((0, 4), (0, 0))), g2n.reshape(8, LANES), gf.reshape(8, LANES)], axis=0)


def _unpack_small(s, like):
    g1, bfv, mix, scale, g2n, gf = like
    return (s[0:8].reshape(g1.shape), s[8, :N_HEADS].reshape(bfv.shape), s[16:528].reshape(mix.shape),
            s[528:532].reshape(scale.shape), s[536:544].reshape(g2n.shape), s[544:552].reshape(gf.shape))


class _MeshLinks:
    def __init__(self, shards_in, shards_rest):
        self.gin = _gather_begin("in", shards_in, None, column_halves=True)
        self.grest = _gather_begin("rest", shards_rest, self.gin["token"])
        self.tokens = {"gather": self.grest["token"]}
        self.groups, self.flight, self.slot = {}, None, 0

    @property
    def token(self):
        return list(self.tokens.values())

    def tie(self, x):
        return _tie(x, self.token)

    def weights_in(self, after):
        st = _gather_forward(self.gin, after)
        (g,), (own,) = _gather_end(st, st["token"], merge=False)
        return g, own, 2 * lax.axis_index("x") + lax.axis_index("y")

    def rest_forward(self, after):
        self.grest = _gather_forward(self.grest, after)
        self.tokens["gather"] = self.grest["token"]

    def weights_rest(self, after):
        g = _gather_end(self.grest, after)
        del self.tokens["gather"]
        return [_col_sharded_full(g[0]), _col_sharded_full(g[1])] + [_row_sharded_full(a) for a in g[2:]]

    def advance(self, after, begin=()):
        slot = self.slot
        self.slot += 1
        if self.flight is not None:
            tags, sems, parts = self.flight
            arrays, plan, _ = _join_parts(parts)
            thru = _copies_wait(f"slot{slot}_wait", sems, arrays, plan, after)
            for tag, part in zip(tags, parts):
                self.groups[tag] = _reduce_next(self.groups[tag], thru[:len(part[0])])
                thru = thru[len(part[0]):]
        for st in begin:
            self.groups[st["tag"]] = st
        live = [(tag, st["part"]) for tag, st in self.groups.items() if "part" in st]
        self.flight = None
        self.tokens.pop("reduce", None)
        if live:
            arrays, plan, m = _join_parts([part for _, part in live])
            sems, thru, token = _copies_start(f"slot{slot}_start", arrays, plan, m)
            parts = []
            for _, (part_arrays, part_plan, part_m) in live:
                parts.append((thru[:len(part_arrays)], part_plan, part_m))
                thru = thru[len(part_arrays):]
            self.flight = ([tag for tag, _ in live], sems, parts)
            self.tokens["reduce"] = token

    def reduced(self, tag):
        return self.groups[tag]["done"]


class _NoLinks:
    token = None

    def __init__(self, w_in, rest):
        self.w_in, self.rest, self.grads = w_in, rest, {}

    def tie(self, x):
        return x

    def weights_in(self, after):
        return self.w_in, None, None

    def rest_forward(self, after):
        pass

    def weights_rest(self, after):
        return self.rest

    def advance(self, after, begin=()):
        for st in begin:
            self.grads[st["tag"]] = st["grads"]


def _local_step(links, x, target, seq, norm1_g, b_forget, pool_mix, pool_scale, norm2_g, norm_f_g, between=None):
    t, d = x.shape
    tq = min(256, seq)
    aw = ATTN_WIDTH
    o_q, o_f, o_g = POOL_WIDTH, POOL_WIDTH + 3 * aw, POOL_WIDTH + 3 * aw + N_HEADS
    bf = jnp.pad(b_forget, ((0, 0), (0, LANES - N_HEADS)))
    mixb = pool_mix.astype(BF16)

    h = _norm_fwd("norm1_fwd", x, links.tie(norm1_g), 512)
    w_in, own, me = links.weights_in(h)
    wu = _chip_rows(w_in, 0, o_q, own, me)
    wqkv = _chip_rows(w_in, o_q, o_f, own, me)
    wft = jnp.pad(_chip_rows(w_in, o_f, o_g, own, me), ((0, LANES - N_HEADS), (0, 0)))
    wg2 = _chip_rows(w_in, o_g, N_CHIPS * w_in.shape[1], own, me)
    wf = wft.T
    u = _matmul("mm_u", h, wu, "nt", F32, 1024, 512, d)
    g2 = _matmul("mm_gates", h, wg2, "nt", BF16, 1024, 1024, d)
    fl, fcum = _forget_fwd(h, wf, bf, seq)
    qa, ka, v = _attn_prep(h, _head_blocks(wqkv[:aw]), _head_blocks(wqkv[aw:2 * aw]), wqkv[2 * aw:], fcum, 1024)
    p, ps = _pool_fwd(u, mixb, pool_scale, seq)
    links.rest_forward([ps, qa, g2])
    o, lse = _attn_fwd(qa, ka, v, seq, tq, dep=links.token)
    w_pool_out, w_attn_out, w_out, w_ffn_gate, w_ffn_up, w_ffn_down = links.weights_rest(o)
    merged, x1 = _merge_fwd(x, ps, o, g2, w_pool_out, w_attn_out, w_out, 512)
    h2, gt, up, act, x2 = _ffn_fwd(x1, norm2_g, w_ffn_gate, w_ffn_up, w_ffn_down, 1024, 256)
    loss, dx2, d_gf = _final_fwd_bwd(x2, target, norm_f_g, 512)

    dgt, dup, dx1, d_g2n = _ffn_bwd(dx2, x1, norm2_g, gt, up, w_ffn_gate, w_ffn_up, w_ffn_down, 1024, 256)
    d_wd = _matmul("dw_down", act, dx2, "tn", F32, 1408, 1024, 1024)
    d_wg = _matmul("dw_gate", dgt, h2, "tn", F32, 1408, 1024, 1024)
    d_wu = _matmul("dw_up", dup, h2, "tn", F32, 1408, 1024, 1024)
    links.advance(None, begin=[_reduce_begin("a", [_row_sharded_to_comm(g) for g in (d_wg, d_wu, d_wd)])])
    dpy, day, dg2, dps, da = _merge_bwd(dx1, ps, o, g2, w_pool_out, w_attn_out, w_out, 512, dep=links.token)
    links.advance(dps)
    d_wout = _matmul("dw_out", merged, dx1, "tn", F32, 1024, 1024, 1024)
    d_wpo = _matmul("dw_pool_out", ps, dpy, "tn", F32, 512, 1024, 1024)
    d_wao = _matmul("dw_attn_out", o, day, "tn", F32, 512, 1024, 1024)
    dq, dk, dv, dfr = _attn_bwd(qa, ka, v, da, lse, seq, tq, dep=links.token)
    links.advance(dq, begin=[_reduce_begin(
        "m", [_col_sharded_to_comm(d_wpo), _col_sharded_to_comm(d_wao), _row_sharded_to_comm(d_wout)])])
    dfc = jnp.pad(dfr.reshape(N_HEADS, t).T, ((0, 0), (0, LANES - N_HEADS)))
    dfl, d_bf = _forget_bwd(dfc, fl, bf, seq)
    du, d_mix, d_scale = _pool_bwd(dps, p, mixb, links.tie(pool_scale), seq)
    d_wu_in = _matmul("dw_in_u", du, h, "tn", F32, 512, 1024, 1024, dep=links.token)
    small = (jnp.zeros_like(norm1_g), d_bf[:, :N_HEADS], d_mix, d_scale, d_g2n, d_gf)
    links.advance([d_wu_in], begin=[_small_begin("small", _pack_small(*small, extra=loss))])
    d_wq = _matmul("dw_in_q", dq, h, "tn", F32, 512, 1024, 1024, dep=links.token)
    d_wk = _matmul("dw_in_k", dk, h, "tn", F32, 512, 1024, 1024, dep=links.token)
    d_wv = _matmul("dw_in_v", dv, h, "tn", F32, 512, 1024, 1024, dep=links.token)
    d_wf = _matmul("dw_in_f", dfl, h, "tn", F32, LANES, 1024, 512)
    d_wg2 = _matmul("dw_in_gates", dg2, h, "tn", F32, 1024, 1024, 1024, dep=links.token)
    d_win = jnp.concatenate([d_wu_in, d_wq, d_wk, d_wv, d_wf[:N_HEADS], d_wg2], axis=0)
    comm_b = [d_win.reshape(N_CHIPS, d_win.shape[0] // N_CHIPS, d)]
    links.advance(comm_b, begin=[_reduce_begin("b", comm_b, column_halves=True)])
    if between is not None:
        between()
    dx, d_g1 = _in_bwd(du, dq, dk, dv, dg2, dfl, dx1, x, links.tie(norm1_g), wu, wqkv, wg2, wft, 512)
    return loss, dx, (d_g1,) + small[1:]


def kernel(x, norm1_g, w_in, b_forget, pool_mix, pool_scale, w_pool_out, w_attn_out, w_out, norm2_g, w_ffn_gate, w_ffn_up, w_ffn_down, norm_f_g, loss_target, m_norm1_g, m_w_in, m_b_forget, m_pool_mix, m_pool_scale, m_w_pool_out, m_w_attn_out, m_w_out, m_norm2_g, m_w_ffn_gate, m_w_ffn_up, m_w_ffn_down, m_norm_f_g, v_norm1_g, v_w_in, v_b_forget, v_pool_mix, v_pool_scale, v_w_pool_out, v_w_attn_out, v_w_out, v_norm2_g, v_w_ffn_gate, v_w_ffn_up, v_w_ffn_down, v_norm_f_g):
    nb, seq, d = x.shape
    group_a = ((w_ffn_gate, m_w_ffn_gate, v_w_ffn_gate, True, 9), (w_ffn_up, m_w_ffn_up, v_w_ffn_up, True, 10),
               (w_ffn_down, m_w_ffn_down, v_w_ffn_down, False, 11))
    group_m = ((w_pool_out, m_w_pool_out, v_w_pool_out, False, 5), (w_attn_out, m_w_attn_out, v_w_attn_out, False, 6),
               (w_out, m_w_out, v_w_out, False, 7))
    group_b = ((w_in, m_w_in, v_w_in, False, 1),)
    small_w = (norm1_g, b_forget, pool_mix, pool_scale, norm2_g, norm_f_g)
    small_m = (m_norm1_g, m_b_forget, m_pool_mix, m_pool_scale, m_norm2_g, m_norm_f_g)
    small_v = (v_norm1_g, v_b_forget, v_pool_mix, v_pool_scale, v_norm2_g, v_norm_f_g)
    small_pos = (0, 2, 3, 4, 8, 12)
    view = lambda a, tr: a[0].T if tr else a[0]
    unview = lambda a, tr, like: (a.T if tr else a).reshape(like.shape)

    def shard(w, tr):
        lw = view(w, tr).astype(BF16)
        return lw.reshape(2, lw.shape[0] // 2, lw.shape[1])

    cm = lambda a: jnp.transpose(a, (2, 0, 1))
    shard_in = _rows_to_bf16("w_in_to_bf16", cm(w_in))
    links = _MeshLinks([shard_in],
                       [shard(w_pool_out, False), shard(w_attn_out, False), shard(w_out, False),
                        shard(w_ffn_gate, True), shard(w_ffn_up, True), shard(w_ffn_down, False)])
    grads, deltas, new_m, new_v = [None] * 13, [None] * 13, [None] * 13, [None] * 13
    pos_c = jnp.stack([lax.axis_index("c")]).astype(jnp.int32)

    def update(tag, group, dep, members=(0, 1, 2)):
        last = []
        reduced = links.reduced(tag)
        for k in members:
            (w, m, v, tr, pos), (mine, other) = group[k], reduced[k]
            outs = _adamw_halves(f"adamw_{tag}{k}", pos_c, view(w, tr), mine, other, view(m, tr), view(v, tr), 256,
                                 dep=dep)
            grads[pos], deltas[pos], new_m[pos], new_v[pos] = (unview(a, tr, w) for a in outs)
            last.append(outs[1])
        return last

    def update_a():
        links.advance(update("a", group_a, links.token, members=(0, 1)))

    loss, dx, small_g = _local_step(
        links, x.reshape(nb * seq, d), loss_target.reshape(nb * seq, d), seq,
        norm1_g, b_forget, pool_mix[0], pool_scale, norm2_g, norm_f_g.reshape(1, d), between=update_a)

    links.advance(dx, begin=[_small_begin("g1", small_g[0].reshape(8, LANES))])
    last = update("m", group_m, links.token) + update("a", group_a, links.token, members=(2,))
    small_rest = _small_sum("small_sum", *links.reduced("small")[0])
    links.advance(last + [small_rest])
    small_sum = jnp.concatenate([_small_sum("g1_sum", *links.reduced("g1")[0]), small_rest[8:]], axis=0)
    loss_out = small_sum[8, N_HEADS]
    dl, mn, vn = _adamw("adamw_small", _pack_small(*small_w), small_sum * _small_mask(), _pack_small(*small_m),
                        _pack_small(*small_v))
    for pos, g, a, b, e in zip(small_pos, _unpack_small(small_sum, small_w), _unpack_small(dl, small_w),
                               _unpack_small(mn, small_w), _unpack_small(vn, small_w)):
        grads[pos], deltas[pos], new_m[pos], new_v[pos] = g, a, b, e
    links.advance([dl])
    (mine, other), = links.reduced("b")
    outs = _adamw_rows("adamw_b0", pos_c, cm(w_in), mine, other, cm(m_w_in), cm(v_w_in))
    grads[1], deltas[1], new_m[1], new_v[1] = (jnp.transpose(a, (1, 2, 0)) for a in outs)

    return (loss_out, dx.reshape(nb, seq, d), *grads, *deltas, *new_m, *new_v)


def _small_mask():
    rows = lax.broadcasted_iota(jnp.int32, (552, LANES), 0)
    lanes = lax.broadcasted_iota(jnp.int32, (552, LANES), 1)
    return jnp.where(jnp.logical_and(rows == 8, lanes == N_HEADS), 0.0, 1.0).astype(F32)
```

```python
import jax
import jax.numpy as jnp
from jax import lax
from jax.experimental import pallas as pl
from jax.experimental.pallas import tpu as pltpu

F32 = jnp.float32
BF16 = jnp.bfloat16

D_MODEL = 1024
POOL_WINDOWS = (2, 4, 8, 16)
POOL_GROUPS = 4
POOL_GROUP_DIM = 128
POOL_WIDTH = 512
HEAD_DIM = 64
N_HEADS = 8
ATTN_WIDTH = 512
D_FF = 2816
RMS_EPS = 1e-6
ATTN_SCALE = HEAD_DIM ** -0.5
NEG_BIG = -1e30

ADAM_LR = 0.001
ADAM_B1 = 0.9
ADAM_B2 = 0.999
ADAM_EPS = 1e-08
ADAM_WD = 0.01
ADAM_STEP = 10

LANES = 128
N_CHIPS = 4
N_DEV = 8
VMEM_LIMIT_V7X = 52 * 1024 * 1024
ROW_CHUNK = 256
MESH = pl.DeviceIdType.MESH
ANY = pl.BlockSpec(memory_space=pl.ANY)


def _cparams(*sem):
    return pltpu.CompilerParams(dimension_semantics=sem if sem else None, vmem_limit_bytes=VMEM_LIMIT_V7X)


def _dep_list(dep):
    return [] if dep is None else (list(dep) if isinstance(dep, (list, tuple)) else [dep])


def _after(body, n_in, dep):
    k = len(_dep_list(dep))
    if k == 0:
        return body

    def wrapped(*refs):
        body(*refs[:n_in], *refs[n_in + k:])

    return wrapped


def _dep_args(dep):
    deps = _dep_list(dep)
    return [ANY] * len(deps), deps


def _dot(a, b):
    return lax.dot_general(a, b, (((1,), (0,)), ((), ())), preferred_element_type=F32)


def _dot_nt(a, b):
    return lax.dot_general(a, b, (((1,), (1,)), ((), ())), preferred_element_type=F32)


def _dot_tn(a, b):
    return lax.dot_general(a, b, (((0,), (0,)), ((), ())), preferred_element_type=F32)


def _sigmoid(x):
    return jax.nn.sigmoid(x)


def _rms_fwd(x, g):
    r = lax.rsqrt(jnp.mean(x * x, axis=-1, keepdims=True) + RMS_EPS)
    return (x * r) * g


def _rms_bwd(x, g, dy):
    r = lax.rsqrt(jnp.mean(x * x, axis=-1, keepdims=True) + RMS_EPS)
    xh = x * r
    dg = jnp.sum(dy * xh, axis=0, keepdims=True)
    dxh = dy * g
    dx = r * (dxh - xh * jnp.mean(dxh * xh, axis=-1, keepdims=True))
    return dx, dg


def _matmul(name, a, b, mode, out_dtype, tm, tn, tk, dep=None):
    if mode == "nn":
        (m, k), (_, n) = a.shape, b.shape
    elif mode == "nt":
        (m, k), (n, _) = a.shape, b.shape
    else:
        (k, m), (_, n) = a.shape, b.shape
    tm, tn, tk = min(tm, m), min(tn, n), min(tk, k)
    assert m % tm == 0 and n % tn == 0 and k % tk == 0, (name, m, n, k, tm, tn, tk)
    nk = k // tk
    if mode == "tn":
        a_spec = pl.BlockSpec((tk, tm), lambda i, j, kk: (kk, i))
    else:
        a_spec = pl.BlockSpec((tm, tk), lambda i, j, kk: (i, kk))
    if mode == "nt":
        b_spec = pl.BlockSpec((tn, tk), lambda i, j, kk: (j, kk))
    else:
        b_spec = pl.BlockSpec((tk, tn), lambda i, j, kk: (kk, j))
    dot = {"nn": _dot, "nt": _dot_nt, "tn": _dot_tn}[mode]
    use_scratch = nk > 1 and out_dtype != F32

    def body(a_ref, b_ref, o_ref, *scratch):
        if nk == 1 and mode != "tn":
            rows = min(ROW_CHUNK, tm)
            bb = b_ref[...].astype(BF16)
            for r0 in range(0, tm, rows):
                o_ref[r0:r0 + rows, :] = dot(a_ref[r0:r0 + rows, :].astype(BF16), bb).astype(out_dtype)
            return
        prod = dot(a_ref[...].astype(BF16), b_ref[...].astype(BF16))
        if nk == 1:
            o_ref[...] = prod.astype(out_dtype)
            return
        acc = scratch[0] if use_scratch else o_ref
        kk = pl.program_id(2)

        @pl.when(kk == 0)
        def _():
            acc[...] = prod

        @pl.when(kk > 0)
        def _():
            acc[...] += prod

        if use_scratch:
            @pl.when(kk == nk - 1)
            def _():
                o_ref[...] = acc[...].astype(out_dtype)

    dep_specs, dep_ops = _dep_args(dep)
    return pl.pallas_call(
        _after(body, 2, dep),
        name=name,
        out_shape=jax.ShapeDtypeStruct((m, n), out_dtype),
        grid=(m // tm, n // tn, nk),
        in_specs=[a_spec, b_spec] + dep_specs,
        out_specs=pl.BlockSpec((tm, tn), lambda i, j, kk: (i, j)),
        scratch_shapes=[pltpu.VMEM((tm, tn), F32)] if use_scratch else [],
        compiler_params=_cparams("parallel", "parallel", "arbitrary"),
    )(a, b, *dep_ops)


def _norm_fwd(name, x, g, tm):
    t, d = x.shape
    tm = min(tm, t)

    def body(x_ref, g_ref, h_ref):
        h_ref[...] = _rms_fwd(x_ref[...], g_ref[...]).astype(BF16)

    return pl.pallas_call(
        body, name=name, out_shape=jax.ShapeDtypeStruct((t, d), BF16), grid=(t // tm,),
        in_specs=[pl.BlockSpec((tm, d), lambda i: (i, 0)), pl.BlockSpec((1, d), lambda i: (0, 0))],
        out_specs=pl.BlockSpec((tm, d), lambda i: (i, 0)),
        compiler_params=_cparams("parallel"),
    )(x, g)


def _split3(x):
    hi = x.astype(BF16)
    r1 = x - hi.astype(F32)
    mid = r1.astype(BF16)
    lo = (r1 - mid.astype(F32)).astype(BF16)
    return hi, mid, lo


def _tri_dot(tri, x):
    hi, mid, lo = _split3(x)
    return _dot(tri, hi) + _dot(tri, mid) + _dot(tri, lo)


def _forget_fwd(h, wf, bf, seq):
    t, d = h.shape
    cb = min(256, seq)

    def body(h_ref, wf_ref, bf_ref, fl_ref, fc_ref):
        fl = _dot(h_ref[...], wf_ref[...])
        fl_ref[...] = fl
        xx = fl + bf_ref[...]
        lf = jnp.minimum(xx, 0.0) - jnp.log(1.0 + jnp.exp(-jnp.abs(xx)))
        ri = lax.broadcasted_iota(jnp.int32, (cb, cb), 0)
        ci = lax.broadcasted_iota(jnp.int32, (cb, cb), 1)
        tri = (ri >= ci).astype(BF16)
        carry = jnp.zeros((1, LANES), F32)
        for blk in range(seq // cb):
            cs = _tri_dot(tri, lf[blk * cb:(blk + 1) * cb]) + carry
            fc_ref[blk * cb:(blk + 1) * cb, :] = cs
            carry = cs[cb - 1:cb, :]

    return pl.pallas_call(
        body, name="forget_fwd",
        out_shape=(jax.ShapeDtypeStruct((t, LANES), F32), jax.ShapeDtypeStruct((t, LANES), F32)),
        grid=(t // seq,),
        in_specs=[pl.BlockSpec((seq, d), lambda b: (b, 0)), pl.BlockSpec((d, LANES), lambda b: (0, 0)),
                  pl.BlockSpec((1, LANES), lambda b: (0, 0))],
        out_specs=(pl.BlockSpec((seq, LANES), lambda b: (b, 0)), pl.BlockSpec((seq, LANES), lambda b: (b, 0))),
        compiler_params=_cparams("parallel"),
    )(h, wf, bf)


def _pool_fwd(u, mix, scale, seq):
    t = u.shape[0]

    def body(u_ref, mix_ref, sc_ref, p_ref, ps_ref):
        tpos = lax.broadcasted_iota(jnp.int32, (seq, POOL_GROUP_DIM), 0)
        for g in range(POOL_GROUPS):
            sl = slice(g * POOL_GROUP_DIM, (g + 1) * POOL_GROUP_DIM)
            ug = u_ref[:, sl]
            s = ug
            for lvl in range(g + 1):
                d = 2 ** lvl
                s = s + jnp.where(tpos >= d, pltpu.roll(s, d, 0), 0.0)
            cnt = jnp.minimum(tpos + 1, POOL_WINDOWS[g]).astype(F32)
            pb = (s / cnt - ug).astype(BF16)
            p_ref[:, sl] = pb
            ps_ref[:, sl] = (_dot(pb, mix_ref[g]) * sc_ref[:, sl]).astype(BF16)

    return pl.pallas_call(
        body, name="pool_fwd",
        out_shape=(jax.ShapeDtypeStruct((t, POOL_WIDTH), BF16), jax.ShapeDtypeStruct((t, POOL_WIDTH), BF16)),
        grid=(t // seq,),
        in_specs=[pl.BlockSpec((seq, POOL_WIDTH), lambda b: (b, 0)),
                  pl.BlockSpec((POOL_GROUPS, POOL_GROUP_DIM, POOL_GROUP_DIM), lambda b: (0, 0, 0)),
                  pl.BlockSpec((1, POOL_WIDTH), lambda b: (0, 0))],
        out_specs=(pl.BlockSpec((seq, POOL_WIDTH), lambda b: (b, 0)), pl.BlockSpec((seq, POOL_WIDTH), lambda b: (b, 0))),
        compiler_params=_cparams("parallel"),
    )(u, mix, scale)


def _aug_constants():
    w = N_HEADS * LANES
    rows = jnp.arange(3 * LANES)
    piece, head = rows // LANES, rows % LANES
    cols = jnp.arange(w)
    live = (head < N_HEADS)[:, None]
    pq = (live & (cols[None, :] == (head * LANES + HEAD_DIM + piece)[:, None])).astype(BF16)
    pk = -(live & (cols[None, :] == (head * LANES + HEAD_DIM + 3 + piece)[:, None])).astype(BF16)
    lane = cols % LANES
    oq = ((lane >= HEAD_DIM + 3) & (lane < HEAD_DIM + 6)).astype(F32)[None, :]
    ok = ((lane >= HEAD_DIM) & (lane < HEAD_DIM + 3)).astype(F32)[None, :]
    return pq, pk, oq, ok


def _head_blocks(wt):
    d = wt.shape[1]
    return jnp.pad(wt.reshape(N_HEADS, HEAD_DIM, d), ((0, 0), (0, LANES - HEAD_DIM), (0, 0))).reshape(N_HEADS * LANES, d)


def _attn_prep(h, wq, wk, wv, fcum, tm):
    t, d = h.shape
    tm = min(tm, t)
    rows = min(ROW_CHUNK, tm)
    w = N_HEADS * LANES
    pq, pk, oq, ok = _aug_constants()

    def body(h_ref, wq_ref, wk_ref, wv_ref, f_ref, pq_ref, pk_ref, oq_ref, ok_ref, qa_ref, ka_ref, v_ref):
        for r0 in range(0, tm, rows):
            rs = slice(r0, r0 + rows)
            hh = h_ref[rs, :]
            fs = jnp.concatenate(_split3(f_ref[rs, :]), axis=1)
            q = _dot_nt(hh, wq_ref[...]).astype(BF16).astype(F32) * ATTN_SCALE
            qa_ref[rs, :] = (q + _dot(fs, pq_ref[...]) + oq_ref[...]).astype(BF16)
            k = _dot_nt(hh, wk_ref[...]).astype(BF16).astype(F32)
            ka_ref[rs, :] = (k + _dot(fs, pk_ref[...]) + ok_ref[...]).astype(BF16)
            v_ref[rs, :] = _dot_nt(hh, wv_ref[...]).astype(BF16)

    row = lambda n: pl.BlockSpec((tm, n), lambda i: (i, 0))
    full = lambda a: pl.BlockSpec(a.shape, lambda i: (0, 0))
    return pl.pallas_call(
        body, name="attn_prep",
        out_shape=(jax.ShapeDtypeStruct((t, w), BF16), jax.ShapeDtypeStruct((t, w), BF16),
                   jax.ShapeDtypeStruct((t, ATTN_WIDTH), BF16)),
        grid=(t // tm,),
        in_specs=[row(d), full(wq), full(wk), full(wv), row(LANES), full(pq), full(pk), full(oq), full(ok)],
        out_specs=(row(w), row(w), row(ATTN_WIDTH)),
        compiler_params=_cparams("parallel"),
    )(h, wq, wk, wv, fcum, pq, pk, oq, ok)


def _fold_lanes(x, op):
    out = x[:, :LANES]
    for g in range(1, x.shape[1] // LANES):
        out = op(out, x[:, g * LANES:(g + 1) * LANES])
    return out


def _causal_sweep(i, tile, carry):
    def quad(jj, c):
        for u in range(4):
            c = tile(4 * jj + u, c, False)
        return c

    carry = lax.fori_loop(0, i // 4, quad, carry)
    base = 4 * (i // 4)
    carry = lax.cond(i % 4 >= 2, lambda c: tile(base + 1, tile(base, c, False), False), lambda c: c, carry)
    return lax.cond(i % 2 == 1, lambda c: tile(i, tile(i - 1, c, False), True), lambda c: tile(i, c, True), carry)


def _attn_fwd(qa, ka, v, seq, tq, dep=None):
    t = qa.shape[0]
    nq = seq // tq
    hp_n = N_HEADS // 2
    heads = [slice(e * LANES, (e + 1) * LANES) for e in range(2)]

    def body(q_ref, k_ref, v_ref, o_ref, lse_ref, s_buf):
        i = pl.program_id(2)
        diag_ok = lax.broadcasted_iota(jnp.int32, (tq, tq), 0) >= lax.broadcasted_iota(jnp.int32, (tq, tq), 1)
        qs = [q_ref[:, hl] for hl in heads]

        def sweep1(j, mxs, diagonal):
            r0 = pl.multiple_of(j * tq, tq)
            out = []
            for e, hl in enumerate(heads):
                s = _dot_nt(qs[e], k_ref[pl.ds(r0, tq), hl])
                if diagonal:
                    s = jnp.where(diag_ok, s, NEG_BIG)
                s_buf[e, j] = s
                out.append(jnp.maximum(mxs[e], _fold_lanes(s, jnp.maximum)))
            return tuple(out)

        mxs = _causal_sweep(i, sweep1, (jnp.full((tq, LANES), NEG_BIG, F32),) * 2)
        ms = [jnp.max(mx, axis=1, keepdims=True) for mx in mxs]

        def sweep2(j, carry, diagonal):
            r0 = pl.multiple_of(j * tq, tq)
            vv = v_ref[pl.ds(r0, tq), :]
            out = []
            for e in range(2):
                p = jnp.exp(s_buf[e, j] - ms[e])
                out += [carry[2 * e] + _fold_lanes(p, jnp.add), carry[2 * e + 1] + _dot(p.astype(BF16), vv)]
            return tuple(out)

        res = _causal_sweep(i, sweep2, (jnp.zeros((tq, LANES), F32),) * 4)
        outs = []
        for e in range(2):
            l = jnp.sum(res[2 * e], axis=1, keepdims=True)
            outs.append(res[2 * e + 1] / l)
            lse_ref[:, e:e + 1] = ms[e] + jnp.log(l)
        lane = lax.broadcasted_iota(jnp.int32, (tq, LANES), 1)
        o_ref[...] = jnp.where(lane < HEAD_DIM, outs[0], outs[1])

    dep_specs, dep_ops = _dep_args(dep)
    return pl.pallas_call(
        _after(body, 3, dep), name="attn_fwd",
        out_shape=(jax.ShapeDtypeStruct((t, ATTN_WIDTH), F32), jax.ShapeDtypeStruct((hp_n, t, 2), F32)),
        grid=(t // seq, hp_n, nq),
        in_specs=[pl.BlockSpec((tq, 2 * LANES), lambda b, hp, i: (b * nq + i, hp)),
                  pl.BlockSpec((seq, 2 * LANES), lambda b, hp, i: (b, hp)),
                  pl.BlockSpec((seq, LANES), lambda b, hp, i: (b, hp))] + dep_specs,
        out_specs=(pl.BlockSpec((tq, LANES), lambda b, hp, i: (b * nq + i, hp)),
                   pl.BlockSpec((None, tq, 2), lambda b, hp, i: (hp, b * nq + i, 0))),
        scratch_shapes=[pltpu.VMEM((2, nq, tq, tq), F32)],
        compiler_params=_cparams("parallel", "parallel", "arbitrary"),
    )(qa, ka, v, *dep_ops)


def _merge_fwd(x, ps, o, g2, wpo, wao, wout, tm):
    t, d = x.shape
    tm = min(tm, t)
    rows = min(ROW_CHUNK, tm)

    def body(x_ref, ps_ref, o_ref, gp_ref, ga_ref, wpo_ref, wao_ref, wout_ref, mg_ref, x1_ref):
        for r0 in range(0, tm, rows):
            rs = slice(r0, r0 + rows)
            py = _dot(ps_ref[rs, :], wpo_ref[...])
            ay = _dot(o_ref[rs, :].astype(BF16), wao_ref[...])
            mb = (_sigmoid(gp_ref[rs, :].astype(F32)) * py + _sigmoid(ga_ref[rs, :].astype(F32)) * ay).astype(BF16)
            mg_ref[rs, :] = mb
            x1_ref[rs, :] = x_ref[rs, :] + _dot(mb, wout_ref[...])

    row = lambda w: pl.BlockSpec((tm, w), lambda i: (i, 0))
    full = lambda a: pl.BlockSpec(a.shape, lambda i: (0, 0))
    return pl.pallas_call(
        body, name="merge_fwd",
        out_shape=(jax.ShapeDtypeStruct((t, d), BF16), jax.ShapeDtypeStruct((t, d), F32)),
        grid=(t // tm,),
        in_specs=[row(d), row(POOL_WIDTH), row(ATTN_WIDTH), pl.BlockSpec((tm, d), lambda i: (i, 0)),
                  pl.BlockSpec((tm, d), lambda i: (i, 1)), full(wpo), full(wao), full(wout)],
        out_specs=(row(d), row(d)),
        compiler_params=_cparams("parallel"),
    )(x, ps, o, g2, g2, wpo, wao, wout)


def _ffn_fwd(x1, g, wg, wu, wd, tm, tf):
    t, d = x1.shape
    f = wg.shape[0]
    tm = min(tm, t)
    nf = f // tf
    rows = min(512, tm)

    def body(x1_ref, g_ref, wg_ref, wu_ref, wd_ref, h2_ref, gt_ref, up_ref, act_ref, x2_ref):
        j = pl.program_id(1)

        @pl.when(j == 0)
        def _():
            h2_ref[...] = _rms_fwd(x1_ref[...], g_ref[...]).astype(BF16)

            x2_ref[...] = x1_ref[...]

        for r0 in range(0, tm, rows):
            rs = slice(r0, r0 + rows)
            h2 = h2_ref[rs, :]
            gt = _dot_nt(h2, wg_ref[...])
            up = _dot_nt(h2, wu_ref[...])
            sg = _sigmoid(gt)
            silu = gt * sg
            act = (silu * up).astype(BF16)
            gt_ref[rs, :] = (up * (sg * (1.0 + gt * (1.0 - sg)))).astype(BF16)
            up_ref[rs, :] = silu.astype(BF16)
            act_ref[rs, :] = act
            x2_ref[rs, :] += _dot(act, wd_ref[...])

    return pl.pallas_call(
        body, name="ffn_fwd",
        out_shape=(jax.ShapeDtypeStruct((t, d), BF16), jax.ShapeDtypeStruct((t, f), BF16),
                   jax.ShapeDtypeStruct((t, f), BF16), jax.ShapeDtypeStruct((t, f), BF16),
                   jax.ShapeDtypeStruct((t, d), F32)),
        grid=(t // tm, nf),
        in_specs=[pl.BlockSpec((tm, d), lambda i, j: (i, 0)), pl.BlockSpec((1, d), lambda i, j: (0, 0)),
                  pl.BlockSpec((tf, d), lambda i, j: (j, 0)), pl.BlockSpec((tf, d), lambda i, j: (j, 0)),
                  pl.BlockSpec((tf, d), lambda i, j: (j, 0))],
        out_specs=(pl.BlockSpec((tm, d), lambda i, j: (i, 0)), pl.BlockSpec((tm, tf), lambda i, j: (i, j)),
                   pl.BlockSpec((tm, tf), lambda i, j: (i, j)), pl.BlockSpec((tm, tf), lambda i, j: (i, j)),
                   pl.BlockSpec((tm, d), lambda i, j: (i, 0))),
        compiler_params=_cparams("parallel", "arbitrary"),
    )(x1, g, wg, wu, wd)


def _final_fwd_bwd(x2, target, g, tm):
    t, d = x2.shape
    tm = min(tm, t)

    def body(x_ref, t_ref, g_ref, loss_ref, dx_ref, dg_ref):
        i = pl.program_id(0)
        x = x_ref[...]
        gg = g_ref[...]
        err = _rms_fwd(x, gg) - t_ref[...]
        part = 0.5 * jnp.sum(jnp.mean(err * err, axis=-1, keepdims=True), axis=0, keepdims=True)
        dx, dg = _rms_bwd(x, gg, err * (1.0 / d))
        dx_ref[...] = dx

        @pl.when(i == 0)
        def _():
            loss_ref[...] = jnp.zeros_like(loss_ref)
            dg_ref[...] = jnp.zeros_like(dg_ref)

        loss_ref[...] += jnp.broadcast_to(part, loss_ref.shape)
        dg_ref[...] += dg

    return pl.pallas_call(
        body, name="final_fwd_bwd",
        out_shape=(jax.ShapeDtypeStruct((1, LANES), F32), jax.ShapeDtypeStruct((t, d), F32),
                   jax.ShapeDtypeStruct((1, d), F32)),
        grid=(t // tm,),
        in_specs=[pl.BlockSpec((tm, d), lambda i: (i, 0)), pl.BlockSpec((tm, d), lambda i: (i, 0)),
                  pl.BlockSpec((1, d), lambda i: (0, 0))],
        out_specs=(pl.BlockSpec((1, LANES), lambda i: (0, 0)), pl.BlockSpec((tm, d), lambda i: (i, 0)),
                   pl.BlockSpec((1, d), lambda i: (0, 0))),
        compiler_params=_cparams("arbitrary"),
    )(x2, target, g)


def _ffn_bwd(dx2, x1, g, gt, up, wg, wu, wd, tm, tf):
    t, d = dx2.shape
    f = gt.shape[1]
    tm = min(tm, t)
    nf = f // tf
    wgu = jnp.concatenate([wg.reshape(nf, tf, d), wu.reshape(nf, tf, d)], axis=1).reshape(2 * f, d)
    rows = min(256, tm)

    def body(dx2_ref, x1_ref, g_ref, gt_ref, up_ref, wgu_ref, wd_ref, dgt_ref, dup_ref, dx1_ref, dg_ref, acc_ref,
             dxb_ref):
        i, j = pl.program_id(0), pl.program_id(1)

        @pl.when(j == 0)
        def _():
            dxb_ref[...] = dx2_ref[...].astype(BF16)
            acc_ref[...] = jnp.zeros_like(acc_ref)

        for r0 in range(0, tm, rows):
            rs = slice(r0, r0 + rows)
            dact = _dot_nt(dxb_ref[rs, :], wd_ref[...])
            dgt = (dact * gt_ref[rs, :].astype(F32)).astype(BF16)
            dup = (dact * up_ref[rs, :].astype(F32)).astype(BF16)
            dgt_ref[rs, :] = dgt
            dup_ref[rs, :] = dup
            acc_ref[rs, :] += _dot(jnp.concatenate([dgt, dup], axis=1), wgu_ref[...])

        @pl.when(jnp.logical_and(i == 0, j == 0))
        def _():
            dg_ref[...] = jnp.zeros_like(dg_ref)

        @pl.when(j == nf - 1)
        def _():
            dxn, dg = _rms_bwd(x1_ref[...], g_ref[...], acc_ref[...])
            dx1_ref[...] = dx2_ref[...] + dxn
            dg_ref[...] += dg

    return pl.pallas_call(
        body, name="ffn_bwd",
        out_shape=(jax.ShapeDtypeStruct((t, f), BF16), jax.ShapeDtypeStruct((t, f), BF16),
                   jax.ShapeDtypeStruct((t, d), F32), jax.ShapeDtypeStruct((1, d), F32)),
        grid=(t // tm, nf),
        in_specs=[pl.BlockSpec((tm, d), lambda i, j: (i, 0)), pl.BlockSpec((tm, d), lambda i, j: (i, 0)),
                  pl.BlockSpec((1, d), lambda i, j: (0, 0)),
                  pl.BlockSpec((tm, tf), lambda i, j: (i, j)), pl.BlockSpec((tm, tf), lambda i, j: (i, j)),
                  pl.BlockSpec((2 * tf, d), lambda i, j: (j, 0)), pl.BlockSpec((tf, d), lambda i, j: (j, 0))],
        out_specs=(pl.BlockSpec((tm, tf), lambda i, j: (i, j)), pl.BlockSpec((tm, tf), lambda i, j: (i, j)),
                   pl.BlockSpec((tm, d), lambda i, j: (i, 0)), pl.BlockSpec((1, d), lambda i, j: (0, 0))),
        scratch_shapes=[pltpu.VMEM((tm, d), F32), pltpu.VMEM((tm, d), BF16)],
        compiler_params=_cparams("arbitrary", "arbitrary"),
    )(dx2, x1, g, gt, up, wgu, wd)


def _merge_bwd(dx1, ps, o, g2, wpo, wao, wout, tm, dep=None):
    t, d = dx1.shape
    tm = min(tm, t)
    rows = min(ROW_CHUNK, tm)

    def body(dx1_ref, ps_ref, o_ref, gp_ref, ga_ref, wpo_ref, wao_ref, wout_ref, dpy_ref, day_ref, dg2_ref, dps_ref, da_ref):
        for r0 in range(0, tm, rows):
            rs = slice(r0, r0 + rows)
            dm = _dot_nt(dx1_ref[rs, :].astype(BF16), wout_ref[...])
            py = _dot(ps_ref[rs, :], wpo_ref[...])
            ay = _dot(o_ref[rs, :].astype(BF16), wao_ref[...])
            sp = _sigmoid(gp_ref[rs, :].astype(F32))
            sa = _sigmoid(ga_ref[rs, :].astype(F32))
            dpy = (dm * sp).astype(BF16)
            day = (dm * sa).astype(BF16)
            dpy_ref[rs, :] = dpy
            day_ref[rs, :] = day
            dg2_ref[rs, :d] = (dm * py * (sp * (1.0 - sp))).astype(BF16)
            dg2_ref[rs, d:] = (dm * ay * (sa * (1.0 - sa))).astype(BF16)
            dps_ref[rs, :] = _dot_nt(dpy, wpo_ref[...])
            da_ref[rs, :] = _dot_nt(day, wao_ref[...]).astype(BF16)

    row = lambda w: pl.BlockSpec((tm, w), lambda i: (i, 0))
    full = lambda a: pl.BlockSpec(a.shape, lambda i: (0, 0))
    dep_specs, dep_ops = _dep_args(dep)
    return pl.pallas_call(
        _after(body, 8, dep), name="merge_bwd",
        out_shape=(jax.ShapeDtypeStruct((t, d), BF16), jax.ShapeDtypeStruct((t, d), BF16),
                   jax.ShapeDtypeStruct((t, 2 * d), BF16), jax.ShapeDtypeStruct((t, POOL_WIDTH), F32),
                   jax.ShapeDtypeStruct((t, ATTN_WIDTH), BF16)),
        grid=(t // tm,),
        in_specs=[row(d), row(POOL_WIDTH), row(ATTN_WIDTH), pl.BlockSpec((tm, d), lambda i: (i, 0)),
                  pl.BlockSpec((tm, d), lambda i: (i, 1)), full(wpo), full(wao), full(wout)] + dep_specs,
        out_specs=(row(d), row(d), row(2 * d), row(POOL_WIDTH), row(ATTN_WIDTH)),
        compiler_params=_cparams("parallel"),
    )(dx1, ps, o, g2, g2, wpo, wao, wout, *dep_ops)


def _attn_bwd(qa, ka, v, do, lse4, seq, tq, dep=None):
    t = qa.shape[0]
    nq = seq // tq
    hp_n = N_HEADS // 2
    heads = [slice(e * LANES, (e + 1) * LANES) for e in range(2)]

    def body(q_ref, k_ref, v_ref, do_ref, lse_ref, dq_ref, dk_ref, dv_ref, dfr_ref, dk_acc, dv_acc, p_buf, dp_buf):
        diag_ok = lax.broadcasted_iota(jnp.int32, (tq, tq), 0) >= lax.broadcasted_iota(jnp.int32, (tq, tq), 1)
        lane_q = lax.broadcasted_iota(jnp.int32, (tq, LANES), 1)
        mine_q = [lane_q < HEAD_DIM, lane_q >= HEAD_DIM]
        dv_acc[...] = jnp.zeros_like(dv_acc)
        dk_acc[...] = jnp.zeros_like(dk_acc)
        dfr_ref[...] = jnp.zeros_like(dfr_ref)
        transposed = lambda a: a.astype(F32).T.astype(BF16)

        def q_step(i, _):
            q0 = pl.multiple_of(i * tq, tq)
            qs = [q_ref[pl.ds(q0, tq), hl] for hl in heads]
            dov = do_ref[pl.ds(q0, tq), :]
            dos = [jnp.where(mq, dov, jnp.zeros((), BF16)) for mq in mine_q]
            qts = [transposed(q) for q in qs]
            dots = [transposed(a) for a in dos]
            lss = [lse_ref[pl.ds(q0, tq), e:e + 1] for e in range(2)]

            def sweep1(j, dls, diagonal):
                r0 = pl.multiple_of(j * tq, tq)
                vv = v_ref[pl.ds(r0, tq), :]
                out = []
                for e, hl in enumerate(heads):
                    s = _dot_nt(qs[e], k_ref[pl.ds(r0, tq), hl])
                    if diagonal:
                        s = jnp.where(diag_ok, s, NEG_BIG)
                    p = jnp.exp(s - lss[e])
                    dp = _dot_nt(dos[e], vv)
                    p_buf[e, j] = p
                    dp_buf[e, j] = dp
                    dv_acc[j] += _dot(dots[e], p.astype(BF16))
                    out.append(dls[e] + _fold_lanes(p * dp, jnp.add))
                return tuple(out)

            dls = _causal_sweep(i, sweep1, (jnp.zeros((tq, LANES), F32),) * 2)
            dls = [jnp.sum(d, axis=1, keepdims=True) for d in dls]

            def sweep2(j, dqs, diagonal):
                r0 = pl.multiple_of(j * tq, tq)
                out = []
                for e, hl in enumerate(heads):
                    ds = p_buf[e, j] * (dp_buf[e, j] - dls[e])
                    dfr_ref[e, pl.ds(j, 1), :] += jnp.sum(ds, axis=0, keepdims=True)
                    dsb = ds.astype(BF16)
                    dk_acc[e, j] += _dot(qts[e], dsb)
                    out.append(dqs[e] + _dot(dsb, k_ref[pl.ds(r0, tq), hl]))
                return tuple(out)

            dqs = _causal_sweep(i, sweep2, (jnp.zeros((tq, LANES), F32),) * 2)
            dq = jnp.where(mine_q[0], dqs[0], pltpu.roll(dqs[1], HEAD_DIM, 1)) * ATTN_SCALE
            dq_ref[pl.ds(q0, tq), :] = dq.astype(BF16)
            return 0

        lax.fori_loop(0, nq, q_step, 0)
        for j in range(nq):
            rs = slice(j * tq, (j + 1) * tq)
            dk = jnp.where(mine_q[0], dk_acc[0, j].T, pltpu.roll(dk_acc[1, j].T, HEAD_DIM, 1))
            dk_ref[rs, :] = dk.astype(BF16)
            dv_ref[rs, :] = dv_acc[j].T.astype(BF16)

    wide = pl.BlockSpec((seq, 2 * LANES), lambda b, hp: (b, hp))
    col = pl.BlockSpec((seq, LANES), lambda b, hp: (b, hp))
    pair = pl.BlockSpec((None, seq, 2), lambda b, hp: (hp, b, 0))
    dep_specs, dep_ops = _dep_args(dep)
    return pl.pallas_call(
        _after(body, 5, dep), name="attn_bwd",
        out_shape=(jax.ShapeDtypeStruct((t, ATTN_WIDTH), BF16),) * 3 + (jax.ShapeDtypeStruct((N_HEADS, t // tq, tq), F32),),
        grid=(t // seq, hp_n),
        in_specs=[wide, wide, col, col, pair] + dep_specs,
        out_specs=(col, col, col, pl.BlockSpec((2, nq, tq), lambda b, hp: (hp, b, 0))),
        scratch_shapes=[pltpu.VMEM((2, nq, LANES, tq), F32), pltpu.VMEM((nq, LANES, tq), F32),
                        pltpu.VMEM((2, nq, tq, tq), F32), pltpu.VMEM((2, nq, tq, tq), F32)],
        compiler_params=_cparams("parallel", "arbitrary"),
    )(qa, ka, v, do, lse4, *dep_ops)


def _forget_bwd(dfc, fl, bf, seq):
    t = fl.shape[0]
    cb = min(256, seq)
    nb = seq // cb

    def body(dfc_ref, fl_ref, bf_ref, dfl_ref, db_ref):
        b = pl.program_id(0)
        ri = lax.broadcasted_iota(jnp.int32, (cb, cb), 0)
        ci = lax.broadcasted_iota(jnp.int32, (cb, cb), 1)
        tri = (ci >= ri).astype(BF16)
        carry = jnp.zeros((1, LANES), F32)
        dbs = jnp.zeros((1, LANES), F32)
        for blk in reversed(range(nb)):
            rs = slice(blk * cb, (blk + 1) * cb)
            dlf = _tri_dot(tri, -dfc_ref[rs, :]) + carry
            carry = dlf[0:1, :]
            dfl = dlf * _sigmoid(-(fl_ref[rs, :] + bf_ref[...]))
            dfl_ref[rs, :] = dfl.astype(BF16)
            dbs = dbs + jnp.sum(dfl, axis=0, keepdims=True)

        @pl.when(b == 0)
        def _():
            db_ref[...] = jnp.zeros_like(db_ref)

        db_ref[...] += dbs

    return pl.pallas_call(
        body, name="forget_bwd",
        out_shape=(jax.ShapeDtypeStruct((t, LANES), BF16), jax.ShapeDtypeStruct((1, LANES), F32)),
        grid=(t // seq,),
        in_specs=[pl.BlockSpec((seq, LANES), lambda b: (b, 0)), pl.BlockSpec((seq, LANES), lambda b: (b, 0)),
                  pl.BlockSpec((1, LANES), lambda b: (0, 0))],
        out_specs=(pl.BlockSpec((seq, LANES), lambda b: (b, 0)), pl.BlockSpec((1, LANES), lambda b: (0, 0))),
        compiler_params=_cparams("arbitrary"),
    )(dfc, fl, bf)


def _pool_bwd(dps, p, mix, scale, seq):
    t = dps.shape[0]

    def body(dps_ref, p_ref, mix_ref, sc_ref, du_ref, dmix_ref, dsc_ref):
        b = pl.program_id(0)

        @pl.when(b == 0)
        def _():
            dmix_ref[...] = jnp.zeros_like(dmix_ref)
            dsc_ref[...] = jnp.zeros_like(dsc_ref)

        tpos = lax.broadcasted_iota(jnp.int32, (seq, POOL_GROUP_DIM), 0)
        for g in range(POOL_GROUPS):
            sl = slice(g * POOL_GROUP_DIM, (g + 1) * POOL_GROUP_DIM)
            pb = p_ref[:, sl]
            dpsg = dps_ref[:, sl]
            pm = _dot(pb, mix_ref[g])
            dsc_ref[:, sl] += jnp.sum(dpsg * pm, axis=0, keepdims=True)
            dpm = (dpsg * sc_ref[:, sl]).astype(BF16)
            dmix_ref[g] += _dot_tn(pb, dpm)
            dp = _dot_nt(dpm, mix_ref[g])
            cnt = jnp.minimum(tpos + 1, POOL_WINDOWS[g]).astype(F32)
            s = dp / cnt
            for lvl in range(g + 1):
                d = 2 ** lvl
                s = s + jnp.where(tpos < seq - d, pltpu.roll(s, seq - d, 0), 0.0)
            du_ref[:, sl] = (s - dp).astype(BF16)

    return pl.pallas_call(
        body, name="pool_bwd",
        out_shape=(jax.ShapeDtypeStruct((t, POOL_WIDTH), BF16),
                   jax.ShapeDtypeStruct((POOL_GROUPS, POOL_GROUP_DIM, POOL_GROUP_DIM), F32),
                   jax.ShapeDtypeStruct((1, POOL_WIDTH), F32)),
        grid=(t // seq,),
        in_specs=[pl.BlockSpec((seq, POOL_WIDTH), lambda b: (b, 0)), pl.BlockSpec((seq, POOL_WIDTH), lambda b: (b, 0)),
                  pl.BlockSpec((POOL_GROUPS, POOL_GROUP_DIM, POOL_GROUP_DIM), lambda b: (0, 0, 0)),
                  pl.BlockSpec((1, POOL_WIDTH), lambda b: (0, 0))],
        out_specs=(pl.BlockSpec((seq, POOL_WIDTH), lambda b: (b, 0)),
                   pl.BlockSpec((POOL_GROUPS, POOL_GROUP_DIM, POOL_GROUP_DIM), lambda b: (0, 0, 0)),
                   pl.BlockSpec((1, POOL_WIDTH), lambda b: (0, 0))),
        compiler_params=_cparams("arbitrary"),
    )(dps, p, mix, scale)


def _in_bwd(du, dq, dk, dv, dg2, dfl, dx1, x, g, wu, wqkv, wg2, wft, tm):
    t, d = x.shape
    tm = min(tm, t)
    rows = min(ROW_CHUNK, tm)
    aw = ATTN_WIDTH

    def body(du_ref, dq_ref, dk_ref, dv_ref, dg2_ref, dfl_ref, dx1_ref, x_ref, g_ref, wu_ref, wqkv_ref, wg2_ref, wft_ref,
             dx_ref, dg_ref):
        i = pl.program_id(0)

        @pl.when(i == 0)
        def _():
            dg_ref[...] = jnp.zeros_like(dg_ref)

        for r0 in range(0, tm, rows):
            rs = slice(r0, r0 + rows)
            dh = _dot(du_ref[rs, :], wu_ref[...])
            dh += _dot(dq_ref[rs, :], wqkv_ref[0:aw, :])
            dh += _dot(dk_ref[rs, :], wqkv_ref[aw:2 * aw, :])
            dh += _dot(dv_ref[rs, :], wqkv_ref[2 * aw:3 * aw, :])
            dh += _dot(dg2_ref[rs, :], wg2_ref[...])
            dh += _dot(dfl_ref[rs, :], wft_ref[...])
            dxn, dg = _rms_bwd(x_ref[rs, :], g_ref[...], dh)
            dx_ref[rs, :] = dx1_ref[rs, :] + dxn
            dg_ref[...] += dg

    row = lambda w: pl.BlockSpec((tm, w), lambda i: (i, 0))
    full = lambda a: pl.BlockSpec(a.shape, lambda i: (0, 0))
    return pl.pallas_call(
        body, name="in_bwd",
        out_shape=(jax.ShapeDtypeStruct((t, d), F32), jax.ShapeDtypeStruct((1, d), F32)),
        grid=(t // tm,),
        in_specs=[row(POOL_WIDTH), row(aw), row(aw), row(aw), row(2 * d), row(LANES), row(d), row(d),
                  pl.BlockSpec((1, d), lambda i: (0, 0)), full(wu), full(wqkv), full(wg2), full(wft)],
        out_specs=(row(d), pl.BlockSpec((1, d), lambda i: (0, 0))),
        compiler_params=_cparams("arbitrary"),
    )(du, dq, dk, dv, dg2, dfl, dx1, x, g, wu, wqkv, wg2, wft)


def _position():
    return lax.axis_index("x"), lax.axis_index("y"), lax.axis_index("c")


def _remote(src, dst, send_sem, recv_sem, device):
    return pltpu.make_async_remote_copy(src_ref=src, dst_ref=dst, send_sem=send_sem, recv_sem=recv_sem,
                                        device_id=device, device_id_type=MESH)


HBM = pl.BlockSpec(memory_space=pltpu.HBM)
SEM = pl.BlockSpec(memory_space=pltpu.SEMAPHORE)
DATAFLOW = pltpu.SideEffectType.DATAFLOW_SIDE_EFFECTING


def _copies_start(name, arrays, plan, m, dep=None):
    n = len(arrays)
    arrays = [pltpu.with_memory_space_constraint(a, pltpu.HBM) for a in arrays]

    def body(*refs):
        ins, send_sem, recv_sem, token = refs[:n], refs[n], refs[n + 1], refs[2 * n + 2]
        for i, (src, dst, device, _) in enumerate(plan(ins, *_position())):
            _remote(src, dst, send_sem.at[i], recv_sem.at[i], device).start()
        token[...] = jnp.zeros_like(token)

    dep_specs, dep_ops = _dep_args(dep)
    outs = pl.pallas_call(
        _after(body, n, dep), name=name,
        out_shape=(pltpu.SemaphoreType.DMA((m,)), pltpu.SemaphoreType.DMA((m,)),
                   *[pltpu.HBM(a.shape, a.dtype) for a in arrays], jax.ShapeDtypeStruct((8, LANES), F32)),
        in_specs=[HBM] * n + dep_specs, out_specs=(SEM, SEM, *[HBM] * n, pl.BlockSpec(memory_space=pltpu.VMEM)),
        input_output_aliases={i: i + 2 for i in range(n)},
        compiler_params=pltpu.CompilerParams(has_side_effects=DATAFLOW),
    )(*arrays, *dep_ops)
    return (outs[0], outs[1]), list(outs[2:2 + n]), outs[2 + n]


def _copies_wait(name, sems, arrays, plan, after):
    n = len(arrays)
    afters = list(after) if isinstance(after, (list, tuple)) else [after]

    def body(*refs):
        ins, send_sem, recv_sem = refs[:n], refs[n], refs[n + 1]
        for i, (src, dst, device, landing) in enumerate(plan(ins, *_position())):
            _remote(src, dst, send_sem.at[i], recv_sem.at[i], device).wait_send()
            _remote(landing, landing, send_sem.at[i], recv_sem.at[i], device).wait_recv()

    outs = pl.pallas_call(
        body, name=name,
        out_shape=tuple(pltpu.HBM(a.shape, a.dtype) for a in arrays),
        in_specs=[HBM] * n + [SEM, SEM] + [ANY] * len(afters), out_specs=tuple([HBM] * n),
        input_output_aliases={i: i for i in range(n)},
        compiler_params=pltpu.CompilerParams(has_side_effects=DATAFLOW),
    )(*arrays, sems[0], sems[1], *afters)
    return list(outs)


def _tie(x, dep):
    for token in _dep_list(dep):
        x = x + token[0, 0]
    return x


def _other_chips(x, y):
    return [(1 - x, y), (x, 1 - y), (1 - x, 1 - y)]


def _gather_begin(tag, shards, token, column_halves=False):
    n = len(shards)
    lands = [lax.empty((N_CHIPS,) + s.shape, s.dtype) for s in shards]
    if column_halves:
        cols = lambda ref, h: pl.ds(pl.multiple_of(h * (ref.shape[-1] // 2), LANES), ref.shape[-1] // 2)
        mine = lambda ref, h: ref.at[:, cols(ref, h)]
        landed = lambda ref, chip, h: ref.at[chip, :, cols(ref, h)]
    else:
        mine = lambda ref, h: ref.at[h]
        landed = lambda ref, chip, h: ref.at[chip, h]

    def plan(refs, x, y, c):
        return [(mine(refs[k], c), landed(refs[n + k], 2 * x + y, c), (ox, oy, c), landed(refs[n + k], 2 * ox + oy, c))
                for k in range(n) for ox, oy in _other_chips(x, y)]

    sems, thru, token = _copies_start(f"gather_{tag}_ici_start", list(shards) + lands, plan, 3 * n, dep=token)
    return dict(tag=tag, n=n, plan=plan, sems=sems, arrays=thru, token=token, landed=landed)


def _gather_forward(st, after):
    n, tag, landed = st["n"], st["tag"], st["landed"]
    thru = _copies_wait(f"gather_{tag}_ici_wait", st["sems"], st["arrays"], st["plan"], after)

    def plan(refs, x, y, c):
        return [(landed(refs[k], 2 * ox + oy, c), landed(refs[k], 2 * ox + oy, c), (x, y, 1 - c),
                 landed(refs[k], 2 * ox + oy, 1 - c))
                for k in range(n) for ox, oy in _other_chips(x, y)]

    sems, lands, token = _copies_start(f"gather_{tag}_fwd_start", thru[n:], plan, 3 * n)
    return dict(tag=tag, n=n, plan=plan, sems=sems, arrays=lands, token=token, shards=thru[:n])


def _gather_end(st, after, merge=True):
    lands = _copies_wait(f"gather_{st['tag']}_fwd_wait", st["sems"], st["arrays"], st["plan"], after)
    if not merge:
        return lands, st["shards"]
    me = 2 * lax.axis_index("x") + lax.axis_index("y")
    return [lax.dynamic_update_index_in_dim(g, s, me, 0) for g, s in zip(lands, st["shards"])]


def _add_keep_give(name, pos, a, a_keep, a_give, b, b_keep, b_give, steps):
    r, c = b.shape[-2:]

    def spec(arr, fn):
        lead = arr.ndim - 2

        def index(i, p):
            idx = tuple(fn(i, p))
            return idx if len(idx) == arr.ndim else idx + (0, 0)

        return pl.BlockSpec((None,) * lead + (r, c), index)

    out_spec = pl.BlockSpec((None, r, c), lambda i, p: (i, 0, 0))

    def body(p_ref, ak_ref, bk_ref, ag_ref, bg_ref, keep_ref, give_ref):
        keep_ref[...] = ak_ref[...] + bk_ref[...].astype(F32)
        give_ref[...] = (ag_ref[...] + bg_ref[...].astype(F32)).astype(BF16)

    return pl.pallas_call(
        body, name=name,
        out_shape=(jax.ShapeDtypeStruct((steps, r, c), F32), jax.ShapeDtypeStruct((steps, r, c), BF16)),
        grid_spec=pltpu.PrefetchScalarGridSpec(
            num_scalar_prefetch=1, grid=(steps,),
            in_specs=[spec(a, a_keep), spec(b, b_keep), spec(a, a_give), spec(b, b_give)],
            out_specs=(out_spec, out_spec)),
        compiler_params=_cparams("parallel"),
    )(pos, a, b, a, b)


def _add_last(name, a, b):
    _, r, c = a.shape
    blk = pl.BlockSpec((None, r, c), lambda i: (0, 0, 0))

    def body(a_ref, b_ref, o_ref):
        o_ref[...] = a_ref[...] + b_ref[...].astype(F32)

    return pl.pallas_call(
        body, name=name, out_shape=jax.ShapeDtypeStruct((r, c), F32), grid=(1,), in_specs=[blk, blk],
        out_specs=pl.BlockSpec((r, c), lambda i: (0, 0)), compiler_params=_cparams("arbitrary"),
    )(a, b)


def _exchange_part(gives, lands, peer_fn):
    n = len(gives)

    def plan(refs, x, y, c):
        return [(refs[k], refs[n + k], peer_fn(x, y, c), refs[n + k]) for k in range(n)]

    return list(gives) + list(lands), plan, n


def _join_parts(parts):
    offsets, total = [], 0
    for arrays, _, _ in parts:
        offsets.append(total)
        total += len(arrays)

    def plan(refs, x, y, c):
        copies = []
        for (arrays, part_plan, _), off in zip(parts, offsets):
            copies += part_plan(refs[off:off + len(arrays)], x, y, c)
        return copies

    return [a for arrays, _, _ in parts for a in arrays], plan, sum(m for _, _, m in parts)


def _reduce_begin(tag, grads, column_halves=False):
    n = len(grads)
    if column_halves:
        half = lambda ref, j, h: ref.at[j, :, pl.ds(pl.multiple_of(h * (ref.shape[2] // 2), LANES), ref.shape[2] // 2)]
        lands = [lax.empty((N_CHIPS, g.shape[1], g.shape[2] // 2), F32) for g in grads]
    else:
        half = lambda ref, j, h: ref.at[j, h]
        lands = [lax.empty((N_CHIPS,) + g.shape[2:], F32) for g in grads]

    def plan(refs, x, y, c):
        return [(half(refs[k], j, 1 - c), refs[n + k].at[j], (x, y, 1 - c), refs[n + k].at[j])
                for k in range(n) for j in range(N_CHIPS)]

    return dict(tag=tag, n=n, stage="c", grads=list(grads), column_halves=column_halves,
                part=(list(grads) + lands, plan, N_CHIPS * n))


def _reduce_next(st, thru):
    tag, n, stage = st["tag"], st["n"], st["stage"]
    first, recv = thru[:n], thru[n:]
    x, y, c = _position()
    if stage == "c":
        pos = jnp.stack([c, x]).astype(jnp.int32)
        if st["column_halves"]:
            mine = lambda chip: (lambda i, p: (chip(p) + i, 0, p[0]))
        else:
            mine = lambda chip: (lambda i, p: (chip(p) + i, p[0]))
        sums = [_add_keep_give(
            f"rs{tag}_c_add{k}", pos,
            first[k], mine(lambda p: 2 * p[1]), mine(lambda p: 2 * (1 - p[1])),
            recv[k], lambda i, p: (2 * p[1] + i,), lambda i, p: (2 * (1 - p[1]) + i,), 2) for k in range(n)]
        lands = [lax.empty(s[1].shape, BF16) for s in sums]
        return dict(tag=tag, n=n, stage="x", keep=[s[0] for s in sums],
                    part=_exchange_part([s[1] for s in sums], lands, lambda x, y, c: (1 - x, y, c)))
    if stage == "x":
        pos = jnp.stack([y]).astype(jnp.int32)
        sums = [_add_keep_give(
            f"rs{tag}_x_add{k}", pos,
            st["keep"][k], lambda i, p: (p[0],), lambda i, p: (1 - p[0],),
            recv[k], lambda i, p: (p[0],), lambda i, p: (1 - p[0],), 1) for k in range(n)]
        lands = [lax.empty(s[1].shape, BF16) for s in sums]
        return dict(tag=tag, n=n, stage="y", keep=[s[0] for s in sums],
                    part=_exchange_part([s[1] for s in sums], lands, lambda x, y, c: (x, 1 - y, c)))
    if stage == "y":
        mine = [_add_last(f"rs{tag}_y_add{k}", st["keep"][k], recv[k]) for k in range(n)]
        lands = [lax.empty(m.shape, F32) for m in mine]
        return dict(tag=tag, n=n, stage="swap", part=_exchange_part(mine, lands, lambda x, y, c: (x, y, 1 - c)))
    return dict(tag=tag, done=list(zip(first, recv)))


def _small_begin(tag, v):
    land = lax.empty((N_DEV,) + v.shape, F32)
    flips = [(fx, fy, fc) for fx in (0, 1) for fy in (0, 1) for fc in (0, 1)][1:]

    def plan(refs, x, y, c):
        copies = []
        for fx, fy, fc in flips:
            px, py, pc = (1 - x if fx else x), (1 - y if fy else y), (1 - c if fc else c)
            copies.append((refs[0], refs[1].at[4 * x + 2 * y + c], (px, py, pc), refs[1].at[4 * px + 2 * py + pc]))
        return copies

    return dict(tag=tag, n=1, stage="swap", grads=[v], part=([v, land], plan, len(flips)))


def _small_sum(name, own, land):
    x, y, c = _position()
    me = jnp.stack([4 * x + 2 * y + c]).astype(jnp.int32)

    def body(me_ref, own_ref, land_ref, out_ref):
        term = lambda dev: jnp.where(me_ref[0] == dev, own_ref[...], land_ref[dev])
        acc = term(0)
        for dev in range(1, N_DEV):
            acc = acc + term(dev)
        out_ref[...] = acc

    return pl.pallas_call(
        body, name=name, out_shape=jax.ShapeDtypeStruct(own.shape, F32),
        grid_spec=pltpu.PrefetchScalarGridSpec(
            num_scalar_prefetch=1, grid=(1,),
            in_specs=[pl.BlockSpec(own.shape, lambda i, m: (0, 0)), pl.BlockSpec(land.shape, lambda i, m: (0, 0, 0))],
            out_specs=pl.BlockSpec(own.shape, lambda i, m: (0, 0))),
        compiler_params=_cparams("arbitrary"),
    )(me, own, land)


def _adamw_update(w, gg, m, v):
    mn = ADAM_B1 * m + (1.0 - ADAM_B1) * gg
    vn = ADAM_B2 * v + (1.0 - ADAM_B2) * (gg * gg)
    m_hat = mn / (1.0 - ADAM_B1 ** ADAM_STEP)
    v_hat = vn / (1.0 - ADAM_B2 ** ADAM_STEP)
    return -ADAM_LR * (m_hat / (jnp.sqrt(v_hat) + ADAM_EPS) + ADAM_WD * w), mn, vn


def _adamw(name, w, g, m, v):
    def body(w_ref, g_ref, m_ref, v_ref, d_ref, mo_ref, vo_ref):
        d_ref[...], mo_ref[...], vo_ref[...] = _adamw_update(w_ref[...], g_ref[...], m_ref[...], v_ref[...])

    blk = pl.BlockSpec(w.shape, lambda i: (0, 0))
    return pl.pallas_call(
        body, name=name, out_shape=(jax.ShapeDtypeStruct(w.shape, F32),) * 3, grid=(1,),
        in_specs=[blk] * 4, out_specs=(blk,) * 3, compiler_params=_cparams("arbitrary"),
    )(w, g, m, v)


def _rows_to_bf16(name, w):
    r, _, c = w.shape

    def body(w_ref, o_ref):
        o_ref[...] = w_ref[:, 0, :].astype(BF16)

    return pl.pallas_call(
        body, name=name, out_shape=jax.ShapeDtypeStruct((r, c), BF16), grid=(1,),
        in_specs=[pl.BlockSpec((r, 1, c), lambda i: (0, 0, 0))], out_specs=pl.BlockSpec((r, c), lambda i: (0, 0)),
        compiler_params=_cparams("arbitrary"),
    )(w)


def _adamw_rows(name, pos_c, w, g_mine, g_other, m, v):
    r, _, c = w.shape
    ch = c // 2

    def body(p_ref, w_ref, gm_ref, go_ref, m_ref, v_ref, g_ref, d_ref, mo_ref, vo_ref):
        gg = jnp.where(pl.program_id(0) == p_ref[0], gm_ref[...], go_ref[...])
        dl, mn, vn = _adamw_update(w_ref[:, 0, :], gg, m_ref[:, 0, :], v_ref[:, 0, :])
        g_ref[:, 0, :] = gg
        d_ref[:, 0, :] = dl
        mo_ref[:, 0, :] = mn
        vo_ref[:, 0, :] = vn

    rows = pl.BlockSpec((r, 1, ch), lambda h, p: (0, 0, h))
    half = pl.BlockSpec((r, ch), lambda h, p: (0, 0))
    return pl.pallas_call(
        body, name=name, out_shape=(jax.ShapeDtypeStruct(w.shape, F32),) * 4,
        grid_spec=pltpu.PrefetchScalarGridSpec(
            num_scalar_prefetch=1, grid=(2,), in_specs=[rows, half, half, rows, rows], out_specs=(rows,) * 4),
        compiler_params=_cparams("parallel"),
    )(pos_c, w, g_mine, g_other, m, v)


def _adamw_halves(name, pos_c, w, g_mine, g_other, m, v, tr, dep=None):
    r, c = w.shape
    rh = r // 2
    tr = tr if rh % tr == 0 else rh
    nt = rh // tr

    def body(p_ref, w_ref, gm_ref, go_ref, m_ref, v_ref, g_ref, d_ref, mo_ref, vo_ref):
        gg = jnp.where(pl.program_id(0) == p_ref[0], gm_ref[...], go_ref[...])
        g_ref[...] = gg
        d_ref[...], mo_ref[...], vo_ref[...] = _adamw_update(w_ref[...], gg, m_ref[...], v_ref[...])

    full = pl.BlockSpec((tr, c), lambda h, i, p: (h * nt + i, 0))
    half = pl.BlockSpec((tr, c), lambda h, i, p: (i, 0))
    dep_specs, dep_ops = _dep_args(dep)
    return pl.pallas_call(
        _after(body, 6, dep), name=name, out_shape=(jax.ShapeDtypeStruct((r, c), F32),) * 4,
        grid_spec=pltpu.PrefetchScalarGridSpec(
            num_scalar_prefetch=1, grid=(2, nt),
            in_specs=[full, half, half, full, full] + dep_specs, out_specs=(full,) * 4),
        compiler_params=_cparams("parallel", "parallel"),
    )(pos_c, w, g_mine, g_other, m, v, *dep_ops)


def _col_sharded_to_comm(g):
    k, n = g.shape
    return g.reshape(2, k // 2, N_CHIPS, n // N_CHIPS).transpose(2, 0, 1, 3)


def _row_sharded_to_comm(g):
    r, c = g.shape
    return g.reshape(N_CHIPS, 2, r // (2 * N_CHIPS), c)


def _col_sharded_full(g):
    _, _, rh, c = g.shape
    return g.reshape(N_CHIPS, 2 * rh, c).transpose(1, 0, 2).reshape(2 * rh, N_CHIPS * c)


def _row_sharded_full(g):
    _, _, rh, c = g.shape
    return g.reshape(N_CHIPS * 2 * rh, c)


def _chip_rows(w3, start, stop, own=None, me=None):
    r = w3.shape[1]
    parts = []
    for chip in range(N_CHIPS):
        lo, hi = max(start - chip * r, 0), min(stop - chip * r, r)
        if lo < hi:
            part = w3[chip, lo:hi]
            parts.append(part if own is None else jnp.where(me == chip, own[lo:hi], part))
    return parts[0] if len(parts) == 1 else jnp.concatenate(parts, axis=0)


def _pack_small(g1, bfv, mix, scale, g2n, gf, extra=None):
    row8 = jnp.pad(bfv.reshape(1, N_HEADS), ((0, 0), (0, LANES - N_HEADS)))
    if extra is not None:
        row8 = row8 + jnp.pad(extra[:, :1], ((0, 0), (N_HEADS, LANES - N_HEADS - 1)))
    return jnp.concatenate([
        g1.reshape(8, LANES), jnp.pad(row8, ((0, 7), (0, 0))), mix.reshape(512, LANES),
        jnp.pad(scale.reshape(4, LANES), ((0, 4), (0, 0))), g2n.reshape(8, LANES), gf.reshape(8, LANES)], axis=0)


def _unpack_small(s, like):
    g1, bfv, mix, scale, g2n, gf = like
    return (s[0:8].reshape(g1.shape), s[8, :N_HEADS].reshape(bfv.shape), s[16:528].reshape(mix.shape),
            s[528:532].reshape(scale.shape), s[536:544].reshape(g2n.shape), s[544:552].reshape(gf.shape))


class _MeshLinks:
    def __init__(self, shards_in, shards_rest):
        self.gin = _gather_begin("in", shards_in, None, column_halves=True)
        self.grest = _gather_begin("rest", shards_rest, self.gin["token"])
        self.tokens = {"gather": self.grest["token"]}
        self.groups, self.flight, self.slot = {}, None, 0

    @property
    def token(self):
        return list(self.tokens.values())

    def tie(self, x):
        return _tie(x, self.token)

    def weights_in(self, after):
        st = _gather_forward(self.gin, after)
        (g,), (own,) = _gather_end(st, st["token"], merge=False)
        return g, own, 2 * lax.axis_index("x") + lax.axis_index("y")

    def rest_forward(self, after):
        self.grest = _gather_forward(self.grest, after)
        self.tokens["gather"] = self.grest["token"]

    def weights_rest(self, after):
        g = _gather_end(self.grest, after)
        del self.tokens["gather"]
        return [_col_sharded_full(g[0]), _col_sharded_full(g[1])] + [_row_sharded_full(a) for a in g[2:]]

    def advance(self, after, begin=()):
        slot = self.slot
        self.slot += 1
        if self.flight is not None:
            tags, sems, parts = self.flight
            arrays, plan, _ = _join_parts(parts)
            thru = _copies_wait(f"slot{slot}_wait", sems, arrays, plan, after)
            for tag, part in zip(tags, parts):
                self.groups[tag] = _reduce_next(self.groups[tag], thru[:len(part[0])])
                thru = thru[len(part[0]):]
        for st in begin:
            self.groups[st["tag"]] = st
        live = [(tag, st["part"]) for tag, st in self.groups.items() if "part" in st]
        self.flight = None
        self.tokens.pop("reduce", None)
        if live:
            arrays, plan, m = _join_parts([part for _, part in live])
            sems, thru, token = _copies_start(f"slot{slot}_start", arrays, plan, m)
            parts = []
            for _, (part_arrays, part_plan, part_m) in live:
                parts.append((thru[:len(part_arrays)], part_plan, part_m))
                thru = thru[len(part_arrays):]
            self.flight = ([tag for tag, _ in live], sems, parts)
            self.tokens["reduce"] = token

    def reduced(self, tag):
        return self.groups[tag]["done"]


class _NoLinks:
    token = None

    def __init__(self, w_in, rest):
        self.w_in, self.rest, self.grads = w_in, rest, {}

    def tie(self, x):
        return x

    def weights_in(self, after):
        return self.w_in, None, None

    def rest_forward(self, after):
        pass

    def weights_rest(self, after):
        return self.rest

    def advance(self, after, begin=()):
        for st in begin:
            self.grads[st["tag"]] = st["grads"]


def _local_step(links, x, target, seq, norm1_g, b_forget, pool_mix, pool_scale, norm2_g, norm_f_g, between=None):
    t, d = x.shape
    tq = min(256, seq)
    aw = ATTN_WIDTH
    o_q, o_f, o_g = POOL_WIDTH, POOL_WIDTH + 3 * aw, POOL_WIDTH + 3 * aw + N_HEADS
    bf = jnp.pad(b_forget, ((0, 0), (0, LANES - N_HEADS)))
    mixb = pool_mix.astype(BF16)

    h = _norm_fwd("norm1_fwd", x, links.tie(norm1_g), 512)
    w_in, own, me = links.weights_in(h)
    wu = _chip_rows(w_in, 0, o_q, own, me)
    wqkv = _chip_rows(w_in, o_q, o_f, own, me)
    wft = jnp.pad(_chip_rows(w_in, o_f, o_g, own, me), ((0, LANES - N_HEADS), (0, 0)))
    wg2 = _chip_rows(w_in, o_g, N_CHIPS * w_in.shape[1], own, me)
    wf = wft.T
    u = _matmul("mm_u", h, wu, "nt", F32, 1024, 512, d)
    g2 = _matmul("mm_gates", h, wg2, "nt", BF16, 1024, 1024, d)
    fl, fcum = _forget_fwd(h, wf, bf, seq)
    qa, ka, v = _attn_prep(h, _head_blocks(wqkv[:aw]), _head_blocks(wqkv[aw:2 * aw]), wqkv[2 * aw:], fcum, 1024)
    p, ps = _pool_fwd(u, mixb, pool_scale, seq)
    links.rest_forward([ps, qa, g2])
    o, lse = _attn_fwd(qa, ka, v, seq, tq, dep=links.token)
    w_pool_out, w_attn_out, w_out, w_ffn_gate, w_ffn_up, w_ffn_down = links.weights_rest(o)
    merged, x1 = _merge_fwd(x, ps, o, g2, w_pool_out, w_attn_out, w_out, 512)
    h2, gt, up, act, x2 = _ffn_fwd(x1, norm2_g, w_ffn_gate, w_ffn_up, w_ffn_down, 1024, 256)
    loss, dx2, d_gf = _final_fwd_bwd(x2, target, norm_f_g, 512)

    dgt, dup, dx1, d_g2n = _ffn_bwd(dx2, x1, norm2_g, gt, up, w_ffn_gate, w_ffn_up, w_ffn_down, 1024, 256)
    d_wd = _matmul("dw_down", act, dx2, "tn", F32, 1408, 1024, 1024)
    d_wg = _matmul("dw_gate", dgt, h2, "tn", F32, 1408, 1024, 1024)
    d_wu = _matmul("dw_up", dup, h2, "tn", F32, 1408, 1024, 1024)
    links.advance(None, begin=[_reduce_begin("a", [_row_sharded_to_comm(g) for g in (d_wg, d_wu, d_wd)])])
    dpy, day, dg2, dps, da = _merge_bwd(dx1, ps, o, g2, w_pool_out, w_attn_out, w_out, 512, dep=links.token)
    links.advance(dps)
    d_wout = _matmul("dw_out", merged, dx1, "tn", F32, 1024, 1024, 1024)
    d_wpo = _matmul("dw_pool_out", ps, dpy, "tn", F32, 512, 1024, 1024)
    d_wao = _matmul("dw_attn_out", o, day, "tn", F32, 512, 1024, 1024)
    dq, dk, dv, dfr = _attn_bwd(qa, ka, v, da, lse, seq, tq, dep=links.token)
    links.advance(dq, begin=[_reduce_begin(
        "m", [_col_sharded_to_comm(d_wpo), _col_sharded_to_comm(d_wao), _row_sharded_to_comm(d_wout)])])
    dfc = jnp.pad(dfr.reshape(N_HEADS, t).T, ((0, 0), (0, LANES - N_HEADS)))
    dfl, d_bf = _forget_bwd(dfc, fl, bf, seq)
    du, d_mix, d_scale = _pool_bwd(dps, p, mixb, links.tie(pool_scale), seq)
    d_wu_in = _matmul("dw_in_u", du, h, "tn", F32, 512, 1024, 1024, dep=links.token)
    small = (jnp.zeros_like(norm1_g), d_bf[:, :N_HEADS], d_mix, d_scale, d_g2n, d_gf)
    d_wq = _matmul("dw_in_q", dq, h, "tn", F32, 512, 1024, 1024, dep=links.token)
    d_wk = _matmul("dw_in_k", dk, h, "tn", F32, 512, 1024, 1024, dep=links.token)
    d_wv = _matmul("dw_in_v", dv, h, "tn", F32, 512, 1024, 1024, dep=links.token)
    links.advance([d_wu_in, d_wq, d_wk, d_wv], begin=[_small_begin("small", _pack_small(*small, extra=loss))])
    d_wf = _matmul("dw_in_f", dfl, h, "tn", F32, LANES, 1024, 512)
    d_wg2 = _matmul("dw_in_gates", dg2, h, "tn", F32, 1024, 1024, 1024, dep=links.token)
    d_win = jnp.concatenate([d_wu_in, d_wq, d_wk, d_wv, d_wf[:N_HEADS], d_wg2], axis=0)
    comm_b = [d_win.reshape(N_CHIPS, d_win.shape[0] // N_CHIPS, d)]
    links.advance(comm_b, begin=[_reduce_begin("b", comm_b, column_halves=True)])
    if between is not None:
        between()
    dx, d_g1 = _in_bwd(du, dq, dk, dv, dg2, dfl, dx1, x, links.tie(norm1_g), wu, wqkv, wg2, wft, 512)
    return loss, dx, (d_g1,) + small[1:]


def kernel(x, norm1_g, w_in, b_forget, pool_mix, pool_scale, w_pool_out, w_attn_out, w_out, norm2_g, w_ffn_gate, w_ffn_up, w_ffn_down, norm_f_g, loss_target, m_norm1_g, m_w_in, m_b_forget, m_pool_mix, m_pool_scale, m_w_pool_out, m_w_attn_out, m_w_out, m_norm2_g, m_w_ffn_gate, m_w_ffn_up, m_w_ffn_down, m_norm_f_g, v_norm1_g, v_w_in, v_b_forget, v_pool_mix, v_pool_scale, v_w_pool_out, v_w_attn_out, v_w_out, v_norm2_g, v_w_ffn_gate, v_w_ffn_up, v_w_ffn_down, v_norm_f_g):
    nb, seq, d = x.shape
    group_a = ((w_ffn_gate, m_w_ffn_gate, v_w_ffn_gate, True, 9), (w_ffn_up, m_w_ffn_up, v_w_ffn_up, True, 10),
               (w_ffn_down, m_w_ffn_down, v_w_ffn_down, False, 11))
    group_m = ((w_pool_out, m_w_pool_out, v_w_pool_out, False, 5), (w_attn_out, m_w_attn_out, v_w_attn_out, False, 6),
               (w_out, m_w_out, v_w_out, False, 7))
    group_b = ((w_in, m_w_in, v_w_in, False, 1),)
    small_w = (norm1_g, b_forget, pool_mix, pool_scale, norm2_g, norm_f_g)
    small_m = (m_norm1_g, m_b_forget, m_pool_mix, m_pool_scale, m_norm2_g, m_norm_f_g)
    small_v = (v_norm1_g, v_b_forget, v_pool_mix, v_pool_scale, v_norm2_g, v_norm_f_g)
    small_pos = (0, 2, 3, 4, 8, 12)
    view = lambda a, tr: a[0].T if tr else a[0]
    unview = lambda a, tr, like: (a.T if tr else a).reshape(like.shape)

    def shard(w, tr):
        lw = view(w, tr).astype(BF16)
        return lw.reshape(2, lw.shape[0] // 2, lw.shape[1])

    cm = lambda a: jnp.transpose(a, (2, 0, 1))
    shard_in = _rows_to_bf16("w_in_to_bf16", cm(w_in))
    links = _MeshLinks([shard_in],
                       [shard(w_pool_out, False), shard(w_attn_out, False), shard(w_out, False),
                        shard(w_ffn_gate, True), shard(w_ffn_up, True), shard(w_ffn_down, False)])
    grads, deltas, new_m, new_v = [None] * 13, [None] * 13, [None] * 13, [None] * 13
    pos_c = jnp.stack([lax.axis_index("c")]).astype(jnp.int32)

    def update(tag, group, dep, members=(0, 1, 2)):
        last = []
        reduced = links.reduced(tag)
        for k in members:
            (w, m, v, tr, pos), (mine, other) = group[k], reduced[k]
            outs = _adamw_halves(f"adamw_{tag}{k}", pos_c, view(w, tr), mine, other, view(m, tr), view(v, tr), 256,
                                 dep=dep)
            grads[pos], deltas[pos], new_m[pos], new_v[pos] = (unview(a, tr, w) for a in outs)
            last.append(outs[1])
        return last

    def update_a():
        links.advance(update("a", group_a, links.token, members=(0, 1)))

    loss, dx, small_g = _local_step(
        links, x.reshape(nb * seq, d), loss_target.reshape(nb * seq, d), seq,
        norm1_g, b_forget, pool_mix[0], pool_scale, norm2_g, norm_f_g.reshape(1, d), between=update_a)

    links.advance(dx, begin=[_small_begin("g1", small_g[0].reshape(8, LANES))])
    last = update("m", group_m, links.token) + update("a", group_a, links.token, members=(2,))
    small_rest = _small_sum("small_sum", *links.reduced("small")[0])
    links.advance(last + [small_rest])
    small_sum = jnp.concatenate([_small_sum("g1_sum", *links.reduced("g1")[0]), small_rest[8:]], axis=0)
    loss_out = small_sum[8, N_HEADS]
    dl, mn, vn = _adamw("adamw_small", _pack_small(*small_w), small_sum * _small_mask(), _pack_small(*small_m),
                        _pack_small(*small_v))
    for pos, g, a, b, e in zip(small_pos, _unpack_small(small_sum, small_w), _unpack_small(dl, small_w),
                               _unpack_small(mn, small_w), _unpack_small(vn, small_w)):
        grads[pos], deltas[pos], new_m[pos], new_v[pos] = g, a, b, e
    links.advance([dl])
    (mine, other), = links.reduced("b")
    outs = _adamw_rows("adamw_b0", pos_c, cm(w_in), mine, other, cm(m_w_in), cm(v_w_in))
    grads[1], deltas[1], new_m[1], new_v[1] = (jnp.transpose(a, (1, 2, 0)) for a in outs)

    return (loss_out, dx.reshape(nb, seq, d), *grads, *deltas, *new_m, *new_v)


def _small_mask():
    rows = lax.broadcasted_iota(jnp.int32, (552, LANES), 0)
    lanes = lax.broadcasted_iota(jnp.int32, (552, LANES), 1)
    return jnp.where(jnp.logical_and(rows == 8, lanes == N_HEADS), 0.0, 1.0).astype(F32)
```

```python
import jax
import jax.numpy as jnp
from jax import lax
from jax.experimental import pallas as pl
from jax.experimental.pallas import tpu as pltpu

F32 = jnp.float32
BF16 = jnp.bfloat16

D_MODEL = 1024
POOL_WINDOWS = (2, 4, 8, 16)
POOL_GROUPS = 4
POOL_GROUP_DIM = 128
POOL_WIDTH = 512
HEAD_DIM = 64
N_HEADS = 8
ATTN_WIDTH = 512
D_FF = 2816
RMS_EPS = 1e-6
ATTN_SCALE = HEAD_DIM ** -0.5
NEG_BIG = -1e30

ADAM_LR = 0.001
ADAM_B1 = 0.9
ADAM_B2 = 0.999
ADAM_EPS = 1e-08
ADAM_WD = 0.01
ADAM_STEP = 10

LANES = 128
N_CHIPS = 4
N_DEV = 8
VMEM_LIMIT_V7X = 52 * 1024 * 1024
ROW_CHUNK = 256
MESH = pl.DeviceIdType.MESH
ANY = pl.BlockSpec(memory_space=pl.ANY)


def _cparams(*sem):
    return pltpu.CompilerParams(dimension_semantics=sem if sem else None, vmem_limit_bytes=VMEM_LIMIT_V7X)


def _dep_list(dep):
    return [] if dep is None else (list(dep) if isinstance(dep, (list, tuple)) else [dep])


def _after(body, n_in, dep):
    k = len(_dep_list(dep))
    if k == 0:
        return body

    def wrapped(*refs):
        body(*refs[:n_in], *refs[n_in + k:])

    return wrapped


def _dep_args(dep):
    deps = _dep_list(dep)
    return [ANY] * len(deps), deps


def _dot(a, b):
    return lax.dot_general(a, b, (((1,), (0,)), ((), ())), preferred_element_type=F32)


def _dot_nt(a, b):
    return lax.dot_general(a, b, (((1,), (1,)), ((), ())), preferred_element_type=F32)


def _dot_tn(a, b):
    return lax.dot_general(a, b, (((0,), (0,)), ((), ())), preferred_element_type=F32)


def _sigmoid(x):
    return jax.nn.sigmoid(x)


def _rms_fwd(x, g):
    r = lax.rsqrt(jnp.mean(x * x, axis=-1, keepdims=True) + RMS_EPS)
    return (x * r) * g


def _rms_bwd(x, g, dy):
    r = lax.rsqrt(jnp.mean(x * x, axis=-1, keepdims=True) + RMS_EPS)
    xh = x * r
    dg = jnp.sum(dy * xh, axis=0, keepdims=True)
    dxh = dy * g
    dx = r * (dxh - xh * jnp.mean(dxh * xh, axis=-1, keepdims=True))
    return dx, dg


def _matmul(name, a, b, mode, out_dtype, tm, tn, tk, dep=None):
    if mode == "nn":
        (m, k), (_, n) = a.shape, b.shape
    elif mode == "nt":
        (m, k), (n, _) = a.shape, b.shape
    else:
        (k, m), (_, n) = a.shape, b.shape
    tm, tn, tk = min(tm, m), min(tn, n), min(tk, k)
    assert m % tm == 0 and n % tn == 0 and k % tk == 0, (name, m, n, k, tm, tn, tk)
    nk = k // tk
    if mode == "tn":
        a_spec = pl.BlockSpec((tk, tm), lambda i, j, kk: (kk, i))
    else:
        a_spec = pl.BlockSpec((tm, tk), lambda i, j, kk: (i, kk))
    if mode == "nt":
        b_spec = pl.BlockSpec((tn, tk), lambda i, j, kk: (j, kk))
    else:
        b_spec = pl.BlockSpec((tk, tn), lambda i, j, kk: (kk, j))
    dot = {"nn": _dot, "nt": _dot_nt, "tn": _dot_tn}[mode]
    use_scratch = nk > 1 and out_dtype != F32

    def body(a_ref, b_ref, o_ref, *scratch):
        if nk == 1 and mode != "tn":
            rows = min(ROW_CHUNK, tm)
            bb = b_ref[...].astype(BF16)
            for r0 in range(0, tm, rows):
                o_ref[r0:r0 + rows, :] = dot(a_ref[r0:r0 + rows, :].astype(BF16), bb).astype(out_dtype)
            return
        prod = dot(a_ref[...].astype(BF16), b_ref[...].astype(BF16))
        if nk == 1:
            o_ref[...] = prod.astype(out_dtype)
            return
        acc = scratch[0] if use_scratch else o_ref
        kk = pl.program_id(2)

        @pl.when(kk == 0)
        def _():
            acc[...] = prod

        @pl.when(kk > 0)
        def _():
            acc[...] += prod

        if use_scratch:
            @pl.when(kk == nk - 1)
            def _():
                o_ref[...] = acc[...].astype(out_dtype)

    dep_specs, dep_ops = _dep_args(dep)
    return pl.pallas_call(
        _after(body, 2, dep),
        name=name,
        out_shape=jax.ShapeDtypeStruct((m, n), out_dtype),
        grid=(m // tm, n // tn, nk),
        in_specs=[a_spec, b_spec] + dep_specs,
        out_specs=pl.BlockSpec((tm, tn), lambda i, j, kk: (i, j)),
        scratch_shapes=[pltpu.VMEM((tm, tn), F32)] if use_scratch else [],
        compiler_params=_cparams("parallel", "parallel", "arbitrary"),
    )(a, b, *dep_ops)


def _norm_fwd(name, x, g, tm):
    t, d = x.shape
    tm = min(tm, t)

    def body(x_ref, g_ref, h_ref):
        h_ref[...] = _rms_fwd(x_ref[...], g_ref[...]).astype(BF16)

    return pl.pallas_call(
        body, name=name, out_shape=jax.ShapeDtypeStruct((t, d), BF16), grid=(t // tm,),
        in_specs=[pl.BlockSpec((tm, d), lambda i: (i, 0)), pl.BlockSpec((1, d), lambda i: (0, 0))],
        out_specs=pl.BlockSpec((tm, d), lambda i: (i, 0)),
        compiler_params=_cparams("parallel"),
    )(x, g)


def _split3(x):
    hi = x.astype(BF16)
    r1 = x - hi.astype(F32)
    mid = r1.astype(BF16)
    lo = (r1 - mid.astype(F32)).astype(BF16)
    return hi, mid, lo


def _tri_dot(tri, x):
    hi, mid, lo = _split3(x)
    return _dot(tri, hi) + _dot(tri, mid) + _dot(tri, lo)


def _forget_fwd(h, wf, bf, seq):
    t, d = h.shape
    cb = min(256, seq)

    def body(h_ref, wf_ref, bf_ref, fl_ref, fc_ref):
        fl = _dot(h_ref[...], wf_ref[...])
        fl_ref[...] = fl
        xx = fl + bf_ref[...]
        lf = jnp.minimum(xx, 0.0) - jnp.log(1.0 + jnp.exp(-jnp.abs(xx)))
        ri = lax.broadcasted_iota(jnp.int32, (cb, cb), 0)
        ci = lax.broadcasted_iota(jnp.int32, (cb, cb), 1)
        tri = (ri >= ci).astype(BF16)
        carry = jnp.zeros((1, LANES), F32)
        for blk in range(seq // cb):
            cs = _tri_dot(tri, lf[blk * cb:(blk + 1) * cb]) + carry
            fc_ref[blk * cb:(blk + 1) * cb, :] = cs
            carry = cs[cb - 1:cb, :]

    return pl.pallas_call(
        body, name="forget_fwd",
        out_shape=(jax.ShapeDtypeStruct((t, LANES), F32), jax.ShapeDtypeStruct((t, LANES), F32)),
        grid=(t // seq,),
        in_specs=[pl.BlockSpec((seq, d), lambda b: (b, 0)), pl.BlockSpec((d, LANES), lambda b: (0, 0)),
                  pl.BlockSpec((1, LANES), lambda b: (0, 0))],
        out_specs=(pl.BlockSpec((seq, LANES), lambda b: (b, 0)), pl.BlockSpec((seq, LANES), lambda b: (b, 0))),
        compiler_params=_cparams("parallel"),
    )(h, wf, bf)


def _pool_fwd(u, mix, scale, seq):
    t = u.shape[0]

    def body(u_ref, mix_ref, sc_ref, p_ref, ps_ref):
        tpos = lax.broadcasted_iota(jnp.int32, (seq, POOL_GROUP_DIM), 0)
        for g in range(POOL_GROUPS):
            sl = slice(g * POOL_GROUP_DIM, (g + 1) * POOL_GROUP_DIM)
            ug = u_ref[:, sl]
            s = ug
            for lvl in range(g + 1):
                d = 2 ** lvl
                s = s + jnp.where(tpos >= d, pltpu.roll(s, d, 0), 0.0)
            cnt = jnp.minimum(tpos + 1, POOL_WINDOWS[g]).astype(F32)
            pb = (s / cnt - ug).astype(BF16)
            p_ref[:, sl] = pb
            ps_ref[:, sl] = (_dot(pb, mix_ref[g]) * sc_ref[:, sl]).astype(BF16)

    return pl.pallas_call(
        body, name="pool_fwd",
        out_shape=(jax.ShapeDtypeStruct((t, POOL_WIDTH), BF16), jax.ShapeDtypeStruct((t, POOL_WIDTH), BF16)),
        grid=(t // seq,),
        in_specs=[pl.BlockSpec((seq, POOL_WIDTH), lambda b: (b, 0)),
                  pl.BlockSpec((POOL_GROUPS, POOL_GROUP_DIM, POOL_GROUP_DIM), lambda b: (0, 0, 0)),
                  pl.BlockSpec((1, POOL_WIDTH), lambda b: (0, 0))],
        out_specs=(pl.BlockSpec((seq, POOL_WIDTH), lambda b: (b, 0)), pl.BlockSpec((seq, POOL_WIDTH), lambda b: (b, 0))),
        compiler_params=_cparams("parallel"),
    )(u, mix, scale)


def _aug_constants():
    w = N_HEADS * LANES
    rows = jnp.arange(3 * LANES)
    piece, head = rows // LANES, rows % LANES
    cols = jnp.arange(w)
    live = (head < N_HEADS)[:, None]
    pq = (live & (cols[None, :] == (head * LANES + HEAD_DIM + piece)[:, None])).astype(BF16)
    pk = -(live & (cols[None, :] == (head * LANES + HEAD_DIM + 3 + piece)[:, None])).astype(BF16)
    lane = cols % LANES
    oq = ((lane >= HEAD_DIM + 3) & (lane < HEAD_DIM + 6)).astype(F32)[None, :]
    ok = ((lane >= HEAD_DIM) & (lane < HEAD_DIM + 3)).astype(F32)[None, :]
    return pq, pk, oq, ok


def _head_blocks(wt):
    d = wt.shape[1]
    return jnp.pad(wt.reshape(N_HEADS, HEAD_DIM, d), ((0, 0), (0, LANES - HEAD_DIM), (0, 0))).reshape(N_HEADS * LANES, d)


def _attn_prep(h, wq, wk, wv, fcum, tm):
    t, d = h.shape
    tm = min(tm, t)
    rows = min(ROW_CHUNK, tm)
    w = N_HEADS * LANES
    pq, pk, oq, ok = _aug_constants()

    def body(h_ref, wq_ref, wk_ref, wv_ref, f_ref, pq_ref, pk_ref, oq_ref, ok_ref, qa_ref, ka_ref, v_ref):
        for r0 in range(0, tm, rows):
            rs = slice(r0, r0 + rows)
            hh = h_ref[rs, :]
            fs = jnp.concatenate(_split3(f_ref[rs, :]), axis=1)
            q = _dot_nt(hh, wq_ref[...]).astype(BF16).astype(F32) * ATTN_SCALE
            qa_ref[rs, :] = (q + _dot(fs, pq_ref[...]) + oq_ref[...]).astype(BF16)
            k = _dot_nt(hh, wk_ref[...]).astype(BF16).astype(F32)
            ka_ref[rs, :] = (k + _dot(fs, pk_ref[...]) + ok_ref[...]).astype(BF16)
            v_ref[rs, :] = _dot_nt(hh, wv_ref[...]).astype(BF16)

    row = lambda n: pl.BlockSpec((tm, n), lambda i: (i, 0))
    full = lambda a: pl.BlockSpec(a.shape, lambda i: (0, 0))
    return pl.pallas_call(
        body, name="attn_prep",
        out_shape=(jax.ShapeDtypeStruct((t, w), BF16), jax.ShapeDtypeStruct((t, w), BF16),
                   jax.ShapeDtypeStruct((t, ATTN_WIDTH), BF16)),
        grid=(t // tm,),
        in_specs=[row(d), full(wq), full(wk), full(wv), row(LANES), full(pq), full(pk), full(oq), full(ok)],
        out_specs=(row(w), row(w), row(ATTN_WIDTH)),
        compiler_params=_cparams("parallel"),
    )(h, wq, wk, wv, fcum, pq, pk, oq, ok)


def _fold_lanes(x, op):
    out = x[:, :LANES]
    for g in range(1, x.shape[1] // LANES):
        out = op(out, x[:, g * LANES:(g + 1) * LANES])
    return out


def _causal_sweep(i, tile, carry):
    def quad(jj, c):
        for u in range(4):
            c = tile(4 * jj + u, c, False)
        return c

    carry = lax.fori_loop(0, i // 4, quad, carry)
    base = 4 * (i // 4)
    carry = lax.cond(i % 4 >= 2, lambda c: tile(base + 1, tile(base, c, False), False), lambda c: c, carry)
    return lax.cond(i % 2 == 1, lambda c: tile(i, tile(i - 1, c, False), True), lambda c: tile(i, c, True), carry)


def _attn_fwd(qa, ka, v, seq, tq, dep=None):
    t = qa.shape[0]
    nq = seq // tq
    hp_n = N_HEADS // 2
    heads = [slice(e * LANES, (e + 1) * LANES) for e in range(2)]

    def body(q_ref, k_ref, v_ref, o_ref, lse_ref, s_buf):
        i = pl.program_id(2)
        diag_ok = lax.broadcasted_iota(jnp.int32, (tq, tq), 0) >= lax.broadcasted_iota(jnp.int32, (tq, tq), 1)
        qs = [q_ref[:, hl] for hl in heads]

        def sweep1(j, mxs, diagonal):
            r0 = pl.multiple_of(j * tq, tq)
            out = []
            for e, hl in enumerate(heads):
                s = _dot_nt(qs[e], k_ref[pl.ds(r0, tq), hl])
                if diagonal:
                    s = jnp.where(diag_ok, s, NEG_BIG)
                s_buf[e, j] = s
                out.append(jnp.maximum(mxs[e], _fold_lanes(s, jnp.maximum)))
            return tuple(out)

        mxs = _causal_sweep(i, sweep1, (jnp.full((tq, LANES), NEG_BIG, F32),) * 2)
        ms = [jnp.max(mx, axis=1, keepdims=True) for mx in mxs]

        def sweep2(j, carry, diagonal):
            r0 = pl.multiple_of(j * tq, tq)
            vv = v_ref[pl.ds(r0, tq), :]
            out = []
            for e in range(2):
                p = jnp.exp(s_buf[e, j] - ms[e])
                out += [carry[2 * e] + _fold_lanes(p, jnp.add), carry[2 * e + 1] + _dot(p.astype(BF16), vv)]
            return tuple(out)

        res = _causal_sweep(i, sweep2, (jnp.zeros((tq, LANES), F32),) * 4)
        outs = []
        for e in range(2):
            l = jnp.sum(res[2 * e], axis=1, keepdims=True)
            outs.append(res[2 * e + 1] / l)
            lse_ref[:, e:e + 1] = ms[e] + jnp.log(l)
        lane = lax.broadcasted_iota(jnp.int32, (tq, LANES), 1)
        o_ref[...] = jnp.where(lane < HEAD_DIM, outs[0], outs[1])

    dep_specs, dep_ops = _dep_args(dep)
    return pl.pallas_call(
        _after(body, 3, dep), name="attn_fwd",
        out_shape=(jax.ShapeDtypeStruct((t, ATTN_WIDTH), F32), jax.ShapeDtypeStruct((hp_n, t, 2), F32)),
        grid=(t // seq, hp_n, nq),
        in_specs=[pl.BlockSpec((tq, 2 * LANES), lambda b, hp, i: (b * nq + i, hp)),
                  pl.BlockSpec((seq, 2 * LANES), lambda b, hp, i: (b, hp)),
                  pl.BlockSpec((seq, LANES), lambda b, hp, i: (b, hp))] + dep_specs,
        out_specs=(pl.BlockSpec((tq, LANES), lambda b, hp, i: (b * nq + i, hp)),
                   pl.BlockSpec((None, tq, 2), lambda b, hp, i: (hp, b * nq + i, 0))),
        scratch_shapes=[pltpu.VMEM((2, nq, tq, tq), F32)],
        compiler_params=_cparams("parallel", "parallel", "arbitrary"),
    )(qa, ka, v, *dep_ops)


def _merge_fwd(x, ps, o, g2, wpo, wao, wout, tm):
    t, d = x.shape
    tm = min(tm, t)
    rows = min(ROW_CHUNK, tm)

    def body(x_ref, ps_ref, o_ref, gp_ref, ga_ref, wpo_ref, wao_ref, wout_ref, mg_ref, x1_ref):
        for r0 in range(0, tm, rows):
            rs = slice(r0, r0 + rows)
            py = _dot(ps_ref[rs, :], wpo_ref[...])
            ay = _dot(o_ref[rs, :].astype(BF16), wao_ref[...])
            mb = (_sigmoid(gp_ref[rs, :].astype(F32)) * py + _sigmoid(ga_ref[rs, :].astype(F32)) * ay).astype(BF16)
            mg_ref[rs, :] = mb
            x1_ref[rs, :] = x_ref[rs, :] + _dot(mb, wout_ref[...])

    row = lambda w: pl.BlockSpec((tm, w), lambda i: (i, 0))
    full = lambda a: pl.BlockSpec(a.shape, lambda i: (0, 0))
    return pl.pallas_call(
        body, name="merge_fwd",
        out_shape=(jax.ShapeDtypeStruct((t, d), BF16), jax.ShapeDtypeStruct((t, d), F32)),
        grid=(t // tm,),
        in_specs=[row(d), row(POOL_WIDTH), row(ATTN_WIDTH), pl.BlockSpec((tm, d), lambda i: (i, 0)),
                  pl.BlockSpec((tm, d), lambda i: (i, 1)), full(wpo), full(wao), full(wout)],
        out_specs=(row(d), row(d)),
        compiler_params=_cparams("parallel"),
    )(x, ps, o, g2, g2, wpo, wao, wout)


def _ffn_fwd(x1, g, wg, wu, wd, tm, tf):
    t, d = x1.shape
    f = wg.shape[0]
    tm = min(tm, t)
    nf = f // tf
    rows = min(512, tm)

    def body(x1_ref, g_ref, wg_ref, wu_ref, wd_ref, h2_ref, gt_ref, up_ref, act_ref, x2_ref):
        j = pl.program_id(1)

        @pl.when(j == 0)
        def _():
            h2_ref[...] = _rms_fwd(x1_ref[...], g_ref[...]).astype(BF16)

            x2_ref[...] = x1_ref[...]

        for r0 in range(0, tm, rows):
            rs = slice(r0, r0 + rows)
            h2 = h2_ref[rs, :]
            gt = _dot_nt(h2, wg_ref[...])
            up = _dot_nt(h2, wu_ref[...])
            sg = _sigmoid(gt)
            silu = gt * sg
            act = (silu * up).astype(BF16)
            gt_ref[rs, :] = (up * (sg * (1.0 + gt * (1.0 - sg)))).astype(BF16)
            up_ref[rs, :] = silu.astype(BF16)
            act_ref[rs, :] = act
            x2_ref[rs, :] += _dot(act, wd_ref[...])

    return pl.pallas_call(
        body, name="ffn_fwd",
        out_shape=(jax.ShapeDtypeStruct((t, d), BF16), jax.ShapeDtypeStruct((t, f), BF16),
                   jax.ShapeDtypeStruct((t, f), BF16), jax.ShapeDtypeStruct((t, f), BF16),
                   jax.ShapeDtypeStruct((t, d), F32)),
        grid=(t // tm, nf),
        in_specs=[pl.BlockSpec((tm, d), lambda i, j: (i, 0)), pl.BlockSpec((1, d), lambda i, j: (0, 0)),
                  pl.BlockSpec((tf, d), lambda i, j: (j, 0)), pl.BlockSpec((tf, d), lambda i, j: (j, 0)),
                  pl.BlockSpec((tf, d), lambda i, j: (j, 0))],
        out_specs=(pl.BlockSpec((tm, d), lambda i, j: (i, 0)), pl.BlockSpec((tm, tf), lambda i, j: (i, j)),
                   pl.BlockSpec((tm, tf), lambda i, j: (i, j)), pl.BlockSpec((tm, tf), lambda i, j: (i, j)),
                   pl.BlockSpec((tm, d), lambda i, j: (i, 0))),
        compiler_params=_cparams("parallel", "arbitrary"),
    )(x1, g, wg, wu, wd)


def _final_fwd_bwd(x2, target, g, tm):
    t, d = x2.shape
    tm = min(tm, t)

    def body(x_ref, t_ref, g_ref, loss_ref, dx_ref, dg_ref):
        i = pl.program_id(0)
        x = x_ref[...]
        gg = g_ref[...]
        err = _rms_fwd(x, gg) - t_ref[...]
        part = 0.5 * jnp.sum(jnp.mean(err * err, axis=-1, keepdims=True), axis=0, keepdims=True)
        dx, dg = _rms_bwd(x, gg, err * (1.0 / d))
        dx_ref[...] = dx

        @pl.when(i == 0)
        def _():
            loss_ref[...] = jnp.zeros_like(loss_ref)
            dg_ref[...] = jnp.zeros_like(dg_ref)

        loss_ref[...] += jnp.broadcast_to(part, loss_ref.shape)
        dg_ref[...] += dg

    return pl.pallas_call(
        body, name="final_fwd_bwd",
        out_shape=(jax.ShapeDtypeStruct((1, LANES), F32), jax.ShapeDtypeStruct((t, d), F32),
                   jax.ShapeDtypeStruct((1, d), F32)),
        grid=(t // tm,),
        in_specs=[pl.BlockSpec((tm, d), lambda i: (i, 0)), pl.BlockSpec((tm, d), lambda i: (i, 0)),
                  pl.BlockSpec((1, d), lambda i: (0, 0))],
        out_specs=(pl.BlockSpec((1, LANES), lambda i: (0, 0)), pl.BlockSpec((tm, d), lambda i: (i, 0)),
                   pl.BlockSpec((1, d), lambda i: (0, 0))),
        compiler_params=_cparams("arbitrary"),
    )(x2, target, g)


def _ffn_bwd(dx2, x1, g, gt, up, wg, wu, wd, tm, tf):
    t, d = dx2.shape
    f = gt.shape[1]
    tm = min(tm, t)
    nf = f // tf
    wgu = jnp.concatenate([wg.reshape(nf, tf, d), wu.reshape(nf, tf, d)], axis=1).reshape(2 * f, d)
    rows = min(256, tm)

    def body(dx2_ref, x1_ref, g_ref, gt_ref, up_ref, wgu_ref, wd_ref, dgt_ref, dup_ref, dx1_ref, dg_ref, acc_ref,
             dxb_ref):
        i, j = pl.program_id(0), pl.program_id(1)

        @pl.when(j == 0)
        def _():
            dxb_ref[...] = dx2_ref[...].astype(BF16)
            acc_ref[...] = jnp.zeros_like(acc_ref)

        for r0 in range(0, tm, rows):
            rs = slice(r0, r0 + rows)
            dact = _dot_nt(dxb_ref[rs, :], wd_ref[...])
            dgt = (dact * gt_ref[rs, :].astype(F32)).astype(BF16)
            dup = (dact * up_ref[rs, :].astype(F32)).astype(BF16)
            dgt_ref[rs, :] = dgt
            dup_ref[rs, :] = dup
            acc_ref[rs, :] += _dot(jnp.concatenate([dgt, dup], axis=1), wgu_ref[...])

        @pl.when(jnp.logical_and(i == 0, j == 0))
        def _():
            dg_ref[...] = jnp.zeros_like(dg_ref)

        @pl.when(j == nf - 1)
        def _():
            dxn, dg = _rms_bwd(x1_ref[...], g_ref[...], acc_ref[...])
            dx1_ref[...] = dx2_ref[...] + dxn
            dg_ref[...] += dg

    return pl.pallas_call(
        body, name="ffn_bwd",
        out_shape=(jax.ShapeDtypeStruct((t, f), BF16), jax.ShapeDtypeStruct((t, f), BF16),
                   jax.ShapeDtypeStruct((t, d), F32), jax.ShapeDtypeStruct((1, d), F32)),
        grid=(t // tm, nf),
        in_specs=[pl.BlockSpec((tm, d), lambda i, j: (i, 0)), pl.BlockSpec((tm, d), lambda i, j: (i, 0)),
                  pl.BlockSpec((1, d), lambda i, j: (0, 0)),
                  pl.BlockSpec((tm, tf), lambda i, j: (i, j)), pl.BlockSpec((tm, tf), lambda i, j: (i, j)),
                  pl.BlockSpec((2 * tf, d), lambda i, j: (j, 0)), pl.BlockSpec((tf, d), lambda i, j: (j, 0))],
        out_specs=(pl.BlockSpec((tm, tf), lambda i, j: (i, j)), pl.BlockSpec((tm, tf), lambda i, j: (i, j)),
                   pl.BlockSpec((tm, d), lambda i, j: (i, 0)), pl.BlockSpec((1, d), lambda i, j: (0, 0))),
        scratch_shapes=[pltpu.VMEM((tm, d), F32), pltpu.VMEM((tm, d), BF16)],
        compiler_params=_cparams("arbitrary", "arbitrary"),
    )(dx2, x1, g, gt, up, wgu, wd)


def _merge_bwd(dx1, ps, o, g2, wpo, wao, wout, tm, dep=None):
    t, d = dx1.shape
    tm = min(tm, t)
    rows = min(ROW_CHUNK, tm)

    def body(dx1_ref, ps_ref, o_ref, gp_ref, ga_ref, wpo_ref, wao_ref, wout_ref, dpy_ref, day_ref, dg2_ref, dps_ref, da_ref):
        for r0 in range(0, tm, rows):
            rs = slice(r0, r0 + rows)
            dm = _dot_nt(dx1_ref[rs, :].astype(BF16), wout_ref[...])
            py = _dot(ps_ref[rs, :], wpo_ref[...])
            ay = _dot(o_ref[rs, :].astype(BF16), wao_ref[...])
            sp = _sigmoid(gp_ref[rs, :].astype(F32))
            sa = _sigmoid(ga_ref[rs, :].astype(F32))
            dpy = (dm * sp).astype(BF16)
            day = (dm * sa).astype(BF16)
            dpy_ref[rs, :] = dpy
            day_ref[rs, :] = day
            dg2_ref[rs, :d] = (dm * py * (sp * (1.0 - sp))).astype(BF16)
            dg2_ref[rs, d:] = (dm * ay * (sa * (1.0 - sa))).astype(BF16)
            dps_ref[rs, :] = _dot_nt(dpy, wpo_ref[...])
            da_ref[rs, :] = _dot_nt(day, wao_ref[...]).astype(BF16)

    row = lambda w: pl.BlockSpec((tm, w), lambda i: (i, 0))
    full = lambda a: pl.BlockSpec(a.shape, lambda i: (0, 0))
    dep_specs, dep_ops = _dep_args(dep)
    return pl.pallas_call(
        _after(body, 8, dep), name="merge_bwd",
        out_shape=(jax.ShapeDtypeStruct((t, d), BF16), jax.ShapeDtypeStruct((t, d), BF16),
                   jax.ShapeDtypeStruct((t, 2 * d), BF16), jax.ShapeDtypeStruct((t, POOL_WIDTH), F32),
                   jax.ShapeDtypeStruct((t, ATTN_WIDTH), BF16)),
        grid=(t // tm,),
        in_specs=[row(d), row(POOL_WIDTH), row(ATTN_WIDTH), pl.BlockSpec((tm, d), lambda i: (i, 0)),
                  pl.BlockSpec((tm, d), lambda i: (i, 1)), full(wpo), full(wao), full(wout)] + dep_specs,
        out_specs=(row(d), row(d), row(2 * d), row(POOL_WIDTH), row(ATTN_WIDTH)),
        compiler_params=_cparams("parallel"),
    )(dx1, ps, o, g2, g2, wpo, wao, wout, *dep_ops)


def _attn_bwd(qa, ka, v, do, lse4, seq, tq, dep=None):
    t = qa.shape[0]
    nq = seq // tq
    hp_n = N_HEADS // 2
    heads = [slice(e * LANES, (e + 1) * LANES) for e in range(2)]

    def body(q_ref, k_ref, v_ref, do_ref, lse_ref, dq_ref, dk_ref, dv_ref, dfr_ref, dk_acc, dv_acc, p_buf, dp_buf):
        diag_ok = lax.broadcasted_iota(jnp.int32, (tq, tq), 0) >= lax.broadcasted_iota(jnp.int32, (tq, tq), 1)
        lane_q = lax.broadcasted_iota(jnp.int32, (tq, LANES), 1)
        mine_q = [lane_q < HEAD_DIM, lane_q >= HEAD_DIM]
        dv_acc[...] = jnp.zeros_like(dv_acc)
        dk_acc[...] = jnp.zeros_like(dk_acc)
        dfr_ref[...] = jnp.zeros_like(dfr_ref)
        transposed = lambda a: a.astype(F32).T.astype(BF16)

        def q_step(i, _):
            q0 = pl.multiple_of(i * tq, tq)
            qs = [q_ref[pl.ds(q0, tq), hl] for hl in heads]
            dov = do_ref[pl.ds(q0, tq), :]
            dos = [jnp.where(mq, dov, jnp.zeros((), BF16)) for mq in mine_q]
            qts = [transposed(q) for q in qs]
            dots = [transposed(a) for a in dos]
            lss = [lse_ref[pl.ds(q0, tq), e:e + 1] for e in range(2)]

            def sweep1(j, dls, diagonal):
                r0 = pl.multiple_of(j * tq, tq)
                vv = v_ref[pl.ds(r0, tq), :]
                out = []
                for e, hl in enumerate(heads):
                    s = _dot_nt(qs[e], k_ref[pl.ds(r0, tq), hl])
                    if diagonal:
                        s = jnp.where(diag_ok, s, NEG_BIG)
                    p = jnp.exp(s - lss[e])
                    dp = _dot_nt(dos[e], vv)
                    p_buf[e, j] = p
                    dp_buf[e, j] = dp
                    dv_acc[j] += _dot(dots[e], p.astype(BF16))
                    out.append(dls[e] + _fold_lanes(p * dp, jnp.add))
                return tuple(out)

            dls = _causal_sweep(i, sweep1, (jnp.zeros((tq, LANES), F32),) * 2)
            dls = [jnp.sum(d, axis=1, keepdims=True) for d in dls]

            def sweep2(j, dqs, diagonal):
                r0 = pl.multiple_of(j * tq, tq)
                out = []
                for e, hl in enumerate(heads):
                    ds = p_buf[e, j] * (dp_buf[e, j] - dls[e])
                    dfr_ref[e, pl.ds(j, 1), :] += jnp.sum(ds, axis=0, keepdims=True)
                    dsb = ds.astype(BF16)
                    dk_acc[e, j] += _dot(qts[e], dsb)
                    out.append(dqs[e] + _dot(dsb, k_ref[pl.ds(r0, tq), hl]))
                return tuple(out)

            dqs = _causal_sweep(i, sweep2, (jnp.zeros((tq, LANES), F32),) * 2)
            dq = jnp.where(mine_q[0], dqs[0], pltpu.roll(dqs[1], HEAD_DIM, 1)) * ATTN_SCALE
            dq_ref[pl.ds(q0, tq), :] = dq.astype(BF16)
            return 0

        lax.fori_loop(0, nq, q_step, 0)
        for j in range(nq):
            rs = slice(j * tq, (j + 1) * tq)
            dk = jnp.where(mine_q[0], dk_acc[0, j].T, pltpu.roll(dk_acc[1, j].T, HEAD_DIM, 1))
            dk_ref[rs, :] = dk.astype(BF16)
            dv_ref[rs, :] = dv_acc[j].T.astype(BF16)

    wide = pl.BlockSpec((seq, 2 * LANES), lambda b, hp: (b, hp))
    col = pl.BlockSpec((seq, LANES), lambda b, hp: (b, hp))
    pair = pl.BlockSpec((None, seq, 2), lambda b, hp: (hp, b, 0))
    dep_specs, dep_ops = _dep_args(dep)
    return pl.pallas_call(
        _after(body, 5, dep), name="attn_bwd",
        out_shape=(jax.ShapeDtypeStruct((t, ATTN_WIDTH), BF16),) * 3 + (jax.ShapeDtypeStruct((N_HEADS, t // tq, tq), F32),),
        grid=(t // seq, hp_n),
        in_specs=[wide, wide, col, col, pair] + dep_specs,
        out_specs=(col, col, col, pl.BlockSpec((2, nq, tq), lambda b, hp: (hp, b, 0))),
        scratch_shapes=[pltpu.VMEM((2, nq, LANES, tq), F32), pltpu.VMEM((nq, LANES, tq), F32),
                        pltpu.VMEM((2, nq, tq, tq), F32), pltpu.VMEM((2, nq, tq, tq), F32)],
        compiler_params=_cparams("parallel", "arbitrary"),
    )(qa, ka, v, do, lse4, *dep_ops)


def _forget_bwd(dfc, fl, bf, seq):
    t = fl.shape[0]
    cb = min(256, seq)
    nb = seq // cb

    def body(dfc_ref, fl_ref, bf_ref, dfl_ref, db_ref):
        b = pl.program_id(0)
        ri = lax.broadcasted_iota(jnp.int32, (cb, cb), 0)
        ci = lax.broadcasted_iota(jnp.int32, (cb, cb), 1)
        tri = (ci >= ri).astype(BF16)
        carry = jnp.zeros((1, LANES), F32)
        dbs = jnp.zeros((1, LANES), F32)
        for blk in reversed(range(nb)):
            rs = slice(blk * cb, (blk + 1) * cb)
            dlf = _tri_dot(tri, -dfc_ref[rs, :]) + carry
            carry = dlf[0:1, :]
            dfl = dlf * _sigmoid(-(fl_ref[rs, :] + bf_ref[...]))
            dfl_ref[rs, :] = dfl.astype(BF16)
            dbs = dbs + jnp.sum(dfl, axis=0, keepdims=True)

        @pl.when(b == 0)
        def _():
            db_ref[...] = jnp.zeros_like(db_ref)

        db_ref[...] += dbs

    return pl.pallas_call(
        body, name="forget_bwd",
        out_shape=(jax.ShapeDtypeStruct((t, LANES), BF16), jax.ShapeDtypeStruct((1, LANES), F32)),
        grid=(t // seq,),
        in_specs=[pl.BlockSpec((seq, LANES), lambda b: (b, 0)), pl.BlockSpec((seq, LANES), lambda b: (b, 0)),
                  pl.BlockSpec((1, LANES), lambda b: (0, 0))],
        out_specs=(pl.BlockSpec((seq, LANES), lambda b: (b, 0)), pl.BlockSpec((1, LANES), lambda b: (0, 0))),
        compiler_params=_cparams("arbitrary"),
    )(dfc, fl, bf)


def _pool_bwd(dps, p, mix, scale, seq):
    t = dps.shape[0]

    def body(dps_ref, p_ref, mix_ref, sc_ref, du_ref, dmix_ref, dsc_ref):
        b = pl.program_id(0)

        @pl.when(b == 0)
        def _():
            dmix_ref[...] = jnp.zeros_like(dmix_ref)
            dsc_ref[...] = jnp.zeros_like(dsc_ref)

        tpos = lax.broadcasted_iota(jnp.int32, (seq, POOL_GROUP_DIM), 0)
        for g in range(POOL_GROUPS):
            sl = slice(g * POOL_GROUP_DIM, (g + 1) * POOL_GROUP_DIM)
            pb = p_ref[:, sl]
            dpsg = dps_ref[:, sl]
            pm = _dot(pb, mix_ref[g])
            dsc_ref[:, sl] += jnp.sum(dpsg * pm, axis=0, keepdims=True)
            dpm = (dpsg * sc_ref[:, sl]).astype(BF16)
            dmix_ref[g] += _dot_tn(pb, dpm)
            dp = _dot_nt(dpm, mix_ref[g])
            cnt = jnp.minimum(tpos + 1, POOL_WINDOWS[g]).astype(F32)
            s = dp / cnt
            for lvl in range(g + 1):
                d = 2 ** lvl
                s = s + jnp.where(tpos < seq - d, pltpu.roll(s, seq - d, 0), 0.0)
            du_ref[:, sl] = (s - dp).astype(BF16)

    return pl.pallas_call(
        body, name="pool_bwd",
        out_shape=(jax.ShapeDtypeStruct((t, POOL_WIDTH), BF16),
                   jax.ShapeDtypeStruct((POOL_GROUPS, POOL_GROUP_DIM, POOL_GROUP_DIM), F32),
                   jax.ShapeDtypeStruct((1, POOL_WIDTH), F32)),
        grid=(t // seq,),
        in_specs=[pl.BlockSpec((seq, POOL_WIDTH), lambda b: (b, 0)), pl.BlockSpec((seq, POOL_WIDTH), lambda b: (b, 0)),
                  pl.BlockSpec((POOL_GROUPS, POOL_GROUP_DIM, POOL_GROUP_DIM), lambda b: (0, 0, 0)),
                  pl.BlockSpec((1, POOL_WIDTH), lambda b: (0, 0))],
        out_specs=(pl.BlockSpec((seq, POOL_WIDTH), lambda b: (b, 0)),
                   pl.BlockSpec((POOL_GROUPS, POOL_GROUP_DIM, POOL_GROUP_DIM), lambda b: (0, 0, 0)),
                   pl.BlockSpec((1, POOL_WIDTH), lambda b: (0, 0))),
        compiler_params=_cparams("arbitrary"),
    )(dps, p, mix, scale)


def _in_bwd(du, dq, dk, dv, dg2, dfl, dx1, x, g, wu, wqkv, wg2, wft, tm):
    t, d = x.shape
    tm = min(tm, t)
    rows = min(ROW_CHUNK, tm)
    aw = ATTN_WIDTH

    def body(du_ref, dq_ref, dk_ref, dv_ref, dg2_ref, dfl_ref, dx1_ref, x_ref, g_ref, wu_ref, wqkv_ref, wg2_ref, wft_ref,
             dx_ref, dg_ref):
        i = pl.program_id(0)

        @pl.when(i == 0)
        def _():
            dg_ref[...] = jnp.zeros_like(dg_ref)

        for r0 in range(0, tm, rows):
            rs = slice(r0, r0 + rows)
            dh = _dot(du_ref[rs, :], wu_ref[...])
            dh += _dot(dq_ref[rs, :], wqkv_ref[0:aw, :])
            dh += _dot(dk_ref[rs, :], wqkv_ref[aw:2 * aw, :])
            dh += _dot(dv_ref[rs, :], wqkv_ref[2 * aw:3 * aw, :])
            dh += _dot(dg2_ref[rs, :], wg2_ref[...])
            dh += _dot(dfl_ref[rs, :], wft_ref[...])
            dxn, dg = _rms_bwd(x_ref[rs, :], g_ref[...], dh)
            dx_ref[rs, :] = dx1_ref[rs, :] + dxn
            dg_ref[...] += dg

    row = lambda w: pl.BlockSpec((tm, w), lambda i: (i, 0))
    full = lambda a: pl.BlockSpec(a.shape, lambda i: (0, 0))
    return pl.pallas_call(
        body, name="in_bwd",
        out_shape=(jax.ShapeDtypeStruct((t, d), F32), jax.ShapeDtypeStruct((1, d), F32)),
        grid=(t // tm,),
        in_specs=[row(POOL_WIDTH), row(aw), row(aw), row(aw), row(2 * d), row(LANES), row(d), row(d),
                  pl.BlockSpec((1, d), lambda i: (0, 0)), full(wu), full(wqkv), full(wg2), full(wft)],
        out_specs=(row(d), pl.BlockSpec((1, d), lambda i: (0, 0))),
        compiler_params=_cparams("arbitrary"),
    )(du, dq, dk, dv, dg2, dfl, dx1, x, g, wu, wqkv, wg2, wft)


def _position():
    return lax.axis_index("x"), lax.axis_index("y"), lax.axis_index("c")


def _remote(src, dst, send_sem, recv_sem, device):
    return pltpu.make_async_remote_copy(src_ref=src, dst_ref=dst, send_sem=send_sem, recv_sem=recv_sem,
                                        device_id=device, device_id_type=MESH)


HBM = pl.BlockSpec(memory_space=pltpu.HBM)
SEM = pl.BlockSpec(memory_space=pltpu.SEMAPHORE)
DATAFLOW = pltpu.SideEffectType.DATAFLOW_SIDE_EFFECTING


def _copies_start(name, arrays, plan, m, dep=None):
    n = len(arrays)
    arrays = [pltpu.with_memory_space_constraint(a, pltpu.HBM) for a in arrays]

    def body(*refs):
        ins, send_sem, recv_sem, token = refs[:n], refs[n], refs[n + 1], refs[2 * n + 2]
        for i, (src, dst, device, _) in enumerate(plan(ins, *_position())):
            _remote(src, dst, send_sem.at[i], recv_sem.at[i], device).start()
        token[...] = jnp.zeros_like(token)

    dep_specs, dep_ops = _dep_args(dep)
    outs = pl.pallas_call(
        _after(body, n, dep), name=name,
        out_shape=(pltpu.SemaphoreType.DMA((m,)), pltpu.SemaphoreType.DMA((m,)),
                   *[pltpu.HBM(a.shape, a.dtype) for a in arrays], jax.ShapeDtypeStruct((8, LANES), F32)),
        in_specs=[HBM] * n + dep_specs, out_specs=(SEM, SEM, *[HBM] * n, pl.BlockSpec(memory_space=pltpu.VMEM)),
        input_output_aliases={i: i + 2 for i in range(n)},
        compiler_params=pltpu.CompilerParams(has_side_effects=DATAFLOW),
    )(*arrays, *dep_ops)
    return (outs[0], outs[1]), list(outs[2:2 + n]), outs[2 + n]


def _copies_wait(name, sems, arrays, plan, after):
    n = len(arrays)
    afters = list(after) if isinstance(after, (list, tuple)) else [after]

    def body(*refs):
        ins, send_sem, recv_sem = refs[:n], refs[n], refs[n + 1]
        for i, (src, dst, device, landing) in enumerate(plan(ins, *_position())):
            _remote(src, dst, send_sem.at[i], recv_sem.at[i], device).wait_send()
            _remote(landing, landing, send_sem.at[i], recv_sem.at[i], device).wait_recv()

    outs = pl.pallas_call(
        body, name=name,
        out_shape=tuple(pltpu.HBM(a.shape, a.dtype) for a in arrays),
        in_specs=[HBM] * n + [SEM, SEM] + [ANY] * len(afters), out_specs=tuple([HBM] * n),
        input_output_aliases={i: i for i in range(n)},
        compiler_params=pltpu.CompilerParams(has_side_effects=DATAFLOW),
    )(*arrays, sems[0], sems[1], *afters)
    return list(outs)


def _tie(x, dep):
    for token in _dep_list(dep):
        x = x + token[0, 0]
    return x


def _other_chips(x, y):
    return [(1 - x, y), (x, 1 - y), (1 - x, 1 - y)]


def _gather_begin(tag, shards, token, column_halves=False):
    n = len(shards)
    lands = [lax.empty((N_CHIPS,) + s.shape, s.dtype) for s in shards]
    if column_halves:
        cols = lambda ref, h: pl.ds(pl.multiple_of(h * (ref.shape[-1] // 2), LANES), ref.shape[-1] // 2)
        mine = lambda ref, h: ref.at[:, cols(ref, h)]
        landed = lambda ref, chip, h: ref.at[chip, :, cols(ref, h)]
    else:
        mine = lambda ref, h: ref.at[h]
        landed = lambda ref, chip, h: ref.at[chip, h]

    def plan(refs, x, y, c):
        return [(mine(refs[k], c), landed(refs[n + k], 2 * x + y, c), (ox, oy, c), landed(refs[n + k], 2 * ox + oy, c))
                for k in range(n) for ox, oy in _other_chips(x, y)]

    sems, thru, token = _copies_start(f"gather_{tag}_ici_start", list(shards) + lands, plan, 3 * n, dep=token)
    return dict(tag=tag, n=n, plan=plan, sems=sems, arrays=thru, token=token, landed=landed)


def _gather_forward(st, after):
    n, tag, landed = st["n"], st["tag"], st["landed"]
    thru = _copies_wait(f"gather_{tag}_ici_wait", st["sems"], st["arrays"], st["plan"], after)

    def plan(refs, x, y, c):
        return [(landed(refs[k], 2 * ox + oy, c), landed(refs[k], 2 * ox + oy, c), (x, y, 1 - c),
                 landed(refs[k], 2 * ox + oy, 1 - c))
                for k in range(n) for ox, oy in _other_chips(x, y)]

    sems, lands, token = _copies_start(f"gather_{tag}_fwd_start", thru[n:], plan, 3 * n)
    return dict(tag=tag, n=n, plan=plan, sems=sems, arrays=lands, token=token, shards=thru[:n])


def _gather_end(st, after, merge=True):
    lands = _copies_wait(f"gather_{st['tag']}_fwd_wait", st["sems"], st["arrays"], st["plan"], after)
    if not merge:
        return lands, st["shards"]
    me = 2 * lax.axis_index("x") + lax.axis_index("y")
    return [lax.dynamic_update_index_in_dim(g, s, me, 0) for g, s in zip(lands, st["shards"])]


def _add_keep_give(name, pos, a, a_keep, a_give, b, b_keep, b_give, steps):
    r, c = b.shape[-2:]

    def spec(arr, fn):
        lead = arr.ndim - 2

        def index(i, p):
            idx = tuple(fn(i, p))
            return idx if len(idx) == arr.ndim else idx + (0, 0)

        return pl.BlockSpec((None,) * lead + (r, c), index)

    out_spec = pl.BlockSpec((None, r, c), lambda i, p: (i, 0, 0))

    def body(p_ref, ak_ref, bk_ref, ag_ref, bg_ref, keep_ref, give_ref):
        keep_ref[...] = ak_ref[...] + bk_ref[...].astype(F32)
        give_ref[...] = (ag_ref[...] + bg_ref[...].astype(F32)).astype(BF16)

    return pl.pallas_call(
        body, name=name,
        out_shape=(jax.ShapeDtypeStruct((steps, r, c), F32), jax.ShapeDtypeStruct((steps, r, c), BF16)),
        grid_spec=pltpu.PrefetchScalarGridSpec(
            num_scalar_prefetch=1, grid=(steps,),
            in_specs=[spec(a, a_keep), spec(b, b_keep), spec(a, a_give), spec(b, b_give)],
            out_specs=(out_spec, out_spec)),
        compiler_params=_cparams("parallel"),
    )(pos, a, b, a, b)


def _add_last(name, a, b):
    _, r, c = a.shape
    blk = pl.BlockSpec((None, r, c), lambda i: (0, 0, 0))

    def body(a_ref, b_ref, o_ref):
        o_ref[...] = a_ref[...] + b_ref[...].astype(F32)

    return pl.pallas_call(
        body, name=name, out_shape=jax.ShapeDtypeStruct((r, c), F32), grid=(1,), in_specs=[blk, blk],
        out_specs=pl.BlockSpec((r, c), lambda i: (0, 0)), compiler_params=_cparams("arbitrary"),
    )(a, b)


def _exchange_part(gives, lands, peer_fn):
    n = len(gives)

    def plan(refs, x, y, c):
        return [(refs[k], refs[n + k], peer_fn(x, y, c), refs[n + k]) for k in range(n)]

    return list(gives) + list(lands), plan, n


def _join_parts(parts):
    offsets, total = [], 0
    for arrays, _, _ in parts:
        offsets.append(total)
        total += len(arrays)

    def plan(refs, x, y, c):
        copies = []
        for (arrays, part_plan, _), off in zip(parts, offsets):
            copies += part_plan(refs[off:off + len(arrays)], x, y, c)
        return copies

    return [a for arrays, _, _ in parts for a in arrays], plan, sum(m for _, _, m in parts)


def _reduce_begin(tag, grads, column_halves=False):
    n = len(grads)
    if column_halves:
        half = lambda ref, j, h: ref.at[j, :, pl.ds(pl.multiple_of(h * (ref.shape[2] // 2), LANES), ref.shape[2] // 2)]
        lands = [lax.empty((N_CHIPS, g.shape[1], g.shape[2] // 2), F32) for g in grads]
    else:
        half = lambda ref, j, h: ref.at[j, h]
        lands = [lax.empty((N_CHIPS,) + g.shape[2:], F32) for g in grads]

    def plan(refs, x, y, c):
        return [(half(refs[k], j, 1 - c), refs[n + k].at[j], (x, y, 1 - c), refs[n + k].at[j])
                for k in range(n) for j in range(N_CHIPS)]

    return dict(tag=tag, n=n, stage="c", grads=list(grads), column_halves=column_halves,
                part=(list(grads) + lands, plan, N_CHIPS * n))


def _reduce_next(st, thru):
    tag, n, stage = st["tag"], st["n"], st["stage"]
    first, recv = thru[:n], thru[n:]
    x, y, c = _position()
    if stage == "c":
        pos = jnp.stack([c, x]).astype(jnp.int32)
        if st["column_halves"]:
            mine = lambda chip: (lambda i, p: (chip(p) + i, 0, p[0]))
        else:
            mine = lambda chip: (lambda i, p: (chip(p) + i, p[0]))
        sums = [_add_keep_give(
            f"rs{tag}_c_add{k}", pos,
            first[k], mine(lambda p: 2 * p[1]), mine(lambda p: 2 * (1 - p[1])),
            recv[k], lambda i, p: (2 * p[1] + i,), lambda i, p: (2 * (1 - p[1]) + i,), 2) for k in range(n)]
        lands = [lax.empty(s[1].shape, BF16) for s in sums]
        return dict(tag=tag, n=n, stage="x", keep=[s[0] for s in sums],
                    part=_exchange_part([s[1] for s in sums], lands, lambda x, y, c: (1 - x, y, c)))
    if stage == "x":
        pos = jnp.stack([y]).astype(jnp.int32)
        sums = [_add_keep_give(
            f"rs{tag}_x_add{k}", pos,
            st["keep"][k], lambda i, p: (p[0],), lambda i, p: (1 - p[0],),
            recv[k], lambda i, p: (p[0],), lambda i, p: (1 - p[0],), 1) for k in range(n)]
        lands = [lax.empty(s[1].shape, BF16) for s in sums]
        return dict(tag=tag, n=n, stage="y", keep=[s[0] for s in sums],
                    part=_exchange_part([s[1] for s in sums], lands, lambda x, y, c: (x, 1 - y, c)))
    if stage == "y":
        mine = [_add_last(f"rs{tag}_y_add{k}", st["keep"][k], recv[k]) for k in range(n)]
        lands = [lax.empty(m.shape, F32) for m in mine]
        return dict(tag=tag, n=n, stage="swap", part=_exchange_part(mine, lands, lambda x, y, c: (x, y, 1 - c)))
    return dict(tag=tag, done=list(zip(first, recv)))


def _small_begin(tag, v):
    land = lax.empty((N_DEV,) + v.shape, F32)
    flips = [(fx, fy, fc) for fx in (0, 1) for fy in (0, 1) for fc in (0, 1)][1:]

    def plan(refs, x, y, c):
        copies = []
        for fx, fy, fc in flips:
            px, py, pc = (1 - x if fx else x), (1 - y if fy else y), (1 - c if fc else c)
            copies.append((refs[0], refs[1].at[4 * x + 2 * y + c], (px, py, pc), refs[1].at[4 * px + 2 * py + pc]))
        return copies

    return dict(tag=tag, n=1, stage="swap", grads=[v], part=([v, land], plan, len(flips)))


def _small_sum(name, own, land):
    x, y, c = _position()
    me = jnp.stack([4 * x + 2 * y + c]).astype(jnp.int32)

    def body(me_ref, own_ref, land_ref, out_ref):
        term = lambda dev: jnp.where(me_ref[0] == dev, own_ref[...], land_ref[dev])
        acc = term(0)
        for dev in range(1, N_DEV):
            acc = acc + term(dev)
        out_ref[...] = acc

    return pl.pallas_call(
        body, name=name, out_shape=jax.ShapeDtypeStruct(own.shape, F32),
        grid_spec=pltpu.PrefetchScalarGridSpec(
            num_scalar_prefetch=1, grid=(1,),
            in_specs=[pl.BlockSpec(own.shape, lambda i, m: (0, 0)), pl.BlockSpec(land.shape, lambda i, m: (0, 0, 0))],
            out_specs=pl.BlockSpec(own.shape, lambda i, m: (0, 0))),
        compiler_params=_cparams("arbitrary"),
    )(me, own, land)


def _adamw_update(w, gg, m, v):
    mn = ADAM_B1 * m + (1.0 - ADAM_B1) * gg
    vn = ADAM_B2 * v + (1.0 - ADAM_B2) * (gg * gg)
    m_hat = mn / (1.0 - ADAM_B1 ** ADAM_STEP)
    v_hat = vn / (1.0 - ADAM_B2 ** ADAM_STEP)
    return -ADAM_LR * (m_hat / (jnp.sqrt(v_hat) + ADAM_EPS) + ADAM_WD * w), mn, vn


def _adamw(name, w, g, m, v):
    def body(w_ref, g_ref, m_ref, v_ref, d_ref, mo_ref, vo_ref):
        d_ref[...], mo_ref[...], vo_ref[...] = _adamw_update(w_ref[...], g_ref[...], m_ref[...], v_ref[...])

    blk = pl.BlockSpec(w.shape, lambda i: (0, 0))
    return pl.pallas_call(
        body, name=name, out_shape=(jax.ShapeDtypeStruct(w.shape, F32),) * 3, grid=(1,),
        in_specs=[blk] * 4, out_specs=(blk,) * 3, compiler_params=_cparams("arbitrary"),
    )(w, g, m, v)


def _rows_to_bf16(name, w):
    r, _, c = w.shape

    def body(w_ref, o_ref):
        o_ref[...] = w_ref[:, 0, :].astype(BF16)

    return pl.pallas_call(
        body, name=name, out_shape=jax.ShapeDtypeStruct((r, c), BF16), grid=(1,),
        in_specs=[pl.BlockSpec((r, 1, c), lambda i: (0, 0, 0))], out_specs=pl.BlockSpec((r, c), lambda i: (0, 0)),
        compiler_params=_cparams("arbitrary"),
    )(w)


def _adamw_rows(name, pos_c, w, g_mine, g_other, m, v):
    r, _, c = w.shape
    ch = c // 2

    def body(p_ref, w_ref, gm_ref, go_ref, m_ref, v_ref, g_ref, d_ref, mo_ref, vo_ref):
        gg = jnp.where(pl.program_id(0) == p_ref[0], gm_ref[...], go_ref[...])
        dl, mn, vn = _adamw_update(w_ref[:, 0, :], gg, m_ref[:, 0, :], v_ref[:, 0, :])
        g_ref[:, 0, :] = gg
        d_ref[:, 0, :] = dl
        mo_ref[:, 0, :] = mn
        vo_ref[:, 0, :] = vn

    rows = pl.BlockSpec((r, 1, ch), lambda h, p: (0, 0, h))
    half = pl.BlockSpec((r, ch), lambda h, p: (0, 0))
    return pl.pallas_call(
        body, name=name, out_shape=(jax.ShapeDtypeStruct(w.shape, F32),) * 4,
        grid_spec=pltpu.PrefetchScalarGridSpec(
            num_scalar_prefetch=1, grid=(2,), in_specs=[rows, half, half, rows, rows], out_specs=(rows,) * 4),
        compiler_params=_cparams("parallel"),
    )(pos_c, w, g_mine, g_other, m, v)


def _adamw_halves(name, pos_c, w, g_mine, g_other, m, v, tr, dep=None):
    r, c = w.shape
    rh = r // 2
    tr = tr if rh % tr == 0 else rh
    nt = rh // tr

    def body(p_ref, w_ref, gm_ref, go_ref, m_ref, v_ref, g_ref, d_ref, mo_ref, vo_ref):
        gg = jnp.where(pl.program_id(0) == p_ref[0], gm_ref[...], go_ref[...])
        g_ref[...] = gg
        d_ref[...], mo_ref[...], vo_ref[...] = _adamw_update(w_ref[...], gg, m_ref[...], v_ref[...])

    full = pl.BlockSpec((tr, c), lambda h, i, p: (h * nt + i, 0))
    half = pl.BlockSpec((tr, c), lambda h, i, p: (i, 0))
    dep_specs, dep_ops = _dep_args(dep)
    return pl.pallas_call(
        _after(body, 6, dep), name=name, out_shape=(jax.ShapeDtypeStruct((r, c), F32),) * 4,
        grid_spec=pltpu.PrefetchScalarGridSpec(
            num_scalar_prefetch=1, grid=(2, nt),
            in_specs=[full, half, half, full, full] + dep_specs, out_specs=(full,) * 4),
        compiler_params=_cparams("parallel", "parallel"),
    )(pos_c, w, g_mine, g_other, m, v, *dep_ops)


def _col_sharded_to_comm(g):
    k, n = g.shape
    return g.reshape(2, k // 2, N_CHIPS, n // N_CHIPS).transpose(2, 0, 1, 3)


def _row_sharded_to_comm(g):
    r, c = g.shape
    return g.reshape(N_CHIPS, 2, r // (2 * N_CHIPS), c)


def _col_sharded_full(g):
    _, _, rh, c = g.shape
    return g.reshape(N_CHIPS, 2 * rh, c).transpose(1, 0, 2).reshape(2 * rh, N_CHIPS * c)


def _row_sharded_full(g):
    _, _, rh, c = g.shape
    return g.reshape(N_CHIPS * 2 * rh, c)


def _chip_rows(w3, start, stop, own=None, me=None):
    r = w3.shape[1]
    parts = []
    for chip in range(N_CHIPS):
        lo, hi = max(start - chip * r, 0), min(stop - chip * r, r)
        if lo < hi:
            part = w3[chip, lo:hi]
            parts.append(part if own is None else jnp.where(me == chip, own[lo:hi], part))
    return parts[0] if len(parts) == 1 else jnp.concatenate(parts, axis=0)


def _pack_small(g1, bfv, mix, scale, g2n, gf, extra=None):
    row8 = jnp.pad(bfv.reshape(1, N_HEADS), ((0, 0), (0, LANES - N_HEADS)))
    if extra is not None:
        row8 = row8 + jnp.pad(extra[:, :1], ((0, 0), (N_HEADS, LANES - N_HEADS - 1)))
    return jnp.concatenate([
        g1.reshape(8, LANES), jnp.pad(row8, ((0, 7), (0, 0))), mix.reshape(512, LANES),
        jnp.pad(scale.reshape(4, LANES), ((0, 4), (0, 0))), g2n.reshape(8, LANES), gf.reshape(8, LANES)], axis=0)


def _unpack_small(s, like):
    g1, bfv, mix, scale, g2n, gf = like
    return (s[0:8].reshape(g1.shape), s[8, :N_HEADS].reshape(bfv.shape), s[16:528].reshape(mix.shape),
            s[528:532].reshape(scale.shape), s[536:544].reshape(g2n.shape), s[544:552].reshape(gf.shape))


class _MeshLinks:
    def __init__(self, shards_in, shards_rest):
        self.gin = _gather_begin("in", shards_in, None, column_halves=True)
        self.grest = _gather_begin("rest", shards_rest, self.gin["token"])
        self.tokens = {"gather": self.grest["token"]}
        self.groups, self.flight, self.slot = {}, None, 0

    @property
    def token(self):
        return list(self.tokens.values())

    def tie(self, x):
        return _tie(x, self.token)

    def weights_in(self, after):
        st = _gather_forward(self.gin, after)
        (g,), (own,) = _gather_end(st, st["token"], merge=False)
        return g, own, 2 * lax.axis_index("x") + lax.axis_index("y")

    def rest_forward(self, after):
        self.grest = _gather_forward(self.grest, after)
        self.tokens["gather"] = self.grest["token"]

    def weights_rest(self, after):
        g = _gather_end(self.grest, after)
        del self.tokens["gather"]
        return [_col_sharded_full(g[0]), _col_sharded_full(g[1])] + [_row_sharded_full(a) for a in g[2:]]

    def advance(self, after, begin=()):
        slot = self.slot
        self.slot += 1
        if self.flight is not None:
            tags, sems, parts = self.flight
            arrays, plan, _ = _join_parts(parts)
            thru = _copies_wait(f"slot{slot}_wait", sems, arrays, plan, after)
            for tag, part in zip(tags, parts):
                self.groups[tag] = _reduce_next(self.groups[tag], thru[:len(part[0])])
                thru = thru[len(part[0]):]
        for st in begin:
            self.groups[st["tag"]] = st
        live = [(tag, st["part"]) for tag, st in self.groups.items() if "part" in st]
        self.flight = None
        self.tokens.pop("reduce", None)
        if live:
            arrays, plan, m = _join_parts([part for _, part in live])
            sems, thru, token = _copies_start(f"slot{slot}_start", arrays, plan, m)
            parts = []
            for _, (part_arrays, part_plan, part_m) in live:
                parts.append((thru[:len(part_arrays)], part_plan, part_m))
                thru = thru[len(part_arrays):]
            self.flight = ([tag for tag, _ in live], sems, parts)
            self.tokens["reduce"] = token

    def reduced(self, tag):
        return self.groups[tag]["done"]


class _NoLinks:
    token = None

    def __init__(self, w_in, rest):
        self.w_in, self.rest, self.grads = w_in, rest, {}

    def tie(self, x):
        return x

    def weights_in(self, after):
        return self.w_in, None, None

    def rest_forward(self, after):
        pass

    def weights_rest(self, after):
        return self.rest

    def advance(self, after, begin=()):
        for st in begin:
            self.grads[st["tag"]] = st["grads"]


def _local_step(links, x, target, seq, norm1_g, b_forget, pool_mix, pool_scale, norm2_g, norm_f_g, between=None):
    t, d = x.shape
    tq = min(256, seq)
    aw = ATTN_WIDTH
    o_q, o_f, o_g = POOL_WIDTH, POOL_WIDTH + 3 * aw, POOL_WIDTH + 3 * aw + N_HEADS
    bf = jnp.pad(b_forget, ((0, 0), (0, LANES - N_HEADS)))
    mixb = pool_mix.astype(BF16)

    h = _norm_fwd("norm1_fwd", x, links.tie(norm1_g), 512)
    w_in, own, me = links.weights_in(h)
    wu = _chip_rows(w_in, 0, o_q, own, me)
    wqkv = _chip_rows(w_in, o_q, o_f, own, me)
    wft = jnp.pad(_chip_rows(w_in, o_f, o_g, own, me), ((0, LANES - N_HEADS), (0, 0)))
    wg2 = _chip_rows(w_in, o_g, N_CHIPS * w_in.shape[1], own, me)
    wf = wft.T
    u = _matmul("mm_u", h, wu, "nt", F32, 1024, 512, d)
    g2 = _matmul("mm_gates", h, wg2, "nt", BF16, 1024, 1024, d)
    fl, fcum = _forget_fwd(h, wf, bf, seq)
    qa, ka, v = _attn_prep(h, _head_blocks(wqkv[:aw]), _head_blocks(wqkv[aw:2 * aw]), wqkv[2 * aw:], fcum, 1024)
    p, ps = _pool_fwd(u, mixb, pool_scale, seq)
    links.rest_forward([ps, qa, g2])
    o, lse = _attn_fwd(qa, ka, v, seq, tq, dep=links.token)
    w_pool_out, w_attn_out, w_out, w_ffn_gate, w_ffn_up, w_ffn_down = links.weights_rest(o)
    merged, x1 = _merge_fwd(x, ps, o, g2, w_pool_out, w_attn_out, w_out, 512)
    h2, gt, up, act, x2 = _ffn_fwd(x1, norm2_g, w_ffn_gate, w_ffn_up, w_ffn_down, 1024, 256)
    loss, dx2, d_gf = _final_fwd_bwd(x2, target, norm_f_g, 512)

    dgt, dup, dx1, d_g2n = _ffn_bwd(dx2, x1, norm2_g, gt, up, w_ffn_gate, w_ffn_up, w_ffn_down, 1024, 256)
    d_wd = _matmul("dw_down", act, dx2, "tn", F32, 1408, 1024, 1024)
    d_wg = _matmul("dw_gate", dgt, h2, "tn", F32, 1408, 1024, 1024)
    d_wu = _matmul("dw_up", dup, h2, "tn", F32, 1408, 1024, 1024)
    links.advance(None, begin=[_reduce_begin("a", [_row_sharded_to_comm(g) for g in (d_wg, d_wu, d_wd)])])
    dpy, day, dg2, dps, da = _merge_bwd(dx1, ps, o, g2, w_pool_out, w_attn_out, w_out, 512, dep=links.token)
    links.advance(dps)
    d_wout = _matmul("dw_out", merged, dx1, "tn", F32, 1024, 1024, 1024)
    d_wpo = _matmul("dw_pool_out", ps, dpy, "tn", F32, 512, 1024, 1024)
    d_wao = _matmul("dw_attn_out", o, day, "tn", F32, 512, 1024, 1024)
    dq, dk, dv, dfr = _attn_bwd(qa, ka, v, da, lse, seq, tq, dep=links.token)
    links.advance(dq, begin=[_reduce_begin(
        "m", [_col_sharded_to_comm(d_wpo), _col_sharded_to_comm(d_wao), _row_sharded_to_comm(d_wout)])])
    dfc = jnp.pad(dfr.reshape(N_HEADS, t).T, ((0, 0), (0, LANES - N_HEADS)))
    dfl, d_bf = _forget_bwd(dfc, fl, bf, seq)
    du, d_mix, d_scale = _pool_bwd(dps, p, mixb, links.tie(pool_scale), seq)
    d_wu_in = _matmul("dw_in_u", du, h, "tn", F32, 512, 1024, 1024, dep=links.token)
    small = (jnp.zeros_like(norm1_g), d_bf[:, :N_HEADS], d_mix, d_scale, d_g2n, d_gf)
    d_wq = _matmul("dw_in_q", dq, h, "tn", F32, 512, 1024, 1024, dep=links.token)
    d_wk = _matmul("dw_in_k", dk, h, "tn", F32, 512, 1024, 1024, dep=links.token)
    d_wv = _matmul("dw_in_v", dv, h, "tn", F32, 512, 1024, 1024, dep=links.token)
    links.advance([d_wu_in, d_wq, d_wk, d_wv], begin=[_small_begin("small", _pack_small(*small, extra=loss))])
    d_wf = _matmul("dw_in_f", dfl, h, "tn", F32, LANES, 1024, 512)
    d_wg2 = _matmul("dw_in_gates", dg2, h, "tn", F32, 1024, 1024, 1024, dep=links.token)
    d_win = jnp.concatenate([d_wu_in, d_wq, d_wk, d_wv, d_wf[:N_HEADS], d_wg2], axis=0)
    comm_b = [d_win.reshape(N_CHIPS, d_win.shape[0] // N_CHIPS, d)]
    links.advance(comm_b, begin=[_reduce_begin("b", comm_b, column_halves=True)])
    if between is not None:
        between()
    dx, d_g1 = _in_bwd(du, dq, dk, dv, dg2, dfl, dx1, x, links.tie(norm1_g), wu, wqkv, wg2, wft, 512)
    return loss, dx, (d_g1,) + small[1:]


def kernel(x, norm1_g, w_in, b_forget, pool_mix, pool_scale, w_pool_out, w_attn_out, w_out, norm2_g, w_ffn_gate, w_ffn_up, w_ffn_down, norm_f_g, loss_target, m_norm1_g, m_w_in, m_b_forget, m_pool_mix, m_pool_scale, m_w_pool_out, m_w_attn_out, m_w_out, m_norm2_g, m_w_ffn_gate, m_w_ffn_up, m_w_ffn_down, m_norm_f_g, v_norm1_g, v_w_in, v_b_forget, v_pool_mix, v_pool_scale, v_w_pool_out, v_w_attn_out, v_w_out, v_norm2_g, v_w_ffn_gate, v_w_ffn_up, v_w_ffn_down, v_norm_f_g):
    nb, seq, d = x.shape
    group_a = ((w_ffn_gate, m_w_ffn_gate, v_w_ffn_gate, True, 9), (w_ffn_up, m_w_ffn_up, v_w_ffn_up, True, 10),
               (w_ffn_down, m_w_ffn_down, v_w_ffn_down, False, 11))
    group_m = ((w_pool_out, m_w_pool_out, v_w_pool_out, False, 5), (w_attn_out, m_w_attn_out, v_w_attn_out, False, 6),
               (w_out, m_w_out, v_w_out, False, 7))
    group_b = ((w_in, m_w_in, v_w_in, False, 1),)
    small_w = (norm1_g, b_forget, pool_mix, pool_scale, norm2_g, norm_f_g)
    small_m = (m_norm1_g, m_b_forget, m_pool_mix, m_pool_scale, m_norm2_g, m_norm_f_g)
    small_v = (v_norm1_g, v_b_forget, v_pool_mix, v_pool_scale, v_norm2_g, v_norm_f_g)
    small_pos = (0, 2, 3, 4, 8, 12)
    view = lambda a, tr: a[0].T if tr else a[0]
    unview = lambda a, tr, like: (a.T if tr else a).reshape(like.shape)

    def shard(w, tr):
        lw = view(w, tr).astype(BF16)
        return lw.reshape(2, lw.shape[0] // 2, lw.shape[1])

    cm = lambda a: jnp.transpose(a, (2, 0, 1))
    shard_in = _rows_to_bf16("w_in_to_bf16", cm(w_in))
    links = _MeshLinks([shard_in],
                       [shard(w_pool_out, False), shard(w_attn_out, False), shard(w_out, False),
                        shard(w_ffn_gate, True), shard(w_ffn_up, True), shard(w_ffn_down, False)])
    grads, deltas, new_m, new_v = [None] * 13, [None] * 13, [None] * 13, [None] * 13
    pos_c = jnp.stack([lax.axis_index("c")]).astype(jnp.int32)

    def update(tag, group, dep, members=(0, 1, 2)):
        last = []
        reduced = links.reduced(tag)
        for k in members:
            (w, m, v, tr, pos), (mine, other) = group[k], reduced[k]
            outs = _adamw_halves(f"adamw_{tag}{k}", pos_c, view(w, tr), mine, other, view(m, tr), view(v, tr), 256,
                                 dep=dep)
            grads[pos], deltas[pos], new_m[pos], new_v[pos] = (unview(a, tr, w) for a in outs)
            last.append(outs[1])
        return last

    def update_a():
        links.advance(update("a", group_a, links.token, members=(0,)))

    loss, dx, small_g = _local_step(
        links, x.reshape(nb * seq, d), loss_target.reshape(nb * seq, d), seq,
        norm1_g, b_forget, pool_mix[0], pool_scale, norm2_g, norm_f_g.reshape(1, d), between=update_a)

    links.advance(dx, begin=[_small_begin("g1", small_g[0].reshape(8, LANES))])
    last = update("m", group_m, links.token) + update("a", group_a, links.token, members=(2,))
    small_rest = _small_sum("small_sum", *links.reduced("small")[0])
    links.advance(last + [small_rest])
    last = update("a", group_a, links.token, members=(1,))
    small_sum = jnp.concatenate([_small_sum("g1_sum", *links.reduced("g1")[0]), small_rest[8:]], axis=0)
    loss_out = small_sum[8, N_HEADS]
    dl, mn, vn = _adamw("adamw_small", _pack_small(*small_w), small_sum * _small_mask(), _pack_small(*small_m),
                        _pack_small(*small_v))
    for pos, g, a, b, e in zip(small_pos, _unpack_small(small_sum, small_w), _unpack_small(dl, small_w),
                               _unpack_small(mn, small_w), _unpack_small(vn, small_w)):
        grads[pos], deltas[pos], new_m[pos], new_v[pos] = g, a, b, e
    links.advance(last + [dl])
    (mine, other), = links.reduced("b")
    outs = _adamw_rows("adamw_b0", pos_c, cm(w_in), mine, other, cm(m_w_in), cm(v_w_in))
    grads[1], deltas[1], new_m[1], new_v[1] = (jnp.transpose(a, (1, 2, 0)) for a in outs)

    return (loss_out, dx.reshape(nb, seq, d), *grads, *deltas, *new_m, *new_v)


def _small_mask():
    rows = lax.broadcasted_iota(jnp.int32, (552, LANES), 0)
    lanes = lax.broadcasted_iota(jnp.int32, (552, LANES), 1)
    return jnp.where(jnp.logical_and(rows == 8, lanes == N_HEADS), 0.0, 1.0).astype(F32)
```

```python
import jax
import jax.numpy as jnp
from jax import lax
from jax.experimental import pallas as pl
from jax.experimental.pallas import tpu as pltpu

F32 = jnp.float32
BF16 = jnp.bfloat16

D_MODEL = 1024
POOL_WINDOWS = (2, 4, 8, 16)
POOL_GROUPS = 4
POOL_GROUP_DIM = 128
POOL_WIDTH = 512
HEAD_DIM = 64
N_HEADS = 8
ATTN_WIDTH = 512
D_FF = 2816
RMS_EPS = 1e-6
ATTN_SCALE = HEAD_DIM ** -0.5
NEG_BIG = -1e30

ADAM_LR = 0.001
ADAM_B1 = 0.9
ADAM_B2 = 0.999
ADAM_EPS = 1e-08
ADAM_WD = 0.01
ADAM_STEP = 10

LANES = 128
N_CHIPS = 4
N_DEV = 8
VMEM_LIMIT_V7X = 52 * 1024 * 1024
ROW_CHUNK = 256
MESH = pl.DeviceIdType.MESH
ANY = pl.BlockSpec(memory_space=pl.ANY)


def _cparams(*sem):
    return pltpu.CompilerParams(dimension_semantics=sem if sem else None, vmem_limit_bytes=VMEM_LIMIT_V7X)


def _dep_list(dep):
    return [] if dep is None else (list(dep) if isinstance(dep, (list, tuple)) else [dep])


def _after(body, n_in, dep):
    k = len(_dep_list(dep))
    if k == 0:
        return body

    def wrapped(*refs):
        body(*refs[:n_in], *refs[n_in + k:])

    return wrapped


def _dep_args(dep):
    deps = _dep_list(dep)
    return [ANY] * len(deps), deps


def _dot(a, b):
    return lax.dot_general(a, b, (((1,), (0,)), ((), ())), preferred_element_type=F32)


def _dot_nt(a, b):
    return lax.dot_general(a, b, (((1,), (1,)), ((), ())), preferred_element_type=F32)


def _dot_tn(a, b):
    return lax.dot_general(a, b, (((0,), (0,)), ((), ())), preferred_element_type=F32)


def _sigmoid(x):
    return jax.nn.sigmoid(x)


def _rms_fwd(x, g):
    r = lax.rsqrt(jnp.mean(x * x, axis=-1, keepdims=True) + RMS_EPS)
    return (x * r) * g


def _rms_bwd(x, g, dy):
    r = lax.rsqrt(jnp.mean(x * x, axis=-1, keepdims=True) + RMS_EPS)
    xh = x * r
    dg = jnp.sum(dy * xh, axis=0, keepdims=True)
    dxh = dy * g
    dx = r * (dxh - xh * jnp.mean(dxh * xh, axis=-1, keepdims=True))
    return dx, dg


def _matmul(name, a, b, mode, out_dtype, tm, tn, tk, dep=None):
    if mode == "nn":
        (m, k), (_, n) = a.shape, b.shape
    elif mode == "nt":
        (m, k), (n, _) = a.shape, b.shape
    else:
        (k, m), (_, n) = a.shape, b.shape
    tm, tn, tk = min(tm, m), min(tn, n), min(tk, k)
    assert m % tm == 0 and n % tn == 0 and k % tk == 0, (name, m, n, k, tm, tn, tk)
    nk = k // tk
    if mode == "tn":
        a_spec = pl.BlockSpec((tk, tm), lambda i, j, kk: (kk, i))
    else:
        a_spec = pl.BlockSpec((tm, tk), lambda i, j, kk: (i, kk))
    if mode == "nt":
        b_spec = pl.BlockSpec((tn, tk), lambda i, j, kk: (j, kk))
    else:
        b_spec = pl.BlockSpec((tk, tn), lambda i, j, kk: (kk, j))
    dot = {"nn": _dot, "nt": _dot_nt, "tn": _dot_tn}[mode]
    use_scratch = nk > 1 and out_dtype != F32

    def body(a_ref, b_ref, o_ref, *scratch):
        if nk == 1 and mode != "tn":
            rows = min(ROW_CHUNK, tm)
            bb = b_ref[...].astype(BF16)
            for r0 in range(0, tm, rows):
                o_ref[r0:r0 + rows, :] = dot(a_ref[r0:r0 + rows, :].astype(BF16), bb).astype(out_dtype)
            return
        prod = dot(a_ref[...].astype(BF16), b_ref[...].astype(BF16))
        if nk == 1:
            o_ref[...] = prod.astype(out_dtype)
            return
        acc = scratch[0] if use_scratch else o_ref
        kk = pl.program_id(2)

        @pl.when(kk == 0)
        def _():
            acc[...] = prod

        @pl.when(kk > 0)
        def _():
            acc[...] += prod

        if use_scratch:
            @pl.when(kk == nk - 1)
            def _():
                o_ref[...] = acc[...].astype(out_dtype)

    dep_specs, dep_ops = _dep_args(dep)
    return pl.pallas_call(
        _after(body, 2, dep),
        name=name,
        out_shape=jax.ShapeDtypeStruct((m, n), out_dtype),
        grid=(m // tm, n // tn, nk),
        in_specs=[a_spec, b_spec] + dep_specs,
        out_specs=pl.BlockSpec((tm, tn), lambda i, j, kk: (i, j)),
        scratch_shapes=[pltpu.VMEM((tm, tn), F32)] if use_scratch else [],
        compiler_params=_cparams("parallel", "parallel", "arbitrary"),
    )(a, b, *dep_ops)


def _norm_fwd(name, x, g, tm):
    t, d = x.shape
    tm = min(tm, t)

    def body(x_ref, g_ref, h_ref):
        h_ref[...] = _rms_fwd(x_ref[...], g_ref[...]).astype(BF16)

    return pl.pallas_call(
        body, name=name, out_shape=jax.ShapeDtypeStruct((t, d), BF16), grid=(t // tm,),
        in_specs=[pl.BlockSpec((tm, d), lambda i: (i, 0)), pl.BlockSpec((1, d), lambda i: (0, 0))],
        out_specs=pl.BlockSpec((tm, d), lambda i: (i, 0)),
        compiler_params=_cparams("parallel"),
    )(x, g)


def _split3(x):
    hi = x.astype(BF16)
    r1 = x - hi.astype(F32)
    mid = r1.astype(BF16)
    lo = (r1 - mid.astype(F32)).astype(BF16)
    return hi, mid, lo


def _tri_dot(tri, x):
    hi, mid, lo = _split3(x)
    return _dot(tri, hi) + _dot(tri, mid) + _dot(tri, lo)


def _forget_fwd(h, wf, bf, seq):
    t, d = h.shape
    cb = min(256, seq)

    def body(h_ref, wf_ref, bf_ref, fl_ref, fc_ref):
        fl = _dot(h_ref[...], wf_ref[...])
        fl_ref[...] = fl
        xx = fl + bf_ref[...]
        lf = jnp.minimum(xx, 0.0) - jnp.log(1.0 + jnp.exp(-jnp.abs(xx)))
        ri = lax.broadcasted_iota(jnp.int32, (cb, cb), 0)
        ci = lax.broadcasted_iota(jnp.int32, (cb, cb), 1)
        tri = (ri >= ci).astype(BF16)
        carry = jnp.zeros((1, LANES), F32)
        for blk in range(seq // cb):
            cs = _tri_dot(tri, lf[blk * cb:(blk + 1) * cb]) + carry
            fc_ref[blk * cb:(blk + 1) * cb, :] = cs
            carry = cs[cb - 1:cb, :]

    return pl.pallas_call(
        body, name="forget_fwd",
        out_shape=(jax.ShapeDtypeStruct((t, LANES), F32), jax.ShapeDtypeStruct((t, LANES), F32)),
        grid=(t // seq,),
        in_specs=[pl.BlockSpec((seq, d), lambda b: (b, 0)), pl.BlockSpec((d, LANES), lambda b: (0, 0)),
                  pl.BlockSpec((1, LANES), lambda b: (0, 0))],
        out_specs=(pl.BlockSpec((seq, LANES), lambda b: (b, 0)), pl.BlockSpec((seq, LANES), lambda b: (b, 0))),
        compiler_params=_cparams("parallel"),
    )(h, wf, bf)


def _pool_fwd(u, mix, scale, seq):
    t = u.shape[0]

    def body(u_ref, mix_ref, sc_ref, p_ref, ps_ref):
        tpos = lax.broadcasted_iota(jnp.int32, (seq, POOL_GROUP_DIM), 0)
        for g in range(POOL_GROUPS):
            sl = slice(g * POOL_GROUP_DIM, (g + 1) * POOL_GROUP_DIM)
            ug = u_ref[:, sl]
            s = ug
            for lvl in range(g + 1):
                d = 2 ** lvl
                s = s + jnp.where(tpos >= d, pltpu.roll(s, d, 0), 0.0)
            cnt = jnp.minimum(tpos + 1, POOL_WINDOWS[g]).astype(F32)
            pb = (s / cnt - ug).astype(BF16)
            p_ref[:, sl] = pb
            ps_ref[:, sl] = (_dot(pb, mix_ref[g]) * sc_ref[:, sl]).astype(BF16)

    return pl.pallas_call(
        body, name="pool_fwd",
        out_shape=(jax.ShapeDtypeStruct((t, POOL_WIDTH), BF16), jax.ShapeDtypeStruct((t, POOL_WIDTH), BF16)),
        grid=(t // seq,),
        in_specs=[pl.BlockSpec((seq, POOL_WIDTH), lambda b: (b, 0)),
                  pl.BlockSpec((POOL_GROUPS, POOL_GROUP_DIM, POOL_GROUP_DIM), lambda b: (0, 0, 0)),
                  pl.BlockSpec((1, POOL_WIDTH), lambda b: (0, 0))],
        out_specs=(pl.BlockSpec((seq, POOL_WIDTH), lambda b: (b, 0)), pl.BlockSpec((seq, POOL_WIDTH), lambda b: (b, 0))),
        compiler_params=_cparams("parallel"),
    )(u, mix, scale)


def _aug_constants():
    w = N_HEADS * LANES
    rows = jnp.arange(3 * LANES)
    piece, head = rows // LANES, rows % LANES
    cols = jnp.arange(w)
    live = (head < N_HEADS)[:, None]
    pq = (live & (cols[None, :] == (head * LANES + HEAD_DIM + piece)[:, None])).astype(BF16)
    pk = -(live & (cols[None, :] == (head * LANES + HEAD_DIM + 3 + piece)[:, None])).astype(BF16)
    lane = cols % LANES
    oq = ((lane >= HEAD_DIM + 3) & (lane < HEAD_DIM + 6)).astype(F32)[None, :]
    ok = ((lane >= HEAD_DIM) & (lane < HEAD_DIM + 3)).astype(F32)[None, :]
    return pq, pk, oq, ok


def _head_blocks(wt):
    d = wt.shape[1]
    return jnp.pad(wt.reshape(N_HEADS, HEAD_DIM, d), ((0, 0), (0, LANES - HEAD_DIM), (0, 0))).reshape(N_HEADS * LANES, d)


def _attn_prep(h, wq, wk, wv, fcum, tm):
    t, d = h.shape
    tm = min(tm, t)
    rows = min(ROW_CHUNK, tm)
    w = N_HEADS * LANES
    pq, pk, oq, ok = _aug_constants()

    def body(h_ref, wq_ref, wk_ref, wv_ref, f_ref, pq_ref, pk_ref, oq_ref, ok_ref, qa_ref, ka_ref, v_ref):
        for r0 in range(0, tm, rows):
            rs = slice(r0, r0 + rows)
            hh = h_ref[rs, :]
            fs = jnp.concatenate(_split3(f_ref[rs, :]), axis=1)
            q = _dot_nt(hh, wq_ref[...]).astype(BF16).astype(F32) * ATTN_SCALE
            qa_ref[rs, :] = (q + _dot(fs, pq_ref[...]) + oq_ref[...]).astype(BF16)
            k = _dot_nt(hh, wk_ref[...]).astype(BF16).astype(F32)
            ka_ref[rs, :] = (k + _dot(fs, pk_ref[...]) + ok_ref[...]).astype(BF16)
            v_ref[rs, :] = _dot_nt(hh, wv_ref[...]).astype(BF16)

    row = lambda n: pl.BlockSpec((tm, n), lambda i: (i, 0))
    full = lambda a: pl.BlockSpec(a.shape, lambda i: (0, 0))
    return pl.pallas_call(
        body, name="attn_prep",
        out_shape=(jax.ShapeDtypeStruct((t, w), BF16), jax.ShapeDtypeStruct((t, w), BF16),
                   jax.ShapeDtypeStruct((t, ATTN_WIDTH), BF16)),
        grid=(t // tm,),
        in_specs=[row(d), full(wq), full(wk), full(wv), row(LANES), full(pq), full(pk), full(oq), full(ok)],
        out_specs=(row(w), row(w), row(ATTN_WIDTH)),
        compiler_params=_cparams("parallel"),
    )(h, wq, wk, wv, fcum, pq, pk, oq, ok)


def _fold_lanes(x, op):
    out = x[:, :LANES]
    for g in range(1, x.shape[1] // LANES):
        out = op(out, x[:, g * LANES:(g + 1) * LANES])
    return out


def _causal_sweep(i, tile, carry):
    def quad(jj, c):
        for u in range(4):
            c = tile(4 * jj + u, c, False)
        return c

    carry = lax.fori_loop(0, i // 4, quad, carry)
    base = 4 * (i // 4)
    carry = lax.cond(i % 4 >= 2, lambda c: tile(base + 1, tile(base, c, False), False), lambda c: c, carry)
    return lax.cond(i % 2 == 1, lambda c: tile(i, tile(i - 1, c, False), True), lambda c: tile(i, c, True), carry)


def _attn_fwd(qa, ka, v, seq, tq, dep=None):
    t = qa.shape[0]
    nq = seq // tq
    hp_n = N_HEADS // 2
    heads = [slice(e * LANES, (e + 1) * LANES) for e in range(2)]

    def body(q_ref, k_ref, v_ref, o_ref, lse_ref, s_buf):
        i = pl.program_id(2)
        diag_ok = lax.broadcasted_iota(jnp.int32, (tq, tq), 0) >= lax.broadcasted_iota(jnp.int32, (tq, tq), 1)
        qs = [q_ref[:, hl] for hl in heads]

        def sweep1(j, mxs, diagonal):
            r0 = pl.multiple_of(j * tq, tq)
            out = []
            for e, hl in enumerate(heads):
                s = _dot_nt(qs[e], k_ref[pl.ds(r0, tq), hl])
                if diagonal:
                    s = jnp.where(diag_ok, s, NEG_BIG)
                s_buf[e, j] = s
                out.append(jnp.maximum(mxs[e], _fold_lanes(s, jnp.maximum)))
            return tuple(out)

        mxs = _causal_sweep(i, sweep1, (jnp.full((tq, LANES), NEG_BIG, F32),) * 2)
        ms = [jnp.max(mx, axis=1, keepdims=True) for mx in mxs]

        def sweep2(j, carry, diagonal):
            r0 = pl.multiple_of(j * tq, tq)
            vv = v_ref[pl.ds(r0, tq), :]
            out = []
            for e in range(2):
                p = jnp.exp(s_buf[e, j] - ms[e])
                out += [carry[2 * e] + _fold_lanes(p, jnp.add), carry[2 * e + 1] + _dot(p.astype(BF16), vv)]
            return tuple(out)

        res = _causal_sweep(i, sweep2, (jnp.zeros((tq, LANES), F32),) * 4)
        outs = []
        for e in range(2):
            l = jnp.sum(res[2 * e], axis=1, keepdims=True)
            outs.append(res[2 * e + 1] / l)
            lse_ref[:, e:e + 1] = ms[e] + jnp.log(l)
        lane = lax.broadcasted_iota(jnp.int32, (tq, LANES), 1)
        o_ref[...] = jnp.where(lane < HEAD_DIM, outs[0], outs[1])

    dep_specs, dep_ops = _dep_args(dep)
    return pl.pallas_call(
        _after(body, 3, dep), name="attn_fwd",
        out_shape=(jax.ShapeDtypeStruct((t, ATTN_WIDTH), F32), jax.ShapeDtypeStruct((hp_n, t, 2), F32)),
        grid=(t // seq, hp_n, nq),
        in_specs=[pl.BlockSpec((tq, 2 * LANES), lambda b, hp, i: (b * nq + i, hp)),
                  pl.BlockSpec((seq, 2 * LANES), lambda b, hp, i: (b, hp)),
                  pl.BlockSpec((seq, LANES), lambda b, hp, i: (b, hp))] + dep_specs,
        out_specs=(pl.BlockSpec((tq, LANES), lambda b, hp, i: (b * nq + i, hp)),
                   pl.BlockSpec((None, tq, 2), lambda b, hp, i: (hp, b * nq + i, 0))),
        scratch_shapes=[pltpu.VMEM((2, nq, tq, tq), F32)],
        compiler_params=_cparams("parallel", "parallel", "arbitrary"),
    )(qa, ka, v, *dep_ops)


def _merge_fwd(x, ps, o, g2, wpo, wao, wout, tm):
    t, d = x.shape
    tm = min(tm, t)
    rows = min(ROW_CHUNK, tm)

    def body(x_ref, ps_ref, o_ref, gp_ref, ga_ref, wpo_ref, wao_ref, wout_ref, mg_ref, x1_ref):
        for r0 in range(0, tm, rows):
            rs = slice(r0, r0 + rows)
            py = _dot(ps_ref[rs, :], wpo_ref[...])
            ay = _dot(o_ref[rs, :].astype(BF16), wao_ref[...])
            mb = (_sigmoid(gp_ref[rs, :].astype(F32)) * py + _sigmoid(ga_ref[rs, :].astype(F32)) * ay).astype(BF16)
            mg_ref[rs, :] = mb
            x1_ref[rs, :] = x_ref[rs, :] + _dot(mb, wout_ref[...])

    row = lambda w: pl.BlockSpec((tm, w), lambda i: (i, 0))
    full = lambda a: pl.BlockSpec(a.shape, lambda i: (0, 0))
    return pl.pallas_call(
        body, name="merge_fwd",
        out_shape=(jax.ShapeDtypeStruct((t, d), BF16), jax.ShapeDtypeStruct((t, d), F32)),
        grid=(t // tm,),
        in_specs=[row(d), row(POOL_WIDTH), row(ATTN_WIDTH), pl.BlockSpec((tm, d), lambda i: (i, 0)),
                  pl.BlockSpec((tm, d), lambda i: (i, 1)), full(wpo), full(wao), full(wout)],
        out_specs=(row(d), row(d)),
        compiler_params=_cparams("parallel"),
    )(x, ps, o, g2, g2, wpo, wao, wout)


def _ffn_fwd(x1, g, wg, wu, wd, tm, tf):
    t, d = x1.shape
    f = wg.shape[0]
    tm = min(tm, t)
    nf = f // tf
    rows = min(512, tm)

    def body(x1_ref, g_ref, wg_ref, wu_ref, wd_ref, h2_ref, gt_ref, up_ref, act_ref, x2_ref):
        j = pl.program_id(1)

        @pl.when(j == 0)
        def _():
            h2_ref[...] = _rms_fwd(x1_ref[...], g_ref[...]).astype(BF16)

            x2_ref[...] = x1_ref[...]

        for r0 in range(0, tm, rows):
            rs = slice(r0, r0 + rows)
            h2 = h2_ref[rs, :]
            gt = _dot_nt(h2, wg_ref[...])
            up = _dot_nt(h2, wu_ref[...])
            sg = _sigmoid(gt)
            silu = gt * sg
            act = (silu * up).astype(BF16)
            gt_ref[rs, :] = (up * (sg * (1.0 + gt * (1.0 - sg)))).astype(BF16)
            up_ref[rs, :] = silu.astype(BF16)
            act_ref[rs, :] = act
            x2_ref[rs, :] += _dot(act, wd_ref[...])

    return pl.pallas_call(
        body, name="ffn_fwd",
        out_shape=(jax.ShapeDtypeStruct((t, d), BF16), jax.ShapeDtypeStruct((t, f), BF16),
                   jax.ShapeDtypeStruct((t, f), BF16), jax.ShapeDtypeStruct((t, f), BF16),
                   jax.ShapeDtypeStruct((t, d), F32)),
        grid=(t // tm, nf),
        in_specs=[pl.BlockSpec((tm, d), lambda i, j: (i, 0)), pl.BlockSpec((1, d), lambda i, j: (0, 0)),
                  pl.BlockSpec((tf, d), lambda i, j: (j, 0)), pl.BlockSpec((tf, d), lambda i, j: (j, 0)),
                  pl.BlockSpec((tf, d), lambda i, j: (j, 0))],
        out_specs=(pl.BlockSpec((tm, d), lambda i, j: (i, 0)), pl.BlockSpec((tm, tf), lambda i, j: (i, j)),
                   pl.BlockSpec((tm, tf), lambda i, j: (i, j)), pl.BlockSpec((tm, tf), lambda i, j: (i, j)),
                   pl.BlockSpec((tm, d), lambda i, j: (i, 0))),
        compiler_params=_cparams("parallel", "arbitrary"),
    )(x1, g, wg, wu, wd)


def _final_fwd_bwd(x2, target, g, tm):
    t, d = x2.shape
    tm = min(tm, t)

    def body(x_ref, t_ref, g_ref, loss_ref, dx_ref, dg_ref):
        i = pl.program_id(0)
        x = x_ref[...]
        gg = g_ref[...]
        err = _rms_fwd(x, gg) - t_ref[...]
        part = 0.5 * jnp.sum(jnp.mean(err * err, axis=-1, keepdims=True), axis=0, keepdims=True)
        dx, dg = _rms_bwd(x, gg, err * (1.0 / d))
        dx_ref[...] = dx

        @pl.when(i == 0)
        def _():
            loss_ref[...] = jnp.zeros_like(loss_ref)
            dg_ref[...] = jnp.zeros_like(dg_ref)

        loss_ref[...] += jnp.broadcast_to(part, loss_ref.shape)
        dg_ref[...] += dg

    return pl.pallas_call(
        body, name="final_fwd_bwd",
        out_shape=(jax.ShapeDtypeStruct((1, LANES), F32), jax.ShapeDtypeStruct((t, d), F32),
                   jax.ShapeDtypeStruct((1, d), F32)),
        grid=(t // tm,),
        in_specs=[pl.BlockSpec((tm, d), lambda i: (i, 0)), pl.BlockSpec((tm, d), lambda i: (i, 0)),
                  pl.BlockSpec((1, d), lambda i: (0, 0))],
        out_specs=(pl.BlockSpec((1, LANES), lambda i: (0, 0)), pl.BlockSpec((tm, d), lambda i: (i, 0)),
                   pl.BlockSpec((1, d), lambda i: (0, 0))),
        compiler_params=_cparams("arbitrary"),
    )(x2, target, g)


def _ffn_bwd(dx2, x1, g, gt, up, wg, wu, wd, tm, tf):
    t, d = dx2.shape
    f = gt.shape[1]
    tm = min(tm, t)
    nf = f // tf
    wgu = jnp.concatenate([wg.reshape(nf, tf, d), wu.reshape(nf, tf, d)], axis=1).reshape(2 * f, d)
    rows = min(256, tm)

    def body(dx2_ref, x1_ref, g_ref, gt_ref, up_ref, wgu_ref, wd_ref, dgt_ref, dup_ref, dx1_ref, dg_ref, acc_ref,
             dxb_ref):
        i, j = pl.program_id(0), pl.program_id(1)

        @pl.when(j == 0)
        def _():
            dxb_ref[...] = dx2_ref[...].astype(BF16)
            acc_ref[...] = jnp.zeros_like(acc_ref)

        for r0 in range(0, tm, rows):
            rs = slice(r0, r0 + rows)
            dact = _dot_nt(dxb_ref[rs, :], wd_ref[...])
            dgt = (dact * gt_ref[rs, :].astype(F32)).astype(BF16)
            dup = (dact * up_ref[rs, :].astype(F32)).astype(BF16)
            dgt_ref[rs, :] = dgt
            dup_ref[rs, :] = dup
            acc_ref[rs, :] += _dot(jnp.concatenate([dgt, dup], axis=1), wgu_ref[...])

        @pl.when(jnp.logical_and(i == 0, j == 0))
        def _():
            dg_ref[...] = jnp.zeros_like(dg_ref)

        @pl.when(j == nf - 1)
        def _():
            dxn, dg = _rms_bwd(x1_ref[...], g_ref[...], acc_ref[...])
            dx1_ref[...] = dx2_ref[...] + dxn
            dg_ref[...] += dg

    return pl.pallas_call(
        body, name="ffn_bwd",
        out_shape=(jax.ShapeDtypeStruct((t, f), BF16), jax.ShapeDtypeStruct((t, f), BF16),
                   jax.ShapeDtypeStruct((t, d), F32), jax.ShapeDtypeStruct((1, d), F32)),
        grid=(t // tm, nf),
        in_specs=[pl.BlockSpec((tm, d), lambda i, j: (i, 0)), pl.BlockSpec((tm, d), lambda i, j: (i, 0)),
                  pl.BlockSpec((1, d), lambda i, j: (0, 0)),
                  pl.BlockSpec((tm, tf), lambda i, j: (i, j)), pl.BlockSpec((tm, tf), lambda i, j: (i, j)),
                  pl.BlockSpec((2 * tf, d), lambda i, j: (j, 0)), pl.BlockSpec((tf, d), lambda i, j: (j, 0))],
        out_specs=(pl.BlockSpec((tm, tf), lambda i, j: (i, j)), pl.BlockSpec((tm, tf), lambda i, j: (i, j)),
                   pl.BlockSpec((tm, d), lambda i, j: (i, 0)), pl.BlockSpec((1, d), lambda i, j: (0, 0))),
        scratch_shapes=[pltpu.VMEM((tm, d), F32), pltpu.VMEM((tm, d), BF16)],
        compiler_params=_cparams("arbitrary", "arbitrary"),
    )(dx2, x1, g, gt, up, wgu, wd)


def _merge_bwd(dx1, ps, o, g2, wpo, wao, wout, tm, dep=None):
    t, d = dx1.shape
    tm = min(tm, t)
    rows = min(ROW_CHUNK, tm)

    def body(dx1_ref, ps_ref, o_ref, gp_ref, ga_ref, wpo_ref, wao_ref, wout_ref, dpy_ref, day_ref, dg2_ref, dps_ref, da_ref):
        for r0 in range(0, tm, rows):
            rs = slice(r0, r0 + rows)
            dm = _dot_nt(dx1_ref[rs, :].astype(BF16), wout_ref[...])
            py = _dot(ps_ref[rs, :], wpo_ref[...])
            ay = _dot(o_ref[rs, :].astype(BF16), wao_ref[...])
            sp = _sigmoid(gp_ref[rs, :].astype(F32))
            sa = _sigmoid(ga_ref[rs, :].astype(F32))
            dpy = (dm * sp).astype(BF16)
            day = (dm * sa).astype(BF16)
            dpy_ref[rs, :] = dpy
            day_ref[rs, :] = day
            dg2_ref[rs, :d] = (dm * py * (sp * (1.0 - sp))).astype(BF16)
            dg2_ref[rs, d:] = (dm * ay * (sa * (1.0 - sa))).astype(BF16)
            dps_ref[rs, :] = _dot_nt(dpy, wpo_ref[...])
            da_ref[rs, :] = _dot_nt(day, wao_ref[...]).astype(BF16)

    row = lambda w: pl.BlockSpec((tm, w), lambda i: (i, 0))
    full = lambda a: pl.BlockSpec(a.shape, lambda i: (0, 0))
    dep_specs, dep_ops = _dep_args(dep)
    return pl.pallas_call(
        _after(body, 8, dep), name="merge_bwd",
        out_shape=(jax.ShapeDtypeStruct((t, d), BF16), jax.ShapeDtypeStruct((t, d), BF16),
                   jax.ShapeDtypeStruct((t, 2 * d), BF16), jax.ShapeDtypeStruct((t, POOL_WIDTH), F32),
                   jax.ShapeDtypeStruct((t, ATTN_WIDTH), BF16)),
        grid=(t // tm,),
        in_specs=[row(d), row(POOL_WIDTH), row(ATTN_WIDTH), pl.BlockSpec((tm, d), lambda i: (i, 0)),
                  pl.BlockSpec((tm, d), lambda i: (i, 1)), full(wpo), full(wao), full(wout)] + dep_specs,
        out_specs=(row(d), row(d), row(2 * d), row(POOL_WIDTH), row(ATTN_WIDTH)),
        compiler_params=_cparams("parallel"),
    )(dx1, ps, o, g2, g2, wpo, wao, wout, *dep_ops)


def _attn_bwd(qa, ka, v, do, lse4, seq, tq, dep=None):
    t = qa.shape[0]
    nq = seq // tq
    hp_n = N_HEADS // 2
    heads = [slice(e * LANES, (e + 1) * LANES) for e in range(2)]

    def body(q_ref, k_ref, v_ref, do_ref, lse_ref, dq_ref, dk_ref, dv_ref, dfr_ref, dk_acc, dv_acc, p_buf, dp_buf):
        diag_ok = lax.broadcasted_iota(jnp.int32, (tq, tq), 0) >= lax.broadcasted_iota(jnp.int32, (tq, tq), 1)
        lane_q = lax.broadcasted_iota(jnp.int32, (tq, LANES), 1)
        mine_q = [lane_q < HEAD_DIM, lane_q >= HEAD_DIM]
        dv_acc[...] = jnp.zeros_like(dv_acc)
        dk_acc[...] = jnp.zeros_like(dk_acc)
        dfr_ref[...] = jnp.zeros_like(dfr_ref)
        transposed = lambda a: a.astype(F32).T.astype(BF16)

        def q_step(i, _):
            q0 = pl.multiple_of(i * tq, tq)
            qs = [q_ref[pl.ds(q0, tq), hl] for hl in heads]
            dov = do_ref[pl.ds(q0, tq), :]
            dos = [jnp.where(mq, dov, jnp.zeros((), BF16)) for mq in mine_q]
            qts = [transposed(q) for q in qs]
            dots = [transposed(a) for a in dos]
            lss = [lse_ref[pl.ds(q0, tq), e:e + 1] for e in range(2)]

            def sweep1(j, dls, diagonal):
                r0 = pl.multiple_of(j * tq, tq)
                vv = v_ref[pl.ds(r0, tq), :]
                out = []
                for e, hl in enumerate(heads):
                    s = _dot_nt(qs[e], k_ref[pl.ds(r0, tq), hl])
                    if diagonal:
                        s = jnp.where(diag_ok, s, NEG_BIG)
                    p = jnp.exp(s - lss[e])
                    dp = _dot_nt(dos[e], vv)
                    p_buf[e, j] = p
                    dp_buf[e, j] = dp
                    dv_acc[j] += _dot(dots[e], p.astype(BF16))
                    out.append(dls[e] + _fold_lanes(p * dp, jnp.add))
                return tuple(out)

            dls = _causal_sweep(i, sweep1, (jnp.zeros((tq, LANES), F32),) * 2)
            dls = [jnp.sum(d, axis=1, keepdims=True) for d in dls]

            def sweep2(j, dqs, diagonal):
                r0 = pl.multiple_of(j * tq, tq)
                out = []
                for e, hl in enumerate(heads):
                    ds = p_buf[e, j] * (dp_buf[e, j] - dls[e])
                    dfr_ref[e, pl.ds(j, 1), :] += jnp.sum(ds, axis=0, keepdims=True)
                    dsb = ds.astype(BF16)
                    dk_acc[e, j] += _dot(qts[e], dsb)
                    out.append(dqs[e] + _dot(dsb, k_ref[pl.ds(r0, tq), hl]))
                return tuple(out)

            dqs = _causal_sweep(i, sweep2, (jnp.zeros((tq, LANES), F32),) * 2)
            dq = jnp.where(mine_q[0], dqs[0], pltpu.roll(dqs[1], HEAD_DIM, 1)) * ATTN_SCALE
            dq_ref[pl.ds(q0, tq), :] = dq.astype(BF16)
            return 0

        lax.fori_loop(0, nq, q_step, 0)
        for j in range(nq):
            rs = slice(j * tq, (j + 1) * tq)
            dk = jnp.where(mine_q[0], dk_acc[0, j].T, pltpu.roll(dk_acc[1, j].T, HEAD_DIM, 1))
            dk_ref[rs, :] = dk.astype(BF16)
            dv_ref[rs, :] = dv_acc[j].T.astype(BF16)

    wide = pl.BlockSpec((seq, 2 * LANES), lambda b, hp: (b, hp))
    col = pl.BlockSpec((seq, LANES), lambda b, hp: (b, hp))
    pair = pl.BlockSpec((None, seq, 2), lambda b, hp: (hp, b, 0))
    dep_specs, dep_ops = _dep_args(dep)
    return pl.pallas_call(
        _after(body, 5, dep), name="attn_bwd",
        out_shape=(jax.ShapeDtypeStruct((t, ATTN_WIDTH), BF16),) * 3 + (jax.ShapeDtypeStruct((N_HEADS, t // tq, tq), F32),),
        grid=(t // seq, hp_n),
        in_specs=[wide, wide, col, col, pair] + dep_specs,
        out_specs=(col, col, col, pl.BlockSpec((2, nq, tq), lambda b, hp: (hp, b, 0))),
        scratch_shapes=[pltpu.VMEM((2, nq, LANES, tq), F32), pltpu.VMEM((nq, LANES, tq), F32),
                        pltpu.VMEM((2, nq, tq, tq), F32), pltpu.VMEM((2, nq, tq, tq), F32)],
        compiler_params=_cparams("parallel", "arbitrary"),
    )(qa, ka, v, do, lse4, *dep_ops)


def _forget_bwd(dfc, fl, bf, seq):
    t = fl.shape[0]
    cb = min(256, seq)
    nb = seq // cb

    def body(dfc_ref, fl_ref, bf_ref, dfl_ref, db_ref):
        b = pl.program_id(0)
        ri = lax.broadcasted_iota(jnp.int32, (cb, cb), 0)
        ci = lax.broadcasted_iota(jnp.int32, (cb, cb), 1)
        tri = (ci >= ri).astype(BF16)
        carry = jnp.zeros((1, LANES), F32)
        dbs = jnp.zeros((1, LANES), F32)
        for blk in reversed(range(nb)):
            rs = slice(blk * cb, (blk + 1) * cb)
            dlf = _tri_dot(tri, -dfc_ref[rs, :]) + carry
            carry = dlf[0:1, :]
            dfl = dlf * _sigmoid(-(fl_ref[rs, :] + bf_ref[...]))
            dfl_ref[rs, :] = dfl.astype(BF16)
            dbs = dbs + jnp.sum(dfl, axis=0, keepdims=True)

        @pl.when(b == 0)
        def _():
            db_ref[...] = jnp.zeros_like(db_ref)

        db_ref[...] += dbs

    return pl.pallas_call(
        body, name="forget_bwd",
        out_shape=(jax.ShapeDtypeStruct((t, LANES), BF16), jax.ShapeDtypeStruct((1, LANES), F32)),
        grid=(t // seq,),
        in_specs=[pl.BlockSpec((seq, LANES), lambda b: (b, 0)), pl.BlockSpec((seq, LANES), lambda b: (b, 0)),
                  pl.BlockSpec((1, LANES), lambda b: (0, 0))],
        out_specs=(pl.BlockSpec((seq, LANES), lambda b: (b, 0)), pl.BlockSpec((1, LANES), lambda b: (0, 0))),
        compiler_params=_cparams("arbitrary"),
    )(dfc, fl, bf)


def _pool_bwd(dps, p, mix, scale, seq):
    t = dps.shape[0]

    def body(dps_ref, p_ref, mix_ref, sc_ref, du_ref, dmix_ref, dsc_ref):
        b = pl.program_id(0)

        @pl.when(b == 0)
        def _():
            dmix_ref[...] = jnp.zeros_like(dmix_ref)
            dsc_ref[...] = jnp.zeros_like(dsc_ref)

        tpos = lax.broadcasted_iota(jnp.int32, (seq, POOL_GROUP_DIM), 0)
        for g in range(POOL_GROUPS):
            sl = slice(g * POOL_GROUP_DIM, (g + 1) * POOL_GROUP_DIM)
            pb = p_ref[:, sl]
            dpsg = dps_ref[:, sl]
            pm = _dot(pb, mix_ref[g])
            dsc_ref[:, sl] += jnp.sum(dpsg * pm, axis=0, keepdims=True)
            dpm = (dpsg * sc_ref[:, sl]).astype(BF16)
            dmix_ref[g] += _dot_tn(pb, dpm)
            dp = _dot_nt(dpm, mix_ref[g])
            cnt = jnp.minimum(tpos + 1, POOL_WINDOWS[g]).astype(F32)
            s = dp / cnt
            for lvl in range(g + 1):
                d = 2 ** lvl
                s = s + jnp.where(tpos < seq - d, pltpu.roll(s, seq - d, 0), 0.0)
            du_ref[:, sl] = (s - dp).astype(BF16)

    return pl.pallas_call(
        body, name="pool_bwd",
        out_shape=(jax.ShapeDtypeStruct((t, POOL_WIDTH), BF16),
                   jax.ShapeDtypeStruct((POOL_GROUPS, POOL_GROUP_DIM, POOL_GROUP_DIM), F32),
                   jax.ShapeDtypeStruct((1, POOL_WIDTH), F32)),
        grid=(t // seq,),
        in_specs=[pl.BlockSpec((seq, POOL_WIDTH), lambda b: (b, 0)), pl.BlockSpec((seq, POOL_WIDTH), lambda b: (b, 0)),
                  pl.BlockSpec((POOL_GROUPS, POOL_GROUP_DIM, POOL_GROUP_DIM), lambda b: (0, 0, 0)),
                  pl.BlockSpec((1, POOL_WIDTH), lambda b: (0, 0))],
        out_specs=(pl.BlockSpec((seq, POOL_WIDTH), lambda b: (b, 0)),
                   pl.BlockSpec((POOL_GROUPS, POOL_GROUP_DIM, POOL_GROUP_DIM), lambda b: (0, 0, 0)),
                   pl.BlockSpec((1, POOL_WIDTH), lambda b: (0, 0))),
        compiler_params=_cparams("arbitrary"),
    )(dps, p, mix, scale)


def _in_bwd(du, dq, dk, dv, dg2, dfl, dx1, x, g, wu, wqkv, wg2, wft, tm):
    t, d = x.shape
    tm = min(tm, t)
    rows = min(ROW_CHUNK, tm)
    aw = ATTN_WIDTH

    def body(du_ref, dq_ref, dk_ref, dv_ref, dg2_ref, dfl_ref, dx1_ref, x_ref, g_ref, wu_ref, wqkv_ref, wg2_ref, wft_ref,
             dx_ref, dg_ref):
        i = pl.program_id(0)

        @pl.when(i == 0)
        def _():
            dg_ref[...] = jnp.zeros_like(dg_ref)

        for r0 in range(0, tm, rows):
            rs = slice(r0, r0 + rows)
            dh = _dot(du_ref[rs, :], wu_ref[...])
            dh += _dot(dq_ref[rs, :], wqkv_ref[0:aw, :])
            dh += _dot(dk_ref[rs, :], wqkv_ref[aw:2 * aw, :])
            dh += _dot(dv_ref[rs, :], wqkv_ref[2 * aw:3 * aw, :])
            dh += _dot(dg2_ref[rs, :], wg2_ref[...])
            dh += _dot(dfl_ref[rs, :], wft_ref[...])
            dxn, dg = _rms_bwd(x_ref[rs, :], g_ref[...], dh)
            dx_ref[rs, :] = dx1_ref[rs, :] + dxn
            dg_ref[...] += dg

    row = lambda w: pl.BlockSpec((tm, w), lambda i: (i, 0))
    full = lambda a: pl.BlockSpec(a.shape, lambda i: (0, 0))
    return pl.pallas_call(
        body, name="in_bwd",
        out_shape=(jax.ShapeDtypeStruct((t, d), F32), jax.ShapeDtypeStruct((1, d), F32)),
        grid=(t // tm,),
        in_specs=[row(POOL_WIDTH), row(aw), row(aw), row(aw), row(2 * d), row(LANES), row(d), row(d),
                  pl.BlockSpec((1, d), lambda i: (0, 0)), full(wu), full(wqkv), full(wg2), full(wft)],
        out_specs=(row(d), pl.BlockSpec((1, d), lambda i: (0, 0))),
        compiler_params=_cparams("arbitrary"),
    )(du, dq, dk, dv, dg2, dfl, dx1, x, g, wu, wqkv, wg2, wft)


def _position():
    return lax.axis_index("x"), lax.axis_index("y"), lax.axis_index("c")


def _remote(src, dst, send_sem, recv_sem, device):
    return pltpu.make_async_remote_copy(src_ref=src, dst_ref=dst, send_sem=send_sem, recv_sem=recv_sem,
                                        device_id=device, device_id_type=MESH)


HBM = pl.BlockSpec(memory_space=pltpu.HBM)
SEM = pl.BlockSpec(memory_space=pltpu.SEMAPHORE)
DATAFLOW = pltpu.SideEffectType.DATAFLOW_SIDE_EFFECTING


def _copies_start(name, arrays, plan, m, dep=None):
    n = len(arrays)
    arrays = [pltpu.with_memory_space_constraint(a, pltpu.HBM) for a in arrays]

    def body(*refs):
        ins, send_sem, recv_sem, token = refs[:n], refs[n], refs[n + 1], refs[2 * n + 2]
        for i, (src, dst, device, _) in enumerate(plan(ins, *_position())):
            _remote(src, dst, send_sem.at[i], recv_sem.at[i], device).start()
        token[...] = jnp.zeros_like(token)

    dep_specs, dep_ops = _dep_args(dep)
    outs = pl.pallas_call(
        _after(body, n, dep), name=name,
        out_shape=(pltpu.SemaphoreType.DMA((m,)), pltpu.SemaphoreType.DMA((m,)),
                   *[pltpu.HBM(a.shape, a.dtype) for a in arrays], jax.ShapeDtypeStruct((8, LANES), F32)),
        in_specs=[HBM] * n + dep_specs, out_specs=(SEM, SEM, *[HBM] * n, pl.BlockSpec(memory_space=pltpu.VMEM)),
        input_output_aliases={i: i + 2 for i in range(n)},
        compiler_params=pltpu.CompilerParams(has_side_effects=DATAFLOW),
    )(*arrays, *dep_ops)
    return (outs[0], outs[1]), list(outs[2:2 + n]), outs[2 + n]


def _copies_wait(name, sems, arrays, plan, after):
    n = len(arrays)
    afters = list(after) if isinstance(after, (list, tuple)) else [after]

    def body(*refs):
        ins, send_sem, recv_sem = refs[:n], refs[n], refs[n + 1]
        for i, (src, dst, device, landing) in enumerate(plan(ins, *_position())):
            _remote(src, dst, send_sem.at[i], recv_sem.at[i], device).wait_send()
            _remote(landing, landing, send_sem.at[i], recv_sem.at[i], device).wait_recv()

    outs = pl.pallas_call(
        body, name=name,
        out_shape=tuple(pltpu.HBM(a.shape, a.dtype) for a in arrays),
        in_specs=[HBM] * n + [SEM, SEM] + [ANY] * len(afters), out_specs=tuple([HBM] * n),
        input_output_aliases={i: i for i in range(n)},
        compiler_params=pltpu.CompilerParams(has_side_effects=DATAFLOW),
    )(*arrays, sems[0], sems[1], *afters)
    return list(outs)


def _tie(x, dep):
    for token in _dep_list(dep):
        x = x + token[0, 0]
    return x


def _other_chips(x, y):
    return [(1 - x, y), (x, 1 - y), (1 - x, 1 - y)]


def _gather_begin(tag, shards, token, column_halves=False):
    n = len(shards)
    lands = [lax.empty((N_CHIPS,) + s.shape, s.dtype) for s in shards]
    if column_halves:
        cols = lambda ref, h: pl.ds(pl.multiple_of(h * (ref.shape[-1] // 2), LANES), ref.shape[-1] // 2)
        mine = lambda ref, h: ref.at[:, cols(ref, h)]
        landed = lambda ref, chip, h: ref.at[chip, :, cols(ref, h)]
    else:
        mine = lambda ref, h: ref.at[h]
        landed = lambda ref, chip, h: ref.at[chip, h]

    def plan(refs, x, y, c):
        return [(mine(refs[k], c), landed(refs[n + k], 2 * x + y, c), (ox, oy, c), landed(refs[n + k], 2 * ox + oy, c))
                for k in range(n) for ox, oy in _other_chips(x, y)]

    sems, thru, token = _copies_start(f"gather_{tag}_ici_start", list(shards) + lands, plan, 3 * n, dep=token)
    return dict(tag=tag, n=n, plan=plan, sems=sems, arrays=thru, token=token, landed=landed)


def _gather_forward(st, after):
    n, tag, landed = st["n"], st["tag"], st["landed"]
    thru = _copies_wait(f"gather_{tag}_ici_wait", st["sems"], st["arrays"], st["plan"], after)

    def plan(refs, x, y, c):
        return [(landed(refs[k], 2 * ox + oy, c), landed(refs[k], 2 * ox + oy, c), (x, y, 1 - c),
                 landed(refs[k], 2 * ox + oy, 1 - c))
                for k in range(n) for ox, oy in _other_chips(x, y)]

    sems, lands, token = _copies_start(f"gather_{tag}_fwd_start", thru[n:], plan, 3 * n)
    return dict(tag=tag, n=n, plan=plan, sems=sems, arrays=lands, token=token, shards=thru[:n])


def _gather_end(st, after, merge=True):
    lands = _copies_wait(f"gather_{st['tag']}_fwd_wait", st["sems"], st["arrays"], st["plan"], after)
    if not merge:
        return lands, st["shards"]
    me = 2 * lax.axis_index("x") + lax.axis_index("y")
    return [lax.dynamic_update_index_in_dim(g, s, me, 0) for g, s in zip(lands, st["shards"])]


def _add_keep_give(name, pos, a, a_keep, a_give, b, b_keep, b_give, steps):
    r, c = b.shape[-2:]

    def spec(arr, fn):
        lead = arr.ndim - 2

        def index(i, p):
            idx = tuple(fn(i, p))
            return idx if len(idx) == arr.ndim else idx + (0, 0)

        return pl.BlockSpec((None,) * lead + (r, c), index)

    out_spec = pl.BlockSpec((None, r, c), lambda i, p: (i, 0, 0))

    def body(p_ref, ak_ref, bk_ref, ag_ref, bg_ref, keep_ref, give_ref):
        keep_ref[...] = ak_ref[...] + bk_ref[...].astype(F32)
        give_ref[...] = (ag_ref[...] + bg_ref[...].astype(F32)).astype(BF16)

    return pl.pallas_call(
        body, name=name,
        out_shape=(jax.ShapeDtypeStruct((steps, r, c), F32), jax.ShapeDtypeStruct((steps, r, c), BF16)),
        grid_spec=pltpu.PrefetchScalarGridSpec(
            num_scalar_prefetch=1, grid=(steps,),
            in_specs=[spec(a, a_keep), spec(b, b_keep), spec(a, a_give), spec(b, b_give)],
            out_specs=(out_spec, out_spec)),
        compiler_params=_cparams("parallel"),
    )(pos, a, b, a, b)


def _add_last(name, a, b):
    _, r, c = a.shape
    blk = pl.BlockSpec((None, r, c), lambda i: (0, 0, 0))

    def body(a_ref, b_ref, o_ref):
        o_ref[...] = a_ref[...] + b_ref[...].astype(F32)

    return pl.pallas_call(
        body, name=name, out_shape=jax.ShapeDtypeStruct((r, c), F32), grid=(1,), in_specs=[blk, blk],
        out_specs=pl.BlockSpec((r, c), lambda i: (0, 0)), compiler_params=_cparams("arbitrary"),
    )(a, b)


def _exchange_part(gives, lands, peer_fn):
    n = len(gives)

    def plan(refs, x, y, c):
        return [(refs[k], refs[n + k], peer_fn(x, y, c), refs[n + k]) for k in range(n)]

    return list(gives) + list(lands), plan, n


def _join_parts(parts):
    offsets, total = [], 0
    for arrays, _, _ in parts:
        offsets.append(total)
        total += len(arrays)

    def plan(refs, x, y, c):
        copies = []
        for (arrays, part_plan, _), off in zip(parts, offsets):
            copies += part_plan(refs[off:off + len(arrays)], x, y, c)
        return copies

    return [a for arrays, _, _ in parts for a in arrays], plan, sum(m for _, _, m in parts)


def _reduce_begin(tag, grads, column_halves=False):
    n = len(grads)
    if column_halves:
        half = lambda ref, j, h: ref.at[j, :, pl.ds(pl.multiple_of(h * (ref.shape[2] // 2), LANES), ref.shape[2] // 2)]
        lands = [lax.empty((N_CHIPS, g.shape[1], g.shape[2] // 2), F32) for g in grads]
    else:
        half = lambda ref, j, h: ref.at[j, h]
        lands = [lax.empty((N_CHIPS,) + g.shape[2:], F32) for g in grads]

    def plan(refs, x, y, c):
        return [(half(refs[k], j, 1 - c), refs[n + k].at[j], (x, y, 1 - c), refs[n + k].at[j])
                for k in range(n) for j in range(N_CHIPS)]

    return dict(tag=tag, n=n, stage="c", grads=list(grads), column_halves=column_halves,
                part=(list(grads) + lands, plan, N_CHIPS * n))


def _reduce_next(st, thru):
    tag, n, stage = st["tag"], st["n"], st["stage"]
    first, recv = thru[:n], thru[n:]
    x, y, c = _position()
    if stage == "c":
        pos = jnp.stack([c, x]).astype(jnp.int32)
        if st["column_halves"]:
            mine = lambda chip: (lambda i, p: (chip(p) + i, 0, p[0]))
        else:
            mine = lambda chip: (lambda i, p: (chip(p) + i, p[0]))
        sums = [_add_keep_give(
            f"rs{tag}_c_add{k}", pos,
            first[k], mine(lambda p: 2 * p[1]), mine(lambda p: 2 * (1 - p[1])),
            recv[k], lambda i, p: (2 * p[1] + i,), lambda i, p: (2 * (1 - p[1]) + i,), 2) for k in range(n)]
        lands = [lax.empty(s[1].shape, BF16) for s in sums]
        return dict(tag=tag, n=n, stage="x", keep=[s[0] for s in sums],
                    part=_exchange_part([s[1] for s in sums], lands, lambda x, y, c: (1 - x, y, c)))
    if stage == "x":
        pos = jnp.stack([y]).astype(jnp.int32)
        sums = [_add_keep_give(
            f"rs{tag}_x_add{k}", pos,
            st["keep"][k], lambda i, p: (p[0],), lambda i, p: (1 - p[0],),
            recv[k], lambda i, p: (p[0],), lambda i, p: (1 - p[0],), 1) for k in range(n)]
        lands = [lax.empty(s[1].shape, BF16) for s in sums]
        return dict(tag=tag, n=n, stage="y", keep=[s[0] for s in sums],
                    part=_exchange_part([s[1] for s in sums], lands, lambda x, y, c: (x, 1 - y, c)))
    if stage == "y":
        mine = [_add_last(f"rs{tag}_y_add{k}", st["keep"][k], recv[k]) for k in range(n)]
        lands = [lax.empty(m.shape, F32) for m in mine]
        return dict(tag=tag, n=n, stage="swap", part=_exchange_part(mine, lands, lambda x, y, c: (x, y, 1 - c)))
    return dict(tag=tag, done=list(zip(first, recv)))


def _small_begin(tag, v):
    land = lax.empty((N_DEV,) + v.shape, F32)
    flips = [(fx, fy, fc) for fx in (0, 1) for fy in (0, 1) for fc in (0, 1)][1:]

    def plan(refs, x, y, c):
        copies = []
        for fx, fy, fc in flips:
            px, py, pc = (1 - x if fx else x), (1 - y if fy else y), (1 - c if fc else c)
            copies.append((refs[0], refs[1].at[4 * x + 2 * y + c], (px, py, pc), refs[1].at[4 * px + 2 * py + pc]))
        return copies

    return dict(tag=tag, n=1, stage="swap", grads=[v], part=([v, land], plan, len(flips)))


def _small_sum(name, own, land):
    x, y, c = _position()
    me = jnp.stack([4 * x + 2 * y + c]).astype(jnp.int32)

    def body(me_ref, own_ref, land_ref, out_ref):
        term = lambda dev: jnp.where(me_ref[0] == dev, own_ref[...], land_ref[dev])
        acc = term(0)
        for dev in range(1, N_DEV):
            acc = acc + term(dev)
        out_ref[...] = acc

    return pl.pallas_call(
        body, name=name, out_shape=jax.ShapeDtypeStruct(own.shape, F32),
        grid_spec=pltpu.PrefetchScalarGridSpec(
            num_scalar_prefetch=1, grid=(1,),
            in_specs=[pl.BlockSpec(own.shape, lambda i, m: (0, 0)), pl.BlockSpec(land.shape, lambda i, m: (0, 0, 0))],
            out_specs=pl.BlockSpec(own.shape, lambda i, m: (0, 0))),
        compiler_params=_cparams("arbitrary"),
    )(me, own, land)


def _adamw_update(w, gg, m, v):
    mn = ADAM_B1 * m + (1.0 - ADAM_B1) * gg
    vn = ADAM_B2 * v + (1.0 - ADAM_B2) * (gg * gg)
    m_hat = mn / (1.0 - ADAM_B1 ** ADAM_STEP)
    v_hat = vn / (1.0 - ADAM_B2 ** ADAM_STEP)
    return -ADAM_LR * (m_hat / (jnp.sqrt(v_hat) + ADAM_EPS) + ADAM_WD * w), mn, vn


def _adamw(name, w, g, m, v):
    def body(w_ref, g_ref, m_ref, v_ref, d_ref, mo_ref, vo_ref):
        d_ref[...], mo_ref[...], vo_ref[...] = _adamw_update(w_ref[...], g_ref[...], m_ref[...], v_ref[...])

    blk = pl.BlockSpec(w.shape, lambda i: (0, 0))
    return pl.pallas_call(
        body, name=name, out_shape=(jax.ShapeDtypeStruct(w.shape, F32),) * 3, grid=(1,),
        in_specs=[blk] * 4, out_specs=(blk,) * 3, compiler_params=_cparams("arbitrary"),
    )(w, g, m, v)


def _rows_to_bf16(name, w):
    r, _, c = w.shape

    def body(w_ref, o_ref):
        o_ref[...] = w_ref[:, 0, :].astype(BF16)

    return pl.pallas_call(
        body, name=name, out_shape=jax.ShapeDtypeStruct((r, c), BF16), grid=(1,),
        in_specs=[pl.BlockSpec((r, 1, c), lambda i: (0, 0, 0))], out_specs=pl.BlockSpec((r, c), lambda i: (0, 0)),
        compiler_params=_cparams("arbitrary"),
    )(w)


def _adamw_rows(name, pos_c, w, g_mine, g_other, m, v):
    r, _, c = w.shape
    per = 2
    ch = c // (2 * per)

    def body(p_ref, w_ref, gm_ref, go_ref, m_ref, v_ref, g_ref, d_ref, mo_ref, vo_ref):
        gg = jnp.where(pl.program_id(0) // per == p_ref[0], gm_ref[...], go_ref[...])
        dl, mn, vn = _adamw_update(w_ref[:, 0, :], gg, m_ref[:, 0, :], v_ref[:, 0, :])
        g_ref[:, 0, :] = gg
        d_ref[:, 0, :] = dl
        mo_ref[:, 0, :] = mn
        vo_ref[:, 0, :] = vn

    rows = pl.BlockSpec((r, 1, ch), lambda h, p: (0, 0, h))
    half = pl.BlockSpec((r, ch), lambda h, p: (0, h % per))
    return pl.pallas_call(
        body, name=name, out_shape=(jax.ShapeDtypeStruct(w.shape, F32),) * 4,
        grid_spec=pltpu.PrefetchScalarGridSpec(
            num_scalar_prefetch=1, grid=(2 * per,), in_specs=[rows, half, half, rows, rows], out_specs=(rows,) * 4),
        compiler_params=_cparams("parallel"),
    )(pos_c, w, g_mine, g_other, m, v)


def _adamw_halves(name, pos_c, w, g_mine, g_other, m, v, tr, dep=None):
    r, c = w.shape
    rh = r // 2
    tr = tr if rh % tr == 0 else rh
    nt = rh // tr

    def body(p_ref, w_ref, gm_ref, go_ref, m_ref, v_ref, g_ref, d_ref, mo_ref, vo_ref):
        gg = jnp.where(pl.program_id(0) == p_ref[0], gm_ref[...], go_ref[...])
        g_ref[...] = gg
        d_ref[...], mo_ref[...], vo_ref[...] = _adamw_update(w_ref[...], gg, m_ref[...], v_ref[...])

    full = pl.BlockSpec((tr, c), lambda h, i, p: (h * nt + i, 0))
    half = pl.BlockSpec((tr, c), lambda h, i, p: (i, 0))
    dep_specs, dep_ops = _dep_args(dep)
    return pl.pallas_call(
        _after(body, 6, dep), name=name, out_shape=(jax.ShapeDtypeStruct((r, c), F32),) * 4,
        grid_spec=pltpu.PrefetchScalarGridSpec(
            num_scalar_prefetch=1, grid=(2, nt),
            in_specs=[full, half, half, full, full] + dep_specs, out_specs=(full,) * 4),
        compiler_params=_cparams("parallel", "parallel"),
    )(pos_c, w, g_mine, g_other, m, v, *dep_ops)


def _col_sharded_to_comm(g):
    k, n = g.shape
    return g.reshape(2, k // 2, N_CHIPS, n // N_CHIPS).transpose(2, 0, 1, 3)


def _row_sharded_to_comm(g):
    r, c = g.shape
    return g.reshape(N_CHIPS, 2, r // (2 * N_CHIPS), c)


def _col_sharded_full(g):
    _, _, rh, c = g.shape
    return g.reshape(N_CHIPS, 2 * rh, c).transpose(1, 0, 2).reshape(2 * rh, N_CHIPS * c)


def _row_sharded_full(g):
    _, _, rh, c = g.shape
    return g.reshape(N_CHIPS * 2 * rh, c)


def _chip_rows(w3, start, stop, own=None, me=None):
    r = w3.shape[1]
    parts = []
    for chip in range(N_CHIPS):
        lo, hi = max(start - chip * r, 0), min(stop - chip * r, r)
        if lo < hi:
            part = w3[chip, lo:hi]
            parts.append(part if own is None else jnp.where(me == chip, own[lo:hi], part))
    return parts[0] if len(parts) == 1 else jnp.concatenate(parts, axis=0)


def _pack_small(g1, bfv, mix, scale, g2n, gf, extra=None):
    row8 = jnp.pad(bfv.reshape(1, N_HEADS), ((0, 0), (0, LANES - N_HEADS)))
    if extra is not None:
        row8 = row8 + jnp.pad(extra[:, :1], ((0, 0), (N_HEADS, LANES - N_HEADS - 1)))
    return jnp.concatenate([
        g1.reshape(8, LANES), jnp.pad(row8, ((0, 7), (0, 0))), mix.reshape(512, LANES),
        jnp.pad(scale.reshape(4, LANES), ((0, 4), (0, 0))), g2n.reshape(8, LANES), gf.reshape(8, LANES)], axis=0)


def _unpack_small(s, like):
    g1, bfv, mix, scale, g2n, gf = like
    return (s[0:8].reshape(g1.shape), s[8, :N_HEADS].reshape(bfv.shape), s[16:528].reshape(mix.shape),
            s[528:532].reshape(scale.shape), s[536:544].reshape(g2n.shape), s[544:552].reshape(gf.shape))


class _MeshLinks:
    def __init__(self, shards_in, shards_rest):
        self.gin = _gather_begin("in", shards_in, None, column_halves=True)
        self.grest = _gather_begin("rest", shards_rest, self.gin["token"])
        self.tokens = {"gather": self.grest["token"]}
        self.groups, self.flight, self.slot = {}, None, 0

    @property
    def token(self):
        return list(self.tokens.values())

    def tie(self, x):
        return _tie(x, self.token)

    def weights_in(self, after):
        st = _gather_forward(self.gin, after)
        (g,), (own,) = _gather_end(st, st["token"], merge=False)
        return g, own, 2 * lax.axis_index("x") + lax.axis_index("y")

    def rest_forward(self, after):
        self.grest = _gather_forward(self.grest, after)
        self.tokens["gather"] = self.grest["token"]

    def weights_rest(self, after):
        g = _gather_end(self.grest, after)
        del self.tokens["gather"]
        return [_col_sharded_full(g[0]), _col_sharded_full(g[1])] + [_row_sharded_full(a) for a in g[2:]]

    def advance(self, after, begin=()):
        slot = self.slot
        self.slot += 1
        if self.flight is not None:
            tags, sems, parts = self.flight
            arrays, plan, _ = _join_parts(parts)
            thru = _copies_wait(f"slot{slot}_wait", sems, arrays, plan, after)
            for tag, part in zip(tags, parts):
                self.groups[tag] = _reduce_next(self.groups[tag], thru[:len(part[0])])
                thru = thru[len(part[0]):]
        for st in begin:
            self.groups[st["tag"]] = st
        live = [(tag, st["part"]) for tag, st in self.groups.items() if "part" in st]
        self.flight = None
        self.tokens.pop("reduce", None)
        if live:
            arrays, plan, m = _join_parts([part for _, part in live])
            sems, thru, token = _copies_start(f"slot{slot}_start", arrays, plan, m)
            parts = []
            for _, (part_arrays, part_plan, part_m) in live:
                parts.append((thru[:len(part_arrays)], part_plan, part_m))
                thru = thru[len(part_arrays):]
            self.flight = ([tag for tag, _ in live], sems, parts)
            self.tokens["reduce"] = token

    def reduced(self, tag):
        return self.groups[tag]["done"]


class _NoLinks:
    token = None

    def __init__(self, w_in, rest):
        self.w_in, self.rest, self.grads = w_in, rest, {}

    def tie(self, x):
        return x

    def weights_in(self, after):
        return self.w_in, None, None

    def rest_forward(self, after):
        pass

    def weights_rest(self, after):
        return self.rest

    def advance(self, after, begin=()):
        for st in begin:
            self.grads[st["tag"]] = st["grads"]


def _local_step(links, x, target, seq, norm1_g, b_forget, pool_mix, pool_scale, norm2_g, norm_f_g, between=None):
    t, d = x.shape
    tq = min(256, seq)
    aw = ATTN_WIDTH
    o_q, o_f, o_g = POOL_WIDTH, POOL_WIDTH + 3 * aw, POOL_WIDTH + 3 * aw + N_HEADS
    bf = jnp.pad(b_forget, ((0, 0), (0, LANES - N_HEADS)))
    mixb = pool_mix.astype(BF16)

    h = _norm_fwd("norm1_fwd", x, links.tie(norm1_g), 512)
    w_in, own, me = links.weights_in(h)
    wu = _chip_rows(w_in, 0, o_q, own, me)
    wqkv = _chip_rows(w_in, o_q, o_f, own, me)
    wft = jnp.pad(_chip_rows(w_in, o_f, o_g, own, me), ((0, LANES - N_HEADS), (0, 0)))
    wg2 = _chip_rows(w_in, o_g, N_CHIPS * w_in.shape[1], own, me)
    wf = wft.T
    u = _matmul("mm_u", h, wu, "nt", F32, 1024, 512, d)
    g2 = _matmul("mm_gates", h, wg2, "nt", BF16, 1024, 1024, d)
    fl, fcum = _forget_fwd(h, wf, bf, seq)
    qa, ka, v = _attn_prep(h, _head_blocks(wqkv[:aw]), _head_blocks(wqkv[aw:2 * aw]), wqkv[2 * aw:], fcum, 1024)
    p, ps = _pool_fwd(u, mixb, pool_scale, seq)
    links.rest_forward([ps, qa, g2])
    o, lse = _attn_fwd(qa, ka, v, seq, tq, dep=links.token)
    w_pool_out, w_attn_out, w_out, w_ffn_gate, w_ffn_up, w_ffn_down = links.weights_rest(o)
    merged, x1 = _merge_fwd(x, ps, o, g2, w_pool_out, w_attn_out, w_out, 512)
    h2, gt, up, act, x2 = _ffn_fwd(x1, norm2_g, w_ffn_gate, w_ffn_up, w_ffn_down, 1024, 256)
    loss, dx2, d_gf = _final_fwd_bwd(x2, target, norm_f_g, 512)

    dgt, dup, dx1, d_g2n = _ffn_bwd(dx2, x1, norm2_g, gt, up, w_ffn_gate, w_ffn_up, w_ffn_down, 1024, 256)
    d_wd = _matmul("dw_down", act, dx2, "tn", F32, 1408, 1024, 1024)
    d_wg = _matmul("dw_gate", dgt, h2, "tn", F32, 1408, 1024, 1024)
    d_wu = _matmul("dw_up", dup, h2, "tn", F32, 1408, 1024, 1024)
    links.advance(None, begin=[_reduce_begin("a", [_row_sharded_to_comm(g) for g in (d_wg, d_wu, d_wd)])])
    dpy, day, dg2, dps, da = _merge_bwd(dx1, ps, o, g2, w_pool_out, w_attn_out, w_out, 512, dep=links.token)
    links.advance(dps)
    d_wout = _matmul("dw_out", merged, dx1, "tn", F32, 1024, 1024, 1024)
    d_wpo = _matmul("dw_pool_out", ps, dpy, "tn", F32, 512, 1024, 1024)
    d_wao = _matmul("dw_attn_out", o, day, "tn", F32, 512, 1024, 1024)
    dq, dk, dv, dfr = _attn_bwd(qa, ka, v, da, lse, seq, tq, dep=links.token)
    links.advance(dq, begin=[_reduce_begin(
        "m", [_col_sharded_to_comm(d_wpo), _col_sharded_to_comm(d_wao), _row_sharded_to_comm(d_wout)])])
    dfc = jnp.pad(dfr.reshape(N_HEADS, t).T, ((0, 0), (0, LANES - N_HEADS)))
    dfl, d_bf = _forget_bwd(dfc, fl, bf, seq)
    du, d_mix, d_scale = _pool_bwd(dps, p, mixb, links.tie(pool_scale), seq)
    d_wu_in = _matmul("dw_in_u", du, h, "tn", F32, 512, 1024, 1024, dep=links.token)
    small = (jnp.zeros_like(norm1_g), d_bf[:, :N_HEADS], d_mix, d_scale, d_g2n, d_gf)
    d_wq = _matmul("dw_in_q", dq, h, "tn", F32, 512, 1024, 1024, dep=links.token)
    d_wk = _matmul("dw_in_k", dk, h, "tn", F32, 512, 1024, 1024, dep=links.token)
    d_wv = _matmul("dw_in_v", dv, h, "tn", F32, 512, 1024, 1024, dep=links.token)
    links.advance([d_wu_in, d_wq, d_wk, d_wv], begin=[_small_begin("small", _pack_small(*small, extra=loss))])
    d_wf = _matmul("dw_in_f", dfl, h, "tn", F32, LANES, 1024, 512)
    d_wg2 = _matmul("dw_in_gates", dg2, h, "tn", F32, 1024, 1024, 1024, dep=links.token)
    d_win = jnp.concatenate([d_wu_in, d_wq, d_wk, d_wv, d_wf[:N_HEADS], d_wg2], axis=0)
    comm_b = [d_win.reshape(N_CHIPS, d_win.shape[0] // N_CHIPS, d)]
    links.advance(comm_b, begin=[_reduce_begin("b", comm_b, column_halves=True)])
    if between is not None:
        between()
    dx, d_g1 = _in_bwd(du, dq, dk, dv, dg2, dfl, dx1, x, links.tie(norm1_g), wu, wqkv, wg2, wft, 512)
    return loss, dx, (d_g1,) + small[1:]


def kernel(x, norm1_g, w_in, b_forget, pool_mix, pool_scale, w_pool_out, w_attn_out, w_out, norm2_g, w_ffn_gate, w_ffn_up, w_ffn_down, norm_f_g, loss_target, m_norm1_g, m_w_in, m_b_forget, m_pool_mix, m_pool_scale, m_w_pool_out, m_w_attn_out, m_w_out, m_norm2_g, m_w_ffn_gate, m_w_ffn_up, m_w_ffn_down, m_norm_f_g, v_norm1_g, v_w_in, v_b_forget, v_pool_mix, v_pool_scale, v_w_pool_out, v_w_attn_out, v_w_out, v_norm2_g, v_w_ffn_gate, v_w_ffn_up, v_w_ffn_down, v_norm_f_g):
    nb, seq, d = x.shape
    group_a = ((w_ffn_gate, m_w_ffn_gate, v_w_ffn_gate, True, 9), (w_ffn_up, m_w_ffn_up, v_w_ffn_up, True, 10),
               (w_ffn_down, m_w_ffn_down, v_w_ffn_down, False, 11))
    group_m = ((w_pool_out, m_w_pool_out, v_w_pool_out, False, 5), (w_attn_out, m_w_attn_out, v_w_attn_out, False, 6),
               (w_out, m_w_out, v_w_out, False, 7))
    group_b = ((w_in, m_w_in, v_w_in, False, 1),)
    small_w = (norm1_g, b_forget, pool_mix, pool_scale, norm2_g, norm_f_g)
    small_m = (m_norm1_g, m_b_forget, m_pool_mix, m_pool_scale, m_norm2_g, m_norm_f_g)
    small_v = (v_norm1_g, v_b_forget, v_pool_mix, v_pool_scale, v_norm2_g, v_norm_f_g)
    small_pos = (0, 2, 3, 4, 8, 12)
    view = lambda a, tr: a[0].T if tr else a[0]
    unview = lambda a, tr, like: (a.T if tr else a).reshape(like.shape)

    def shard(w, tr):
        lw = view(w, tr).astype(BF16)
        return lw.reshape(2, lw.shape[0] // 2, lw.shape[1])

    cm = lambda a: jnp.transpose(a, (2, 0, 1))
    shard_in = _rows_to_bf16("w_in_to_bf16", cm(w_in))
    links = _MeshLinks([shard_in],
                       [shard(w_pool_out, False), shard(w_attn_out, False), shard(w_out, False),
                        shard(w_ffn_gate, True), shard(w_ffn_up, True), shard(w_ffn_down, False)])
    grads, deltas, new_m, new_v = [None] * 13, [None] * 13, [None] * 13, [None] * 13
    pos_c = jnp.stack([lax.axis_index("c")]).astype(jnp.int32)

    def update(tag, group, dep, members=(0, 1, 2)):
        last = []
        reduced = links.reduced(tag)
        for k in members:
            (w, m, v, tr, pos), (mine, other) = group[k], reduced[k]
            outs = _adamw_halves(f"adamw_{tag}{k}", pos_c, view(w, tr), mine, other, view(m, tr), view(v, tr), 256,
                                 dep=dep)
            grads[pos], deltas[pos], new_m[pos], new_v[pos] = (unview(a, tr, w) for a in outs)
            last.append(outs[1])
        return last

    def update_a():
        links.advance(update("a", group_a, links.token, members=(0,)))

    loss, dx, small_g = _local_step(
        links, x.reshape(nb * seq, d), loss_target.reshape(nb * seq, d), seq,
        norm1_g, b_forget, pool_mix[0], pool_scale, norm2_g, norm_f_g.reshape(1, d), between=update_a)

    links.advance(dx, begin=[_small_begin("g1", small_g[0].reshape(8, LANES))])
    last = update("m", group_m, links.token) + update("a", group_a, links.token, members=(2,))
    small_rest = _small_sum("small_sum", *links.reduced("small")[0])
    links.advance(last + [small_rest])
    last = update("a", group_a, links.token, members=(1,))
    small_sum = jnp.concatenate([_small_sum("g1_sum", *links.reduced("g1")[0]), small_rest[8:]], axis=0)
    loss_out = small_sum[8, N_HEADS]
    dl, mn, vn = _adamw("adamw_small", _pack_small(*small_w), small_sum * _small_mask(), _pack_small(*small_m),
                        _pack_small(*small_v))
    for pos, g, a, b, e in zip(small_pos, _unpack_small(small_sum, small_w), _unpack_small(dl, small_w),
                               _unpack_small(mn, small_w), _unpack_small(vn, small_w)):
        grads[pos], deltas[pos], new_m[pos], new_v[pos] = g, a, b, e
    links.advance(last + [dl])
    (mine, other), = links.reduced("b")
    outs = _adamw_rows("adamw_b0", pos_c, cm(w_in), mine, other, cm(m_w_in), cm(v_w_in))
    grads[1], deltas[1], new_m[1], new_v[1] = (jnp.transpose(a, (1, 2, 0)) for a in outs)

    return (loss_out, dx.reshape(nb, seq, d), *grads, *deltas, *new_m, *new_v)


def _small_mask():
    rows = lax.broadcasted_iota(jnp.int32, (552, LANES), 0)
    lanes = lax.broadcasted_iota(jnp.int32, (552, LANES), 1)
    return jnp.where(jnp.logical_and(rows == 8, lanes == N_HEADS), 0.0, 1.0).astype(F32)
```

```python
import jax
import jax.numpy as jnp
from jax import lax
from jax.experimental import pallas as pl
from jax.experimental.pallas import tpu as pltpu

F32 = jnp.float32
BF16 = jnp.bfloat16

D_MODEL = 1024
POOL_WINDOWS = (2, 4, 8, 16)
POOL_GROUPS = 4
POOL_GROUP_DIM = 128
POOL_WIDTH = 512
HEAD_DIM = 64
N_HEADS = 8
ATTN_WIDTH = 512
D_FF = 2816
RMS_EPS = 1e-6
ATTN_SCALE = HEAD_DIM ** -0.5
NEG_BIG = -1e30

ADAM_LR = 0.001
ADAM_B1 = 0.9
ADAM_B2 = 0.999
ADAM_EPS = 1e-08
ADAM_WD = 0.01
ADAM_STEP = 10

LANES = 128
N_CHIPS = 4
N_DEV = 8
VMEM_LIMIT_V7X = 52 * 1024 * 1024
ROW_CHUNK = 256
MESH = pl.DeviceIdType.MESH
ANY = pl.BlockSpec(memory_space=pl.ANY)


def _cparams(*sem):
    return pltpu.CompilerParams(dimension_semantics=sem if sem else None, vmem_limit_bytes=VMEM_LIMIT_V7X)


def _dep_list(dep):
    return [] if dep is None else (list(dep) if isinstance(dep, (list, tuple)) else [dep])


def _after(body, n_in, dep):
    k = len(_dep_list(dep))
    if k == 0:
        return body

    def wrapped(*refs):
        body(*refs[:n_in], *refs[n_in + k:])

    return wrapped


def _dep_args(dep):
    deps = _dep_list(dep)
    return [ANY] * len(deps), deps


def _dot(a, b):
    return lax.dot_general(a, b, (((1,), (0,)), ((), ())), preferred_element_type=F32)


def _dot_nt(a, b):
    return lax.dot_general(a, b, (((1,), (1,)), ((), ())), preferred_element_type=F32)


def _dot_tn(a, b):
    return lax.dot_general(a, b, (((0,), (0,)), ((), ())), preferred_element_type=F32)


def _sigmoid(x):
    return jax.nn.sigmoid(x)


def _rms_fwd(x, g):
    r = lax.rsqrt(jnp.mean(x * x, axis=-1, keepdims=True) + RMS_EPS)
    return (x * r) * g


def _rms_bwd(x, g, dy):
    r = lax.rsqrt(jnp.mean(x * x, axis=-1, keepdims=True) + RMS_EPS)
    xh = x * r
    dg = jnp.sum(dy * xh, axis=0, keepdims=True)
    dxh = dy * g
    dx = r * (dxh - xh * jnp.mean(dxh * xh, axis=-1, keepdims=True))
    return dx, dg


def _matmul(name, a, b, mode, out_dtype, tm, tn, tk, dep=None):
    if mode == "nn":
        (m, k), (_, n) = a.shape, b.shape
    elif mode == "nt":
        (m, k), (n, _) = a.shape, b.shape
    else:
        (k, m), (_, n) = a.shape, b.shape
    tm, tn, tk = min(tm, m), min(tn, n), min(tk, k)
    assert m % tm == 0 and n % tn == 0 and k % tk == 0, (name, m, n, k, tm, tn, tk)
    nk = k // tk
    if mode == "tn":
        a_spec = pl.BlockSpec((tk, tm), lambda i, j, kk: (kk, i))
    else:
        a_spec = pl.BlockSpec((tm, tk), lambda i, j, kk: (i, kk))
    if mode == "nt":
        b_spec = pl.BlockSpec((tn, tk), lambda i, j, kk: (j, kk))
    else:
        b_spec = pl.BlockSpec((tk, tn), lambda i, j, kk: (kk, j))
    dot = {"nn": _dot, "nt": _dot_nt, "tn": _dot_tn}[mode]
    use_scratch = nk > 1 and out_dtype != F32

    def body(a_ref, b_ref, o_ref, *scratch):
        if nk == 1 and mode != "tn":
            rows = min(ROW_CHUNK, tm)
            bb = b_ref[...].astype(BF16)
            for r0 in range(0, tm, rows):
                o_ref[r0:r0 + rows, :] = dot(a_ref[r0:r0 + rows, :].astype(BF16), bb).astype(out_dtype)
            return
        prod = dot(a_ref[...].astype(BF16), b_ref[...].astype(BF16))
        if nk == 1:
            o_ref[...] = prod.astype(out_dtype)
            return
        acc = scratch[0] if use_scratch else o_ref
        kk = pl.program_id(2)

        @pl.when(kk == 0)
        def _():
            acc[...] = prod

        @pl.when(kk > 0)
        def _():
            acc[...] += prod

        if use_scratch:
            @pl.when(kk == nk - 1)
            def _():
                o_ref[...] = acc[...].astype(out_dtype)

    dep_specs, dep_ops = _dep_args(dep)
    return pl.pallas_call(
        _after(body, 2, dep),
        name=name,
        out_shape=jax.ShapeDtypeStruct((m, n), out_dtype),
        grid=(m // tm, n // tn, nk),
        in_specs=[a_spec, b_spec] + dep_specs,
        out_specs=pl.BlockSpec((tm, tn), lambda i, j, kk: (i, j)),
        scratch_shapes=[pltpu.VMEM((tm, tn), F32)] if use_scratch else [],
        compiler_params=_cparams("parallel", "parallel", "arbitrary"),
    )(a, b, *dep_ops)


def _norm_fwd(name, x, g, tm):
    t, d = x.shape
    tm = min(tm, t)

    def body(x_ref, g_ref, h_ref):
        h_ref[...] = _rms_fwd(x_ref[...], g_ref[...]).astype(BF16)

    return pl.pallas_call(
        body, name=name, out_shape=jax.ShapeDtypeStruct((t, d), BF16), grid=(t // tm,),
        in_specs=[pl.BlockSpec((tm, d), lambda i: (i, 0)), pl.BlockSpec((1, d), lambda i: (0, 0))],
        out_specs=pl.BlockSpec((tm, d), lambda i: (i, 0)),
        compiler_params=_cparams("parallel"),
    )(x, g)


def _split3(x):
    hi = x.astype(BF16)
    r1 = x - hi.astype(F32)
    mid = r1.astype(BF16)
    lo = (r1 - mid.astype(F32)).astype(BF16)
    return hi, mid, lo


def _tri_dot(tri, x):
    hi, mid, lo = _split3(x)
    return _dot(tri, hi) + _dot(tri, mid) + _dot(tri, lo)


def _forget_fwd(h, wf, bf, seq):
    t, d = h.shape
    cb = min(256, seq)

    def body(h_ref, wf_ref, bf_ref, fl_ref, fc_ref):
        fl = _dot(h_ref[...], wf_ref[...])
        fl_ref[...] = fl
        xx = fl + bf_ref[...]
        lf = jnp.minimum(xx, 0.0) - jnp.log(1.0 + jnp.exp(-jnp.abs(xx)))
        ri = lax.broadcasted_iota(jnp.int32, (cb, cb), 0)
        ci = lax.broadcasted_iota(jnp.int32, (cb, cb), 1)
        tri = (ri >= ci).astype(BF16)
        carry = jnp.zeros((1, LANES), F32)
        for blk in range(seq // cb):
            cs = _tri_dot(tri, lf[blk * cb:(blk + 1) * cb]) + carry
            fc_ref[blk * cb:(blk + 1) * cb, :] = cs
            carry = cs[cb - 1:cb, :]

    return pl.pallas_call(
        body, name="forget_fwd",
        out_shape=(jax.ShapeDtypeStruct((t, LANES), F32), jax.ShapeDtypeStruct((t, LANES), F32)),
        grid=(t // seq,),
        in_specs=[pl.BlockSpec((seq, d), lambda b: (b, 0)), pl.BlockSpec((d, LANES), lambda b: (0, 0)),
                  pl.BlockSpec((1, LANES), lambda b: (0, 0))],
        out_specs=(pl.BlockSpec((seq, LANES), lambda b: (b, 0)), pl.BlockSpec((seq, LANES), lambda b: (b, 0))),
        compiler_params=_cparams("parallel"),
    )(h, wf, bf)


def _pool_fwd(u, mix, scale, seq):
    t = u.shape[0]

    def body(u_ref, mix_ref, sc_ref, p_ref, ps_ref):
        tpos = lax.broadcasted_iota(jnp.int32, (seq, POOL_GROUP_DIM), 0)
        for g in range(POOL_GROUPS):
            sl = slice(g * POOL_GROUP_DIM, (g + 1) * POOL_GROUP_DIM)
            ug = u_ref[:, sl]
            s = ug
            for lvl in range(g + 1):
                d = 2 ** lvl
                s = s + jnp.where(tpos >= d, pltpu.roll(s, d, 0), 0.0)
            cnt = jnp.minimum(tpos + 1, POOL_WINDOWS[g]).astype(F32)
            pb = (s / cnt - ug).astype(BF16)
            p_ref[:, sl] = pb
            ps_ref[:, sl] = (_dot(pb, mix_ref[g]) * sc_ref[:, sl]).astype(BF16)

    return pl.pallas_call(
        body, name="pool_fwd",
        out_shape=(jax.ShapeDtypeStruct((t, POOL_WIDTH), BF16), jax.ShapeDtypeStruct((t, POOL_WIDTH), BF16)),
        grid=(t // seq,),
        in_specs=[pl.BlockSpec((seq, POOL_WIDTH), lambda b: (b, 0)),
                  pl.BlockSpec((POOL_GROUPS, POOL_GROUP_DIM, POOL_GROUP_DIM), lambda b: (0, 0, 0)),
                  pl.BlockSpec((1, POOL_WIDTH), lambda b: (0, 0))],
        out_specs=(pl.BlockSpec((seq, POOL_WIDTH), lambda b: (b, 0)), pl.BlockSpec((seq, POOL_WIDTH), lambda b: (b, 0))),
        compiler_params=_cparams("parallel"),
    )(u, mix, scale)


def _aug_constants():
    w = N_HEADS * LANES
    rows = jnp.arange(3 * LANES)
    piece, head = rows // LANES, rows % LANES
    cols = jnp.arange(w)
    live = (head < N_HEADS)[:, None]
    pq = (live & (cols[None, :] == (head * LANES + HEAD_DIM + piece)[:, None])).astype(BF16)
    pk = -(live & (cols[None, :] == (head * LANES + HEAD_DIM + 3 + piece)[:, None])).astype(BF16)
    lane = cols % LANES
    oq = ((lane >= HEAD_DIM + 3) & (lane < HEAD_DIM + 6)).astype(F32)[None, :]
    ok = ((lane >= HEAD_DIM) & (lane < HEAD_DIM + 3)).astype(F32)[None, :]
    return pq, pk, oq, ok


def _head_blocks(wt):
    d = wt.shape[1]
    return jnp.pad(wt.reshape(N_HEADS, HEAD_DIM, d), ((0, 0), (0, LANES - HEAD_DIM), (0, 0))).reshape(N_HEADS * LANES, d)


def _attn_prep(h, wq, wk, wv, fcum, tm):
    t, d = h.shape
    tm = min(tm, t)
    rows = min(ROW_CHUNK, tm)
    w = N_HEADS * LANES
    pq, pk, oq, ok = _aug_constants()

    def body(h_ref, wq_ref, wk_ref, wv_ref, f_ref, pq_ref, pk_ref, oq_ref, ok_ref, qa_ref, ka_ref, v_ref):
        for r0 in range(0, tm, rows):
            rs = slice(r0, r0 + rows)
            hh = h_ref[rs, :]
            fs = jnp.concatenate(_split3(f_ref[rs, :]), axis=1)
            q = _dot_nt(hh, wq_ref[...]).astype(BF16).astype(F32) * ATTN_SCALE
            qa_ref[rs, :] = (q + _dot(fs, pq_ref[...]) + oq_ref[...]).astype(BF16)
            k = _dot_nt(hh, wk_ref[...]).astype(BF16).astype(F32)
            ka_ref[rs, :] = (k + _dot(fs, pk_ref[...]) + ok_ref[...]).astype(BF16)
            v_ref[rs, :] = _dot_nt(hh, wv_ref[...]).astype(BF16)

    row = lambda n: pl.BlockSpec((tm, n), lambda i: (i, 0))
    full = lambda a: pl.BlockSpec(a.shape, lambda i: (0, 0))
    return pl.pallas_call(
        body, name="attn_prep",
        out_shape=(jax.ShapeDtypeStruct((t, w), BF16), jax.ShapeDtypeStruct((t, w), BF16),
                   jax.ShapeDtypeStruct((t, ATTN_WIDTH), BF16)),
        grid=(t // tm,),
        in_specs=[row(d), full(wq), full(wk), full(wv), row(LANES), full(pq), full(pk), full(oq), full(ok)],
        out_specs=(row(w), row(w), row(ATTN_WIDTH)),
        compiler_params=_cparams("parallel"),
    )(h, wq, wk, wv, fcum, pq, pk, oq, ok)


def _fold_lanes(x, op):
    out = x[:, :LANES]
    for g in range(1, x.shape[1] // LANES):
        out = op(out, x[:, g * LANES:(g + 1) * LANES])
    return out


def _causal_sweep(i, tile, carry):
    def quad(jj, c):
        for u in range(4):
            c = tile(4 * jj + u, c, False)
        return c

    carry = lax.fori_loop(0, i // 4, quad, carry)
    base = 4 * (i // 4)
    carry = lax.cond(i % 4 >= 2, lambda c: tile(base + 1, tile(base, c, False), False), lambda c: c, carry)
    return lax.cond(i % 2 == 1, lambda c: tile(i, tile(i - 1, c, False), True), lambda c: tile(i, c, True), carry)


def _attn_fwd(qa, ka, v, seq, tq, dep=None):
    t = qa.shape[0]
    nq = seq // tq
    hp_n = N_HEADS // 2
    heads = [slice(e * LANES, (e + 1) * LANES) for e in range(2)]

    def body(q_ref, k_ref, v_ref, o_ref, lse_ref, s_buf):
        i = pl.program_id(2)
        diag_ok = lax.broadcasted_iota(jnp.int32, (tq, tq), 0) >= lax.broadcasted_iota(jnp.int32, (tq, tq), 1)
        qs = [q_ref[:, hl] for hl in heads]

        def sweep1(j, mxs, diagonal):
            r0 = pl.multiple_of(j * tq, tq)
            out = []
            for e, hl in enumerate(heads):
                s = _dot_nt(qs[e], k_ref[pl.ds(r0, tq), hl])
                if diagonal:
                    s = jnp.where(diag_ok, s, NEG_BIG)
                s_buf[e, j] = s
                out.append(jnp.maximum(mxs[e], _fold_lanes(s, jnp.maximum)))
            return tuple(out)

        mxs = _causal_sweep(i, sweep1, (jnp.full((tq, LANES), NEG_BIG, F32),) * 2)
        ms = [jnp.max(mx, axis=1, keepdims=True) for mx in mxs]

        def sweep2(j, carry, diagonal):
            r0 = pl.multiple_of(j * tq, tq)
            vv = v_ref[pl.ds(r0, tq), :]
            out = []
            for e in range(2):
                p = jnp.exp(s_buf[e, j] - ms[e])
                out += [carry[2 * e] + _fold_lanes(p, jnp.add), carry[2 * e + 1] + _dot(p.astype(BF16), vv)]
            return tuple(out)

        res = _causal_sweep(i, sweep2, (jnp.zeros((tq, LANES), F32),) * 4)
        outs = []
        for e in range(2):
            l = jnp.sum(res[2 * e], axis=1, keepdims=True)
            outs.append(res[2 * e + 1] / l)
            lse_ref[:, e:e + 1] = ms[e] + jnp.log(l)
        lane = lax.broadcasted_iota(jnp.int32, (tq, LANES), 1)
        o_ref[...] = jnp.where(lane < HEAD_DIM, outs[0], outs[1])

    dep_specs, dep_ops = _dep_args(dep)
    return pl.pallas_call(
        _after(body, 3, dep), name="attn_fwd",
        out_shape=(jax.ShapeDtypeStruct((t, ATTN_WIDTH), F32), jax.ShapeDtypeStruct((hp_n, t, 2), F32)),
        grid=(t // seq, hp_n, nq),
        in_specs=[pl.BlockSpec((tq, 2 * LANES), lambda b, hp, i: (b * nq + i, hp)),
                  pl.BlockSpec((seq, 2 * LANES), lambda b, hp, i: (b, hp)),
                  pl.BlockSpec((seq, LANES), lambda b, hp, i: (b, hp))] + dep_specs,
        out_specs=(pl.BlockSpec((tq, LANES), lambda b, hp, i: (b * nq + i, hp)),
                   pl.BlockSpec((None, tq, 2), lambda b, hp, i: (hp, b * nq + i, 0))),
        scratch_shapes=[pltpu.VMEM((2, nq, tq, tq), F32)],
        compiler_params=_cparams("parallel", "parallel", "arbitrary"),
    )(qa, ka, v, *dep_ops)


def _merge_fwd(x, ps, o, g2, wpo, wao, wout, tm):
    t, d = x.shape
    tm = min(tm, t)
    rows = min(ROW_CHUNK, tm)

    def body(x_ref, ps_ref, o_ref, gp_ref, ga_ref, wpo_ref, wao_ref, wout_ref, mg_ref, x1_ref):
        for r0 in range(0, tm, rows):
            rs = slice(r0, r0 + rows)
            py = _dot(ps_ref[rs, :], wpo_ref[...])
            ay = _dot(o_ref[rs, :].astype(BF16), wao_ref[...])
            mb = (_sigmoid(gp_ref[rs, :].astype(F32)) * py + _sigmoid(ga_ref[rs, :].astype(F32)) * ay).astype(BF16)
            mg_ref[rs, :] = mb
            x1_ref[rs, :] = x_ref[rs, :] + _dot(mb, wout_ref[...])

    row = lambda w: pl.BlockSpec((tm, w), lambda i: (i, 0))
    full = lambda a: pl.BlockSpec(a.shape, lambda i: (0, 0))
    return pl.pallas_call(
        body, name="merge_fwd",
        out_shape=(jax.ShapeDtypeStruct((t, d), BF16), jax.ShapeDtypeStruct((t, d), F32)),
        grid=(t // tm,),
        in_specs=[row(d), row(POOL_WIDTH), row(ATTN_WIDTH), pl.BlockSpec((tm, d), lambda i: (i, 0)),
                  pl.BlockSpec((tm, d), lambda i: (i, 1)), full(wpo), full(wao), full(wout)],
        out_specs=(row(d), row(d)),
        compiler_params=_cparams("parallel"),
    )(x, ps, o, g2, g2, wpo, wao, wout)


def _ffn_fwd(x1, g, wg, wu, wd, tm, tf):
    t, d = x1.shape
    f = wg.shape[0]
    tm = min(tm, t)
    nf = f // tf
    rows = min(512, tm)

    def body(x1_ref, g_ref, wg_ref, wu_ref, wd_ref, h2_ref, gt_ref, up_ref, act_ref, x2_ref):
        j = pl.program_id(1)

        @pl.when(j == 0)
        def _():
            h2_ref[...] = _rms_fwd(x1_ref[...], g_ref[...]).astype(BF16)

            x2_ref[...] = x1_ref[...]

        for r0 in range(0, tm, rows):
            rs = slice(r0, r0 + rows)
            h2 = h2_ref[rs, :]
            gt = _dot_nt(h2, wg_ref[...])
            up = _dot_nt(h2, wu_ref[...])
            sg = _sigmoid(gt)
            silu = gt * sg
            act = (silu * up).astype(BF16)
            gt_ref[rs, :] = (up * (sg * (1.0 + gt * (1.0 - sg)))).astype(BF16)
            up_ref[rs, :] = silu.astype(BF16)
            act_ref[rs, :] = act
            x2_ref[rs, :] += _dot(act, wd_ref[...])

    return pl.pallas_call(
        body, name="ffn_fwd",
        out_shape=(jax.ShapeDtypeStruct((t, d), BF16), jax.ShapeDtypeStruct((t, f), BF16),
                   jax.ShapeDtypeStruct((t, f), BF16), jax.ShapeDtypeStruct((t, f), BF16),
                   jax.ShapeDtypeStruct((t, d), F32)),
        grid=(t // tm, nf),
        in_specs=[pl.BlockSpec((tm, d), lambda i, j: (i, 0)), pl.BlockSpec((1, d), lambda i, j: (0, 0)),
                  pl.BlockSpec((tf, d), lambda i, j: (j, 0)), pl.BlockSpec((tf, d), lambda i, j: (j, 0)),
                  pl.BlockSpec((tf, d), lambda i, j: (j, 0))],
        out_specs=(pl.BlockSpec((tm, d), lambda i, j: (i, 0)), pl.BlockSpec((tm, tf), lambda i, j: (i, j)),
                   pl.BlockSpec((tm, tf), lambda i, j: (i, j)), pl.BlockSpec((tm, tf), lambda i, j: (i, j)),
                   pl.BlockSpec((tm, d), lambda i, j: (i, 0))),
        compiler_params=_cparams("parallel", "arbitrary"),
    )(x1, g, wg, wu, wd)


def _final_fwd_bwd(x2, target, g, tm):
    t, d = x2.shape
    tm = min(tm, t)

    def body(x_ref, t_ref, g_ref, loss_ref, dx_ref, dg_ref):
        i = pl.program_id(0)
        x = x_ref[...]
        gg = g_ref[...]
        err = _rms_fwd(x, gg) - t_ref[...]
        part = 0.5 * jnp.sum(jnp.mean(err * err, axis=-1, keepdims=True), axis=0, keepdims=True)
        dx, dg = _rms_bwd(x, gg, err * (1.0 / d))
        dx_ref[...] = dx

        @pl.when(i == 0)
        def _():
            loss_ref[...] = jnp.zeros_like(loss_ref)
            dg_ref[...] = jnp.zeros_like(dg_ref)

        loss_ref[...] += jnp.broadcast_to(part, loss_ref.shape)
        dg_ref[...] += dg

    return pl.pallas_call(
        body, name="final_fwd_bwd",
        out_shape=(jax.ShapeDtypeStruct((1, LANES), F32), jax.ShapeDtypeStruct((t, d), F32),
                   jax.ShapeDtypeStruct((1, d), F32)),
        grid=(t // tm,),
        in_specs=[pl.BlockSpec((tm, d), lambda i: (i, 0)), pl.BlockSpec((tm, d), lambda i: (i, 0)),
                  pl.BlockSpec((1, d), lambda i: (0, 0))],
        out_specs=(pl.BlockSpec((1, LANES), lambda i: (0, 0)), pl.BlockSpec((tm, d), lambda i: (i, 0)),
                   pl.BlockSpec((1, d), lambda i: (0, 0))),
        compiler_params=_cparams("arbitrary"),
    )(x2, target, g)


def _ffn_bwd(dx2, x1, g, gt, up, wg, wu, wd, tm, tf):
    t, d = dx2.shape
    f = gt.shape[1]
    tm = min(tm, t)
    nf = f // tf
    wgu = jnp.concatenate([wg.reshape(nf, tf, d), wu.reshape(nf, tf, d)], axis=1).reshape(2 * f, d)
    rows = min(256, tm)

    def body(dx2_ref, x1_ref, g_ref, gt_ref, up_ref, wgu_ref, wd_ref, dgt_ref, dup_ref, dx1_ref, dg_ref, acc_ref,
             dxb_ref):
        i, j = pl.program_id(0), pl.program_id(1)

        @pl.when(j == 0)
        def _():
            dxb_ref[...] = dx2_ref[...].astype(BF16)
            acc_ref[...] = jnp.zeros_like(acc_ref)

        for r0 in range(0, tm, rows):
            rs = slice(r0, r0 + rows)
            dact = _dot_nt(dxb_ref[rs, :], wd_ref[...])
            dgt = (dact * gt_ref[rs, :].astype(F32)).astype(BF16)
            dup = (dact * up_ref[rs, :].astype(F32)).astype(BF16)
            dgt_ref[rs, :] = dgt
            dup_ref[rs, :] = dup
            acc_ref[rs, :] += _dot(jnp.concatenate([dgt, dup], axis=1), wgu_ref[...])

        @pl.when(jnp.logical_and(i == 0, j == 0))
        def _():
            dg_ref[...] = jnp.zeros_like(dg_ref)

        @pl.when(j == nf - 1)
        def _():
            dxn, dg = _rms_bwd(x1_ref[...], g_ref[...], acc_ref[...])
            dx1_ref[...] = dx2_ref[...] + dxn
            dg_ref[...] += dg

    return pl.pallas_call(
        body, name="ffn_bwd",
        out_shape=(jax.ShapeDtypeStruct((t, f), BF16), jax.ShapeDtypeStruct((t, f), BF16),
                   jax.ShapeDtypeStruct((t, d), F32), jax.ShapeDtypeStruct((1, d), F32)),
        grid=(t // tm, nf),
        in_specs=[pl.BlockSpec((tm, d), lambda i, j: (i, 0)), pl.BlockSpec((tm, d), lambda i, j: (i, 0)),
                  pl.BlockSpec((1, d), lambda i, j: (0, 0)),
                  pl.BlockSpec((tm, tf), lambda i, j: (i, j)), pl.BlockSpec((tm, tf), lambda i, j: (i, j)),
                  pl.BlockSpec((2 * tf, d), lambda i, j: (j, 0)), pl.BlockSpec((tf, d), lambda i, j: (j, 0))],
        out_specs=(pl.BlockSpec((tm, tf), lambda i, j: (i, j)), pl.BlockSpec((tm, tf), lambda i, j: (i, j)),
                   pl.BlockSpec((tm, d), lambda i, j: (i, 0)), pl.BlockSpec((1, d), lambda i, j: (0, 0))),
        scratch_shapes=[pltpu.VMEM((tm, d), F32), pltpu.VMEM((tm, d), BF16)],
        compiler_params=_cparams("arbitrary", "arbitrary"),
    )(dx2, x1, g, gt, up, wgu, wd)


def _merge_bwd(dx1, ps, o, g2, wpo, wao, wout, tm, dep=None):
    t, d = dx1.shape
    tm = min(tm, t)
    rows = min(ROW_CHUNK, tm)

    def body(dx1_ref, ps_ref, o_ref, gp_ref, ga_ref, wpo_ref, wao_ref, wout_ref, dpy_ref, day_ref, dg2_ref, dps_ref, da_ref):
        for r0 in range(0, tm, rows):
            rs = slice(r0, r0 + rows)
            dm = _dot_nt(dx1_ref[rs, :].astype(BF16), wout_ref[...])
            py = _dot(ps_ref[rs, :], wpo_ref[...])
            ay = _dot(o_ref[rs, :].astype(BF16), wao_ref[...])
            sp = _sigmoid(gp_ref[rs, :].astype(F32))
            sa = _sigmoid(ga_ref[rs, :].astype(F32))
            dpy = (dm * sp).astype(BF16)
            day = (dm * sa).astype(BF16)
            dpy_ref[rs, :] = dpy
            day_ref[rs, :] = day
            dg2_ref[rs, :d] = (dm * py * (sp * (1.0 - sp))).astype(BF16)
            dg2_ref[rs, d:] = (dm * ay * (sa * (1.0 - sa))).astype(BF16)
            dps_ref[rs, :] = _dot_nt(dpy, wpo_ref[...])
            da_ref[rs, :] = _dot_nt(day, wao_ref[...]).astype(BF16)

    row = lambda w: pl.BlockSpec((tm, w), lambda i: (i, 0))
    full = lambda a: pl.BlockSpec(a.shape, lambda i: (0, 0))
    dep_specs, dep_ops = _dep_args(dep)
    return pl.pallas_call(
        _after(body, 8, dep), name="merge_bwd",
        out_shape=(jax.ShapeDtypeStruct((t, d), BF16), jax.ShapeDtypeStruct((t, d), BF16),
                   jax.ShapeDtypeStruct((t, 2 * d), BF16), jax.ShapeDtypeStruct((t, POOL_WIDTH), F32),
                   jax.ShapeDtypeStruct((t, ATTN_WIDTH), BF16)),
        grid=(t // tm,),
        in_specs=[row(d), row(POOL_WIDTH), row(ATTN_WIDTH), pl.BlockSpec((tm, d), lambda i: (i, 0)),
                  pl.BlockSpec((tm, d), lambda i: (i, 1)), full(wpo), full(wao), full(wout)] + dep_specs,
        out_specs=(row(d), row(d), row(2 * d), row(POOL_WIDTH), row(ATTN_WIDTH)),
        compiler_params=_cparams("parallel"),
    )(dx1, ps, o, g2, g2, wpo, wao, wout, *dep_ops)


def _attn_bwd(qa, ka, v, do, lse4, seq, tq, dep=None):
    t = qa.shape[0]
    nq = seq // tq
    hp_n = N_HEADS // 2
    heads = [slice(e * LANES, (e + 1) * LANES) for e in range(2)]

    def body(q_ref, k_ref, v_ref, do_ref, lse_ref, dq_ref, dk_ref, dv_ref, dfr_ref, dk_acc, dv_acc, p_buf, dp_buf):
        diag_ok = lax.broadcasted_iota(jnp.int32, (tq, tq), 0) >= lax.broadcasted_iota(jnp.int32, (tq, tq), 1)
        lane_q = lax.broadcasted_iota(jnp.int32, (tq, LANES), 1)
        mine_q = [lane_q < HEAD_DIM, lane_q >= HEAD_DIM]
        dv_acc[...] = jnp.zeros_like(dv_acc)
        dk_acc[...] = jnp.zeros_like(dk_acc)
        dfr_ref[...] = jnp.zeros_like(dfr_ref)
        transposed = lambda a: a.astype(F32).T.astype(BF16)

        def q_step(i, _):
            q0 = pl.multiple_of(i * tq, tq)
            qs = [q_ref[pl.ds(q0, tq), hl] for hl in heads]
            dov = do_ref[pl.ds(q0, tq), :]
            dos = [jnp.where(mq, dov, jnp.zeros((), BF16)) for mq in mine_q]
            qts = [transposed(q) for q in qs]
            dots = [transposed(a) for a in dos]
            lss = [lse_ref[pl.ds(q0, tq), e:e + 1] for e in range(2)]

            def sweep1(j, dls, diagonal):
                r0 = pl.multiple_of(j * tq, tq)
                vv = v_ref[pl.ds(r0, tq), :]
                out = []
                for e, hl in enumerate(heads):
                    s = _dot_nt(qs[e], k_ref[pl.ds(r0, tq), hl])
                    if diagonal:
                        s = jnp.where(diag_ok, s, NEG_BIG)
                    p = jnp.exp(s - lss[e])
                    dp = _dot_nt(dos[e], vv)
                    p_buf[e, j] = p
                    dp_buf[e, j] = dp
                    dv_acc[j] += _dot(dots[e], p.astype(BF16))
                    out.append(dls[e] + _fold_lanes(p * dp, jnp.add))
                return tuple(out)

            dls = _causal_sweep(i, sweep1, (jnp.zeros((tq, LANES), F32),) * 2)
            dls = [jnp.sum(d, axis=1, keepdims=True) for d in dls]

            def sweep2(j, dqs, diagonal):
                r0 = pl.multiple_of(j * tq, tq)
                out = []
                for e, hl in enumerate(heads):
                    ds = p_buf[e, j] * (dp_buf[e, j] - dls[e])
                    dfr_ref[e, pl.ds(j, 1), :] += jnp.sum(ds, axis=0, keepdims=True)
                    dsb = ds.astype(BF16)
                    dk_acc[e, j] += _dot(qts[e], dsb)
                    out.append(dqs[e] + _dot(dsb, k_ref[pl.ds(r0, tq), hl]))
                return tuple(out)

            dqs = _causal_sweep(i, sweep2, (jnp.zeros((tq, LANES), F32),) * 2)
            dq = jnp.where(mine_q[0], dqs[0], pltpu.roll(dqs[1], HEAD_DIM, 1)) * ATTN_SCALE
            dq_ref[pl.ds(q0, tq), :] = dq.astype(BF16)
            return 0

        lax.fori_loop(0, nq, q_step, 0)
        for j in range(nq):
            rs = slice(j * tq, (j + 1) * tq)
            dk = jnp.where(mine_q[0], dk_acc[0, j].T, pltpu.roll(dk_acc[1, j].T, HEAD_DIM, 1))
            dk_ref[rs, :] = dk.astype(BF16)
            dv_ref[rs, :] = dv_acc[j].T.astype(BF16)

    wide = pl.BlockSpec((seq, 2 * LANES), lambda b, hp: (b, hp))
    col = pl.BlockSpec((seq, LANES), lambda b, hp: (b, hp))
    pair = pl.BlockSpec((None, seq, 2), lambda b, hp: (hp, b, 0))
    dep_specs, dep_ops = _dep_args(dep)
    return pl.pallas_call(
        _after(body, 5, dep), name="attn_bwd",
        out_shape=(jax.ShapeDtypeStruct((t, ATTN_WIDTH), BF16),) * 3 + (jax.ShapeDtypeStruct((N_HEADS, t // tq, tq), F32),),
        grid=(t // seq, hp_n),
        in_specs=[wide, wide, col, col, pair] + dep_specs,
        out_specs=(col, col, col, pl.BlockSpec((2, nq, tq), lambda b, hp: (hp, b, 0))),
        scratch_shapes=[pltpu.VMEM((2, nq, LANES, tq), F32), pltpu.VMEM((nq, LANES, tq), F32),
                        pltpu.VMEM((2, nq, tq, tq), F32), pltpu.VMEM((2, nq, tq, tq), F32)],
        compiler_params=_cparams("parallel", "arbitrary"),
    )(qa, ka, v, do, lse4, *dep_ops)


def _forget_bwd(dfc, fl, bf, seq):
    t = fl.shape[0]
    cb = min(256, seq)
    nb = seq // cb

    def body(dfc_ref, fl_ref, bf_ref, dfl_ref, db_ref):
        b = pl.program_id(0)
        ri = lax.broadcasted_iota(jnp.int32, (cb, cb), 0)
        ci = lax.broadcasted_iota(jnp.int32, (cb, cb), 1)
        tri = (ci >= ri).astype(BF16)
        carry = jnp.zeros((1, LANES), F32)
        dbs = jnp.zeros((1, LANES), F32)
        for blk in reversed(range(nb)):
            rs = slice(blk * cb, (blk + 1) * cb)
            dlf = _tri_dot(tri, -dfc_ref[rs, :]) + carry
            carry = dlf[0:1, :]
            dfl = dlf * _sigmoid(-(fl_ref[rs, :] + bf_ref[...]))
            dfl_ref[rs, :] = dfl.astype(BF16)
            dbs = dbs + jnp.sum(dfl, axis=0, keepdims=True)

        @pl.when(b == 0)
        def _():
            db_ref[...] = jnp.zeros_like(db_ref)

        db_ref[...] += dbs

    return pl.pallas_call(
        body, name="forget_bwd",
        out_shape=(jax.ShapeDtypeStruct((t, LANES), BF16), jax.ShapeDtypeStruct((1, LANES), F32)),
        grid=(t // seq,),
        in_specs=[pl.BlockSpec((seq, LANES), lambda b: (b, 0)), pl.BlockSpec((seq, LANES), lambda b: (b, 0)),
                  pl.BlockSpec((1, LANES), lambda b: (0, 0))],
        out_specs=(pl.BlockSpec((seq, LANES), lambda b: (b, 0)), pl.BlockSpec((1, LANES), lambda b: (0, 0))),
        compiler_params=_cparams("arbitrary"),
    )(dfc, fl, bf)


def _pool_bwd(dps, p, mix, scale, seq):
    t = dps.shape[0]

    def body(dps_ref, p_ref, mix_ref, sc_ref, du_ref, dmix_ref, dsc_ref):
        b = pl.program_id(0)

        @pl.when(b == 0)
        def _():
            dmix_ref[...] = jnp.zeros_like(dmix_ref)
            dsc_ref[...] = jnp.zeros_like(dsc_ref)

        tpos = lax.broadcasted_iota(jnp.int32, (seq, POOL_GROUP_DIM), 0)
        for g in range(POOL_GROUPS):
            sl = slice(g * POOL_GROUP_DIM, (g + 1) * POOL_GROUP_DIM)
            pb = p_ref[:, sl]
            dpsg = dps_ref[:, sl]
            pm = _dot(pb, mix_ref[g])
            dsc_ref[:, sl] += jnp.sum(dpsg * pm, axis=0, keepdims=True)
            dpm = (dpsg * sc_ref[:, sl]).astype(BF16)
            dmix_ref[g] += _dot_tn(pb, dpm)
            dp = _dot_nt(dpm, mix_ref[g])
            cnt = jnp.minimum(tpos + 1, POOL_WINDOWS[g]).astype(F32)
            s = dp / cnt
            for lvl in range(g + 1):
                d = 2 ** lvl
                s = s + jnp.where(tpos < seq - d, pltpu.roll(s, seq - d, 0), 0.0)
            du_ref[:, sl] = (s - dp).astype(BF16)

    return pl.pallas_call(
        body, name="pool_bwd",
        out_shape=(jax.ShapeDtypeStruct((t, POOL_WIDTH), BF16),
                   jax.ShapeDtypeStruct((POOL_GROUPS, POOL_GROUP_DIM, POOL_GROUP_DIM), F32),
                   jax.ShapeDtypeStruct((1, POOL_WIDTH), F32)),
        grid=(t // seq,),
        in_specs=[pl.BlockSpec((seq, POOL_WIDTH), lambda b: (b, 0)), pl.BlockSpec((seq, POOL_WIDTH), lambda b: (b, 0)),
                  pl.BlockSpec((POOL_GROUPS, POOL_GROUP_DIM, POOL_GROUP_DIM), lambda b: (0, 0, 0)),
                  pl.BlockSpec((1, POOL_WIDTH), lambda b: (0, 0))],
        out_specs=(pl.BlockSpec((seq, POOL_WIDTH), lambda b: (b, 0)),
                   pl.BlockSpec((POOL_GROUPS, POOL_GROUP_DIM, POOL_GROUP_DIM), lambda b: (0, 0, 0)),
                   pl.BlockSpec((1, POOL_WIDTH), lambda b: (0, 0))),
        compiler_params=_cparams("arbitrary"),
    )(dps, p, mix, scale)


def _in_bwd(du, dq, dk, dv, dg2, dfl, dx1, x, g, wu, wqkv, wg2, wft, tm):
    t, d = x.shape
    tm = min(tm, t)
    rows = min(ROW_CHUNK, tm)
    aw = ATTN_WIDTH

    def body(du_ref, dq_ref, dk_ref, dv_ref, dg2_ref, dfl_ref, dx1_ref, x_ref, g_ref, wu_ref, wqkv_ref, wg2_ref, wft_ref,
             dx_ref, dg_ref):
        i = pl.program_id(0)

        @pl.when(i == 0)
        def _():
            dg_ref[...] = jnp.zeros_like(dg_ref)

        for r0 in range(0, tm, rows):
            rs = slice(r0, r0 + rows)
            dh = _dot(du_ref[rs, :], wu_ref[...])
            dh += _dot(dq_ref[rs, :], wqkv_ref[0:aw, :])
            dh += _dot(dk_ref[rs, :], wqkv_ref[aw:2 * aw, :])
            dh += _dot(dv_ref[rs, :], wqkv_ref[2 * aw:3 * aw, :])
            dh += _dot(dg2_ref[rs, :], wg2_ref[...])
            dh += _dot(dfl_ref[rs, :], wft_ref[...])
            dxn, dg = _rms_bwd(x_ref[rs, :], g_ref[...], dh)
            dx_ref[rs, :] = dx1_ref[rs, :] + dxn
            dg_ref[...] += dg

    row = lambda w: pl.BlockSpec((tm, w), lambda i: (i, 0))
    full = lambda a: pl.BlockSpec(a.shape, lambda i: (0, 0))
    return pl.pallas_call(
        body, name="in_bwd",
        out_shape=(jax.ShapeDtypeStruct((t, d), F32), jax.ShapeDtypeStruct((1, d), F32)),
        grid=(t // tm,),
        in_specs=[row(POOL_WIDTH), row(aw), row(aw), row(aw), row(2 * d), row(LANES), row(d), row(d),
                  pl.BlockSpec((1, d), lambda i: (0, 0)), full(wu), full(wqkv), full(wg2), full(wft)],
        out_specs=(row(d), pl.BlockSpec((1, d), lambda i: (0, 0))),
        compiler_params=_cparams("arbitrary"),
    )(du, dq, dk, dv, dg2, dfl, dx1, x, g, wu, wqkv, wg2, wft)


def _position():
    return lax.axis_index("x"), lax.axis_index("y"), lax.axis_index("c")


def _remote(src, dst, send_sem, recv_sem, device):
    return pltpu.make_async_remote_copy(src_ref=src, dst_ref=dst, send_sem=send_sem, recv_sem=recv_sem,
                                        device_id=device, device_id_type=MESH)


HBM = pl.BlockSpec(memory_space=pltpu.HBM)
SEM = pl.BlockSpec(memory_space=pltpu.SEMAPHORE)
DATAFLOW = pltpu.SideEffectType.DATAFLOW_SIDE_EFFECTING


def _copies_start(name, arrays, plan, m, dep=None):
    n = len(arrays)
    arrays = [pltpu.with_memory_space_constraint(a, pltpu.HBM) for a in arrays]

    def body(*refs):
        ins, send_sem, recv_sem, token = refs[:n], refs[n], refs[n + 1], refs[2 * n + 2]
        for i, (src, dst, device, _) in enumerate(plan(ins, *_position())):
            _remote(src, dst, send_sem.at[i], recv_sem.at[i], device).start()
        token[...] = jnp.zeros_like(token)

    dep_specs, dep_ops = _dep_args(dep)
    outs = pl.pallas_call(
        _after(body, n, dep), name=name,
        out_shape=(pltpu.SemaphoreType.DMA((m,)), pltpu.SemaphoreType.DMA((m,)),
                   *[pltpu.HBM(a.shape, a.dtype) for a in arrays], jax.ShapeDtypeStruct((8, LANES), F32)),
        in_specs=[HBM] * n + dep_specs, out_specs=(SEM, SEM, *[HBM] * n, pl.BlockSpec(memory_space=pltpu.VMEM)),
        input_output_aliases={i: i + 2 for i in range(n)},
        compiler_params=pltpu.CompilerParams(has_side_effects=DATAFLOW),
    )(*arrays, *dep_ops)
    return (outs[0], outs[1]), list(outs[2:2 + n]), outs[2 + n]


def _copies_wait(name, sems, arrays, plan, after):
    n = len(arrays)
    afters = list(after) if isinstance(after, (list, tuple)) else [after]

    def body(*refs):
        ins, send_sem, recv_sem = refs[:n], refs[n], refs[n + 1]
        for i, (src, dst, device, landing) in enumerate(plan(ins, *_position())):
            _remote(src, dst, send_sem.at[i], recv_sem.at[i], device).wait_send()
            _remote(landing, landing, send_sem.at[i], recv_sem.at[i], device).wait_recv()

    outs = pl.pallas_call(
        body, name=name,
        out_shape=tuple(pltpu.HBM(a.shape, a.dtype) for a in arrays),
        in_specs=[HBM] * n + [SEM, SEM] + [ANY] * len(afters), out_specs=tuple([HBM] * n),
        input_output_aliases={i: i for i in range(n)},
        compiler_params=pltpu.CompilerParams(has_side_effects=DATAFLOW),
    )(*arrays, sems[0], sems[1], *afters)
    return list(outs)


def _tie(x, dep):
    for token in _dep_list(dep):
        x = x + token[0, 0]
    return x


def _other_chips(x, y):
    return [(1 - x, y), (x, 1 - y), (1 - x, 1 - y)]


def _gather_begin(tag, shards, token, column_halves=False):
    n = len(shards)
    lands = [lax.empty((N_CHIPS,) + s.shape, s.dtype) for s in shards]
    if column_halves:
        cols = lambda ref, h: pl.ds(pl.multiple_of(h * (ref.shape[-1] // 2), LANES), ref.shape[-1] // 2)
        mine = lambda ref, h: ref.at[:, cols(ref, h)]
        landed = lambda ref, chip, h: ref.at[chip, :, cols(ref, h)]
    else:
        mine = lambda ref, h: ref.at[h]
        landed = lambda ref, chip, h: ref.at[chip, h]

    def plan(refs, x, y, c):
        return [(mine(refs[k], c), landed(refs[n + k], 2 * x + y, c), (ox, oy, c), landed(refs[n + k], 2 * ox + oy, c))
                for k in range(n) for ox, oy in _other_chips(x, y)]

    sems, thru, token = _copies_start(f"gather_{tag}_ici_start", list(shards) + lands, plan, 3 * n, dep=token)
    return dict(tag=tag, n=n, plan=plan, sems=sems, arrays=thru, token=token, landed=landed)


def _gather_forward(st, after):
    n, tag, landed = st["n"], st["tag"], st["landed"]
    thru = _copies_wait(f"gather_{tag}_ici_wait", st["sems"], st["arrays"], st["plan"], after)

    def plan(refs, x, y, c):
        return [(landed(refs[k], 2 * ox + oy, c), landed(refs[k], 2 * ox + oy, c), (x, y, 1 - c),
                 landed(refs[k], 2 * ox + oy, 1 - c))
                for k in range(n) for ox, oy in _other_chips(x, y)]

    sems, lands, token = _copies_start(f"gather_{tag}_fwd_start", thru[n:], plan, 3 * n)
    return dict(tag=tag, n=n, plan=plan, sems=sems, arrays=lands, token=token, shards=thru[:n])


def _gather_end(st, after, merge=True):
    lands = _copies_wait(f"gather_{st['tag']}_fwd_wait", st["sems"], st["arrays"], st["plan"], after)
    if not merge:
        return lands, st["shards"]
    me = 2 * lax.axis_index("x") + lax.axis_index("y")
    return [lax.dynamic_update_index_in_dim(g, s, me, 0) for g, s in zip(lands, st["shards"])]


def _add_keep_give(name, pos, a, a_keep, a_give, b, b_keep, b_give, steps):
    r, c = b.shape[-2:]

    def spec(arr, fn):
        lead = arr.ndim - 2

        def index(i, p):
            idx = tuple(fn(i, p))
            return idx if len(idx) == arr.ndim else idx + (0, 0)

        return pl.BlockSpec((None,) * lead + (r, c), index)

    out_spec = pl.BlockSpec((None, r, c), lambda i, p: (i, 0, 0))

    def body(p_ref, ak_ref, bk_ref, ag_ref, bg_ref, keep_ref, give_ref):
        keep_ref[...] = ak_ref[...] + bk_ref[...].astype(F32)
        give_ref[...] = (ag_ref[...] + bg_ref[...].astype(F32)).astype(BF16)

    return pl.pallas_call(
        body, name=name,
        out_shape=(jax.ShapeDtypeStruct((steps, r, c), F32), jax.ShapeDtypeStruct((steps, r, c), BF16)),
        grid_spec=pltpu.PrefetchScalarGridSpec(
            num_scalar_prefetch=1, grid=(steps,),
            in_specs=[spec(a, a_keep), spec(b, b_keep), spec(a, a_give), spec(b, b_give)],
            out_specs=(out_spec, out_spec)),
        compiler_params=_cparams("parallel"),
    )(pos, a, b, a, b)


def _add_last(name, a, b):
    _, r, c = a.shape
    blk = pl.BlockSpec((None, r, c), lambda i: (0, 0, 0))

    def body(a_ref, b_ref, o_ref):
        o_ref[...] = a_ref[...] + b_ref[...].astype(F32)

    return pl.pallas_call(
        body, name=name, out_shape=jax.ShapeDtypeStruct((r, c), F32), grid=(1,), in_specs=[blk, blk],
        out_specs=pl.BlockSpec((r, c), lambda i: (0, 0)), compiler_params=_cparams("arbitrary"),
    )(a, b)


def _exchange_part(gives, lands, peer_fn):
    n = len(gives)

    def plan(refs, x, y, c):
        return [(refs[k], refs[n + k], peer_fn(x, y, c), refs[n + k]) for k in range(n)]

    return list(gives) + list(lands), plan, n


def _join_parts(parts):
    offsets, total = [], 0
    for arrays, _, _ in parts:
        offsets.append(total)
        total += len(arrays)

    def plan(refs, x, y, c):
        copies = []
        for (arrays, part_plan, _), off in zip(parts, offsets):
            copies += part_plan(refs[off:off + len(arrays)], x, y, c)
        return copies

    return [a for arrays, _, _ in parts for a in arrays], plan, sum(m for _, _, m in parts)


def _reduce_begin(tag, grads, column_halves=False):
    n = len(grads)
    if column_halves:
        half = lambda ref, j, h: ref.at[j, :, pl.ds(pl.multiple_of(h * (ref.shape[2] // 2), LANES), ref.shape[2] // 2)]
        lands = [lax.empty((N_CHIPS, g.shape[1], g.shape[2] // 2), F32) for g in grads]
    else:
        half = lambda ref, j, h: ref.at[j, h]
        lands = [lax.empty((N_CHIPS,) + g.shape[2:], F32) for g in grads]

    def plan(refs, x, y, c):
        return [(half(refs[k], j, 1 - c), refs[n + k].at[j], (x, y, 1 - c), refs[n + k].at[j])
                for k in range(n) for j in range(N_CHIPS)]

    return dict(tag=tag, n=n, stage="c", grads=list(grads), column_halves=column_halves,
                part=(list(grads) + lands, plan, N_CHIPS * n))


def _reduce_next(st, thru):
    tag, n, stage = st["tag"], st["n"], st["stage"]
    first, recv = thru[:n], thru[n:]
    x, y, c = _position()
    if stage == "c":
        pos = jnp.stack([c, x]).astype(jnp.int32)
        if st["column_halves"]:
            mine = lambda chip: (lambda i, p: (chip(p) + i, 0, p[0]))
        else:
            mine = lambda chip: (lambda i, p: (chip(p) + i, p[0]))
        sums = [_add_keep_give(
            f"rs{tag}_c_add{k}", pos,
            first[k], mine(lambda p: 2 * p[1]), mine(lambda p: 2 * (1 - p[1])),
            recv[k], lambda i, p: (2 * p[1] + i,), lambda i, p: (2 * (1 - p[1]) + i,), 2) for k in range(n)]
        lands = [lax.empty(s[1].shape, BF16) for s in sums]
        return dict(tag=tag, n=n, stage="x", keep=[s[0] for s in sums],
                    part=_exchange_part([s[1] for s in sums], lands, lambda x, y, c: (1 - x, y, c)))
    if stage == "x":
        pos = jnp.stack([y]).astype(jnp.int32)
        sums = [_add_keep_give(
            f"rs{tag}_x_add{k}", pos,
            st["keep"][k], lambda i, p: (p[0],), lambda i, p: (1 - p[0],),
            recv[k], lambda i, p: (p[0],), lambda i, p: (1 - p[0],), 1) for k in range(n)]
        lands = [lax.empty(s[1].shape, BF16) for s in sums]
        return dict(tag=tag, n=n, stage="y", keep=[s[0] for s in sums],
                    part=_exchange_part([s[1] for s in sums], lands, lambda x, y, c: (x, 1 - y, c)))
    if stage == "y":
        mine = [_add_last(f"rs{tag}_y_add{k}", st["keep"][k], recv[k]) for k in range(n)]
        lands = [lax.empty(m.shape, F32) for m in mine]
        return dict(tag=tag, n=n, stage="swap", part=_exchange_part(mine, lands, lambda x, y, c: (x, y, 1 - c)))
    return dict(tag=tag, done=list(zip(first, recv)))


def _small_begin(tag, v):
    land = lax.empty((N_DEV,) + v.shape, F32)
    flips = [(fx, fy, fc) for fx in (0, 1) for fy in (0, 1) for fc in (0, 1)][1:]

    def plan(refs, x, y, c):
        copies = []
        for fx, fy, fc in flips:
            px, py, pc = (1 - x if fx else x), (1 - y if fy else y), (1 - c if fc else c)
            copies.append((refs[0], refs[1].at[4 * x + 2 * y + c], (px, py, pc), refs[1].at[4 * px + 2 * py + pc]))
        return copies

    return dict(tag=tag, n=1, stage="swap", grads=[v], part=([v, land], plan, len(flips)))


def _small_sum(name, own, land):
    x, y, c = _position()
    me = jnp.stack([4 * x + 2 * y + c]).astype(jnp.int32)

    def body(me_ref, own_ref, land_ref, out_ref):
        term = lambda dev: jnp.where(me_ref[0] == dev, own_ref[...], land_ref[dev])
        acc = term(0)
        for dev in range(1, N_DEV):
            acc = acc + term(dev)
        out_ref[...] = acc

    return pl.pallas_call(
        body, name=name, out_shape=jax.ShapeDtypeStruct(own.shape, F32),
        grid_spec=pltpu.PrefetchScalarGridSpec(
            num_scalar_prefetch=1, grid=(1,),
            in_specs=[pl.BlockSpec(own.shape, lambda i, m: (0, 0)), pl.BlockSpec(land.shape, lambda i, m: (0, 0, 0))],
            out_specs=pl.BlockSpec(own.shape, lambda i, m: (0, 0))),
        compiler_params=_cparams("arbitrary"),
    )(me, own, land)


def _adamw_update(w, gg, m, v):
    mn = ADAM_B1 * m + (1.0 - ADAM_B1) * gg
    vn = ADAM_B2 * v + (1.0 - ADAM_B2) * (gg * gg)
    m_hat = mn / (1.0 - ADAM_B1 ** ADAM_STEP)
    v_hat = vn / (1.0 - ADAM_B2 ** ADAM_STEP)
    return -ADAM_LR * (m_hat / (jnp.sqrt(v_hat) + ADAM_EPS) + ADAM_WD * w), mn, vn


def _adamw(name, w, g, m, v):
    def body(w_ref, g_ref, m_ref, v_ref, d_ref, mo_ref, vo_ref):
        d_ref[...], mo_ref[...], vo_ref[...] = _adamw_update(w_ref[...], g_ref[...], m_ref[...], v_ref[...])

    blk = pl.BlockSpec(w.shape, lambda i: (0, 0))
    return pl.pallas_call(
        body, name=name, out_shape=(jax.ShapeDtypeStruct(w.shape, F32),) * 3, grid=(1,),
        in_specs=[blk] * 4, out_specs=(blk,) * 3, compiler_params=_cparams("arbitrary"),
    )(w, g, m, v)


def _rows_to_bf16(name, w):
    r, _, c = w.shape

    def body(w_ref, o_ref):
        o_ref[...] = w_ref[:, 0, :].astype(BF16)

    return pl.pallas_call(
        body, name=name, out_shape=jax.ShapeDtypeStruct((r, c), BF16), grid=(1,),
        in_specs=[pl.BlockSpec((r, 1, c), lambda i: (0, 0, 0))], out_specs=pl.BlockSpec((r, c), lambda i: (0, 0)),
        compiler_params=_cparams("arbitrary"),
    )(w)


def _adamw_rows(name, pos_c, w, g_mine, g_other, m, v):
    r, _, c = w.shape
    ch = c // 2

    def body(p_ref, w_ref, gm_ref, go_ref, m_ref, v_ref, g_ref, d_ref, mo_ref, vo_ref):
        gg = jnp.where(pl.program_id(0) == p_ref[0], gm_ref[...], go_ref[...])
        dl, mn, vn = _adamw_update(w_ref[:, 0, :], gg, m_ref[:, 0, :], v_ref[:, 0, :])
        g_ref[:, 0, :] = gg
        d_ref[:, 0, :] = dl
        mo_ref[:, 0, :] = mn
        vo_ref[:, 0, :] = vn

    rows = pl.BlockSpec((r, 1, ch), lambda h, p: (0, 0, h))
    half = pl.BlockSpec((r, ch), lambda h, p: (0, 0))
    return pl.pallas_call(
        body, name=name, out_shape=(jax.ShapeDtypeStruct(w.shape, F32),) * 4,
        grid_spec=pltpu.PrefetchScalarGridSpec(
            num_scalar_prefetch=1, grid=(2,), in_specs=[rows, half, half, rows, rows], out_specs=(rows,) * 4),
        compiler_params=_cparams("parallel"),
    )(pos_c, w, g_mine, g_other, m, v)


def _adamw_halves(name, pos_c, w, g_mine, g_other, m, v, tr, dep=None):
    r, c = w.shape
    rh = r // 2
    tr = tr if rh % tr == 0 else rh
    nt = rh // tr

    def body(p_ref, w_ref, gm_ref, go_ref, m_ref, v_ref, g_ref, d_ref, mo_ref, vo_ref):
        gg = jnp.where(pl.program_id(0) == p_ref[0], gm_ref[...], go_ref[...])
        g_ref[...] = gg
        d_ref[...], mo_ref[...], vo_ref[...] = _adamw_update(w_ref[...], gg, m_ref[...], v_ref[...])

    full = pl.BlockSpec((tr, c), lambda h, i, p: (h * nt + i, 0))
    half = pl.BlockSpec((tr, c), lambda h, i, p: (i, 0))
    dep_specs, dep_ops = _dep_args(dep)
    return pl.pallas_call(
        _after(body, 6, dep), name=name, out_shape=(jax.ShapeDtypeStruct((r, c), F32),) * 4,
        grid_spec=pltpu.PrefetchScalarGridSpec(
            num_scalar_prefetch=1, grid=(2, nt),
            in_specs=[full, half, half, full, full] + dep_specs, out_specs=(full,) * 4),
        compiler_params=_cparams("parallel", "parallel"),
    )(pos_c, w, g_mine, g_other, m, v, *dep_ops)


def _col_sharded_to_comm(g):
    k, n = g.shape
    return g.reshape(2, k // 2, N_CHIPS, n // N_CHIPS).transpose(2, 0, 1, 3)


def _row_sharded_to_comm(g):
    r, c = g.shape
    return g.reshape(N_CHIPS, 2, r // (2 * N_CHIPS), c)


def _col_sharded_full(g):
    _, _, rh, c = g.shape
    return g.reshape(N_CHIPS, 2 * rh, c).transpose(1, 0, 2).reshape(2 * rh, N_CHIPS * c)


def _row_sharded_full(g):
    _, _, rh, c = g.shape
    return g.reshape(N_CHIPS * 2 * rh, c)


def _chip_rows(w3, start, stop, own=None, me=None):
    r = w3.shape[1]
    parts = []
    for chip in range(N_CHIPS):
        lo, hi = max(start - chip * r, 0), min(stop - chip * r, r)
        if lo < hi:
            part = w3[chip, lo:hi]
            parts.append(part if own is None else jnp.where(me == chip, own[lo:hi], part))
    return parts[0] if len(parts) == 1 else jnp.concatenate(parts, axis=0)


def _pack_small(g1, bfv, mix, scale, g2n, gf, extra=None):
    row8 = jnp.pad(bfv.reshape(1, N_HEADS), ((0, 0), (0, LANES - N_HEADS)))
    if extra is not None:
        row8 = row8 + jnp.pad(extra[:, :1], ((0, 0), (N_HEADS, LANES - N_HEADS - 1)))
    return jnp.concatenate([
        g1.reshape(8, LANES), jnp.pad(row8, ((0, 7), (0, 0))), mix.reshape(512, LANES),
        jnp.pad(scale.reshape(4, LANES), ((0, 4), (0, 0))), g2n.reshape(8, LANES), gf.reshape(8, LANES)], axis=0)


def _unpack_small(s, like):
    g1, bfv, mix, scale, g2n, gf = like
    return (s[0:8].reshape(g1.shape), s[8, :N_HEADS].reshape(bfv.shape), s[16:528].reshape(mix.shape),
            s[528:532].reshape(scale.shape), s[536:544].reshape(g2n.shape), s[544:552].reshape(gf.shape))


class _MeshLinks:
    def __init__(self, shards_in, shards_rest):
        self.gin = _gather_begin("in", shards_in, None, column_halves=True)
        self.shards_rest = shards_rest
        self.tokens = {"gather": self.gin["token"]}
        self.groups, self.flight, self.slot = {}, None, 0

    @property
    def token(self):
        return list(self.tokens.values())

    def tie(self, x):
        return _tie(x, self.token)

    def weights_in(self, after):
        st = _gather_forward(self.gin, after)
        self.grest = _gather_begin("rest", self.shards_rest, st["token"])
        self.tokens["gather"] = self.grest["token"]
        (g,), (own,) = _gather_end(st, self.grest["token"], merge=False)
        return g, own, 2 * lax.axis_index("x") + lax.axis_index("y")

    def rest_forward(self, after):
        self.grest = _gather_forward(self.grest, after)
        self.tokens["gather"] = self.grest["token"]

    def weights_rest(self, after):
        g = _gather_end(self.grest, after)
        del self.tokens["gather"]
        return [_col_sharded_full(g[0]), _col_sharded_full(g[1])] + [_row_sharded_full(a) for a in g[2:]]

    def advance(self, after, begin=()):
        slot = self.slot
        self.slot += 1
        if self.flight is not None:
            tags, sems, parts = self.flight
            arrays, plan, _ = _join_parts(parts)
            thru = _copies_wait(f"slot{slot}_wait", sems, arrays, plan, after)
            for tag, part in zip(tags, parts):
                self.groups[tag] = _reduce_next(self.groups[tag], thru[:len(part[0])])
                thru = thru[len(part[0]):]
        for st in begin:
            self.groups[st["tag"]] = st
        live = [(tag, st["part"]) for tag, st in self.groups.items() if "part" in st]
        self.flight = None
        self.tokens.pop("reduce", None)
        if live:
            arrays, plan, m = _join_parts([part for _, part in live])
            sems, thru, token = _copies_start(f"slot{slot}_start", arrays, plan, m)
            parts = []
            for _, (part_arrays, part_plan, part_m) in live:
                parts.append((thru[:len(part_arrays)], part_plan, part_m))
                thru = thru[len(part_arrays):]
            self.flight = ([tag for tag, _ in live], sems, parts)
            self.tokens["reduce"] = token

    def reduced(self, tag):
        return self.groups[tag]["done"]


class _NoLinks:
    token = None

    def __init__(self, w_in, rest):
        self.w_in, self.rest, self.grads = w_in, rest, {}

    def tie(self, x):
        return x

    def weights_in(self, after):
        return self.w_in, None, None

    def rest_forward(self, after):
        pass

    def weights_rest(self, after):
        return self.rest

    def advance(self, after, begin=()):
        for st in begin:
            self.grads[st["tag"]] = st["grads"]


def _local_step(links, x, target, seq, norm1_g, b_forget, pool_mix, pool_scale, norm2_g, norm_f_g, between=None):
    t, d = x.shape
    tq = min(256, seq)
    aw = ATTN_WIDTH
    o_q, o_f, o_g = POOL_WIDTH, POOL_WIDTH + 3 * aw, POOL_WIDTH + 3 * aw + N_HEADS
    bf = jnp.pad(b_forget, ((0, 0), (0, LANES - N_HEADS)))
    mixb = pool_mix.astype(BF16)

    h = _norm_fwd("norm1_fwd", x, links.tie(norm1_g), 512)
    w_in, own, me = links.weights_in(h)
    wu = _chip_rows(w_in, 0, o_q, own, me)
    wqkv = _chip_rows(w_in, o_q, o_f, own, me)
    wft = jnp.pad(_chip_rows(w_in, o_f, o_g, own, me), ((0, LANES - N_HEADS), (0, 0)))
    wg2 = _chip_rows(w_in, o_g, N_CHIPS * w_in.shape[1], own, me)
    wf = wft.T
    u = _matmul("mm_u", h, wu, "nt", F32, 1024, 512, d)
    g2 = _matmul("mm_gates", h, wg2, "nt", BF16, 1024, 1024, d)
    fl, fcum = _forget_fwd(h, wf, bf, seq)
    qa, ka, v = _attn_prep(h, _head_blocks(wqkv[:aw]), _head_blocks(wqkv[aw:2 * aw]), wqkv[2 * aw:], fcum, 1024)
    p, ps = _pool_fwd(u, mixb, pool_scale, seq)
    links.rest_forward([ps, qa, g2])
    o, lse = _attn_fwd(qa, ka, v, seq, tq, dep=links.token)
    w_pool_out, w_attn_out, w_out, w_ffn_gate, w_ffn_up, w_ffn_down = links.weights_rest(o)
    merged, x1 = _merge_fwd(x, ps, o, g2, w_pool_out, w_attn_out, w_out, 512)
    h2, gt, up, act, x2 = _ffn_fwd(x1, norm2_g, w_ffn_gate, w_ffn_up, w_ffn_down, 1024, 256)
    loss, dx2, d_gf = _final_fwd_bwd(x2, target, norm_f_g, 512)

    dgt, dup, dx1, d_g2n = _ffn_bwd(dx2, x1, norm2_g, gt, up, w_ffn_gate, w_ffn_up, w_ffn_down, 1024, 256)
    d_wd = _matmul("dw_down", act, dx2, "tn", F32, 1408, 1024, 1024)
    d_wg = _matmul("dw_gate", dgt, h2, "tn", F32, 1408, 1024, 1024)
    d_wu = _matmul("dw_up", dup, h2, "tn", F32, 1408, 1024, 1024)
    links.advance(None, begin=[_reduce_begin("a", [_row_sharded_to_comm(g) for g in (d_wg, d_wu, d_wd)])])
    dpy, day, dg2, dps, da = _merge_bwd(dx1, ps, o, g2, w_pool_out, w_attn_out, w_out, 512, dep=links.token)
    links.advance(dps)
    d_wout = _matmul("dw_out", merged, dx1, "tn", F32, 1024, 1024, 1024)
    d_wpo = _matmul("dw_pool_out", ps, dpy, "tn", F32, 512, 1024, 1024)
    d_wao = _matmul("dw_attn_out", o, day, "tn", F32, 512, 1024, 1024)
    dq, dk, dv, dfr = _attn_bwd(qa, ka, v, da, lse, seq, tq, dep=links.token)
    links.advance(dq, begin=[_reduce_begin(
        "m", [_col_sharded_to_comm(d_wpo), _col_sharded_to_comm(d_wao), _row_sharded_to_comm(d_wout)])])
    dfc = jnp.pad(dfr.reshape(N_HEADS, t).T, ((0, 0), (0, LANES - N_HEADS)))
    dfl, d_bf = _forget_bwd(dfc, fl, bf, seq)
    du, d_mix, d_scale = _pool_bwd(dps, p, mixb, links.tie(pool_scale), seq)
    d_wu_in = _matmul("dw_in_u", du, h, "tn", F32, 512, 1024, 1024, dep=links.token)
    small = (jnp.zeros_like(norm1_g), d_bf[:, :N_HEADS], d_mix, d_scale, d_g2n, d_gf)
    d_wq = _matmul("dw_in_q", dq, h, "tn", F32, 512, 1024, 1024, dep=links.token)
    d_wk = _matmul("dw_in_k", dk, h, "tn", F32, 512, 1024, 1024, dep=links.token)
    d_wv = _matmul("dw_in_v", dv, h, "tn", F32, 512, 1024, 1024, dep=links.token)
    links.advance([d_wu_in, d_wq, d_wk, d_wv], begin=[_small_begin("small", _pack_small(*small, extra=loss))])
    d_wf = _matmul("dw_in_f", dfl, h, "tn", F32, LANES, 1024, 512)
    d_wg2 = _matmul("dw_in_gates", dg2, h, "tn", F32, 1024, 1024, 1024, dep=links.token)
    d_win = jnp.concatenate([d_wu_in, d_wq, d_wk, d_wv, d_wf[:N_HEADS], d_wg2], axis=0)
    comm_b = [d_win.reshape(N_CHIPS, d_win.shape[0] // N_CHIPS, d)]
    links.advance(comm_b, begin=[_reduce_begin("b", comm_b, column_halves=True)])
    if between is not None:
        between()
    dx, d_g1 = _in_bwd(du, dq, dk, dv, dg2, dfl, dx1, x, links.tie(norm1_g), wu, wqkv, wg2, wft, 512)
    return loss, dx, (d_g1,) + small[1:]


def kernel(x, norm1_g, w_in, b_forget, pool_mix, pool_scale, w_pool_out, w_attn_out, w_out, norm2_g, w_ffn_gate, w_ffn_up, w_ffn_down, norm_f_g, loss_target, m_norm1_g, m_w_in, m_b_forget, m_pool_mix, m_pool_scale, m_w_pool_out, m_w_attn_out, m_w_out, m_norm2_g, m_w_ffn_gate, m_w_ffn_up, m_w_ffn_down, m_norm_f_g, v_norm1_g, v_w_in, v_b_forget, v_pool_mix, v_pool_scale, v_w_pool_out, v_w_attn_out, v_w_out, v_norm2_g, v_w_ffn_gate, v_w_ffn_up, v_w_ffn_down, v_norm_f_g):
    nb, seq, d = x.shape
    group_a = ((w_ffn_gate, m_w_ffn_gate, v_w_ffn_gate, True, 9), (w_ffn_up, m_w_ffn_up, v_w_ffn_up, True, 10),
               (w_ffn_down, m_w_ffn_down, v_w_ffn_down, False, 11))
    group_m = ((w_pool_out, m_w_pool_out, v_w_pool_out, False, 5), (w_attn_out, m_w_attn_out, v_w_attn_out, False, 6),
               (w_out, m_w_out, v_w_out, False, 7))
    group_b = ((w_in, m_w_in, v_w_in, False, 1),)
    small_w = (norm1_g, b_forget, pool_mix, pool_scale, norm2_g, norm_f_g)
    small_m = (m_norm1_g, m_b_forget, m_pool_mix, m_pool_scale, m_norm2_g, m_norm_f_g)
    small_v = (v_norm1_g, v_b_forget, v_pool_mix, v_pool_scale, v_norm2_g, v_norm_f_g)
    small_pos = (0, 2, 3, 4, 8, 12)
    view = lambda a, tr: a[0].T if tr else a[0]
    unview = lambda a, tr, like: (a.T if tr else a).reshape(like.shape)

    def shard(w, tr):
        lw = view(w, tr).astype(BF16)
        return lw.reshape(2, lw.shape[0] // 2, lw.shape[1])

    cm = lambda a: jnp.transpose(a, (2, 0, 1))
    shard_in = _rows_to_bf16("w_in_to_bf16", cm(w_in))
    links = _MeshLinks([shard_in],
                       [shard(w_pool_out, False), shard(w_attn_out, False), shard(w_out, False),
                        shard(w_ffn_gate, True), shard(w_ffn_up, True), shard(w_ffn_down, False)])
    grads, deltas, new_m, new_v = [None] * 13, [None] * 13, [None] * 13, [None] * 13
    pos_c = jnp.stack([lax.axis_index("c")]).astype(jnp.int32)

    def update(tag, group, dep, members=(0, 1, 2)):
        last = []
        reduced = links.reduced(tag)
        for k in members:
            (w, m, v, tr, pos), (mine, other) = group[k], reduced[k]
            outs = _adamw_halves(f"adamw_{tag}{k}", pos_c, view(w, tr), mine, other, view(m, tr), view(v, tr), 256,
                                 dep=dep)
            grads[pos], deltas[pos], new_m[pos], new_v[pos] = (unview(a, tr, w) for a in outs)
            last.append(outs[1])
        return last

    def update_a():
        links.advance(update("a", group_a, links.token, members=(0,)))

    loss, dx, small_g = _local_step(
        links, x.reshape(nb * seq, d), loss_target.reshape(nb * seq, d), seq,
        norm1_g, b_forget, pool_mix[0], pool_scale, norm2_g, norm_f_g.reshape(1, d), between=update_a)

    links.advance(dx, begin=[_small_begin("g1", small_g[0].reshape(8, LANES))])
    last = update("m", group_m, links.token) + update("a", group_a, links.token, members=(2,))
    small_rest = _small_sum("small_sum", *links.reduced("small")[0])
    links.advance(last + [small_rest])
    last = update("a", group_a, links.token, members=(1,))
    small_sum = jnp.concatenate([_small_sum("g1_sum", *links.reduced("g1")[0]), small_rest[8:]], axis=0)
    loss_out = small_sum[8, N_HEADS]
    dl, mn, vn = _adamw("adamw_small", _pack_small(*small_w), small_sum * _small_mask(), _pack_small(*small_m),
                        _pack_small(*small_v))
    for pos, g, a, b, e in zip(small_pos, _unpack_small(small_sum, small_w), _unpack_small(dl, small_w),
                               _unpack_small(mn, small_w), _unpack_small(vn, small_w)):
        grads[pos], deltas[pos], new_m[pos], new_v[pos] = g, a, b, e
    links.advance(last + [dl])
    (mine, other), = links.reduced("b")
    outs = _adamw_rows("adamw_b0", pos_c, cm(w_in), mine, other, cm(m_w_in), cm(v_w_in))
    grads[1], deltas[1], new_m[1], new_v[1] = (jnp.transpose(a, (1, 2, 0)) for a in outs)

    return (loss_out, dx.reshape(nb, seq, d), *grads, *deltas, *new_m, *new_v)


def _small_mask():
    rows = lax.broadcasted_iota(jnp.int32, (552, LANES), 0)
    lanes = lax.broadcasted_iota(jnp.int32, (552, LANES), 1)
    return jnp.where(jnp.logical_and(rows == 8, lanes == N_HEADS), 0.0, 1.0).astype(F32)
```

```python
import jax
import jax.numpy as jnp
from jax import lax
from jax.experimental import pallas as pl
from jax.experimental.pallas import tpu as pltpu

F32 = jnp.float32
BF16 = jnp.bfloat16

D_MODEL = 1024
POOL_WINDOWS = (2, 4, 8, 16)
POOL_GROUPS = 4
POOL_GROUP_DIM = 128
POOL_WIDTH = 512
HEAD_DIM = 64
N_HEADS = 8
ATTN_WIDTH = 512
D_FF = 2816
RMS_EPS = 1e-6
ATTN_SCALE = HEAD_DIM ** -0.5
NEG_BIG = -1e30

ADAM_LR = 0.001
ADAM_B1 = 0.9
ADAM_B2 = 0.999
ADAM_EPS = 1e-08
ADAM_WD = 0.01
ADAM_STEP = 10

LANES = 128
N_CHIPS = 4
N_DEV = 8
VMEM_LIMIT_V7X = 52 * 1024 * 1024
ROW_CHUNK = 256
MESH = pl.DeviceIdType.MESH
ANY = pl.BlockSpec(memory_space=pl.ANY)


def _cparams(*sem):
    return pltpu.CompilerParams(dimension_semantics=sem if sem else None, vmem_limit_bytes=VMEM_LIMIT_V7X)


def _dep_list(dep):
    return [] if dep is None else (list(dep) if isinstance(dep, (list, tuple)) else [dep])


def _after(body, n_in, dep):
    k = len(_dep_list(dep))
    if k == 0:
        return body

    def wrapped(*refs):
        body(*refs[:n_in], *refs[n_in + k:])

    return wrapped


def _dep_args(dep):
    deps = _dep_list(dep)
    return [ANY] * len(deps), deps


def _dot(a, b):
    return lax.dot_general(a, b, (((1,), (0,)), ((), ())), preferred_element_type=F32)


def _dot_nt(a, b):
    return lax.dot_general(a, b, (((1,), (1,)), ((), ())), preferred_element_type=F32)


def _dot_tn(a, b):
    return lax.dot_general(a, b, (((0,), (0,)), ((), ())), preferred_element_type=F32)


def _sigmoid(x):
    return jax.nn.sigmoid(x)


def _rms_fwd(x, g):
    r = lax.rsqrt(jnp.mean(x * x, axis=-1, keepdims=True) + RMS_EPS)
    return (x * r) * g


def _rms_bwd(x, g, dy):
    r = lax.rsqrt(jnp.mean(x * x, axis=-1, keepdims=True) + RMS_EPS)
    xh = x * r
    dg = jnp.sum(dy * xh, axis=0, keepdims=True)
    dxh = dy * g
    dx = r * (dxh - xh * jnp.mean(dxh * xh, axis=-1, keepdims=True))
    return dx, dg


def _matmul(name, a, b, mode, out_dtype, tm, tn, tk, dep=None):
    if mode == "nn":
        (m, k), (_, n) = a.shape, b.shape
    elif mode == "nt":
        (m, k), (n, _) = a.shape, b.shape
    else:
        (k, m), (_, n) = a.shape, b.shape
    tm, tn, tk = min(tm, m), min(tn, n), min(tk, k)
    assert m % tm == 0 and n % tn == 0 and k % tk == 0, (name, m, n, k, tm, tn, tk)
    nk = k // tk
    if mode == "tn":
        a_spec = pl.BlockSpec((tk, tm), lambda i, j, kk: (kk, i))
    else:
        a_spec = pl.BlockSpec((tm, tk), lambda i, j, kk: (i, kk))
    if mode == "nt":
        b_spec = pl.BlockSpec((tn, tk), lambda i, j, kk: (j, kk))
    else:
        b_spec = pl.BlockSpec((tk, tn), lambda i, j, kk: (kk, j))
    dot = {"nn": _dot, "nt": _dot_nt, "tn": _dot_tn}[mode]
    use_scratch = nk > 1 and out_dtype != F32

    def body(a_ref, b_ref, o_ref, *scratch):
        if nk == 1 and mode != "tn":
            rows = min(ROW_CHUNK, tm)
            bb = b_ref[...].astype(BF16)
            for r0 in range(0, tm, rows):
                o_ref[r0:r0 + rows, :] = dot(a_ref[r0:r0 + rows, :].astype(BF16), bb).astype(out_dtype)
            return
        prod = dot(a_ref[...].astype(BF16), b_ref[...].astype(BF16))
        if nk == 1:
            o_ref[...] = prod.astype(out_dtype)
            return
        acc = scratch[0] if use_scratch else o_ref
        kk = pl.program_id(2)

        @pl.when(kk == 0)
        def _():
            acc[...] = prod

        @pl.when(kk > 0)
        def _():
            acc[...] += prod

        if use_scratch:
            @pl.when(kk == nk - 1)
            def _():
                o_ref[...] = acc[...].astype(out_dtype)

    dep_specs, dep_ops = _dep_args(dep)
    return pl.pallas_call(
        _after(body, 2, dep),
        name=name,
        out_shape=jax.ShapeDtypeStruct((m, n), out_dtype),
        grid=(m // tm, n // tn, nk),
        in_specs=[a_spec, b_spec] + dep_specs,
        out_specs=pl.BlockSpec((tm, tn), lambda i, j, kk: (i, j)),
        scratch_shapes=[pltpu.VMEM((tm, tn), F32)] if use_scratch else [],
        compiler_params=_cparams("parallel", "parallel", "arbitrary"),
    )(a, b, *dep_ops)


def _norm_fwd(name, x, g, tm):
    t, d = x.shape
    tm = min(tm, t)

    def body(x_ref, g_ref, h_ref):
        h_ref[...] = _rms_fwd(x_ref[...], g_ref[...]).astype(BF16)

    return pl.pallas_call(
        body, name=name, out_shape=jax.ShapeDtypeStruct((t, d), BF16), grid=(t // tm,),
        in_specs=[pl.BlockSpec((tm, d), lambda i: (i, 0)), pl.BlockSpec((1, d), lambda i: (0, 0))],
        out_specs=pl.BlockSpec((tm, d), lambda i: (i, 0)),
        compiler_params=_cparams("parallel"),
    )(x, g)


def _split3(x):
    hi = x.astype(BF16)
    r1 = x - hi.astype(F32)
    mid = r1.astype(BF16)
    lo = (r1 - mid.astype(F32)).astype(BF16)
    return hi, mid, lo


def _tri_dot(tri, x):
    hi, mid, lo = _split3(x)
    return _dot(tri, hi) + _dot(tri, mid) + _dot(tri, lo)


def _forget_fwd(h, wf, bf, seq):
    t, d = h.shape
    cb = min(256, seq)

    def body(h_ref, wf_ref, bf_ref, fl_ref, fc_ref):
        fl = _dot(h_ref[...], wf_ref[...])
        fl_ref[...] = fl
        xx = fl + bf_ref[...]
        lf = jnp.minimum(xx, 0.0) - jnp.log(1.0 + jnp.exp(-jnp.abs(xx)))
        ri = lax.broadcasted_iota(jnp.int32, (cb, cb), 0)
        ci = lax.broadcasted_iota(jnp.int32, (cb, cb), 1)
        tri = (ri >= ci).astype(BF16)
        carry = jnp.zeros((1, LANES), F32)
        for blk in range(seq // cb):
            cs = _tri_dot(tri, lf[blk * cb:(blk + 1) * cb]) + carry
            fc_ref[blk * cb:(blk + 1) * cb, :] = cs
            carry = cs[cb - 1:cb, :]

    return pl.pallas_call(
        body, name="forget_fwd",
        out_shape=(jax.ShapeDtypeStruct((t, LANES), F32), jax.ShapeDtypeStruct((t, LANES), F32)),
        grid=(t // seq,),
        in_specs=[pl.BlockSpec((seq, d), lambda b: (b, 0)), pl.BlockSpec((d, LANES), lambda b: (0, 0)),
                  pl.BlockSpec((1, LANES), lambda b: (0, 0))],
        out_specs=(pl.BlockSpec((seq, LANES), lambda b: (b, 0)), pl.BlockSpec((seq, LANES), lambda b: (b, 0))),
        compiler_params=_cparams("parallel"),
    )(h, wf, bf)


def _pool_fwd(u, mix, scale, seq):
    t = u.shape[0]

    def body(u_ref, mix_ref, sc_ref, p_ref, ps_ref):
        tpos = lax.broadcasted_iota(jnp.int32, (seq, POOL_GROUP_DIM), 0)
        for g in range(POOL_GROUPS):
            sl = slice(g * POOL_GROUP_DIM, (g + 1) * POOL_GROUP_DIM)
            ug = u_ref[:, sl]
            s = ug
            for lvl in range(g + 1):
                d = 2 ** lvl
                s = s + jnp.where(tpos >= d, pltpu.roll(s, d, 0), 0.0)
            cnt = jnp.minimum(tpos + 1, POOL_WINDOWS[g]).astype(F32)
            pb = (s / cnt - ug).astype(BF16)
            p_ref[:, sl] = pb
            ps_ref[:, sl] = (_dot(pb, mix_ref[g]) * sc_ref[:, sl]).astype(BF16)

    return pl.pallas_call(
        body, name="pool_fwd",
        out_shape=(jax.ShapeDtypeStruct((t, POOL_WIDTH), BF16), jax.ShapeDtypeStruct((t, POOL_WIDTH), BF16)),
        grid=(t // seq,),
        in_specs=[pl.BlockSpec((seq, POOL_WIDTH), lambda b: (b, 0)),
                  pl.BlockSpec((POOL_GROUPS, POOL_GROUP_DIM, POOL_GROUP_DIM), lambda b: (0, 0, 0)),
                  pl.BlockSpec((1, POOL_WIDTH), lambda b: (0, 0))],
        out_specs=(pl.BlockSpec((seq, POOL_WIDTH), lambda b: (b, 0)), pl.BlockSpec((seq, POOL_WIDTH), lambda b: (b, 0))),
        compiler_params=_cparams("parallel"),
    )(u, mix, scale)


def _aug_constants():
    w = N_HEADS * LANES
    rows = jnp.arange(3 * LANES)
    piece, head = rows // LANES, rows % LANES
    cols = jnp.arange(w)
    live = (head < N_HEADS)[:, None]
    pq = (live & (cols[None, :] == (head * LANES + HEAD_DIM + piece)[:, None])).astype(BF16)
    pk = -(live & (cols[None, :] == (head * LANES + HEAD_DIM + 3 + piece)[:, None])).astype(BF16)
    lane = cols % LANES
    oq = ((lane >= HEAD_DIM + 3) & (lane < HEAD_DIM + 6)).astype(F32)[None, :]
    ok = ((lane >= HEAD_DIM) & (lane < HEAD_DIM + 3)).astype(F32)[None, :]
    return pq, pk, oq, ok


def _head_blocks(wt):
    d = wt.shape[1]
    return jnp.pad(wt.reshape(N_HEADS, HEAD_DIM, d), ((0, 0), (0, LANES - HEAD_DIM), (0, 0))).reshape(N_HEADS * LANES, d)


def _attn_prep(h, wq, wk, wv, fcum, tm):
    t, d = h.shape
    tm = min(tm, t)
    rows = min(ROW_CHUNK, tm)
    w = N_HEADS * LANES
    pq, pk, oq, ok = _aug_constants()

    def body(h_ref, wq_ref, wk_ref, wv_ref, f_ref, pq_ref, pk_ref, oq_ref, ok_ref, qa_ref, ka_ref, v_ref):
        for r0 in range(0, tm, rows):
            rs = slice(r0, r0 + rows)
            hh = h_ref[rs, :]
            fs = jnp.concatenate(_split3(f_ref[rs, :]), axis=1)
            q = _dot_nt(hh, wq_ref[...]).astype(BF16).astype(F32) * ATTN_SCALE
            qa_ref[rs, :] = (q + _dot(fs, pq_ref[...]) + oq_ref[...]).astype(BF16)
            k = _dot_nt(hh, wk_ref[...]).astype(BF16).astype(F32)
            ka_ref[rs, :] = (k + _dot(fs, pk_ref[...]) + ok_ref[...]).astype(BF16)
            v_ref[rs, :] = _dot_nt(hh, wv_ref[...]).astype(BF16)

    row = lambda n: pl.BlockSpec((tm, n), lambda i: (i, 0))
    full = lambda a: pl.BlockSpec(a.shape, lambda i: (0, 0))
    return pl.pallas_call(
        body, name="attn_prep",
        out_shape=(jax.ShapeDtypeStruct((t, w), BF16), jax.ShapeDtypeStruct((t, w), BF16),
                   jax.ShapeDtypeStruct((t, ATTN_WIDTH), BF16)),
        grid=(t // tm,),
        in_specs=[row(d), full(wq), full(wk), full(wv), row(LANES), full(pq), full(pk), full(oq), full(ok)],
        out_specs=(row(w), row(w), row(ATTN_WIDTH)),
        compiler_params=_cparams("parallel"),
    )(h, wq, wk, wv, fcum, pq, pk, oq, ok)


def _fold_lanes(x, op):
    out = x[:, :LANES]
    for g in range(1, x.shape[1] // LANES):
        out = op(out, x[:, g * LANES:(g + 1) * LANES])
    return out


def _causal_sweep(i, tile, carry):
    def quad(jj, c):
        for u in range(4):
            c = tile(4 * jj + u, c, False)
        return c

    carry = lax.fori_loop(0, i // 4, quad, carry)
    base = 4 * (i // 4)
    carry = lax.cond(i % 4 >= 2, lambda c: tile(base + 1, tile(base, c, False), False), lambda c: c, carry)
    return lax.cond(i % 2 == 1, lambda c: tile(i, tile(i - 1, c, False), True), lambda c: tile(i, c, True), carry)


def _attn_fwd(qa, ka, v, seq, tq, dep=None):
    t = qa.shape[0]
    nq = seq // tq
    hp_n = N_HEADS // 2
    heads = [slice(e * LANES, (e + 1) * LANES) for e in range(2)]

    def body(q_ref, k_ref, v_ref, o_ref, lse_ref, s_buf):
        i = pl.program_id(2)
        diag_ok = lax.broadcasted_iota(jnp.int32, (tq, tq), 0) >= lax.broadcasted_iota(jnp.int32, (tq, tq), 1)
        qs = [q_ref[:, hl] for hl in heads]

        def sweep1(j, mxs, diagonal):
            r0 = pl.multiple_of(j * tq, tq)
            out = []
            for e, hl in enumerate(heads):
                s = _dot_nt(qs[e], k_ref[pl.ds(r0, tq), hl])
                if diagonal:
                    s = jnp.where(diag_ok, s, NEG_BIG)
                s_buf[e, j] = s
                out.append(jnp.maximum(mxs[e], _fold_lanes(s, jnp.maximum)))
            return tuple(out)

        mxs = _causal_sweep(i, sweep1, (jnp.full((tq, LANES), NEG_BIG, F32),) * 2)
        ms = [jnp.max(mx, axis=1, keepdims=True) for mx in mxs]

        def sweep2(j, carry, diagonal):
            r0 = pl.multiple_of(j * tq, tq)
            vv = v_ref[pl.ds(r0, tq), :]
            out = []
            for e in range(2):
                p = jnp.exp(s_buf[e, j] - ms[e])
                out += [carry[2 * e] + _fold_lanes(p, jnp.add), carry[2 * e + 1] + _dot(p.astype(BF16), vv)]
            return tuple(out)

        res = _causal_sweep(i, sweep2, (jnp.zeros((tq, LANES), F32),) * 4)
        outs = []
        for e in range(2):
            l = jnp.sum(res[2 * e], axis=1, keepdims=True)
            outs.append(res[2 * e + 1] / l)
            lse_ref[:, e:e + 1] = ms[e] + jnp.log(l)
        lane = lax.broadcasted_iota(jnp.int32, (tq, LANES), 1)
        o_ref[...] = jnp.where(lane < HEAD_DIM, outs[0], outs[1])

    dep_specs, dep_ops = _dep_args(dep)
    return pl.pallas_call(
        _after(body, 3, dep), name="attn_fwd",
        out_shape=(jax.ShapeDtypeStruct((t, ATTN_WIDTH), F32), jax.ShapeDtypeStruct((hp_n, t, 2), F32)),
        grid=(t // seq, hp_n, nq),
        in_specs=[pl.BlockSpec((tq, 2 * LANES), lambda b, hp, i: (b * nq + i, hp)),
                  pl.BlockSpec((seq, 2 * LANES), lambda b, hp, i: (b, hp)),
                  pl.BlockSpec((seq, LANES), lambda b, hp, i: (b, hp))] + dep_specs,
        out_specs=(pl.BlockSpec((tq, LANES), lambda b, hp, i: (b * nq + i, hp)),
                   pl.BlockSpec((None, tq, 2), lambda b, hp, i: (hp, b * nq + i, 0))),
        scratch_shapes=[pltpu.VMEM((2, nq, tq, tq), F32)],
        compiler_params=_cparams("parallel", "parallel", "arbitrary"),
    )(qa, ka, v, *dep_ops)


def _merge_fwd(x, ps, o, g2, wpo, wao, wout, tm):
    t, d = x.shape
    tm = min(tm, t)
    rows = min(ROW_CHUNK, tm)

    def body(x_ref, ps_ref, o_ref, gp_ref, ga_ref, wpo_ref, wao_ref, wout_ref, mg_ref, x1_ref):
        for r0 in range(0, tm, rows):
            rs = slice(r0, r0 + rows)
            py = _dot(ps_ref[rs, :], wpo_ref[...])
            ay = _dot(o_ref[rs, :].astype(BF16), wao_ref[...])
            mb = (_sigmoid(gp_ref[rs, :].astype(F32)) * py + _sigmoid(ga_ref[rs, :].astype(F32)) * ay).astype(BF16)
            mg_ref[rs, :] = mb
            x1_ref[rs, :] = x_ref[rs, :] + _dot(mb, wout_ref[...])

    row = lambda w: pl.BlockSpec((tm, w), lambda i: (i, 0))
    full = lambda a: pl.BlockSpec(a.shape, lambda i: (0, 0))
    return pl.pallas_call(
        body, name="merge_fwd",
        out_shape=(jax.ShapeDtypeStruct((t, d), BF16), jax.ShapeDtypeStruct((t, d), F32)),
        grid=(t // tm,),
        in_specs=[row(d), row(POOL_WIDTH), row(ATTN_WIDTH), pl.BlockSpec((tm, d), lambda i: (i, 0)),
                  pl.BlockSpec((tm, d), lambda i: (i, 1)), full(wpo), full(wao), full(wout)],
        out_specs=(row(d), row(d)),
        compiler_params=_cparams("parallel"),
    )(x, ps, o, g2, g2, wpo, wao, wout)


def _ffn_fwd(x1, g, wg, wu, wd, tm, tf):
    t, d = x1.shape
    f = wg.shape[0]
    tm = min(tm, t)
    nf = f // tf
    rows = min(512, tm)

    def body(x1_ref, g_ref, wg_ref, wu_ref, wd_ref, h2_ref, gt_ref, up_ref, act_ref, x2_ref):
        j = pl.program_id(1)

        @pl.when(j == 0)
        def _():
            h2_ref[...] = _rms_fwd(x1_ref[...], g_ref[...]).astype(BF16)

            x2_ref[...] = x1_ref[...]

        for r0 in range(0, tm, rows):
            rs = slice(r0, r0 + rows)
            h2 = h2_ref[rs, :]
            gt = _dot_nt(h2, wg_ref[...])
            up = _dot_nt(h2, wu_ref[...])
            sg = _sigmoid(gt)
            silu = gt * sg
            act = (silu * up).astype(BF16)
            gt_ref[rs, :] = (up * (sg * (1.0 + gt * (1.0 - sg)))).astype(BF16)
            up_ref[rs, :] = silu.astype(BF16)
            act_ref[rs, :] = act
            x2_ref[rs, :] += _dot(act, wd_ref[...])

    return pl.pallas_call(
        body, name="ffn_fwd",
        out_shape=(jax.ShapeDtypeStruct((t, d), BF16), jax.ShapeDtypeStruct((t, f), BF16),
                   jax.ShapeDtypeStruct((t, f), BF16), jax.ShapeDtypeStruct((t, f), BF16),
                   jax.ShapeDtypeStruct((t, d), F32)),
        grid=(t // tm, nf),
        in_specs=[pl.BlockSpec((tm, d), lambda i, j: (i, 0)), pl.BlockSpec((1, d), lambda i, j: (0, 0)),
                  pl.BlockSpec((tf, d), lambda i, j: (j, 0)), pl.BlockSpec((tf, d), lambda i, j: (j, 0)),
                  pl.BlockSpec((tf, d), lambda i, j: (j, 0))],
        out_specs=(pl.BlockSpec((tm, d), lambda i, j: (i, 0)), pl.BlockSpec((tm, tf), lambda i, j: (i, j)),
                   pl.BlockSpec((tm, tf), lambda i, j: (i, j)), pl.BlockSpec((tm, tf), lambda i, j: (i, j)),
                   pl.BlockSpec((tm, d), lambda i, j: (i, 0))),
        compiler_params=_cparams("parallel", "arbitrary"),
    )(x1, g, wg, wu, wd)


def _final_fwd_bwd(x2, target, g, tm):
    t, d = x2.shape
    tm = min(tm, t)

    def body(x_ref, t_ref, g_ref, loss_ref, dx_ref, dg_ref):
        i = pl.program_id(0)
        x = x_ref[...]
        gg = g_ref[...]
        err = _rms_fwd(x, gg) - t_ref[...]
        part = 0.5 * jnp.sum(jnp.mean(err * err, axis=-1, keepdims=True), axis=0, keepdims=True)
        dx, dg = _rms_bwd(x, gg, err * (1.0 / d))
        dx_ref[...] = dx

        @pl.when(i == 0)
        def _():
            loss_ref[...] = jnp.zeros_like(loss_ref)
            dg_ref[...] = jnp.zeros_like(dg_ref)

        loss_ref[...] += jnp.broadcast_to(part, loss_ref.shape)
        dg_ref[...] += dg

    return pl.pallas_call(
        body, name="final_fwd_bwd",
        out_shape=(jax.ShapeDtypeStruct((1, LANES), F32), jax.ShapeDtypeStruct((t, d), F32),
                   jax.ShapeDtypeStruct((1, d), F32)),
        grid=(t // tm,),
        in_specs=[pl.BlockSpec((tm, d), lambda i: (i, 0)), pl.BlockSpec((tm, d), lambda i: (i, 0)),
                  pl.BlockSpec((1, d), lambda i: (0, 0))],
        out_specs=(pl.BlockSpec((1, LANES), lambda i: (0, 0)), pl.BlockSpec((tm, d), lambda i: (i, 0)),
                   pl.BlockSpec((1, d), lambda i: (0, 0))),
        compiler_params=_cparams("arbitrary"),
    )(x2, target, g)


def _ffn_bwd(dx2, x1, g, gt, up, wg, wu, wd, tm, tf):
    t, d = dx2.shape
    f = gt.shape[1]
    tm = min(tm, t)
    nf = f // tf
    wgu = jnp.concatenate([wg.reshape(nf, tf, d), wu.reshape(nf, tf, d)], axis=1).reshape(2 * f, d)
    rows = min(256, tm)

    def body(dx2_ref, x1_ref, g_ref, gt_ref, up_ref, wgu_ref, wd_ref, dgt_ref, dup_ref, dx1_ref, dg_ref, acc_ref,
             dxb_ref):
        i, j = pl.program_id(0), pl.program_id(1)

        @pl.when(j == 0)
        def _():
            dxb_ref[...] = dx2_ref[...].astype(BF16)
            acc_ref[...] = jnp.zeros_like(acc_ref)

        for r0 in range(0, tm, rows):
            rs = slice(r0, r0 + rows)
            dact = _dot_nt(dxb_ref[rs, :], wd_ref[...])
            dgt = (dact * gt_ref[rs, :].astype(F32)).astype(BF16)
            dup = (dact * up_ref[rs, :].astype(F32)).astype(BF16)
            dgt_ref[rs, :] = dgt
            dup_ref[rs, :] = dup
            acc_ref[rs, :] += _dot(jnp.concatenate([dgt, dup], axis=1), wgu_ref[...])

        @pl.when(jnp.logical_and(i == 0, j == 0))
        def _():
            dg_ref[...] = jnp.zeros_like(dg_ref)

        @pl.when(j == nf - 1)
        def _():
            dxn, dg = _rms_bwd(x1_ref[...], g_ref[...], acc_ref[...])
            dx1_ref[...] = dx2_ref[...] + dxn
            dg_ref[...] += dg

    return pl.pallas_call(
        body, name="ffn_bwd",
        out_shape=(jax.ShapeDtypeStruct((t, f), BF16), jax.ShapeDtypeStruct((t, f), BF16),
                   jax.ShapeDtypeStruct((t, d), F32), jax.ShapeDtypeStruct((1, d), F32)),
        grid=(t // tm, nf),
        in_specs=[pl.BlockSpec((tm, d), lambda i, j: (i, 0)), pl.BlockSpec((tm, d), lambda i, j: (i, 0)),
                  pl.BlockSpec((1, d), lambda i, j: (0, 0)),
                  pl.BlockSpec((tm, tf), lambda i, j: (i, j)), pl.BlockSpec((tm, tf), lambda i, j: (i, j)),
                  pl.BlockSpec((2 * tf, d), lambda i, j: (j, 0)), pl.BlockSpec((tf, d), lambda i, j: (j, 0))],
        out_specs=(pl.BlockSpec((tm, tf), lambda i, j: (i, j)), pl.BlockSpec((tm, tf), lambda i, j: (i, j)),
                   pl.BlockSpec((tm, d), lambda i, j: (i, 0)), pl.BlockSpec((1, d), lambda i, j: (0, 0))),
        scratch_shapes=[pltpu.VMEM((tm, d), F32), pltpu.VMEM((tm, d), BF16)],
        compiler_params=_cparams("arbitrary", "arbitrary"),
    )(dx2, x1, g, gt, up, wgu, wd)


def _merge_bwd(dx1, ps, o, g2, wpo, wao, wout, tm, dep=None):
    t, d = dx1.shape
    tm = min(tm, t)
    rows = min(ROW_CHUNK, tm)

    def body(dx1_ref, ps_ref, o_ref, gp_ref, ga_ref, wpo_ref, wao_ref, wout_ref, dpy_ref, day_ref, dg2_ref, dps_ref, da_ref):
        for r0 in range(0, tm, rows):
            rs = slice(r0, r0 + rows)
            dm = _dot_nt(dx1_ref[rs, :].astype(BF16), wout_ref[...])
            py = _dot(ps_ref[rs, :], wpo_ref[...])
            ay = _dot(o_ref[rs, :].astype(BF16), wao_ref[...])
            sp = _sigmoid(gp_ref[rs, :].astype(F32))
            sa = _sigmoid(ga_ref[rs, :].astype(F32))
            dpy = (dm * sp).astype(BF16)
            day = (dm * sa).astype(BF16)
            dpy_ref[rs, :] = dpy
            day_ref[rs, :] = day
            dg2_ref[rs, :d] = (dm * py * (sp * (1.0 - sp))).astype(BF16)
            dg2_ref[rs, d:] = (dm * ay * (sa * (1.0 - sa))).astype(BF16)
            dps_ref[rs, :] = _dot_nt(dpy, wpo_ref[...])
            da_ref[rs, :] = _dot_nt(day, wao_ref[...]).astype(BF16)

    row = lambda w: pl.BlockSpec((tm, w), lambda i: (i, 0))
    full = lambda a: pl.BlockSpec(a.shape, lambda i: (0, 0))
    dep_specs, dep_ops = _dep_args(dep)
    return pl.pallas_call(
        _after(body, 8, dep), name="merge_bwd",
        out_shape=(jax.ShapeDtypeStruct((t, d), BF16), jax.ShapeDtypeStruct((t, d), BF16),
                   jax.ShapeDtypeStruct((t, 2 * d), BF16), jax.ShapeDtypeStruct((t, POOL_WIDTH), F32),
                   jax.ShapeDtypeStruct((t, ATTN_WIDTH), BF16)),
        grid=(t // tm,),
        in_specs=[row(d), row(POOL_WIDTH), row(ATTN_WIDTH), pl.BlockSpec((tm, d), lambda i: (i, 0)),
                  pl.BlockSpec((tm, d), lambda i: (i, 1)), full(wpo), full(wao), full(wout)] + dep_specs,
        out_specs=(row(d), row(d), row(2 * d), row(POOL_WIDTH), row(ATTN_WIDTH)),
        compiler_params=_cparams("parallel"),
    )(dx1, ps, o, g2, g2, wpo, wao, wout, *dep_ops)


def _attn_bwd(qa, ka, v, do, lse4, seq, tq, dep=None):
    t = qa.shape[0]
    nq = seq // tq
    hp_n = N_HEADS // 2
    heads = [slice(e * LANES, (e + 1) * LANES) for e in range(2)]

    def body(q_ref, k_ref, v_ref, do_ref, lse_ref, dq_ref, dk_ref, dv_ref, dfr_ref, dk_acc, dv_acc, p_buf, dp_buf):
        diag_ok = lax.broadcasted_iota(jnp.int32, (tq, tq), 0) >= lax.broadcasted_iota(jnp.int32, (tq, tq), 1)
        lane_q = lax.broadcasted_iota(jnp.int32, (tq, LANES), 1)
        mine_q = [lane_q < HEAD_DIM, lane_q >= HEAD_DIM]
        dv_acc[...] = jnp.zeros_like(dv_acc)
        dk_acc[...] = jnp.zeros_like(dk_acc)
        dfr_ref[...] = jnp.zeros_like(dfr_ref)
        transposed = lambda a: a.astype(F32).T.astype(BF16)

        def q_step(i, _):
            q0 = pl.multiple_of(i * tq, tq)
            qs = [q_ref[pl.ds(q0, tq), hl] for hl in heads]
            dov = do_ref[pl.ds(q0, tq), :]
            dos = [jnp.where(mq, dov, jnp.zeros((), BF16)) for mq in mine_q]
            qts = [transposed(q) for q in qs]
            dots = [transposed(a) for a in dos]
            lss = [lse_ref[pl.ds(q0, tq), e:e + 1] for e in range(2)]

            def sweep1(j, dls, diagonal):
                r0 = pl.multiple_of(j * tq, tq)
                vv = v_ref[pl.ds(r0, tq), :]
                out = []
                for e, hl in enumerate(heads):
                    s = _dot_nt(qs[e], k_ref[pl.ds(r0, tq), hl])
                    if diagonal:
                        s = jnp.where(diag_ok, s, NEG_BIG)
                    p = jnp.exp(s - lss[e])
                    dp = _dot_nt(dos[e], vv)
                    p_buf[e, j] = p
                    dp_buf[e, j] = dp
                    dv_acc[j] += _dot(dots[e], p.astype(BF16))
                    out.append(dls[e] + _fold_lanes(p * dp, jnp.add))
                return tuple(out)

            dls = _causal_sweep(i, sweep1, (jnp.zeros((tq, LANES), F32),) * 2)
            dls = [jnp.sum(d, axis=1, keepdims=True) for d in dls]

            def sweep2(j, dqs, diagonal):
                r0 = pl.multiple_of(j * tq, tq)
                out = []
                for e, hl in enumerate(heads):
                    ds = p_buf[e, j] * (dp_buf[e, j] - dls[e])
                    dfr_ref[e, pl.ds(j, 1), :] += jnp.sum(ds, axis=0, keepdims=True)
                    dsb = ds.astype(BF16)
                    dk_acc[e, j] += _dot(qts[e], dsb)
                    out.append(dqs[e] + _dot(dsb, k_ref[pl.ds(r0, tq), hl]))
                return tuple(out)

            dqs = _causal_sweep(i, sweep2, (jnp.zeros((tq, LANES), F32),) * 2)
            dq = jnp.where(mine_q[0], dqs[0], pltpu.roll(dqs[1], HEAD_DIM, 1)) * ATTN_SCALE
            dq_ref[pl.ds(q0, tq), :] = dq.astype(BF16)
            return 0

        lax.fori_loop(0, nq, q_step, 0)
        for j in range(nq):
            rs = slice(j * tq, (j + 1) * tq)
            dk = jnp.where(mine_q[0], dk_acc[0, j].T, pltpu.roll(dk_acc[1, j].T, HEAD_DIM, 1))
            dk_ref[rs, :] = dk.astype(BF16)
            dv_ref[rs, :] = dv_acc[j].T.astype(BF16)

    wide = pl.BlockSpec((seq, 2 * LANES), lambda b, hp: (b, hp))
    col = pl.BlockSpec((seq, LANES), lambda b, hp: (b, hp))
    pair = pl.BlockSpec((None, seq, 2), lambda b, hp: (hp, b, 0))
    dep_specs, dep_ops = _dep_args(dep)
    return pl.pallas_call(
        _after(body, 5, dep), name="attn_bwd",
        out_shape=(jax.ShapeDtypeStruct((t, ATTN_WIDTH), BF16),) * 3 + (jax.ShapeDtypeStruct((N_HEADS, t // tq, tq), F32),),
        grid=(t // seq, hp_n),
        in_specs=[wide, wide, col, col, pair] + dep_specs,
        out_specs=(col, col, col, pl.BlockSpec((2, nq, tq), lambda b, hp: (hp, b, 0))),
        scratch_shapes=[pltpu.VMEM((2, nq, LANES, tq), F32), pltpu.VMEM((nq, LANES, tq), F32),
                        pltpu.VMEM((2, nq, tq, tq), F32), pltpu.VMEM((2, nq, tq, tq), F32)],
        compiler_params=_cparams("parallel", "arbitrary"),
    )(qa, ka, v, do, lse4, *dep_ops)


def _forget_bwd(dfc, fl, bf, seq):
    t = fl.shape[0]
    cb = min(256, seq)
    nb = seq // cb

    def body(dfc_ref, fl_ref, bf_ref, dfl_ref, db_ref):
        b = pl.program_id(0)
        ri = lax.broadcasted_iota(jnp.int32, (cb, cb), 0)
        ci = lax.broadcasted_iota(jnp.int32, (cb, cb), 1)
        tri = (ci >= ri).astype(BF16)
        carry = jnp.zeros((1, LANES), F32)
        dbs = jnp.zeros((1, LANES), F32)
        for blk in reversed(range(nb)):
            rs = slice(blk * cb, (blk + 1) * cb)
            dlf = _tri_dot(tri, -dfc_ref[rs, :]) + carry
            carry = dlf[0:1, :]
            dfl = dlf * _sigmoid(-(fl_ref[rs, :] + bf_ref[...]))
            dfl_ref[rs, :] = dfl.astype(BF16)
            dbs = dbs + jnp.sum(dfl, axis=0, keepdims=True)

        @pl.when(b == 0)
        def _():
            db_ref[...] = jnp.zeros_like(db_ref)

        db_ref[...] += dbs

    return pl.pallas_call(
        body, name="forget_bwd",
        out_shape=(jax.ShapeDtypeStruct((t, LANES), BF16), jax.ShapeDtypeStruct((1, LANES), F32)),
        grid=(t // seq,),
        in_specs=[pl.BlockSpec((seq, LANES), lambda b: (b, 0)), pl.BlockSpec((seq, LANES), lambda b: (b, 0)),
                  pl.BlockSpec((1, LANES), lambda b: (0, 0))],
        out_specs=(pl.BlockSpec((seq, LANES), lambda b: (b, 0)), pl.BlockSpec((1, LANES), lambda b: (0, 0))),
        compiler_params=_cparams("arbitrary"),
    )(dfc, fl, bf)


def _pool_bwd(dps, p, mix, scale, seq):
    t = dps.shape[0]

    def body(dps_ref, p_ref, mix_ref, sc_ref, du_ref, dmix_ref, dsc_ref):
        b = pl.program_id(0)

        @pl.when(b == 0)
        def _():
            dmix_ref[...] = jnp.zeros_like(dmix_ref)
            dsc_ref[...] = jnp.zeros_like(dsc_ref)

        tpos = lax.broadcasted_iota(jnp.int32, (seq, POOL_GROUP_DIM), 0)
        for g in range(POOL_GROUPS):
            sl = slice(g * POOL_GROUP_DIM, (g + 1) * POOL_GROUP_DIM)
            pb = p_ref[:, sl]
            dpsg = dps_ref[:, sl]
            pm = _dot(pb, mix_ref[g])
            dsc_ref[:, sl] += jnp.sum(dpsg * pm, axis=0, keepdims=True)
            dpm = (dpsg * sc_ref[:, sl]).astype(BF16)
            dmix_ref[g] += _dot_tn(pb, dpm)
            dp = _dot_nt(dpm, mix_ref[g])
            cnt = jnp.minimum(tpos + 1, POOL_WINDOWS[g]).astype(F32)
            s = dp / cnt
            for lvl in range(g + 1):
                d = 2 ** lvl
                s = s + jnp.where(tpos < seq - d, pltpu.roll(s, seq - d, 0), 0.0)
            du_ref[:, sl] = (s - dp).astype(BF16)

    return pl.pallas_call(
        body, name="pool_bwd",
        out_shape=(jax.ShapeDtypeStruct((t, POOL_WIDTH), BF16),
                   jax.ShapeDtypeStruct((POOL_GROUPS, POOL_GROUP_DIM, POOL_GROUP_DIM), F32),
                   jax.ShapeDtypeStruct((1, POOL_WIDTH), F32)),
        grid=(t // seq,),
        in_specs=[pl.BlockSpec((seq, POOL_WIDTH), lambda b: (b, 0)), pl.BlockSpec((seq, POOL_WIDTH), lambda b: (b, 0)),
                  pl.BlockSpec((POOL_GROUPS, POOL_GROUP_DIM, POOL_GROUP_DIM), lambda b: (0, 0, 0)),
                  pl.BlockSpec((1, POOL_WIDTH), lambda b: (0, 0))],
        out_specs=(pl.BlockSpec((seq, POOL_WIDTH), lambda b: (b, 0)),
                   pl.BlockSpec((POOL_GROUPS, POOL_GROUP_DIM, POOL_GROUP_DIM), lambda b: (0, 0, 0)),
                   pl.BlockSpec((1, POOL_WIDTH), lambda b: (0, 0))),
        compiler_params=_cparams("arbitrary"),
    )(dps, p, mix, scale)


def _in_bwd(du, dq, dk, dv, dg2, dfl, dx1, x, g, wu, wqkv, wg2, wft, tm):
    t, d = x.shape
    tm = min(tm, t)
    rows = min(ROW_CHUNK, tm)
    aw = ATTN_WIDTH

    def body(du_ref, dq_ref, dk_ref, dv_ref, dg2_ref, dfl_ref, dx1_ref, x_ref, g_ref, wu_ref, wqkv_ref, wg2_ref, wft_ref,
             dx_ref, dg_ref):
        i = pl.program_id(0)

        @pl.when(i == 0)
        def _():
            dg_ref[...] = jnp.zeros_like(dg_ref)

        for r0 in range(0, tm, rows):
            rs = slice(r0, r0 + rows)
            dh = _dot(du_ref[rs, :], wu_ref[...])
            dh += _dot(dq_ref[rs, :], wqkv_ref[0:aw, :])
            dh += _dot(dk_ref[rs, :], wqkv_ref[aw:2 * aw, :])
            dh += _dot(dv_ref[rs, :], wqkv_ref[2 * aw:3 * aw, :])
            dh += _dot(dg2_ref[rs, :], wg2_ref[...])
            dh += _dot(dfl_ref[rs, :], wft_ref[...])
            dxn, dg = _rms_bwd(x_ref[rs, :], g_ref[...], dh)
            dx_ref[rs, :] = dx1_ref[rs, :] + dxn
            dg_ref[...] += dg

    row = lambda w: pl.BlockSpec((tm, w), lambda i: (i, 0))
    full = lambda a: pl.BlockSpec(a.shape, lambda i: (0, 0))
    return pl.pallas_call(
        body, name="in_bwd",
        out_shape=(jax.ShapeDtypeStruct((t, d), F32), jax.ShapeDtypeStruct((1, d), F32)),
        grid=(t // tm,),
        in_specs=[row(POOL_WIDTH), row(aw), row(aw), row(aw), row(2 * d), row(LANES), row(d), row(d),
                  pl.BlockSpec((1, d), lambda i: (0, 0)), full(wu), full(wqkv), full(wg2), full(wft)],
        out_specs=(row(d), pl.BlockSpec((1, d), lambda i: (0, 0))),
        compiler_params=_cparams("arbitrary"),
    )(du, dq, dk, dv, dg2, dfl, dx1, x, g, wu, wqkv, wg2, wft)


def _position():
    return lax.axis_index("x"), lax.axis_index("y"), lax.axis_index("c")


def _remote(src, dst, send_sem, recv_sem, device):
    return pltpu.make_async_remote_copy(src_ref=src, dst_ref=dst, send_sem=send_sem, recv_sem=recv_sem,
                                        device_id=device, device_id_type=MESH)


HBM = pl.BlockSpec(memory_space=pltpu.HBM)
SEM = pl.BlockSpec(memory_space=pltpu.SEMAPHORE)
DATAFLOW = pltpu.SideEffectType.DATAFLOW_SIDE_EFFECTING


def _copies_start(name, arrays, plan, m, dep=None):
    n = len(arrays)
    arrays = [pltpu.with_memory_space_constraint(a, pltpu.HBM) for a in arrays]

    def body(*refs):
        ins, send_sem, recv_sem, token = refs[:n], refs[n], refs[n + 1], refs[2 * n + 2]
        for i, (src, dst, device, _) in enumerate(plan(ins, *_position())):
            _remote(src, dst, send_sem.at[i], recv_sem.at[i], device).start()
        token[...] = jnp.zeros_like(token)

    dep_specs, dep_ops = _dep_args(dep)
    outs = pl.pallas_call(
        _after(body, n, dep), name=name,
        out_shape=(pltpu.SemaphoreType.DMA((m,)), pltpu.SemaphoreType.DMA((m,)),
                   *[pltpu.HBM(a.shape, a.dtype) for a in arrays], jax.ShapeDtypeStruct((8, LANES), F32)),
        in_specs=[HBM] * n + dep_specs, out_specs=(SEM, SEM, *[HBM] * n, pl.BlockSpec(memory_space=pltpu.VMEM)),
        input_output_aliases={i: i + 2 for i in range(n)},
        compiler_params=pltpu.CompilerParams(has_side_effects=DATAFLOW),
    )(*arrays, *dep_ops)
    return (outs[0], outs[1]), list(outs[2:2 + n]), outs[2 + n]


def _copies_wait(name, sems, arrays, plan, after):
    n = len(arrays)
    afters = list(after) if isinstance(after, (list, tuple)) else [after]

    def body(*refs):
        ins, send_sem, recv_sem = refs[:n], refs[n], refs[n + 1]
        for i, (src, dst, device, landing) in enumerate(plan(ins, *_position())):
            _remote(src, dst, send_sem.at[i], recv_sem.at[i], device).wait_send()
            _remote(landing, landing, send_sem.at[i], recv_sem.at[i], device).wait_recv()

    outs = pl.pallas_call(
        body, name=name,
        out_shape=tuple(pltpu.HBM(a.shape, a.dtype) for a in arrays),
        in_specs=[HBM] * n + [SEM, SEM] + [ANY] * len(afters), out_specs=tuple([HBM] * n),
        input_output_aliases={i: i for i in range(n)},
        compiler_params=pltpu.CompilerParams(has_side_effects=DATAFLOW),
    )(*arrays, sems[0], sems[1], *afters)
    return list(outs)


def _tie(x, dep):
    for token in _dep_list(dep):
        x = x + token[0, 0]
    return x


def _other_chips(x, y):
    return [(1 - x, y), (x, 1 - y), (1 - x, 1 - y)]


def _gather_begin(tag, shards, token, column_halves=False):
    n = len(shards)
    lands = [lax.empty((N_CHIPS,) + s.shape, s.dtype) for s in shards]
    if column_halves:
        cols = lambda ref, h: pl.ds(pl.multiple_of(h * (ref.shape[-1] // 2), LANES), ref.shape[-1] // 2)
        mine = lambda ref, h: ref.at[:, cols(ref, h)]
        landed = lambda ref, chip, h: ref.at[chip, :, cols(ref, h)]
    else:
        mine = lambda ref, h: ref.at[h]
        landed = lambda ref, chip, h: ref.at[chip, h]

    def plan(refs, x, y, c):
        return [(mine(refs[k], c), landed(refs[n + k], 2 * x + y, c), (ox, oy, c), landed(refs[n + k], 2 * ox + oy, c))
                for k in range(n) for ox, oy in _other_chips(x, y)]

    sems, thru, token = _copies_start(f"gather_{tag}_ici_start", list(shards) + lands, plan, 3 * n, dep=token)
    return dict(tag=tag, n=n, plan=plan, sems=sems, arrays=thru, token=token, landed=landed)


def _gather_forward(st, after):
    n, tag, landed = st["n"], st["tag"], st["landed"]
    thru = _copies_wait(f"gather_{tag}_ici_wait", st["sems"], st["arrays"], st["plan"], after)

    def plan(refs, x, y, c):
        return [(landed(refs[k], 2 * ox + oy, c), landed(refs[k], 2 * ox + oy, c), (x, y, 1 - c),
                 landed(refs[k], 2 * ox + oy, 1 - c))
                for k in range(n) for ox, oy in _other_chips(x, y)]

    sems, lands, token = _copies_start(f"gather_{tag}_fwd_start", thru[n:], plan, 3 * n)
    return dict(tag=tag, n=n, plan=plan, sems=sems, arrays=lands, token=token, shards=thru[:n])


def _gather_end(st, after, merge=True):
    lands = _copies_wait(f"gather_{st['tag']}_fwd_wait", st["sems"], st["arrays"], st["plan"], after)
    if not merge:
        return lands, st["shards"]
    me = 2 * lax.axis_index("x") + lax.axis_index("y")
    return [lax.dynamic_update_index_in_dim(g, s, me, 0) for g, s in zip(lands, st["shards"])]


def _add_keep_give(name, pos, a, a_keep, a_give, b, b_keep, b_give, steps):
    r, c = b.shape[-2:]

    def spec(arr, fn):
        lead = arr.ndim - 2

        def index(i, p):
            idx = tuple(fn(i, p))
            return idx if len(idx) == arr.ndim else idx + (0, 0)

        return pl.BlockSpec((None,) * lead + (r, c), index)

    out_spec = pl.BlockSpec((None, r, c), lambda i, p: (i, 0, 0))

    def body(p_ref, ak_ref, bk_ref, ag_ref, bg_ref, keep_ref, give_ref):
        keep_ref[...] = ak_ref[...] + bk_ref[...].astype(F32)
        give_ref[...] = (ag_ref[...] + bg_ref[...].astype(F32)).astype(BF16)

    return pl.pallas_call(
        body, name=name,
        out_shape=(jax.ShapeDtypeStruct((steps, r, c), F32), jax.ShapeDtypeStruct((steps, r, c), BF16)),
        grid_spec=pltpu.PrefetchScalarGridSpec(
            num_scalar_prefetch=1, grid=(steps,),
            in_specs=[spec(a, a_keep), spec(b, b_keep), spec(a, a_give), spec(b, b_give)],
            out_specs=(out_spec, out_spec)),
        compiler_params=_cparams("parallel"),
    )(pos, a, b, a, b)


def _add_keep_give_group(name, pos, items, steps, split=2):
    n = len(items)

    def spec(arr, fn, rows, c):
        return pl.BlockSpec((None,) * (arr.ndim - 2) + (rows, c), lambda i, j, p: tuple(fn(i, p)) + (j, 0))

    in_specs, out_specs, out_shape, operands = [], [], [], []
    for a, a_keep, a_give, b, b_keep, b_give in items:
        r, c = b.shape[-2:]
        rows = r // split
        assert r % split == 0 and rows % 16 == 0, (r, split)
        in_specs += [spec(a, a_keep, rows, c), spec(b, b_keep, rows, c), spec(a, a_give, rows, c),
                     spec(b, b_give, rows, c)]
        out_specs += [pl.BlockSpec((None, rows, c), lambda i, j, p: (i, j, 0))] * 2
        out_shape += [jax.ShapeDtypeStruct((steps, r, c), F32), jax.ShapeDtypeStruct((steps, r, c), BF16)]
        operands += [a, b, a, b]

    def body(p_ref, *refs):
        ins, outs = refs[:4 * n], refs[4 * n:]
        for k in range(n):
            ak_ref, bk_ref, ag_ref, bg_ref = ins[4 * k:4 * k + 4]
            outs[2 * k][...] = ak_ref[...] + bk_ref[...].astype(F32)
            outs[2 * k + 1][...] = (ag_ref[...] + bg_ref[...].astype(F32)).astype(BF16)

    outs = pl.pallas_call(
        body, name=name, out_shape=tuple(out_shape),
        grid_spec=pltpu.PrefetchScalarGridSpec(
            num_scalar_prefetch=1, grid=(steps, split), in_specs=in_specs, out_specs=tuple(out_specs)),
        compiler_params=_cparams("parallel", "parallel"),
    )(pos, *operands)
    return [(outs[2 * k], outs[2 * k + 1]) for k in range(n)]


def _add_last(name, a, b):
    _, r, c = a.shape
    blk = pl.BlockSpec((None, r, c), lambda i: (0, 0, 0))

    def body(a_ref, b_ref, o_ref):
        o_ref[...] = a_ref[...] + b_ref[...].astype(F32)

    return pl.pallas_call(
        body, name=name, out_shape=jax.ShapeDtypeStruct((r, c), F32), grid=(1,), in_specs=[blk, blk],
        out_specs=pl.BlockSpec((r, c), lambda i: (0, 0)), compiler_params=_cparams("arbitrary"),
    )(a, b)


def _exchange_part(gives, lands, peer_fn):
    n = len(gives)

    def plan(refs, x, y, c):
        return [(refs[k], refs[n + k], peer_fn(x, y, c), refs[n + k]) for k in range(n)]

    return list(gives) + list(lands), plan, n


def _join_parts(parts):
    offsets, total = [], 0
    for arrays, _, _ in parts:
        offsets.append(total)
        total += len(arrays)

    def plan(refs, x, y, c):
        copies = []
        for (arrays, part_plan, _), off in zip(parts, offsets):
            copies += part_plan(refs[off:off + len(arrays)], x, y, c)
        return copies

    return [a for arrays, _, _ in parts for a in arrays], plan, sum(m for _, _, m in parts)


def _reduce_begin(tag, grads, column_halves=False):
    n = len(grads)
    if column_halves:
        half = lambda ref, j, h: ref.at[j, :, pl.ds(pl.multiple_of(h * (ref.shape[2] // 2), LANES), ref.shape[2] // 2)]
        lands = [lax.empty((N_CHIPS, g.shape[1], g.shape[2] // 2), F32) for g in grads]
    else:
        half = lambda ref, j, h: ref.at[j, h]
        lands = [lax.empty((N_CHIPS,) + g.shape[2:], F32) for g in grads]

    def plan(refs, x, y, c):
        return [(half(refs[k], j, 1 - c), refs[n + k].at[j], (x, y, 1 - c), refs[n + k].at[j])
                for k in range(n) for j in range(N_CHIPS)]

    return dict(tag=tag, n=n, stage="c", grads=list(grads), column_halves=column_halves,
                part=(list(grads) + lands, plan, N_CHIPS * n))


def _reduce_next(st, thru):
    tag, n, stage = st["tag"], st["n"], st["stage"]
    first, recv = thru[:n], thru[n:]
    x, y, c = _position()
    if stage == "c":
        pos = jnp.stack([c, x]).astype(jnp.int32)
        if st["column_halves"]:
            mine = lambda chip: (lambda i, p: (chip(p) + i, 0, p[0]))
        else:
            mine = lambda chip: (lambda i, p: (chip(p) + i, p[0]))
        items = [(first[k], mine(lambda p: 2 * p[1]), mine(lambda p: 2 * (1 - p[1])),
                  recv[k], lambda i, p: (2 * p[1] + i,), lambda i, p: (2 * (1 - p[1]) + i,)) for k in range(n)]
        if n > 1:
            sums = _add_keep_give_group(f"rs{tag}_c_add", pos, items, 2)
        else:
            sums = [_add_keep_give(f"rs{tag}_c_add{k}", pos, *item, 2) for k, item in enumerate(items)]
        lands = [lax.empty(s[1].shape, BF16) for s in sums]
        return dict(tag=tag, n=n, stage="x", keep=[s[0] for s in sums],
                    part=_exchange_part([s[1] for s in sums], lands, lambda x, y, c: (1 - x, y, c)))
    if stage == "x":
        pos = jnp.stack([y]).astype(jnp.int32)
        items = [(st["keep"][k], lambda i, p: (p[0],), lambda i, p: (1 - p[0],),
                  recv[k], lambda i, p: (p[0],), lambda i, p: (1 - p[0],)) for k in range(n)]
        if n > 1:
            sums = _add_keep_give_group(f"rs{tag}_x_add", pos, items, 1)
        else:
            sums = [_add_keep_give(f"rs{tag}_x_add{k}", pos, *item, 1) for k, item in enumerate(items)]
        lands = [lax.empty(s[1].shape, BF16) for s in sums]
        return dict(tag=tag, n=n, stage="y", keep=[s[0] for s in sums],
                    part=_exchange_part([s[1] for s in sums], lands, lambda x, y, c: (x, 1 - y, c)))
    if stage == "y":
        mine = [_add_last(f"rs{tag}_y_add{k}", st["keep"][k], recv[k]) for k in range(n)]
        lands = [lax.empty(m.shape, F32) for m in mine]
        return dict(tag=tag, n=n, stage="swap", part=_exchange_part(mine, lands, lambda x, y, c: (x, y, 1 - c)))
    return dict(tag=tag, done=list(zip(first, recv)))


def _small_begin(tag, v):
    land = lax.empty((N_DEV,) + v.shape, F32)
    flips = [(fx, fy, fc) for fx in (0, 1) for fy in (0, 1) for fc in (0, 1)][1:]

    def plan(refs, x, y, c):
        copies = []
        for fx, fy, fc in flips:
            px, py, pc = (1 - x if fx else x), (1 - y if fy else y), (1 - c if fc else c)
            copies.append((refs[0], refs[1].at[4 * x + 2 * y + c], (px, py, pc), refs[1].at[4 * px + 2 * py + pc]))
        return copies

    return dict(tag=tag, n=1, stage="swap", grads=[v], part=([v, land], plan, len(flips)))


def _small_sum(name, own, land):
    x, y, c = _position()
    me = jnp.stack([4 * x + 2 * y + c]).astype(jnp.int32)

    def body(me_ref, own_ref, land_ref, out_ref):
        term = lambda dev: jnp.where(me_ref[0] == dev, own_ref[...], land_ref[dev])
        acc = term(0)
        for dev in range(1, N_DEV):
            acc = acc + term(dev)
        out_ref[...] = acc

    return pl.pallas_call(
        body, name=name, out_shape=jax.ShapeDtypeStruct(own.shape, F32),
        grid_spec=pltpu.PrefetchScalarGridSpec(
            num_scalar_prefetch=1, grid=(1,),
            in_specs=[pl.BlockSpec(own.shape, lambda i, m: (0, 0)), pl.BlockSpec(land.shape, lambda i, m: (0, 0, 0))],
            out_specs=pl.BlockSpec(own.shape, lambda i, m: (0, 0))),
        compiler_params=_cparams("arbitrary"),
    )(me, own, land)


def _adamw_update(w, gg, m, v):
    mn = ADAM_B1 * m + (1.0 - ADAM_B1) * gg
    vn = ADAM_B2 * v + (1.0 - ADAM_B2) * (gg * gg)
    m_hat = mn / (1.0 - ADAM_B1 ** ADAM_STEP)
    v_hat = vn / (1.0 - ADAM_B2 ** ADAM_STEP)
    return -ADAM_LR * (m_hat / (jnp.sqrt(v_hat) + ADAM_EPS) + ADAM_WD * w), mn, vn


def _adamw(name, w, g, m, v):
    def body(w_ref, g_ref, m_ref, v_ref, d_ref, mo_ref, vo_ref):
        d_ref[...], mo_ref[...], vo_ref[...] = _adamw_update(w_ref[...], g_ref[...], m_ref[...], v_ref[...])

    blk = pl.BlockSpec(w.shape, lambda i: (0, 0))
    return pl.pallas_call(
        body, name=name, out_shape=(jax.ShapeDtypeStruct(w.shape, F32),) * 3, grid=(1,),
        in_specs=[blk] * 4, out_specs=(blk,) * 3, compiler_params=_cparams("arbitrary"),
    )(w, g, m, v)


def _rows_to_bf16(name, w):
    r, _, c = w.shape

    def body(w_ref, o_ref):
        o_ref[...] = w_ref[:, 0, :].astype(BF16)

    return pl.pallas_call(
        body, name=name, out_shape=jax.ShapeDtypeStruct((r, c), BF16), grid=(1,),
        in_specs=[pl.BlockSpec((r, 1, c), lambda i: (0, 0, 0))], out_specs=pl.BlockSpec((r, c), lambda i: (0, 0)),
        compiler_params=_cparams("arbitrary"),
    )(w)


def _adamw_rows(name, pos_c, w, g_mine, g_other, m, v):
    r, _, c = w.shape
    ch = c // 2

    def body(p_ref, w_ref, gm_ref, go_ref, m_ref, v_ref, g_ref, d_ref, mo_ref, vo_ref):
        gg = jnp.where(pl.program_id(0) == p_ref[0], gm_ref[...], go_ref[...])
        dl, mn, vn = _adamw_update(w_ref[:, 0, :], gg, m_ref[:, 0, :], v_ref[:, 0, :])
        g_ref[:, 0, :] = gg
        d_ref[:, 0, :] = dl
        mo_ref[:, 0, :] = mn
        vo_ref[:, 0, :] = vn

    rows = pl.BlockSpec((r, 1, ch), lambda h, p: (0, 0, h))
    half = pl.BlockSpec((r, ch), lambda h, p: (0, 0))
    return pl.pallas_call(
        body, name=name, out_shape=(jax.ShapeDtypeStruct(w.shape, F32),) * 4,
        grid_spec=pltpu.PrefetchScalarGridSpec(
            num_scalar_prefetch=1, grid=(2,), in_specs=[rows, half, half, rows, rows], out_specs=(rows,) * 4),
        compiler_params=_cparams("parallel"),
    )(pos_c, w, g_mine, g_other, m, v)


def _adamw_halves(name, pos_c, w, g_mine, g_other, m, v, tr, dep=None):
    r, c = w.shape
    rh = r // 2
    tr = tr if rh % tr == 0 else rh
    nt = rh // tr

    def body(p_ref, w_ref, gm_ref, go_ref, m_ref, v_ref, g_ref, d_ref, mo_ref, vo_ref):
        gg = jnp.where(pl.program_id(0) == p_ref[0], gm_ref[...], go_ref[...])
        g_ref[...] = gg
        d_ref[...], mo_ref[...], vo_ref[...] = _adamw_update(w_ref[...], gg, m_ref[...], v_ref[...])

    full = pl.BlockSpec((tr, c), lambda h, i, p: (h * nt + i, 0))
    half = pl.BlockSpec((tr, c), lambda h, i, p: (i, 0))
    dep_specs, dep_ops = _dep_args(dep)
    return pl.pallas_call(
        _after(body, 6, dep), name=name, out_shape=(jax.ShapeDtypeStruct((r, c), F32),) * 4,
        grid_spec=pltpu.PrefetchScalarGridSpec(
            num_scalar_prefetch=1, grid=(2, nt),
            in_specs=[full, half, half, full, full] + dep_specs, out_specs=(full,) * 4),
        compiler_params=_cparams("parallel", "parallel"),
    )(pos_c, w, g_mine, g_other, m, v, *dep_ops)


def _col_sharded_to_comm(g):
    k, n = g.shape
    return g.reshape(2, k // 2, N_CHIPS, n // N_CHIPS).transpose(2, 0, 1, 3)


def _row_sharded_to_comm(g):
    r, c = g.shape
    return g.reshape(N_CHIPS, 2, r // (2 * N_CHIPS), c)


def _col_sharded_full(g):
    _, _, rh, c = g.shape
    return g.reshape(N_CHIPS, 2 * rh, c).transpose(1, 0, 2).reshape(2 * rh, N_CHIPS * c)


def _row_sharded_full(g):
    _, _, rh, c = g.shape
    return g.reshape(N_CHIPS * 2 * rh, c)


def _chip_rows(w3, start, stop, own=None, me=None):
    r = w3.shape[1]
    parts = []
    for chip in range(N_CHIPS):
        lo, hi = max(start - chip * r, 0), min(stop - chip * r, r)
        if lo < hi:
            part = w3[chip, lo:hi]
            parts.append(part if own is None else jnp.where(me == chip, own[lo:hi], part))
    return parts[0] if len(parts) == 1 else jnp.concatenate(parts, axis=0)


def _pack_small(g1, bfv, mix, scale, g2n, gf, extra=None):
    row8 = jnp.pad(bfv.reshape(1, N_HEADS), ((0, 0), (0, LANES - N_HEADS)))
    if extra is not None:
        row8 = row8 + jnp.pad(extra[:, :1], ((0, 0), (N_HEADS, LANES - N_HEADS - 1)))
    return jnp.concatenate([
        g1.reshape(8, LANES), jnp.pad(row8, ((0, 7), (0, 0))), mix.reshape(512, LANES),
        jnp.pad(scale.reshape(4, LANES), ((0, 4), (0, 0))), g2n.reshape(8, LANES), gf.reshape(8, LANES)], axis=0)


def _unpack_small(s, like):
    g1, bfv, mix, scale, g2n, gf = like
    return (s[0:8].reshape(g1.shape), s[8, :N_HEADS].reshape(bfv.shape), s[16:528].reshape(mix.shape),
            s[528:532].reshape(scale.shape), s[536:544].reshape(g2n.shape), s[544:552].reshape(gf.shape))


class _MeshLinks:
    def __init__(self, shards_in, shards_rest):
        self.gin = _gather_begin("in", shards_in, None, column_halves=True)
        self.grest = _gather_begin("rest", shards_rest, self.gin["token"])
        self.tokens = {"gather": self.grest["token"]}
        self.groups, self.flight, self.slot = {}, None, 0

    @property
    def token(self):
        return list(self.tokens.values())

    def tie(self, x):
        return _tie(x, self.token)

    def weights_in(self, after):
        st = _gather_forward(self.gin, after)
        (g,), (own,) = _gather_end(st, st["token"], merge=False)
        return g, own, 2 * lax.axis_index("x") + lax.axis_index("y")

    def rest_forward(self, after):
        self.grest = _gather_forward(self.grest, after)
        self.tokens["gather"] = self.grest["token"]

    def weights_rest(self, after):
        g = _gather_end(self.grest, after)
        del self.tokens["gather"]
        return [_col_sharded_full(g[0]), _col_sharded_full(g[1])] + [_row_sharded_full(a) for a in g[2:]]

    def advance(self, after, begin=()):
        slot = self.slot
        self.slot += 1
        if self.flight is not None:
            tags, sems, parts = self.flight
            arrays, plan, _ = _join_parts(parts)
            thru = _copies_wait(f"slot{slot}_wait", sems, arrays, plan, after)
            for tag, part in zip(tags, parts):
                self.groups[tag] = _reduce_next(self.groups[tag], thru[:len(part[0])])
                thru = thru[len(part[0]):]
        for st in begin:
            self.groups[st["tag"]] = st
        live = [(tag, st["part"]) for tag, st in self.groups.items() if "part" in st]
        self.flight = None
        self.tokens.pop("reduce", None)
        if live:
            arrays, plan, m = _join_parts([part for _, part in live])
            sems, thru, token = _copies_start(f"slot{slot}_start", arrays, plan, m)
            parts = []
            for _, (part_arrays, part_plan, part_m) in live:
                parts.append((thru[:len(part_arrays)], part_plan, part_m))
                thru = thru[len(part_arrays):]
            self.flight = ([tag for tag, _ in live], sems, parts)
            self.tokens["reduce"] = token

    def reduced(self, tag):
        return self.groups[tag]["done"]


class _NoLinks:
    token = None

    def __init__(self, w_in, rest):
        self.w_in, self.rest, self.grads = w_in, rest, {}

    def tie(self, x):
        return x

    def weights_in(self, after):
        return self.w_in, None, None

    def rest_forward(self, after):
        pass

    def weights_rest(self, after):
        return self.rest

    def advance(self, after, begin=()):
        for st in begin:
            self.grads[st["tag"]] = st["grads"]


def _local_step(links, x, target, seq, norm1_g, b_forget, pool_mix, pool_scale, norm2_g, norm_f_g, between=None):
    t, d = x.shape
    tq = min(256, seq)
    aw = ATTN_WIDTH
    o_q, o_f, o_g = POOL_WIDTH, POOL_WIDTH + 3 * aw, POOL_WIDTH + 3 * aw + N_HEADS
    bf = jnp.pad(b_forget, ((0, 0), (0, LANES - N_HEADS)))
    mixb = pool_mix.astype(BF16)

    h = _norm_fwd("norm1_fwd", x, links.tie(norm1_g), 512)
    w_in, own, me = links.weights_in(h)
    wu = _chip_rows(w_in, 0, o_q, own, me)
    wqkv = _chip_rows(w_in, o_q, o_f, own, me)
    wft = jnp.pad(_chip_rows(w_in, o_f, o_g, own, me), ((0, LANES - N_HEADS), (0, 0)))
    wg2 = _chip_rows(w_in, o_g, N_CHIPS * w_in.shape[1], own, me)
    wf = wft.T
    u = _matmul("mm_u", h, wu, "nt", F32, 1024, 512, d)
    g2 = _matmul("mm_gates", h, wg2, "nt", BF16, 1024, 1024, d)
    fl, fcum = _forget_fwd(h, wf, bf, seq)
    qa, ka, v = _attn_prep(h, _head_blocks(wqkv[:aw]), _head_blocks(wqkv[aw:2 * aw]), wqkv[2 * aw:], fcum, 1024)
    p, ps = _pool_fwd(u, mixb, pool_scale, seq)
    links.rest_forward([ps, qa, g2])
    o, lse = _attn_fwd(qa, ka, v, seq, tq, dep=links.token)
    w_pool_out, w_attn_out, w_out, w_ffn_gate, w_ffn_up, w_ffn_down = links.weights_rest(o)
    merged, x1 = _merge_fwd(x, ps, o, g2, w_pool_out, w_attn_out, w_out, 512)
    h2, gt, up, act, x2 = _ffn_fwd(x1, norm2_g, w_ffn_gate, w_ffn_up, w_ffn_down, 1024, 256)
    loss, dx2, d_gf = _final_fwd_bwd(x2, target, norm_f_g, 512)

    dgt, dup, dx1, d_g2n = _ffn_bwd(dx2, x1, norm2_g, gt, up, w_ffn_gate, w_ffn_up, w_ffn_down, 1024, 256)
    d_wd = _matmul("dw_down", act, dx2, "tn", F32, 1408, 1024, 1024)
    d_wg = _matmul("dw_gate", dgt, h2, "tn", F32, 1408, 1024, 1024)
    d_wu = _matmul("dw_up", dup, h2, "tn", F32, 1408, 1024, 1024)
    links.advance(None, begin=[_reduce_begin("a", [_row_sharded_to_comm(g) for g in (d_wg, d_wu, d_wd)])])
    dpy, day, dg2, dps, da = _merge_bwd(dx1, ps, o, g2, w_pool_out, w_attn_out, w_out, 512, dep=links.token)
    links.advance(dps)
    d_wout = _matmul("dw_out", merged, dx1, "tn", F32, 1024, 1024, 1024)
    d_wpo = _matmul("dw_pool_out", ps, dpy, "tn", F32, 512, 1024, 1024)
    d_wao = _matmul("dw_attn_out", o, day, "tn", F32, 512, 1024, 1024)
    dq, dk, dv, dfr = _attn_bwd(qa, ka, v, da, lse, seq, tq, dep=links.token)
    links.advance(dq, begin=[_reduce_begin(
        "m", [_col_sharded_to_comm(d_wpo), _col_sharded_to_comm(d_wao), _row_sharded_to_comm(d_wout)])])
    dfc = jnp.pad(dfr.reshape(N_HEADS, t).T, ((0, 0), (0, LANES - N_HEADS)))
    dfl, d_bf = _forget_bwd(dfc, fl, bf, seq)
    du, d_mix, d_scale = _pool_bwd(dps, p, mixb, links.tie(pool_scale), seq)
    d_wu_in = _matmul("dw_in_u", du, h, "tn", F32, 512, 1024, 1024, dep=links.token)
    small = (jnp.zeros_like(norm1_g), d_bf[:, :N_HEADS], d_mix, d_scale, d_g2n, d_gf)
    d_wq = _matmul("dw_in_q", dq, h, "tn", F32, 512, 1024, 1024, dep=links.token)
    d_wk = _matmul("dw_in_k", dk, h, "tn", F32, 512, 1024, 1024, dep=links.token)
    d_wv = _matmul("dw_in_v", dv, h, "tn", F32, 512, 1024, 1024, dep=links.token)
    links.advance([d_wu_in, d_wq, d_wk, d_wv], begin=[_small_begin("small", _pack_small(*small, extra=loss))])
    d_wf = _matmul("dw_in_f", dfl, h, "tn", F32, LANES, 1024, 512)
    d_wg2 = _matmul("dw_in_gates", dg2, h, "tn", F32, 1024, 1024, 1024, dep=links.token)
    d_win = jnp.concatenate([d_wu_in, d_wq, d_wk, d_wv, d_wf[:N_HEADS], d_wg2], axis=0)
    comm_b = [d_win.reshape(N_CHIPS, d_win.shape[0] // N_CHIPS, d)]
    links.advance(comm_b, begin=[_reduce_begin("b", comm_b, column_halves=True)])
    if between is not None:
        between()
    dx, d_g1 = _in_bwd(du, dq, dk, dv, dg2, dfl, dx1, x, links.tie(norm1_g), wu, wqkv, wg2, wft, 512)
    return loss, dx, (d_g1,) + small[1:]


def kernel(x, norm1_g, w_in, b_forget, pool_mix, pool_scale, w_pool_out, w_attn_out, w_out, norm2_g, w_ffn_gate, w_ffn_up, w_ffn_down, norm_f_g, loss_target, m_norm1_g, m_w_in, m_b_forget, m_pool_mix, m_pool_scale, m_w_pool_out, m_w_attn_out, m_w_out, m_norm2_g, m_w_ffn_gate, m_w_ffn_up, m_w_ffn_down, m_norm_f_g, v_norm1_g, v_w_in, v_b_forget, v_pool_mix, v_pool_scale, v_w_pool_out, v_w_attn_out, v_w_out, v_norm2_g, v_w_ffn_gate, v_w_ffn_up, v_w_ffn_down, v_norm_f_g):
    nb, seq, d = x.shape
    group_a = ((w_ffn_gate, m_w_ffn_gate, v_w_ffn_gate, True, 9), (w_ffn_up, m_w_ffn_up, v_w_ffn_up, True, 10),
               (w_ffn_down, m_w_ffn_down, v_w_ffn_down, False, 11))
    group_m = ((w_pool_out, m_w_pool_out, v_w_pool_out, False, 5), (w_attn_out, m_w_attn_out, v_w_attn_out, False, 6),
               (w_out, m_w_out, v_w_out, False, 7))
    group_b = ((w_in, m_w_in, v_w_in, False, 1),)
    small_w = (norm1_g, b_forget, pool_mix, pool_scale, norm2_g, norm_f_g)
    small_m = (m_norm1_g, m_b_forget, m_pool_mix, m_pool_scale, m_norm2_g, m_norm_f_g)
    small_v = (v_norm1_g, v_b_forget, v_pool_mix, v_pool_scale, v_norm2_g, v_norm_f_g)
    small_pos = (0, 2, 3, 4, 8, 12)
    view = lambda a, tr: a[0].T if tr else a[0]
    unview = lambda a, tr, like: (a.T if tr else a).reshape(like.shape)

    def shard(w, tr):
        lw = view(w, tr).astype(BF16)
        return lw.reshape(2, lw.shape[0] // 2, lw.shape[1])

    cm = lambda a: jnp.transpose(a, (2, 0, 1))
    shard_in = _rows_to_bf16("w_in_to_bf16", cm(w_in))
    links = _MeshLinks([shard_in],
                       [shard(w_pool_out, False), shard(w_attn_out, False), shard(w_out, False),
                        shard(w_ffn_gate, True), shard(w_ffn_up, True), shard(w_ffn_down, False)])
    grads, deltas, new_m, new_v = [None] * 13, [None] * 13, [None] * 13, [None] * 13
    pos_c = jnp.stack([lax.axis_index("c")]).astype(jnp.int32)

    def update(tag, group, dep, members=(0, 1, 2)):
        last = []
        reduced = links.reduced(tag)
        for k in members:
            (w, m, v, tr, pos), (mine, other) = group[k], reduced[k]
            outs = _adamw_halves(f"adamw_{tag}{k}", pos_c, view(w, tr), mine, other, view(m, tr), view(v, tr), 256,
                                 dep=dep)
            grads[pos], deltas[pos], new_m[pos], new_v[pos] = (unview(a, tr, w) for a in outs)
            last.append(outs[1])
        return last

    def update_a():
        links.advance(update("a", group_a, links.token, members=(0,)))

    loss, dx, small_g = _local_step(
        links, x.reshape(nb * seq, d), loss_target.reshape(nb * seq, d), seq,
        norm1_g, b_forget, pool_mix[0], pool_scale, norm2_g, norm_f_g.reshape(1, d), between=update_a)

    links.advance(dx, begin=[_small_begin("g1", small_g[0].reshape(8, LANES))])
    last = update("m", group_m, links.token) + update("a", group_a, links.token, members=(2,))
    small_rest = _small_sum("small_sum", *links.reduced("small")[0])
    links.advance(last + [small_rest])
    last = update("a", group_a, links.token, members=(1,))
    small_sum = jnp.concatenate([_small_sum("g1_sum", *links.reduced("g1")[0]), small_rest[8:]], axis=0)
    loss_out = small_sum[8, N_HEADS]
    dl, mn, vn = _adamw("adamw_small", _pack_small(*small_w), small_sum * _small_mask(), _pack_small(*small_m),
                        _pack_small(*small_v))
    for pos, g, a, b, e in zip(small_pos, _unpack_small(small_sum, small_w), _unpack_small(dl, small_w),
                               _unpack_small(mn, small_w), _unpack_small(vn, small_w)):
        grads[pos], deltas[pos], new_m[pos], new_v[pos] = g, a, b, e
    links.advance(last + [dl])
    (mine, other), = links.reduced("b")
    outs = _adamw_rows("adamw_b0", pos_c, cm(w_in), mine, other, cm(m_w_in), cm(v_w_in))
    grads[1], deltas[1], new_m[1], new_v[1] = (jnp.transpose(a, (1, 2, 0)) for a in outs)

    return (loss_out, dx.reshape(nb, seq, d), *grads, *deltas, *new_m, *new_v)


def _small_mask():
    rows = lax.broadcasted_iota(jnp.int32, (552, LANES), 0)
    lanes = lax.broadcasted_iota(jnp.int32, (552, LANES), 1)
    return jnp.where(jnp.logical_and(rows == 8, lanes == N_HEADS), 0.0, 1.0).astype(F32)
```

```python
import jax
import jax.numpy as jnp
from jax import lax
from jax.experimental import pallas as pl
from jax.experimental.pallas import tpu as pltpu

F32 = jnp.float32
BF16 = jnp.bfloat16

D_MODEL = 1024
POOL_WINDOWS = (2, 4, 8, 16)
POOL_GROUPS = 4
POOL_GROUP_DIM = 128
POOL_WIDTH = 512
HEAD_DIM = 64
N_HEADS = 8
ATTN_WIDTH = 512
D_FF = 2816
RMS_EPS = 1e-6
ATTN_SCALE = HEAD_DIM ** -0.5
NEG_BIG = -1e30

ADAM_LR = 0.001
ADAM_B1 = 0.9
ADAM_B2 = 0.999
ADAM_EPS = 1e-08
ADAM_WD = 0.01
ADAM_STEP = 10

LANES = 128
N_CHIPS = 4
N_DEV = 8
VMEM_LIMIT_V7X = 52 * 1024 * 1024
ROW_CHUNK = 256
MESH = pl.DeviceIdType.MESH
ANY = pl.BlockSpec(memory_space=pl.ANY)


def _cparams(*sem):
    return pltpu.CompilerParams(dimension_semantics=sem if sem else None, vmem_limit_bytes=VMEM_LIMIT_V7X)


def _dep_list(dep):
    return [] if dep is None else (list(dep) if isinstance(dep, (list, tuple)) else [dep])


def _after(body, n_in, dep):
    k = len(_dep_list(dep))
    if k == 0:
        return body

    def wrapped(*refs):
        body(*refs[:n_in], *refs[n_in + k:])

    return wrapped


def _dep_args(dep):
    deps = _dep_list(dep)
    return [ANY] * len(deps), deps


def _dot(a, b):
    return lax.dot_general(a, b, (((1,), (0,)), ((), ())), preferred_element_type=F32)


def _dot_nt(a, b):
    return lax.dot_general(a, b, (((1,), (1,)), ((), ())), preferred_element_type=F32)


def _dot_tn(a, b):
    return lax.dot_general(a, b, (((0,), (0,)), ((), ())), preferred_element_type=F32)


def _sigmoid(x):
    return jax.nn.sigmoid(x)


def _rms_fwd(x, g):
    r = lax.rsqrt(jnp.mean(x * x, axis=-1, keepdims=True) + RMS_EPS)
    return (x * r) * g


def _rms_bwd(x, g, dy):
    r = lax.rsqrt(jnp.mean(x * x, axis=-1, keepdims=True) + RMS_EPS)
    xh = x * r
    dg = jnp.sum(dy * xh, axis=0, keepdims=True)
    dxh = dy * g
    dx = r * (dxh - xh * jnp.mean(dxh * xh, axis=-1, keepdims=True))
    return dx, dg


def _matmul(name, a, b, mode, out_dtype, tm, tn, tk, dep=None):
    if mode == "nn":
        (m, k), (_, n) = a.shape, b.shape
    elif mode == "nt":
        (m, k), (n, _) = a.shape, b.shape
    else:
        (k, m), (_, n) = a.shape, b.shape
    tm, tn, tk = min(tm, m), min(tn, n), min(tk, k)
    assert m % tm == 0 and n % tn == 0 and k % tk == 0, (name, m, n, k, tm, tn, tk)
    nk = k // tk
    if mode == "tn":
        a_spec = pl.BlockSpec((tk, tm), lambda i, j, kk: (kk, i))
    else:
        a_spec = pl.BlockSpec((tm, tk), lambda i, j, kk: (i, kk))
    if mode == "nt":
        b_spec = pl.BlockSpec((tn, tk), lambda i, j, kk: (j, kk))
    else:
        b_spec = pl.BlockSpec((tk, tn), lambda i, j, kk: (kk, j))
    dot = {"nn": _dot, "nt": _dot_nt, "tn": _dot_tn}[mode]
    use_scratch = nk > 1 and out_dtype != F32

    def body(a_ref, b_ref, o_ref, *scratch):
        if nk == 1 and mode != "tn":
            rows = min(ROW_CHUNK, tm)
            bb = b_ref[...].astype(BF16)
            for r0 in range(0, tm, rows):
                o_ref[r0:r0 + rows, :] = dot(a_ref[r0:r0 + rows, :].astype(BF16), bb).astype(out_dtype)
            return
        prod = dot(a_ref[...].astype(BF16), b_ref[...].astype(BF16))
        if nk == 1:
            o_ref[...] = prod.astype(out_dtype)
            return
        acc = scratch[0] if use_scratch else o_ref
        kk = pl.program_id(2)

        @pl.when(kk == 0)
        def _():
            acc[...] = prod

        @pl.when(kk > 0)
        def _():
            acc[...] += prod

        if use_scratch:
            @pl.when(kk == nk - 1)
            def _():
                o_ref[...] = acc[...].astype(out_dtype)

    dep_specs, dep_ops = _dep_args(dep)
    return pl.pallas_call(
        _after(body, 2, dep),
        name=name,
        out_shape=jax.ShapeDtypeStruct((m, n), out_dtype),
        grid=(m // tm, n // tn, nk),
        in_specs=[a_spec, b_spec] + dep_specs,
        out_specs=pl.BlockSpec((tm, tn), lambda i, j, kk: (i, j)),
        scratch_shapes=[pltpu.VMEM((tm, tn), F32)] if use_scratch else [],
        compiler_params=_cparams("parallel", "parallel", "arbitrary"),
    )(a, b, *dep_ops)


def _norm_fwd(name, x, g, tm):
    t, d = x.shape
    tm = min(tm, t)

    def body(x_ref, g_ref, h_ref):
        h_ref[...] = _rms_fwd(x_ref[...], g_ref[...]).astype(BF16)

    return pl.pallas_call(
        body, name=name, out_shape=jax.ShapeDtypeStruct((t, d), BF16), grid=(t // tm,),
        in_specs=[pl.BlockSpec((tm, d), lambda i: (i, 0)), pl.BlockSpec((1, d), lambda i: (0, 0))],
        out_specs=pl.BlockSpec((tm, d), lambda i: (i, 0)),
        compiler_params=_cparams("parallel"),
    )(x, g)


def _split3(x):
    hi = x.astype(BF16)
    r1 = x - hi.astype(F32)
    mid = r1.astype(BF16)
    lo = (r1 - mid.astype(F32)).astype(BF16)
    return hi, mid, lo


def _tri_dot(tri, x):
    hi, mid, lo = _split3(x)
    return _dot(tri, hi) + _dot(tri, mid) + _dot(tri, lo)


def _forget_fwd(h, wf, bf, seq):
    t, d = h.shape
    cb = min(256, seq)

    def body(h_ref, wf_ref, bf_ref, fl_ref, fc_ref):
        fl = _dot(h_ref[...], wf_ref[...])
        fl_ref[...] = fl
        xx = fl + bf_ref[...]
        lf = jnp.minimum(xx, 0.0) - jnp.log(1.0 + jnp.exp(-jnp.abs(xx)))
        ri = lax.broadcasted_iota(jnp.int32, (cb, cb), 0)
        ci = lax.broadcasted_iota(jnp.int32, (cb, cb), 1)
        tri = (ri >= ci).astype(BF16)
        carry = jnp.zeros((1, LANES), F32)
        for blk in range(seq // cb):
            cs = _tri_dot(tri, lf[blk * cb:(blk + 1) * cb]) + carry
            fc_ref[blk * cb:(blk + 1) * cb, :] = cs
            carry = cs[cb - 1:cb, :]

    return pl.pallas_call(
        body, name="forget_fwd",
        out_shape=(jax.ShapeDtypeStruct((t, LANES), F32), jax.ShapeDtypeStruct((t, LANES), F32)),
        grid=(t // seq,),
        in_specs=[pl.BlockSpec((seq, d), lambda b: (b, 0)), pl.BlockSpec((d, LANES), lambda b: (0, 0)),
                  pl.BlockSpec((1, LANES), lambda b: (0, 0))],
        out_specs=(pl.BlockSpec((seq, LANES), lambda b: (b, 0)), pl.BlockSpec((seq, LANES), lambda b: (b, 0))),
        compiler_params=_cparams("parallel"),
    )(h, wf, bf)


def _pool_fwd(u, mix, scale, seq):
    t = u.shape[0]

    def body(u_ref, mix_ref, sc_ref, p_ref, ps_ref):
        tpos = lax.broadcasted_iota(jnp.int32, (seq, POOL_GROUP_DIM), 0)
        for g in range(POOL_GROUPS):
            sl = slice(g * POOL_GROUP_DIM, (g + 1) * POOL_GROUP_DIM)
            ug = u_ref[:, sl]
            s = ug
            for lvl in range(g + 1):
                d = 2 ** lvl
                s = s + jnp.where(tpos >= d, pltpu.roll(s, d, 0), 0.0)
            cnt = jnp.minimum(tpos + 1, POOL_WINDOWS[g]).astype(F32)
            pb = (s / cnt - ug).astype(BF16)
            p_ref[:, sl] = pb
            ps_ref[:, sl] = (_dot(pb, mix_ref[g]) * sc_ref[:, sl]).astype(BF16)

    return pl.pallas_call(
        body, name="pool_fwd",
        out_shape=(jax.ShapeDtypeStruct((t, POOL_WIDTH), BF16), jax.ShapeDtypeStruct((t, POOL_WIDTH), BF16)),
        grid=(t // seq,),
        in_specs=[pl.BlockSpec((seq, POOL_WIDTH), lambda b: (b, 0)),
                  pl.BlockSpec((POOL_GROUPS, POOL_GROUP_DIM, POOL_GROUP_DIM), lambda b: (0, 0, 0)),
                  pl.BlockSpec((1, POOL_WIDTH), lambda b: (0, 0))],
        out_specs=(pl.BlockSpec((seq, POOL_WIDTH), lambda b: (b, 0)), pl.BlockSpec((seq, POOL_WIDTH), lambda b: (b, 0))),
        compiler_params=_cparams("parallel"),
    )(u, mix, scale)


def _aug_constants():
    w = N_HEADS * LANES
    rows = jnp.arange(3 * LANES)
    piece, head = rows // LANES, rows % LANES
    cols = jnp.arange(w)
    live = (head < N_HEADS)[:, None]
    pq = (live & (cols[None, :] == (head * LANES + HEAD_DIM + piece)[:, None])).astype(BF16)
    pk = -(live & (cols[None, :] == (head * LANES + HEAD_DIM + 3 + piece)[:, None])).astype(BF16)
    lane = cols % LANES
    oq = ((lane >= HEAD_DIM + 3) & (lane < HEAD_DIM + 6)).astype(F32)[None, :]
    ok = ((lane >= HEAD_DIM) & (lane < HEAD_DIM + 3)).astype(F32)[None, :]
    return pq, pk, oq, ok


def _head_blocks(wt):
    d = wt.shape[1]
    return jnp.pad(wt.reshape(N_HEADS, HEAD_DIM, d), ((0, 0), (0, LANES - HEAD_DIM), (0, 0))).reshape(N_HEADS * LANES, d)


def _attn_prep(h, wq, wk, wv, fcum, tm):
    t, d = h.shape
    tm = min(tm, t)
    rows = min(ROW_CHUNK, tm)
    w = N_HEADS * LANES
    pq, pk, oq, ok = _aug_constants()

    def body(h_ref, wq_ref, wk_ref, wv_ref, f_ref, pq_ref, pk_ref, oq_ref, ok_ref, qa_ref, ka_ref, v_ref):
        for r0 in range(0, tm, rows):
            rs = slice(r0, r0 + rows)
            hh = h_ref[rs, :]
            fs = jnp.concatenate(_split3(f_ref[rs, :]), axis=1)
            q = _dot_nt(hh, wq_ref[...]).astype(BF16).astype(F32) * ATTN_SCALE
            qa_ref[rs, :] = (q + _dot(fs, pq_ref[...]) + oq_ref[...]).astype(BF16)
            k = _dot_nt(hh, wk_ref[...]).astype(BF16).astype(F32)
            ka_ref[rs, :] = (k + _dot(fs, pk_ref[...]) + ok_ref[...]).astype(BF16)
            v_ref[rs, :] = _dot_nt(hh, wv_ref[...]).astype(BF16)

    row = lambda n: pl.BlockSpec((tm, n), lambda i: (i, 0))
    full = lambda a: pl.BlockSpec(a.shape, lambda i: (0, 0))
    return pl.pallas_call(
        body, name="attn_prep",
        out_shape=(jax.ShapeDtypeStruct((t, w), BF16), jax.ShapeDtypeStruct((t, w), BF16),
                   jax.ShapeDtypeStruct((t, ATTN_WIDTH), BF16)),
        grid=(t // tm,),
        in_specs=[row(d), full(wq), full(wk), full(wv), row(LANES), full(pq), full(pk), full(oq), full(ok)],
        out_specs=(row(w), row(w), row(ATTN_WIDTH)),
        compiler_params=_cparams("parallel"),
    )(h, wq, wk, wv, fcum, pq, pk, oq, ok)


def _fold_lanes(x, op):
    out = x[:, :LANES]
    for g in range(1, x.shape[1] // LANES):
        out = op(out, x[:, g * LANES:(g + 1) * LANES])
    return out


def _causal_sweep(i, tile, carry):
    def quad(jj, c):
        for u in range(4):
            c = tile(4 * jj + u, c, False)
        return c

    carry = lax.fori_loop(0, i // 4, quad, carry)
    base = 4 * (i // 4)
    carry = lax.cond(i % 4 >= 2, lambda c: tile(base + 1, tile(base, c, False), False), lambda c: c, carry)
    return lax.cond(i % 2 == 1, lambda c: tile(i, tile(i - 1, c, False), True), lambda c: tile(i, c, True), carry)


def _attn_fwd(qa, ka, v, seq, tq, dep=None):
    t = qa.shape[0]
    nq = seq // tq
    hp_n = N_HEADS // 2
    heads = [slice(e * LANES, (e + 1) * LANES) for e in range(2)]

    def body(q_ref, k_ref, v_ref, o_ref, lse_ref, s_buf):
        i = pl.program_id(2)
        diag_ok = lax.broadcasted_iota(jnp.int32, (tq, tq), 0) >= lax.broadcasted_iota(jnp.int32, (tq, tq), 1)
        qs = [q_ref[:, hl] for hl in heads]

        def sweep1(j, mxs, diagonal):
            r0 = pl.multiple_of(j * tq, tq)
            out = []
            for e, hl in enumerate(heads):
                s = _dot_nt(qs[e], k_ref[pl.ds(r0, tq), hl])
                if diagonal:
                    s = jnp.where(diag_ok, s, NEG_BIG)
                s_buf[e, j] = s
                out.append(jnp.maximum(mxs[e], _fold_lanes(s, jnp.maximum)))
            return tuple(out)

        mxs = _causal_sweep(i, sweep1, (jnp.full((tq, LANES), NEG_BIG, F32),) * 2)
        ms = [jnp.max(mx, axis=1, keepdims=True) for mx in mxs]

        def sweep2(j, carry, diagonal):
            r0 = pl.multiple_of(j * tq, tq)
            vv = v_ref[pl.ds(r0, tq), :]
            out = []
            for e in range(2):
                p = jnp.exp(s_buf[e, j] - ms[e])
                out += [carry[2 * e] + _fold_lanes(p, jnp.add), carry[2 * e + 1] + _dot(p.astype(BF16), vv)]
            return tuple(out)

        res = _causal_sweep(i, sweep2, (jnp.zeros((tq, LANES), F32),) * 4)
        outs = []
        for e in range(2):
            l = jnp.sum(res[2 * e], axis=1, keepdims=True)
            outs.append(res[2 * e + 1] / l)
            lse_ref[:, e:e + 1] = ms[e] + jnp.log(l)
        lane = lax.broadcasted_iota(jnp.int32, (tq, LANES), 1)
        o_ref[...] = jnp.where(lane < HEAD_DIM, outs[0], outs[1])

    dep_specs, dep_ops = _dep_args(dep)
    return pl.pallas_call(
        _after(body, 3, dep), name="attn_fwd",
        out_shape=(jax.ShapeDtypeStruct((t, ATTN_WIDTH), F32), jax.ShapeDtypeStruct((hp_n, t, 2), F32)),
        grid=(t // seq, hp_n, nq),
        in_specs=[pl.BlockSpec((tq, 2 * LANES), lambda b, hp, i: (b * nq + i, hp)),
                  pl.BlockSpec((seq, 2 * LANES), lambda b, hp, i: (b, hp)),
                  pl.BlockSpec((seq, LANES), lambda b, hp, i: (b, hp))] + dep_specs,
        out_specs=(pl.BlockSpec((tq, LANES), lambda b, hp, i: (b * nq + i, hp)),
                   pl.BlockSpec((None, tq, 2), lambda b, hp, i: (hp, b * nq + i, 0))),
        scratch_shapes=[pltpu.VMEM((2, nq, tq, tq), F32)],
        compiler_params=_cparams("parallel", "parallel", "arbitrary"),
    )(qa, ka, v, *dep_ops)


def _merge_fwd(x, ps, o, g2, wpo, wao, wout, tm):
    t, d = x.shape
    tm = min(tm, t)
    rows = min(ROW_CHUNK, tm)

    def body(x_ref, ps_ref, o_ref, gp_ref, ga_ref, wpo_ref, wao_ref, wout_ref, mg_ref, x1_ref):
        for r0 in range(0, tm, rows):
            rs = slice(r0, r0 + rows)
            py = _dot(ps_ref[rs, :], wpo_ref[...])
            ay = _dot(o_ref[rs, :].astype(BF16), wao_ref[...])
            mb = (_sigmoid(gp_ref[rs, :].astype(F32)) * py + _sigmoid(ga_ref[rs, :].astype(F32)) * ay).astype(BF16)
            mg_ref[rs, :] = mb
            x1_ref[rs, :] = x_ref[rs, :] + _dot(mb, wout_ref[...])

    row = lambda w: pl.BlockSpec((tm, w), lambda i: (i, 0))
    full = lambda a: pl.BlockSpec(a.shape, lambda i: (0, 0))
    return pl.pallas_call(
        body, name="merge_fwd",
        out_shape=(jax.ShapeDtypeStruct((t, d), BF16), jax.ShapeDtypeStruct((t, d), F32)),
        grid=(t // tm,),
        in_specs=[row(d), row(POOL_WIDTH), row(ATTN_WIDTH), pl.BlockSpec((tm, d), lambda i: (i, 0)),
                  pl.BlockSpec((tm, d), lambda i: (i, 1)), full(wpo), full(wao), full(wout)],
        out_specs=(row(d), row(d)),
        compiler_params=_cparams("parallel"),
    )(x, ps, o, g2, g2, wpo, wao, wout)


def _ffn_fwd(x1, g, wg, wu, wd, tm, tf):
    t, d = x1.shape
    f = wg.shape[0]
    tm = min(tm, t)
    nf = f // tf
    rows = min(512, tm)

    def body(x1_ref, g_ref, wg_ref, wu_ref, wd_ref, h2_ref, gt_ref, up_ref, act_ref, x2_ref):
        j = pl.program_id(1)

        @pl.when(j == 0)
        def _():
            h2_ref[...] = _rms_fwd(x1_ref[...], g_ref[...]).astype(BF16)

            x2_ref[...] = x1_ref[...]

        for r0 in range(0, tm, rows):
            rs = slice(r0, r0 + rows)
            h2 = h2_ref[rs, :]
            gt = _dot_nt(h2, wg_ref[...])
            up = _dot_nt(h2, wu_ref[...])
            sg = _sigmoid(gt)
            silu = gt * sg
            act = (silu * up).astype(BF16)
            gt_ref[rs, :] = (up * (sg * (1.0 + gt * (1.0 - sg)))).astype(BF16)
            up_ref[rs, :] = silu.astype(BF16)
            act_ref[rs, :] = act
            x2_ref[rs, :] += _dot(act, wd_ref[...])

    return pl.pallas_call(
        body, name="ffn_fwd",
        out_shape=(jax.ShapeDtypeStruct((t, d), BF16), jax.ShapeDtypeStruct((t, f), BF16),
                   jax.ShapeDtypeStruct((t, f), BF16), jax.ShapeDtypeStruct((t, f), BF16),
                   jax.ShapeDtypeStruct((t, d), F32)),
        grid=(t // tm, nf),
        in_specs=[pl.BlockSpec((tm, d), lambda i, j: (i, 0)), pl.BlockSpec((1, d), lambda i, j: (0, 0)),
                  pl.BlockSpec((tf, d), lambda i, j: (j, 0)), pl.BlockSpec((tf, d), lambda i, j: (j, 0)),
                  pl.BlockSpec((tf, d), lambda i, j: (j, 0))],
        out_specs=(pl.BlockSpec((tm, d), lambda i, j: (i, 0)), pl.BlockSpec((tm, tf), lambda i, j: (i, j)),
                   pl.BlockSpec((tm, tf), lambda i, j: (i, j)), pl.BlockSpec((tm, tf), lambda i, j: (i, j)),
                   pl.BlockSpec((tm, d), lambda i, j: (i, 0))),
        compiler_params=_cparams("parallel", "arbitrary"),
    )(x1, g, wg, wu, wd)


def _final_fwd_bwd(x2, target, g, tm):
    t, d = x2.shape
    tm = min(tm, t)

    def body(x_ref, t_ref, g_ref, loss_ref, dx_ref, dg_ref):
        i = pl.program_id(0)
        x = x_ref[...]
        gg = g_ref[...]
        err = _rms_fwd(x, gg) - t_ref[...]
        part = 0.5 * jnp.sum(jnp.mean(err * err, axis=-1, keepdims=True), axis=0, keepdims=True)
        dx, dg = _rms_bwd(x, gg, err * (1.0 / d))
        dx_ref[...] = dx

        @pl.when(i == 0)
        def _():
            loss_ref[...] = jnp.zeros_like(loss_ref)
            dg_ref[...] = jnp.zeros_like(dg_ref)

        loss_ref[...] += jnp.broadcast_to(part, loss_ref.shape)
        dg_ref[...] += dg

    return pl.pallas_call(
        body, name="final_fwd_bwd",
        out_shape=(jax.ShapeDtypeStruct((1, LANES), F32), jax.ShapeDtypeStruct((t, d), F32),
                   jax.ShapeDtypeStruct((1, d), F32)),
        grid=(t // tm,),
        in_specs=[pl.BlockSpec((tm, d), lambda i: (i, 0)), pl.BlockSpec((tm, d), lambda i: (i, 0)),
                  pl.BlockSpec((1, d), lambda i: (0, 0))],
        out_specs=(pl.BlockSpec((1, LANES), lambda i: (0, 0)), pl.BlockSpec((tm, d), lambda i: (i, 0)),
                   pl.BlockSpec((1, d), lambda i: (0, 0))),
        compiler_params=_cparams("arbitrary"),
    )(x2, target, g)


def _ffn_bwd(dx2, x1, g, gt, up, wg, wu, wd, tm, tf):
    t, d = dx2.shape
    f = gt.shape[1]
    tm = min(tm, t)
    nf = f // tf
    wgu = jnp.concatenate([wg.reshape(nf, tf, d), wu.reshape(nf, tf, d)], axis=1).reshape(2 * f, d)
    rows = min(256, tm)

    def body(dx2_ref, x1_ref, g_ref, gt_ref, up_ref, wgu_ref, wd_ref, dgt_ref, dup_ref, dx1_ref, dg_ref, acc_ref,
             dxb_ref):
        i, j = pl.program_id(0), pl.program_id(1)

        @pl.when(j == 0)
        def _():
            dxb_ref[...] = dx2_ref[...].astype(BF16)
            acc_ref[...] = jnp.zeros_like(acc_ref)

        for r0 in range(0, tm, rows):
            rs = slice(r0, r0 + rows)
            dact = _dot_nt(dxb_ref[rs, :], wd_ref[...])
            dgt = (dact * gt_ref[rs, :].astype(F32)).astype(BF16)
            dup = (dact * up_ref[rs, :].astype(F32)).astype(BF16)
            dgt_ref[rs, :] = dgt
            dup_ref[rs, :] = dup
            acc_ref[rs, :] += _dot(jnp.concatenate([dgt, dup], axis=1), wgu_ref[...])

        @pl.when(jnp.logical_and(i == 0, j == 0))
        def _():
            dg_ref[...] = jnp.zeros_like(dg_ref)

        @pl.when(j == nf - 1)
        def _():
            dxn, dg = _rms_bwd(x1_ref[...], g_ref[...], acc_ref[...])
            dx1_ref[...] = dx2_ref[...] + dxn
            dg_ref[...] += dg

    return pl.pallas_call(
        body, name="ffn_bwd",
        out_shape=(jax.ShapeDtypeStruct((t, f), BF16), jax.ShapeDtypeStruct((t, f), BF16),
                   jax.ShapeDtypeStruct((t, d), F32), jax.ShapeDtypeStruct((1, d), F32)),
        grid=(t // tm, nf),
        in_specs=[pl.BlockSpec((tm, d), lambda i, j: (i, 0)), pl.BlockSpec((tm, d), lambda i, j: (i, 0)),
                  pl.BlockSpec((1, d), lambda i, j: (0, 0)),
                  pl.BlockSpec((tm, tf), lambda i, j: (i, j)), pl.BlockSpec((tm, tf), lambda i, j: (i, j)),
                  pl.BlockSpec((2 * tf, d), lambda i, j: (j, 0)), pl.BlockSpec((tf, d), lambda i, j: (j, 0))],
        out_specs=(pl.BlockSpec((tm, tf), lambda i, j: (i, j)), pl.BlockSpec((tm, tf), lambda i, j: (i, j)),
                   pl.BlockSpec((tm, d), lambda i, j: (i, 0)), pl.BlockSpec((1, d), lambda i, j: (0, 0))),
        scratch_shapes=[pltpu.VMEM((tm, d), F32), pltpu.VMEM((tm, d), BF16)],
        compiler_params=_cparams("arbitrary", "arbitrary"),
    )(dx2, x1, g, gt, up, wgu, wd)


def _merge_bwd(dx1, ps, o, g2, wpo, wao, wout, tm, dep=None):
    t, d = dx1.shape
    tm = min(tm, t)
    rows = min(ROW_CHUNK, tm)

    def body(dx1_ref, ps_ref, o_ref, gp_ref, ga_ref, wpo_ref, wao_ref, wout_ref, dpy_ref, day_ref, dg2_ref, dps_ref, da_ref):
        for r0 in range(0, tm, rows):
            rs = slice(r0, r0 + rows)
            dm = _dot_nt(dx1_ref[rs, :].astype(BF16), wout_ref[...])
            py = _dot(ps_ref[rs, :], wpo_ref[...])
            ay = _dot(o_ref[rs, :].astype(BF16), wao_ref[...])
            sp = _sigmoid(gp_ref[rs, :].astype(F32))
            sa = _sigmoid(ga_ref[rs, :].astype(F32))
            dpy = (dm * sp).astype(BF16)
            day = (dm * sa).astype(BF16)
            dpy_ref[rs, :] = dpy
            day_ref[rs, :] = day
            dg2_ref[rs, :d] = (dm * py * (sp * (1.0 - sp))).astype(BF16)
            dg2_ref[rs, d:] = (dm * ay * (sa * (1.0 - sa))).astype(BF16)
            dps_ref[rs, :] = _dot_nt(dpy, wpo_ref[...])
            da_ref[rs, :] = _dot_nt(day, wao_ref[...]).astype(BF16)

    row = lambda w: pl.BlockSpec((tm, w), lambda i: (i, 0))
    full = lambda a: pl.BlockSpec(a.shape, lambda i: (0, 0))
    dep_specs, dep_ops = _dep_args(dep)
    return pl.pallas_call(
        _after(body, 8, dep), name="merge_bwd",
        out_shape=(jax.ShapeDtypeStruct((t, d), BF16), jax.ShapeDtypeStruct((t, d), BF16),
                   jax.ShapeDtypeStruct((t, 2 * d), BF16), jax.ShapeDtypeStruct((t, POOL_WIDTH), F32),
                   jax.ShapeDtypeStruct((t, ATTN_WIDTH), BF16)),
        grid=(t // tm,),
        in_specs=[row(d), row(POOL_WIDTH), row(ATTN_WIDTH), pl.BlockSpec((tm, d), lambda i: (i, 0)),
                  pl.BlockSpec((tm, d), lambda i: (i, 1)), full(wpo), full(wao), full(wout)] + dep_specs,
        out_specs=(row(d), row(d), row(2 * d), row(POOL_WIDTH), row(ATTN_WIDTH)),
        compiler_params=_cparams("parallel"),
    )(dx1, ps, o, g2, g2, wpo, wao, wout, *dep_ops)


def _attn_bwd(qa, ka, v, do, lse4, seq, tq, dep=None):
    t = qa.shape[0]
    nq = seq // tq
    hp_n = N_HEADS // 2
    heads = [slice(e * LANES, (e + 1) * LANES) for e in range(2)]

    def body(q_ref, k_ref, v_ref, do_ref, lse_ref, dq_ref, dk_ref, dv_ref, dfr_ref, dk_acc, dv_acc, p_buf, dp_buf):
        diag_ok = lax.broadcasted_iota(jnp.int32, (tq, tq), 0) >= lax.broadcasted_iota(jnp.int32, (tq, tq), 1)
        lane_q = lax.broadcasted_iota(jnp.int32, (tq, LANES), 1)
        mine_q = [lane_q < HEAD_DIM, lane_q >= HEAD_DIM]
        dv_acc[...] = jnp.zeros_like(dv_acc)
        dk_acc[...] = jnp.zeros_like(dk_acc)
        dfr_ref[...] = jnp.zeros_like(dfr_ref)
        transposed = lambda a: a.astype(F32).T.astype(BF16)

        def q_step(i, _):
            q0 = pl.multiple_of(i * tq, tq)
            qs = [q_ref[pl.ds(q0, tq), hl] for hl in heads]
            dov = do_ref[pl.ds(q0, tq), :]
            dos = [jnp.where(mq, dov, jnp.zeros((), BF16)) for mq in mine_q]
            qts = [transposed(q) for q in qs]
            dots = [transposed(a) for a in dos]
            lss = [lse_ref[pl.ds(q0, tq), e:e + 1] for e in range(2)]

            def sweep1(j, dls, diagonal):
                r0 = pl.multiple_of(j * tq, tq)
                vv = v_ref[pl.ds(r0, tq), :]
                out = []
                for e, hl in enumerate(heads):
                    s = _dot_nt(qs[e], k_ref[pl.ds(r0, tq), hl])
                    if diagonal:
                        s = jnp.where(diag_ok, s, NEG_BIG)
                    p = jnp.exp(s - lss[e])
                    dp = _dot_nt(dos[e], vv)
                    p_buf[e, j] = p
                    dp_buf[e, j] = dp
                    dv_acc[j] += _dot(dots[e], p.astype(BF16))
                    out.append(dls[e] + _fold_lanes(p * dp, jnp.add))
                return tuple(out)

            dls = _causal_sweep(i, sweep1, (jnp.zeros((tq, LANES), F32),) * 2)
            dls = [jnp.sum(d, axis=1, keepdims=True) for d in dls]

            def sweep2(j, dqs, diagonal):
                r0 = pl.multiple_of(j * tq, tq)
                out = []
                for e, hl in enumerate(heads):
                    ds = p_buf[e, j] * (dp_buf[e, j] - dls[e])
                    dfr_ref[e, pl.ds(j, 1), :] += jnp.sum(ds, axis=0, keepdims=True)
                    dsb = ds.astype(BF16)
                    dk_acc[e, j] += _dot(qts[e], dsb)
                    out.append(dqs[e] + _dot(dsb, k_ref[pl.ds(r0, tq), hl]))
                return tuple(out)

            dqs = _causal_sweep(i, sweep2, (jnp.zeros((tq, LANES), F32),) * 2)
            dq = jnp.where(mine_q[0], dqs[0], pltpu.roll(dqs[1], HEAD_DIM, 1)) * ATTN_SCALE
            dq_ref[pl.ds(q0, tq), :] = dq.astype(BF16)
            return 0

        lax.fori_loop(0, nq, q_step, 0)
        for j in range(nq):
            rs = slice(j * tq, (j + 1) * tq)
            dk = jnp.where(mine_q[0], dk_acc[0, j].T, pltpu.roll(dk_acc[1, j].T, HEAD_DIM, 1))
            dk_ref[rs, :] = dk.astype(BF16)
            dv_ref[rs, :] = dv_acc[j].T.astype(BF16)

    wide = pl.BlockSpec((seq, 2 * LANES), lambda b, hp: (b, hp))
    col = pl.BlockSpec((seq, LANES), lambda b, hp: (b, hp))
    pair = pl.BlockSpec((None, seq, 2), lambda b, hp: (hp, b, 0))
    dep_specs, dep_ops = _dep_args(dep)
    return pl.pallas_call(
        _after(body, 5, dep), name="attn_bwd",
        out_shape=(jax.ShapeDtypeStruct((t, ATTN_WIDTH), BF16),) * 3 + (jax.ShapeDtypeStruct((N_HEADS, t // tq, tq), F32),),
        grid=(t // seq, hp_n),
        in_specs=[wide, wide, col, col, pair] + dep_specs,
        out_specs=(col, col, col, pl.BlockSpec((2, nq, tq), lambda b, hp: (hp, b, 0))),
        scratch_shapes=[pltpu.VMEM((2, nq, LANES, tq), F32), pltpu.VMEM((nq, LANES, tq), F32),
                        pltpu.VMEM((2, nq, tq, tq), F32), pltpu.VMEM((2, nq, tq, tq), F32)],
        compiler_params=_cparams("parallel", "arbitrary"),
    )(qa, ka, v, do, lse4, *dep_ops)


def _forget_bwd(dfc, fl, bf, seq):
    t = fl.shape[0]
    cb = min(256, seq)
    nb = seq // cb

    def body(dfc_ref, fl_ref, bf_ref, dfl_ref, db_ref):
        b = pl.program_id(0)
        ri = lax.broadcasted_iota(jnp.int32, (cb, cb), 0)
        ci = lax.broadcasted_iota(jnp.int32, (cb, cb), 1)
        tri = (ci >= ri).astype(BF16)
        carry = jnp.zeros((1, LANES), F32)
        dbs = jnp.zeros((1, LANES), F32)
        for blk in reversed(range(nb)):
            rs = slice(blk * cb, (blk + 1) * cb)
            dlf = _tri_dot(tri, -dfc_ref[rs, :]) + carry
            carry = dlf[0:1, :]
            dfl = dlf * _sigmoid(-(fl_ref[rs, :] + bf_ref[...]))
            dfl_ref[rs, :] = dfl.astype(BF16)
            dbs = dbs + jnp.sum(dfl, axis=0, keepdims=True)

        @pl.when(b == 0)
        def _():
            db_ref[...] = jnp.zeros_like(db_ref)

        db_ref[...] += dbs

    return pl.pallas_call(
        body, name="forget_bwd",
        out_shape=(jax.ShapeDtypeStruct((t, LANES), BF16), jax.ShapeDtypeStruct((1, LANES), F32)),
        grid=(t // seq,),
        in_specs=[pl.BlockSpec((seq, LANES), lambda b: (b, 0)), pl.BlockSpec((seq, LANES), lambda b: (b, 0)),
                  pl.BlockSpec((1, LANES), lambda b: (0, 0))],
        out_specs=(pl.BlockSpec((seq, LANES), lambda b: (b, 0)), pl.BlockSpec((1, LANES), lambda b: (0, 0))),
        compiler_params=_cparams("arbitrary"),
    )(dfc, fl, bf)


def _pool_bwd(dps, p, mix, scale, seq):
    t = dps.shape[0]

    def body(dps_ref, p_ref, mix_ref, sc_ref, du_ref, dmix_ref, dsc_ref):
        b = pl.program_id(0)

        @pl.when(b == 0)
        def _():
            dmix_ref[...] = jnp.zeros_like(dmix_ref)
            dsc_ref[...] = jnp.zeros_like(dsc_ref)

        tpos = lax.broadcasted_iota(jnp.int32, (seq, POOL_GROUP_DIM), 0)
        for g in range(POOL_GROUPS):
            sl = slice(g * POOL_GROUP_DIM, (g + 1) * POOL_GROUP_DIM)
            pb = p_ref[:, sl]
            dpsg = dps_ref[:, sl]
            pm = _dot(pb, mix_ref[g])
            dsc_ref[:, sl] += jnp.sum(dpsg * pm, axis=0, keepdims=True)
            dpm = (dpsg * sc_ref[:, sl]).astype(BF16)
            dmix_ref[g] += _dot_tn(pb, dpm)
            dp = _dot_nt(dpm, mix_ref[g])
            cnt = jnp.minimum(tpos + 1, POOL_WINDOWS[g]).astype(F32)
            s = dp / cnt
            for lvl in range(g + 1):
                d = 2 ** lvl
                s = s + jnp.where(tpos < seq - d, pltpu.roll(s, seq - d, 0), 0.0)
            du_ref[:, sl] = (s - dp).astype(BF16)

    return pl.pallas_call(
        body, name="pool_bwd",
        out_shape=(jax.ShapeDtypeStruct((t, POOL_WIDTH), BF16),
                   jax.ShapeDtypeStruct((POOL_GROUPS, POOL_GROUP_DIM, POOL_GROUP_DIM), F32),
                   jax.ShapeDtypeStruct((1, POOL_WIDTH), F32)),
        grid=(t // seq,),
        in_specs=[pl.BlockSpec((seq, POOL_WIDTH), lambda b: (b, 0)), pl.BlockSpec((seq, POOL_WIDTH), lambda b: (b, 0)),
                  pl.BlockSpec((POOL_GROUPS, POOL_GROUP_DIM, POOL_GROUP_DIM), lambda b: (0, 0, 0)),
                  pl.BlockSpec((1, POOL_WIDTH), lambda b: (0, 0))],
        out_specs=(pl.BlockSpec((seq, POOL_WIDTH), lambda b: (b, 0)),
                   pl.BlockSpec((POOL_GROUPS, POOL_GROUP_DIM, POOL_GROUP_DIM), lambda b: (0, 0, 0)),
                   pl.BlockSpec((1, POOL_WIDTH), lambda b: (0, 0))),
        compiler_params=_cparams("arbitrary"),
    )(dps, p, mix, scale)


def _in_bwd(du, dq, dk, dv, dg2, dfl, dx1, x, g, wu, wqkv, wg2, wft, tm):
    t, d = x.shape
    tm = min(tm, t)
    rows = min(ROW_CHUNK, tm)
    aw = ATTN_WIDTH

    def body(du_ref, dq_ref, dk_ref, dv_ref, dg2_ref, dfl_ref, dx1_ref, x_ref, g_ref, wu_ref, wqkv_ref, wg2_ref, wft_ref,
             dx_ref, dg_ref):
        i = pl.program_id(0)

        @pl.when(i == 0)
        def _():
            dg_ref[...] = jnp.zeros_like(dg_ref)

        for r0 in range(0, tm, rows):
            rs = slice(r0, r0 + rows)
            dh = _dot(du_ref[rs, :], wu_ref[...])
            dh += _dot(dq_ref[rs, :], wqkv_ref[0:aw, :])
            dh += _dot(dk_ref[rs, :], wqkv_ref[aw:2 * aw, :])
            dh += _dot(dv_ref[rs, :], wqkv_ref[2 * aw:3 * aw, :])
            dh += _dot(dg2_ref[rs, :], wg2_ref[...])
            dh += _dot(dfl_ref[rs, :], wft_ref[...])
            dxn, dg = _rms_bwd(x_ref[rs, :], g_ref[...], dh)
            dx_ref[rs, :] = dx1_ref[rs, :] + dxn
            dg_ref[...] += dg

    row = lambda w: pl.BlockSpec((tm, w), lambda i: (i, 0))
    full = lambda a: pl.BlockSpec(a.shape, lambda i: (0, 0))
    return pl.pallas_call(
        body, name="in_bwd",
        out_shape=(jax.ShapeDtypeStruct((t, d), F32), jax.ShapeDtypeStruct((1, d), F32)),
        grid=(t // tm,),
        in_specs=[row(POOL_WIDTH), row(aw), row(aw), row(aw), row(2 * d), row(LANES), row(d), row(d),
                  pl.BlockSpec((1, d), lambda i: (0, 0)), full(wu), full(wqkv), full(wg2), full(wft)],
        out_specs=(row(d), pl.BlockSpec((1, d), lambda i: (0, 0))),
        compiler_params=_cparams("arbitrary"),
    )(du, dq, dk, dv, dg2, dfl, dx1, x, g, wu, wqkv, wg2, wft)


def _position():
    return lax.axis_index("x"), lax.axis_index("y"), lax.axis_index("c")


def _remote(src, dst, send_sem, recv_sem, device):
    return pltpu.make_async_remote_copy(src_ref=src, dst_ref=dst, send_sem=send_sem, recv_sem=recv_sem,
                                        device_id=device, device_id_type=MESH)


HBM = pl.BlockSpec(memory_space=pltpu.HBM)
SEM = pl.BlockSpec(memory_space=pltpu.SEMAPHORE)
DATAFLOW = pltpu.SideEffectType.DATAFLOW_SIDE_EFFECTING


def _copies_start(name, arrays, plan, m, dep=None):
    n = len(arrays)
    arrays = [pltpu.with_memory_space_constraint(a, pltpu.HBM) for a in arrays]

    def body(*refs):
        ins, send_sem, recv_sem, token = refs[:n], refs[n], refs[n + 1], refs[2 * n + 2]
        for i, (src, dst, device, _) in enumerate(plan(ins, *_position())):
            _remote(src, dst, send_sem.at[i], recv_sem.at[i], device).start()
        token[...] = jnp.zeros_like(token)

    dep_specs, dep_ops = _dep_args(dep)
    outs = pl.pallas_call(
        _after(body, n, dep), name=name,
        out_shape=(pltpu.SemaphoreType.DMA((m,)), pltpu.SemaphoreType.DMA((m,)),
                   *[pltpu.HBM(a.shape, a.dtype) for a in arrays], jax.ShapeDtypeStruct((8, LANES), F32)),
        in_specs=[HBM] * n + dep_specs, out_specs=(SEM, SEM, *[HBM] * n, pl.BlockSpec(memory_space=pltpu.VMEM)),
        input_output_aliases={i: i + 2 for i in range(n)},
        compiler_params=pltpu.CompilerParams(has_side_effects=DATAFLOW),
    )(*arrays, *dep_ops)
    return (outs[0], outs[1]), list(outs[2:2 + n]), outs[2 + n]


def _copies_wait(name, sems, arrays, plan, after):
    n = len(arrays)
    afters = list(after) if isinstance(after, (list, tuple)) else [after]

    def body(*refs):
        ins, send_sem, recv_sem = refs[:n], refs[n], refs[n + 1]
        for i, (src, dst, device, landing) in enumerate(plan(ins, *_position())):
            _remote(src, dst, send_sem.at[i], recv_sem.at[i], device).wait_send()
            _remote(landing, landing, send_sem.at[i], recv_sem.at[i], device).wait_recv()

    outs = pl.pallas_call(
        body, name=name,
        out_shape=tuple(pltpu.HBM(a.shape, a.dtype) for a in arrays),
        in_specs=[HBM] * n + [SEM, SEM] + [ANY] * len(afters), out_specs=tuple([HBM] * n),
        input_output_aliases={i: i for i in range(n)},
        compiler_params=pltpu.CompilerParams(has_side_effects=DATAFLOW),
    )(*arrays, sems[0], sems[1], *afters)
    return list(outs)


def _tie(x, dep):
    for token in _dep_list(dep):
        x = x + token[0, 0]
    return x


def _other_chips(x, y):
    return [(1 - x, y), (x, 1 - y), (1 - x, 1 - y)]


def _gather_begin(tag, shards, token, column_halves=False):
    n = len(shards)
    lands = [lax.empty((N_CHIPS,) + s.shape, s.dtype) for s in shards]
    if column_halves:
        cols = lambda ref, h: pl.ds(pl.multiple_of(h * (ref.shape[-1] // 2), LANES), ref.shape[-1] // 2)
        mine = lambda ref, h: ref.at[:, cols(ref, h)]
        landed = lambda ref, chip, h: ref.at[chip, :, cols(ref, h)]
    else:
        mine = lambda ref, h: ref.at[h]
        landed = lambda ref, chip, h: ref.at[chip, h]

    def plan(refs, x, y, c):
        return [(mine(refs[k], c), landed(refs[n + k], 2 * x + y, c), (ox, oy, c), landed(refs[n + k], 2 * ox + oy, c))
                for k in range(n) for ox, oy in _other_chips(x, y)]

    sems, thru, token = _copies_start(f"gather_{tag}_ici_start", list(shards) + lands, plan, 3 * n, dep=token)
    return dict(tag=tag, n=n, plan=plan, sems=sems, arrays=thru, token=token, landed=landed)


def _gather_forward(st, after):
    n, tag, landed = st["n"], st["tag"], st["landed"]
    thru = _copies_wait(f"gather_{tag}_ici_wait", st["sems"], st["arrays"], st["plan"], after)

    def plan(refs, x, y, c):
        return [(landed(refs[k], 2 * ox + oy, c), landed(refs[k], 2 * ox + oy, c), (x, y, 1 - c),
                 landed(refs[k], 2 * ox + oy, 1 - c))
                for k in range(n) for ox, oy in _other_chips(x, y)]

    sems, lands, token = _copies_start(f"gather_{tag}_fwd_start", thru[n:], plan, 3 * n)
    return dict(tag=tag, n=n, plan=plan, sems=sems, arrays=lands, token=token, shards=thru[:n])


def _gather_end(st, after, merge=True):
    lands = _copies_wait(f"gather_{st['tag']}_fwd_wait", st["sems"], st["arrays"], st["plan"], after)
    if not merge:
        return lands, st["shards"]
    me = 2 * lax.axis_index("x") + lax.axis_index("y")
    return [lax.dynamic_update_index_in_dim(g, s, me, 0) for g, s in zip(lands, st["shards"])]


def _add_keep_give(name, pos, a, a_keep, a_give, b, b_keep, b_give, steps):
    r, c = b.shape[-2:]

    def spec(arr, fn):
        lead = arr.ndim - 2

        def index(i, p):
            idx = tuple(fn(i, p))
            return idx if len(idx) == arr.ndim else idx + (0, 0)

        return pl.BlockSpec((None,) * lead + (r, c), index)

    out_spec = pl.BlockSpec((None, r, c), lambda i, p: (i, 0, 0))

    def body(p_ref, ak_ref, bk_ref, ag_ref, bg_ref, keep_ref, give_ref):
        keep_ref[...] = ak_ref[...] + bk_ref[...].astype(F32)
        give_ref[...] = (ag_ref[...] + bg_ref[...].astype(F32)).astype(BF16)

    return pl.pallas_call(
        body, name=name,
        out_shape=(jax.ShapeDtypeStruct((steps, r, c), F32), jax.ShapeDtypeStruct((steps, r, c), BF16)),
        grid_spec=pltpu.PrefetchScalarGridSpec(
            num_scalar_prefetch=1, grid=(steps,),
            in_specs=[spec(a, a_keep), spec(b, b_keep), spec(a, a_give), spec(b, b_give)],
            out_specs=(out_spec, out_spec)),
        compiler_params=_cparams("parallel"),
    )(pos, a, b, a, b)


def _add_keep_give_group(name, pos, items, steps, split=2):
    n = len(items)

    def spec(arr, fn, rows, c):
        return pl.BlockSpec((None,) * (arr.ndim - 2) + (rows, c), lambda i, j, p: tuple(fn(i, p)) + (j, 0))

    in_specs, out_specs, out_shape, operands = [], [], [], []
    for a, a_keep, a_give, b, b_keep, b_give in items:
        r, c = b.shape[-2:]
        rows = r // split
        assert r % split == 0 and rows % 16 == 0, (r, split)
        in_specs += [spec(a, a_keep, rows, c), spec(b, b_keep, rows, c), spec(a, a_give, rows, c),
                     spec(b, b_give, rows, c)]
        out_specs += [pl.BlockSpec((None, rows, c), lambda i, j, p: (i, j, 0))] * 2
        out_shape += [jax.ShapeDtypeStruct((steps, r, c), F32), jax.ShapeDtypeStruct((steps, r, c), BF16)]
        operands += [a, b, a, b]

    def body(p_ref, *refs):
        ins, outs = refs[:4 * n], refs[4 * n:]
        for k in range(n):
            ak_ref, bk_ref, ag_ref, bg_ref = ins[4 * k:4 * k + 4]
            outs[2 * k][...] = ak_ref[...] + bk_ref[...].astype(F32)
            outs[2 * k + 1][...] = (ag_ref[...] + bg_ref[...].astype(F32)).astype(BF16)

    outs = pl.pallas_call(
        body, name=name, out_shape=tuple(out_shape),
        grid_spec=pltpu.PrefetchScalarGridSpec(
            num_scalar_prefetch=1, grid=(steps, split), in_specs=in_specs, out_specs=tuple(out_specs)),
        compiler_params=_cparams("parallel", "parallel"),
    )(pos, *operands)
    return [(outs[2 * k], outs[2 * k + 1]) for k in range(n)]


def _add_last(name, a, b):
    _, r, c = a.shape
    blk = pl.BlockSpec((None, r, c), lambda i: (0, 0, 0))

    def body(a_ref, b_ref, o_ref):
        o_ref[...] = a_ref[...] + b_ref[...].astype(F32)

    return pl.pallas_call(
        body, name=name, out_shape=jax.ShapeDtypeStruct((r, c), F32), grid=(1,), in_specs=[blk, blk],
        out_specs=pl.BlockSpec((r, c), lambda i: (0, 0)), compiler_params=_cparams("arbitrary"),
    )(a, b)


def _add_last_group(name, pairs, split=2):
    n = len(pairs)
    in_specs, out_specs, out_shape, operands = [], [], [], []
    for a, b in pairs:
        _, r, c = a.shape
        rows = r // split
        assert r % split == 0 and rows % 16 == 0, (r, split)
        in_specs += [pl.BlockSpec((None, rows, c), lambda j: (0, j, 0))] * 2
        out_specs.append(pl.BlockSpec((rows, c), lambda j: (j, 0)))
        out_shape.append(jax.ShapeDtypeStruct((r, c), F32))
        operands += [a, b]

    def body(*refs):
        for k in range(n):
            refs[2 * n + k][...] = refs[2 * k][...] + refs[2 * k + 1][...].astype(F32)

    return list(pl.pallas_call(
        body, name=name, out_shape=tuple(out_shape), grid=(split,), in_specs=in_specs, out_specs=tuple(out_specs),
        compiler_params=_cparams("parallel"),
    )(*operands))


def _exchange_part(gives, lands, peer_fn):
    n = len(gives)

    def plan(refs, x, y, c):
        return [(refs[k], refs[n + k], peer_fn(x, y, c), refs[n + k]) for k in range(n)]

    return list(gives) + list(lands), plan, n


def _join_parts(parts):
    offsets, total = [], 0
    for arrays, _, _ in parts:
        offsets.append(total)
        total += len(arrays)

    def plan(refs, x, y, c):
        copies = []
        for (arrays, part_plan, _), off in zip(parts, offsets):
            copies += part_plan(refs[off:off + len(arrays)], x, y, c)
        return copies

    return [a for arrays, _, _ in parts for a in arrays], plan, sum(m for _, _, m in parts)


def _reduce_begin(tag, grads, column_halves=False):
    n = len(grads)
    if column_halves:
        half = lambda ref, j, h: ref.at[j, :, pl.ds(pl.multiple_of(h * (ref.shape[2] // 2), LANES), ref.shape[2] // 2)]
        lands = [lax.empty((N_CHIPS, g.shape[1], g.shape[2] // 2), F32) for g in grads]
    else:
        half = lambda ref, j, h: ref.at[j, h]
        lands = [lax.empty((N_CHIPS,) + g.shape[2:], F32) for g in grads]

    def plan(refs, x, y, c):
        return [(half(refs[k], j, 1 - c), refs[n + k].at[j], (x, y, 1 - c), refs[n + k].at[j])
                for k in range(n) for j in range(N_CHIPS)]

    return dict(tag=tag, n=n, stage="c", grads=list(grads), column_halves=column_halves,
                part=(list(grads) + lands, plan, N_CHIPS * n))


def _reduce_next(st, thru):
    tag, n, stage = st["tag"], st["n"], st["stage"]
    first, recv = thru[:n], thru[n:]
    x, y, c = _position()
    if stage == "c":
        pos = jnp.stack([c, x]).astype(jnp.int32)
        if st["column_halves"]:
            mine = lambda chip: (lambda i, p: (chip(p) + i, 0, p[0]))
        else:
            mine = lambda chip: (lambda i, p: (chip(p) + i, p[0]))
        items = [(first[k], mine(lambda p: 2 * p[1]), mine(lambda p: 2 * (1 - p[1])),
                  recv[k], lambda i, p: (2 * p[1] + i,), lambda i, p: (2 * (1 - p[1]) + i,)) for k in range(n)]
        if n > 1:
            sums = _add_keep_give_group(f"rs{tag}_c_add", pos, items, 2)
        else:
            sums = [_add_keep_give(f"rs{tag}_c_add{k}", pos, *item, 2) for k, item in enumerate(items)]
        lands = [lax.empty(s[1].shape, BF16) for s in sums]
        return dict(tag=tag, n=n, stage="x", keep=[s[0] for s in sums],
                    part=_exchange_part([s[1] for s in sums], lands, lambda x, y, c: (1 - x, y, c)))
    if stage == "x":
        pos = jnp.stack([y]).astype(jnp.int32)
        items = [(st["keep"][k], lambda i, p: (p[0],), lambda i, p: (1 - p[0],),
                  recv[k], lambda i, p: (p[0],), lambda i, p: (1 - p[0],)) for k in range(n)]
        if n > 1:
            sums = _add_keep_give_group(f"rs{tag}_x_add", pos, items, 1)
        else:
            sums = [_add_keep_give(f"rs{tag}_x_add{k}", pos, *item, 1) for k, item in enumerate(items)]
        lands = [lax.empty(s[1].shape, BF16) for s in sums]
        return dict(tag=tag, n=n, stage="y", keep=[s[0] for s in sums],
                    part=_exchange_part([s[1] for s in sums], lands, lambda x, y, c: (x, 1 - y, c)))
    if stage == "y":
        if n > 1:
            mine = _add_last_group(f"rs{tag}_y_add", list(zip(st["keep"], recv)))
        else:
            mine = [_add_last(f"rs{tag}_y_add{k}", st["keep"][k], recv[k]) for k in range(n)]
        lands = [lax.empty(m.shape, F32) for m in mine]
        return dict(tag=tag, n=n, stage="swap", part=_exchange_part(mine, lands, lambda x, y, c: (x, y, 1 - c)))
    return dict(tag=tag, done=list(zip(first, recv)))


def _small_begin(tag, v):
    land = lax.empty((N_DEV,) + v.shape, F32)
    flips = [(fx, fy, fc) for fx in (0, 1) for fy in (0, 1) for fc in (0, 1)][1:]

    def plan(refs, x, y, c):
        copies = []
        for fx, fy, fc in flips:
            px, py, pc = (1 - x if fx else x), (1 - y if fy else y), (1 - c if fc else c)
            copies.append((refs[0], refs[1].at[4 * x + 2 * y + c], (px, py, pc), refs[1].at[4 * px + 2 * py + pc]))
        return copies

    return dict(tag=tag, n=1, stage="swap", grads=[v], part=([v, land], plan, len(flips)))


def _small_sum(name, own, land):
    x, y, c = _position()
    me = jnp.stack([4 * x + 2 * y + c]).astype(jnp.int32)

    def body(me_ref, own_ref, land_ref, out_ref):
        term = lambda dev: jnp.where(me_ref[0] == dev, own_ref[...], land_ref[dev])
        acc = term(0)
        for dev in range(1, N_DEV):
            acc = acc + term(dev)
        out_ref[...] = acc

    return pl.pallas_call(
        body, name=name, out_shape=jax.ShapeDtypeStruct(own.shape, F32),
        grid_spec=pltpu.PrefetchScalarGridSpec(
            num_scalar_prefetch=1, grid=(1,),
            in_specs=[pl.BlockSpec(own.shape, lambda i, m: (0, 0)), pl.BlockSpec(land.shape, lambda i, m: (0, 0, 0))],
            out_specs=pl.BlockSpec(own.shape, lambda i, m: (0, 0))),
        compiler_params=_cparams("arbitrary"),
    )(me, own, land)


def _adamw_update(w, gg, m, v):
    mn = ADAM_B1 * m + (1.0 - ADAM_B1) * gg
    vn = ADAM_B2 * v + (1.0 - ADAM_B2) * (gg * gg)
    m_hat = mn / (1.0 - ADAM_B1 ** ADAM_STEP)
    v_hat = vn / (1.0 - ADAM_B2 ** ADAM_STEP)
    return -ADAM_LR * (m_hat / (jnp.sqrt(v_hat) + ADAM_EPS) + ADAM_WD * w), mn, vn


def _adamw(name, w, g, m, v):
    def body(w_ref, g_ref, m_ref, v_ref, d_ref, mo_ref, vo_ref):
        d_ref[...], mo_ref[...], vo_ref[...] = _adamw_update(w_ref[...], g_ref[...], m_ref[...], v_ref[...])

    blk = pl.BlockSpec(w.shape, lambda i: (0, 0))
    return pl.pallas_call(
        body, name=name, out_shape=(jax.ShapeDtypeStruct(w.shape, F32),) * 3, grid=(1,),
        in_specs=[blk] * 4, out_specs=(blk,) * 3, compiler_params=_cparams("arbitrary"),
    )(w, g, m, v)


def _rows_to_bf16(name, w):
    r, _, c = w.shape

    def body(w_ref, o_ref):
        o_ref[...] = w_ref[:, 0, :].astype(BF16)

    return pl.pallas_call(
        body, name=name, out_shape=jax.ShapeDtypeStruct((r, c), BF16), grid=(1,),
        in_specs=[pl.BlockSpec((r, 1, c), lambda i: (0, 0, 0))], out_specs=pl.BlockSpec((r, c), lambda i: (0, 0)),
        compiler_params=_cparams("arbitrary"),
    )(w)


def _adamw_rows(name, pos_c, w, g_mine, g_other, m, v):
    r, _, c = w.shape
    ch = c // 2

    def body(p_ref, w_ref, gm_ref, go_ref, m_ref, v_ref, g_ref, d_ref, mo_ref, vo_ref):
        gg = jnp.where(pl.program_id(0) == p_ref[0], gm_ref[...], go_ref[...])
        dl, mn, vn = _adamw_update(w_ref[:, 0, :], gg, m_ref[:, 0, :], v_ref[:, 0, :])
        g_ref[:, 0, :] = gg
        d_ref[:, 0, :] = dl
        mo_ref[:, 0, :] = mn
        vo_ref[:, 0, :] = vn

    rows = pl.BlockSpec((r, 1, ch), lambda h, p: (0, 0, h))
    half = pl.BlockSpec((r, ch), lambda h, p: (0, 0))
    return pl.pallas_call(
        body, name=name, out_shape=(jax.ShapeDtypeStruct(w.shape, F32),) * 4,
        grid_spec=pltpu.PrefetchScalarGridSpec(
            num_scalar_prefetch=1, grid=(2,), in_specs=[rows, half, half, rows, rows], out_specs=(rows,) * 4),
        compiler_params=_cparams("parallel"),
    )(pos_c, w, g_mine, g_other, m, v)


def _adamw_halves(name, pos_c, w, g_mine, g_other, m, v, tr, dep=None):
    r, c = w.shape
    rh = r // 2
    tr = tr if rh % tr == 0 else rh
    nt = rh // tr

    def body(p_ref, w_ref, gm_ref, go_ref, m_ref, v_ref, g_ref, d_ref, mo_ref, vo_ref):
        gg = jnp.where(pl.program_id(0) == p_ref[0], gm_ref[...], go_ref[...])
        g_ref[...] = gg
        d_ref[...], mo_ref[...], vo_ref[...] = _adamw_update(w_ref[...], gg, m_ref[...], v_ref[...])

    full = pl.BlockSpec((tr, c), lambda h, i, p: (h * nt + i, 0))
    half = pl.BlockSpec((tr, c), lambda h, i, p: (i, 0))
    dep_specs, dep_ops = _dep_args(dep)
    return pl.pallas_call(
        _after(body, 6, dep), name=name, out_shape=(jax.ShapeDtypeStruct((r, c), F32),) * 4,
        grid_spec=pltpu.PrefetchScalarGridSpec(
            num_scalar_prefetch=1, grid=(2, nt),
            in_specs=[full, half, half, full, full] + dep_specs, out_specs=(full,) * 4),
        compiler_params=_cparams("parallel", "parallel"),
    )(pos_c, w, g_mine, g_other, m, v, *dep_ops)


def _col_sharded_to_comm(g):
    k, n = g.shape
    return g.reshape(2, k // 2, N_CHIPS, n // N_CHIPS).transpose(2, 0, 1, 3)


def _row_sharded_to_comm(g):
    r, c = g.shape
    return g.reshape(N_CHIPS, 2, r // (2 * N_CHIPS), c)


def _col_sharded_full(g):
    _, _, rh, c = g.shape
    return g.reshape(N_CHIPS, 2 * rh, c).transpose(1, 0, 2).reshape(2 * rh, N_CHIPS * c)


def _row_sharded_full(g):
    _, _, rh, c = g.shape
    return g.reshape(N_CHIPS * 2 * rh, c)


def _chip_rows(w3, start, stop, own=None, me=None):
    r = w3.shape[1]
    parts = []
    for chip in range(N_CHIPS):
        lo, hi = max(start - chip * r, 0), min(stop - chip * r, r)
        if lo < hi:
            part = w3[chip, lo:hi]
            parts.append(part if own is None else jnp.where(me == chip, own[lo:hi], part))
    return parts[0] if len(parts) == 1 else jnp.concatenate(parts, axis=0)


def _pack_small(g1, bfv, mix, scale, g2n, gf, extra=None):
    row8 = jnp.pad(bfv.reshape(1, N_HEADS), ((0, 0), (0, LANES - N_HEADS)))
    if extra is not None:
        row8 = row8 + jnp.pad(extra[:, :1], ((0, 0), (N_HEADS, LANES - N_HEADS - 1)))
    return jnp.concatenate([
        g1.reshape(8, LANES), jnp.pad(row8, ((0, 7), (0, 0))), mix.reshape(512, LANES),
        jnp.pad(scale.reshape(4, LANES), ((0, 4), (0, 0))), g2n.reshape(8, LANES), gf.reshape(8, LANES)], axis=0)


def _unpack_small(s, like):
    g1, bfv, mix, scale, g2n, gf = like
    return (s[0:8].reshape(g1.shape), s[8, :N_HEADS].reshape(bfv.shape), s[16:528].reshape(mix.shape),
            s[528:532].reshape(scale.shape), s[536:544].reshape(g2n.shape), s[544:552].reshape(gf.shape))


class _MeshLinks:
    def __init__(self, shards_in, shards_rest):
        self.gin = _gather_begin("in", shards_in, None, column_halves=True)
        self.grest = _gather_begin("rest", shards_rest, self.gin["token"])
        self.tokens = {"gather": self.grest["token"]}
        self.groups, self.flight, self.slot = {}, None, 0

    @property
    def token(self):
        return list(self.tokens.values())

    def tie(self, x):
        return _tie(x, self.token)

    def weights_in(self, after):
        st = _gather_forward(self.gin, after)
        (g,), (own,) = _gather_end(st, st["token"], merge=False)
        return g, own, 2 * lax.axis_index("x") + lax.axis_index("y")

    def rest_forward(self, after):
        self.grest = _gather_forward(self.grest, after)
        self.tokens["gather"] = self.grest["token"]

    def weights_rest(self, after):
        g = _gather_end(self.grest, after)
        del self.tokens["gather"]
        return [_col_sharded_full(g[0]), _col_sharded_full(g[1])] + [_row_sharded_full(a) for a in g[2:]]

    def advance(self, after, begin=()):
        slot = self.slot
        self.slot += 1
        if self.flight is not None:
            tags, sems, parts = self.flight
            arrays, plan, _ = _join_parts(parts)
            thru = _copies_wait(f"slot{slot}_wait", sems, arrays, plan, after)
            for tag, part in zip(tags, parts):
                self.groups[tag] = _reduce_next(self.groups[tag], thru[:len(part[0])])
                thru = thru[len(part[0]):]
        for st in begin:
            self.groups[st["tag"]] = st
        live = [(tag, st["part"]) for tag, st in self.groups.items() if "part" in st]
        self.flight = None
        self.tokens.pop("reduce", None)
        if live:
            arrays, plan, m = _join_parts([part for _, part in live])
            sems, thru, token = _copies_start(f"slot{slot}_start", arrays, plan, m)
            parts = []
            for _, (part_arrays, part_plan, part_m) in live:
                parts.append((thru[:len(part_arrays)], part_plan, part_m))
                thru = thru[len(part_arrays):]
            self.flight = ([tag for tag, _ in live], sems, parts)
            self.tokens["reduce"] = token

    def reduced(self, tag):
        return self.groups[tag]["done"]


class _NoLinks:
    token = None

    def __init__(self, w_in, rest):
        self.w_in, self.rest, self.grads = w_in, rest, {}

    def tie(self, x):
        return x

    def weights_in(self, after):
        return self.w_in, None, None

    def rest_forward(self, after):
        pass

    def weights_rest(self, after):
        return self.rest

    def advance(self, after, begin=()):
        for st in begin:
            self.grads[st["tag"]] = st["grads"]


def _local_step(links, x, target, seq, norm1_g, b_forget, pool_mix, pool_scale, norm2_g, norm_f_g, between=None):
    t, d = x.shape
    tq = min(256, seq)
    aw = ATTN_WIDTH
    o_q, o_f, o_g = POOL_WIDTH, POOL_WIDTH + 3 * aw, POOL_WIDTH + 3 * aw + N_HEADS
    bf = jnp.pad(b_forget, ((0, 0), (0, LANES - N_HEADS)))
    mixb = pool_mix.astype(BF16)

    h = _norm_fwd("norm1_fwd", x, links.tie(norm1_g), 512)
    w_in, own, me = links.weights_in(h)
    wu = _chip_rows(w_in, 0, o_q, own, me)
    wqkv = _chip_rows(w_in, o_q, o_f, own, me)
    wft = jnp.pad(_chip_rows(w_in, o_f, o_g, own, me), ((0, LANES - N_HEADS), (0, 0)))
    wg2 = _chip_rows(w_in, o_g, N_CHIPS * w_in.shape[1], own, me)
    wf = wft.T
    u = _matmul("mm_u", h, wu, "nt", F32, 1024, 512, d)
    g2 = _matmul("mm_gates", h, wg2, "nt", BF16, 1024, 1024, d)
    fl, fcum = _forget_fwd(h, wf, bf, seq)
    qa, ka, v = _attn_prep(h, _head_blocks(wqkv[:aw]), _head_blocks(wqkv[aw:2 * aw]), wqkv[2 * aw:], fcum, 1024)
    p, ps = _pool_fwd(u, mixb, pool_scale, seq)
    links.rest_forward([ps, qa, g2])
    o, lse = _attn_fwd(qa, ka, v, seq, tq, dep=links.token)
    w_pool_out, w_attn_out, w_out, w_ffn_gate, w_ffn_up, w_ffn_down = links.weights_rest(o)
    merged, x1 = _merge_fwd(x, ps, o, g2, w_pool_out, w_attn_out, w_out, 512)
    h2, gt, up, act, x2 = _ffn_fwd(x1, norm2_g, w_ffn_gate, w_ffn_up, w_ffn_down, 1024, 256)
    loss, dx2, d_gf = _final_fwd_bwd(x2, target, norm_f_g, 512)

    dgt, dup, dx1, d_g2n = _ffn_bwd(dx2, x1, norm2_g, gt, up, w_ffn_gate, w_ffn_up, w_ffn_down, 1024, 256)
    d_wd = _matmul("dw_down", act, dx2, "tn", F32, 1408, 1024, 1024)
    d_wg = _matmul("dw_gate", dgt, h2, "tn", F32, 1408, 1024, 1024)
    d_wu = _matmul("dw_up", dup, h2, "tn", F32, 1408, 1024, 1024)
    links.advance(None, begin=[_reduce_begin("a", [_row_sharded_to_comm(g) for g in (d_wg, d_wu, d_wd)])])
    dpy, day, dg2, dps, da = _merge_bwd(dx1, ps, o, g2, w_pool_out, w_attn_out, w_out, 512, dep=links.token)
    links.advance(dps)
    d_wout = _matmul("dw_out", merged, dx1, "tn", F32, 1024, 1024, 1024)
    d_wpo = _matmul("dw_pool_out", ps, dpy, "tn", F32, 512, 1024, 1024)
    d_wao = _matmul("dw_attn_out", o, day, "tn", F32, 512, 1024, 1024)
    dq, dk, dv, dfr = _attn_bwd(qa, ka, v, da, lse, seq, tq, dep=links.token)
    links.advance(dq, begin=[_reduce_begin(
        "m", [_col_sharded_to_comm(d_wpo), _col_sharded_to_comm(d_wao), _row_sharded_to_comm(d_wout)])])
    dfc = jnp.pad(dfr.reshape(N_HEADS, t).T, ((0, 0), (0, LANES - N_HEADS)))
    dfl, d_bf = _forget_bwd(dfc, fl, bf, seq)
    du, d_mix, d_scale = _pool_bwd(dps, p, mixb, links.tie(pool_scale), seq)
    d_wu_in = _matmul("dw_in_u", du, h, "tn", F32, 512, 1024, 1024, dep=links.token)
    small = (jnp.zeros_like(norm1_g), d_bf[:, :N_HEADS], d_mix, d_scale, d_g2n, d_gf)
    d_wq = _matmul("dw_in_q", dq, h, "tn", F32, 512, 1024, 1024, dep=links.token)
    d_wk = _matmul("dw_in_k", dk, h, "tn", F32, 512, 1024, 1024, dep=links.token)
    d_wv = _matmul("dw_in_v", dv, h, "tn", F32, 512, 1024, 1024, dep=links.token)
    links.advance([d_wu_in, d_wq, d_wk, d_wv], begin=[_small_begin("small", _pack_small(*small, extra=loss))])
    d_wf = _matmul("dw_in_f", dfl, h, "tn", F32, LANES, 1024, 512)
    d_wg2 = _matmul("dw_in_gates", dg2, h, "tn", F32, 1024, 1024, 1024, dep=links.token)
    d_win = jnp.concatenate([d_wu_in, d_wq, d_wk, d_wv, d_wf[:N_HEADS], d_wg2], axis=0)
    comm_b = [d_win.reshape(N_CHIPS, d_win.shape[0] // N_CHIPS, d)]
    links.advance(comm_b, begin=[_reduce_begin("b", comm_b, column_halves=True)])
    if between is not None:
        between()
    dx, d_g1 = _in_bwd(du, dq, dk, dv, dg2, dfl, dx1, x, links.tie(norm1_g), wu, wqkv, wg2, wft, 512)
    return loss, dx, (d_g1,) + small[1:]


def kernel(x, norm1_g, w_in, b_forget, pool_mix, pool_scale, w_pool_out, w_attn_out, w_out, norm2_g, w_ffn_gate, w_ffn_up, w_ffn_down, norm_f_g, loss_target, m_norm1_g, m_w_in, m_b_forget, m_pool_mix, m_pool_scale, m_w_pool_out, m_w_attn_out, m_w_out, m_norm2_g, m_w_ffn_gate, m_w_ffn_up, m_w_ffn_down, m_norm_f_g, v_norm1_g, v_w_in, v_b_forget, v_pool_mix, v_pool_scale, v_w_pool_out, v_w_attn_out, v_w_out, v_norm2_g, v_w_ffn_gate, v_w_ffn_up, v_w_ffn_down, v_norm_f_g):
    nb, seq, d = x.shape
    group_a = ((w_ffn_gate, m_w_ffn_gate, v_w_ffn_gate, True, 9), (w_ffn_up, m_w_ffn_up, v_w_ffn_up, True, 10),
               (w_ffn_down, m_w_ffn_down, v_w_ffn_down, False, 11))
    group_m = ((w_pool_out, m_w_pool_out, v_w_pool_out, False, 5), (w_attn_out, m_w_attn_out, v_w_attn_out, False, 6),
               (w_out, m_w_out, v_w_out, False, 7))
    group_b = ((w_in, m_w_in, v_w_in, False, 1),)
    small_w = (norm1_g, b_forget, pool_mix, pool_scale, norm2_g, norm_f_g)
    small_m = (m_norm1_g, m_b_forget, m_pool_mix, m_pool_scale, m_norm2_g, m_norm_f_g)
    small_v = (v_norm1_g, v_b_forget, v_pool_mix, v_pool_scale, v_norm2_g, v_norm_f_g)
    small_pos = (0, 2, 3, 4, 8, 12)
    view = lambda a, tr: a[0].T if tr else a[0]
    unview = lambda a, tr, like: (a.T if tr else a).reshape(like.shape)

    def shard(w, tr):
        lw = view(w, tr).astype(BF16)
        return lw.reshape(2, lw.shape[0] // 2, lw.shape[1])

    cm = lambda a: jnp.transpose(a, (2, 0, 1))
    shard_in = _rows_to_bf16("w_in_to_bf16", cm(w_in))
    links = _MeshLinks([shard_in],
                       [shard(w_pool_out, False), shard(w_attn_out, False), shard(w_out, False),
                        shard(w_ffn_gate, True), shard(w_ffn_up, True), shard(w_ffn_down, False)])
    grads, deltas, new_m, new_v = [None] * 13, [None] * 13, [None] * 13, [None] * 13
    pos_c = jnp.stack([lax.axis_index("c")]).astype(jnp.int32)

    def update(tag, group, dep, members=(0, 1, 2)):
        last = []
        reduced = links.reduced(tag)
        for k in members:
            (w, m, v, tr, pos), (mine, other) = group[k], reduced[k]
            outs = _adamw_halves(f"adamw_{tag}{k}", pos_c, view(w, tr), mine, other, view(m, tr), view(v, tr), 256,
                                 dep=dep)
            grads[pos], deltas[pos], new_m[pos], new_v[pos] = (unview(a, tr, w) for a in outs)
            last.append(outs[1])
        return last

    def update_a():
        links.advance(update("a", group_a, links.token, members=(0,)))

    loss, dx, small_g = _local_step(
        links, x.reshape(nb * seq, d), loss_target.reshape(nb * seq, d), seq,
        norm1_g, b_forget, pool_mix[0], pool_scale, norm2_g, norm_f_g.reshape(1, d), between=update_a)

    links.advance(dx, begin=[_small_begin("g1", small_g[0].reshape(8, LANES))])
    last = update("m", group_m, links.token) + update("a", group_a, links.token, members=(2,))
    small_rest = _small_sum("small_sum", *links.reduced("small")[0])
    links.advance(last + [small_rest])
    last = update("a", group_a, links.token, members=(1,))
    small_sum = jnp.concatenate([_small_sum("g1_sum", *links.reduced("g1")[0]), small_rest[8:]], axis=0)
    loss_out = small_sum[8, N_HEADS]
    dl, mn, vn = _adamw("adamw_small", _pack_small(*small_w), small_sum * _small_mask(), _pack_small(*small_m),
                        _pack_small(*small_v))
    for pos, g, a, b, e in zip(small_pos, _unpack_small(small_sum, small_w), _unpack_small(dl, small_w),
                               _unpack_small(mn, small_w), _unpack_small(vn, small_w)):
        grads[pos], deltas[pos], new_m[pos], new_v[pos] = g, a, b, e
    links.advance(last + [dl])
    (mine, other), = links.reduced("b")
    outs = _adamw_rows("adamw_b0", pos_c, cm(w_in), mine, other, cm(m_w_in), cm(v_w_in))
    grads[1], deltas[1], new_m[1], new_v[1] = (jnp.transpose(a, (1, 2, 0)) for a in outs)

    return (loss_out, dx.reshape(nb, seq, d), *grads, *deltas, *new_m, *new_v)


def _small_mask():
    rows = lax.broadcasted_iota(jnp.int32, (552, LANES), 0)
    lanes = lax.broadcasted_iota(jnp.int32, (552, LANES), 1)
    return jnp.where(jnp.logical_and(rows == 8, lanes == N_HEADS), 0.0, 1.0).astype(F32)
```
